```python
import math
import jax, jax.numpy as jnp
from jax import lax
import numpy as np

D_MODEL = 1024
BATCH = 8
SEQ = 8192
DEPTH = 1

MEM_LEN = 256
HG_HEADS = 8
HG_DK = 128
HG_DV = D_MODEL // HG_HEADS
HG_WIDTH = HG_HEADS * HG_DK
HG_VWIDTH = HG_HEADS * HG_DV
HG_CHUNK = 64
SWA_HEADS = 16
SWA_KV_HEADS = 2
SWA_HEAD_DIM = 64
SWA_GROUP = SWA_HEADS // SWA_KV_HEADS
SWA_WIDTH = SWA_HEADS * SWA_HEAD_DIM
SWA_KV_WIDTH = SWA_KV_HEADS * SWA_HEAD_DIM
SWA_WINDOW = 128
SWA_BLOCK = 128
MEM_HEADS = 4
MEM_HEAD_DIM = 256
MEM_WIDTH = MEM_HEADS * MEM_HEAD_DIM
N_BRANCHES = 3
NUM_BUCKETS = 32
MAX_DISTANCE = 128
D_FF = 4 * D_MODEL
LN_EPS = 1e-5
RMS_EPS = 1e-6
IN_SPLITS = (HG_WIDTH, HG_WIDTH, HG_VWIDTH, HG_VWIDTH, SWA_WIDTH, SWA_KV_WIDTH, SWA_KV_WIDTH, MEM_WIDTH, N_BRANCHES * D_MODEL)
IN_COLS = sum(IN_SPLITS)

kernel_name = "hybrid_hgrn2_swa_sink_memory_deepnorm"


def split_cols(z, sizes):
    out = []
    start = 0
    for s in sizes:
        out.append(z[..., start:start + s])
        start += s
    return out


def layer_norm(x, g, b):
    x = x.astype(jnp.float32)
    mu = jnp.mean(x, axis=-1, keepdims=True)
    xc = x - mu
    var = jnp.mean(xc * xc, axis=-1, keepdims=True)
    return xc * lax.rsqrt(var + LN_EPS) * g.astype(jnp.float32) + b.astype(jnp.float32)


def t5_bucket(n):
    max_exact = NUM_BUCKETS // 2
    nf = jnp.maximum(n, 1).astype(jnp.float32)
    large = max_exact + (jnp.log(nf / max_exact) / math.log(MAX_DISTANCE / max_exact)
                         * (NUM_BUCKETS - max_exact)).astype(jnp.int32)
    large = jnp.minimum(large, NUM_BUCKETS - 1)
    return jnp.where(n < max_exact, n, large)


def hgrn2(q, f_logit, v, lb):
    B, S = q.shape[0], q.shape[1]
    n_chunks = S // HG_CHUNK
    f = lb + (1.0 - lb) * jax.nn.sigmoid(f_logit.astype(jnp.float32))
    log_f = jnp.log(f)
    k = 1.0 - f

    def to_chunks(t):
        return t.astype(jnp.float32).reshape(B, n_chunks, HG_CHUNK, HG_HEADS, -1).transpose(1, 0, 3, 2, 4)

    qc, kc, vc, gc = to_chunks(q), to_chunks(k), to_chunks(v), to_chunks(log_f)
    b = jnp.cumsum(gc, axis=3)
    b_last = b[:, :, :, -1:, :]
    q_in = qc * jnp.exp(b)
    k_in = kc * jnp.exp(-b)
    k_out = kc * jnp.exp(b_last - b)
    causal = jnp.tril(jnp.ones((HG_CHUNK, HG_CHUNK), dtype=bool))
    attn = jnp.einsum('nbhcd,nbhsd->nbhcs', q_in, k_in)
    attn = jnp.where(causal, attn, 0.0)
    o_intra = jnp.einsum('nbhcs,nbhsv->nbhcv', attn, vc)

    def step(state, inp):
        k_o, v_n, decay = inp
        new = decay[..., :, None] * state + jnp.einsum('bhsd,bhsv->bhdv', k_o, v_n)
        return new, state

    init = jnp.zeros((B, HG_HEADS, HG_DK, HG_DV), jnp.float32)
    _, states = lax.scan(step, init, (k_out, vc, jnp.exp(b_last[:, :, :, 0, :])))
    o_inter = jnp.einsum('nbhcd,nbhdv->nbhcv', q_in, states)
    o = o_intra + o_inter
    return o.transpose(1, 0, 3, 2, 4).reshape(B, S, HG_HEADS, HG_DV)


def sliding_window_attention(q, k, v, rel_bias, sinks):
    B, S = q.shape[0], q.shape[1]
    nb = S // SWA_BLOCK
    scale = SWA_HEAD_DIM ** -0.5
    qb = q.astype(jnp.float32).reshape(B, nb, SWA_BLOCK, SWA_KV_HEADS, SWA_GROUP, SWA_HEAD_DIM) * scale
    kb = k.astype(jnp.float32).reshape(B, nb, SWA_BLOCK, SWA_KV_HEADS, SWA_HEAD_DIM)
    vb = v.astype(jnp.float32).reshape(B, nb, SWA_BLOCK, SWA_KV_HEADS, SWA_HEAD_DIM)
    pad = jnp.zeros_like(kb[:, :1])
    kk = jnp.concatenate([jnp.concatenate([pad, kb[:, :-1]], axis=1), kb], axis=2)
    vv = jnp.concatenate([jnp.concatenate([pad, vb[:, :-1]], axis=1), vb], axis=2)
    s = jnp.einsum('bnqkgd,bnskd->bkgnqs', qb, kk)
    qi = jnp.arange(SWA_BLOCK)[:, None] + SWA_BLOCK
    kj = jnp.arange(2 * SWA_BLOCK)[None, :]
    dist = qi - kj
    band = (dist >= 0) & (dist < SWA_WINDOW)
    valid = band[None] & ((jnp.arange(nb)[:, None, None] > 0) | (kj[None] >= SWA_BLOCK))
    bucket = t5_bucket(jnp.clip(dist, 0, SWA_WINDOW - 1))
    bias = rel_bias.astype(jnp.float32)[bucket].transpose(2, 0, 1)
    bias = bias.reshape(SWA_KV_HEADS, SWA_GROUP, 1, SWA_BLOCK, 2 * SWA_BLOCK)
    s = jnp.where(valid, s + bias, -jnp.inf)
    sink = sinks.astype(jnp.float32).reshape(SWA_KV_HEADS, SWA_GROUP, 1, 1, 1)
    m = jnp.maximum(jnp.max(s, axis=-1, keepdims=True), sink)
    p = jnp.exp(s - m)
    p = p / (jnp.sum(p, axis=-1, keepdims=True) + jnp.exp(sink - m))
    o = jnp.einsum('bkgnqs,bnskd->bnqkgd', p, vv)
    return o.reshape(B, S, SWA_WIDTH)


def memory_attention(q, mk, mv):
    B, S = q.shape[0], q.shape[1]
    M = mk.shape[1]
    qh = q.astype(jnp.float32).reshape(B, S, MEM_HEADS, MEM_HEAD_DIM) * (MEM_HEAD_DIM ** -0.5)
    kh = mk.astype(jnp.float32).reshape(B, M, MEM_HEADS, MEM_HEAD_DIM)
    vh = mv.astype(jnp.float32).reshape(B, M, MEM_HEADS, MEM_HEAD_DIM)
    p = jax.nn.softmax(jnp.einsum('bshd,bmhd->bhsm', qh, kh), axis=-1)
    return jnp.einsum('bhsm,bmhd->bshd', p, vh).reshape(B, S, MEM_WIDTH)


def _fwd_setup_inputs(seed: int = 0) -> dict:
    key = jax.random.key(seed)
    ks = jax.random.split(key, 18)
    beta = (8.0 * DEPTH) ** -0.25

    def nrm(k, shape, scale):
        return jax.random.normal(k, shape, jnp.float32) * scale

    return {
        "x": nrm(ks[0], (BATCH, SEQ, D_MODEL), 1.0),
        "mem": nrm(ks[1], (BATCH, MEM_LEN, D_MODEL), 1.0),
        "w_in": nrm(ks[2], (DEPTH, D_MODEL, IN_COLS), D_MODEL ** -0.5),
        "lb_logits": nrm(ks[3], (DEPTH + 1, HG_WIDTH), 0.1),
        "hg_norm_gain": 1.0 + nrm(ks[4], (DEPTH, HG_VWIDTH), 0.02),
        "swa_sinks": nrm(ks[5], (DEPTH, SWA_HEADS), 0.5),
        "rel_bias": nrm(ks[6], (NUM_BUCKETS, SWA_HEADS), 0.5),
        "w_mem_kv": nrm(ks[7], (DEPTH, D_MODEL, 2 * MEM_WIDTH), D_MODEL ** -0.5),
        "w_branch_hg": nrm(ks[8], (DEPTH, HG_VWIDTH, D_MODEL), HG_VWIDTH ** -0.5),
        "w_branch_swa": nrm(ks[9], (DEPTH, SWA_WIDTH, D_MODEL), SWA_WIDTH ** -0.5),
        "w_branch_mem": nrm(ks[10], (DEPTH, MEM_WIDTH, D_MODEL), MEM_WIDTH ** -0.5),
        "w_out": nrm(ks[11], (DEPTH, D_MODEL, D_MODEL), (D_MODEL ** -0.5) * beta),
        "ln1_g": 1.0 + nrm(ks[12], (DEPTH, D_MODEL), 0.02),
        "ln1_b": nrm(ks[13], (DEPTH, D_MODEL), 0.02),
        "w_up": nrm(ks[14], (DEPTH, D_MODEL, D_FF), D_MODEL ** -0.5),
        "w_down": nrm(ks[15], (DEPTH, D_FF, D_MODEL), (D_FF ** -0.5) * beta),
        "ln2_g": 1.0 + nrm(ks[16], (DEPTH, D_MODEL), 0.02),
        "ln2_b": nrm(ks[17], (DEPTH, D_MODEL), 0.02),
    }


def _fwd_reference(x, mem, w_in, lb_logits, hg_norm_gain, swa_sinks, rel_bias, w_mem_kv,
              w_branch_hg, w_branch_swa, w_branch_mem, w_out, ln1_g, ln1_b,
              w_up, w_down, ln2_g, ln2_b):
    B, S = x.shape[0], x.shape[1]
    out_dtype = x.dtype
    alpha = (2.0 * DEPTH) ** 0.25
    lb_all = jnp.cumsum(jax.nn.softmax(lb_logits.astype(jnp.float32), axis=0), axis=0)
    memf = mem.astype(jnp.float32)
    h = x.astype(jnp.float32)
    for layer in range(DEPTH):
        z = h @ w_in[layer]
        hq, hf, hi, hg, sq, sk, sv, mq, gl = split_cols(z, IN_SPLITS)
        o_a = hgrn2(hq.reshape(B, S, HG_HEADS, HG_DK), hf.reshape(B, S, HG_HEADS, HG_DK),
                    hi.reshape(B, S, HG_HEADS, HG_DV), lb_all[layer].reshape(HG_HEADS, HG_DK))
        o_a = o_a * lax.rsqrt(jnp.mean(o_a * o_a, axis=-1, keepdims=True) + RMS_EPS)
        o_a = o_a.reshape(B, S, HG_VWIDTH) * hg_norm_gain[layer].astype(jnp.float32) * jax.nn.silu(hg.astype(jnp.float32))
        o_b = sliding_window_attention(sq, sk, sv, rel_bias, swa_sinks[layer])
        mk, mv = split_cols(memf @ w_mem_kv[layer], (MEM_WIDTH, MEM_WIDTH))
        o_c = memory_attention(mq, mk, mv)
        gates = jax.nn.sigmoid(gl.astype(jnp.float32).reshape(B, S, N_BRANCHES, D_MODEL))
        merged = (gates[:, :, 0] * (o_a @ w_branch_hg[layer])
                  + gates[:, :, 1] * (o_b @ w_branch_swa[layer])
                  + gates[:, :, 2] * (o_c @ w_branch_mem[layer]))
        mix = merged @ w_out[layer]
        h = layer_norm(alpha * h + mix, ln1_g[layer], ln1_b[layer])
        ff = jnp.square(jax.nn.relu(h @ w_up[layer])) @ w_down[layer]
        h = layer_norm(alpha * h + ff, ln2_g[layer], ln2_b[layer])
    return h.astype(out_dtype)


import jax as _jax
import jax.numpy as _jnp

TWIN_FORMAT = 'train_step'
FWD_PARAMS = ['x', 'mem', 'w_in', 'lb_logits', 'hg_norm_gain', 'swa_sinks', 'rel_bias', 'w_mem_kv', 'w_branch_hg', 'w_branch_swa', 'w_branch_mem', 'w_out', 'ln1_g', 'ln1_b', 'w_up', 'w_down', 'ln2_g', 'ln2_b']
TWIN_WEIGHTS = ['w_in', 'lb_logits', 'hg_norm_gain', 'swa_sinks', 'rel_bias', 'w_mem_kv', 'w_branch_hg', 'w_branch_swa', 'w_branch_mem', 'w_out', 'ln1_g', 'ln1_b', 'w_up', 'w_down', 'ln2_g', 'ln2_b']
TWIN_DIFF_INPUT = 'x'
TWIN_INPUTS = ['x', 'mem', 'w_in', 'lb_logits', 'hg_norm_gain', 'swa_sinks', 'rel_bias', 'w_mem_kv', 'w_branch_hg', 'w_branch_swa', 'w_branch_mem', 'w_out', 'ln1_g', 'ln1_b', 'w_up', 'w_down', 'ln2_g', 'ln2_b', 'loss_target', 'm_w_in', 'm_lb_logits', 'm_hg_norm_gain', 'm_swa_sinks', 'm_rel_bias', 'm_w_mem_kv', 'm_w_branch_hg', 'm_w_branch_swa', 'm_w_branch_mem', 'm_w_out', 'm_ln1_g', 'm_ln1_b', 'm_w_up', 'm_w_down', 'm_ln2_g', 'm_ln2_b', 'v_w_in', 'v_lb_logits', 'v_hg_norm_gain', 'v_swa_sinks', 'v_rel_bias', 'v_w_mem_kv', 'v_w_branch_hg', 'v_w_branch_swa', 'v_w_branch_mem', 'v_w_out', 'v_ln1_g', 'v_ln1_b', 'v_w_up', 'v_w_down', 'v_ln2_g', 'v_ln2_b']
TWIN_OUTPUTS = ['loss', 'grad_x', 'grad_w_in', 'grad_lb_logits', 'grad_hg_norm_gain', 'grad_swa_sinks', 'grad_rel_bias', 'grad_w_mem_kv', 'grad_w_branch_hg', 'grad_w_branch_swa', 'grad_w_branch_mem', 'grad_w_out', 'grad_ln1_g', 'grad_ln1_b', 'grad_w_up', 'grad_w_down', 'grad_ln2_g', 'grad_ln2_b', 'delta_w_in', 'delta_lb_logits', 'delta_hg_norm_gain', 'delta_swa_sinks', 'delta_rel_bias', 'delta_w_mem_kv', 'delta_w_branch_hg', 'delta_w_branch_swa', 'delta_w_branch_mem', 'delta_w_out', 'delta_ln1_g', 'delta_ln1_b', 'delta_w_up', 'delta_w_down', 'delta_ln2_g', 'delta_ln2_b', 'new_m_w_in', 'new_m_lb_logits', 'new_m_hg_norm_gain', 'new_m_swa_sinks', 'new_m_rel_bias', 'new_m_w_mem_kv', 'new_m_w_branch_hg', 'new_m_w_branch_swa', 'new_m_w_branch_mem', 'new_m_w_out', 'new_m_ln1_g', 'new_m_ln1_b', 'new_m_w_up', 'new_m_w_down', 'new_m_ln2_g', 'new_m_ln2_b', 'new_v_w_in', 'new_v_lb_logits', 'new_v_hg_norm_gain', 'new_v_swa_sinks', 'new_v_rel_bias', 'new_v_w_mem_kv', 'new_v_w_branch_hg', 'new_v_w_branch_swa', 'new_v_w_branch_mem', 'new_v_w_out', 'new_v_ln1_g', 'new_v_ln1_b', 'new_v_w_up', 'new_v_w_down', 'new_v_ln2_g', 'new_v_ln2_b']
TWIN_LEAF_KINDS = {'loss': 'loss', 'grad_x': 'grad_x', 'grad_w_in': 'grad_w', 'grad_lb_logits': 'grad_w', 'grad_hg_norm_gain': 'grad_w', 'grad_swa_sinks': 'grad_w', 'grad_rel_bias': 'grad_w', 'grad_w_mem_kv': 'grad_w', 'grad_w_branch_hg': 'grad_w', 'grad_w_branch_swa': 'grad_w', 'grad_w_branch_mem': 'grad_w', 'grad_w_out': 'grad_w', 'grad_ln1_g': 'grad_w', 'grad_ln1_b': 'grad_w', 'grad_w_up': 'grad_w', 'grad_w_down': 'grad_w', 'grad_ln2_g': 'grad_w', 'grad_ln2_b': 'grad_w', 'delta_w_in': 'delta_w', 'delta_lb_logits': 'delta_w', 'delta_hg_norm_gain': 'delta_w', 'delta_swa_sinks': 'delta_w', 'delta_rel_bias': 'delta_w', 'delta_w_mem_kv': 'delta_w', 'delta_w_branch_hg': 'delta_w', 'delta_w_branch_swa': 'delta_w', 'delta_w_branch_mem': 'delta_w', 'delta_w_out': 'delta_w', 'delta_ln1_g': 'delta_w', 'delta_ln1_b': 'delta_w', 'delta_w_up': 'delta_w', 'delta_w_down': 'delta_w', 'delta_ln2_g': 'delta_w', 'delta_ln2_b': 'delta_w', 'new_m_w_in': 'new_m', 'new_m_lb_logits': 'new_m', 'new_m_hg_norm_gain': 'new_m', 'new_m_swa_sinks': 'new_m', 'new_m_rel_bias': 'new_m', 'new_m_w_mem_kv': 'new_m', 'new_m_w_branch_hg': 'new_m', 'new_m_w_branch_swa': 'new_m', 'new_m_w_branch_mem': 'new_m', 'new_m_w_out': 'new_m', 'new_m_ln1_g': 'new_m', 'new_m_ln1_b': 'new_m', 'new_m_w_up': 'new_m', 'new_m_w_down': 'new_m', 'new_m_ln2_g': 'new_m', 'new_m_ln2_b': 'new_m', 'new_v_w_in': 'new_v', 'new_v_lb_logits': 'new_v', 'new_v_hg_norm_gain': 'new_v', 'new_v_swa_sinks': 'new_v', 'new_v_rel_bias': 'new_v', 'new_v_w_mem_kv': 'new_v', 'new_v_w_branch_hg': 'new_v', 'new_v_w_branch_swa': 'new_v', 'new_v_w_branch_mem': 'new_v', 'new_v_w_out': 'new_v', 'new_v_ln1_g': 'new_v', 'new_v_ln1_b': 'new_v', 'new_v_w_up': 'new_v', 'new_v_w_down': 'new_v', 'new_v_ln2_g': 'new_v', 'new_v_ln2_b': 'new_v'}


def _forward(args):
    return _fwd_reference(*[args[k] for k in FWD_PARAMS])


def _output_shape():
    def fwd():
        inp = _fwd_setup_inputs(0)
        return _fwd_reference(*[inp[k] for k in FWD_PARAMS])
    out = _jax.eval_shape(fwd)
    return out.shape, out.dtype

N_MICROBATCH = 1
ADAM_LR = 0.001
ADAM_B1 = 0.9
ADAM_B2 = 0.999
ADAM_EPS = 1e-08
ADAM_WD = 0.01
ADAM_STEP = 10
PER_EXAMPLE_BATCH_AXIS = {'x': 0, 'mem': 0, 'loss_target': 0}
SHARED_INPUTS = []
_WEIGHT_DTYPES = {'w_in': _jnp.float32, 'lb_logits': _jnp.float32, 'hg_norm_gain': _jnp.float32, 'swa_sinks': _jnp.float32, 'rel_bias': _jnp.float32, 'w_mem_kv': _jnp.float32, 'w_branch_hg': _jnp.float32, 'w_branch_swa': _jnp.float32, 'w_branch_mem': _jnp.float32, 'w_out': _jnp.float32, 'ln1_g': _jnp.float32, 'ln1_b': _jnp.float32, 'w_up': _jnp.float32, 'w_down': _jnp.float32, 'ln2_g': _jnp.float32, 'ln2_b': _jnp.float32}
MOMENT_SCALE = {'w_in': 3.138170e-02, 'lb_logits': 2.743268e-02, 'hg_norm_gain': 4.226602e-02, 'swa_sinks': 1.119279e-02, 'rel_bias': 1.834935e-02, 'w_mem_kv': 7.862468e-03, 'w_branch_hg': 4.178244e-02, 'w_branch_swa': 1.555354e-02, 'w_branch_mem': 8.336841e-03, 'w_out': 7.570235e-02, 'ln1_g': 1.516884e+00, 'ln1_b': 9.976868e-01, 'w_up': 7.619610e-02, 'w_down': 3.706490e-01, 'ln2_g': 6.414630e+01, 'ln2_b': 1.370181e+01}


def _to_microbatches(a, axis):
    t = _jnp.moveaxis(a, axis, 0)
    t = t.reshape((N_MICROBATCH, t.shape[0] // N_MICROBATCH) + t.shape[1:])
    return _jnp.moveaxis(t, 1, axis + 1)


def setup_inputs(seed: int = 0) -> dict:
    inp = _fwd_setup_inputs(seed)
    key = _jax.random.fold_in(_jax.random.key(seed), 7919)
    shape, _ = _output_shape()
    out = dict(inp)
    out["loss_target"] = _jax.random.normal(_jax.random.fold_in(key, 0), shape, _jnp.float32)
    for i, name in enumerate(TWIN_WEIGHTS):
        w = inp[name].astype(_jnp.float32)
        if MOMENT_SCALE is None:
            s = _jnp.sqrt(_jnp.mean(_jnp.square(w)) + 1e-30)
        else:
            s = MOMENT_SCALE[name]
        km, kv = _jax.random.split(_jax.random.fold_in(key, i + 1))
        out[name] = w
        out["m_" + name] = s * _jax.random.normal(km, w.shape, _jnp.float32)
        out["v_" + name] = (s * s) * _jax.random.uniform(kv, w.shape, _jnp.float32, 0.5, 1.5)
    if N_MICROBATCH > 1:
        for name, axis in PER_EXAMPLE_BATCH_AXIS.items():
            out[name] = _to_microbatches(out[name], axis)
    return {'x': out['x'], 'mem': out['mem'], 'w_in': out['w_in'], 'lb_logits': out['lb_logits'], 'hg_norm_gain': out['hg_norm_gain'], 'swa_sinks': out['swa_sinks'], 'rel_bias': out['rel_bias'], 'w_mem_kv': out['w_mem_kv'], 'w_branch_hg': out['w_branch_hg'], 'w_branch_swa': out['w_branch_swa'], 'w_branch_mem': out['w_branch_mem'], 'w_out': out['w_out'], 'ln1_g': out['ln1_g'], 'ln1_b': out['ln1_b'], 'w_up': out['w_up'], 'w_down': out['w_down'], 'ln2_g': out['ln2_g'], 'ln2_b': out['ln2_b'], 'loss_target': out['loss_target'], 'm_w_in': out['m_w_in'], 'm_lb_logits': out['m_lb_logits'], 'm_hg_norm_gain': out['m_hg_norm_gain'], 'm_swa_sinks': out['m_swa_sinks'], 'm_rel_bias': out['m_rel_bias'], 'm_w_mem_kv': out['m_w_mem_kv'], 'm_w_branch_hg': out['m_w_branch_hg'], 'm_w_branch_swa': out['m_w_branch_swa'], 'm_w_branch_mem': out['m_w_branch_mem'], 'm_w_out': out['m_w_out'], 'm_ln1_g': out['m_ln1_g'], 'm_ln1_b': out['m_ln1_b'], 'm_w_up': out['m_w_up'], 'm_w_down': out['m_w_down'], 'm_ln2_g': out['m_ln2_g'], 'm_ln2_b': out['m_ln2_b'], 'v_w_in': out['v_w_in'], 'v_lb_logits': out['v_lb_logits'], 'v_hg_norm_gain': out['v_hg_norm_gain'], 'v_swa_sinks': out['v_swa_sinks'], 'v_rel_bias': out['v_rel_bias'], 'v_w_mem_kv': out['v_w_mem_kv'], 'v_w_branch_hg': out['v_w_branch_hg'], 'v_w_branch_swa': out['v_w_branch_swa'], 'v_w_branch_mem': out['v_w_branch_mem'], 'v_w_out': out['v_w_out'], 'v_ln1_g': out['v_ln1_g'], 'v_ln1_b': out['v_ln1_b'], 'v_w_up': out['v_w_up'], 'v_w_down': out['v_w_down'], 'v_ln2_g': out['v_ln2_g'], 'v_ln2_b': out['v_ln2_b']}


def _loss(weights, diff, rest, loss_target):
    with _jax.named_scope("forward"):
        args = {**rest, TWIN_DIFF_INPUT: diff, **{k: w.astype(_WEIGHT_DTYPES[k]) for k, w in weights.items()}}
        y = _forward(args)
    with _jax.named_scope("loss_head"):
        err = _jnp.square(y.astype(_jnp.float32) - loss_target)
        return 0.5 * _jnp.sum(_jnp.mean(err, axis=-1)) if err.ndim else 0.5 * err


def _adamw(w, g, m, v):
    m = ADAM_B1 * m + (1.0 - ADAM_B1) * g
    v = ADAM_B2 * v + (1.0 - ADAM_B2) * _jnp.square(g)
    m_hat = m / (1.0 - ADAM_B1 ** ADAM_STEP)
    v_hat = v / (1.0 - ADAM_B2 ** ADAM_STEP)
    delta = -ADAM_LR * (m_hat / (_jnp.sqrt(v_hat) + ADAM_EPS) + ADAM_WD * w)
    return delta, m, v


def reference(x, mem, w_in, lb_logits, hg_norm_gain, swa_sinks, rel_bias, w_mem_kv, w_branch_hg, w_branch_swa, w_branch_mem, w_out, ln1_g, ln1_b, w_up, w_down, ln2_g, ln2_b, loss_target, m_w_in, m_lb_logits, m_hg_norm_gain, m_swa_sinks, m_rel_bias, m_w_mem_kv, m_w_branch_hg, m_w_branch_swa, m_w_branch_mem, m_w_out, m_ln1_g, m_ln1_b, m_w_up, m_w_down, m_ln2_g, m_ln2_b, v_w_in, v_lb_logits, v_hg_norm_gain, v_swa_sinks, v_rel_bias, v_w_mem_kv, v_w_branch_hg, v_w_branch_swa, v_w_branch_mem, v_w_out, v_ln1_g, v_ln1_b, v_w_up, v_w_down, v_ln2_g, v_ln2_b):
    given = dict(x=x, mem=mem, w_in=w_in, lb_logits=lb_logits, hg_norm_gain=hg_norm_gain, swa_sinks=swa_sinks, rel_bias=rel_bias, w_mem_kv=w_mem_kv, w_branch_hg=w_branch_hg, w_branch_swa=w_branch_swa, w_branch_mem=w_branch_mem, w_out=w_out, ln1_g=ln1_g, ln1_b=ln1_b, w_up=w_up, w_down=w_down, ln2_g=ln2_g, ln2_b=ln2_b, loss_target=loss_target, m_w_in=m_w_in, m_lb_logits=m_lb_logits, m_hg_norm_gain=m_hg_norm_gain, m_swa_sinks=m_swa_sinks, m_rel_bias=m_rel_bias, m_w_mem_kv=m_w_mem_kv, m_w_branch_hg=m_w_branch_hg, m_w_branch_swa=m_w_branch_swa, m_w_branch_mem=m_w_branch_mem, m_w_out=m_w_out, m_ln1_g=m_ln1_g, m_ln1_b=m_ln1_b, m_w_up=m_w_up, m_w_down=m_w_down, m_ln2_g=m_ln2_g, m_ln2_b=m_ln2_b, v_w_in=v_w_in, v_lb_logits=v_lb_logits, v_hg_norm_gain=v_hg_norm_gain, v_swa_sinks=v_swa_sinks, v_rel_bias=v_rel_bias, v_w_mem_kv=v_w_mem_kv, v_w_branch_hg=v_w_branch_hg, v_w_branch_swa=v_w_branch_swa, v_w_branch_mem=v_w_branch_mem, v_w_out=v_w_out, v_ln1_g=v_ln1_g, v_ln1_b=v_ln1_b, v_w_up=v_w_up, v_w_down=v_w_down, v_ln2_g=v_ln2_g, v_ln2_b=v_ln2_b)
    weights = {n: given[n] for n in TWIN_WEIGHTS}
    shared = {n: given[n] for n in SHARED_INPUTS}
    per_example = {n: given[n] for n in ['x', 'mem']}
    grad_fn = _jax.value_and_grad(_loss, argnums=(0, 1))

    def one_microbatch(ex, loss_target):
        ex = dict(ex)
        diff = ex.pop(TWIN_DIFF_INPUT)
        return grad_fn(weights, diff, {**shared, **ex}, loss_target)

    if N_MICROBATCH == 1:
        loss, (grad_w, grad_x) = one_microbatch(per_example, given["loss_target"])
    else:
        def body(carry, xs):
            loss_sum, grad_sum = carry
            l_k, (gw_k, gx_k) = one_microbatch(xs[0], xs[1])
            with _jax.named_scope("update"):
                return (loss_sum + l_k, _jax.tree.map(_jnp.add, grad_sum, gw_k)), gx_k

        init = (_jnp.zeros((), _jnp.float32), _jax.tree.map(_jnp.zeros_like, weights))
        (loss, grad_w), grad_x = _jax.lax.scan(body, init, (per_example, given["loss_target"]))
    with _jax.named_scope("update"):
        delta_w, new_m, new_v = {}, {}, {}
        for n in TWIN_WEIGHTS:
            delta_w[n], new_m[n], new_v[n] = _adamw(weights[n], grad_w[n], given["m_" + n], given["v_" + n])
    return (loss, grad_x, *[grad_w[n] for n in TWIN_WEIGHTS], *[delta_w[n] for n in TWIN_WEIGHTS],
            *[new_m[n] for n in TWIN_WEIGHTS], *[new_v[n] for n in TWIN_WEIGHTS])
```

```python
import functools
import math

import jax
import jax.numpy as jnp
from jax import lax
from jax.experimental import pallas as pl
from jax.experimental.pallas import tpu as pltpu

F32 = jnp.float32
BF16 = jnp.bfloat16
HIGHEST = lax.Precision.HIGHEST
MESH = pl.DeviceIdType.MESH

D_MODEL = 1024
MEM_LEN = 256
HG_HEADS = 8
HG_DK = 128
HG_CHUNK = 64
SWA_HEADS = 16
SWA_KV_HEADS = 2
SWA_GROUP = 8
SWA_HEAD_DIM = 64
SWA_BLOCK = 128
SWA_WINDOW = 128
MEM_HEADS = 4
MEM_HEAD_DIM = 256
NUM_BUCKETS = 32
MAX_DISTANCE = 128
D_FF = 4096
LN_EPS = 1e-5
RMS_EPS = 1e-6
ALPHA = 2.0 ** 0.25
W_A, W_B, W_C, W_D = 4096, 1280, 1024, 3072
IN_COLS = W_A + W_B + W_C + W_D
N_SHARDS = 4
ADAM_LR = 0.001
ADAM_B1 = 0.9
ADAM_B2 = 0.999
ADAM_EPS = 1e-08
ADAM_WD = 0.01
ADAM_STEP = 10
MASK_VALUE = -1e30
VMEM_LIMIT = 56 * 1024 * 1024

NN = ((1,), (0,))
NT = ((1,), (1,))
TN = ((0,), (0,))
HBM = pl.BlockSpec(memory_space=pltpu.HBM)


def _dot(a, b, dims=NN, precision=None):
    return lax.dot_general(a, b, (dims, ((), ())), precision=precision, preferred_element_type=F32)


def _params(sem=None):
    return pltpu.CompilerParams(dimension_semantics=sem, vmem_limit_bytes=VMEM_LIMIT)


def _resident(shape):
    zeros = (0,) * len(shape)
    return pl.BlockSpec(shape, lambda *_: zeros, pipeline_mode=pl.Buffered(1))


def _mm(a, b, *, mode, tm, tn, tk, name, out_dtype=F32, b_panels=False, out_panels=False, add=None, add_scale=1.0):
    if mode == "tn":
        kdim, m = a.shape
    else:
        m, kdim = a.shape
    if b_panels:
        n = b.shape[0] * b.shape[2]
        assert b.shape[2] == tn and mode == "nn"
    elif mode == "nt":
        n = b.shape[0]
    else:
        n = b.shape[1]
    assert m % tm == 0 and n % tn == 0 and kdim % tk == 0, (name, m, n, kdim)
    nk = kdim // tk
    dims = {"nn": NN, "nt": NT, "tn": TN}[mode]
    a_spec = pl.BlockSpec((tk, tm), lambda i, j, k: (k, i)) if mode == "tn" else pl.BlockSpec((tm, tk), lambda i, j, k: (i, k))
    if b_panels:
        b_spec = pl.BlockSpec((None, tk, tn), lambda i, j, k: (j, k, 0))
    elif mode == "nt":
        b_spec = pl.BlockSpec((tn, tk), lambda i, j, k: (j, k))
    else:
        b_spec = pl.BlockSpec((tk, tn), lambda i, j, k: (k, j))
    if out_panels:
        out_shape = jax.ShapeDtypeStruct((n // tn, m, tn), out_dtype)
        o_spec = pl.BlockSpec((None, tm, tn), lambda i, j, k: (j, i, 0))
    else:
        out_shape = jax.ShapeDtypeStruct((m, n), out_dtype)
        o_spec = pl.BlockSpec((tm, tn), lambda i, j, k: (i, j))
    in_specs = [a_spec, b_spec]
    operands = [a, b]
    if add is not None:
        in_specs.append(pl.BlockSpec((tm, tn), lambda i, j, k: (i, j)))
        operands.append(add)

    def body(*refs):
        a_ref, b_ref = refs[0], refs[1]
        add_ref = refs[2] if add is not None else None
        o_ref = refs[3] if add is not None else refs[2]
        part = _dot(a_ref[...].astype(BF16), b_ref[...].astype(BF16), dims)

        def finish(acc):
            if add_ref is not None:
                acc = acc + add_scale * add_ref[...]
            o_ref[...] = acc.astype(out_dtype)

        if nk == 1:
            finish(part)
        else:
            acc_ref = refs[-1]
            k = pl.program_id(2)

            @pl.when(k == 0)
            def _():
                acc_ref[...] = part

            @pl.when(k > 0)
            def _():
                acc_ref[...] += part

            @pl.when(k == nk - 1)
            def _():
                finish(acc_ref[...])

    return pl.pallas_call(
        body, name=name, out_shape=out_shape, grid=(m // tm, n // tn, nk), in_specs=in_specs, out_specs=o_spec,
        scratch_shapes=[pltpu.VMEM((tm, tn), F32)] if nk > 1 else [],
        compiler_params=_params(("parallel", "parallel", "arbitrary")),
    )(*operands)


def _dx_matmul(dzs, wis, resid, *, tm, tks, name):
    s = resid.shape[0]
    counts = [dz.shape[1] // tk for dz, tk in zip(dzs, tks)]
    starts = [sum(counts[:p]) for p in range(len(counts))]
    nk = sum(counts)
    npieces = len(dzs)

    def piece_block(p):
        return lambda i, k: (i, jnp.clip(k - starts[p], 0, counts[p] - 1))

    def weight_block(p):
        return lambda i, k: (0, jnp.clip(k - starts[p], 0, counts[p] - 1))

    in_specs = [pl.BlockSpec((tm, tks[p]), piece_block(p)) for p in range(npieces)]
    in_specs += [pl.BlockSpec((D_MODEL, tks[p]), weight_block(p)) for p in range(npieces)]
    in_specs += [pl.BlockSpec((tm, D_MODEL), lambda i, k: (i, 0))]

    def body(*refs):
        dz_refs, w_refs = refs[:npieces], refs[npieces:2 * npieces]
        r_ref, o_ref, acc_ref = refs[2 * npieces], refs[2 * npieces + 1], refs[2 * npieces + 2]
        k = pl.program_id(1)

        @pl.when(k == 0)
        def _():
            acc_ref[...] = ALPHA * r_ref[...]

        for p in range(npieces):
            @pl.when((k >= starts[p]) & (k < starts[p] + counts[p]))
            def _(p=p):
                acc_ref[...] += _dot(dz_refs[p][...], w_refs[p][...], NT)

        @pl.when(k == nk - 1)
        def _():
            o_ref[...] = acc_ref[...]

    return pl.pallas_call(
        body, name=name, out_shape=jax.ShapeDtypeStruct((s, D_MODEL), F32), grid=(s // tm, nk), in_specs=in_specs,
        out_specs=pl.BlockSpec((tm, D_MODEL), lambda i, k: (i, 0)), scratch_shapes=[pltpu.VMEM((tm, D_MODEL), F32)],
        compiler_params=_params(("parallel", "arbitrary")),
    )(*dzs, *wis, resid)


def _lower_bound(lbl_ref):
    l0, l1 = lbl_ref[0:1, :], lbl_ref[1:2, :]
    mx = jnp.maximum(l0, l1)
    e0, e1 = jnp.exp(l0 - mx), jnp.exp(l1 - mx)
    return e0 / (e0 + e1)


def _chunk_forward(q, fl, v, lb, tril_f):
    sg = jax.nn.sigmoid(fl)
    f = lb + (1.0 - lb) * sg
    k = 1.0 - f
    b = _dot(tril_f, jnp.log(f), NN, HIGHEST)
    b_last = b[HG_CHUNK - 1:HG_CHUNK, :]
    eb, enb, eo = jnp.exp(b), jnp.exp(-b), jnp.exp(b_last - b)
    return sg, f, k, b_last, eb, enb, eo, q * eb, k * enb, k * eo


def _hgrn_fwd(za, lb_logits, gain, *, name):
    s = za.shape[0]
    t = min(256, s)
    ncs = t // HG_CHUNK

    def body(z_ref, lbl_ref, gain_ref, oa_ref, oraw_ref, st_ref, state):
        @pl.when(pl.program_id(0) == 0)
        def _():
            state[...] = jnp.zeros_like(state)

        lb_all = _lower_bound(lbl_ref)
        row = lax.broadcasted_iota(jnp.int32, (HG_CHUNK, HG_CHUNK), 0)
        col = lax.broadcasted_iota(jnp.int32, (HG_CHUNK, HG_CHUNK), 1)
        tril = row >= col
        tril_f = tril.astype(F32)
        for h in range(HG_HEADS):
            cols = slice(h * HG_DK, (h + 1) * HG_DK)
            lb = lb_all[:, cols]
            gn = gain_ref[:, cols]

            def chunk(i, carry, h=h, lb=lb, gn=gn):
                r = pl.ds(pl.multiple_of(i * HG_CHUNK, HG_CHUNK), HG_CHUNK)
                q = z_ref[r, h * HG_DK:(h + 1) * HG_DK]
                fl = z_ref[r, 1024 + h * HG_DK:1024 + (h + 1) * HG_DK]
                v = z_ref[r, 2048 + h * HG_DK:2048 + (h + 1) * HG_DK]
                hg = z_ref[r, 3072 + h * HG_DK:3072 + (h + 1) * HG_DK]
                _, _, _, b_last, _, _, _, q_in, k_in, k_out = _chunk_forward(q, fl, v, lb, tril_f)
                q_in_b, vb = q_in.astype(BF16), v.astype(BF16)
                attn = jnp.where(tril, _dot(q_in_b, k_in.astype(BF16), NT), 0.0)
                st = state[h]
                st_ref[h, i] = st
                o = _dot(attn.astype(BF16), vb, NN) + _dot(q_in_b, st.astype(BF16), NT)
                state[h] = st * jnp.exp(b_last) + _dot(vb, k_out.astype(BF16), TN)
                oraw_ref[r, h * HG_DK:(h + 1) * HG_DK] = o
                n = o * lax.rsqrt(jnp.mean(o * o, axis=-1, keepdims=True) + RMS_EPS)
                oa_ref[r, h * HG_DK:(h + 1) * HG_DK] = (n * gn * (hg * jax.nn.sigmoid(hg))).astype(BF16)
                return carry

            lax.fori_loop(0, ncs, chunk, 0)

    return pl.pallas_call(
        body, name=name, grid=(s // t,),
        out_shape=(jax.ShapeDtypeStruct((s, D_MODEL), BF16), jax.ShapeDtypeStruct((s, D_MODEL), F32),
                   jax.ShapeDtypeStruct((HG_HEADS, s // HG_CHUNK, HG_DK, HG_DK), F32)),
        in_specs=[pl.BlockSpec((t, W_A), lambda i: (i, 0)), _resident((2, D_MODEL)), _resident((1, D_MODEL))],
        out_specs=(pl.BlockSpec((t, D_MODEL), lambda i: (i, 0)), pl.BlockSpec((t, D_MODEL), lambda i: (i, 0)),
                   pl.BlockSpec((HG_HEADS, ncs, HG_DK, HG_DK), lambda i: (0, i, 0, 0))),
        scratch_shapes=[pltpu.VMEM((HG_HEADS, HG_DK, HG_DK), F32)],
        compiler_params=_params(("arbitrary",)),
    )(za, lb_logits, gain)


def _hgrn_bwd(za, oraw, do_a, states, lb_logits, gain, *, name):
    s = za.shape[0]
    t = min(256, s)
    ncs = t // HG_CHUNK
    nt = s // t

    def body(z_ref, oraw_ref, do_ref, st_ref, lbl_ref, gain_ref, dz_ref, stats_ref, dstate):
        step = pl.program_id(0)

        @pl.when(step == 0)
        def _():
            dstate[...] = jnp.zeros_like(dstate)
            stats_ref[...] = jnp.zeros_like(stats_ref)

        lb_all = _lower_bound(lbl_ref)
        row = lax.broadcasted_iota(jnp.int32, (HG_CHUNK, HG_CHUNK), 0)
        col = lax.broadcasted_iota(jnp.int32, (HG_CHUNK, HG_CHUNK), 1)
        tril = row >= col
        tril_f = tril.astype(F32)
        triu_f = (row <= col).astype(F32)
        for h in range(HG_HEADS):
            cols = slice(h * HG_DK, (h + 1) * HG_DK)
            lb = lb_all[:, cols]
            gn = gain_ref[:, cols]

            def chunk(ii, carry, h=h, lb=lb, gn=gn, cols=cols):
                i = ncs - 1 - ii
                r = pl.ds(pl.multiple_of(i * HG_CHUNK, HG_CHUNK), HG_CHUNK)
                q = z_ref[r, h * HG_DK:(h + 1) * HG_DK]
                fl = z_ref[r, 1024 + h * HG_DK:1024 + (h + 1) * HG_DK]
                v = z_ref[r, 2048 + h * HG_DK:2048 + (h + 1) * HG_DK]
                hg = z_ref[r, 3072 + h * HG_DK:3072 + (h + 1) * HG_DK]
                o = oraw_ref[r, h * HG_DK:(h + 1) * HG_DK]
                doa = do_ref[r, h * HG_DK:(h + 1) * HG_DK]
                rms = lax.rsqrt(jnp.mean(o * o, axis=-1, keepdims=True) + RMS_EPS)
                n = o * rms
                sgg = jax.nn.sigmoid(hg)
                silu = hg * sgg
                dhg = doa * n * gn * (sgg * (1.0 + hg * (1.0 - sgg)))
                stats_ref[0:1, cols] += jnp.sum(doa * n * silu, axis=0, keepdims=True)
                dn = doa * gn * silu
                do = rms * (dn - n * jnp.mean(dn * n, axis=-1, keepdims=True))
                sg, f, k, b_last, eb, enb, eo, q_in, k_in, k_out = _chunk_forward(q, fl, v, lb, tril_f)
                q_in_b, k_in_b, k_out_b, vb, dob = (u.astype(BF16) for u in (q_in, k_in, k_out, v, do))
                attn = jnp.where(tril, _dot(q_in_b, k_in_b, NT), 0.0)
                st = st_ref[h, i]
                dst = dstate[h]
                dst_b = dst.astype(BF16)
                decay = jnp.exp(b_last)
                dattn = jnp.where(tril, _dot(dob, vb, NT), 0.0).astype(BF16)
                dq_in = _dot(dob, st.astype(BF16), NN) + _dot(dattn, k_in_b, NN)
                dk_in = _dot(dattn, q_in_b, TN)
                dk_out = _dot(vb, dst_b, NN)
                dv = _dot(attn.astype(BF16), dob, TN) + _dot(k_out_b, dst_b, NT)
                db_last = decay * jnp.sum(dst * st, axis=0, keepdims=True) + jnp.sum(dk_out * k_out, axis=0, keepdims=True)
                dstate[h] = dst * decay + _dot(dob, q_in_b, TN)
                db = dq_in * q_in - dk_in * k_in - dk_out * k_out
                dg = _dot(triu_f, db, NN, HIGHEST) + db_last
                dk = dk_in * enb + dk_out * eo
                df = dg / f - dk
                stats_ref[1:2, cols] += jnp.sum(df * (1.0 - sg), axis=0, keepdims=True)
                dz_ref[r, h * HG_DK:(h + 1) * HG_DK] = (dq_in * eb).astype(BF16)
                dz_ref[r, 1024 + h * HG_DK:1024 + (h + 1) * HG_DK] = (df * (1.0 - lb) * sg * (1.0 - sg)).astype(BF16)
                dz_ref[r, 2048 + h * HG_DK:2048 + (h + 1) * HG_DK] = dv.astype(BF16)
                dz_ref[r, 3072 + h * HG_DK:3072 + (h + 1) * HG_DK] = dhg.astype(BF16)
                return carry

            lax.fori_loop(0, ncs, chunk, 0)

        @pl.when(step == nt - 1)
        def _():
            dl0 = stats_ref[1:2, :] * lb_all * (1.0 - lb_all)
            stats_ref[1:2, :] = dl0
            stats_ref[2:3, :] = -dl0

    rev = lambda i: (nt - 1 - i, 0)
    return pl.pallas_call(
        body, name=name, grid=(nt,),
        out_shape=(jax.ShapeDtypeStruct((s, W_A), BF16), jax.ShapeDtypeStruct((8, D_MODEL), F32)),
        in_specs=[pl.BlockSpec((t, W_A), rev), pl.BlockSpec((t, D_MODEL), rev), pl.BlockSpec((t, D_MODEL), rev),
                  pl.BlockSpec((HG_HEADS, ncs, HG_DK, HG_DK), lambda i: (0, nt - 1 - i, 0, 0)),
                  _resident((2, D_MODEL)), _resident((1, D_MODEL))],
        out_specs=(pl.BlockSpec((t, W_A), rev), pl.BlockSpec((8, D_MODEL), lambda i: (0, 0))),
        scratch_shapes=[pltpu.VMEM((HG_HEADS, HG_DK, HG_DK), F32)],
        compiler_params=_params(("arbitrary",)),
    )(za, oraw, do_a, states, lb_logits, gain)


def _t5_bucket(n):
    max_exact = NUM_BUCKETS // 2
    nf = jnp.maximum(n, 1).astype(F32)
    large = max_exact + (jnp.log(nf / max_exact) / math.log(MAX_DISTANCE / max_exact) * (NUM_BUCKETS - max_exact)).astype(jnp.int32)
    large = jnp.minimum(large, NUM_BUCKETS - 1)
    return jnp.where(n < max_exact, n, large)


def _bias_selector():
    qi = jnp.arange(SWA_BLOCK)[:, None] + SWA_BLOCK
    kj = jnp.arange(2 * SWA_BLOCK)[None, :]
    dist = qi - kj
    band = ((dist >= 0) & (dist < SWA_WINDOW)).reshape(1, -1)
    bucket = _t5_bucket(jnp.clip(dist, 0, SWA_WINDOW - 1)).reshape(1, -1)
    onehot = ((bucket == jnp.arange(NUM_BUCKETS)[:, None]) & band).astype(F32)
    return onehot, jnp.where(band, 0.0, MASK_VALUE).astype(F32)


def _bias_table(rel_bias_t, onehot, maskrow, *, name):
    def body(rb_ref, oh_ref, mask_ref, o_ref):
        o_ref[...] = _dot(rb_ref[...], oh_ref[...], NN, HIGHEST) + mask_ref[...]

    return pl.pallas_call(body, name=name, out_shape=jax.ShapeDtypeStruct((SWA_HEADS, onehot.shape[1]), F32),
                          compiler_params=_params())(rel_bias_t, onehot, maskrow)


def _bias_grad(dbias2d, onehot, *, name):
    def body(db_ref, oh_ref, o_ref):
        o_ref[...] = _dot(db_ref[...], oh_ref[...], NT, HIGHEST)

    return pl.pallas_call(body, name=name, out_shape=jax.ShapeDtypeStruct((SWA_HEADS, NUM_BUCKETS), F32),
                          compiler_params=_params())(dbias2d, onehot)


def _swa_scores(zq_ref, kv_cur_ref, kv_prev_ref, bias_ref, first):
    q = (zq_ref[:, 0:1024] * (SWA_HEAD_DIM ** -0.5)).astype(BF16)
    kv_c = kv_cur_ref[...].astype(BF16)
    kv_p = kv_prev_ref[...].astype(BF16)
    key = lax.broadcasted_iota(jnp.int32, (1, 2 * SWA_BLOCK), 1)
    first_mask = jnp.where(first & (key < SWA_BLOCK), MASK_VALUE, 0.0)
    return q, kv_c, kv_p, first_mask


def _swa_fwd(zb, bias, sinks, *, name):
    s = zb.shape[0]
    nb = s // SWA_BLOCK

    def body(zq_ref, kvc_ref, kvp_ref, bias_ref, sink_ref, o_ref, lse_ref):
        n = pl.program_id(0)
        q, kv_c, kv_p, first_mask = _swa_scores(zq_ref, kvc_ref, kvp_ref, bias_ref, n == 0)
        lses = []
        outs = []
        for g in range(SWA_KV_HEADS):
            kk = jnp.concatenate([kv_p[:, g * 64:(g + 1) * 64], kv_c[:, g * 64:(g + 1) * 64]], axis=0)
            vv = jnp.concatenate([kv_p[:, 128 + g * 64:128 + (g + 1) * 64], kv_c[:, 128 + g * 64:128 + (g + 1) * 64]], axis=0)
            for j in range(SWA_GROUP):
                h = g * SWA_GROUP + j
                sc = _dot(q[:, h * 64:(h + 1) * 64], kk, NT) + bias_ref[h] + first_mask
                sink = sink_ref[0:1, h:h + 1]
                m = jnp.maximum(jnp.max(sc, axis=-1, keepdims=True), sink)
                p = jnp.exp(sc - m)
                den = jnp.sum(p, axis=-1, keepdims=True) + jnp.exp(sink - m)
                outs.append(_dot((p / den).astype(BF16), vv, NN))
                lses.append(m + jnp.log(den))
        o_ref[...] = jnp.concatenate(outs, axis=1).astype(BF16)
        lse_ref[...] = jnp.concatenate(lses, axis=1)

    return pl.pallas_call(
        body, name=name, grid=(nb,),
        out_shape=(jax.ShapeDtypeStruct((s, D_MODEL), BF16), jax.ShapeDtypeStruct((s, SWA_HEADS), F32)),
        in_specs=[pl.BlockSpec((SWA_BLOCK, W_B), lambda n: (n, 0)),
                  pl.BlockSpec((SWA_BLOCK, 256), lambda n: (n, 4)),
                  pl.BlockSpec((SWA_BLOCK, 256), lambda n: (jnp.maximum(n - 1, 0), 4)),
                  _resident((SWA_HEADS, SWA_BLOCK, 2 * SWA_BLOCK)), _resident((1, SWA_HEADS))],
        out_specs=(pl.BlockSpec((SWA_BLOCK, D_MODEL), lambda n: (n, 0)), pl.BlockSpec((SWA_BLOCK, SWA_HEADS), lambda n: (n, 0))),
        compiler_params=_params(("arbitrary",)),
    )(zb, zb, zb, bias, sinks)


def _swa_bwd(zb, do_b, lse, bias, sinks, *, name):
    s = zb.shape[0]
    nb = s // SWA_BLOCK
    scale = SWA_HEAD_DIM ** -0.5

    def body(zq_ref, kvc_ref, kvp_ref, do_ref, lse_ref, bias_ref, sink_ref, dz_ref, dbias_ref, dsink_ref, carry, dsink_acc):
        step = pl.program_id(0)
        n = nb - 1 - step

        @pl.when(step == 0)
        def _():
            carry[...] = jnp.zeros_like(carry)
            dsink_acc[...] = jnp.zeros_like(dsink_acc)
            dbias_ref[...] = jnp.zeros_like(dbias_ref)

        q, kv_c, kv_p, first_mask = _swa_scores(zq_ref, kvc_ref, kvp_ref, bias_ref, n == 0)
        dqs, dkks, dvvs, dsk = [], [], [], []
        for g in range(SWA_KV_HEADS):
            kk = jnp.concatenate([kv_p[:, g * 64:(g + 1) * 64], kv_c[:, g * 64:(g + 1) * 64]], axis=0)
            vv = jnp.concatenate([kv_p[:, 128 + g * 64:128 + (g + 1) * 64], kv_c[:, 128 + g * 64:128 + (g + 1) * 64]], axis=0)
            dkk = jnp.zeros((2 * SWA_BLOCK, SWA_HEAD_DIM), F32)
            dvv = jnp.zeros((2 * SWA_BLOCK, SWA_HEAD_DIM), F32)
            for j in range(SWA_GROUP):
                h = g * SWA_GROUP + j
                qh = q[:, h * 64:(h + 1) * 64]
                doh = do_ref[:, h * 64:(h + 1) * 64].astype(BF16)
                lse_h = lse_ref[:, h:h + 1]
                sc = _dot(qh, kk, NT) + bias_ref[h] + first_mask
                p = jnp.exp(sc - lse_h)
                dp = _dot(doh, vv, NT)
                delta = jnp.sum(p * dp, axis=-1, keepdims=True)
                ds = p * (dp - delta)
                dbias_ref[h] += ds
                dsk.append(-jnp.exp(sink_ref[0:1, h:h + 1] - lse_h) * delta)
                ds_b = ds.astype(BF16)
                dqs.append(_dot(ds_b, kk, NN) * scale)
                dkk = dkk + _dot(ds_b, qh, TN)
                dvv = dvv + _dot(p.astype(BF16), doh, TN)
            dkks.append(dkk)
            dvvs.append(dvv)
        dsink_acc[...] += jnp.concatenate(dsk, axis=1)
        dkv = jnp.concatenate(dkks + dvvs, axis=1)
        dz_ref[:, 0:1024] = jnp.concatenate(dqs, axis=1).astype(BF16)
        dz_ref[:, 1024:1280] = (dkv[SWA_BLOCK:, :] + carry[...]).astype(BF16)
        carry[...] = dkv[:SWA_BLOCK, :]

        @pl.when(step == nb - 1)
        def _():
            dsink_ref[...] = jnp.sum(dsink_acc[...], axis=0, keepdims=True)

    rev = lambda i: (nb - 1 - i, 0)
    return pl.pallas_call(
        body, name=name, grid=(nb,),
        out_shape=(jax.ShapeDtypeStruct((s, W_B), BF16), jax.ShapeDtypeStruct((SWA_HEADS, SWA_BLOCK, 2 * SWA_BLOCK), F32),
                   jax.ShapeDtypeStruct((1, SWA_HEADS), F32)),
        in_specs=[pl.BlockSpec((SWA_BLOCK, W_B), rev),
                  pl.BlockSpec((SWA_BLOCK, 256), lambda i: (nb - 1 - i, 4)),
                  pl.BlockSpec((SWA_BLOCK, 256), lambda i: (jnp.maximum(nb - 2 - i, 0), 4)),
                  pl.BlockSpec((SWA_BLOCK, D_MODEL), rev), pl.BlockSpec((SWA_BLOCK, SWA_HEADS), rev),
                  _resident((SWA_HEADS, SWA_BLOCK, 2 * SWA_BLOCK)), _resident((1, SWA_HEADS))],
        out_specs=(pl.BlockSpec((SWA_BLOCK, W_B), rev),
                   pl.BlockSpec((SWA_HEADS, SWA_BLOCK, 2 * SWA_BLOCK), lambda i: (0, 0, 0)),
                   pl.BlockSpec((1, SWA_HEADS), lambda i: (0, 0))),
        scratch_shapes=[pltpu.VMEM((SWA_BLOCK, 256), F32), pltpu.VMEM((SWA_BLOCK, SWA_HEADS), F32)],
        compiler_params=_params(("arbitrary",)),
    )(zb, zb, zb, do_b, lse, bias, sinks)


def _mem_probs(zc_ref, mkv_ref, h):
    cols = slice(h * MEM_HEAD_DIM, (h + 1) * MEM_HEAD_DIM)
    qh = (zc_ref[:, cols] * (MEM_HEAD_DIM ** -0.5)).astype(BF16)
    sc = _dot(qh, mkv_ref[:, cols], NT)
    e = jnp.exp(sc - jnp.max(sc, axis=-1, keepdims=True))
    return qh, e / jnp.sum(e, axis=-1, keepdims=True)


def _mem_fwd(zc, mkv, *, name):
    s = zc.shape[0]
    t = min(512, s)

    def body(zc_ref, mkv_ref, o_ref):
        for h in range(MEM_HEADS):
            _, p = _mem_probs(zc_ref, mkv_ref, h)
            vh = mkv_ref[:, D_MODEL + h * MEM_HEAD_DIM:D_MODEL + (h + 1) * MEM_HEAD_DIM]
            o_ref[:, h * MEM_HEAD_DIM:(h + 1) * MEM_HEAD_DIM] = _dot(p.astype(BF16), vh, NN).astype(BF16)

    return pl.pallas_call(
        body, name=name, grid=(s // t,), out_shape=jax.ShapeDtypeStruct((s, D_MODEL), BF16),
        in_specs=[pl.BlockSpec((t, D_MODEL), lambda i: (i, 0)), _resident((MEM_LEN, 2 * D_MODEL))],
        out_specs=pl.BlockSpec((t, D_MODEL), lambda i: (i, 0)), compiler_params=_params(("parallel",)),
    )(zc, mkv)


def _mem_bwd(zc, do_c, mkv, *, name):
    s = zc.shape[0]
    t = min(512, s)

    def body(zc_ref, do_ref, mkv_ref, dz_ref, dmkv_ref):
        @pl.when(pl.program_id(0) == 0)
        def _():
            dmkv_ref[...] = jnp.zeros_like(dmkv_ref)

        for h in range(MEM_HEADS):
            cols = slice(h * MEM_HEAD_DIM, (h + 1) * MEM_HEAD_DIM)
            vcols = slice(D_MODEL + h * MEM_HEAD_DIM, D_MODEL + (h + 1) * MEM_HEAD_DIM)
            qh, p = _mem_probs(zc_ref, mkv_ref, h)
            doh = do_ref[:, cols].astype(BF16)
            dp = _dot(doh, mkv_ref[:, vcols], NT)
            ds = (p * (dp - jnp.sum(p * dp, axis=-1, keepdims=True))).astype(BF16)
            dz_ref[:, cols] = (_dot(ds, mkv_ref[:, cols], NN) * (MEM_HEAD_DIM ** -0.5)).astype(BF16)
            dmkv_ref[:, cols] += _dot(ds, qh, TN)
            dmkv_ref[:, vcols] += _dot(p.astype(BF16), doh, TN)

    return pl.pallas_call(
        body, name=name, grid=(s // t,),
        out_shape=(jax.ShapeDtypeStruct((s, D_MODEL), BF16), jax.ShapeDtypeStruct((MEM_LEN, 2 * D_MODEL), F32)),
        in_specs=[pl.BlockSpec((t, D_MODEL), lambda i: (i, 0)), pl.BlockSpec((t, D_MODEL), lambda i: (i, 0)),
                  _resident((MEM_LEN, 2 * D_MODEL))],
        out_specs=(pl.BlockSpec((t, D_MODEL), lambda i: (i, 0)), pl.BlockSpec((MEM_LEN, 2 * D_MODEL), lambda i: (0, 0))),
        compiler_params=_params(("arbitrary",)),
    )(zc, do_c, mkv)


def _normalize(pre):
    mu = jnp.mean(pre, axis=-1, keepdims=True)
    xc = pre - mu
    rstd = lax.rsqrt(jnp.mean(xc * xc, axis=-1, keepdims=True) + LN_EPS)
    return xc * rstd, rstd


def _layer_norm_bwd(dh, xhat, rstd, g):
    dxh = dh * g
    dpre = rstd * (dxh - jnp.mean(dxh, axis=-1, keepdims=True) - xhat * jnp.mean(dxh * xhat, axis=-1, keepdims=True))
    return dpre, jnp.sum(dh * xhat, axis=0, keepdims=True), jnp.sum(dh, axis=0, keepdims=True)


def _merge_fwd(o_a, o_b, o_c, zd, x, wbr, wo, *, name):
    s = x.shape[0]
    t = min(256, s)
    row = lambda w, dt=None: pl.BlockSpec((t, w), lambda i: (i, 0))

    def body(oa_ref, ob_ref, oc_ref, zd_ref, x_ref, wbr_ref, wo_ref, xhat_ref, rstd_ref, merged_ref, pa_ref, pb_ref, pc_ref):
        merged = jnp.zeros((t, D_MODEL), F32)
        for b, (o_ref, p_ref) in enumerate(((oa_ref, pa_ref), (ob_ref, pb_ref), (oc_ref, pc_ref))):
            p = _dot(o_ref[...], wbr_ref[b], NN)
            p_ref[...] = p
            merged = merged + jax.nn.sigmoid(zd_ref[:, b * D_MODEL:(b + 1) * D_MODEL]) * p
        merged_b = merged.astype(BF16)
        merged_ref[...] = merged_b
        xhat, rstd = _normalize(ALPHA * x_ref[...] + _dot(merged_b, wo_ref[...], NN))
        xhat_ref[...] = xhat
        rstd_ref[...] = rstd

    act = jax.ShapeDtypeStruct((s, D_MODEL), F32)
    return pl.pallas_call(
        body, name=name, grid=(s // t,),
        out_shape=(act, jax.ShapeDtypeStruct((s, 1), F32), jax.ShapeDtypeStruct((s, D_MODEL), BF16), act, act, act),
        in_specs=[row(D_MODEL), row(D_MODEL), row(D_MODEL), row(W_D), row(D_MODEL),
                  _resident((3, D_MODEL, D_MODEL)), _resident((D_MODEL, D_MODEL))],
        out_specs=(row(D_MODEL), row(1), row(D_MODEL), row(D_MODEL), row(D_MODEL), row(D_MODEL)),
        compiler_params=_params(("parallel",)),
    )(o_a, o_b, o_c, zd, x, wbr, wo)


def _merge_bwd(dpre1, zd, pa, pb, pc, wbr, wo, *, name):
    s = dpre1.shape[0]
    t = min(256, s)
    row = lambda w: pl.BlockSpec((t, w), lambda i: (i, 0))

    def body(dpre_ref, zd_ref, pa_ref, pb_ref, pc_ref, wbr_ref, wo_ref, dzd_ref, dpa_ref, dpb_ref, dpc_ref, doa_ref, dob_ref, doc_ref):
        dmerged = _dot(dpre_ref[...].astype(BF16), wo_ref[...], NT)
        branches = ((pa_ref, dpa_ref, doa_ref), (pb_ref, dpb_ref, dob_ref), (pc_ref, dpc_ref, doc_ref))
        for b, (p_ref, dp_ref, do_ref) in enumerate(branches):
            gate = jax.nn.sigmoid(zd_ref[:, b * D_MODEL:(b + 1) * D_MODEL])
            dzd_ref[:, b * D_MODEL:(b + 1) * D_MODEL] = (dmerged * p_ref[...] * gate * (1.0 - gate)).astype(BF16)
            dp = (dmerged * gate).astype(BF16)
            dp_ref[...] = dp
            do_ref[...] = _dot(dp, wbr_ref[b], NT)

    act = jax.ShapeDtypeStruct((s, D_MODEL), F32)
    actb = jax.ShapeDtypeStruct((s, D_MODEL), BF16)
    return pl.pallas_call(
        body, name=name, grid=(s // t,),
        out_shape=(jax.ShapeDtypeStruct((s, W_D), BF16), actb, actb, actb, act, act, act),
        in_specs=[row(D_MODEL), row(W_D), row(D_MODEL), row(D_MODEL), row(D_MODEL),
                  _resident((3, D_MODEL, D_MODEL)), _resident((D_MODEL, D_MODEL))],
        out_specs=(row(W_D),) + (row(D_MODEL),) * 6,
        compiler_params=_params(("parallel",)),
    )(dpre1, zd, pa, pb, pc, wbr, wo)


def _mlp_loss(xhat1, rstd1, target, ln1_g, ln1_b, ln2_g, ln2_b, wu, wd, *, name):
    s = xhat1.shape[0]
    t = min(256, s)
    npan = wu.shape[0]
    row = lambda w: pl.BlockSpec((t, w), lambda i: (i, 0))
    vec = _resident((1, D_MODEL))

    def body(xhat_ref, rstd_ref, tgt_ref, g1_ref, b1_ref, g2_ref, b2_ref, wu_ref, wd_ref,
             dpre1_ref, dpre2_ref, h1_ref, a_ref, du_ref, stats_ref):
        @pl.when(pl.program_id(0) == 0)
        def _():
            stats_ref[...] = jnp.zeros_like(stats_ref)

        xhat1_v = xhat_ref[...]
        h1 = xhat1_v * g1_ref[...] + b1_ref[...]
        h1_b = h1.astype(BF16)
        h1_ref[...] = h1_b
        us = []
        ff = jnp.zeros((t, D_MODEL), F32)
        for j in range(npan):
            u = _dot(h1_b, wu_ref[j], NN)
            us.append(u)
            r = jnp.maximum(u, 0.0)
            a_b = (r * r).astype(BF16)
            a_ref[:, j * D_MODEL:(j + 1) * D_MODEL] = a_b
            ff = ff + _dot(a_b, wd_ref[j], NN)
        xhat2, rstd2 = _normalize(ALPHA * h1 + ff)
        err = xhat2 * g2_ref[...] + b2_ref[...] - tgt_ref[...]
        stats_ref[4:5, :] += jnp.sum(err * err, axis=0, keepdims=True)
        dpre2, dg2, db2 = _layer_norm_bwd(err * (1.0 / D_MODEL), xhat2, rstd2, g2_ref[...])
        stats_ref[0:1, :] += dg2
        stats_ref[1:2, :] += db2
        dpre2_b = dpre2.astype(BF16)
        dpre2_ref[...] = dpre2_b
        dh1 = ALPHA * dpre2
        for j in range(npan):
            du_b = (_dot(dpre2_b, wd_ref[j], NT) * (2.0 * jnp.maximum(us[j], 0.0))).astype(BF16)
            du_ref[:, j * D_MODEL:(j + 1) * D_MODEL] = du_b
            dh1 = dh1 + _dot(du_b, wu_ref[j], NT)
        dpre1, dg1, db1 = _layer_norm_bwd(dh1, xhat1_v, rstd_ref[...], g1_ref[...])
        stats_ref[2:3, :] += dg1
        stats_ref[3:4, :] += db1
        dpre1_ref[...] = dpre1

    actb = jax.ShapeDtypeStruct((s, D_MODEL), BF16)
    wide = jax.ShapeDtypeStruct((s, D_FF), BF16)
    return pl.pallas_call(
        body, name=name, grid=(s // t,),
        out_shape=(jax.ShapeDtypeStruct((s, D_MODEL), F32), actb, actb, wide, wide, jax.ShapeDtypeStruct((8, D_MODEL), F32)),
        in_specs=[row(D_MODEL), row(1), row(D_MODEL), vec, vec, vec, vec,
                  _resident((npan, D_MODEL, D_MODEL)), _resident((npan, D_MODEL, D_MODEL))],
        out_specs=(row(D_MODEL), row(D_MODEL), row(D_MODEL), row(D_FF), row(D_FF), pl.BlockSpec((8, D_MODEL), lambda i: (0, 0))),
        compiler_params=_params(("arbitrary",)),
    )(xhat1, rstd1, target, ln1_g, ln1_b, ln2_g, ln2_b, wu, wd)


def _local_step(x, mem, target, wi_parts, wmkv, wbr, wo, wu, wd, lb_logits, gain, sinks, rel_bias, ln1_g, ln1_b, ln2_g, ln2_b):
    s = x.shape[0]
    tm = min(1024, s)
    tk = min(512, s)
    xb = x.astype(BF16)
    memb = mem.astype(BF16)
    wia, wib, wic, wid = wi_parts

    za = _mm(xb, wia, mode="nn", tm=min(512, s), tn=W_A, tk=D_MODEL, name="proj_a")
    zb = _mm(xb, wib, mode="nn", tm=tm, tn=W_B, tk=D_MODEL, name="proj_b")
    zc = _mm(xb, wic, mode="nn", tm=tm, tn=W_C, tk=D_MODEL, name="proj_c")
    zd = _mm(xb, wid, mode="nn", tm=min(512, s), tn=W_D, tk=D_MODEL, name="proj_d")
    mkv = _mm(memb, wmkv, mode="nn", tm=MEM_LEN, tn=512, tk=D_MODEL, name="mem_kv", out_dtype=BF16, b_panels=True)
    onehot, maskrow = _bias_selector()
    bias = _bias_table(rel_bias.T, onehot, maskrow, name="bias_table").reshape(SWA_HEADS, SWA_BLOCK, 2 * SWA_BLOCK)
    o_a, o_raw, states = _hgrn_fwd(za, lb_logits, gain, name="hgrn_fwd")
    o_b, lse = _swa_fwd(zb, bias, sinks, name="swa_fwd")
    o_c = _mem_fwd(zc, mkv, name="mem_fwd")
    xhat1, rstd1, merged, pa, pb, pc = _merge_fwd(o_a, o_b, o_c, zd, x, wbr, wo, name="merge_fwd")

    dpre1, dpre2, h1, act, du, ln_stats = _mlp_loss(xhat1, rstd1, target, ln1_g, ln1_b, ln2_g, ln2_b, wu, wd, name="mlp_loss")
    g_wd = _mm(act, dpre2, mode="tn", tm=1024, tn=D_MODEL, tk=tk, name="grad_w_down")
    g_wu = _mm(h1, du, mode="tn", tm=D_MODEL, tn=1024, tk=tk, name="grad_w_up", out_panels=True)

    dzd, dpa, dpb, dpc, do_a, do_b, do_c = _merge_bwd(dpre1, zd, pa, pb, pc, wbr, wo, name="merge_bwd")
    g_wo = _mm(merged, dpre1, mode="tn", tm=D_MODEL, tn=D_MODEL, tk=tk, name="grad_w_out")
    g_wbh = _mm(o_a, dpa, mode="tn", tm=D_MODEL, tn=D_MODEL, tk=tk, name="grad_w_branch_hg")
    g_wbs = _mm(o_b, dpb, mode="tn", tm=D_MODEL, tn=D_MODEL, tk=tk, name="grad_w_branch_swa")
    g_wbm = _mm(o_c, dpc, mode="tn", tm=D_MODEL, tn=D_MODEL, tk=tk, name="grad_w_branch_mem")
    dza, hg_stats = _hgrn_bwd(za, o_raw, do_a, states, lb_logits, gain, name="hgrn_bwd")
    dzb, dbias, dsinks = _swa_bwd(zb, do_b, lse, bias, sinks, name="swa_bwd")
    d_rel_bias = _bias_grad(dbias.reshape(SWA_HEADS, -1), onehot, name="bias_grad").T
    dzc, dmkv = _mem_bwd(zc, do_c, mkv, name="mem_bwd")
    g_wmkv = _mm(memb, dmkv, mode="tn", tm=D_MODEL, tn=512, tk=MEM_LEN, name="grad_w_mem_kv", out_panels=True)

    g_wi = [_mm(xb, dz, mode="tn", tm=min(512, D_MODEL), tn=dz.shape[1] if dz.shape[1] <= 1280 else 1024, tk=tk, name=nm)
            for dz, nm in ((dza, "grad_w_in_a"), (dzb, "grad_w_in_b"), (dzc, "grad_w_in_c"), (dzd, "grad_w_in_d"))]
    grad_x = _dx_matmul([dza, dzb, dzc, dzd], [wia, wib, wic, wid], dpre1, tm=min(512, s), tks=[512, W_B, 512, 512], name="grad_x")
    big = dict(w_in=jnp.concatenate(g_wi, axis=1), w_mem_kv=g_wmkv, w_branch_hg=g_wbh, w_branch_swa=g_wbs, w_branch_mem=g_wbm,
               w_out=g_wo, w_up=g_wu, w_down=g_wd.reshape(N_SHARDS, D_FF // N_SHARDS, D_MODEL))
    small = dict(lb_logits=hg_stats[1:3], hg_norm_gain=hg_stats[0:1], swa_sinks=dsinks, rel_bias=d_rel_bias,
                 ln1_g=ln_stats[2:3], ln1_b=ln_stats[3:4], ln2_g=ln_stats[0:1], ln2_b=ln_stats[1:2], sq_err=ln_stats[4:5])
    return grad_x, big, small


def _mesh_position():
    x, y, c = lax.axis_index("x"), lax.axis_index("y"), lax.axis_index("c")
    chips = [(1 - x, y), (x, 1 - y), (1 - x, 1 - y)]
    return x, y, c, chips


def _all_gather(shards, *, name):
    n = len(shards)

    def body(*refs):
        ins, outs = refs[:n], refs[n:2 * n]
        send_sems, recv_sems, local_sems = refs[2 * n:]
        x, y, c, chips = _mesh_position()
        me = 2 * x + y
        sibling = (x, y, 1 - c)

        def half(a, slot, hc):
            rh = shards[a].shape[0] // 2
            return outs[a].at[slot, pl.ds(hc * rh, rh), :]

        def copy(a, k, src, dst, to):
            return pltpu.make_async_remote_copy(src_ref=src, dst_ref=dst, send_sem=send_sems.at[a * 6 + k], recv_sem=recv_sems.at[a * 6 + k],
                                                device_id=to, device_id_type=MESH)

        local = [pltpu.make_async_copy(ins[a], outs[a].at[me], local_sems.at[a]) for a in range(n)]
        for cp in local:
            cp.start()
        started = []
        for k, (cx, cy) in enumerate(chips):
            for a in range(n):
                rh = shards[a].shape[0] // 2
                cp = copy(a, k, ins[a].at[pl.ds(c * rh, rh), :], half(a, me, c), (cx, cy, c))
                cp.start()
                started.append(cp)
        for k, (cx, cy) in enumerate(chips):
            slot = 2 * cx + cy
            for a in range(n):
                copy(a, k, half(a, slot, c), half(a, slot, c), (cx, cy, c)).wait_recv()
                cp = copy(a, 3 + k, half(a, slot, c), half(a, slot, c), sibling)
                cp.start()
                started.append(cp)
        for k, (cx, cy) in enumerate(chips):
            slot = 2 * cx + cy
            for a in range(n):
                copy(a, 3 + k, half(a, slot, 1 - c), half(a, slot, 1 - c), sibling).wait_recv()
        for cp in started:
            cp.wait_send()
        for cp in local:
            cp.wait()

    return pl.pallas_call(
        body, name=name, out_shape=[jax.ShapeDtypeStruct((N_SHARDS,) + w.shape, w.dtype) for w in shards],
        in_specs=[HBM] * n, out_specs=[HBM] * n,
        scratch_shapes=[pltpu.SemaphoreType.DMA((6 * n,)), pltpu.SemaphoreType.DMA((6 * n,)), pltpu.SemaphoreType.DMA((n,))],
    )(*shards)


def _exchange_sibling_halves(grads, *, name):
    n = len(grads)

    def body(*refs):
        ins, outs = refs[:n], refs[n:2 * n]
        send_sems, recv_sems = refs[2 * n:]
        x, y, c, _ = _mesh_position()
        copies = []
        for a in range(n):
            rh = grads[a].shape[1] // 2
            cp = pltpu.make_async_remote_copy(src_ref=ins[a].at[:, pl.ds((1 - c) * rh, rh), :], dst_ref=outs[a], send_sem=send_sems.at[a],
                                              recv_sem=recv_sems.at[a], device_id=(x, y, 1 - c), device_id_type=MESH)
            cp.start()
            copies.append(cp)
        for cp in copies:
            cp.wait()

    return pl.pallas_call(
        body, name=name, out_shape=[jax.ShapeDtypeStruct((g.shape[0], g.shape[1] // 2, g.shape[2]), g.dtype) for g in grads],
        in_specs=[HBM] * n, out_specs=[HBM] * n,
        scratch_shapes=[pltpu.SemaphoreType.DMA((n,)), pltpu.SemaphoreType.DMA((n,))],
    )(*grads)


def _row_tile(rows):
    for tr in (256, 128, 64, 32, 16, 8):
        if rows % tr == 0:
            return tr
    raise ValueError(rows)


def _add_sibling(grad, other, c, *, name):
    p, r, cols = grad.shape
    rh = r // 2
    tr = _row_tile(rh)
    nb = rh // tr

    def body(c_ref, g_ref, o_ref, s_ref):
        s_ref[...] = g_ref[...] + o_ref[...]

    return pl.pallas_call(
        body, name=name, out_shape=jax.ShapeDtypeStruct((p, rh, cols), F32),
        grid_spec=pltpu.PrefetchScalarGridSpec(
            num_scalar_prefetch=1, grid=(p, nb),
            in_specs=[pl.BlockSpec((None, tr, cols), lambda j, i, c_ref: (j, c_ref[0] * nb + i, 0)),
                      pl.BlockSpec((None, tr, cols), lambda j, i, c_ref: (j, i, 0))],
            out_specs=pl.BlockSpec((None, tr, cols), lambda j, i, c_ref: (j, i, 0))),
        compiler_params=_params(("parallel", "parallel")),
    )(c, grad, other)


def _exchange_chip_partials(sums, *, name):
    n = len(sums)

    def body(*refs):
        ins, outs = refs[:n], refs[n:2 * n]
        send_sems, recv_sems = refs[2 * n:]
        _, _, c, chips = _mesh_position()
        copies = []
        for k, (cx, cy) in enumerate(chips):
            for a in range(n):
                cp = pltpu.make_async_remote_copy(src_ref=ins[a].at[2 * cx + cy], dst_ref=outs[a].at[k], send_sem=send_sems.at[a * 3 + k],
                                                  recv_sem=recv_sems.at[a * 3 + k], device_id=(cx, cy, c), device_id_type=MESH)
                cp.start()
                copies.append(cp)
        for cp in copies:
            cp.wait()

    return pl.pallas_call(
        body, name=name, out_shape=[jax.ShapeDtypeStruct((3,) + g.shape[1:], g.dtype) for g in sums],
        in_specs=[HBM] * n, out_specs=[HBM] * n,
        scratch_shapes=[pltpu.SemaphoreType.DMA((3 * n,)), pltpu.SemaphoreType.DMA((3 * n,))],
    )(*sums)


def _add_chips(sums, others, me, *, name):
    _, rh, cols = sums.shape
    tr = _row_tile(rh)

    def body(me_ref, s_ref, o_ref, r_ref):
        r_ref[...] = ((s_ref[...] + o_ref[0]) + o_ref[1]) + o_ref[2]

    return pl.pallas_call(
        body, name=name, out_shape=jax.ShapeDtypeStruct((rh, cols), F32),
        grid_spec=pltpu.PrefetchScalarGridSpec(
            num_scalar_prefetch=1, grid=(rh // tr,),
            in_specs=[pl.BlockSpec((None, tr, cols), lambda i, me_ref: (me_ref[0], i, 0)),
                      pl.BlockSpec((3, tr, cols), lambda i, me_ref: (0, i, 0))],
            out_specs=pl.BlockSpec((tr, cols), lambda i, me_ref: (i, 0))),
        compiler_params=_params(("parallel",)),
    )(me, sums, others)


def _join_halves(halves, *, name):
    n = len(halves)

    def body(*refs):
        ins, outs = refs[:n], refs[n:2 * n]
        send_sems, recv_sems, local_sems = refs[2 * n:]
        x, y, c, _ = _mesh_position()
        copies = []
        for a in range(n):
            rh = halves[a].shape[0]
            mine = outs[a].at[pl.ds(c * rh, rh), :]
            lc = pltpu.make_async_copy(ins[a], mine, local_sems.at[a])
            lc.start()
            cp = pltpu.make_async_remote_copy(src_ref=ins[a], dst_ref=mine, send_sem=send_sems.at[a], recv_sem=recv_sems.at[a],
                                              device_id=(x, y, 1 - c), device_id_type=MESH)
            cp.start()
            copies.append((lc, cp))
        for a, (lc, cp) in enumerate(copies):
            rh = halves[a].shape[0]
            theirs = outs[a].at[pl.ds((1 - c) * rh, rh), :]
            cp.wait_send()
            pltpu.make_async_remote_copy(src_ref=ins[a], dst_ref=theirs, send_sem=send_sems.at[a], recv_sem=recv_sems.at[a],
                                         device_id=(x, y, 1 - c), device_id_type=MESH).wait_recv()
            lc.wait()

    return pl.pallas_call(
        body, name=name, out_shape=[jax.ShapeDtypeStruct((2 * h.shape[0], h.shape[1]), h.dtype) for h in halves],
        in_specs=[HBM] * n, out_specs=[HBM] * n,
        scratch_shapes=[pltpu.SemaphoreType.DMA((n,)), pltpu.SemaphoreType.DMA((n,)), pltpu.SemaphoreType.DMA((n,))],
    )(*halves)


def _all_reduce_small(packed, *, name):
    rows, cols = packed.shape

    def body(in_ref, out_ref, gathered, send_sems, recv_sems):
        x, y, c, _ = _mesh_position()
        me = 4 * x + 2 * y + c
        gathered[me] = in_ref[...]
        copies = []
        for d in range(1, 8):
            dx, dy, dc = (d >> 2) & 1, (d >> 1) & 1, d & 1
            peer = (x ^ dx, y ^ dy, c ^ dc)
            cp = pltpu.make_async_remote_copy(src_ref=in_ref, dst_ref=gathered.at[me], send_sem=send_sems.at[d - 1], recv_sem=recv_sems.at[d - 1],
                                              device_id=peer, device_id_type=MESH)
            cp.start()
            copies.append(cp)
        for cp in copies:
            cp.wait()
        total = gathered[0]
        for j in range(1, 8):
            total = total + gathered[j]
        out_ref[...] = total

    vm = pl.BlockSpec(memory_space=pltpu.VMEM)
    return pl.pallas_call(
        body, name=name, out_shape=jax.ShapeDtypeStruct((rows, cols), F32), in_specs=[vm], out_specs=vm,
        scratch_shapes=[pltpu.VMEM((8, rows, cols), F32), pltpu.SemaphoreType.DMA((7,)), pltpu.SemaphoreType.DMA((7,))],
    )(packed)


def _adamw_math(w, g, m, v):
    m = ADAM_B1 * m + (1.0 - ADAM_B1) * g
    v = ADAM_B2 * v + (1.0 - ADAM_B2) * (g * g)
    m_hat = m / (1.0 - ADAM_B1 ** ADAM_STEP)
    v_hat = v / (1.0 - ADAM_B2 ** ADAM_STEP)
    delta = -ADAM_LR * (m_hat / (jnp.sqrt(v_hat) + ADAM_EPS) + ADAM_WD * w)
    return delta, m, v


def _adamw(w, g, m, v, *, name):
    rows, cols = w.shape
    tr = _row_tile(rows)
    blk = pl.BlockSpec((tr, cols), lambda i: (i, 0))

    def body(w_ref, g_ref, m_ref, v_ref, d_ref, nm_ref, nv_ref):
        d_ref[...], nm_ref[...], nv_ref[...] = _adamw_math(w_ref[...], g_ref[...], m_ref[...], v_ref[...])

    shape = jax.ShapeDtypeStruct((rows, cols), F32)
    return pl.pallas_call(body, name=name, grid=(rows // tr,), out_shape=(shape, shape, shape), in_specs=[blk] * 4, out_specs=(blk,) * 3,
                          compiler_params=_params(("parallel",)))(w, g, m, v)


def _adamw_small(w, g, m, v, *, name):
    def body(w_ref, g_ref, m_ref, v_ref, d_ref, nm_ref, nv_ref, loss_ref):
        d_ref[...], nm_ref[...], nv_ref[...] = _adamw_math(w_ref[...], g_ref[...], m_ref[...], v_ref[...])
        loss_ref[...] = (0.5 / D_MODEL) * jnp.sum(g_ref[8:9, :], axis=1, keepdims=True)

    shape = jax.ShapeDtypeStruct(w.shape, F32)
    return pl.pallas_call(body, name=name, out_shape=(shape, shape, shape, jax.ShapeDtypeStruct((1, 1), F32)),
                          compiler_params=_params())(w, g, m, v)


SMALL_ROWS = 16


def _pack_small(lb_logits, gain, sinks, rel_bias, ln1_g, ln1_b, ln2_g, ln2_b, extra=None):
    misc = jnp.concatenate([sinks.reshape(1, -1), rel_bias.reshape(1, -1)], axis=1)
    misc = jnp.pad(misc, ((0, 0), (0, D_MODEL - misc.shape[1])))
    rows = [lb_logits, gain, ln1_g, ln1_b, ln2_g, ln2_b, misc, extra if extra is not None else jnp.zeros((1, D_MODEL), F32)]
    used = sum(r.shape[0] for r in rows)
    return jnp.concatenate(rows + [jnp.zeros((SMALL_ROWS - used, D_MODEL), F32)], axis=0)


def _unpack_small(p):
    return dict(lb_logits=p[0:2], hg_norm_gain=p[2:3], ln1_g=p[3:4], ln1_b=p[4:5], ln2_g=p[5:6], ln2_b=p[6:7],
                swa_sinks=p[7:8, 0:SWA_HEADS], rel_bias=p[7:8, SWA_HEADS:SWA_HEADS + NUM_BUCKETS * SWA_HEADS].reshape(NUM_BUCKETS, SWA_HEADS))


WEIGHTS = ["w_in", "lb_logits", "hg_norm_gain", "swa_sinks", "rel_bias", "w_mem_kv", "w_branch_hg", "w_branch_swa", "w_branch_mem",
           "w_out", "ln1_g", "ln1_b", "w_up", "w_down", "ln2_g", "ln2_b"]
BIG = ["w_in", "w_mem_kv", "w_branch_hg", "w_branch_swa", "w_branch_mem", "w_out", "w_up", "w_down"]
SMALL = ["lb_logits", "hg_norm_gain", "swa_sinks", "rel_bias", "ln1_g", "ln1_b", "ln2_g", "ln2_b"]


def kernel(x, mem, w_in, lb_logits, hg_norm_gain, swa_sinks, rel_bias, w_mem_kv, w_branch_hg, w_branch_swa, w_branch_mem, w_out, ln1_g, ln1_b, w_up, w_down, ln2_g, ln2_b, loss_target, m_w_in, m_lb_logits, m_hg_norm_gain, m_swa_sinks, m_rel_bias, m_w_mem_kv, m_w_branch_hg, m_w_branch_swa, m_w_branch_mem, m_w_out, m_ln1_g, m_ln1_b, m_w_up, m_w_down, m_ln2_g, m_ln2_b, v_w_in, v_lb_logits, v_hg_norm_gain, v_swa_sinks, v_rel_bias, v_w_mem_kv, v_w_branch_hg, v_w_branch_swa, v_w_branch_mem, v_w_out, v_ln1_g, v_ln1_b, v_w_up, v_w_down, v_ln2_g, v_ln2_b):
    w = dict(w_in=w_in, lb_logits=lb_logits, hg_norm_gain=hg_norm_gain, swa_sinks=swa_sinks, rel_bias=rel_bias, w_mem_kv=w_mem_kv,
             w_branch_hg=w_branch_hg, w_branch_swa=w_branch_swa, w_branch_mem=w_branch_mem, w_out=w_out, ln1_g=ln1_g, ln1_b=ln1_b,
             w_up=w_up, w_down=w_down, ln2_g=ln2_g, ln2_b=ln2_b)
    m = dict(w_in=m_w_in, lb_logits=m_lb_logits, hg_norm_gain=m_hg_norm_gain, swa_sinks=m_swa_sinks, rel_bias=m_rel_bias, w_mem_kv=m_w_mem_kv,
             w_branch_hg=m_w_branch_hg, w_branch_swa=m_w_branch_swa, w_branch_mem=m_w_branch_mem, w_out=m_w_out, ln1_g=m_ln1_g, ln1_b=m_ln1_b,
             w_up=m_w_up, w_down=m_w_down, ln2_g=m_ln2_g, ln2_b=m_ln2_b)
    v = dict(w_in=v_w_in, lb_logits=v_lb_logits, hg_norm_gain=v_hg_norm_gain, swa_sinks=v_swa_sinks, rel_bias=v_rel_bias, w_mem_kv=v_w_mem_kv,
             w_branch_hg=v_w_branch_hg, w_branch_swa=v_w_branch_swa, w_branch_mem=v_w_branch_mem, w_out=v_w_out, ln1_g=v_ln1_g, ln1_b=v_ln1_b,
             w_up=v_w_up, w_down=v_w_down, ln2_g=v_ln2_g, ln2_b=v_ln2_b)
    shapes = {k: w[k].shape for k in WEIGHTS}
    for d in (w, m, v):
        for k in BIG:
            d[k] = d[k].reshape(d[k].shape[-2], d[k].shape[-1])

    gathered = _all_gather([w[k].astype(BF16) for k in BIG], name="gather_weights")
    full = dict(zip(BIG, gathered))
    wi = full["w_in"].transpose(1, 0, 2).reshape(D_MODEL, IN_COLS)
    wi_parts = (wi[:, 0:W_A], wi[:, W_A:W_A + W_B], wi[:, W_A + W_B:W_A + W_B + W_C], wi[:, W_A + W_B + W_C:])
    wbr = jnp.stack([full[k].reshape(D_MODEL, D_MODEL) for k in ("w_branch_hg", "w_branch_swa", "w_branch_mem")])
    wo = full["w_out"].reshape(D_MODEL, D_MODEL)

    grad_x, big, small = _local_step(
        x.reshape(x.shape[-2], D_MODEL), mem.reshape(MEM_LEN, D_MODEL), loss_target.reshape(loss_target.shape[-2], D_MODEL),
        wi_parts, full["w_mem_kv"], wbr, wo, full["w_up"], full["w_down"],
        lb_logits, hg_norm_gain, swa_sinks, rel_bias, ln1_g, ln1_b, ln2_g, ln2_b)

    big["w_in"] = big["w_in"].reshape(D_MODEL, N_SHARDS, IN_COLS // N_SHARDS).transpose(1, 0, 2)
    for k in ("w_branch_hg", "w_branch_swa", "w_branch_mem", "w_out"):
        big[k] = big[k].reshape(N_SHARDS, D_MODEL // N_SHARDS, D_MODEL)
    grads = [big[k] for k in BIG]
    cx, cy, cc = lax.axis_index("x"), lax.axis_index("y"), lax.axis_index("c")
    c_arr = jnp.reshape(cc, (1,)).astype(jnp.int32)
    me_arr = jnp.reshape(2 * cx + cy, (1,)).astype(jnp.int32)
    from_sibling = _exchange_sibling_halves(grads, name="reduce_sibling")
    chip_sums = [_add_sibling(g, o, c_arr, name="add_sibling_" + k) for g, o, k in zip(grads, from_sibling, BIG)]
    from_chips = _exchange_chip_partials(chip_sums, name="reduce_chips")
    halves = [_add_chips(s_, o, me_arr, name="add_chips_" + k) for s_, o, k in zip(chip_sums, from_chips, BIG)]
    reduced = dict(zip(BIG, _join_halves(halves, name="join_halves")))

    packed_g = _pack_small(small["lb_logits"], small["hg_norm_gain"], small["swa_sinks"], small["rel_bias"], small["ln1_g"], small["ln1_b"],
                           small["ln2_g"], small["ln2_b"], extra=small["sq_err"])
    packed_g = _all_reduce_small(packed_g, name="reduce_small")

    grad_out, delta_out, m_out, v_out = {}, {}, {}, {}
    for k in BIG:
        d_, m_, v_ = _adamw(w[k], reduced[k], m[k], v[k], name="adamw_" + k)
        grad_out[k], delta_out[k], m_out[k], v_out[k] = reduced[k], d_, m_, v_
    pack = lambda d: _pack_small(d["lb_logits"], d["hg_norm_gain"], d["swa_sinks"], d["rel_bias"], d["ln1_g"], d["ln1_b"], d["ln2_g"], d["ln2_b"])
    d_s, m_s, v_s, loss = _adamw_small(pack(w), packed_g, pack(m), pack(v), name="adamw_small")
    for out, p in ((grad_out, packed_g), (delta_out, d_s), (m_out, m_s), (v_out, v_s)):
        out.update(_unpack_small(p))

    result = [loss.reshape(()), grad_x.reshape(x.shape)]
    for out in (grad_out, delta_out, m_out, v_out):
        result += [out[k].reshape(shapes[k]) for k in WEIGHTS]
    return tuple(result)
```

```python
import functools
import math

import jax
import jax.numpy as jnp
from jax import lax
from jax.experimental import pallas as pl
from jax.experimental.pallas import tpu as pltpu

F32 = jnp.float32
BF16 = jnp.bfloat16
HIGHEST = lax.Precision.HIGHEST
MESH = pl.DeviceIdType.MESH

D_MODEL = 1024
MEM_LEN = 256
HG_HEADS = 8
HG_DK = 128
HG_CHUNK = 64
SWA_HEADS = 16
SWA_KV_HEADS = 2
SWA_GROUP = 8
SWA_HEAD_DIM = 64
SWA_BLOCK = 128
SWA_WINDOW = 128
MEM_HEADS = 4
MEM_HEAD_DIM = 256
NUM_BUCKETS = 32
MAX_DISTANCE = 128
D_FF = 4096
LN_EPS = 1e-5
RMS_EPS = 1e-6
ALPHA = 2.0 ** 0.25
W_A, W_B, W_C, W_D = 4096, 1280, 1024, 3072
IN_COLS = W_A + W_B + W_C + W_D
N_SHARDS = 4
ADAM_LR = 0.001
ADAM_B1 = 0.9
ADAM_B2 = 0.999
ADAM_EPS = 1e-08
ADAM_WD = 0.01
ADAM_STEP = 10
MASK_VALUE = -1e30
VMEM_LIMIT = 56 * 1024 * 1024

NN = ((1,), (0,))
NT = ((1,), (1,))
TN = ((0,), (0,))
HBM = pl.BlockSpec(memory_space=pltpu.HBM)


def _dot(a, b, dims=NN, precision=None):
    return lax.dot_general(a, b, (dims, ((), ())), precision=precision, preferred_element_type=F32)


def _params(sem=None):
    return pltpu.CompilerParams(dimension_semantics=sem, vmem_limit_bytes=VMEM_LIMIT)


def _resident(shape):
    zeros = (0,) * len(shape)
    return pl.BlockSpec(shape, lambda *_: zeros, pipeline_mode=pl.Buffered(1))


def _mm(a, b, *, mode, tm, tn, tk, name, out_dtype=F32, b_panels=False, out_panels=False, add=None, add_scale=1.0):
    if mode == "tn":
        kdim, m = a.shape
    else:
        m, kdim = a.shape
    if b_panels:
        n = b.shape[0] * b.shape[2]
        assert b.shape[2] == tn and mode == "nn"
    elif mode == "nt":
        n = b.shape[0]
    else:
        n = b.shape[1]
    assert m % tm == 0 and n % tn == 0 and kdim % tk == 0, (name, m, n, kdim)
    nk = kdim // tk
    dims = {"nn": NN, "nt": NT, "tn": TN}[mode]
    a_spec = pl.BlockSpec((tk, tm), lambda i, j, k: (k, i)) if mode == "tn" else pl.BlockSpec((tm, tk), lambda i, j, k: (i, k))
    if b_panels:
        b_spec = pl.BlockSpec((None, tk, tn), lambda i, j, k: (j, k, 0))
    elif mode == "nt":
        b_spec = pl.BlockSpec((tn, tk), lambda i, j, k: (j, k))
    else:
        b_spec = pl.BlockSpec((tk, tn), lambda i, j, k: (k, j))
    if out_panels:
        out_shape = jax.ShapeDtypeStruct((n // tn, m, tn), out_dtype)
        o_spec = pl.BlockSpec((None, tm, tn), lambda i, j, k: (j, i, 0))
    else:
        out_shape = jax.ShapeDtypeStruct((m, n), out_dtype)
        o_spec = pl.BlockSpec((tm, tn), lambda i, j, k: (i, j))
    in_specs = [a_spec, b_spec]
    operands = [a, b]
    if add is not None:
        in_specs.append(pl.BlockSpec((tm, tn), lambda i, j, k: (i, j)))
        operands.append(add)

    def body(*refs):
        a_ref, b_ref = refs[0], refs[1]
        add_ref = refs[2] if add is not None else None
        o_ref = refs[3] if add is not None else refs[2]
        part = _dot(a_ref[...].astype(BF16), b_ref[...].astype(BF16), dims)

        def finish(acc):
            if add_ref is not None:
                acc = acc + add_scale * add_ref[...]
            o_ref[...] = acc.astype(out_dtype)

        if nk == 1:
            finish(part)
        else:
            acc_ref = refs[-1]
            k = pl.program_id(2)

            @pl.when(k == 0)
            def _():
                acc_ref[...] = part

            @pl.when(k > 0)
            def _():
                acc_ref[...] += part

            @pl.when(k == nk - 1)
            def _():
                finish(acc_ref[...])

    return pl.pallas_call(
        body, name=name, out_shape=out_shape, grid=(m // tm, n // tn, nk), in_specs=in_specs, out_specs=o_spec,
        scratch_shapes=[pltpu.VMEM((tm, tn), F32)] if nk > 1 else [],
        compiler_params=_params(("parallel", "parallel", "arbitrary")),
    )(*operands)


def _dx_matmul(dzs, wis, resid, *, tm, tks, name):
    s = resid.shape[0]
    counts = [dz.shape[1] // tk for dz, tk in zip(dzs, tks)]
    starts = [sum(counts[:p]) for p in range(len(counts))]
    nk = sum(counts)
    npieces = len(dzs)

    def piece_block(p):
        return lambda i, k: (i, jnp.clip(k - starts[p], 0, counts[p] - 1))

    def weight_block(p):
        return lambda i, k: (0, jnp.clip(k - starts[p], 0, counts[p] - 1))

    in_specs = [pl.BlockSpec((tm, tks[p]), piece_block(p)) for p in range(npieces)]
    in_specs += [pl.BlockSpec((D_MODEL, tks[p]), weight_block(p)) for p in range(npieces)]
    in_specs += [pl.BlockSpec((tm, D_MODEL), lambda i, k: (i, 0))]

    def body(*refs):
        dz_refs, w_refs = refs[:npieces], refs[npieces:2 * npieces]
        r_ref, o_ref, acc_ref = refs[2 * npieces], refs[2 * npieces + 1], refs[2 * npieces + 2]
        k = pl.program_id(1)

        @pl.when(k == 0)
        def _():
            acc_ref[...] = ALPHA * r_ref[...]

        for p in range(npieces):
            @pl.when((k >= starts[p]) & (k < starts[p] + counts[p]))
            def _(p=p):
                acc_ref[...] += _dot(dz_refs[p][...], w_refs[p][...], NT)

        @pl.when(k == nk - 1)
        def _():
            o_ref[...] = acc_ref[...]

    return pl.pallas_call(
        body, name=name, out_shape=jax.ShapeDtypeStruct((s, D_MODEL), F32), grid=(s // tm, nk), in_specs=in_specs,
        out_specs=pl.BlockSpec((tm, D_MODEL), lambda i, k: (i, 0)), scratch_shapes=[pltpu.VMEM((tm, D_MODEL), F32)],
        compiler_params=_params(("parallel", "arbitrary")),
    )(*dzs, *wis, resid)


def _lower_bound(lbl_ref):
    l0, l1 = lbl_ref[0:1, :], lbl_ref[1:2, :]
    mx = jnp.maximum(l0, l1)
    e0, e1 = jnp.exp(l0 - mx), jnp.exp(l1 - mx)
    return e0 / (e0 + e1)


def _chunk_forward(q, fl, v, lb, tril_f):
    sg = jax.nn.sigmoid(fl)
    f = lb + (1.0 - lb) * sg
    k = 1.0 - f
    b = _dot(tril_f, jnp.log(f), NN, HIGHEST)
    b_last = b[HG_CHUNK - 1:HG_CHUNK, :]
    eb, enb, eo = jnp.exp(b), jnp.exp(-b), jnp.exp(b_last - b)
    return sg, f, k, b_last, eb, enb, eo, q * eb, k * enb, k * eo


def _hgrn_fwd(za, lb_logits, gain, *, name):
    s = za.shape[0]
    t = min(256, s)
    ncs = t // HG_CHUNK

    def body(z_ref, lbl_ref, gain_ref, oa_ref, oraw_ref, st_ref, state):
        @pl.when(pl.program_id(0) == 0)
        def _():
            state[...] = jnp.zeros_like(state)

        lb_all = _lower_bound(lbl_ref)
        row = lax.broadcasted_iota(jnp.int32, (HG_CHUNK, HG_CHUNK), 0)
        col = lax.broadcasted_iota(jnp.int32, (HG_CHUNK, HG_CHUNK), 1)
        tril = row >= col
        tril_f = tril.astype(F32)
        gain_all = gain_ref[...]

        def chunk(i, carry):
            r = pl.ds(pl.multiple_of(i * HG_CHUNK, HG_CHUNK), HG_CHUNK)
            for h in range(HG_HEADS):
                cols = slice(h * HG_DK, (h + 1) * HG_DK)
                lb, gn = lb_all[:, cols], gain_all[:, cols]
                q = z_ref[r, h * HG_DK:(h + 1) * HG_DK]
                fl = z_ref[r, 1024 + h * HG_DK:1024 + (h + 1) * HG_DK]
                v = z_ref[r, 2048 + h * HG_DK:2048 + (h + 1) * HG_DK]
                hg = z_ref[r, 3072 + h * HG_DK:3072 + (h + 1) * HG_DK]
                _, _, _, b_last, _, _, _, q_in, k_in, k_out = _chunk_forward(q, fl, v, lb, tril_f)
                q_in_b, vb = q_in.astype(BF16), v.astype(BF16)
                attn = jnp.where(tril, _dot(q_in_b, k_in.astype(BF16), NT), 0.0)
                st = state[h]
                st_ref[h, i] = st
                o = _dot(attn.astype(BF16), vb, NN) + _dot(q_in_b, st.astype(BF16), NT)
                state[h] = st * jnp.exp(b_last) + _dot(vb, k_out.astype(BF16), TN)
                oraw_ref[r, h * HG_DK:(h + 1) * HG_DK] = o
                n = o * lax.rsqrt(jnp.mean(o * o, axis=-1, keepdims=True) + RMS_EPS)
                oa_ref[r, h * HG_DK:(h + 1) * HG_DK] = (n * gn * (hg * jax.nn.sigmoid(hg))).astype(BF16)
            return carry

        lax.fori_loop(0, ncs, chunk, 0)

    return pl.pallas_call(
        body, name=name, grid=(s // t,),
        out_shape=(jax.ShapeDtypeStruct((s, D_MODEL), BF16), jax.ShapeDtypeStruct((s, D_MODEL), F32),
                   jax.ShapeDtypeStruct((HG_HEADS, s // HG_CHUNK, HG_DK, HG_DK), F32)),
        in_specs=[pl.BlockSpec((t, W_A), lambda i: (i, 0)), _resident((2, D_MODEL)), _resident((1, D_MODEL))],
        out_specs=(pl.BlockSpec((t, D_MODEL), lambda i: (i, 0)), pl.BlockSpec((t, D_MODEL), lambda i: (i, 0)),
                   pl.BlockSpec((HG_HEADS, ncs, HG_DK, HG_DK), lambda i: (0, i, 0, 0))),
        scratch_shapes=[pltpu.VMEM((HG_HEADS, HG_DK, HG_DK), F32)],
        compiler_params=_params(("arbitrary",)),
    )(za, lb_logits, gain)


def _hgrn_bwd(za, oraw, do_a, states, lb_logits, gain, *, name):
    s = za.shape[0]
    t = min(256, s)
    ncs = t // HG_CHUNK
    nt = s // t

    def body(z_ref, oraw_ref, do_ref, st_ref, lbl_ref, gain_ref, dz_ref, stats_ref, dstate):
        step = pl.program_id(0)

        @pl.when(step == 0)
        def _():
            dstate[...] = jnp.zeros_like(dstate)
            stats_ref[...] = jnp.zeros_like(stats_ref)

        lb_all = _lower_bound(lbl_ref)
        row = lax.broadcasted_iota(jnp.int32, (HG_CHUNK, HG_CHUNK), 0)
        col = lax.broadcasted_iota(jnp.int32, (HG_CHUNK, HG_CHUNK), 1)
        tril = row >= col
        tril_f = tril.astype(F32)
        triu_f = (row <= col).astype(F32)
        gain_all = gain_ref[...]

        def chunk(ii, carry):
            i = ncs - 1 - ii
            r = pl.ds(pl.multiple_of(i * HG_CHUNK, HG_CHUNK), HG_CHUNK)
            for h in range(HG_HEADS):
                cols = slice(h * HG_DK, (h + 1) * HG_DK)
                lb, gn = lb_all[:, cols], gain_all[:, cols]
                q = z_ref[r, h * HG_DK:(h + 1) * HG_DK]
                fl = z_ref[r, 1024 + h * HG_DK:1024 + (h + 1) * HG_DK]
                v = z_ref[r, 2048 + h * HG_DK:2048 + (h + 1) * HG_DK]
                hg = z_ref[r, 3072 + h * HG_DK:3072 + (h + 1) * HG_DK]
                o = oraw_ref[r, h * HG_DK:(h + 1) * HG_DK]
                doa = do_ref[r, h * HG_DK:(h + 1) * HG_DK]
                rms = lax.rsqrt(jnp.mean(o * o, axis=-1, keepdims=True) + RMS_EPS)
                n = o * rms
                sgg = jax.nn.sigmoid(hg)
                silu = hg * sgg
                dhg = doa * n * gn * (sgg * (1.0 + hg * (1.0 - sgg)))
                stats_ref[0:1, cols] += jnp.sum(doa * n * silu, axis=0, keepdims=True)
                dn = doa * gn * silu
                do = rms * (dn - n * jnp.mean(dn * n, axis=-1, keepdims=True))
                sg, f, k, b_last, eb, enb, eo, q_in, k_in, k_out = _chunk_forward(q, fl, v, lb, tril_f)
                q_in_b, k_in_b, k_out_b, vb, dob = (u.astype(BF16) for u in (q_in, k_in, k_out, v, do))
                attn = jnp.where(tril, _dot(q_in_b, k_in_b, NT), 0.0)
                st = st_ref[h, i]
                dst = dstate[h]
                dst_b = dst.astype(BF16)
                decay = jnp.exp(b_last)
                dattn = jnp.where(tril, _dot(dob, vb, NT), 0.0).astype(BF16)
                dq_in = _dot(dob, st.astype(BF16), NN) + _dot(dattn, k_in_b, NN)
                dk_in = _dot(dattn, q_in_b, TN)
                dk_out = _dot(vb, dst_b, NN)
                dv = _dot(attn.astype(BF16), dob, TN) + _dot(k_out_b, dst_b, NT)
                db_last = decay * jnp.sum(dst * st, axis=0, keepdims=True) + jnp.sum(dk_out * k_out, axis=0, keepdims=True)
                dstate[h] = dst * decay + _dot(dob, q_in_b, TN)
                db = dq_in * q_in - dk_in * k_in - dk_out * k_out
                dg = _dot(triu_f, db, NN, HIGHEST) + db_last
                dk = dk_in * enb + dk_out * eo
                df = dg / f - dk
                stats_ref[1:2, cols] += jnp.sum(df * (1.0 - sg), axis=0, keepdims=True)
                dz_ref[r, h * HG_DK:(h + 1) * HG_DK] = (dq_in * eb).astype(BF16)
                dz_ref[r, 1024 + h * HG_DK:1024 + (h + 1) * HG_DK] = (df * (1.0 - lb) * sg * (1.0 - sg)).astype(BF16)
                dz_ref[r, 2048 + h * HG_DK:2048 + (h + 1) * HG_DK] = dv.astype(BF16)
                dz_ref[r, 3072 + h * HG_DK:3072 + (h + 1) * HG_DK] = dhg.astype(BF16)
            return carry

        lax.fori_loop(0, ncs, chunk, 0)

        @pl.when(step == nt - 1)
        def _():
            dl0 = stats_ref[1:2, :] * lb_all * (1.0 - lb_all)
            stats_ref[1:2, :] = dl0
            stats_ref[2:3, :] = -dl0

    rev = lambda i: (nt - 1 - i, 0)
    return pl.pallas_call(
        body, name=name, grid=(nt,),
        out_shape=(jax.ShapeDtypeStruct((s, W_A), BF16), jax.ShapeDtypeStruct((8, D_MODEL), F32)),
        in_specs=[pl.BlockSpec((t, W_A), rev), pl.BlockSpec((t, D_MODEL), rev), pl.BlockSpec((t, D_MODEL), rev),
                  pl.BlockSpec((HG_HEADS, ncs, HG_DK, HG_DK), lambda i: (0, nt - 1 - i, 0, 0)),
                  _resident((2, D_MODEL)), _resident((1, D_MODEL))],
        out_specs=(pl.BlockSpec((t, W_A), rev), pl.BlockSpec((8, D_MODEL), lambda i: (0, 0))),
        scratch_shapes=[pltpu.VMEM((HG_HEADS, HG_DK, HG_DK), F32)],
        compiler_params=_params(("arbitrary",)),
    )(za, oraw, do_a, states, lb_logits, gain)


def _t5_bucket(n):
    max_exact = NUM_BUCKETS // 2
    nf = jnp.maximum(n, 1).astype(F32)
    large = max_exact + (jnp.log(nf / max_exact) / math.log(MAX_DISTANCE / max_exact) * (NUM_BUCKETS - max_exact)).astype(jnp.int32)
    large = jnp.minimum(large, NUM_BUCKETS - 1)
    return jnp.where(n < max_exact, n, large)


def _bias_selector():
    qi = jnp.arange(SWA_BLOCK)[:, None] + SWA_BLOCK
    kj = jnp.arange(2 * SWA_BLOCK)[None, :]
    dist = qi - kj
    band = ((dist >= 0) & (dist < SWA_WINDOW)).reshape(1, -1)
    bucket = _t5_bucket(jnp.clip(dist, 0, SWA_WINDOW - 1)).reshape(1, -1)
    onehot = ((bucket == jnp.arange(NUM_BUCKETS)[:, None]) & band).astype(F32)
    return onehot, jnp.where(band, 0.0, MASK_VALUE).astype(F32)


def _bias_table(rel_bias_t, onehot, maskrow, *, name):
    def body(rb_ref, oh_ref, mask_ref, o_ref):
        o_ref[...] = _dot(rb_ref[...], oh_ref[...], NN, HIGHEST) + mask_ref[...]

    return pl.pallas_call(body, name=name, out_shape=jax.ShapeDtypeStruct((SWA_HEADS, onehot.shape[1]), F32),
                          compiler_params=_params())(rel_bias_t, onehot, maskrow)


def _bias_grad(dbias2d, onehot, *, name):
    def body(db_ref, oh_ref, o_ref):
        o_ref[...] = _dot(db_ref[...], oh_ref[...], NT, HIGHEST)

    return pl.pallas_call(body, name=name, out_shape=jax.ShapeDtypeStruct((SWA_HEADS, NUM_BUCKETS), F32),
                          compiler_params=_params())(dbias2d, onehot)


def _swa_scores(zq_ref, kv_cur_ref, kv_prev_ref, bias_ref, first):
    q = (zq_ref[:, 0:1024] * (SWA_HEAD_DIM ** -0.5)).astype(BF16)
    kv_c = kv_cur_ref[...].astype(BF16)
    kv_p = kv_prev_ref[...].astype(BF16)
    key = lax.broadcasted_iota(jnp.int32, (1, 2 * SWA_BLOCK), 1)
    first_mask = jnp.where(first & (key < SWA_BLOCK), MASK_VALUE, 0.0)
    return q, kv_c, kv_p, first_mask


def _swa_fwd(zb, bias, sinks, *, name):
    s = zb.shape[0]
    nb = s // SWA_BLOCK

    def body(zq_ref, kvc_ref, kvp_ref, bias_ref, sink_ref, o_ref, lse_ref):
        n = pl.program_id(0)
        q, kv_c, kv_p, first_mask = _swa_scores(zq_ref, kvc_ref, kvp_ref, bias_ref, n == 0)
        lses = []
        outs = []
        for g in range(SWA_KV_HEADS):
            kk = jnp.concatenate([kv_p[:, g * 64:(g + 1) * 64], kv_c[:, g * 64:(g + 1) * 64]], axis=0)
            vv = jnp.concatenate([kv_p[:, 128 + g * 64:128 + (g + 1) * 64], kv_c[:, 128 + g * 64:128 + (g + 1) * 64]], axis=0)
            for j in range(SWA_GROUP):
                h = g * SWA_GROUP + j
                sc = _dot(q[:, h * 64:(h + 1) * 64], kk, NT) + bias_ref[h] + first_mask
                sink = sink_ref[0:1, h:h + 1]
                m = jnp.maximum(jnp.max(sc, axis=-1, keepdims=True), sink)
                p = jnp.exp(sc - m)
                den = jnp.sum(p, axis=-1, keepdims=True) + jnp.exp(sink - m)
                outs.append(_dot((p / den).astype(BF16), vv, NN))
                lses.append(m + jnp.log(den))
        o_ref[...] = jnp.concatenate(outs, axis=1).astype(BF16)
        lse_ref[...] = jnp.concatenate(lses, axis=1)

    return pl.pallas_call(
        body, name=name, grid=(nb,),
        out_shape=(jax.ShapeDtypeStruct((s, D_MODEL), BF16), jax.ShapeDtypeStruct((s, SWA_HEADS), F32)),
        in_specs=[pl.BlockSpec((SWA_BLOCK, W_B), lambda n: (n, 0)),
                  pl.BlockSpec((SWA_BLOCK, 256), lambda n: (n, 4)),
                  pl.BlockSpec((SWA_BLOCK, 256), lambda n: (jnp.maximum(n - 1, 0), 4)),
                  _resident((SWA_HEADS, SWA_BLOCK, 2 * SWA_BLOCK)), _resident((1, SWA_HEADS))],
        out_specs=(pl.BlockSpec((SWA_BLOCK, D_MODEL), lambda n: (n, 0)), pl.BlockSpec((SWA_BLOCK, SWA_HEADS), lambda n: (n, 0))),
        compiler_params=_params(("arbitrary",)),
    )(zb, zb, zb, bias, sinks)


def _swa_bwd(zb, do_b, lse, bias, sinks, *, name):
    s = zb.shape[0]
    nb = s // SWA_BLOCK
    scale = SWA_HEAD_DIM ** -0.5

    def body(zq_ref, kvc_ref, kvp_ref, do_ref, lse_ref, bias_ref, sink_ref, dz_ref, dbias_ref, dsink_ref, carry, dsink_acc):
        step = pl.program_id(0)
        n = nb - 1 - step

        @pl.when(step == 0)
        def _():
            carry[...] = jnp.zeros_like(carry)
            dsink_acc[...] = jnp.zeros_like(dsink_acc)
            dbias_ref[...] = jnp.zeros_like(dbias_ref)

        q, kv_c, kv_p, first_mask = _swa_scores(zq_ref, kvc_ref, kvp_ref, bias_ref, n == 0)
        dqs, dkks, dvvs, dsk = [], [], [], []
        for g in range(SWA_KV_HEADS):
            kk = jnp.concatenate([kv_p[:, g * 64:(g + 1) * 64], kv_c[:, g * 64:(g + 1) * 64]], axis=0)
            vv = jnp.concatenate([kv_p[:, 128 + g * 64:128 + (g + 1) * 64], kv_c[:, 128 + g * 64:128 + (g + 1) * 64]], axis=0)
            dkk = jnp.zeros((2 * SWA_BLOCK, SWA_HEAD_DIM), F32)
            dvv = jnp.zeros((2 * SWA_BLOCK, SWA_HEAD_DIM), F32)
            for j in range(SWA_GROUP):
                h = g * SWA_GROUP + j
                qh = q[:, h * 64:(h + 1) * 64]
                doh = do_ref[:, h * 64:(h + 1) * 64].astype(BF16)
                lse_h = lse_ref[:, h:h + 1]
                sc = _dot(qh, kk, NT) + bias_ref[h] + first_mask
                p = jnp.exp(sc - lse_h)
                dp = _dot(doh, vv, NT)
                delta = jnp.sum(p * dp, axis=-1, keepdims=True)
                ds = p * (dp - delta)
                dbias_ref[h] += ds
                dsk.append(-jnp.exp(sink_ref[0:1, h:h + 1] - lse_h) * delta)
                ds_b = ds.astype(BF16)
                dqs.append(_dot(ds_b, kk, NN) * scale)
                dkk = dkk + _dot(ds_b, qh, TN)
                dvv = dvv + _dot(p.astype(BF16), doh, TN)
            dkks.append(dkk)
            dvvs.append(dvv)
        dsink_acc[...] += jnp.concatenate(dsk, axis=1)
        dkv = jnp.concatenate(dkks + dvvs, axis=1)
        dz_ref[:, 0:1024] = jnp.concatenate(dqs, axis=1).astype(BF16)
        dz_ref[:, 1024:1280] = (dkv[SWA_BLOCK:, :] + carry[...]).astype(BF16)
        carry[...] = dkv[:SWA_BLOCK, :]

        @pl.when(step == nb - 1)
        def _():
            dsink_ref[...] = jnp.sum(dsink_acc[...], axis=0, keepdims=True)

    rev = lambda i: (nb - 1 - i, 0)
    return pl.pallas_call(
        body, name=name, grid=(nb,),
        out_shape=(jax.ShapeDtypeStruct((s, W_B), BF16), jax.ShapeDtypeStruct((SWA_HEADS, SWA_BLOCK, 2 * SWA_BLOCK), F32),
                   jax.ShapeDtypeStruct((1, SWA_HEADS), F32)),
        in_specs=[pl.BlockSpec((SWA_BLOCK, W_B), rev),
                  pl.BlockSpec((SWA_BLOCK, 256), lambda i: (nb - 1 - i, 4)),
                  pl.BlockSpec((SWA_BLOCK, 256), lambda i: (jnp.maximum(nb - 2 - i, 0), 4)),
                  pl.BlockSpec((SWA_BLOCK, D_MODEL), rev), pl.BlockSpec((SWA_BLOCK, SWA_HEADS), rev),
                  _resident((SWA_HEADS, SWA_BLOCK, 2 * SWA_BLOCK)), _resident((1, SWA_HEADS))],
        out_specs=(pl.BlockSpec((SWA_BLOCK, W_B), rev),
                   pl.BlockSpec((SWA_HEADS, SWA_BLOCK, 2 * SWA_BLOCK), lambda i: (0, 0, 0)),
                   pl.BlockSpec((1, SWA_HEADS), lambda i: (0, 0))),
        scratch_shapes=[pltpu.VMEM((SWA_BLOCK, 256), F32), pltpu.VMEM((SWA_BLOCK, SWA_HEADS), F32)],
        compiler_params=_params(("arbitrary",)),
    )(zb, zb, zb, do_b, lse, bias, sinks)


def _mem_probs(zc_ref, mkv_ref, h):
    cols = slice(h * MEM_HEAD_DIM, (h + 1) * MEM_HEAD_DIM)
    qh = (zc_ref[:, cols] * (MEM_HEAD_DIM ** -0.5)).astype(BF16)
    sc = _dot(qh, mkv_ref[:, cols], NT)
    e = jnp.exp(sc - jnp.max(sc, axis=-1, keepdims=True))
    return qh, e / jnp.sum(e, axis=-1, keepdims=True)


def _mem_fwd(zc, mkv, *, name):
    s = zc.shape[0]
    t = min(512, s)

    def body(zc_ref, mkv_ref, o_ref):
        for h in range(MEM_HEADS):
            _, p = _mem_probs(zc_ref, mkv_ref, h)
            vh = mkv_ref[:, D_MODEL + h * MEM_HEAD_DIM:D_MODEL + (h + 1) * MEM_HEAD_DIM]
            o_ref[:, h * MEM_HEAD_DIM:(h + 1) * MEM_HEAD_DIM] = _dot(p.astype(BF16), vh, NN).astype(BF16)

    return pl.pallas_call(
        body, name=name, grid=(s // t,), out_shape=jax.ShapeDtypeStruct((s, D_MODEL), BF16),
        in_specs=[pl.BlockSpec((t, D_MODEL), lambda i: (i, 0)), _resident((MEM_LEN, 2 * D_MODEL))],
        out_specs=pl.BlockSpec((t, D_MODEL), lambda i: (i, 0)), compiler_params=_params(("parallel",)),
    )(zc, mkv)


def _mem_bwd(zc, do_c, mkv, *, name):
    s = zc.shape[0]
    t = min(512, s)

    def body(zc_ref, do_ref, mkv_ref, dz_ref, dmkv_ref):
        @pl.when(pl.program_id(0) == 0)
        def _():
            dmkv_ref[...] = jnp.zeros_like(dmkv_ref)

        for h in range(MEM_HEADS):
            cols = slice(h * MEM_HEAD_DIM, (h + 1) * MEM_HEAD_DIM)
            vcols = slice(D_MODEL + h * MEM_HEAD_DIM, D_MODEL + (h + 1) * MEM_HEAD_DIM)
            qh, p = _mem_probs(zc_ref, mkv_ref, h)
            doh = do_ref[:, cols].astype(BF16)
            dp = _dot(doh, mkv_ref[:, vcols], NT)
            ds = (p * (dp - jnp.sum(p * dp, axis=-1, keepdims=True))).astype(BF16)
            dz_ref[:, cols] = (_dot(ds, mkv_ref[:, cols], NN) * (MEM_HEAD_DIM ** -0.5)).astype(BF16)
            dmkv_ref[:, cols] += _dot(ds, qh, TN)
            dmkv_ref[:, vcols] += _dot(p.astype(BF16), doh, TN)

    return pl.pallas_call(
        body, name=name, grid=(s // t,),
        out_shape=(jax.ShapeDtypeStruct((s, D_MODEL), BF16), jax.ShapeDtypeStruct((MEM_LEN, 2 * D_MODEL), F32)),
        in_specs=[pl.BlockSpec((t, D_MODEL), lambda i: (i, 0)), pl.BlockSpec((t, D_MODEL), lambda i: (i, 0)),
                  _resident((MEM_LEN, 2 * D_MODEL))],
        out_specs=(pl.BlockSpec((t, D_MODEL), lambda i: (i, 0)), pl.BlockSpec((MEM_LEN, 2 * D_MODEL), lambda i: (0, 0))),
        compiler_params=_params(("arbitrary",)),
    )(zc, do_c, mkv)


def _normalize(pre):
    mu = jnp.mean(pre, axis=-1, keepdims=True)
    xc = pre - mu
    rstd = lax.rsqrt(jnp.mean(xc * xc, axis=-1, keepdims=True) + LN_EPS)
    return xc * rstd, rstd


def _layer_norm_bwd(dh, xhat, rstd, g):
    dxh = dh * g
    dpre = rstd * (dxh - jnp.mean(dxh, axis=-1, keepdims=True) - xhat * jnp.mean(dxh * xhat, axis=-1, keepdims=True))
    return dpre, jnp.sum(dh * xhat, axis=0, keepdims=True), jnp.sum(dh, axis=0, keepdims=True)


def _merge_fwd(o_a, o_b, o_c, zd, x, wbr, wo, *, name):
    s = x.shape[0]
    t = min(256, s)
    row = lambda w, dt=None: pl.BlockSpec((t, w), lambda i: (i, 0))

    def body(oa_ref, ob_ref, oc_ref, zd_ref, x_ref, wbr_ref, wo_ref, xhat_ref, rstd_ref, merged_ref, pa_ref, pb_ref, pc_ref):
        merged = jnp.zeros((t, D_MODEL), F32)
        for b, (o_ref, p_ref) in enumerate(((oa_ref, pa_ref), (ob_ref, pb_ref), (oc_ref, pc_ref))):
            p = _dot(o_ref[...], wbr_ref[b], NN)
            p_ref[...] = p
            merged = merged + jax.nn.sigmoid(zd_ref[:, b * D_MODEL:(b + 1) * D_MODEL]) * p
        merged_b = merged.astype(BF16)
        merged_ref[...] = merged_b
        xhat, rstd = _normalize(ALPHA * x_ref[...] + _dot(merged_b, wo_ref[...], NN))
        xhat_ref[...] = xhat
        rstd_ref[...] = rstd

    act = jax.ShapeDtypeStruct((s, D_MODEL), F32)
    return pl.pallas_call(
        body, name=name, grid=(s // t,),
        out_shape=(act, jax.ShapeDtypeStruct((s, 1), F32), jax.ShapeDtypeStruct((s, D_MODEL), BF16), act, act, act),
        in_specs=[row(D_MODEL), row(D_MODEL), row(D_MODEL), row(W_D), row(D_MODEL),
                  _resident((3, D_MODEL, D_MODEL)), _resident((D_MODEL, D_MODEL))],
        out_specs=(row(D_MODEL), row(1), row(D_MODEL), row(D_MODEL), row(D_MODEL), row(D_MODEL)),
        compiler_params=_params(("parallel",)),
    )(o_a, o_b, o_c, zd, x, wbr, wo)


def _merge_bwd(dpre1, zd, pa, pb, pc, wbr, wo, *, name):
    s = dpre1.shape[0]
    t = min(256, s)
    row = lambda w: pl.BlockSpec((t, w), lambda i: (i, 0))

    def body(dpre_ref, zd_ref, pa_ref, pb_ref, pc_ref, wbr_ref, wo_ref, dzd_ref, dpa_ref, dpb_ref, dpc_ref, doa_ref, dob_ref, doc_ref):
        dmerged = _dot(dpre_ref[...].astype(BF16), wo_ref[...], NT)
        branches = ((pa_ref, dpa_ref, doa_ref), (pb_ref, dpb_ref, dob_ref), (pc_ref, dpc_ref, doc_ref))
        for b, (p_ref, dp_ref, do_ref) in enumerate(branches):
            gate = jax.nn.sigmoid(zd_ref[:, b * D_MODEL:(b + 1) * D_MODEL])
            dzd_ref[:, b * D_MODEL:(b + 1) * D_MODEL] = (dmerged * p_ref[...] * gate * (1.0 - gate)).astype(BF16)
            dp = (dmerged * gate).astype(BF16)
            dp_ref[...] = dp
            do_ref[...] = _dot(dp, wbr_ref[b], NT)

    act = jax.ShapeDtypeStruct((s, D_MODEL), F32)
    actb = jax.ShapeDtypeStruct((s, D_MODEL), BF16)
    return pl.pallas_call(
        body, name=name, grid=(s // t,),
        out_shape=(jax.ShapeDtypeStruct((s, W_D), BF16), actb, actb, actb, act, act, act),
        in_specs=[row(D_MODEL), row(W_D), row(D_MODEL), row(D_MODEL), row(D_MODEL),
                  _resident((3, D_MODEL, D_MODEL)), _resident((D_MODEL, D_MODEL))],
        out_specs=(row(W_D),) + (row(D_MODEL),) * 6,
        compiler_params=_params(("parallel",)),
    )(dpre1, zd, pa, pb, pc, wbr, wo)


def _mlp_loss(xhat1, rstd1, target, ln1_g, ln1_b, ln2_g, ln2_b, wu, wd, *, name):
    s = xhat1.shape[0]
    t = min(256, s)
    npan = wu.shape[0]
    row = lambda w: pl.BlockSpec((t, w), lambda i: (i, 0))
    vec = _resident((1, D_MODEL))

    def body(xhat_ref, rstd_ref, tgt_ref, g1_ref, b1_ref, g2_ref, b2_ref, wu_ref, wd_ref,
             dpre1_ref, dpre2_ref, h1_ref, a_ref, du_ref, stats_ref):
        @pl.when(pl.program_id(0) == 0)
        def _():
            stats_ref[...] = jnp.zeros_like(stats_ref)

        xhat1_v = xhat_ref[...]
        h1 = xhat1_v * g1_ref[...] + b1_ref[...]
        h1_b = h1.astype(BF16)
        h1_ref[...] = h1_b
        us = []
        ff = jnp.zeros((t, D_MODEL), F32)
        for j in range(npan):
            u = _dot(h1_b, wu_ref[j], NN)
            us.append(u)
            r = jnp.maximum(u, 0.0)
            a_b = (r * r).astype(BF16)
            a_ref[:, j * D_MODEL:(j + 1) * D_MODEL] = a_b
            ff = ff + _dot(a_b, wd_ref[j], NN)
        xhat2, rstd2 = _normalize(ALPHA * h1 + ff)
        err = xhat2 * g2_ref[...] + b2_ref[...] - tgt_ref[...]
        stats_ref[4:5, :] += jnp.sum(err * err, axis=0, keepdims=True)
        dpre2, dg2, db2 = _layer_norm_bwd(err * (1.0 / D_MODEL), xhat2, rstd2, g2_ref[...])
        stats_ref[0:1, :] += dg2
        stats_ref[1:2, :] += db2
        dpre2_b = dpre2.astype(BF16)
        dpre2_ref[...] = dpre2_b
        dh1 = ALPHA * dpre2
        for j in range(npan):
            du_b = (_dot(dpre2_b, wd_ref[j], NT) * (2.0 * jnp.maximum(us[j], 0.0))).astype(BF16)
            du_ref[:, j * D_MODEL:(j + 1) * D_MODEL] = du_b
            dh1 = dh1 + _dot(du_b, wu_ref[j], NT)
        dpre1, dg1, db1 = _layer_norm_bwd(dh1, xhat1_v, rstd_ref[...], g1_ref[...])
        stats_ref[2:3, :] += dg1
        stats_ref[3:4, :] += db1
        dpre1_ref[...] = dpre1

    actb = jax.ShapeDtypeStruct((s, D_MODEL), BF16)
    wide = jax.ShapeDtypeStruct((s, D_FF), BF16)
    return pl.pallas_call(
        body, name=name, grid=(s // t,),
        out_shape=(jax.ShapeDtypeStruct((s, D_MODEL), F32), actb, actb, wide, wide, jax.ShapeDtypeStruct((8, D_MODEL), F32)),
        in_specs=[row(D_MODEL), row(1), row(D_MODEL), vec, vec, vec, vec,
                  _resident((npan, D_MODEL, D_MODEL)), _resident((npan, D_MODEL, D_MODEL))],
        out_specs=(row(D_MODEL), row(D_MODEL), row(D_MODEL), row(D_FF), row(D_FF), pl.BlockSpec((8, D_MODEL), lambda i: (0, 0))),
        compiler_params=_params(("arbitrary",)),
    )(xhat1, rstd1, target, ln1_g, ln1_b, ln2_g, ln2_b, wu, wd)


def _local_step(x, mem, target, wi_parts, wmkv, wbr, wo, wu, wd, lb_logits, gain, sinks, rel_bias, ln1_g, ln1_b, ln2_g, ln2_b):
    s = x.shape[0]
    tm = min(1024, s)
    tk = min(2048, s)
    xb = x.astype(BF16)
    memb = mem.astype(BF16)
    wia, wib, wic, wid = wi_parts

    za = _mm(xb, wia, mode="nn", tm=min(512, s), tn=W_A, tk=D_MODEL, name="proj_a")
    zb = _mm(xb, wib, mode="nn", tm=tm, tn=W_B, tk=D_MODEL, name="proj_b")
    zc = _mm(xb, wic, mode="nn", tm=tm, tn=W_C, tk=D_MODEL, name="proj_c")
    zd = _mm(xb, wid, mode="nn", tm=min(512, s), tn=W_D, tk=D_MODEL, name="proj_d")
    mkv = _mm(memb, wmkv, mode="nn", tm=MEM_LEN, tn=512, tk=D_MODEL, name="mem_kv", out_dtype=BF16, b_panels=True)
    onehot, maskrow = _bias_selector()
    bias = _bias_table(rel_bias.T, onehot, maskrow, name="bias_table").reshape(SWA_HEADS, SWA_BLOCK, 2 * SWA_BLOCK)
    o_a, o_raw, states = _hgrn_fwd(za, lb_logits, gain, name="hgrn_fwd")
    o_b, lse = _swa_fwd(zb, bias, sinks, name="swa_fwd")
    o_c = _mem_fwd(zc, mkv, name="mem_fwd")
    xhat1, rstd1, merged, pa, pb, pc = _merge_fwd(o_a, o_b, o_c, zd, x, wbr, wo, name="merge_fwd")

    dpre1, dpre2, h1, act, du, ln_stats = _mlp_loss(xhat1, rstd1, target, ln1_g, ln1_b, ln2_g, ln2_b, wu, wd, name="mlp_loss")
    g_wd = _mm(act, dpre2, mode="tn", tm=1024, tn=D_MODEL, tk=tk, name="grad_w_down")
    g_wu = _mm(h1, du, mode="tn", tm=D_MODEL, tn=1024, tk=tk, name="grad_w_up", out_panels=True)

    dzd, dpa, dpb, dpc, do_a, do_b, do_c = _merge_bwd(dpre1, zd, pa, pb, pc, wbr, wo, name="merge_bwd")
    g_wo = _mm(merged, dpre1, mode="tn", tm=D_MODEL, tn=D_MODEL, tk=tk, name="grad_w_out")
    g_wbh = _mm(o_a, dpa, mode="tn", tm=D_MODEL, tn=D_MODEL, tk=tk, name="grad_w_branch_hg")
    g_wbs = _mm(o_b, dpb, mode="tn", tm=D_MODEL, tn=D_MODEL, tk=tk, name="grad_w_branch_swa")
    g_wbm = _mm(o_c, dpc, mode="tn", tm=D_MODEL, tn=D_MODEL, tk=tk, name="grad_w_branch_mem")
    dza, hg_stats = _hgrn_bwd(za, o_raw, do_a, states, lb_logits, gain, name="hgrn_bwd")
    dzb, dbias, dsinks = _swa_bwd(zb, do_b, lse, bias, sinks, name="swa_bwd")
    d_rel_bias = _bias_grad(dbias.reshape(SWA_HEADS, -1), onehot, name="bias_grad").T
    dzc, dmkv = _mem_bwd(zc, do_c, mkv, name="mem_bwd")
    g_wmkv = _mm(memb, dmkv, mode="tn", tm=D_MODEL, tn=512, tk=MEM_LEN, name="grad_w_mem_kv", out_panels=True)

    g_wi = [_mm(xb, dz, mode="tn", tm=min(512, D_MODEL), tn=dz.shape[1] if dz.shape[1] <= 1280 else 1024, tk=tk, name=nm)
            for dz, nm in ((dza, "grad_w_in_a"), (dzb, "grad_w_in_b"), (dzc, "grad_w_in_c"), (dzd, "grad_w_in_d"))]
    grad_x = _dx_matmul([dza, dzb, dzc, dzd], [wia, wib, wic, wid], dpre1, tm=min(512, s), tks=[1024, W_B, 1024, 1024], name="grad_x")
    big = dict(w_in=jnp.concatenate(g_wi, axis=1), w_mem_kv=g_wmkv, w_branch_hg=g_wbh, w_branch_swa=g_wbs, w_branch_mem=g_wbm,
               w_out=g_wo, w_up=g_wu, w_down=g_wd.reshape(N_SHARDS, D_FF // N_SHARDS, D_MODEL))
    small = dict(lb_logits=hg_stats[1:3], hg_norm_gain=hg_stats[0:1], swa_sinks=dsinks, rel_bias=d_rel_bias,
                 ln1_g=ln_stats[2:3], ln1_b=ln_stats[3:4], ln2_g=ln_stats[0:1], ln2_b=ln_stats[1:2], sq_err=ln_stats[4:5])
    return grad_x, big, small


def _mesh_position():
    x, y, c = lax.axis_index("x"), lax.axis_index("y"), lax.axis_index("c")
    chips = [(1 - x, y), (x, 1 - y), (1 - x, 1 - y)]
    return x, y, c, chips


def _all_gather(shards, *, name):
    n = len(shards)
    per = 7

    def body(*refs):
        ins, outs = refs[:n], refs[n:2 * n]
        send_sems, recv_sems = refs[2 * n:]
        x, y, c, chips = _mesh_position()
        me = 2 * x + y
        sibling = (x, y, 1 - c)

        def half(a, slot, hc):
            rh = shards[a].shape[0] // 2
            return outs[a].at[slot, pl.ds(hc * rh, rh), :]

        def copy(a, k, src, dst, to):
            return pltpu.make_async_remote_copy(src_ref=src, dst_ref=dst, send_sem=send_sems.at[a * per + k], recv_sem=recv_sems.at[a * per + k],
                                                device_id=to, device_id_type=MESH)

        started = []
        for a in range(n):
            cp = copy(a, 6, ins[a], outs[a].at[me], sibling)
            cp.start()
            started.append(cp)
        for k, (cx, cy) in enumerate(chips):
            for a in range(n):
                rh = shards[a].shape[0] // 2
                cp = copy(a, k, ins[a].at[pl.ds(c * rh, rh), :], half(a, me, c), (cx, cy, c))
                cp.start()
                started.append(cp)
        for k, (cx, cy) in enumerate(chips):
            slot = 2 * cx + cy
            for a in range(n):
                copy(a, k, half(a, slot, c), half(a, slot, c), (cx, cy, c)).wait_recv()
                cp = copy(a, 3 + k, half(a, slot, c), half(a, slot, c), sibling)
                cp.start()
                started.append(cp)
        for k, (cx, cy) in enumerate(chips):
            slot = 2 * cx + cy
            for a in range(n):
                copy(a, 3 + k, half(a, slot, 1 - c), half(a, slot, 1 - c), sibling).wait_recv()
        for a in range(n):
            copy(a, 6, outs[a].at[me], outs[a].at[me], sibling).wait_recv()
        for cp in started:
            cp.wait_send()

    return pl.pallas_call(
        body, name=name, out_shape=[jax.ShapeDtypeStruct((N_SHARDS,) + w.shape, w.dtype) for w in shards],
        in_specs=[HBM] * n, out_specs=[HBM] * n,
        scratch_shapes=[pltpu.SemaphoreType.DMA((per * n,)), pltpu.SemaphoreType.DMA((per * n,))],
    )(*shards)


def _exchange_sibling_halves(grads, *, name):
    n = len(grads)

    def body(*refs):
        ins, outs = refs[:n], refs[n:2 * n]
        send_sems, recv_sems = refs[2 * n:]
        x, y, c, _ = _mesh_position()
        copies = []
        for a in range(n):
            rh = grads[a].shape[1] // 2
            cp = pltpu.make_async_remote_copy(src_ref=ins[a].at[:, pl.ds((1 - c) * rh, rh), :], dst_ref=outs[a], send_sem=send_sems.at[a],
                                              recv_sem=recv_sems.at[a], device_id=(x, y, 1 - c), device_id_type=MESH)
            cp.start()
            copies.append(cp)
        for cp in copies:
            cp.wait()

    return pl.pallas_call(
        body, name=name, out_shape=[jax.ShapeDtypeStruct((g.shape[0], g.shape[1] // 2, g.shape[2]), g.dtype) for g in grads],
        in_specs=[HBM] * n, out_specs=[HBM] * n,
        scratch_shapes=[pltpu.SemaphoreType.DMA((n,)), pltpu.SemaphoreType.DMA((n,))],
    )(*grads)


def _row_tile(rows):
    for tr in (256, 128, 64, 32, 16, 8):
        if rows % tr == 0:
            return tr
    raise ValueError(rows)


def _add_sibling(grad, other, pos, *, name):
    p, r, cols = grad.shape
    rh = r // 2
    tr = _row_tile(rh)
    nb = rh // tr

    def body(pos_ref, g_ref, o_ref, sb_ref, mine_ref):
        total = g_ref[...] + o_ref[...]
        sb_ref[...] = total.astype(BF16)

        @pl.when(pl.program_id(1) == pos_ref[0])
        def _():
            mine_ref[...] = total

    return pl.pallas_call(
        body, name=name, out_shape=(jax.ShapeDtypeStruct((p, rh, cols), BF16), jax.ShapeDtypeStruct((rh, cols), F32)),
        grid_spec=pltpu.PrefetchScalarGridSpec(
            num_scalar_prefetch=1, grid=(nb, p),
            in_specs=[pl.BlockSpec((None, tr, cols), lambda i, j, pos_ref: (j, pos_ref[1] * nb + i, 0)),
                      pl.BlockSpec((None, tr, cols), lambda i, j, pos_ref: (j, i, 0))],
            out_specs=(pl.BlockSpec((None, tr, cols), lambda i, j, pos_ref: (j, i, 0)),
                       pl.BlockSpec((tr, cols), lambda i, j, pos_ref: (i, 0)))),
        compiler_params=_params(("parallel", "arbitrary")),
    )(pos, grad, other)


def _exchange_chip_partials(sums, *, name):
    n = len(sums)

    def body(*refs):
        ins, outs = refs[:n], refs[n:2 * n]
        send_sems, recv_sems = refs[2 * n:]
        _, _, c, chips = _mesh_position()
        copies = []
        for k, (cx, cy) in enumerate(chips):
            for a in range(n):
                cp = pltpu.make_async_remote_copy(src_ref=ins[a].at[2 * cx + cy], dst_ref=outs[a].at[k], send_sem=send_sems.at[a * 3 + k],
                                                  recv_sem=recv_sems.at[a * 3 + k], device_id=(cx, cy, c), device_id_type=MESH)
                cp.start()
                copies.append(cp)
        for cp in copies:
            cp.wait()

    return pl.pallas_call(
        body, name=name, out_shape=[jax.ShapeDtypeStruct((3,) + g.shape[1:], g.dtype) for g in sums],
        in_specs=[HBM] * n, out_specs=[HBM] * n,
        scratch_shapes=[pltpu.SemaphoreType.DMA((3 * n,)), pltpu.SemaphoreType.DMA((3 * n,))],
    )(*sums)


def _add_chips(mine, others, pos, *, name):
    rh, cols = mine.shape
    tr = _row_tile(rh)
    nb = rh // tr

    def body(pos_ref, s_ref, o_ref, r_ref):
        r_ref[...] = ((s_ref[...] + o_ref[0].astype(F32)) + o_ref[1].astype(F32)) + o_ref[2].astype(F32)

    return pl.pallas_call(
        body, name=name, out_shape=jax.ShapeDtypeStruct((2 * rh, cols), F32),
        grid_spec=pltpu.PrefetchScalarGridSpec(
            num_scalar_prefetch=1, grid=(nb,),
            in_specs=[pl.BlockSpec((tr, cols), lambda i, pos_ref: (i, 0)),
                      pl.BlockSpec((3, tr, cols), lambda i, pos_ref: (0, i, 0))],
            out_specs=pl.BlockSpec((tr, cols), lambda i, pos_ref: (pos_ref[1] * nb + i, 0))),
        compiler_params=_params(("parallel",)),
    )(pos, mine, others)


def _join_halves(bufs, *, name):
    n = len(bufs)

    def body(*refs):
        ins, outs = refs[:n], refs[n:2 * n]
        send_sems, recv_sems = refs[2 * n:]
        x, y, c, _ = _mesh_position()

        def copy(a, hc):
            rh = bufs[a].shape[0] // 2
            rows = pl.ds(hc * rh, rh)
            return pltpu.make_async_remote_copy(src_ref=ins[a].at[rows, :], dst_ref=outs[a].at[rows, :], send_sem=send_sems.at[a],
                                                recv_sem=recv_sems.at[a], device_id=(x, y, 1 - c), device_id_type=MESH)

        for a in range(n):
            copy(a, c).start()
        for a in range(n):
            copy(a, c).wait_send()
            copy(a, 1 - c).wait_recv()

    return pl.pallas_call(
        body, name=name, out_shape=[jax.ShapeDtypeStruct(b.shape, b.dtype) for b in bufs],
        in_specs=[HBM] * n, out_specs=[HBM] * n, input_output_aliases={a: a for a in range(n)},
        scratch_shapes=[pltpu.SemaphoreType.DMA((n,)), pltpu.SemaphoreType.DMA((n,))],
    )(*bufs)


def _all_reduce_small(packed, *, name):
    rows, cols = packed.shape

    def body(in_ref, out_ref, gathered, send_sems, recv_sems):
        x, y, c, _ = _mesh_position()
        me = 4 * x + 2 * y + c
        gathered[me] = in_ref[...]
        copies = []
        for d in range(1, 8):
            dx, dy, dc = (d >> 2) & 1, (d >> 1) & 1, d & 1
            peer = (x ^ dx, y ^ dy, c ^ dc)
            cp = pltpu.make_async_remote_copy(src_ref=in_ref, dst_ref=gathered.at[me], send_sem=send_sems.at[d - 1], recv_sem=recv_sems.at[d - 1],
                                              device_id=peer, device_id_type=MESH)
            cp.start()
            copies.append(cp)
        for cp in copies:
            cp.wait()
        total = gathered[0]
        for j in range(1, 8):
            total = total + gathered[j]
        out_ref[...] = total

    vm = pl.BlockSpec(memory_space=pltpu.VMEM)
    return pl.pallas_call(
        body, name=name, out_shape=jax.ShapeDtypeStruct((rows, cols), F32), in_specs=[vm], out_specs=vm,
        scratch_shapes=[pltpu.VMEM((8, rows, cols), F32), pltpu.SemaphoreType.DMA((7,)), pltpu.SemaphoreType.DMA((7,))],
    )(packed)


def _adamw_math(w, g, m, v):
    m = ADAM_B1 * m + (1.0 - ADAM_B1) * g
    v = ADAM_B2 * v + (1.0 - ADAM_B2) * (g * g)
    m_hat = m / (1.0 - ADAM_B1 ** ADAM_STEP)
    v_hat = v / (1.0 - ADAM_B2 ** ADAM_STEP)
    delta = -ADAM_LR * (m_hat / (jnp.sqrt(v_hat) + ADAM_EPS) + ADAM_WD * w)
    return delta, m, v


def _adamw(w, g, m, v, *, name):
    rows, cols = w.shape
    tr = _row_tile(rows)
    blk = pl.BlockSpec((tr, cols), lambda i: (i, 0))

    def body(w_ref, g_ref, m_ref, v_ref, d_ref, nm_ref, nv_ref):
        d_ref[...], nm_ref[...], nv_ref[...] = _adamw_math(w_ref[...], g_ref[...], m_ref[...], v_ref[...])

    shape = jax.ShapeDtypeStruct((rows, cols), F32)
    return pl.pallas_call(body, name=name, grid=(rows // tr,), out_shape=(shape, shape, shape), in_specs=[blk] * 4, out_specs=(blk,) * 3,
                          compiler_params=_params(("parallel",)))(w, g, m, v)


def _adamw_small(w, g, m, v, *, name):
    def body(w_ref, g_ref, m_ref, v_ref, d_ref, nm_ref, nv_ref, loss_ref):
        d_ref[...], nm_ref[...], nv_ref[...] = _adamw_math(w_ref[...], g_ref[...], m_ref[...], v_ref[...])
        loss_ref[...] = (0.5 / D_MODEL) * jnp.sum(g_ref[8:9, :], axis=1, keepdims=True)

    shape = jax.ShapeDtypeStruct(w.shape, F32)
    return pl.pallas_call(body, name=name, out_shape=(shape, shape, shape, jax.ShapeDtypeStruct((1, 1), F32)),
                          compiler_params=_params())(w, g, m, v)


SMALL_ROWS = 16


def _pack_small(lb_logits, gain, sinks, rel_bias, ln1_g, ln1_b, ln2_g, ln2_b, extra=None):
    misc = jnp.concatenate([sinks.reshape(1, -1), rel_bias.reshape(1, -1)], axis=1)
    misc = jnp.pad(misc, ((0, 0), (0, D_MODEL - misc.shape[1])))
    rows = [lb_logits, gain, ln1_g, ln1_b, ln2_g, ln2_b, misc, extra if extra is not None else jnp.zeros((1, D_MODEL), F32)]
    used = sum(r.shape[0] for r in rows)
    return jnp.concatenate(rows + [jnp.zeros((SMALL_ROWS - used, D_MODEL), F32)], axis=0)


def _unpack_small(p):
    return dict(lb_logits=p[0:2], hg_norm_gain=p[2:3], ln1_g=p[3:4], ln1_b=p[4:5], ln2_g=p[5:6], ln2_b=p[6:7],
                swa_sinks=p[7:8, 0:SWA_HEADS], rel_bias=p[7:8, SWA_HEADS:SWA_HEADS + NUM_BUCKETS * SWA_HEADS].reshape(NUM_BUCKETS, SWA_HEADS))


WEIGHTS = ["w_in", "lb_logits", "hg_norm_gain", "swa_sinks", "rel_bias", "w_mem_kv", "w_branch_hg", "w_branch_swa", "w_branch_mem",
           "w_out", "ln1_g", "ln1_b", "w_up", "w_down", "ln2_g", "ln2_b"]
BIG = ["w_in", "w_mem_kv", "w_branch_hg", "w_branch_swa", "w_branch_mem", "w_out", "w_up", "w_down"]
SMALL = ["lb_logits", "hg_norm_gain", "swa_sinks", "rel_bias", "ln1_g", "ln1_b", "ln2_g", "ln2_b"]


def kernel(x, mem, w_in, lb_logits, hg_norm_gain, swa_sinks, rel_bias, w_mem_kv, w_branch_hg, w_branch_swa, w_branch_mem, w_out, ln1_g, ln1_b, w_up, w_down, ln2_g, ln2_b, loss_target, m_w_in, m_lb_logits, m_hg_norm_gain, m_swa_sinks, m_rel_bias, m_w_mem_kv, m_w_branch_hg, m_w_branch_swa, m_w_branch_mem, m_w_out, m_ln1_g, m_ln1_b, m_w_up, m_w_down, m_ln2_g, m_ln2_b, v_w_in, v_lb_logits, v_hg_norm_gain, v_swa_sinks, v_rel_bias, v_w_mem_kv, v_w_branch_hg, v_w_branch_swa, v_w_branch_mem, v_w_out, v_ln1_g, v_ln1_b, v_w_up, v_w_down, v_ln2_g, v_ln2_b):
    w = dict(w_in=w_in, lb_logits=lb_logits, hg_norm_gain=hg_norm_gain, swa_sinks=swa_sinks, rel_bias=rel_bias, w_mem_kv=w_mem_kv,
             w_branch_hg=w_branch_hg, w_branch_swa=w_branch_swa, w_branch_mem=w_branch_mem, w_out=w_out, ln1_g=ln1_g, ln1_b=ln1_b,
             w_up=w_up, w_down=w_down, ln2_g=ln2_g, ln2_b=ln2_b)
    m = dict(w_in=m_w_in, lb_logits=m_lb_logits, hg_norm_gain=m_hg_norm_gain, swa_sinks=m_swa_sinks, rel_bias=m_rel_bias, w_mem_kv=m_w_mem_kv,
             w_branch_hg=m_w_branch_hg, w_branch_swa=m_w_branch_swa, w_branch_mem=m_w_branch_mem, w_out=m_w_out, ln1_g=m_ln1_g, ln1_b=m_ln1_b,
             w_up=m_w_up, w_down=m_w_down, ln2_g=m_ln2_g, ln2_b=m_ln2_b)
    v = dict(w_in=v_w_in, lb_logits=v_lb_logits, hg_norm_gain=v_hg_norm_gain, swa_sinks=v_swa_sinks, rel_bias=v_rel_bias, w_mem_kv=v_w_mem_kv,
             w_branch_hg=v_w_branch_hg, w_branch_swa=v_w_branch_swa, w_branch_mem=v_w_branch_mem, w_out=v_w_out, ln1_g=v_ln1_g, ln1_b=v_ln1_b,
             w_up=v_w_up, w_down=v_w_down, ln2_g=v_ln2_g, ln2_b=v_ln2_b)
    shapes = {k: w[k].shape for k in WEIGHTS}
    for d in (w, m, v):
        for k in BIG:
            d[k] = d[k].reshape(d[k].shape[-2], d[k].shape[-1])

    gathered = _all_gather([w[k].astype(BF16) for k in BIG], name="gather_weights")
    full = dict(zip(BIG, gathered))
    wi = full["w_in"].transpose(1, 0, 2).reshape(D_MODEL, IN_COLS)
    wi_parts = (wi[:, 0:W_A], wi[:, W_A:W_A + W_B], wi[:, W_A + W_B:W_A + W_B + W_C], wi[:, W_A + W_B + W_C:])
    wbr = jnp.stack([full[k].reshape(D_MODEL, D_MODEL) for k in ("w_branch_hg", "w_branch_swa", "w_branch_mem")])
    wo = full["w_out"].reshape(D_MODEL, D_MODEL)

    grad_x, big, small = _local_step(
        x.reshape(x.shape[-2], D_MODEL), mem.reshape(MEM_LEN, D_MODEL), loss_target.reshape(loss_target.shape[-2], D_MODEL),
        wi_parts, full["w_mem_kv"], wbr, wo, full["w_up"], full["w_down"],
        lb_logits, hg_norm_gain, swa_sinks, rel_bias, ln1_g, ln1_b, ln2_g, ln2_b)

    big["w_in"] = big["w_in"].reshape(D_MODEL, N_SHARDS, IN_COLS // N_SHARDS).transpose(1, 0, 2)
    for k in ("w_branch_hg", "w_branch_swa", "w_branch_mem", "w_out"):
        big[k] = big[k].reshape(N_SHARDS, D_MODEL // N_SHARDS, D_MODEL)
    grads = [big[k] for k in BIG]
    cx, cy, cc = lax.axis_index("x"), lax.axis_index("y"), lax.axis_index("c")
    pos = jnp.stack([2 * cx + cy, cc]).astype(jnp.int32)
    from_sibling = _exchange_sibling_halves(grads, name="reduce_sibling")
    chip_sums = [_add_sibling(g, o, pos, name="add_sibling_" + k) for g, o, k in zip(grads, from_sibling, BIG)]
    from_chips = _exchange_chip_partials([s_[0] for s_ in chip_sums], name="reduce_chips")
    halves = [_add_chips(s_[1], o, pos, name="add_chips_" + k) for s_, o, k in zip(chip_sums, from_chips, BIG)]
    reduced = dict(zip(BIG, _join_halves(halves, name="join_halves")))

    packed_g = _pack_small(small["lb_logits"], small["hg_norm_gain"], small["swa_sinks"], small["rel_bias"], small["ln1_g"], small["ln1_b"],
                           small["ln2_g"], small["ln2_b"], extra=small["sq_err"])
    packed_g = _all_reduce_small(packed_g, name="reduce_small")

    grad_out, delta_out, m_out, v_out = {}, {}, {}, {}
    for k in BIG:
        d_, m_, v_ = _adamw(w[k], reduced[k], m[k], v[k], name="adamw_" + k)
        grad_out[k], delta_out[k], m_out[k], v_out[k] = reduced[k], d_, m_, v_
    pack = lambda d: _pack_small(d["lb_logits"], d["hg_norm_gain"], d["swa_sinks"], d["rel_bias"], d["ln1_g"], d["ln1_b"], d["ln2_g"], d["ln2_b"])
    d_s, m_s, v_s, loss = _adamw_small(pack(w), packed_g, pack(m), pack(v), name="adamw_small")
    for out, p in ((grad_out, packed_g), (delta_out, d_s), (m_out, m_s), (v_out, v_s)):
        out.update(_unpack_small(p))

    result = [loss.reshape(()), grad_x.reshape(x.shape)]
    for out in (grad_out, delta_out, m_out, v_out):
        result += [out[k].reshape(shapes[k]) for k in WEIGHTS]
    return tuple(result)
```

```python
import functools
import math

import jax
import jax.numpy as jnp
from jax import lax
from jax.experimental import pallas as pl
from jax.experimental.pallas import tpu as pltpu

F32 = jnp.float32
BF16 = jnp.bfloat16
HIGHEST = lax.Precision.HIGHEST
MESH = pl.DeviceIdType.MESH

D_MODEL = 1024
MEM_LEN = 256
HG_HEADS = 8
HG_DK = 128
HG_CHUNK = 64
SWA_HEADS = 16
SWA_KV_HEADS = 2
SWA_GROUP = 8
SWA_HEAD_DIM = 64
SWA_BLOCK = 128
SWA_WINDOW = 128
MEM_HEADS = 4
MEM_HEAD_DIM = 256
NUM_BUCKETS = 32
MAX_DISTANCE = 128
D_FF = 4096
LN_EPS = 1e-5
RMS_EPS = 1e-6
ALPHA = 2.0 ** 0.25
W_A, W_B, W_C, W_D = 4096, 1280, 1024, 3072
IN_COLS = W_A + W_B + W_C + W_D
N_SHARDS = 4
ADAM_LR = 0.001
ADAM_B1 = 0.9
ADAM_B2 = 0.999
ADAM_EPS = 1e-08
ADAM_WD = 0.01
ADAM_STEP = 10
MASK_VALUE = -1e30
VMEM_LIMIT = 56 * 1024 * 1024

NN = ((1,), (0,))
NT = ((1,), (1,))
TN = ((0,), (0,))
HBM = pl.BlockSpec(memory_space=pltpu.HBM)


def _dot(a, b, dims=NN, precision=None):
    return lax.dot_general(a, b, (dims, ((), ())), precision=precision, preferred_element_type=F32)


def _params(sem=None):
    return pltpu.CompilerParams(dimension_semantics=sem, vmem_limit_bytes=VMEM_LIMIT)


def _resident(shape):
    zeros = (0,) * len(shape)
    return pl.BlockSpec(shape, lambda *_: zeros, pipeline_mode=pl.Buffered(1))


def _mm(a, b, *, mode, tm, tn, tk, name, out_dtype=F32, b_panels=False, out_panels=False, add=None, add_scale=1.0):
    if mode == "tn":
        kdim, m = a.shape
    else:
        m, kdim = a.shape
    if b_panels:
        n = b.shape[0] * b.shape[2]
        assert b.shape[2] == tn and mode == "nn"
    elif mode == "nt":
        n = b.shape[0]
    else:
        n = b.shape[1]
    assert m % tm == 0 and n % tn == 0 and kdim % tk == 0, (name, m, n, kdim)
    nk = kdim // tk
    dims = {"nn": NN, "nt": NT, "tn": TN}[mode]
    a_spec = pl.BlockSpec((tk, tm), lambda i, j, k: (k, i)) if mode == "tn" else pl.BlockSpec((tm, tk), lambda i, j, k: (i, k))
    if b_panels:
        b_spec = pl.BlockSpec((None, tk, tn), lambda i, j, k: (j, k, 0))
    elif mode == "nt":
        b_spec = pl.BlockSpec((tn, tk), lambda i, j, k: (j, k))
    else:
        b_spec = pl.BlockSpec((tk, tn), lambda i, j, k: (k, j))
    if out_panels:
        out_shape = jax.ShapeDtypeStruct((n // tn, m, tn), out_dtype)
        o_spec = pl.BlockSpec((None, tm, tn), lambda i, j, k: (j, i, 0))
    else:
        out_shape = jax.ShapeDtypeStruct((m, n), out_dtype)
        o_spec = pl.BlockSpec((tm, tn), lambda i, j, k: (i, j))
    in_specs = [a_spec, b_spec]
    operands = [a, b]
    if add is not None:
        in_specs.append(pl.BlockSpec((tm, tn), lambda i, j, k: (i, j)))
        operands.append(add)

    def body(*refs):
        a_ref, b_ref = refs[0], refs[1]
        add_ref = refs[2] if add is not None else None
        o_ref = refs[3] if add is not None else refs[2]
        part = _dot(a_ref[...].astype(BF16), b_ref[...].astype(BF16), dims)

        def finish(acc):
            if add_ref is not None:
                acc = acc + add_scale * add_ref[...]
            o_ref[...] = acc.astype(out_dtype)

        if nk == 1:
            finish(part)
        else:
            acc_ref = refs[-1]
            k = pl.program_id(2)

            @pl.when(k == 0)
            def _():
                acc_ref[...] = part

            @pl.when(k > 0)
            def _():
                acc_ref[...] += part

            @pl.when(k == nk - 1)
            def _():
                finish(acc_ref[...])

    return pl.pallas_call(
        body, name=name, out_shape=out_shape, grid=(m // tm, n // tn, nk), in_specs=in_specs, out_specs=o_spec,
        scratch_shapes=[pltpu.VMEM((tm, tn), F32)] if nk > 1 else [],
        compiler_params=_params(("parallel", "parallel", "arbitrary")),
    )(*operands)


def _dx_matmul(dzs, wis, resid, *, tm, tks, name):
    s = resid.shape[0]
    counts = [dz.shape[1] // tk for dz, tk in zip(dzs, tks)]
    starts = [sum(counts[:p]) for p in range(len(counts))]
    nk = sum(counts)
    npieces = len(dzs)

    def piece_block(p):
        return lambda i, k: (i, jnp.clip(k - starts[p], 0, counts[p] - 1))

    def weight_block(p):
        return lambda i, k: (0, jnp.clip(k - starts[p], 0, counts[p] - 1))

    in_specs = [pl.BlockSpec((tm, tks[p]), piece_block(p)) for p in range(npieces)]
    in_specs += [pl.BlockSpec((D_MODEL, tks[p]), weight_block(p)) for p in range(npieces)]
    in_specs += [pl.BlockSpec((tm, D_MODEL), lambda i, k: (i, 0))]

    def body(*refs):
        dz_refs, w_refs = refs[:npieces], refs[npieces:2 * npieces]
        r_ref, o_ref, acc_ref = refs[2 * npieces], refs[2 * npieces + 1], refs[2 * npieces + 2]
        k = pl.program_id(1)

        @pl.when(k == 0)
        def _():
            acc_ref[...] = ALPHA * r_ref[...]

        for p in range(npieces):
            @pl.when((k >= starts[p]) & (k < starts[p] + counts[p]))
            def _(p=p):
                acc_ref[...] += _dot(dz_refs[p][...], w_refs[p][...], NT)

        @pl.when(k == nk - 1)
        def _():
            o_ref[...] = acc_ref[...]

    return pl.pallas_call(
        body, name=name, out_shape=jax.ShapeDtypeStruct((s, D_MODEL), F32), grid=(s // tm, nk), in_specs=in_specs,
        out_specs=pl.BlockSpec((tm, D_MODEL), lambda i, k: (i, 0)), scratch_shapes=[pltpu.VMEM((tm, D_MODEL), F32)],
        compiler_params=_params(("parallel", "arbitrary")),
    )(*dzs, *wis, resid)


def _lower_bound(lbl_ref):
    l0, l1 = lbl_ref[0:1, :], lbl_ref[1:2, :]
    mx = jnp.maximum(l0, l1)
    e0, e1 = jnp.exp(l0 - mx), jnp.exp(l1 - mx)
    return e0 / (e0 + e1)


HEAD_COLS = [slice(h * HG_DK, (h + 1) * HG_DK) for h in range(HG_HEADS)]


def _head_mean(x):
    return jnp.concatenate([jnp.broadcast_to(jnp.mean(x[:, c], axis=-1, keepdims=True), (x.shape[0], HG_DK)) for c in HEAD_COLS], axis=1)


def _chunk_forward(q, fl, v, lb, tril_f):
    sg = jax.nn.sigmoid(fl)
    f = lb + (1.0 - lb) * sg
    k = 1.0 - f
    b = _dot(tril_f, jnp.log(f), NN, HIGHEST)
    b_last = b[HG_CHUNK - 1:HG_CHUNK, :]
    eb, enb, eo = jnp.exp(b), jnp.exp(-b), jnp.exp(b_last - b)
    return sg, f, k, b_last, eb, enb, eo, q * eb, k * enb, k * eo


def _hgrn_fwd(za, lb_logits, gain, *, name):
    s = za.shape[0]
    t = min(256, s)
    ncs = t // HG_CHUNK

    def body(z_ref, lbl_ref, gain_ref, oa_ref, oraw_ref, st_ref, state):
        @pl.when(pl.program_id(0) == 0)
        def _():
            state[...] = jnp.zeros_like(state)

        lb_all = _lower_bound(lbl_ref)
        row = lax.broadcasted_iota(jnp.int32, (HG_CHUNK, HG_CHUNK), 0)
        col = lax.broadcasted_iota(jnp.int32, (HG_CHUNK, HG_CHUNK), 1)
        tril = row >= col
        tril_f = tril.astype(F32)
        gain_all = gain_ref[...]

        def chunk(i, carry):
            r = pl.ds(pl.multiple_of(i * HG_CHUNK, HG_CHUNK), HG_CHUNK)
            q, fl, v, hg = (z_ref[r, j * D_MODEL:(j + 1) * D_MODEL] for j in range(4))
            _, _, _, b_last, _, _, _, q_in, k_in, k_out = _chunk_forward(q, fl, v, lb_all, tril_f)
            q_in_b, k_in_b, k_out_b, vb = (u.astype(BF16) for u in (q_in, k_in, k_out, v))
            decay = jnp.exp(b_last)
            sts = [state[h] for h in range(HG_HEADS)]
            attn = [_dot(q_in_b[:, c], k_in_b[:, c], NT) for c in HEAD_COLS]
            inter = [_dot(q_in_b[:, c], sts[h].astype(BF16), NT) for h, c in enumerate(HEAD_COLS)]
            upd = [_dot(vb[:, c], k_out_b[:, c], TN) for c in HEAD_COLS]
            attn = [jnp.where(tril, a, 0.0).astype(BF16) for a in attn]
            outs = [_dot(attn[h], vb[:, c], NN) + inter[h] for h, c in enumerate(HEAD_COLS)]
            for h, c in enumerate(HEAD_COLS):
                st_ref[h, i] = sts[h]
                state[h] = sts[h] * decay[:, c] + upd[h]
            o = jnp.concatenate(outs, axis=1)
            oraw_ref[r, :] = o
            n = o * lax.rsqrt(_head_mean(o * o) + RMS_EPS)
            oa_ref[r, :] = (n * gain_all * (hg * jax.nn.sigmoid(hg))).astype(BF16)
            return carry

        lax.fori_loop(0, ncs, chunk, 0)

    return pl.pallas_call(
        body, name=name, grid=(s // t,),
        out_shape=(jax.ShapeDtypeStruct((s, D_MODEL), BF16), jax.ShapeDtypeStruct((s, D_MODEL), F32),
                   jax.ShapeDtypeStruct((HG_HEADS, s // HG_CHUNK, HG_DK, HG_DK), F32)),
        in_specs=[pl.BlockSpec((t, W_A), lambda i: (i, 0)), _resident((2, D_MODEL)), _resident((1, D_MODEL))],
        out_specs=(pl.BlockSpec((t, D_MODEL), lambda i: (i, 0)), pl.BlockSpec((t, D_MODEL), lambda i: (i, 0)),
                   pl.BlockSpec((HG_HEADS, ncs, HG_DK, HG_DK), lambda i: (0, i, 0, 0))),
        scratch_shapes=[pltpu.VMEM((HG_HEADS, HG_DK, HG_DK), F32)],
        compiler_params=_params(("arbitrary",)),
    )(za, lb_logits, gain)


def _hgrn_bwd(za, oraw, do_a, states, lb_logits, gain, *, name):
    s = za.shape[0]
    t = min(256, s)
    ncs = t // HG_CHUNK
    nt = s // t

    def body(z_ref, oraw_ref, do_ref, st_ref, lbl_ref, gain_ref, dz_ref, stats_ref, dstate):
        step = pl.program_id(0)

        @pl.when(step == 0)
        def _():
            dstate[...] = jnp.zeros_like(dstate)
            stats_ref[...] = jnp.zeros_like(stats_ref)

        lb_all = _lower_bound(lbl_ref)
        row = lax.broadcasted_iota(jnp.int32, (HG_CHUNK, HG_CHUNK), 0)
        col = lax.broadcasted_iota(jnp.int32, (HG_CHUNK, HG_CHUNK), 1)
        tril = row >= col
        tril_f = tril.astype(F32)
        triu_f = (row <= col).astype(F32)
        gain_all = gain_ref[...]

        def chunk(ii, carry):
            i = ncs - 1 - ii
            r = pl.ds(pl.multiple_of(i * HG_CHUNK, HG_CHUNK), HG_CHUNK)
            q, fl, v, hg = (z_ref[r, j * D_MODEL:(j + 1) * D_MODEL] for j in range(4))
            o = oraw_ref[r, :]
            doa = do_ref[r, :]
            rms = lax.rsqrt(_head_mean(o * o) + RMS_EPS)
            n = o * rms
            sgg = jax.nn.sigmoid(hg)
            silu = hg * sgg
            dhg = doa * n * gain_all * (sgg * (1.0 + hg * (1.0 - sgg)))
            dgain = jnp.sum(doa * n * silu, axis=0, keepdims=True)
            dn = doa * gain_all * silu
            do = rms * (dn - n * _head_mean(dn * n))
            sg, f, k, b_last, eb, enb, eo, q_in, k_in, k_out = _chunk_forward(q, fl, v, lb_all, tril_f)
            q_in_b, k_in_b, k_out_b, vb, dob = (u.astype(BF16) for u in (q_in, k_in, k_out, v, do))
            decay = jnp.exp(b_last)
            sts = [st_ref[h, i] for h in range(HG_HEADS)]
            dsts = [dstate[h] for h in range(HG_HEADS)]
            dsts_b = [d.astype(BF16) for d in dsts]
            heads = list(enumerate(HEAD_COLS))
            attn = [_dot(q_in_b[:, c], k_in_b[:, c], NT) for h, c in heads]
            dattn = [_dot(dob[:, c], vb[:, c], NT) for h, c in heads]
            dq_st = [_dot(dob[:, c], sts[h].astype(BF16), NN) for h, c in heads]
            dk_out = [_dot(vb[:, c], dsts_b[h], NN) for h, c in heads]
            dv_st = [_dot(k_out_b[:, c], dsts_b[h], NT) for h, c in heads]
            dst_o = [_dot(dob[:, c], q_in_b[:, c], TN) for h, c in heads]
            attn = [jnp.where(tril, a, 0.0).astype(BF16) for a in attn]
            dattn = [jnp.where(tril, a, 0.0).astype(BF16) for a in dattn]
            dq_in = jnp.concatenate([_dot(dattn[h], k_in_b[:, c], NN) + dq_st[h] for h, c in heads], axis=1)
            dk_in = jnp.concatenate([_dot(dattn[h], q_in_b[:, c], TN) for h, c in heads], axis=1)
            dv = jnp.concatenate([_dot(attn[h], dob[:, c], TN) + dv_st[h] for h, c in heads], axis=1)
            dk_out = jnp.concatenate(dk_out, axis=1)
            dst_st = jnp.concatenate([jnp.sum(dsts[h] * sts[h], axis=0, keepdims=True) for h in range(HG_HEADS)], axis=1)
            for h, c in heads:
                dstate[h] = dsts[h] * decay[:, c] + dst_o[h]
            db_last = decay * dst_st + jnp.sum(dk_out * k_out, axis=0, keepdims=True)
            db = dq_in * q_in - dk_in * k_in - dk_out * k_out
            dg = _dot(triu_f, db, NN, HIGHEST) + db_last
            dk = dk_in * enb + dk_out * eo
            df = dg / f - dk
            stats_ref[0:1, :] += dgain
            stats_ref[1:2, :] += jnp.sum(df * (1.0 - sg), axis=0, keepdims=True)
            dz_ref[r, 0:1024] = (dq_in * eb).astype(BF16)
            dz_ref[r, 1024:2048] = (df * (1.0 - lb_all) * sg * (1.0 - sg)).astype(BF16)
            dz_ref[r, 2048:3072] = dv.astype(BF16)
            dz_ref[r, 3072:4096] = dhg.astype(BF16)
            return carry

        lax.fori_loop(0, ncs, chunk, 0)

        @pl.when(step == nt - 1)
        def _():
            dl0 = stats_ref[1:2, :] * lb_all * (1.0 - lb_all)
            stats_ref[1:2, :] = dl0
            stats_ref[2:3, :] = -dl0

    rev = lambda i: (nt - 1 - i, 0)
    return pl.pallas_call(
        body, name=name, grid=(nt,),
        out_shape=(jax.ShapeDtypeStruct((s, W_A), BF16), jax.ShapeDtypeStruct((8, D_MODEL), F32)),
        in_specs=[pl.BlockSpec((t, W_A), rev), pl.BlockSpec((t, D_MODEL), rev), pl.BlockSpec((t, D_MODEL), rev),
                  pl.BlockSpec((HG_HEADS, ncs, HG_DK, HG_DK), lambda i: (0, nt - 1 - i, 0, 0)),
                  _resident((2, D_MODEL)), _resident((1, D_MODEL))],
        out_specs=(pl.BlockSpec((t, W_A), rev), pl.BlockSpec((8, D_MODEL), lambda i: (0, 0))),
        scratch_shapes=[pltpu.VMEM((HG_HEADS, HG_DK, HG_DK), F32)],
        compiler_params=_params(("arbitrary",)),
    )(za, oraw, do_a, states, lb_logits, gain)


def _t5_bucket(n):
    max_exact = NUM_BUCKETS // 2
    nf = jnp.maximum(n, 1).astype(F32)
    large = max_exact + (jnp.log(nf / max_exact) / math.log(MAX_DISTANCE / max_exact) * (NUM_BUCKETS - max_exact)).astype(jnp.int32)
    large = jnp.minimum(large, NUM_BUCKETS - 1)
    return jnp.where(n < max_exact, n, large)


def _bias_selector():
    qi = jnp.arange(SWA_BLOCK)[:, None] + SWA_BLOCK
    kj = jnp.arange(2 * SWA_BLOCK)[None, :]
    dist = qi - kj
    band = ((dist >= 0) & (dist < SWA_WINDOW)).reshape(1, -1)
    bucket = _t5_bucket(jnp.clip(dist, 0, SWA_WINDOW - 1)).reshape(1, -1)
    onehot = ((bucket == jnp.arange(NUM_BUCKETS)[:, None]) & band).astype(F32)
    return onehot, jnp.where(band, 0.0, MASK_VALUE).astype(F32)


def _bias_table(rel_bias_t, onehot, maskrow, *, name):
    def body(rb_ref, oh_ref, mask_ref, o_ref):
        o_ref[...] = _dot(rb_ref[...], oh_ref[...], NN, HIGHEST) + mask_ref[...]

    return pl.pallas_call(body, name=name, out_shape=jax.ShapeDtypeStruct((SWA_HEADS, onehot.shape[1]), F32),
                          compiler_params=_params())(rel_bias_t, onehot, maskrow)


def _bias_grad(dbias2d, onehot, *, name):
    def body(db_ref, oh_ref, o_ref):
        o_ref[...] = _dot(db_ref[...], oh_ref[...], NT, HIGHEST)

    return pl.pallas_call(body, name=name, out_shape=jax.ShapeDtypeStruct((SWA_HEADS, NUM_BUCKETS), F32),
                          compiler_params=_params())(dbias2d, onehot)


GROUP_LANES = SWA_GROUP * SWA_BLOCK


def _swa_operands(zq_ref, kv_cur_ref, kv_prev_ref):
    q = (zq_ref[:, 0:1024] * (SWA_HEAD_DIM ** -0.5)).astype(BF16)
    kv_c = kv_cur_ref[...].astype(BF16)
    kv_p = kv_prev_ref[...].astype(BF16)
    kks = [jnp.concatenate([kv_p[:, g * 64:(g + 1) * 64], kv_c[:, g * 64:(g + 1) * 64]], axis=0) for g in range(SWA_KV_HEADS)]
    vvs = [jnp.concatenate([kv_p[:, 128 + g * 64:128 + (g + 1) * 64], kv_c[:, 128 + g * 64:128 + (g + 1) * 64]], axis=0)
           for g in range(SWA_KV_HEADS)]
    return q, kks, vvs


def _stack_heads(x, g):
    return jnp.concatenate([x[:, h * SWA_HEAD_DIM:(h + 1) * SWA_HEAD_DIM] for h in range(g * SWA_GROUP, (g + 1) * SWA_GROUP)], axis=0)


def _heads_to_lanes(xt):
    pairs = []
    for j in range(0, SWA_GROUP, 2):
        two = jnp.concatenate([xt[:, j * SWA_BLOCK:(j + 1) * SWA_BLOCK], xt[:, (j + 1) * SWA_BLOCK:(j + 2) * SWA_BLOCK]], axis=0)
        pairs.append(two.T)
    return jnp.concatenate(pairs, axis=1)


def _swa_softmax(score_t, bias_ref, sink_ref, g):
    lanes = slice(g * GROUP_LANES, (g + 1) * GROUP_LANES)
    sc = score_t + bias_ref[:, lanes]
    sink = sink_ref[:, lanes]
    m = jnp.maximum(jnp.max(sc, axis=0, keepdims=True), sink)
    e = jnp.exp(sc - m)
    e_sink = jnp.exp(sink - m)
    return e, 1.0 / (jnp.sum(e, axis=0, keepdims=True) + e_sink), e_sink


def _swa_tables(bias2d, sinks):
    bias_t = bias2d.reshape(SWA_HEADS, SWA_BLOCK, 2 * SWA_BLOCK).transpose(2, 0, 1).reshape(2 * SWA_BLOCK, SWA_HEADS * SWA_BLOCK)
    first = jnp.where(jnp.arange(2 * SWA_BLOCK)[:, None] < SWA_BLOCK, MASK_VALUE, bias_t)
    return jnp.stack([first, bias_t]), jnp.repeat(sinks, SWA_BLOCK, axis=1)


def _swa_fwd(zb, bias_tables, sink_lanes, *, name):
    s = zb.shape[0]
    nb = s // SWA_BLOCK

    def body(zq_ref, kvc_ref, kvp_ref, bias_ref, sink_ref, o_ref):
        q, kks, vvs = _swa_operands(zq_ref, kvc_ref, kvp_ref)
        groups = range(SWA_KV_HEADS)
        scores = [_dot(kks[g], _stack_heads(q, g), NT) for g in groups]
        probs = []
        for g in groups:
            e, inv, _ = _swa_softmax(scores[g], bias_ref, sink_ref, g)
            probs.append((e * inv).astype(BF16))
        outs = [_dot(vvs[g], probs[g], TN) for g in groups]
        o_ref[...] = jnp.concatenate([_heads_to_lanes(outs[g]) for g in groups], axis=1).astype(BF16)

    return pl.pallas_call(
        body, name=name, grid=(nb,), out_shape=jax.ShapeDtypeStruct((s, D_MODEL), BF16),
        in_specs=[pl.BlockSpec((SWA_BLOCK, W_B), lambda n: (n, 0)),
                  pl.BlockSpec((SWA_BLOCK, 256), lambda n: (n, 4)),
                  pl.BlockSpec((SWA_BLOCK, 256), lambda n: (jnp.maximum(n - 1, 0), 4)),
                  pl.BlockSpec((None, 2 * SWA_BLOCK, SWA_HEADS * SWA_BLOCK), lambda n: (jnp.minimum(n, 1), 0, 0)),
                  _resident((1, SWA_HEADS * SWA_BLOCK))],
        out_specs=pl.BlockSpec((SWA_BLOCK, D_MODEL), lambda n: (n, 0)),
        compiler_params=_params(("arbitrary",)),
    )(zb, zb, zb, bias_tables, sink_lanes)


def _swa_bwd(zb, do_b, bias_tables, sink_lanes, *, name):
    s = zb.shape[0]
    nb = s // SWA_BLOCK
    scale = SWA_HEAD_DIM ** -0.5

    def body(zq_ref, kvc_ref, kvp_ref, do_ref, bias_ref, sink_ref, dz_ref, dbias_ref, dsink_ref, carry, dsink_acc):
        step = pl.program_id(0)

        @pl.when(step == 0)
        def _():
            carry[...] = jnp.zeros_like(carry)
            dsink_acc[...] = jnp.zeros_like(dsink_acc)
            dbias_ref[...] = jnp.zeros_like(dbias_ref)

        q, kks, vvs = _swa_operands(zq_ref, kvc_ref, kvp_ref)
        groups = range(SWA_KV_HEADS)
        do = do_ref[...].astype(BF16)
        q_rows = [_stack_heads(q, g) for g in groups]
        do_rows = [_stack_heads(do, g) for g in groups]
        scores = [_dot(kks[g], q_rows[g], NT) for g in groups]
        dps = [_dot(vvs[g], do_rows[g], NT) for g in groups]
        ps, dss = [], []
        for g in groups:
            lanes = slice(g * GROUP_LANES, (g + 1) * GROUP_LANES)
            e, inv, e_sink = _swa_softmax(scores[g], bias_ref, sink_ref, g)
            p = e * inv
            delta = jnp.sum(p * dps[g], axis=0, keepdims=True)
            ds = p * (dps[g] - delta)
            dbias_ref[:, lanes] += ds
            dsink_acc[:, lanes] -= e_sink * inv * delta
            ps.append(p.astype(BF16))
            dss.append(ds.astype(BF16))
        dqs = [_dot(kks[g], dss[g], TN) * scale for g in groups]
        dkks = [_dot(dss[g], q_rows[g], NN) for g in groups]
        dvvs = [_dot(ps[g], do_rows[g], NN) for g in groups]
        dkv = jnp.concatenate(dkks + dvvs, axis=1)
        dz_ref[:, 0:1024] = jnp.concatenate([_heads_to_lanes(dqs[g]) for g in groups], axis=1).astype(BF16)
        dz_ref[:, 1024:1280] = (dkv[SWA_BLOCK:, :] + carry[...]).astype(BF16)
        carry[...] = dkv[:SWA_BLOCK, :]

        @pl.when(step == nb - 1)
        def _():
            acc = dsink_acc[...]
            dsink_ref[...] = jnp.concatenate([jnp.sum(acc[:, h * SWA_BLOCK:(h + 1) * SWA_BLOCK], axis=1, keepdims=True)
                                              for h in range(SWA_HEADS)], axis=1)

    rev = lambda i: (nb - 1 - i, 0)
    table_shape = (2 * SWA_BLOCK, SWA_HEADS * SWA_BLOCK)
    return pl.pallas_call(
        body, name=name, grid=(nb,),
        out_shape=(jax.ShapeDtypeStruct((s, W_B), BF16), jax.ShapeDtypeStruct(table_shape, F32), jax.ShapeDtypeStruct((1, SWA_HEADS), F32)),
        in_specs=[pl.BlockSpec((SWA_BLOCK, W_B), rev),
                  pl.BlockSpec((SWA_BLOCK, 256), lambda i: (nb - 1 - i, 4)),
                  pl.BlockSpec((SWA_BLOCK, 256), lambda i: (jnp.maximum(nb - 2 - i, 0), 4)),
                  pl.BlockSpec((SWA_BLOCK, D_MODEL), rev),
                  pl.BlockSpec((None,) + table_shape, lambda i: (jnp.minimum(nb - 1 - i, 1), 0, 0)),
                  _resident((1, SWA_HEADS * SWA_BLOCK))],
        out_specs=(pl.BlockSpec((SWA_BLOCK, W_B), rev), pl.BlockSpec(table_shape, lambda i: (0, 0)),
                   pl.BlockSpec((1, SWA_HEADS), lambda i: (0, 0))),
        scratch_shapes=[pltpu.VMEM((SWA_BLOCK, 256), F32), pltpu.VMEM((1, SWA_HEADS * SWA_BLOCK), F32)],
        compiler_params=_params(("arbitrary",)),
    )(zb, zb, zb, do_b, bias_tables, sink_lanes)


def _mem_probs(zc_ref, mkv_ref, h):
    cols = slice(h * MEM_HEAD_DIM, (h + 1) * MEM_HEAD_DIM)
    qh = (zc_ref[:, cols] * (MEM_HEAD_DIM ** -0.5)).astype(BF16)
    sc = _dot(qh, mkv_ref[:, cols], NT)
    e = jnp.exp(sc - jnp.max(sc, axis=-1, keepdims=True))
    return qh, e / jnp.sum(e, axis=-1, keepdims=True)


def _mem_fwd(zc, mkv, *, name):
    s = zc.shape[0]
    t = min(512, s)

    def body(zc_ref, mkv_ref, o_ref):
        for h in range(MEM_HEADS):
            _, p = _mem_probs(zc_ref, mkv_ref, h)
            vh = mkv_ref[:, D_MODEL + h * MEM_HEAD_DIM:D_MODEL + (h + 1) * MEM_HEAD_DIM]
            o_ref[:, h * MEM_HEAD_DIM:(h + 1) * MEM_HEAD_DIM] = _dot(p.astype(BF16), vh, NN).astype(BF16)

    return pl.pallas_call(
        body, name=name, grid=(s // t,), out_shape=jax.ShapeDtypeStruct((s, D_MODEL), BF16),
        in_specs=[pl.BlockSpec((t, D_MODEL), lambda i: (i, 0)), _resident((MEM_LEN, 2 * D_MODEL))],
        out_specs=pl.BlockSpec((t, D_MODEL), lambda i: (i, 0)), compiler_params=_params(("parallel",)),
    )(zc, mkv)


def _mem_bwd(zc, do_c, mkv, *, name):
    s = zc.shape[0]
    t = min(512, s)

    def body(zc_ref, do_ref, mkv_ref, dz_ref, dmkv_ref):
        @pl.when(pl.program_id(0) == 0)
        def _():
            dmkv_ref[...] = jnp.zeros_like(dmkv_ref)

        for h in range(MEM_HEADS):
            cols = slice(h * MEM_HEAD_DIM, (h + 1) * MEM_HEAD_DIM)
            vcols = slice(D_MODEL + h * MEM_HEAD_DIM, D_MODEL + (h + 1) * MEM_HEAD_DIM)
            qh, p = _mem_probs(zc_ref, mkv_ref, h)
            doh = do_ref[:, cols].astype(BF16)
            dp = _dot(doh, mkv_ref[:, vcols], NT)
            ds = (p * (dp - jnp.sum(p * dp, axis=-1, keepdims=True))).astype(BF16)
            dz_ref[:, cols] = (_dot(ds, mkv_ref[:, cols], NN) * (MEM_HEAD_DIM ** -0.5)).astype(BF16)
            dmkv_ref[:, cols] += _dot(ds, qh, TN)
            dmkv_ref[:, vcols] += _dot(p.astype(BF16), doh, TN)

    return pl.pallas_call(
        body, name=name, grid=(s // t,),
        out_shape=(jax.ShapeDtypeStruct((s, D_MODEL), BF16), jax.ShapeDtypeStruct((MEM_LEN, 2 * D_MODEL), F32)),
        in_specs=[pl.BlockSpec((t, D_MODEL), lambda i: (i, 0)), pl.BlockSpec((t, D_MODEL), lambda i: (i, 0)),
                  _resident((MEM_LEN, 2 * D_MODEL))],
        out_specs=(pl.BlockSpec((t, D_MODEL), lambda i: (i, 0)), pl.BlockSpec((MEM_LEN, 2 * D_MODEL), lambda i: (0, 0))),
        compiler_params=_params(("arbitrary",)),
    )(zc, do_c, mkv)


def _normalize(pre):
    mu = jnp.mean(pre, axis=-1, keepdims=True)
    xc = pre - mu
    rstd = lax.rsqrt(jnp.mean(xc * xc, axis=-1, keepdims=True) + LN_EPS)
    return xc * rstd, rstd


def _layer_norm_bwd(dh, xhat, rstd, g):
    dxh = dh * g
    dpre = rstd * (dxh - jnp.mean(dxh, axis=-1, keepdims=True) - xhat * jnp.mean(dxh * xhat, axis=-1, keepdims=True))
    return dpre, jnp.sum(dh * xhat, axis=0, keepdims=True), jnp.sum(dh, axis=0, keepdims=True)


def _merge_fwd(o_a, o_b, o_c, zd, x, wbr, wo, *, name):
    s = x.shape[0]
    t = min(256, s)
    row = lambda w, dt=None: pl.BlockSpec((t, w), lambda i: (i, 0))

    def body(oa_ref, ob_ref, oc_ref, zd_ref, x_ref, wbr_ref, wo_ref, xhat_ref, rstd_ref, merged_ref, pa_ref, pb_ref, pc_ref):
        merged = jnp.zeros((t, D_MODEL), F32)
        for b, (o_ref, p_ref) in enumerate(((oa_ref, pa_ref), (ob_ref, pb_ref), (oc_ref, pc_ref))):
            p = _dot(o_ref[...], wbr_ref[b], NN)
            p_ref[...] = p
            merged = merged + jax.nn.sigmoid(zd_ref[:, b * D_MODEL:(b + 1) * D_MODEL]) * p
        merged_b = merged.astype(BF16)
        merged_ref[...] = merged_b
        xhat, rstd = _normalize(ALPHA * x_ref[...] + _dot(merged_b, wo_ref[...], NN))
        xhat_ref[...] = xhat
        rstd_ref[...] = rstd

    act = jax.ShapeDtypeStruct((s, D_MODEL), F32)
    return pl.pallas_call(
        body, name=name, grid=(s // t,),
        out_shape=(act, jax.ShapeDtypeStruct((s, 1), F32), jax.ShapeDtypeStruct((s, D_MODEL), BF16), act, act, act),
        in_specs=[row(D_MODEL), row(D_MODEL), row(D_MODEL), row(W_D), row(D_MODEL),
                  _resident((3, D_MODEL, D_MODEL)), _resident((D_MODEL, D_MODEL))],
        out_specs=(row(D_MODEL), row(1), row(D_MODEL), row(D_MODEL), row(D_MODEL), row(D_MODEL)),
        compiler_params=_params(("parallel",)),
    )(o_a, o_b, o_c, zd, x, wbr, wo)


def _merge_bwd(dpre1, zd, pa, pb, pc, wbr, wo, *, name):
    s = dpre1.shape[0]
    t = min(256, s)
    row = lambda w: pl.BlockSpec((t, w), lambda i: (i, 0))

    def body(dpre_ref, zd_ref, pa_ref, pb_ref, pc_ref, wbr_ref, wo_ref, dzd_ref, dpa_ref, dpb_ref, dpc_ref, doa_ref, dob_ref, doc_ref):
        dmerged = _dot(dpre_ref[...].astype(BF16), wo_ref[...], NT)
        branches = ((pa_ref, dpa_ref, doa_ref), (pb_ref, dpb_ref, dob_ref), (pc_ref, dpc_ref, doc_ref))
        for b, (p_ref, dp_ref, do_ref) in enumerate(branches):
            gate = jax.nn.sigmoid(zd_ref[:, b * D_MODEL:(b + 1) * D_MODEL])
            dzd_ref[:, b * D_MODEL:(b + 1) * D_MODEL] = (dmerged * p_ref[...] * gate * (1.0 - gate)).astype(BF16)
            dp = (dmerged * gate).astype(BF16)
            dp_ref[...] = dp
            do_ref[...] = _dot(dp, wbr_ref[b], NT)

    act = jax.ShapeDtypeStruct((s, D_MODEL), F32)
    actb = jax.ShapeDtypeStruct((s, D_MODEL), BF16)
    return pl.pallas_call(
        body, name=name, grid=(s // t,),
        out_shape=(jax.ShapeDtypeStruct((s, W_D), BF16), actb, actb, actb, act, act, act),
        in_specs=[row(D_MODEL), row(W_D), row(D_MODEL), row(D_MODEL), row(D_MODEL),
                  _resident((3, D_MODEL, D_MODEL)), _resident((D_MODEL, D_MODEL))],
        out_specs=(row(W_D),) + (row(D_MODEL),) * 6,
        compiler_params=_params(("parallel",)),
    )(dpre1, zd, pa, pb, pc, wbr, wo)


def _mlp_loss(xhat1, rstd1, target, ln1_g, ln1_b, ln2_g, ln2_b, wu, wd, *, name):
    s = xhat1.shape[0]
    t = min(256, s)
    npan = wu.shape[0]
    row = lambda w: pl.BlockSpec((t, w), lambda i: (i, 0))
    vec = _resident((1, D_MODEL))

    def body(xhat_ref, rstd_ref, tgt_ref, g1_ref, b1_ref, g2_ref, b2_ref, wu_ref, wd_ref,
             dpre1_ref, dpre2_ref, h1_ref, a_ref, du_ref, stats_ref):
        @pl.when(pl.program_id(0) == 0)
        def _():
            stats_ref[...] = jnp.zeros_like(stats_ref)

        xhat1_v = xhat_ref[...]
        h1 = xhat1_v * g1_ref[...] + b1_ref[...]
        h1_b = h1.astype(BF16)
        h1_ref[...] = h1_b
        us = []
        ff = jnp.zeros((t, D_MODEL), F32)
        for j in range(npan):
            u = _dot(h1_b, wu_ref[j], NN)
            us.append(u)
            r = jnp.maximum(u, 0.0)
            a_b = (r * r).astype(BF16)
            a_ref[:, j * D_MODEL:(j + 1) * D_MODEL] = a_b
            ff = ff + _dot(a_b, wd_ref[j], NN)
        xhat2, rstd2 = _normalize(ALPHA * h1 + ff)
        err = xhat2 * g2_ref[...] + b2_ref[...] - tgt_ref[...]
        stats_ref[4:5, :] += jnp.sum(err * err, axis=0, keepdims=True)
        dpre2, dg2, db2 = _layer_norm_bwd(err * (1.0 / D_MODEL), xhat2, rstd2, g2_ref[...])
        stats_ref[0:1, :] += dg2
        stats_ref[1:2, :] += db2
        dpre2_b = dpre2.astype(BF16)
        dpre2_ref[...] = dpre2_b
        dh1 = ALPHA * dpre2
        for j in range(npan):
            du_b = (_dot(dpre2_b, wd_ref[j], NT) * (2.0 * jnp.maximum(us[j], 0.0))).astype(BF16)
            du_ref[:, j * D_MODEL:(j + 1) * D_MODEL] = du_b
            dh1 = dh1 + _dot(du_b, wu_ref[j], NT)
        dpre1, dg1, db1 = _layer_norm_bwd(dh1, xhat1_v, rstd_ref[...], g1_ref[...])
        stats_ref[2:3, :] += dg1
        stats_ref[3:4, :] += db1
        dpre1_ref[...] = dpre1

    actb = jax.ShapeDtypeStruct((s, D_MODEL), BF16)
    wide = jax.ShapeDtypeStruct((s, D_FF), BF16)
    return pl.pallas_call(
        body, name=name, grid=(s // t,),
        out_shape=(jax.ShapeDtypeStruct((s, D_MODEL), F32), actb, actb, wide, wide, jax.ShapeDtypeStruct((8, D_MODEL), F32)),
        in_specs=[row(D_MODEL), row(1), row(D_MODEL), vec, vec, vec, vec,
                  _resident((npan, D_MODEL, D_MODEL)), _resident((npan, D_MODEL, D_MODEL))],
        out_specs=(row(D_MODEL), row(D_MODEL), row(D_MODEL), row(D_FF), row(D_FF), pl.BlockSpec((8, D_MODEL), lambda i: (0, 0))),
        compiler_params=_params(("arbitrary",)),
    )(xhat1, rstd1, target, ln1_g, ln1_b, ln2_g, ln2_b, wu, wd)


def _local_step(x, mem, target, wi_parts, wmkv, wbr, wo, wu, wd, lb_logits, gain, sinks, rel_bias, ln1_g, ln1_b, ln2_g, ln2_b):
    s = x.shape[0]
    tm = min(1024, s)
    tk = min(2048, s)
    xb = x.astype(BF16)
    memb = mem.astype(BF16)
    wia, wib, wic, wid = wi_parts

    za = _mm(xb, wia, mode="nn", tm=min(512, s), tn=W_A, tk=D_MODEL, name="proj_a")
    zb = _mm(xb, wib, mode="nn", tm=tm, tn=W_B, tk=D_MODEL, name="proj_b")
    zc = _mm(xb, wic, mode="nn", tm=tm, tn=W_C, tk=D_MODEL, name="proj_c")
    zd = _mm(xb, wid, mode="nn", tm=min(512, s), tn=W_D, tk=D_MODEL, name="proj_d")
    mkv = _mm(memb, wmkv, mode="nn", tm=MEM_LEN, tn=512, tk=D_MODEL, name="mem_kv", out_dtype=BF16, b_panels=True)
    onehot, maskrow = _bias_selector()
    bias_tables, sink_lanes = _swa_tables(_bias_table(rel_bias.T, onehot, maskrow, name="bias_table"), sinks)
    o_a, o_raw, states = _hgrn_fwd(za, lb_logits, gain, name="hgrn_fwd")
    o_b = _swa_fwd(zb, bias_tables, sink_lanes, name="swa_fwd")
    o_c = _mem_fwd(zc, mkv, name="mem_fwd")
    xhat1, rstd1, merged, pa, pb, pc = _merge_fwd(o_a, o_b, o_c, zd, x, wbr, wo, name="merge_fwd")

    dpre1, dpre2, h1, act, du, ln_stats = _mlp_loss(xhat1, rstd1, target, ln1_g, ln1_b, ln2_g, ln2_b, wu, wd, name="mlp_loss")
    g_wd = _mm(act, dpre2, mode="tn", tm=1024, tn=D_MODEL, tk=tk, name="grad_w_down")
    g_wu = _mm(h1, du, mode="tn", tm=D_MODEL, tn=1024, tk=tk, name="grad_w_up", out_panels=True)

    dzd, dpa, dpb, dpc, do_a, do_b, do_c = _merge_bwd(dpre1, zd, pa, pb, pc, wbr, wo, name="merge_bwd")
    g_wo = _mm(merged, dpre1, mode="tn", tm=D_MODEL, tn=D_MODEL, tk=tk, name="grad_w_out")
    g_wbh = _mm(o_a, dpa, mode="tn", tm=D_MODEL, tn=D_MODEL, tk=tk, name="grad_w_branch_hg")
    g_wbs = _mm(o_b, dpb, mode="tn", tm=D_MODEL, tn=D_MODEL, tk=tk, name="grad_w_branch_swa")
    g_wbm = _mm(o_c, dpc, mode="tn", tm=D_MODEL, tn=D_MODEL, tk=tk, name="grad_w_branch_mem")
    dza, hg_stats = _hgrn_bwd(za, o_raw, do_a, states, lb_logits, gain, name="hgrn_bwd")
    dzb, dbias_t, dsinks = _swa_bwd(zb, do_b, bias_tables, sink_lanes, name="swa_bwd")
    dbias = dbias_t.reshape(2 * SWA_BLOCK, SWA_HEADS, SWA_BLOCK).transpose(1, 2, 0).reshape(SWA_HEADS, -1)
    d_rel_bias = _bias_grad(dbias, onehot, name="bias_grad").T
    dzc, dmkv = _mem_bwd(zc, do_c, mkv, name="mem_bwd")
    g_wmkv = _mm(memb, dmkv, mode="tn", tm=D_MODEL, tn=512, tk=MEM_LEN, name="grad_w_mem_kv", out_panels=True)

    g_wi = [_mm(xb, dz, mode="tn", tm=min(512, D_MODEL), tn=dz.shape[1] if dz.shape[1] <= 1280 else 1024, tk=tk, name=nm)
            for dz, nm in ((dza, "grad_w_in_a"), (dzb, "grad_w_in_b"), (dzc, "grad_w_in_c"), (dzd, "grad_w_in_d"))]
    grad_x = _dx_matmul([dza, dzb, dzc, dzd], [wia, wib, wic, wid], dpre1, tm=min(512, s), tks=[1024, W_B, 1024, 1024], name="grad_x")
    big = dict(w_in=jnp.concatenate(g_wi, axis=1), w_mem_kv=g_wmkv, w_branch_hg=g_wbh, w_branch_swa=g_wbs, w_branch_mem=g_wbm,
               w_out=g_wo, w_up=g_wu, w_down=g_wd.reshape(N_SHARDS, D_FF // N_SHARDS, D_MODEL))
    small = dict(lb_logits=hg_stats[1:3], hg_norm_gain=hg_stats[0:1], swa_sinks=dsinks, rel_bias=d_rel_bias,
                 ln1_g=ln_stats[2:3], ln1_b=ln_stats[3:4], ln2_g=ln_stats[0:1], ln2_b=ln_stats[1:2], sq_err=ln_stats[4:5])
    return grad_x, big, small


def _mesh_position():
    x, y, c = lax.axis_index("x"), lax.axis_index("y"), lax.axis_index("c")
    chips = [(1 - x, y), (x, 1 - y), (1 - x, 1 - y)]
    return x, y, c, chips


def _all_gather(shards, *, name):
    n = len(shards)
    per = 7

    def body(*refs):
        ins, outs = refs[:n], refs[n:2 * n]
        send_sems, recv_sems = refs[2 * n:]
        x, y, c, chips = _mesh_position()
        me = 2 * x + y
        sibling = (x, y, 1 - c)

        def half(a, slot, hc):
            rh = shards[a].shape[0] // 2
            return outs[a].at[slot, pl.ds(hc * rh, rh), :]

        def copy(a, k, src, dst, to):
            return pltpu.make_async_remote_copy(src_ref=src, dst_ref=dst, send_sem=send_sems.at[a * per + k], recv_sem=recv_sems.at[a * per + k],
                                                device_id=to, device_id_type=MESH)

        started = []
        for a in range(n):
            cp = copy(a, 6, ins[a], outs[a].at[me], sibling)
            cp.start()
            started.append(cp)
        for k, (cx, cy) in enumerate(chips):
            for a in range(n):
                rh = shards[a].shape[0] // 2
                cp = copy(a, k, ins[a].at[pl.ds(c * rh, rh), :], half(a, me, c), (cx, cy, c))
                cp.start()
                started.append(cp)
        for k, (cx, cy) in enumerate(chips):
            slot = 2 * cx + cy
            for a in range(n):
                copy(a, k, half(a, slot, c), half(a, slot, c), (cx, cy, c)).wait_recv()
                cp = copy(a, 3 + k, half(a, slot, c), half(a, slot, c), sibling)
                cp.start()
                started.append(cp)
        for k, (cx, cy) in enumerate(chips):
            slot = 2 * cx + cy
            for a in range(n):
                copy(a, 3 + k, half(a, slot, 1 - c), half(a, slot, 1 - c), sibling).wait_recv()
        for a in range(n):
            copy(a, 6, outs[a].at[me], outs[a].at[me], sibling).wait_recv()
        for cp in started:
            cp.wait_send()

    return pl.pallas_call(
        body, name=name, out_shape=[jax.ShapeDtypeStruct((N_SHARDS,) + w.shape, w.dtype) for w in shards],
        in_specs=[HBM] * n, out_specs=[HBM] * n,
        scratch_shapes=[pltpu.SemaphoreType.DMA((per * n,)), pltpu.SemaphoreType.DMA((per * n,))],
    )(*shards)


def _exchange_sibling_halves(grads, *, name):
    n = len(grads)

    def body(*refs):
        ins, outs = refs[:n], refs[n:2 * n]
        send_sems, recv_sems = refs[2 * n:]
        x, y, c, _ = _mesh_position()
        copies = []
        for a in range(n):
            rh = grads[a].shape[1] // 2
            cp = pltpu.make_async_remote_copy(src_ref=ins[a].at[:, pl.ds((1 - c) * rh, rh), :], dst_ref=outs[a], send_sem=send_sems.at[a],
                                              recv_sem=recv_sems.at[a], device_id=(x, y, 1 - c), device_id_type=MESH)
            cp.start()
            copies.append(cp)
        for cp in copies:
            cp.wait()

    return pl.pallas_call(
        body, name=name, out_shape=[jax.ShapeDtypeStruct((g.shape[0], g.shape[1] // 2, g.shape[2]), g.dtype) for g in grads],
        in_specs=[HBM] * n, out_specs=[HBM] * n,
        scratch_shapes=[pltpu.SemaphoreType.DMA((n,)), pltpu.SemaphoreType.DMA((n,))],
    )(*grads)


def _row_tile(rows):
    for tr in (256, 128, 64, 32, 16, 8):
        if rows % tr == 0:
            return tr
    raise ValueError(rows)


def _add_sibling(grad, other, pos, *, name):
    p, r, cols = grad.shape
    rh = r // 2
    tr = _row_tile(rh)
    nb = rh // tr

    def body(pos_ref, g_ref, o_ref, sb_ref, mine_ref):
        total = g_ref[...] + o_ref[...]
        sb_ref[...] = total.astype(BF16)

        @pl.when(pl.program_id(1) == pos_ref[0])
        def _():
            mine_ref[...] = total

    return pl.pallas_call(
        body, name=name, out_shape=(jax.ShapeDtypeStruct((p, rh, cols), BF16), jax.ShapeDtypeStruct((rh, cols), F32)),
        grid_spec=pltpu.PrefetchScalarGridSpec(
            num_scalar_prefetch=1, grid=(nb, p),
            in_specs=[pl.BlockSpec((None, tr, cols), lambda i, j, pos_ref: (j, pos_ref[1] * nb + i, 0)),
                      pl.BlockSpec((None, tr, cols), lambda i, j, pos_ref: (j, i, 0))],
            out_specs=(pl.BlockSpec((None, tr, cols), lambda i, j, pos_ref: (j, i, 0)),
                       pl.BlockSpec((tr, cols), lambda i, j, pos_ref: (i, 0)))),
        compiler_params=_params(("parallel", "arbitrary")),
    )(pos, grad, other)


def _exchange_chip_partials(sums, *, name):
    n = len(sums)

    def body(*refs):
        ins, outs = refs[:n], refs[n:2 * n]
        send_sems, recv_sems = refs[2 * n:]
        _, _, c, chips = _mesh_position()
        copies = []
        for k, (cx, cy) in enumerate(chips):
            for a in range(n):
                cp = pltpu.make_async_remote_copy(src_ref=ins[a].at[2 * cx + cy], dst_ref=outs[a].at[k], send_sem=send_sems.at[a * 3 + k],
                                                  recv_sem=recv_sems.at[a * 3 + k], device_id=(cx, cy, c), device_id_type=MESH)
                cp.start()
                copies.append(cp)
        for cp in copies:
            cp.wait()

    return pl.pallas_call(
        body, name=name, out_shape=[jax.ShapeDtypeStruct((3,) + g.shape[1:], g.dtype) for g in sums],
        in_specs=[HBM] * n, out_specs=[HBM] * n,
        scratch_shapes=[pltpu.SemaphoreType.DMA((3 * n,)), pltpu.SemaphoreType.DMA((3 * n,))],
    )(*sums)


def _add_chips(mine, others, pos, *, name):
    rh, cols = mine.shape
    tr = _row_tile(rh)
    nb = rh // tr

    def body(pos_ref, s_ref, o_ref, r_ref):
        r_ref[...] = ((s_ref[...] + o_ref[0].astype(F32)) + o_ref[1].astype(F32)) + o_ref[2].astype(F32)

    return pl.pallas_call(
        body, name=name, out_shape=jax.ShapeDtypeStruct((2 * rh, cols), F32),
        grid_spec=pltpu.PrefetchScalarGridSpec(
            num_scalar_prefetch=1, grid=(nb,),
            in_specs=[pl.BlockSpec((tr, cols), lambda i, pos_ref: (i, 0)),
                      pl.BlockSpec((3, tr, cols), lambda i, pos_ref: (0, i, 0))],
            out_specs=pl.BlockSpec((tr, cols), lambda i, pos_ref: (pos_ref[1] * nb + i, 0))),
        compiler_params=_params(("parallel",)),
    )(pos, mine, others)


def _join_halves(bufs, *, name):
    n = len(bufs)

    def body(*refs):
        ins, outs = refs[:n], refs[n:2 * n]
        send_sems, recv_sems = refs[2 * n:]
        x, y, c, _ = _mesh_position()

        def copy(a, hc):
            rh = bufs[a].shape[0] // 2
            rows = pl.ds(hc * rh, rh)
            return pltpu.make_async_remote_copy(src_ref=ins[a].at[rows, :], dst_ref=outs[a].at[rows, :], send_sem=send_sems.at[a],
                                                recv_sem=recv_sems.at[a], device_id=(x, y, 1 - c), device_id_type=MESH)

        for a in range(n):
            copy(a, c).start()
        for a in range(n):
            copy(a, c).wait_send()
            copy(a, 1 - c).wait_recv()

    return pl.pallas_call(
        body, name=name, out_shape=[jax.ShapeDtypeStruct(b.shape, b.dtype) for b in bufs],
        in_specs=[HBM] * n, out_specs=[HBM] * n, input_output_aliases={a: a for a in range(n)},
        scratch_shapes=[pltpu.SemaphoreType.DMA((n,)), pltpu.SemaphoreType.DMA((n,))],
    )(*bufs)


def _all_reduce_small(packed, *, name):
    rows, cols = packed.shape

    def body(in_ref, out_ref, gathered, send_sems, recv_sems):
        x, y, c, _ = _mesh_position()
        me = 4 * x + 2 * y + c
        gathered[me] = in_ref[...]
        copies = []
        for d in range(1, 8):
            dx, dy, dc = (d >> 2) & 1, (d >> 1) & 1, d & 1
            peer = (x ^ dx, y ^ dy, c ^ dc)
            cp = pltpu.make_async_remote_copy(src_ref=in_ref, dst_ref=gathered.at[me], send_sem=send_sems.at[d - 1], recv_sem=recv_sems.at[d - 1],
                                              device_id=peer, device_id_type=MESH)
            cp.start()
            copies.append(cp)
        for cp in copies:
            cp.wait()
        total = gathered[0]
        for j in range(1, 8):
            total = total + gathered[j]
        out_ref[...] = total

    vm = pl.BlockSpec(memory_space=pltpu.VMEM)
    return pl.pallas_call(
        body, name=name, out_shape=jax.ShapeDtypeStruct((rows, cols), F32), in_specs=[vm], out_specs=vm,
        scratch_shapes=[pltpu.VMEM((8, rows, cols), F32), pltpu.SemaphoreType.DMA((7,)), pltpu.SemaphoreType.DMA((7,))],
    )(packed)


def _adamw_math(w, g, m, v):
    m = ADAM_B1 * m + (1.0 - ADAM_B1) * g
    v = ADAM_B2 * v + (1.0 - ADAM_B2) * (g * g)
    m_hat = m / (1.0 - ADAM_B1 ** ADAM_STEP)
    v_hat = v / (1.0 - ADAM_B2 ** ADAM_STEP)
    delta = -ADAM_LR * (m_hat / (jnp.sqrt(v_hat) + ADAM_EPS) + ADAM_WD * w)
    return delta, m, v


def _adamw(w, g, m, v, *, name):
    rows, cols = w.shape
    tr = _row_tile(rows)
    blk = pl.BlockSpec((tr, cols), lambda i: (i, 0))

    def body(w_ref, g_ref, m_ref, v_ref, d_ref, nm_ref, nv_ref):
        d_ref[...], nm_ref[...], nv_ref[...] = _adamw_math(w_ref[...], g_ref[...], m_ref[...], v_ref[...])

    shape = jax.ShapeDtypeStruct((rows, cols), F32)
    return pl.pallas_call(body, name=name, grid=(rows // tr,), out_shape=(shape, shape, shape), in_specs=[blk] * 4, out_specs=(blk,) * 3,
                          compiler_params=_params(("parallel",)))(w, g, m, v)


def _adamw_small(w, g, m, v, *, name):
    def body(w_ref, g_ref, m_ref, v_ref, d_ref, nm_ref, nv_ref, loss_ref):
        d_ref[...], nm_ref[...], nv_ref[...] = _adamw_math(w_ref[...], g_ref[...], m_ref[...], v_ref[...])
        loss_ref[...] = (0.5 / D_MODEL) * jnp.sum(g_ref[8:9, :], axis=1, keepdims=True)

    shape = jax.ShapeDtypeStruct(w.shape, F32)
    return pl.pallas_call(body, name=name, out_shape=(shape, shape, shape, jax.ShapeDtypeStruct((1, 1), F32)),
                          compiler_params=_params())(w, g, m, v)


SMALL_ROWS = 16


def _pack_small(lb_logits, gain, sinks, rel_bias, ln1_g, ln1_b, ln2_g, ln2_b, extra=None):
    misc = jnp.concatenate([sinks.reshape(1, -1), rel_bias.reshape(1, -1)], axis=1)
    misc = jnp.pad(misc, ((0, 0), (0, D_MODEL - misc.shape[1])))
    rows = [lb_logits, gain, ln1_g, ln1_b, ln2_g, ln2_b, misc, extra if extra is not None else jnp.zeros((1, D_MODEL), F32)]
    used = sum(r.shape[0] for r in rows)
    return jnp.concatenate(rows + [jnp.zeros((SMALL_ROWS - used, D_MODEL), F32)], axis=0)


def _unpack_small(p):
    return dict(lb_logits=p[0:2], hg_norm_gain=p[2:3], ln1_g=p[3:4], ln1_b=p[4:5], ln2_g=p[5:6], ln2_b=p[6:7],
                swa_sinks=p[7:8, 0:SWA_HEADS], rel_bias=p[7:8, SWA_HEADS:SWA_HEADS + NUM_BUCKETS * SWA_HEADS].reshape(NUM_BUCKETS, SWA_HEADS))


WEIGHTS = ["w_in", "lb_logits", "hg_norm_gain", "swa_sinks", "rel_bias", "w_mem_kv", "w_branch_hg", "w_branch_swa", "w_branch_mem",
           "w_out", "ln1_g", "ln1_b", "w_up", "w_down", "ln2_g", "ln2_b"]
BIG = ["w_in", "w_mem_kv", "w_branch_hg", "w_branch_swa", "w_branch_mem", "w_out", "w_up", "w_down"]
SMALL = ["lb_logits", "hg_norm_gain", "swa_sinks", "rel_bias", "ln1_g", "ln1_b", "ln2_g", "ln2_b"]


def kernel(x, mem, w_in, lb_logits, hg_norm_gain, swa_sinks, rel_bias, w_mem_kv, w_branch_hg, w_branch_swa, w_branch_mem, w_out, ln1_g, ln1_b, w_up, w_down, ln2_g, ln2_b, loss_target, m_w_in, m_lb_logits, m_hg_norm_gain, m_swa_sinks, m_rel_bias, m_w_mem_kv, m_w_branch_hg, m_w_branch_swa, m_w_branch_mem, m_w_out, m_ln1_g, m_ln1_b, m_w_up, m_w_down, m_ln2_g, m_ln2_b, v_w_in, v_lb_logits, v_hg_norm_gain, v_swa_sinks, v_rel_bias, v_w_mem_kv, v_w_branch_hg, v_w_branch_swa, v_w_branch_mem, v_w_out, v_ln1_g, v_ln1_b, v_w_up, v_w_down, v_ln2_g, v_ln2_b):
    w = dict(w_in=w_in, lb_logits=lb_logits, hg_norm_gain=hg_norm_gain, swa_sinks=swa_sinks, rel_bias=rel_bias, w_mem_kv=w_mem_kv,
             w_branch_hg=w_branch_hg, w_branch_swa=w_branch_swa, w_branch_mem=w_branch_mem, w_out=w_out, ln1_g=ln1_g, ln1_b=ln1_b,
             w_up=w_up, w_down=w_down, ln2_g=ln2_g, ln2_b=ln2_b)
    m = dict(w_in=m_w_in, lb_logits=m_lb_logits, hg_norm_gain=m_hg_norm_gain, swa_sinks=m_swa_sinks, rel_bias=m_rel_bias, w_mem_kv=m_w_mem_kv,
             w_branch_hg=m_w_branch_hg, w_branch_swa=m_w_branch_swa, w_branch_mem=m_w_branch_mem, w_out=m_w_out, ln1_g=m_ln1_g, ln1_b=m_ln1_b,
             w_up=m_w_up, w_down=m_w_down, ln2_g=m_ln2_g, ln2_b=m_ln2_b)
    v = dict(w_in=v_w_in, lb_logits=v_lb_logits, hg_norm_gain=v_hg_norm_gain, swa_sinks=v_swa_sinks, rel_bias=v_rel_bias, w_mem_kv=v_w_mem_kv,
             w_branch_hg=v_w_branch_hg, w_branch_swa=v_w_branch_swa, w_branch_mem=v_w_branch_mem, w_out=v_w_out, ln1_g=v_ln1_g, ln1_b=v_ln1_b,
             w_up=v_w_up, w_down=v_w_down, ln2_g=v_ln2_g, ln2_b=v_ln2_b)
    shapes = {k: w[k].shape for k in WEIGHTS}
    for d in (w, m, v):
        for k in BIG:
            d[k] = d[k].reshape(d[k].shape[-2], d[k].shape[-1])

    gathered = _all_gather([w[k].astype(BF16) for k in BIG], name="gather_weights")
    full = dict(zip(BIG, gathered))
    wi = full["w_in"].transpose(1, 0, 2).reshape(D_MODEL, IN_COLS)
    wi_parts = (wi[:, 0:W_A], wi[:, W_A:W_A + W_B], wi[:, W_A + W_B:W_A + W_B + W_C], wi[:, W_A + W_B + W_C:])
    wbr = jnp.stack([full[k].reshape(D_MODEL, D_MODEL) for k in ("w_branch_hg", "w_branch_swa", "w_branch_mem")])
    wo = full["w_out"].reshape(D_MODEL, D_MODEL)

    grad_x, big, small = _local_step(
        x.reshape(x.shape[-2], D_MODEL), mem.reshape(MEM_LEN, D_MODEL), loss_target.reshape(loss_target.shape[-2], D_MODEL),
        wi_parts, full["w_mem_kv"], wbr, wo, full["w_up"], full["w_down"],
        lb_logits, hg_norm_gain, swa_sinks, rel_bias, ln1_g, ln1_b, ln2_g, ln2_b)

    big["w_in"] = big["w_in"].reshape(D_MODEL, N_SHARDS, IN_COLS // N_SHARDS).transpose(1, 0, 2)
    for k in ("w_branch_hg", "w_branch_swa", "w_branch_mem", "w_out"):
        big[k] = big[k].reshape(N_SHARDS, D_MODEL // N_SHARDS, D_MODEL)
    grads = [big[k] for k in BIG]
    cx, cy, cc = lax.axis_index("x"), lax.axis_index("y"), lax.axis_index("c")
    pos = jnp.stack([2 * cx + cy, cc]).astype(jnp.int32)
    from_sibling = _exchange_sibling_halves(grads, name="reduce_sibling")
    chip_sums = [_add_sibling(g, o, pos, name="add_sibling_" + k) for g, o, k in zip(grads, from_sibling, BIG)]
    from_chips = _exchange_chip_partials([s_[0] for s_ in chip_sums], name="reduce_chips")
    halves = [_add_chips(s_[1], o, pos, name="add_chips_" + k) for s_, o, k in zip(chip_sums, from_chips, BIG)]
    reduced = dict(zip(BIG, _join_halves(halves, name="join_halves")))

    packed_g = _pack_small(small["lb_logits"], small["hg_norm_gain"], small["swa_sinks"], small["rel_bias"], small["ln1_g"], small["ln1_b"],
                           small["ln2_g"], small["ln2_b"], extra=small["sq_err"])
    packed_g = _all_reduce_small(packed_g, name="reduce_small")

    grad_out, delta_out, m_out, v_out = {}, {}, {}, {}
    for k in BIG:
        d_, m_, v_ = _adamw(w[k], reduced[k], m[k], v[k], name="adamw_" + k)
        grad_out[k], delta_out[k], m_out[k], v_out[k] = reduced[k], d_, m_, v_
    pack = lambda d: _pack_small(d["lb_logits"], d["hg_norm_gain"], d["swa_sinks"], d["rel_bias"], d["ln1_g"], d["ln1_b"], d["ln2_g"], d["ln2_b"])
    d_s, m_s, v_s, loss = _adamw_small(pack(w), packed_g, pack(m), pack(v), name="adamw_small")
    for out, p in ((grad_out, packed_g), (delta_out, d_s), (m_out, m_s), (v_out, v_s)):
        out.update(_unpack_small(p))

    result = [loss.reshape(()), grad_x.reshape(x.shape)]
    for out in (grad_out, delta_out, m_out, v_out):
        result += [out[k].reshape(shapes[k]) for k in WEIGHTS]
    return tuple(result)
```

```python
import math
from typing import Callable, NamedTuple

import jax
import jax.numpy as jnp
from jax import lax
from jax.experimental import pallas as pl
from jax.experimental.pallas import tpu as pltpu

F32 = jnp.float32
BF16 = jnp.bfloat16
HIGHEST = lax.Precision.HIGHEST
MESH = pl.DeviceIdType.MESH

D_MODEL = 1024
MEM_LEN = 256
HG_HEADS = 8
HG_DK = 128
HG_CHUNK = 64
SWA_HEADS = 16
SWA_KV_HEADS = 2
SWA_GROUP = 8
SWA_HEAD_DIM = 64
SWA_BLOCK = 128
SWA_WINDOW = 128
MEM_HEADS = 4
MEM_HEAD_DIM = 256
NUM_BUCKETS = 32
MAX_DISTANCE = 128
D_FF = 4096
LN_EPS = 1e-5
RMS_EPS = 1e-6
ALPHA = 2.0 ** 0.25
W_A, W_B, W_C, W_D = 4096, 1280, 1024, 3072
IN_COLS = W_A + W_B + W_C + W_D
N_SHARDS = 4
ADAM_LR = 0.001
ADAM_B1 = 0.9
ADAM_B2 = 0.999
ADAM_EPS = 1e-08
ADAM_WD = 0.01
ADAM_STEP = 10
MASK_VALUE = -1e30
VMEM_LIMIT = 56 * 1024 * 1024

NN = ((1,), (0,))
NT = ((1,), (1,))
TN = ((0,), (0,))
HBM = pl.BlockSpec(memory_space=pltpu.HBM)


def _dot(a, b, dims=NN, precision=None):
    return lax.dot_general(a, b, (dims, ((), ())), precision=precision, preferred_element_type=F32)


def _params(sem=None):
    return pltpu.CompilerParams(dimension_semantics=sem, vmem_limit_bytes=VMEM_LIMIT)


def _resident(shape):
    zeros = (0,) * len(shape)
    return pl.BlockSpec(shape, lambda *_: zeros, pipeline_mode=pl.Buffered(1))


def _mm(a, b, *, mode, tm, tn, tk, name, out_dtype=F32, b_panels=False, out_panels=False, add=None, add_scale=1.0):
    if mode == "tn":
        kdim, m = a.shape
    else:
        m, kdim = a.shape
    if b_panels:
        n = b.shape[0] * b.shape[2]
        assert b.shape[2] == tn and mode == "nn"
    elif mode == "nt":
        n = b.shape[0]
    else:
        n = b.shape[1]
    assert m % tm == 0 and n % tn == 0 and kdim % tk == 0, (name, m, n, kdim)
    nk = kdim // tk
    dims = {"nn": NN, "nt": NT, "tn": TN}[mode]
    a_spec = pl.BlockSpec((tk, tm), lambda i, j, k: (k, i)) if mode == "tn" else pl.BlockSpec((tm, tk), lambda i, j, k: (i, k))
    if b_panels:
        b_spec = pl.BlockSpec((None, tk, tn), lambda i, j, k: (j, k, 0))
    elif mode == "nt":
        b_spec = pl.BlockSpec((tn, tk), lambda i, j, k: (j, k))
    else:
        b_spec = pl.BlockSpec((tk, tn), lambda i, j, k: (k, j))
    if out_panels:
        out_shape = jax.ShapeDtypeStruct((n // tn, m, tn), out_dtype)
        o_spec = pl.BlockSpec((None, tm, tn), lambda i, j, k: (j, i, 0))
    else:
        out_shape = jax.ShapeDtypeStruct((m, n), out_dtype)
        o_spec = pl.BlockSpec((tm, tn), lambda i, j, k: (i, j))
    in_specs = [a_spec, b_spec]
    operands = [a, b]
    if add is not None:
        in_specs.append(pl.BlockSpec((tm, tn), lambda i, j, k: (i, j)))
        operands.append(add)

    def body(*refs):
        a_ref, b_ref = refs[0], refs[1]
        add_ref = refs[2] if add is not None else None
        o_ref = refs[3] if add is not None else refs[2]
        part = _dot(a_ref[...].astype(BF16), b_ref[...].astype(BF16), dims)

        def finish(acc):
            if add_ref is not None:
                acc = acc + add_scale * add_ref[...]
            o_ref[...] = acc.astype(out_dtype)

        if nk == 1:
            finish(part)
        else:
            acc_ref = refs[-1]
            k = pl.program_id(2)

            @pl.when(k == 0)
            def _():
                acc_ref[...] = part

            @pl.when(k > 0)
            def _():
                acc_ref[...] += part

            @pl.when(k == nk - 1)
            def _():
                finish(acc_ref[...])

    return pl.pallas_call(
        body, name=name, out_shape=out_shape, grid=(m // tm, n // tn, nk), in_specs=in_specs, out_specs=o_spec,
        scratch_shapes=[pltpu.VMEM((tm, tn), F32)] if nk > 1 else [],
        compiler_params=_params(("parallel", "parallel", "arbitrary")),
    )(*operands)


def _dx_matmul(dzs, wis, resid, *, tm, tks, name, exchanges=()):
    s = resid.shape[0]
    counts = [dz.shape[1] // tk for dz, tk in zip(dzs, tks)]
    starts = [sum(counts[:p]) for p in range(len(counts))]
    nk = sum(counts)
    npieces = len(dzs)

    def piece_block(p):
        return lambda i, k: (i, jnp.clip(k - starts[p], 0, counts[p] - 1))

    def weight_block(p):
        return lambda i, k: (0, jnp.clip(k - starts[p], 0, counts[p] - 1))

    in_specs = [pl.BlockSpec((tm, tks[p]), piece_block(p)) for p in range(npieces)]
    in_specs += [pl.BlockSpec((D_MODEL, tks[p]), weight_block(p)) for p in range(npieces)]
    in_specs += [pl.BlockSpec((tm, D_MODEL), lambda i, k: (i, 0))]

    def body(*refs):
        dz_refs, w_refs = refs[:npieces], refs[npieces:2 * npieces]
        r_ref, o_ref, acc_ref = refs[2 * npieces], refs[2 * npieces + 1], refs[2 * npieces + 2]
        k = pl.program_id(1)

        @pl.when(k == 0)
        def _():
            acc_ref[...] = ALPHA * r_ref[...]

        for p in range(npieces):
            @pl.when((k >= starts[p]) & (k < starts[p] + counts[p]))
            def _(p=p):
                acc_ref[...] += _dot(dz_refs[p][...], w_refs[p][...], NT)

        @pl.when(k == nk - 1)
        def _():
            o_ref[...] = acc_ref[...]

    return _fused_call(
        body, name=name, out_shape=jax.ShapeDtypeStruct((s, D_MODEL), F32), grid=(s // tm, nk), in_specs=in_specs,
        out_specs=pl.BlockSpec((tm, D_MODEL), lambda i, k: (i, 0)), scratch_shapes=[pltpu.VMEM((tm, D_MODEL), F32)],
        operands=[*dzs, *wis, resid], exchanges=exchanges)


def _lower_bound(lbl_ref):
    l0, l1 = lbl_ref[0:1, :], lbl_ref[1:2, :]
    mx = jnp.maximum(l0, l1)
    e0, e1 = jnp.exp(l0 - mx), jnp.exp(l1 - mx)
    return e0 / (e0 + e1)


HEAD_COLS = [slice(h * HG_DK, (h + 1) * HG_DK) for h in range(HG_HEADS)]


def _head_mean(x):
    return jnp.concatenate([jnp.broadcast_to(jnp.mean(x[:, c], axis=-1, keepdims=True), (x.shape[0], HG_DK)) for c in HEAD_COLS], axis=1)


def _chunk_forward(q, fl, v, lb, tril_f):
    sg = jax.nn.sigmoid(fl)
    f = lb + (1.0 - lb) * sg
    k = 1.0 - f
    b = _dot(tril_f, jnp.log(f), NN, HIGHEST)
    b_last = b[HG_CHUNK - 1:HG_CHUNK, :]
    eb, enb, eo = jnp.exp(b), jnp.exp(-b), jnp.exp(b_last - b)
    return sg, f, k, b_last, eb, enb, eo, q * eb, k * enb, k * eo


def _hgrn_fwd(za, lb_logits, gain, *, name, exchanges=()):
    s = za.shape[0]
    t = min(256, s)
    ncs = t // HG_CHUNK

    def body(z_ref, lbl_ref, gain_ref, oa_ref, oraw_ref, st_ref, state):
        @pl.when(pl.program_id(0) == 0)
        def _():
            state[...] = jnp.zeros_like(state)

        lb_all = _lower_bound(lbl_ref)
        row = lax.broadcasted_iota(jnp.int32, (HG_CHUNK, HG_CHUNK), 0)
        col = lax.broadcasted_iota(jnp.int32, (HG_CHUNK, HG_CHUNK), 1)
        tril = row >= col
        tril_f = tril.astype(F32)
        gain_all = gain_ref[...]

        def chunk(i, carry):
            r = pl.ds(pl.multiple_of(i * HG_CHUNK, HG_CHUNK), HG_CHUNK)
            q, fl, v, hg = (z_ref[r, j * D_MODEL:(j + 1) * D_MODEL] for j in range(4))
            _, _, _, b_last, _, _, _, q_in, k_in, k_out = _chunk_forward(q, fl, v, lb_all, tril_f)
            q_in_b, k_in_b, k_out_b, vb = (u.astype(BF16) for u in (q_in, k_in, k_out, v))
            decay = jnp.exp(b_last)
            sts = [state[h] for h in range(HG_HEADS)]
            attn = [_dot(q_in_b[:, c], k_in_b[:, c], NT) for c in HEAD_COLS]
            inter = [_dot(q_in_b[:, c], sts[h].astype(BF16), NT) for h, c in enumerate(HEAD_COLS)]
            upd = [_dot(vb[:, c], k_out_b[:, c], TN) for c in HEAD_COLS]
            attn = [jnp.where(tril, a, 0.0).astype(BF16) for a in attn]
            outs = [_dot(attn[h], vb[:, c], NN) + inter[h] for h, c in enumerate(HEAD_COLS)]
            for h, c in enumerate(HEAD_COLS):
                st_ref[h, i] = sts[h]
                state[h] = sts[h] * decay[:, c] + upd[h]
            o = jnp.concatenate(outs, axis=1)
            oraw_ref[r, :] = o
            n = o * lax.rsqrt(_head_mean(o * o) + RMS_EPS)
            oa_ref[r, :] = (n * gain_all * (hg * jax.nn.sigmoid(hg))).astype(BF16)
            return carry

        lax.fori_loop(0, ncs, chunk, 0)

    return _fused_call(
        body, name=name, grid=(s // t,),
        out_shape=(jax.ShapeDtypeStruct((s, D_MODEL), BF16), jax.ShapeDtypeStruct((s, D_MODEL), F32),
                   jax.ShapeDtypeStruct((HG_HEADS, s // HG_CHUNK, HG_DK, HG_DK), F32)),
        in_specs=[pl.BlockSpec((t, W_A), lambda i: (i, 0)), _resident((2, D_MODEL)), _resident((1, D_MODEL))],
        out_specs=(pl.BlockSpec((t, D_MODEL), lambda i: (i, 0)), pl.BlockSpec((t, D_MODEL), lambda i: (i, 0)),
                   pl.BlockSpec((HG_HEADS, ncs, HG_DK, HG_DK), lambda i: (0, i, 0, 0))),
        scratch_shapes=[pltpu.VMEM((HG_HEADS, HG_DK, HG_DK), F32)],
        operands=[za, lb_logits, gain], exchanges=exchanges)


def _hgrn_bwd(za, oraw, do_a, states, lb_logits, gain, *, name, exchanges=()):
    s = za.shape[0]
    t = min(256, s)
    ncs = t // HG_CHUNK
    nt = s // t

    def body(z_ref, oraw_ref, do_ref, st_ref, lbl_ref, gain_ref, dz_ref, stats_ref, dstate):
        step = pl.program_id(0)

        @pl.when(step == 0)
        def _():
            dstate[...] = jnp.zeros_like(dstate)
            stats_ref[...] = jnp.zeros_like(stats_ref)

        lb_all = _lower_bound(lbl_ref)
        row = lax.broadcasted_iota(jnp.int32, (HG_CHUNK, HG_CHUNK), 0)
        col = lax.broadcasted_iota(jnp.int32, (HG_CHUNK, HG_CHUNK), 1)
        tril = row >= col
        tril_f = tril.astype(F32)
        triu_f = (row <= col).astype(F32)
        gain_all = gain_ref[...]

        def chunk(ii, carry):
            i = ncs - 1 - ii
            r = pl.ds(pl.multiple_of(i * HG_CHUNK, HG_CHUNK), HG_CHUNK)
            q, fl, v, hg = (z_ref[r, j * D_MODEL:(j + 1) * D_MODEL] for j in range(4))
            o = oraw_ref[r, :]
            doa = do_ref[r, :]
            rms = lax.rsqrt(_head_mean(o * o) + RMS_EPS)
            n = o * rms
            sgg = jax.nn.sigmoid(hg)
            silu = hg * sgg
            dhg = doa * n * gain_all * (sgg * (1.0 + hg * (1.0 - sgg)))
            dgain = jnp.sum(doa * n * silu, axis=0, keepdims=True)
            dn = doa * gain_all * silu
            do = rms * (dn - n * _head_mean(dn * n))
            sg, f, k, b_last, eb, enb, eo, q_in, k_in, k_out = _chunk_forward(q, fl, v, lb_all, tril_f)
            q_in_b, k_in_b, k_out_b, vb, dob = (u.astype(BF16) for u in (q_in, k_in, k_out, v, do))
            decay = jnp.exp(b_last)
            sts = [st_ref[h, i] for h in range(HG_HEADS)]
            dsts = [dstate[h] for h in range(HG_HEADS)]
            dsts_b = [d.astype(BF16) for d in dsts]
            heads = list(enumerate(HEAD_COLS))
            attn = [_dot(q_in_b[:, c], k_in_b[:, c], NT) for h, c in heads]
            dattn = [_dot(dob[:, c], vb[:, c], NT) for h, c in heads]
            dq_st = [_dot(dob[:, c], sts[h].astype(BF16), NN) for h, c in heads]
            dk_out = [_dot(vb[:, c], dsts_b[h], NN) for h, c in heads]
            dv_st = [_dot(k_out_b[:, c], dsts_b[h], NT) for h, c in heads]
            dst_o = [_dot(dob[:, c], q_in_b[:, c], TN) for h, c in heads]
            attn = [jnp.where(tril, a, 0.0).astype(BF16) for a in attn]
            dattn = [jnp.where(tril, a, 0.0).astype(BF16) for a in dattn]
            dq_in = jnp.concatenate([_dot(dattn[h], k_in_b[:, c], NN) + dq_st[h] for h, c in heads], axis=1)
            dk_in = jnp.concatenate([_dot(dattn[h], q_in_b[:, c], TN) for h, c in heads], axis=1)
            dv = jnp.concatenate([_dot(attn[h], dob[:, c], TN) + dv_st[h] for h, c in heads], axis=1)
            dk_out = jnp.concatenate(dk_out, axis=1)
            dst_st = jnp.concatenate([jnp.sum(dsts[h] * sts[h], axis=0, keepdims=True) for h in range(HG_HEADS)], axis=1)
            for h, c in heads:
                dstate[h] = dsts[h] * decay[:, c] + dst_o[h]
            db_last = decay * dst_st + jnp.sum(dk_out * k_out, axis=0, keepdims=True)
            db = dq_in * q_in - dk_in * k_in - dk_out * k_out
            dg = _dot(triu_f, db, NN, HIGHEST) + db_last
            dk = dk_in * enb + dk_out * eo
            df = dg / f - dk
            stats_ref[0:1, :] += dgain
            stats_ref[1:2, :] += jnp.sum(df * (1.0 - sg), axis=0, keepdims=True)
            dz_ref[r, 0:1024] = (dq_in * eb).astype(BF16)
            dz_ref[r, 1024:2048] = (df * (1.0 - lb_all) * sg * (1.0 - sg)).astype(BF16)
            dz_ref[r, 2048:3072] = dv.astype(BF16)
            dz_ref[r, 3072:4096] = dhg.astype(BF16)
            return carry

        lax.fori_loop(0, ncs, chunk, 0)

        @pl.when(step == nt - 1)
        def _():
            dl0 = stats_ref[1:2, :] * lb_all * (1.0 - lb_all)
            stats_ref[1:2, :] = dl0
            stats_ref[2:3, :] = -dl0

    rev = lambda i: (nt - 1 - i, 0)
    return _fused_call(
        body, name=name, grid=(nt,),
        out_shape=(jax.ShapeDtypeStruct((s, W_A), BF16), jax.ShapeDtypeStruct((8, D_MODEL), F32)),
        in_specs=[pl.BlockSpec((t, W_A), rev), pl.BlockSpec((t, D_MODEL), rev), pl.BlockSpec((t, D_MODEL), rev),
                  pl.BlockSpec((HG_HEADS, ncs, HG_DK, HG_DK), lambda i: (0, nt - 1 - i, 0, 0)),
                  _resident((2, D_MODEL)), _resident((1, D_MODEL))],
        out_specs=(pl.BlockSpec((t, W_A), rev), pl.BlockSpec((8, D_MODEL), lambda i: (0, 0))),
        scratch_shapes=[pltpu.VMEM((HG_HEADS, HG_DK, HG_DK), F32)],
        operands=[za, oraw, do_a, states, lb_logits, gain], exchanges=exchanges)


def _t5_bucket(n):
    max_exact = NUM_BUCKETS // 2
    nf = jnp.maximum(n, 1).astype(F32)
    large = max_exact + (jnp.log(nf / max_exact) / math.log(MAX_DISTANCE / max_exact) * (NUM_BUCKETS - max_exact)).astype(jnp.int32)
    large = jnp.minimum(large, NUM_BUCKETS - 1)
    return jnp.where(n < max_exact, n, large)


def _bias_selector():
    qi = jnp.arange(SWA_BLOCK)[:, None] + SWA_BLOCK
    kj = jnp.arange(2 * SWA_BLOCK)[None, :]
    dist = qi - kj
    band = ((dist >= 0) & (dist < SWA_WINDOW)).reshape(1, -1)
    bucket = _t5_bucket(jnp.clip(dist, 0, SWA_WINDOW - 1)).reshape(1, -1)
    onehot = ((bucket == jnp.arange(NUM_BUCKETS)[:, None]) & band).astype(F32)
    return onehot, jnp.where(band, 0.0, MASK_VALUE).astype(F32)


def _bias_table(rel_bias_t, onehot, maskrow, *, name):
    def body(rb_ref, oh_ref, mask_ref, o_ref):
        o_ref[...] = _dot(rb_ref[...], oh_ref[...], NN, HIGHEST) + mask_ref[...]

    return pl.pallas_call(body, name=name, out_shape=jax.ShapeDtypeStruct((SWA_HEADS, onehot.shape[1]), F32),
                          compiler_params=_params())(rel_bias_t, onehot, maskrow)


def _bias_grad(dbias2d, onehot, *, name):
    def body(db_ref, oh_ref, o_ref):
        o_ref[...] = _dot(db_ref[...], oh_ref[...], NT, HIGHEST)

    return pl.pallas_call(body, name=name, out_shape=jax.ShapeDtypeStruct((SWA_HEADS, NUM_BUCKETS), F32),
                          compiler_params=_params())(dbias2d, onehot)


GROUP_LANES = SWA_GROUP * SWA_BLOCK


def _swa_operands(zq_ref, kv_cur_ref, kv_prev_ref):
    q = (zq_ref[:, 0:1024] * (SWA_HEAD_DIM ** -0.5)).astype(BF16)
    kv_c = kv_cur_ref[...].astype(BF16)
    kv_p = kv_prev_ref[...].astype(BF16)
    kks = [jnp.concatenate([kv_p[:, g * 64:(g + 1) * 64], kv_c[:, g * 64:(g + 1) * 64]], axis=0) for g in range(SWA_KV_HEADS)]
    vvs = [jnp.concatenate([kv_p[:, 128 + g * 64:128 + (g + 1) * 64], kv_c[:, 128 + g * 64:128 + (g + 1) * 64]], axis=0)
           for g in range(SWA_KV_HEADS)]
    return q, kks, vvs


def _stack_heads(x, g):
    return jnp.concatenate([x[:, h * SWA_HEAD_DIM:(h + 1) * SWA_HEAD_DIM] for h in range(g * SWA_GROUP, (g + 1) * SWA_GROUP)], axis=0)


def _heads_to_lanes(xt):
    pairs = []
    for j in range(0, SWA_GROUP, 2):
        two = jnp.concatenate([xt[:, j * SWA_BLOCK:(j + 1) * SWA_BLOCK], xt[:, (j + 1) * SWA_BLOCK:(j + 2) * SWA_BLOCK]], axis=0)
        pairs.append(two.T)
    return jnp.concatenate(pairs, axis=1)


def _swa_softmax(score_t, bias_ref, sink_ref, g):
    lanes = slice(g * GROUP_LANES, (g + 1) * GROUP_LANES)
    sc = score_t + bias_ref[:, lanes]
    sink = sink_ref[:, lanes]
    m = jnp.maximum(jnp.max(sc, axis=0, keepdims=True), sink)
    e = jnp.exp(sc - m)
    e_sink = jnp.exp(sink - m)
    return e, 1.0 / (jnp.sum(e, axis=0, keepdims=True) + e_sink), e_sink


def _swa_tables(bias2d, sinks):
    bias_t = bias2d.reshape(SWA_HEADS, SWA_BLOCK, 2 * SWA_BLOCK).transpose(2, 0, 1).reshape(2 * SWA_BLOCK, SWA_HEADS * SWA_BLOCK)
    first = jnp.where(jnp.arange(2 * SWA_BLOCK)[:, None] < SWA_BLOCK, MASK_VALUE, bias_t)
    return jnp.stack([first, bias_t]), jnp.repeat(sinks, SWA_BLOCK, axis=1)


def _swa_fwd(zb, bias_tables, sink_lanes, *, name, exchanges=()):
    s = zb.shape[0]
    nb = s // SWA_BLOCK

    def body(zq_ref, kvc_ref, kvp_ref, bias_ref, sink_ref, o_ref):
        q, kks, vvs = _swa_operands(zq_ref, kvc_ref, kvp_ref)
        groups = range(SWA_KV_HEADS)
        scores = [_dot(kks[g], _stack_heads(q, g), NT) for g in groups]
        probs = []
        for g in groups:
            e, inv, _ = _swa_softmax(scores[g], bias_ref, sink_ref, g)
            probs.append((e * inv).astype(BF16))
        outs = [_dot(vvs[g], probs[g], TN) for g in groups]
        o_ref[...] = jnp.concatenate([_heads_to_lanes(outs[g]) for g in groups], axis=1).astype(BF16)

    return _fused_call(
        body, name=name, grid=(nb,), out_shape=jax.ShapeDtypeStruct((s, D_MODEL), BF16),
        in_specs=[pl.BlockSpec((SWA_BLOCK, W_B), lambda n: (n, 0)),
                  pl.BlockSpec((SWA_BLOCK, 256), lambda n: (n, 4)),
                  pl.BlockSpec((SWA_BLOCK, 256), lambda n: (jnp.maximum(n - 1, 0), 4)),
                  pl.BlockSpec((None, 2 * SWA_BLOCK, SWA_HEADS * SWA_BLOCK), lambda n: (jnp.minimum(n, 1), 0, 0)),
                  _resident((1, SWA_HEADS * SWA_BLOCK))],
        out_specs=pl.BlockSpec((SWA_BLOCK, D_MODEL), lambda n: (n, 0)), scratch_shapes=[],
        operands=[zb, zb, zb, bias_tables, sink_lanes], exchanges=exchanges)


def _swa_bwd(zb, do_b, bias_tables, sink_lanes, *, name, exchanges=()):
    s = zb.shape[0]
    nb = s // SWA_BLOCK
    scale = SWA_HEAD_DIM ** -0.5

    def body(zq_ref, kvc_ref, kvp_ref, do_ref, bias_ref, sink_ref, dz_ref, dbias_ref, dsink_ref, carry, dsink_acc):
        step = pl.program_id(0)

        @pl.when(step == 0)
        def _():
            carry[...] = jnp.zeros_like(carry)
            dsink_acc[...] = jnp.zeros_like(dsink_acc)
            dbias_ref[...] = jnp.zeros_like(dbias_ref)

        q, kks, vvs = _swa_operands(zq_ref, kvc_ref, kvp_ref)
        groups = range(SWA_KV_HEADS)
        do = do_ref[...].astype(BF16)
        q_rows = [_stack_heads(q, g) for g in groups]
        do_rows = [_stack_heads(do, g) for g in groups]
        scores = [_dot(kks[g], q_rows[g], NT) for g in groups]
        dps = [_dot(vvs[g], do_rows[g], NT) for g in groups]
        ps, dss = [], []
        for g in groups:
            lanes = slice(g * GROUP_LANES, (g + 1) * GROUP_LANES)
            e, inv, e_sink = _swa_softmax(scores[g], bias_ref, sink_ref, g)
            p = e * inv
            delta = jnp.sum(p * dps[g], axis=0, keepdims=True)
            ds = p * (dps[g] - delta)
            dbias_ref[:, lanes] += ds
            dsink_acc[:, lanes] -= e_sink * inv * delta
            ps.append(p.astype(BF16))
            dss.append(ds.astype(BF16))
        dqs = [_dot(kks[g], dss[g], TN) * scale for g in groups]
        dkks = [_dot(dss[g], q_rows[g], NN) for g in groups]
        dvvs = [_dot(ps[g], do_rows[g], NN) for g in groups]
        dkv = jnp.concatenate(dkks + dvvs, axis=1)
        dz_ref[:, 0:1024] = jnp.concatenate([_heads_to_lanes(dqs[g]) for g in groups], axis=1).astype(BF16)
        dz_ref[:, 1024:1280] = (dkv[SWA_BLOCK:, :] + carry[...]).astype(BF16)
        carry[...] = dkv[:SWA_BLOCK, :]

        @pl.when(step == nb - 1)
        def _():
            acc = dsink_acc[...]
            dsink_ref[...] = jnp.concatenate([jnp.sum(acc[:, h * SWA_BLOCK:(h + 1) * SWA_BLOCK], axis=1, keepdims=True)
                                              for h in range(SWA_HEADS)], axis=1)

    rev = lambda i: (nb - 1 - i, 0)
    table_shape = (2 * SWA_BLOCK, SWA_HEADS * SWA_BLOCK)
    return _fused_call(
        body, name=name, grid=(nb,),
        out_shape=(jax.ShapeDtypeStruct((s, W_B), BF16), jax.ShapeDtypeStruct(table_shape, F32), jax.ShapeDtypeStruct((1, SWA_HEADS), F32)),
        in_specs=[pl.BlockSpec((SWA_BLOCK, W_B), rev),
                  pl.BlockSpec((SWA_BLOCK, 256), lambda i: (nb - 1 - i, 4)),
                  pl.BlockSpec((SWA_BLOCK, 256), lambda i: (jnp.maximum(nb - 2 - i, 0), 4)),
                  pl.BlockSpec((SWA_BLOCK, D_MODEL), rev),
                  pl.BlockSpec((None,) + table_shape, lambda i: (jnp.minimum(nb - 1 - i, 1), 0, 0)),
                  _resident((1, SWA_HEADS * SWA_BLOCK))],
        out_specs=(pl.BlockSpec((SWA_BLOCK, W_B), rev), pl.BlockSpec(table_shape, lambda i: (0, 0)),
                   pl.BlockSpec((1, SWA_HEADS), lambda i: (0, 0))),
        scratch_shapes=[pltpu.VMEM((SWA_BLOCK, 256), F32), pltpu.VMEM((1, SWA_HEADS * SWA_BLOCK), F32)],
        operands=[zb, zb, zb, do_b, bias_tables, sink_lanes], exchanges=exchanges)


def _mem_probs(zc_ref, mkv_ref, h):
    cols = slice(h * MEM_HEAD_DIM, (h + 1) * MEM_HEAD_DIM)
    qh = (zc_ref[:, cols] * (MEM_HEAD_DIM ** -0.5)).astype(BF16)
    sc = _dot(qh, mkv_ref[:, cols], NT)
    e = jnp.exp(sc - jnp.max(sc, axis=-1, keepdims=True))
    return qh, e / jnp.sum(e, axis=-1, keepdims=True)


def _mem_fwd(zc, mkv, *, name):
    s = zc.shape[0]
    t = min(512, s)

    def body(zc_ref, mkv_ref, o_ref):
        for h in range(MEM_HEADS):
            _, p = _mem_probs(zc_ref, mkv_ref, h)
            vh = mkv_ref[:, D_MODEL + h * MEM_HEAD_DIM:D_MODEL + (h + 1) * MEM_HEAD_DIM]
            o_ref[:, h * MEM_HEAD_DIM:(h + 1) * MEM_HEAD_DIM] = _dot(p.astype(BF16), vh, NN).astype(BF16)

    return pl.pallas_call(
        body, name=name, grid=(s // t,), out_shape=jax.ShapeDtypeStruct((s, D_MODEL), BF16),
        in_specs=[pl.BlockSpec((t, D_MODEL), lambda i: (i, 0)), _resident((MEM_LEN, 2 * D_MODEL))],
        out_specs=pl.BlockSpec((t, D_MODEL), lambda i: (i, 0)), compiler_params=_params(("parallel",)),
    )(zc, mkv)


def _mem_bwd(zc, do_c, mkv, *, name):
    s = zc.shape[0]
    t = min(512, s)

    def body(zc_ref, do_ref, mkv_ref, dz_ref, dmkv_ref):
        @pl.when(pl.program_id(0) == 0)
        def _():
            dmkv_ref[...] = jnp.zeros_like(dmkv_ref)

        for h in range(MEM_HEADS):
            cols = slice(h * MEM_HEAD_DIM, (h + 1) * MEM_HEAD_DIM)
            vcols = slice(D_MODEL + h * MEM_HEAD_DIM, D_MODEL + (h + 1) * MEM_HEAD_DIM)
            qh, p = _mem_probs(zc_ref, mkv_ref, h)
            doh = do_ref[:, cols].astype(BF16)
            dp = _dot(doh, mkv_ref[:, vcols], NT)
            ds = (p * (dp - jnp.sum(p * dp, axis=-1, keepdims=True))).astype(BF16)
            dz_ref[:, cols] = (_dot(ds, mkv_ref[:, cols], NN) * (MEM_HEAD_DIM ** -0.5)).astype(BF16)
            dmkv_ref[:, cols] += _dot(ds, qh, TN)
            dmkv_ref[:, vcols] += _dot(p.astype(BF16), doh, TN)

    return pl.pallas_call(
        body, name=name, grid=(s // t,),
        out_shape=(jax.ShapeDtypeStruct((s, D_MODEL), BF16), jax.ShapeDtypeStruct((MEM_LEN, 2 * D_MODEL), F32)),
        in_specs=[pl.BlockSpec((t, D_MODEL), lambda i: (i, 0)), pl.BlockSpec((t, D_MODEL), lambda i: (i, 0)),
                  _resident((MEM_LEN, 2 * D_MODEL))],
        out_specs=(pl.BlockSpec((t, D_MODEL), lambda i: (i, 0)), pl.BlockSpec((MEM_LEN, 2 * D_MODEL), lambda i: (0, 0))),
        compiler_params=_params(("arbitrary",)),
    )(zc, do_c, mkv)


def _normalize(pre):
    mu = jnp.mean(pre, axis=-1, keepdims=True)
    xc = pre - mu
    rstd = lax.rsqrt(jnp.mean(xc * xc, axis=-1, keepdims=True) + LN_EPS)
    return xc * rstd, rstd


def _layer_norm_bwd(dh, xhat, rstd, g):
    dxh = dh * g
    dpre = rstd * (dxh - jnp.mean(dxh, axis=-1, keepdims=True) - xhat * jnp.mean(dxh * xhat, axis=-1, keepdims=True))
    return dpre, jnp.sum(dh * xhat, axis=0, keepdims=True), jnp.sum(dh, axis=0, keepdims=True)


def _merge_fwd(o_a, o_b, o_c, zd, x, wbr, wo, *, name):
    s = x.shape[0]
    t = min(256, s)
    row = lambda w, dt=None: pl.BlockSpec((t, w), lambda i: (i, 0))

    def body(oa_ref, ob_ref, oc_ref, zd_ref, x_ref, wbr_ref, wo_ref, xhat_ref, rstd_ref, merged_ref, pa_ref, pb_ref, pc_ref):
        merged = jnp.zeros((t, D_MODEL), F32)
        for b, (o_ref, p_ref) in enumerate(((oa_ref, pa_ref), (ob_ref, pb_ref), (oc_ref, pc_ref))):
            p = _dot(o_ref[...], wbr_ref[b], NN)
            p_ref[...] = p
            merged = merged + jax.nn.sigmoid(zd_ref[:, b * D_MODEL:(b + 1) * D_MODEL]) * p
        merged_b = merged.astype(BF16)
        merged_ref[...] = merged_b
        xhat, rstd = _normalize(ALPHA * x_ref[...] + _dot(merged_b, wo_ref[...], NN))
        xhat_ref[...] = xhat
        rstd_ref[...] = rstd

    act = jax.ShapeDtypeStruct((s, D_MODEL), F32)
    return pl.pallas_call(
        body, name=name, grid=(s // t,),
        out_shape=(act, jax.ShapeDtypeStruct((s, 1), F32), jax.ShapeDtypeStruct((s, D_MODEL), BF16), act, act, act),
        in_specs=[row(D_MODEL), row(D_MODEL), row(D_MODEL), row(W_D), row(D_MODEL),
                  _resident((3, D_MODEL, D_MODEL)), _resident((D_MODEL, D_MODEL))],
        out_specs=(row(D_MODEL), row(1), row(D_MODEL), row(D_MODEL), row(D_MODEL), row(D_MODEL)),
        compiler_params=_params(("parallel",)),
    )(o_a, o_b, o_c, zd, x, wbr, wo)


def _merge_bwd(dpre1, zd, pa, pb, pc, wbr, wo, *, name, exchanges=()):
    s = dpre1.shape[0]
    t = min(256, s)
    row = lambda w: pl.BlockSpec((t, w), lambda i: (i, 0))

    def body(dpre_ref, zd_ref, pa_ref, pb_ref, pc_ref, wbr_ref, wo_ref, dzd_ref, dpa_ref, dpb_ref, dpc_ref, doa_ref, dob_ref, doc_ref):
        dmerged = _dot(dpre_ref[...].astype(BF16), wo_ref[...], NT)
        branches = ((pa_ref, dpa_ref, doa_ref), (pb_ref, dpb_ref, dob_ref), (pc_ref, dpc_ref, doc_ref))
        for b, (p_ref, dp_ref, do_ref) in enumerate(branches):
            gate = jax.nn.sigmoid(zd_ref[:, b * D_MODEL:(b + 1) * D_MODEL])
            dzd_ref[:, b * D_MODEL:(b + 1) * D_MODEL] = (dmerged * p_ref[...] * gate * (1.0 - gate)).astype(BF16)
            dp = (dmerged * gate).astype(BF16)
            dp_ref[...] = dp
            do_ref[...] = _dot(dp, wbr_ref[b], NT)

    act = jax.ShapeDtypeStruct((s, D_MODEL), F32)
    actb = jax.ShapeDtypeStruct((s, D_MODEL), BF16)
    return _fused_call(
        body, name=name, grid=(s // t,),
        out_shape=(jax.ShapeDtypeStruct((s, W_D), BF16), actb, actb, actb, act, act, act),
        in_specs=[row(D_MODEL), row(W_D), row(D_MODEL), row(D_MODEL), row(D_MODEL),
                  _resident((3, D_MODEL, D_MODEL)), _resident((D_MODEL, D_MODEL))],
        out_specs=(row(W_D),) + (row(D_MODEL),) * 6, scratch_shapes=[],
        operands=[dpre1, zd, pa, pb, pc, wbr, wo], exchanges=exchanges)


def _mlp_loss(xhat1, rstd1, target, ln1_g, ln1_b, ln2_g, ln2_b, wu, wd, *, name):
    s = xhat1.shape[0]
    t = min(256, s)
    npan = wu.shape[0]
    row = lambda w: pl.BlockSpec((t, w), lambda i: (i, 0))
    vec = _resident((1, D_MODEL))

    def body(xhat_ref, rstd_ref, tgt_ref, g1_ref, b1_ref, g2_ref, b2_ref, wu_ref, wd_ref,
             dpre1_ref, dpre2_ref, h1_ref, a_ref, du_ref, stats_ref):
        @pl.when(pl.program_id(0) == 0)
        def _():
            stats_ref[...] = jnp.zeros_like(stats_ref)

        xhat1_v = xhat_ref[...]
        h1 = xhat1_v * g1_ref[...] + b1_ref[...]
        h1_b = h1.astype(BF16)
        h1_ref[...] = h1_b
        us = []
        ff = jnp.zeros((t, D_MODEL), F32)
        for j in range(npan):
            u = _dot(h1_b, wu_ref[j], NN)
            us.append(u)
            r = jnp.maximum(u, 0.0)
            a_b = (r * r).astype(BF16)
            a_ref[:, j * D_MODEL:(j + 1) * D_MODEL] = a_b
            ff = ff + _dot(a_b, wd_ref[j], NN)
        xhat2, rstd2 = _normalize(ALPHA * h1 + ff)
        err = xhat2 * g2_ref[...] + b2_ref[...] - tgt_ref[...]
        stats_ref[4:5, :] += jnp.sum(err * err, axis=0, keepdims=True)
        dpre2, dg2, db2 = _layer_norm_bwd(err * (1.0 / D_MODEL), xhat2, rstd2, g2_ref[...])
        stats_ref[0:1, :] += dg2
        stats_ref[1:2, :] += db2
        dpre2_b = dpre2.astype(BF16)
        dpre2_ref[...] = dpre2_b
        dh1 = ALPHA * dpre2
        for j in range(npan):
            du_b = (_dot(dpre2_b, wd_ref[j], NT) * (2.0 * jnp.maximum(us[j], 0.0))).astype(BF16)
            du_ref[:, j * D_MODEL:(j + 1) * D_MODEL] = du_b
            dh1 = dh1 + _dot(du_b, wu_ref[j], NT)
        dpre1, dg1, db1 = _layer_norm_bwd(dh1, xhat1_v, rstd_ref[...], g1_ref[...])
        stats_ref[2:3, :] += dg1
        stats_ref[3:4, :] += db1
        dpre1_ref[...] = dpre1

    actb = jax.ShapeDtypeStruct((s, D_MODEL), BF16)
    wide = jax.ShapeDtypeStruct((s, D_FF), BF16)
    return pl.pallas_call(
        body, name=name, grid=(s // t,),
        out_shape=(jax.ShapeDtypeStruct((s, D_MODEL), F32), actb, actb, wide, wide, jax.ShapeDtypeStruct((8, D_MODEL), F32)),
        in_specs=[row(D_MODEL), row(1), row(D_MODEL), vec, vec, vec, vec,
                  _resident((npan, D_MODEL, D_MODEL)), _resident((npan, D_MODEL, D_MODEL))],
        out_specs=(row(D_MODEL), row(D_MODEL), row(D_MODEL), row(D_FF), row(D_FF), pl.BlockSpec((8, D_MODEL), lambda i: (0, 0))),
        compiler_params=_params(("arbitrary",)),
    )(xhat1, rstd1, target, ln1_g, ln1_b, ln2_g, ln2_b, wu, wd)


BRANCH_WEIGHTS = ("w_branch_hg", "w_branch_swa", "w_branch_mem")


def _local_step(x, mem, target, wi_parts, wmkv, late, lb_logits, gain, sinks, rel_bias, ln1_g, ln1_b, ln2_g, ln2_b, *, distributed):
    s = x.shape[0]
    tm = min(1024, s)
    tk = min(2048, s)
    xb = x.astype(BF16)
    memb = mem.astype(BF16)
    wia, wib, wic, wid = wi_parts
    if distributed:
        cx, cy, cc = lax.axis_index("x"), lax.axis_index("y"), lax.axis_index("c")
        pos = jnp.stack([2 * cx + cy, cc]).astype(jnp.int32)
    gather = (lambda names: [_gather_exchange([late[k] for k in names])]) if distributed else (lambda names: [])
    to_sibling = (lambda grads: [_sibling_halves_exchange(grads)]) if distributed else (lambda grads: [])
    to_chips = (lambda sums: [_chip_partials_exchange([bf for bf, _ in sums])]) if distributed else (lambda sums: [])

    def chip_sums(names, grads, from_sibling):
        return [_add_sibling(g, o, pos, name="add_sibling_" + k) for k, g, o in zip(names, grads, from_sibling)]

    def shard_sums(names, sums, from_chips):
        return {k: _add_chips(mine, o, pos, name="add_chips_" + k) for k, (_, mine), o in zip(names, sums, from_chips)}

    za = _mm(xb, wia, mode="nn", tm=min(512, s), tn=W_A, tk=D_MODEL, name="proj_a")
    zb = _mm(xb, wib, mode="nn", tm=tm, tn=W_B, tk=D_MODEL, name="proj_b")
    zc = _mm(xb, wic, mode="nn", tm=tm, tn=W_C, tk=D_MODEL, name="proj_c")
    zd = _mm(xb, wid, mode="nn", tm=min(512, s), tn=W_D, tk=D_MODEL, name="proj_d")
    mkv = _mm(memb, wmkv, mode="nn", tm=MEM_LEN, tn=512, tk=D_MODEL, name="mem_kv", out_dtype=BF16, b_panels=True)
    onehot, maskrow = _bias_selector()
    bias_tables, sink_lanes = _swa_tables(_bias_table(rel_bias.T, onehot, maskrow, name="bias_table"), sinks)
    (o_a, o_raw, states), landed = _hgrn_fwd(za, lb_logits, gain, name="hgrn_fwd", exchanges=gather(("w_up", "w_down")))
    wu, wd = landed[0] if distributed else (late["wu"], late["wd"])
    o_b, landed = _swa_fwd(zb, bias_tables, sink_lanes, name="swa_fwd", exchanges=gather(BRANCH_WEIGHTS + ("w_out",)))
    if distributed:
        wbr = jnp.stack([wb.reshape(D_MODEL, D_MODEL) for wb in landed[0][:3]])
        wo = landed[0][3].reshape(D_MODEL, D_MODEL)
    else:
        wbr, wo = late["wbr"], late["wo"]
    o_c = _mem_fwd(zc, mkv, name="mem_fwd")
    xhat1, rstd1, merged, pa, pb, pc = _merge_fwd(o_a, o_b, o_c, zd, x, wbr, wo, name="merge_fwd")

    dpre1, dpre2, h1, act, du, ln_stats = _mlp_loss(xhat1, rstd1, target, ln1_g, ln1_b, ln2_g, ln2_b, wu, wd, name="mlp_loss")
    ffn = ("w_down", "w_up")
    g_ffn = [_mm(act, dpre2, mode="tn", tm=1024, tn=D_MODEL, tk=tk, name="grad_w_down").reshape(N_SHARDS, D_FF // N_SHARDS, D_MODEL),
             _mm(h1, du, mode="tn", tm=D_MODEL, tn=1024, tk=tk, name="grad_w_up", out_panels=True)]

    (dzd, dpa, dpb, dpc, do_a, do_b, do_c), landed = _merge_bwd(dpre1, zd, pa, pb, pc, wbr, wo, name="merge_bwd", exchanges=to_sibling(g_ffn))
    sums_ffn = chip_sums(ffn, g_ffn, landed[0]) if distributed else []
    merge = BRANCH_WEIGHTS + ("w_out",)
    g_merge = [_mm(o, dp, mode="tn", tm=D_MODEL, tn=D_MODEL, tk=tk, name="grad_" + k).reshape(N_SHARDS, D_MODEL // N_SHARDS, D_MODEL)
               for k, o, dp in zip(merge, (o_a, o_b, o_c, merged), (dpa, dpb, dpc, dpre1))]
    (dza, hg_stats), landed = _hgrn_bwd(za, o_raw, do_a, states, lb_logits, gain, name="hgrn_bwd",
                                        exchanges=to_chips(sums_ffn) + to_sibling(g_merge))
    halves = shard_sums(ffn, sums_ffn, landed[0]) if distributed else {}
    sums_merge = chip_sums(merge, g_merge, landed[1]) if distributed else []
    (dzb, dbias_t, dsinks), landed = _swa_bwd(zb, do_b, bias_tables, sink_lanes, name="swa_bwd", exchanges=to_chips(sums_merge))
    if distributed:
        halves.update(shard_sums(merge, sums_merge, landed[0]))
    dbias = dbias_t.reshape(2 * SWA_BLOCK, SWA_HEADS, SWA_BLOCK).transpose(1, 2, 0).reshape(SWA_HEADS, -1)
    d_rel_bias = _bias_grad(dbias, onehot, name="bias_grad").T
    dzc, dmkv = _mem_bwd(zc, do_c, mkv, name="mem_bwd")

    proj = ("w_in", "w_mem_kv")
    g_wi = [_mm(xb, dz, mode="tn", tm=min(512, D_MODEL), tn=dz.shape[1] if dz.shape[1] <= 1280 else 1024, tk=tk, name=nm)
            for dz, nm in ((dza, "grad_w_in_a"), (dzb, "grad_w_in_b"), (dzc, "grad_w_in_c"), (dzd, "grad_w_in_d"))]
    g_proj = [jnp.concatenate(g_wi, axis=1).reshape(D_MODEL, N_SHARDS, IN_COLS // N_SHARDS).transpose(1, 0, 2),
              _mm(memb, dmkv, mode="tn", tm=D_MODEL, tn=512, tk=MEM_LEN, name="grad_w_mem_kv", out_panels=True)]
    sums_proj = chip_sums(proj, g_proj, _run_exchanges(to_sibling(g_proj), name="reduce_sibling_proj")[0]) if distributed else []
    grad_x, landed = _dx_matmul([dza, dzb, dzc, dzd], [wia, wib, wic, wid], dpre1, tm=min(512, s), tks=[1024, W_B, 1024, 1024], name="grad_x",
                                exchanges=to_chips(sums_proj))
    if distributed:
        halves.update(shard_sums(proj, sums_proj, landed[0]))
    else:
        halves = dict(zip(ffn + merge + proj, g_ffn + g_merge + g_proj))
    small = dict(lb_logits=hg_stats[1:3], hg_norm_gain=hg_stats[0:1], swa_sinks=dsinks, rel_bias=d_rel_bias,
                 ln1_g=ln_stats[2:3], ln1_b=ln_stats[3:4], ln2_g=ln_stats[0:1], ln2_b=ln_stats[1:2], sq_err=ln_stats[4:5])
    return grad_x, halves, small


def _mesh_position():
    x, y, c = lax.axis_index("x"), lax.axis_index("y"), lax.axis_index("c")
    chips = [(1 - x, y), (x, 1 - y), (1 - x, 1 - y)]
    return x, y, c, chips


class _Exchange(NamedTuple):
    operands: list
    out_shapes: list
    n_sems: int
    start: Callable
    finish: Callable


def _gather_exchange(shards):
    n = len(shards)
    per = 7

    def plan(ins, outs, send_sems, recv_sems):
        x, y, c, chips = _mesh_position()
        me = 2 * x + y
        sibling = (x, y, 1 - c)

        def half(a, slot, hc):
            rh = shards[a].shape[0] // 2
            return outs[a].at[slot, pl.ds(hc * rh, rh), :]

        def copy(a, k, src, dst, to):
            return pltpu.make_async_remote_copy(src_ref=src, dst_ref=dst, send_sem=send_sems.at[a * per + k], recv_sem=recv_sems.at[a * per + k],
                                                device_id=to, device_id_type=MESH)

        own = [copy(a, 6, ins[a], outs[a].at[me], sibling) for a in range(n)]
        to_chips = [copy(a, k, ins[a].at[pl.ds(c * (shards[a].shape[0] // 2), shards[a].shape[0] // 2), :], half(a, me, c), (cx, cy, c))
                    for k, (cx, cy) in enumerate(chips) for a in range(n)]
        arrived = [copy(a, k, half(a, 2 * cx + cy, c), half(a, 2 * cx + cy, c), (cx, cy, c)) for k, (cx, cy) in enumerate(chips) for a in range(n)]
        passed_on = [copy(a, 3 + k, half(a, 2 * cx + cy, c), half(a, 2 * cx + cy, c), sibling) for k, (cx, cy) in enumerate(chips) for a in range(n)]
        from_sibling = [copy(a, 3 + k, half(a, 2 * cx + cy, 1 - c), half(a, 2 * cx + cy, 1 - c), sibling)
                        for k, (cx, cy) in enumerate(chips) for a in range(n)]
        own_arrived = [copy(a, 6, outs[a].at[me], outs[a].at[me], sibling) for a in range(n)]
        return own, to_chips, arrived, passed_on, from_sibling, own_arrived

    def start(*refs):
        own, to_chips, _, _, _, _ = plan(*refs)
        for cp in own + to_chips:
            cp.start()

    def finish(*refs):
        own, to_chips, arrived, passed_on, from_sibling, own_arrived = plan(*refs)
        for landed, onward in zip(arrived, passed_on):
            landed.wait_recv()
            onward.start()
        for cp in from_sibling + own_arrived:
            cp.wait_recv()
        for cp in own + to_chips + passed_on:
            cp.wait_send()

    return _Exchange(list(shards), [jax.ShapeDtypeStruct((N_SHARDS,) + w.shape, w.dtype) for w in shards], per * n, start, finish)


def _sibling_halves_exchange(grads):
    n = len(grads)

    def plan(ins, outs, send_sems, recv_sems):
        x, y, c, _ = _mesh_position()
        return [pltpu.make_async_remote_copy(src_ref=ins[a].at[:, pl.ds((1 - c) * (grads[a].shape[1] // 2), grads[a].shape[1] // 2), :],
                                             dst_ref=outs[a], send_sem=send_sems.at[a], recv_sem=recv_sems.at[a],
                                             device_id=(x, y, 1 - c), device_id_type=MESH) for a in range(n)]

    def start(*refs):
        for cp in plan(*refs):
            cp.start()

    def finish(*refs):
        for cp in plan(*refs):
            cp.wait()

    return _Exchange(list(grads), [jax.ShapeDtypeStruct((g.shape[0], g.shape[1] // 2, g.shape[2]), g.dtype) for g in grads], n, start, finish)


def _chip_partials_exchange(sums):
    n = len(sums)

    def plan(ins, outs, send_sems, recv_sems):
        _, _, c, chips = _mesh_position()
        return [pltpu.make_async_remote_copy(src_ref=ins[a].at[2 * cx + cy], dst_ref=outs[a].at[k], send_sem=send_sems.at[a * 3 + k],
                                             recv_sem=recv_sems.at[a * 3 + k], device_id=(cx, cy, c), device_id_type=MESH)
                for k, (cx, cy) in enumerate(chips) for a in range(n)]

    def start(*refs):
        for cp in plan(*refs):
            cp.start()

    def finish(*refs):
        for cp in plan(*refs):
            cp.wait()

    return _Exchange(list(sums), [jax.ShapeDtypeStruct((3,) + g.shape[1:], g.dtype) for g in sums], 3 * n, start, finish)


def _fused_call(body, *, name, grid, in_specs, out_specs, out_shape, scratch_shapes, operands, exchanges=()):
    single = not isinstance(out_shape, (tuple, list))
    out_specs = [out_specs] if single else list(out_specs)
    out_shape = [out_shape] if single else list(out_shape)
    n_in, n_out, n_scr = len(in_specs), len(out_specs), len(scratch_shapes)
    x_in = [len(e.operands) for e in exchanges]
    x_out = [len(e.out_shapes) for e in exchanges]

    def wrapped(*refs):
        refs = list(refs)
        ins = refs[:n_in]
        pos = n_in
        ex_ins = []
        for k in x_in:
            ex_ins.append(refs[pos:pos + k])
            pos += k
        outs = refs[pos:pos + n_out]
        pos += n_out
        ex_outs = []
        for k in x_out:
            ex_outs.append(refs[pos:pos + k])
            pos += k
        scratch = refs[pos:pos + n_scr]
        sems = refs[pos + n_scr:]
        first, last = None, None
        for axis, size in enumerate(grid):
            at_start, at_end = pl.program_id(axis) == 0, pl.program_id(axis) == size - 1
            first = at_start if first is None else first & at_start
            last = at_end if last is None else last & at_end

        @pl.when(first)
        def _():
            for i, e in enumerate(exchanges):
                e.start(ex_ins[i], ex_outs[i], sems[2 * i], sems[2 * i + 1])

        body(*ins, *outs, *scratch)

        @pl.when(last)
        def _():
            for i, e in enumerate(exchanges):
                e.finish(ex_ins[i], ex_outs[i], sems[2 * i], sems[2 * i + 1])

    n_x_in, n_x_out = sum(x_in), sum(x_out)
    results = pl.pallas_call(
        wrapped if exchanges else body, name=name, grid=grid,
        in_specs=list(in_specs) + [HBM] * n_x_in,
        out_specs=out_specs + [HBM] * n_x_out,
        out_shape=out_shape + [s for e in exchanges for s in e.out_shapes],
        scratch_shapes=list(scratch_shapes) + [pltpu.SemaphoreType.DMA((e.n_sems,)) for e in exchanges for _ in range(2)],
        compiler_params=_params(("arbitrary",) * len(grid)),
    )(*operands, *[a for e in exchanges for a in e.operands])
    own = results[0] if single else tuple(results[:n_out])
    landed, pos = [], n_out
    for k in x_out:
        landed.append(list(results[pos:pos + k]))
        pos += k
    return own, landed


def _run_exchanges(exchanges, *, name):
    def body(*refs):
        n_in = sum(len(e.operands) for e in exchanges)
        n_out = sum(len(e.out_shapes) for e in exchanges)
        ins, outs, sems = refs[:n_in], refs[n_in:n_in + n_out], refs[n_in + n_out:]
        spans, i, o = [], 0, 0
        for e in exchanges:
            spans.append((ins[i:i + len(e.operands)], outs[o:o + len(e.out_shapes)]))
            i, o = i + len(e.operands), o + len(e.out_shapes)
        for k, e in enumerate(exchanges):
            e.start(*spans[k], sems[2 * k], sems[2 * k + 1])
        for k, e in enumerate(exchanges):
            e.finish(*spans[k], sems[2 * k], sems[2 * k + 1])

    operands = [a for e in exchanges for a in e.operands]
    shapes = [s for e in exchanges for s in e.out_shapes]
    results = pl.pallas_call(
        body, name=name, out_shape=shapes, in_specs=[HBM] * len(operands), out_specs=[HBM] * len(shapes),
        scratch_shapes=[pltpu.SemaphoreType.DMA((e.n_sems,)) for e in exchanges for _ in range(2)],
    )(*operands)
    landed, pos = [], 0
    for e in exchanges:
        landed.append(list(results[pos:pos + len(e.out_shapes)]))
        pos += len(e.out_shapes)
    return landed


def _row_tile(rows):
    for tr in (256, 128, 64, 32, 16, 8):
        if rows % tr == 0:
            return tr
    raise ValueError(rows)


def _add_sibling(grad, other, pos, *, name):
    p, r, cols = grad.shape
    rh = r // 2
    tr = _row_tile(rh)
    nb = rh // tr

    def body(pos_ref, g_ref, o_ref, sb_ref, mine_ref):
        total = g_ref[...] + o_ref[...]
        sb_ref[...] = total.astype(BF16)

        @pl.when(pl.program_id(1) == pos_ref[0])
        def _():
            mine_ref[...] = total

    return pl.pallas_call(
        body, name=name, out_shape=(jax.ShapeDtypeStruct((p, rh, cols), BF16), jax.ShapeDtypeStruct((rh, cols), F32)),
        grid_spec=pltpu.PrefetchScalarGridSpec(
            num_scalar_prefetch=1, grid=(nb, p),
            in_specs=[pl.BlockSpec((None, tr, cols), lambda i, j, pos_ref: (j, pos_ref[1] * nb + i, 0)),
                      pl.BlockSpec((None, tr, cols), lambda i, j, pos_ref: (j, i, 0))],
            out_specs=(pl.BlockSpec((None, tr, cols), lambda i, j, pos_ref: (j, i, 0)),
                       pl.BlockSpec((tr, cols), lambda i, j, pos_ref: (i, 0)))),
        compiler_params=_params(("parallel", "arbitrary")),
    )(pos, grad, other)


def _add_chips(mine, others, pos, *, name):
    rh, cols = mine.shape
    tr = _row_tile(rh)
    nb = rh // tr

    def body(pos_ref, s_ref, o_ref, r_ref):
        r_ref[...] = ((s_ref[...] + o_ref[0].astype(F32)) + o_ref[1].astype(F32)) + o_ref[2].astype(F32)

    return pl.pallas_call(
        body, name=name, out_shape=jax.ShapeDtypeStruct((2 * rh, cols), F32),
        grid_spec=pltpu.PrefetchScalarGridSpec(
            num_scalar_prefetch=1, grid=(nb,),
            in_specs=[pl.BlockSpec((tr, cols), lambda i, pos_ref: (i, 0)),
                      pl.BlockSpec((3, tr, cols), lambda i, pos_ref: (0, i, 0))],
            out_specs=pl.BlockSpec((tr, cols), lambda i, pos_ref: (pos_ref[1] * nb + i, 0))),
        compiler_params=_params(("parallel",)),
    )(pos, mine, others)


def _join_halves(bufs, *, name):
    n = len(bufs)

    def body(*refs):
        ins, outs = refs[:n], refs[n:2 * n]
        send_sems, recv_sems = refs[2 * n:]
        x, y, c, _ = _mesh_position()

        def copy(a, hc):
            rh = bufs[a].shape[0] // 2
            rows = pl.ds(hc * rh, rh)
            return pltpu.make_async_remote_copy(src_ref=ins[a].at[rows, :], dst_ref=outs[a].at[rows, :], send_sem=send_sems.at[a],
                                                recv_sem=recv_sems.at[a], device_id=(x, y, 1 - c), device_id_type=MESH)

        for a in range(n):
            copy(a, c).start()
        for a in range(n):
            copy(a, c).wait_send()
            copy(a, 1 - c).wait_recv()

    return pl.pallas_call(
        body, name=name, out_shape=[jax.ShapeDtypeStruct(b.shape, b.dtype) for b in bufs],
        in_specs=[HBM] * n, out_specs=[HBM] * n, input_output_aliases={a: a for a in range(n)},
        scratch_shapes=[pltpu.SemaphoreType.DMA((n,)), pltpu.SemaphoreType.DMA((n,))],
    )(*bufs)


def _all_reduce_small(packed, *, name):
    rows, cols = packed.shape

    def body(in_ref, out_ref, gathered, send_sems, recv_sems):
        x, y, c, _ = _mesh_position()
        me = 4 * x + 2 * y + c
        gathered[me] = in_ref[...]
        copies = []
        for d in range(1, 8):
            dx, dy, dc = (d >> 2) & 1, (d >> 1) & 1, d & 1
            peer = (x ^ dx, y ^ dy, c ^ dc)
            cp = pltpu.make_async_remote_copy(src_ref=in_ref, dst_ref=gathered.at[me], send_sem=send_sems.at[d - 1], recv_sem=recv_sems.at[d - 1],
                                              device_id=peer, device_id_type=MESH)
            cp.start()
            copies.append(cp)
        for cp in copies:
            cp.wait()
        total = gathered[0]
        for j in range(1, 8):
            total = total + gathered[j]
        out_ref[...] = total

    vm = pl.BlockSpec(memory_space=pltpu.VMEM)
    return pl.pallas_call(
        body, name=name, out_shape=jax.ShapeDtypeStruct((rows, cols), F32), in_specs=[vm], out_specs=vm,
        scratch_shapes=[pltpu.VMEM((8, rows, cols), F32), pltpu.SemaphoreType.DMA((7,)), pltpu.SemaphoreType.DMA((7,))],
    )(packed)


def _adamw_math(w, g, m, v):
    m = ADAM_B1 * m + (1.0 - ADAM_B1) * g
    v = ADAM_B2 * v + (1.0 - ADAM_B2) * (g * g)
    m_hat = m / (1.0 - ADAM_B1 ** ADAM_STEP)
    v_hat = v / (1.0 - ADAM_B2 ** ADAM_STEP)
    delta = -ADAM_LR * (m_hat / (jnp.sqrt(v_hat) + ADAM_EPS) + ADAM_WD * w)
    return delta, m, v


def _adamw(w, g, m, v, *, name):
    rows, cols = w.shape
    tr = _row_tile(rows)
    blk = pl.BlockSpec((tr, cols), lambda i: (i, 0))

    def body(w_ref, g_ref, m_ref, v_ref, d_ref, nm_ref, nv_ref):
        d_ref[...], nm_ref[...], nv_ref[...] = _adamw_math(w_ref[...], g_ref[...], m_ref[...], v_ref[...])

    shape = jax.ShapeDtypeStruct((rows, cols), F32)
    return pl.pallas_call(body, name=name, grid=(rows // tr,), out_shape=(shape, shape, shape), in_specs=[blk] * 4, out_specs=(blk,) * 3,
                          compiler_params=_params(("parallel",)))(w, g, m, v)


def _adamw_small(w, g, m, v, *, name):
    def body(w_ref, g_ref, m_ref, v_ref, d_ref, nm_ref, nv_ref, loss_ref):
        d_ref[...], nm_ref[...], nv_ref[...] = _adamw_math(w_ref[...], g_ref[...], m_ref[...], v_ref[...])
        loss_ref[...] = (0.5 / D_MODEL) * jnp.sum(g_ref[8:9, :], axis=1, keepdims=True)

    shape = jax.ShapeDtypeStruct(w.shape, F32)
    return pl.pallas_call(body, name=name, out_shape=(shape, shape, shape, jax.ShapeDtypeStruct((1, 1), F32)),
                          compiler_params=_params())(w, g, m, v)


SMALL_ROWS = 16


def _pack_small(lb_logits, gain, sinks, rel_bias, ln1_g, ln1_b, ln2_g, ln2_b, extra=None):
    misc = jnp.concatenate([sinks.reshape(1, -1), rel_bias.reshape(1, -1)], axis=1)
    misc = jnp.pad(misc, ((0, 0), (0, D_MODEL - misc.shape[1])))
    rows = [lb_logits, gain, ln1_g, ln1_b, ln2_g, ln2_b, misc, extra if extra is not None else jnp.zeros((1, D_MODEL), F32)]
    used = sum(r.shape[0] for r in rows)
    return jnp.concatenate(rows + [jnp.zeros((SMALL_ROWS - used, D_MODEL), F32)], axis=0)


def _unpack_small(p):
    return dict(lb_logits=p[0:2], hg_norm_gain=p[2:3], ln1_g=p[3:4], ln1_b=p[4:5], ln2_g=p[5:6], ln2_b=p[6:7],
                swa_sinks=p[7:8, 0:SWA_HEADS], rel_bias=p[7:8, SWA_HEADS:SWA_HEADS + NUM_BUCKETS * SWA_HEADS].reshape(NUM_BUCKETS, SWA_HEADS))


WEIGHTS = ["w_in", "lb_logits", "hg_norm_gain", "swa_sinks", "rel_bias", "w_mem_kv", "w_branch_hg", "w_branch_swa", "w_branch_mem",
           "w_out", "ln1_g", "ln1_b", "w_up", "w_down", "ln2_g", "ln2_b"]
BIG = ["w_in", "w_mem_kv", "w_branch_hg", "w_branch_swa", "w_branch_mem", "w_out", "w_up", "w_down"]
SMALL = ["lb_logits", "hg_norm_gain", "swa_sinks", "rel_bias", "ln1_g", "ln1_b", "ln2_g", "ln2_b"]


def kernel(x, mem, w_in, lb_logits, hg_norm_gain, swa_sinks, rel_bias, w_mem_kv, w_branch_hg, w_branch_swa, w_branch_mem, w_out, ln1_g, ln1_b, w_up, w_down, ln2_g, ln2_b, loss_target, m_w_in, m_lb_logits, m_hg_norm_gain, m_swa_sinks, m_rel_bias, m_w_mem_kv, m_w_branch_hg, m_w_branch_swa, m_w_branch_mem, m_w_out, m_ln1_g, m_ln1_b, m_w_up, m_w_down, m_ln2_g, m_ln2_b, v_w_in, v_lb_logits, v_hg_norm_gain, v_swa_sinks, v_rel_bias, v_w_mem_kv, v_w_branch_hg, v_w_branch_swa, v_w_branch_mem, v_w_out, v_ln1_g, v_ln1_b, v_w_up, v_w_down, v_ln2_g, v_ln2_b):
    w = dict(w_in=w_in, lb_logits=lb_logits, hg_norm_gain=hg_norm_gain, swa_sinks=swa_sinks, rel_bias=rel_bias, w_mem_kv=w_mem_kv,
             w_branch_hg=w_branch_hg, w_branch_swa=w_branch_swa, w_branch_mem=w_branch_mem, w_out=w_out, ln1_g=ln1_g, ln1_b=ln1_b,
             w_up=w_up, w_down=w_down, ln2_g=ln2_g, ln2_b=ln2_b)
    m = dict(w_in=m_w_in, lb_logits=m_lb_logits, hg_norm_gain=m_hg_norm_gain, swa_sinks=m_swa_sinks, rel_bias=m_rel_bias, w_mem_kv=m_w_mem_kv,
             w_branch_hg=m_w_branch_hg, w_branch_swa=m_w_branch_swa, w_branch_mem=m_w_branch_mem, w_out=m_w_out, ln1_g=m_ln1_g, ln1_b=m_ln1_b,
             w_up=m_w_up, w_down=m_w_down, ln2_g=m_ln2_g, ln2_b=m_ln2_b)
    v = dict(w_in=v_w_in, lb_logits=v_lb_logits, hg_norm_gain=v_hg_norm_gain, swa_sinks=v_swa_sinks, rel_bias=v_rel_bias, w_mem_kv=v_w_mem_kv,
             w_branch_hg=v_w_branch_hg, w_branch_swa=v_w_branch_swa, w_branch_mem=v_w_branch_mem, w_out=v_w_out, ln1_g=v_ln1_g, ln1_b=v_ln1_b,
             w_up=v_w_up, w_down=v_w_down, ln2_g=v_ln2_g, ln2_b=v_ln2_b)
    shapes = {k: w[k].shape for k in WEIGHTS}
    for d in (w, m, v):
        for k in BIG:
            d[k] = d[k].reshape(d[k].shape[-2], d[k].shape[-1])

    shards = {k: w[k].astype(BF16) for k in BIG}
    wi4, wmkv = _run_exchanges([_gather_exchange([shards["w_in"], shards["w_mem_kv"]])], name="gather_weights")[0]
    wi = wi4.transpose(1, 0, 2).reshape(D_MODEL, IN_COLS)
    wi_parts = (wi[:, 0:W_A], wi[:, W_A:W_A + W_B], wi[:, W_A + W_B:W_A + W_B + W_C], wi[:, W_A + W_B + W_C:])

    grad_x, halves, small = _local_step(
        x.reshape(x.shape[-2], D_MODEL), mem.reshape(MEM_LEN, D_MODEL), loss_target.reshape(loss_target.shape[-2], D_MODEL),
        wi_parts, wmkv, shards, lb_logits, hg_norm_gain, swa_sinks, rel_bias, ln1_g, ln1_b, ln2_g, ln2_b, distributed=True)

    reduced = dict(zip(BIG, _join_halves([halves[k] for k in BIG], name="join_halves")))

    packed_g = _pack_small(small["lb_logits"], small["hg_norm_gain"], small["swa_sinks"], small["rel_bias"], small["ln1_g"], small["ln1_b"],
                           small["ln2_g"], small["ln2_b"], extra=small["sq_err"])
    packed_g = _all_reduce_small(packed_g, name="reduce_small")

    grad_out, delta_out, m_out, v_out = {}, {}, {}, {}
    for k in BIG:
        d_, m_, v_ = _adamw(w[k], reduced[k], m[k], v[k], name="adamw_" + k)
        grad_out[k], delta_out[k], m_out[k], v_out[k] = reduced[k], d_, m_, v_
    pack = lambda d: _pack_small(d["lb_logits"], d["hg_norm_gain"], d["swa_sinks"], d["rel_bias"], d["ln1_g"], d["ln1_b"], d["ln2_g"], d["ln2_b"])
    d_s, m_s, v_s, loss = _adamw_small(pack(w), packed_g, pack(m), pack(v), name="adamw_small")
    for out, p in ((grad_out, packed_g), (delta_out, d_s), (m_out, m_s), (v_out, v_s)):
        out.update(_unpack_small(p))

    result = [loss.reshape(()), grad_x.reshape(x.shape)]
    for out in (grad_out, delta_out, m_out, v_out):
        result += [out[k].reshape(shapes[k]) for k in WEIGHTS]
    return tuple(result)
```

```python
import math
from typing import Callable, NamedTuple

import jax
import jax.numpy as jnp
from jax import lax
from jax.experimental import pallas as pl
from jax.experimental.pallas import tpu as pltpu

F32 = jnp.float32
BF16 = jnp.bfloat16
HIGHEST = lax.Precision.HIGHEST
MESH = pl.DeviceIdType.MESH

D_MODEL = 1024
MEM_LEN = 256
HG_HEADS = 8
HG_DK = 128
HG_CHUNK = 64
SWA_HEADS = 16
SWA_KV_HEADS = 2
SWA_GROUP = 8
SWA_HEAD_DIM = 64
SWA_BLOCK = 128
SWA_WINDOW = 128
MEM_HEADS = 4
MEM_HEAD_DIM = 256
NUM_BUCKETS = 32
MAX_DISTANCE = 128
D_FF = 4096
LN_EPS = 1e-5
RMS_EPS = 1e-6
ALPHA = 2.0 ** 0.25
W_A, W_B, W_C, W_D = 4096, 1280, 1024, 3072
IN_COLS = W_A + W_B + W_C + W_D
N_SHARDS = 4
ADAM_LR = 0.001
ADAM_B1 = 0.9
ADAM_B2 = 0.999
ADAM_EPS = 1e-08
ADAM_WD = 0.01
ADAM_STEP = 10
MASK_VALUE = -1e30
VMEM_LIMIT = 56 * 1024 * 1024

NN = ((1,), (0,))
NT = ((1,), (1,))
TN = ((0,), (0,))
HBM = pl.BlockSpec(memory_space=pltpu.HBM)


def _dot(a, b, dims=NN, precision=None):
    return lax.dot_general(a, b, (dims, ((), ())), precision=precision, preferred_element_type=F32)


def _params(sem=None):
    return pltpu.CompilerParams(dimension_semantics=sem, vmem_limit_bytes=VMEM_LIMIT)


def _resident(shape):
    zeros = (0,) * len(shape)
    return pl.BlockSpec(shape, lambda *_: zeros, pipeline_mode=pl.Buffered(1))


def _mm(a, b, *, mode, tm, tn, tk, name, out_dtype=F32, b_panels=False, out_panels=False, add=None, add_scale=1.0):
    if mode == "tn":
        kdim, m = a.shape
    else:
        m, kdim = a.shape
    if b_panels:
        n = b.shape[0] * b.shape[2]
        assert b.shape[2] == tn and mode == "nn"
    elif mode == "nt":
        n = b.shape[0]
    else:
        n = b.shape[1]
    assert m % tm == 0 and n % tn == 0 and kdim % tk == 0, (name, m, n, kdim)
    nk = kdim // tk
    dims = {"nn": NN, "nt": NT, "tn": TN}[mode]
    a_spec = pl.BlockSpec((tk, tm), lambda i, j, k: (k, i)) if mode == "tn" else pl.BlockSpec((tm, tk), lambda i, j, k: (i, k))
    if b_panels:
        b_spec = pl.BlockSpec((None, tk, tn), lambda i, j, k: (j, k, 0))
    elif mode == "nt":
        b_spec = pl.BlockSpec((tn, tk), lambda i, j, k: (j, k))
    else:
        b_spec = pl.BlockSpec((tk, tn), lambda i, j, k: (k, j))
    if out_panels:
        out_shape = jax.ShapeDtypeStruct((n // tn, m, tn), out_dtype)
        o_spec = pl.BlockSpec((None, tm, tn), lambda i, j, k: (j, i, 0))
    else:
        out_shape = jax.ShapeDtypeStruct((m, n), out_dtype)
        o_spec = pl.BlockSpec((tm, tn), lambda i, j, k: (i, j))
    in_specs = [a_spec, b_spec]
    operands = [a, b]
    if add is not None:
        in_specs.append(pl.BlockSpec((tm, tn), lambda i, j, k: (i, j)))
        operands.append(add)

    def body(*refs):
        a_ref, b_ref = refs[0], refs[1]
        add_ref = refs[2] if add is not None else None
        o_ref = refs[3] if add is not None else refs[2]
        part = _dot(a_ref[...].astype(BF16), b_ref[...].astype(BF16), dims)

        def finish(acc):
            if add_ref is not None:
                acc = acc + add_scale * add_ref[...]
            o_ref[...] = acc.astype(out_dtype)

        if nk == 1:
            finish(part)
        else:
            acc_ref = refs[-1]
            k = pl.program_id(2)

            @pl.when(k == 0)
            def _():
                acc_ref[...] = part

            @pl.when(k > 0)
            def _():
                acc_ref[...] += part

            @pl.when(k == nk - 1)
            def _():
                finish(acc_ref[...])

    return pl.pallas_call(
        body, name=name, out_shape=out_shape, grid=(m // tm, n // tn, nk), in_specs=in_specs, out_specs=o_spec,
        scratch_shapes=[pltpu.VMEM((tm, tn), F32)] if nk > 1 else [],
        compiler_params=_params(("parallel", "parallel", "arbitrary")),
    )(*operands)


def _dx_matmul(dzs, wis, resid, *, tm, name, exchanges=()):
    s = resid.shape[0]
    npieces = len(dzs)
    in_specs = [pl.BlockSpec((tm, dz.shape[1]), lambda i: (i, 0)) for dz in dzs]
    in_specs += [_resident(w.shape) for w in wis]
    in_specs += [pl.BlockSpec((tm, D_MODEL), lambda i: (i, 0))]

    def body(*refs):
        dz_refs, w_refs = refs[:npieces], refs[npieces:2 * npieces]
        r_ref, o_ref = refs[2 * npieces], refs[2 * npieces + 1]
        total = ALPHA * r_ref[...]
        for p in range(npieces):
            total = total + _dot(dz_refs[p][...], w_refs[p][...], NT)
        o_ref[...] = total

    return _fused_call(
        body, name=name, out_shape=jax.ShapeDtypeStruct((s, D_MODEL), F32), grid=(s // tm,), in_specs=in_specs,
        out_specs=pl.BlockSpec((tm, D_MODEL), lambda i: (i, 0)), scratch_shapes=[],
        operands=[*dzs, *wis, resid], exchanges=exchanges)


def _lower_bound(lbl_ref):
    l0, l1 = lbl_ref[0:1, :], lbl_ref[1:2, :]
    mx = jnp.maximum(l0, l1)
    e0, e1 = jnp.exp(l0 - mx), jnp.exp(l1 - mx)
    return e0 / (e0 + e1)


HEAD_COLS = [slice(h * HG_DK, (h + 1) * HG_DK) for h in range(HG_HEADS)]


def _head_mean(x):
    return jnp.concatenate([jnp.broadcast_to(jnp.mean(x[:, c], axis=-1, keepdims=True), (x.shape[0], HG_DK)) for c in HEAD_COLS], axis=1)


def _chunk_forward(q, fl, v, lb, tril_f):
    sg = jax.nn.sigmoid(fl)
    f = lb + (1.0 - lb) * sg
    k = 1.0 - f
    b = _dot(tril_f, jnp.log(f), NN, HIGHEST)
    b_last = b[HG_CHUNK - 1:HG_CHUNK, :]
    eb, enb, eo = jnp.exp(b), jnp.exp(-b), jnp.exp(b_last - b)
    return sg, f, k, b_last, eb, enb, eo, q * eb, k * enb, k * eo


def _hgrn_fwd(za, lb_logits, gain, *, name, exchanges=()):
    s = za.shape[0]
    t = min(256, s)
    ncs = t // HG_CHUNK

    def body(z_ref, lbl_ref, gain_ref, oa_ref, oraw_ref, st_ref, state):
        @pl.when(pl.program_id(0) == 0)
        def _():
            state[...] = jnp.zeros_like(state)

        lb_all = _lower_bound(lbl_ref)
        row = lax.broadcasted_iota(jnp.int32, (HG_CHUNK, HG_CHUNK), 0)
        col = lax.broadcasted_iota(jnp.int32, (HG_CHUNK, HG_CHUNK), 1)
        tril = row >= col
        tril_f = tril.astype(F32)
        gain_all = gain_ref[...]

        def chunk(i, carry):
            r = pl.ds(pl.multiple_of(i * HG_CHUNK, HG_CHUNK), HG_CHUNK)
            q, fl, v, hg = (z_ref[r, j * D_MODEL:(j + 1) * D_MODEL] for j in range(4))
            _, _, _, b_last, _, _, _, q_in, k_in, k_out = _chunk_forward(q, fl, v, lb_all, tril_f)
            q_in_b, k_in_b, k_out_b, vb = (u.astype(BF16) for u in (q_in, k_in, k_out, v))
            decay = jnp.exp(b_last)
            sts = [state[h] for h in range(HG_HEADS)]
            attn = [_dot(q_in_b[:, c], k_in_b[:, c], NT) for c in HEAD_COLS]
            inter = [_dot(q_in_b[:, c], sts[h].astype(BF16), NT) for h, c in enumerate(HEAD_COLS)]
            upd = [_dot(vb[:, c], k_out_b[:, c], TN) for c in HEAD_COLS]
            attn = [jnp.where(tril, a, 0.0).astype(BF16) for a in attn]
            outs = [_dot(attn[h], vb[:, c], NN) + inter[h] for h, c in enumerate(HEAD_COLS)]
            for h, c in enumerate(HEAD_COLS):
                st_ref[h, i] = sts[h]
                state[h] = sts[h] * decay[:, c] + upd[h]
            o = jnp.concatenate(outs, axis=1)
            oraw_ref[r, :] = o
            n = o * lax.rsqrt(_head_mean(o * o) + RMS_EPS)
            oa_ref[r, :] = (n * gain_all * (hg * jax.nn.sigmoid(hg))).astype(BF16)
            return carry

        lax.fori_loop(0, ncs, chunk, 0)

    return _fused_call(
        body, name=name, grid=(s // t,),
        out_shape=(jax.ShapeDtypeStruct((s, D_MODEL), BF16), jax.ShapeDtypeStruct((s, D_MODEL), F32),
                   jax.ShapeDtypeStruct((HG_HEADS, s // HG_CHUNK, HG_DK, HG_DK), F32)),
        in_specs=[pl.BlockSpec((t, W_A), lambda i: (i, 0)), _resident((2, D_MODEL)), _resident((1, D_MODEL))],
        out_specs=(pl.BlockSpec((t, D_MODEL), lambda i: (i, 0)), pl.BlockSpec((t, D_MODEL), lambda i: (i, 0)),
                   pl.BlockSpec((HG_HEADS, ncs, HG_DK, HG_DK), lambda i: (0, i, 0, 0))),
        scratch_shapes=[pltpu.VMEM((HG_HEADS, HG_DK, HG_DK), F32)],
        operands=[za, lb_logits, gain], exchanges=exchanges)


def _hgrn_bwd(za, oraw, do_a, states, lb_logits, gain, *, name, exchanges=()):
    s = za.shape[0]
    t = min(256, s)
    ncs = t // HG_CHUNK
    nt = s // t

    def body(z_ref, oraw_ref, do_ref, st_ref, lbl_ref, gain_ref, dz_ref, stats_ref, dstate):
        step = pl.program_id(0)

        @pl.when(step == 0)
        def _():
            dstate[...] = jnp.zeros_like(dstate)
            stats_ref[...] = jnp.zeros_like(stats_ref)

        lb_all = _lower_bound(lbl_ref)
        row = lax.broadcasted_iota(jnp.int32, (HG_CHUNK, HG_CHUNK), 0)
        col = lax.broadcasted_iota(jnp.int32, (HG_CHUNK, HG_CHUNK), 1)
        tril = row >= col
        tril_f = tril.astype(F32)
        triu_f = (row <= col).astype(F32)
        gain_all = gain_ref[...]

        def chunk(ii, carry):
            i = ncs - 1 - ii
            r = pl.ds(pl.multiple_of(i * HG_CHUNK, HG_CHUNK), HG_CHUNK)
            q, fl, v, hg = (z_ref[r, j * D_MODEL:(j + 1) * D_MODEL] for j in range(4))
            o = oraw_ref[r, :]
            doa = do_ref[r, :]
            rms = lax.rsqrt(_head_mean(o * o) + RMS_EPS)
            n = o * rms
            sgg = jax.nn.sigmoid(hg)
            silu = hg * sgg
            dhg = doa * n * gain_all * (sgg * (1.0 + hg * (1.0 - sgg)))
            dgain = jnp.sum(doa * n * silu, axis=0, keepdims=True)
            dn = doa * gain_all * silu
            do = rms * (dn - n * _head_mean(dn * n))
            sg, f, k, b_last, eb, enb, eo, q_in, k_in, k_out = _chunk_forward(q, fl, v, lb_all, tril_f)
            q_in_b, k_in_b, k_out_b, vb, dob = (u.astype(BF16) for u in (q_in, k_in, k_out, v, do))
            decay = jnp.exp(b_last)
            sts = [st_ref[h, i] for h in range(HG_HEADS)]
            dsts = [dstate[h] for h in range(HG_HEADS)]
            dsts_b = [d.astype(BF16) for d in dsts]
            heads = list(enumerate(HEAD_COLS))
            attn = [_dot(q_in_b[:, c], k_in_b[:, c], NT) for h, c in heads]
            dattn = [_dot(dob[:, c], vb[:, c], NT) for h, c in heads]
            dq_st = [_dot(dob[:, c], sts[h].astype(BF16), NN) for h, c in heads]
            dk_out = [_dot(vb[:, c], dsts_b[h], NN) for h, c in heads]
            dv_st = [_dot(k_out_b[:, c], dsts_b[h], NT) for h, c in heads]
            dst_o = [_dot(dob[:, c], q_in_b[:, c], TN) for h, c in heads]
            attn = [jnp.where(tril, a, 0.0).astype(BF16) for a in attn]
            dattn = [jnp.where(tril, a, 0.0).astype(BF16) for a in dattn]
            dq_in = jnp.concatenate([_dot(dattn[h], k_in_b[:, c], NN) + dq_st[h] for h, c in heads], axis=1)
            dk_in = jnp.concatenate([_dot(dattn[h], q_in_b[:, c], TN) for h, c in heads], axis=1)
            dv = jnp.concatenate([_dot(attn[h], dob[:, c], TN) + dv_st[h] for h, c in heads], axis=1)
            dk_out = jnp.concatenate(dk_out, axis=1)
            dst_st = jnp.concatenate([jnp.sum(dsts[h] * sts[h], axis=0, keepdims=True) for h in range(HG_HEADS)], axis=1)
            for h, c in heads:
                dstate[h] = dsts[h] * decay[:, c] + dst_o[h]
            db_last = decay * dst_st + jnp.sum(dk_out * k_out, axis=0, keepdims=True)
            db = dq_in * q_in - dk_in * k_in - dk_out * k_out
            dg = _dot(triu_f, db, NN, HIGHEST) + db_last
            dk = dk_in * enb + dk_out * eo
            df = dg / f - dk
            stats_ref[0:1, :] += dgain
            stats_ref[1:2, :] += jnp.sum(df * (1.0 - sg), axis=0, keepdims=True)
            dz_ref[r, 0:1024] = (dq_in * eb).astype(BF16)
            dz_ref[r, 1024:2048] = (df * (1.0 - lb_all) * sg * (1.0 - sg)).astype(BF16)
            dz_ref[r, 2048:3072] = dv.astype(BF16)
            dz_ref[r, 3072:4096] = dhg.astype(BF16)
            return carry

        lax.fori_loop(0, ncs, chunk, 0)

        @pl.when(step == nt - 1)
        def _():
            dl0 = stats_ref[1:2, :] * lb_all * (1.0 - lb_all)
            stats_ref[1:2, :] = dl0
            stats_ref[2:3, :] = -dl0

    rev = lambda i: (nt - 1 - i, 0)
    return _fused_call(
        body, name=name, grid=(nt,),
        out_shape=(jax.ShapeDtypeStruct((s, W_A), BF16), jax.ShapeDtypeStruct((8, D_MODEL), F32)),
        in_specs=[pl.BlockSpec((t, W_A), rev), pl.BlockSpec((t, D_MODEL), rev), pl.BlockSpec((t, D_MODEL), rev),
                  pl.BlockSpec((HG_HEADS, ncs, HG_DK, HG_DK), lambda i: (0, nt - 1 - i, 0, 0)),
                  _resident((2, D_MODEL)), _resident((1, D_MODEL))],
        out_specs=(pl.BlockSpec((t, W_A), rev), pl.BlockSpec((8, D_MODEL), lambda i: (0, 0))),
        scratch_shapes=[pltpu.VMEM((HG_HEADS, HG_DK, HG_DK), F32)],
        operands=[za, oraw, do_a, states, lb_logits, gain], exchanges=exchanges)


def _t5_bucket(n):
    max_exact = NUM_BUCKETS // 2
    nf = jnp.maximum(n, 1).astype(F32)
    large = max_exact + (jnp.log(nf / max_exact) / math.log(MAX_DISTANCE / max_exact) * (NUM_BUCKETS - max_exact)).astype(jnp.int32)
    large = jnp.minimum(large, NUM_BUCKETS - 1)
    return jnp.where(n < max_exact, n, large)


def _bias_selector():
    qi = jnp.arange(SWA_BLOCK)[:, None] + SWA_BLOCK
    kj = jnp.arange(2 * SWA_BLOCK)[None, :]
    dist = qi - kj
    band = ((dist >= 0) & (dist < SWA_WINDOW)).reshape(1, -1)
    bucket = _t5_bucket(jnp.clip(dist, 0, SWA_WINDOW - 1)).reshape(1, -1)
    onehot = ((bucket == jnp.arange(NUM_BUCKETS)[:, None]) & band).astype(F32)
    return onehot, jnp.where(band, 0.0, MASK_VALUE).astype(F32)


def _bias_table(rel_bias_t, onehot, maskrow, *, name):
    def body(rb_ref, oh_ref, mask_ref, o_ref):
        o_ref[...] = _dot(rb_ref[...], oh_ref[...], NN, HIGHEST) + mask_ref[...]

    return pl.pallas_call(body, name=name, out_shape=jax.ShapeDtypeStruct((SWA_HEADS, onehot.shape[1]), F32),
                          compiler_params=_params())(rel_bias_t, onehot, maskrow)


def _bias_grad(dbias2d, onehot, *, name):
    def body(db_ref, oh_ref, o_ref):
        o_ref[...] = _dot(db_ref[...], oh_ref[...], NT, HIGHEST)

    return pl.pallas_call(body, name=name, out_shape=jax.ShapeDtypeStruct((SWA_HEADS, NUM_BUCKETS), F32),
                          compiler_params=_params())(dbias2d, onehot)


GROUP_LANES = SWA_GROUP * SWA_BLOCK


def _swa_operands(zq_ref, kv_cur_ref, kv_prev_ref):
    q = (zq_ref[:, 0:1024] * (SWA_HEAD_DIM ** -0.5)).astype(BF16)
    kv_c = kv_cur_ref[...].astype(BF16)
    kv_p = kv_prev_ref[...].astype(BF16)
    kks = [jnp.concatenate([kv_p[:, g * 64:(g + 1) * 64], kv_c[:, g * 64:(g + 1) * 64]], axis=0) for g in range(SWA_KV_HEADS)]
    vvs = [jnp.concatenate([kv_p[:, 128 + g * 64:128 + (g + 1) * 64], kv_c[:, 128 + g * 64:128 + (g + 1) * 64]], axis=0)
           for g in range(SWA_KV_HEADS)]
    return q, kks, vvs


def _stack_heads(x, g):
    return jnp.concatenate([x[:, h * SWA_HEAD_DIM:(h + 1) * SWA_HEAD_DIM] for h in range(g * SWA_GROUP, (g + 1) * SWA_GROUP)], axis=0)


def _heads_to_lanes(xt):
    pairs = []
    for j in range(0, SWA_GROUP, 2):
        two = jnp.concatenate([xt[:, j * SWA_BLOCK:(j + 1) * SWA_BLOCK], xt[:, (j + 1) * SWA_BLOCK:(j + 2) * SWA_BLOCK]], axis=0)
        pairs.append(two.T)
    return jnp.concatenate(pairs, axis=1)


def _swa_softmax(score_t, bias_ref, sink_ref, g):
    lanes = slice(g * GROUP_LANES, (g + 1) * GROUP_LANES)
    sc = score_t + bias_ref[:, lanes]
    sink = sink_ref[:, lanes]
    m = jnp.maximum(jnp.max(sc, axis=0, keepdims=True), sink)
    e = jnp.exp(sc - m)
    e_sink = jnp.exp(sink - m)
    return e, 1.0 / (jnp.sum(e, axis=0, keepdims=True) + e_sink), e_sink


def _swa_tables(bias2d, sinks):
    bias_t = bias2d.reshape(SWA_HEADS, SWA_BLOCK, 2 * SWA_BLOCK).transpose(2, 0, 1).reshape(2 * SWA_BLOCK, SWA_HEADS * SWA_BLOCK)
    first = jnp.where(jnp.arange(2 * SWA_BLOCK)[:, None] < SWA_BLOCK, MASK_VALUE, bias_t)
    return jnp.stack([first, bias_t]), jnp.repeat(sinks, SWA_BLOCK, axis=1)


def _swa_fwd(zb, bias_tables, sink_lanes, *, name, exchanges=()):
    s = zb.shape[0]
    nb = s // SWA_BLOCK

    def body(zq_ref, kvc_ref, kvp_ref, bias_ref, sink_ref, o_ref):
        q, kks, vvs = _swa_operands(zq_ref, kvc_ref, kvp_ref)
        groups = range(SWA_KV_HEADS)
        scores = [_dot(kks[g], _stack_heads(q, g), NT) for g in groups]
        probs = []
        for g in groups:
            e, inv, _ = _swa_softmax(scores[g], bias_ref, sink_ref, g)
            probs.append((e * inv).astype(BF16))
        outs = [_dot(vvs[g], probs[g], TN) for g in groups]
        o_ref[...] = jnp.concatenate([_heads_to_lanes(outs[g]) for g in groups], axis=1).astype(BF16)

    return _fused_call(
        body, name=name, grid=(nb,), out_shape=jax.ShapeDtypeStruct((s, D_MODEL), BF16),
        in_specs=[pl.BlockSpec((SWA_BLOCK, W_B), lambda n: (n, 0)),
                  pl.BlockSpec((SWA_BLOCK, 256), lambda n: (n, 4)),
                  pl.BlockSpec((SWA_BLOCK, 256), lambda n: (jnp.maximum(n - 1, 0), 4)),
                  pl.BlockSpec((None, 2 * SWA_BLOCK, SWA_HEADS * SWA_BLOCK), lambda n: (jnp.minimum(n, 1), 0, 0)),
                  _resident((1, SWA_HEADS * SWA_BLOCK))],
        out_specs=pl.BlockSpec((SWA_BLOCK, D_MODEL), lambda n: (n, 0)), scratch_shapes=[],
        operands=[zb, zb, zb, bias_tables, sink_lanes], exchanges=exchanges)


def _swa_bwd(zb, do_b, bias_tables, sink_lanes, *, name, exchanges=()):
    s = zb.shape[0]
    nb = s // SWA_BLOCK
    scale = SWA_HEAD_DIM ** -0.5

    def body(zq_ref, kvc_ref, kvp_ref, do_ref, bias_ref, sink_ref, dz_ref, dbias_ref, dsink_ref, carry, dsink_acc):
        step = pl.program_id(0)

        @pl.when(step == 0)
        def _():
            carry[...] = jnp.zeros_like(carry)
            dsink_acc[...] = jnp.zeros_like(dsink_acc)
            dbias_ref[...] = jnp.zeros_like(dbias_ref)

        q, kks, vvs = _swa_operands(zq_ref, kvc_ref, kvp_ref)
        groups = range(SWA_KV_HEADS)
        do = do_ref[...].astype(BF16)
        q_rows = [_stack_heads(q, g) for g in groups]
        do_rows = [_stack_heads(do, g) for g in groups]
        scores = [_dot(kks[g], q_rows[g], NT) for g in groups]
        dps = [_dot(vvs[g], do_rows[g], NT) for g in groups]
        ps, dss = [], []
        for g in groups:
            lanes = slice(g * GROUP_LANES, (g + 1) * GROUP_LANES)
            e, inv, e_sink = _swa_softmax(scores[g], bias_ref, sink_ref, g)
            p = e * inv
            delta = jnp.sum(p * dps[g], axis=0, keepdims=True)
            ds = p * (dps[g] - delta)
            dbias_ref[:, lanes] += ds
            dsink_acc[:, lanes] -= e_sink * inv * delta
            ps.append(p.astype(BF16))
            dss.append(ds.astype(BF16))
        dqs = [_dot(kks[g], dss[g], TN) * scale for g in groups]
        dkks = [_dot(dss[g], q_rows[g], NN) for g in groups]
        dvvs = [_dot(ps[g], do_rows[g], NN) for g in groups]
        dkv = jnp.concatenate(dkks + dvvs, axis=1)
        dz_ref[:, 0:1024] = jnp.concatenate([_heads_to_lanes(dqs[g]) for g in groups], axis=1).astype(BF16)
        dz_ref[:, 1024:1280] = (dkv[SWA_BLOCK:, :] + carry[...]).astype(BF16)
        carry[...] = dkv[:SWA_BLOCK, :]

        @pl.when(step == nb - 1)
        def _():
            acc = dsink_acc[...]
            dsink_ref[...] = jnp.concatenate([jnp.sum(acc[:, h * SWA_BLOCK:(h + 1) * SWA_BLOCK], axis=1, keepdims=True)
                                              for h in range(SWA_HEADS)], axis=1)

    rev = lambda i: (nb - 1 - i, 0)
    table_shape = (2 * SWA_BLOCK, SWA_HEADS * SWA_BLOCK)
    return _fused_call(
        body, name=name, grid=(nb,),
        out_shape=(jax.ShapeDtypeStruct((s, W_B), BF16), jax.ShapeDtypeStruct(table_shape, F32), jax.ShapeDtypeStruct((1, SWA_HEADS), F32)),
        in_specs=[pl.BlockSpec((SWA_BLOCK, W_B), rev),
                  pl.BlockSpec((SWA_BLOCK, 256), lambda i: (nb - 1 - i, 4)),
                  pl.BlockSpec((SWA_BLOCK, 256), lambda i: (jnp.maximum(nb - 2 - i, 0), 4)),
                  pl.BlockSpec((SWA_BLOCK, D_MODEL), rev),
                  pl.BlockSpec((None,) + table_shape, lambda i: (jnp.minimum(nb - 1 - i, 1), 0, 0)),
                  _resident((1, SWA_HEADS * SWA_BLOCK))],
        out_specs=(pl.BlockSpec((SWA_BLOCK, W_B), rev), pl.BlockSpec(table_shape, lambda i: (0, 0)),
                   pl.BlockSpec((1, SWA_HEADS), lambda i: (0, 0))),
        scratch_shapes=[pltpu.VMEM((SWA_BLOCK, 256), F32), pltpu.VMEM((1, SWA_HEADS * SWA_BLOCK), F32)],
        operands=[zb, zb, zb, do_b, bias_tables, sink_lanes], exchanges=exchanges)


def _mem_probs(zc_ref, mkv_ref, h):
    cols = slice(h * MEM_HEAD_DIM, (h + 1) * MEM_HEAD_DIM)
    qh = (zc_ref[:, cols] * (MEM_HEAD_DIM ** -0.5)).astype(BF16)
    sc = _dot(qh, mkv_ref[:, cols], NT)
    e = jnp.exp(sc - jnp.max(sc, axis=-1, keepdims=True))
    return qh, e / jnp.sum(e, axis=-1, keepdims=True)


def _mem_fwd(zc, mkv, *, name):
    s = zc.shape[0]
    t = min(512, s)

    def body(zc_ref, mkv_ref, o_ref):
        for h in range(MEM_HEADS):
            _, p = _mem_probs(zc_ref, mkv_ref, h)
            vh = mkv_ref[:, D_MODEL + h * MEM_HEAD_DIM:D_MODEL + (h + 1) * MEM_HEAD_DIM]
            o_ref[:, h * MEM_HEAD_DIM:(h + 1) * MEM_HEAD_DIM] = _dot(p.astype(BF16), vh, NN).astype(BF16)

    return pl.pallas_call(
        body, name=name, grid=(s // t,), out_shape=jax.ShapeDtypeStruct((s, D_MODEL), BF16),
        in_specs=[pl.BlockSpec((t, D_MODEL), lambda i: (i, 0)), _resident((MEM_LEN, 2 * D_MODEL))],
        out_specs=pl.BlockSpec((t, D_MODEL), lambda i: (i, 0)), compiler_params=_params(("parallel",)),
    )(zc, mkv)


def _mem_bwd(zc, do_c, mkv, *, name):
    s = zc.shape[0]
    t = min(512, s)

    def body(zc_ref, do_ref, mkv_ref, dz_ref, dmkv_ref):
        @pl.when(pl.program_id(0) == 0)
        def _():
            dmkv_ref[...] = jnp.zeros_like(dmkv_ref)

        for h in range(MEM_HEADS):
            cols = slice(h * MEM_HEAD_DIM, (h + 1) * MEM_HEAD_DIM)
            vcols = slice(D_MODEL + h * MEM_HEAD_DIM, D_MODEL + (h + 1) * MEM_HEAD_DIM)
            qh, p = _mem_probs(zc_ref, mkv_ref, h)
            doh = do_ref[:, cols].astype(BF16)
            dp = _dot(doh, mkv_ref[:, vcols], NT)
            ds = (p * (dp - jnp.sum(p * dp, axis=-1, keepdims=True))).astype(BF16)
            dz_ref[:, cols] = (_dot(ds, mkv_ref[:, cols], NN) * (MEM_HEAD_DIM ** -0.5)).astype(BF16)
            dmkv_ref[:, cols] += _dot(ds, qh, TN)
            dmkv_ref[:, vcols] += _dot(p.astype(BF16), doh, TN)

    return pl.pallas_call(
        body, name=name, grid=(s // t,),
        out_shape=(jax.ShapeDtypeStruct((s, D_MODEL), BF16), jax.ShapeDtypeStruct((MEM_LEN, 2 * D_MODEL), F32)),
        in_specs=[pl.BlockSpec((t, D_MODEL), lambda i: (i, 0)), pl.BlockSpec((t, D_MODEL), lambda i: (i, 0)),
                  _resident((MEM_LEN, 2 * D_MODEL))],
        out_specs=(pl.BlockSpec((t, D_MODEL), lambda i: (i, 0)), pl.BlockSpec((MEM_LEN, 2 * D_MODEL), lambda i: (0, 0))),
        compiler_params=_params(("arbitrary",)),
    )(zc, do_c, mkv)


def _normalize(pre):
    mu = jnp.mean(pre, axis=-1, keepdims=True)
    xc = pre - mu
    rstd = lax.rsqrt(jnp.mean(xc * xc, axis=-1, keepdims=True) + LN_EPS)
    return xc * rstd, rstd


def _layer_norm_bwd(dh, xhat, rstd, g):
    dxh = dh * g
    dpre = rstd * (dxh - jnp.mean(dxh, axis=-1, keepdims=True) - xhat * jnp.mean(dxh * xhat, axis=-1, keepdims=True))
    return dpre, jnp.sum(dh * xhat, axis=0, keepdims=True), jnp.sum(dh, axis=0, keepdims=True)


def _merge_fwd(o_a, o_b, o_c, zd, x, wbr, wo, *, name):
    s = x.shape[0]
    t = min(256, s)
    row = lambda w: pl.BlockSpec((t, w), lambda i: (i, 0))

    def body(oa_ref, ob_ref, oc_ref, zd_ref, x_ref, wbr_ref, wo_ref, xhat_ref, rstd_ref, merged_ref, pa_ref, pb_ref, pc_ref):
        merged = jnp.zeros((t, D_MODEL), F32)
        for b, (o_ref, p_ref) in enumerate(((oa_ref, pa_ref), (ob_ref, pb_ref), (oc_ref, pc_ref))):
            p = _dot(o_ref[...], wbr_ref[b], NN)
            p_ref[...] = p.astype(BF16)
            merged = merged + jax.nn.sigmoid(zd_ref[:, b * D_MODEL:(b + 1) * D_MODEL]) * p
        merged_b = merged.astype(BF16)
        merged_ref[...] = merged_b
        xhat, rstd = _normalize(ALPHA * x_ref[...] + _dot(merged_b, wo_ref[...], NN))
        xhat_ref[...] = xhat
        rstd_ref[...] = rstd

    act = jax.ShapeDtypeStruct((s, D_MODEL), F32)
    return pl.pallas_call(
        body, name=name, grid=(s // t,),
        out_shape=(act, jax.ShapeDtypeStruct((s, 1), F32)) + (jax.ShapeDtypeStruct((s, D_MODEL), BF16),) * 4,
        in_specs=[row(D_MODEL), row(D_MODEL), row(D_MODEL), row(W_D), row(D_MODEL),
                  _resident((3, D_MODEL, D_MODEL)), _resident((D_MODEL, D_MODEL))],
        out_specs=(row(D_MODEL), row(1), row(D_MODEL), row(D_MODEL), row(D_MODEL), row(D_MODEL)),
        compiler_params=_params(("parallel",)),
    )(o_a, o_b, o_c, zd, x, wbr, wo)


def _merge_bwd(dpre1, zd, pa, pb, pc, wbr, wo, *, name, exchanges=()):
    s = dpre1.shape[0]
    t = min(256, s)
    row = lambda w: pl.BlockSpec((t, w), lambda i: (i, 0))

    def body(dpre_ref, zd_ref, pa_ref, pb_ref, pc_ref, wbr_ref, wo_ref, dzd_ref, dpa_ref, dpb_ref, dpc_ref, doa_ref, dob_ref, doc_ref):
        dmerged = _dot(dpre_ref[...].astype(BF16), wo_ref[...], NT)
        branches = ((pa_ref, dpa_ref, doa_ref), (pb_ref, dpb_ref, dob_ref), (pc_ref, dpc_ref, doc_ref))
        for b, (p_ref, dp_ref, do_ref) in enumerate(branches):
            gate = jax.nn.sigmoid(zd_ref[:, b * D_MODEL:(b + 1) * D_MODEL])
            dzd_ref[:, b * D_MODEL:(b + 1) * D_MODEL] = (dmerged * p_ref[...] * gate * (1.0 - gate)).astype(BF16)
            dp = (dmerged * gate).astype(BF16)
            dp_ref[...] = dp
            do_ref[...] = _dot(dp, wbr_ref[b], NT).astype(do_ref.dtype)

    act = jax.ShapeDtypeStruct((s, D_MODEL), F32)
    actb = jax.ShapeDtypeStruct((s, D_MODEL), BF16)
    return _fused_call(
        body, name=name, grid=(s // t,),
        out_shape=(jax.ShapeDtypeStruct((s, W_D), BF16), actb, actb, actb, act, actb, actb),
        in_specs=[row(D_MODEL), row(W_D), row(D_MODEL), row(D_MODEL), row(D_MODEL),
                  _resident((3, D_MODEL, D_MODEL)), _resident((D_MODEL, D_MODEL))],
        out_specs=(row(W_D),) + (row(D_MODEL),) * 6, scratch_shapes=[],
        operands=[dpre1, zd, pa, pb, pc, wbr, wo], exchanges=exchanges)


def _mlp_loss(xhat1, rstd1, target, ln1_g, ln1_b, ln2_g, ln2_b, wu, wd, *, name):
    s = xhat1.shape[0]
    t = min(256, s)
    npan = wu.shape[0]
    row = lambda w: pl.BlockSpec((t, w), lambda i: (i, 0))
    vec = _resident((1, D_MODEL))

    def body(xhat_ref, rstd_ref, tgt_ref, g1_ref, b1_ref, g2_ref, b2_ref, wu_ref, wd_ref,
             dpre1_ref, dpre2_ref, h1_ref, a_ref, du_ref, stats_ref):
        @pl.when(pl.program_id(0) == 0)
        def _():
            stats_ref[...] = jnp.zeros_like(stats_ref)

        xhat1_v = xhat_ref[...]
        h1 = xhat1_v * g1_ref[...] + b1_ref[...]
        h1_b = h1.astype(BF16)
        h1_ref[...] = h1_b
        us = []
        ff = jnp.zeros((t, D_MODEL), F32)
        for j in range(npan):
            u = _dot(h1_b, wu_ref[j], NN)
            us.append(u)
            r = jnp.maximum(u, 0.0)
            a_b = (r * r).astype(BF16)
            a_ref[:, j * D_MODEL:(j + 1) * D_MODEL] = a_b
            ff = ff + _dot(a_b, wd_ref[j], NN)
        xhat2, rstd2 = _normalize(ALPHA * h1 + ff)
        err = xhat2 * g2_ref[...] + b2_ref[...] - tgt_ref[...]
        stats_ref[4:5, :] += jnp.sum(err * err, axis=0, keepdims=True)
        dpre2, dg2, db2 = _layer_norm_bwd(err * (1.0 / D_MODEL), xhat2, rstd2, g2_ref[...])
        stats_ref[0:1, :] += dg2
        stats_ref[1:2, :] += db2
        dpre2_b = dpre2.astype(BF16)
        dpre2_ref[...] = dpre2_b
        dh1 = ALPHA * dpre2
        for j in range(npan):
            du_b = (_dot(dpre2_b, wd_ref[j], NT) * (2.0 * jnp.maximum(us[j], 0.0))).astype(BF16)
            du_ref[:, j * D_MODEL:(j + 1) * D_MODEL] = du_b
            dh1 = dh1 + _dot(du_b, wu_ref[j], NT)
        dpre1, dg1, db1 = _layer_norm_bwd(dh1, xhat1_v, rstd_ref[...], g1_ref[...])
        stats_ref[2:3, :] += dg1
        stats_ref[3:4, :] += db1
        dpre1_ref[...] = dpre1

    actb = jax.ShapeDtypeStruct((s, D_MODEL), BF16)
    wide = jax.ShapeDtypeStruct((s, D_FF), BF16)
    return pl.pallas_call(
        body, name=name, grid=(s // t,),
        out_shape=(jax.ShapeDtypeStruct((s, D_MODEL), F32), actb, actb, wide, wide, jax.ShapeDtypeStruct((8, D_MODEL), F32)),
        in_specs=[row(D_MODEL), row(1), row(D_MODEL), vec, vec, vec, vec,
                  _resident((npan, D_MODEL, D_MODEL)), _resident((npan, D_MODEL, D_MODEL))],
        out_specs=(row(D_MODEL), row(D_MODEL), row(D_MODEL), row(D_FF), row(D_FF), pl.BlockSpec((8, D_MODEL), lambda i: (0, 0))),
        compiler_params=_params(("arbitrary",)),
    )(xhat1, rstd1, target, ln1_g, ln1_b, ln2_g, ln2_b, wu, wd)


BRANCH_WEIGHTS = ("w_branch_hg", "w_branch_swa", "w_branch_mem")


def _local_step(x, mem, target, wi_parts, wmkv, late, lb_logits, gain, sinks, rel_bias, ln1_g, ln1_b, ln2_g, ln2_b, *, distributed):
    s = x.shape[0]
    tm = min(1024, s)
    tk = min(2048, s)
    xb = x.astype(BF16)
    memb = mem.astype(BF16)
    wia, wib, wic, wid = wi_parts
    if distributed:
        cx, cy, cc = lax.axis_index("x"), lax.axis_index("y"), lax.axis_index("c")
        pos = jnp.stack([2 * cx + cy, cc]).astype(jnp.int32)
    gather = (lambda names: [_gather_exchange([late[k] for k in names])]) if distributed else (lambda names: [])
    to_sibling = (lambda grads: [_sibling_halves_exchange(grads)]) if distributed else (lambda grads: [])
    to_chips = (lambda sums: [_chip_partials_exchange([bf for bf, _ in sums])]) if distributed else (lambda sums: [])

    def chip_sums(names, grads, from_sibling):
        return [_add_sibling(g, o, pos, name="add_sibling_" + k) for k, g, o in zip(names, grads, from_sibling)]

    def shard_sums(names, sums, from_chips):
        return {k: _add_chips(mine, o, pos, name="add_chips_" + k) for k, (_, mine), o in zip(names, sums, from_chips)}

    za = _mm(xb, wia, mode="nn", tm=min(512, s), tn=W_A, tk=D_MODEL, name="proj_a")
    zb = _mm(xb, wib, mode="nn", tm=tm, tn=W_B, tk=D_MODEL, name="proj_b")
    zc = _mm(xb, wic, mode="nn", tm=tm, tn=W_C, tk=D_MODEL, name="proj_c")
    zd = _mm(xb, wid, mode="nn", tm=min(512, s), tn=W_D, tk=D_MODEL, name="proj_d")
    mkv = _mm(memb, wmkv, mode="nn", tm=MEM_LEN, tn=512, tk=D_MODEL, name="mem_kv", out_dtype=BF16, b_panels=True)
    onehot, maskrow = _bias_selector()
    bias_tables, sink_lanes = _swa_tables(_bias_table(rel_bias.T, onehot, maskrow, name="bias_table"), sinks)
    (o_a, o_raw, states), landed = _hgrn_fwd(za, lb_logits, gain, name="hgrn_fwd", exchanges=gather(("w_up", "w_down")))
    wu, wd = landed[0] if distributed else (late["wu"], late["wd"])
    o_b, landed = _swa_fwd(zb, bias_tables, sink_lanes, name="swa_fwd", exchanges=gather(BRANCH_WEIGHTS + ("w_out",)))
    if distributed:
        wbr = jnp.stack([wb.reshape(D_MODEL, D_MODEL) for wb in landed[0][:3]])
        wo = landed[0][3].reshape(D_MODEL, D_MODEL)
    else:
        wbr, wo = late["wbr"], late["wo"]
    o_c = _mem_fwd(zc, mkv, name="mem_fwd")
    xhat1, rstd1, merged, pa, pb, pc = _merge_fwd(o_a, o_b, o_c, zd, x, wbr, wo, name="merge_fwd")

    dpre1, dpre2, h1, act, du, ln_stats = _mlp_loss(xhat1, rstd1, target, ln1_g, ln1_b, ln2_g, ln2_b, wu, wd, name="mlp_loss")
    ffn = ("w_down", "w_up")
    g_ffn = [_mm(act, dpre2, mode="tn", tm=1024, tn=D_MODEL, tk=tk, name="grad_w_down").reshape(N_SHARDS, D_FF // N_SHARDS, D_MODEL),
             _mm(h1, du, mode="tn", tm=D_MODEL, tn=1024, tk=tk, name="grad_w_up", out_panels=True)]

    (dzd, dpa, dpb, dpc, do_a, do_b, do_c), landed = _merge_bwd(dpre1, zd, pa, pb, pc, wbr, wo, name="merge_bwd", exchanges=to_sibling(g_ffn))
    sums_ffn = chip_sums(ffn, g_ffn, landed[0]) if distributed else []
    merge = BRANCH_WEIGHTS + ("w_out",)
    g_merge = [_mm(o, dp, mode="tn", tm=D_MODEL, tn=D_MODEL, tk=tk, name="grad_" + k).reshape(N_SHARDS, D_MODEL // N_SHARDS, D_MODEL)
               for k, o, dp in zip(merge, (o_a, o_b, o_c, merged), (dpa, dpb, dpc, dpre1))]
    (dza, hg_stats), landed = _hgrn_bwd(za, o_raw, do_a, states, lb_logits, gain, name="hgrn_bwd",
                                        exchanges=to_chips(sums_ffn) + to_sibling(g_merge))
    halves = shard_sums(ffn, sums_ffn, landed[0]) if distributed else {}
    sums_merge = chip_sums(merge, g_merge, landed[1]) if distributed else []
    (dzb, dbias_t, dsinks), landed = _swa_bwd(zb, do_b, bias_tables, sink_lanes, name="swa_bwd", exchanges=to_chips(sums_merge))
    if distributed:
        halves.update(shard_sums(merge, sums_merge, landed[0]))
    dbias = dbias_t.reshape(2 * SWA_BLOCK, SWA_HEADS, SWA_BLOCK).transpose(1, 2, 0).reshape(SWA_HEADS, -1)
    d_rel_bias = _bias_grad(dbias, onehot, name="bias_grad").T
    dzc, dmkv = _mem_bwd(zc, do_c, mkv, name="mem_bwd")

    proj = ("w_in", "w_mem_kv")
    g_wi = [_mm(xb, dz, mode="tn", tm=min(512, D_MODEL), tn=dz.shape[1] if dz.shape[1] <= 1280 else 1024, tk=tk, name=nm)
            for dz, nm in ((dza, "grad_w_in_a"), (dzb, "grad_w_in_b"), (dzc, "grad_w_in_c"), (dzd, "grad_w_in_d"))]
    g_proj = [jnp.concatenate(g_wi, axis=1).reshape(D_MODEL, N_SHARDS, IN_COLS // N_SHARDS).transpose(1, 0, 2),
              _mm(memb, dmkv, mode="tn", tm=D_MODEL, tn=512, tk=MEM_LEN, name="grad_w_mem_kv", out_panels=True)]
    sums_proj = chip_sums(proj, g_proj, _run_exchanges(to_sibling(g_proj), name="reduce_sibling_proj")[0]) if distributed else []
    grad_x, landed = _dx_matmul([dza, dzb, dzc, dzd], [wia, wib, wic, wid], dpre1, tm=min(512, s), name="grad_x",
                                exchanges=to_chips(sums_proj))
    if distributed:
        halves.update(shard_sums(proj, sums_proj, landed[0]))
    else:
        halves = dict(zip(ffn + merge + proj, g_ffn + g_merge + g_proj))
    small = dict(lb_logits=hg_stats[1:3], hg_norm_gain=hg_stats[0:1], swa_sinks=dsinks, rel_bias=d_rel_bias,
                 ln1_g=ln_stats[2:3], ln1_b=ln_stats[3:4], ln2_g=ln_stats[0:1], ln2_b=ln_stats[1:2], sq_err=ln_stats[4:5])
    return grad_x, halves, small


def _mesh_position():
    x, y, c = lax.axis_index("x"), lax.axis_index("y"), lax.axis_index("c")
    chips = [(1 - x, y), (x, 1 - y), (1 - x, 1 - y)]
    return x, y, c, chips


class _Exchange(NamedTuple):
    operands: list
    out_shapes: list
    n_sems: int
    start: Callable
    finish: Callable


def _gather_exchange(shards):
    n = len(shards)
    per = 7

    def plan(ins, outs, send_sems, recv_sems):
        x, y, c, chips = _mesh_position()
        me = 2 * x + y
        sibling = (x, y, 1 - c)

        def half(a, slot, hc):
            rh = shards[a].shape[0] // 2
            return outs[a].at[slot, pl.ds(hc * rh, rh), :]

        def copy(a, k, src, dst, to):
            return pltpu.make_async_remote_copy(src_ref=src, dst_ref=dst, send_sem=send_sems.at[a * per + k], recv_sem=recv_sems.at[a * per + k],
                                                device_id=to, device_id_type=MESH)

        own = [copy(a, 6, ins[a], outs[a].at[me], sibling) for a in range(n)]
        to_chips = [copy(a, k, ins[a].at[pl.ds(c * (shards[a].shape[0] // 2), shards[a].shape[0] // 2), :], half(a, me, c), (cx, cy, c))
                    for k, (cx, cy) in enumerate(chips) for a in range(n)]
        arrived = [copy(a, k, half(a, 2 * cx + cy, c), half(a, 2 * cx + cy, c), (cx, cy, c)) for k, (cx, cy) in enumerate(chips) for a in range(n)]
        passed_on = [copy(a, 3 + k, half(a, 2 * cx + cy, c), half(a, 2 * cx + cy, c), sibling) for k, (cx, cy) in enumerate(chips) for a in range(n)]
        from_sibling = [copy(a, 3 + k, half(a, 2 * cx + cy, 1 - c), half(a, 2 * cx + cy, 1 - c), sibling)
                        for k, (cx, cy) in enumerate(chips) for a in range(n)]
        own_arrived = [copy(a, 6, outs[a].at[me], outs[a].at[me], sibling) for a in range(n)]
        return own, to_chips, arrived, passed_on, from_sibling, own_arrived

    def start(*refs):
        own, to_chips, _, _, _, _ = plan(*refs)
        for cp in own + to_chips:
            cp.start()

    def finish(*refs):
        own, to_chips, arrived, passed_on, from_sibling, own_arrived = plan(*refs)
        for landed, onward in zip(arrived, passed_on):
            landed.wait_recv()
            onward.start()
        for cp in from_sibling + own_arrived:
            cp.wait_recv()
        for cp in own + to_chips + passed_on:
            cp.wait_send()

    return _Exchange(list(shards), [jax.ShapeDtypeStruct((N_SHARDS,) + w.shape, w.dtype) for w in shards], per * n, start, finish)


def _sibling_halves_exchange(grads):
    n = len(grads)

    def plan(ins, outs, send_sems, recv_sems):
        x, y, c, _ = _mesh_position()
        return [pltpu.make_async_remote_copy(src_ref=ins[a].at[:, pl.ds((1 - c) * (grads[a].shape[1] // 2), grads[a].shape[1] // 2), :],
                                             dst_ref=outs[a], send_sem=send_sems.at[a], recv_sem=recv_sems.at[a],
                                             device_id=(x, y, 1 - c), device_id_type=MESH) for a in range(n)]

    def start(*refs):
        for cp in plan(*refs):
            cp.start()

    def finish(*refs):
        for cp in plan(*refs):
            cp.wait()

    return _Exchange(list(grads), [jax.ShapeDtypeStruct((g.shape[0], g.shape[1] // 2, g.shape[2]), g.dtype) for g in grads], n, start, finish)


def _chip_partials_exchange(sums):
    n = len(sums)

    def plan(ins, outs, send_sems, recv_sems):
        _, _, c, chips = _mesh_position()
        return [pltpu.make_async_remote_copy(src_ref=ins[a].at[2 * cx + cy], dst_ref=outs[a].at[k], send_sem=send_sems.at[a * 3 + k],
                                             recv_sem=recv_sems.at[a * 3 + k], device_id=(cx, cy, c), device_id_type=MESH)
                for k, (cx, cy) in enumerate(chips) for a in range(n)]

    def start(*refs):
        for cp in plan(*refs):
            cp.start()

    def finish(*refs):
        for cp in plan(*refs):
            cp.wait()

    return _Exchange(list(sums), [jax.ShapeDtypeStruct((3,) + g.shape[1:], g.dtype) for g in sums], 3 * n, start, finish)


def _fused_call(body, *, name, grid, in_specs, out_specs, out_shape, scratch_shapes, operands, exchanges=()):
    single = not isinstance(out_shape, (tuple, list))
    out_specs = [out_specs] if single else list(out_specs)
    out_shape = [out_shape] if single else list(out_shape)
    n_in, n_out, n_scr = len(in_specs), len(out_specs), len(scratch_shapes)
    x_in = [len(e.operands) for e in exchanges]
    x_out = [len(e.out_shapes) for e in exchanges]

    def wrapped(*refs):
        refs = list(refs)
        ins = refs[:n_in]
        pos = n_in
        ex_ins = []
        for k in x_in:
            ex_ins.append(refs[pos:pos + k])
            pos += k
        outs = refs[pos:pos + n_out]
        pos += n_out
        ex_outs = []
        for k in x_out:
            ex_outs.append(refs[pos:pos + k])
            pos += k
        scratch = refs[pos:pos + n_scr]
        sems = refs[pos + n_scr:]
        first, last = None, None
        for axis, size in enumerate(grid):
            at_start, at_end = pl.program_id(axis) == 0, pl.program_id(axis) == size - 1
            first = at_start if first is None else first & at_start
            last = at_end if last is None else last & at_end

        @pl.when(first)
        def _():
            for i, e in enumerate(exchanges):
                e.start(ex_ins[i], ex_outs[i], sems[2 * i], sems[2 * i + 1])

        body(*ins, *outs, *scratch)

        @pl.when(last)
        def _():
            for i, e in enumerate(exchanges):
                e.finish(ex_ins[i], ex_outs[i], sems[2 * i], sems[2 * i + 1])

    n_x_in, n_x_out = sum(x_in), sum(x_out)
    results = pl.pallas_call(
        wrapped if exchanges else body, name=name, grid=grid,
        in_specs=list(in_specs) + [HBM] * n_x_in,
        out_specs=out_specs + [HBM] * n_x_out,
        out_shape=out_shape + [s for e in exchanges for s in e.out_shapes],
        scratch_shapes=list(scratch_shapes) + [pltpu.SemaphoreType.DMA((e.n_sems,)) for e in exchanges for _ in range(2)],
        compiler_params=_params(("arbitrary",) * len(grid)),
    )(*operands, *[a for e in exchanges for a in e.operands])
    own = results[0] if single else tuple(results[:n_out])
    landed, pos = [], n_out
    for k in x_out:
        landed.append(list(results[pos:pos + k]))
        pos += k
    return own, landed


def _run_exchanges(exchanges, *, name):
    def body(*refs):
        n_in = sum(len(e.operands) for e in exchanges)
        n_out = sum(len(e.out_shapes) for e in exchanges)
        ins, outs, sems = refs[:n_in], refs[n_in:n_in + n_out], refs[n_in + n_out:]
        spans, i, o = [], 0, 0
        for e in exchanges:
            spans.append((ins[i:i + len(e.operands)], outs[o:o + len(e.out_shapes)]))
            i, o = i + len(e.operands), o + len(e.out_shapes)
        for k, e in enumerate(exchanges):
            e.start(*spans[k], sems[2 * k], sems[2 * k + 1])
        for k, e in enumerate(exchanges):
            e.finish(*spans[k], sems[2 * k], sems[2 * k + 1])

    operands = [a for e in exchanges for a in e.operands]
    shapes = [s for e in exchanges for s in e.out_shapes]
    results = pl.pallas_call(
        body, name=name, out_shape=shapes, in_specs=[HBM] * len(operands), out_specs=[HBM] * len(shapes),
        scratch_shapes=[pltpu.SemaphoreType.DMA((e.n_sems,)) for e in exchanges for _ in range(2)],
    )(*operands)
    landed, pos = [], 0
    for e in exchanges:
        landed.append(list(results[pos:pos + len(e.out_shapes)]))
        pos += len(e.out_shapes)
    return landed


def _row_tile(rows):
    for tr in (256, 128, 64, 32, 16, 8):
        if rows % tr == 0:
            return tr
    raise ValueError(rows)


def _add_sibling(grad, other, pos, *, name):
    p, r, cols = grad.shape
    rh = r // 2
    tr = _row_tile(rh)
    nb = rh // tr

    def body(pos_ref, g_ref, o_ref, sb_ref, mine_ref):
        total = g_ref[...] + o_ref[...]
        sb_ref[...] = total.astype(BF16)

        @pl.when(pl.program_id(1) == pos_ref[0])
        def _():
            mine_ref[...] = total

    return pl.pallas_call(
        body, name=name, out_shape=(jax.ShapeDtypeStruct((p, rh, cols), BF16), jax.ShapeDtypeStruct((rh, cols), F32)),
        grid_spec=pltpu.PrefetchScalarGridSpec(
            num_scalar_prefetch=1, grid=(nb, p),
            in_specs=[pl.BlockSpec((None, tr, cols), lambda i, j, pos_ref: (j, pos_ref[1] * nb + i, 0)),
                      pl.BlockSpec((None, tr, cols), lambda i, j, pos_ref: (j, i, 0))],
            out_specs=(pl.BlockSpec((None, tr, cols), lambda i, j, pos_ref: (j, i, 0)),
                       pl.BlockSpec((tr, cols), lambda i, j, pos_ref: (i, 0)))),
        compiler_params=_params(("parallel", "arbitrary")),
    )(pos, grad, other)


def _add_chips(mine, others, pos, *, name):
    rh, cols = mine.shape
    tr = _row_tile(rh)
    nb = rh // tr

    def body(pos_ref, s_ref, o_ref, r_ref):
        r_ref[...] = ((s_ref[...] + o_ref[0].astype(F32)) + o_ref[1].astype(F32)) + o_ref[2].astype(F32)

    return pl.pallas_call(
        body, name=name, out_shape=jax.ShapeDtypeStruct((2 * rh, cols), F32),
        grid_spec=pltpu.PrefetchScalarGridSpec(
            num_scalar_prefetch=1, grid=(nb,),
            in_specs=[pl.BlockSpec((tr, cols), lambda i, pos_ref: (i, 0)),
                      pl.BlockSpec((3, tr, cols), lambda i, pos_ref: (0, i, 0))],
            out_specs=pl.BlockSpec((tr, cols), lambda i, pos_ref: (pos_ref[1] * nb + i, 0))),
        compiler_params=_params(("parallel",)),
    )(pos, mine, others)


def _join_halves(bufs, *, name):
    n = len(bufs)

    def body(*refs):
        ins, outs = refs[:n], refs[n:2 * n]
        send_sems, recv_sems = refs[2 * n:]
        x, y, c, _ = _mesh_position()

        def copy(a, hc):
            rh = bufs[a].shape[0] // 2
            rows = pl.ds(hc * rh, rh)
            return pltpu.make_async_remote_copy(src_ref=ins[a].at[rows, :], dst_ref=outs[a].at[rows, :], send_sem=send_sems.at[a],
                                                recv_sem=recv_sems.at[a], device_id=(x, y, 1 - c), device_id_type=MESH)

        for a in range(n):
            copy(a, c).start()
        for a in range(n):
            copy(a, c).wait_send()
            copy(a, 1 - c).wait_recv()

    return pl.pallas_call(
        body, name=name, out_shape=[jax.ShapeDtypeStruct(b.shape, b.dtype) for b in bufs],
        in_specs=[HBM] * n, out_specs=[HBM] * n, input_output_aliases={a: a for a in range(n)},
        scratch_shapes=[pltpu.SemaphoreType.DMA((n,)), pltpu.SemaphoreType.DMA((n,))],
    )(*bufs)


def _all_reduce_small(packed, *, name):
    rows, cols = packed.shape

    def body(in_ref, out_ref, gathered, send_sems, recv_sems):
        x, y, c, _ = _mesh_position()
        me = 4 * x + 2 * y + c
        gathered[me] = in_ref[...]
        copies = []
        for d in range(1, 8):
            dx, dy, dc = (d >> 2) & 1, (d >> 1) & 1, d & 1
            peer = (x ^ dx, y ^ dy, c ^ dc)
            cp = pltpu.make_async_remote_copy(src_ref=in_ref, dst_ref=gathered.at[me], send_sem=send_sems.at[d - 1], recv_sem=recv_sems.at[d - 1],
                                              device_id=peer, device_id_type=MESH)
            cp.start()
            copies.append(cp)
        for cp in copies:
            cp.wait()
        total = gathered[0]
        for j in range(1, 8):
            total = total + gathered[j]
        out_ref[...] = total

    vm = pl.BlockSpec(memory_space=pltpu.VMEM)
    return pl.pallas_call(
        body, name=name, out_shape=jax.ShapeDtypeStruct((rows, cols), F32), in_specs=[vm], out_specs=vm,
        scratch_shapes=[pltpu.VMEM((8, rows, cols), F32), pltpu.SemaphoreType.DMA((7,)), pltpu.SemaphoreType.DMA((7,))],
    )(packed)


def _adamw_math(w, g, m, v):
    m = ADAM_B1 * m + (1.0 - ADAM_B1) * g
    v = ADAM_B2 * v + (1.0 - ADAM_B2) * (g * g)
    m_hat = m / (1.0 - ADAM_B1 ** ADAM_STEP)
    v_hat = v / (1.0 - ADAM_B2 ** ADAM_STEP)
    delta = -ADAM_LR * (m_hat / (jnp.sqrt(v_hat) + ADAM_EPS) + ADAM_WD * w)
    return delta, m, v


def _adamw(w, g, m, v, *, name):
    _, rows, cols = w.shape
    tr = _row_tile(rows)
    blk = pl.BlockSpec((None, tr, cols), lambda i: (0, i, 0))
    flat = pl.BlockSpec((tr, cols), lambda i: (i, 0))

    def body(w_ref, g_ref, m_ref, v_ref, go_ref, d_ref, nm_ref, nv_ref):
        g_v = g_ref[...]
        go_ref[...] = g_v
        d_ref[...], nm_ref[...], nv_ref[...] = _adamw_math(w_ref[...], g_v, m_ref[...], v_ref[...])

    shape = jax.ShapeDtypeStruct((1, rows, cols), F32)
    return pl.pallas_call(body, name=name, grid=(rows // tr,), out_shape=(shape,) * 4, in_specs=[blk, flat, blk, blk], out_specs=(blk,) * 4,
                          compiler_params=_params(("parallel",)))(w, g, m, v)


def _adamw_small(w, g, m, v, *, name):
    def body(w_ref, g_ref, m_ref, v_ref, d_ref, nm_ref, nv_ref, loss_ref):
        d_ref[...], nm_ref[...], nv_ref[...] = _adamw_math(w_ref[...], g_ref[...], m_ref[...], v_ref[...])
        loss_ref[...] = (0.5 / D_MODEL) * jnp.sum(g_ref[8:9, :], axis=1, keepdims=True)

    shape = jax.ShapeDtypeStruct(w.shape, F32)
    return pl.pallas_call(body, name=name, out_shape=(shape, shape, shape, jax.ShapeDtypeStruct((1, 1), F32)),
                          compiler_params=_params())(w, g, m, v)


SMALL_ROWS = 16


def _pack_small(lb_logits, gain, sinks, rel_bias, ln1_g, ln1_b, ln2_g, ln2_b, extra=None):
    misc = jnp.concatenate([sinks.reshape(1, -1), rel_bias.reshape(1, -1)], axis=1)
    misc = jnp.pad(misc, ((0, 0), (0, D_MODEL - misc.shape[1])))
    rows = [lb_logits, gain, ln1_g, ln1_b, ln2_g, ln2_b, misc, extra if extra is not None else jnp.zeros((1, D_MODEL), F32)]
    used = sum(r.shape[0] for r in rows)
    return jnp.concatenate(rows + [jnp.zeros((SMALL_ROWS - used, D_MODEL), F32)], axis=0)


def _unpack_small(p):
    return dict(lb_logits=p[0:2], hg_norm_gain=p[2:3], ln1_g=p[3:4], ln1_b=p[4:5], ln2_g=p[5:6], ln2_b=p[6:7],
                swa_sinks=p[7:8, 0:SWA_HEADS], rel_bias=p[7:8, SWA_HEADS:SWA_HEADS + NUM_BUCKETS * SWA_HEADS].reshape(NUM_BUCKETS, SWA_HEADS))


WEIGHTS = ["w_in", "lb_logits", "hg_norm_gain", "swa_sinks", "rel_bias", "w_mem_kv", "w_branch_hg", "w_branch_swa", "w_branch_mem",
           "w_out", "ln1_g", "ln1_b", "w_up", "w_down", "ln2_g", "ln2_b"]
BIG = ["w_in", "w_mem_kv", "w_branch_hg", "w_branch_swa", "w_branch_mem", "w_out", "w_up", "w_down"]
SMALL = ["lb_logits", "hg_norm_gain", "swa_sinks", "rel_bias", "ln1_g", "ln1_b", "ln2_g", "ln2_b"]


def kernel(x, mem, w_in, lb_logits, hg_norm_gain, swa_sinks, rel_bias, w_mem_kv, w_branch_hg, w_branch_swa, w_branch_mem, w_out, ln1_g, ln1_b, w_up, w_down, ln2_g, ln2_b, loss_target, m_w_in, m_lb_logits, m_hg_norm_gain, m_swa_sinks, m_rel_bias, m_w_mem_kv, m_w_branch_hg, m_w_branch_swa, m_w_branch_mem, m_w_out, m_ln1_g, m_ln1_b, m_w_up, m_w_down, m_ln2_g, m_ln2_b, v_w_in, v_lb_logits, v_hg_norm_gain, v_swa_sinks, v_rel_bias, v_w_mem_kv, v_w_branch_hg, v_w_branch_swa, v_w_branch_mem, v_w_out, v_ln1_g, v_ln1_b, v_w_up, v_w_down, v_ln2_g, v_ln2_b):
    w = dict(w_in=w_in, lb_logits=lb_logits, hg_norm_gain=hg_norm_gain, swa_sinks=swa_sinks, rel_bias=rel_bias, w_mem_kv=w_mem_kv,
             w_branch_hg=w_branch_hg, w_branch_swa=w_branch_swa, w_branch_mem=w_branch_mem, w_out=w_out, ln1_g=ln1_g, ln1_b=ln1_b,
             w_up=w_up, w_down=w_down, ln2_g=ln2_g, ln2_b=ln2_b)
    m = dict(w_in=m_w_in, lb_logits=m_lb_logits, hg_norm_gain=m_hg_norm_gain, swa_sinks=m_swa_sinks, rel_bias=m_rel_bias, w_mem_kv=m_w_mem_kv,
             w_branch_hg=m_w_branch_hg, w_branch_swa=m_w_branch_swa, w_branch_mem=m_w_branch_mem, w_out=m_w_out, ln1_g=m_ln1_g, ln1_b=m_ln1_b,
             w_up=m_w_up, w_down=m_w_down, ln2_g=m_ln2_g, ln2_b=m_ln2_b)
    v = dict(w_in=v_w_in, lb_logits=v_lb_logits, hg_norm_gain=v_hg_norm_gain, swa_sinks=v_swa_sinks, rel_bias=v_rel_bias, w_mem_kv=v_w_mem_kv,
             w_branch_hg=v_w_branch_hg, w_branch_swa=v_w_branch_swa, w_branch_mem=v_w_branch_mem, w_out=v_w_out, ln1_g=v_ln1_g, ln1_b=v_ln1_b,
             w_up=v_w_up, w_down=v_w_down, ln2_g=v_ln2_g, ln2_b=v_ln2_b)
    shapes = {k: w[k].shape for k in WEIGHTS}
    shards = {k: w[k].reshape(w[k].shape[-2], w[k].shape[-1]).astype(BF16) for k in BIG}
    wi4, wmkv = _run_exchanges([_gather_exchange([shards["w_in"], shards["w_mem_kv"]])], name="gather_weights")[0]
    wi = wi4.transpose(1, 0, 2).reshape(D_MODEL, IN_COLS)
    wi_parts = (wi[:, 0:W_A], wi[:, W_A:W_A + W_B], wi[:, W_A + W_B:W_A + W_B + W_C], wi[:, W_A + W_B + W_C:])

    grad_x, halves, small = _local_step(
        x.reshape(x.shape[-2], D_MODEL), mem.reshape(MEM_LEN, D_MODEL), loss_target.reshape(loss_target.shape[-2], D_MODEL),
        wi_parts, wmkv, shards, lb_logits, hg_norm_gain, swa_sinks, rel_bias, ln1_g, ln1_b, ln2_g, ln2_b, distributed=True)

    reduced = dict(zip(BIG, _join_halves([halves[k] for k in BIG], name="join_halves")))

    packed_g = _pack_small(small["lb_logits"], small["hg_norm_gain"], small["swa_sinks"], small["rel_bias"], small["ln1_g"], small["ln1_b"],
                           small["ln2_g"], small["ln2_b"], extra=small["sq_err"])
    packed_g = _all_reduce_small(packed_g, name="reduce_small")

    grad_out, delta_out, m_out, v_out = {}, {}, {}, {}
    for k in BIG:
        grad_out[k], delta_out[k], m_out[k], v_out[k] = _adamw(w[k], reduced[k], m[k], v[k], name="adamw_" + k)
    pack = lambda d: _pack_small(d["lb_logits"], d["hg_norm_gain"], d["swa_sinks"], d["rel_bias"], d["ln1_g"], d["ln1_b"], d["ln2_g"], d["ln2_b"])
    d_s, m_s, v_s, loss = _adamw_small(pack(w), packed_g, pack(m), pack(v), name="adamw_small")
    for out, p in ((grad_out, packed_g), (delta_out, d_s), (m_out, m_s), (v_out, v_s)):
        out.update(_unpack_small(p))

    result = [loss.reshape(()), grad_x.reshape(x.shape)]
    for out in (grad_out, delta_out, m_out, v_out):
        result += [out[k].reshape(shapes[k]) for k in WEIGHTS]
    return tuple(result)
```

```python
import math
from typing import Callable, NamedTuple

import jax
import jax.numpy as jnp
from jax import lax
from jax.experimental import pallas as pl
from jax.experimental.pallas import tpu as pltpu

F32 = jnp.float32
BF16 = jnp.bfloat16
HIGHEST = lax.Precision.HIGHEST
MESH = pl.DeviceIdType.MESH

D_MODEL = 1024
MEM_LEN = 256
HG_HEADS = 8
HG_DK = 128
HG_CHUNK = 64
SWA_HEADS = 16
SWA_KV_HEADS = 2
SWA_GROUP = 8
SWA_HEAD_DIM = 64
SWA_BLOCK = 128
SWA_WINDOW = 128
MEM_HEADS = 4
MEM_HEAD_DIM = 256
NUM_BUCKETS = 32
MAX_DISTANCE = 128
D_FF = 4096
LN_EPS = 1e-5
RMS_EPS = 1e-6
ALPHA = 2.0 ** 0.25
W_A, W_B, W_C, W_D = 4096, 1280, 1024, 3072
IN_COLS = W_A + W_B + W_C + W_D
N_SHARDS = 4
ADAM_LR = 0.001
ADAM_B1 = 0.9
ADAM_B2 = 0.999
ADAM_EPS = 1e-08
ADAM_WD = 0.01
ADAM_STEP = 10
MASK_VALUE = -1e30
VMEM_LIMIT = 56 * 1024 * 1024

NN = ((1,), (0,))
NT = ((1,), (1,))
TN = ((0,), (0,))
HBM = pl.BlockSpec(memory_space=pltpu.HBM)


def _dot(a, b, dims=NN, precision=None):
    return lax.dot_general(a, b, (dims, ((), ())), precision=precision, preferred_element_type=F32)


def _params(sem=None):
    return pltpu.CompilerParams(dimension_semantics=sem, vmem_limit_bytes=VMEM_LIMIT)


def _resident(shape):
    zeros = (0,) * len(shape)
    return pl.BlockSpec(shape, lambda *_: zeros, pipeline_mode=pl.Buffered(1))


def _mm(a, b, *, mode, tm, tn, tk, name, out_dtype=F32, b_panels=False, out_panels=False, add=None, add_scale=1.0):
    if mode == "tn":
        kdim, m = a.shape
    else:
        m, kdim = a.shape
    if b_panels:
        n = b.shape[0] * b.shape[2]
        assert b.shape[2] == tn and mode == "nn"
    elif mode == "nt":
        n = b.shape[0]
    else:
        n = b.shape[1]
    assert m % tm == 0 and n % tn == 0 and kdim % tk == 0, (name, m, n, kdim)
    nk = kdim // tk
    dims = {"nn": NN, "nt": NT, "tn": TN}[mode]
    a_spec = pl.BlockSpec((tk, tm), lambda i, j, k: (k, i)) if mode == "tn" else pl.BlockSpec((tm, tk), lambda i, j, k: (i, k))
    if b_panels:
        b_spec = pl.BlockSpec((None, tk, tn), lambda i, j, k: (j, k, 0))
    elif mode == "nt":
        b_spec = pl.BlockSpec((tn, tk), lambda i, j, k: (j, k))
    else:
        b_spec = pl.BlockSpec((tk, tn), lambda i, j, k: (k, j))
    if out_panels:
        out_shape = jax.ShapeDtypeStruct((n // tn, m, tn), out_dtype)
        o_spec = pl.BlockSpec((None, tm, tn), lambda i, j, k: (j, i, 0))
    else:
        out_shape = jax.ShapeDtypeStruct((m, n), out_dtype)
        o_spec = pl.BlockSpec((tm, tn), lambda i, j, k: (i, j))
    in_specs = [a_spec, b_spec]
    operands = [a, b]
    if add is not None:
        in_specs.append(pl.BlockSpec((tm, tn), lambda i, j, k: (i, j)))
        operands.append(add)

    def body(*refs):
        a_ref, b_ref = refs[0], refs[1]
        add_ref = refs[2] if add is not None else None
        o_ref = refs[3] if add is not None else refs[2]
        part = _dot(a_ref[...].astype(BF16), b_ref[...].astype(BF16), dims)

        def finish(acc):
            if add_ref is not None:
                acc = acc + add_scale * add_ref[...]
            o_ref[...] = acc.astype(out_dtype)

        if nk == 1:
            finish(part)
        else:
            acc_ref = refs[-1]
            k = pl.program_id(2)

            @pl.when(k == 0)
            def _():
                acc_ref[...] = part

            @pl.when(k > 0)
            def _():
                acc_ref[...] += part

            @pl.when(k == nk - 1)
            def _():
                finish(acc_ref[...])

    return pl.pallas_call(
        body, name=name, out_shape=out_shape, grid=(m // tm, n // tn, nk), in_specs=in_specs, out_specs=o_spec,
        scratch_shapes=[pltpu.VMEM((tm, tn), F32)] if nk > 1 else [],
        compiler_params=_params(("parallel", "parallel", "arbitrary")),
    )(*operands)


def _dx_matmul(dzs, wis, resid, *, tm, name, exchanges=()):
    s = resid.shape[0]
    npieces = len(dzs)
    in_specs = [pl.BlockSpec((tm, dz.shape[1]), lambda i: (i, 0)) for dz in dzs]
    in_specs += [_resident(w.shape) for w in wis]
    in_specs += [pl.BlockSpec((tm, D_MODEL), lambda i: (i, 0))]

    def body(*refs):
        dz_refs, w_refs = refs[:npieces], refs[npieces:2 * npieces]
        r_ref, o_ref = refs[2 * npieces], refs[2 * npieces + 1]
        total = ALPHA * r_ref[...]
        for p in range(npieces):
            total = total + _dot(dz_refs[p][...], w_refs[p][...], NN)
        o_ref[...] = total

    return _fused_call(
        body, name=name, out_shape=jax.ShapeDtypeStruct((s, D_MODEL), F32), grid=(s // tm,), in_specs=in_specs,
        out_specs=pl.BlockSpec((tm, D_MODEL), lambda i: (i, 0)), scratch_shapes=[],
        operands=[*dzs, *wis, resid], exchanges=exchanges)


def _lower_bound(lbl_ref):
    l0, l1 = lbl_ref[0:1, :], lbl_ref[1:2, :]
    mx = jnp.maximum(l0, l1)
    e0, e1 = jnp.exp(l0 - mx), jnp.exp(l1 - mx)
    return e0 / (e0 + e1)


HEAD_COLS = [slice(h * HG_DK, (h + 1) * HG_DK) for h in range(HG_HEADS)]


def _head_mean(x):
    return jnp.concatenate([jnp.broadcast_to(jnp.mean(x[:, c], axis=-1, keepdims=True), (x.shape[0], HG_DK)) for c in HEAD_COLS], axis=1)


def _chunk_forward(q, fl, v, lb, tril_f):
    sg = jax.nn.sigmoid(fl)
    f = lb + (1.0 - lb) * sg
    k = 1.0 - f
    b = _dot(tril_f, jnp.log(f), NN, HIGHEST)
    b_last = b[HG_CHUNK - 1:HG_CHUNK, :]
    eb, enb, eo = jnp.exp(b), jnp.exp(-b), jnp.exp(b_last - b)
    return sg, f, k, b_last, eb, enb, eo, q * eb, k * enb, k * eo


def _hgrn_fwd(za, lb_logits, gain, *, name, exchanges=()):
    s = za.shape[0]
    t = min(256, s)
    ncs = t // HG_CHUNK

    def body(z_ref, lbl_ref, gain_ref, oa_ref, oraw_ref, st_ref, state):
        @pl.when(pl.program_id(0) == 0)
        def _():
            state[...] = jnp.zeros_like(state)

        lb_all = _lower_bound(lbl_ref)
        row = lax.broadcasted_iota(jnp.int32, (HG_CHUNK, HG_CHUNK), 0)
        col = lax.broadcasted_iota(jnp.int32, (HG_CHUNK, HG_CHUNK), 1)
        tril = row >= col
        tril_f = tril.astype(F32)
        gain_all = gain_ref[...]

        def chunk(i, carry):
            r = pl.ds(pl.multiple_of(i * HG_CHUNK, HG_CHUNK), HG_CHUNK)
            q, fl, v, hg = (z_ref[r, j * D_MODEL:(j + 1) * D_MODEL] for j in range(4))
            _, _, _, b_last, _, _, _, q_in, k_in, k_out = _chunk_forward(q, fl, v, lb_all, tril_f)
            q_in_b, k_in_b, k_out_b, vb = (u.astype(BF16) for u in (q_in, k_in, k_out, v))
            decay = jnp.exp(b_last)
            sts = [state[h] for h in range(HG_HEADS)]
            attn = [_dot(q_in_b[:, c], k_in_b[:, c], NT) for c in HEAD_COLS]
            inter = [_dot(q_in_b[:, c], sts[h].astype(BF16), NT) for h, c in enumerate(HEAD_COLS)]
            upd = [_dot(vb[:, c], k_out_b[:, c], TN) for c in HEAD_COLS]
            attn = [jnp.where(tril, a, 0.0).astype(BF16) for a in attn]
            outs = [_dot(attn[h], vb[:, c], NN) + inter[h] for h, c in enumerate(HEAD_COLS)]
            for h, c in enumerate(HEAD_COLS):
                st_ref[h, i] = sts[h]
                state[h] = sts[h] * decay[:, c] + upd[h]
            o = jnp.concatenate(outs, axis=1)
            oraw_ref[r, :] = o
            n = o * lax.rsqrt(_head_mean(o * o) + RMS_EPS)
            oa_ref[r, :] = (n * gain_all * (hg * jax.nn.sigmoid(hg))).astype(BF16)
            return carry

        lax.fori_loop(0, ncs, chunk, 0)

    return _fused_call(
        body, name=name, grid=(s // t,),
        out_shape=(jax.ShapeDtypeStruct((s, D_MODEL), BF16), jax.ShapeDtypeStruct((s, D_MODEL), F32),
                   jax.ShapeDtypeStruct((HG_HEADS, s // HG_CHUNK, HG_DK, HG_DK), F32)),
        in_specs=[pl.BlockSpec((t, W_A), lambda i: (i, 0)), _resident((2, D_MODEL)), _resident((1, D_MODEL))],
        out_specs=(pl.BlockSpec((t, D_MODEL), lambda i: (i, 0)), pl.BlockSpec((t, D_MODEL), lambda i: (i, 0)),
                   pl.BlockSpec((HG_HEADS, ncs, HG_DK, HG_DK), lambda i: (0, i, 0, 0))),
        scratch_shapes=[pltpu.VMEM((HG_HEADS, HG_DK, HG_DK), F32)],
        operands=[za, lb_logits, gain], exchanges=exchanges)


def _hgrn_bwd(za, oraw, do_a, states, lb_logits, gain, *, name, exchanges=()):
    s = za.shape[0]
    t = min(256, s)
    ncs = t // HG_CHUNK
    nt = s // t

    def body(z_ref, oraw_ref, do_ref, st_ref, lbl_ref, gain_ref, dz_ref, stats_ref, dstate):
        step = pl.program_id(0)

        @pl.when(step == 0)
        def _():
            dstate[...] = jnp.zeros_like(dstate)
            stats_ref[...] = jnp.zeros_like(stats_ref)

        lb_all = _lower_bound(lbl_ref)
        row = lax.broadcasted_iota(jnp.int32, (HG_CHUNK, HG_CHUNK), 0)
        col = lax.broadcasted_iota(jnp.int32, (HG_CHUNK, HG_CHUNK), 1)
        tril = row >= col
        tril_f = tril.astype(F32)
        triu_f = (row <= col).astype(F32)
        gain_all = gain_ref[...]

        def chunk(ii, carry):
            i = ncs - 1 - ii
            r = pl.ds(pl.multiple_of(i * HG_CHUNK, HG_CHUNK), HG_CHUNK)
            q, fl, v, hg = (z_ref[r, j * D_MODEL:(j + 1) * D_MODEL] for j in range(4))
            o = oraw_ref[r, :]
            doa = do_ref[r, :]
            rms = lax.rsqrt(_head_mean(o * o) + RMS_EPS)
            n = o * rms
            sgg = jax.nn.sigmoid(hg)
            silu = hg * sgg
            dhg = doa * n * gain_all * (sgg * (1.0 + hg * (1.0 - sgg)))
            dgain = jnp.sum(doa * n * silu, axis=0, keepdims=True)
            dn = doa * gain_all * silu
            do = rms * (dn - n * _head_mean(dn * n))
            sg, f, k, b_last, eb, enb, eo, q_in, k_in, k_out = _chunk_forward(q, fl, v, lb_all, tril_f)
            q_in_b, k_in_b, k_out_b, vb, dob = (u.astype(BF16) for u in (q_in, k_in, k_out, v, do))
            decay = jnp.exp(b_last)
            sts = [st_ref[h, i] for h in range(HG_HEADS)]
            dsts = [dstate[h] for h in range(HG_HEADS)]
            dsts_b = [d.astype(BF16) for d in dsts]
            heads = list(enumerate(HEAD_COLS))
            attn = [_dot(q_in_b[:, c], k_in_b[:, c], NT) for h, c in heads]
            dattn = [_dot(dob[:, c], vb[:, c], NT) for h, c in heads]
            dq_st = [_dot(dob[:, c], sts[h].astype(BF16), NN) for h, c in heads]
            dk_out = [_dot(vb[:, c], dsts_b[h], NN) for h, c in heads]
            dv_st = [_dot(k_out_b[:, c], dsts_b[h], NT) for h, c in heads]
            dst_o = [_dot(dob[:, c], q_in_b[:, c], TN) for h, c in heads]
            attn = [jnp.where(tril, a, 0.0).astype(BF16) for a in attn]
            dattn = [jnp.where(tril, a, 0.0).astype(BF16) for a in dattn]
            dq_in = jnp.concatenate([_dot(dattn[h], k_in_b[:, c], NN) + dq_st[h] for h, c in heads], axis=1)
            dk_in = jnp.concatenate([_dot(dattn[h], q_in_b[:, c], TN) for h, c in heads], axis=1)
            dv = jnp.concatenate([_dot(attn[h], dob[:, c], TN) + dv_st[h] for h, c in heads], axis=1)
            dk_out = jnp.concatenate(dk_out, axis=1)
            dst_st = jnp.concatenate([jnp.sum(dsts[h] * sts[h], axis=0, keepdims=True) for h in range(HG_HEADS)], axis=1)
            for h, c in heads:
                dstate[h] = dsts[h] * decay[:, c] + dst_o[h]
            db_last = decay * dst_st + jnp.sum(dk_out * k_out, axis=0, keepdims=True)
            db = dq_in * q_in - dk_in * k_in - dk_out * k_out
            dg = _dot(triu_f, db, NN, HIGHEST) + db_last
            dk = dk_in * enb + dk_out * eo
            df = dg / f - dk
            stats_ref[0:1, :] += dgain
            stats_ref[1:2, :] += jnp.sum(df * (1.0 - sg), axis=0, keepdims=True)
            dz_ref[r, 0:1024] = (dq_in * eb).astype(BF16)
            dz_ref[r, 1024:2048] = (df * (1.0 - lb_all) * sg * (1.0 - sg)).astype(BF16)
            dz_ref[r, 2048:3072] = dv.astype(BF16)
            dz_ref[r, 3072:4096] = dhg.astype(BF16)
            return carry

        lax.fori_loop(0, ncs, chunk, 0)

        @pl.when(step == nt - 1)
        def _():
            dl0 = stats_ref[1:2, :] * lb_all * (1.0 - lb_all)
            stats_ref[1:2, :] = dl0
            stats_ref[2:3, :] = -dl0

    rev = lambda i: (nt - 1 - i, 0)
    return _fused_call(
        body, name=name, grid=(nt,),
        out_shape=(jax.ShapeDtypeStruct((s, W_A), BF16), jax.ShapeDtypeStruct((8, D_MODEL), F32)),
        in_specs=[pl.BlockSpec((t, W_A), rev), pl.BlockSpec((t, D_MODEL), rev), pl.BlockSpec((t, D_MODEL), rev),
                  pl.BlockSpec((HG_HEADS, ncs, HG_DK, HG_DK), lambda i: (0, nt - 1 - i, 0, 0)),
                  _resident((2, D_MODEL)), _resident((1, D_MODEL))],
        out_specs=(pl.BlockSpec((t, W_A), rev), pl.BlockSpec((8, D_MODEL), lambda i: (0, 0))),
        scratch_shapes=[pltpu.VMEM((HG_HEADS, HG_DK, HG_DK), F32)],
        operands=[za, oraw, do_a, states, lb_logits, gain], exchanges=exchanges)


def _t5_bucket(n):
    max_exact = NUM_BUCKETS // 2
    nf = jnp.maximum(n, 1).astype(F32)
    large = max_exact + (jnp.log(nf / max_exact) / math.log(MAX_DISTANCE / max_exact) * (NUM_BUCKETS - max_exact)).astype(jnp.int32)
    large = jnp.minimum(large, NUM_BUCKETS - 1)
    return jnp.where(n < max_exact, n, large)


def _bias_selector():
    qi = jnp.arange(SWA_BLOCK)[:, None] + SWA_BLOCK
    kj = jnp.arange(2 * SWA_BLOCK)[None, :]
    dist = qi - kj
    band = ((dist >= 0) & (dist < SWA_WINDOW)).reshape(1, -1)
    bucket = _t5_bucket(jnp.clip(dist, 0, SWA_WINDOW - 1)).reshape(1, -1)
    onehot = ((bucket == jnp.arange(NUM_BUCKETS)[:, None]) & band).astype(F32)
    return onehot, jnp.where(band, 0.0, MASK_VALUE).astype(F32)


def _bias_table(rel_bias_t, onehot, maskrow, *, name):
    def body(rb_ref, oh_ref, mask_ref, o_ref):
        o_ref[...] = _dot(rb_ref[...], oh_ref[...], NN, HIGHEST) + mask_ref[...]

    return pl.pallas_call(body, name=name, out_shape=jax.ShapeDtypeStruct((SWA_HEADS, onehot.shape[1]), F32),
                          compiler_params=_params())(rel_bias_t, onehot, maskrow)


def _bias_grad(dbias2d, onehot, *, name):
    def body(db_ref, oh_ref, o_ref):
        o_ref[...] = _dot(db_ref[...], oh_ref[...], NT, HIGHEST)

    return pl.pallas_call(body, name=name, out_shape=jax.ShapeDtypeStruct((SWA_HEADS, NUM_BUCKETS), F32),
                          compiler_params=_params())(dbias2d, onehot)


GROUP_LANES = SWA_GROUP * SWA_BLOCK


def _swa_operands(zq_ref, kv_cur_ref, kv_prev_ref):
    q = (zq_ref[:, 0:1024] * (SWA_HEAD_DIM ** -0.5)).astype(BF16)
    kv_c = kv_cur_ref[...].astype(BF16)
    kv_p = kv_prev_ref[...].astype(BF16)
    kks = [jnp.concatenate([kv_p[:, g * 64:(g + 1) * 64], kv_c[:, g * 64:(g + 1) * 64]], axis=0) for g in range(SWA_KV_HEADS)]
    vvs = [jnp.concatenate([kv_p[:, 128 + g * 64:128 + (g + 1) * 64], kv_c[:, 128 + g * 64:128 + (g + 1) * 64]], axis=0)
           for g in range(SWA_KV_HEADS)]
    return q, kks, vvs


def _stack_heads(x, g):
    return jnp.concatenate([x[:, h * SWA_HEAD_DIM:(h + 1) * SWA_HEAD_DIM] for h in range(g * SWA_GROUP, (g + 1) * SWA_GROUP)], axis=0)


def _heads_to_lanes(xt):
    pairs = []
    for j in range(0, SWA_GROUP, 2):
        two = jnp.concatenate([xt[:, j * SWA_BLOCK:(j + 1) * SWA_BLOCK], xt[:, (j + 1) * SWA_BLOCK:(j + 2) * SWA_BLOCK]], axis=0)
        pairs.append(two.T)
    return jnp.concatenate(pairs, axis=1)


def _swa_softmax(score_t, bias_ref, sink_ref, g):
    lanes = slice(g * GROUP_LANES, (g + 1) * GROUP_LANES)
    sc = score_t + bias_ref[:, lanes]
    sink = sink_ref[:, lanes]
    m = jnp.maximum(jnp.max(sc, axis=0, keepdims=True), sink)
    e = jnp.exp(sc - m)
    e_sink = jnp.exp(sink - m)
    return e, 1.0 / (jnp.sum(e, axis=0, keepdims=True) + e_sink), e_sink


def _swa_tables(bias2d, sinks):
    bias_t = bias2d.reshape(SWA_HEADS, SWA_BLOCK, 2 * SWA_BLOCK).transpose(2, 0, 1).reshape(2 * SWA_BLOCK, SWA_HEADS * SWA_BLOCK)
    first = jnp.where(jnp.arange(2 * SWA_BLOCK)[:, None] < SWA_BLOCK, MASK_VALUE, bias_t)
    return jnp.stack([first, bias_t]), jnp.repeat(sinks, SWA_BLOCK, axis=1)


def _swa_fwd(zb, bias_tables, sink_lanes, *, name, exchanges=()):
    s = zb.shape[0]
    nb = s // SWA_BLOCK

    def body(zq_ref, kvc_ref, kvp_ref, bias_ref, sink_ref, o_ref):
        q, kks, vvs = _swa_operands(zq_ref, kvc_ref, kvp_ref)
        groups = range(SWA_KV_HEADS)
        scores = [_dot(kks[g], _stack_heads(q, g), NT) for g in groups]
        probs = []
        for g in groups:
            e, inv, _ = _swa_softmax(scores[g], bias_ref, sink_ref, g)
            probs.append((e * inv).astype(BF16))
        outs = [_dot(vvs[g], probs[g], TN) for g in groups]
        o_ref[...] = jnp.concatenate([_heads_to_lanes(outs[g]) for g in groups], axis=1).astype(BF16)

    return _fused_call(
        body, name=name, grid=(nb,), out_shape=jax.ShapeDtypeStruct((s, D_MODEL), BF16),
        in_specs=[pl.BlockSpec((SWA_BLOCK, W_B), lambda n: (n, 0)),
                  pl.BlockSpec((SWA_BLOCK, 256), lambda n: (n, 4)),
                  pl.BlockSpec((SWA_BLOCK, 256), lambda n: (jnp.maximum(n - 1, 0), 4)),
                  pl.BlockSpec((None, 2 * SWA_BLOCK, SWA_HEADS * SWA_BLOCK), lambda n: (jnp.minimum(n, 1), 0, 0)),
                  _resident((1, SWA_HEADS * SWA_BLOCK))],
        out_specs=pl.BlockSpec((SWA_BLOCK, D_MODEL), lambda n: (n, 0)), scratch_shapes=[],
        operands=[zb, zb, zb, bias_tables, sink_lanes], exchanges=exchanges)


def _swa_bwd(zb, do_b, bias_tables, sink_lanes, *, name, exchanges=()):
    s = zb.shape[0]
    nb = s // SWA_BLOCK
    scale = SWA_HEAD_DIM ** -0.5

    def body(zq_ref, kvc_ref, kvp_ref, do_ref, bias_ref, sink_ref, dz_ref, dbias_ref, dsink_ref, carry, dsink_acc):
        step = pl.program_id(0)

        @pl.when(step == 0)
        def _():
            carry[...] = jnp.zeros_like(carry)
            dsink_acc[...] = jnp.zeros_like(dsink_acc)
            dbias_ref[...] = jnp.zeros_like(dbias_ref)

        q, kks, vvs = _swa_operands(zq_ref, kvc_ref, kvp_ref)
        groups = range(SWA_KV_HEADS)
        do = do_ref[...].astype(BF16)
        q_rows = [_stack_heads(q, g) for g in groups]
        do_rows = [_stack_heads(do, g) for g in groups]
        scores = [_dot(kks[g], q_rows[g], NT) for g in groups]
        dps = [_dot(vvs[g], do_rows[g], NT) for g in groups]
        ps, dss = [], []
        for g in groups:
            lanes = slice(g * GROUP_LANES, (g + 1) * GROUP_LANES)
            e, inv, e_sink = _swa_softmax(scores[g], bias_ref, sink_ref, g)
            p = e * inv
            delta = jnp.sum(p * dps[g], axis=0, keepdims=True)
            ds = p * (dps[g] - delta)
            dbias_ref[:, lanes] += ds
            dsink_acc[:, lanes] -= e_sink * inv * delta
            ps.append(p.astype(BF16))
            dss.append(ds.astype(BF16))
        dqs = [_dot(kks[g], dss[g], TN) * scale for g in groups]
        dkks = [_dot(dss[g], q_rows[g], NN) for g in groups]
        dvvs = [_dot(ps[g], do_rows[g], NN) for g in groups]
        dkv = jnp.concatenate(dkks + dvvs, axis=1)
        dz_ref[:, 0:1024] = jnp.concatenate([_heads_to_lanes(dqs[g]) for g in groups], axis=1).astype(BF16)
        dz_ref[:, 1024:1280] = (dkv[SWA_BLOCK:, :] + carry[...]).astype(BF16)
        carry[...] = dkv[:SWA_BLOCK, :]

        @pl.when(step == nb - 1)
        def _():
            acc = dsink_acc[...]
            dsink_ref[...] = jnp.concatenate([jnp.sum(acc[:, h * SWA_BLOCK:(h + 1) * SWA_BLOCK], axis=1, keepdims=True)
                                              for h in range(SWA_HEADS)], axis=1)

    rev = lambda i: (nb - 1 - i, 0)
    table_shape = (2 * SWA_BLOCK, SWA_HEADS * SWA_BLOCK)
    return _fused_call(
        body, name=name, grid=(nb,),
        out_shape=(jax.ShapeDtypeStruct((s, W_B), BF16), jax.ShapeDtypeStruct(table_shape, F32), jax.ShapeDtypeStruct((1, SWA_HEADS), F32)),
        in_specs=[pl.BlockSpec((SWA_BLOCK, W_B), rev),
                  pl.BlockSpec((SWA_BLOCK, 256), lambda i: (nb - 1 - i, 4)),
                  pl.BlockSpec((SWA_BLOCK, 256), lambda i: (jnp.maximum(nb - 2 - i, 0), 4)),
                  pl.BlockSpec((SWA_BLOCK, D_MODEL), rev),
                  pl.BlockSpec((None,) + table_shape, lambda i: (jnp.minimum(nb - 1 - i, 1), 0, 0)),
                  _resident((1, SWA_HEADS * SWA_BLOCK))],
        out_specs=(pl.BlockSpec((SWA_BLOCK, W_B), rev), pl.BlockSpec(table_shape, lambda i: (0, 0)),
                   pl.BlockSpec((1, SWA_HEADS), lambda i: (0, 0))),
        scratch_shapes=[pltpu.VMEM((SWA_BLOCK, 256), F32), pltpu.VMEM((1, SWA_HEADS * SWA_BLOCK), F32)],
        operands=[zb, zb, zb, do_b, bias_tables, sink_lanes], exchanges=exchanges)


def _mem_probs(zc_ref, mkv_ref, h):
    cols = slice(h * MEM_HEAD_DIM, (h + 1) * MEM_HEAD_DIM)
    qh = (zc_ref[:, cols] * (MEM_HEAD_DIM ** -0.5)).astype(BF16)
    sc = _dot(qh, mkv_ref[:, cols], NT)
    e = jnp.exp(sc - jnp.max(sc, axis=-1, keepdims=True))
    return qh, e / jnp.sum(e, axis=-1, keepdims=True)


def _mem_fwd(zc, mkv, *, name):
    s = zc.shape[0]
    t = min(512, s)

    def body(zc_ref, mkv_ref, o_ref):
        for h in range(MEM_HEADS):
            _, p = _mem_probs(zc_ref, mkv_ref, h)
            vh = mkv_ref[:, D_MODEL + h * MEM_HEAD_DIM:D_MODEL + (h + 1) * MEM_HEAD_DIM]
            o_ref[:, h * MEM_HEAD_DIM:(h + 1) * MEM_HEAD_DIM] = _dot(p.astype(BF16), vh, NN).astype(BF16)

    return pl.pallas_call(
        body, name=name, grid=(s // t,), out_shape=jax.ShapeDtypeStruct((s, D_MODEL), BF16),
        in_specs=[pl.BlockSpec((t, D_MODEL), lambda i: (i, 0)), _resident((MEM_LEN, 2 * D_MODEL))],
        out_specs=pl.BlockSpec((t, D_MODEL), lambda i: (i, 0)), compiler_params=_params(("parallel",)),
    )(zc, mkv)


def _mem_bwd(zc, do_c, mkv, *, name):
    s = zc.shape[0]
    t = min(512, s)

    def body(zc_ref, do_ref, mkv_ref, dz_ref, dmkv_ref):
        @pl.when(pl.program_id(0) == 0)
        def _():
            dmkv_ref[...] = jnp.zeros_like(dmkv_ref)

        for h in range(MEM_HEADS):
            cols = slice(h * MEM_HEAD_DIM, (h + 1) * MEM_HEAD_DIM)
            vcols = slice(D_MODEL + h * MEM_HEAD_DIM, D_MODEL + (h + 1) * MEM_HEAD_DIM)
            qh, p = _mem_probs(zc_ref, mkv_ref, h)
            doh = do_ref[:, cols].astype(BF16)
            dp = _dot(doh, mkv_ref[:, vcols], NT)
            ds = (p * (dp - jnp.sum(p * dp, axis=-1, keepdims=True))).astype(BF16)
            dz_ref[:, cols] = (_dot(ds, mkv_ref[:, cols], NN) * (MEM_HEAD_DIM ** -0.5)).astype(BF16)
            dmkv_ref[:, cols] += _dot(ds, qh, TN)
            dmkv_ref[:, vcols] += _dot(p.astype(BF16), doh, TN)

    return pl.pallas_call(
        body, name=name, grid=(s // t,),
        out_shape=(jax.ShapeDtypeStruct((s, D_MODEL), BF16), jax.ShapeDtypeStruct((MEM_LEN, 2 * D_MODEL), F32)),
        in_specs=[pl.BlockSpec((t, D_MODEL), lambda i: (i, 0)), pl.BlockSpec((t, D_MODEL), lambda i: (i, 0)),
                  _resident((MEM_LEN, 2 * D_MODEL))],
        out_specs=(pl.BlockSpec((t, D_MODEL), lambda i: (i, 0)), pl.BlockSpec((MEM_LEN, 2 * D_MODEL), lambda i: (0, 0))),
        compiler_params=_params(("arbitrary",)),
    )(zc, do_c, mkv)


def _normalize(pre):
    mu = jnp.mean(pre, axis=-1, keepdims=True)
    xc = pre - mu
    rstd = lax.rsqrt(jnp.mean(xc * xc, axis=-1, keepdims=True) + LN_EPS)
    return xc * rstd, rstd


def _layer_norm_bwd(dh, xhat, rstd, g):
    dxh = dh * g
    dpre = rstd * (dxh - jnp.mean(dxh, axis=-1, keepdims=True) - xhat * jnp.mean(dxh * xhat, axis=-1, keepdims=True))
    return dpre, jnp.sum(dh * xhat, axis=0, keepdims=True), jnp.sum(dh, axis=0, keepdims=True)


def _merge_fwd(o_a, o_b, o_c, zd, x, wbr, wo, *, name):
    s = x.shape[0]
    t = min(256, s)
    row = lambda w: pl.BlockSpec((t, w), lambda i: (i, 0))

    def body(oa_ref, ob_ref, oc_ref, zd_ref, x_ref, wbr_ref, wo_ref, xhat_ref, rstd_ref, merged_ref, pa_ref, pb_ref, pc_ref):
        merged = jnp.zeros((t, D_MODEL), F32)
        for b, (o_ref, p_ref) in enumerate(((oa_ref, pa_ref), (ob_ref, pb_ref), (oc_ref, pc_ref))):
            p = _dot(o_ref[...], wbr_ref[b], NN)
            p_ref[...] = p.astype(BF16)
            merged = merged + jax.nn.sigmoid(zd_ref[:, b * D_MODEL:(b + 1) * D_MODEL]) * p
        merged_b = merged.astype(BF16)
        merged_ref[...] = merged_b
        xhat, rstd = _normalize(ALPHA * x_ref[...] + _dot(merged_b, wo_ref[...], NN))
        xhat_ref[...] = xhat
        rstd_ref[...] = rstd

    act = jax.ShapeDtypeStruct((s, D_MODEL), F32)
    return pl.pallas_call(
        body, name=name, grid=(s // t,),
        out_shape=(act, jax.ShapeDtypeStruct((s, 1), F32)) + (jax.ShapeDtypeStruct((s, D_MODEL), BF16),) * 4,
        in_specs=[row(D_MODEL), row(D_MODEL), row(D_MODEL), row(W_D), row(D_MODEL),
                  _resident((3, D_MODEL, D_MODEL)), _resident((D_MODEL, D_MODEL))],
        out_specs=(row(D_MODEL), row(1), row(D_MODEL), row(D_MODEL), row(D_MODEL), row(D_MODEL)),
        compiler_params=_params(("parallel",)),
    )(o_a, o_b, o_c, zd, x, wbr, wo)


def _merge_bwd(dpre1, zd, pa, pb, pc, wbr, wo, *, name, exchanges=()):
    s = dpre1.shape[0]
    t = min(256, s)
    row = lambda w: pl.BlockSpec((t, w), lambda i: (i, 0))

    def body(dpre_ref, zd_ref, pa_ref, pb_ref, pc_ref, wbr_ref, wo_ref, dzd_ref, dpa_ref, dpb_ref, dpc_ref, doa_ref, dob_ref, doc_ref):
        dmerged = _dot(dpre_ref[...].astype(BF16), wo_ref[...], NT)
        branches = ((pa_ref, dpa_ref, doa_ref), (pb_ref, dpb_ref, dob_ref), (pc_ref, dpc_ref, doc_ref))
        for b, (p_ref, dp_ref, do_ref) in enumerate(branches):
            gate = jax.nn.sigmoid(zd_ref[:, b * D_MODEL:(b + 1) * D_MODEL])
            dzd_ref[:, b * D_MODEL:(b + 1) * D_MODEL] = (dmerged * p_ref[...] * gate * (1.0 - gate)).astype(BF16)
            dp = (dmerged * gate).astype(BF16)
            dp_ref[...] = dp
            do_ref[...] = _dot(dp, wbr_ref[b], NT).astype(do_ref.dtype)

    act = jax.ShapeDtypeStruct((s, D_MODEL), F32)
    actb = jax.ShapeDtypeStruct((s, D_MODEL), BF16)
    return _fused_call(
        body, name=name, grid=(s // t,),
        out_shape=(jax.ShapeDtypeStruct((s, W_D), BF16), actb, actb, actb, act, actb, actb),
        in_specs=[row(D_MODEL), row(W_D), row(D_MODEL), row(D_MODEL), row(D_MODEL),
                  _resident((3, D_MODEL, D_MODEL)), _resident((D_MODEL, D_MODEL))],
        out_specs=(row(W_D),) + (row(D_MODEL),) * 6, scratch_shapes=[],
        operands=[dpre1, zd, pa, pb, pc, wbr, wo], exchanges=exchanges)


def _mlp_loss(xhat1, rstd1, target, ln1_g, ln1_b, ln2_g, ln2_b, wu, wd, *, name):
    s = xhat1.shape[0]
    t = min(256, s)
    npan = wu.shape[0]
    row = lambda w: pl.BlockSpec((t, w), lambda i: (i, 0))
    vec = _resident((1, D_MODEL))

    def body(xhat_ref, rstd_ref, tgt_ref, g1_ref, b1_ref, g2_ref, b2_ref, wu_ref, wd_ref,
             dpre1_ref, dpre2_ref, h1_ref, a_ref, du_ref, stats_ref):
        @pl.when(pl.program_id(0) == 0)
        def _():
            stats_ref[...] = jnp.zeros_like(stats_ref)

        xhat1_v = xhat_ref[...]
        h1 = xhat1_v * g1_ref[...] + b1_ref[...]
        h1_b = h1.astype(BF16)
        h1_ref[...] = h1_b
        us = []
        ff = jnp.zeros((t, D_MODEL), F32)
        for j in range(npan):
            u = _dot(h1_b, wu_ref[j], NN)
            us.append(u)
            r = jnp.maximum(u, 0.0)
            a_b = (r * r).astype(BF16)
            a_ref[:, j * D_MODEL:(j + 1) * D_MODEL] = a_b
            ff = ff + _dot(a_b, wd_ref[j], NN)
        xhat2, rstd2 = _normalize(ALPHA * h1 + ff)
        err = xhat2 * g2_ref[...] + b2_ref[...] - tgt_ref[...]
        stats_ref[4:5, :] += jnp.sum(err * err, axis=0, keepdims=True)
        dpre2, dg2, db2 = _layer_norm_bwd(err * (1.0 / D_MODEL), xhat2, rstd2, g2_ref[...])
        stats_ref[0:1, :] += dg2
        stats_ref[1:2, :] += db2
        dpre2_b = dpre2.astype(BF16)
        dpre2_ref[...] = dpre2_b
        dh1 = ALPHA * dpre2
        for j in range(npan):
            du_b = (_dot(dpre2_b, wd_ref[j], NT) * (2.0 * jnp.maximum(us[j], 0.0))).astype(BF16)
            du_ref[:, j * D_MODEL:(j + 1) * D_MODEL] = du_b
            dh1 = dh1 + _dot(du_b, wu_ref[j], NT)
        dpre1, dg1, db1 = _layer_norm_bwd(dh1, xhat1_v, rstd_ref[...], g1_ref[...])
        stats_ref[2:3, :] += dg1
        stats_ref[3:4, :] += db1
        dpre1_ref[...] = dpre1

    actb = jax.ShapeDtypeStruct((s, D_MODEL), BF16)
    wide = jax.ShapeDtypeStruct((s, D_FF), BF16)
    return pl.pallas_call(
        body, name=name, grid=(s // t,),
        out_shape=(jax.ShapeDtypeStruct((s, D_MODEL), F32), actb, actb, wide, wide, jax.ShapeDtypeStruct((8, D_MODEL), F32)),
        in_specs=[row(D_MODEL), row(1), row(D_MODEL), vec, vec, vec, vec,
                  _resident((npan, D_MODEL, D_MODEL)), _resident((npan, D_MODEL, D_MODEL))],
        out_specs=(row(D_MODEL), row(D_MODEL), row(D_MODEL), row(D_FF), row(D_FF), pl.BlockSpec((8, D_MODEL), lambda i: (0, 0))),
        compiler_params=_params(("arbitrary",)),
    )(xhat1, rstd1, target, ln1_g, ln1_b, ln2_g, ln2_b, wu, wd)


BRANCH_WEIGHTS = ("w_branch_hg", "w_branch_swa", "w_branch_mem")


def _local_step(x, mem, target, wi_parts, wmkv, late, lb_logits, gain, sinks, rel_bias, ln1_g, ln1_b, ln2_g, ln2_b, *, distributed):
    s = x.shape[0]
    tm = min(1024, s)
    tk = min(2048, s)
    xb = x.astype(BF16)
    memb = mem.astype(BF16)
    wia, wib, wic, wid = wi_parts
    if distributed:
        cx, cy, cc = lax.axis_index("x"), lax.axis_index("y"), lax.axis_index("c")
        pos = jnp.stack([2 * cx + cy, cc]).astype(jnp.int32)
    gather = (lambda names: [_gather_exchange([late[k] for k in names])]) if distributed else (lambda names: [])
    to_sibling = (lambda grads: [_sibling_halves_exchange(grads)]) if distributed else (lambda grads: [])
    to_chips = (lambda sums: [_chip_partials_exchange([bf for bf, _ in sums])]) if distributed else (lambda sums: [])

    def chip_sums(names, grads, from_sibling):
        return [_add_sibling(g, o, pos, name="add_sibling_" + k) for k, g, o in zip(names, grads, from_sibling)]

    def shard_sums(names, sums, from_chips):
        return {k: _add_chips(mine, o, pos, name="add_chips_" + k) for k, (_, mine), o in zip(names, sums, from_chips)}

    za = _mm(xb, wia, mode="nt", tm=min(512, s), tn=W_A, tk=D_MODEL, name="proj_a")
    zb = _mm(xb, wib, mode="nt", tm=tm, tn=W_B, tk=D_MODEL, name="proj_b")
    zc = _mm(xb, wic, mode="nt", tm=tm, tn=W_C, tk=D_MODEL, name="proj_c")
    zd = _mm(xb, wid, mode="nt", tm=min(512, s), tn=W_D, tk=D_MODEL, name="proj_d")
    mkv = _mm(memb, wmkv, mode="nn", tm=MEM_LEN, tn=512, tk=D_MODEL, name="mem_kv", out_dtype=BF16, b_panels=True)
    onehot, maskrow = _bias_selector()
    bias_tables, sink_lanes = _swa_tables(_bias_table(rel_bias.T, onehot, maskrow, name="bias_table"), sinks)
    (o_a, o_raw, states), landed = _hgrn_fwd(za, lb_logits, gain, name="hgrn_fwd", exchanges=gather(("w_up", "w_down")))
    wu, wd = landed[0] if distributed else (late["wu"], late["wd"])
    o_b, landed = _swa_fwd(zb, bias_tables, sink_lanes, name="swa_fwd", exchanges=gather(BRANCH_WEIGHTS + ("w_out",)))
    if distributed:
        wbr = jnp.stack([wb.reshape(D_MODEL, D_MODEL) for wb in landed[0][:3]])
        wo = landed[0][3].reshape(D_MODEL, D_MODEL)
    else:
        wbr, wo = late["wbr"], late["wo"]
    o_c = _mem_fwd(zc, mkv, name="mem_fwd")
    xhat1, rstd1, merged, pa, pb, pc = _merge_fwd(o_a, o_b, o_c, zd, x, wbr, wo, name="merge_fwd")

    dpre1, dpre2, h1, act, du, ln_stats = _mlp_loss(xhat1, rstd1, target, ln1_g, ln1_b, ln2_g, ln2_b, wu, wd, name="mlp_loss")
    ffn = ("w_down", "w_up")
    g_ffn = [_mm(act, dpre2, mode="tn", tm=1024, tn=D_MODEL, tk=tk, name="grad_w_down").reshape(N_SHARDS, D_FF // N_SHARDS, D_MODEL),
             _mm(h1, du, mode="tn", tm=D_MODEL, tn=1024, tk=tk, name="grad_w_up", out_panels=True)]

    (dzd, dpa, dpb, dpc, do_a, do_b, do_c), landed = _merge_bwd(dpre1, zd, pa, pb, pc, wbr, wo, name="merge_bwd", exchanges=to_sibling(g_ffn))
    sums_ffn = chip_sums(ffn, g_ffn, landed[0]) if distributed else []
    merge = BRANCH_WEIGHTS + ("w_out",)
    g_merge = [_mm(o, dp, mode="tn", tm=D_MODEL, tn=D_MODEL, tk=tk, name="grad_" + k).reshape(N_SHARDS, D_MODEL // N_SHARDS, D_MODEL)
               for k, o, dp in zip(merge, (o_a, o_b, o_c, merged), (dpa, dpb, dpc, dpre1))]
    (dza, hg_stats), landed = _hgrn_bwd(za, o_raw, do_a, states, lb_logits, gain, name="hgrn_bwd",
                                        exchanges=to_chips(sums_ffn) + to_sibling(g_merge))
    halves = shard_sums(ffn, sums_ffn, landed[0]) if distributed else {}
    sums_merge = chip_sums(merge, g_merge, landed[1]) if distributed else []
    (dzb, dbias_t, dsinks), landed = _swa_bwd(zb, do_b, bias_tables, sink_lanes, name="swa_bwd", exchanges=to_chips(sums_merge))
    if distributed:
        halves.update(shard_sums(merge, sums_merge, landed[0]))
    dbias = dbias_t.reshape(2 * SWA_BLOCK, SWA_HEADS, SWA_BLOCK).transpose(1, 2, 0).reshape(SWA_HEADS, -1)
    d_rel_bias = _bias_grad(dbias, onehot, name="bias_grad").T
    dzc, dmkv = _mem_bwd(zc, do_c, mkv, name="mem_bwd")

    proj = ("w_in", "w_mem_kv")
    g_wi = [_mm(dz, xb, mode="tn", tm=dz.shape[1] if dz.shape[1] <= 1280 else 1024, tn=D_MODEL, tk=tk, name=nm)
            for dz, nm in ((dza, "grad_w_in_a"), (dzb, "grad_w_in_b"), (dzc, "grad_w_in_c"), (dzd, "grad_w_in_d"))]
    g_proj = [jnp.concatenate(g_wi, axis=0).reshape(N_SHARDS, IN_COLS // N_SHARDS, D_MODEL),
              _mm(memb, dmkv, mode="tn", tm=D_MODEL, tn=512, tk=MEM_LEN, name="grad_w_mem_kv", out_panels=True)]
    sums_proj = chip_sums(proj, g_proj, _run_exchanges(to_sibling(g_proj), name="reduce_sibling_proj")[0]) if distributed else []
    grad_x, landed = _dx_matmul([dza, dzb, dzc, dzd], [wia, wib, wic, wid], dpre1, tm=min(512, s), name="grad_x",
                                exchanges=to_chips(sums_proj))
    if distributed:
        halves.update(shard_sums(proj, sums_proj, landed[0]))
    else:
        halves = dict(zip(ffn + merge + proj, g_ffn + g_merge + g_proj))
    small = dict(lb_logits=hg_stats[1:3], hg_norm_gain=hg_stats[0:1], swa_sinks=dsinks, rel_bias=d_rel_bias,
                 ln1_g=ln_stats[2:3], ln1_b=ln_stats[3:4], ln2_g=ln_stats[0:1], ln2_b=ln_stats[1:2], sq_err=ln_stats[4:5])
    return grad_x, halves, small


def _mesh_position():
    x, y, c = lax.axis_index("x"), lax.axis_index("y"), lax.axis_index("c")
    chips = [(1 - x, y), (x, 1 - y), (1 - x, 1 - y)]
    return x, y, c, chips


class _Exchange(NamedTuple):
    operands: list
    out_shapes: list
    n_sems: int
    start: Callable
    finish: Callable


def _gather_exchange(shards):
    n = len(shards)
    per = 7

    def plan(ins, outs, send_sems, recv_sems):
        x, y, c, chips = _mesh_position()
        me = 2 * x + y
        sibling = (x, y, 1 - c)

        def half(a, slot, hc):
            rh = shards[a].shape[0] // 2
            return outs[a].at[slot, pl.ds(hc * rh, rh), :]

        def copy(a, k, src, dst, to):
            return pltpu.make_async_remote_copy(src_ref=src, dst_ref=dst, send_sem=send_sems.at[a * per + k], recv_sem=recv_sems.at[a * per + k],
                                                device_id=to, device_id_type=MESH)

        own = [copy(a, 6, ins[a], outs[a].at[me], sibling) for a in range(n)]
        to_chips = [copy(a, k, ins[a].at[pl.ds(c * (shards[a].shape[0] // 2), shards[a].shape[0] // 2), :], half(a, me, c), (cx, cy, c))
                    for k, (cx, cy) in enumerate(chips) for a in range(n)]
        arrived = [copy(a, k, half(a, 2 * cx + cy, c), half(a, 2 * cx + cy, c), (cx, cy, c)) for k, (cx, cy) in enumerate(chips) for a in range(n)]
        passed_on = [copy(a, 3 + k, half(a, 2 * cx + cy, c), half(a, 2 * cx + cy, c), sibling) for k, (cx, cy) in enumerate(chips) for a in range(n)]
        from_sibling = [copy(a, 3 + k, half(a, 2 * cx + cy, 1 - c), half(a, 2 * cx + cy, 1 - c), sibling)
                        for k, (cx, cy) in enumerate(chips) for a in range(n)]
        own_arrived = [copy(a, 6, outs[a].at[me], outs[a].at[me], sibling) for a in range(n)]
        return own, to_chips, arrived, passed_on, from_sibling, own_arrived

    def start(*refs):
        own, to_chips, _, _, _, _ = plan(*refs)
        for cp in own + to_chips:
            cp.start()

    def finish(*refs):
        own, to_chips, arrived, passed_on, from_sibling, own_arrived = plan(*refs)
        for landed, onward in zip(arrived, passed_on):
            landed.wait_recv()
            onward.start()
        for cp in from_sibling + own_arrived:
            cp.wait_recv()
        for cp in own + to_chips + passed_on:
            cp.wait_send()

    return _Exchange(list(shards), [jax.ShapeDtypeStruct((N_SHARDS,) + w.shape, w.dtype) for w in shards], per * n, start, finish)


def _sibling_halves_exchange(grads):
    n = len(grads)

    def plan(ins, outs, send_sems, recv_sems):
        x, y, c, _ = _mesh_position()
        return [pltpu.make_async_remote_copy(src_ref=ins[a].at[:, pl.ds((1 - c) * (grads[a].shape[1] // 2), grads[a].shape[1] // 2), :],
                                             dst_ref=outs[a], send_sem=send_sems.at[a], recv_sem=recv_sems.at[a],
                                             device_id=(x, y, 1 - c), device_id_type=MESH) for a in range(n)]

    def start(*refs):
        for cp in plan(*refs):
            cp.start()

    def finish(*refs):
        for cp in plan(*refs):
            cp.wait()

    return _Exchange(list(grads), [jax.ShapeDtypeStruct((g.shape[0], g.shape[1] // 2, g.shape[2]), g.dtype) for g in grads], n, start, finish)


def _chip_partials_exchange(sums):
    n = len(sums)

    def plan(ins, outs, send_sems, recv_sems):
        _, _, c, chips = _mesh_position()
        return [pltpu.make_async_remote_copy(src_ref=ins[a].at[2 * cx + cy], dst_ref=outs[a].at[k], send_sem=send_sems.at[a * 3 + k],
                                             recv_sem=recv_sems.at[a * 3 + k], device_id=(cx, cy, c), device_id_type=MESH)
                for k, (cx, cy) in enumerate(chips) for a in range(n)]

    def start(*refs):
        for cp in plan(*refs):
            cp.start()

    def finish(*refs):
        for cp in plan(*refs):
            cp.wait()

    return _Exchange(list(sums), [jax.ShapeDtypeStruct((3,) + g.shape[1:], g.dtype) for g in sums], 3 * n, start, finish)


def _fused_call(body, *, name, grid, in_specs, out_specs, out_shape, scratch_shapes, operands, exchanges=()):
    single = not isinstance(out_shape, (tuple, list))
    out_specs = [out_specs] if single else list(out_specs)
    out_shape = [out_shape] if single else list(out_shape)
    n_in, n_out, n_scr = len(in_specs), len(out_specs), len(scratch_shapes)
    x_in = [len(e.operands) for e in exchanges]
    x_out = [len(e.out_shapes) for e in exchanges]

    def wrapped(*refs):
        refs = list(refs)
        ins = refs[:n_in]
        pos = n_in
        ex_ins = []
        for k in x_in:
            ex_ins.append(refs[pos:pos + k])
            pos += k
        outs = refs[pos:pos + n_out]
        pos += n_out
        ex_outs = []
        for k in x_out:
            ex_outs.append(refs[pos:pos + k])
            pos += k
        scratch = refs[pos:pos + n_scr]
        sems = refs[pos + n_scr:]
        first, last = None, None
        for axis, size in enumerate(grid):
            at_start, at_end = pl.program_id(axis) == 0, pl.program_id(axis) == size - 1
            first = at_start if first is None else first & at_start
            last = at_end if last is None else last & at_end

        @pl.when(first)
        def _():
            for i, e in enumerate(exchanges):
                e.start(ex_ins[i], ex_outs[i], sems[2 * i], sems[2 * i + 1])

        body(*ins, *outs, *scratch)

        @pl.when(last)
        def _():
            for i, e in enumerate(exchanges):
                e.finish(ex_ins[i], ex_outs[i], sems[2 * i], sems[2 * i + 1])

    n_x_in, n_x_out = sum(x_in), sum(x_out)
    results = pl.pallas_call(
        wrapped if exchanges else body, name=name, grid=grid,
        in_specs=list(in_specs) + [HBM] * n_x_in,
        out_specs=out_specs + [HBM] * n_x_out,
        out_shape=out_shape + [s for e in exchanges for s in e.out_shapes],
        scratch_shapes=list(scratch_shapes) + [pltpu.SemaphoreType.DMA((e.n_sems,)) for e in exchanges for _ in range(2)],
        compiler_params=_params(("arbitrary",) * len(grid)),
    )(*operands, *[a for e in exchanges for a in e.operands])
    own = results[0] if single else tuple(results[:n_out])
    landed, pos = [], n_out
    for k in x_out:
        landed.append(list(results[pos:pos + k]))
        pos += k
    return own, landed


def _run_exchanges(exchanges, *, name):
    def body(*refs):
        n_in = sum(len(e.operands) for e in exchanges)
        n_out = sum(len(e.out_shapes) for e in exchanges)
        ins, outs, sems = refs[:n_in], refs[n_in:n_in + n_out], refs[n_in + n_out:]
        spans, i, o = [], 0, 0
        for e in exchanges:
            spans.append((ins[i:i + len(e.operands)], outs[o:o + len(e.out_shapes)]))
            i, o = i + len(e.operands), o + len(e.out_shapes)
        for k, e in enumerate(exchanges):
            e.start(*spans[k], sems[2 * k], sems[2 * k + 1])
        for k, e in enumerate(exchanges):
            e.finish(*spans[k], sems[2 * k], sems[2 * k + 1])

    operands = [a for e in exchanges for a in e.operands]
    shapes = [s for e in exchanges for s in e.out_shapes]
    results = pl.pallas_call(
        body, name=name, out_shape=shapes, in_specs=[HBM] * len(operands), out_specs=[HBM] * len(shapes),
        scratch_shapes=[pltpu.SemaphoreType.DMA((e.n_sems,)) for e in exchanges for _ in range(2)],
    )(*operands)
    landed, pos = [], 0
    for e in exchanges:
        landed.append(list(results[pos:pos + len(e.out_shapes)]))
        pos += len(e.out_shapes)
    return landed


ROW_TILE_MAX = 640
BF16_SUBLANES = 16


def _row_tile(rows):
    for tr in range(min(rows, ROW_TILE_MAX), 0, -1):
        if rows % tr == 0 and tr % BF16_SUBLANES == 0:
            return tr
    raise ValueError(rows)


def _add_sibling(grad, other, pos, *, name):
    p, r, cols = grad.shape
    rh = r // 2
    tr = _row_tile(rh)
    nb = rh // tr

    def body(pos_ref, g_ref, o_ref, sb_ref, mine_ref):
        total = g_ref[...] + o_ref[...]
        sb_ref[...] = total.astype(BF16)

        @pl.when(pl.program_id(1) == pos_ref[0])
        def _():
            mine_ref[...] = total

    return pl.pallas_call(
        body, name=name, out_shape=(jax.ShapeDtypeStruct((p, rh, cols), BF16), jax.ShapeDtypeStruct((rh, cols), F32)),
        grid_spec=pltpu.PrefetchScalarGridSpec(
            num_scalar_prefetch=1, grid=(nb, p),
            in_specs=[pl.BlockSpec((None, tr, cols), lambda i, j, pos_ref: (j, pos_ref[1] * nb + i, 0)),
                      pl.BlockSpec((None, tr, cols), lambda i, j, pos_ref: (j, i, 0))],
            out_specs=(pl.BlockSpec((None, tr, cols), lambda i, j, pos_ref: (j, i, 0)),
                       pl.BlockSpec((tr, cols), lambda i, j, pos_ref: (i, 0)))),
        compiler_params=_params(("parallel", "arbitrary")),
    )(pos, grad, other)


def _add_chips(mine, others, pos, *, name):
    rh, cols = mine.shape
    tr = _row_tile(rh)
    nb = rh // tr

    def body(pos_ref, s_ref, o_ref, r_ref):
        r_ref[...] = ((s_ref[...] + o_ref[0].astype(F32)) + o_ref[1].astype(F32)) + o_ref[2].astype(F32)

    return pl.pallas_call(
        body, name=name, out_shape=jax.ShapeDtypeStruct((2 * rh, cols), F32),
        grid_spec=pltpu.PrefetchScalarGridSpec(
            num_scalar_prefetch=1, grid=(nb,),
            in_specs=[pl.BlockSpec((tr, cols), lambda i, pos_ref: (i, 0)),
                      pl.BlockSpec((3, tr, cols), lambda i, pos_ref: (0, i, 0))],
            out_specs=pl.BlockSpec((tr, cols), lambda i, pos_ref: (pos_ref[1] * nb + i, 0))),
        compiler_params=_params(("parallel",)),
    )(pos, mine, others)


def _join_halves(bufs, *, name):
    n = len(bufs)

    def body(*refs):
        ins, outs = refs[:n], refs[n:2 * n]
        send_sems, recv_sems = refs[2 * n:]
        x, y, c, _ = _mesh_position()

        def copy(a, hc):
            rh = bufs[a].shape[0] // 2
            rows = pl.ds(hc * rh, rh)
            return pltpu.make_async_remote_copy(src_ref=ins[a].at[rows, :], dst_ref=outs[a].at[rows, :], send_sem=send_sems.at[a],
                                                recv_sem=recv_sems.at[a], device_id=(x, y, 1 - c), device_id_type=MESH)

        for a in range(n):
            copy(a, c).start()
        for a in range(n):
            copy(a, c).wait_send()
            copy(a, 1 - c).wait_recv()

    return pl.pallas_call(
        body, name=name, out_shape=[jax.ShapeDtypeStruct(b.shape, b.dtype) for b in bufs],
        in_specs=[HBM] * n, out_specs=[HBM] * n, input_output_aliases={a: a for a in range(n)},
        scratch_shapes=[pltpu.SemaphoreType.DMA((n,)), pltpu.SemaphoreType.DMA((n,))],
    )(*bufs)


def _all_reduce_small(packed, *, name):
    rows, cols = packed.shape

    def body(in_ref, out_ref, gathered, send_sems, recv_sems):
        x, y, c, _ = _mesh_position()
        me = 4 * x + 2 * y + c
        gathered[me] = in_ref[...]
        copies = []
        for d in range(1, 8):
            dx, dy, dc = (d >> 2) & 1, (d >> 1) & 1, d & 1
            peer = (x ^ dx, y ^ dy, c ^ dc)
            cp = pltpu.make_async_remote_copy(src_ref=in_ref, dst_ref=gathered.at[me], send_sem=send_sems.at[d - 1], recv_sem=recv_sems.at[d - 1],
                                              device_id=peer, device_id_type=MESH)
            cp.start()
            copies.append(cp)
        for cp in copies:
            cp.wait()
        total = gathered[0]
        for j in range(1, 8):
            total = total + gathered[j]
        out_ref[...] = total

    vm = pl.BlockSpec(memory_space=pltpu.VMEM)
    return pl.pallas_call(
        body, name=name, out_shape=jax.ShapeDtypeStruct((rows, cols), F32), in_specs=[vm], out_specs=vm,
        scratch_shapes=[pltpu.VMEM((8, rows, cols), F32), pltpu.SemaphoreType.DMA((7,)), pltpu.SemaphoreType.DMA((7,))],
    )(packed)


def _adamw_math(w, g, m, v):
    m = ADAM_B1 * m + (1.0 - ADAM_B1) * g
    v = ADAM_B2 * v + (1.0 - ADAM_B2) * (g * g)
    m_hat = m / (1.0 - ADAM_B1 ** ADAM_STEP)
    v_hat = v / (1.0 - ADAM_B2 ** ADAM_STEP)
    delta = -ADAM_LR * (m_hat / (jnp.sqrt(v_hat) + ADAM_EPS) + ADAM_WD * w)
    return delta, m, v


def _adamw(w, g, m, v, *, name):
    _, rows, cols = w.shape
    tr = _row_tile(rows)
    blk = pl.BlockSpec((None, tr, cols), lambda i: (0, i, 0))
    flat = pl.BlockSpec((tr, cols), lambda i: (i, 0))

    def body(w_ref, g_ref, m_ref, v_ref, go_ref, d_ref, nm_ref, nv_ref):
        g_v = g_ref[...]
        go_ref[...] = g_v
        d_ref[...], nm_ref[...], nv_ref[...] = _adamw_math(w_ref[...], g_v, m_ref[...], v_ref[...])

    shape = jax.ShapeDtypeStruct((1, rows, cols), F32)
    return pl.pallas_call(body, name=name, grid=(rows // tr,), out_shape=(shape,) * 4, in_specs=[blk, flat, blk, blk], out_specs=(blk,) * 4,
                          compiler_params=_params(("parallel",)))(w, g, m, v)


def _adamw_small(w, g, m, v, *, name):
    def body(w_ref, g_ref, m_ref, v_ref, d_ref, nm_ref, nv_ref, loss_ref):
        d_ref[...], nm_ref[...], nv_ref[...] = _adamw_math(w_ref[...], g_ref[...], m_ref[...], v_ref[...])
        loss_ref[...] = (0.5 / D_MODEL) * jnp.sum(g_ref[8:9, :], axis=1, keepdims=True)

    shape = jax.ShapeDtypeStruct(w.shape, F32)
    return pl.pallas_call(body, name=name, out_shape=(shape, shape, shape, jax.ShapeDtypeStruct((1, 1), F32)),
                          compiler_params=_params())(w, g, m, v)


SMALL_ROWS = 16


def _pack_small(lb_logits, gain, sinks, rel_bias, ln1_g, ln1_b, ln2_g, ln2_b, extra=None):
    misc = jnp.concatenate([sinks.reshape(1, -1), rel_bias.reshape(1, -1)], axis=1)
    misc = jnp.pad(misc, ((0, 0), (0, D_MODEL - misc.shape[1])))
    rows = [lb_logits, gain, ln1_g, ln1_b, ln2_g, ln2_b, misc, extra if extra is not None else jnp.zeros((1, D_MODEL), F32)]
    used = sum(r.shape[0] for r in rows)
    return jnp.concatenate(rows + [jnp.zeros((SMALL_ROWS - used, D_MODEL), F32)], axis=0)


def _unpack_small(p):
    return dict(lb_logits=p[0:2], hg_norm_gain=p[2:3], ln1_g=p[3:4], ln1_b=p[4:5], ln2_g=p[5:6], ln2_b=p[6:7],
                swa_sinks=p[7:8, 0:SWA_HEADS], rel_bias=p[7:8, SWA_HEADS:SWA_HEADS + NUM_BUCKETS * SWA_HEADS].reshape(NUM_BUCKETS, SWA_HEADS))


WEIGHTS = ["w_in", "lb_logits", "hg_norm_gain", "swa_sinks", "rel_bias", "w_mem_kv", "w_branch_hg", "w_branch_swa", "w_branch_mem",
           "w_out", "ln1_g", "ln1_b", "w_up", "w_down", "ln2_g", "ln2_b"]
BIG = ["w_in", "w_mem_kv", "w_branch_hg", "w_branch_swa", "w_branch_mem", "w_out", "w_up", "w_down"]
SMALL = ["lb_logits", "hg_norm_gain", "swa_sinks", "rel_bias", "ln1_g", "ln1_b", "ln2_g", "ln2_b"]


def kernel(x, mem, w_in, lb_logits, hg_norm_gain, swa_sinks, rel_bias, w_mem_kv, w_branch_hg, w_branch_swa, w_branch_mem, w_out, ln1_g, ln1_b, w_up, w_down, ln2_g, ln2_b, loss_target, m_w_in, m_lb_logits, m_hg_norm_gain, m_swa_sinks, m_rel_bias, m_w_mem_kv, m_w_branch_hg, m_w_branch_swa, m_w_branch_mem, m_w_out, m_ln1_g, m_ln1_b, m_w_up, m_w_down, m_ln2_g, m_ln2_b, v_w_in, v_lb_logits, v_hg_norm_gain, v_swa_sinks, v_rel_bias, v_w_mem_kv, v_w_branch_hg, v_w_branch_swa, v_w_branch_mem, v_w_out, v_ln1_g, v_ln1_b, v_w_up, v_w_down, v_ln2_g, v_ln2_b):
    w = dict(w_in=w_in, lb_logits=lb_logits, hg_norm_gain=hg_norm_gain, swa_sinks=swa_sinks, rel_bias=rel_bias, w_mem_kv=w_mem_kv,
             w_branch_hg=w_branch_hg, w_branch_swa=w_branch_swa, w_branch_mem=w_branch_mem, w_out=w_out, ln1_g=ln1_g, ln1_b=ln1_b,
             w_up=w_up, w_down=w_down, ln2_g=ln2_g, ln2_b=ln2_b)
    m = dict(w_in=m_w_in, lb_logits=m_lb_logits, hg_norm_gain=m_hg_norm_gain, swa_sinks=m_swa_sinks, rel_bias=m_rel_bias, w_mem_kv=m_w_mem_kv,
             w_branch_hg=m_w_branch_hg, w_branch_swa=m_w_branch_swa, w_branch_mem=m_w_branch_mem, w_out=m_w_out, ln1_g=m_ln1_g, ln1_b=m_ln1_b,
             w_up=m_w_up, w_down=m_w_down, ln2_g=m_ln2_g, ln2_b=m_ln2_b)
    v = dict(w_in=v_w_in, lb_logits=v_lb_logits, hg_norm_gain=v_hg_norm_gain, swa_sinks=v_swa_sinks, rel_bias=v_rel_bias, w_mem_kv=v_w_mem_kv,
             w_branch_hg=v_w_branch_hg, w_branch_swa=v_w_branch_swa, w_branch_mem=v_w_branch_mem, w_out=v_w_out, ln1_g=v_ln1_g, ln1_b=v_ln1_b,
             w_up=v_w_up, w_down=v_w_down, ln2_g=v_ln2_g, ln2_b=v_ln2_b)
    shapes = {k: w[k].shape for k in WEIGHTS}
    for d in (w, m, v):
        d["w_in"] = d["w_in"].reshape(D_MODEL, IN_COLS // N_SHARDS).T[None]
    shards = {k: w[k].reshape(w[k].shape[-2], w[k].shape[-1]).astype(BF16) for k in BIG}
    wi4, wmkv = _run_exchanges([_gather_exchange([shards["w_in"], shards["w_mem_kv"]])], name="gather_weights")[0]
    wi_t = wi4.reshape(IN_COLS, D_MODEL)
    wi_parts = (wi_t[0:W_A], wi_t[W_A:W_A + W_B], wi_t[W_A + W_B:W_A + W_B + W_C], wi_t[W_A + W_B + W_C:])

    grad_x, halves, small = _local_step(
        x.reshape(x.shape[-2], D_MODEL), mem.reshape(MEM_LEN, D_MODEL), loss_target.reshape(loss_target.shape[-2], D_MODEL),
        wi_parts, wmkv, shards, lb_logits, hg_norm_gain, swa_sinks, rel_bias, ln1_g, ln1_b, ln2_g, ln2_b, distributed=True)

    reduced = dict(zip(BIG, _join_halves([halves[k] for k in BIG], name="join_halves")))

    packed_g = _pack_small(small["lb_logits"], small["hg_norm_gain"], small["swa_sinks"], small["rel_bias"], small["ln1_g"], small["ln1_b"],
                           small["ln2_g"], small["ln2_b"], extra=small["sq_err"])
    packed_g = _all_reduce_small(packed_g, name="reduce_small")

    grad_out, delta_out, m_out, v_out = {}, {}, {}, {}
    for k in BIG:
        grad_out[k], delta_out[k], m_out[k], v_out[k] = _adamw(w[k], reduced[k], m[k], v[k], name="adamw_" + k)
    pack = lambda d: _pack_small(d["lb_logits"], d["hg_norm_gain"], d["swa_sinks"], d["rel_bias"], d["ln1_g"], d["ln1_b"], d["ln2_g"], d["ln2_b"])
    d_s, m_s, v_s, loss = _adamw_small(pack(w), packed_g, pack(m), pack(v), name="adamw_small")
    for out, p in ((grad_out, packed_g), (delta_out, d_s), (m_out, m_s), (v_out, v_s)):
        out["w_in"] = out["w_in"][0].T
        out.update(_unpack_small(p))

    result = [loss.reshape(()), grad_x.reshape(x.shape)]
    for out in (grad_out, delta_out, m_out, v_out):
        result += [out[k].reshape(shapes[k]) for k in WEIGHTS]
    return tuple(result)
```

```python
import math
from typing import Callable, NamedTuple

import jax
import jax.numpy as jnp
from jax import lax
from jax.experimental import pallas as pl
from jax.experimental.pallas import tpu as pltpu

F32 = jnp.float32
BF16 = jnp.bfloat16
HIGHEST = lax.Precision.HIGHEST
MESH = pl.DeviceIdType.MESH

D_MODEL = 1024
MEM_LEN = 256
HG_HEADS = 8
HG_DK = 128
HG_CHUNK = 64
SWA_HEADS = 16
SWA_KV_HEADS = 2
SWA_GROUP = 8
SWA_HEAD_DIM = 64
SWA_BLOCK = 128
SWA_WINDOW = 128
MEM_HEADS = 4
MEM_HEAD_DIM = 256
NUM_BUCKETS = 32
MAX_DISTANCE = 128
D_FF = 4096
LN_EPS = 1e-5
RMS_EPS = 1e-6
ALPHA = 2.0 ** 0.25
W_A, W_B, W_C, W_D = 4096, 1280, 1024, 3072
IN_COLS = W_A + W_B + W_C + W_D
N_SHARDS = 4
ADAM_LR = 0.001
ADAM_B1 = 0.9
ADAM_B2 = 0.999
ADAM_EPS = 1e-08
ADAM_WD = 0.01
ADAM_STEP = 10
MASK_VALUE = -1e30
VMEM_LIMIT = 56 * 1024 * 1024

NN = ((1,), (0,))
NT = ((1,), (1,))
TN = ((0,), (0,))
HBM = pl.BlockSpec(memory_space=pltpu.HBM)


def _dot(a, b, dims=NN, precision=None):
    return lax.dot_general(a, b, (dims, ((), ())), precision=precision, preferred_element_type=F32)


def _params(sem=None):
    return pltpu.CompilerParams(dimension_semantics=sem, vmem_limit_bytes=VMEM_LIMIT)


def _resident(shape):
    zeros = (0,) * len(shape)
    return pl.BlockSpec(shape, lambda *_: zeros, pipeline_mode=pl.Buffered(1))


def _mm(a, b, *, mode, tm, tn, tk, name, out_dtype=F32, b_panels=False, out_panels=False, add=None, add_scale=1.0):
    if mode == "tn":
        kdim, m = a.shape
    else:
        m, kdim = a.shape
    if b_panels:
        n = b.shape[0] * b.shape[2]
        assert b.shape[2] == tn and mode == "nn"
    elif mode == "nt":
        n = b.shape[0]
    else:
        n = b.shape[1]
    assert m % tm == 0 and n % tn == 0 and kdim % tk == 0, (name, m, n, kdim)
    nk = kdim // tk
    dims = {"nn": NN, "nt": NT, "tn": TN}[mode]
    a_spec = pl.BlockSpec((tk, tm), lambda i, j, k: (k, i)) if mode == "tn" else pl.BlockSpec((tm, tk), lambda i, j, k: (i, k))
    if b_panels:
        b_spec = pl.BlockSpec((None, tk, tn), lambda i, j, k: (j, k, 0))
    elif mode == "nt":
        b_spec = pl.BlockSpec((tn, tk), lambda i, j, k: (j, k))
    else:
        b_spec = pl.BlockSpec((tk, tn), lambda i, j, k: (k, j))
    if out_panels:
        out_shape = jax.ShapeDtypeStruct((n // tn, m, tn), out_dtype)
        o_spec = pl.BlockSpec((None, tm, tn), lambda i, j, k: (j, i, 0))
    else:
        out_shape = jax.ShapeDtypeStruct((m, n), out_dtype)
        o_spec = pl.BlockSpec((tm, tn), lambda i, j, k: (i, j))
    in_specs = [a_spec, b_spec]
    operands = [a, b]
    if add is not None:
        in_specs.append(pl.BlockSpec((tm, tn), lambda i, j, k: (i, j)))
        operands.append(add)

    def body(*refs):
        a_ref, b_ref = refs[0], refs[1]
        add_ref = refs[2] if add is not None else None
        o_ref = refs[3] if add is not None else refs[2]
        part = _dot(a_ref[...].astype(BF16), b_ref[...].astype(BF16), dims)

        def finish(acc):
            if add_ref is not None:
                acc = acc + add_scale * add_ref[...]
            o_ref[...] = acc.astype(out_dtype)

        if nk == 1:
            finish(part)
        else:
            acc_ref = refs[-1]
            k = pl.program_id(2)

            @pl.when(k == 0)
            def _():
                acc_ref[...] = part

            @pl.when(k > 0)
            def _():
                acc_ref[...] += part

            @pl.when(k == nk - 1)
            def _():
                finish(acc_ref[...])

    return pl.pallas_call(
        body, name=name, out_shape=out_shape, grid=(m // tm, n // tn, nk), in_specs=in_specs, out_specs=o_spec,
        scratch_shapes=[pltpu.VMEM((tm, tn), F32)] if nk > 1 else [],
        compiler_params=_params(("parallel", "parallel", "arbitrary")),
    )(*operands)


def _dx_matmul(dzs, wis, resid, *, tm, name, exchanges=()):
    s = resid.shape[0]
    npieces = len(dzs)
    in_specs = [pl.BlockSpec((tm, dz.shape[1]), lambda i: (i, 0)) for dz in dzs]
    in_specs += [_resident(w.shape) for w in wis]
    in_specs += [pl.BlockSpec((tm, D_MODEL), lambda i: (i, 0))]

    def body(*refs):
        dz_refs, w_refs = refs[:npieces], refs[npieces:2 * npieces]
        r_ref, o_ref = refs[2 * npieces], refs[2 * npieces + 1]
        total = ALPHA * r_ref[...]
        for p in range(npieces):
            total = total + _dot(dz_refs[p][...], w_refs[p][...], NN)
        o_ref[...] = total

    return _fused_call(
        body, name=name, out_shape=jax.ShapeDtypeStruct((s, D_MODEL), F32), grid=(s // tm,), in_specs=in_specs,
        out_specs=pl.BlockSpec((tm, D_MODEL), lambda i: (i, 0)), scratch_shapes=[],
        operands=[*dzs, *wis, resid], exchanges=exchanges)


def _lower_bound(lbl_ref):
    l0, l1 = lbl_ref[0:1, :], lbl_ref[1:2, :]
    mx = jnp.maximum(l0, l1)
    e0, e1 = jnp.exp(l0 - mx), jnp.exp(l1 - mx)
    return e0 / (e0 + e1)


HEAD_COLS = [slice(h * HG_DK, (h + 1) * HG_DK) for h in range(HG_HEADS)]


def _head_mean(x):
    return jnp.concatenate([jnp.broadcast_to(jnp.mean(x[:, c], axis=-1, keepdims=True), (x.shape[0], HG_DK)) for c in HEAD_COLS], axis=1)


def _chunk_forward(q, fl, v, lb, tril_f):
    sg = jax.nn.sigmoid(fl)
    f = lb + (1.0 - lb) * sg
    k = 1.0 - f
    b = _dot(tril_f, jnp.log(f), NN, HIGHEST)
    b_last = b[HG_CHUNK - 1:HG_CHUNK, :]
    eb, enb, eo = jnp.exp(b), jnp.exp(-b), jnp.exp(b_last - b)
    return sg, f, k, b_last, eb, enb, eo, q * eb, k * enb, k * eo


def _hgrn_fwd(za, lb_logits, gain, *, name, exchanges=()):
    s = za.shape[0]
    t = min(256, s)
    ncs = t // HG_CHUNK

    def body(z_ref, lbl_ref, gain_ref, oa_ref, oraw_ref, st_ref, state):
        @pl.when(pl.program_id(0) == 0)
        def _():
            state[...] = jnp.zeros_like(state)

        lb_all = _lower_bound(lbl_ref)
        row = lax.broadcasted_iota(jnp.int32, (HG_CHUNK, HG_CHUNK), 0)
        col = lax.broadcasted_iota(jnp.int32, (HG_CHUNK, HG_CHUNK), 1)
        tril = row >= col
        tril_f = tril.astype(F32)
        gain_all = gain_ref[...]

        def chunk(i, carry):
            r = pl.ds(pl.multiple_of(i * HG_CHUNK, HG_CHUNK), HG_CHUNK)
            q, fl, v, hg = (z_ref[r, j * D_MODEL:(j + 1) * D_MODEL] for j in range(4))
            _, _, _, b_last, _, _, _, q_in, k_in, k_out = _chunk_forward(q, fl, v, lb_all, tril_f)
            q_in_b, k_in_b, k_out_b, vb = (u.astype(BF16) for u in (q_in, k_in, k_out, v))
            decay = jnp.exp(b_last)
            sts = [state[h] for h in range(HG_HEADS)]
            attn = [_dot(q_in_b[:, c], k_in_b[:, c], NT) for c in HEAD_COLS]
            inter = [_dot(q_in_b[:, c], sts[h].astype(BF16), NT) for h, c in enumerate(HEAD_COLS)]
            upd = [_dot(vb[:, c], k_out_b[:, c], TN) for c in HEAD_COLS]
            attn = [jnp.where(tril, a, 0.0).astype(BF16) for a in attn]
            outs = [_dot(attn[h], vb[:, c], NN) + inter[h] for h, c in enumerate(HEAD_COLS)]
            for h, c in enumerate(HEAD_COLS):
                st_ref[h, i] = sts[h]
                state[h] = sts[h] * decay[:, c] + upd[h]
            o = jnp.concatenate(outs, axis=1)
            oraw_ref[r, :] = o
            n = o * lax.rsqrt(_head_mean(o * o) + RMS_EPS)
            oa_ref[r, :] = (n * gain_all * (hg * jax.nn.sigmoid(hg))).astype(BF16)
            return carry

        lax.fori_loop(0, ncs, chunk, 0, unroll=True)

    return _fused_call(
        body, name=name, grid=(s // t,),
        out_shape=(jax.ShapeDtypeStruct((s, D_MODEL), BF16), jax.ShapeDtypeStruct((s, D_MODEL), F32),
                   jax.ShapeDtypeStruct((HG_HEADS, s // HG_CHUNK, HG_DK, HG_DK), F32)),
        in_specs=[pl.BlockSpec((t, W_A), lambda i: (i, 0)), _resident((2, D_MODEL)), _resident((1, D_MODEL))],
        out_specs=(pl.BlockSpec((t, D_MODEL), lambda i: (i, 0)), pl.BlockSpec((t, D_MODEL), lambda i: (i, 0)),
                   pl.BlockSpec((HG_HEADS, ncs, HG_DK, HG_DK), lambda i: (0, i, 0, 0))),
        scratch_shapes=[pltpu.VMEM((HG_HEADS, HG_DK, HG_DK), F32)],
        operands=[za, lb_logits, gain], exchanges=exchanges)


def _hgrn_bwd(za, oraw, do_a, states, lb_logits, gain, *, name, exchanges=()):
    s = za.shape[0]
    t = min(256, s)
    ncs = t // HG_CHUNK
    nt = s // t

    def body(z_ref, oraw_ref, do_ref, st_ref, lbl_ref, gain_ref, dz_ref, stats_ref, dstate):
        step = pl.program_id(0)

        @pl.when(step == 0)
        def _():
            dstate[...] = jnp.zeros_like(dstate)
            stats_ref[...] = jnp.zeros_like(stats_ref)

        lb_all = _lower_bound(lbl_ref)
        row = lax.broadcasted_iota(jnp.int32, (HG_CHUNK, HG_CHUNK), 0)
        col = lax.broadcasted_iota(jnp.int32, (HG_CHUNK, HG_CHUNK), 1)
        tril = row >= col
        tril_f = tril.astype(F32)
        triu_f = (row <= col).astype(F32)
        gain_all = gain_ref[...]

        def chunk(ii, carry):
            i = ncs - 1 - ii
            r = pl.ds(pl.multiple_of(i * HG_CHUNK, HG_CHUNK), HG_CHUNK)
            q, fl, v, hg = (z_ref[r, j * D_MODEL:(j + 1) * D_MODEL] for j in range(4))
            o = oraw_ref[r, :]
            doa = do_ref[r, :]
            rms = lax.rsqrt(_head_mean(o * o) + RMS_EPS)
            n = o * rms
            sgg = jax.nn.sigmoid(hg)
            silu = hg * sgg
            dhg = doa * n * gain_all * (sgg * (1.0 + hg * (1.0 - sgg)))
            dgain = jnp.sum(doa * n * silu, axis=0, keepdims=True)
            dn = doa * gain_all * silu
            do = rms * (dn - n * _head_mean(dn * n))
            sg, f, k, b_last, eb, enb, eo, q_in, k_in, k_out = _chunk_forward(q, fl, v, lb_all, tril_f)
            q_in_b, k_in_b, k_out_b, vb, dob = (u.astype(BF16) for u in (q_in, k_in, k_out, v, do))
            decay = jnp.exp(b_last)
            sts = [st_ref[h, i] for h in range(HG_HEADS)]
            dsts = [dstate[h] for h in range(HG_HEADS)]
            dsts_b = [d.astype(BF16) for d in dsts]
            heads = list(enumerate(HEAD_COLS))
            attn = [_dot(q_in_b[:, c], k_in_b[:, c], NT) for h, c in heads]
            dattn = [_dot(dob[:, c], vb[:, c], NT) for h, c in heads]
            dq_st = [_dot(dob[:, c], sts[h].astype(BF16), NN) for h, c in heads]
            dk_out = [_dot(vb[:, c], dsts_b[h], NN) for h, c in heads]
            dv_st = [_dot(k_out_b[:, c], dsts_b[h], NT) for h, c in heads]
            dst_o = [_dot(dob[:, c], q_in_b[:, c], TN) for h, c in heads]
            attn = [jnp.where(tril, a, 0.0).astype(BF16) for a in attn]
            dattn = [jnp.where(tril, a, 0.0).astype(BF16) for a in dattn]
            dq_in = jnp.concatenate([_dot(dattn[h], k_in_b[:, c], NN) + dq_st[h] for h, c in heads], axis=1)
            dk_in = jnp.concatenate([_dot(dattn[h], q_in_b[:, c], TN) for h, c in heads], axis=1)
            dv = jnp.concatenate([_dot(attn[h], dob[:, c], TN) + dv_st[h] for h, c in heads], axis=1)
            dk_out = jnp.concatenate(dk_out, axis=1)
            dst_st = jnp.concatenate([jnp.sum(dsts[h] * sts[h], axis=0, keepdims=True) for h in range(HG_HEADS)], axis=1)
            for h, c in heads:
                dstate[h] = dsts[h] * decay[:, c] + dst_o[h]
            db_last = decay * dst_st + jnp.sum(dk_out * k_out, axis=0, keepdims=True)
            db = dq_in * q_in - dk_in * k_in - dk_out * k_out
            dg = _dot(triu_f, db, NN, HIGHEST) + db_last
            dk = dk_in * enb + dk_out * eo
            df = dg / f - dk
            stats_ref[0:1, :] += dgain
            stats_ref[1:2, :] += jnp.sum(df * (1.0 - sg), axis=0, keepdims=True)
            dz_ref[r, 0:1024] = (dq_in * eb).astype(BF16)
            dz_ref[r, 1024:2048] = (df * (1.0 - lb_all) * sg * (1.0 - sg)).astype(BF16)
            dz_ref[r, 2048:3072] = dv.astype(BF16)
            dz_ref[r, 3072:4096] = dhg.astype(BF16)
            return carry

        lax.fori_loop(0, ncs, chunk, 0, unroll=True)

        @pl.when(step == nt - 1)
        def _():
            dl0 = stats_ref[1:2, :] * lb_all * (1.0 - lb_all)
            stats_ref[1:2, :] = dl0
            stats_ref[2:3, :] = -dl0

    rev = lambda i: (nt - 1 - i, 0)
    return _fused_call(
        body, name=name, grid=(nt,),
        out_shape=(jax.ShapeDtypeStruct((s, W_A), BF16), jax.ShapeDtypeStruct((8, D_MODEL), F32)),
        in_specs=[pl.BlockSpec((t, W_A), rev), pl.BlockSpec((t, D_MODEL), rev), pl.BlockSpec((t, D_MODEL), rev),
                  pl.BlockSpec((HG_HEADS, ncs, HG_DK, HG_DK), lambda i: (0, nt - 1 - i, 0, 0)),
                  _resident((2, D_MODEL)), _resident((1, D_MODEL))],
        out_specs=(pl.BlockSpec((t, W_A), rev), pl.BlockSpec((8, D_MODEL), lambda i: (0, 0))),
        scratch_shapes=[pltpu.VMEM((HG_HEADS, HG_DK, HG_DK), F32)],
        operands=[za, oraw, do_a, states, lb_logits, gain], exchanges=exchanges)


def _t5_bucket(n):
    max_exact = NUM_BUCKETS // 2
    nf = jnp.maximum(n, 1).astype(F32)
    large = max_exact + (jnp.log(nf / max_exact) / math.log(MAX_DISTANCE / max_exact) * (NUM_BUCKETS - max_exact)).astype(jnp.int32)
    large = jnp.minimum(large, NUM_BUCKETS - 1)
    return jnp.where(n < max_exact, n, large)


def _bias_selector():
    qi = jnp.arange(SWA_BLOCK)[:, None] + SWA_BLOCK
    kj = jnp.arange(2 * SWA_BLOCK)[None, :]
    dist = qi - kj
    band = ((dist >= 0) & (dist < SWA_WINDOW)).reshape(1, -1)
    bucket = _t5_bucket(jnp.clip(dist, 0, SWA_WINDOW - 1)).reshape(1, -1)
    onehot = ((bucket == jnp.arange(NUM_BUCKETS)[:, None]) & band).astype(F32)
    return onehot, jnp.where(band, 0.0, MASK_VALUE).astype(F32)


def _bias_table(rel_bias_t, onehot, maskrow, *, name):
    def body(rb_ref, oh_ref, mask_ref, o_ref):
        o_ref[...] = _dot(rb_ref[...], oh_ref[...], NN, HIGHEST) + mask_ref[...]

    return pl.pallas_call(body, name=name, out_shape=jax.ShapeDtypeStruct((SWA_HEADS, onehot.shape[1]), F32),
                          compiler_params=_params())(rel_bias_t, onehot, maskrow)


def _bias_grad(dbias2d, onehot, *, name):
    def body(db_ref, oh_ref, o_ref):
        o_ref[...] = _dot(db_ref[...], oh_ref[...], NT, HIGHEST)

    return pl.pallas_call(body, name=name, out_shape=jax.ShapeDtypeStruct((SWA_HEADS, NUM_BUCKETS), F32),
                          compiler_params=_params())(dbias2d, onehot)


GROUP_LANES = SWA_GROUP * SWA_BLOCK


def _swa_operands(zq_ref, kv_cur_ref, kv_prev_ref):
    q = (zq_ref[:, 0:1024] * (SWA_HEAD_DIM ** -0.5)).astype(BF16)
    kv_c = kv_cur_ref[...].astype(BF16)
    kv_p = kv_prev_ref[...].astype(BF16)
    kks = [jnp.concatenate([kv_p[:, g * 64:(g + 1) * 64], kv_c[:, g * 64:(g + 1) * 64]], axis=0) for g in range(SWA_KV_HEADS)]
    vvs = [jnp.concatenate([kv_p[:, 128 + g * 64:128 + (g + 1) * 64], kv_c[:, 128 + g * 64:128 + (g + 1) * 64]], axis=0)
           for g in range(SWA_KV_HEADS)]
    return q, kks, vvs


def _stack_heads(x, g):
    return jnp.concatenate([x[:, h * SWA_HEAD_DIM:(h + 1) * SWA_HEAD_DIM] for h in range(g * SWA_GROUP, (g + 1) * SWA_GROUP)], axis=0)


def _heads_to_lanes(xt):
    pairs = []
    for j in range(0, SWA_GROUP, 2):
        two = jnp.concatenate([xt[:, j * SWA_BLOCK:(j + 1) * SWA_BLOCK], xt[:, (j + 1) * SWA_BLOCK:(j + 2) * SWA_BLOCK]], axis=0)
        pairs.append(two.T)
    return jnp.concatenate(pairs, axis=1)


def _swa_softmax(score_t, bias_ref, sink_ref, g):
    lanes = slice(g * GROUP_LANES, (g + 1) * GROUP_LANES)
    sc = score_t + bias_ref[:, lanes]
    sink = sink_ref[:, lanes]
    m = jnp.maximum(jnp.max(sc, axis=0, keepdims=True), sink)
    e = jnp.exp(sc - m)
    e_sink = jnp.exp(sink - m)
    return e, 1.0 / (jnp.sum(e, axis=0, keepdims=True) + e_sink), e_sink


def _swa_tables(bias2d, sinks):
    bias_t = bias2d.reshape(SWA_HEADS, SWA_BLOCK, 2 * SWA_BLOCK).transpose(2, 0, 1).reshape(2 * SWA_BLOCK, SWA_HEADS * SWA_BLOCK)
    first = jnp.where(jnp.arange(2 * SWA_BLOCK)[:, None] < SWA_BLOCK, MASK_VALUE, bias_t)
    return jnp.stack([first, bias_t]), jnp.repeat(sinks, SWA_BLOCK, axis=1)


def _swa_fwd(zb, bias_tables, sink_lanes, *, name, exchanges=()):
    s = zb.shape[0]
    nb = s // SWA_BLOCK

    def body(zq_ref, kvc_ref, kvp_ref, bias_ref, sink_ref, o_ref):
        q, kks, vvs = _swa_operands(zq_ref, kvc_ref, kvp_ref)
        groups = range(SWA_KV_HEADS)
        scores = [_dot(kks[g], _stack_heads(q, g), NT) for g in groups]
        probs = []
        for g in groups:
            e, inv, _ = _swa_softmax(scores[g], bias_ref, sink_ref, g)
            probs.append((e * inv).astype(BF16))
        outs = [_dot(vvs[g], probs[g], TN) for g in groups]
        o_ref[...] = jnp.concatenate([_heads_to_lanes(outs[g]) for g in groups], axis=1).astype(BF16)

    return _fused_call(
        body, name=name, grid=(nb,), out_shape=jax.ShapeDtypeStruct((s, D_MODEL), BF16),
        in_specs=[pl.BlockSpec((SWA_BLOCK, W_B), lambda n: (n, 0)),
                  pl.BlockSpec((SWA_BLOCK, 256), lambda n: (n, 4)),
                  pl.BlockSpec((SWA_BLOCK, 256), lambda n: (jnp.maximum(n - 1, 0), 4)),
                  pl.BlockSpec((None, 2 * SWA_BLOCK, SWA_HEADS * SWA_BLOCK), lambda n: (jnp.minimum(n, 1), 0, 0)),
                  _resident((1, SWA_HEADS * SWA_BLOCK))],
        out_specs=pl.BlockSpec((SWA_BLOCK, D_MODEL), lambda n: (n, 0)), scratch_shapes=[],
        operands=[zb, zb, zb, bias_tables, sink_lanes], exchanges=exchanges)


def _swa_bwd(zb, do_b, bias_tables, sink_lanes, *, name, exchanges=()):
    s = zb.shape[0]
    nb = s // SWA_BLOCK
    scale = SWA_HEAD_DIM ** -0.5

    def body(zq_ref, kvc_ref, kvp_ref, do_ref, bias_ref, sink_ref, dz_ref, dbias_ref, dsink_ref, carry, dsink_acc):
        step = pl.program_id(0)

        @pl.when(step == 0)
        def _():
            carry[...] = jnp.zeros_like(carry)
            dsink_acc[...] = jnp.zeros_like(dsink_acc)
            dbias_ref[...] = jnp.zeros_like(dbias_ref)

        q, kks, vvs = _swa_operands(zq_ref, kvc_ref, kvp_ref)
        groups = range(SWA_KV_HEADS)
        do = do_ref[...].astype(BF16)
        q_rows = [_stack_heads(q, g) for g in groups]
        do_rows = [_stack_heads(do, g) for g in groups]
        scores = [_dot(kks[g], q_rows[g], NT) for g in groups]
        dps = [_dot(vvs[g], do_rows[g], NT) for g in groups]
        ps, dss = [], []
        for g in groups:
            lanes = slice(g * GROUP_LANES, (g + 1) * GROUP_LANES)
            e, inv, e_sink = _swa_softmax(scores[g], bias_ref, sink_ref, g)
            p = e * inv
            delta = jnp.sum(p * dps[g], axis=0, keepdims=True)
            ds = p * (dps[g] - delta)
            dbias_ref[:, lanes] += ds
            dsink_acc[:, lanes] -= e_sink * inv * delta
            ps.append(p.astype(BF16))
            dss.append(ds.astype(BF16))
        dqs = [_dot(kks[g], dss[g], TN) * scale for g in groups]
        dkks = [_dot(dss[g], q_rows[g], NN) for g in groups]
        dvvs = [_dot(ps[g], do_rows[g], NN) for g in groups]
        dkv = jnp.concatenate(dkks + dvvs, axis=1)
        dz_ref[:, 0:1024] = jnp.concatenate([_heads_to_lanes(dqs[g]) for g in groups], axis=1).astype(BF16)
        dz_ref[:, 1024:1280] = (dkv[SWA_BLOCK:, :] + carry[...]).astype(BF16)
        carry[...] = dkv[:SWA_BLOCK, :]

        @pl.when(step == nb - 1)
        def _():
            acc = dsink_acc[...]
            dsink_ref[...] = jnp.concatenate([jnp.sum(acc[:, h * SWA_BLOCK:(h + 1) * SWA_BLOCK], axis=1, keepdims=True)
                                              for h in range(SWA_HEADS)], axis=1)

    rev = lambda i: (nb - 1 - i, 0)
    table_shape = (2 * SWA_BLOCK, SWA_HEADS * SWA_BLOCK)
    return _fused_call(
        body, name=name, grid=(nb,),
        out_shape=(jax.ShapeDtypeStruct((s, W_B), BF16), jax.ShapeDtypeStruct(table_shape, F32), jax.ShapeDtypeStruct((1, SWA_HEADS), F32)),
        in_specs=[pl.BlockSpec((SWA_BLOCK, W_B), rev),
                  pl.BlockSpec((SWA_BLOCK, 256), lambda i: (nb - 1 - i, 4)),
                  pl.BlockSpec((SWA_BLOCK, 256), lambda i: (jnp.maximum(nb - 2 - i, 0), 4)),
                  pl.BlockSpec((SWA_BLOCK, D_MODEL), rev),
                  pl.BlockSpec((None,) + table_shape, lambda i: (jnp.minimum(nb - 1 - i, 1), 0, 0)),
                  _resident((1, SWA_HEADS * SWA_BLOCK))],
        out_specs=(pl.BlockSpec((SWA_BLOCK, W_B), rev), pl.BlockSpec(table_shape, lambda i: (0, 0)),
                   pl.BlockSpec((1, SWA_HEADS), lambda i: (0, 0))),
        scratch_shapes=[pltpu.VMEM((SWA_BLOCK, 256), F32), pltpu.VMEM((1, SWA_HEADS * SWA_BLOCK), F32)],
        operands=[zb, zb, zb, do_b, bias_tables, sink_lanes], exchanges=exchanges)


def _mem_probs(zc_ref, mkv_ref, h):
    cols = slice(h * MEM_HEAD_DIM, (h + 1) * MEM_HEAD_DIM)
    qh = (zc_ref[:, cols] * (MEM_HEAD_DIM ** -0.5)).astype(BF16)
    sc = _dot(qh, mkv_ref[:, cols], NT)
    e = jnp.exp(sc - jnp.max(sc, axis=-1, keepdims=True))
    return qh, e / jnp.sum(e, axis=-1, keepdims=True)


def _mem_fwd(zc, mkv, *, name):
    s = zc.shape[0]
    t = min(512, s)

    def body(zc_ref, mkv_ref, o_ref):
        for h in range(MEM_HEADS):
            _, p = _mem_probs(zc_ref, mkv_ref, h)
            vh = mkv_ref[:, D_MODEL + h * MEM_HEAD_DIM:D_MODEL + (h + 1) * MEM_HEAD_DIM]
            o_ref[:, h * MEM_HEAD_DIM:(h + 1) * MEM_HEAD_DIM] = _dot(p.astype(BF16), vh, NN).astype(BF16)

    return pl.pallas_call(
        body, name=name, grid=(s // t,), out_shape=jax.ShapeDtypeStruct((s, D_MODEL), BF16),
        in_specs=[pl.BlockSpec((t, D_MODEL), lambda i: (i, 0)), _resident((MEM_LEN, 2 * D_MODEL))],
        out_specs=pl.BlockSpec((t, D_MODEL), lambda i: (i, 0)), compiler_params=_params(("parallel",)),
    )(zc, mkv)


def _mem_bwd(zc, do_c, mkv, *, name):
    s = zc.shape[0]
    t = min(512, s)

    def body(zc_ref, do_ref, mkv_ref, dz_ref, dmkv_ref):
        @pl.when(pl.program_id(0) == 0)
        def _():
            dmkv_ref[...] = jnp.zeros_like(dmkv_ref)

        for h in range(MEM_HEADS):
            cols = slice(h * MEM_HEAD_DIM, (h + 1) * MEM_HEAD_DIM)
            vcols = slice(D_MODEL + h * MEM_HEAD_DIM, D_MODEL + (h + 1) * MEM_HEAD_DIM)
            qh, p = _mem_probs(zc_ref, mkv_ref, h)
            doh = do_ref[:, cols].astype(BF16)
            dp = _dot(doh, mkv_ref[:, vcols], NT)
            ds = (p * (dp - jnp.sum(p * dp, axis=-1, keepdims=True))).astype(BF16)
            dz_ref[:, cols] = (_dot(ds, mkv_ref[:, cols], NN) * (MEM_HEAD_DIM ** -0.5)).astype(BF16)
            dmkv_ref[:, cols] += _dot(ds, qh, TN)
            dmkv_ref[:, vcols] += _dot(p.astype(BF16), doh, TN)

    return pl.pallas_call(
        body, name=name, grid=(s // t,),
        out_shape=(jax.ShapeDtypeStruct((s, D_MODEL), BF16), jax.ShapeDtypeStruct((MEM_LEN, 2 * D_MODEL), F32)),
        in_specs=[pl.BlockSpec((t, D_MODEL), lambda i: (i, 0)), pl.BlockSpec((t, D_MODEL), lambda i: (i, 0)),
                  _resident((MEM_LEN, 2 * D_MODEL))],
        out_specs=(pl.BlockSpec((t, D_MODEL), lambda i: (i, 0)), pl.BlockSpec((MEM_LEN, 2 * D_MODEL), lambda i: (0, 0))),
        compiler_params=_params(("arbitrary",)),
    )(zc, do_c, mkv)


def _normalize(pre):
    mu = jnp.mean(pre, axis=-1, keepdims=True)
    xc = pre - mu
    rstd = lax.rsqrt(jnp.mean(xc * xc, axis=-1, keepdims=True) + LN_EPS)
    return xc * rstd, rstd


def _layer_norm_bwd(dh, xhat, rstd, g):
    dxh = dh * g
    dpre = rstd * (dxh - jnp.mean(dxh, axis=-1, keepdims=True) - xhat * jnp.mean(dxh * xhat, axis=-1, keepdims=True))
    return dpre, jnp.sum(dh * xhat, axis=0, keepdims=True), jnp.sum(dh, axis=0, keepdims=True)


def _merge_fwd(o_a, o_b, o_c, zd, x, wbr, wo, *, name):
    s = x.shape[0]
    t = min(256, s)
    row = lambda w: pl.BlockSpec((t, w), lambda i: (i, 0))

    def body(oa_ref, ob_ref, oc_ref, zd_ref, x_ref, wbr_ref, wo_ref, xhat_ref, rstd_ref, merged_ref, pa_ref, pb_ref, pc_ref):
        merged = jnp.zeros((t, D_MODEL), F32)
        for b, (o_ref, p_ref) in enumerate(((oa_ref, pa_ref), (ob_ref, pb_ref), (oc_ref, pc_ref))):
            p = _dot(o_ref[...], wbr_ref[b], NN)
            p_ref[...] = p.astype(BF16)
            merged = merged + jax.nn.sigmoid(zd_ref[:, b * D_MODEL:(b + 1) * D_MODEL]) * p
        merged_b = merged.astype(BF16)
        merged_ref[...] = merged_b
        xhat, rstd = _normalize(ALPHA * x_ref[...] + _dot(merged_b, wo_ref[...], NN))
        xhat_ref[...] = xhat
        rstd_ref[...] = rstd

    act = jax.ShapeDtypeStruct((s, D_MODEL), F32)
    return pl.pallas_call(
        body, name=name, grid=(s // t,),
        out_shape=(act, jax.ShapeDtypeStruct((s, 1), F32)) + (jax.ShapeDtypeStruct((s, D_MODEL), BF16),) * 4,
        in_specs=[row(D_MODEL), row(D_MODEL), row(D_MODEL), row(W_D), row(D_MODEL),
                  _resident((3, D_MODEL, D_MODEL)), _resident((D_MODEL, D_MODEL))],
        out_specs=(row(D_MODEL), row(1), row(D_MODEL), row(D_MODEL), row(D_MODEL), row(D_MODEL)),
        compiler_params=_params(("parallel",)),
    )(o_a, o_b, o_c, zd, x, wbr, wo)


def _merge_bwd(dpre1, zd, pa, pb, pc, wbr, wo, *, name, exchanges=()):
    s = dpre1.shape[0]
    t = min(256, s)
    row = lambda w: pl.BlockSpec((t, w), lambda i: (i, 0))

    def body(dpre_ref, zd_ref, pa_ref, pb_ref, pc_ref, wbr_ref, wo_ref, dzd_ref, dpa_ref, dpb_ref, dpc_ref, doa_ref, dob_ref, doc_ref):
        dmerged = _dot(dpre_ref[...].astype(BF16), wo_ref[...], NT)
        branches = ((pa_ref, dpa_ref, doa_ref), (pb_ref, dpb_ref, dob_ref), (pc_ref, dpc_ref, doc_ref))
        for b, (p_ref, dp_ref, do_ref) in enumerate(branches):
            gate = jax.nn.sigmoid(zd_ref[:, b * D_MODEL:(b + 1) * D_MODEL])
            dzd_ref[:, b * D_MODEL:(b + 1) * D_MODEL] = (dmerged * p_ref[...] * gate * (1.0 - gate)).astype(BF16)
            dp = (dmerged * gate).astype(BF16)
            dp_ref[...] = dp
            do_ref[...] = _dot(dp, wbr_ref[b], NT).astype(do_ref.dtype)

    act = jax.ShapeDtypeStruct((s, D_MODEL), F32)
    actb = jax.ShapeDtypeStruct((s, D_MODEL), BF16)
    return _fused_call(
        body, name=name, grid=(s // t,),
        out_shape=(jax.ShapeDtypeStruct((s, W_D), BF16), actb, actb, actb, act, actb, actb),
        in_specs=[row(D_MODEL), row(W_D), row(D_MODEL), row(D_MODEL), row(D_MODEL),
                  _resident((3, D_MODEL, D_MODEL)), _resident((D_MODEL, D_MODEL))],
        out_specs=(row(W_D),) + (row(D_MODEL),) * 6, scratch_shapes=[],
        operands=[dpre1, zd, pa, pb, pc, wbr, wo], exchanges=exchanges)


def _mlp_loss(xhat1, rstd1, target, ln1_g, ln1_b, ln2_g, ln2_b, wu, wd, *, name):
    s = xhat1.shape[0]
    t = min(256, s)
    npan = wu.shape[0]
    row = lambda w: pl.BlockSpec((t, w), lambda i: (i, 0))
    vec = _resident((1, D_MODEL))

    def body(xhat_ref, rstd_ref, tgt_ref, g1_ref, b1_ref, g2_ref, b2_ref, wu_ref, wd_ref,
             dpre1_ref, dpre2_ref, h1_ref, a_ref, du_ref, stats_ref):
        @pl.when(pl.program_id(0) == 0)
        def _():
            stats_ref[...] = jnp.zeros_like(stats_ref)

        xhat1_v = xhat_ref[...]
        h1 = xhat1_v * g1_ref[...] + b1_ref[...]
        h1_b = h1.astype(BF16)
        h1_ref[...] = h1_b
        us = []
        ff = jnp.zeros((t, D_MODEL), F32)
        for j in range(npan):
            u = _dot(h1_b, wu_ref[j], NN)
            us.append(u)
            r = jnp.maximum(u, 0.0)
            a_b = (r * r).astype(BF16)
            a_ref[:, j * D_MODEL:(j + 1) * D_MODEL] = a_b
            ff = ff + _dot(a_b, wd_ref[j], NN)
        xhat2, rstd2 = _normalize(ALPHA * h1 + ff)
        err = xhat2 * g2_ref[...] + b2_ref[...] - tgt_ref[...]
        stats_ref[4:5, :] += jnp.sum(err * err, axis=0, keepdims=True)
        dpre2, dg2, db2 = _layer_norm_bwd(err * (1.0 / D_MODEL), xhat2, rstd2, g2_ref[...])
        stats_ref[0:1, :] += dg2
        stats_ref[1:2, :] += db2
        dpre2_b = dpre2.astype(BF16)
        dpre2_ref[...] = dpre2_b
        dh1 = ALPHA * dpre2
        for j in range(npan):
            du_b = (_dot(dpre2_b, wd_ref[j], NT) * (2.0 * jnp.maximum(us[j], 0.0))).astype(BF16)
            du_ref[:, j * D_MODEL:(j + 1) * D_MODEL] = du_b
            dh1 = dh1 + _dot(du_b, wu_ref[j], NT)
        dpre1, dg1, db1 = _layer_norm_bwd(dh1, xhat1_v, rstd_ref[...], g1_ref[...])
        stats_ref[2:3, :] += dg1
        stats_ref[3:4, :] += db1
        dpre1_ref[...] = dpre1

    actb = jax.ShapeDtypeStruct((s, D_MODEL), BF16)
    wide = jax.ShapeDtypeStruct((s, D_FF), BF16)
    return pl.pallas_call(
        body, name=name, grid=(s // t,),
        out_shape=(jax.ShapeDtypeStruct((s, D_MODEL), F32), actb, actb, wide, wide, jax.ShapeDtypeStruct((8, D_MODEL), F32)),
        in_specs=[row(D_MODEL), row(1), row(D_MODEL), vec, vec, vec, vec,
                  _resident((npan, D_MODEL, D_MODEL)), _resident((npan, D_MODEL, D_MODEL))],
        out_specs=(row(D_MODEL), row(D_MODEL), row(D_MODEL), row(D_FF), row(D_FF), pl.BlockSpec((8, D_MODEL), lambda i: (0, 0))),
        compiler_params=_params(("arbitrary",)),
    )(xhat1, rstd1, target, ln1_g, ln1_b, ln2_g, ln2_b, wu, wd)


BRANCH_WEIGHTS = ("w_branch_hg", "w_branch_swa", "w_branch_mem")


def _local_step(x, mem, target, wi_parts, wmkv, late, lb_logits, gain, sinks, rel_bias, ln1_g, ln1_b, ln2_g, ln2_b, *, distributed):
    s = x.shape[0]
    tm = min(1024, s)
    tk = min(2048, s)
    xb = x.astype(BF16)
    memb = mem.astype(BF16)
    wia, wib, wic, wid = wi_parts
    if distributed:
        cx, cy, cc = lax.axis_index("x"), lax.axis_index("y"), lax.axis_index("c")
        pos = jnp.stack([2 * cx + cy, cc]).astype(jnp.int32)
    gather = (lambda names: [_gather_exchange([late[k] for k in names])]) if distributed else (lambda names: [])
    to_sibling = (lambda grads: [_sibling_halves_exchange(grads)]) if distributed else (lambda grads: [])
    to_chips = (lambda sums: [_chip_partials_exchange([bf for bf, _ in sums])]) if distributed else (lambda sums: [])

    def chip_sums(names, grads, from_sibling):
        return [_add_sibling(g, o, pos, name="add_sibling_" + k) for k, g, o in zip(names, grads, from_sibling)]

    def shard_sums(names, sums, from_chips):
        return {k: _add_chips(mine, o, pos, name="add_chips_" + k) for k, (_, mine), o in zip(names, sums, from_chips)}

    za = _mm(xb, wia, mode="nt", tm=min(512, s), tn=W_A, tk=D_MODEL, name="proj_a")
    zb = _mm(xb, wib, mode="nt", tm=tm, tn=W_B, tk=D_MODEL, name="proj_b", out_dtype=BF16)
    zc = _mm(xb, wic, mode="nt", tm=tm, tn=W_C, tk=D_MODEL, name="proj_c", out_dtype=BF16)
    zd = _mm(xb, wid, mode="nt", tm=min(512, s), tn=W_D, tk=D_MODEL, name="proj_d")
    mkv = _mm(memb, wmkv, mode="nn", tm=MEM_LEN, tn=512, tk=D_MODEL, name="mem_kv", out_dtype=BF16, b_panels=True)
    onehot, maskrow = _bias_selector()
    bias_tables, sink_lanes = _swa_tables(_bias_table(rel_bias.T, onehot, maskrow, name="bias_table"), sinks)
    (o_a, o_raw, states), landed = _hgrn_fwd(za, lb_logits, gain, name="hgrn_fwd", exchanges=gather(("w_up", "w_down")))
    wu, wd = landed[0] if distributed else (late["wu"], late["wd"])
    o_b, landed = _swa_fwd(zb, bias_tables, sink_lanes, name="swa_fwd", exchanges=gather(BRANCH_WEIGHTS + ("w_out",)))
    if distributed:
        wbr = jnp.stack([wb.reshape(D_MODEL, D_MODEL) for wb in landed[0][:3]])
        wo = landed[0][3].reshape(D_MODEL, D_MODEL)
    else:
        wbr, wo = late["wbr"], late["wo"]
    o_c = _mem_fwd(zc, mkv, name="mem_fwd")
    xhat1, rstd1, merged, pa, pb, pc = _merge_fwd(o_a, o_b, o_c, zd, x, wbr, wo, name="merge_fwd")

    dpre1, dpre2, h1, act, du, ln_stats = _mlp_loss(xhat1, rstd1, target, ln1_g, ln1_b, ln2_g, ln2_b, wu, wd, name="mlp_loss")
    ffn = ("w_down", "w_up")
    g_ffn = [_mm(act, dpre2, mode="tn", tm=1024, tn=D_MODEL, tk=tk, name="grad_w_down").reshape(N_SHARDS, D_FF // N_SHARDS, D_MODEL),
             _mm(h1, du, mode="tn", tm=D_MODEL, tn=1024, tk=tk, name="grad_w_up", out_panels=True)]

    (dzd, dpa, dpb, dpc, do_a, do_b, do_c), landed = _merge_bwd(dpre1, zd, pa, pb, pc, wbr, wo, name="merge_bwd", exchanges=to_sibling(g_ffn))
    sums_ffn = chip_sums(ffn, g_ffn, landed[0]) if distributed else []
    merge = BRANCH_WEIGHTS + ("w_out",)
    g_merge = [_mm(o, dp, mode="tn", tm=D_MODEL, tn=D_MODEL, tk=tk, name="grad_" + k).reshape(N_SHARDS, D_MODEL // N_SHARDS, D_MODEL)
               for k, o, dp in zip(merge, (o_a, o_b, o_c, merged), (dpa, dpb, dpc, dpre1))]
    (dza, hg_stats), landed = _hgrn_bwd(za, o_raw, do_a, states, lb_logits, gain, name="hgrn_bwd",
                                        exchanges=to_chips(sums_ffn) + to_sibling(g_merge))
    halves = shard_sums(ffn, sums_ffn, landed[0]) if distributed else {}
    sums_merge = chip_sums(merge, g_merge, landed[1]) if distributed else []
    (dzb, dbias_t, dsinks), landed = _swa_bwd(zb, do_b, bias_tables, sink_lanes, name="swa_bwd", exchanges=to_chips(sums_merge))
    if distributed:
        halves.update(shard_sums(merge, sums_merge, landed[0]))
    dbias = dbias_t.reshape(2 * SWA_BLOCK, SWA_HEADS, SWA_BLOCK).transpose(1, 2, 0).reshape(SWA_HEADS, -1)
    d_rel_bias = _bias_grad(dbias, onehot, name="bias_grad").T
    dzc, dmkv = _mem_bwd(zc, do_c, mkv, name="mem_bwd")

    proj = ("w_in", "w_mem_kv")
    g_wi = [_mm(dz, xb, mode="tn", tm=dz.shape[1] if dz.shape[1] <= 1280 else 1024, tn=D_MODEL, tk=tk, name=nm)
            for dz, nm in ((dza, "grad_w_in_a"), (dzb, "grad_w_in_b"), (dzc, "grad_w_in_c"), (dzd, "grad_w_in_d"))]
    g_proj = [jnp.concatenate(g_wi, axis=0).reshape(N_SHARDS, IN_COLS // N_SHARDS, D_MODEL),
              _mm(memb, dmkv, mode="tn", tm=D_MODEL, tn=512, tk=MEM_LEN, name="grad_w_mem_kv", out_panels=True)]
    sums_proj = chip_sums(proj, g_proj, _run_exchanges(to_sibling(g_proj), name="reduce_sibling_proj")[0]) if distributed else []
    grad_x, landed = _dx_matmul([dza, dzb, dzc, dzd], [wia, wib, wic, wid], dpre1, tm=min(512, s), name="grad_x",
                                exchanges=to_chips(sums_proj))
    if distributed:
        halves.update(shard_sums(proj, sums_proj, landed[0]))
    else:
        halves = dict(zip(ffn + merge + proj, g_ffn + g_merge + g_proj))
    small = dict(lb_logits=hg_stats[1:3], hg_norm_gain=hg_stats[0:1], swa_sinks=dsinks, rel_bias=d_rel_bias,
                 ln1_g=ln_stats[2:3], ln1_b=ln_stats[3:4], ln2_g=ln_stats[0:1], ln2_b=ln_stats[1:2], sq_err=ln_stats[4:5])
    return grad_x, halves, small


def _mesh_position():
    x, y, c = lax.axis_index("x"), lax.axis_index("y"), lax.axis_index("c")
    chips = [(1 - x, y), (x, 1 - y), (1 - x, 1 - y)]
    return x, y, c, chips


class _Exchange(NamedTuple):
    operands: list
    out_shapes: list
    n_sems: int
    start: Callable
    finish: Callable


def _gather_exchange(shards):
    n = len(shards)
    per = 7

    def plan(ins, outs, send_sems, recv_sems):
        x, y, c, chips = _mesh_position()
        me = 2 * x + y
        sibling = (x, y, 1 - c)

        def half(a, slot, hc):
            rh = shards[a].shape[0] // 2
            return outs[a].at[slot, pl.ds(hc * rh, rh), :]

        def copy(a, k, src, dst, to):
            return pltpu.make_async_remote_copy(src_ref=src, dst_ref=dst, send_sem=send_sems.at[a * per + k], recv_sem=recv_sems.at[a * per + k],
                                                device_id=to, device_id_type=MESH)

        own = [copy(a, 6, ins[a], outs[a].at[me], sibling) for a in range(n)]
        to_chips = [copy(a, k, ins[a].at[pl.ds(c * (shards[a].shape[0] // 2), shards[a].shape[0] // 2), :], half(a, me, c), (cx, cy, c))
                    for k, (cx, cy) in enumerate(chips) for a in range(n)]
        arrived = [copy(a, k, half(a, 2 * cx + cy, c), half(a, 2 * cx + cy, c), (cx, cy, c)) for k, (cx, cy) in enumerate(chips) for a in range(n)]
        passed_on = [copy(a, 3 + k, half(a, 2 * cx + cy, c), half(a, 2 * cx + cy, c), sibling) for k, (cx, cy) in enumerate(chips) for a in range(n)]
        from_sibling = [copy(a, 3 + k, half(a, 2 * cx + cy, 1 - c), half(a, 2 * cx + cy, 1 - c), sibling)
                        for k, (cx, cy) in enumerate(chips) for a in range(n)]
        own_arrived = [copy(a, 6, outs[a].at[me], outs[a].at[me], sibling) for a in range(n)]
        return own, to_chips, arrived, passed_on, from_sibling, own_arrived

    def start(*refs):
        own, to_chips, _, _, _, _ = plan(*refs)
        for cp in own + to_chips:
            cp.start()

    def finish(*refs):
        own, to_chips, arrived, passed_on, from_sibling, own_arrived = plan(*refs)
        for landed, onward in zip(arrived, passed_on):
            landed.wait_recv()
            onward.start()
        for cp in from_sibling + own_arrived:
            cp.wait_recv()
        for cp in own + to_chips + passed_on:
            cp.wait_send()

    return _Exchange(list(shards), [jax.ShapeDtypeStruct((N_SHARDS,) + w.shape, w.dtype) for w in shards], per * n, start, finish)


def _sibling_halves_exchange(grads):
    n = len(grads)

    def plan(ins, outs, send_sems, recv_sems):
        x, y, c, _ = _mesh_position()
        return [pltpu.make_async_remote_copy(src_ref=ins[a].at[:, pl.ds((1 - c) * (grads[a].shape[1] // 2), grads[a].shape[1] // 2), :],
                                             dst_ref=outs[a], send_sem=send_sems.at[a], recv_sem=recv_sems.at[a],
                                             device_id=(x, y, 1 - c), device_id_type=MESH) for a in range(n)]

    def start(*refs):
        for cp in plan(*refs):
            cp.start()

    def finish(*refs):
        for cp in plan(*refs):
            cp.wait()

    return _Exchange(list(grads), [jax.ShapeDtypeStruct((g.shape[0], g.shape[1] // 2, g.shape[2]), g.dtype) for g in grads], n, start, finish)


def _chip_partials_exchange(sums):
    n = len(sums)

    def plan(ins, outs, send_sems, recv_sems):
        _, _, c, chips = _mesh_position()
        return [pltpu.make_async_remote_copy(src_ref=ins[a].at[2 * cx + cy], dst_ref=outs[a].at[k], send_sem=send_sems.at[a * 3 + k],
                                             recv_sem=recv_sems.at[a * 3 + k], device_id=(cx, cy, c), device_id_type=MESH)
                for k, (cx, cy) in enumerate(chips) for a in range(n)]

    def start(*refs):
        for cp in plan(*refs):
            cp.start()

    def finish(*refs):
        for cp in plan(*refs):
            cp.wait()

    return _Exchange(list(sums), [jax.ShapeDtypeStruct((3,) + g.shape[1:], g.dtype) for g in sums], 3 * n, start, finish)


def _fused_call(body, *, name, grid, in_specs, out_specs, out_shape, scratch_shapes, operands, exchanges=()):
    single = not isinstance(out_shape, (tuple, list))
    out_specs = [out_specs] if single else list(out_specs)
    out_shape = [out_shape] if single else list(out_shape)
    n_in, n_out, n_scr = len(in_specs), len(out_specs), len(scratch_shapes)
    x_in = [len(e.operands) for e in exchanges]
    x_out = [len(e.out_shapes) for e in exchanges]

    def wrapped(*refs):
        refs = list(refs)
        ins = refs[:n_in]
        pos = n_in
        ex_ins = []
        for k in x_in:
            ex_ins.append(refs[pos:pos + k])
            pos += k
        outs = refs[pos:pos + n_out]
        pos += n_out
        ex_outs = []
        for k in x_out:
            ex_outs.append(refs[pos:pos + k])
            pos += k
        scratch = refs[pos:pos + n_scr]
        sems = refs[pos + n_scr:]
        first, last = None, None
        for axis, size in enumerate(grid):
            at_start, at_end = pl.program_id(axis) == 0, pl.program_id(axis) == size - 1
            first = at_start if first is None else first & at_start
            last = at_end if last is None else last & at_end

        @pl.when(first)
        def _():
            for i, e in enumerate(exchanges):
                e.start(ex_ins[i], ex_outs[i], sems[2 * i], sems[2 * i + 1])

        body(*ins, *outs, *scratch)

        @pl.when(last)
        def _():
            for i, e in enumerate(exchanges):
                e.finish(ex_ins[i], ex_outs[i], sems[2 * i], sems[2 * i + 1])

    n_x_in, n_x_out = sum(x_in), sum(x_out)
    results = pl.pallas_call(
        wrapped if exchanges else body, name=name, grid=grid,
        in_specs=list(in_specs) + [HBM] * n_x_in,
        out_specs=out_specs + [HBM] * n_x_out,
        out_shape=out_shape + [s for e in exchanges for s in e.out_shapes],
        scratch_shapes=list(scratch_shapes) + [pltpu.SemaphoreType.DMA((e.n_sems,)) for e in exchanges for _ in range(2)],
        compiler_params=_params(("arbitrary",) * len(grid)),
    )(*operands, *[a for e in exchanges for a in e.operands])
    own = results[0] if single else tuple(results[:n_out])
    landed, pos = [], n_out
    for k in x_out:
        landed.append(list(results[pos:pos + k]))
        pos += k
    return own, landed


def _run_exchanges(exchanges, *, name):
    def body(*refs):
        n_in = sum(len(e.operands) for e in exchanges)
        n_out = sum(len(e.out_shapes) for e in exchanges)
        ins, outs, sems = refs[:n_in], refs[n_in:n_in + n_out], refs[n_in + n_out:]
        spans, i, o = [], 0, 0
        for e in exchanges:
            spans.append((ins[i:i + len(e.operands)], outs[o:o + len(e.out_shapes)]))
            i, o = i + len(e.operands), o + len(e.out_shapes)
        for k, e in enumerate(exchanges):
            e.start(*spans[k], sems[2 * k], sems[2 * k + 1])
        for k, e in enumerate(exchanges):
            e.finish(*spans[k], sems[2 * k], sems[2 * k + 1])

    operands = [a for e in exchanges for a in e.operands]
    shapes = [s for e in exchanges for s in e.out_shapes]
    results = pl.pallas_call(
        body, name=name, out_shape=shapes, in_specs=[HBM] * len(operands), out_specs=[HBM] * len(shapes),
        scratch_shapes=[pltpu.SemaphoreType.DMA((e.n_sems,)) for e in exchanges for _ in range(2)],
    )(*operands)
    landed, pos = [], 0
    for e in exchanges:
        landed.append(list(results[pos:pos + len(e.out_shapes)]))
        pos += len(e.out_shapes)
    return landed


ROW_TILE_MAX = 640
BF16_SUBLANES = 16


def _row_tile(rows):
    for tr in range(min(rows, ROW_TILE_MAX), 0, -1):
        if rows % tr == 0 and tr % BF16_SUBLANES == 0:
            return tr
    raise ValueError(rows)


def _add_sibling(grad, other, pos, *, name):
    p, r, cols = grad.shape
    rh = r // 2
    tr = _row_tile(rh)
    nb = rh // tr

    def body(pos_ref, g_ref, o_ref, sb_ref, mine_ref):
        total = g_ref[...] + o_ref[...]
        sb_ref[...] = total.astype(BF16)

        @pl.when(pl.program_id(1) == pos_ref[0])
        def _():
            mine_ref[...] = total

    return pl.pallas_call(
        body, name=name, out_shape=(jax.ShapeDtypeStruct((p, rh, cols), BF16), jax.ShapeDtypeStruct((rh, cols), F32)),
        grid_spec=pltpu.PrefetchScalarGridSpec(
            num_scalar_prefetch=1, grid=(nb, p),
            in_specs=[pl.BlockSpec((None, tr, cols), lambda i, j, pos_ref: (j, pos_ref[1] * nb + i, 0)),
                      pl.BlockSpec((None, tr, cols), lambda i, j, pos_ref: (j, i, 0))],
            out_specs=(pl.BlockSpec((None, tr, cols), lambda i, j, pos_ref: (j, i, 0)),
                       pl.BlockSpec((tr, cols), lambda i, j, pos_ref: (i, 0)))),
        compiler_params=_params(("parallel", "arbitrary")),
    )(pos, grad, other)


def _add_chips(mine, others, pos, *, name):
    rh, cols = mine.shape
    tr = _row_tile(rh)
    nb = rh // tr

    def body(pos_ref, s_ref, o_ref, r_ref):
        r_ref[...] = ((s_ref[...] + o_ref[0].astype(F32)) + o_ref[1].astype(F32)) + o_ref[2].astype(F32)

    return pl.pallas_call(
        body, name=name, out_shape=jax.ShapeDtypeStruct((2 * rh, cols), F32),
        grid_spec=pltpu.PrefetchScalarGridSpec(
            num_scalar_prefetch=1, grid=(nb,),
            in_specs=[pl.BlockSpec((tr, cols), lambda i, pos_ref: (i, 0)),
                      pl.BlockSpec((3, tr, cols), lambda i, pos_ref: (0, i, 0))],
            out_specs=pl.BlockSpec((tr, cols), lambda i, pos_ref: (pos_ref[1] * nb + i, 0))),
        compiler_params=_params(("parallel",)),
    )(pos, mine, others)


def _join_halves(bufs, *, name):
    n = len(bufs)

    def body(*refs):
        ins, outs = refs[:n], refs[n:2 * n]
        send_sems, recv_sems = refs[2 * n:]
        x, y, c, _ = _mesh_position()

        def copy(a, hc):
            rh = bufs[a].shape[0] // 2
            rows = pl.ds(hc * rh, rh)
            return pltpu.make_async_remote_copy(src_ref=ins[a].at[rows, :], dst_ref=outs[a].at[rows, :], send_sem=send_sems.at[a],
                                                recv_sem=recv_sems.at[a], device_id=(x, y, 1 - c), device_id_type=MESH)

        for a in range(n):
            copy(a, c).start()
        for a in range(n):
            copy(a, c).wait_send()
            copy(a, 1 - c).wait_recv()

    return pl.pallas_call(
        body, name=name, out_shape=[jax.ShapeDtypeStruct(b.shape, b.dtype) for b in bufs],
        in_specs=[HBM] * n, out_specs=[HBM] * n, input_output_aliases={a: a for a in range(n)},
        scratch_shapes=[pltpu.SemaphoreType.DMA((n,)), pltpu.SemaphoreType.DMA((n,))],
    )(*bufs)


def _all_reduce_small(packed, *, name):
    rows, cols = packed.shape

    def body(in_ref, out_ref, gathered, send_sems, recv_sems):
        x, y, c, _ = _mesh_position()
        me = 4 * x + 2 * y + c
        gathered[me] = in_ref[...]
        copies = []
        for d in range(1, 8):
            dx, dy, dc = (d >> 2) & 1, (d >> 1) & 1, d & 1
            peer = (x ^ dx, y ^ dy, c ^ dc)
            cp = pltpu.make_async_remote_copy(src_ref=in_ref, dst_ref=gathered.at[me], send_sem=send_sems.at[d - 1], recv_sem=recv_sems.at[d - 1],
                                              device_id=peer, device_id_type=MESH)
            cp.start()
            copies.append(cp)
        for cp in copies:
            cp.wait()
        total = gathered[0]
        for j in range(1, 8):
            total = total + gathered[j]
        out_ref[...] = total

    vm = pl.BlockSpec(memory_space=pltpu.VMEM)
    return pl.pallas_call(
        body, name=name, out_shape=jax.ShapeDtypeStruct((rows, cols), F32), in_specs=[vm], out_specs=vm,
        scratch_shapes=[pltpu.VMEM((8, rows, cols), F32), pltpu.SemaphoreType.DMA((7,)), pltpu.SemaphoreType.DMA((7,))],
    )(packed)


def _adamw_math(w, g, m, v):
    m = ADAM_B1 * m + (1.0 - ADAM_B1) * g
    v = ADAM_B2 * v + (1.0 - ADAM_B2) * (g * g)
    m_hat = m / (1.0 - ADAM_B1 ** ADAM_STEP)
    v_hat = v / (1.0 - ADAM_B2 ** ADAM_STEP)
    delta = -ADAM_LR * (m_hat / (jnp.sqrt(v_hat) + ADAM_EPS) + ADAM_WD * w)
    return delta, m, v


def _adamw(w, g, m, v, *, name):
    _, rows, cols = w.shape
    tr = _row_tile(rows)
    blk = pl.BlockSpec((None, tr, cols), lambda i: (0, i, 0))
    flat = pl.BlockSpec((tr, cols), lambda i: (i, 0))

    def body(w_ref, g_ref, m_ref, v_ref, go_ref, d_ref, nm_ref, nv_ref):
        g_v = g_ref[...]
        go_ref[...] = g_v
        d_ref[...], nm_ref[...], nv_ref[...] = _adamw_math(w_ref[...], g_v, m_ref[...], v_ref[...])

    shape = jax.ShapeDtypeStruct((1, rows, cols), F32)
    return pl.pallas_call(body, name=name, grid=(rows // tr,), out_shape=(shape,) * 4, in_specs=[blk, flat, blk, blk], out_specs=(blk,) * 4,
                          compiler_params=_params(("parallel",)))(w, g, m, v)


def _adamw_small(w, g, m, v, *, name):
    def body(w_ref, g_ref, m_ref, v_ref, d_ref, nm_ref, nv_ref, loss_ref):
        d_ref[...], nm_ref[...], nv_ref[...] = _adamw_math(w_ref[...], g_ref[...], m_ref[...], v_ref[...])
        loss_ref[...] = (0.5 / D_MODEL) * jnp.sum(g_ref[8:9, :], axis=1, keepdims=True)

    shape = jax.ShapeDtypeStruct(w.shape, F32)
    return pl.pallas_call(body, name=name, out_shape=(shape, shape, shape, jax.ShapeDtypeStruct((1, 1), F32)),
                          compiler_params=_params())(w, g, m, v)


SMALL_ROWS = 16


def _pack_small(lb_logits, gain, sinks, rel_bias, ln1_g, ln1_b, ln2_g, ln2_b, extra=None):
    misc = jnp.concatenate([sinks.reshape(1, -1), rel_bias.reshape(1, -1)], axis=1)
    misc = jnp.pad(misc, ((0, 0), (0, D_MODEL - misc.shape[1])))
    rows = [lb_logits, gain, ln1_g, ln1_b, ln2_g, ln2_b, misc, extra if extra is not None else jnp.zeros((1, D_MODEL), F32)]
    used = sum(r.shape[0] for r in rows)
    return jnp.concatenate(rows + [jnp.zeros((SMALL_ROWS - used, D_MODEL), F32)], axis=0)


def _unpack_small(p):
    return dict(lb_logits=p[0:2], hg_norm_gain=p[2:3], ln1_g=p[3:4], ln1_b=p[4:5], ln2_g=p[5:6], ln2_b=p[6:7],
                swa_sinks=p[7:8, 0:SWA_HEADS], rel_bias=p[7:8, SWA_HEADS:SWA_HEADS + NUM_BUCKETS * SWA_HEADS].reshape(NUM_BUCKETS, SWA_HEADS))


WEIGHTS = ["w_in", "lb_logits", "hg_norm_gain", "swa_sinks", "rel_bias", "w_mem_kv", "w_branch_hg", "w_branch_swa", "w_branch_mem",
           "w_out", "ln1_g", "ln1_b", "w_up", "w_down", "ln2_g", "ln2_b"]
BIG = ["w_in", "w_mem_kv", "w_branch_hg", "w_branch_swa", "w_branch_mem", "w_out", "w_up", "w_down"]
SMALL = ["lb_logits", "hg_norm_gain", "swa_sinks", "rel_bias", "ln1_g", "ln1_b", "ln2_g", "ln2_b"]


def kernel(x, mem, w_in, lb_logits, hg_norm_gain, swa_sinks, rel_bias, w_mem_kv, w_branch_hg, w_branch_swa, w_branch_mem, w_out, ln1_g, ln1_b, w_up, w_down, ln2_g, ln2_b, loss_target, m_w_in, m_lb_logits, m_hg_norm_gain, m_swa_sinks, m_rel_bias, m_w_mem_kv, m_w_branch_hg, m_w_branch_swa, m_w_branch_mem, m_w_out, m_ln1_g, m_ln1_b, m_w_up, m_w_down, m_ln2_g, m_ln2_b, v_w_in, v_lb_logits, v_hg_norm_gain, v_swa_sinks, v_rel_bias, v_w_mem_kv, v_w_branch_hg, v_w_branch_swa, v_w_branch_mem, v_w_out, v_ln1_g, v_ln1_b, v_w_up, v_w_down, v_ln2_g, v_ln2_b):
    w = dict(w_in=w_in, lb_logits=lb_logits, hg_norm_gain=hg_norm_gain, swa_sinks=swa_sinks, rel_bias=rel_bias, w_mem_kv=w_mem_kv,
             w_branch_hg=w_branch_hg, w_branch_swa=w_branch_swa, w_branch_mem=w_branch_mem, w_out=w_out, ln1_g=ln1_g, ln1_b=ln1_b,
             w_up=w_up, w_down=w_down, ln2_g=ln2_g, ln2_b=ln2_b)
    m = dict(w_in=m_w_in, lb_logits=m_lb_logits, hg_norm_gain=m_hg_norm_gain, swa_sinks=m_swa_sinks, rel_bias=m_rel_bias, w_mem_kv=m_w_mem_kv,
             w_branch_hg=m_w_branch_hg, w_branch_swa=m_w_branch_swa, w_branch_mem=m_w_branch_mem, w_out=m_w_out, ln1_g=m_ln1_g, ln1_b=m_ln1_b,
             w_up=m_w_up, w_down=m_w_down, ln2_g=m_ln2_g, ln2_b=m_ln2_b)
    v = dict(w_in=v_w_in, lb_logits=v_lb_logits, hg_norm_gain=v_hg_norm_gain, swa_sinks=v_swa_sinks, rel_bias=v_rel_bias, w_mem_kv=v_w_mem_kv,
             w_branch_hg=v_w_branch_hg, w_branch_swa=v_w_branch_swa, w_branch_mem=v_w_branch_mem, w_out=v_w_out, ln1_g=v_ln1_g, ln1_b=v_ln1_b,
             w_up=v_w_up, w_down=v_w_down, ln2_g=v_ln2_g, ln2_b=v_ln2_b)
    shapes = {k: w[k].shape for k in WEIGHTS}
    for d in (w, m, v):
        d["w_in"] = d["w_in"].reshape(D_MODEL, IN_COLS // N_SHARDS).T[None]
    shards = {k: w[k].reshape(w[k].shape[-2], w[k].shape[-1]).astype(BF16) for k in BIG}
    wi4, wmkv = _run_exchanges([_gather_exchange([shards["w_in"], shards["w_mem_kv"]])], name="gather_weights")[0]
    wi_t = wi4.reshape(IN_COLS, D_MODEL)
    wi_parts = (wi_t[0:W_A], wi_t[W_A:W_A + W_B], wi_t[W_A + W_B:W_A + W_B + W_C], wi_t[W_A + W_B + W_C:])

    grad_x, halves, small = _local_step(
        x.reshape(x.shape[-2], D_MODEL), mem.reshape(MEM_LEN, D_MODEL), loss_target.reshape(loss_target.shape[-2], D_MODEL),
        wi_parts, wmkv, shards, lb_logits, hg_norm_gain, swa_sinks, rel_bias, ln1_g, ln1_b, ln2_g, ln2_b, distributed=True)

    reduced = dict(zip(BIG, _join_halves([halves[k] for k in BIG], name="join_halves")))

    packed_g = _pack_small(small["lb_logits"], small["hg_norm_gain"], small["swa_sinks"], small["rel_bias"], small["ln1_g"], small["ln1_b"],
                           small["ln2_g"], small["ln2_b"], extra=small["sq_err"])
    packed_g = _all_reduce_small(packed_g, name="reduce_small")

    grad_out, delta_out, m_out, v_out = {}, {}, {}, {}
    for k in BIG:
        grad_out[k], delta_out[k], m_out[k], v_out[k] = _adamw(w[k], reduced[k], m[k], v[k], name="adamw_" + k)
    pack = lambda d: _pack_small(d["lb_logits"], d["hg_norm_gain"], d["swa_sinks"], d["rel_bias"], d["ln1_g"], d["ln1_b"], d["ln2_g"], d["ln2_b"])
    d_s, m_s, v_s, loss = _adamw_small(pack(w), packed_g, pack(m), pack(v), name="adamw_small")
    for out, p in ((grad_out, packed_g), (delta_out, d_s), (m_out, m_s), (v_out, v_s)):
        out["w_in"] = out["w_in"][0].T
        out.update(_unpack_small(p))

    result = [loss.reshape(()), grad_x.reshape(x.shape)]
    for out in (grad_out, delta_out, m_out, v_out):
        result += [out[k].reshape(shapes[k]) for k in WEIGHTS]
    return tuple(result)
```

```python
import math
from typing import Callable, NamedTuple

import jax
import jax.numpy as jnp
from jax import lax
from jax.experimental import pallas as pl
from jax.experimental.pallas import tpu as pltpu

F32 = jnp.float32
BF16 = jnp.bfloat16
HIGHEST = lax.Precision.HIGHEST
MESH = pl.DeviceIdType.MESH

D_MODEL = 1024
MEM_LEN = 256
HG_HEADS = 8
HG_DK = 128
HG_CHUNK = 64
SWA_HEADS = 16
SWA_KV_HEADS = 2
SWA_GROUP = 8
SWA_HEAD_DIM = 64
SWA_BLOCK = 128
SWA_WINDOW = 128
MEM_HEADS = 4
MEM_HEAD_DIM = 256
NUM_BUCKETS = 32
MAX_DISTANCE = 128
D_FF = 4096
LN_EPS = 1e-5
RMS_EPS = 1e-6
ALPHA = 2.0 ** 0.25
W_A, W_B, W_C, W_D = 4096, 1280, 1024, 3072
IN_COLS = W_A + W_B + W_C + W_D
N_SHARDS = 4
ADAM_LR = 0.001
ADAM_B1 = 0.9
ADAM_B2 = 0.999
ADAM_EPS = 1e-08
ADAM_WD = 0.01
ADAM_STEP = 10
MASK_VALUE = -1e30
VMEM_LIMIT = 56 * 1024 * 1024

NN = ((1,), (0,))
NT = ((1,), (1,))
TN = ((0,), (0,))
HBM = pl.BlockSpec(memory_space=pltpu.HBM)


def _dot(a, b, dims=NN, precision=None):
    return lax.dot_general(a, b, (dims, ((), ())), precision=precision, preferred_element_type=F32)


def _params(sem=None):
    return pltpu.CompilerParams(dimension_semantics=sem, vmem_limit_bytes=VMEM_LIMIT)


def _resident(shape):
    zeros = (0,) * len(shape)
    return pl.BlockSpec(shape, lambda *_: zeros, pipeline_mode=pl.Buffered(1))


def _mm(a, b, *, mode, tm, tn, tk, name, out_dtype=F32, b_panels=False, out_panels=False, rows_of=None, row_offset=0, into=None):
    if mode == "tn":
        kdim, m = a.shape
    else:
        m, kdim = a.shape
    if b_panels:
        n = b.shape[0] * b.shape[2]
        assert b.shape[2] == tn and mode == "nn"
    elif mode == "nt":
        n = b.shape[0]
    else:
        n = b.shape[1]
    assert m % tm == 0 and n % tn == 0 and kdim % tk == 0, (name, m, n, kdim)
    nk = kdim // tk
    dims = {"nn": NN, "nt": NT, "tn": TN}[mode]
    a_spec = pl.BlockSpec((tk, tm), lambda i, j, k: (k, i)) if mode == "tn" else pl.BlockSpec((tm, tk), lambda i, j, k: (i, k))
    if b_panels:
        b_spec = pl.BlockSpec((None, tk, tn), lambda i, j, k: (j, k, 0))
    elif mode == "nt":
        b_spec = pl.BlockSpec((tn, tk), lambda i, j, k: (j, k))
    else:
        b_spec = pl.BlockSpec((tk, tn), lambda i, j, k: (k, j))
    in_specs = [a_spec, b_spec]
    operands = [a, b]
    aliases = {}
    if out_panels:
        out_shape = jax.ShapeDtypeStruct((n // tn, m, tn), out_dtype)
        o_spec = pl.BlockSpec((None, tm, tn), lambda i, j, k: (j, i, 0))
    elif rows_of is not None:
        out_shape = jax.ShapeDtypeStruct((rows_of, n), out_dtype)
        assert row_offset % BF16_SUBLANES == 0 and tm % BF16_SUBLANES == 0 and tn % 128 == 0
        o_spec = pl.BlockSpec((pl.Element(tm), pl.Element(tn)),
                              lambda i, j, k: (pl.multiple_of(row_offset + i * tm, BF16_SUBLANES), pl.multiple_of(j * tn, 128)))
        if into is not None:
            in_specs.append(pl.BlockSpec(memory_space=pl.ANY))
            operands.append(into)
            aliases = {2: 0}
    else:
        out_shape = jax.ShapeDtypeStruct((m, n), out_dtype)
        o_spec = pl.BlockSpec((tm, tn), lambda i, j, k: (i, j))
    n_in = len(operands)

    def body(*refs):
        a_ref, b_ref, o_ref = refs[0], refs[1], refs[n_in]
        part = _dot(a_ref[...].astype(BF16), b_ref[...].astype(BF16), dims)

        def finish(acc):
            o_ref[...] = acc.astype(out_dtype)

        if nk == 1:
            finish(part)
        else:
            acc_ref = refs[-1]
            k = pl.program_id(2)

            @pl.when(k == 0)
            def _():
                acc_ref[...] = part

            @pl.when(k > 0)
            def _():
                acc_ref[...] += part

            @pl.when(k == nk - 1)
            def _():
                finish(acc_ref[...])

    return pl.pallas_call(
        body, name=name, out_shape=out_shape, grid=(m // tm, n // tn, nk), in_specs=in_specs, out_specs=o_spec,
        scratch_shapes=[pltpu.VMEM((tm, tn), F32)] if nk > 1 else [], input_output_aliases=aliases,
        compiler_params=_params(("parallel", "parallel", "arbitrary")),
    )(*operands)


def _dx_matmul(dzs, wis, resid, *, tm, name, exchanges=()):
    s = resid.shape[0]
    npieces = len(dzs)
    in_specs = [pl.BlockSpec((tm, dz.shape[1]), lambda i: (i, 0)) for dz in dzs]
    in_specs += [_resident(w.shape) for w in wis]
    in_specs += [pl.BlockSpec((tm, D_MODEL), lambda i: (i, 0))]

    def body(*refs):
        dz_refs, w_refs = refs[:npieces], refs[npieces:2 * npieces]
        r_ref, o_ref = refs[2 * npieces], refs[2 * npieces + 1]
        total = ALPHA * r_ref[...]
        for p in range(npieces):
            total = total + _dot(dz_refs[p][...], w_refs[p][...], NN)
        o_ref[...] = total

    return _fused_call(
        body, name=name, out_shape=jax.ShapeDtypeStruct((s, D_MODEL), F32), grid=(s // tm,), in_specs=in_specs,
        out_specs=pl.BlockSpec((tm, D_MODEL), lambda i: (i, 0)), scratch_shapes=[],
        operands=[*dzs, *wis, resid], exchanges=exchanges)


def _lower_bound(lbl_ref):
    l0, l1 = lbl_ref[0:1, :], lbl_ref[1:2, :]
    mx = jnp.maximum(l0, l1)
    e0, e1 = jnp.exp(l0 - mx), jnp.exp(l1 - mx)
    return e0 / (e0 + e1)


HEAD_COLS = [slice(h * HG_DK, (h + 1) * HG_DK) for h in range(HG_HEADS)]


def _head_mean(x):
    return jnp.concatenate([jnp.broadcast_to(jnp.mean(x[:, c], axis=-1, keepdims=True), (x.shape[0], HG_DK)) for c in HEAD_COLS], axis=1)


def _chunk_forward(q, fl, v, lb, tril_f):
    sg = jax.nn.sigmoid(fl)
    f = lb + (1.0 - lb) * sg
    k = 1.0 - f
    b = _dot(tril_f, jnp.log(f), NN, HIGHEST)
    b_last = b[HG_CHUNK - 1:HG_CHUNK, :]
    eb, enb, eo = jnp.exp(b), jnp.exp(-b), jnp.exp(b_last - b)
    return sg, f, k, b_last, eb, enb, eo, q * eb, k * enb, k * eo


def _hgrn_fwd(za, lb_logits, gain, *, name, exchanges=()):
    s = za.shape[0]
    t = min(256, s)
    ncs = t // HG_CHUNK

    def body(z_ref, lbl_ref, gain_ref, oa_ref, oraw_ref, st_ref, state):
        @pl.when(pl.program_id(0) == 0)
        def _():
            state[...] = jnp.zeros_like(state)

        lb_all = _lower_bound(lbl_ref)
        row = lax.broadcasted_iota(jnp.int32, (HG_CHUNK, HG_CHUNK), 0)
        col = lax.broadcasted_iota(jnp.int32, (HG_CHUNK, HG_CHUNK), 1)
        tril = row >= col
        tril_f = tril.astype(F32)
        gain_all = gain_ref[...]

        def chunk(i, carry):
            r = pl.ds(pl.multiple_of(i * HG_CHUNK, HG_CHUNK), HG_CHUNK)
            q, fl, v, hg = (z_ref[r, j * D_MODEL:(j + 1) * D_MODEL] for j in range(4))
            _, _, _, b_last, _, _, _, q_in, k_in, k_out = _chunk_forward(q, fl, v, lb_all, tril_f)
            q_in_b, k_in_b, k_out_b, vb = (u.astype(BF16) for u in (q_in, k_in, k_out, v))
            decay = jnp.exp(b_last)
            sts = [state[h] for h in range(HG_HEADS)]
            attn = [_dot(q_in_b[:, c], k_in_b[:, c], NT) for c in HEAD_COLS]
            inter = [_dot(q_in_b[:, c], sts[h].astype(BF16), NT) for h, c in enumerate(HEAD_COLS)]
            upd = [_dot(vb[:, c], k_out_b[:, c], TN) for c in HEAD_COLS]
            attn = [jnp.where(tril, a, 0.0).astype(BF16) for a in attn]
            outs = [_dot(attn[h], vb[:, c], NN) + inter[h] for h, c in enumerate(HEAD_COLS)]
            for h, c in enumerate(HEAD_COLS):
                st_ref[h, i] = sts[h]
                state[h] = sts[h] * decay[:, c] + upd[h]
            o = jnp.concatenate(outs, axis=1)
            oraw_ref[r, :] = o
            n = o * lax.rsqrt(_head_mean(o * o) + RMS_EPS)
            oa_ref[r, :] = (n * gain_all * (hg * jax.nn.sigmoid(hg))).astype(BF16)
            return carry

        lax.fori_loop(0, ncs, chunk, 0, unroll=True)

    return _fused_call(
        body, name=name, grid=(s // t,),
        out_shape=(jax.ShapeDtypeStruct((s, D_MODEL), BF16), jax.ShapeDtypeStruct((s, D_MODEL), F32),
                   jax.ShapeDtypeStruct((HG_HEADS, s // HG_CHUNK, HG_DK, HG_DK), F32)),
        in_specs=[pl.BlockSpec((t, W_A), lambda i: (i, 0)), _resident((2, D_MODEL)), _resident((1, D_MODEL))],
        out_specs=(pl.BlockSpec((t, D_MODEL), lambda i: (i, 0)), pl.BlockSpec((t, D_MODEL), lambda i: (i, 0)),
                   pl.BlockSpec((HG_HEADS, ncs, HG_DK, HG_DK), lambda i: (0, i, 0, 0))),
        scratch_shapes=[pltpu.VMEM((HG_HEADS, HG_DK, HG_DK), F32)],
        operands=[za, lb_logits, gain], exchanges=exchanges)


def _hgrn_bwd(za, oraw, do_a, states, lb_logits, gain, *, name, exchanges=()):
    s = za.shape[0]
    t = min(256, s)
    ncs = t // HG_CHUNK
    nt = s // t

    def body(z_ref, oraw_ref, do_ref, st_ref, lbl_ref, gain_ref, dz_ref, stats_ref, dstate):
        step = pl.program_id(0)

        @pl.when(step == 0)
        def _():
            dstate[...] = jnp.zeros_like(dstate)
            stats_ref[...] = jnp.zeros_like(stats_ref)

        lb_all = _lower_bound(lbl_ref)
        row = lax.broadcasted_iota(jnp.int32, (HG_CHUNK, HG_CHUNK), 0)
        col = lax.broadcasted_iota(jnp.int32, (HG_CHUNK, HG_CHUNK), 1)
        tril = row >= col
        tril_f = tril.astype(F32)
        triu_f = (row <= col).astype(F32)
        gain_all = gain_ref[...]

        def chunk(ii, carry):
            i = ncs - 1 - ii
            r = pl.ds(pl.multiple_of(i * HG_CHUNK, HG_CHUNK), HG_CHUNK)
            q, fl, v, hg = (z_ref[r, j * D_MODEL:(j + 1) * D_MODEL] for j in range(4))
            o = oraw_ref[r, :]
            doa = do_ref[r, :]
            rms = lax.rsqrt(_head_mean(o * o) + RMS_EPS)
            n = o * rms
            sgg = jax.nn.sigmoid(hg)
            silu = hg * sgg
            dhg = doa * n * gain_all * (sgg * (1.0 + hg * (1.0 - sgg)))
            dgain = jnp.sum(doa * n * silu, axis=0, keepdims=True)
            dn = doa * gain_all * silu
            do = rms * (dn - n * _head_mean(dn * n))
            sg, f, k, b_last, eb, enb, eo, q_in, k_in, k_out = _chunk_forward(q, fl, v, lb_all, tril_f)
            q_in_b, k_in_b, k_out_b, vb, dob = (u.astype(BF16) for u in (q_in, k_in, k_out, v, do))
            decay = jnp.exp(b_last)
            sts = [st_ref[h, i] for h in range(HG_HEADS)]
            dsts = [dstate[h] for h in range(HG_HEADS)]
            dsts_b = [d.astype(BF16) for d in dsts]
            heads = list(enumerate(HEAD_COLS))
            attn = [_dot(q_in_b[:, c], k_in_b[:, c], NT) for h, c in heads]
            dattn = [_dot(dob[:, c], vb[:, c], NT) for h, c in heads]
            dq_st = [_dot(dob[:, c], sts[h].astype(BF16), NN) for h, c in heads]
            dk_out = [_dot(vb[:, c], dsts_b[h], NN) for h, c in heads]
            dv_st = [_dot(k_out_b[:, c], dsts_b[h], NT) for h, c in heads]
            dst_o = [_dot(dob[:, c], q_in_b[:, c], TN) for h, c in heads]
            attn = [jnp.where(tril, a, 0.0).astype(BF16) for a in attn]
            dattn = [jnp.where(tril, a, 0.0).astype(BF16) for a in dattn]
            dq_in = jnp.concatenate([_dot(dattn[h], k_in_b[:, c], NN) + dq_st[h] for h, c in heads], axis=1)
            dk_in = jnp.concatenate([_dot(dattn[h], q_in_b[:, c], TN) for h, c in heads], axis=1)
            dv = jnp.concatenate([_dot(attn[h], dob[:, c], TN) + dv_st[h] for h, c in heads], axis=1)
            dk_out = jnp.concatenate(dk_out, axis=1)
            dst_st = jnp.concatenate([jnp.sum(dsts[h] * sts[h], axis=0, keepdims=True) for h in range(HG_HEADS)], axis=1)
            for h, c in heads:
                dstate[h] = dsts[h] * decay[:, c] + dst_o[h]
            db_last = decay * dst_st + jnp.sum(dk_out * k_out, axis=0, keepdims=True)
            db = dq_in * q_in - dk_in * k_in - dk_out * k_out
            dg = _dot(triu_f, db, NN, HIGHEST) + db_last
            dk = dk_in * enb + dk_out * eo
            df = dg / f - dk
            stats_ref[0:1, :] += dgain
            stats_ref[1:2, :] += jnp.sum(df * (1.0 - sg), axis=0, keepdims=True)
            dz_ref[r, 0:1024] = (dq_in * eb).astype(BF16)
            dz_ref[r, 1024:2048] = (df * (1.0 - lb_all) * sg * (1.0 - sg)).astype(BF16)
            dz_ref[r, 2048:3072] = dv.astype(BF16)
            dz_ref[r, 3072:4096] = dhg.astype(BF16)
            return carry

        lax.fori_loop(0, ncs, chunk, 0, unroll=True)

        @pl.when(step == nt - 1)
        def _():
            dl0 = stats_ref[1:2, :] * lb_all * (1.0 - lb_all)
            stats_ref[1:2, :] = dl0
            stats_ref[2:3, :] = -dl0

    rev = lambda i: (nt - 1 - i, 0)
    return _fused_call(
        body, name=name, grid=(nt,),
        out_shape=(jax.ShapeDtypeStruct((s, W_A), BF16), jax.ShapeDtypeStruct((8, D_MODEL), F32)),
        in_specs=[pl.BlockSpec((t, W_A), rev), pl.BlockSpec((t, D_MODEL), rev), pl.BlockSpec((t, D_MODEL), rev),
                  pl.BlockSpec((HG_HEADS, ncs, HG_DK, HG_DK), lambda i: (0, nt - 1 - i, 0, 0)),
                  _resident((2, D_MODEL)), _resident((1, D_MODEL))],
        out_specs=(pl.BlockSpec((t, W_A), rev), pl.BlockSpec((8, D_MODEL), lambda i: (0, 0))),
        scratch_shapes=[pltpu.VMEM((HG_HEADS, HG_DK, HG_DK), F32)],
        operands=[za, oraw, do_a, states, lb_logits, gain], exchanges=exchanges)


def _t5_bucket(n):
    max_exact = NUM_BUCKETS // 2
    nf = jnp.maximum(n, 1).astype(F32)
    large = max_exact + (jnp.log(nf / max_exact) / math.log(MAX_DISTANCE / max_exact) * (NUM_BUCKETS - max_exact)).astype(jnp.int32)
    large = jnp.minimum(large, NUM_BUCKETS - 1)
    return jnp.where(n < max_exact, n, large)


def _bias_selector():
    qi = jnp.arange(SWA_BLOCK)[:, None] + SWA_BLOCK
    kj = jnp.arange(2 * SWA_BLOCK)[None, :]
    dist = qi - kj
    band = ((dist >= 0) & (dist < SWA_WINDOW)).reshape(1, -1)
    bucket = _t5_bucket(jnp.clip(dist, 0, SWA_WINDOW - 1)).reshape(1, -1)
    onehot = ((bucket == jnp.arange(NUM_BUCKETS)[:, None]) & band).astype(F32)
    return onehot, jnp.where(band, 0.0, MASK_VALUE).astype(F32)


def _bias_table(rel_bias_t, onehot, maskrow, *, name):
    def body(rb_ref, oh_ref, mask_ref, o_ref):
        o_ref[...] = _dot(rb_ref[...], oh_ref[...], NN, HIGHEST) + mask_ref[...]

    return pl.pallas_call(body, name=name, out_shape=jax.ShapeDtypeStruct((SWA_HEADS, onehot.shape[1]), F32),
                          compiler_params=_params())(rel_bias_t, onehot, maskrow)


def _bias_grad(dbias2d, onehot, *, name):
    def body(db_ref, oh_ref, o_ref):
        o_ref[...] = _dot(db_ref[...], oh_ref[...], NT, HIGHEST)

    return pl.pallas_call(body, name=name, out_shape=jax.ShapeDtypeStruct((SWA_HEADS, NUM_BUCKETS), F32),
                          compiler_params=_params())(dbias2d, onehot)


GROUP_LANES = SWA_GROUP * SWA_BLOCK


def _swa_operands(zq_ref, kv_cur_ref, kv_prev_ref):
    q = (zq_ref[:, 0:1024] * (SWA_HEAD_DIM ** -0.5)).astype(BF16)
    kv_c = kv_cur_ref[...].astype(BF16)
    kv_p = kv_prev_ref[...].astype(BF16)
    kks = [jnp.concatenate([kv_p[:, g * 64:(g + 1) * 64], kv_c[:, g * 64:(g + 1) * 64]], axis=0) for g in range(SWA_KV_HEADS)]
    vvs = [jnp.concatenate([kv_p[:, 128 + g * 64:128 + (g + 1) * 64], kv_c[:, 128 + g * 64:128 + (g + 1) * 64]], axis=0)
           for g in range(SWA_KV_HEADS)]
    return q, kks, vvs


def _stack_heads(x, g):
    return jnp.concatenate([x[:, h * SWA_HEAD_DIM:(h + 1) * SWA_HEAD_DIM] for h in range(g * SWA_GROUP, (g + 1) * SWA_GROUP)], axis=0)


def _heads_to_lanes(xt):
    pairs = []
    for j in range(0, SWA_GROUP, 2):
        two = jnp.concatenate([xt[:, j * SWA_BLOCK:(j + 1) * SWA_BLOCK], xt[:, (j + 1) * SWA_BLOCK:(j + 2) * SWA_BLOCK]], axis=0)
        pairs.append(two.T)
    return jnp.concatenate(pairs, axis=1)


def _swa_softmax(score_t, bias_ref, sink_ref, g):
    lanes = slice(g * GROUP_LANES, (g + 1) * GROUP_LANES)
    sc = score_t + bias_ref[:, lanes]
    sink = sink_ref[:, lanes]
    m = jnp.maximum(jnp.max(sc, axis=0, keepdims=True), sink)
    e = jnp.exp(sc - m)
    e_sink = jnp.exp(sink - m)
    return e, 1.0 / (jnp.sum(e, axis=0, keepdims=True) + e_sink), e_sink


def _swa_tables(bias2d, sinks):
    bias_t = bias2d.reshape(SWA_HEADS, SWA_BLOCK, 2 * SWA_BLOCK).transpose(2, 0, 1).reshape(2 * SWA_BLOCK, SWA_HEADS * SWA_BLOCK)
    first = jnp.where(jnp.arange(2 * SWA_BLOCK)[:, None] < SWA_BLOCK, MASK_VALUE, bias_t)
    return jnp.stack([first, bias_t]), jnp.repeat(sinks, SWA_BLOCK, axis=1)


def _swa_fwd(zb, bias_tables, sink_lanes, *, name, exchanges=()):
    s = zb.shape[0]
    nb = s // SWA_BLOCK

    def body(zq_ref, kvc_ref, kvp_ref, bias_ref, sink_ref, o_ref):
        q, kks, vvs = _swa_operands(zq_ref, kvc_ref, kvp_ref)
        groups = range(SWA_KV_HEADS)
        scores = [_dot(kks[g], _stack_heads(q, g), NT) for g in groups]
        probs = []
        for g in groups:
            e, inv, _ = _swa_softmax(scores[g], bias_ref, sink_ref, g)
            probs.append((e * inv).astype(BF16))
        outs = [_dot(vvs[g], probs[g], TN) for g in groups]
        o_ref[...] = jnp.concatenate([_heads_to_lanes(outs[g]) for g in groups], axis=1).astype(BF16)

    return _fused_call(
        body, name=name, grid=(nb,), out_shape=jax.ShapeDtypeStruct((s, D_MODEL), BF16),
        in_specs=[pl.BlockSpec((SWA_BLOCK, W_B), lambda n: (n, 0)),
                  pl.BlockSpec((SWA_BLOCK, 256), lambda n: (n, 4)),
                  pl.BlockSpec((SWA_BLOCK, 256), lambda n: (jnp.maximum(n - 1, 0), 4)),
                  pl.BlockSpec((None, 2 * SWA_BLOCK, SWA_HEADS * SWA_BLOCK), lambda n: (jnp.minimum(n, 1), 0, 0)),
                  _resident((1, SWA_HEADS * SWA_BLOCK))],
        out_specs=pl.BlockSpec((SWA_BLOCK, D_MODEL), lambda n: (n, 0)), scratch_shapes=[],
        operands=[zb, zb, zb, bias_tables, sink_lanes], exchanges=exchanges)


def _swa_bwd(zb, do_b, bias_tables, sink_lanes, *, name, exchanges=()):
    s = zb.shape[0]
    nb = s // SWA_BLOCK
    scale = SWA_HEAD_DIM ** -0.5

    def body(zq_ref, kvc_ref, kvp_ref, do_ref, bias_ref, sink_ref, dz_ref, dbias_ref, dsink_ref, carry, dsink_acc):
        step = pl.program_id(0)

        @pl.when(step == 0)
        def _():
            carry[...] = jnp.zeros_like(carry)
            dsink_acc[...] = jnp.zeros_like(dsink_acc)
            dbias_ref[...] = jnp.zeros_like(dbias_ref)

        q, kks, vvs = _swa_operands(zq_ref, kvc_ref, kvp_ref)
        groups = range(SWA_KV_HEADS)
        do = do_ref[...].astype(BF16)
        q_rows = [_stack_heads(q, g) for g in groups]
        do_rows = [_stack_heads(do, g) for g in groups]
        scores = [_dot(kks[g], q_rows[g], NT) for g in groups]
        dps = [_dot(vvs[g], do_rows[g], NT) for g in groups]
        ps, dss = [], []
        for g in groups:
            lanes = slice(g * GROUP_LANES, (g + 1) * GROUP_LANES)
            e, inv, e_sink = _swa_softmax(scores[g], bias_ref, sink_ref, g)
            p = e * inv
            delta = jnp.sum(p * dps[g], axis=0, keepdims=True)
            ds = p * (dps[g] - delta)
            dbias_ref[:, lanes] += ds
            dsink_acc[:, lanes] -= e_sink * inv * delta
            ps.append(p.astype(BF16))
            dss.append(ds.astype(BF16))
        dqs = [_dot(kks[g], dss[g], TN) * scale for g in groups]
        dkks = [_dot(dss[g], q_rows[g], NN) for g in groups]
        dvvs = [_dot(ps[g], do_rows[g], NN) for g in groups]
        dkv = jnp.concatenate(dkks + dvvs, axis=1)
        dz_ref[:, 0:1024] = jnp.concatenate([_heads_to_lanes(dqs[g]) for g in groups], axis=1).astype(BF16)
        dz_ref[:, 1024:1280] = (dkv[SWA_BLOCK:, :] + carry[...]).astype(BF16)
        carry[...] = dkv[:SWA_BLOCK, :]

        @pl.when(step == nb - 1)
        def _():
            acc = dsink_acc[...]
            dsink_ref[...] = jnp.concatenate([jnp.sum(acc[:, h * SWA_BLOCK:(h + 1) * SWA_BLOCK], axis=1, keepdims=True)
                                              for h in range(SWA_HEADS)], axis=1)

    rev = lambda i: (nb - 1 - i, 0)
    table_shape = (2 * SWA_BLOCK, SWA_HEADS * SWA_BLOCK)
    return _fused_call(
        body, name=name, grid=(nb,),
        out_shape=(jax.ShapeDtypeStruct((s, W_B), BF16), jax.ShapeDtypeStruct(table_shape, F32), jax.ShapeDtypeStruct((1, SWA_HEADS), F32)),
        in_specs=[pl.BlockSpec((SWA_BLOCK, W_B), rev),
                  pl.BlockSpec((SWA_BLOCK, 256), lambda i: (nb - 1 - i, 4)),
                  pl.BlockSpec((SWA_BLOCK, 256), lambda i: (jnp.maximum(nb - 2 - i, 0), 4)),
                  pl.BlockSpec((SWA_BLOCK, D_MODEL), rev),
                  pl.BlockSpec((None,) + table_shape, lambda i: (jnp.minimum(nb - 1 - i, 1), 0, 0)),
                  _resident((1, SWA_HEADS * SWA_BLOCK))],
        out_specs=(pl.BlockSpec((SWA_BLOCK, W_B), rev), pl.BlockSpec(table_shape, lambda i: (0, 0)),
                   pl.BlockSpec((1, SWA_HEADS), lambda i: (0, 0))),
        scratch_shapes=[pltpu.VMEM((SWA_BLOCK, 256), F32), pltpu.VMEM((1, SWA_HEADS * SWA_BLOCK), F32)],
        operands=[zb, zb, zb, do_b, bias_tables, sink_lanes], exchanges=exchanges)


def _mem_probs(zc_ref, mkv_ref, h):
    cols = slice(h * MEM_HEAD_DIM, (h + 1) * MEM_HEAD_DIM)
    qh = (zc_ref[:, cols] * (MEM_HEAD_DIM ** -0.5)).astype(BF16)
    sc = _dot(qh, mkv_ref[:, cols], NT)
    e = jnp.exp(sc - jnp.max(sc, axis=-1, keepdims=True))
    return qh, e / jnp.sum(e, axis=-1, keepdims=True)


def _mem_fwd(zc, mkv, *, name):
    s = zc.shape[0]
    t = min(512, s)

    def body(zc_ref, mkv_ref, o_ref):
        for h in range(MEM_HEADS):
            _, p = _mem_probs(zc_ref, mkv_ref, h)
            vh = mkv_ref[:, D_MODEL + h * MEM_HEAD_DIM:D_MODEL + (h + 1) * MEM_HEAD_DIM]
            o_ref[:, h * MEM_HEAD_DIM:(h + 1) * MEM_HEAD_DIM] = _dot(p.astype(BF16), vh, NN).astype(BF16)

    return pl.pallas_call(
        body, name=name, grid=(s // t,), out_shape=jax.ShapeDtypeStruct((s, D_MODEL), BF16),
        in_specs=[pl.BlockSpec((t, D_MODEL), lambda i: (i, 0)), _resident((MEM_LEN, 2 * D_MODEL))],
        out_specs=pl.BlockSpec((t, D_MODEL), lambda i: (i, 0)), compiler_params=_params(("parallel",)),
    )(zc, mkv)


def _mem_bwd(zc, do_c, mkv, *, name):
    s = zc.shape[0]
    t = min(512, s)

    def body(zc_ref, do_ref, mkv_ref, dz_ref, dmkv_ref):
        @pl.when(pl.program_id(0) == 0)
        def _():
            dmkv_ref[...] = jnp.zeros_like(dmkv_ref)

        for h in range(MEM_HEADS):
            cols = slice(h * MEM_HEAD_DIM, (h + 1) * MEM_HEAD_DIM)
            vcols = slice(D_MODEL + h * MEM_HEAD_DIM, D_MODEL + (h + 1) * MEM_HEAD_DIM)
            qh, p = _mem_probs(zc_ref, mkv_ref, h)
            doh = do_ref[:, cols].astype(BF16)
            dp = _dot(doh, mkv_ref[:, vcols], NT)
            ds = (p * (dp - jnp.sum(p * dp, axis=-1, keepdims=True))).astype(BF16)
            dz_ref[:, cols] = (_dot(ds, mkv_ref[:, cols], NN) * (MEM_HEAD_DIM ** -0.5)).astype(BF16)
            dmkv_ref[:, cols] += _dot(ds, qh, TN)
            dmkv_ref[:, vcols] += _dot(p.astype(BF16), doh, TN)

    return pl.pallas_call(
        body, name=name, grid=(s // t,),
        out_shape=(jax.ShapeDtypeStruct((s, D_MODEL), BF16), jax.ShapeDtypeStruct((MEM_LEN, 2 * D_MODEL), F32)),
        in_specs=[pl.BlockSpec((t, D_MODEL), lambda i: (i, 0)), pl.BlockSpec((t, D_MODEL), lambda i: (i, 0)),
                  _resident((MEM_LEN, 2 * D_MODEL))],
        out_specs=(pl.BlockSpec((t, D_MODEL), lambda i: (i, 0)), pl.BlockSpec((MEM_LEN, 2 * D_MODEL), lambda i: (0, 0))),
        compiler_params=_params(("arbitrary",)),
    )(zc, do_c, mkv)


def _normalize(pre):
    mu = jnp.mean(pre, axis=-1, keepdims=True)
    xc = pre - mu
    rstd = lax.rsqrt(jnp.mean(xc * xc, axis=-1, keepdims=True) + LN_EPS)
    return xc * rstd, rstd


def _layer_norm_bwd(dh, xhat, rstd, g):
    dxh = dh * g
    dpre = rstd * (dxh - jnp.mean(dxh, axis=-1, keepdims=True) - xhat * jnp.mean(dxh * xhat, axis=-1, keepdims=True))
    return dpre, jnp.sum(dh * xhat, axis=0, keepdims=True), jnp.sum(dh, axis=0, keepdims=True)


def _merge_fwd(o_a, o_b, o_c, zd, x, wbr, wo, *, name):
    s = x.shape[0]
    t = min(256, s)
    row = lambda w: pl.BlockSpec((t, w), lambda i: (i, 0))

    def body(oa_ref, ob_ref, oc_ref, zd_ref, x_ref, wbr_ref, wo_ref, xhat_ref, rstd_ref, merged_ref, pa_ref, pb_ref, pc_ref):
        merged = jnp.zeros((t, D_MODEL), F32)
        for b, (o_ref, p_ref) in enumerate(((oa_ref, pa_ref), (ob_ref, pb_ref), (oc_ref, pc_ref))):
            p = _dot(o_ref[...], wbr_ref[b], NN)
            p_ref[...] = p.astype(BF16)
            merged = merged + jax.nn.sigmoid(zd_ref[:, b * D_MODEL:(b + 1) * D_MODEL]) * p
        merged_b = merged.astype(BF16)
        merged_ref[...] = merged_b
        xhat, rstd = _normalize(ALPHA * x_ref[...] + _dot(merged_b, wo_ref[...], NN))
        xhat_ref[...] = xhat
        rstd_ref[...] = rstd

    act = jax.ShapeDtypeStruct((s, D_MODEL), F32)
    return pl.pallas_call(
        body, name=name, grid=(s // t,),
        out_shape=(act, jax.ShapeDtypeStruct((s, 1), F32)) + (jax.ShapeDtypeStruct((s, D_MODEL), BF16),) * 4,
        in_specs=[row(D_MODEL), row(D_MODEL), row(D_MODEL), row(W_D), row(D_MODEL),
                  _resident((3, D_MODEL, D_MODEL)), _resident((D_MODEL, D_MODEL))],
        out_specs=(row(D_MODEL), row(1), row(D_MODEL), row(D_MODEL), row(D_MODEL), row(D_MODEL)),
        compiler_params=_params(("parallel",)),
    )(o_a, o_b, o_c, zd, x, wbr, wo)


def _merge_bwd(dpre1, zd, pa, pb, pc, wbr, wo, *, name, exchanges=()):
    s = dpre1.shape[0]
    t = min(256, s)
    row = lambda w: pl.BlockSpec((t, w), lambda i: (i, 0))

    def body(dpre_ref, zd_ref, pa_ref, pb_ref, pc_ref, wbr_ref, wo_ref, dzd_ref, dpa_ref, dpb_ref, dpc_ref, doa_ref, dob_ref, doc_ref):
        dmerged = _dot(dpre_ref[...].astype(BF16), wo_ref[...], NT)
        branches = ((pa_ref, dpa_ref, doa_ref), (pb_ref, dpb_ref, dob_ref), (pc_ref, dpc_ref, doc_ref))
        for b, (p_ref, dp_ref, do_ref) in enumerate(branches):
            gate = jax.nn.sigmoid(zd_ref[:, b * D_MODEL:(b + 1) * D_MODEL])
            dzd_ref[:, b * D_MODEL:(b + 1) * D_MODEL] = (dmerged * p_ref[...] * gate * (1.0 - gate)).astype(BF16)
            dp = (dmerged * gate).astype(BF16)
            dp_ref[...] = dp
            do_ref[...] = _dot(dp, wbr_ref[b], NT).astype(do_ref.dtype)

    act = jax.ShapeDtypeStruct((s, D_MODEL), F32)
    actb = jax.ShapeDtypeStruct((s, D_MODEL), BF16)
    return _fused_call(
        body, name=name, grid=(s // t,),
        out_shape=(jax.ShapeDtypeStruct((s, W_D), BF16), actb, actb, actb, act, actb, actb),
        in_specs=[row(D_MODEL), row(W_D), row(D_MODEL), row(D_MODEL), row(D_MODEL),
                  _resident((3, D_MODEL, D_MODEL)), _resident((D_MODEL, D_MODEL))],
        out_specs=(row(W_D),) + (row(D_MODEL),) * 6, scratch_shapes=[],
        operands=[dpre1, zd, pa, pb, pc, wbr, wo], exchanges=exchanges)


def _mlp_loss(xhat1, rstd1, target, ln1_g, ln1_b, ln2_g, ln2_b, wu, wd, *, name):
    s = xhat1.shape[0]
    t = min(256, s)
    npan = wu.shape[0]
    row = lambda w: pl.BlockSpec((t, w), lambda i: (i, 0))
    vec = _resident((1, D_MODEL))

    def body(xhat_ref, rstd_ref, tgt_ref, g1_ref, b1_ref, g2_ref, b2_ref, wu_ref, wd_ref,
             dpre1_ref, dpre2_ref, h1_ref, a_ref, du_ref, stats_ref):
        @pl.when(pl.program_id(0) == 0)
        def _():
            stats_ref[...] = jnp.zeros_like(stats_ref)

        xhat1_v = xhat_ref[...]
        h1 = xhat1_v * g1_ref[...] + b1_ref[...]
        h1_b = h1.astype(BF16)
        h1_ref[...] = h1_b
        us = []
        ff = jnp.zeros((t, D_MODEL), F32)
        for j in range(npan):
            u = _dot(h1_b, wu_ref[j], NN)
            us.append(u)
            r = jnp.maximum(u, 0.0)
            a_b = (r * r).astype(BF16)
            a_ref[:, j * D_MODEL:(j + 1) * D_MODEL] = a_b
            ff = ff + _dot(a_b, wd_ref[j], NN)
        xhat2, rstd2 = _normalize(ALPHA * h1 + ff)
        err = xhat2 * g2_ref[...] + b2_ref[...] - tgt_ref[...]
        stats_ref[4:5, :] += jnp.sum(err * err, axis=0, keepdims=True)
        dpre2, dg2, db2 = _layer_norm_bwd(err * (1.0 / D_MODEL), xhat2, rstd2, g2_ref[...])
        stats_ref[0:1, :] += dg2
        stats_ref[1:2, :] += db2
        dpre2_b = dpre2.astype(BF16)
        dpre2_ref[...] = dpre2_b
        dh1 = ALPHA * dpre2
        for j in range(npan):
            du_b = (_dot(dpre2_b, wd_ref[j], NT) * (2.0 * jnp.maximum(us[j], 0.0))).astype(BF16)
            du_ref[:, j * D_MODEL:(j + 1) * D_MODEL] = du_b
            dh1 = dh1 + _dot(du_b, wu_ref[j], NT)
        dpre1, dg1, db1 = _layer_norm_bwd(dh1, xhat1_v, rstd_ref[...], g1_ref[...])
        stats_ref[2:3, :] += dg1
        stats_ref[3:4, :] += db1
        dpre1_ref[...] = dpre1

    actb = jax.ShapeDtypeStruct((s, D_MODEL), BF16)
    wide = jax.ShapeDtypeStruct((s, D_FF), BF16)
    return pl.pallas_call(
        body, name=name, grid=(s // t,),
        out_shape=(jax.ShapeDtypeStruct((s, D_MODEL), F32), actb, actb, wide, wide, jax.ShapeDtypeStruct((8, D_MODEL), F32)),
        in_specs=[row(D_MODEL), row(1), row(D_MODEL), vec, vec, vec, vec,
                  _resident((npan, D_MODEL, D_MODEL)), _resident((npan, D_MODEL, D_MODEL))],
        out_specs=(row(D_MODEL), row(D_MODEL), row(D_MODEL), row(D_FF), row(D_FF), pl.BlockSpec((8, D_MODEL), lambda i: (0, 0))),
        compiler_params=_params(("arbitrary",)),
    )(xhat1, rstd1, target, ln1_g, ln1_b, ln2_g, ln2_b, wu, wd)


BRANCH_WEIGHTS = ("w_branch_hg", "w_branch_swa", "w_branch_mem")


def _local_step(x, mem, target, wi_parts, wmkv, late, lb_logits, gain, sinks, rel_bias, ln1_g, ln1_b, ln2_g, ln2_b, *, distributed):
    s = x.shape[0]
    tm = min(1024, s)
    tk = min(2048, s)
    xb = x.astype(BF16)
    memb = mem.astype(BF16)
    wia, wib, wic, wid = wi_parts
    if distributed:
        cx, cy, cc = lax.axis_index("x"), lax.axis_index("y"), lax.axis_index("c")
        pos = jnp.stack([2 * cx + cy, cc]).astype(jnp.int32)
    gather = (lambda names: [_gather_exchange([late[k] for k in names])]) if distributed else (lambda names: [])
    to_sibling = (lambda grads: [_sibling_halves_exchange(grads)]) if distributed else (lambda grads: [])
    to_chips = (lambda sums: [_chip_partials_exchange([bf for bf, _ in sums])]) if distributed else (lambda sums: [])

    def chip_sums(names, grads, from_sibling):
        return [_add_sibling(g, o, pos, name="add_sibling_" + k) for k, g, o in zip(names, grads, from_sibling)]

    def shard_sums(names, sums, from_chips):
        return {k: _add_chips(mine, o, pos, name="add_chips_" + k) for k, (_, mine), o in zip(names, sums, from_chips)}

    za = _mm(xb, wia, mode="nt", tm=min(512, s), tn=W_A, tk=D_MODEL, name="proj_a")
    zb = _mm(xb, wib, mode="nt", tm=tm, tn=W_B, tk=D_MODEL, name="proj_b", out_dtype=BF16)
    zc = _mm(xb, wic, mode="nt", tm=tm, tn=W_C, tk=D_MODEL, name="proj_c", out_dtype=BF16)
    zd = _mm(xb, wid, mode="nt", tm=min(512, s), tn=W_D, tk=D_MODEL, name="proj_d")
    mkv = _mm(memb, wmkv, mode="nn", tm=MEM_LEN, tn=512, tk=D_MODEL, name="mem_kv", out_dtype=BF16, b_panels=True)
    onehot, maskrow = _bias_selector()
    bias_tables, sink_lanes = _swa_tables(_bias_table(rel_bias.T, onehot, maskrow, name="bias_table"), sinks)
    (o_a, o_raw, states), landed = _hgrn_fwd(za, lb_logits, gain, name="hgrn_fwd", exchanges=gather(("w_up", "w_down")))
    wu, wd = landed[0] if distributed else (late["wu"], late["wd"])
    o_b, landed = _swa_fwd(zb, bias_tables, sink_lanes, name="swa_fwd", exchanges=gather(BRANCH_WEIGHTS + ("w_out",)))
    if distributed:
        wbr = jnp.stack([wb.reshape(D_MODEL, D_MODEL) for wb in landed[0][:3]])
        wo = landed[0][3].reshape(D_MODEL, D_MODEL)
    else:
        wbr, wo = late["wbr"], late["wo"]
    o_c = _mem_fwd(zc, mkv, name="mem_fwd")
    xhat1, rstd1, merged, pa, pb, pc = _merge_fwd(o_a, o_b, o_c, zd, x, wbr, wo, name="merge_fwd")

    dpre1, dpre2, h1, act, du, ln_stats = _mlp_loss(xhat1, rstd1, target, ln1_g, ln1_b, ln2_g, ln2_b, wu, wd, name="mlp_loss")
    ffn = ("w_down", "w_up")
    g_ffn = [_mm(act, dpre2, mode="tn", tm=1024, tn=D_MODEL, tk=tk, name="grad_w_down").reshape(N_SHARDS, D_FF // N_SHARDS, D_MODEL),
             _mm(h1, du, mode="tn", tm=D_MODEL, tn=1024, tk=tk, name="grad_w_up", out_panels=True)]

    (dzd, dpa, dpb, dpc, do_a, do_b, do_c), landed = _merge_bwd(dpre1, zd, pa, pb, pc, wbr, wo, name="merge_bwd", exchanges=to_sibling(g_ffn))
    sums_ffn = chip_sums(ffn, g_ffn, landed[0]) if distributed else []
    merge = BRANCH_WEIGHTS + ("w_out",)
    g_merge = [_mm(o, dp, mode="tn", tm=D_MODEL, tn=D_MODEL, tk=tk, name="grad_" + k).reshape(N_SHARDS, D_MODEL // N_SHARDS, D_MODEL)
               for k, o, dp in zip(merge, (o_a, o_b, o_c, merged), (dpa, dpb, dpc, dpre1))]
    (dza, hg_stats), landed = _hgrn_bwd(za, o_raw, do_a, states, lb_logits, gain, name="hgrn_bwd",
                                        exchanges=to_chips(sums_ffn) + to_sibling(g_merge))
    halves = shard_sums(ffn, sums_ffn, landed[0]) if distributed else {}
    sums_merge = chip_sums(merge, g_merge, landed[1]) if distributed else []
    (dzb, dbias_t, dsinks), landed = _swa_bwd(zb, do_b, bias_tables, sink_lanes, name="swa_bwd", exchanges=to_chips(sums_merge))
    if distributed:
        halves.update(shard_sums(merge, sums_merge, landed[0]))
    dbias = dbias_t.reshape(2 * SWA_BLOCK, SWA_HEADS, SWA_BLOCK).transpose(1, 2, 0).reshape(SWA_HEADS, -1)
    d_rel_bias = _bias_grad(dbias, onehot, name="bias_grad").T
    dzc, dmkv = _mem_bwd(zc, do_c, mkv, name="mem_bwd")

    proj = ("w_in", "w_mem_kv")
    g_wi, offset = None, 0
    for dz, nm in ((dza, "grad_w_in_a"), (dzb, "grad_w_in_b"), (dzc, "grad_w_in_c"), (dzd, "grad_w_in_d")):
        g_wi = _mm(dz, xb, mode="tn", tm=dz.shape[1] if dz.shape[1] <= 1280 else 1024, tn=D_MODEL, tk=tk, name=nm,
                   rows_of=IN_COLS, row_offset=offset, into=g_wi)
        offset += dz.shape[1]
    g_proj = [g_wi.reshape(N_SHARDS, IN_COLS // N_SHARDS, D_MODEL),
              _mm(memb, dmkv, mode="tn", tm=D_MODEL, tn=512, tk=MEM_LEN, name="grad_w_mem_kv", out_panels=True)]
    sums_proj = chip_sums(proj, g_proj, _run_exchanges(to_sibling(g_proj), name="reduce_sibling_proj")[0]) if distributed else []
    grad_x, landed = _dx_matmul([dza, dzb, dzc, dzd], [wia, wib, wic, wid], dpre1, tm=min(512, s), name="grad_x",
                                exchanges=to_chips(sums_proj))
    if distributed:
        halves.update(shard_sums(proj, sums_proj, landed[0]))
    else:
        halves = dict(zip(ffn + merge + proj, g_ffn + g_merge + g_proj))
    small = dict(lb_logits=hg_stats[1:3], hg_norm_gain=hg_stats[0:1], swa_sinks=dsinks, rel_bias=d_rel_bias,
                 ln1_g=ln_stats[2:3], ln1_b=ln_stats[3:4], ln2_g=ln_stats[0:1], ln2_b=ln_stats[1:2], sq_err=ln_stats[4:5])
    return grad_x, halves, small


def _mesh_position():
    x, y, c = lax.axis_index("x"), lax.axis_index("y"), lax.axis_index("c")
    chips = [(1 - x, y), (x, 1 - y), (1 - x, 1 - y)]
    return x, y, c, chips


class _Exchange(NamedTuple):
    operands: list
    out_shapes: list
    n_sems: int
    start: Callable
    finish: Callable


def _gather_exchange(shards):
    n = len(shards)
    per = 7

    def plan(ins, outs, send_sems, recv_sems):
        x, y, c, chips = _mesh_position()
        me = 2 * x + y
        sibling = (x, y, 1 - c)

        def half(a, slot, hc):
            rh = shards[a].shape[0] // 2
            return outs[a].at[slot, pl.ds(hc * rh, rh), :]

        def copy(a, k, src, dst, to):
            return pltpu.make_async_remote_copy(src_ref=src, dst_ref=dst, send_sem=send_sems.at[a * per + k], recv_sem=recv_sems.at[a * per + k],
                                                device_id=to, device_id_type=MESH)

        own = [copy(a, 6, ins[a], outs[a].at[me], sibling) for a in range(n)]
        to_chips = [copy(a, k, ins[a].at[pl.ds(c * (shards[a].shape[0] // 2), shards[a].shape[0] // 2), :], half(a, me, c), (cx, cy, c))
                    for k, (cx, cy) in enumerate(chips) for a in range(n)]
        arrived = [copy(a, k, half(a, 2 * cx + cy, c), half(a, 2 * cx + cy, c), (cx, cy, c)) for k, (cx, cy) in enumerate(chips) for a in range(n)]
        passed_on = [copy(a, 3 + k, half(a, 2 * cx + cy, c), half(a, 2 * cx + cy, c), sibling) for k, (cx, cy) in enumerate(chips) for a in range(n)]
        from_sibling = [copy(a, 3 + k, half(a, 2 * cx + cy, 1 - c), half(a, 2 * cx + cy, 1 - c), sibling)
                        for k, (cx, cy) in enumerate(chips) for a in range(n)]
        own_arrived = [copy(a, 6, outs[a].at[me], outs[a].at[me], sibling) for a in range(n)]
        return own, to_chips, arrived, passed_on, from_sibling, own_arrived

    def start(*refs):
        own, to_chips, _, _, _, _ = plan(*refs)
        for cp in own + to_chips:
            cp.start()

    def finish(*refs):
        own, to_chips, arrived, passed_on, from_sibling, own_arrived = plan(*refs)
        for landed, onward in zip(arrived, passed_on):
            landed.wait_recv()
            onward.start()
        for cp in from_sibling + own_arrived:
            cp.wait_recv()
        for cp in own + to_chips + passed_on:
            cp.wait_send()

    return _Exchange(list(shards), [jax.ShapeDtypeStruct((N_SHARDS,) + w.shape, w.dtype) for w in shards], per * n, start, finish)


def _sibling_halves_exchange(grads):
    n = len(grads)

    def plan(ins, outs, send_sems, recv_sems):
        x, y, c, _ = _mesh_position()
        return [pltpu.make_async_remote_copy(src_ref=ins[a].at[:, pl.ds((1 - c) * (grads[a].shape[1] // 2), grads[a].shape[1] // 2), :],
                                             dst_ref=outs[a], send_sem=send_sems.at[a], recv_sem=recv_sems.at[a],
                                             device_id=(x, y, 1 - c), device_id_type=MESH) for a in range(n)]

    def start(*refs):
        for cp in plan(*refs):
            cp.start()

    def finish(*refs):
        for cp in plan(*refs):
            cp.wait()

    return _Exchange(list(grads), [jax.ShapeDtypeStruct((g.shape[0], g.shape[1] // 2, g.shape[2]), g.dtype) for g in grads], n, start, finish)


def _chip_partials_exchange(sums):
    n = len(sums)

    def plan(ins, outs, send_sems, recv_sems):
        _, _, c, chips = _mesh_position()
        return [pltpu.make_async_remote_copy(src_ref=ins[a].at[2 * cx + cy], dst_ref=outs[a].at[k], send_sem=send_sems.at[a * 3 + k],
                                             recv_sem=recv_sems.at[a * 3 + k], device_id=(cx, cy, c), device_id_type=MESH)
                for k, (cx, cy) in enumerate(chips) for a in range(n)]

    def start(*refs):
        for cp in plan(*refs):
            cp.start()

    def finish(*refs):
        for cp in plan(*refs):
            cp.wait()

    return _Exchange(list(sums), [jax.ShapeDtypeStruct((3,) + g.shape[1:], g.dtype) for g in sums], 3 * n, start, finish)


def _fused_call(body, *, name, grid, in_specs, out_specs, out_shape, scratch_shapes, operands, exchanges=()):
    single = not isinstance(out_shape, (tuple, list))
    out_specs = [out_specs] if single else list(out_specs)
    out_shape = [out_shape] if single else list(out_shape)
    n_in, n_out, n_scr = len(in_specs), len(out_specs), len(scratch_shapes)
    x_in = [len(e.operands) for e in exchanges]
    x_out = [len(e.out_shapes) for e in exchanges]

    def wrapped(*refs):
        refs = list(refs)
        ins = refs[:n_in]
        pos = n_in
        ex_ins = []
        for k in x_in:
            ex_ins.append(refs[pos:pos + k])
            pos += k
        outs = refs[pos:pos + n_out]
        pos += n_out
        ex_outs = []
        for k in x_out:
            ex_outs.append(refs[pos:pos + k])
            pos += k
        scratch = refs[pos:pos + n_scr]
        sems = refs[pos + n_scr:]
        first, last = None, None
        for axis, size in enumerate(grid):
            at_start, at_end = pl.program_id(axis) == 0, pl.program_id(axis) == size - 1
            first = at_start if first is None else first & at_start
            last = at_end if last is None else last & at_end

        @pl.when(first)
        def _():
            for i, e in enumerate(exchanges):
                e.start(ex_ins[i], ex_outs[i], sems[2 * i], sems[2 * i + 1])

        body(*ins, *outs, *scratch)

        @pl.when(last)
        def _():
            for i, e in enumerate(exchanges):
                e.finish(ex_ins[i], ex_outs[i], sems[2 * i], sems[2 * i + 1])

    n_x_in, n_x_out = sum(x_in), sum(x_out)
    results = pl.pallas_call(
        wrapped if exchanges else body, name=name, grid=grid,
        in_specs=list(in_specs) + [HBM] * n_x_in,
        out_specs=out_specs + [HBM] * n_x_out,
        out_shape=out_shape + [s for e in exchanges for s in e.out_shapes],
        scratch_shapes=list(scratch_shapes) + [pltpu.SemaphoreType.DMA((e.n_sems,)) for e in exchanges for _ in range(2)],
        compiler_params=_params(("arbitrary",) * len(grid)),
    )(*operands, *[a for e in exchanges for a in e.operands])
    own = results[0] if single else tuple(results[:n_out])
    landed, pos = [], n_out
    for k in x_out:
        landed.append(list(results[pos:pos + k]))
        pos += k
    return own, landed


def _run_exchanges(exchanges, *, name):
    def body(*refs):
        n_in = sum(len(e.operands) for e in exchanges)
        n_out = sum(len(e.out_shapes) for e in exchanges)
        ins, outs, sems = refs[:n_in], refs[n_in:n_in + n_out], refs[n_in + n_out:]
        spans, i, o = [], 0, 0
        for e in exchanges:
            spans.append((ins[i:i + len(e.operands)], outs[o:o + len(e.out_shapes)]))
            i, o = i + len(e.operands), o + len(e.out_shapes)
        for k, e in enumerate(exchanges):
            e.start(*spans[k], sems[2 * k], sems[2 * k + 1])
        for k, e in enumerate(exchanges):
            e.finish(*spans[k], sems[2 * k], sems[2 * k + 1])

    operands = [a for e in exchanges for a in e.operands]
    shapes = [s for e in exchanges for s in e.out_shapes]
    results = pl.pallas_call(
        body, name=name, out_shape=shapes, in_specs=[HBM] * len(operands), out_specs=[HBM] * len(shapes),
        scratch_shapes=[pltpu.SemaphoreType.DMA((e.n_sems,)) for e in exchanges for _ in range(2)],
    )(*operands)
    landed, pos = [], 0
    for e in exchanges:
        landed.append(list(results[pos:pos + len(e.out_shapes)]))
        pos += len(e.out_shapes)
    return landed


ROW_TILE_MAX = 640
BF16_SUBLANES = 16


def _row_tile(rows):
    for tr in range(min(rows, ROW_TILE_MAX), 0, -1):
        if rows % tr == 0 and tr % BF16_SUBLANES == 0:
            return tr
    raise ValueError(rows)


def _add_sibling(grad, other, pos, *, name):
    p, r, cols = grad.shape
    rh = r // 2
    tr = _row_tile(rh)
    nb = rh // tr

    def body(pos_ref, g_ref, o_ref, sb_ref, mine_ref):
        total = g_ref[...] + o_ref[...]
        sb_ref[...] = total.astype(BF16)

        @pl.when(pl.program_id(1) == pos_ref[0])
        def _():
            mine_ref[...] = total

    return pl.pallas_call(
        body, name=name, out_shape=(jax.ShapeDtypeStruct((p, rh, cols), BF16), jax.ShapeDtypeStruct((rh, cols), F32)),
        grid_spec=pltpu.PrefetchScalarGridSpec(
            num_scalar_prefetch=1, grid=(nb, p),
            in_specs=[pl.BlockSpec((None, tr, cols), lambda i, j, pos_ref: (j, pos_ref[1] * nb + i, 0)),
                      pl.BlockSpec((None, tr, cols), lambda i, j, pos_ref: (j, i, 0))],
            out_specs=(pl.BlockSpec((None, tr, cols), lambda i, j, pos_ref: (j, i, 0)),
                       pl.BlockSpec((tr, cols), lambda i, j, pos_ref: (i, 0)))),
        compiler_params=_params(("parallel", "arbitrary")),
    )(pos, grad, other)


def _add_chips(mine, others, pos, *, name):
    rh, cols = mine.shape
    tr = _row_tile(rh)
    nb = rh // tr

    def body(pos_ref, s_ref, o_ref, r_ref):
        r_ref[...] = ((s_ref[...] + o_ref[0].astype(F32)) + o_ref[1].astype(F32)) + o_ref[2].astype(F32)

    return pl.pallas_call(
        body, name=name, out_shape=jax.ShapeDtypeStruct((2 * rh, cols), F32),
        grid_spec=pltpu.PrefetchScalarGridSpec(
            num_scalar_prefetch=1, grid=(nb,),
            in_specs=[pl.BlockSpec((tr, cols), lambda i, pos_ref: (i, 0)),
                      pl.BlockSpec((3, tr, cols), lambda i, pos_ref: (0, i, 0))],
            out_specs=pl.BlockSpec((tr, cols), lambda i, pos_ref: (pos_ref[1] * nb + i, 0))),
        compiler_params=_params(("parallel",)),
    )(pos, mine, others)


def _join_halves(bufs, *, name):
    n = len(bufs)

    def body(*refs):
        ins, outs = refs[:n], refs[n:2 * n]
        send_sems, recv_sems = refs[2 * n:]
        x, y, c, _ = _mesh_position()

        def copy(a, hc):
            rh = bufs[a].shape[0] // 2
            rows = pl.ds(hc * rh, rh)
            return pltpu.make_async_remote_copy(src_ref=ins[a].at[rows, :], dst_ref=outs[a].at[rows, :], send_sem=send_sems.at[a],
                                                recv_sem=recv_sems.at[a], device_id=(x, y, 1 - c), device_id_type=MESH)

        for a in range(n):
            copy(a, c).start()
        for a in range(n):
            copy(a, c).wait_send()
            copy(a, 1 - c).wait_recv()

    return pl.pallas_call(
        body, name=name, out_shape=[jax.ShapeDtypeStruct(b.shape, b.dtype) for b in bufs],
        in_specs=[HBM] * n, out_specs=[HBM] * n, input_output_aliases={a: a for a in range(n)},
        scratch_shapes=[pltpu.SemaphoreType.DMA((n,)), pltpu.SemaphoreType.DMA((n,))],
    )(*bufs)


SMALL = ["lb_logits", "hg_norm_gain", "swa_sinks", "rel_bias", "ln1_g", "ln1_b", "ln2_g", "ln2_b"]
PACK_ROWS = 48
PACK_AT = dict(lb_logits=(slice(0, 2), slice(0, D_MODEL)), hg_norm_gain=(slice(2, 3), slice(0, D_MODEL)), ln1_g=(slice(3, 4), slice(0, D_MODEL)),
               ln1_b=(slice(4, 5), slice(0, D_MODEL)), ln2_g=(slice(5, 6), slice(0, D_MODEL)), ln2_b=(slice(6, 7), slice(0, D_MODEL)),
               swa_sinks=(slice(7, 8), slice(0, SWA_HEADS)), sq_err=(slice(8, 9), slice(0, D_MODEL)),
               rel_bias=(slice(16, 16 + NUM_BUCKETS), slice(0, SWA_HEADS)))


def _small_step(grads, w, m, v, *, name):
    names = SMALL
    n = len(names)

    def body(*refs):
        g_refs = dict(zip(names + ["sq_err"], refs[:n + 1]))
        w_refs, m_refs, v_refs = (dict(zip(names, refs[n + 1 + i * n:n + 1 + (i + 1) * n])) for i in range(3))
        outs = refs[4 * n + 1:8 * n + 2]
        loss_ref = outs[0]
        go_refs, d_refs, nm_refs, nv_refs = (dict(zip(names, outs[1 + i * n:1 + (i + 1) * n])) for i in range(4))
        packed, gathered, send_sems, recv_sems = refs[8 * n + 2:]
        x, y, c, _ = _mesh_position()
        me = 4 * x + 2 * y + c
        packed[...] = jnp.zeros_like(packed)
        for k, g_ref in g_refs.items():
            packed[PACK_AT[k]] = g_ref[...]
        gathered[me] = packed[...]
        copies = []
        for d in range(1, 8):
            dx, dy, dc = (d >> 2) & 1, (d >> 1) & 1, d & 1
            cp = pltpu.make_async_remote_copy(src_ref=packed, dst_ref=gathered.at[me], send_sem=send_sems.at[d - 1], recv_sem=recv_sems.at[d - 1],
                                              device_id=(x ^ dx, y ^ dy, c ^ dc), device_id_type=MESH)
            cp.start()
            copies.append(cp)
        for cp in copies:
            cp.wait()
        total = gathered[0]
        for j in range(1, 8):
            total = total + gathered[j]
        loss_ref[...] = (0.5 / D_MODEL) * jnp.sum(total[PACK_AT["sq_err"]], axis=1, keepdims=True)
        for k in names:
            g = total[PACK_AT[k]]
            go_refs[k][...] = g
            d_refs[k][...], nm_refs[k][...], nv_refs[k][...] = _adamw_math(w_refs[k][...], g, m_refs[k][...], v_refs[k][...])

    vm = pl.BlockSpec(memory_space=pltpu.VMEM)
    like = [jax.ShapeDtypeStruct(w[k].shape, F32) for k in names]
    operands = [grads[k] for k in names + ["sq_err"]] + [d[k] for d in (w, m, v) for k in names]
    results = pl.pallas_call(
        body, name=name, out_shape=[jax.ShapeDtypeStruct((1, 1), F32)] + like * 4, in_specs=[vm] * len(operands), out_specs=[vm] * (1 + 4 * n),
        scratch_shapes=[pltpu.VMEM((PACK_ROWS, D_MODEL), F32), pltpu.VMEM((8, PACK_ROWS, D_MODEL), F32),
                        pltpu.SemaphoreType.DMA((7,)), pltpu.SemaphoreType.DMA((7,))],
    )(*operands)
    return results[0], {k: tuple(results[1 + i * n + j] for i in range(4)) for j, k in enumerate(names)}


def _adamw_math(w, g, m, v):
    m = ADAM_B1 * m + (1.0 - ADAM_B1) * g
    v = ADAM_B2 * v + (1.0 - ADAM_B2) * (g * g)
    m_hat = m / (1.0 - ADAM_B1 ** ADAM_STEP)
    v_hat = v / (1.0 - ADAM_B2 ** ADAM_STEP)
    delta = -ADAM_LR * (m_hat / (jnp.sqrt(v_hat) + ADAM_EPS) + ADAM_WD * w)
    return delta, m, v


def _adamw(w, g, m, v, *, name):
    _, rows, cols = w.shape
    tr = _row_tile(rows)
    blk = pl.BlockSpec((None, tr, cols), lambda i: (0, i, 0))
    flat = pl.BlockSpec((tr, cols), lambda i: (i, 0))

    def body(w_ref, g_ref, m_ref, v_ref, go_ref, d_ref, nm_ref, nv_ref):
        g_v = g_ref[...]
        go_ref[...] = g_v
        d_ref[...], nm_ref[...], nv_ref[...] = _adamw_math(w_ref[...], g_v, m_ref[...], v_ref[...])

    shape = jax.ShapeDtypeStruct((1, rows, cols), F32)
    return pl.pallas_call(body, name=name, grid=(rows // tr,), out_shape=(shape,) * 4, in_specs=[blk, flat, blk, blk], out_specs=(blk,) * 4,
                          compiler_params=_params(("parallel",)))(w, g, m, v)


WEIGHTS = ["w_in", "lb_logits", "hg_norm_gain", "swa_sinks", "rel_bias", "w_mem_kv", "w_branch_hg", "w_branch_swa", "w_branch_mem",
           "w_out", "ln1_g", "ln1_b", "w_up", "w_down", "ln2_g", "ln2_b"]
BIG = ["w_in", "w_mem_kv", "w_branch_hg", "w_branch_swa", "w_branch_mem", "w_out", "w_up", "w_down"]


def kernel(x, mem, w_in, lb_logits, hg_norm_gain, swa_sinks, rel_bias, w_mem_kv, w_branch_hg, w_branch_swa, w_branch_mem, w_out, ln1_g, ln1_b, w_up, w_down, ln2_g, ln2_b, loss_target, m_w_in, m_lb_logits, m_hg_norm_gain, m_swa_sinks, m_rel_bias, m_w_mem_kv, m_w_branch_hg, m_w_branch_swa, m_w_branch_mem, m_w_out, m_ln1_g, m_ln1_b, m_w_up, m_w_down, m_ln2_g, m_ln2_b, v_w_in, v_lb_logits, v_hg_norm_gain, v_swa_sinks, v_rel_bias, v_w_mem_kv, v_w_branch_hg, v_w_branch_swa, v_w_branch_mem, v_w_out, v_ln1_g, v_ln1_b, v_w_up, v_w_down, v_ln2_g, v_ln2_b):
    w = dict(w_in=w_in, lb_logits=lb_logits, hg_norm_gain=hg_norm_gain, swa_sinks=swa_sinks, rel_bias=rel_bias, w_mem_kv=w_mem_kv,
             w_branch_hg=w_branch_hg, w_branch_swa=w_branch_swa, w_branch_mem=w_branch_mem, w_out=w_out, ln1_g=ln1_g, ln1_b=ln1_b,
             w_up=w_up, w_down=w_down, ln2_g=ln2_g, ln2_b=ln2_b)
    m = dict(w_in=m_w_in, lb_logits=m_lb_logits, hg_norm_gain=m_hg_norm_gain, swa_sinks=m_swa_sinks, rel_bias=m_rel_bias, w_mem_kv=m_w_mem_kv,
             w_branch_hg=m_w_branch_hg, w_branch_swa=m_w_branch_swa, w_branch_mem=m_w_branch_mem, w_out=m_w_out, ln1_g=m_ln1_g, ln1_b=m_ln1_b,
             w_up=m_w_up, w_down=m_w_down, ln2_g=m_ln2_g, ln2_b=m_ln2_b)
    v = dict(w_in=v_w_in, lb_logits=v_lb_logits, hg_norm_gain=v_hg_norm_gain, swa_sinks=v_swa_sinks, rel_bias=v_rel_bias, w_mem_kv=v_w_mem_kv,
             w_branch_hg=v_w_branch_hg, w_branch_swa=v_w_branch_swa, w_branch_mem=v_w_branch_mem, w_out=v_w_out, ln1_g=v_ln1_g, ln1_b=v_ln1_b,
             w_up=v_w_up, w_down=v_w_down, ln2_g=v_ln2_g, ln2_b=v_ln2_b)
    shapes = {k: w[k].shape for k in WEIGHTS}
    for d in (w, m, v):
        d["w_in"] = d["w_in"].reshape(D_MODEL, IN_COLS // N_SHARDS).T[None]
    shards = {k: w[k].reshape(w[k].shape[-2], w[k].shape[-1]).astype(BF16) for k in BIG}
    wi4, wmkv = _run_exchanges([_gather_exchange([shards["w_in"], shards["w_mem_kv"]])], name="gather_weights")[0]
    wi_t = wi4.reshape(IN_COLS, D_MODEL)
    wi_parts = (wi_t[0:W_A], wi_t[W_A:W_A + W_B], wi_t[W_A + W_B:W_A + W_B + W_C], wi_t[W_A + W_B + W_C:])

    grad_x, halves, small = _local_step(
        x.reshape(x.shape[-2], D_MODEL), mem.reshape(MEM_LEN, D_MODEL), loss_target.reshape(loss_target.shape[-2], D_MODEL),
        wi_parts, wmkv, shards, lb_logits, hg_norm_gain, swa_sinks, rel_bias, ln1_g, ln1_b, ln2_g, ln2_b, distributed=True)

    reduced = dict(zip(BIG, _join_halves([halves[k] for k in BIG], name="join_halves")))

    outs = {k: _adamw(w[k], reduced[k], m[k], v[k], name="adamw_" + k) for k in BIG}
    loss, small_outs = _small_step(small, w, m, v, name="small_step")
    outs.update(small_outs)
    grad_out, delta_out, m_out, v_out = ({k: outs[k][i] for k in WEIGHTS} for i in range(4))
    for out in (grad_out, delta_out, m_out, v_out):
        out["w_in"] = out["w_in"][0].T

    result = [loss.reshape(()), grad_x.reshape(x.shape)]
    for out in (grad_out, delta_out, m_out, v_out):
        result += [out[k].reshape(shapes[k]) for k in WEIGHTS]
    return tuple(result)
```

```python
import math
from typing import Callable, NamedTuple

import jax
import jax.numpy as jnp
from jax import lax
from jax.experimental import pallas as pl
from jax.experimental.pallas import tpu as pltpu

F32 = jnp.float32
BF16 = jnp.bfloat16
HIGHEST = lax.Precision.HIGHEST
MESH = pl.DeviceIdType.MESH

D_MODEL = 1024
MEM_LEN = 256
HG_HEADS = 8
HG_DK = 128
HG_CHUNK = 64
SWA_HEADS = 16
SWA_KV_HEADS = 2
SWA_GROUP = 8
SWA_HEAD_DIM = 64
SWA_BLOCK = 128
SWA_WINDOW = 128
MEM_HEADS = 4
MEM_HEAD_DIM = 256
NUM_BUCKETS = 32
MAX_DISTANCE = 128
D_FF = 4096
LN_EPS = 1e-5
RMS_EPS = 1e-6
ALPHA = 2.0 ** 0.25
W_A, W_B, W_C, W_D = 4096, 1280, 1024, 3072
IN_COLS = W_A + W_B + W_C + W_D
N_SHARDS = 4
ADAM_LR = 0.001
ADAM_B1 = 0.9
ADAM_B2 = 0.999
ADAM_EPS = 1e-08
ADAM_WD = 0.01
ADAM_STEP = 10
MASK_VALUE = -1e30
VMEM_LIMIT = 56 * 1024 * 1024

NN = ((1,), (0,))
NT = ((1,), (1,))
TN = ((0,), (0,))
HBM = pl.BlockSpec(memory_space=pltpu.HBM)


def _dot(a, b, dims=NN, precision=None):
    return lax.dot_general(a, b, (dims, ((), ())), precision=precision, preferred_element_type=F32)


def _params(sem=None):
    return pltpu.CompilerParams(dimension_semantics=sem, vmem_limit_bytes=VMEM_LIMIT)


def _resident(shape):
    zeros = (0,) * len(shape)
    return pl.BlockSpec(shape, lambda *_: zeros, pipeline_mode=pl.Buffered(1))


def _mm(a, b, *, mode, tm, tn, tk, name, out_dtype=F32, b_panels=False, b_rows=None, out_panels=False, rows_of=None, row_offset=0,
        into=None):
    if mode == "tn":
        kdim, m = a.shape
    else:
        m, kdim = a.shape
    if b_panels:
        n = b.shape[0] * b.shape[2]
        assert b.shape[2] == tn and mode == "nn"
    elif b_rows is not None:
        assert mode == "nt"
        b_offset, n = b_rows
    elif mode == "nt":
        n = b.shape[0]
    else:
        n = b.shape[1]
    assert m % tm == 0 and n % tn == 0 and kdim % tk == 0, (name, m, n, kdim)
    nk = kdim // tk
    dims = {"nn": NN, "nt": NT, "tn": TN}[mode]
    a_spec = pl.BlockSpec((tk, tm), lambda i, j, k: (k, i)) if mode == "tn" else pl.BlockSpec((tm, tk), lambda i, j, k: (i, k))
    if b_panels:
        b_spec = pl.BlockSpec((None, tk, tn), lambda i, j, k: (j, k, 0))
    elif b_rows is not None:
        assert b_offset % BF16_SUBLANES == 0 and tn % BF16_SUBLANES == 0 and tk % 128 == 0
        b_spec = pl.BlockSpec((pl.Element(tn), pl.Element(tk)),
                              lambda i, j, k: (pl.multiple_of(b_offset + j * tn, BF16_SUBLANES), pl.multiple_of(k * tk, 128)))
    elif mode == "nt":
        b_spec = pl.BlockSpec((tn, tk), lambda i, j, k: (j, k))
    else:
        b_spec = pl.BlockSpec((tk, tn), lambda i, j, k: (k, j))
    in_specs = [a_spec, b_spec]
    operands = [a, b]
    aliases = {}
    if out_panels:
        out_shape = jax.ShapeDtypeStruct((n // tn, m, tn), out_dtype)
        o_spec = pl.BlockSpec((None, tm, tn), lambda i, j, k: (j, i, 0))
    elif rows_of is not None:
        out_shape = jax.ShapeDtypeStruct((rows_of, n), out_dtype)
        assert row_offset % BF16_SUBLANES == 0 and tm % BF16_SUBLANES == 0 and tn % 128 == 0
        o_spec = pl.BlockSpec((pl.Element(tm), pl.Element(tn)),
                              lambda i, j, k: (pl.multiple_of(row_offset + i * tm, BF16_SUBLANES), pl.multiple_of(j * tn, 128)))
        if into is not None:
            in_specs.append(pl.BlockSpec(memory_space=pl.ANY))
            operands.append(into)
            aliases = {2: 0}
    else:
        out_shape = jax.ShapeDtypeStruct((m, n), out_dtype)
        o_spec = pl.BlockSpec((tm, tn), lambda i, j, k: (i, j))
    n_in = len(operands)

    def body(*refs):
        a_ref, b_ref, o_ref = refs[0], refs[1], refs[n_in]
        part = _dot(a_ref[...].astype(BF16), b_ref[...].astype(BF16), dims)

        def finish(acc):
            o_ref[...] = acc.astype(out_dtype)

        if nk == 1:
            finish(part)
        else:
            acc_ref = refs[-1]
            k = pl.program_id(2)

            @pl.when(k == 0)
            def _():
                acc_ref[...] = part

            @pl.when(k > 0)
            def _():
                acc_ref[...] += part

            @pl.when(k == nk - 1)
            def _():
                finish(acc_ref[...])

    return pl.pallas_call(
        body, name=name, out_shape=out_shape, grid=(m // tm, n // tn, nk), in_specs=in_specs, out_specs=o_spec,
        scratch_shapes=[pltpu.VMEM((tm, tn), F32)] if nk > 1 else [], input_output_aliases=aliases,
        compiler_params=_params(("parallel", "parallel", "arbitrary")),
    )(*operands)


def _dx_matmul(dzs, wi_t, resid, *, tm, name, exchanges=()):
    s = resid.shape[0]
    npieces = len(dzs)
    offsets = [sum(dz.shape[1] for dz in dzs[:p]) for p in range(npieces)]
    in_specs = [pl.BlockSpec((tm, dz.shape[1]), lambda i: (i, 0)) for dz in dzs]
    in_specs += [_resident(wi_t.shape), pl.BlockSpec((tm, D_MODEL), lambda i: (i, 0))]

    def body(*refs):
        dz_refs, w_ref, r_ref, o_ref = refs[:npieces], refs[npieces], refs[npieces + 1], refs[npieces + 2]
        total = ALPHA * r_ref[...]
        for p in range(npieces):
            total = total + _dot(dz_refs[p][...], w_ref[offsets[p]:offsets[p] + dzs[p].shape[1], :], NN)
        o_ref[...] = total

    return _fused_call(
        body, name=name, out_shape=jax.ShapeDtypeStruct((s, D_MODEL), F32), grid=(s // tm,), in_specs=in_specs,
        out_specs=pl.BlockSpec((tm, D_MODEL), lambda i: (i, 0)), scratch_shapes=[],
        operands=[*dzs, wi_t, resid], exchanges=exchanges)


def _lower_bound(lbl_ref):
    l0, l1 = lbl_ref[0:1, :], lbl_ref[1:2, :]
    mx = jnp.maximum(l0, l1)
    e0, e1 = jnp.exp(l0 - mx), jnp.exp(l1 - mx)
    return e0 / (e0 + e1)


HEAD_COLS = [slice(h * HG_DK, (h + 1) * HG_DK) for h in range(HG_HEADS)]


def _head_mean(x):
    return jnp.concatenate([jnp.broadcast_to(jnp.mean(x[:, c], axis=-1, keepdims=True), (x.shape[0], HG_DK)) for c in HEAD_COLS], axis=1)


def _chunk_forward(q, fl, v, lb, tril_f):
    sg = jax.nn.sigmoid(fl)
    f = lb + (1.0 - lb) * sg
    k = 1.0 - f
    b = _dot(tril_f, jnp.log(f), NN, HIGHEST)
    b_last = b[HG_CHUNK - 1:HG_CHUNK, :]
    eb, enb, eo = jnp.exp(b), jnp.exp(-b), jnp.exp(b_last - b)
    return sg, f, k, b_last, eb, enb, eo, q * eb, k * enb, k * eo


def _hgrn_fwd(za, lb_logits, gain, *, name, exchanges=()):
    s = za.shape[0]
    t = min(256, s)
    ncs = t // HG_CHUNK

    def body(z_ref, lbl_ref, gain_ref, oa_ref, oraw_ref, st_ref, state):
        @pl.when(pl.program_id(0) == 0)
        def _():
            state[...] = jnp.zeros_like(state)

        lb_all = _lower_bound(lbl_ref)
        row = lax.broadcasted_iota(jnp.int32, (HG_CHUNK, HG_CHUNK), 0)
        col = lax.broadcasted_iota(jnp.int32, (HG_CHUNK, HG_CHUNK), 1)
        tril = row >= col
        tril_f = tril.astype(F32)
        gain_all = gain_ref[...]

        def chunk(i, carry):
            r = pl.ds(pl.multiple_of(i * HG_CHUNK, HG_CHUNK), HG_CHUNK)
            q, fl, v, hg = (z_ref[r, j * D_MODEL:(j + 1) * D_MODEL] for j in range(4))
            _, _, _, b_last, _, _, _, q_in, k_in, k_out = _chunk_forward(q, fl, v, lb_all, tril_f)
            q_in_b, k_in_b, k_out_b, vb = (u.astype(BF16) for u in (q_in, k_in, k_out, v))
            decay = jnp.exp(b_last)
            sts = [state[h] for h in range(HG_HEADS)]
            attn = [_dot(q_in_b[:, c], k_in_b[:, c], NT) for c in HEAD_COLS]
            inter = [_dot(q_in_b[:, c], sts[h].astype(BF16), NT) for h, c in enumerate(HEAD_COLS)]
            upd = [_dot(vb[:, c], k_out_b[:, c], TN) for c in HEAD_COLS]
            attn = [jnp.where(tril, a, 0.0).astype(BF16) for a in attn]
            outs = [_dot(attn[h], vb[:, c], NN) + inter[h] for h, c in enumerate(HEAD_COLS)]
            for h, c in enumerate(HEAD_COLS):
                st_ref[h, i] = sts[h]
                state[h] = sts[h] * decay[:, c] + upd[h]
            o = jnp.concatenate(outs, axis=1)
            oraw_ref[r, :] = o
            n = o * lax.rsqrt(_head_mean(o * o) + RMS_EPS)
            oa_ref[r, :] = (n * gain_all * (hg * jax.nn.sigmoid(hg))).astype(BF16)
            return carry

        lax.fori_loop(0, ncs, chunk, 0, unroll=True)

    return _fused_call(
        body, name=name, grid=(s // t,),
        out_shape=(jax.ShapeDtypeStruct((s, D_MODEL), BF16), jax.ShapeDtypeStruct((s, D_MODEL), F32),
                   jax.ShapeDtypeStruct((HG_HEADS, s // HG_CHUNK, HG_DK, HG_DK), F32)),
        in_specs=[pl.BlockSpec((t, W_A), lambda i: (i, 0)), _resident((2, D_MODEL)), _resident((1, D_MODEL))],
        out_specs=(pl.BlockSpec((t, D_MODEL), lambda i: (i, 0)), pl.BlockSpec((t, D_MODEL), lambda i: (i, 0)),
                   pl.BlockSpec((HG_HEADS, ncs, HG_DK, HG_DK), lambda i: (0, i, 0, 0))),
        scratch_shapes=[pltpu.VMEM((HG_HEADS, HG_DK, HG_DK), F32)],
        operands=[za, lb_logits, gain], exchanges=exchanges)


def _hgrn_bwd(za, oraw, do_a, states, lb_logits, gain, *, name, exchanges=()):
    s = za.shape[0]
    t = min(256, s)
    ncs = t // HG_CHUNK
    nt = s // t

    def body(z_ref, oraw_ref, do_ref, st_ref, lbl_ref, gain_ref, dz_ref, stats_ref, dstate):
        step = pl.program_id(0)

        @pl.when(step == 0)
        def _():
            dstate[...] = jnp.zeros_like(dstate)
            stats_ref[...] = jnp.zeros_like(stats_ref)

        lb_all = _lower_bound(lbl_ref)
        row = lax.broadcasted_iota(jnp.int32, (HG_CHUNK, HG_CHUNK), 0)
        col = lax.broadcasted_iota(jnp.int32, (HG_CHUNK, HG_CHUNK), 1)
        tril = row >= col
        tril_f = tril.astype(F32)
        triu_f = (row <= col).astype(F32)
        gain_all = gain_ref[...]

        def chunk(ii, carry):
            i = ncs - 1 - ii
            r = pl.ds(pl.multiple_of(i * HG_CHUNK, HG_CHUNK), HG_CHUNK)
            q, fl, v, hg = (z_ref[r, j * D_MODEL:(j + 1) * D_MODEL] for j in range(4))
            o = oraw_ref[r, :]
            doa = do_ref[r, :]
            rms = lax.rsqrt(_head_mean(o * o) + RMS_EPS)
            n = o * rms
            sgg = jax.nn.sigmoid(hg)
            silu = hg * sgg
            dhg = doa * n * gain_all * (sgg * (1.0 + hg * (1.0 - sgg)))
            dgain = jnp.sum(doa * n * silu, axis=0, keepdims=True)
            dn = doa * gain_all * silu
            do = rms * (dn - n * _head_mean(dn * n))
            sg, f, k, b_last, eb, enb, eo, q_in, k_in, k_out = _chunk_forward(q, fl, v, lb_all, tril_f)
            q_in_b, k_in_b, k_out_b, vb, dob = (u.astype(BF16) for u in (q_in, k_in, k_out, v, do))
            decay = jnp.exp(b_last)
            sts = [st_ref[h, i] for h in range(HG_HEADS)]
            dsts = [dstate[h] for h in range(HG_HEADS)]
            dsts_b = [d.astype(BF16) for d in dsts]
            heads = list(enumerate(HEAD_COLS))
            attn = [_dot(q_in_b[:, c], k_in_b[:, c], NT) for h, c in heads]
            dattn = [_dot(dob[:, c], vb[:, c], NT) for h, c in heads]
            dq_st = [_dot(dob[:, c], sts[h].astype(BF16), NN) for h, c in heads]
            dk_out = [_dot(vb[:, c], dsts_b[h], NN) for h, c in heads]
            dv_st = [_dot(k_out_b[:, c], dsts_b[h], NT) for h, c in heads]
            dst_o = [_dot(dob[:, c], q_in_b[:, c], TN) for h, c in heads]
            attn = [jnp.where(tril, a, 0.0).astype(BF16) for a in attn]
            dattn = [jnp.where(tril, a, 0.0).astype(BF16) for a in dattn]
            dq_in = jnp.concatenate([_dot(dattn[h], k_in_b[:, c], NN) + dq_st[h] for h, c in heads], axis=1)
            dk_in = jnp.concatenate([_dot(dattn[h], q_in_b[:, c], TN) for h, c in heads], axis=1)
            dv = jnp.concatenate([_dot(attn[h], dob[:, c], TN) + dv_st[h] for h, c in heads], axis=1)
            dk_out = jnp.concatenate(dk_out, axis=1)
            dst_st = jnp.concatenate([jnp.sum(dsts[h] * sts[h], axis=0, keepdims=True) for h in range(HG_HEADS)], axis=1)
            for h, c in heads:
                dstate[h] = dsts[h] * decay[:, c] + dst_o[h]
            db_last = decay * dst_st + jnp.sum(dk_out * k_out, axis=0, keepdims=True)
            db = dq_in * q_in - dk_in * k_in - dk_out * k_out
            dg = _dot(triu_f, db, NN, HIGHEST) + db_last
            dk = dk_in * enb + dk_out * eo
            df = dg / f - dk
            stats_ref[0:1, :] += dgain
            stats_ref[1:2, :] += jnp.sum(df * (1.0 - sg), axis=0, keepdims=True)
            dz_ref[r, 0:1024] = (dq_in * eb).astype(BF16)
            dz_ref[r, 1024:2048] = (df * (1.0 - lb_all) * sg * (1.0 - sg)).astype(BF16)
            dz_ref[r, 2048:3072] = dv.astype(BF16)
            dz_ref[r, 3072:4096] = dhg.astype(BF16)
            return carry

        lax.fori_loop(0, ncs, chunk, 0, unroll=True)

        @pl.when(step == nt - 1)
        def _():
            dl0 = stats_ref[1:2, :] * lb_all * (1.0 - lb_all)
            stats_ref[1:2, :] = dl0
            stats_ref[2:3, :] = -dl0

    rev = lambda i: (nt - 1 - i, 0)
    return _fused_call(
        body, name=name, grid=(nt,),
        out_shape=(jax.ShapeDtypeStruct((s, W_A), BF16), jax.ShapeDtypeStruct((8, D_MODEL), F32)),
        in_specs=[pl.BlockSpec((t, W_A), rev), pl.BlockSpec((t, D_MODEL), rev), pl.BlockSpec((t, D_MODEL), rev),
                  pl.BlockSpec((HG_HEADS, ncs, HG_DK, HG_DK), lambda i: (0, nt - 1 - i, 0, 0)),
                  _resident((2, D_MODEL)), _resident((1, D_MODEL))],
        out_specs=(pl.BlockSpec((t, W_A), rev), pl.BlockSpec((8, D_MODEL), lambda i: (0, 0))),
        scratch_shapes=[pltpu.VMEM((HG_HEADS, HG_DK, HG_DK), F32)],
        operands=[za, oraw, do_a, states, lb_logits, gain], exchanges=exchanges)


def _t5_bucket(n):
    max_exact = NUM_BUCKETS // 2
    nf = jnp.maximum(n, 1).astype(F32)
    large = max_exact + (jnp.log(nf / max_exact) / math.log(MAX_DISTANCE / max_exact) * (NUM_BUCKETS - max_exact)).astype(jnp.int32)
    large = jnp.minimum(large, NUM_BUCKETS - 1)
    return jnp.where(n < max_exact, n, large)


def _bias_selector():
    qi = jnp.arange(SWA_BLOCK)[:, None] + SWA_BLOCK
    kj = jnp.arange(2 * SWA_BLOCK)[None, :]
    dist = qi - kj
    band = ((dist >= 0) & (dist < SWA_WINDOW)).reshape(1, -1)
    bucket = _t5_bucket(jnp.clip(dist, 0, SWA_WINDOW - 1)).reshape(1, -1)
    onehot = ((bucket == jnp.arange(NUM_BUCKETS)[:, None]) & band).astype(F32)
    return onehot, jnp.where(band, 0.0, MASK_VALUE).astype(F32)


def _bias_table(rel_bias_t, onehot, maskrow, *, name):
    def body(rb_ref, oh_ref, mask_ref, o_ref):
        o_ref[...] = _dot(rb_ref[...], oh_ref[...], NN, HIGHEST) + mask_ref[...]

    return pl.pallas_call(body, name=name, out_shape=jax.ShapeDtypeStruct((SWA_HEADS, onehot.shape[1]), F32),
                          compiler_params=_params())(rel_bias_t, onehot, maskrow)


def _bias_grad(dbias2d, onehot, *, name):
    def body(db_ref, oh_ref, o_ref):
        o_ref[...] = _dot(db_ref[...], oh_ref[...], NT, HIGHEST)

    return pl.pallas_call(body, name=name, out_shape=jax.ShapeDtypeStruct((SWA_HEADS, NUM_BUCKETS), F32),
                          compiler_params=_params())(dbias2d, onehot)


GROUP_LANES = SWA_GROUP * SWA_BLOCK


def _swa_operands(zq_ref, kv_cur_ref, kv_prev_ref):
    q = (zq_ref[:, 0:1024] * (SWA_HEAD_DIM ** -0.5)).astype(BF16)
    kv_c = kv_cur_ref[...].astype(BF16)
    kv_p = kv_prev_ref[...].astype(BF16)
    kks = [jnp.concatenate([kv_p[:, g * 64:(g + 1) * 64], kv_c[:, g * 64:(g + 1) * 64]], axis=0) for g in range(SWA_KV_HEADS)]
    vvs = [jnp.concatenate([kv_p[:, 128 + g * 64:128 + (g + 1) * 64], kv_c[:, 128 + g * 64:128 + (g + 1) * 64]], axis=0)
           for g in range(SWA_KV_HEADS)]
    return q, kks, vvs


def _stack_heads(x, g):
    return jnp.concatenate([x[:, h * SWA_HEAD_DIM:(h + 1) * SWA_HEAD_DIM] for h in range(g * SWA_GROUP, (g + 1) * SWA_GROUP)], axis=0)


def _heads_to_lanes(xt):
    pairs = []
    for j in range(0, SWA_GROUP, 2):
        two = jnp.concatenate([xt[:, j * SWA_BLOCK:(j + 1) * SWA_BLOCK], xt[:, (j + 1) * SWA_BLOCK:(j + 2) * SWA_BLOCK]], axis=0)
        pairs.append(two.T)
    return jnp.concatenate(pairs, axis=1)


def _swa_softmax(score_t, bias_ref, sink_ref, g):
    lanes = slice(g * GROUP_LANES, (g + 1) * GROUP_LANES)
    sc = score_t + bias_ref[:, lanes]
    sink = sink_ref[:, lanes]
    m = jnp.maximum(jnp.max(sc, axis=0, keepdims=True), sink)
    e = jnp.exp(sc - m)
    e_sink = jnp.exp(sink - m)
    return e, 1.0 / (jnp.sum(e, axis=0, keepdims=True) + e_sink), e_sink


def _swa_tables(bias2d, sinks):
    bias_t = bias2d.reshape(SWA_HEADS, SWA_BLOCK, 2 * SWA_BLOCK).transpose(2, 0, 1).reshape(2 * SWA_BLOCK, SWA_HEADS * SWA_BLOCK)
    first = jnp.where(jnp.arange(2 * SWA_BLOCK)[:, None] < SWA_BLOCK, MASK_VALUE, bias_t)
    return jnp.stack([first, bias_t]), jnp.repeat(sinks, SWA_BLOCK, axis=1)


def _swa_fwd(zb, bias_tables, sink_lanes, *, name, exchanges=()):
    s = zb.shape[0]
    nb = s // SWA_BLOCK

    def body(zq_ref, kvc_ref, kvp_ref, bias_ref, sink_ref, o_ref):
        q, kks, vvs = _swa_operands(zq_ref, kvc_ref, kvp_ref)
        groups = range(SWA_KV_HEADS)
        scores = [_dot(kks[g], _stack_heads(q, g), NT) for g in groups]
        probs = []
        for g in groups:
            e, inv, _ = _swa_softmax(scores[g], bias_ref, sink_ref, g)
            probs.append((e * inv).astype(BF16))
        outs = [_dot(vvs[g], probs[g], TN) for g in groups]
        o_ref[...] = jnp.concatenate([_heads_to_lanes(outs[g]) for g in groups], axis=1).astype(BF16)

    return _fused_call(
        body, name=name, grid=(nb,), out_shape=jax.ShapeDtypeStruct((s, D_MODEL), BF16),
        in_specs=[pl.BlockSpec((SWA_BLOCK, W_B), lambda n: (n, 0)),
                  pl.BlockSpec((SWA_BLOCK, 256), lambda n: (n, 4)),
                  pl.BlockSpec((SWA_BLOCK, 256), lambda n: (jnp.maximum(n - 1, 0), 4)),
                  pl.BlockSpec((None, 2 * SWA_BLOCK, SWA_HEADS * SWA_BLOCK), lambda n: (jnp.minimum(n, 1), 0, 0)),
                  _resident((1, SWA_HEADS * SWA_BLOCK))],
        out_specs=pl.BlockSpec((SWA_BLOCK, D_MODEL), lambda n: (n, 0)), scratch_shapes=[],
        operands=[zb, zb, zb, bias_tables, sink_lanes], exchanges=exchanges)


def _swa_bwd(zb, do_b, bias_tables, sink_lanes, *, name, exchanges=()):
    s = zb.shape[0]
    nb = s // SWA_BLOCK
    scale = SWA_HEAD_DIM ** -0.5

    def body(zq_ref, kvc_ref, kvp_ref, do_ref, bias_ref, sink_ref, dz_ref, dbias_ref, dsink_ref, carry, dsink_acc):
        step = pl.program_id(0)

        @pl.when(step == 0)
        def _():
            carry[...] = jnp.zeros_like(carry)
            dsink_acc[...] = jnp.zeros_like(dsink_acc)
            dbias_ref[...] = jnp.zeros_like(dbias_ref)

        q, kks, vvs = _swa_operands(zq_ref, kvc_ref, kvp_ref)
        groups = range(SWA_KV_HEADS)
        do = do_ref[...].astype(BF16)
        q_rows = [_stack_heads(q, g) for g in groups]
        do_rows = [_stack_heads(do, g) for g in groups]
        scores = [_dot(kks[g], q_rows[g], NT) for g in groups]
        dps = [_dot(vvs[g], do_rows[g], NT) for g in groups]
        ps, dss = [], []
        for g in groups:
            lanes = slice(g * GROUP_LANES, (g + 1) * GROUP_LANES)
            e, inv, e_sink = _swa_softmax(scores[g], bias_ref, sink_ref, g)
            p = e * inv
            delta = jnp.sum(p * dps[g], axis=0, keepdims=True)
            ds = p * (dps[g] - delta)
            dbias_ref[:, lanes] += ds
            dsink_acc[:, lanes] -= e_sink * inv * delta
            ps.append(p.astype(BF16))
            dss.append(ds.astype(BF16))
        dqs = [_dot(kks[g], dss[g], TN) * scale for g in groups]
        dkks = [_dot(dss[g], q_rows[g], NN) for g in groups]
        dvvs = [_dot(ps[g], do_rows[g], NN) for g in groups]
        dkv = jnp.concatenate(dkks + dvvs, axis=1)
        dz_ref[:, 0:1024] = jnp.concatenate([_heads_to_lanes(dqs[g]) for g in groups], axis=1).astype(BF16)
        dz_ref[:, 1024:1280] = (dkv[SWA_BLOCK:, :] + carry[...]).astype(BF16)
        carry[...] = dkv[:SWA_BLOCK, :]

        @pl.when(step == nb - 1)
        def _():
            acc = dsink_acc[...]
            dsink_ref[...] = jnp.concatenate([jnp.sum(acc[:, h * SWA_BLOCK:(h + 1) * SWA_BLOCK], axis=1, keepdims=True)
                                              for h in range(SWA_HEADS)], axis=1)

    rev = lambda i: (nb - 1 - i, 0)
    table_shape = (2 * SWA_BLOCK, SWA_HEADS * SWA_BLOCK)
    return _fused_call(
        body, name=name, grid=(nb,),
        out_shape=(jax.ShapeDtypeStruct((s, W_B), BF16), jax.ShapeDtypeStruct(table_shape, F32), jax.ShapeDtypeStruct((1, SWA_HEADS), F32)),
        in_specs=[pl.BlockSpec((SWA_BLOCK, W_B), rev),
                  pl.BlockSpec((SWA_BLOCK, 256), lambda i: (nb - 1 - i, 4)),
                  pl.BlockSpec((SWA_BLOCK, 256), lambda i: (jnp.maximum(nb - 2 - i, 0), 4)),
                  pl.BlockSpec((SWA_BLOCK, D_MODEL), rev),
                  pl.BlockSpec((None,) + table_shape, lambda i: (jnp.minimum(nb - 1 - i, 1), 0, 0)),
                  _resident((1, SWA_HEADS * SWA_BLOCK))],
        out_specs=(pl.BlockSpec((SWA_BLOCK, W_B), rev), pl.BlockSpec(table_shape, lambda i: (0, 0)),
                   pl.BlockSpec((1, SWA_HEADS), lambda i: (0, 0))),
        scratch_shapes=[pltpu.VMEM((SWA_BLOCK, 256), F32), pltpu.VMEM((1, SWA_HEADS * SWA_BLOCK), F32)],
        operands=[zb, zb, zb, do_b, bias_tables, sink_lanes], exchanges=exchanges)


def _mem_probs(zc_ref, mkv_ref, h):
    cols = slice(h * MEM_HEAD_DIM, (h + 1) * MEM_HEAD_DIM)
    qh = (zc_ref[:, cols] * (MEM_HEAD_DIM ** -0.5)).astype(BF16)
    sc = _dot(qh, mkv_ref[:, cols], NT)
    e = jnp.exp(sc - jnp.max(sc, axis=-1, keepdims=True))
    return qh, e / jnp.sum(e, axis=-1, keepdims=True)


def _mem_fwd(zc, mkv, *, name):
    s = zc.shape[0]
    t = min(512, s)

    def body(zc_ref, mkv_ref, o_ref):
        for h in range(MEM_HEADS):
            _, p = _mem_probs(zc_ref, mkv_ref, h)
            vh = mkv_ref[:, D_MODEL + h * MEM_HEAD_DIM:D_MODEL + (h + 1) * MEM_HEAD_DIM]
            o_ref[:, h * MEM_HEAD_DIM:(h + 1) * MEM_HEAD_DIM] = _dot(p.astype(BF16), vh, NN).astype(BF16)

    return pl.pallas_call(
        body, name=name, grid=(s // t,), out_shape=jax.ShapeDtypeStruct((s, D_MODEL), BF16),
        in_specs=[pl.BlockSpec((t, D_MODEL), lambda i: (i, 0)), _resident((MEM_LEN, 2 * D_MODEL))],
        out_specs=pl.BlockSpec((t, D_MODEL), lambda i: (i, 0)), compiler_params=_params(("parallel",)),
    )(zc, mkv)


def _mem_bwd(zc, do_c, mkv, *, name):
    s = zc.shape[0]
    t = min(512, s)

    def body(zc_ref, do_ref, mkv_ref, dz_ref, dmkv_ref):
        @pl.when(pl.program_id(0) == 0)
        def _():
            dmkv_ref[...] = jnp.zeros_like(dmkv_ref)

        for h in range(MEM_HEADS):
            cols = slice(h * MEM_HEAD_DIM, (h + 1) * MEM_HEAD_DIM)
            vcols = slice(D_MODEL + h * MEM_HEAD_DIM, D_MODEL + (h + 1) * MEM_HEAD_DIM)
            qh, p = _mem_probs(zc_ref, mkv_ref, h)
            doh = do_ref[:, cols].astype(BF16)
            dp = _dot(doh, mkv_ref[:, vcols], NT)
            ds = (p * (dp - jnp.sum(p * dp, axis=-1, keepdims=True))).astype(BF16)
            dz_ref[:, cols] = (_dot(ds, mkv_ref[:, cols], NN) * (MEM_HEAD_DIM ** -0.5)).astype(BF16)
            dmkv_ref[:, cols] += _dot(ds, qh, TN)
            dmkv_ref[:, vcols] += _dot(p.astype(BF16), doh, TN)

    return pl.pallas_call(
        body, name=name, grid=(s // t,),
        out_shape=(jax.ShapeDtypeStruct((s, D_MODEL), BF16), jax.ShapeDtypeStruct((MEM_LEN, 2 * D_MODEL), F32)),
        in_specs=[pl.BlockSpec((t, D_MODEL), lambda i: (i, 0)), pl.BlockSpec((t, D_MODEL), lambda i: (i, 0)),
                  _resident((MEM_LEN, 2 * D_MODEL))],
        out_specs=(pl.BlockSpec((t, D_MODEL), lambda i: (i, 0)), pl.BlockSpec((MEM_LEN, 2 * D_MODEL), lambda i: (0, 0))),
        compiler_params=_params(("arbitrary",)),
    )(zc, do_c, mkv)


def _normalize(pre):
    mu = jnp.mean(pre, axis=-1, keepdims=True)
    xc = pre - mu
    rstd = lax.rsqrt(jnp.mean(xc * xc, axis=-1, keepdims=True) + LN_EPS)
    return xc * rstd, rstd


def _layer_norm_bwd(dh, xhat, rstd, g):
    dxh = dh * g
    dpre = rstd * (dxh - jnp.mean(dxh, axis=-1, keepdims=True) - xhat * jnp.mean(dxh * xhat, axis=-1, keepdims=True))
    return dpre, jnp.sum(dh * xhat, axis=0, keepdims=True), jnp.sum(dh, axis=0, keepdims=True)


def _merge_fwd(o_a, o_b, o_c, zd, x, wbr, wo, *, name):
    s = x.shape[0]
    t = min(256, s)
    row = lambda w: pl.BlockSpec((t, w), lambda i: (i, 0))

    def body(oa_ref, ob_ref, oc_ref, zd_ref, x_ref, wbr_ref, wo_ref, xhat_ref, rstd_ref, merged_ref, pa_ref, pb_ref, pc_ref):
        merged = jnp.zeros((t, D_MODEL), F32)
        for b, (o_ref, p_ref) in enumerate(((oa_ref, pa_ref), (ob_ref, pb_ref), (oc_ref, pc_ref))):
            p = _dot(o_ref[...], wbr_ref[b], NN)
            p_ref[...] = p.astype(BF16)
            merged = merged + jax.nn.sigmoid(zd_ref[:, b * D_MODEL:(b + 1) * D_MODEL]) * p
        merged_b = merged.astype(BF16)
        merged_ref[...] = merged_b
        xhat, rstd = _normalize(ALPHA * x_ref[...] + _dot(merged_b, wo_ref[...], NN))
        xhat_ref[...] = xhat
        rstd_ref[...] = rstd

    act = jax.ShapeDtypeStruct((s, D_MODEL), F32)
    return pl.pallas_call(
        body, name=name, grid=(s // t,),
        out_shape=(act, jax.ShapeDtypeStruct((s, 1), F32)) + (jax.ShapeDtypeStruct((s, D_MODEL), BF16),) * 4,
        in_specs=[row(D_MODEL), row(D_MODEL), row(D_MODEL), row(W_D), row(D_MODEL),
                  _resident((3, D_MODEL, D_MODEL)), _resident((D_MODEL, D_MODEL))],
        out_specs=(row(D_MODEL), row(1), row(D_MODEL), row(D_MODEL), row(D_MODEL), row(D_MODEL)),
        compiler_params=_params(("parallel",)),
    )(o_a, o_b, o_c, zd, x, wbr, wo)


def _merge_bwd(dpre1, zd, pa, pb, pc, wbr, wo, *, name, exchanges=()):
    s = dpre1.shape[0]
    t = min(256, s)
    row = lambda w: pl.BlockSpec((t, w), lambda i: (i, 0))

    def body(dpre_ref, zd_ref, pa_ref, pb_ref, pc_ref, wbr_ref, wo_ref, dzd_ref, dpa_ref, dpb_ref, dpc_ref, doa_ref, dob_ref, doc_ref):
        dmerged = _dot(dpre_ref[...].astype(BF16), wo_ref[...], NT)
        branches = ((pa_ref, dpa_ref, doa_ref), (pb_ref, dpb_ref, dob_ref), (pc_ref, dpc_ref, doc_ref))
        for b, (p_ref, dp_ref, do_ref) in enumerate(branches):
            gate = jax.nn.sigmoid(zd_ref[:, b * D_MODEL:(b + 1) * D_MODEL])
            dzd_ref[:, b * D_MODEL:(b + 1) * D_MODEL] = (dmerged * p_ref[...] * gate * (1.0 - gate)).astype(BF16)
            dp = (dmerged * gate).astype(BF16)
            dp_ref[...] = dp
            do_ref[...] = _dot(dp, wbr_ref[b], NT).astype(do_ref.dtype)

    act = jax.ShapeDtypeStruct((s, D_MODEL), F32)
    actb = jax.ShapeDtypeStruct((s, D_MODEL), BF16)
    return _fused_call(
        body, name=name, grid=(s // t,),
        out_shape=(jax.ShapeDtypeStruct((s, W_D), BF16), actb, actb, actb, act, actb, actb),
        in_specs=[row(D_MODEL), row(W_D), row(D_MODEL), row(D_MODEL), row(D_MODEL),
                  _resident((3, D_MODEL, D_MODEL)), _resident((D_MODEL, D_MODEL))],
        out_specs=(row(W_D),) + (row(D_MODEL),) * 6, scratch_shapes=[],
        operands=[dpre1, zd, pa, pb, pc, wbr, wo], exchanges=exchanges)


def _mlp_loss(xhat1, rstd1, target, ln1_g, ln1_b, ln2_g, ln2_b, wu, wd, *, name):
    s = xhat1.shape[0]
    t = min(256, s)
    npan = wu.shape[0]
    row = lambda w: pl.BlockSpec((t, w), lambda i: (i, 0))
    vec = _resident((1, D_MODEL))

    def body(xhat_ref, rstd_ref, tgt_ref, g1_ref, b1_ref, g2_ref, b2_ref, wu_ref, wd_ref,
             dpre1_ref, dpre2_ref, h1_ref, a_ref, du_ref, stats_ref):
        @pl.when(pl.program_id(0) == 0)
        def _():
            stats_ref[...] = jnp.zeros_like(stats_ref)

        xhat1_v = xhat_ref[...]
        h1 = xhat1_v * g1_ref[...] + b1_ref[...]
        h1_b = h1.astype(BF16)
        h1_ref[...] = h1_b
        us = []
        ff = jnp.zeros((t, D_MODEL), F32)
        for j in range(npan):
            u = _dot(h1_b, wu_ref[j], NN)
            us.append(u)
            r = jnp.maximum(u, 0.0)
            a_b = (r * r).astype(BF16)
            a_ref[:, j * D_MODEL:(j + 1) * D_MODEL] = a_b
            ff = ff + _dot(a_b, wd_ref[j], NN)
        xhat2, rstd2 = _normalize(ALPHA * h1 + ff)
        err = xhat2 * g2_ref[...] + b2_ref[...] - tgt_ref[...]
        stats_ref[4:5, :] += jnp.sum(err * err, axis=0, keepdims=True)
        dpre2, dg2, db2 = _layer_norm_bwd(err * (1.0 / D_MODEL), xhat2, rstd2, g2_ref[...])
        stats_ref[0:1, :] += dg2
        stats_ref[1:2, :] += db2
        dpre2_b = dpre2.astype(BF16)
        dpre2_ref[...] = dpre2_b
        dh1 = ALPHA * dpre2
        for j in range(npan):
            du_b = (_dot(dpre2_b, wd_ref[j], NT) * (2.0 * jnp.maximum(us[j], 0.0))).astype(BF16)
            du_ref[:, j * D_MODEL:(j + 1) * D_MODEL] = du_b
            dh1 = dh1 + _dot(du_b, wu_ref[j], NT)
        dpre1, dg1, db1 = _layer_norm_bwd(dh1, xhat1_v, rstd_ref[...], g1_ref[...])
        stats_ref[2:3, :] += dg1
        stats_ref[3:4, :] += db1
        dpre1_ref[...] = dpre1

    actb = jax.ShapeDtypeStruct((s, D_MODEL), BF16)
    wide = jax.ShapeDtypeStruct((s, D_FF), BF16)
    return pl.pallas_call(
        body, name=name, grid=(s // t,),
        out_shape=(jax.ShapeDtypeStruct((s, D_MODEL), F32), actb, actb, wide, wide, jax.ShapeDtypeStruct((8, D_MODEL), F32)),
        in_specs=[row(D_MODEL), row(1), row(D_MODEL), vec, vec, vec, vec,
                  _resident((npan, D_MODEL, D_MODEL)), _resident((npan, D_MODEL, D_MODEL))],
        out_specs=(row(D_MODEL), row(D_MODEL), row(D_MODEL), row(D_FF), row(D_FF), pl.BlockSpec((8, D_MODEL), lambda i: (0, 0))),
        compiler_params=_params(("arbitrary",)),
    )(xhat1, rstd1, target, ln1_g, ln1_b, ln2_g, ln2_b, wu, wd)


BRANCH_WEIGHTS = ("w_branch_hg", "w_branch_swa", "w_branch_mem")


def _local_step(x, mem, target, wi_t, wmkv, late, lb_logits, gain, sinks, rel_bias, ln1_g, ln1_b, ln2_g, ln2_b, *, distributed):
    s = x.shape[0]
    tm = min(1024, s)
    tk = min(2048, s)
    xb = x.astype(BF16)
    memb = mem.astype(BF16)
    if distributed:
        cx, cy, cc = lax.axis_index("x"), lax.axis_index("y"), lax.axis_index("c")
        pos = jnp.stack([2 * cx + cy, cc]).astype(jnp.int32)
    gather = (lambda names: [_gather_exchange([late[k] for k in names])]) if distributed else (lambda names: [])
    to_sibling = (lambda grads: [_sibling_halves_exchange(grads)]) if distributed else (lambda grads: [])
    to_chips = (lambda sums: [_chip_partials_exchange([bf for bf, _ in sums])]) if distributed else (lambda sums: [])

    def chip_sums(names, grads, from_sibling):
        return [_add_sibling(g, o, pos, name="add_sibling_" + k) for k, g, o in zip(names, grads, from_sibling)]

    def shard_sums(names, sums, from_chips):
        return {k: _add_chips(mine, o, pos, name="add_chips_" + k) for k, (_, mine), o in zip(names, sums, from_chips)}

    za = _mm(xb, wi_t, mode="nt", tm=min(512, s), tn=W_A, tk=D_MODEL, name="proj_a", b_rows=(0, W_A))
    zb = _mm(xb, wi_t, mode="nt", tm=tm, tn=W_B, tk=D_MODEL, name="proj_b", out_dtype=BF16, b_rows=(W_A, W_B))
    zc = _mm(xb, wi_t, mode="nt", tm=tm, tn=W_C, tk=D_MODEL, name="proj_c", out_dtype=BF16, b_rows=(W_A + W_B, W_C))
    zd = _mm(xb, wi_t, mode="nt", tm=min(512, s), tn=W_D, tk=D_MODEL, name="proj_d", b_rows=(W_A + W_B + W_C, W_D))
    mkv = _mm(memb, wmkv, mode="nn", tm=MEM_LEN, tn=512, tk=D_MODEL, name="mem_kv", out_dtype=BF16, b_panels=True)
    onehot, maskrow = _bias_selector()
    bias_tables, sink_lanes = _swa_tables(_bias_table(rel_bias.T, onehot, maskrow, name="bias_table"), sinks)
    (o_a, o_raw, states), landed = _hgrn_fwd(za, lb_logits, gain, name="hgrn_fwd", exchanges=gather(("w_up", "w_down")))
    wu, wd = landed[0] if distributed else (late["wu"], late["wd"])
    o_b, landed = _swa_fwd(zb, bias_tables, sink_lanes, name="swa_fwd", exchanges=gather(BRANCH_WEIGHTS + ("w_out",)))
    if distributed:
        wbr = jnp.stack([wb.reshape(D_MODEL, D_MODEL) for wb in landed[0][:3]])
        wo = landed[0][3].reshape(D_MODEL, D_MODEL)
    else:
        wbr, wo = late["wbr"], late["wo"]
    o_c = _mem_fwd(zc, mkv, name="mem_fwd")
    xhat1, rstd1, merged, pa, pb, pc = _merge_fwd(o_a, o_b, o_c, zd, x, wbr, wo, name="merge_fwd")

    dpre1, dpre2, h1, act, du, ln_stats = _mlp_loss(xhat1, rstd1, target, ln1_g, ln1_b, ln2_g, ln2_b, wu, wd, name="mlp_loss")
    ffn = ("w_down", "w_up")
    g_ffn = [_mm(act, dpre2, mode="tn", tm=1024, tn=D_MODEL, tk=tk, name="grad_w_down").reshape(N_SHARDS, D_FF // N_SHARDS, D_MODEL),
             _mm(h1, du, mode="tn", tm=D_MODEL, tn=1024, tk=tk, name="grad_w_up", out_panels=True)]

    (dzd, dpa, dpb, dpc, do_a, do_b, do_c), landed = _merge_bwd(dpre1, zd, pa, pb, pc, wbr, wo, name="merge_bwd", exchanges=to_sibling(g_ffn))
    sums_ffn = chip_sums(ffn, g_ffn, landed[0]) if distributed else []
    merge = BRANCH_WEIGHTS + ("w_out",)
    g_merge = [_mm(o, dp, mode="tn", tm=D_MODEL, tn=D_MODEL, tk=tk, name="grad_" + k).reshape(N_SHARDS, D_MODEL // N_SHARDS, D_MODEL)
               for k, o, dp in zip(merge, (o_a, o_b, o_c, merged), (dpa, dpb, dpc, dpre1))]
    (dza, hg_stats), landed = _hgrn_bwd(za, o_raw, do_a, states, lb_logits, gain, name="hgrn_bwd",
                                        exchanges=to_chips(sums_ffn) + to_sibling(g_merge))
    halves = shard_sums(ffn, sums_ffn, landed[0]) if distributed else {}
    sums_merge = chip_sums(merge, g_merge, landed[1]) if distributed else []
    (dzb, dbias_t, dsinks), landed = _swa_bwd(zb, do_b, bias_tables, sink_lanes, name="swa_bwd", exchanges=to_chips(sums_merge))
    if distributed:
        halves.update(shard_sums(merge, sums_merge, landed[0]))
    dbias = dbias_t.reshape(2 * SWA_BLOCK, SWA_HEADS, SWA_BLOCK).transpose(1, 2, 0).reshape(SWA_HEADS, -1)
    d_rel_bias = _bias_grad(dbias, onehot, name="bias_grad").T
    dzc, dmkv = _mem_bwd(zc, do_c, mkv, name="mem_bwd")

    proj = ("w_in", "w_mem_kv")
    g_wi, offset = None, 0
    for dz, nm in ((dza, "grad_w_in_a"), (dzb, "grad_w_in_b"), (dzc, "grad_w_in_c"), (dzd, "grad_w_in_d")):
        g_wi = _mm(dz, xb, mode="tn", tm=dz.shape[1] if dz.shape[1] <= 1280 else 1024, tn=D_MODEL, tk=tk, name=nm,
                   rows_of=IN_COLS, row_offset=offset, into=g_wi)
        offset += dz.shape[1]
    g_proj = [g_wi.reshape(N_SHARDS, IN_COLS // N_SHARDS, D_MODEL),
              _mm(memb, dmkv, mode="tn", tm=D_MODEL, tn=512, tk=MEM_LEN, name="grad_w_mem_kv", out_panels=True)]
    sums_proj = chip_sums(proj, g_proj, _run_exchanges(to_sibling(g_proj), name="reduce_sibling_proj")[0]) if distributed else []
    grad_x, landed = _dx_matmul([dza, dzb, dzc, dzd], wi_t, dpre1, tm=min(512, s), name="grad_x",
                                exchanges=to_chips(sums_proj))
    if distributed:
        halves.update(shard_sums(proj, sums_proj, landed[0]))
    else:
        halves = dict(zip(ffn + merge + proj, g_ffn + g_merge + g_proj))
    small = dict(lb_logits=hg_stats[1:3], hg_norm_gain=hg_stats[0:1], swa_sinks=dsinks, rel_bias=d_rel_bias,
                 ln1_g=ln_stats[2:3], ln1_b=ln_stats[3:4], ln2_g=ln_stats[0:1], ln2_b=ln_stats[1:2], sq_err=ln_stats[4:5])
    return grad_x, halves, small


def _mesh_position():
    x, y, c = lax.axis_index("x"), lax.axis_index("y"), lax.axis_index("c")
    chips = [(1 - x, y), (x, 1 - y), (1 - x, 1 - y)]
    return x, y, c, chips


class _Exchange(NamedTuple):
    operands: list
    out_shapes: list
    n_sems: int
    start: Callable
    finish: Callable


def _gather_exchange(shards):
    n = len(shards)
    per = 7

    def plan(ins, outs, send_sems, recv_sems):
        x, y, c, chips = _mesh_position()
        me = 2 * x + y
        sibling = (x, y, 1 - c)

        def half(a, slot, hc):
            rh = shards[a].shape[0] // 2
            return outs[a].at[slot, pl.ds(hc * rh, rh), :]

        def copy(a, k, src, dst, to):
            return pltpu.make_async_remote_copy(src_ref=src, dst_ref=dst, send_sem=send_sems.at[a * per + k], recv_sem=recv_sems.at[a * per + k],
                                                device_id=to, device_id_type=MESH)

        own = [copy(a, 6, ins[a], outs[a].at[me], sibling) for a in range(n)]
        to_chips = [copy(a, k, ins[a].at[pl.ds(c * (shards[a].shape[0] // 2), shards[a].shape[0] // 2), :], half(a, me, c), (cx, cy, c))
                    for k, (cx, cy) in enumerate(chips) for a in range(n)]
        arrived = [copy(a, k, half(a, 2 * cx + cy, c), half(a, 2 * cx + cy, c), (cx, cy, c)) for k, (cx, cy) in enumerate(chips) for a in range(n)]
        passed_on = [copy(a, 3 + k, half(a, 2 * cx + cy, c), half(a, 2 * cx + cy, c), sibling) for k, (cx, cy) in enumerate(chips) for a in range(n)]
        from_sibling = [copy(a, 3 + k, half(a, 2 * cx + cy, 1 - c), half(a, 2 * cx + cy, 1 - c), sibling)
                        for k, (cx, cy) in enumerate(chips) for a in range(n)]
        own_arrived = [copy(a, 6, outs[a].at[me], outs[a].at[me], sibling) for a in range(n)]
        return own, to_chips, arrived, passed_on, from_sibling, own_arrived

    def start(*refs):
        own, to_chips, _, _, _, _ = plan(*refs)
        for cp in own + to_chips:
            cp.start()

    def finish(*refs):
        own, to_chips, arrived, passed_on, from_sibling, own_arrived = plan(*refs)
        for landed, onward in zip(arrived, passed_on):
            landed.wait_recv()
            onward.start()
        for cp in from_sibling + own_arrived:
            cp.wait_recv()
        for cp in own + to_chips + passed_on:
            cp.wait_send()

    return _Exchange(list(shards), [jax.ShapeDtypeStruct((N_SHARDS,) + w.shape, w.dtype) for w in shards], per * n, start, finish)


def _sibling_halves_exchange(grads):
    n = len(grads)

    def plan(ins, outs, send_sems, recv_sems):
        x, y, c, _ = _mesh_position()
        return [pltpu.make_async_remote_copy(src_ref=ins[a].at[:, pl.ds((1 - c) * (grads[a].shape[1] // 2), grads[a].shape[1] // 2), :],
                                             dst_ref=outs[a], send_sem=send_sems.at[a], recv_sem=recv_sems.at[a],
                                             device_id=(x, y, 1 - c), device_id_type=MESH) for a in range(n)]

    def start(*refs):
        for cp in plan(*refs):
            cp.start()

    def finish(*refs):
        for cp in plan(*refs):
            cp.wait()

    return _Exchange(list(grads), [jax.ShapeDtypeStruct((g.shape[0], g.shape[1] // 2, g.shape[2]), g.dtype) for g in grads], n, start, finish)


def _chip_partials_exchange(sums):
    n = len(sums)

    def plan(ins, outs, send_sems, recv_sems):
        _, _, c, chips = _mesh_position()
        return [pltpu.make_async_remote_copy(src_ref=ins[a].at[2 * cx + cy], dst_ref=outs[a].at[k], send_sem=send_sems.at[a * 3 + k],
                                             recv_sem=recv_sems.at[a * 3 + k], device_id=(cx, cy, c), device_id_type=MESH)
                for k, (cx, cy) in enumerate(chips) for a in range(n)]

    def start(*refs):
        for cp in plan(*refs):
            cp.start()

    def finish(*refs):
        for cp in plan(*refs):
            cp.wait()

    return _Exchange(list(sums), [jax.ShapeDtypeStruct((3,) + g.shape[1:], g.dtype) for g in sums], 3 * n, start, finish)


def _fused_call(body, *, name, grid, in_specs, out_specs, out_shape, scratch_shapes, operands, exchanges=()):
    single = not isinstance(out_shape, (tuple, list))
    out_specs = [out_specs] if single else list(out_specs)
    out_shape = [out_shape] if single else list(out_shape)
    n_in, n_out, n_scr = len(in_specs), len(out_specs), len(scratch_shapes)
    x_in = [len(e.operands) for e in exchanges]
    x_out = [len(e.out_shapes) for e in exchanges]

    def wrapped(*refs):
        refs = list(refs)
        ins = refs[:n_in]
        pos = n_in
        ex_ins = []
        for k in x_in:
            ex_ins.append(refs[pos:pos + k])
            pos += k
        outs = refs[pos:pos + n_out]
        pos += n_out
        ex_outs = []
        for k in x_out:
            ex_outs.append(refs[pos:pos + k])
            pos += k
        scratch = refs[pos:pos + n_scr]
        sems = refs[pos + n_scr:]
        first, last = None, None
        for axis, size in enumerate(grid):
            at_start, at_end = pl.program_id(axis) == 0, pl.program_id(axis) == size - 1
            first = at_start if first is None else first & at_start
            last = at_end if last is None else last & at_end

        @pl.when(first)
        def _():
            for i, e in enumerate(exchanges):
                e.start(ex_ins[i], ex_outs[i], sems[2 * i], sems[2 * i + 1])

        body(*ins, *outs, *scratch)

        @pl.when(last)
        def _():
            for i, e in enumerate(exchanges):
                e.finish(ex_ins[i], ex_outs[i], sems[2 * i], sems[2 * i + 1])

    n_x_in, n_x_out = sum(x_in), sum(x_out)
    results = pl.pallas_call(
        wrapped if exchanges else body, name=name, grid=grid,
        in_specs=list(in_specs) + [HBM] * n_x_in,
        out_specs=out_specs + [HBM] * n_x_out,
        out_shape=out_shape + [s for e in exchanges for s in e.out_shapes],
        scratch_shapes=list(scratch_shapes) + [pltpu.SemaphoreType.DMA((e.n_sems,)) for e in exchanges for _ in range(2)],
        compiler_params=_params(("arbitrary",) * len(grid)),
    )(*operands, *[a for e in exchanges for a in e.operands])
    own = results[0] if single else tuple(results[:n_out])
    landed, pos = [], n_out
    for k in x_out:
        landed.append(list(results[pos:pos + k]))
        pos += k
    return own, landed


def _run_exchanges(exchanges, *, name):
    def body(*refs):
        n_in = sum(len(e.operands) for e in exchanges)
        n_out = sum(len(e.out_shapes) for e in exchanges)
        ins, outs, sems = refs[:n_in], refs[n_in:n_in + n_out], refs[n_in + n_out:]
        spans, i, o = [], 0, 0
        for e in exchanges:
            spans.append((ins[i:i + len(e.operands)], outs[o:o + len(e.out_shapes)]))
            i, o = i + len(e.operands), o + len(e.out_shapes)
        for k, e in enumerate(exchanges):
            e.start(*spans[k], sems[2 * k], sems[2 * k + 1])
        for k, e in enumerate(exchanges):
            e.finish(*spans[k], sems[2 * k], sems[2 * k + 1])

    operands = [a for e in exchanges for a in e.operands]
    shapes = [s for e in exchanges for s in e.out_shapes]
    results = pl.pallas_call(
        body, name=name, out_shape=shapes, in_specs=[HBM] * len(operands), out_specs=[HBM] * len(shapes),
        scratch_shapes=[pltpu.SemaphoreType.DMA((e.n_sems,)) for e in exchanges for _ in range(2)],
    )(*operands)
    landed, pos = [], 0
    for e in exchanges:
        landed.append(list(results[pos:pos + len(e.out_shapes)]))
        pos += len(e.out_shapes)
    return landed


ROW_TILE_MAX = 640
BF16_SUBLANES = 16


def _row_tile(rows):
    for tr in range(min(rows, ROW_TILE_MAX), 0, -1):
        if rows % tr == 0 and tr % BF16_SUBLANES == 0:
            return tr
    raise ValueError(rows)


def _add_sibling(grad, other, pos, *, name):
    p, r, cols = grad.shape
    rh = r // 2
    tr = _row_tile(rh)
    nb = rh // tr

    def body(pos_ref, g_ref, o_ref, sb_ref, mine_ref):
        total = g_ref[...] + o_ref[...]
        sb_ref[...] = total.astype(BF16)

        @pl.when(pl.program_id(1) == pos_ref[0])
        def _():
            mine_ref[...] = total

    return pl.pallas_call(
        body, name=name, out_shape=(jax.ShapeDtypeStruct((p, rh, cols), BF16), jax.ShapeDtypeStruct((rh, cols), F32)),
        grid_spec=pltpu.PrefetchScalarGridSpec(
            num_scalar_prefetch=1, grid=(nb, p),
            in_specs=[pl.BlockSpec((None, tr, cols), lambda i, j, pos_ref: (j, pos_ref[1] * nb + i, 0)),
                      pl.BlockSpec((None, tr, cols), lambda i, j, pos_ref: (j, i, 0))],
            out_specs=(pl.BlockSpec((None, tr, cols), lambda i, j, pos_ref: (j, i, 0)),
                       pl.BlockSpec((tr, cols), lambda i, j, pos_ref: (i, 0)))),
        compiler_params=_params(("parallel", "arbitrary")),
    )(pos, grad, other)


def _add_chips(mine, others, pos, *, name):
    rh, cols = mine.shape
    tr = _row_tile(rh)
    nb = rh // tr

    def body(pos_ref, s_ref, o_ref, r_ref):
        r_ref[...] = ((s_ref[...] + o_ref[0].astype(F32)) + o_ref[1].astype(F32)) + o_ref[2].astype(F32)

    return pl.pallas_call(
        body, name=name, out_shape=jax.ShapeDtypeStruct((2 * rh, cols), F32),
        grid_spec=pltpu.PrefetchScalarGridSpec(
            num_scalar_prefetch=1, grid=(nb,),
            in_specs=[pl.BlockSpec((tr, cols), lambda i, pos_ref: (i, 0)),
                      pl.BlockSpec((3, tr, cols), lambda i, pos_ref: (0, i, 0))],
            out_specs=pl.BlockSpec((tr, cols), lambda i, pos_ref: (pos_ref[1] * nb + i, 0))),
        compiler_params=_params(("parallel",)),
    )(pos, mine, others)


def _join_halves(bufs, *, name):
    n = len(bufs)

    def body(*refs):
        ins, outs = refs[:n], refs[n:2 * n]
        send_sems, recv_sems = refs[2 * n:]
        x, y, c, _ = _mesh_position()

        def copy(a, hc):
            rh = bufs[a].shape[0] // 2
            rows = pl.ds(hc * rh, rh)
            return pltpu.make_async_remote_copy(src_ref=ins[a].at[rows, :], dst_ref=outs[a].at[rows, :], send_sem=send_sems.at[a],
                                                recv_sem=recv_sems.at[a], device_id=(x, y, 1 - c), device_id_type=MESH)

        for a in range(n):
            copy(a, c).start()
        for a in range(n):
            copy(a, c).wait_send()
            copy(a, 1 - c).wait_recv()

    return pl.pallas_call(
        body, name=name, out_shape=[jax.ShapeDtypeStruct(b.shape, b.dtype) for b in bufs],
        in_specs=[HBM] * n, out_specs=[HBM] * n, input_output_aliases={a: a for a in range(n)},
        scratch_shapes=[pltpu.SemaphoreType.DMA((n,)), pltpu.SemaphoreType.DMA((n,))],
    )(*bufs)


SMALL = ["lb_logits", "hg_norm_gain", "swa_sinks", "rel_bias", "ln1_g", "ln1_b", "ln2_g", "ln2_b"]
PACK_ROWS = 48
PACK_AT = dict(lb_logits=(slice(0, 2), slice(0, D_MODEL)), hg_norm_gain=(slice(2, 3), slice(0, D_MODEL)), ln1_g=(slice(3, 4), slice(0, D_MODEL)),
               ln1_b=(slice(4, 5), slice(0, D_MODEL)), ln2_g=(slice(5, 6), slice(0, D_MODEL)), ln2_b=(slice(6, 7), slice(0, D_MODEL)),
               swa_sinks=(slice(7, 8), slice(0, SWA_HEADS)), sq_err=(slice(8, 9), slice(0, D_MODEL)),
               rel_bias=(slice(16, 16 + NUM_BUCKETS), slice(0, SWA_HEADS)))


def _reduce_small(grads, *, name):
    names = SMALL + ["sq_err"]

    def body(*refs):
        g_refs = dict(zip(names, refs[:len(names)]))
        total_ref, packed, gathered, send_sems, recv_sems = refs[len(names):]
        x, y, c, _ = _mesh_position()
        me = 4 * x + 2 * y + c
        packed[...] = jnp.zeros_like(packed)
        for k, g_ref in g_refs.items():
            packed[PACK_AT[k]] = g_ref[...]
        gathered[me] = packed[...]
        copies = []
        for d in range(1, 8):
            dx, dy, dc = (d >> 2) & 1, (d >> 1) & 1, d & 1
            cp = pltpu.make_async_remote_copy(src_ref=packed, dst_ref=gathered.at[me], send_sem=send_sems.at[d - 1], recv_sem=recv_sems.at[d - 1],
                                              device_id=(x ^ dx, y ^ dy, c ^ dc), device_id_type=MESH)
            cp.start()
            copies.append(cp)
        for cp in copies:
            cp.wait()
        total = gathered[0]
        for j in range(1, 8):
            total = total + gathered[j]
        total_ref[...] = total

    vm = pl.BlockSpec(memory_space=pltpu.VMEM)
    return pl.pallas_call(
        body, name=name, out_shape=jax.ShapeDtypeStruct((PACK_ROWS, D_MODEL), F32), in_specs=[vm] * len(names), out_specs=vm,
        scratch_shapes=[pltpu.VMEM((PACK_ROWS, D_MODEL), F32), pltpu.VMEM((8, PACK_ROWS, D_MODEL), F32),
                        pltpu.SemaphoreType.DMA((7,)), pltpu.SemaphoreType.DMA((7,))],
    )(*[grads[k] for k in names])


def _adamw_small(total, w, m, v, *, name):
    names = SMALL
    n = len(names)

    def body(*refs):
        total_ref = refs[0]
        w_refs, m_refs, v_refs = (dict(zip(names, refs[1 + i * n:1 + (i + 1) * n])) for i in range(3))
        loss_ref = refs[1 + 3 * n]
        go_refs, d_refs, nm_refs, nv_refs = (dict(zip(names, refs[2 + (3 + i) * n:2 + (4 + i) * n])) for i in range(4))
        loss_ref[...] = (0.5 / D_MODEL) * jnp.sum(total_ref[PACK_AT["sq_err"]], axis=1, keepdims=True)
        for k in names:
            g = total_ref[PACK_AT[k]]
            go_refs[k][...] = g
            d_refs[k][...], nm_refs[k][...], nv_refs[k][...] = _adamw_math(w_refs[k][...], g, m_refs[k][...], v_refs[k][...])

    like = [jax.ShapeDtypeStruct(w[k].shape, F32) for k in names]
    results = pl.pallas_call(body, name=name, out_shape=[jax.ShapeDtypeStruct((1, 1), F32)] + like * 4,
                             compiler_params=_params())(total, *[d[k] for d in (w, m, v) for k in names])
    return results[0], {k: tuple(results[1 + i * n + j] for i in range(4)) for j, k in enumerate(names)}


def _adamw_math(w, g, m, v):
    m = ADAM_B1 * m + (1.0 - ADAM_B1) * g
    v = ADAM_B2 * v + (1.0 - ADAM_B2) * (g * g)
    m_hat = m / (1.0 - ADAM_B1 ** ADAM_STEP)
    v_hat = v / (1.0 - ADAM_B2 ** ADAM_STEP)
    delta = -ADAM_LR * (m_hat / (jnp.sqrt(v_hat) + ADAM_EPS) + ADAM_WD * w)
    return delta, m, v


def _adamw(w, g, m, v, *, name):
    _, rows, cols = w.shape
    tr = _row_tile(rows)
    blk = pl.BlockSpec((None, tr, cols), lambda i: (0, i, 0))
    flat = pl.BlockSpec((tr, cols), lambda i: (i, 0))

    def body(w_ref, g_ref, m_ref, v_ref, go_ref, d_ref, nm_ref, nv_ref):
        g_v = g_ref[...]
        go_ref[...] = g_v
        d_ref[...], nm_ref[...], nv_ref[...] = _adamw_math(w_ref[...], g_v, m_ref[...], v_ref[...])

    shape = jax.ShapeDtypeStruct((1, rows, cols), F32)
    return pl.pallas_call(body, name=name, grid=(rows // tr,), out_shape=(shape,) * 4, in_specs=[blk, flat, blk, blk], out_specs=(blk,) * 4,
                          compiler_params=_params(("parallel",)))(w, g, m, v)


WEIGHTS = ["w_in", "lb_logits", "hg_norm_gain", "swa_sinks", "rel_bias", "w_mem_kv", "w_branch_hg", "w_branch_swa", "w_branch_mem",
           "w_out", "ln1_g", "ln1_b", "w_up", "w_down", "ln2_g", "ln2_b"]
BIG = ["w_in", "w_mem_kv", "w_branch_hg", "w_branch_swa", "w_branch_mem", "w_out", "w_up", "w_down"]


def kernel(x, mem, w_in, lb_logits, hg_norm_gain, swa_sinks, rel_bias, w_mem_kv, w_branch_hg, w_branch_swa, w_branch_mem, w_out, ln1_g, ln1_b, w_up, w_down, ln2_g, ln2_b, loss_target, m_w_in, m_lb_logits, m_hg_norm_gain, m_swa_sinks, m_rel_bias, m_w_mem_kv, m_w_branch_hg, m_w_branch_swa, m_w_branch_mem, m_w_out, m_ln1_g, m_ln1_b, m_w_up, m_w_down, m_ln2_g, m_ln2_b, v_w_in, v_lb_logits, v_hg_norm_gain, v_swa_sinks, v_rel_bias, v_w_mem_kv, v_w_branch_hg, v_w_branch_swa, v_w_branch_mem, v_w_out, v_ln1_g, v_ln1_b, v_w_up, v_w_down, v_ln2_g, v_ln2_b):
    w = dict(w_in=w_in, lb_logits=lb_logits, hg_norm_gain=hg_norm_gain, swa_sinks=swa_sinks, rel_bias=rel_bias, w_mem_kv=w_mem_kv,
             w_branch_hg=w_branch_hg, w_branch_swa=w_branch_swa, w_branch_mem=w_branch_mem, w_out=w_out, ln1_g=ln1_g, ln1_b=ln1_b,
             w_up=w_up, w_down=w_down, ln2_g=ln2_g, ln2_b=ln2_b)
    m = dict(w_in=m_w_in, lb_logits=m_lb_logits, hg_norm_gain=m_hg_norm_gain, swa_sinks=m_swa_sinks, rel_bias=m_rel_bias, w_mem_kv=m_w_mem_kv,
             w_branch_hg=m_w_branch_hg, w_branch_swa=m_w_branch_swa, w_branch_mem=m_w_branch_mem, w_out=m_w_out, ln1_g=m_ln1_g, ln1_b=m_ln1_b,
             w_up=m_w_up, w_down=m_w_down, ln2_g=m_ln2_g, ln2_b=m_ln2_b)
    v = dict(w_in=v_w_in, lb_logits=v_lb_logits, hg_norm_gain=v_hg_norm_gain, swa_sinks=v_swa_sinks, rel_bias=v_rel_bias, w_mem_kv=v_w_mem_kv,
             w_branch_hg=v_w_branch_hg, w_branch_swa=v_w_branch_swa, w_branch_mem=v_w_branch_mem, w_out=v_w_out, ln1_g=v_ln1_g, ln1_b=v_ln1_b,
             w_up=v_w_up, w_down=v_w_down, ln2_g=v_ln2_g, ln2_b=v_ln2_b)
    shapes = {k: w[k].shape for k in WEIGHTS}
    for d in (w, m, v):
        d["w_in"] = d["w_in"].reshape(D_MODEL, IN_COLS // N_SHARDS).T[None]
    shards = {k: w[k].reshape(w[k].shape[-2], w[k].shape[-1]).astype(BF16) for k in BIG}
    wi4, wmkv = _run_exchanges([_gather_exchange([shards["w_in"], shards["w_mem_kv"]])], name="gather_weights")[0]
    wi_t = wi4.reshape(IN_COLS, D_MODEL)

    grad_x, halves, small = _local_step(
        x.reshape(x.shape[-2], D_MODEL), mem.reshape(MEM_LEN, D_MODEL), loss_target.reshape(loss_target.shape[-2], D_MODEL),
        wi_t, wmkv, shards, lb_logits, hg_norm_gain, swa_sinks, rel_bias, ln1_g, ln1_b, ln2_g, ln2_b, distributed=True)

    reduced = dict(zip(BIG, _join_halves([halves[k] for k in BIG], name="join_halves")))

    outs = {k: _adamw(w[k], reduced[k], m[k], v[k], name="adamw_" + k) for k in BIG}
    loss, small_outs = _adamw_small(_reduce_small(small, name="reduce_small"), w, m, v, name="adamw_small")
    outs.update(small_outs)
    grad_out, delta_out, m_out, v_out = ({k: outs[k][i] for k in WEIGHTS} for i in range(4))
    for out in (grad_out, delta_out, m_out, v_out):
        out["w_in"] = out["w_in"][0].T

    result = [loss.reshape(()), grad_x.reshape(x.shape)]
    for out in (grad_out, delta_out, m_out, v_out):
        result += [out[k].reshape(shapes[k]) for k in WEIGHTS]
    return tuple(result)
```

```python
import math
from typing import Callable, NamedTuple

import jax
import jax.numpy as jnp
from jax import lax
from jax.experimental import pallas as pl
from jax.experimental.pallas import tpu as pltpu

F32 = jnp.float32
BF16 = jnp.bfloat16
HIGHEST = lax.Precision.HIGHEST
MESH = pl.DeviceIdType.MESH

D_MODEL = 1024
MEM_LEN = 256
HG_HEADS = 8
HG_DK = 128
HG_CHUNK = 64
SWA_HEADS = 16
SWA_KV_HEADS = 2
SWA_GROUP = 8
SWA_HEAD_DIM = 64
SWA_BLOCK = 128
SWA_WINDOW = 128
MEM_HEADS = 4
MEM_HEAD_DIM = 256
NUM_BUCKETS = 32
MAX_DISTANCE = 128
D_FF = 4096
LN_EPS = 1e-5
RMS_EPS = 1e-6
ALPHA = 2.0 ** 0.25
W_A, W_B, W_C, W_D = 4096, 1280, 1024, 3072
IN_COLS = W_A + W_B + W_C + W_D
N_SHARDS = 4
ADAM_LR = 0.001
ADAM_B1 = 0.9
ADAM_B2 = 0.999
ADAM_EPS = 1e-08
ADAM_WD = 0.01
ADAM_STEP = 10
MASK_VALUE = -1e30
VMEM_LIMIT = 56 * 1024 * 1024

NN = ((1,), (0,))
NT = ((1,), (1,))
TN = ((0,), (0,))
HBM = pl.BlockSpec(memory_space=pltpu.HBM)


def _dot(a, b, dims=NN, precision=None):
    return lax.dot_general(a, b, (dims, ((), ())), precision=precision, preferred_element_type=F32)


def _params(sem=None):
    return pltpu.CompilerParams(dimension_semantics=sem, vmem_limit_bytes=VMEM_LIMIT)


def _resident(shape):
    zeros = (0,) * len(shape)
    return pl.BlockSpec(shape, lambda *_: zeros, pipeline_mode=pl.Buffered(1))


def _mm(a, b, *, mode, tm, tn, tk, name, out_dtype=F32, b_panels=False, b_rows=None, out_panels=False, rows_of=None, row_offset=0,
        into=None):
    if mode == "tn":
        kdim, m = a.shape
    else:
        m, kdim = a.shape
    if b_panels:
        n = b.shape[0] * b.shape[2]
        assert b.shape[2] == tn and mode == "nn"
    elif b_rows is not None:
        assert mode == "nt"
        b_offset, n = b_rows
    elif mode == "nt":
        n = b.shape[0]
    else:
        n = b.shape[1]
    assert m % tm == 0 and n % tn == 0 and kdim % tk == 0, (name, m, n, kdim)
    nk = kdim // tk
    dims = {"nn": NN, "nt": NT, "tn": TN}[mode]
    a_spec = pl.BlockSpec((tk, tm), lambda i, j, k: (k, i)) if mode == "tn" else pl.BlockSpec((tm, tk), lambda i, j, k: (i, k))
    if b_panels:
        b_spec = pl.BlockSpec((None, tk, tn), lambda i, j, k: (j, k, 0))
    elif b_rows is not None:
        assert b_offset % BF16_SUBLANES == 0 and tn % BF16_SUBLANES == 0 and tk % 128 == 0
        b_spec = pl.BlockSpec((pl.Element(tn), pl.Element(tk)),
                              lambda i, j, k: (pl.multiple_of(b_offset + j * tn, BF16_SUBLANES), pl.multiple_of(k * tk, 128)))
    elif mode == "nt":
        b_spec = pl.BlockSpec((tn, tk), lambda i, j, k: (j, k))
    else:
        b_spec = pl.BlockSpec((tk, tn), lambda i, j, k: (k, j))
    in_specs = [a_spec, b_spec]
    operands = [a, b]
    aliases = {}
    if out_panels:
        out_shape = jax.ShapeDtypeStruct((n // tn, m, tn), out_dtype)
        o_spec = pl.BlockSpec((None, tm, tn), lambda i, j, k: (j, i, 0))
    elif rows_of is not None:
        out_shape = jax.ShapeDtypeStruct((rows_of, n), out_dtype)
        assert row_offset % BF16_SUBLANES == 0 and tm % BF16_SUBLANES == 0 and tn % 128 == 0
        o_spec = pl.BlockSpec((pl.Element(tm), pl.Element(tn)),
                              lambda i, j, k: (pl.multiple_of(row_offset + i * tm, BF16_SUBLANES), pl.multiple_of(j * tn, 128)))
        if into is not None:
            in_specs.append(pl.BlockSpec(memory_space=pl.ANY))
            operands.append(into)
            aliases = {2: 0}
    else:
        out_shape = jax.ShapeDtypeStruct((m, n), out_dtype)
        o_spec = pl.BlockSpec((tm, tn), lambda i, j, k: (i, j))
    n_in = len(operands)

    def body(*refs):
        a_ref, b_ref, o_ref = refs[0], refs[1], refs[n_in]
        part = _dot(a_ref[...].astype(BF16), b_ref[...].astype(BF16), dims)

        def finish(acc):
            o_ref[...] = acc.astype(out_dtype)

        if nk == 1:
            finish(part)
        else:
            acc_ref = refs[-1]
            k = pl.program_id(2)

            @pl.when(k == 0)
            def _():
                acc_ref[...] = part

            @pl.when(k > 0)
            def _():
                acc_ref[...] += part

            @pl.when(k == nk - 1)
            def _():
                finish(acc_ref[...])

    return pl.pallas_call(
        body, name=name, out_shape=out_shape, grid=(m // tm, n // tn, nk), in_specs=in_specs, out_specs=o_spec,
        scratch_shapes=[pltpu.VMEM((tm, tn), F32)] if nk > 1 else [], input_output_aliases=aliases,
        compiler_params=_params(("parallel", "parallel", "arbitrary")),
    )(*operands)


def _dx_matmul(dzs, wi_t, resid, *, tm, name, exchanges=()):
    s = resid.shape[0]
    npieces = len(dzs)
    offsets = [sum(dz.shape[1] for dz in dzs[:p]) for p in range(npieces)]
    in_specs = [pl.BlockSpec((tm, dz.shape[1]), lambda i: (i, 0)) for dz in dzs]
    in_specs += [_resident(wi_t.shape), pl.BlockSpec((tm, D_MODEL), lambda i: (i, 0))]

    def body(*refs):
        dz_refs, w_ref, r_ref, o_ref = refs[:npieces], refs[npieces], refs[npieces + 1], refs[npieces + 2]
        total = ALPHA * r_ref[...]
        for p in range(npieces):
            total = total + _dot(dz_refs[p][...], w_ref[offsets[p]:offsets[p] + dzs[p].shape[1], :], NN)
        o_ref[...] = total

    return _fused_call(
        body, name=name, out_shape=jax.ShapeDtypeStruct((s, D_MODEL), F32), grid=(s // tm,), in_specs=in_specs,
        out_specs=pl.BlockSpec((tm, D_MODEL), lambda i: (i, 0)), scratch_shapes=[],
        operands=[*dzs, wi_t, resid], exchanges=exchanges)


def _lower_bound(lbl_ref):
    l0, l1 = lbl_ref[0:1, :], lbl_ref[1:2, :]
    mx = jnp.maximum(l0, l1)
    e0, e1 = jnp.exp(l0 - mx), jnp.exp(l1 - mx)
    return e0 / (e0 + e1)


HEAD_COLS = [slice(h * HG_DK, (h + 1) * HG_DK) for h in range(HG_HEADS)]


def _head_mean(x):
    return jnp.concatenate([jnp.broadcast_to(jnp.mean(x[:, c], axis=-1, keepdims=True), (x.shape[0], HG_DK)) for c in HEAD_COLS], axis=1)


def _chunk_forward(q, fl, v, lb, tril_f):
    sg = jax.nn.sigmoid(fl)
    f = lb + (1.0 - lb) * sg
    k = 1.0 - f
    b = _dot(tril_f, jnp.log(f), NN, HIGHEST)
    b_last = b[HG_CHUNK - 1:HG_CHUNK, :]
    eb, enb, eo = jnp.exp(b), jnp.exp(-b), jnp.exp(b_last - b)
    return sg, f, k, b_last, eb, enb, eo, q * eb, k * enb, k * eo


def _hgrn_fwd(za, lb_logits, gain, *, name, exchanges=()):
    s = za.shape[0]
    t = min(256, s)
    ncs = t // HG_CHUNK

    def body(z_ref, lbl_ref, gain_ref, oa_ref, oraw_ref, st_ref, state):
        @pl.when(pl.program_id(0) == 0)
        def _():
            state[...] = jnp.zeros_like(state)

        lb_all = _lower_bound(lbl_ref)
        row = lax.broadcasted_iota(jnp.int32, (HG_CHUNK, HG_CHUNK), 0)
        col = lax.broadcasted_iota(jnp.int32, (HG_CHUNK, HG_CHUNK), 1)
        tril = row >= col
        tril_f = tril.astype(F32)
        gain_all = gain_ref[...]

        def chunk(i, carry):
            r = pl.ds(pl.multiple_of(i * HG_CHUNK, HG_CHUNK), HG_CHUNK)
            q, fl, v, hg = (z_ref[r, j * D_MODEL:(j + 1) * D_MODEL] for j in range(4))
            _, _, _, b_last, _, _, _, q_in, k_in, k_out = _chunk_forward(q, fl, v, lb_all, tril_f)
            q_in_b, k_in_b, k_out_b, vb = (u.astype(BF16) for u in (q_in, k_in, k_out, v))
            decay = jnp.exp(b_last)
            sts = [state[h] for h in range(HG_HEADS)]
            attn = [_dot(q_in_b[:, c], k_in_b[:, c], NT) for c in HEAD_COLS]
            inter = [_dot(q_in_b[:, c], sts[h].astype(BF16), NT) for h, c in enumerate(HEAD_COLS)]
            upd = [_dot(vb[:, c], k_out_b[:, c], TN) for c in HEAD_COLS]
            attn = [jnp.where(tril, a, 0.0).astype(BF16) for a in attn]
            outs = [_dot(attn[h], vb[:, c], NN) + inter[h] for h, c in enumerate(HEAD_COLS)]
            for h, c in enumerate(HEAD_COLS):
                st_ref[h, i] = sts[h]
                state[h] = sts[h] * decay[:, c] + upd[h]
            o = jnp.concatenate(outs, axis=1)
            oraw_ref[r, :] = o
            n = o * lax.rsqrt(_head_mean(o * o) + RMS_EPS)
            oa_ref[r, :] = (n * gain_all * (hg * jax.nn.sigmoid(hg))).astype(BF16)
            return carry

        lax.fori_loop(0, ncs, chunk, 0, unroll=True)

    return _fused_call(
        body, name=name, grid=(s // t,),
        out_shape=(jax.ShapeDtypeStruct((s, D_MODEL), BF16), jax.ShapeDtypeStruct((s, D_MODEL), F32),
                   jax.ShapeDtypeStruct((HG_HEADS, s // HG_CHUNK, HG_DK, HG_DK), F32)),
        in_specs=[pl.BlockSpec((t, W_A), lambda i: (i, 0)), _resident((2, D_MODEL)), _resident((1, D_MODEL))],
        out_specs=(pl.BlockSpec((t, D_MODEL), lambda i: (i, 0)), pl.BlockSpec((t, D_MODEL), lambda i: (i, 0)),
                   pl.BlockSpec((HG_HEADS, ncs, HG_DK, HG_DK), lambda i: (0, i, 0, 0))),
        scratch_shapes=[pltpu.VMEM((HG_HEADS, HG_DK, HG_DK), F32)],
        operands=[za, lb_logits, gain], exchanges=exchanges)


def _hgrn_bwd(za, oraw, do_a, states, lb_logits, gain, *, name, exchanges=()):
    s = za.shape[0]
    t = min(256, s)
    ncs = t // HG_CHUNK
    nt = s // t

    def body(z_ref, oraw_ref, do_ref, st_ref, lbl_ref, gain_ref, dz_ref, stats_ref, dstate):
        step = pl.program_id(0)

        @pl.when(step == 0)
        def _():
            dstate[...] = jnp.zeros_like(dstate)
            stats_ref[...] = jnp.zeros_like(stats_ref)

        lb_all = _lower_bound(lbl_ref)
        row = lax.broadcasted_iota(jnp.int32, (HG_CHUNK, HG_CHUNK), 0)
        col = lax.broadcasted_iota(jnp.int32, (HG_CHUNK, HG_CHUNK), 1)
        tril = row >= col
        tril_f = tril.astype(F32)
        triu_f = (row <= col).astype(F32)
        gain_all = gain_ref[...]

        def chunk(ii, carry):
            i = ncs - 1 - ii
            r = pl.ds(pl.multiple_of(i * HG_CHUNK, HG_CHUNK), HG_CHUNK)
            q, fl, v, hg = (z_ref[r, j * D_MODEL:(j + 1) * D_MODEL] for j in range(4))
            o = oraw_ref[r, :]
            doa = do_ref[r, :]
            rms = lax.rsqrt(_head_mean(o * o) + RMS_EPS)
            n = o * rms
            sgg = jax.nn.sigmoid(hg)
            silu = hg * sgg
            dhg = doa * n * gain_all * (sgg * (1.0 + hg * (1.0 - sgg)))
            dgain = jnp.sum(doa * n * silu, axis=0, keepdims=True)
            dn = doa * gain_all * silu
            do = rms * (dn - n * _head_mean(dn * n))
            sg, f, k, b_last, eb, enb, eo, q_in, k_in, k_out = _chunk_forward(q, fl, v, lb_all, tril_f)
            q_in_b, k_in_b, k_out_b, vb, dob = (u.astype(BF16) for u in (q_in, k_in, k_out, v, do))
            decay = jnp.exp(b_last)
            sts = [st_ref[h, i] for h in range(HG_HEADS)]
            dsts = [dstate[h] for h in range(HG_HEADS)]
            dsts_b = [d.astype(BF16) for d in dsts]
            heads = list(enumerate(HEAD_COLS))
            attn = [_dot(q_in_b[:, c], k_in_b[:, c], NT) for h, c in heads]
            dattn = [_dot(dob[:, c], vb[:, c], NT) for h, c in heads]
            dq_st = [_dot(dob[:, c], sts[h].astype(BF16), NN) for h, c in heads]
            dk_out = [_dot(vb[:, c], dsts_b[h], NN) for h, c in heads]
            dv_st = [_dot(k_out_b[:, c], dsts_b[h], NT) for h, c in heads]
            dst_o = [_dot(dob[:, c], q_in_b[:, c], TN) for h, c in heads]
            attn = [jnp.where(tril, a, 0.0).astype(BF16) for a in attn]
            dattn = [jnp.where(tril, a, 0.0).astype(BF16) for a in dattn]
            dq_in = jnp.concatenate([_dot(dattn[h], k_in_b[:, c], NN) + dq_st[h] for h, c in heads], axis=1)
            dk_in = jnp.concatenate([_dot(dattn[h], q_in_b[:, c], TN) for h, c in heads], axis=1)
            dv = jnp.concatenate([_dot(attn[h], dob[:, c], TN) + dv_st[h] for h, c in heads], axis=1)
            dk_out = jnp.concatenate(dk_out, axis=1)
            dst_st = jnp.concatenate([jnp.sum(dsts[h] * sts[h], axis=0, keepdims=True) for h in range(HG_HEADS)], axis=1)
            for h, c in heads:
                dstate[h] = dsts[h] * decay[:, c] + dst_o[h]
            db_last = decay * dst_st + jnp.sum(dk_out * k_out, axis=0, keepdims=True)
            db = dq_in * q_in - dk_in * k_in - dk_out * k_out
            dg = _dot(triu_f, db, NN, HIGHEST) + db_last
            dk = dk_in * enb + dk_out * eo
            df = dg / f - dk
            stats_ref[0:1, :] += dgain
            stats_ref[1:2, :] += jnp.sum(df * (1.0 - sg), axis=0, keepdims=True)
            dz_ref[r, 0:1024] = (dq_in * eb).astype(BF16)
            dz_ref[r, 1024:2048] = (df * (1.0 - lb_all) * sg * (1.0 - sg)).astype(BF16)
            dz_ref[r, 2048:3072] = dv.astype(BF16)
            dz_ref[r, 3072:4096] = dhg.astype(BF16)
            return carry

        lax.fori_loop(0, ncs, chunk, 0, unroll=True)

        @pl.when(step == nt - 1)
        def _():
            dl0 = stats_ref[1:2, :] * lb_all * (1.0 - lb_all)
            stats_ref[1:2, :] = dl0
            stats_ref[2:3, :] = -dl0

    rev = lambda i: (nt - 1 - i, 0)
    return _fused_call(
        body, name=name, grid=(nt,),
        out_shape=(jax.ShapeDtypeStruct((s, W_A), BF16), jax.ShapeDtypeStruct((8, D_MODEL), F32)),
        in_specs=[pl.BlockSpec((t, W_A), rev), pl.BlockSpec((t, D_MODEL), rev), pl.BlockSpec((t, D_MODEL), rev),
                  pl.BlockSpec((HG_HEADS, ncs, HG_DK, HG_DK), lambda i: (0, nt - 1 - i, 0, 0)),
                  _resident((2, D_MODEL)), _resident((1, D_MODEL))],
        out_specs=(pl.BlockSpec((t, W_A), rev), pl.BlockSpec((8, D_MODEL), lambda i: (0, 0))),
        scratch_shapes=[pltpu.VMEM((HG_HEADS, HG_DK, HG_DK), F32)],
        operands=[za, oraw, do_a, states, lb_logits, gain], exchanges=exchanges)


def _t5_bucket(n):
    max_exact = NUM_BUCKETS // 2
    nf = jnp.maximum(n, 1).astype(F32)
    large = max_exact + (jnp.log(nf / max_exact) / math.log(MAX_DISTANCE / max_exact) * (NUM_BUCKETS - max_exact)).astype(jnp.int32)
    large = jnp.minimum(large, NUM_BUCKETS - 1)
    return jnp.where(n < max_exact, n, large)


def _bias_selector():
    qi = jnp.arange(SWA_BLOCK)[:, None] + SWA_BLOCK
    kj = jnp.arange(2 * SWA_BLOCK)[None, :]
    dist = qi - kj
    band = ((dist >= 0) & (dist < SWA_WINDOW)).reshape(1, -1)
    bucket = _t5_bucket(jnp.clip(dist, 0, SWA_WINDOW - 1)).reshape(1, -1)
    onehot = ((bucket == jnp.arange(NUM_BUCKETS)[:, None]) & band).astype(F32)
    return onehot, jnp.where(band, 0.0, MASK_VALUE).astype(F32)


def _bias_table(rel_bias_t, onehot, maskrow, *, name):
    def body(rb_ref, oh_ref, mask_ref, o_ref):
        o_ref[...] = _dot(rb_ref[...], oh_ref[...], NN, HIGHEST) + mask_ref[...]

    return pl.pallas_call(body, name=name, out_shape=jax.ShapeDtypeStruct((SWA_HEADS, onehot.shape[1]), F32),
                          compiler_params=_params())(rel_bias_t, onehot, maskrow)


def _bias_grad(dbias2d, onehot, *, name):
    def body(db_ref, oh_ref, o_ref):
        o_ref[...] = _dot(db_ref[...], oh_ref[...], NT, HIGHEST)

    return pl.pallas_call(body, name=name, out_shape=jax.ShapeDtypeStruct((SWA_HEADS, NUM_BUCKETS), F32),
                          compiler_params=_params())(dbias2d, onehot)


GROUP_LANES = SWA_GROUP * SWA_BLOCK


def _swa_operands(zq_ref, kv_cur_ref, kv_prev_ref):
    q = (zq_ref[:, 0:1024] * (SWA_HEAD_DIM ** -0.5)).astype(BF16)
    kv_c = kv_cur_ref[...].astype(BF16)
    kv_p = kv_prev_ref[...].astype(BF16)
    kks = [jnp.concatenate([kv_p[:, g * 64:(g + 1) * 64], kv_c[:, g * 64:(g + 1) * 64]], axis=0) for g in range(SWA_KV_HEADS)]
    vvs = [jnp.concatenate([kv_p[:, 128 + g * 64:128 + (g + 1) * 64], kv_c[:, 128 + g * 64:128 + (g + 1) * 64]], axis=0)
           for g in range(SWA_KV_HEADS)]
    return q, kks, vvs


SWA_PART_HEADS = 8
SWA_PARTS = [(h0 // SWA_GROUP, h0) for h0 in range(0, SWA_HEADS, SWA_PART_HEADS)]


def _part_lanes(h0):
    return slice(h0 * SWA_BLOCK, (h0 + SWA_PART_HEADS) * SWA_BLOCK)


def _stack_heads(x, h0):
    return jnp.concatenate([x[:, h * SWA_HEAD_DIM:(h + 1) * SWA_HEAD_DIM] for h in range(h0, h0 + SWA_PART_HEADS)], axis=0)


def _heads_to_lanes(xt):
    pairs = []
    for j in range(0, xt.shape[1] // SWA_BLOCK, 2):
        two = jnp.concatenate([xt[:, j * SWA_BLOCK:(j + 1) * SWA_BLOCK], xt[:, (j + 1) * SWA_BLOCK:(j + 2) * SWA_BLOCK]], axis=0)
        pairs.append(two.T)
    return jnp.concatenate(pairs, axis=1)


def _swa_softmax(score_t, bias_ref, sink_ref, h0):
    sc = score_t + bias_ref[:, _part_lanes(h0)]
    sink = sink_ref[:, _part_lanes(h0)]
    m = jnp.maximum(jnp.max(sc, axis=0, keepdims=True), sink)
    e = jnp.exp(sc - m)
    e_sink = jnp.exp(sink - m)
    return e, 1.0 / (jnp.sum(e, axis=0, keepdims=True) + e_sink), e_sink


def _swa_tables(bias2d, sinks):
    bias_t = bias2d.reshape(SWA_HEADS, SWA_BLOCK, 2 * SWA_BLOCK).transpose(2, 0, 1).reshape(2 * SWA_BLOCK, SWA_HEADS * SWA_BLOCK)
    first = jnp.where(jnp.arange(2 * SWA_BLOCK)[:, None] < SWA_BLOCK, MASK_VALUE, bias_t)
    return jnp.stack([first, bias_t]), jnp.repeat(sinks, SWA_BLOCK, axis=1)


def _swa_fwd(zb, bias_tables, sink_lanes, *, name, exchanges=()):
    s = zb.shape[0]
    nb = s // SWA_BLOCK

    def body(zq_ref, kvc_ref, kvp_ref, bias_ref, sink_ref, o_ref):
        q, kks, vvs = _swa_operands(zq_ref, kvc_ref, kvp_ref)
        scores = [_dot(kks[g], _stack_heads(q, h0), NT) for g, h0 in SWA_PARTS]
        probs = []
        for score, (_, h0) in zip(scores, SWA_PARTS):
            e, inv, _ = _swa_softmax(score, bias_ref, sink_ref, h0)
            probs.append((e * inv).astype(BF16))
        outs = [_dot(vvs[g], p, TN) for p, (g, _) in zip(probs, SWA_PARTS)]
        o_ref[...] = jnp.concatenate([_heads_to_lanes(o) for o in outs], axis=1).astype(BF16)

    return _fused_call(
        body, name=name, grid=(nb,), out_shape=jax.ShapeDtypeStruct((s, D_MODEL), BF16),
        in_specs=[pl.BlockSpec((SWA_BLOCK, W_B), lambda n: (n, 0)),
                  pl.BlockSpec((SWA_BLOCK, 256), lambda n: (n, 4)),
                  pl.BlockSpec((SWA_BLOCK, 256), lambda n: (jnp.maximum(n - 1, 0), 4)),
                  pl.BlockSpec((None, 2 * SWA_BLOCK, SWA_HEADS * SWA_BLOCK), lambda n: (jnp.minimum(n, 1), 0, 0)),
                  _resident((1, SWA_HEADS * SWA_BLOCK))],
        out_specs=pl.BlockSpec((SWA_BLOCK, D_MODEL), lambda n: (n, 0)), scratch_shapes=[],
        operands=[zb, zb, zb, bias_tables, sink_lanes], exchanges=exchanges)


def _swa_bwd(zb, do_b, bias_tables, sink_lanes, *, name, exchanges=()):
    s = zb.shape[0]
    nb = s // SWA_BLOCK
    scale = SWA_HEAD_DIM ** -0.5

    def body(zq_ref, kvc_ref, kvp_ref, do_ref, bias_ref, sink_ref, dz_ref, dbias_ref, dsink_ref, carry, dsink_acc):
        step = pl.program_id(0)

        @pl.when(step == 0)
        def _():
            carry[...] = jnp.zeros_like(carry)
            dsink_acc[...] = jnp.zeros_like(dsink_acc)
            dbias_ref[...] = jnp.zeros_like(dbias_ref)

        q, kks, vvs = _swa_operands(zq_ref, kvc_ref, kvp_ref)
        do = do_ref[...].astype(BF16)
        parts = range(len(SWA_PARTS))
        q_rows = [_stack_heads(q, h0) for _, h0 in SWA_PARTS]
        do_rows = [_stack_heads(do, h0) for _, h0 in SWA_PARTS]
        scores = [_dot(kks[g], q_rows[i], NT) for i, (g, _) in enumerate(SWA_PARTS)]
        dps = [_dot(vvs[g], do_rows[i], NT) for i, (g, _) in enumerate(SWA_PARTS)]
        ps, dss = [], []
        for i, (_, h0) in enumerate(SWA_PARTS):
            e, inv, e_sink = _swa_softmax(scores[i], bias_ref, sink_ref, h0)
            p = e * inv
            delta = jnp.sum(p * dps[i], axis=0, keepdims=True)
            ds = p * (dps[i] - delta)
            dbias_ref[:, _part_lanes(h0)] += ds
            dsink_acc[:, _part_lanes(h0)] -= e_sink * inv * delta
            ps.append(p.astype(BF16))
            dss.append(ds.astype(BF16))
        dqs = [_dot(kks[g], dss[i], TN) * scale for i, (g, _) in enumerate(SWA_PARTS)]
        in_group = lambda xs, g, axis: jnp.concatenate([xs[i] for i in parts if SWA_PARTS[i][0] == g], axis=axis)
        dkks = [_dot(in_group(dss, g, 1), in_group(q_rows, g, 0), NN) for g in range(SWA_KV_HEADS)]
        dvvs = [_dot(in_group(ps, g, 1), in_group(do_rows, g, 0), NN) for g in range(SWA_KV_HEADS)]
        dkv = jnp.concatenate(dkks + dvvs, axis=1)
        dz_ref[:, 0:1024] = jnp.concatenate([_heads_to_lanes(dq) for dq in dqs], axis=1).astype(BF16)
        dz_ref[:, 1024:1280] = (dkv[SWA_BLOCK:, :] + carry[...]).astype(BF16)
        carry[...] = dkv[:SWA_BLOCK, :]

        @pl.when(step == nb - 1)
        def _():
            acc = dsink_acc[...]
            dsink_ref[...] = jnp.concatenate([jnp.sum(acc[:, h * SWA_BLOCK:(h + 1) * SWA_BLOCK], axis=1, keepdims=True)
                                              for h in range(SWA_HEADS)], axis=1)

    rev = lambda i: (nb - 1 - i, 0)
    table_shape = (2 * SWA_BLOCK, SWA_HEADS * SWA_BLOCK)
    return _fused_call(
        body, name=name, grid=(nb,),
        out_shape=(jax.ShapeDtypeStruct((s, W_B), BF16), jax.ShapeDtypeStruct(table_shape, F32), jax.ShapeDtypeStruct((1, SWA_HEADS), F32)),
        in_specs=[pl.BlockSpec((SWA_BLOCK, W_B), rev),
                  pl.BlockSpec((SWA_BLOCK, 256), lambda i: (nb - 1 - i, 4)),
                  pl.BlockSpec((SWA_BLOCK, 256), lambda i: (jnp.maximum(nb - 2 - i, 0), 4)),
                  pl.BlockSpec((SWA_BLOCK, D_MODEL), rev),
                  pl.BlockSpec((None,) + table_shape, lambda i: (jnp.minimum(nb - 1 - i, 1), 0, 0)),
                  _resident((1, SWA_HEADS * SWA_BLOCK))],
        out_specs=(pl.BlockSpec((SWA_BLOCK, W_B), rev), pl.BlockSpec(table_shape, lambda i: (0, 0)),
                   pl.BlockSpec((1, SWA_HEADS), lambda i: (0, 0))),
        scratch_shapes=[pltpu.VMEM((SWA_BLOCK, 256), F32), pltpu.VMEM((1, SWA_HEADS * SWA_BLOCK), F32)],
        operands=[zb, zb, zb, do_b, bias_tables, sink_lanes], exchanges=exchanges)


MEM_COLS = [slice(h * MEM_HEAD_DIM, (h + 1) * MEM_HEAD_DIM) for h in range(MEM_HEADS)]
MEM_VCOLS = [slice(D_MODEL + h * MEM_HEAD_DIM, D_MODEL + (h + 1) * MEM_HEAD_DIM) for h in range(MEM_HEADS)]


def _mem_probs(zc_ref, mkv_ref):
    qs = [(zc_ref[:, c] * (MEM_HEAD_DIM ** -0.5)).astype(BF16) for c in MEM_COLS]
    scores = [_dot(qs[h], mkv_ref[:, c], NT) for h, c in enumerate(MEM_COLS)]
    ps = []
    for sc in scores:
        e = jnp.exp(sc - jnp.max(sc, axis=-1, keepdims=True))
        ps.append(e / jnp.sum(e, axis=-1, keepdims=True))
    return qs, ps


def _mem_fwd(zc, mkv, *, name):
    s = zc.shape[0]
    t = min(512, s)

    def body(zc_ref, mkv_ref, o_ref):
        _, ps = _mem_probs(zc_ref, mkv_ref)
        ps = [p.astype(BF16) for p in ps]
        o_ref[...] = jnp.concatenate([_dot(ps[h], mkv_ref[:, vc], NN) for h, vc in enumerate(MEM_VCOLS)], axis=1).astype(BF16)

    return pl.pallas_call(
        body, name=name, grid=(s // t,), out_shape=jax.ShapeDtypeStruct((s, D_MODEL), BF16),
        in_specs=[pl.BlockSpec((t, D_MODEL), lambda i: (i, 0)), _resident((MEM_LEN, 2 * D_MODEL))],
        out_specs=pl.BlockSpec((t, D_MODEL), lambda i: (i, 0)), compiler_params=_params(("parallel",)),
    )(zc, mkv)


def _mem_bwd(zc, do_c, mkv, *, name):
    s = zc.shape[0]
    t = min(512, s)

    def body(zc_ref, do_ref, mkv_ref, dz_ref, dmkv_ref):
        @pl.when(pl.program_id(0) == 0)
        def _():
            dmkv_ref[...] = jnp.zeros_like(dmkv_ref)

        heads = range(MEM_HEADS)
        qs, ps = _mem_probs(zc_ref, mkv_ref)
        dos = [do_ref[:, c].astype(BF16) for c in MEM_COLS]
        dps = [_dot(dos[h], mkv_ref[:, MEM_VCOLS[h]], NT) for h in heads]
        dss = [(ps[h] * (dps[h] - jnp.sum(ps[h] * dps[h], axis=-1, keepdims=True))).astype(BF16) for h in heads]
        ps = [p.astype(BF16) for p in ps]
        dz_ref[...] = jnp.concatenate([_dot(dss[h], mkv_ref[:, MEM_COLS[h]], NN) * (MEM_HEAD_DIM ** -0.5) for h in heads], axis=1).astype(BF16)
        dmkv_ref[...] += jnp.concatenate([_dot(dss[h], qs[h], TN) for h in heads] + [_dot(ps[h], dos[h], TN) for h in heads], axis=1)

    return pl.pallas_call(
        body, name=name, grid=(s // t,),
        out_shape=(jax.ShapeDtypeStruct((s, D_MODEL), BF16), jax.ShapeDtypeStruct((MEM_LEN, 2 * D_MODEL), F32)),
        in_specs=[pl.BlockSpec((t, D_MODEL), lambda i: (i, 0)), pl.BlockSpec((t, D_MODEL), lambda i: (i, 0)),
                  _resident((MEM_LEN, 2 * D_MODEL))],
        out_specs=(pl.BlockSpec((t, D_MODEL), lambda i: (i, 0)), pl.BlockSpec((MEM_LEN, 2 * D_MODEL), lambda i: (0, 0))),
        compiler_params=_params(("arbitrary",)),
    )(zc, do_c, mkv)


def _normalize(pre):
    mu = jnp.mean(pre, axis=-1, keepdims=True)
    xc = pre - mu
    rstd = lax.rsqrt(jnp.mean(xc * xc, axis=-1, keepdims=True) + LN_EPS)
    return xc * rstd, rstd


def _layer_norm_bwd(dh, xhat, rstd, g):
    dxh = dh * g
    dpre = rstd * (dxh - jnp.mean(dxh, axis=-1, keepdims=True) - xhat * jnp.mean(dxh * xhat, axis=-1, keepdims=True))
    return dpre, jnp.sum(dh * xhat, axis=0, keepdims=True), jnp.sum(dh, axis=0, keepdims=True)


def _merge_fwd(o_a, o_b, o_c, zd, x, wbr, wo, *, name):
    s = x.shape[0]
    t = min(256, s)
    row = lambda w: pl.BlockSpec((t, w), lambda i: (i, 0))

    def body(oa_ref, ob_ref, oc_ref, zd_ref, x_ref, wbr_ref, wo_ref, xhat_ref, rstd_ref, merged_ref, pa_ref, pb_ref, pc_ref):
        merged = jnp.zeros((t, D_MODEL), F32)
        for b, (o_ref, p_ref) in enumerate(((oa_ref, pa_ref), (ob_ref, pb_ref), (oc_ref, pc_ref))):
            p = _dot(o_ref[...], wbr_ref[b], NN)
            p_ref[...] = p.astype(BF16)
            merged = merged + jax.nn.sigmoid(zd_ref[:, b * D_MODEL:(b + 1) * D_MODEL]) * p
        merged_b = merged.astype(BF16)
        merged_ref[...] = merged_b
        xhat, rstd = _normalize(ALPHA * x_ref[...] + _dot(merged_b, wo_ref[...], NN))
        xhat_ref[...] = xhat
        rstd_ref[...] = rstd

    act = jax.ShapeDtypeStruct((s, D_MODEL), F32)
    return pl.pallas_call(
        body, name=name, grid=(s // t,),
        out_shape=(act, jax.ShapeDtypeStruct((s, 1), F32)) + (jax.ShapeDtypeStruct((s, D_MODEL), BF16),) * 4,
        in_specs=[row(D_MODEL), row(D_MODEL), row(D_MODEL), row(W_D), row(D_MODEL),
                  _resident((3, D_MODEL, D_MODEL)), _resident((D_MODEL, D_MODEL))],
        out_specs=(row(D_MODEL), row(1), row(D_MODEL), row(D_MODEL), row(D_MODEL), row(D_MODEL)),
        compiler_params=_params(("parallel",)),
    )(o_a, o_b, o_c, zd, x, wbr, wo)


def _merge_bwd(dpre1, zd, pa, pb, pc, o_a, o_b, o_c, merged, wbr, wo, *, name, exchanges=()):
    s = dpre1.shape[0]
    t = min(256, s)
    nt = s // t
    row = lambda w: pl.BlockSpec((t, w), lambda i: (i, 0))

    def body(dpre_ref, zd_ref, pa_ref, pb_ref, pc_ref, oa_ref, ob_ref, oc_ref, mg_ref, wbr_ref, wo_ref,
             dzd_ref, doa_ref, dob_ref, doc_ref, gwa_ref, gwb_ref, gwc_ref, gwo_ref, acc):
        step = pl.program_id(0)

        @pl.when(step == 0)
        def _():
            acc[...] = jnp.zeros_like(acc)

        dpre_b = dpre_ref[...].astype(BF16)
        dmerged = _dot(dpre_b, wo_ref[...], NT)
        acc[3] += _dot(mg_ref[...], dpre_b, TN)
        branches = ((pa_ref, oa_ref, doa_ref), (pb_ref, ob_ref, dob_ref), (pc_ref, oc_ref, doc_ref))
        for b, (p_ref, o_ref, do_ref) in enumerate(branches):
            gate = jax.nn.sigmoid(zd_ref[:, b * D_MODEL:(b + 1) * D_MODEL])
            dzd_ref[:, b * D_MODEL:(b + 1) * D_MODEL] = (dmerged * p_ref[...] * gate * (1.0 - gate)).astype(BF16)
            dp = (dmerged * gate).astype(BF16)
            acc[b] += _dot(o_ref[...], dp, TN)
            do_ref[...] = _dot(dp, wbr_ref[b], NT).astype(do_ref.dtype)

        @pl.when(step == nt - 1)
        def _():
            for b, gw_ref in enumerate((gwa_ref, gwb_ref, gwc_ref, gwo_ref)):
                pltpu.sync_copy(acc.at[b], gw_ref)

    act = jax.ShapeDtypeStruct((s, D_MODEL), F32)
    actb = jax.ShapeDtypeStruct((s, D_MODEL), BF16)
    gw = jax.ShapeDtypeStruct((D_MODEL, D_MODEL), F32)
    return _fused_call(
        body, name=name, grid=(nt,),
        out_shape=(jax.ShapeDtypeStruct((s, W_D), BF16), act, actb, actb, gw, gw, gw, gw),
        in_specs=[row(D_MODEL), row(W_D)] + [row(D_MODEL)] * 7 + [_resident((3, D_MODEL, D_MODEL)), _resident((D_MODEL, D_MODEL))],
        out_specs=(row(W_D),) + (row(D_MODEL),) * 3 + (HBM,) * 4, scratch_shapes=[pltpu.VMEM((4, D_MODEL, D_MODEL), F32)],
        operands=[dpre1, zd, pa, pb, pc, o_a, o_b, o_c, merged, wbr, wo], exchanges=exchanges)


def _mlp_loss(xhat1, rstd1, target, ln1_g, ln1_b, ln2_g, ln2_b, wu, wd, *, name):
    s = xhat1.shape[0]
    t = min(256, s)
    npan = wu.shape[0]
    row = lambda w: pl.BlockSpec((t, w), lambda i: (i, 0))
    vec = _resident((1, D_MODEL))

    def body(xhat_ref, rstd_ref, tgt_ref, g1_ref, b1_ref, g2_ref, b2_ref, wu_ref, wd_ref,
             dpre1_ref, dpre2_ref, h1_ref, a_ref, du_ref, stats_ref):
        @pl.when(pl.program_id(0) == 0)
        def _():
            stats_ref[...] = jnp.zeros_like(stats_ref)

        xhat1_v = xhat_ref[...]
        h1 = xhat1_v * g1_ref[...] + b1_ref[...]
        h1_b = h1.astype(BF16)
        h1_ref[...] = h1_b
        us = []
        ff = jnp.zeros((t, D_MODEL), F32)
        for j in range(npan):
            u = _dot(h1_b, wu_ref[j], NN)
            us.append(u)
            r = jnp.maximum(u, 0.0)
            a_b = (r * r).astype(BF16)
            a_ref[:, j * D_MODEL:(j + 1) * D_MODEL] = a_b
            ff = ff + _dot(a_b, wd_ref[j], NN)
        xhat2, rstd2 = _normalize(ALPHA * h1 + ff)
        err = xhat2 * g2_ref[...] + b2_ref[...] - tgt_ref[...]
        stats_ref[4:5, :] += jnp.sum(err * err, axis=0, keepdims=True)
        dpre2, dg2, db2 = _layer_norm_bwd(err * (1.0 / D_MODEL), xhat2, rstd2, g2_ref[...])
        stats_ref[0:1, :] += dg2
        stats_ref[1:2, :] += db2
        dpre2_b = dpre2.astype(BF16)
        dpre2_ref[...] = dpre2_b
        dh1 = ALPHA * dpre2
        for j in range(npan):
            du_b = (_dot(dpre2_b, wd_ref[j], NT) * (2.0 * jnp.maximum(us[j], 0.0))).astype(BF16)
            du_ref[:, j * D_MODEL:(j + 1) * D_MODEL] = du_b
            dh1 = dh1 + _dot(du_b, wu_ref[j], NT)
        dpre1, dg1, db1 = _layer_norm_bwd(dh1, xhat1_v, rstd_ref[...], g1_ref[...])
        stats_ref[2:3, :] += dg1
        stats_ref[3:4, :] += db1
        dpre1_ref[...] = dpre1

    actb = jax.ShapeDtypeStruct((s, D_MODEL), BF16)
    wide = jax.ShapeDtypeStruct((s, D_FF), BF16)
    return pl.pallas_call(
        body, name=name, grid=(s // t,),
        out_shape=(jax.ShapeDtypeStruct((s, D_MODEL), F32), actb, actb, wide, wide, jax.ShapeDtypeStruct((8, D_MODEL), F32)),
        in_specs=[row(D_MODEL), row(1), row(D_MODEL), vec, vec, vec, vec,
                  _resident((npan, D_MODEL, D_MODEL)), _resident((npan, D_MODEL, D_MODEL))],
        out_specs=(row(D_MODEL), row(D_MODEL), row(D_MODEL), row(D_FF), row(D_FF), pl.BlockSpec((8, D_MODEL), lambda i: (0, 0))),
        compiler_params=_params(("arbitrary",)),
    )(xhat1, rstd1, target, ln1_g, ln1_b, ln2_g, ln2_b, wu, wd)


BRANCH_WEIGHTS = ("w_branch_hg", "w_branch_swa", "w_branch_mem")


def _local_step(x, mem, target, wi_t, wmkv, late, lb_logits, gain, sinks, rel_bias, ln1_g, ln1_b, ln2_g, ln2_b, *, distributed):
    s = x.shape[0]
    tm = min(1024, s)
    tk = min(2048, s)
    xb = x.astype(BF16)
    memb = mem.astype(BF16)
    if distributed:
        cx, cy, cc = lax.axis_index("x"), lax.axis_index("y"), lax.axis_index("c")
        pos = jnp.stack([2 * cx + cy, cc]).astype(jnp.int32)
    gather = (lambda names: [_gather_exchange([late[k] for k in names])]) if distributed else (lambda names: [])
    to_sibling = (lambda grads: [_sibling_halves_exchange(grads)]) if distributed else (lambda grads: [])
    to_chips = (lambda sums: [_chip_partials_exchange([bf for bf, _ in sums])]) if distributed else (lambda sums: [])

    def chip_sums(names, grads, from_sibling):
        return [_add_sibling(g, o, pos, name="add_sibling_" + k) for k, g, o in zip(names, grads, from_sibling)]

    def shard_sums(names, sums, from_chips):
        return {k: _add_chips(mine, o, pos, name="add_chips_" + k) for k, (_, mine), o in zip(names, sums, from_chips)}

    za = _mm(xb, wi_t, mode="nt", tm=min(512, s), tn=W_A, tk=D_MODEL, name="proj_a", b_rows=(0, W_A))
    zb = _mm(xb, wi_t, mode="nt", tm=tm, tn=W_B, tk=D_MODEL, name="proj_b", out_dtype=BF16, b_rows=(W_A, W_B))
    zc = _mm(xb, wi_t, mode="nt", tm=tm, tn=W_C, tk=D_MODEL, name="proj_c", out_dtype=BF16, b_rows=(W_A + W_B, W_C))
    zd = _mm(xb, wi_t, mode="nt", tm=min(512, s), tn=W_D, tk=D_MODEL, name="proj_d", b_rows=(W_A + W_B + W_C, W_D))
    mkv = _mm(memb, wmkv, mode="nn", tm=MEM_LEN, tn=512, tk=D_MODEL, name="mem_kv", out_dtype=BF16, b_panels=True)
    onehot, maskrow = _bias_selector()
    bias_tables, sink_lanes = _swa_tables(_bias_table(rel_bias.T, onehot, maskrow, name="bias_table"), sinks)
    (o_a, o_raw, states), landed = _hgrn_fwd(za, lb_logits, gain, name="hgrn_fwd", exchanges=gather(("w_up", "w_down")))
    wu, wd = landed[0] if distributed else (late["wu"], late["wd"])
    o_b, landed = _swa_fwd(zb, bias_tables, sink_lanes, name="swa_fwd", exchanges=gather(BRANCH_WEIGHTS + ("w_out",)))
    if distributed:
        wbr = jnp.stack([wb.reshape(D_MODEL, D_MODEL) for wb in landed[0][:3]])
        wo = landed[0][3].reshape(D_MODEL, D_MODEL)
    else:
        wbr, wo = late["wbr"], late["wo"]
    o_c = _mem_fwd(zc, mkv, name="mem_fwd")
    xhat1, rstd1, merged, pa, pb, pc = _merge_fwd(o_a, o_b, o_c, zd, x, wbr, wo, name="merge_fwd")

    dpre1, dpre2, h1, act, du, ln_stats = _mlp_loss(xhat1, rstd1, target, ln1_g, ln1_b, ln2_g, ln2_b, wu, wd, name="mlp_loss")
    ffn = ("w_down", "w_up")
    g_ffn = [_mm(act, dpre2, mode="tn", tm=1024, tn=D_MODEL, tk=tk, name="grad_w_down").reshape(N_SHARDS, D_FF // N_SHARDS, D_MODEL),
             _mm(h1, du, mode="tn", tm=D_MODEL, tn=1024, tk=tk, name="grad_w_up", out_panels=True)]

    (dzd, do_a, do_b, do_c, *g_merge), landed = _merge_bwd(dpre1, zd, pa, pb, pc, o_a, o_b, o_c, merged, wbr, wo, name="merge_bwd",
                                                           exchanges=to_sibling(g_ffn))
    sums_ffn = chip_sums(ffn, g_ffn, landed[0]) if distributed else []
    merge = BRANCH_WEIGHTS + ("w_out",)
    g_merge = [g.reshape(N_SHARDS, D_MODEL // N_SHARDS, D_MODEL) for g in g_merge]
    (dza, hg_stats), landed = _hgrn_bwd(za, o_raw, do_a, states, lb_logits, gain, name="hgrn_bwd",
                                        exchanges=to_chips(sums_ffn) + to_sibling(g_merge))
    halves = shard_sums(ffn, sums_ffn, landed[0]) if distributed else {}
    sums_merge = chip_sums(merge, g_merge, landed[1]) if distributed else []
    (dzb, dbias_t, dsinks), landed = _swa_bwd(zb, do_b, bias_tables, sink_lanes, name="swa_bwd", exchanges=to_chips(sums_merge))
    if distributed:
        halves.update(shard_sums(merge, sums_merge, landed[0]))
    dbias = dbias_t.reshape(2 * SWA_BLOCK, SWA_HEADS, SWA_BLOCK).transpose(1, 2, 0).reshape(SWA_HEADS, -1)
    d_rel_bias = _bias_grad(dbias, onehot, name="bias_grad").T
    dzc, dmkv = _mem_bwd(zc, do_c, mkv, name="mem_bwd")

    proj = ("w_in", "w_mem_kv")
    g_wi, offset = None, 0
    for dz, nm in ((dza, "grad_w_in_a"), (dzb, "grad_w_in_b"), (dzc, "grad_w_in_c"), (dzd, "grad_w_in_d")):
        g_wi = _mm(dz, xb, mode="tn", tm=dz.shape[1] if dz.shape[1] <= 1280 else 1024, tn=D_MODEL, tk=tk, name=nm,
                   rows_of=IN_COLS, row_offset=offset, into=g_wi)
        offset += dz.shape[1]
    g_proj = [g_wi.reshape(N_SHARDS, IN_COLS // N_SHARDS, D_MODEL),
              _mm(memb, dmkv, mode="tn", tm=D_MODEL, tn=512, tk=MEM_LEN, name="grad_w_mem_kv", out_panels=True)]
    sums_proj = chip_sums(proj, g_proj, _run_exchanges(to_sibling(g_proj), name="reduce_sibling_proj")[0]) if distributed else []
    grad_x, landed = _dx_matmul([dza, dzb, dzc, dzd], wi_t, dpre1, tm=min(512, s), name="grad_x",
                                exchanges=to_chips(sums_proj))
    if distributed:
        halves.update(shard_sums(proj, sums_proj, landed[0]))
    else:
        halves = dict(zip(ffn + merge + proj, g_ffn + g_merge + g_proj))
    small = dict(lb_logits=hg_stats[1:3], hg_norm_gain=hg_stats[0:1], swa_sinks=dsinks, rel_bias=d_rel_bias,
                 ln1_g=ln_stats[2:3], ln1_b=ln_stats[3:4], ln2_g=ln_stats[0:1], ln2_b=ln_stats[1:2], sq_err=ln_stats[4:5])
    return grad_x, halves, small


def _mesh_position():
    x, y, c = lax.axis_index("x"), lax.axis_index("y"), lax.axis_index("c")
    chips = [(1 - x, y), (x, 1 - y), (1 - x, 1 - y)]
    return x, y, c, chips


class _Exchange(NamedTuple):
    operands: list
    out_shapes: list
    n_sems: int
    start: Callable
    finish: Callable


def _gather_exchange(shards):
    n = len(shards)
    per = 7

    def plan(ins, outs, send_sems, recv_sems):
        x, y, c, chips = _mesh_position()
        me = 2 * x + y
        sibling = (x, y, 1 - c)

        def half(a, slot, hc):
            rh = shards[a].shape[0] // 2
            return outs[a].at[slot, pl.ds(hc * rh, rh), :]

        def copy(a, k, src, dst, to):
            return pltpu.make_async_remote_copy(src_ref=src, dst_ref=dst, send_sem=send_sems.at[a * per + k], recv_sem=recv_sems.at[a * per + k],
                                                device_id=to, device_id_type=MESH)

        own = [copy(a, 6, ins[a], outs[a].at[me], sibling) for a in range(n)]
        to_chips = [copy(a, k, ins[a].at[pl.ds(c * (shards[a].shape[0] // 2), shards[a].shape[0] // 2), :], half(a, me, c), (cx, cy, c))
                    for k, (cx, cy) in enumerate(chips) for a in range(n)]
        arrived = [copy(a, k, half(a, 2 * cx + cy, c), half(a, 2 * cx + cy, c), (cx, cy, c)) for k, (cx, cy) in enumerate(chips) for a in range(n)]
        passed_on = [copy(a, 3 + k, half(a, 2 * cx + cy, c), half(a, 2 * cx + cy, c), sibling) for k, (cx, cy) in enumerate(chips) for a in range(n)]
        from_sibling = [copy(a, 3 + k, half(a, 2 * cx + cy, 1 - c), half(a, 2 * cx + cy, 1 - c), sibling)
                        for k, (cx, cy) in enumerate(chips) for a in range(n)]
        own_arrived = [copy(a, 6, outs[a].at[me], outs[a].at[me], sibling) for a in range(n)]
        return own, to_chips, arrived, passed_on, from_sibling, own_arrived

    def start(*refs):
        own, to_chips, _, _, _, _ = plan(*refs)
        for cp in own + to_chips:
            cp.start()

    def finish(*refs):
        own, to_chips, arrived, passed_on, from_sibling, own_arrived = plan(*refs)
        for landed, onward in zip(arrived, passed_on):
            landed.wait_recv()
            onward.start()
        for cp in from_sibling + own_arrived:
            cp.wait_recv()
        for cp in own + to_chips + passed_on:
            cp.wait_send()

    return _Exchange(list(shards), [jax.ShapeDtypeStruct((N_SHARDS,) + w.shape, w.dtype) for w in shards], per * n, start, finish)


def _sibling_halves_exchange(grads):
    n = len(grads)

    def plan(ins, outs, send_sems, recv_sems):
        x, y, c, _ = _mesh_position()
        return [pltpu.make_async_remote_copy(src_ref=ins[a].at[:, pl.ds((1 - c) * (grads[a].shape[1] // 2), grads[a].shape[1] // 2), :],
                                             dst_ref=outs[a], send_sem=send_sems.at[a], recv_sem=recv_sems.at[a],
                                             device_id=(x, y, 1 - c), device_id_type=MESH) for a in range(n)]

    def start(*refs):
        for cp in plan(*refs):
            cp.start()

    def finish(*refs):
        for cp in plan(*refs):
            cp.wait()

    return _Exchange(list(grads), [jax.ShapeDtypeStruct((g.shape[0], g.shape[1] // 2, g.shape[2]), g.dtype) for g in grads], n, start, finish)


def _chip_partials_exchange(sums):
    n = len(sums)

    def plan(ins, outs, send_sems, recv_sems):
        _, _, c, chips = _mesh_position()
        return [pltpu.make_async_remote_copy(src_ref=ins[a].at[2 * cx + cy], dst_ref=outs[a].at[k], send_sem=send_sems.at[a * 3 + k],
                                             recv_sem=recv_sems.at[a * 3 + k], device_id=(cx, cy, c), device_id_type=MESH)
                for k, (cx, cy) in enumerate(chips) for a in range(n)]

    def start(*refs):
        for cp in plan(*refs):
            cp.start()

    def finish(*refs):
        for cp in plan(*refs):
            cp.wait()

    return _Exchange(list(sums), [jax.ShapeDtypeStruct((3,) + g.shape[1:], g.dtype) for g in sums], 3 * n, start, finish)


def _fused_call(body, *, name, grid, in_specs, out_specs, out_shape, scratch_shapes, operands, exchanges=()):
    single = not isinstance(out_shape, (tuple, list))
    out_specs = [out_specs] if single else list(out_specs)
    out_shape = [out_shape] if single else list(out_shape)
    n_in, n_out, n_scr = len(in_specs), len(out_specs), len(scratch_shapes)
    x_in = [len(e.operands) for e in exchanges]
    x_out = [len(e.out_shapes) for e in exchanges]

    def wrapped(*refs):
        refs = list(refs)
        ins = refs[:n_in]
        pos = n_in
        ex_ins = []
        for k in x_in:
            ex_ins.append(refs[pos:pos + k])
            pos += k
        outs = refs[pos:pos + n_out]
        pos += n_out
        ex_outs = []
        for k in x_out:
            ex_outs.append(refs[pos:pos + k])
            pos += k
        scratch = refs[pos:pos + n_scr]
        sems = refs[pos + n_scr:]
        first, last = None, None
        for axis, size in enumerate(grid):
            at_start, at_end = pl.program_id(axis) == 0, pl.program_id(axis) == size - 1
            first = at_start if first is None else first & at_start
            last = at_end if last is None else last & at_end

        @pl.when(first)
        def _():
            for i, e in enumerate(exchanges):
                e.start(ex_ins[i], ex_outs[i], sems[2 * i], sems[2 * i + 1])

        body(*ins, *outs, *scratch)

        @pl.when(last)
        def _():
            for i, e in enumerate(exchanges):
                e.finish(ex_ins[i], ex_outs[i], sems[2 * i], sems[2 * i + 1])

    n_x_in, n_x_out = sum(x_in), sum(x_out)
    results = pl.pallas_call(
        wrapped if exchanges else body, name=name, grid=grid,
        in_specs=list(in_specs) + [HBM] * n_x_in,
        out_specs=out_specs + [HBM] * n_x_out,
        out_shape=out_shape + [s for e in exchanges for s in e.out_shapes],
        scratch_shapes=list(scratch_shapes) + [pltpu.SemaphoreType.DMA((e.n_sems,)) for e in exchanges for _ in range(2)],
        compiler_params=_params(("arbitrary",) * len(grid)),
    )(*operands, *[a for e in exchanges for a in e.operands])
    own = results[0] if single else tuple(results[:n_out])
    landed, pos = [], n_out
    for k in x_out:
        landed.append(list(results[pos:pos + k]))
        pos += k
    return own, landed


def _run_exchanges(exchanges, *, name):
    def body(*refs):
        n_in = sum(len(e.operands) for e in exchanges)
        n_out = sum(len(e.out_shapes) for e in exchanges)
        ins, outs, sems = refs[:n_in], refs[n_in:n_in + n_out], refs[n_in + n_out:]
        spans, i, o = [], 0, 0
        for e in exchanges:
            spans.append((ins[i:i + len(e.operands)], outs[o:o + len(e.out_shapes)]))
            i, o = i + len(e.operands), o + len(e.out_shapes)
        for k, e in enumerate(exchanges):
            e.start(*spans[k], sems[2 * k], sems[2 * k + 1])
        for k, e in enumerate(exchanges):
            e.finish(*spans[k], sems[2 * k], sems[2 * k + 1])

    operands = [a for e in exchanges for a in e.operands]
    shapes = [s for e in exchanges for s in e.out_shapes]
    results = pl.pallas_call(
        body, name=name, out_shape=shapes, in_specs=[HBM] * len(operands), out_specs=[HBM] * len(shapes),
        scratch_shapes=[pltpu.SemaphoreType.DMA((e.n_sems,)) for e in exchanges for _ in range(2)],
    )(*operands)
    landed, pos = [], 0
    for e in exchanges:
        landed.append(list(results[pos:pos + len(e.out_shapes)]))
        pos += len(e.out_shapes)
    return landed


ROW_TILE_MAX = 640
BF16_SUBLANES = 16


def _row_tile(rows):
    for tr in range(min(rows, ROW_TILE_MAX), 0, -1):
        if rows % tr == 0 and tr % BF16_SUBLANES == 0:
            return tr
    raise ValueError(rows)


def _add_sibling(grad, other, pos, *, name):
    p, r, cols = grad.shape
    rh = r // 2
    tr = _row_tile(rh)
    nb = rh // tr

    def body(pos_ref, g_ref, o_ref, sb_ref, mine_ref):
        total = g_ref[...] + o_ref[...]
        sb_ref[...] = total.astype(BF16)

        @pl.when(pl.program_id(1) == pos_ref[0])
        def _():
            mine_ref[...] = total

    return pl.pallas_call(
        body, name=name, out_shape=(jax.ShapeDtypeStruct((p, rh, cols), BF16), jax.ShapeDtypeStruct((rh, cols), F32)),
        grid_spec=pltpu.PrefetchScalarGridSpec(
            num_scalar_prefetch=1, grid=(nb, p),
            in_specs=[pl.BlockSpec((None, tr, cols), lambda i, j, pos_ref: (j, pos_ref[1] * nb + i, 0)),
                      pl.BlockSpec((None, tr, cols), lambda i, j, pos_ref: (j, i, 0))],
            out_specs=(pl.BlockSpec((None, tr, cols), lambda i, j, pos_ref: (j, i, 0)),
                       pl.BlockSpec((tr, cols), lambda i, j, pos_ref: (i, 0)))),
        compiler_params=_params(("parallel", "arbitrary")),
    )(pos, grad, other)


def _add_chips(mine, others, pos, *, name):
    rh, cols = mine.shape
    tr = _row_tile(rh)
    nb = rh // tr

    def body(pos_ref, s_ref, o_ref, r_ref):
        r_ref[...] = ((s_ref[...] + o_ref[0].astype(F32)) + o_ref[1].astype(F32)) + o_ref[2].astype(F32)

    return pl.pallas_call(
        body, name=name, out_shape=jax.ShapeDtypeStruct((2 * rh, cols), F32),
        grid_spec=pltpu.PrefetchScalarGridSpec(
            num_scalar_prefetch=1, grid=(nb,),
            in_specs=[pl.BlockSpec((tr, cols), lambda i, pos_ref: (i, 0)),
                      pl.BlockSpec((3, tr, cols), lambda i, pos_ref: (0, i, 0))],
            out_specs=pl.BlockSpec((tr, cols), lambda i, pos_ref: (pos_ref[1] * nb + i, 0))),
        compiler_params=_params(("parallel",)),
    )(pos, mine, others)


def _join_halves(bufs, *, name):
    n = len(bufs)

    def body(*refs):
        ins, outs = refs[:n], refs[n:2 * n]
        send_sems, recv_sems = refs[2 * n:]
        x, y, c, _ = _mesh_position()

        def copy(a, hc):
            rh = bufs[a].shape[0] // 2
            rows = pl.ds(hc * rh, rh)
            return pltpu.make_async_remote_copy(src_ref=ins[a].at[rows, :], dst_ref=outs[a].at[rows, :], send_sem=send_sems.at[a],
                                                recv_sem=recv_sems.at[a], device_id=(x, y, 1 - c), device_id_type=MESH)

        for a in range(n):
            copy(a, c).start()
        for a in range(n):
            copy(a, c).wait_send()
            copy(a, 1 - c).wait_recv()

    return pl.pallas_call(
        body, name=name, out_shape=[jax.ShapeDtypeStruct(b.shape, b.dtype) for b in bufs],
        in_specs=[HBM] * n, out_specs=[HBM] * n, input_output_aliases={a: a for a in range(n)},
        scratch_shapes=[pltpu.SemaphoreType.DMA((n,)), pltpu.SemaphoreType.DMA((n,))],
    )(*bufs)


SMALL = ["lb_logits", "hg_norm_gain", "swa_sinks", "rel_bias", "ln1_g", "ln1_b", "ln2_g", "ln2_b"]
PACK_ROWS = 48
PACK_AT = dict(lb_logits=(slice(0, 2), slice(0, D_MODEL)), hg_norm_gain=(slice(2, 3), slice(0, D_MODEL)), ln1_g=(slice(3, 4), slice(0, D_MODEL)),
               ln1_b=(slice(4, 5), slice(0, D_MODEL)), ln2_g=(slice(5, 6), slice(0, D_MODEL)), ln2_b=(slice(6, 7), slice(0, D_MODEL)),
               swa_sinks=(slice(7, 8), slice(0, SWA_HEADS)), sq_err=(slice(8, 9), slice(0, D_MODEL)),
               rel_bias=(slice(16, 16 + NUM_BUCKETS), slice(0, SWA_HEADS)))


def _reduce_small(grads, *, name):
    names = SMALL + ["sq_err"]

    def body(*refs):
        g_refs = dict(zip(names, refs[:len(names)]))
        total_ref, packed, gathered, send_sems, recv_sems = refs[len(names):]
        x, y, c, _ = _mesh_position()
        me = 4 * x + 2 * y + c
        packed[...] = jnp.zeros_like(packed)
        for k, g_ref in g_refs.items():
            packed[PACK_AT[k]] = g_ref[...]
        gathered[me] = packed[...]
        copies = []
        for d in range(1, 8):
            dx, dy, dc = (d >> 2) & 1, (d >> 1) & 1, d & 1
            cp = pltpu.make_async_remote_copy(src_ref=packed, dst_ref=gathered.at[me], send_sem=send_sems.at[d - 1], recv_sem=recv_sems.at[d - 1],
                                              device_id=(x ^ dx, y ^ dy, c ^ dc), device_id_type=MESH)
            cp.start()
            copies.append(cp)
        for cp in copies:
            cp.wait()
        total = gathered[0]
        for j in range(1, 8):
            total = total + gathered[j]
        total_ref[...] = total

    vm = pl.BlockSpec(memory_space=pltpu.VMEM)
    return pl.pallas_call(
        body, name=name, out_shape=jax.ShapeDtypeStruct((PACK_ROWS, D_MODEL), F32), in_specs=[vm] * len(names), out_specs=vm,
        scratch_shapes=[pltpu.VMEM((PACK_ROWS, D_MODEL), F32), pltpu.VMEM((8, PACK_ROWS, D_MODEL), F32),
                        pltpu.SemaphoreType.DMA((7,)), pltpu.SemaphoreType.DMA((7,))],
    )(*[grads[k] for k in names])


def _adamw_small(total, w, m, v, *, name):
    names = SMALL
    n = len(names)

    def body(*refs):
        total_ref = refs[0]
        w_refs, m_refs, v_refs = (dict(zip(names, refs[1 + i * n:1 + (i + 1) * n])) for i in range(3))
        loss_ref = refs[1 + 3 * n]
        go_refs, d_refs, nm_refs, nv_refs = (dict(zip(names, refs[2 + (3 + i) * n:2 + (4 + i) * n])) for i in range(4))
        loss_ref[...] = (0.5 / D_MODEL) * jnp.sum(total_ref[PACK_AT["sq_err"]], axis=1, keepdims=True)
        for k in names:
            g = total_ref[PACK_AT[k]]
            go_refs[k][...] = g
            d_refs[k][...], nm_refs[k][...], nv_refs[k][...] = _adamw_math(w_refs[k][...], g, m_refs[k][...], v_refs[k][...])

    like = [jax.ShapeDtypeStruct(w[k].shape, F32) for k in names]
    results = pl.pallas_call(body, name=name, out_shape=[jax.ShapeDtypeStruct((1, 1), F32)] + like * 4,
                             compiler_params=_params())(total, *[d[k] for d in (w, m, v) for k in names])
    return results[0], {k: tuple(results[1 + i * n + j] for i in range(4)) for j, k in enumerate(names)}


def _adamw_math(w, g, m, v):
    m = ADAM_B1 * m + (1.0 - ADAM_B1) * g
    v = ADAM_B2 * v + (1.0 - ADAM_B2) * (g * g)
    m_hat = m / (1.0 - ADAM_B1 ** ADAM_STEP)
    v_hat = v / (1.0 - ADAM_B2 ** ADAM_STEP)
    delta = -ADAM_LR * (m_hat / (jnp.sqrt(v_hat) + ADAM_EPS) + ADAM_WD * w)
    return delta, m, v


def _adamw(w, g, m, v, *, name):
    _, rows, cols = w.shape
    tr = _row_tile(rows)
    blk = pl.BlockSpec((None, tr, cols), lambda i: (0, i, 0))
    flat = pl.BlockSpec((tr, cols), lambda i: (i, 0))

    def body(w_ref, g_ref, m_ref, v_ref, go_ref, d_ref, nm_ref, nv_ref):
        g_v = g_ref[...]
        go_ref[...] = g_v
        d_ref[...], nm_ref[...], nv_ref[...] = _adamw_math(w_ref[...], g_v, m_ref[...], v_ref[...])

    shape = jax.ShapeDtypeStruct((1, rows, cols), F32)
    return pl.pallas_call(body, name=name, grid=(rows // tr,), out_shape=(shape,) * 4, in_specs=[blk, flat, blk, blk], out_specs=(blk,) * 4,
                          compiler_params=_params(("parallel",)))(w, g, m, v)


WEIGHTS = ["w_in", "lb_logits", "hg_norm_gain", "swa_sinks", "rel_bias", "w_mem_kv", "w_branch_hg", "w_branch_swa", "w_branch_mem",
           "w_out", "ln1_g", "ln1_b", "w_up", "w_down", "ln2_g", "ln2_b"]
BIG = ["w_in", "w_mem_kv", "w_branch_hg", "w_branch_swa", "w_branch_mem", "w_out", "w_up", "w_down"]


def kernel(x, mem, w_in, lb_logits, hg_norm_gain, swa_sinks, rel_bias, w_mem_kv, w_branch_hg, w_branch_swa, w_branch_mem, w_out, ln1_g, ln1_b, w_up, w_down, ln2_g, ln2_b, loss_target, m_w_in, m_lb_logits, m_hg_norm_gain, m_swa_sinks, m_rel_bias, m_w_mem_kv, m_w_branch_hg, m_w_branch_swa, m_w_branch_mem, m_w_out, m_ln1_g, m_ln1_b, m_w_up, m_w_down, m_ln2_g, m_ln2_b, v_w_in, v_lb_logits, v_hg_norm_gain, v_swa_sinks, v_rel_bias, v_w_mem_kv, v_w_branch_hg, v_w_branch_swa, v_w_branch_mem, v_w_out, v_ln1_g, v_ln1_b, v_w_up, v_w_down, v_ln2_g, v_ln2_b):
    w = dict(w_in=w_in, lb_logits=lb_logits, hg_norm_gain=hg_norm_gain, swa_sinks=swa_sinks, rel_bias=rel_bias, w_mem_kv=w_mem_kv,
             w_branch_hg=w_branch_hg, w_branch_swa=w_branch_swa, w_branch_mem=w_branch_mem, w_out=w_out, ln1_g=ln1_g, ln1_b=ln1_b,
             w_up=w_up, w_down=w_down, ln2_g=ln2_g, ln2_b=ln2_b)
    m = dict(w_in=m_w_in, lb_logits=m_lb_logits, hg_norm_gain=m_hg_norm_gain, swa_sinks=m_swa_sinks, rel_bias=m_rel_bias, w_mem_kv=m_w_mem_kv,
             w_branch_hg=m_w_branch_hg, w_branch_swa=m_w_branch_swa, w_branch_mem=m_w_branch_mem, w_out=m_w_out, ln1_g=m_ln1_g, ln1_b=m_ln1_b,
             w_up=m_w_up, w_down=m_w_down, ln2_g=m_ln2_g, ln2_b=m_ln2_b)
    v = dict(w_in=v_w_in, lb_logits=v_lb_logits, hg_norm_gain=v_hg_norm_gain, swa_sinks=v_swa_sinks, rel_bias=v_rel_bias, w_mem_kv=v_w_mem_kv,
             w_branch_hg=v_w_branch_hg, w_branch_swa=v_w_branch_swa, w_branch_mem=v_w_branch_mem, w_out=v_w_out, ln1_g=v_ln1_g, ln1_b=v_ln1_b,
             w_up=v_w_up, w_down=v_w_down, ln2_g=v_ln2_g, ln2_b=v_ln2_b)
    shapes = {k: w[k].shape for k in WEIGHTS}
    for d in (w, m, v):
        d["w_in"] = d["w_in"].reshape(D_MODEL, IN_COLS // N_SHARDS).T[None]
    shards = {k: w[k].reshape(w[k].shape[-2], w[k].shape[-1]).astype(BF16) for k in BIG}
    wi4, wmkv = _run_exchanges([_gather_exchange([shards["w_in"], shards["w_mem_kv"]])], name="gather_weights")[0]
    wi_t = wi4.reshape(IN_COLS, D_MODEL)

    grad_x, halves, small = _local_step(
        x.reshape(x.shape[-2], D_MODEL), mem.reshape(MEM_LEN, D_MODEL), loss_target.reshape(loss_target.shape[-2], D_MODEL),
        wi_t, wmkv, shards, lb_logits, hg_norm_gain, swa_sinks, rel_bias, ln1_g, ln1_b, ln2_g, ln2_b, distributed=True)

    reduced = dict(zip(BIG, _join_halves([halves[k] for k in BIG], name="join_halves")))

    outs = {k: _adamw(w[k], reduced[k], m[k], v[k], name="adamw_" + k) for k in BIG}
    loss, small_outs = _adamw_small(_reduce_small(small, name="reduce_small"), w, m, v, name="adamw_small")
    outs.update(small_outs)
    grad_out, delta_out, m_out, v_out = ({k: outs[k][i] for k in WEIGHTS} for i in range(4))
    for out in (grad_out, delta_out, m_out, v_out):
        out["w_in"] = out["w_in"][0].T

    result = [loss.reshape(()), grad_x.reshape(x.shape)]
    for out in (grad_out, delta_out, m_out, v_out):
        result += [out[k].reshape(shapes[k]) for k in WEIGHTS]
    return tuple(result)
```

```python
import math
from typing import Callable, NamedTuple

import jax
import jax.numpy as jnp
from jax import lax
from jax.experimental import pallas as pl
from jax.experimental.pallas import tpu as pltpu

F32 = jnp.float32
BF16 = jnp.bfloat16
HIGHEST = lax.Precision.HIGHEST
MESH = pl.DeviceIdType.MESH

D_MODEL = 1024
MEM_LEN = 256
HG_HEADS = 8
HG_DK = 128
HG_CHUNK = 64
SWA_HEADS = 16
SWA_KV_HEADS = 2
SWA_GROUP = 8
SWA_HEAD_DIM = 64
SWA_BLOCK = 128
SWA_WINDOW = 128
MEM_HEADS = 4
MEM_HEAD_DIM = 256
NUM_BUCKETS = 32
MAX_DISTANCE = 128
D_FF = 4096
LN_EPS = 1e-5
RMS_EPS = 1e-6
ALPHA = 2.0 ** 0.25
W_A, W_B, W_C, W_D = 4096, 1280, 1024, 3072
IN_COLS = W_A + W_B + W_C + W_D
N_SHARDS = 4
ADAM_LR = 0.001
ADAM_B1 = 0.9
ADAM_B2 = 0.999
ADAM_EPS = 1e-08
ADAM_WD = 0.01
ADAM_STEP = 10
MASK_VALUE = -1e30
VMEM_LIMIT = 56 * 1024 * 1024

NN = ((1,), (0,))
NT = ((1,), (1,))
TN = ((0,), (0,))
HBM = pl.BlockSpec(memory_space=pltpu.HBM)


def _dot(a, b, dims=NN, precision=None):
    return lax.dot_general(a, b, (dims, ((), ())), precision=precision, preferred_element_type=F32)


def _params(sem=None):
    return pltpu.CompilerParams(dimension_semantics=sem, vmem_limit_bytes=VMEM_LIMIT)


def _resident(shape):
    zeros = (0,) * len(shape)
    return pl.BlockSpec(shape, lambda *_: zeros, pipeline_mode=pl.Buffered(1))


def _mm(a, b, *, mode, tm, tn, tk, name, out_dtype=F32, b_panels=False, b_rows=None, out_panels=False, rows_of=None, row_offset=0,
        into=None):
    if mode == "tn":
        kdim, m = a.shape
    else:
        m, kdim = a.shape
    if b_panels:
        n = b.shape[0] * b.shape[2]
        assert b.shape[2] == tn and mode == "nn"
    elif b_rows is not None:
        assert mode == "nt"
        b_offset, n = b_rows
    elif mode == "nt":
        n = b.shape[0]
    else:
        n = b.shape[1]
    assert m % tm == 0 and n % tn == 0 and kdim % tk == 0, (name, m, n, kdim)
    nk = kdim // tk
    dims = {"nn": NN, "nt": NT, "tn": TN}[mode]
    a_spec = pl.BlockSpec((tk, tm), lambda i, j, k: (k, i)) if mode == "tn" else pl.BlockSpec((tm, tk), lambda i, j, k: (i, k))
    if b_panels:
        b_spec = pl.BlockSpec((None, tk, tn), lambda i, j, k: (j, k, 0))
    elif b_rows is not None:
        assert b_offset % BF16_SUBLANES == 0 and tn % BF16_SUBLANES == 0 and tk % 128 == 0
        b_spec = pl.BlockSpec((pl.Element(tn), pl.Element(tk)),
                              lambda i, j, k: (pl.multiple_of(b_offset + j * tn, BF16_SUBLANES), pl.multiple_of(k * tk, 128)))
    elif mode == "nt":
        b_spec = pl.BlockSpec((tn, tk), lambda i, j, k: (j, k))
    else:
        b_spec = pl.BlockSpec((tk, tn), lambda i, j, k: (k, j))
    in_specs = [a_spec, b_spec]
    operands = [a, b]
    aliases = {}
    if out_panels:
        out_shape = jax.ShapeDtypeStruct((n // tn, m, tn), out_dtype)
        o_spec = pl.BlockSpec((None, tm, tn), lambda i, j, k: (j, i, 0))
    elif rows_of is not None:
        out_shape = jax.ShapeDtypeStruct((rows_of, n), out_dtype)
        assert row_offset % BF16_SUBLANES == 0 and tm % BF16_SUBLANES == 0 and tn % 128 == 0
        o_spec = pl.BlockSpec((pl.Element(tm), pl.Element(tn)),
                              lambda i, j, k: (pl.multiple_of(row_offset + i * tm, BF16_SUBLANES), pl.multiple_of(j * tn, 128)))
        if into is not None:
            in_specs.append(pl.BlockSpec(memory_space=pl.ANY))
            operands.append(into)
            aliases = {2: 0}
    else:
        out_shape = jax.ShapeDtypeStruct((m, n), out_dtype)
        o_spec = pl.BlockSpec((tm, tn), lambda i, j, k: (i, j))
    n_in = len(operands)

    def body(*refs):
        a_ref, b_ref, o_ref = refs[0], refs[1], refs[n_in]
        part = _dot(a_ref[...].astype(BF16), b_ref[...].astype(BF16), dims)

        def finish(acc):
            o_ref[...] = acc.astype(out_dtype)

        if nk == 1:
            finish(part)
        else:
            acc_ref = refs[-1]
            k = pl.program_id(2)

            @pl.when(k == 0)
            def _():
                acc_ref[...] = part

            @pl.when(k > 0)
            def _():
                acc_ref[...] += part

            @pl.when(k == nk - 1)
            def _():
                finish(acc_ref[...])

    return pl.pallas_call(
        body, name=name, out_shape=out_shape, grid=(m // tm, n // tn, nk), in_specs=in_specs, out_specs=o_spec,
        scratch_shapes=[pltpu.VMEM((tm, tn), F32)] if nk > 1 else [], input_output_aliases=aliases,
        compiler_params=_params(("parallel", "parallel", "arbitrary")),
    )(*operands)


def _dx_matmul(dzs, wi_t, resid, *, tm, name, tiles, into=None, exchanges=()):
    s = resid.shape[0]
    npieces = len(dzs)
    offsets = [sum(dz.shape[1] for dz in dzs[:p]) for p in range(npieces)]
    first, count = tiles
    tile = lambda i: (first + i, 0)
    in_specs = [pl.BlockSpec((tm, dz.shape[1]), tile) for dz in dzs] + [_resident(wi_t.shape), pl.BlockSpec((tm, D_MODEL), tile)]
    operands = [*dzs, wi_t, resid]
    if into is not None:
        in_specs.append(pl.BlockSpec(memory_space=pl.ANY))
        operands.append(into)
    n_in = len(operands)

    def body(*refs):
        dz_refs, w_ref, r_ref, o_ref = refs[:npieces], refs[npieces], refs[npieces + 1], refs[n_in]
        total = ALPHA * r_ref[...]
        for p in range(npieces):
            total = total + _dot(dz_refs[p][...], w_ref[offsets[p]:offsets[p] + dzs[p].shape[1], :], NN)
        o_ref[...] = total

    return _fused_call(
        body, name=name, out_shape=jax.ShapeDtypeStruct((s, D_MODEL), F32), grid=(count,), in_specs=in_specs,
        out_specs=pl.BlockSpec((tm, D_MODEL), tile), scratch_shapes=[], operands=operands, exchanges=exchanges,
        aliases={n_in - 1: 0} if into is not None else None)


def _lower_bound(lbl_ref):
    l0, l1 = lbl_ref[0:1, :], lbl_ref[1:2, :]
    mx = jnp.maximum(l0, l1)
    e0, e1 = jnp.exp(l0 - mx), jnp.exp(l1 - mx)
    return e0 / (e0 + e1)


HEAD_COLS = [slice(h * HG_DK, (h + 1) * HG_DK) for h in range(HG_HEADS)]


def _head_mean(x):
    return jnp.concatenate([jnp.broadcast_to(jnp.mean(x[:, c], axis=-1, keepdims=True), (x.shape[0], HG_DK)) for c in HEAD_COLS], axis=1)


def _triangle_sum(tri_b, x):
    p0 = x.astype(BF16)
    r1 = x - p0.astype(F32)
    p1 = r1.astype(BF16)
    p2 = (r1 - p1.astype(F32)).astype(BF16)
    return _dot(tri_b, p0) + _dot(tri_b, p1) + _dot(tri_b, p2)


def _chunk_forward(q, fl, v, lb, tril_b):
    sg = jax.nn.sigmoid(fl)
    f = lb + (1.0 - lb) * sg
    k = 1.0 - f
    b = _triangle_sum(tril_b, jnp.log(f))
    b_last = b[HG_CHUNK - 1:HG_CHUNK, :]
    eb, enb, eo = jnp.exp(b), jnp.exp(-b), jnp.exp(b_last - b)
    return sg, f, k, b_last, eb, enb, eo, q * eb, k * enb, k * eo


def _hgrn_fwd(za, lb_logits, gain, *, name, exchanges=()):
    s = za.shape[0]
    t = min(256, s)
    ncs = t // HG_CHUNK

    def body(z_ref, lbl_ref, gain_ref, oa_ref, oraw_ref, st_ref, state):
        @pl.when(pl.program_id(0) == 0)
        def _():
            state[...] = jnp.zeros_like(state)

        lb_all = _lower_bound(lbl_ref)
        row = lax.broadcasted_iota(jnp.int32, (HG_CHUNK, HG_CHUNK), 0)
        col = lax.broadcasted_iota(jnp.int32, (HG_CHUNK, HG_CHUNK), 1)
        tril = row >= col
        tril_b = tril.astype(BF16)
        gain_all = gain_ref[...]

        def chunk(i, carry):
            r = pl.ds(pl.multiple_of(i * HG_CHUNK, HG_CHUNK), HG_CHUNK)
            q, fl, v, hg = (z_ref[r, j * D_MODEL:(j + 1) * D_MODEL] for j in range(4))
            _, _, _, b_last, _, _, _, q_in, k_in, k_out = _chunk_forward(q, fl, v, lb_all, tril_b)
            q_in_b, k_in_b, k_out_b, vb = (u.astype(BF16) for u in (q_in, k_in, k_out, v))
            decay = jnp.exp(b_last)
            sts = [state[h] for h in range(HG_HEADS)]
            attn = [_dot(q_in_b[:, c], k_in_b[:, c], NT) for c in HEAD_COLS]
            inter = [_dot(q_in_b[:, c], sts[h].astype(BF16), NT) for h, c in enumerate(HEAD_COLS)]
            upd = [_dot(vb[:, c], k_out_b[:, c], TN) for c in HEAD_COLS]
            attn = [jnp.where(tril, a, 0.0).astype(BF16) for a in attn]
            outs = [_dot(attn[h], vb[:, c], NN) + inter[h] for h, c in enumerate(HEAD_COLS)]
            for h, c in enumerate(HEAD_COLS):
                st_ref[h, i] = sts[h]
                state[h] = sts[h] * decay[:, c] + upd[h]
            o = jnp.concatenate(outs, axis=1)
            oraw_ref[r, :] = o
            n = o * lax.rsqrt(_head_mean(o * o) + RMS_EPS)
            oa_ref[r, :] = (n * gain_all * (hg * jax.nn.sigmoid(hg))).astype(BF16)
            return carry

        lax.fori_loop(0, ncs, chunk, 0, unroll=True)

    return _fused_call(
        body, name=name, grid=(s // t,),
        out_shape=(jax.ShapeDtypeStruct((s, D_MODEL), BF16), jax.ShapeDtypeStruct((s, D_MODEL), F32),
                   jax.ShapeDtypeStruct((HG_HEADS, s // HG_CHUNK, HG_DK, HG_DK), F32)),
        in_specs=[pl.BlockSpec((t, W_A), lambda i: (i, 0)), _resident((2, D_MODEL)), _resident((1, D_MODEL))],
        out_specs=(pl.BlockSpec((t, D_MODEL), lambda i: (i, 0)), pl.BlockSpec((t, D_MODEL), lambda i: (i, 0)),
                   pl.BlockSpec((HG_HEADS, ncs, HG_DK, HG_DK), lambda i: (0, i, 0, 0))),
        scratch_shapes=[pltpu.VMEM((HG_HEADS, HG_DK, HG_DK), F32)],
        operands=[za, lb_logits, gain], exchanges=exchanges)


def _hgrn_bwd(za, oraw, do_a, states, lb_logits, gain, *, name, exchanges=()):
    s = za.shape[0]
    t = min(256, s)
    ncs = t // HG_CHUNK
    nt = s // t

    def body(z_ref, oraw_ref, do_ref, st_ref, lbl_ref, gain_ref, dz_ref, stats_ref, dstate):
        step = pl.program_id(0)

        @pl.when(step == 0)
        def _():
            dstate[...] = jnp.zeros_like(dstate)
            stats_ref[...] = jnp.zeros_like(stats_ref)

        lb_all = _lower_bound(lbl_ref)
        row = lax.broadcasted_iota(jnp.int32, (HG_CHUNK, HG_CHUNK), 0)
        col = lax.broadcasted_iota(jnp.int32, (HG_CHUNK, HG_CHUNK), 1)
        tril = row >= col
        tril_b = tril.astype(BF16)
        triu_b = (row <= col).astype(BF16)
        gain_all = gain_ref[...]

        def chunk(ii, carry):
            i = ncs - 1 - ii
            r = pl.ds(pl.multiple_of(i * HG_CHUNK, HG_CHUNK), HG_CHUNK)
            q, fl, v, hg = (z_ref[r, j * D_MODEL:(j + 1) * D_MODEL] for j in range(4))
            o = oraw_ref[r, :]
            doa = do_ref[r, :]
            rms = lax.rsqrt(_head_mean(o * o) + RMS_EPS)
            n = o * rms
            sgg = jax.nn.sigmoid(hg)
            silu = hg * sgg
            dhg = doa * n * gain_all * (sgg * (1.0 + hg * (1.0 - sgg)))
            dgain = jnp.sum(doa * n * silu, axis=0, keepdims=True)
            dn = doa * gain_all * silu
            do = rms * (dn - n * _head_mean(dn * n))
            sg, f, k, b_last, eb, enb, eo, q_in, k_in, k_out = _chunk_forward(q, fl, v, lb_all, tril_b)
            q_in_b, k_in_b, k_out_b, vb, dob = (u.astype(BF16) for u in (q_in, k_in, k_out, v, do))
            decay = jnp.exp(b_last)
            sts = [st_ref[h, i] for h in range(HG_HEADS)]
            dsts = [dstate[h] for h in range(HG_HEADS)]
            dsts_b = [d.astype(BF16) for d in dsts]
            heads = list(enumerate(HEAD_COLS))
            attn = [_dot(q_in_b[:, c], k_in_b[:, c], NT) for h, c in heads]
            dattn = [_dot(dob[:, c], vb[:, c], NT) for h, c in heads]
            dq_st = [_dot(dob[:, c], sts[h].astype(BF16), NN) for h, c in heads]
            dk_out = [_dot(vb[:, c], dsts_b[h], NN) for h, c in heads]
            dv_st = [_dot(k_out_b[:, c], dsts_b[h], NT) for h, c in heads]
            dst_o = [_dot(dob[:, c], q_in_b[:, c], TN) for h, c in heads]
            attn = [jnp.where(tril, a, 0.0).astype(BF16) for a in attn]
            dattn = [jnp.where(tril, a, 0.0).astype(BF16) for a in dattn]
            dq_in = jnp.concatenate([_dot(dattn[h], k_in_b[:, c], NN) + dq_st[h] for h, c in heads], axis=1)
            dk_in = jnp.concatenate([_dot(dattn[h], q_in_b[:, c], TN) for h, c in heads], axis=1)
            dv = jnp.concatenate([_dot(attn[h], dob[:, c], TN) + dv_st[h] for h, c in heads], axis=1)
            dk_out = jnp.concatenate(dk_out, axis=1)
            dst_st = jnp.concatenate([jnp.sum(dsts[h] * sts[h], axis=0, keepdims=True) for h in range(HG_HEADS)], axis=1)
            for h, c in heads:
                dstate[h] = dsts[h] * decay[:, c] + dst_o[h]
            db_last = decay * dst_st + jnp.sum(dk_out * k_out, axis=0, keepdims=True)
            db = dq_in * q_in - dk_in * k_in - dk_out * k_out
            dg = _triangle_sum(triu_b, db) + db_last
            dk = dk_in * enb + dk_out * eo
            df = dg / f - dk
            stats_ref[0:1, :] += dgain
            stats_ref[1:2, :] += jnp.sum(df * (1.0 - sg), axis=0, keepdims=True)
            dz_ref[r, 0:1024] = (dq_in * eb).astype(BF16)
            dz_ref[r, 1024:2048] = (df * (1.0 - lb_all) * sg * (1.0 - sg)).astype(BF16)
            dz_ref[r, 2048:3072] = dv.astype(BF16)
            dz_ref[r, 3072:4096] = dhg.astype(BF16)
            return carry

        lax.fori_loop(0, ncs, chunk, 0, unroll=True)

        @pl.when(step == nt - 1)
        def _():
            dl0 = stats_ref[1:2, :] * lb_all * (1.0 - lb_all)
            stats_ref[1:2, :] = dl0
            stats_ref[2:3, :] = -dl0

    rev = lambda i: (nt - 1 - i, 0)
    return _fused_call(
        body, name=name, grid=(nt,),
        out_shape=(jax.ShapeDtypeStruct((s, W_A), BF16), jax.ShapeDtypeStruct((8, D_MODEL), F32)),
        in_specs=[pl.BlockSpec((t, W_A), rev), pl.BlockSpec((t, D_MODEL), rev), pl.BlockSpec((t, D_MODEL), rev),
                  pl.BlockSpec((HG_HEADS, ncs, HG_DK, HG_DK), lambda i: (0, nt - 1 - i, 0, 0)),
                  _resident((2, D_MODEL)), _resident((1, D_MODEL))],
        out_specs=(pl.BlockSpec((t, W_A), rev), pl.BlockSpec((8, D_MODEL), lambda i: (0, 0))),
        scratch_shapes=[pltpu.VMEM((HG_HEADS, HG_DK, HG_DK), F32)],
        operands=[za, oraw, do_a, states, lb_logits, gain], exchanges=exchanges)


def _t5_bucket(n):
    max_exact = NUM_BUCKETS // 2
    nf = jnp.maximum(n, 1).astype(F32)
    large = max_exact + (jnp.log(nf / max_exact) / math.log(MAX_DISTANCE / max_exact) * (NUM_BUCKETS - max_exact)).astype(jnp.int32)
    large = jnp.minimum(large, NUM_BUCKETS - 1)
    return jnp.where(n < max_exact, n, large)


def _bias_selector():
    qi = jnp.arange(SWA_BLOCK)[:, None] + SWA_BLOCK
    kj = jnp.arange(2 * SWA_BLOCK)[None, :]
    dist = qi - kj
    band = ((dist >= 0) & (dist < SWA_WINDOW)).reshape(1, -1)
    bucket = _t5_bucket(jnp.clip(dist, 0, SWA_WINDOW - 1)).reshape(1, -1)
    onehot = ((bucket == jnp.arange(NUM_BUCKETS)[:, None]) & band).astype(F32)
    return onehot, jnp.where(band, 0.0, MASK_VALUE).astype(F32)


def _bias_table(rel_bias_t, onehot, maskrow, *, name):
    def body(rb_ref, oh_ref, mask_ref, o_ref):
        o_ref[...] = _dot(rb_ref[...], oh_ref[...], NN, HIGHEST) + mask_ref[...]

    return pl.pallas_call(body, name=name, out_shape=jax.ShapeDtypeStruct((SWA_HEADS, onehot.shape[1]), F32),
                          compiler_params=_params())(rel_bias_t, onehot, maskrow)


def _bias_grad(dbias2d, onehot, *, name):
    def body(db_ref, oh_ref, o_ref):
        o_ref[...] = _dot(db_ref[...], oh_ref[...], NT, HIGHEST)

    return pl.pallas_call(body, name=name, out_shape=jax.ShapeDtypeStruct((SWA_HEADS, NUM_BUCKETS), F32),
                          compiler_params=_params())(dbias2d, onehot)


GROUP_LANES = SWA_GROUP * SWA_BLOCK


def _swa_operands(zq_ref, kv_cur_ref, kv_prev_ref):
    q = (zq_ref[:, 0:1024] * (SWA_HEAD_DIM ** -0.5)).astype(BF16)
    kv_c = kv_cur_ref[...].astype(BF16)
    kv_p = kv_prev_ref[...].astype(BF16)
    kks = [jnp.concatenate([kv_p[:, g * 64:(g + 1) * 64], kv_c[:, g * 64:(g + 1) * 64]], axis=0) for g in range(SWA_KV_HEADS)]
    vvs = [jnp.concatenate([kv_p[:, 128 + g * 64:128 + (g + 1) * 64], kv_c[:, 128 + g * 64:128 + (g + 1) * 64]], axis=0)
           for g in range(SWA_KV_HEADS)]
    return q, kks, vvs


SWA_PART_HEADS = 8
SWA_PARTS = [(h0 // SWA_GROUP, h0) for h0 in range(0, SWA_HEADS, SWA_PART_HEADS)]


def _part_lanes(h0):
    return slice(h0 * SWA_BLOCK, (h0 + SWA_PART_HEADS) * SWA_BLOCK)


def _stack_heads(x, h0):
    return jnp.concatenate([x[:, h * SWA_HEAD_DIM:(h + 1) * SWA_HEAD_DIM] for h in range(h0, h0 + SWA_PART_HEADS)], axis=0)


def _heads_to_lanes(xt):
    pairs = []
    for j in range(0, xt.shape[1] // SWA_BLOCK, 2):
        two = jnp.concatenate([xt[:, j * SWA_BLOCK:(j + 1) * SWA_BLOCK], xt[:, (j + 1) * SWA_BLOCK:(j + 2) * SWA_BLOCK]], axis=0)
        pairs.append(two.T)
    return jnp.concatenate(pairs, axis=1)


def _swa_softmax(score_t, bias_ref, sink_ref, h0):
    sc = score_t + bias_ref[:, _part_lanes(h0)]
    sink = sink_ref[:, _part_lanes(h0)]
    m = jnp.maximum(jnp.max(sc, axis=0, keepdims=True), sink)
    e = jnp.exp(sc - m)
    e_sink = jnp.exp(sink - m)
    return e, 1.0 / (jnp.sum(e, axis=0, keepdims=True) + e_sink), e_sink


def _swa_tables(bias2d, sinks):
    bias_t = bias2d.reshape(SWA_HEADS, SWA_BLOCK, 2 * SWA_BLOCK).transpose(2, 0, 1).reshape(2 * SWA_BLOCK, SWA_HEADS * SWA_BLOCK)
    first = jnp.where(jnp.arange(2 * SWA_BLOCK)[:, None] < SWA_BLOCK, MASK_VALUE, bias_t)
    return jnp.stack([first, bias_t]), jnp.repeat(sinks, SWA_BLOCK, axis=1)


def _swa_fwd(zb, bias_tables, sink_lanes, *, name, exchanges=()):
    s = zb.shape[0]
    nb = s // SWA_BLOCK

    def body(zq_ref, kvc_ref, kvp_ref, bias_ref, sink_ref, o_ref):
        q, kks, vvs = _swa_operands(zq_ref, kvc_ref, kvp_ref)
        scores = [_dot(kks[g], _stack_heads(q, h0), NT) for g, h0 in SWA_PARTS]
        probs = []
        for score, (_, h0) in zip(scores, SWA_PARTS):
            e, inv, _ = _swa_softmax(score, bias_ref, sink_ref, h0)
            probs.append((e * inv).astype(BF16))
        outs = [_dot(vvs[g], p, TN) for p, (g, _) in zip(probs, SWA_PARTS)]
        o_ref[...] = jnp.concatenate([_heads_to_lanes(o) for o in outs], axis=1).astype(BF16)

    return _fused_call(
        body, name=name, grid=(nb,), out_shape=jax.ShapeDtypeStruct((s, D_MODEL), BF16),
        in_specs=[pl.BlockSpec((SWA_BLOCK, W_B), lambda n: (n, 0)),
                  pl.BlockSpec((SWA_BLOCK, 256), lambda n: (n, 4)),
                  pl.BlockSpec((SWA_BLOCK, 256), lambda n: (jnp.maximum(n - 1, 0), 4)),
                  pl.BlockSpec((None, 2 * SWA_BLOCK, SWA_HEADS * SWA_BLOCK), lambda n: (jnp.minimum(n, 1), 0, 0)),
                  _resident((1, SWA_HEADS * SWA_BLOCK))],
        out_specs=pl.BlockSpec((SWA_BLOCK, D_MODEL), lambda n: (n, 0)), scratch_shapes=[],
        operands=[zb, zb, zb, bias_tables, sink_lanes], exchanges=exchanges)


def _swa_bwd(zb, do_b, bias_tables, sink_lanes, *, name, exchanges=()):
    s = zb.shape[0]
    nb = s // SWA_BLOCK
    scale = SWA_HEAD_DIM ** -0.5

    def body(zq_ref, kvc_ref, kvp_ref, do_ref, bias_ref, sink_ref, dz_ref, dbias_ref, dsink_ref, carry, dsink_acc):
        step = pl.program_id(0)

        @pl.when(step == 0)
        def _():
            carry[...] = jnp.zeros_like(carry)
            dsink_acc[...] = jnp.zeros_like(dsink_acc)
            dbias_ref[...] = jnp.zeros_like(dbias_ref)

        q, kks, vvs = _swa_operands(zq_ref, kvc_ref, kvp_ref)
        do = do_ref[...].astype(BF16)
        parts = range(len(SWA_PARTS))
        q_rows = [_stack_heads(q, h0) for _, h0 in SWA_PARTS]
        do_rows = [_stack_heads(do, h0) for _, h0 in SWA_PARTS]
        scores = [_dot(kks[g], q_rows[i], NT) for i, (g, _) in enumerate(SWA_PARTS)]
        dps = [_dot(vvs[g], do_rows[i], NT) for i, (g, _) in enumerate(SWA_PARTS)]
        ps, dss = [], []
        for i, (_, h0) in enumerate(SWA_PARTS):
            e, inv, e_sink = _swa_softmax(scores[i], bias_ref, sink_ref, h0)
            p = e * inv
            delta = jnp.sum(p * dps[i], axis=0, keepdims=True)
            ds = p * (dps[i] - delta)
            dbias_ref[:, _part_lanes(h0)] += ds
            dsink_acc[:, _part_lanes(h0)] -= e_sink * inv * delta
            ps.append(p.astype(BF16))
            dss.append(ds.astype(BF16))
        dqs = [_dot(kks[g], dss[i], TN) * scale for i, (g, _) in enumerate(SWA_PARTS)]
        in_group = lambda xs, g, axis: jnp.concatenate([xs[i] for i in parts if SWA_PARTS[i][0] == g], axis=axis)
        dkks = [_dot(in_group(dss, g, 1), in_group(q_rows, g, 0), NN) for g in range(SWA_KV_HEADS)]
        dvvs = [_dot(in_group(ps, g, 1), in_group(do_rows, g, 0), NN) for g in range(SWA_KV_HEADS)]
        dkv = jnp.concatenate(dkks + dvvs, axis=1)
        dz_ref[:, 0:1024] = jnp.concatenate([_heads_to_lanes(dq) for dq in dqs], axis=1).astype(BF16)
        dz_ref[:, 1024:1280] = (dkv[SWA_BLOCK:, :] + carry[...]).astype(BF16)
        carry[...] = dkv[:SWA_BLOCK, :]

        @pl.when(step == nb - 1)
        def _():
            acc = dsink_acc[...]
            dsink_ref[...] = jnp.concatenate([jnp.sum(acc[:, h * SWA_BLOCK:(h + 1) * SWA_BLOCK], axis=1, keepdims=True)
                                              for h in range(SWA_HEADS)], axis=1)

    rev = lambda i: (nb - 1 - i, 0)
    table_shape = (2 * SWA_BLOCK, SWA_HEADS * SWA_BLOCK)
    return _fused_call(
        body, name=name, grid=(nb,),
        out_shape=(jax.ShapeDtypeStruct((s, W_B), BF16), jax.ShapeDtypeStruct(table_shape, F32), jax.ShapeDtypeStruct((1, SWA_HEADS), F32)),
        in_specs=[pl.BlockSpec((SWA_BLOCK, W_B), rev),
                  pl.BlockSpec((SWA_BLOCK, 256), lambda i: (nb - 1 - i, 4)),
                  pl.BlockSpec((SWA_BLOCK, 256), lambda i: (jnp.maximum(nb - 2 - i, 0), 4)),
                  pl.BlockSpec((SWA_BLOCK, D_MODEL), rev),
                  pl.BlockSpec((None,) + table_shape, lambda i: (jnp.minimum(nb - 1 - i, 1), 0, 0)),
                  _resident((1, SWA_HEADS * SWA_BLOCK))],
        out_specs=(pl.BlockSpec((SWA_BLOCK, W_B), rev), pl.BlockSpec(table_shape, lambda i: (0, 0)),
                   pl.BlockSpec((1, SWA_HEADS), lambda i: (0, 0))),
        scratch_shapes=[pltpu.VMEM((SWA_BLOCK, 256), F32), pltpu.VMEM((1, SWA_HEADS * SWA_BLOCK), F32)],
        operands=[zb, zb, zb, do_b, bias_tables, sink_lanes], exchanges=exchanges)


MEM_COLS = [slice(h * MEM_HEAD_DIM, (h + 1) * MEM_HEAD_DIM) for h in range(MEM_HEADS)]
MEM_VCOLS = [slice(D_MODEL + h * MEM_HEAD_DIM, D_MODEL + (h + 1) * MEM_HEAD_DIM) for h in range(MEM_HEADS)]


def _mem_probs(zc_ref, mkv_ref):
    qs = [(zc_ref[:, c] * (MEM_HEAD_DIM ** -0.5)).astype(BF16) for c in MEM_COLS]
    scores = [_dot(qs[h], mkv_ref[:, c], NT) for h, c in enumerate(MEM_COLS)]
    ps = []
    for sc in scores:
        e = jnp.exp(sc - jnp.max(sc, axis=-1, keepdims=True))
        ps.append(e / jnp.sum(e, axis=-1, keepdims=True))
    return qs, ps


def _mem_fwd(zc, mkv, *, name):
    s = zc.shape[0]
    t = min(512, s)

    def body(zc_ref, mkv_ref, o_ref):
        _, ps = _mem_probs(zc_ref, mkv_ref)
        ps = [p.astype(BF16) for p in ps]
        o_ref[...] = jnp.concatenate([_dot(ps[h], mkv_ref[:, vc], NN) for h, vc in enumerate(MEM_VCOLS)], axis=1).astype(BF16)

    return pl.pallas_call(
        body, name=name, grid=(s // t,), out_shape=jax.ShapeDtypeStruct((s, D_MODEL), BF16),
        in_specs=[pl.BlockSpec((t, D_MODEL), lambda i: (i, 0)), _resident((MEM_LEN, 2 * D_MODEL))],
        out_specs=pl.BlockSpec((t, D_MODEL), lambda i: (i, 0)), compiler_params=_params(("parallel",)),
    )(zc, mkv)


def _mem_bwd(zc, do_c, mkv, *, name):
    s = zc.shape[0]
    t = min(512, s)

    def body(zc_ref, do_ref, mkv_ref, dz_ref, dmkv_ref):
        @pl.when(pl.program_id(0) == 0)
        def _():
            dmkv_ref[...] = jnp.zeros_like(dmkv_ref)

        heads = range(MEM_HEADS)
        qs, ps = _mem_probs(zc_ref, mkv_ref)
        dos = [do_ref[:, c].astype(BF16) for c in MEM_COLS]
        dps = [_dot(dos[h], mkv_ref[:, MEM_VCOLS[h]], NT) for h in heads]
        dss = [(ps[h] * (dps[h] - jnp.sum(ps[h] * dps[h], axis=-1, keepdims=True))).astype(BF16) for h in heads]
        ps = [p.astype(BF16) for p in ps]
        dz_ref[...] = jnp.concatenate([_dot(dss[h], mkv_ref[:, MEM_COLS[h]], NN) * (MEM_HEAD_DIM ** -0.5) for h in heads], axis=1).astype(BF16)
        dmkv_ref[...] += jnp.concatenate([_dot(dss[h], qs[h], TN) for h in heads] + [_dot(ps[h], dos[h], TN) for h in heads], axis=1)

    return pl.pallas_call(
        body, name=name, grid=(s // t,),
        out_shape=(jax.ShapeDtypeStruct((s, D_MODEL), BF16), jax.ShapeDtypeStruct((MEM_LEN, 2 * D_MODEL), F32)),
        in_specs=[pl.BlockSpec((t, D_MODEL), lambda i: (i, 0)), pl.BlockSpec((t, D_MODEL), lambda i: (i, 0)),
                  _resident((MEM_LEN, 2 * D_MODEL))],
        out_specs=(pl.BlockSpec((t, D_MODEL), lambda i: (i, 0)), pl.BlockSpec((MEM_LEN, 2 * D_MODEL), lambda i: (0, 0))),
        compiler_params=_params(("arbitrary",)),
    )(zc, do_c, mkv)


def _normalize(pre):
    mu = jnp.mean(pre, axis=-1, keepdims=True)
    xc = pre - mu
    rstd = lax.rsqrt(jnp.mean(xc * xc, axis=-1, keepdims=True) + LN_EPS)
    return xc * rstd, rstd


def _layer_norm_bwd(dh, xhat, rstd, g):
    dxh = dh * g
    dpre = rstd * (dxh - jnp.mean(dxh, axis=-1, keepdims=True) - xhat * jnp.mean(dxh * xhat, axis=-1, keepdims=True))
    return dpre, jnp.sum(dh * xhat, axis=0, keepdims=True), jnp.sum(dh, axis=0, keepdims=True)


def _merge_fwd(o_a, o_b, o_c, zd, x, wbr, wo, *, name):
    s = x.shape[0]
    t = min(256, s)
    row = lambda w: pl.BlockSpec((t, w), lambda i: (i, 0))

    def body(oa_ref, ob_ref, oc_ref, zd_ref, x_ref, wbr_ref, wo_ref, xhat_ref, rstd_ref, merged_ref, pa_ref, pb_ref, pc_ref):
        merged = jnp.zeros((t, D_MODEL), F32)
        for b, (o_ref, p_ref) in enumerate(((oa_ref, pa_ref), (ob_ref, pb_ref), (oc_ref, pc_ref))):
            p = _dot(o_ref[...], wbr_ref[b], NN)
            p_ref[...] = p.astype(BF16)
            merged = merged + jax.nn.sigmoid(zd_ref[:, b * D_MODEL:(b + 1) * D_MODEL]) * p
        merged_b = merged.astype(BF16)
        merged_ref[...] = merged_b
        xhat, rstd = _normalize(ALPHA * x_ref[...] + _dot(merged_b, wo_ref[...], NN))
        xhat_ref[...] = xhat
        rstd_ref[...] = rstd

    act = jax.ShapeDtypeStruct((s, D_MODEL), F32)
    return pl.pallas_call(
        body, name=name, grid=(s // t,),
        out_shape=(act, jax.ShapeDtypeStruct((s, 1), F32)) + (jax.ShapeDtypeStruct((s, D_MODEL), BF16),) * 4,
        in_specs=[row(D_MODEL), row(D_MODEL), row(D_MODEL), row(W_D), row(D_MODEL),
                  _resident((3, D_MODEL, D_MODEL)), _resident((D_MODEL, D_MODEL))],
        out_specs=(row(D_MODEL), row(1), row(D_MODEL), row(D_MODEL), row(D_MODEL), row(D_MODEL)),
        compiler_params=_params(("parallel",)),
    )(o_a, o_b, o_c, zd, x, wbr, wo)


def _merge_bwd(dpre1, zd, pa, pb, pc, o_a, o_b, o_c, merged, wbr, wo, *, name, exchanges=()):
    s = dpre1.shape[0]
    t = min(256, s)
    nt = s // t
    row = lambda w: pl.BlockSpec((t, w), lambda i: (i, 0))

    def body(dpre_ref, zd_ref, pa_ref, pb_ref, pc_ref, oa_ref, ob_ref, oc_ref, mg_ref, wbr_ref, wo_ref,
             dzd_ref, doa_ref, dob_ref, doc_ref, gwa_ref, gwb_ref, gwc_ref, gwo_ref, acc):
        step = pl.program_id(0)

        @pl.when(step == 0)
        def _():
            acc[...] = jnp.zeros_like(acc)

        dpre_b = dpre_ref[...].astype(BF16)
        dmerged = _dot(dpre_b, wo_ref[...], NT)
        acc[3] += _dot(mg_ref[...], dpre_b, TN)
        branches = ((pa_ref, oa_ref, doa_ref), (pb_ref, ob_ref, dob_ref), (pc_ref, oc_ref, doc_ref))
        for b, (p_ref, o_ref, do_ref) in enumerate(branches):
            gate = jax.nn.sigmoid(zd_ref[:, b * D_MODEL:(b + 1) * D_MODEL])
            dzd_ref[:, b * D_MODEL:(b + 1) * D_MODEL] = (dmerged * p_ref[...] * gate * (1.0 - gate)).astype(BF16)
            dp = (dmerged * gate).astype(BF16)
            acc[b] += _dot(o_ref[...], dp, TN)
            do_ref[...] = _dot(dp, wbr_ref[b], NT).astype(do_ref.dtype)

        @pl.when(step == nt - 1)
        def _():
            for b, gw_ref in enumerate((gwa_ref, gwb_ref, gwc_ref, gwo_ref)):
                pltpu.sync_copy(acc.at[b], gw_ref)

    act = jax.ShapeDtypeStruct((s, D_MODEL), F32)
    actb = jax.ShapeDtypeStruct((s, D_MODEL), BF16)
    gw = jax.ShapeDtypeStruct((D_MODEL, D_MODEL), F32)
    return _fused_call(
        body, name=name, grid=(nt,),
        out_shape=(jax.ShapeDtypeStruct((s, W_D), BF16), act, actb, actb, gw, gw, gw, gw),
        in_specs=[row(D_MODEL), row(W_D)] + [row(D_MODEL)] * 7 + [_resident((3, D_MODEL, D_MODEL)), _resident((D_MODEL, D_MODEL))],
        out_specs=(row(W_D),) + (row(D_MODEL),) * 3 + (HBM,) * 4, scratch_shapes=[pltpu.VMEM((4, D_MODEL, D_MODEL), F32)],
        operands=[dpre1, zd, pa, pb, pc, o_a, o_b, o_c, merged, wbr, wo], exchanges=exchanges)


def _mlp_loss(xhat1, rstd1, target, ln1_g, ln1_b, ln2_g, ln2_b, wu, wd, *, name):
    s = xhat1.shape[0]
    t = min(256, s)
    npan = wu.shape[0]
    row = lambda w: pl.BlockSpec((t, w), lambda i: (i, 0))
    vec = _resident((1, D_MODEL))

    def body(xhat_ref, rstd_ref, tgt_ref, g1_ref, b1_ref, g2_ref, b2_ref, wu_ref, wd_ref,
             dpre1_ref, dpre2_ref, h1_ref, a_ref, du_ref, stats_ref):
        @pl.when(pl.program_id(0) == 0)
        def _():
            stats_ref[...] = jnp.zeros_like(stats_ref)

        xhat1_v = xhat_ref[...]
        h1 = xhat1_v * g1_ref[...] + b1_ref[...]
        h1_b = h1.astype(BF16)
        h1_ref[...] = h1_b
        us = []
        ff = jnp.zeros((t, D_MODEL), F32)
        for j in range(npan):
            u = _dot(h1_b, wu_ref[j], NN)
            us.append(u)
            r = jnp.maximum(u, 0.0)
            a_b = (r * r).astype(BF16)
            a_ref[:, j * D_MODEL:(j + 1) * D_MODEL] = a_b
            ff = ff + _dot(a_b, wd_ref[j], NN)
        xhat2, rstd2 = _normalize(ALPHA * h1 + ff)
        err = xhat2 * g2_ref[...] + b2_ref[...] - tgt_ref[...]
        stats_ref[4:5, :] += jnp.sum(err * err, axis=0, keepdims=True)
        dpre2, dg2, db2 = _layer_norm_bwd(err * (1.0 / D_MODEL), xhat2, rstd2, g2_ref[...])
        stats_ref[0:1, :] += dg2
        stats_ref[1:2, :] += db2
        dpre2_b = dpre2.astype(BF16)
        dpre2_ref[...] = dpre2_b
        dh1 = ALPHA * dpre2
        for j in range(npan):
            du_b = (_dot(dpre2_b, wd_ref[j], NT) * (2.0 * jnp.maximum(us[j], 0.0))).astype(BF16)
            du_ref[:, j * D_MODEL:(j + 1) * D_MODEL] = du_b
            dh1 = dh1 + _dot(du_b, wu_ref[j], NT)
        dpre1, dg1, db1 = _layer_norm_bwd(dh1, xhat1_v, rstd_ref[...], g1_ref[...])
        stats_ref[2:3, :] += dg1
        stats_ref[3:4, :] += db1
        dpre1_ref[...] = dpre1

    actb = jax.ShapeDtypeStruct((s, D_MODEL), BF16)
    wide = jax.ShapeDtypeStruct((s, D_FF), BF16)
    return pl.pallas_call(
        body, name=name, grid=(s // t,),
        out_shape=(jax.ShapeDtypeStruct((s, D_MODEL), F32), actb, actb, wide, wide, jax.ShapeDtypeStruct((8, D_MODEL), F32)),
        in_specs=[row(D_MODEL), row(1), row(D_MODEL), vec, vec, vec, vec,
                  _resident((npan, D_MODEL, D_MODEL)), _resident((npan, D_MODEL, D_MODEL))],
        out_specs=(row(D_MODEL), row(D_MODEL), row(D_MODEL), row(D_FF), row(D_FF), pl.BlockSpec((8, D_MODEL), lambda i: (0, 0))),
        compiler_params=_params(("arbitrary",)),
    )(xhat1, rstd1, target, ln1_g, ln1_b, ln2_g, ln2_b, wu, wd)


BRANCH_WEIGHTS = ("w_branch_hg", "w_branch_swa", "w_branch_mem")


def _local_step(x, mem, target, wi_t, wmkv, late, lb_logits, gain, sinks, rel_bias, ln1_g, ln1_b, ln2_g, ln2_b, *, distributed):
    s = x.shape[0]
    tm = min(1024, s)
    tk = min(2048, s)
    xb = x.astype(BF16)
    memb = mem.astype(BF16)
    if distributed:
        cx, cy, cc = lax.axis_index("x"), lax.axis_index("y"), lax.axis_index("c")
        pos = jnp.stack([2 * cx + cy, cc]).astype(jnp.int32)
    gather = (lambda names: [_gather_exchange([late[k] for k in names])]) if distributed else (lambda names: [])
    to_sibling = (lambda grads: [_sibling_halves_exchange(grads)]) if distributed else (lambda grads: [])
    to_chips = (lambda sums: [_chip_partials_exchange([bf for bf, _ in sums])]) if distributed else (lambda sums: [])

    def chip_sums(names, grads, from_sibling):
        return [_add_sibling(g, o, pos, name="add_sibling_" + k) for k, g, o in zip(names, grads, from_sibling)]

    def shard_sums(names, sums, from_chips):
        return {k: _add_chips(mine, o, pos, name="add_chips_" + k) for k, (_, mine), o in zip(names, sums, from_chips)}

    za = _mm(xb, wi_t, mode="nt", tm=min(512, s), tn=W_A, tk=D_MODEL, name="proj_a", b_rows=(0, W_A))
    zb = _mm(xb, wi_t, mode="nt", tm=tm, tn=W_B, tk=D_MODEL, name="proj_b", out_dtype=BF16, b_rows=(W_A, W_B))
    zc = _mm(xb, wi_t, mode="nt", tm=tm, tn=W_C, tk=D_MODEL, name="proj_c", out_dtype=BF16, b_rows=(W_A + W_B, W_C))
    zd = _mm(xb, wi_t, mode="nt", tm=min(512, s), tn=W_D, tk=D_MODEL, name="proj_d", b_rows=(W_A + W_B + W_C, W_D))
    mkv = _mm(memb, wmkv, mode="nn", tm=MEM_LEN, tn=512, tk=D_MODEL, name="mem_kv", out_dtype=BF16, b_panels=True)
    onehot, maskrow = _bias_selector()
    bias_tables, sink_lanes = _swa_tables(_bias_table(rel_bias.T, onehot, maskrow, name="bias_table"), sinks)
    (o_a, o_raw, states), landed = _hgrn_fwd(za, lb_logits, gain, name="hgrn_fwd", exchanges=gather(("w_up", "w_down")))
    wu, wd = landed[0] if distributed else (late["wu"], late["wd"])
    o_b, landed = _swa_fwd(zb, bias_tables, sink_lanes, name="swa_fwd", exchanges=gather(BRANCH_WEIGHTS + ("w_out",)))
    if distributed:
        wbr = jnp.stack([wb.reshape(D_MODEL, D_MODEL) for wb in landed[0][:3]])
        wo = landed[0][3].reshape(D_MODEL, D_MODEL)
    else:
        wbr, wo = late["wbr"], late["wo"]
    o_c = _mem_fwd(zc, mkv, name="mem_fwd")
    xhat1, rstd1, merged, pa, pb, pc = _merge_fwd(o_a, o_b, o_c, zd, x, wbr, wo, name="merge_fwd")

    dpre1, dpre2, h1, act, du, ln_stats = _mlp_loss(xhat1, rstd1, target, ln1_g, ln1_b, ln2_g, ln2_b, wu, wd, name="mlp_loss")
    ffn = ("w_down", "w_up")
    g_ffn = [_mm(act, dpre2, mode="tn", tm=1024, tn=D_MODEL, tk=tk, name="grad_w_down").reshape(N_SHARDS, D_FF // N_SHARDS, D_MODEL),
             _mm(h1, du, mode="tn", tm=D_MODEL, tn=1024, tk=tk, name="grad_w_up", out_panels=True)]

    (dzd, do_a, do_b, do_c, *g_merge), landed = _merge_bwd(dpre1, zd, pa, pb, pc, o_a, o_b, o_c, merged, wbr, wo, name="merge_bwd",
                                                           exchanges=to_sibling(g_ffn))
    sums_ffn = chip_sums(ffn, g_ffn, landed[0]) if distributed else []
    merge = BRANCH_WEIGHTS + ("w_out",)
    g_merge = [g.reshape(N_SHARDS, D_MODEL // N_SHARDS, D_MODEL) for g in g_merge]
    (dza, hg_stats), landed = _hgrn_bwd(za, o_raw, do_a, states, lb_logits, gain, name="hgrn_bwd",
                                        exchanges=to_chips(sums_ffn) + to_sibling(g_merge))
    halves = shard_sums(ffn, sums_ffn, landed[0]) if distributed else {}
    sums_merge = chip_sums(merge, g_merge, landed[1]) if distributed else []
    (dzb, dbias_t, dsinks), landed = _swa_bwd(zb, do_b, bias_tables, sink_lanes, name="swa_bwd", exchanges=to_chips(sums_merge))
    if distributed:
        halves.update(shard_sums(merge, sums_merge, landed[0]))
    dbias = dbias_t.reshape(2 * SWA_BLOCK, SWA_HEADS, SWA_BLOCK).transpose(1, 2, 0).reshape(SWA_HEADS, -1)
    d_rel_bias = _bias_grad(dbias, onehot, name="bias_grad").T
    dzc, dmkv = _mem_bwd(zc, do_c, mkv, name="mem_bwd")

    proj = ("w_in", "w_mem_kv")
    g_wi, offset = None, 0
    for dz, nm in ((dza, "grad_w_in_a"), (dzb, "grad_w_in_b"), (dzc, "grad_w_in_c"), (dzd, "grad_w_in_d")):
        g_wi = _mm(dz, xb, mode="tn", tm=dz.shape[1] if dz.shape[1] <= 1280 else 1024, tn=D_MODEL, tk=tk, name=nm,
                   rows_of=IN_COLS, row_offset=offset, into=g_wi)
        offset += dz.shape[1]
    g_proj = [g_wi.reshape(N_SHARDS, IN_COLS // N_SHARDS, D_MODEL),
              _mm(memb, dmkv, mode="tn", tm=D_MODEL, tn=512, tk=MEM_LEN, name="grad_w_mem_kv", out_panels=True)]
    sums_proj = chip_sums(proj, g_proj, _run_exchanges(to_sibling(g_proj), name="reduce_sibling_proj")[0]) if distributed else []
    dx_tm = min(512, s)
    dx_tiles = s // dx_tm
    dx_tail = dx_tiles // 5
    grad_x, landed = _dx_matmul([dza, dzb, dzc, dzd], wi_t, dpre1, tm=dx_tm, name="grad_x", tiles=(0, dx_tiles - dx_tail),
                                exchanges=to_chips(sums_proj))
    if dx_tail:
        grad_x, _ = _dx_matmul([dza, dzb, dzc, dzd], wi_t, dpre1, tm=dx_tm, name="grad_x_tail", tiles=(dx_tiles - dx_tail, dx_tail), into=grad_x)
    if distributed:
        halves.update(shard_sums(proj, sums_proj, landed[0]))
    else:
        halves = dict(zip(ffn + merge + proj, g_ffn + g_merge + g_proj))
    small = dict(lb_logits=hg_stats[1:3], hg_norm_gain=hg_stats[0:1], swa_sinks=dsinks, rel_bias=d_rel_bias,
                 ln1_g=ln_stats[2:3], ln1_b=ln_stats[3:4], ln2_g=ln_stats[0:1], ln2_b=ln_stats[1:2], sq_err=ln_stats[4:5])
    return grad_x, halves, small


def _mesh_position():
    x, y, c = lax.axis_index("x"), lax.axis_index("y"), lax.axis_index("c")
    chips = [(1 - x, y), (x, 1 - y), (1 - x, 1 - y)]
    return x, y, c, chips


class _Exchange(NamedTuple):
    operands: list
    out_shapes: list
    n_sems: int
    start: Callable
    finish: Callable


def _gather_exchange(shards):
    n = len(shards)
    per = 7

    def plan(ins, outs, send_sems, recv_sems):
        x, y, c, chips = _mesh_position()
        me = 2 * x + y
        sibling = (x, y, 1 - c)

        def half(a, slot, hc):
            rh = shards[a].shape[0] // 2
            return outs[a].at[slot, pl.ds(hc * rh, rh), :]

        def copy(a, k, src, dst, to):
            return pltpu.make_async_remote_copy(src_ref=src, dst_ref=dst, send_sem=send_sems.at[a * per + k], recv_sem=recv_sems.at[a * per + k],
                                                device_id=to, device_id_type=MESH)

        own = [copy(a, 6, ins[a], outs[a].at[me], sibling) for a in range(n)]
        to_chips = [copy(a, k, ins[a].at[pl.ds(c * (shards[a].shape[0] // 2), shards[a].shape[0] // 2), :], half(a, me, c), (cx, cy, c))
                    for k, (cx, cy) in enumerate(chips) for a in range(n)]
        arrived = [copy(a, k, half(a, 2 * cx + cy, c), half(a, 2 * cx + cy, c), (cx, cy, c)) for k, (cx, cy) in enumerate(chips) for a in range(n)]
        passed_on = [copy(a, 3 + k, half(a, 2 * cx + cy, c), half(a, 2 * cx + cy, c), sibling) for k, (cx, cy) in enumerate(chips) for a in range(n)]
        from_sibling = [copy(a, 3 + k, half(a, 2 * cx + cy, 1 - c), half(a, 2 * cx + cy, 1 - c), sibling)
                        for k, (cx, cy) in enumerate(chips) for a in range(n)]
        own_arrived = [copy(a, 6, outs[a].at[me], outs[a].at[me], sibling) for a in range(n)]
        return own, to_chips, arrived, passed_on, from_sibling, own_arrived

    def start(*refs):
        own, to_chips, _, _, _, _ = plan(*refs)
        for cp in own + to_chips:
            cp.start()

    def finish(*refs):
        own, to_chips, arrived, passed_on, from_sibling, own_arrived = plan(*refs)
        for landed, onward in zip(arrived, passed_on):
            landed.wait_recv()
            onward.start()
        for cp in from_sibling + own_arrived:
            cp.wait_recv()
        for cp in own + to_chips + passed_on:
            cp.wait_send()

    return _Exchange(list(shards), [jax.ShapeDtypeStruct((N_SHARDS,) + w.shape, w.dtype) for w in shards], per * n, start, finish)


def _sibling_halves_exchange(grads):
    n = len(grads)

    def plan(ins, outs, send_sems, recv_sems):
        x, y, c, _ = _mesh_position()
        return [pltpu.make_async_remote_copy(src_ref=ins[a].at[:, pl.ds((1 - c) * (grads[a].shape[1] // 2), grads[a].shape[1] // 2), :],
                                             dst_ref=outs[a], send_sem=send_sems.at[a], recv_sem=recv_sems.at[a],
                                             device_id=(x, y, 1 - c), device_id_type=MESH) for a in range(n)]

    def start(*refs):
        for cp in plan(*refs):
            cp.start()

    def finish(*refs):
        for cp in plan(*refs):
            cp.wait()

    return _Exchange(list(grads), [jax.ShapeDtypeStruct((g.shape[0], g.shape[1] // 2, g.shape[2]), g.dtype) for g in grads], n, start, finish)


def _chip_partials_exchange(sums):
    n = len(sums)

    def plan(ins, outs, send_sems, recv_sems):
        _, _, c, chips = _mesh_position()
        return [pltpu.make_async_remote_copy(src_ref=ins[a].at[2 * cx + cy], dst_ref=outs[a].at[k], send_sem=send_sems.at[a * 3 + k],
                                             recv_sem=recv_sems.at[a * 3 + k], device_id=(cx, cy, c), device_id_type=MESH)
                for k, (cx, cy) in enumerate(chips) for a in range(n)]

    def start(*refs):
        for cp in plan(*refs):
            cp.start()

    def finish(*refs):
        for cp in plan(*refs):
            cp.wait()

    return _Exchange(list(sums), [jax.ShapeDtypeStruct((3,) + g.shape[1:], g.dtype) for g in sums], 3 * n, start, finish)


def _fused_call(body, *, name, grid, in_specs, out_specs, out_shape, scratch_shapes, operands, exchanges=(), aliases=None):
    single = not isinstance(out_shape, (tuple, list))
    out_specs = [out_specs] if single else list(out_specs)
    out_shape = [out_shape] if single else list(out_shape)
    n_in, n_out, n_scr = len(in_specs), len(out_specs), len(scratch_shapes)
    x_in = [len(e.operands) for e in exchanges]
    x_out = [len(e.out_shapes) for e in exchanges]

    def wrapped(*refs):
        refs = list(refs)
        ins = refs[:n_in]
        pos = n_in
        ex_ins = []
        for k in x_in:
            ex_ins.append(refs[pos:pos + k])
            pos += k
        outs = refs[pos:pos + n_out]
        pos += n_out
        ex_outs = []
        for k in x_out:
            ex_outs.append(refs[pos:pos + k])
            pos += k
        scratch = refs[pos:pos + n_scr]
        sems = refs[pos + n_scr:]
        first, last = None, None
        for axis, size in enumerate(grid):
            at_start, at_end = pl.program_id(axis) == 0, pl.program_id(axis) == size - 1
            first = at_start if first is None else first & at_start
            last = at_end if last is None else last & at_end

        @pl.when(first)
        def _():
            for i, e in enumerate(exchanges):
                e.start(ex_ins[i], ex_outs[i], sems[2 * i], sems[2 * i + 1])

        body(*ins, *outs, *scratch)

        @pl.when(last)
        def _():
            for i, e in enumerate(exchanges):
                e.finish(ex_ins[i], ex_outs[i], sems[2 * i], sems[2 * i + 1])

    n_x_in, n_x_out = sum(x_in), sum(x_out)
    results = pl.pallas_call(
        wrapped if exchanges else body, name=name, grid=grid,
        in_specs=list(in_specs) + [HBM] * n_x_in,
        out_specs=out_specs + [HBM] * n_x_out,
        out_shape=out_shape + [s for e in exchanges for s in e.out_shapes],
        scratch_shapes=list(scratch_shapes) + [pltpu.SemaphoreType.DMA((e.n_sems,)) for e in exchanges for _ in range(2)],
        input_output_aliases=aliases or {}, compiler_params=_params(("arbitrary",) * len(grid)),
    )(*operands, *[a for e in exchanges for a in e.operands])
    own = results[0] if single else tuple(results[:n_out])
    landed, pos = [], n_out
    for k in x_out:
        landed.append(list(results[pos:pos + k]))
        pos += k
    return own, landed


def _run_exchanges(exchanges, *, name):
    def body(*refs):
        n_in = sum(len(e.operands) for e in exchanges)
        n_out = sum(len(e.out_shapes) for e in exchanges)
        ins, outs, sems = refs[:n_in], refs[n_in:n_in + n_out], refs[n_in + n_out:]
        spans, i, o = [], 0, 0
        for e in exchanges:
            spans.append((ins[i:i + len(e.operands)], outs[o:o + len(e.out_shapes)]))
            i, o = i + len(e.operands), o + len(e.out_shapes)
        for k, e in enumerate(exchanges):
            e.start(*spans[k], sems[2 * k], sems[2 * k + 1])
        for k, e in enumerate(exchanges):
            e.finish(*spans[k], sems[2 * k], sems[2 * k + 1])

    operands = [a for e in exchanges for a in e.operands]
    shapes = [s for e in exchanges for s in e.out_shapes]
    results = pl.pallas_call(
        body, name=name, out_shape=shapes, in_specs=[HBM] * len(operands), out_specs=[HBM] * len(shapes),
        scratch_shapes=[pltpu.SemaphoreType.DMA((e.n_sems,)) for e in exchanges for _ in range(2)],
    )(*operands)
    landed, pos = [], 0
    for e in exchanges:
        landed.append(list(results[pos:pos + len(e.out_shapes)]))
        pos += len(e.out_shapes)
    return landed


ROW_TILE_MAX = 640
BF16_SUBLANES = 16


def _row_tile(rows):
    for tr in range(min(rows, ROW_TILE_MAX), 0, -1):
        if rows % tr == 0 and tr % BF16_SUBLANES == 0:
            return tr
    raise ValueError(rows)


def _add_sibling(grad, other, pos, *, name):
    p, r, cols = grad.shape
    rh = r // 2
    tr = _row_tile(rh)
    nb = rh // tr

    def body(pos_ref, g_ref, o_ref, sb_ref, mine_ref):
        total = g_ref[...] + o_ref[...]
        sb_ref[...] = total.astype(BF16)

        @pl.when(pl.program_id(1) == pos_ref[0])
        def _():
            mine_ref[...] = total

    return pl.pallas_call(
        body, name=name, out_shape=(jax.ShapeDtypeStruct((p, rh, cols), BF16), jax.ShapeDtypeStruct((rh, cols), F32)),
        grid_spec=pltpu.PrefetchScalarGridSpec(
            num_scalar_prefetch=1, grid=(nb, p),
            in_specs=[pl.BlockSpec((None, tr, cols), lambda i, j, pos_ref: (j, pos_ref[1] * nb + i, 0)),
                      pl.BlockSpec((None, tr, cols), lambda i, j, pos_ref: (j, i, 0))],
            out_specs=(pl.BlockSpec((None, tr, cols), lambda i, j, pos_ref: (j, i, 0)),
                       pl.BlockSpec((tr, cols), lambda i, j, pos_ref: (i, 0)))),
        compiler_params=_params(("parallel", "arbitrary")),
    )(pos, grad, other)


def _add_chips(mine, others, pos, *, name):
    rh, cols = mine.shape
    tr = _row_tile(rh)
    nb = rh // tr

    def body(pos_ref, s_ref, o_ref, r_ref):
        r_ref[...] = ((s_ref[...] + o_ref[0].astype(F32)) + o_ref[1].astype(F32)) + o_ref[2].astype(F32)

    return pl.pallas_call(
        body, name=name, out_shape=jax.ShapeDtypeStruct((2 * rh, cols), F32),
        grid_spec=pltpu.PrefetchScalarGridSpec(
            num_scalar_prefetch=1, grid=(nb,),
            in_specs=[pl.BlockSpec((tr, cols), lambda i, pos_ref: (i, 0)),
                      pl.BlockSpec((3, tr, cols), lambda i, pos_ref: (0, i, 0))],
            out_specs=pl.BlockSpec((tr, cols), lambda i, pos_ref: (pos_ref[1] * nb + i, 0))),
        compiler_params=_params(("parallel",)),
    )(pos, mine, others)


def _join_halves(bufs, *, name):
    n = len(bufs)

    def body(*refs):
        ins, outs = refs[:n], refs[n:2 * n]
        send_sems, recv_sems = refs[2 * n:]
        x, y, c, _ = _mesh_position()

        def copy(a, hc):
            rh = bufs[a].shape[0] // 2
            rows = pl.ds(hc * rh, rh)
            return pltpu.make_async_remote_copy(src_ref=ins[a].at[rows, :], dst_ref=outs[a].at[rows, :], send_sem=send_sems.at[a],
                                                recv_sem=recv_sems.at[a], device_id=(x, y, 1 - c), device_id_type=MESH)

        for a in range(n):
            copy(a, c).start()
        for a in range(n):
            copy(a, c).wait_send()
            copy(a, 1 - c).wait_recv()

    return pl.pallas_call(
        body, name=name, out_shape=[jax.ShapeDtypeStruct(b.shape, b.dtype) for b in bufs],
        in_specs=[HBM] * n, out_specs=[HBM] * n, input_output_aliases={a: a for a in range(n)},
        scratch_shapes=[pltpu.SemaphoreType.DMA((n,)), pltpu.SemaphoreType.DMA((n,))],
    )(*bufs)


SMALL = ["lb_logits", "hg_norm_gain", "swa_sinks", "rel_bias", "ln1_g", "ln1_b", "ln2_g", "ln2_b"]
PACK_ROWS = 48
PACK_AT = dict(lb_logits=(slice(0, 2), slice(0, D_MODEL)), hg_norm_gain=(slice(2, 3), slice(0, D_MODEL)), ln1_g=(slice(3, 4), slice(0, D_MODEL)),
               ln1_b=(slice(4, 5), slice(0, D_MODEL)), ln2_g=(slice(5, 6), slice(0, D_MODEL)), ln2_b=(slice(6, 7), slice(0, D_MODEL)),
               swa_sinks=(slice(7, 8), slice(0, SWA_HEADS)), sq_err=(slice(8, 9), slice(0, D_MODEL)),
               rel_bias=(slice(16, 16 + NUM_BUCKETS), slice(0, SWA_HEADS)))


def _reduce_small(grads, *, name):
    names = SMALL + ["sq_err"]

    def body(*refs):
        g_refs = dict(zip(names, refs[:len(names)]))
        total_ref, packed, gathered, send_sems, recv_sems = refs[len(names):]
        x, y, c, _ = _mesh_position()
        me = 4 * x + 2 * y + c
        packed[...] = jnp.zeros_like(packed)
        for k, g_ref in g_refs.items():
            packed[PACK_AT[k]] = g_ref[...]
        gathered[me] = packed[...]
        copies = []
        for d in range(1, 8):
            dx, dy, dc = (d >> 2) & 1, (d >> 1) & 1, d & 1
            cp = pltpu.make_async_remote_copy(src_ref=packed, dst_ref=gathered.at[me], send_sem=send_sems.at[d - 1], recv_sem=recv_sems.at[d - 1],
                                              device_id=(x ^ dx, y ^ dy, c ^ dc), device_id_type=MESH)
            cp.start()
            copies.append(cp)
        for cp in copies:
            cp.wait()
        total = gathered[0]
        for j in range(1, 8):
            total = total + gathered[j]
        total_ref[...] = total

    vm = pl.BlockSpec(memory_space=pltpu.VMEM)
    return pl.pallas_call(
        body, name=name, out_shape=jax.ShapeDtypeStruct((PACK_ROWS, D_MODEL), F32), in_specs=[vm] * len(names), out_specs=vm,
        scratch_shapes=[pltpu.VMEM((PACK_ROWS, D_MODEL), F32), pltpu.VMEM((8, PACK_ROWS, D_MODEL), F32),
                        pltpu.SemaphoreType.DMA((7,)), pltpu.SemaphoreType.DMA((7,))],
    )(*[grads[k] for k in names])


def _adamw_small(total, w, m, v, *, name):
    names = SMALL
    n = len(names)

    def body(*refs):
        total_ref = refs[0]
        w_refs, m_refs, v_refs = (dict(zip(names, refs[1 + i * n:1 + (i + 1) * n])) for i in range(3))
        loss_ref = refs[1 + 3 * n]
        go_refs, d_refs, nm_refs, nv_refs = (dict(zip(names, refs[2 + (3 + i) * n:2 + (4 + i) * n])) for i in range(4))
        loss_ref[...] = (0.5 / D_MODEL) * jnp.sum(total_ref[PACK_AT["sq_err"]], axis=1, keepdims=True)
        for k in names:
            g = total_ref[PACK_AT[k]]
            go_refs[k][...] = g
            d_refs[k][...], nm_refs[k][...], nv_refs[k][...] = _adamw_math(w_refs[k][...], g, m_refs[k][...], v_refs[k][...])

    like = [jax.ShapeDtypeStruct(w[k].shape, F32) for k in names]
    results = pl.pallas_call(body, name=name, out_shape=[jax.ShapeDtypeStruct((1, 1), F32)] + like * 4,
                             compiler_params=_params())(total, *[d[k] for d in (w, m, v) for k in names])
    return results[0], {k: tuple(results[1 + i * n + j] for i in range(4)) for j, k in enumerate(names)}


def _adamw_math(w, g, m, v):
    m = ADAM_B1 * m + (1.0 - ADAM_B1) * g
    v = ADAM_B2 * v + (1.0 - ADAM_B2) * (g * g)
    m_hat = m / (1.0 - ADAM_B1 ** ADAM_STEP)
    v_hat = v / (1.0 - ADAM_B2 ** ADAM_STEP)
    delta = -ADAM_LR * (m_hat / (jnp.sqrt(v_hat) + ADAM_EPS) + ADAM_WD * w)
    return delta, m, v


def _adamw(w, g, m, v, *, name):
    _, rows, cols = w.shape
    tr = _row_tile(rows)
    blk = pl.BlockSpec((None, tr, cols), lambda i: (0, i, 0))
    flat = pl.BlockSpec((tr, cols), lambda i: (i, 0))

    def body(w_ref, g_ref, m_ref, v_ref, go_ref, d_ref, nm_ref, nv_ref):
        g_v = g_ref[...]
        go_ref[...] = g_v
        d_ref[...], nm_ref[...], nv_ref[...] = _adamw_math(w_ref[...], g_v, m_ref[...], v_ref[...])

    shape = jax.ShapeDtypeStruct((1, rows, cols), F32)
    return pl.pallas_call(body, name=name, grid=(rows // tr,), out_shape=(shape,) * 4, in_specs=[blk, flat, blk, blk], out_specs=(blk,) * 4,
                          compiler_params=_params(("parallel",)))(w, g, m, v)


WEIGHTS = ["w_in", "lb_logits", "hg_norm_gain", "swa_sinks", "rel_bias", "w_mem_kv", "w_branch_hg", "w_branch_swa", "w_branch_mem",
           "w_out", "ln1_g", "ln1_b", "w_up", "w_down", "ln2_g", "ln2_b"]
BIG = ["w_in", "w_mem_kv", "w_branch_hg", "w_branch_swa", "w_branch_mem", "w_out", "w_up", "w_down"]


def kernel(x, mem, w_in, lb_logits, hg_norm_gain, swa_sinks, rel_bias, w_mem_kv, w_branch_hg, w_branch_swa, w_branch_mem, w_out, ln1_g, ln1_b, w_up, w_down, ln2_g, ln2_b, loss_target, m_w_in, m_lb_logits, m_hg_norm_gain, m_swa_sinks, m_rel_bias, m_w_mem_kv, m_w_branch_hg, m_w_branch_swa, m_w_branch_mem, m_w_out, m_ln1_g, m_ln1_b, m_w_up, m_w_down, m_ln2_g, m_ln2_b, v_w_in, v_lb_logits, v_hg_norm_gain, v_swa_sinks, v_rel_bias, v_w_mem_kv, v_w_branch_hg, v_w_branch_swa, v_w_branch_mem, v_w_out, v_ln1_g, v_ln1_b, v_w_up, v_w_down, v_ln2_g, v_ln2_b):
    w = dict(w_in=w_in, lb_logits=lb_logits, hg_norm_gain=hg_norm_gain, swa_sinks=swa_sinks, rel_bias=rel_bias, w_mem_kv=w_mem_kv,
             w_branch_hg=w_branch_hg, w_branch_swa=w_branch_swa, w_branch_mem=w_branch_mem, w_out=w_out, ln1_g=ln1_g, ln1_b=ln1_b,
             w_up=w_up, w_down=w_down, ln2_g=ln2_g, ln2_b=ln2_b)
    m = dict(w_in=m_w_in, lb_logits=m_lb_logits, hg_norm_gain=m_hg_norm_gain, swa_sinks=m_swa_sinks, rel_bias=m_rel_bias, w_mem_kv=m_w_mem_kv,
             w_branch_hg=m_w_branch_hg, w_branch_swa=m_w_branch_swa, w_branch_mem=m_w_branch_mem, w_out=m_w_out, ln1_g=m_ln1_g, ln1_b=m_ln1_b,
             w_up=m_w_up, w_down=m_w_down, ln2_g=m_ln2_g, ln2_b=m_ln2_b)
    v = dict(w_in=v_w_in, lb_logits=v_lb_logits, hg_norm_gain=v_hg_norm_gain, swa_sinks=v_swa_sinks, rel_bias=v_rel_bias, w_mem_kv=v_w_mem_kv,
             w_branch_hg=v_w_branch_hg, w_branch_swa=v_w_branch_swa, w_branch_mem=v_w_branch_mem, w_out=v_w_out, ln1_g=v_ln1_g, ln1_b=v_ln1_b,
             w_up=v_w_up, w_down=v_w_down, ln2_g=v_ln2_g, ln2_b=v_ln2_b)
    shapes = {k: w[k].shape for k in WEIGHTS}
    for d in (w, m, v):
        d["w_in"] = d["w_in"].reshape(D_MODEL, IN_COLS // N_SHARDS).T[None]
    shards = {k: w[k].reshape(w[k].shape[-2], w[k].shape[-1]).astype(BF16) for k in BIG}
    wi4, wmkv = _run_exchanges([_gather_exchange([shards["w_in"], shards["w_mem_kv"]])], name="gather_weights")[0]
    wi_t = wi4.reshape(IN_COLS, D_MODEL)

    grad_x, halves, small = _local_step(
        x.reshape(x.shape[-2], D_MODEL), mem.reshape(MEM_LEN, D_MODEL), loss_target.reshape(loss_target.shape[-2], D_MODEL),
        wi_t, wmkv, shards, lb_logits, hg_norm_gain, swa_sinks, rel_bias, ln1_g, ln1_b, ln2_g, ln2_b, distributed=True)

    reduced = dict(zip(BIG, _join_halves([halves[k] for k in BIG], name="join_halves")))

    outs = {k: _adamw(w[k], reduced[k], m[k], v[k], name="adamw_" + k) for k in BIG}
    loss, small_outs = _adamw_small(_reduce_small(small, name="reduce_small"), w, m, v, name="adamw_small")
    outs.update(small_outs)
    grad_out, delta_out, m_out, v_out = ({k: outs[k][i] for k in WEIGHTS} for i in range(4))
    for out in (grad_out, delta_out, m_out, v_out):
        out["w_in"] = out["w_in"][0].T

    result = [loss.reshape(()), grad_x.reshape(x.shape)]
    for out in (grad_out, delta_out, m_out, v_out):
        result += [out[k].reshape(shapes[k]) for k in WEIGHTS]
    return tuple(result)
```

```python
import math
from typing import Callable, NamedTuple

import jax
import jax.numpy as jnp
from jax import lax
from jax.experimental import pallas as pl
from jax.experimental.pallas import tpu as pltpu

F32 = jnp.float32
BF16 = jnp.bfloat16
HIGHEST = lax.Precision.HIGHEST
MESH = pl.DeviceIdType.MESH

D_MODEL = 1024
MEM_LEN = 256
HG_HEADS = 8
HG_DK = 128
HG_CHUNK = 64
SWA_HEADS = 16
SWA_KV_HEADS = 2
SWA_GROUP = 8
SWA_HEAD_DIM = 64
SWA_BLOCK = 128
SWA_WINDOW = 128
MEM_HEADS = 4
MEM_HEAD_DIM = 256
NUM_BUCKETS = 32
MAX_DISTANCE = 128
D_FF = 4096
LN_EPS = 1e-5
RMS_EPS = 1e-6
ALPHA = 2.0 ** 0.25
W_A, W_B, W_C, W_D = 4096, 1280, 1024, 3072
IN_COLS = W_A + W_B + W_C + W_D
N_SHARDS = 4
ADAM_LR = 0.001
ADAM_B1 = 0.9
ADAM_B2 = 0.999
ADAM_EPS = 1e-08
ADAM_WD = 0.01
ADAM_STEP = 10
MASK_VALUE = -1e30
VMEM_LIMIT = 56 * 1024 * 1024

NN = ((1,), (0,))
NT = ((1,), (1,))
TN = ((0,), (0,))
HBM = pl.BlockSpec(memory_space=pltpu.HBM)


def _dot(a, b, dims=NN, precision=None):
    return lax.dot_general(a, b, (dims, ((), ())), precision=precision, preferred_element_type=F32)


def _params(sem=None):
    return pltpu.CompilerParams(dimension_semantics=sem, vmem_limit_bytes=VMEM_LIMIT)


def _resident(shape):
    zeros = (0,) * len(shape)
    return pl.BlockSpec(shape, lambda *_: zeros, pipeline_mode=pl.Buffered(1))


def _resident_rows(arr, offset, rows):
    return pl.BlockSpec((pl.Element(rows), pl.Element(arr.shape[1])), lambda *_: (offset, 0), pipeline_mode=pl.Buffered(1))


def _mm(a, b, *, mode, tm, tn, tk, name, out_dtype=F32, b_panels=False, b_rows=None, out_panels=False, rows_of=None, row_offset=0,
        into=None):
    if mode == "tn":
        kdim, m = a.shape
    else:
        m, kdim = a.shape
    if b_panels:
        n = b.shape[0] * b.shape[2]
        assert b.shape[2] == tn and mode == "nn"
    elif b_rows is not None:
        assert mode == "nt"
        b_offset, n = b_rows
    elif mode == "nt":
        n = b.shape[0]
    else:
        n = b.shape[1]
    assert m % tm == 0 and n % tn == 0 and kdim % tk == 0, (name, m, n, kdim)
    nk = kdim // tk
    dims = {"nn": NN, "nt": NT, "tn": TN}[mode]
    a_spec = pl.BlockSpec((tk, tm), lambda i, j, k: (k, i)) if mode == "tn" else pl.BlockSpec((tm, tk), lambda i, j, k: (i, k))
    if b_panels:
        b_spec = pl.BlockSpec((None, tk, tn), lambda i, j, k: (j, k, 0))
    elif b_rows is not None:
        assert b_offset % BF16_SUBLANES == 0 and tn % BF16_SUBLANES == 0 and tk % 128 == 0
        b_spec = pl.BlockSpec((pl.Element(tn), pl.Element(tk)),
                              lambda i, j, k: (pl.multiple_of(b_offset + j * tn, BF16_SUBLANES), pl.multiple_of(k * tk, 128)))
    elif mode == "nt":
        b_spec = pl.BlockSpec((tn, tk), lambda i, j, k: (j, k))
    else:
        b_spec = pl.BlockSpec((tk, tn), lambda i, j, k: (k, j))
    in_specs = [a_spec, b_spec]
    operands = [a, b]
    aliases = {}
    if out_panels:
        out_shape = jax.ShapeDtypeStruct((n // tn, m, tn), out_dtype)
        o_spec = pl.BlockSpec((None, tm, tn), lambda i, j, k: (j, i, 0))
    elif rows_of is not None:
        out_shape = jax.ShapeDtypeStruct((rows_of, n), out_dtype)
        assert row_offset % BF16_SUBLANES == 0 and tm % BF16_SUBLANES == 0 and tn % 128 == 0
        o_spec = pl.BlockSpec((pl.Element(tm), pl.Element(tn)),
                              lambda i, j, k: (pl.multiple_of(row_offset + i * tm, BF16_SUBLANES), pl.multiple_of(j * tn, 128)))
        if into is not None:
            in_specs.append(pl.BlockSpec(memory_space=pl.ANY))
            operands.append(into)
            aliases = {2: 0}
    else:
        out_shape = jax.ShapeDtypeStruct((m, n), out_dtype)
        o_spec = pl.BlockSpec((tm, tn), lambda i, j, k: (i, j))
    n_in = len(operands)

    def body(*refs):
        a_ref, b_ref, o_ref = refs[0], refs[1], refs[n_in]
        part = _dot(a_ref[...].astype(BF16), b_ref[...].astype(BF16), dims)

        def finish(acc):
            o_ref[...] = acc.astype(out_dtype)

        if nk == 1:
            finish(part)
        else:
            acc_ref = refs[-1]
            k = pl.program_id(2)

            @pl.when(k == 0)
            def _():
                acc_ref[...] = part

            @pl.when(k > 0)
            def _():
                acc_ref[...] += part

            @pl.when(k == nk - 1)
            def _():
                finish(acc_ref[...])

    return pl.pallas_call(
        body, name=name, out_shape=out_shape, grid=(m // tm, n // tn, nk), in_specs=in_specs, out_specs=o_spec,
        scratch_shapes=[pltpu.VMEM((tm, tn), F32)] if nk > 1 else [], input_output_aliases=aliases,
        compiler_params=_params(("parallel", "parallel", "arbitrary")),
    )(*operands)


def _dx_matmul(dzs, wi_t, resid, *, tm, name, tiles, into=None, exchanges=()):
    s = resid.shape[0]
    npieces = len(dzs)
    offsets = [sum(dz.shape[1] for dz in dzs[:p]) for p in range(npieces)]
    first, count = tiles
    tile = lambda i: (first + i, 0)
    in_specs = [pl.BlockSpec((tm, dz.shape[1]), tile) for dz in dzs] + [_resident(wi_t.shape), pl.BlockSpec((tm, D_MODEL), tile)]
    operands = [*dzs, wi_t, resid]
    if into is not None:
        in_specs.append(pl.BlockSpec(memory_space=pl.ANY))
        operands.append(into)
    n_in = len(operands)

    def body(*refs):
        dz_refs, w_ref, r_ref, o_ref = refs[:npieces], refs[npieces], refs[npieces + 1], refs[n_in]
        total = ALPHA * r_ref[...]
        for p in range(npieces):
            total = total + _dot(dz_refs[p][...], w_ref[offsets[p]:offsets[p] + dzs[p].shape[1], :], NN)
        o_ref[...] = total

    return _fused_call(
        body, name=name, out_shape=jax.ShapeDtypeStruct((s, D_MODEL), F32), grid=(count,), in_specs=in_specs,
        out_specs=pl.BlockSpec((tm, D_MODEL), tile), scratch_shapes=[], operands=operands, exchanges=exchanges,
        aliases={n_in - 1: 0} if into is not None else None)


def _lower_bound(lbl_ref):
    l0, l1 = lbl_ref[0:1, :], lbl_ref[1:2, :]
    mx = jnp.maximum(l0, l1)
    e0, e1 = jnp.exp(l0 - mx), jnp.exp(l1 - mx)
    return e0 / (e0 + e1)


HEAD_COLS = [slice(h * HG_DK, (h + 1) * HG_DK) for h in range(HG_HEADS)]


def _head_mean(x):
    return jnp.concatenate([jnp.broadcast_to(jnp.mean(x[:, c], axis=-1, keepdims=True), (x.shape[0], HG_DK)) for c in HEAD_COLS], axis=1)


def _triangle_sum(tri_b, x):
    p0 = x.astype(BF16)
    r1 = x - p0.astype(F32)
    p1 = r1.astype(BF16)
    p2 = (r1 - p1.astype(F32)).astype(BF16)
    return _dot(tri_b, p0) + _dot(tri_b, p1) + _dot(tri_b, p2)


def _chunk_forward(q, fl, v, lb, tril_b):
    sg = jax.nn.sigmoid(fl)
    f = lb + (1.0 - lb) * sg
    k = 1.0 - f
    b = _triangle_sum(tril_b, jnp.log(f))
    b_last = b[HG_CHUNK - 1:HG_CHUNK, :]
    eb, enb, eo = jnp.exp(b), jnp.exp(-b), jnp.exp(b_last - b)
    return sg, f, k, b_last, eb, enb, eo, q * eb, k * enb, k * eo


def _hgrn_fwd(za, lb_logits, gain, *, name, exchanges=()):
    s = za.shape[0]
    t = min(256, s)
    ncs = t // HG_CHUNK

    def body(z_ref, lbl_ref, gain_ref, oa_ref, oraw_ref, st_ref, state):
        @pl.when(pl.program_id(0) == 0)
        def _():
            state[...] = jnp.zeros_like(state)

        lb_all = _lower_bound(lbl_ref)
        row = lax.broadcasted_iota(jnp.int32, (HG_CHUNK, HG_CHUNK), 0)
        col = lax.broadcasted_iota(jnp.int32, (HG_CHUNK, HG_CHUNK), 1)
        tril = row >= col
        tril_b = tril.astype(BF16)
        gain_all = gain_ref[...]

        def chunk(i, carry):
            r = pl.ds(pl.multiple_of(i * HG_CHUNK, HG_CHUNK), HG_CHUNK)
            q, fl, v, hg = (z_ref[r, j * D_MODEL:(j + 1) * D_MODEL] for j in range(4))
            _, _, _, b_last, _, _, _, q_in, k_in, k_out = _chunk_forward(q, fl, v, lb_all, tril_b)
            q_in_b, k_in_b, k_out_b, vb = (u.astype(BF16) for u in (q_in, k_in, k_out, v))
            decay = jnp.exp(b_last)
            sts = [state[h] for h in range(HG_HEADS)]
            attn = [_dot(q_in_b[:, c], k_in_b[:, c], NT) for c in HEAD_COLS]
            inter = [_dot(q_in_b[:, c], sts[h].astype(BF16), NT) for h, c in enumerate(HEAD_COLS)]
            upd = [_dot(vb[:, c], k_out_b[:, c], TN) for c in HEAD_COLS]
            attn = [jnp.where(tril, a, 0.0).astype(BF16) for a in attn]
            outs = [_dot(attn[h], vb[:, c], NN) + inter[h] for h, c in enumerate(HEAD_COLS)]
            for h, c in enumerate(HEAD_COLS):
                st_ref[h, i] = sts[h]
                state[h] = sts[h] * decay[:, c] + upd[h]
            o = jnp.concatenate(outs, axis=1)
            oraw_ref[r, :] = o
            n = o * lax.rsqrt(_head_mean(o * o) + RMS_EPS)
            oa_ref[r, :] = (n * gain_all * (hg * jax.nn.sigmoid(hg))).astype(BF16)
            return carry

        lax.fori_loop(0, ncs, chunk, 0, unroll=True)

    return _fused_call(
        body, name=name, grid=(s // t,),
        out_shape=(jax.ShapeDtypeStruct((s, D_MODEL), BF16), jax.ShapeDtypeStruct((s, D_MODEL), F32),
                   jax.ShapeDtypeStruct((HG_HEADS, s // HG_CHUNK, HG_DK, HG_DK), F32)),
        in_specs=[pl.BlockSpec((t, W_A), lambda i: (i, 0)), _resident((2, D_MODEL)), _resident((1, D_MODEL))],
        out_specs=(pl.BlockSpec((t, D_MODEL), lambda i: (i, 0)), pl.BlockSpec((t, D_MODEL), lambda i: (i, 0)),
                   pl.BlockSpec((HG_HEADS, ncs, HG_DK, HG_DK), lambda i: (0, i, 0, 0))),
        scratch_shapes=[pltpu.VMEM((HG_HEADS, HG_DK, HG_DK), F32)],
        operands=[za, lb_logits, gain], exchanges=exchanges)


def _hgrn_bwd(za, oraw, do_a, states, lb_logits, gain, *, name, exchanges=()):
    s = za.shape[0]
    t = min(256, s)
    ncs = t // HG_CHUNK
    nt = s // t

    def body(z_ref, oraw_ref, do_ref, st_ref, lbl_ref, gain_ref, dz_ref, stats_ref, dstate):
        step = pl.program_id(0)

        @pl.when(step == 0)
        def _():
            dstate[...] = jnp.zeros_like(dstate)
            stats_ref[...] = jnp.zeros_like(stats_ref)

        lb_all = _lower_bound(lbl_ref)
        row = lax.broadcasted_iota(jnp.int32, (HG_CHUNK, HG_CHUNK), 0)
        col = lax.broadcasted_iota(jnp.int32, (HG_CHUNK, HG_CHUNK), 1)
        tril = row >= col
        tril_b = tril.astype(BF16)
        triu_b = (row <= col).astype(BF16)
        gain_all = gain_ref[...]

        def chunk(ii, carry):
            i = ncs - 1 - ii
            r = pl.ds(pl.multiple_of(i * HG_CHUNK, HG_CHUNK), HG_CHUNK)
            q, fl, v, hg = (z_ref[r, j * D_MODEL:(j + 1) * D_MODEL] for j in range(4))
            o = oraw_ref[r, :]
            doa = do_ref[r, :]
            rms = lax.rsqrt(_head_mean(o * o) + RMS_EPS)
            n = o * rms
            sgg = jax.nn.sigmoid(hg)
            silu = hg * sgg
            dhg = doa * n * gain_all * (sgg * (1.0 + hg * (1.0 - sgg)))
            dgain = jnp.sum(doa * n * silu, axis=0, keepdims=True)
            dn = doa * gain_all * silu
            do = rms * (dn - n * _head_mean(dn * n))
            sg, f, k, b_last, eb, enb, eo, q_in, k_in, k_out = _chunk_forward(q, fl, v, lb_all, tril_b)
            q_in_b, k_in_b, k_out_b, vb, dob = (u.astype(BF16) for u in (q_in, k_in, k_out, v, do))
            decay = jnp.exp(b_last)
            sts = [st_ref[h, i] for h in range(HG_HEADS)]
            dsts = [dstate[h] for h in range(HG_HEADS)]
            dsts_b = [d.astype(BF16) for d in dsts]
            heads = list(enumerate(HEAD_COLS))
            attn = [_dot(q_in_b[:, c], k_in_b[:, c], NT) for h, c in heads]
            dattn = [_dot(dob[:, c], vb[:, c], NT) for h, c in heads]
            dq_st = [_dot(dob[:, c], sts[h].astype(BF16), NN) for h, c in heads]
            dk_out = [_dot(vb[:, c], dsts_b[h], NN) for h, c in heads]
            dv_st = [_dot(k_out_b[:, c], dsts_b[h], NT) for h, c in heads]
            dst_o = [_dot(dob[:, c], q_in_b[:, c], TN) for h, c in heads]
            attn = [jnp.where(tril, a, 0.0).astype(BF16) for a in attn]
            dattn = [jnp.where(tril, a, 0.0).astype(BF16) for a in dattn]
            dq_in = jnp.concatenate([_dot(dattn[h], k_in_b[:, c], NN) + dq_st[h] for h, c in heads], axis=1)
            dk_in = jnp.concatenate([_dot(dattn[h], q_in_b[:, c], TN) for h, c in heads], axis=1)
            dv = jnp.concatenate([_dot(attn[h], dob[:, c], TN) + dv_st[h] for h, c in heads], axis=1)
            dk_out = jnp.concatenate(dk_out, axis=1)
            dst_st = jnp.concatenate([jnp.sum(dsts[h] * sts[h], axis=0, keepdims=True) for h in range(HG_HEADS)], axis=1)
            for h, c in heads:
                dstate[h] = dsts[h] * decay[:, c] + dst_o[h]
            db_last = decay * dst_st + jnp.sum(dk_out * k_out, axis=0, keepdims=True)
            db = dq_in * q_in - dk_in * k_in - dk_out * k_out
            dg = _triangle_sum(triu_b, db) + db_last
            dk = dk_in * enb + dk_out * eo
            df = dg / f - dk
            stats_ref[0:1, :] += dgain
            stats_ref[1:2, :] += jnp.sum(df * (1.0 - sg), axis=0, keepdims=True)
            dz_ref[r, 0:1024] = (dq_in * eb).astype(BF16)
            dz_ref[r, 1024:2048] = (df * (1.0 - lb_all) * sg * (1.0 - sg)).astype(BF16)
            dz_ref[r, 2048:3072] = dv.astype(BF16)
            dz_ref[r, 3072:4096] = dhg.astype(BF16)
            return carry

        lax.fori_loop(0, ncs, chunk, 0, unroll=True)

        @pl.when(step == nt - 1)
        def _():
            dl0 = stats_ref[1:2, :] * lb_all * (1.0 - lb_all)
            stats_ref[1:2, :] = dl0
            stats_ref[2:3, :] = -dl0

    rev = lambda i: (nt - 1 - i, 0)
    return _fused_call(
        body, name=name, grid=(nt,),
        out_shape=(jax.ShapeDtypeStruct((s, W_A), BF16), jax.ShapeDtypeStruct((8, D_MODEL), F32)),
        in_specs=[pl.BlockSpec((t, W_A), rev), pl.BlockSpec((t, D_MODEL), rev), pl.BlockSpec((t, D_MODEL), rev),
                  pl.BlockSpec((HG_HEADS, ncs, HG_DK, HG_DK), lambda i: (0, nt - 1 - i, 0, 0)),
                  _resident((2, D_MODEL)), _resident((1, D_MODEL))],
        out_specs=(pl.BlockSpec((t, W_A), rev), pl.BlockSpec((8, D_MODEL), lambda i: (0, 0))),
        scratch_shapes=[pltpu.VMEM((HG_HEADS, HG_DK, HG_DK), F32)],
        operands=[za, oraw, do_a, states, lb_logits, gain], exchanges=exchanges)


def _t5_bucket(n):
    max_exact = NUM_BUCKETS // 2
    nf = jnp.maximum(n, 1).astype(F32)
    large = max_exact + (jnp.log(nf / max_exact) / math.log(MAX_DISTANCE / max_exact) * (NUM_BUCKETS - max_exact)).astype(jnp.int32)
    large = jnp.minimum(large, NUM_BUCKETS - 1)
    return jnp.where(n < max_exact, n, large)


def _bias_selector():
    qi = jnp.arange(SWA_BLOCK)[:, None] + SWA_BLOCK
    kj = jnp.arange(2 * SWA_BLOCK)[None, :]
    dist = qi - kj
    band = ((dist >= 0) & (dist < SWA_WINDOW)).reshape(1, -1)
    bucket = _t5_bucket(jnp.clip(dist, 0, SWA_WINDOW - 1)).reshape(1, -1)
    onehot = ((bucket == jnp.arange(NUM_BUCKETS)[:, None]) & band).astype(F32)
    return onehot, jnp.where(band, 0.0, MASK_VALUE).astype(F32)


def _bias_table(rel_bias_t, onehot, maskrow, *, name):
    def body(rb_ref, oh_ref, mask_ref, o_ref):
        o_ref[...] = _dot(rb_ref[...], oh_ref[...], NN, HIGHEST) + mask_ref[...]

    return pl.pallas_call(body, name=name, out_shape=jax.ShapeDtypeStruct((SWA_HEADS, onehot.shape[1]), F32),
                          compiler_params=_params())(rel_bias_t, onehot, maskrow)


def _bias_grad(dbias2d, onehot, *, name):
    def body(db_ref, oh_ref, o_ref):
        o_ref[...] = _dot(db_ref[...], oh_ref[...], NT, HIGHEST)

    return pl.pallas_call(body, name=name, out_shape=jax.ShapeDtypeStruct((SWA_HEADS, NUM_BUCKETS), F32),
                          compiler_params=_params())(dbias2d, onehot)


GROUP_LANES = SWA_GROUP * SWA_BLOCK


def _swa_operands(zq_ref, kv_cur_ref, kv_prev_ref):
    q = (zq_ref[:, 0:1024] * (SWA_HEAD_DIM ** -0.5)).astype(BF16)
    kv_c = kv_cur_ref[...].astype(BF16)
    kv_p = kv_prev_ref[...].astype(BF16)
    kks = [jnp.concatenate([kv_p[:, g * 64:(g + 1) * 64], kv_c[:, g * 64:(g + 1) * 64]], axis=0) for g in range(SWA_KV_HEADS)]
    vvs = [jnp.concatenate([kv_p[:, 128 + g * 64:128 + (g + 1) * 64], kv_c[:, 128 + g * 64:128 + (g + 1) * 64]], axis=0)
           for g in range(SWA_KV_HEADS)]
    return q, kks, vvs


SWA_PART_HEADS = 8
SWA_PARTS = [(h0 // SWA_GROUP, h0) for h0 in range(0, SWA_HEADS, SWA_PART_HEADS)]


def _part_lanes(h0):
    return slice(h0 * SWA_BLOCK, (h0 + SWA_PART_HEADS) * SWA_BLOCK)


def _stack_heads(x, h0):
    return jnp.concatenate([x[:, h * SWA_HEAD_DIM:(h + 1) * SWA_HEAD_DIM] for h in range(h0, h0 + SWA_PART_HEADS)], axis=0)


def _heads_to_lanes(xt):
    pairs = []
    for j in range(0, xt.shape[1] // SWA_BLOCK, 2):
        two = jnp.concatenate([xt[:, j * SWA_BLOCK:(j + 1) * SWA_BLOCK], xt[:, (j + 1) * SWA_BLOCK:(j + 2) * SWA_BLOCK]], axis=0)
        pairs.append(two.T)
    return jnp.concatenate(pairs, axis=1)


def _swa_softmax(score_t, bias_ref, sink_ref, h0):
    sc = score_t + bias_ref[:, _part_lanes(h0)]
    sink = sink_ref[:, _part_lanes(h0)]
    m = jnp.maximum(jnp.max(sc, axis=0, keepdims=True), sink)
    e = jnp.exp(sc - m)
    e_sink = jnp.exp(sink - m)
    return e, 1.0 / (jnp.sum(e, axis=0, keepdims=True) + e_sink), e_sink


def _swa_tables(bias2d, sinks):
    bias_t = bias2d.reshape(SWA_HEADS, SWA_BLOCK, 2 * SWA_BLOCK).transpose(2, 0, 1).reshape(2 * SWA_BLOCK, SWA_HEADS * SWA_BLOCK)
    first = jnp.where(jnp.arange(2 * SWA_BLOCK)[:, None] < SWA_BLOCK, MASK_VALUE, bias_t)
    return jnp.stack([first, bias_t]), jnp.repeat(sinks, SWA_BLOCK, axis=1)


def _swa_fwd(zb, bias_tables, sink_lanes, *, name, exchanges=()):
    s = zb.shape[0]
    nb = s // SWA_BLOCK

    def body(zq_ref, kvc_ref, kvp_ref, bias_ref, sink_ref, o_ref):
        q, kks, vvs = _swa_operands(zq_ref, kvc_ref, kvp_ref)
        scores = [_dot(kks[g], _stack_heads(q, h0), NT) for g, h0 in SWA_PARTS]
        probs = []
        for score, (_, h0) in zip(scores, SWA_PARTS):
            e, inv, _ = _swa_softmax(score, bias_ref, sink_ref, h0)
            probs.append((e * inv).astype(BF16))
        outs = [_dot(vvs[g], p, TN) for p, (g, _) in zip(probs, SWA_PARTS)]
        o_ref[...] = jnp.concatenate([_heads_to_lanes(o) for o in outs], axis=1).astype(BF16)

    return _fused_call(
        body, name=name, grid=(nb,), out_shape=jax.ShapeDtypeStruct((s, D_MODEL), BF16),
        in_specs=[pl.BlockSpec((SWA_BLOCK, W_B), lambda n: (n, 0)),
                  pl.BlockSpec((SWA_BLOCK, 256), lambda n: (n, 4)),
                  pl.BlockSpec((SWA_BLOCK, 256), lambda n: (jnp.maximum(n - 1, 0), 4)),
                  pl.BlockSpec((None, 2 * SWA_BLOCK, SWA_HEADS * SWA_BLOCK), lambda n: (jnp.minimum(n, 1), 0, 0)),
                  _resident((1, SWA_HEADS * SWA_BLOCK))],
        out_specs=pl.BlockSpec((SWA_BLOCK, D_MODEL), lambda n: (n, 0)), scratch_shapes=[],
        operands=[zb, zb, zb, bias_tables, sink_lanes], exchanges=exchanges)


def _swa_bwd(zb, do_b, bias_tables, sink_lanes, *, name, exchanges=()):
    s = zb.shape[0]
    nb = s // SWA_BLOCK
    scale = SWA_HEAD_DIM ** -0.5

    def body(zq_ref, kvc_ref, kvp_ref, do_ref, bias_ref, sink_ref, dz_ref, dbias_ref, dsink_ref, carry, dsink_acc):
        step = pl.program_id(0)

        @pl.when(step == 0)
        def _():
            carry[...] = jnp.zeros_like(carry)
            dsink_acc[...] = jnp.zeros_like(dsink_acc)
            dbias_ref[...] = jnp.zeros_like(dbias_ref)

        q, kks, vvs = _swa_operands(zq_ref, kvc_ref, kvp_ref)
        do = do_ref[...].astype(BF16)
        parts = range(len(SWA_PARTS))
        q_rows = [_stack_heads(q, h0) for _, h0 in SWA_PARTS]
        do_rows = [_stack_heads(do, h0) for _, h0 in SWA_PARTS]
        scores = [_dot(kks[g], q_rows[i], NT) for i, (g, _) in enumerate(SWA_PARTS)]
        dps = [_dot(vvs[g], do_rows[i], NT) for i, (g, _) in enumerate(SWA_PARTS)]
        ps, dss = [], []
        for i, (_, h0) in enumerate(SWA_PARTS):
            e, inv, e_sink = _swa_softmax(scores[i], bias_ref, sink_ref, h0)
            p = e * inv
            delta = jnp.sum(p * dps[i], axis=0, keepdims=True)
            ds = p * (dps[i] - delta)
            dbias_ref[:, _part_lanes(h0)] += ds
            dsink_acc[:, _part_lanes(h0)] -= e_sink * inv * delta
            ps.append(p.astype(BF16))
            dss.append(ds.astype(BF16))
        dqs = [_dot(kks[g], dss[i], TN) * scale for i, (g, _) in enumerate(SWA_PARTS)]
        in_group = lambda xs, g, axis: jnp.concatenate([xs[i] for i in parts if SWA_PARTS[i][0] == g], axis=axis)
        dkks = [_dot(in_group(dss, g, 1), in_group(q_rows, g, 0), NN) for g in range(SWA_KV_HEADS)]
        dvvs = [_dot(in_group(ps, g, 1), in_group(do_rows, g, 0), NN) for g in range(SWA_KV_HEADS)]
        dkv = jnp.concatenate(dkks + dvvs, axis=1)
        dz_ref[:, 0:1024] = jnp.concatenate([_heads_to_lanes(dq) for dq in dqs], axis=1).astype(BF16)
        dz_ref[:, 1024:1280] = (dkv[SWA_BLOCK:, :] + carry[...]).astype(BF16)
        carry[...] = dkv[:SWA_BLOCK, :]

        @pl.when(step == nb - 1)
        def _():
            acc = dsink_acc[...]
            dsink_ref[...] = jnp.concatenate([jnp.sum(acc[:, h * SWA_BLOCK:(h + 1) * SWA_BLOCK], axis=1, keepdims=True)
                                              for h in range(SWA_HEADS)], axis=1)

    rev = lambda i: (nb - 1 - i, 0)
    table_shape = (2 * SWA_BLOCK, SWA_HEADS * SWA_BLOCK)
    return _fused_call(
        body, name=name, grid=(nb,),
        out_shape=(jax.ShapeDtypeStruct((s, W_B), BF16), jax.ShapeDtypeStruct(table_shape, F32), jax.ShapeDtypeStruct((1, SWA_HEADS), F32)),
        in_specs=[pl.BlockSpec((SWA_BLOCK, W_B), rev),
                  pl.BlockSpec((SWA_BLOCK, 256), lambda i: (nb - 1 - i, 4)),
                  pl.BlockSpec((SWA_BLOCK, 256), lambda i: (jnp.maximum(nb - 2 - i, 0), 4)),
                  pl.BlockSpec((SWA_BLOCK, D_MODEL), rev),
                  pl.BlockSpec((None,) + table_shape, lambda i: (jnp.minimum(nb - 1 - i, 1), 0, 0)),
                  _resident((1, SWA_HEADS * SWA_BLOCK))],
        out_specs=(pl.BlockSpec((SWA_BLOCK, W_B), rev), pl.BlockSpec(table_shape, lambda i: (0, 0)),
                   pl.BlockSpec((1, SWA_HEADS), lambda i: (0, 0))),
        scratch_shapes=[pltpu.VMEM((SWA_BLOCK, 256), F32), pltpu.VMEM((1, SWA_HEADS * SWA_BLOCK), F32)],
        operands=[zb, zb, zb, do_b, bias_tables, sink_lanes], exchanges=exchanges)


MEM_COLS = [slice(h * MEM_HEAD_DIM, (h + 1) * MEM_HEAD_DIM) for h in range(MEM_HEADS)]
MEM_VCOLS = [slice(D_MODEL + h * MEM_HEAD_DIM, D_MODEL + (h + 1) * MEM_HEAD_DIM) for h in range(MEM_HEADS)]


def _mem_probs(zc_ref, mkv_ref):
    qs = [(zc_ref[:, c] * (MEM_HEAD_DIM ** -0.5)).astype(BF16) for c in MEM_COLS]
    scores = [_dot(qs[h], mkv_ref[:, c], NT) for h, c in enumerate(MEM_COLS)]
    ps = []
    for sc in scores:
        e = jnp.exp(sc - jnp.max(sc, axis=-1, keepdims=True))
        ps.append(e / jnp.sum(e, axis=-1, keepdims=True))
    return qs, ps


def _mem_fwd(xb, wi_t, mkv, *, name):
    s = xb.shape[0]
    t = min(512, s)

    def body(x_ref, w_ref, mkv_ref, zc_ref, o_ref):
        zc_ref[...] = _dot(x_ref[...], w_ref[...], NT).astype(BF16)
        _, ps = _mem_probs(zc_ref, mkv_ref)
        ps = [p.astype(BF16) for p in ps]
        o_ref[...] = jnp.concatenate([_dot(ps[h], mkv_ref[:, vc], NN) for h, vc in enumerate(MEM_VCOLS)], axis=1).astype(BF16)

    row = pl.BlockSpec((t, D_MODEL), lambda i: (i, 0))
    return pl.pallas_call(
        body, name=name, grid=(s // t,), out_shape=(jax.ShapeDtypeStruct((s, D_MODEL), BF16),) * 2,
        in_specs=[row, _resident_rows(wi_t, W_A + W_B, W_C), _resident((MEM_LEN, 2 * D_MODEL))],
        out_specs=(row, row), compiler_params=_params(("parallel",)),
    )(xb, wi_t, mkv)


def _mem_bwd(xb, zc, do_c, mkv, *, name):
    s = zc.shape[0]
    t = min(512, s)
    nt = s // t

    def body(x_ref, zc_ref, do_ref, mkv_ref, dz_ref, dmkv_ref, gwi_ref, acc):
        @pl.when(pl.program_id(0) == 0)
        def _():
            dmkv_ref[...] = jnp.zeros_like(dmkv_ref)
            acc[...] = jnp.zeros_like(acc)

        heads = range(MEM_HEADS)
        qs, ps = _mem_probs(zc_ref, mkv_ref)
        dos = [do_ref[:, c].astype(BF16) for c in MEM_COLS]
        dps = [_dot(dos[h], mkv_ref[:, MEM_VCOLS[h]], NT) for h in heads]
        dss = [(ps[h] * (dps[h] - jnp.sum(ps[h] * dps[h], axis=-1, keepdims=True))).astype(BF16) for h in heads]
        ps = [p.astype(BF16) for p in ps]
        dz = jnp.concatenate([_dot(dss[h], mkv_ref[:, MEM_COLS[h]], NN) * (MEM_HEAD_DIM ** -0.5) for h in heads], axis=1).astype(BF16)
        dz_ref[...] = dz
        dmkv_ref[...] += jnp.concatenate([_dot(dss[h], qs[h], TN) for h in heads] + [_dot(ps[h], dos[h], TN) for h in heads], axis=1)
        acc[...] += _dot(dz, x_ref[...], TN)

        @pl.when(pl.program_id(0) == nt - 1)
        def _():
            pltpu.sync_copy(acc, gwi_ref.at[pl.ds(W_A + W_B, W_C), :])

    row = pl.BlockSpec((t, D_MODEL), lambda i: (i, 0))
    return pl.pallas_call(
        body, name=name, grid=(nt,),
        out_shape=(jax.ShapeDtypeStruct((s, D_MODEL), BF16), jax.ShapeDtypeStruct((MEM_LEN, 2 * D_MODEL), F32),
                   jax.ShapeDtypeStruct((IN_COLS, D_MODEL), F32)),
        in_specs=[row, row, row, _resident((MEM_LEN, 2 * D_MODEL))],
        out_specs=(row, pl.BlockSpec((MEM_LEN, 2 * D_MODEL), lambda i: (0, 0)), HBM),
        scratch_shapes=[pltpu.VMEM((W_C, D_MODEL), F32)],
        compiler_params=_params(("arbitrary",)),
    )(xb, zc, do_c, mkv)


def _normalize(pre):
    mu = jnp.mean(pre, axis=-1, keepdims=True)
    xc = pre - mu
    rstd = lax.rsqrt(jnp.mean(xc * xc, axis=-1, keepdims=True) + LN_EPS)
    return xc * rstd, rstd


def _layer_norm_bwd(dh, xhat, rstd, g):
    dxh = dh * g
    dpre = rstd * (dxh - jnp.mean(dxh, axis=-1, keepdims=True) - xhat * jnp.mean(dxh * xhat, axis=-1, keepdims=True))
    return dpre, jnp.sum(dh * xhat, axis=0, keepdims=True), jnp.sum(dh, axis=0, keepdims=True)


def _merge_fwd(o_a, o_b, o_c, x, wi_t, wbr, wo, *, name):
    s = x.shape[0]
    t = min(256, s)
    row = lambda w: pl.BlockSpec((t, w), lambda i: (i, 0))

    def body(oa_ref, ob_ref, oc_ref, x_ref, wg_ref, wbr_ref, wo_ref, zd_ref, xhat_ref, rstd_ref, merged_ref, pa_ref, pb_ref, pc_ref):
        zd_ref[...] = _dot(x_ref[...].astype(BF16), wg_ref[...], NT)
        merged = jnp.zeros((t, D_MODEL), F32)
        for b, (o_ref, p_ref) in enumerate(((oa_ref, pa_ref), (ob_ref, pb_ref), (oc_ref, pc_ref))):
            p = _dot(o_ref[...], wbr_ref[b], NN)
            p_ref[...] = p.astype(BF16)
            merged = merged + jax.nn.sigmoid(zd_ref[:, b * D_MODEL:(b + 1) * D_MODEL]) * p
        merged_b = merged.astype(BF16)
        merged_ref[...] = merged_b
        xhat, rstd = _normalize(ALPHA * x_ref[...] + _dot(merged_b, wo_ref[...], NN))
        xhat_ref[...] = xhat
        rstd_ref[...] = rstd

    act = jax.ShapeDtypeStruct((s, D_MODEL), F32)
    return pl.pallas_call(
        body, name=name, grid=(s // t,),
        out_shape=(jax.ShapeDtypeStruct((s, W_D), F32), act, jax.ShapeDtypeStruct((s, 1), F32)) + (jax.ShapeDtypeStruct((s, D_MODEL), BF16),) * 4,
        in_specs=[row(D_MODEL), row(D_MODEL), row(D_MODEL), row(D_MODEL), _resident_rows(wi_t, W_A + W_B + W_C, W_D),
                  _resident((3, D_MODEL, D_MODEL)), _resident((D_MODEL, D_MODEL))],
        out_specs=(row(W_D), row(D_MODEL), row(1), row(D_MODEL), row(D_MODEL), row(D_MODEL), row(D_MODEL)),
        compiler_params=_params(("parallel",)),
    )(o_a, o_b, o_c, x, wi_t, wbr, wo)


def _merge_bwd(dpre1, zd, pa, pb, pc, o_a, o_b, o_c, merged, wbr, wo, *, name, exchanges=()):
    s = dpre1.shape[0]
    t = min(256, s)
    nt = s // t
    row = lambda w: pl.BlockSpec((t, w), lambda i: (i, 0))

    def body(dpre_ref, zd_ref, pa_ref, pb_ref, pc_ref, oa_ref, ob_ref, oc_ref, mg_ref, wbr_ref, wo_ref,
             dzd_ref, doa_ref, dob_ref, doc_ref, gwa_ref, gwb_ref, gwc_ref, gwo_ref, acc):
        step = pl.program_id(0)

        @pl.when(step == 0)
        def _():
            acc[...] = jnp.zeros_like(acc)

        dpre_b = dpre_ref[...].astype(BF16)
        dmerged = _dot(dpre_b, wo_ref[...], NT)
        acc[3] += _dot(mg_ref[...], dpre_b, TN)
        branches = ((pa_ref, oa_ref, doa_ref), (pb_ref, ob_ref, dob_ref), (pc_ref, oc_ref, doc_ref))
        for b, (p_ref, o_ref, do_ref) in enumerate(branches):
            gate = jax.nn.sigmoid(zd_ref[:, b * D_MODEL:(b + 1) * D_MODEL])
            dzd_ref[:, b * D_MODEL:(b + 1) * D_MODEL] = (dmerged * p_ref[...] * gate * (1.0 - gate)).astype(BF16)
            dp = (dmerged * gate).astype(BF16)
            acc[b] += _dot(o_ref[...], dp, TN)
            do_ref[...] = _dot(dp, wbr_ref[b], NT).astype(do_ref.dtype)

        @pl.when(step == nt - 1)
        def _():
            for b, gw_ref in enumerate((gwa_ref, gwb_ref, gwc_ref, gwo_ref)):
                pltpu.sync_copy(acc.at[b], gw_ref)

    act = jax.ShapeDtypeStruct((s, D_MODEL), F32)
    actb = jax.ShapeDtypeStruct((s, D_MODEL), BF16)
    gw = jax.ShapeDtypeStruct((D_MODEL, D_MODEL), F32)
    return _fused_call(
        body, name=name, grid=(nt,),
        out_shape=(jax.ShapeDtypeStruct((s, W_D), BF16), act, actb, actb, gw, gw, gw, gw),
        in_specs=[row(D_MODEL), row(W_D)] + [row(D_MODEL)] * 7 + [_resident((3, D_MODEL, D_MODEL)), _resident((D_MODEL, D_MODEL))],
        out_specs=(row(W_D),) + (row(D_MODEL),) * 3 + (HBM,) * 4, scratch_shapes=[pltpu.VMEM((4, D_MODEL, D_MODEL), F32)],
        operands=[dpre1, zd, pa, pb, pc, o_a, o_b, o_c, merged, wbr, wo], exchanges=exchanges)


def _mlp_loss(xhat1, rstd1, target, ln1_g, ln1_b, ln2_g, ln2_b, wu, wd, *, name):
    s = xhat1.shape[0]
    t = min(256, s)
    npan = wu.shape[0]
    row = lambda w: pl.BlockSpec((t, w), lambda i: (i, 0))
    vec = _resident((1, D_MODEL))

    def body(xhat_ref, rstd_ref, tgt_ref, g1_ref, b1_ref, g2_ref, b2_ref, wu_ref, wd_ref,
             dpre1_ref, dpre2_ref, h1_ref, a_ref, du_ref, stats_ref):
        @pl.when(pl.program_id(0) == 0)
        def _():
            stats_ref[...] = jnp.zeros_like(stats_ref)

        xhat1_v = xhat_ref[...]
        h1 = xhat1_v * g1_ref[...] + b1_ref[...]
        h1_b = h1.astype(BF16)
        h1_ref[...] = h1_b
        us = []
        ff = jnp.zeros((t, D_MODEL), F32)
        for j in range(npan):
            u = _dot(h1_b, wu_ref[j], NN)
            us.append(u)
            r = jnp.maximum(u, 0.0)
            a_b = (r * r).astype(BF16)
            a_ref[:, j * D_MODEL:(j + 1) * D_MODEL] = a_b
            ff = ff + _dot(a_b, wd_ref[j], NN)
        xhat2, rstd2 = _normalize(ALPHA * h1 + ff)
        err = xhat2 * g2_ref[...] + b2_ref[...] - tgt_ref[...]
        stats_ref[4:5, :] += jnp.sum(err * err, axis=0, keepdims=True)
        dpre2, dg2, db2 = _layer_norm_bwd(err * (1.0 / D_MODEL), xhat2, rstd2, g2_ref[...])
        stats_ref[0:1, :] += dg2
        stats_ref[1:2, :] += db2
        dpre2_b = dpre2.astype(BF16)
        dpre2_ref[...] = dpre2_b
        dh1 = ALPHA * dpre2
        for j in range(npan):
            du_b = (_dot(dpre2_b, wd_ref[j], NT) * (2.0 * jnp.maximum(us[j], 0.0))).astype(BF16)
            du_ref[:, j * D_MODEL:(j + 1) * D_MODEL] = du_b
            dh1 = dh1 + _dot(du_b, wu_ref[j], NT)
        dpre1, dg1, db1 = _layer_norm_bwd(dh1, xhat1_v, rstd_ref[...], g1_ref[...])
        stats_ref[2:3, :] += dg1
        stats_ref[3:4, :] += db1
        dpre1_ref[...] = dpre1

    actb = jax.ShapeDtypeStruct((s, D_MODEL), BF16)
    wide = jax.ShapeDtypeStruct((s, D_FF), BF16)
    return pl.pallas_call(
        body, name=name, grid=(s // t,),
        out_shape=(jax.ShapeDtypeStruct((s, D_MODEL), F32), actb, actb, wide, wide, jax.ShapeDtypeStruct((8, D_MODEL), F32)),
        in_specs=[row(D_MODEL), row(1), row(D_MODEL), vec, vec, vec, vec,
                  _resident((npan, D_MODEL, D_MODEL)), _resident((npan, D_MODEL, D_MODEL))],
        out_specs=(row(D_MODEL), row(D_MODEL), row(D_MODEL), row(D_FF), row(D_FF), pl.BlockSpec((8, D_MODEL), lambda i: (0, 0))),
        compiler_params=_params(("arbitrary",)),
    )(xhat1, rstd1, target, ln1_g, ln1_b, ln2_g, ln2_b, wu, wd)


BRANCH_WEIGHTS = ("w_branch_hg", "w_branch_swa", "w_branch_mem")


def _local_step(x, mem, target, wi_t, wmkv, late, lb_logits, gain, sinks, rel_bias, ln1_g, ln1_b, ln2_g, ln2_b, *, distributed):
    s = x.shape[0]
    tm = min(1024, s)
    tk = min(2048, s)
    xb = x.astype(BF16)
    memb = mem.astype(BF16)
    if distributed:
        cx, cy, cc = lax.axis_index("x"), lax.axis_index("y"), lax.axis_index("c")
        pos = jnp.stack([2 * cx + cy, cc]).astype(jnp.int32)
    gather = (lambda names: [_gather_exchange([late[k] for k in names])]) if distributed else (lambda names: [])
    to_sibling = (lambda grads: [_sibling_halves_exchange(grads)]) if distributed else (lambda grads: [])
    to_chips = (lambda sums: [_chip_partials_exchange([bf for bf, _ in sums])]) if distributed else (lambda sums: [])

    def chip_sums(names, grads, from_sibling):
        return [_add_sibling(g, o, pos, name="add_sibling_" + k) for k, g, o in zip(names, grads, from_sibling)]

    def shard_sums(names, sums, from_chips):
        return {k: _add_chips(mine, o, pos, name="add_chips_" + k) for k, (_, mine), o in zip(names, sums, from_chips)}

    za = _mm(xb, wi_t, mode="nt", tm=min(512, s), tn=W_A, tk=D_MODEL, name="proj_a", b_rows=(0, W_A))
    zb = _mm(xb, wi_t, mode="nt", tm=tm, tn=W_B, tk=D_MODEL, name="proj_b", out_dtype=BF16, b_rows=(W_A, W_B))
    mkv = _mm(memb, wmkv, mode="nn", tm=MEM_LEN, tn=512, tk=D_MODEL, name="mem_kv", out_dtype=BF16, b_panels=True)
    onehot, maskrow = _bias_selector()
    bias_tables, sink_lanes = _swa_tables(_bias_table(rel_bias.T, onehot, maskrow, name="bias_table"), sinks)
    (o_a, o_raw, states), landed = _hgrn_fwd(za, lb_logits, gain, name="hgrn_fwd", exchanges=gather(("w_up", "w_down")))
    wu, wd = landed[0] if distributed else (late["wu"], late["wd"])
    o_b, landed = _swa_fwd(zb, bias_tables, sink_lanes, name="swa_fwd", exchanges=gather(BRANCH_WEIGHTS + ("w_out",)))
    if distributed:
        wbr = jnp.stack([wb.reshape(D_MODEL, D_MODEL) for wb in landed[0][:3]])
        wo = landed[0][3].reshape(D_MODEL, D_MODEL)
    else:
        wbr, wo = late["wbr"], late["wo"]
    zc, o_c = _mem_fwd(xb, wi_t, mkv, name="mem_fwd")
    zd, xhat1, rstd1, merged, pa, pb, pc = _merge_fwd(o_a, o_b, o_c, x, wi_t, wbr, wo, name="merge_fwd")

    dpre1, dpre2, h1, act, du, ln_stats = _mlp_loss(xhat1, rstd1, target, ln1_g, ln1_b, ln2_g, ln2_b, wu, wd, name="mlp_loss")
    ffn = ("w_down", "w_up")
    g_ffn = [_mm(act, dpre2, mode="tn", tm=1024, tn=D_MODEL, tk=tk, name="grad_w_down").reshape(N_SHARDS, D_FF // N_SHARDS, D_MODEL),
             _mm(h1, du, mode="tn", tm=D_MODEL, tn=1024, tk=tk, name="grad_w_up", out_panels=True)]

    (dzd, do_a, do_b, do_c, *g_merge), landed = _merge_bwd(dpre1, zd, pa, pb, pc, o_a, o_b, o_c, merged, wbr, wo, name="merge_bwd",
                                                           exchanges=to_sibling(g_ffn))
    sums_ffn = chip_sums(ffn, g_ffn, landed[0]) if distributed else []
    merge = BRANCH_WEIGHTS + ("w_out",)
    g_merge = [g.reshape(N_SHARDS, D_MODEL // N_SHARDS, D_MODEL) for g in g_merge]
    (dza, hg_stats), landed = _hgrn_bwd(za, o_raw, do_a, states, lb_logits, gain, name="hgrn_bwd",
                                        exchanges=to_chips(sums_ffn) + to_sibling(g_merge))
    halves = shard_sums(ffn, sums_ffn, landed[0]) if distributed else {}
    sums_merge = chip_sums(merge, g_merge, landed[1]) if distributed else []
    (dzb, dbias_t, dsinks), landed = _swa_bwd(zb, do_b, bias_tables, sink_lanes, name="swa_bwd", exchanges=to_chips(sums_merge))
    if distributed:
        halves.update(shard_sums(merge, sums_merge, landed[0]))
    dbias = dbias_t.reshape(2 * SWA_BLOCK, SWA_HEADS, SWA_BLOCK).transpose(1, 2, 0).reshape(SWA_HEADS, -1)
    d_rel_bias = _bias_grad(dbias, onehot, name="bias_grad").T
    dzc, dmkv, g_wi = _mem_bwd(xb, zc, do_c, mkv, name="mem_bwd")

    proj = ("w_in", "w_mem_kv")
    for dz, offset, nm in ((dza, 0, "grad_w_in_a"), (dzb, W_A, "grad_w_in_b"), (dzd, W_A + W_B + W_C, "grad_w_in_d")):
        g_wi = _mm(dz, xb, mode="tn", tm=dz.shape[1] if dz.shape[1] <= 1280 else 1024, tn=D_MODEL, tk=tk, name=nm,
                   rows_of=IN_COLS, row_offset=offset, into=g_wi)
    g_proj = [g_wi.reshape(N_SHARDS, IN_COLS // N_SHARDS, D_MODEL),
              _mm(memb, dmkv, mode="tn", tm=D_MODEL, tn=512, tk=MEM_LEN, name="grad_w_mem_kv", out_panels=True)]
    sums_proj = chip_sums(proj, g_proj, _run_exchanges(to_sibling(g_proj), name="reduce_sibling_proj")[0]) if distributed else []
    dx_tm = min(512, s)
    grad_x, landed = _dx_matmul([dza, dzb, dzc, dzd], wi_t, dpre1, tm=dx_tm, name="grad_x", tiles=(0, s // dx_tm),
                                exchanges=to_chips(sums_proj))
    if distributed:
        halves.update(shard_sums(proj, sums_proj, landed[0]))
    else:
        halves = dict(zip(ffn + merge + proj, g_ffn + g_merge + g_proj))
    small = dict(lb_logits=hg_stats[1:3], hg_norm_gain=hg_stats[0:1], swa_sinks=dsinks, rel_bias=d_rel_bias,
                 ln1_g=ln_stats[2:3], ln1_b=ln_stats[3:4], ln2_g=ln_stats[0:1], ln2_b=ln_stats[1:2], sq_err=ln_stats[4:5])
    return grad_x, halves, small


def _mesh_position():
    x, y, c = lax.axis_index("x"), lax.axis_index("y"), lax.axis_index("c")
    chips = [(1 - x, y), (x, 1 - y), (1 - x, 1 - y)]
    return x, y, c, chips


class _Exchange(NamedTuple):
    operands: list
    out_shapes: list
    n_sems: int
    start: Callable
    finish: Callable


def _gather_exchange(shards):
    n = len(shards)
    per = 7

    def plan(ins, outs, send_sems, recv_sems):
        x, y, c, chips = _mesh_position()
        me = 2 * x + y
        sibling = (x, y, 1 - c)

        def half(a, slot, hc):
            rh = shards[a].shape[0] // 2
            return outs[a].at[slot, pl.ds(hc * rh, rh), :]

        def copy(a, k, src, dst, to):
            return pltpu.make_async_remote_copy(src_ref=src, dst_ref=dst, send_sem=send_sems.at[a * per + k], recv_sem=recv_sems.at[a * per + k],
                                                device_id=to, device_id_type=MESH)

        own = [copy(a, 6, ins[a], outs[a].at[me], sibling) for a in range(n)]
        to_chips = [copy(a, k, ins[a].at[pl.ds(c * (shards[a].shape[0] // 2), shards[a].shape[0] // 2), :], half(a, me, c), (cx, cy, c))
                    for k, (cx, cy) in enumerate(chips) for a in range(n)]
        arrived = [copy(a, k, half(a, 2 * cx + cy, c), half(a, 2 * cx + cy, c), (cx, cy, c)) for k, (cx, cy) in enumerate(chips) for a in range(n)]
        passed_on = [copy(a, 3 + k, half(a, 2 * cx + cy, c), half(a, 2 * cx + cy, c), sibling) for k, (cx, cy) in enumerate(chips) for a in range(n)]
        from_sibling = [copy(a, 3 + k, half(a, 2 * cx + cy, 1 - c), half(a, 2 * cx + cy, 1 - c), sibling)
                        for k, (cx, cy) in enumerate(chips) for a in range(n)]
        own_arrived = [copy(a, 6, outs[a].at[me], outs[a].at[me], sibling) for a in range(n)]
        return own, to_chips, arrived, passed_on, from_sibling, own_arrived

    def start(*refs):
        own, to_chips, _, _, _, _ = plan(*refs)
        for cp in own + to_chips:
            cp.start()

    def finish(*refs):
        own, to_chips, arrived, passed_on, from_sibling, own_arrived = plan(*refs)
        for landed, onward in zip(arrived, passed_on):
            landed.wait_recv()
            onward.start()
        for cp in from_sibling + own_arrived:
            cp.wait_recv()
        for cp in own + to_chips + passed_on:
            cp.wait_send()

    return _Exchange(list(shards), [jax.ShapeDtypeStruct((N_SHARDS,) + w.shape, w.dtype) for w in shards], per * n, start, finish)


def _sibling_halves_exchange(grads):
    n = len(grads)

    def plan(ins, outs, send_sems, recv_sems):
        x, y, c, _ = _mesh_position()
        return [pltpu.make_async_remote_copy(src_ref=ins[a].at[:, pl.ds((1 - c) * (grads[a].shape[1] // 2), grads[a].shape[1] // 2), :],
                                             dst_ref=outs[a], send_sem=send_sems.at[a], recv_sem=recv_sems.at[a],
                                             device_id=(x, y, 1 - c), device_id_type=MESH) for a in range(n)]

    def start(*refs):
        for cp in plan(*refs):
            cp.start()

    def finish(*refs):
        for cp in plan(*refs):
            cp.wait()

    return _Exchange(list(grads), [jax.ShapeDtypeStruct((g.shape[0], g.shape[1] // 2, g.shape[2]), g.dtype) for g in grads], n, start, finish)


def _chip_partials_exchange(sums):
    n = len(sums)

    def plan(ins, outs, send_sems, recv_sems):
        _, _, c, chips = _mesh_position()
        return [pltpu.make_async_remote_copy(src_ref=ins[a].at[2 * cx + cy], dst_ref=outs[a].at[k], send_sem=send_sems.at[a * 3 + k],
                                             recv_sem=recv_sems.at[a * 3 + k], device_id=(cx, cy, c), device_id_type=MESH)
                for k, (cx, cy) in enumerate(chips) for a in range(n)]

    def start(*refs):
        for cp in plan(*refs):
            cp.start()

    def finish(*refs):
        for cp in plan(*refs):
            cp.wait()

    return _Exchange(list(sums), [jax.ShapeDtypeStruct((3,) + g.shape[1:], g.dtype) for g in sums], 3 * n, start, finish)


def _fused_call(body, *, name, grid, in_specs, out_specs, out_shape, scratch_shapes, operands, exchanges=(), aliases=None):
    single = not isinstance(out_shape, (tuple, list))
    out_specs = [out_specs] if single else list(out_specs)
    out_shape = [out_shape] if single else list(out_shape)
    n_in, n_out, n_scr = len(in_specs), len(out_specs), len(scratch_shapes)
    x_in = [len(e.operands) for e in exchanges]
    x_out = [len(e.out_shapes) for e in exchanges]

    def wrapped(*refs):
        refs = list(refs)
        ins = refs[:n_in]
        pos = n_in
        ex_ins = []
        for k in x_in:
            ex_ins.append(refs[pos:pos + k])
            pos += k
        outs = refs[pos:pos + n_out]
        pos += n_out
        ex_outs = []
        for k in x_out:
            ex_outs.append(refs[pos:pos + k])
            pos += k
        scratch = refs[pos:pos + n_scr]
        sems = refs[pos + n_scr:]
        first, last = None, None
        for axis, size in enumerate(grid):
            at_start, at_end = pl.program_id(axis) == 0, pl.program_id(axis) == size - 1
            first = at_start if first is None else first & at_start
            last = at_end if last is None else last & at_end

        @pl.when(first)
        def _():
            for i, e in enumerate(exchanges):
                e.start(ex_ins[i], ex_outs[i], sems[2 * i], sems[2 * i + 1])

        body(*ins, *outs, *scratch)

        @pl.when(last)
        def _():
            for i, e in enumerate(exchanges):
                e.finish(ex_ins[i], ex_outs[i], sems[2 * i], sems[2 * i + 1])

    n_x_in, n_x_out = sum(x_in), sum(x_out)
    results = pl.pallas_call(
        wrapped if exchanges else body, name=name, grid=grid,
        in_specs=list(in_specs) + [HBM] * n_x_in,
        out_specs=out_specs + [HBM] * n_x_out,
        out_shape=out_shape + [s for e in exchanges for s in e.out_shapes],
        scratch_shapes=list(scratch_shapes) + [pltpu.SemaphoreType.DMA((e.n_sems,)) for e in exchanges for _ in range(2)],
        input_output_aliases=aliases or {}, compiler_params=_params(("arbitrary",) * len(grid)),
    )(*operands, *[a for e in exchanges for a in e.operands])
    own = results[0] if single else tuple(results[:n_out])
    landed, pos = [], n_out
    for k in x_out:
        landed.append(list(results[pos:pos + k]))
        pos += k
    return own, landed


def _run_exchanges(exchanges, *, name):
    def body(*refs):
        n_in = sum(len(e.operands) for e in exchanges)
        n_out = sum(len(e.out_shapes) for e in exchanges)
        ins, outs, sems = refs[:n_in], refs[n_in:n_in + n_out], refs[n_in + n_out:]
        spans, i, o = [], 0, 0
        for e in exchanges:
            spans.append((ins[i:i + len(e.operands)], outs[o:o + len(e.out_shapes)]))
            i, o = i + len(e.operands), o + len(e.out_shapes)
        for k, e in enumerate(exchanges):
            e.start(*spans[k], sems[2 * k], sems[2 * k + 1])
        for k, e in enumerate(exchanges):
            e.finish(*spans[k], sems[2 * k], sems[2 * k + 1])

    operands = [a for e in exchanges for a in e.operands]
    shapes = [s for e in exchanges for s in e.out_shapes]
    results = pl.pallas_call(
        body, name=name, out_shape=shapes, in_specs=[HBM] * len(operands), out_specs=[HBM] * len(shapes),
        scratch_shapes=[pltpu.SemaphoreType.DMA((e.n_sems,)) for e in exchanges for _ in range(2)],
    )(*operands)
    landed, pos = [], 0
    for e in exchanges:
        landed.append(list(results[pos:pos + len(e.out_shapes)]))
        pos += len(e.out_shapes)
    return landed


ROW_TILE_MAX = 640
BF16_SUBLANES = 16


def _row_tile(rows):
    for tr in range(min(rows, ROW_TILE_MAX), 0, -1):
        if rows % tr == 0 and tr % BF16_SUBLANES == 0:
            return tr
    raise ValueError(rows)


def _add_sibling(grad, other, pos, *, name):
    p, r, cols = grad.shape
    rh = r // 2
    tr = _row_tile(rh)
    nb = rh // tr

    def body(pos_ref, g_ref, o_ref, sb_ref, mine_ref):
        total = g_ref[...] + o_ref[...]
        sb_ref[...] = total.astype(BF16)

        @pl.when(pl.program_id(1) == pos_ref[0])
        def _():
            mine_ref[...] = total

    return pl.pallas_call(
        body, name=name, out_shape=(jax.ShapeDtypeStruct((p, rh, cols), BF16), jax.ShapeDtypeStruct((rh, cols), F32)),
        grid_spec=pltpu.PrefetchScalarGridSpec(
            num_scalar_prefetch=1, grid=(nb, p),
            in_specs=[pl.BlockSpec((None, tr, cols), lambda i, j, pos_ref: (j, pos_ref[1] * nb + i, 0)),
                      pl.BlockSpec((None, tr, cols), lambda i, j, pos_ref: (j, i, 0))],
            out_specs=(pl.BlockSpec((None, tr, cols), lambda i, j, pos_ref: (j, i, 0)),
                       pl.BlockSpec((tr, cols), lambda i, j, pos_ref: (i, 0)))),
        compiler_params=_params(("parallel", "arbitrary")),
    )(pos, grad, other)


def _add_chips(mine, others, pos, *, name):
    rh, cols = mine.shape
    tr = _row_tile(rh)
    nb = rh // tr

    def body(pos_ref, s_ref, o_ref, r_ref):
        r_ref[...] = ((s_ref[...] + o_ref[0].astype(F32)) + o_ref[1].astype(F32)) + o_ref[2].astype(F32)

    return pl.pallas_call(
        body, name=name, out_shape=jax.ShapeDtypeStruct((2 * rh, cols), F32),
        grid_spec=pltpu.PrefetchScalarGridSpec(
            num_scalar_prefetch=1, grid=(nb,),
            in_specs=[pl.BlockSpec((tr, cols), lambda i, pos_ref: (i, 0)),
                      pl.BlockSpec((3, tr, cols), lambda i, pos_ref: (0, i, 0))],
            out_specs=pl.BlockSpec((tr, cols), lambda i, pos_ref: (pos_ref[1] * nb + i, 0))),
        compiler_params=_params(("parallel",)),
    )(pos, mine, others)


def _join_halves(bufs, *, name):
    n = len(bufs)

    def body(*refs):
        ins, outs = refs[:n], refs[n:2 * n]
        send_sems, recv_sems = refs[2 * n:]
        x, y, c, _ = _mesh_position()

        def copy(a, hc):
            rh = bufs[a].shape[0] // 2
            rows = pl.ds(hc * rh, rh)
            return pltpu.make_async_remote_copy(src_ref=ins[a].at[rows, :], dst_ref=outs[a].at[rows, :], send_sem=send_sems.at[a],
                                                recv_sem=recv_sems.at[a], device_id=(x, y, 1 - c), device_id_type=MESH)

        for a in range(n):
            copy(a, c).start()
        for a in range(n):
            copy(a, c).wait_send()
            copy(a, 1 - c).wait_recv()

    return pl.pallas_call(
        body, name=name, out_shape=[jax.ShapeDtypeStruct(b.shape, b.dtype) for b in bufs],
        in_specs=[HBM] * n, out_specs=[HBM] * n, input_output_aliases={a: a for a in range(n)},
        scratch_shapes=[pltpu.SemaphoreType.DMA((n,)), pltpu.SemaphoreType.DMA((n,))],
    )(*bufs)


SMALL = ["lb_logits", "hg_norm_gain", "swa_sinks", "rel_bias", "ln1_g", "ln1_b", "ln2_g", "ln2_b"]
PACK_ROWS = 48
PACK_AT = dict(lb_logits=(slice(0, 2), slice(0, D_MODEL)), hg_norm_gain=(slice(2, 3), slice(0, D_MODEL)), ln1_g=(slice(3, 4), slice(0, D_MODEL)),
               ln1_b=(slice(4, 5), slice(0, D_MODEL)), ln2_g=(slice(5, 6), slice(0, D_MODEL)), ln2_b=(slice(6, 7), slice(0, D_MODEL)),
               swa_sinks=(slice(7, 8), slice(0, SWA_HEADS)), sq_err=(slice(8, 9), slice(0, D_MODEL)),
               rel_bias=(slice(16, 16 + NUM_BUCKETS), slice(0, SWA_HEADS)))


def _reduce_small(grads, *, name):
    names = SMALL + ["sq_err"]

    def body(*refs):
        g_refs = dict(zip(names, refs[:len(names)]))
        total_ref, packed, gathered, send_sems, recv_sems = refs[len(names):]
        x, y, c, _ = _mesh_position()
        me = 4 * x + 2 * y + c
        packed[...] = jnp.zeros_like(packed)
        for k, g_ref in g_refs.items():
            packed[PACK_AT[k]] = g_ref[...]
        gathered[me] = packed[...]
        copies = []
        for d in range(1, 8):
            dx, dy, dc = (d >> 2) & 1, (d >> 1) & 1, d & 1
            cp = pltpu.make_async_remote_copy(src_ref=packed, dst_ref=gathered.at[me], send_sem=send_sems.at[d - 1], recv_sem=recv_sems.at[d - 1],
                                              device_id=(x ^ dx, y ^ dy, c ^ dc), device_id_type=MESH)
            cp.start()
            copies.append(cp)
        for cp in copies:
            cp.wait()
        total = gathered[0]
        for j in range(1, 8):
            total = total + gathered[j]
        total_ref[...] = total

    vm = pl.BlockSpec(memory_space=pltpu.VMEM)
    return pl.pallas_call(
        body, name=name, out_shape=jax.ShapeDtypeStruct((PACK_ROWS, D_MODEL), F32), in_specs=[vm] * len(names), out_specs=vm,
        scratch_shapes=[pltpu.VMEM((PACK_ROWS, D_MODEL), F32), pltpu.VMEM((8, PACK_ROWS, D_MODEL), F32),
                        pltpu.SemaphoreType.DMA((7,)), pltpu.SemaphoreType.DMA((7,))],
    )(*[grads[k] for k in names])


def _adamw_small(total, w, m, v, *, name):
    names = SMALL
    n = len(names)

    def body(*refs):
        total_ref = refs[0]
        w_refs, m_refs, v_refs = (dict(zip(names, refs[1 + i * n:1 + (i + 1) * n])) for i in range(3))
        loss_ref = refs[1 + 3 * n]
        go_refs, d_refs, nm_refs, nv_refs = (dict(zip(names, refs[2 + (3 + i) * n:2 + (4 + i) * n])) for i in range(4))
        loss_ref[...] = (0.5 / D_MODEL) * jnp.sum(total_ref[PACK_AT["sq_err"]], axis=1, keepdims=True)
        for k in names:
            g = total_ref[PACK_AT[k]]
            go_refs[k][...] = g
            d_refs[k][...], nm_refs[k][...], nv_refs[k][...] = _adamw_math(w_refs[k][...], g, m_refs[k][...], v_refs[k][...])

    like = [jax.ShapeDtypeStruct(w[k].shape, F32) for k in names]
    results = pl.pallas_call(body, name=name, out_shape=[jax.ShapeDtypeStruct((1, 1), F32)] + like * 4,
                             compiler_params=_params())(total, *[d[k] for d in (w, m, v) for k in names])
    return results[0], {k: tuple(results[1 + i * n + j] for i in range(4)) for j, k in enumerate(names)}


def _adamw_math(w, g, m, v):
    m = ADAM_B1 * m + (1.0 - ADAM_B1) * g
    v = ADAM_B2 * v + (1.0 - ADAM_B2) * (g * g)
    m_hat = m / (1.0 - ADAM_B1 ** ADAM_STEP)
    v_hat = v / (1.0 - ADAM_B2 ** ADAM_STEP)
    delta = -ADAM_LR * (m_hat / (jnp.sqrt(v_hat) + ADAM_EPS) + ADAM_WD * w)
    return delta, m, v


def _adamw(w, g, m, v, *, name):
    _, rows, cols = w.shape
    tr = _row_tile(rows)
    blk = pl.BlockSpec((None, tr, cols), lambda i: (0, i, 0))
    flat = pl.BlockSpec((tr, cols), lambda i: (i, 0))

    def body(w_ref, g_ref, m_ref, v_ref, go_ref, d_ref, nm_ref, nv_ref):
        g_v = g_ref[...]
        go_ref[...] = g_v
        d_ref[...], nm_ref[...], nv_ref[...] = _adamw_math(w_ref[...], g_v, m_ref[...], v_ref[...])

    shape = jax.ShapeDtypeStruct((1, rows, cols), F32)
    return pl.pallas_call(body, name=name, grid=(rows // tr,), out_shape=(shape,) * 4, in_specs=[blk, flat, blk, blk], out_specs=(blk,) * 4,
                          compiler_params=_params(("parallel",)))(w, g, m, v)


WEIGHTS = ["w_in", "lb_logits", "hg_norm_gain", "swa_sinks", "rel_bias", "w_mem_kv", "w_branch_hg", "w_branch_swa", "w_branch_mem",
           "w_out", "ln1_g", "ln1_b", "w_up", "w_down", "ln2_g", "ln2_b"]
BIG = ["w_in", "w_mem_kv", "w_branch_hg", "w_branch_swa", "w_branch_mem", "w_out", "w_up", "w_down"]


def kernel(x, mem, w_in, lb_logits, hg_norm_gain, swa_sinks, rel_bias, w_mem_kv, w_branch_hg, w_branch_swa, w_branch_mem, w_out, ln1_g, ln1_b, w_up, w_down, ln2_g, ln2_b, loss_target, m_w_in, m_lb_logits, m_hg_norm_gain, m_swa_sinks, m_rel_bias, m_w_mem_kv, m_w_branch_hg, m_w_branch_swa, m_w_branch_mem, m_w_out, m_ln1_g, m_ln1_b, m_w_up, m_w_down, m_ln2_g, m_ln2_b, v_w_in, v_lb_logits, v_hg_norm_gain, v_swa_sinks, v_rel_bias, v_w_mem_kv, v_w_branch_hg, v_w_branch_swa, v_w_branch_mem, v_w_out, v_ln1_g, v_ln1_b, v_w_up, v_w_down, v_ln2_g, v_ln2_b):
    w = dict(w_in=w_in, lb_logits=lb_logits, hg_norm_gain=hg_norm_gain, swa_sinks=swa_sinks, rel_bias=rel_bias, w_mem_kv=w_mem_kv,
             w_branch_hg=w_branch_hg, w_branch_swa=w_branch_swa, w_branch_mem=w_branch_mem, w_out=w_out, ln1_g=ln1_g, ln1_b=ln1_b,
             w_up=w_up, w_down=w_down, ln2_g=ln2_g, ln2_b=ln2_b)
    m = dict(w_in=m_w_in, lb_logits=m_lb_logits, hg_norm_gain=m_hg_norm_gain, swa_sinks=m_swa_sinks, rel_bias=m_rel_bias, w_mem_kv=m_w_mem_kv,
             w_branch_hg=m_w_branch_hg, w_branch_swa=m_w_branch_swa, w_branch_mem=m_w_branch_mem, w_out=m_w_out, ln1_g=m_ln1_g, ln1_b=m_ln1_b,
             w_up=m_w_up, w_down=m_w_down, ln2_g=m_ln2_g, ln2_b=m_ln2_b)
    v = dict(w_in=v_w_in, lb_logits=v_lb_logits, hg_norm_gain=v_hg_norm_gain, swa_sinks=v_swa_sinks, rel_bias=v_rel_bias, w_mem_kv=v_w_mem_kv,
             w_branch_hg=v_w_branch_hg, w_branch_swa=v_w_branch_swa, w_branch_mem=v_w_branch_mem, w_out=v_w_out, ln1_g=v_ln1_g, ln1_b=v_ln1_b,
             w_up=v_w_up, w_down=v_w_down, ln2_g=v_ln2_g, ln2_b=v_ln2_b)
    shapes = {k: w[k].shape for k in WEIGHTS}
    for d in (w, m, v):
        d["w_in"] = d["w_in"].reshape(D_MODEL, IN_COLS // N_SHARDS).T[None]
    shards = {k: w[k].reshape(w[k].shape[-2], w[k].shape[-1]).astype(BF16) for k in BIG}
    wi4, wmkv = _run_exchanges([_gather_exchange([shards["w_in"], shards["w_mem_kv"]])], name="gather_weights")[0]
    wi_t = wi4.reshape(IN_COLS, D_MODEL)

    grad_x, halves, small = _local_step(
        x.reshape(x.shape[-2], D_MODEL), mem.reshape(MEM_LEN, D_MODEL), loss_target.reshape(loss_target.shape[-2], D_MODEL),
        wi_t, wmkv, shards, lb_logits, hg_norm_gain, swa_sinks, rel_bias, ln1_g, ln1_b, ln2_g, ln2_b, distributed=True)

    reduced = dict(zip(BIG, _join_halves([halves[k] for k in BIG], name="join_halves")))

    outs = {k: _adamw(w[k], reduced[k], m[k], v[k], name="adamw_" + k) for k in BIG}
    loss, small_outs = _adamw_small(_reduce_small(small, name="reduce_small"), w, m, v, name="adamw_small")
    outs.update(small_outs)
    grad_out, delta_out, m_out, v_out = ({k: outs[k][i] for k in WEIGHTS} for i in range(4))
    for out in (grad_out, delta_out, m_out, v_out):
        out["w_in"] = out["w_in"][0].T

    result = [loss.reshape(()), grad_x.reshape(x.shape)]
    for out in (grad_out, delta_out, m_out, v_out):
        result += [out[k].reshape(shapes[k]) for k in WEIGHTS]
    return tuple(result)
```

```python
import math
from typing import Callable, NamedTuple

import jax
import jax.numpy as jnp
from jax import lax
from jax.experimental import pallas as pl
from jax.experimental.pallas import tpu as pltpu

F32 = jnp.float32
BF16 = jnp.bfloat16
HIGHEST = lax.Precision.HIGHEST
MESH = pl.DeviceIdType.MESH

D_MODEL = 1024
MEM_LEN = 256
HG_HEADS = 8
HG_DK = 128
HG_CHUNK = 64
SWA_HEADS = 16
SWA_KV_HEADS = 2
SWA_GROUP = 8
SWA_HEAD_DIM = 64
SWA_BLOCK = 128
SWA_WINDOW = 128
MEM_HEADS = 4
MEM_HEAD_DIM = 256
NUM_BUCKETS = 32
MAX_DISTANCE = 128
D_FF = 4096
LN_EPS = 1e-5
RMS_EPS = 1e-6
ALPHA = 2.0 ** 0.25
W_A, W_B, W_C, W_D = 4096, 1280, 1024, 3072
IN_COLS = W_A + W_B + W_C + W_D
N_SHARDS = 4
ADAM_LR = 0.001
ADAM_B1 = 0.9
ADAM_B2 = 0.999
ADAM_EPS = 1e-08
ADAM_WD = 0.01
ADAM_STEP = 10
MASK_VALUE = -1e30
VMEM_LIMIT = 56 * 1024 * 1024

NN = ((1,), (0,))
NT = ((1,), (1,))
TN = ((0,), (0,))
HBM = pl.BlockSpec(memory_space=pltpu.HBM)


def _dot(a, b, dims=NN, precision=None):
    return lax.dot_general(a, b, (dims, ((), ())), precision=precision, preferred_element_type=F32)


def _params(sem=None):
    return pltpu.CompilerParams(dimension_semantics=sem, vmem_limit_bytes=VMEM_LIMIT)


def _resident(shape):
    zeros = (0,) * len(shape)
    return pl.BlockSpec(shape, lambda *_: zeros, pipeline_mode=pl.Buffered(1))


def _resident_rows(arr, offset, rows):
    return pl.BlockSpec((pl.Element(rows), pl.Element(arr.shape[1])), lambda *_: (offset, 0), pipeline_mode=pl.Buffered(1))


def _mm(a, b, *, mode, tm, tn, tk, name, out_dtype=F32, b_panels=False, b_rows=None, out_panels=False, rows_of=None, row_offset=0,
        into=None):
    if mode == "tn":
        kdim, m = a.shape
    else:
        m, kdim = a.shape
    if b_panels:
        n = b.shape[0] * b.shape[2]
        assert b.shape[2] == tn and mode == "nn"
    elif b_rows is not None:
        assert mode == "nt"
        b_offset, n = b_rows
    elif mode == "nt":
        n = b.shape[0]
    else:
        n = b.shape[1]
    assert m % tm == 0 and n % tn == 0 and kdim % tk == 0, (name, m, n, kdim)
    nk = kdim // tk
    dims = {"nn": NN, "nt": NT, "tn": TN}[mode]
    a_spec = pl.BlockSpec((tk, tm), lambda i, j, k: (k, i)) if mode == "tn" else pl.BlockSpec((tm, tk), lambda i, j, k: (i, k))
    if b_panels:
        b_spec = pl.BlockSpec((None, tk, tn), lambda i, j, k: (j, k, 0))
    elif b_rows is not None:
        assert b_offset % BF16_SUBLANES == 0 and tn % BF16_SUBLANES == 0 and tk % 128 == 0
        b_spec = pl.BlockSpec((pl.Element(tn), pl.Element(tk)),
                              lambda i, j, k: (pl.multiple_of(b_offset + j * tn, BF16_SUBLANES), pl.multiple_of(k * tk, 128)))
    elif mode == "nt":
        b_spec = pl.BlockSpec((tn, tk), lambda i, j, k: (j, k))
    else:
        b_spec = pl.BlockSpec((tk, tn), lambda i, j, k: (k, j))
    in_specs = [a_spec, b_spec]
    operands = [a, b]
    aliases = {}
    if out_panels:
        out_shape = jax.ShapeDtypeStruct((n // tn, m, tn), out_dtype)
        o_spec = pl.BlockSpec((None, tm, tn), lambda i, j, k: (j, i, 0))
    elif rows_of is not None:
        out_shape = jax.ShapeDtypeStruct((rows_of, n), out_dtype)
        assert row_offset % BF16_SUBLANES == 0 and tm % BF16_SUBLANES == 0 and tn % 128 == 0
        o_spec = pl.BlockSpec((pl.Element(tm), pl.Element(tn)),
                              lambda i, j, k: (pl.multiple_of(row_offset + i * tm, BF16_SUBLANES), pl.multiple_of(j * tn, 128)))
        if into is not None:
            in_specs.append(pl.BlockSpec(memory_space=pl.ANY))
            operands.append(into)
            aliases = {2: 0}
    else:
        out_shape = jax.ShapeDtypeStruct((m, n), out_dtype)
        o_spec = pl.BlockSpec((tm, tn), lambda i, j, k: (i, j))
    n_in = len(operands)

    def body(*refs):
        a_ref, b_ref, o_ref = refs[0], refs[1], refs[n_in]
        part = _dot(a_ref[...].astype(BF16), b_ref[...].astype(BF16), dims)

        def finish(acc):
            o_ref[...] = acc.astype(out_dtype)

        if nk == 1:
            finish(part)
        else:
            acc_ref = refs[-1]
            k = pl.program_id(2)

            @pl.when(k == 0)
            def _():
                acc_ref[...] = part

            @pl.when(k > 0)
            def _():
                acc_ref[...] += part

            @pl.when(k == nk - 1)
            def _():
                finish(acc_ref[...])

    return pl.pallas_call(
        body, name=name, out_shape=out_shape, grid=(m // tm, n // tn, nk), in_specs=in_specs, out_specs=o_spec,
        scratch_shapes=[pltpu.VMEM((tm, tn), F32)] if nk > 1 else [], input_output_aliases=aliases,
        compiler_params=_params(("parallel", "parallel", "arbitrary")),
    )(*operands)


def _dx_matmul(dzs, wi_t, resid, *, tm, name, tiles, into=None, exchanges=()):
    s = resid.shape[0]
    npieces = len(dzs)
    offsets = [sum(dz.shape[1] for dz in dzs[:p]) for p in range(npieces)]
    first, count = tiles
    tile = lambda i: (first + i, 0)
    in_specs = [pl.BlockSpec((tm, dz.shape[1]), tile) for dz in dzs] + [_resident(wi_t.shape), pl.BlockSpec((tm, D_MODEL), tile)]
    operands = [*dzs, wi_t, resid]
    if into is not None:
        in_specs.append(pl.BlockSpec(memory_space=pl.ANY))
        operands.append(into)
    n_in = len(operands)

    def body(*refs):
        dz_refs, w_ref, r_ref, o_ref = refs[:npieces], refs[npieces], refs[npieces + 1], refs[n_in]
        total = ALPHA * r_ref[...]
        for p in range(npieces):
            total = total + _dot(dz_refs[p][...], w_ref[offsets[p]:offsets[p] + dzs[p].shape[1], :], NN)
        o_ref[...] = total

    return _fused_call(
        body, name=name, out_shape=jax.ShapeDtypeStruct((s, D_MODEL), F32), grid=(count,), in_specs=in_specs,
        out_specs=pl.BlockSpec((tm, D_MODEL), tile), scratch_shapes=[], operands=operands, exchanges=exchanges,
        aliases={n_in - 1: 0} if into is not None else None)


def _lower_bound(lbl_ref):
    l0, l1 = lbl_ref[0:1, :], lbl_ref[1:2, :]
    mx = jnp.maximum(l0, l1)
    e0, e1 = jnp.exp(l0 - mx), jnp.exp(l1 - mx)
    return e0 / (e0 + e1)


HEAD_COLS = [slice(h * HG_DK, (h + 1) * HG_DK) for h in range(HG_HEADS)]


def _head_mean(x):
    return jnp.concatenate([jnp.broadcast_to(jnp.mean(x[:, c], axis=-1, keepdims=True), (x.shape[0], HG_DK)) for c in HEAD_COLS], axis=1)


def _triangle_sum(tri_b, x):
    p0 = x.astype(BF16)
    r1 = x - p0.astype(F32)
    p1 = r1.astype(BF16)
    p2 = (r1 - p1.astype(F32)).astype(BF16)
    return _dot(tri_b, p0) + _dot(tri_b, p1) + _dot(tri_b, p2)


def _chunk_forward(q, fl, v, lb, tril_b):
    sg = jax.nn.sigmoid(fl)
    f = lb + (1.0 - lb) * sg
    k = 1.0 - f
    b = _triangle_sum(tril_b, jnp.log(f))
    b_last = b[HG_CHUNK - 1:HG_CHUNK, :]
    eb, enb, eo = jnp.exp(b), jnp.exp(-b), jnp.exp(b_last - b)
    return sg, f, k, b_last, eb, enb, eo, q * eb, k * enb, k * eo


def _hgrn_fwd(xb, wi_t, lb_logits, gain, *, name, exchanges=()):
    s = xb.shape[0]
    t = min(256, s)
    ncs = t // HG_CHUNK

    def body(x_ref, w_ref, lbl_ref, gain_ref, z_ref, oa_ref, oraw_ref, st_ref, state):
        @pl.when(pl.program_id(0) == 0)
        def _():
            state[...] = jnp.zeros_like(state)

        z_ref[...] = _dot(x_ref[...], w_ref[...], NT)
        lb_all = _lower_bound(lbl_ref)
        row = lax.broadcasted_iota(jnp.int32, (HG_CHUNK, HG_CHUNK), 0)
        col = lax.broadcasted_iota(jnp.int32, (HG_CHUNK, HG_CHUNK), 1)
        tril = row >= col
        tril_b = tril.astype(BF16)
        gain_all = gain_ref[...]

        def chunk(i, carry):
            r = pl.ds(pl.multiple_of(i * HG_CHUNK, HG_CHUNK), HG_CHUNK)
            q, fl, v, hg = (z_ref[r, j * D_MODEL:(j + 1) * D_MODEL] for j in range(4))
            _, _, _, b_last, _, _, _, q_in, k_in, k_out = _chunk_forward(q, fl, v, lb_all, tril_b)
            q_in_b, k_in_b, k_out_b, vb = (u.astype(BF16) for u in (q_in, k_in, k_out, v))
            decay = jnp.exp(b_last)
            sts = [state[h] for h in range(HG_HEADS)]
            attn = [_dot(q_in_b[:, c], k_in_b[:, c], NT) for c in HEAD_COLS]
            inter = [_dot(q_in_b[:, c], sts[h].astype(BF16), NT) for h, c in enumerate(HEAD_COLS)]
            upd = [_dot(vb[:, c], k_out_b[:, c], TN) for c in HEAD_COLS]
            attn = [jnp.where(tril, a, 0.0).astype(BF16) for a in attn]
            outs = [_dot(attn[h], vb[:, c], NN) + inter[h] for h, c in enumerate(HEAD_COLS)]
            for h, c in enumerate(HEAD_COLS):
                st_ref[h, i] = sts[h]
                state[h] = sts[h] * decay[:, c] + upd[h]
            o = jnp.concatenate(outs, axis=1)
            oraw_ref[r, :] = o
            n = o * lax.rsqrt(_head_mean(o * o) + RMS_EPS)
            oa_ref[r, :] = (n * gain_all * (hg * jax.nn.sigmoid(hg))).astype(BF16)
            return carry

        lax.fori_loop(0, ncs, chunk, 0, unroll=True)

    tile = lambda i: (i, 0)
    return _fused_call(
        body, name=name, grid=(s // t,),
        out_shape=(jax.ShapeDtypeStruct((s, W_A), F32), jax.ShapeDtypeStruct((s, D_MODEL), BF16), jax.ShapeDtypeStruct((s, D_MODEL), F32),
                   jax.ShapeDtypeStruct((HG_HEADS, s // HG_CHUNK, HG_DK, HG_DK), F32)),
        in_specs=[pl.BlockSpec((t, D_MODEL), tile), _resident_rows(wi_t, 0, W_A), _resident((2, D_MODEL)), _resident((1, D_MODEL))],
        out_specs=(pl.BlockSpec((t, W_A), tile), pl.BlockSpec((t, D_MODEL), tile), pl.BlockSpec((t, D_MODEL), tile),
                   pl.BlockSpec((HG_HEADS, ncs, HG_DK, HG_DK), lambda i: (0, i, 0, 0))),
        scratch_shapes=[pltpu.VMEM((HG_HEADS, HG_DK, HG_DK), F32)],
        operands=[xb, wi_t, lb_logits, gain], exchanges=exchanges)


def _hgrn_bwd(za, oraw, do_a, states, lb_logits, gain, *, name, exchanges=()):
    s = za.shape[0]
    t = min(256, s)
    ncs = t // HG_CHUNK
    nt = s // t

    def body(z_ref, oraw_ref, do_ref, st_ref, lbl_ref, gain_ref, dz_ref, stats_ref, dstate):
        step = pl.program_id(0)

        @pl.when(step == 0)
        def _():
            dstate[...] = jnp.zeros_like(dstate)
            stats_ref[...] = jnp.zeros_like(stats_ref)

        lb_all = _lower_bound(lbl_ref)
        row = lax.broadcasted_iota(jnp.int32, (HG_CHUNK, HG_CHUNK), 0)
        col = lax.broadcasted_iota(jnp.int32, (HG_CHUNK, HG_CHUNK), 1)
        tril = row >= col
        tril_b = tril.astype(BF16)
        triu_b = (row <= col).astype(BF16)
        gain_all = gain_ref[...]

        def chunk(ii, carry):
            i = ncs - 1 - ii
            r = pl.ds(pl.multiple_of(i * HG_CHUNK, HG_CHUNK), HG_CHUNK)
            q, fl, v, hg = (z_ref[r, j * D_MODEL:(j + 1) * D_MODEL] for j in range(4))
            o = oraw_ref[r, :]
            doa = do_ref[r, :]
            rms = lax.rsqrt(_head_mean(o * o) + RMS_EPS)
            n = o * rms
            sgg = jax.nn.sigmoid(hg)
            silu = hg * sgg
            dhg = doa * n * gain_all * (sgg * (1.0 + hg * (1.0 - sgg)))
            dgain = jnp.sum(doa * n * silu, axis=0, keepdims=True)
            dn = doa * gain_all * silu
            do = rms * (dn - n * _head_mean(dn * n))
            sg, f, k, b_last, eb, enb, eo, q_in, k_in, k_out = _chunk_forward(q, fl, v, lb_all, tril_b)
            q_in_b, k_in_b, k_out_b, vb, dob = (u.astype(BF16) for u in (q_in, k_in, k_out, v, do))
            decay = jnp.exp(b_last)
            sts = [st_ref[h, i] for h in range(HG_HEADS)]
            dsts = [dstate[h] for h in range(HG_HEADS)]
            dsts_b = [d.astype(BF16) for d in dsts]
            heads = list(enumerate(HEAD_COLS))
            attn = [_dot(q_in_b[:, c], k_in_b[:, c], NT) for h, c in heads]
            dattn = [_dot(dob[:, c], vb[:, c], NT) for h, c in heads]
            dq_st = [_dot(dob[:, c], sts[h].astype(BF16), NN) for h, c in heads]
            dk_out = [_dot(vb[:, c], dsts_b[h], NN) for h, c in heads]
            dv_st = [_dot(k_out_b[:, c], dsts_b[h], NT) for h, c in heads]
            dst_o = [_dot(dob[:, c], q_in_b[:, c], TN) for h, c in heads]
            attn = [jnp.where(tril, a, 0.0).astype(BF16) for a in attn]
            dattn = [jnp.where(tril, a, 0.0).astype(BF16) for a in dattn]
            dq_in = jnp.concatenate([_dot(dattn[h], k_in_b[:, c], NN) + dq_st[h] for h, c in heads], axis=1)
            dk_in = jnp.concatenate([_dot(dattn[h], q_in_b[:, c], TN) for h, c in heads], axis=1)
            dv = jnp.concatenate([_dot(attn[h], dob[:, c], TN) + dv_st[h] for h, c in heads], axis=1)
            dk_out = jnp.concatenate(dk_out, axis=1)
            dst_st = jnp.concatenate([jnp.sum(dsts[h] * sts[h], axis=0, keepdims=True) for h in range(HG_HEADS)], axis=1)
            for h, c in heads:
                dstate[h] = dsts[h] * decay[:, c] + dst_o[h]
            db_last = decay * dst_st + jnp.sum(dk_out * k_out, axis=0, keepdims=True)
            db = dq_in * q_in - dk_in * k_in - dk_out * k_out
            dg = _triangle_sum(triu_b, db) + db_last
            dk = dk_in * enb + dk_out * eo
            df = dg / f - dk
            stats_ref[0:1, :] += dgain
            stats_ref[1:2, :] += jnp.sum(df * (1.0 - sg), axis=0, keepdims=True)
            dz_ref[r, 0:1024] = (dq_in * eb).astype(BF16)
            dz_ref[r, 1024:2048] = (df * (1.0 - lb_all) * sg * (1.0 - sg)).astype(BF16)
            dz_ref[r, 2048:3072] = dv.astype(BF16)
            dz_ref[r, 3072:4096] = dhg.astype(BF16)
            return carry

        lax.fori_loop(0, ncs, chunk, 0, unroll=True)

        @pl.when(step == nt - 1)
        def _():
            dl0 = stats_ref[1:2, :] * lb_all * (1.0 - lb_all)
            stats_ref[1:2, :] = dl0
            stats_ref[2:3, :] = -dl0

    rev = lambda i: (nt - 1 - i, 0)
    return _fused_call(
        body, name=name, grid=(nt,),
        out_shape=(jax.ShapeDtypeStruct((s, W_A), BF16), jax.ShapeDtypeStruct((8, D_MODEL), F32)),
        in_specs=[pl.BlockSpec((t, W_A), rev), pl.BlockSpec((t, D_MODEL), rev), pl.BlockSpec((t, D_MODEL), rev),
                  pl.BlockSpec((HG_HEADS, ncs, HG_DK, HG_DK), lambda i: (0, nt - 1 - i, 0, 0)),
                  _resident((2, D_MODEL)), _resident((1, D_MODEL))],
        out_specs=(pl.BlockSpec((t, W_A), rev), pl.BlockSpec((8, D_MODEL), lambda i: (0, 0))),
        scratch_shapes=[pltpu.VMEM((HG_HEADS, HG_DK, HG_DK), F32)],
        operands=[za, oraw, do_a, states, lb_logits, gain], exchanges=exchanges)


def _t5_bucket(n):
    max_exact = NUM_BUCKETS // 2
    nf = jnp.maximum(n, 1).astype(F32)
    large = max_exact + (jnp.log(nf / max_exact) / math.log(MAX_DISTANCE / max_exact) * (NUM_BUCKETS - max_exact)).astype(jnp.int32)
    large = jnp.minimum(large, NUM_BUCKETS - 1)
    return jnp.where(n < max_exact, n, large)


def _bias_selector():
    qi = jnp.arange(SWA_BLOCK)[:, None] + SWA_BLOCK
    kj = jnp.arange(2 * SWA_BLOCK)[None, :]
    dist = qi - kj
    band = ((dist >= 0) & (dist < SWA_WINDOW)).reshape(1, -1)
    bucket = _t5_bucket(jnp.clip(dist, 0, SWA_WINDOW - 1)).reshape(1, -1)
    onehot = ((bucket == jnp.arange(NUM_BUCKETS)[:, None]) & band).astype(F32)
    return onehot, jnp.where(band, 0.0, MASK_VALUE).astype(F32)


def _bias_table(rel_bias_t, onehot, maskrow, *, name):
    def body(rb_ref, oh_ref, mask_ref, o_ref):
        o_ref[...] = _dot(rb_ref[...], oh_ref[...], NN, HIGHEST) + mask_ref[...]

    return pl.pallas_call(body, name=name, out_shape=jax.ShapeDtypeStruct((SWA_HEADS, onehot.shape[1]), F32),
                          compiler_params=_params())(rel_bias_t, onehot, maskrow)


def _bias_grad(dbias2d, onehot, *, name):
    def body(db_ref, oh_ref, o_ref):
        o_ref[...] = _dot(db_ref[...], oh_ref[...], NT, HIGHEST)

    return pl.pallas_call(body, name=name, out_shape=jax.ShapeDtypeStruct((SWA_HEADS, NUM_BUCKETS), F32),
                          compiler_params=_params())(dbias2d, onehot)


GROUP_LANES = SWA_GROUP * SWA_BLOCK


def _swa_operands(zq_ref, kv_cur_ref, kv_prev_ref):
    q = (zq_ref[:, 0:1024] * (SWA_HEAD_DIM ** -0.5)).astype(BF16)
    kv_c = kv_cur_ref[...].astype(BF16)
    kv_p = kv_prev_ref[...].astype(BF16)
    kks = [jnp.concatenate([kv_p[:, g * 64:(g + 1) * 64], kv_c[:, g * 64:(g + 1) * 64]], axis=0) for g in range(SWA_KV_HEADS)]
    vvs = [jnp.concatenate([kv_p[:, 128 + g * 64:128 + (g + 1) * 64], kv_c[:, 128 + g * 64:128 + (g + 1) * 64]], axis=0)
           for g in range(SWA_KV_HEADS)]
    return q, kks, vvs


SWA_PART_HEADS = 8
SWA_PARTS = [(h0 // SWA_GROUP, h0) for h0 in range(0, SWA_HEADS, SWA_PART_HEADS)]


def _part_lanes(h0):
    return slice(h0 * SWA_BLOCK, (h0 + SWA_PART_HEADS) * SWA_BLOCK)


def _stack_heads(x, h0):
    return jnp.concatenate([x[:, h * SWA_HEAD_DIM:(h + 1) * SWA_HEAD_DIM] for h in range(h0, h0 + SWA_PART_HEADS)], axis=0)


def _heads_to_lanes(xt):
    pairs = []
    for j in range(0, xt.shape[1] // SWA_BLOCK, 2):
        two = jnp.concatenate([xt[:, j * SWA_BLOCK:(j + 1) * SWA_BLOCK], xt[:, (j + 1) * SWA_BLOCK:(j + 2) * SWA_BLOCK]], axis=0)
        pairs.append(two.T)
    return jnp.concatenate(pairs, axis=1)


def _swa_softmax(score_t, bias_ref, sink_ref, h0):
    sc = score_t + bias_ref[:, _part_lanes(h0)]
    sink = sink_ref[:, _part_lanes(h0)]
    m = jnp.maximum(jnp.max(sc, axis=0, keepdims=True), sink)
    e = jnp.exp(sc - m)
    e_sink = jnp.exp(sink - m)
    return e, 1.0 / (jnp.sum(e, axis=0, keepdims=True) + e_sink), e_sink


def _swa_tables(bias2d, sinks):
    bias_t = bias2d.reshape(SWA_HEADS, SWA_BLOCK, 2 * SWA_BLOCK).transpose(2, 0, 1).reshape(2 * SWA_BLOCK, SWA_HEADS * SWA_BLOCK)
    first = jnp.where(jnp.arange(2 * SWA_BLOCK)[:, None] < SWA_BLOCK, MASK_VALUE, bias_t)
    return jnp.stack([first, bias_t]), jnp.repeat(sinks, SWA_BLOCK, axis=1)


def _swa_fwd(zb, bias_tables, sink_lanes, *, name, exchanges=()):
    s = zb.shape[0]
    nb = s // SWA_BLOCK

    def body(zq_ref, kvc_ref, kvp_ref, bias_ref, sink_ref, o_ref):
        q, kks, vvs = _swa_operands(zq_ref, kvc_ref, kvp_ref)
        scores = [_dot(kks[g], _stack_heads(q, h0), NT) for g, h0 in SWA_PARTS]
        probs = []
        for score, (_, h0) in zip(scores, SWA_PARTS):
            e, inv, _ = _swa_softmax(score, bias_ref, sink_ref, h0)
            probs.append((e * inv).astype(BF16))
        outs = [_dot(vvs[g], p, TN) for p, (g, _) in zip(probs, SWA_PARTS)]
        o_ref[...] = jnp.concatenate([_heads_to_lanes(o) for o in outs], axis=1).astype(BF16)

    return _fused_call(
        body, name=name, grid=(nb,), out_shape=jax.ShapeDtypeStruct((s, D_MODEL), BF16),
        in_specs=[pl.BlockSpec((SWA_BLOCK, W_B), lambda n: (n, 0)),
                  pl.BlockSpec((SWA_BLOCK, 256), lambda n: (n, 4)),
                  pl.BlockSpec((SWA_BLOCK, 256), lambda n: (jnp.maximum(n - 1, 0), 4)),
                  pl.BlockSpec((None, 2 * SWA_BLOCK, SWA_HEADS * SWA_BLOCK), lambda n: (jnp.minimum(n, 1), 0, 0)),
                  _resident((1, SWA_HEADS * SWA_BLOCK))],
        out_specs=pl.BlockSpec((SWA_BLOCK, D_MODEL), lambda n: (n, 0)), scratch_shapes=[],
        operands=[zb, zb, zb, bias_tables, sink_lanes], exchanges=exchanges)


def _swa_bwd(zb, do_b, bias_tables, sink_lanes, *, name, exchanges=()):
    s = zb.shape[0]
    nb = s // SWA_BLOCK
    scale = SWA_HEAD_DIM ** -0.5

    def body(zq_ref, kvc_ref, kvp_ref, do_ref, bias_ref, sink_ref, dz_ref, dbias_ref, dsink_ref, carry, dsink_acc):
        step = pl.program_id(0)

        @pl.when(step == 0)
        def _():
            carry[...] = jnp.zeros_like(carry)
            dsink_acc[...] = jnp.zeros_like(dsink_acc)
            dbias_ref[...] = jnp.zeros_like(dbias_ref)

        q, kks, vvs = _swa_operands(zq_ref, kvc_ref, kvp_ref)
        do = do_ref[...].astype(BF16)
        parts = range(len(SWA_PARTS))
        q_rows = [_stack_heads(q, h0) for _, h0 in SWA_PARTS]
        do_rows = [_stack_heads(do, h0) for _, h0 in SWA_PARTS]
        scores = [_dot(kks[g], q_rows[i], NT) for i, (g, _) in enumerate(SWA_PARTS)]
        dps = [_dot(vvs[g], do_rows[i], NT) for i, (g, _) in enumerate(SWA_PARTS)]
        ps, dss = [], []
        for i, (_, h0) in enumerate(SWA_PARTS):
            e, inv, e_sink = _swa_softmax(scores[i], bias_ref, sink_ref, h0)
            p = e * inv
            delta = jnp.sum(p * dps[i], axis=0, keepdims=True)
            ds = p * (dps[i] - delta)
            dbias_ref[:, _part_lanes(h0)] += ds
            dsink_acc[:, _part_lanes(h0)] -= e_sink * inv * delta
            ps.append(p.astype(BF16))
            dss.append(ds.astype(BF16))
        dqs = [_dot(kks[g], dss[i], TN) * scale for i, (g, _) in enumerate(SWA_PARTS)]
        in_group = lambda xs, g, axis: jnp.concatenate([xs[i] for i in parts if SWA_PARTS[i][0] == g], axis=axis)
        dkks = [_dot(in_group(dss, g, 1), in_group(q_rows, g, 0), NN) for g in range(SWA_KV_HEADS)]
        dvvs = [_dot(in_group(ps, g, 1), in_group(do_rows, g, 0), NN) for g in range(SWA_KV_HEADS)]
        dkv = jnp.concatenate(dkks + dvvs, axis=1)
        dz_ref[:, 0:1024] = jnp.concatenate([_heads_to_lanes(dq) for dq in dqs], axis=1).astype(BF16)
        dz_ref[:, 1024:1280] = (dkv[SWA_BLOCK:, :] + carry[...]).astype(BF16)
        carry[...] = dkv[:SWA_BLOCK, :]

        @pl.when(step == nb - 1)
        def _():
            acc = dsink_acc[...]
            dsink_ref[...] = jnp.concatenate([jnp.sum(acc[:, h * SWA_BLOCK:(h + 1) * SWA_BLOCK], axis=1, keepdims=True)
                                              for h in range(SWA_HEADS)], axis=1)

    rev = lambda i: (nb - 1 - i, 0)
    table_shape = (2 * SWA_BLOCK, SWA_HEADS * SWA_BLOCK)
    return _fused_call(
        body, name=name, grid=(nb,),
        out_shape=(jax.ShapeDtypeStruct((s, W_B), BF16), jax.ShapeDtypeStruct(table_shape, F32), jax.ShapeDtypeStruct((1, SWA_HEADS), F32)),
        in_specs=[pl.BlockSpec((SWA_BLOCK, W_B), rev),
                  pl.BlockSpec((SWA_BLOCK, 256), lambda i: (nb - 1 - i, 4)),
                  pl.BlockSpec((SWA_BLOCK, 256), lambda i: (jnp.maximum(nb - 2 - i, 0), 4)),
                  pl.BlockSpec((SWA_BLOCK, D_MODEL), rev),
                  pl.BlockSpec((None,) + table_shape, lambda i: (jnp.minimum(nb - 1 - i, 1), 0, 0)),
                  _resident((1, SWA_HEADS * SWA_BLOCK))],
        out_specs=(pl.BlockSpec((SWA_BLOCK, W_B), rev), pl.BlockSpec(table_shape, lambda i: (0, 0)),
                   pl.BlockSpec((1, SWA_HEADS), lambda i: (0, 0))),
        scratch_shapes=[pltpu.VMEM((SWA_BLOCK, 256), F32), pltpu.VMEM((1, SWA_HEADS * SWA_BLOCK), F32)],
        operands=[zb, zb, zb, do_b, bias_tables, sink_lanes], exchanges=exchanges)


MEM_COLS = [slice(h * MEM_HEAD_DIM, (h + 1) * MEM_HEAD_DIM) for h in range(MEM_HEADS)]
MEM_VCOLS = [slice(D_MODEL + h * MEM_HEAD_DIM, D_MODEL + (h + 1) * MEM_HEAD_DIM) for h in range(MEM_HEADS)]


def _mem_probs(zc_ref, mkv_ref):
    qs = [(zc_ref[:, c] * (MEM_HEAD_DIM ** -0.5)).astype(BF16) for c in MEM_COLS]
    scores = [_dot(qs[h], mkv_ref[:, c], NT) for h, c in enumerate(MEM_COLS)]
    ps = []
    for sc in scores:
        e = jnp.exp(sc - jnp.max(sc, axis=-1, keepdims=True))
        ps.append(e / jnp.sum(e, axis=-1, keepdims=True))
    return qs, ps


def _mem_fwd(xb, wi_t, mkv, *, name):
    s = xb.shape[0]
    t = min(512, s)

    def body(x_ref, w_ref, mkv_ref, zc_ref, o_ref):
        zc_ref[...] = _dot(x_ref[...], w_ref[...], NT).astype(BF16)
        _, ps = _mem_probs(zc_ref, mkv_ref)
        ps = [p.astype(BF16) for p in ps]
        o_ref[...] = jnp.concatenate([_dot(ps[h], mkv_ref[:, vc], NN) for h, vc in enumerate(MEM_VCOLS)], axis=1).astype(BF16)

    row = pl.BlockSpec((t, D_MODEL), lambda i: (i, 0))
    return pl.pallas_call(
        body, name=name, grid=(s // t,), out_shape=(jax.ShapeDtypeStruct((s, D_MODEL), BF16),) * 2,
        in_specs=[row, _resident_rows(wi_t, W_A + W_B, W_C), _resident((MEM_LEN, 2 * D_MODEL))],
        out_specs=(row, row), compiler_params=_params(("parallel",)),
    )(xb, wi_t, mkv)


def _mem_bwd(xb, zc, do_c, mkv, *, name):
    s = zc.shape[0]
    t = min(512, s)
    nt = s // t

    def body(x_ref, zc_ref, do_ref, mkv_ref, dz_ref, dmkv_ref, gwi_ref, acc):
        @pl.when(pl.program_id(0) == 0)
        def _():
            dmkv_ref[...] = jnp.zeros_like(dmkv_ref)
            acc[...] = jnp.zeros_like(acc)

        heads = range(MEM_HEADS)
        qs, ps = _mem_probs(zc_ref, mkv_ref)
        dos = [do_ref[:, c].astype(BF16) for c in MEM_COLS]
        dps = [_dot(dos[h], mkv_ref[:, MEM_VCOLS[h]], NT) for h in heads]
        dss = [(ps[h] * (dps[h] - jnp.sum(ps[h] * dps[h], axis=-1, keepdims=True))).astype(BF16) for h in heads]
        ps = [p.astype(BF16) for p in ps]
        dz = jnp.concatenate([_dot(dss[h], mkv_ref[:, MEM_COLS[h]], NN) * (MEM_HEAD_DIM ** -0.5) for h in heads], axis=1).astype(BF16)
        dz_ref[...] = dz
        dmkv_ref[...] += jnp.concatenate([_dot(dss[h], qs[h], TN) for h in heads] + [_dot(ps[h], dos[h], TN) for h in heads], axis=1)
        acc[...] += _dot(dz, x_ref[...], TN)

        @pl.when(pl.program_id(0) == nt - 1)
        def _():
            pltpu.sync_copy(acc, gwi_ref.at[pl.ds(W_A + W_B, W_C), :])

    row = pl.BlockSpec((t, D_MODEL), lambda i: (i, 0))
    return pl.pallas_call(
        body, name=name, grid=(nt,),
        out_shape=(jax.ShapeDtypeStruct((s, D_MODEL), BF16), jax.ShapeDtypeStruct((MEM_LEN, 2 * D_MODEL), F32),
                   jax.ShapeDtypeStruct((IN_COLS, D_MODEL), F32)),
        in_specs=[row, row, row, _resident((MEM_LEN, 2 * D_MODEL))],
        out_specs=(row, pl.BlockSpec((MEM_LEN, 2 * D_MODEL), lambda i: (0, 0)), HBM),
        scratch_shapes=[pltpu.VMEM((W_C, D_MODEL), F32)],
        compiler_params=_params(("arbitrary",)),
    )(xb, zc, do_c, mkv)


def _normalize(pre):
    mu = jnp.mean(pre, axis=-1, keepdims=True)
    xc = pre - mu
    rstd = lax.rsqrt(jnp.mean(xc * xc, axis=-1, keepdims=True) + LN_EPS)
    return xc * rstd, rstd


def _layer_norm_bwd(dh, xhat, rstd, g):
    dxh = dh * g
    dpre = rstd * (dxh - jnp.mean(dxh, axis=-1, keepdims=True) - xhat * jnp.mean(dxh * xhat, axis=-1, keepdims=True))
    return dpre, jnp.sum(dh * xhat, axis=0, keepdims=True), jnp.sum(dh, axis=0, keepdims=True)


def _merge_fwd(o_a, o_b, o_c, x, wi_t, wbr, wo, *, name):
    s = x.shape[0]
    t = min(256, s)
    row = lambda w: pl.BlockSpec((t, w), lambda i: (i, 0))

    def body(oa_ref, ob_ref, oc_ref, x_ref, wg_ref, wbr_ref, wo_ref, zd_ref, xhat_ref, rstd_ref, merged_ref, pa_ref, pb_ref, pc_ref):
        zd_ref[...] = _dot(x_ref[...].astype(BF16), wg_ref[...], NT)
        merged = jnp.zeros((t, D_MODEL), F32)
        for b, (o_ref, p_ref) in enumerate(((oa_ref, pa_ref), (ob_ref, pb_ref), (oc_ref, pc_ref))):
            p = _dot(o_ref[...], wbr_ref[b], NN)
            p_ref[...] = p.astype(BF16)
            merged = merged + jax.nn.sigmoid(zd_ref[:, b * D_MODEL:(b + 1) * D_MODEL]) * p
        merged_b = merged.astype(BF16)
        merged_ref[...] = merged_b
        xhat, rstd = _normalize(ALPHA * x_ref[...] + _dot(merged_b, wo_ref[...], NN))
        xhat_ref[...] = xhat
        rstd_ref[...] = rstd

    act = jax.ShapeDtypeStruct((s, D_MODEL), F32)
    return pl.pallas_call(
        body, name=name, grid=(s // t,),
        out_shape=(jax.ShapeDtypeStruct((s, W_D), F32), act, jax.ShapeDtypeStruct((s, 1), F32)) + (jax.ShapeDtypeStruct((s, D_MODEL), BF16),) * 4,
        in_specs=[row(D_MODEL), row(D_MODEL), row(D_MODEL), row(D_MODEL), _resident_rows(wi_t, W_A + W_B + W_C, W_D),
                  _resident((3, D_MODEL, D_MODEL)), _resident((D_MODEL, D_MODEL))],
        out_specs=(row(W_D), row(D_MODEL), row(1), row(D_MODEL), row(D_MODEL), row(D_MODEL), row(D_MODEL)),
        compiler_params=_params(("parallel",)),
    )(o_a, o_b, o_c, x, wi_t, wbr, wo)


def _merge_bwd(dpre1, zd, pa, pb, pc, o_a, o_b, o_c, merged, wbr, wo, *, name, exchanges=()):
    s = dpre1.shape[0]
    t = min(256, s)
    nt = s // t
    row = lambda w: pl.BlockSpec((t, w), lambda i: (i, 0))

    def body(dpre_ref, zd_ref, pa_ref, pb_ref, pc_ref, oa_ref, ob_ref, oc_ref, mg_ref, wbr_ref, wo_ref,
             dzd_ref, doa_ref, dob_ref, doc_ref, gwa_ref, gwb_ref, gwc_ref, gwo_ref, acc):
        step = pl.program_id(0)

        @pl.when(step == 0)
        def _():
            acc[...] = jnp.zeros_like(acc)

        dpre_b = dpre_ref[...].astype(BF16)
        dmerged = _dot(dpre_b, wo_ref[...], NT)
        acc[3] += _dot(mg_ref[...], dpre_b, TN)
        branches = ((pa_ref, oa_ref, doa_ref), (pb_ref, ob_ref, dob_ref), (pc_ref, oc_ref, doc_ref))
        for b, (p_ref, o_ref, do_ref) in enumerate(branches):
            gate = jax.nn.sigmoid(zd_ref[:, b * D_MODEL:(b + 1) * D_MODEL])
            dzd_ref[:, b * D_MODEL:(b + 1) * D_MODEL] = (dmerged * p_ref[...] * gate * (1.0 - gate)).astype(BF16)
            dp = (dmerged * gate).astype(BF16)
            acc[b] += _dot(o_ref[...], dp, TN)
            do_ref[...] = _dot(dp, wbr_ref[b], NT).astype(do_ref.dtype)

        @pl.when(step == nt - 1)
        def _():
            for b, gw_ref in enumerate((gwa_ref, gwb_ref, gwc_ref, gwo_ref)):
                pltpu.sync_copy(acc.at[b], gw_ref)

    act = jax.ShapeDtypeStruct((s, D_MODEL), F32)
    actb = jax.ShapeDtypeStruct((s, D_MODEL), BF16)
    gw = jax.ShapeDtypeStruct((D_MODEL, D_MODEL), F32)
    return _fused_call(
        body, name=name, grid=(nt,),
        out_shape=(jax.ShapeDtypeStruct((s, W_D), BF16), act, actb, actb, gw, gw, gw, gw),
        in_specs=[row(D_MODEL), row(W_D)] + [row(D_MODEL)] * 7 + [_resident((3, D_MODEL, D_MODEL)), _resident((D_MODEL, D_MODEL))],
        out_specs=(row(W_D),) + (row(D_MODEL),) * 3 + (HBM,) * 4, scratch_shapes=[pltpu.VMEM((4, D_MODEL, D_MODEL), F32)],
        operands=[dpre1, zd, pa, pb, pc, o_a, o_b, o_c, merged, wbr, wo], exchanges=exchanges)


def _mlp_loss(xhat1, rstd1, target, ln1_g, ln1_b, ln2_g, ln2_b, wu, wd, *, name):
    s = xhat1.shape[0]
    t = min(256, s)
    npan = wu.shape[0]
    row = lambda w: pl.BlockSpec((t, w), lambda i: (i, 0))
    vec = _resident((1, D_MODEL))

    def body(xhat_ref, rstd_ref, tgt_ref, g1_ref, b1_ref, g2_ref, b2_ref, wu_ref, wd_ref,
             dpre1_ref, dpre2_ref, h1_ref, a_ref, du_ref, stats_ref):
        @pl.when(pl.program_id(0) == 0)
        def _():
            stats_ref[...] = jnp.zeros_like(stats_ref)

        xhat1_v = xhat_ref[...]
        h1 = xhat1_v * g1_ref[...] + b1_ref[...]
        h1_b = h1.astype(BF16)
        h1_ref[...] = h1_b
        us = []
        ff = jnp.zeros((t, D_MODEL), F32)
        for j in range(npan):
            u = _dot(h1_b, wu_ref[j], NN)
            us.append(u)
            r = jnp.maximum(u, 0.0)
            a_b = (r * r).astype(BF16)
            a_ref[:, j * D_MODEL:(j + 1) * D_MODEL] = a_b
            ff = ff + _dot(a_b, wd_ref[j], NN)
        xhat2, rstd2 = _normalize(ALPHA * h1 + ff)
        err = xhat2 * g2_ref[...] + b2_ref[...] - tgt_ref[...]
        stats_ref[4:5, :] += jnp.sum(err * err, axis=0, keepdims=True)
        dpre2, dg2, db2 = _layer_norm_bwd(err * (1.0 / D_MODEL), xhat2, rstd2, g2_ref[...])
        stats_ref[0:1, :] += dg2
        stats_ref[1:2, :] += db2
        dpre2_b = dpre2.astype(BF16)
        dpre2_ref[...] = dpre2_b
        dh1 = ALPHA * dpre2
        for j in range(npan):
            du_b = (_dot(dpre2_b, wd_ref[j], NT) * (2.0 * jnp.maximum(us[j], 0.0))).astype(BF16)
            du_ref[:, j * D_MODEL:(j + 1) * D_MODEL] = du_b
            dh1 = dh1 + _dot(du_b, wu_ref[j], NT)
        dpre1, dg1, db1 = _layer_norm_bwd(dh1, xhat1_v, rstd_ref[...], g1_ref[...])
        stats_ref[2:3, :] += dg1
        stats_ref[3:4, :] += db1
        dpre1_ref[...] = dpre1

    actb = jax.ShapeDtypeStruct((s, D_MODEL), BF16)
    wide = jax.ShapeDtypeStruct((s, D_FF), BF16)
    return pl.pallas_call(
        body, name=name, grid=(s // t,),
        out_shape=(jax.ShapeDtypeStruct((s, D_MODEL), F32), actb, actb, wide, wide, jax.ShapeDtypeStruct((8, D_MODEL), F32)),
        in_specs=[row(D_MODEL), row(1), row(D_MODEL), vec, vec, vec, vec,
                  _resident((npan, D_MODEL, D_MODEL)), _resident((npan, D_MODEL, D_MODEL))],
        out_specs=(row(D_MODEL), row(D_MODEL), row(D_MODEL), row(D_FF), row(D_FF), pl.BlockSpec((8, D_MODEL), lambda i: (0, 0))),
        compiler_params=_params(("arbitrary",)),
    )(xhat1, rstd1, target, ln1_g, ln1_b, ln2_g, ln2_b, wu, wd)


BRANCH_WEIGHTS = ("w_branch_hg", "w_branch_swa", "w_branch_mem")


def _local_step(x, mem, target, wi_t, wmkv, late, lb_logits, gain, sinks, rel_bias, ln1_g, ln1_b, ln2_g, ln2_b, *, distributed):
    s = x.shape[0]
    tm = min(1024, s)
    tk = min(2048, s)
    xb = x.astype(BF16)
    memb = mem.astype(BF16)
    if distributed:
        cx, cy, cc = lax.axis_index("x"), lax.axis_index("y"), lax.axis_index("c")
        pos = jnp.stack([2 * cx + cy, cc]).astype(jnp.int32)
    gather = (lambda names: [_gather_exchange([late[k] for k in names])]) if distributed else (lambda names: [])
    to_sibling = (lambda grads: [_sibling_halves_exchange(grads)]) if distributed else (lambda grads: [])
    to_chips = (lambda sums: [_chip_partials_exchange([bf for bf, _ in sums])]) if distributed else (lambda sums: [])

    def chip_sums(names, grads, from_sibling):
        return [_add_sibling(g, o, pos, name="add_sibling_" + k) for k, g, o in zip(names, grads, from_sibling)]

    def shard_sums(names, sums, from_chips):
        return {k: _add_chips(mine, o, pos, name="add_chips_" + k) for k, (_, mine), o in zip(names, sums, from_chips)}

    zb = _mm(xb, wi_t, mode="nt", tm=tm, tn=W_B, tk=D_MODEL, name="proj_b", out_dtype=BF16, b_rows=(W_A, W_B))
    mkv = _mm(memb, wmkv, mode="nn", tm=MEM_LEN, tn=512, tk=D_MODEL, name="mem_kv", out_dtype=BF16, b_panels=True)
    onehot, maskrow = _bias_selector()
    bias_tables, sink_lanes = _swa_tables(_bias_table(rel_bias.T, onehot, maskrow, name="bias_table"), sinks)
    (za, o_a, o_raw, states), landed = _hgrn_fwd(xb, wi_t, lb_logits, gain, name="hgrn_fwd", exchanges=gather(("w_up", "w_down")))
    wu, wd = landed[0] if distributed else (late["wu"], late["wd"])
    o_b, landed = _swa_fwd(zb, bias_tables, sink_lanes, name="swa_fwd", exchanges=gather(BRANCH_WEIGHTS + ("w_out",)))
    if distributed:
        wbr = jnp.stack([wb.reshape(D_MODEL, D_MODEL) for wb in landed[0][:3]])
        wo = landed[0][3].reshape(D_MODEL, D_MODEL)
    else:
        wbr, wo = late["wbr"], late["wo"]
    zc, o_c = _mem_fwd(xb, wi_t, mkv, name="mem_fwd")
    zd, xhat1, rstd1, merged, pa, pb, pc = _merge_fwd(o_a, o_b, o_c, x, wi_t, wbr, wo, name="merge_fwd")

    dpre1, dpre2, h1, act, du, ln_stats = _mlp_loss(xhat1, rstd1, target, ln1_g, ln1_b, ln2_g, ln2_b, wu, wd, name="mlp_loss")
    ffn = ("w_down", "w_up")
    g_ffn = [_mm(act, dpre2, mode="tn", tm=1024, tn=D_MODEL, tk=tk, name="grad_w_down").reshape(N_SHARDS, D_FF // N_SHARDS, D_MODEL),
             _mm(h1, du, mode="tn", tm=D_MODEL, tn=1024, tk=tk, name="grad_w_up", out_panels=True)]

    (dzd, do_a, do_b, do_c, *g_merge), landed = _merge_bwd(dpre1, zd, pa, pb, pc, o_a, o_b, o_c, merged, wbr, wo, name="merge_bwd",
                                                           exchanges=to_sibling(g_ffn))
    sums_ffn = chip_sums(ffn, g_ffn, landed[0]) if distributed else []
    merge = BRANCH_WEIGHTS + ("w_out",)
    g_merge = [g.reshape(N_SHARDS, D_MODEL // N_SHARDS, D_MODEL) for g in g_merge]
    (dza, hg_stats), landed = _hgrn_bwd(za, o_raw, do_a, states, lb_logits, gain, name="hgrn_bwd",
                                        exchanges=to_chips(sums_ffn) + to_sibling(g_merge))
    halves = shard_sums(ffn, sums_ffn, landed[0]) if distributed else {}
    sums_merge = chip_sums(merge, g_merge, landed[1]) if distributed else []
    (dzb, dbias_t, dsinks), landed = _swa_bwd(zb, do_b, bias_tables, sink_lanes, name="swa_bwd", exchanges=to_chips(sums_merge))
    if distributed:
        halves.update(shard_sums(merge, sums_merge, landed[0]))
    dbias = dbias_t.reshape(2 * SWA_BLOCK, SWA_HEADS, SWA_BLOCK).transpose(1, 2, 0).reshape(SWA_HEADS, -1)
    d_rel_bias = _bias_grad(dbias, onehot, name="bias_grad").T
    dzc, dmkv, g_wi = _mem_bwd(xb, zc, do_c, mkv, name="mem_bwd")

    proj = ("w_in", "w_mem_kv")
    for dz, offset, nm in ((dza, 0, "grad_w_in_a"), (dzb, W_A, "grad_w_in_b"), (dzd, W_A + W_B + W_C, "grad_w_in_d")):
        g_wi = _mm(dz, xb, mode="tn", tm=dz.shape[1] if dz.shape[1] <= 1280 else 1024, tn=D_MODEL, tk=tk, name=nm,
                   rows_of=IN_COLS, row_offset=offset, into=g_wi)
    g_proj = [g_wi.reshape(N_SHARDS, IN_COLS // N_SHARDS, D_MODEL),
              _mm(memb, dmkv, mode="tn", tm=D_MODEL, tn=512, tk=MEM_LEN, name="grad_w_mem_kv", out_panels=True)]
    sums_proj = chip_sums(proj, g_proj, _run_exchanges(to_sibling(g_proj), name="reduce_sibling_proj")[0]) if distributed else []
    small = dict(lb_logits=hg_stats[1:3], hg_norm_gain=hg_stats[0:1], swa_sinks=dsinks, rel_bias=d_rel_bias,
                 ln1_g=ln_stats[2:3], ln1_b=ln_stats[3:4], ln2_g=ln_stats[0:1], ln2_b=ln_stats[1:2], sq_err=ln_stats[4:5])
    dx_tm = min(512, s)
    small_exchange = [_small_gather_exchange(_pack_small(small, name="pack_small"))] if distributed else []
    grad_x, landed = _dx_matmul([dza, dzb, dzc, dzd], wi_t, dpre1, tm=dx_tm, name="grad_x", tiles=(0, s // dx_tm),
                                exchanges=to_chips(sums_proj) + small_exchange)
    if distributed:
        halves.update(shard_sums(proj, sums_proj, landed[0]))
        small = landed[1][0]
    else:
        halves = dict(zip(ffn + merge + proj, g_ffn + g_merge + g_proj))
    return grad_x, halves, small


def _mesh_position():
    x, y, c = lax.axis_index("x"), lax.axis_index("y"), lax.axis_index("c")
    chips = [(1 - x, y), (x, 1 - y), (1 - x, 1 - y)]
    return x, y, c, chips


class _Exchange(NamedTuple):
    operands: list
    out_shapes: list
    n_sems: int
    start: Callable
    finish: Callable


def _gather_exchange(shards):
    n = len(shards)
    per = 7

    def plan(ins, outs, send_sems, recv_sems):
        x, y, c, chips = _mesh_position()
        me = 2 * x + y
        sibling = (x, y, 1 - c)

        def half(a, slot, hc):
            rh = shards[a].shape[0] // 2
            return outs[a].at[slot, pl.ds(hc * rh, rh), :]

        def copy(a, k, src, dst, to):
            return pltpu.make_async_remote_copy(src_ref=src, dst_ref=dst, send_sem=send_sems.at[a * per + k], recv_sem=recv_sems.at[a * per + k],
                                                device_id=to, device_id_type=MESH)

        own = [copy(a, 6, ins[a], outs[a].at[me], sibling) for a in range(n)]
        to_chips = [copy(a, k, ins[a].at[pl.ds(c * (shards[a].shape[0] // 2), shards[a].shape[0] // 2), :], half(a, me, c), (cx, cy, c))
                    for k, (cx, cy) in enumerate(chips) for a in range(n)]
        arrived = [copy(a, k, half(a, 2 * cx + cy, c), half(a, 2 * cx + cy, c), (cx, cy, c)) for k, (cx, cy) in enumerate(chips) for a in range(n)]
        passed_on = [copy(a, 3 + k, half(a, 2 * cx + cy, c), half(a, 2 * cx + cy, c), sibling) for k, (cx, cy) in enumerate(chips) for a in range(n)]
        from_sibling = [copy(a, 3 + k, half(a, 2 * cx + cy, 1 - c), half(a, 2 * cx + cy, 1 - c), sibling)
                        for k, (cx, cy) in enumerate(chips) for a in range(n)]
        own_arrived = [copy(a, 6, outs[a].at[me], outs[a].at[me], sibling) for a in range(n)]
        return own, to_chips, arrived, passed_on, from_sibling, own_arrived

    def start(*refs):
        own, to_chips, _, _, _, _ = plan(*refs)
        for cp in own + to_chips:
            cp.start()

    def finish(*refs):
        own, to_chips, arrived, passed_on, from_sibling, own_arrived = plan(*refs)
        for landed, onward in zip(arrived, passed_on):
            landed.wait_recv()
            onward.start()
        for cp in from_sibling + own_arrived:
            cp.wait_recv()
        for cp in own + to_chips + passed_on:
            cp.wait_send()

    return _Exchange(list(shards), [jax.ShapeDtypeStruct((N_SHARDS,) + w.shape, w.dtype) for w in shards], per * n, start, finish)


def _sibling_halves_exchange(grads):
    n = len(grads)

    def plan(ins, outs, send_sems, recv_sems):
        x, y, c, _ = _mesh_position()
        return [pltpu.make_async_remote_copy(src_ref=ins[a].at[:, pl.ds((1 - c) * (grads[a].shape[1] // 2), grads[a].shape[1] // 2), :],
                                             dst_ref=outs[a], send_sem=send_sems.at[a], recv_sem=recv_sems.at[a],
                                             device_id=(x, y, 1 - c), device_id_type=MESH) for a in range(n)]

    def start(*refs):
        for cp in plan(*refs):
            cp.start()

    def finish(*refs):
        for cp in plan(*refs):
            cp.wait()

    return _Exchange(list(grads), [jax.ShapeDtypeStruct((g.shape[0], g.shape[1] // 2, g.shape[2]), g.dtype) for g in grads], n, start, finish)


def _chip_partials_exchange(sums):
    n = len(sums)

    def plan(ins, outs, send_sems, recv_sems):
        _, _, c, chips = _mesh_position()
        return [pltpu.make_async_remote_copy(src_ref=ins[a].at[2 * cx + cy], dst_ref=outs[a].at[k], send_sem=send_sems.at[a * 3 + k],
                                             recv_sem=recv_sems.at[a * 3 + k], device_id=(cx, cy, c), device_id_type=MESH)
                for k, (cx, cy) in enumerate(chips) for a in range(n)]

    def start(*refs):
        for cp in plan(*refs):
            cp.start()

    def finish(*refs):
        for cp in plan(*refs):
            cp.wait()

    return _Exchange(list(sums), [jax.ShapeDtypeStruct((3,) + g.shape[1:], g.dtype) for g in sums], 3 * n, start, finish)


def _fused_call(body, *, name, grid, in_specs, out_specs, out_shape, scratch_shapes, operands, exchanges=(), aliases=None):
    single = not isinstance(out_shape, (tuple, list))
    out_specs = [out_specs] if single else list(out_specs)
    out_shape = [out_shape] if single else list(out_shape)
    n_in, n_out, n_scr = len(in_specs), len(out_specs), len(scratch_shapes)
    x_in = [len(e.operands) for e in exchanges]
    x_out = [len(e.out_shapes) for e in exchanges]

    def wrapped(*refs):
        refs = list(refs)
        ins = refs[:n_in]
        pos = n_in
        ex_ins = []
        for k in x_in:
            ex_ins.append(refs[pos:pos + k])
            pos += k
        outs = refs[pos:pos + n_out]
        pos += n_out
        ex_outs = []
        for k in x_out:
            ex_outs.append(refs[pos:pos + k])
            pos += k
        scratch = refs[pos:pos + n_scr]
        sems = refs[pos + n_scr:]
        first, last = None, None
        for axis, size in enumerate(grid):
            at_start, at_end = pl.program_id(axis) == 0, pl.program_id(axis) == size - 1
            first = at_start if first is None else first & at_start
            last = at_end if last is None else last & at_end

        @pl.when(first)
        def _():
            for i, e in enumerate(exchanges):
                e.start(ex_ins[i], ex_outs[i], sems[2 * i], sems[2 * i + 1])

        body(*ins, *outs, *scratch)

        @pl.when(last)
        def _():
            for i, e in enumerate(exchanges):
                e.finish(ex_ins[i], ex_outs[i], sems[2 * i], sems[2 * i + 1])

    n_x_in, n_x_out = sum(x_in), sum(x_out)
    results = pl.pallas_call(
        wrapped if exchanges else body, name=name, grid=grid,
        in_specs=list(in_specs) + [HBM] * n_x_in,
        out_specs=out_specs + [HBM] * n_x_out,
        out_shape=out_shape + [s for e in exchanges for s in e.out_shapes],
        scratch_shapes=list(scratch_shapes) + [pltpu.SemaphoreType.DMA((e.n_sems,)) for e in exchanges for _ in range(2)],
        input_output_aliases=aliases or {}, compiler_params=_params(("arbitrary",) * len(grid)),
    )(*operands, *[a for e in exchanges for a in e.operands])
    own = results[0] if single else tuple(results[:n_out])
    landed, pos = [], n_out
    for k in x_out:
        landed.append(list(results[pos:pos + k]))
        pos += k
    return own, landed


def _run_exchanges(exchanges, *, name):
    def body(*refs):
        n_in = sum(len(e.operands) for e in exchanges)
        n_out = sum(len(e.out_shapes) for e in exchanges)
        ins, outs, sems = refs[:n_in], refs[n_in:n_in + n_out], refs[n_in + n_out:]
        spans, i, o = [], 0, 0
        for e in exchanges:
            spans.append((ins[i:i + len(e.operands)], outs[o:o + len(e.out_shapes)]))
            i, o = i + len(e.operands), o + len(e.out_shapes)
        for k, e in enumerate(exchanges):
            e.start(*spans[k], sems[2 * k], sems[2 * k + 1])
        for k, e in enumerate(exchanges):
            e.finish(*spans[k], sems[2 * k], sems[2 * k + 1])

    operands = [a for e in exchanges for a in e.operands]
    shapes = [s for e in exchanges for s in e.out_shapes]
    results = pl.pallas_call(
        body, name=name, out_shape=shapes, in_specs=[HBM] * len(operands), out_specs=[HBM] * len(shapes),
        scratch_shapes=[pltpu.SemaphoreType.DMA((e.n_sems,)) for e in exchanges for _ in range(2)],
    )(*operands)
    landed, pos = [], 0
    for e in exchanges:
        landed.append(list(results[pos:pos + len(e.out_shapes)]))
        pos += len(e.out_shapes)
    return landed


ROW_TILE_MAX = 640
BF16_SUBLANES = 16


def _row_tile(rows):
    for tr in range(min(rows, ROW_TILE_MAX), 0, -1):
        if rows % tr == 0 and tr % BF16_SUBLANES == 0:
            return tr
    raise ValueError(rows)


def _add_sibling(grad, other, pos, *, name):
    p, r, cols = grad.shape
    rh = r // 2
    tr = _row_tile(rh)
    nb = rh // tr

    def body(pos_ref, g_ref, o_ref, sb_ref, mine_ref):
        total = g_ref[...] + o_ref[...]
        sb_ref[...] = total.astype(BF16)

        @pl.when(pl.program_id(1) == pos_ref[0])
        def _():
            mine_ref[...] = total

    return pl.pallas_call(
        body, name=name, out_shape=(jax.ShapeDtypeStruct((p, rh, cols), BF16), jax.ShapeDtypeStruct((rh, cols), F32)),
        grid_spec=pltpu.PrefetchScalarGridSpec(
            num_scalar_prefetch=1, grid=(nb, p),
            in_specs=[pl.BlockSpec((None, tr, cols), lambda i, j, pos_ref: (j, pos_ref[1] * nb + i, 0)),
                      pl.BlockSpec((None, tr, cols), lambda i, j, pos_ref: (j, i, 0))],
            out_specs=(pl.BlockSpec((None, tr, cols), lambda i, j, pos_ref: (j, i, 0)),
                       pl.BlockSpec((tr, cols), lambda i, j, pos_ref: (i, 0)))),
        compiler_params=_params(("parallel", "arbitrary")),
    )(pos, grad, other)


def _add_chips(mine, others, pos, *, name):
    rh, cols = mine.shape
    tr = _row_tile(rh)
    nb = rh // tr

    def body(pos_ref, s_ref, o_ref, r_ref):
        r_ref[...] = ((s_ref[...] + o_ref[0].astype(F32)) + o_ref[1].astype(F32)) + o_ref[2].astype(F32)

    return pl.pallas_call(
        body, name=name, out_shape=jax.ShapeDtypeStruct((2 * rh, cols), F32),
        grid_spec=pltpu.PrefetchScalarGridSpec(
            num_scalar_prefetch=1, grid=(nb,),
            in_specs=[pl.BlockSpec((tr, cols), lambda i, pos_ref: (i, 0)),
                      pl.BlockSpec((3, tr, cols), lambda i, pos_ref: (0, i, 0))],
            out_specs=pl.BlockSpec((tr, cols), lambda i, pos_ref: (pos_ref[1] * nb + i, 0))),
        compiler_params=_params(("parallel",)),
    )(pos, mine, others)


def _join_halves(bufs, *, name):
    n = len(bufs)

    def body(*refs):
        ins, outs = refs[:n], refs[n:2 * n]
        send_sems, recv_sems = refs[2 * n:]
        x, y, c, _ = _mesh_position()

        def copy(a, hc):
            rh = bufs[a].shape[0] // 2
            rows = pl.ds(hc * rh, rh)
            return pltpu.make_async_remote_copy(src_ref=ins[a].at[rows, :], dst_ref=outs[a].at[rows, :], send_sem=send_sems.at[a],
                                                recv_sem=recv_sems.at[a], device_id=(x, y, 1 - c), device_id_type=MESH)

        for a in range(n):
            copy(a, c).start()
        for a in range(n):
            copy(a, c).wait_send()
            copy(a, 1 - c).wait_recv()

    return pl.pallas_call(
        body, name=name, out_shape=[jax.ShapeDtypeStruct(b.shape, b.dtype) for b in bufs],
        in_specs=[HBM] * n, out_specs=[HBM] * n, input_output_aliases={a: a for a in range(n)},
        scratch_shapes=[pltpu.SemaphoreType.DMA((n,)), pltpu.SemaphoreType.DMA((n,))],
    )(*bufs)


SMALL = ["lb_logits", "hg_norm_gain", "swa_sinks", "rel_bias", "ln1_g", "ln1_b", "ln2_g", "ln2_b"]
PACK_ROWS = 48
PACK_AT = dict(lb_logits=(slice(0, 2), slice(0, D_MODEL)), hg_norm_gain=(slice(2, 3), slice(0, D_MODEL)), ln1_g=(slice(3, 4), slice(0, D_MODEL)),
               ln1_b=(slice(4, 5), slice(0, D_MODEL)), ln2_g=(slice(5, 6), slice(0, D_MODEL)), ln2_b=(slice(6, 7), slice(0, D_MODEL)),
               swa_sinks=(slice(7, 8), slice(0, SWA_HEADS)), sq_err=(slice(8, 9), slice(0, D_MODEL)),
               rel_bias=(slice(16, 16 + NUM_BUCKETS), slice(0, SWA_HEADS)))


def _pack_small(grads, *, name):
    names = SMALL + ["sq_err"]

    def body(*refs):
        packed = refs[len(names)]
        packed[...] = jnp.zeros_like(packed)
        for k, g_ref in zip(names, refs):
            packed[PACK_AT[k]] = g_ref[...]

    return pl.pallas_call(body, name=name, out_shape=jax.ShapeDtypeStruct((PACK_ROWS, D_MODEL), F32), compiler_params=_params(),
                          )(*[grads[k] for k in names])


def _small_gather_exchange(packed):
    def plan(ins, outs, send_sems, recv_sems):
        x, y, c, _ = _mesh_position()
        me = 4 * x + 2 * y + c
        own = pltpu.make_async_copy(ins[0], outs[0].at[me], send_sems.at[7])
        remote = []
        for d in range(1, 8):
            dx, dy, dc = (d >> 2) & 1, (d >> 1) & 1, d & 1
            remote.append(pltpu.make_async_remote_copy(src_ref=ins[0], dst_ref=outs[0].at[me], send_sem=send_sems.at[d - 1],
                                                       recv_sem=recv_sems.at[d - 1], device_id=(x ^ dx, y ^ dy, c ^ dc), device_id_type=MESH))
        return own, remote

    def start(*refs):
        own, remote = plan(*refs)
        own.start()
        for cp in remote:
            cp.start()

    def finish(*refs):
        own, remote = plan(*refs)
        for cp in remote:
            cp.wait()
        own.wait()

    return _Exchange([packed], [jax.ShapeDtypeStruct((8,) + packed.shape, packed.dtype)], 8, start, finish)


def _adamw_small(gathered, w, m, v, *, name):
    names = SMALL
    n = len(names)

    def body(*refs):
        gathered_ref = refs[0]
        w_refs, m_refs, v_refs = (dict(zip(names, refs[1 + i * n:1 + (i + 1) * n])) for i in range(3))
        loss_ref = refs[1 + 3 * n]
        go_refs, d_refs, nm_refs, nv_refs = (dict(zip(names, refs[2 + (3 + i) * n:2 + (4 + i) * n])) for i in range(4))
        total_ref = refs[2 + 7 * n]
        total = gathered_ref[0]
        for j in range(1, 8):
            total = total + gathered_ref[j]
        total_ref[...] = total
        loss_ref[...] = (0.5 / D_MODEL) * jnp.sum(total_ref[PACK_AT["sq_err"]], axis=1, keepdims=True)
        for k in names:
            g = total_ref[PACK_AT[k]]
            go_refs[k][...] = g
            d_refs[k][...], nm_refs[k][...], nv_refs[k][...] = _adamw_math(w_refs[k][...], g, m_refs[k][...], v_refs[k][...])

    like = [jax.ShapeDtypeStruct(w[k].shape, F32) for k in names]
    results = pl.pallas_call(body, name=name, out_shape=[jax.ShapeDtypeStruct((1, 1), F32)] + like * 4,
                             scratch_shapes=[pltpu.VMEM((PACK_ROWS, D_MODEL), F32)],
                             compiler_params=_params())(gathered, *[d[k] for d in (w, m, v) for k in names])
    return results[0], {k: tuple(results[1 + i * n + j] for i in range(4)) for j, k in enumerate(names)}


def _adamw_math(w, g, m, v):
    m = ADAM_B1 * m + (1.0 - ADAM_B1) * g
    v = ADAM_B2 * v + (1.0 - ADAM_B2) * (g * g)
    m_hat = m / (1.0 - ADAM_B1 ** ADAM_STEP)
    v_hat = v / (1.0 - ADAM_B2 ** ADAM_STEP)
    delta = -ADAM_LR * (m_hat / (jnp.sqrt(v_hat) + ADAM_EPS) + ADAM_WD * w)
    return delta, m, v


def _adamw(w, g, m, v, *, name):
    _, rows, cols = w.shape
    tr = _row_tile(rows)
    blk = pl.BlockSpec((None, tr, cols), lambda i: (0, i, 0))
    flat = pl.BlockSpec((tr, cols), lambda i: (i, 0))

    def body(w_ref, g_ref, m_ref, v_ref, go_ref, d_ref, nm_ref, nv_ref):
        g_v = g_ref[...]
        go_ref[...] = g_v
        d_ref[...], nm_ref[...], nv_ref[...] = _adamw_math(w_ref[...], g_v, m_ref[...], v_ref[...])

    shape = jax.ShapeDtypeStruct((1, rows, cols), F32)
    return pl.pallas_call(body, name=name, grid=(rows // tr,), out_shape=(shape,) * 4, in_specs=[blk, flat, blk, blk], out_specs=(blk,) * 4,
                          compiler_params=_params(("parallel",)))(w, g, m, v)


WEIGHTS = ["w_in", "lb_logits", "hg_norm_gain", "swa_sinks", "rel_bias", "w_mem_kv", "w_branch_hg", "w_branch_swa", "w_branch_mem",
           "w_out", "ln1_g", "ln1_b", "w_up", "w_down", "ln2_g", "ln2_b"]
BIG = ["w_in", "w_mem_kv", "w_branch_hg", "w_branch_swa", "w_branch_mem", "w_out", "w_up", "w_down"]


def kernel(x, mem, w_in, lb_logits, hg_norm_gain, swa_sinks, rel_bias, w_mem_kv, w_branch_hg, w_branch_swa, w_branch_mem, w_out, ln1_g, ln1_b, w_up, w_down, ln2_g, ln2_b, loss_target, m_w_in, m_lb_logits, m_hg_norm_gain, m_swa_sinks, m_rel_bias, m_w_mem_kv, m_w_branch_hg, m_w_branch_swa, m_w_branch_mem, m_w_out, m_ln1_g, m_ln1_b, m_w_up, m_w_down, m_ln2_g, m_ln2_b, v_w_in, v_lb_logits, v_hg_norm_gain, v_swa_sinks, v_rel_bias, v_w_mem_kv, v_w_branch_hg, v_w_branch_swa, v_w_branch_mem, v_w_out, v_ln1_g, v_ln1_b, v_w_up, v_w_down, v_ln2_g, v_ln2_b):
    w = dict(w_in=w_in, lb_logits=lb_logits, hg_norm_gain=hg_norm_gain, swa_sinks=swa_sinks, rel_bias=rel_bias, w_mem_kv=w_mem_kv,
             w_branch_hg=w_branch_hg, w_branch_swa=w_branch_swa, w_branch_mem=w_branch_mem, w_out=w_out, ln1_g=ln1_g, ln1_b=ln1_b,
             w_up=w_up, w_down=w_down, ln2_g=ln2_g, ln2_b=ln2_b)
    m = dict(w_in=m_w_in, lb_logits=m_lb_logits, hg_norm_gain=m_hg_norm_gain, swa_sinks=m_swa_sinks, rel_bias=m_rel_bias, w_mem_kv=m_w_mem_kv,
             w_branch_hg=m_w_branch_hg, w_branch_swa=m_w_branch_swa, w_branch_mem=m_w_branch_mem, w_out=m_w_out, ln1_g=m_ln1_g, ln1_b=m_ln1_b,
             w_up=m_w_up, w_down=m_w_down, ln2_g=m_ln2_g, ln2_b=m_ln2_b)
    v = dict(w_in=v_w_in, lb_logits=v_lb_logits, hg_norm_gain=v_hg_norm_gain, swa_sinks=v_swa_sinks, rel_bias=v_rel_bias, w_mem_kv=v_w_mem_kv,
             w_branch_hg=v_w_branch_hg, w_branch_swa=v_w_branch_swa, w_branch_mem=v_w_branch_mem, w_out=v_w_out, ln1_g=v_ln1_g, ln1_b=v_ln1_b,
             w_up=v_w_up, w_down=v_w_down, ln2_g=v_ln2_g, ln2_b=v_ln2_b)
    shapes = {k: w[k].shape for k in WEIGHTS}
    for d in (w, m, v):
        d["w_in"] = d["w_in"].reshape(D_MODEL, IN_COLS // N_SHARDS).T[None]
    shards = {k: w[k].reshape(w[k].shape[-2], w[k].shape[-1]).astype(BF16) for k in BIG}
    wi4, wmkv = _run_exchanges([_gather_exchange([shards["w_in"], shards["w_mem_kv"]])], name="gather_weights")[0]
    wi_t = wi4.reshape(IN_COLS, D_MODEL)

    grad_x, halves, small = _local_step(
        x.reshape(x.shape[-2], D_MODEL), mem.reshape(MEM_LEN, D_MODEL), loss_target.reshape(loss_target.shape[-2], D_MODEL),
        wi_t, wmkv, shards, lb_logits, hg_norm_gain, swa_sinks, rel_bias, ln1_g, ln1_b, ln2_g, ln2_b, distributed=True)

    reduced = dict(zip(BIG, _join_halves([halves[k] for k in BIG], name="join_halves")))

    outs = {k: _adamw(w[k], reduced[k], m[k], v[k], name="adamw_" + k) for k in BIG}
    loss, small_outs = _adamw_small(small, w, m, v, name="adamw_small")
    outs.update(small_outs)
    grad_out, delta_out, m_out, v_out = ({k: outs[k][i] for k in WEIGHTS} for i in range(4))
    for out in (grad_out, delta_out, m_out, v_out):
        out["w_in"] = out["w_in"][0].T

    result = [loss.reshape(()), grad_x.reshape(x.shape)]
    for out in (grad_out, delta_out, m_out, v_out):
        result += [out[k].reshape(shapes[k]) for k in WEIGHTS]
    return tuple(result)
```

```python
import math
from typing import Callable, NamedTuple

import jax
import jax.numpy as jnp
from jax import lax
from jax.experimental import pallas as pl
from jax.experimental.pallas import tpu as pltpu

F32 = jnp.float32
BF16 = jnp.bfloat16
HIGHEST = lax.Precision.HIGHEST
MESH = pl.DeviceIdType.MESH

D_MODEL = 1024
MEM_LEN = 256
HG_HEADS = 8
HG_DK = 128
HG_CHUNK = 64
SWA_HEADS = 16
SWA_KV_HEADS = 2
SWA_GROUP = 8
SWA_HEAD_DIM = 64
SWA_BLOCK = 128
SWA_WINDOW = 128
MEM_HEADS = 4
MEM_HEAD_DIM = 256
NUM_BUCKETS = 32
MAX_DISTANCE = 128
D_FF = 4096
LN_EPS = 1e-5
RMS_EPS = 1e-6
ALPHA = 2.0 ** 0.25
W_A, W_B, W_C, W_D = 4096, 1280, 1024, 3072
IN_COLS = W_A + W_B + W_C + W_D
N_SHARDS = 4
ADAM_LR = 0.001
ADAM_B1 = 0.9
ADAM_B2 = 0.999
ADAM_EPS = 1e-08
ADAM_WD = 0.01
ADAM_STEP = 10
MASK_VALUE = -1e30
VMEM_LIMIT = 56 * 1024 * 1024

NN = ((1,), (0,))
NT = ((1,), (1,))
TN = ((0,), (0,))
HBM = pl.BlockSpec(memory_space=pltpu.HBM)


def _dot(a, b, dims=NN, precision=None):
    return lax.dot_general(a, b, (dims, ((), ())), precision=precision, preferred_element_type=F32)


def _params(sem=None):
    return pltpu.CompilerParams(dimension_semantics=sem, vmem_limit_bytes=VMEM_LIMIT)


def _resident(shape):
    zeros = (0,) * len(shape)
    return pl.BlockSpec(shape, lambda *_: zeros, pipeline_mode=pl.Buffered(1))


def _resident_rows(arr, offset, rows):
    return pl.BlockSpec((pl.Element(rows), pl.Element(arr.shape[1])), lambda *_: (offset, 0), pipeline_mode=pl.Buffered(1))


def _mm(a, b, *, mode, tm, tn, tk, name, out_dtype=F32, b_panels=False, b_rows=None, out_panels=False, rows_of=None, row_offset=0,
        into=None):
    if mode == "tn":
        kdim, m = a.shape
    else:
        m, kdim = a.shape
    if b_panels:
        n = b.shape[0] * b.shape[2]
        assert b.shape[2] == tn and mode == "nn"
    elif b_rows is not None:
        assert mode == "nt"
        b_offset, n = b_rows
    elif mode == "nt":
        n = b.shape[0]
    else:
        n = b.shape[1]
    assert m % tm == 0 and n % tn == 0 and kdim % tk == 0, (name, m, n, kdim)
    nk = kdim // tk
    dims = {"nn": NN, "nt": NT, "tn": TN}[mode]
    a_spec = pl.BlockSpec((tk, tm), lambda i, j, k: (k, i)) if mode == "tn" else pl.BlockSpec((tm, tk), lambda i, j, k: (i, k))
    if b_panels:
        b_spec = pl.BlockSpec((None, tk, tn), lambda i, j, k: (j, k, 0))
    elif b_rows is not None:
        assert b_offset % BF16_SUBLANES == 0 and tn % BF16_SUBLANES == 0 and tk % 128 == 0
        b_spec = pl.BlockSpec((pl.Element(tn), pl.Element(tk)),
                              lambda i, j, k: (pl.multiple_of(b_offset + j * tn, BF16_SUBLANES), pl.multiple_of(k * tk, 128)))
    elif mode == "nt":
        b_spec = pl.BlockSpec((tn, tk), lambda i, j, k: (j, k))
    else:
        b_spec = pl.BlockSpec((tk, tn), lambda i, j, k: (k, j))
    in_specs = [a_spec, b_spec]
    operands = [a, b]
    aliases = {}
    if out_panels:
        out_shape = jax.ShapeDtypeStruct((n // tn, m, tn), out_dtype)
        o_spec = pl.BlockSpec((None, tm, tn), lambda i, j, k: (j, i, 0))
    elif rows_of is not None:
        out_shape = jax.ShapeDtypeStruct((rows_of, n), out_dtype)
        assert row_offset % BF16_SUBLANES == 0 and tm % BF16_SUBLANES == 0 and tn % 128 == 0
        o_spec = pl.BlockSpec((pl.Element(tm), pl.Element(tn)),
                              lambda i, j, k: (pl.multiple_of(row_offset + i * tm, BF16_SUBLANES), pl.multiple_of(j * tn, 128)))
        if into is not None:
            in_specs.append(pl.BlockSpec(memory_space=pl.ANY))
            operands.append(into)
            aliases = {2: 0}
    else:
        out_shape = jax.ShapeDtypeStruct((m, n), out_dtype)
        o_spec = pl.BlockSpec((tm, tn), lambda i, j, k: (i, j))
    n_in = len(operands)

    def body(*refs):
        a_ref, b_ref, o_ref = refs[0], refs[1], refs[n_in]
        part = _dot(a_ref[...].astype(BF16), b_ref[...].astype(BF16), dims)

        def finish(acc):
            o_ref[...] = acc.astype(out_dtype)

        if nk == 1:
            finish(part)
        else:
            acc_ref = refs[-1]
            k = pl.program_id(2)

            @pl.when(k == 0)
            def _():
                acc_ref[...] = part

            @pl.when(k > 0)
            def _():
                acc_ref[...] += part

            @pl.when(k == nk - 1)
            def _():
                finish(acc_ref[...])

    return pl.pallas_call(
        body, name=name, out_shape=out_shape, grid=(m // tm, n // tn, nk), in_specs=in_specs, out_specs=o_spec,
        scratch_shapes=[pltpu.VMEM((tm, tn), F32)] if nk > 1 else [], input_output_aliases=aliases,
        compiler_params=_params(("parallel", "parallel", "arbitrary")),
    )(*operands)


def _dx_matmul(dzs, wi_t, resid, *, tm, name, tiles, into=None, exchanges=()):
    s = resid.shape[0]
    npieces = len(dzs)
    offsets = [sum(dz.shape[1] for dz in dzs[:p]) for p in range(npieces)]
    first, count = tiles
    tile = lambda i: (first + i, 0)
    in_specs = [pl.BlockSpec((tm, dz.shape[1]), tile) for dz in dzs] + [_resident(wi_t.shape), pl.BlockSpec((tm, D_MODEL), tile)]
    operands = [*dzs, wi_t, resid]
    if into is not None:
        in_specs.append(pl.BlockSpec(memory_space=pl.ANY))
        operands.append(into)
    n_in = len(operands)

    def body(*refs):
        dz_refs, w_ref, r_ref, o_ref = refs[:npieces], refs[npieces], refs[npieces + 1], refs[n_in]
        total = ALPHA * r_ref[...]
        for p in range(npieces):
            total = total + _dot(dz_refs[p][...], w_ref[offsets[p]:offsets[p] + dzs[p].shape[1], :], NN)
        o_ref[...] = total

    return _fused_call(
        body, name=name, out_shape=jax.ShapeDtypeStruct((s, D_MODEL), F32), grid=(count,), in_specs=in_specs,
        out_specs=pl.BlockSpec((tm, D_MODEL), tile), scratch_shapes=[], operands=operands, exchanges=exchanges,
        aliases={n_in - 1: 0} if into is not None else None)


def _lower_bound(lbl_ref):
    l0, l1 = lbl_ref[0:1, :], lbl_ref[1:2, :]
    mx = jnp.maximum(l0, l1)
    e0, e1 = jnp.exp(l0 - mx), jnp.exp(l1 - mx)
    return e0 / (e0 + e1)


HEAD_COLS = [slice(h * HG_DK, (h + 1) * HG_DK) for h in range(HG_HEADS)]


def _head_mean(x):
    return jnp.concatenate([jnp.broadcast_to(jnp.mean(x[:, c], axis=-1, keepdims=True), (x.shape[0], HG_DK)) for c in HEAD_COLS], axis=1)


def _triangle_sum(tri_b, x):
    p0 = x.astype(BF16)
    r1 = x - p0.astype(F32)
    p1 = r1.astype(BF16)
    p2 = (r1 - p1.astype(F32)).astype(BF16)
    return _dot(tri_b, p0) + _dot(tri_b, p1) + _dot(tri_b, p2)


def _chunk_forward(q, fl, v, lb, tril_b):
    sg = jax.nn.sigmoid(fl)
    f = lb + (1.0 - lb) * sg
    k = 1.0 - f
    b = _triangle_sum(tril_b, jnp.log(f))
    b_last = b[HG_CHUNK - 1:HG_CHUNK, :]
    eb, enb, eo = jnp.exp(b), jnp.exp(-b), jnp.exp(b_last - b)
    return sg, f, k, b_last, eb, enb, eo, q * eb, k * enb, k * eo


HG_TILE = 256


def _hgrn_fwd(xb, wi_t, lb_logits, gain, *, name, exchanges=()):
    s = xb.shape[0]
    t = min(2 * HG_TILE, s)
    tiles = t // HG_TILE
    ncs = HG_TILE // HG_CHUNK
    nt = s // t
    groups = W_A // D_MODEL

    def body(x_ref, xnext_ref, w_ref, lbl_ref, gain_ref, z_ref, oa_ref, oraw_ref, st_ref, state, z_first):
        @pl.when(pl.program_id(0) == 0)
        def _():
            state[...] = jnp.zeros_like(state)
            z_first[...] = _dot(x_ref[0:HG_TILE, :], w_ref[...], NT)

        z_ref[0:HG_TILE, :] = z_first[...]
        lb_all = _lower_bound(lbl_ref)
        row = lax.broadcasted_iota(jnp.int32, (HG_CHUNK, HG_CHUNK), 0)
        col = lax.broadcasted_iota(jnp.int32, (HG_CHUNK, HG_CHUNK), 1)
        tril = row >= col
        tril_b = tril.astype(BF16)
        gain_all = gain_ref[...]
        for tile in range(tiles):
            if tile + 1 < tiles:
                x_ahead, ahead = x_ref[(tile + 1) * HG_TILE:(tile + 2) * HG_TILE, :], z_ref.at[(tile + 1) * HG_TILE:(tile + 2) * HG_TILE, :]
            else:
                x_ahead, ahead = xnext_ref[...], z_first
            for i in range(ncs):
                if i < groups:
                    ahead[:, i * D_MODEL:(i + 1) * D_MODEL] = _dot(x_ahead, w_ref[i * D_MODEL:(i + 1) * D_MODEL, :], NT)
                r = slice(tile * HG_TILE + i * HG_CHUNK, tile * HG_TILE + (i + 1) * HG_CHUNK)
                q, fl, v, hg = (z_ref[r, j * D_MODEL:(j + 1) * D_MODEL] for j in range(groups))
                _, _, _, b_last, _, _, _, q_in, k_in, k_out = _chunk_forward(q, fl, v, lb_all, tril_b)
                q_in_b, k_in_b, k_out_b, vb = (u.astype(BF16) for u in (q_in, k_in, k_out, v))
                decay = jnp.exp(b_last)
                sts = [state[h] for h in range(HG_HEADS)]
                attn = [_dot(q_in_b[:, c], k_in_b[:, c], NT) for c in HEAD_COLS]
                inter = [_dot(q_in_b[:, c], sts[h].astype(BF16), NT) for h, c in enumerate(HEAD_COLS)]
                upd = [_dot(vb[:, c], k_out_b[:, c], TN) for c in HEAD_COLS]
                attn = [jnp.where(tril, a, 0.0).astype(BF16) for a in attn]
                outs = [_dot(attn[h], vb[:, c], NN) + inter[h] for h, c in enumerate(HEAD_COLS)]
                for h, c in enumerate(HEAD_COLS):
                    st_ref[h, tile * ncs + i] = sts[h]
                    state[h] = sts[h] * decay[:, c] + upd[h]
                o = jnp.concatenate(outs, axis=1)
                oraw_ref[r, :] = o
                n = o * lax.rsqrt(_head_mean(o * o) + RMS_EPS)
                oa_ref[r, :] = (n * gain_all * (hg * jax.nn.sigmoid(hg))).astype(BF16)
            for j in range(ncs, groups):
                ahead[:, j * D_MODEL:(j + 1) * D_MODEL] = _dot(x_ahead, w_ref[j * D_MODEL:(j + 1) * D_MODEL, :], NT)

    step = lambda i: (i, 0)
    last_tile = s // HG_TILE - 1
    return _fused_call(
        body, name=name, grid=(nt,),
        out_shape=(jax.ShapeDtypeStruct((s, W_A), F32), jax.ShapeDtypeStruct((s, D_MODEL), BF16), jax.ShapeDtypeStruct((s, D_MODEL), F32),
                   jax.ShapeDtypeStruct((HG_HEADS, s // HG_CHUNK, HG_DK, HG_DK), F32)),
        in_specs=[pl.BlockSpec((t, D_MODEL), step), pl.BlockSpec((HG_TILE, D_MODEL), lambda i: (jnp.minimum((i + 1) * tiles, last_tile), 0)),
                  _resident_rows(wi_t, 0, W_A), _resident((2, D_MODEL)), _resident((1, D_MODEL))],
        out_specs=(pl.BlockSpec((t, W_A), step), pl.BlockSpec((t, D_MODEL), step), pl.BlockSpec((t, D_MODEL), step),
                   pl.BlockSpec((HG_HEADS, tiles * ncs, HG_DK, HG_DK), lambda i: (0, i, 0, 0))),
        scratch_shapes=[pltpu.VMEM((HG_HEADS, HG_DK, HG_DK), F32), pltpu.VMEM((HG_TILE, W_A), F32)],
        operands=[xb, xb, wi_t, lb_logits, gain], exchanges=exchanges)


def _hgrn_bwd(za, oraw, do_a, states, lb_logits, gain, *, name, exchanges=()):
    s = za.shape[0]
    t = min(256, s)
    ncs = t // HG_CHUNK
    nt = s // t

    def body(z_ref, oraw_ref, do_ref, st_ref, lbl_ref, gain_ref, dz_ref, stats_ref, dstate):
        step = pl.program_id(0)

        @pl.when(step == 0)
        def _():
            dstate[...] = jnp.zeros_like(dstate)
            stats_ref[...] = jnp.zeros_like(stats_ref)

        lb_all = _lower_bound(lbl_ref)
        row = lax.broadcasted_iota(jnp.int32, (HG_CHUNK, HG_CHUNK), 0)
        col = lax.broadcasted_iota(jnp.int32, (HG_CHUNK, HG_CHUNK), 1)
        tril = row >= col
        tril_b = tril.astype(BF16)
        triu_b = (row <= col).astype(BF16)
        gain_all = gain_ref[...]

        def chunk(ii, carry):
            i = ncs - 1 - ii
            r = pl.ds(pl.multiple_of(i * HG_CHUNK, HG_CHUNK), HG_CHUNK)
            q, fl, v, hg = (z_ref[r, j * D_MODEL:(j + 1) * D_MODEL] for j in range(4))
            o = oraw_ref[r, :]
            doa = do_ref[r, :]
            rms = lax.rsqrt(_head_mean(o * o) + RMS_EPS)
            n = o * rms
            sgg = jax.nn.sigmoid(hg)
            silu = hg * sgg
            dhg = doa * n * gain_all * (sgg * (1.0 + hg * (1.0 - sgg)))
            dgain = jnp.sum(doa * n * silu, axis=0, keepdims=True)
            dn = doa * gain_all * silu
            do = rms * (dn - n * _head_mean(dn * n))
            sg, f, k, b_last, eb, enb, eo, q_in, k_in, k_out = _chunk_forward(q, fl, v, lb_all, tril_b)
            q_in_b, k_in_b, k_out_b, vb, dob = (u.astype(BF16) for u in (q_in, k_in, k_out, v, do))
            decay = jnp.exp(b_last)
            sts = [st_ref[h, i] for h in range(HG_HEADS)]
            dsts = [dstate[h] for h in range(HG_HEADS)]
            dsts_b = [d.astype(BF16) for d in dsts]
            heads = list(enumerate(HEAD_COLS))
            attn = [_dot(q_in_b[:, c], k_in_b[:, c], NT) for h, c in heads]
            dattn = [_dot(dob[:, c], vb[:, c], NT) for h, c in heads]
            dq_st = [_dot(dob[:, c], sts[h].astype(BF16), NN) for h, c in heads]
            dk_out = [_dot(vb[:, c], dsts_b[h], NN) for h, c in heads]
            dv_st = [_dot(k_out_b[:, c], dsts_b[h], NT) for h, c in heads]
            dst_o = [_dot(dob[:, c], q_in_b[:, c], TN) for h, c in heads]
            attn = [jnp.where(tril, a, 0.0).astype(BF16) for a in attn]
            dattn = [jnp.where(tril, a, 0.0).astype(BF16) for a in dattn]
            dq_in = jnp.concatenate([_dot(dattn[h], k_in_b[:, c], NN) + dq_st[h] for h, c in heads], axis=1)
            dk_in = jnp.concatenate([_dot(dattn[h], q_in_b[:, c], TN) for h, c in heads], axis=1)
            dv = jnp.concatenate([_dot(attn[h], dob[:, c], TN) + dv_st[h] for h, c in heads], axis=1)
            dk_out = jnp.concatenate(dk_out, axis=1)
            dst_st = jnp.concatenate([jnp.sum(dsts[h] * sts[h], axis=0, keepdims=True) for h in range(HG_HEADS)], axis=1)
            for h, c in heads:
                dstate[h] = dsts[h] * decay[:, c] + dst_o[h]
            db_last = decay * dst_st + jnp.sum(dk_out * k_out, axis=0, keepdims=True)
            db = dq_in * q_in - dk_in * k_in - dk_out * k_out
            dg = _triangle_sum(triu_b, db) + db_last
            dk = dk_in * enb + dk_out * eo
            df = dg / f - dk
            stats_ref[0:1, :] += dgain
            stats_ref[1:2, :] += jnp.sum(df * (1.0 - sg), axis=0, keepdims=True)
            dz_ref[r, 0:1024] = (dq_in * eb).astype(BF16)
            dz_ref[r, 1024:2048] = (df * (1.0 - lb_all) * sg * (1.0 - sg)).astype(BF16)
            dz_ref[r, 2048:3072] = dv.astype(BF16)
            dz_ref[r, 3072:4096] = dhg.astype(BF16)
            return carry

        lax.fori_loop(0, ncs, chunk, 0, unroll=True)

        @pl.when(step == nt - 1)
        def _():
            dl0 = stats_ref[1:2, :] * lb_all * (1.0 - lb_all)
            stats_ref[1:2, :] = dl0
            stats_ref[2:3, :] = -dl0

    rev = lambda i: (nt - 1 - i, 0)
    return _fused_call(
        body, name=name, grid=(nt,),
        out_shape=(jax.ShapeDtypeStruct((s, W_A), BF16), jax.ShapeDtypeStruct((8, D_MODEL), F32)),
        in_specs=[pl.BlockSpec((t, W_A), rev), pl.BlockSpec((t, D_MODEL), rev), pl.BlockSpec((t, D_MODEL), rev),
                  pl.BlockSpec((HG_HEADS, ncs, HG_DK, HG_DK), lambda i: (0, nt - 1 - i, 0, 0)),
                  _resident((2, D_MODEL)), _resident((1, D_MODEL))],
        out_specs=(pl.BlockSpec((t, W_A), rev), pl.BlockSpec((8, D_MODEL), lambda i: (0, 0))),
        scratch_shapes=[pltpu.VMEM((HG_HEADS, HG_DK, HG_DK), F32)],
        operands=[za, oraw, do_a, states, lb_logits, gain], exchanges=exchanges)


def _t5_bucket(n):
    max_exact = NUM_BUCKETS // 2
    nf = jnp.maximum(n, 1).astype(F32)
    large = max_exact + (jnp.log(nf / max_exact) / math.log(MAX_DISTANCE / max_exact) * (NUM_BUCKETS - max_exact)).astype(jnp.int32)
    large = jnp.minimum(large, NUM_BUCKETS - 1)
    return jnp.where(n < max_exact, n, large)


def _bias_selector():
    qi = jnp.arange(SWA_BLOCK)[:, None] + SWA_BLOCK
    kj = jnp.arange(2 * SWA_BLOCK)[None, :]
    dist = qi - kj
    band = ((dist >= 0) & (dist < SWA_WINDOW)).reshape(1, -1)
    bucket = _t5_bucket(jnp.clip(dist, 0, SWA_WINDOW - 1)).reshape(1, -1)
    onehot = ((bucket == jnp.arange(NUM_BUCKETS)[:, None]) & band).astype(F32)
    return onehot, jnp.where(band, 0.0, MASK_VALUE).astype(F32)


def _bias_table(rel_bias_t, onehot, maskrow, *, name):
    def body(rb_ref, oh_ref, mask_ref, o_ref):
        o_ref[...] = _dot(rb_ref[...], oh_ref[...], NN, HIGHEST) + mask_ref[...]

    return pl.pallas_call(body, name=name, out_shape=jax.ShapeDtypeStruct((SWA_HEADS, onehot.shape[1]), F32),
                          compiler_params=_params())(rel_bias_t, onehot, maskrow)


def _bias_grad(dbias2d, onehot, *, name):
    def body(db_ref, oh_ref, o_ref):
        o_ref[...] = _dot(db_ref[...], oh_ref[...], NT, HIGHEST)

    return pl.pallas_call(body, name=name, out_shape=jax.ShapeDtypeStruct((SWA_HEADS, NUM_BUCKETS), F32),
                          compiler_params=_params())(dbias2d, onehot)


GROUP_LANES = SWA_GROUP * SWA_BLOCK


def _swa_operands(zq_ref, kv_cur_ref, kv_prev_ref):
    q = (zq_ref[:, 0:1024] * (SWA_HEAD_DIM ** -0.5)).astype(BF16)
    kv_c = kv_cur_ref[...].astype(BF16)
    kv_p = kv_prev_ref[...].astype(BF16)
    kks = [jnp.concatenate([kv_p[:, g * 64:(g + 1) * 64], kv_c[:, g * 64:(g + 1) * 64]], axis=0) for g in range(SWA_KV_HEADS)]
    vvs = [jnp.concatenate([kv_p[:, 128 + g * 64:128 + (g + 1) * 64], kv_c[:, 128 + g * 64:128 + (g + 1) * 64]], axis=0)
           for g in range(SWA_KV_HEADS)]
    return q, kks, vvs


SWA_PART_HEADS = 8
SWA_PARTS = [(h0 // SWA_GROUP, h0) for h0 in range(0, SWA_HEADS, SWA_PART_HEADS)]


def _part_lanes(h0):
    return slice(h0 * SWA_BLOCK, (h0 + SWA_PART_HEADS) * SWA_BLOCK)


def _stack_heads(x, h0):
    return jnp.concatenate([x[:, h * SWA_HEAD_DIM:(h + 1) * SWA_HEAD_DIM] for h in range(h0, h0 + SWA_PART_HEADS)], axis=0)


def _heads_to_lanes(xt):
    pairs = []
    for j in range(0, xt.shape[1] // SWA_BLOCK, 2):
        two = jnp.concatenate([xt[:, j * SWA_BLOCK:(j + 1) * SWA_BLOCK], xt[:, (j + 1) * SWA_BLOCK:(j + 2) * SWA_BLOCK]], axis=0)
        pairs.append(two.T)
    return jnp.concatenate(pairs, axis=1)


def _swa_softmax(score_t, bias_ref, sink_ref, h0):
    sc = score_t + bias_ref[:, _part_lanes(h0)]
    sink = sink_ref[:, _part_lanes(h0)]
    m = jnp.maximum(jnp.max(sc, axis=0, keepdims=True), sink)
    e = jnp.exp(sc - m)
    e_sink = jnp.exp(sink - m)
    return e, 1.0 / (jnp.sum(e, axis=0, keepdims=True) + e_sink), e_sink


def _swa_tables(bias2d, sinks):
    bias_t = bias2d.reshape(SWA_HEADS, SWA_BLOCK, 2 * SWA_BLOCK).transpose(2, 0, 1).reshape(2 * SWA_BLOCK, SWA_HEADS * SWA_BLOCK)
    first = jnp.where(jnp.arange(2 * SWA_BLOCK)[:, None] < SWA_BLOCK, MASK_VALUE, bias_t)
    return jnp.stack([first, bias_t]), jnp.repeat(sinks, SWA_BLOCK, axis=1)


def _swa_fwd(zb, bias_tables, sink_lanes, *, name, exchanges=()):
    s = zb.shape[0]
    nb = s // SWA_BLOCK

    def body(zq_ref, kvc_ref, kvp_ref, bias_ref, sink_ref, o_ref):
        q, kks, vvs = _swa_operands(zq_ref, kvc_ref, kvp_ref)
        scores = [_dot(kks[g], _stack_heads(q, h0), NT) for g, h0 in SWA_PARTS]
        probs = []
        for score, (_, h0) in zip(scores, SWA_PARTS):
            e, inv, _ = _swa_softmax(score, bias_ref, sink_ref, h0)
            probs.append((e * inv).astype(BF16))
        outs = [_dot(vvs[g], p, TN) for p, (g, _) in zip(probs, SWA_PARTS)]
        o_ref[...] = jnp.concatenate([_heads_to_lanes(o) for o in outs], axis=1).astype(BF16)

    return _fused_call(
        body, name=name, grid=(nb,), out_shape=jax.ShapeDtypeStruct((s, D_MODEL), BF16),
        in_specs=[pl.BlockSpec((SWA_BLOCK, W_B), lambda n: (n, 0)),
                  pl.BlockSpec((SWA_BLOCK, 256), lambda n: (n, 4)),
                  pl.BlockSpec((SWA_BLOCK, 256), lambda n: (jnp.maximum(n - 1, 0), 4)),
                  pl.BlockSpec((None, 2 * SWA_BLOCK, SWA_HEADS * SWA_BLOCK), lambda n: (jnp.minimum(n, 1), 0, 0)),
                  _resident((1, SWA_HEADS * SWA_BLOCK))],
        out_specs=pl.BlockSpec((SWA_BLOCK, D_MODEL), lambda n: (n, 0)), scratch_shapes=[],
        operands=[zb, zb, zb, bias_tables, sink_lanes], exchanges=exchanges)


def _swa_bwd(zb, do_b, bias_tables, sink_lanes, *, name, exchanges=()):
    s = zb.shape[0]
    nb = s // SWA_BLOCK
    scale = SWA_HEAD_DIM ** -0.5

    def body(zq_ref, kvc_ref, kvp_ref, do_ref, bias_ref, sink_ref, dz_ref, dbias_ref, dsink_ref, carry, dsink_acc):
        step = pl.program_id(0)

        @pl.when(step == 0)
        def _():
            carry[...] = jnp.zeros_like(carry)
            dsink_acc[...] = jnp.zeros_like(dsink_acc)
            dbias_ref[...] = jnp.zeros_like(dbias_ref)

        q, kks, vvs = _swa_operands(zq_ref, kvc_ref, kvp_ref)
        do = do_ref[...].astype(BF16)
        parts = range(len(SWA_PARTS))
        q_rows = [_stack_heads(q, h0) for _, h0 in SWA_PARTS]
        do_rows = [_stack_heads(do, h0) for _, h0 in SWA_PARTS]
        scores = [_dot(kks[g], q_rows[i], NT) for i, (g, _) in enumerate(SWA_PARTS)]
        dps = [_dot(vvs[g], do_rows[i], NT) for i, (g, _) in enumerate(SWA_PARTS)]
        ps, dss = [], []
        for i, (_, h0) in enumerate(SWA_PARTS):
            e, inv, e_sink = _swa_softmax(scores[i], bias_ref, sink_ref, h0)
            p = e * inv
            delta = jnp.sum(p * dps[i], axis=0, keepdims=True)
            ds = p * (dps[i] - delta)
            dbias_ref[:, _part_lanes(h0)] += ds
            dsink_acc[:, _part_lanes(h0)] -= e_sink * inv * delta
            ps.append(p.astype(BF16))
            dss.append(ds.astype(BF16))
        dqs = [_dot(kks[g], dss[i], TN) * scale for i, (g, _) in enumerate(SWA_PARTS)]
        in_group = lambda xs, g, axis: jnp.concatenate([xs[i] for i in parts if SWA_PARTS[i][0] == g], axis=axis)
        dkks = [_dot(in_group(dss, g, 1), in_group(q_rows, g, 0), NN) for g in range(SWA_KV_HEADS)]
        dvvs = [_dot(in_group(ps, g, 1), in_group(do_rows, g, 0), NN) for g in range(SWA_KV_HEADS)]
        dkv = jnp.concatenate(dkks + dvvs, axis=1)
        dz_ref[:, 0:1024] = jnp.concatenate([_heads_to_lanes(dq) for dq in dqs], axis=1).astype(BF16)
        dz_ref[:, 1024:1280] = (dkv[SWA_BLOCK:, :] + carry[...]).astype(BF16)
        carry[...] = dkv[:SWA_BLOCK, :]

        @pl.when(step == nb - 1)
        def _():
            acc = dsink_acc[...]
            dsink_ref[...] = jnp.concatenate([jnp.sum(acc[:, h * SWA_BLOCK:(h + 1) * SWA_BLOCK], axis=1, keepdims=True)
                                              for h in range(SWA_HEADS)], axis=1)

    rev = lambda i: (nb - 1 - i, 0)
    table_shape = (2 * SWA_BLOCK, SWA_HEADS * SWA_BLOCK)
    return _fused_call(
        body, name=name, grid=(nb,),
        out_shape=(jax.ShapeDtypeStruct((s, W_B), BF16), jax.ShapeDtypeStruct(table_shape, F32), jax.ShapeDtypeStruct((1, SWA_HEADS), F32)),
        in_specs=[pl.BlockSpec((SWA_BLOCK, W_B), rev),
                  pl.BlockSpec((SWA_BLOCK, 256), lambda i: (nb - 1 - i, 4)),
                  pl.BlockSpec((SWA_BLOCK, 256), lambda i: (jnp.maximum(nb - 2 - i, 0), 4)),
                  pl.BlockSpec((SWA_BLOCK, D_MODEL), rev),
                  pl.BlockSpec((None,) + table_shape, lambda i: (jnp.minimum(nb - 1 - i, 1), 0, 0)),
                  _resident((1, SWA_HEADS * SWA_BLOCK))],
        out_specs=(pl.BlockSpec((SWA_BLOCK, W_B), rev), pl.BlockSpec(table_shape, lambda i: (0, 0)),
                   pl.BlockSpec((1, SWA_HEADS), lambda i: (0, 0))),
        scratch_shapes=[pltpu.VMEM((SWA_BLOCK, 256), F32), pltpu.VMEM((1, SWA_HEADS * SWA_BLOCK), F32)],
        operands=[zb, zb, zb, do_b, bias_tables, sink_lanes], exchanges=exchanges)


MEM_COLS = [slice(h * MEM_HEAD_DIM, (h + 1) * MEM_HEAD_DIM) for h in range(MEM_HEADS)]
MEM_VCOLS = [slice(D_MODEL + h * MEM_HEAD_DIM, D_MODEL + (h + 1) * MEM_HEAD_DIM) for h in range(MEM_HEADS)]


def _mem_probs(zc_ref, mkv_ref):
    qs = [(zc_ref[:, c] * (MEM_HEAD_DIM ** -0.5)).astype(BF16) for c in MEM_COLS]
    scores = [_dot(qs[h], mkv_ref[:, c], NT) for h, c in enumerate(MEM_COLS)]
    ps = []
    for sc in scores:
        e = jnp.exp(sc - jnp.max(sc, axis=-1, keepdims=True))
        ps.append(e / jnp.sum(e, axis=-1, keepdims=True))
    return qs, ps


def _mem_fwd(xb, wi_t, mkv, *, name):
    s = xb.shape[0]
    t = min(512, s)

    def body(x_ref, w_ref, mkv_ref, zc_ref, o_ref):
        zc_ref[...] = _dot(x_ref[...], w_ref[...], NT).astype(BF16)
        _, ps = _mem_probs(zc_ref, mkv_ref)
        ps = [p.astype(BF16) for p in ps]
        o_ref[...] = jnp.concatenate([_dot(ps[h], mkv_ref[:, vc], NN) for h, vc in enumerate(MEM_VCOLS)], axis=1).astype(BF16)

    row = pl.BlockSpec((t, D_MODEL), lambda i: (i, 0))
    return pl.pallas_call(
        body, name=name, grid=(s // t,), out_shape=(jax.ShapeDtypeStruct((s, D_MODEL), BF16),) * 2,
        in_specs=[row, _resident_rows(wi_t, W_A + W_B, W_C), _resident((MEM_LEN, 2 * D_MODEL))],
        out_specs=(row, row), compiler_params=_params(("parallel",)),
    )(xb, wi_t, mkv)


def _mem_bwd(xb, zc, do_c, mkv, *, name):
    s = zc.shape[0]
    t = min(512, s)
    nt = s // t

    def body(x_ref, zc_ref, do_ref, mkv_ref, dz_ref, dmkv_ref, gwi_ref, acc):
        @pl.when(pl.program_id(0) == 0)
        def _():
            dmkv_ref[...] = jnp.zeros_like(dmkv_ref)
            acc[...] = jnp.zeros_like(acc)

        heads = range(MEM_HEADS)
        qs, ps = _mem_probs(zc_ref, mkv_ref)
        dos = [do_ref[:, c].astype(BF16) for c in MEM_COLS]
        dps = [_dot(dos[h], mkv_ref[:, MEM_VCOLS[h]], NT) for h in heads]
        dss = [(ps[h] * (dps[h] - jnp.sum(ps[h] * dps[h], axis=-1, keepdims=True))).astype(BF16) for h in heads]
        ps = [p.astype(BF16) for p in ps]
        dz = jnp.concatenate([_dot(dss[h], mkv_ref[:, MEM_COLS[h]], NN) * (MEM_HEAD_DIM ** -0.5) for h in heads], axis=1).astype(BF16)
        dz_ref[...] = dz
        dmkv_ref[...] += jnp.concatenate([_dot(dss[h], qs[h], TN) for h in heads] + [_dot(ps[h], dos[h], TN) for h in heads], axis=1)
        acc[...] += _dot(dz, x_ref[...], TN)

        @pl.when(pl.program_id(0) == nt - 1)
        def _():
            pltpu.sync_copy(acc, gwi_ref.at[pl.ds(W_A + W_B, W_C), :])

    row = pl.BlockSpec((t, D_MODEL), lambda i: (i, 0))
    return pl.pallas_call(
        body, name=name, grid=(nt,),
        out_shape=(jax.ShapeDtypeStruct((s, D_MODEL), BF16), jax.ShapeDtypeStruct((MEM_LEN, 2 * D_MODEL), F32),
                   jax.ShapeDtypeStruct((IN_COLS, D_MODEL), F32)),
        in_specs=[row, row, row, _resident((MEM_LEN, 2 * D_MODEL))],
        out_specs=(row, pl.BlockSpec((MEM_LEN, 2 * D_MODEL), lambda i: (0, 0)), HBM),
        scratch_shapes=[pltpu.VMEM((W_C, D_MODEL), F32)],
        compiler_params=_params(("arbitrary",)),
    )(xb, zc, do_c, mkv)


def _normalize(pre):
    mu = jnp.mean(pre, axis=-1, keepdims=True)
    xc = pre - mu
    rstd = lax.rsqrt(jnp.mean(xc * xc, axis=-1, keepdims=True) + LN_EPS)
    return xc * rstd, rstd


def _layer_norm_bwd(dh, xhat, rstd, g):
    dxh = dh * g
    dpre = rstd * (dxh - jnp.mean(dxh, axis=-1, keepdims=True) - xhat * jnp.mean(dxh * xhat, axis=-1, keepdims=True))
    return dpre, jnp.sum(dh * xhat, axis=0, keepdims=True), jnp.sum(dh, axis=0, keepdims=True)


def _merge_fwd(o_a, o_b, o_c, x, wi_t, wbr, wo, *, name):
    s = x.shape[0]
    t = min(256, s)
    row = lambda w: pl.BlockSpec((t, w), lambda i: (i, 0))

    def body(oa_ref, ob_ref, oc_ref, x_ref, wg_ref, wbr_ref, wo_ref, zd_ref, xhat_ref, rstd_ref, merged_ref, pa_ref, pb_ref, pc_ref):
        zd_ref[...] = _dot(x_ref[...].astype(BF16), wg_ref[...], NT)
        merged = jnp.zeros((t, D_MODEL), F32)
        for b, (o_ref, p_ref) in enumerate(((oa_ref, pa_ref), (ob_ref, pb_ref), (oc_ref, pc_ref))):
            p = _dot(o_ref[...], wbr_ref[b], NN)
            p_ref[...] = p.astype(BF16)
            merged = merged + jax.nn.sigmoid(zd_ref[:, b * D_MODEL:(b + 1) * D_MODEL]) * p
        merged_b = merged.astype(BF16)
        merged_ref[...] = merged_b
        xhat, rstd = _normalize(ALPHA * x_ref[...] + _dot(merged_b, wo_ref[...], NN))
        xhat_ref[...] = xhat
        rstd_ref[...] = rstd

    act = jax.ShapeDtypeStruct((s, D_MODEL), F32)
    return pl.pallas_call(
        body, name=name, grid=(s // t,),
        out_shape=(jax.ShapeDtypeStruct((s, W_D), F32), act, jax.ShapeDtypeStruct((s, 1), F32)) + (jax.ShapeDtypeStruct((s, D_MODEL), BF16),) * 4,
        in_specs=[row(D_MODEL), row(D_MODEL), row(D_MODEL), row(D_MODEL), _resident_rows(wi_t, W_A + W_B + W_C, W_D),
                  _resident((3, D_MODEL, D_MODEL)), _resident((D_MODEL, D_MODEL))],
        out_specs=(row(W_D), row(D_MODEL), row(1), row(D_MODEL), row(D_MODEL), row(D_MODEL), row(D_MODEL)),
        compiler_params=_params(("parallel",)),
    )(o_a, o_b, o_c, x, wi_t, wbr, wo)


def _merge_bwd(dpre1, zd, pa, pb, pc, o_a, o_b, o_c, merged, wbr, wo, *, name, exchanges=()):
    s = dpre1.shape[0]
    t = min(256, s)
    nt = s // t
    row = lambda w: pl.BlockSpec((t, w), lambda i: (i, 0))

    def body(dpre_ref, zd_ref, pa_ref, pb_ref, pc_ref, oa_ref, ob_ref, oc_ref, mg_ref, wbr_ref, wo_ref,
             dzd_ref, doa_ref, dob_ref, doc_ref, gwa_ref, gwb_ref, gwc_ref, gwo_ref, acc):
        step = pl.program_id(0)

        @pl.when(step == 0)
        def _():
            acc[...] = jnp.zeros_like(acc)

        dpre_b = dpre_ref[...].astype(BF16)
        dmerged = _dot(dpre_b, wo_ref[...], NT)
        acc[3] += _dot(mg_ref[...], dpre_b, TN)
        branches = ((pa_ref, oa_ref, doa_ref), (pb_ref, ob_ref, dob_ref), (pc_ref, oc_ref, doc_ref))
        for b, (p_ref, o_ref, do_ref) in enumerate(branches):
            gate = jax.nn.sigmoid(zd_ref[:, b * D_MODEL:(b + 1) * D_MODEL])
            dzd_ref[:, b * D_MODEL:(b + 1) * D_MODEL] = (dmerged * p_ref[...] * gate * (1.0 - gate)).astype(BF16)
            dp = (dmerged * gate).astype(BF16)
            acc[b] += _dot(o_ref[...], dp, TN)
            do_ref[...] = _dot(dp, wbr_ref[b], NT).astype(do_ref.dtype)

        @pl.when(step == nt - 1)
        def _():
            for b, gw_ref in enumerate((gwa_ref, gwb_ref, gwc_ref, gwo_ref)):
                pltpu.sync_copy(acc.at[b], gw_ref)

    act = jax.ShapeDtypeStruct((s, D_MODEL), F32)
    actb = jax.ShapeDtypeStruct((s, D_MODEL), BF16)
    gw = jax.ShapeDtypeStruct((D_MODEL, D_MODEL), F32)
    return _fused_call(
        body, name=name, grid=(nt,),
        out_shape=(jax.ShapeDtypeStruct((s, W_D), BF16), act, actb, actb, gw, gw, gw, gw),
        in_specs=[row(D_MODEL), row(W_D)] + [row(D_MODEL)] * 7 + [_resident((3, D_MODEL, D_MODEL)), _resident((D_MODEL, D_MODEL))],
        out_specs=(row(W_D),) + (row(D_MODEL),) * 3 + (HBM,) * 4, scratch_shapes=[pltpu.VMEM((4, D_MODEL, D_MODEL), F32)],
        operands=[dpre1, zd, pa, pb, pc, o_a, o_b, o_c, merged, wbr, wo], exchanges=exchanges)


def _mlp_loss(xhat1, rstd1, target, ln1_g, ln1_b, ln2_g, ln2_b, wu, wd, *, name):
    s = xhat1.shape[0]
    t = min(256, s)
    npan = wu.shape[0]
    row = lambda w: pl.BlockSpec((t, w), lambda i: (i, 0))
    vec = _resident((1, D_MODEL))

    def body(xhat_ref, rstd_ref, tgt_ref, g1_ref, b1_ref, g2_ref, b2_ref, wu_ref, wd_ref,
             dpre1_ref, dpre2_ref, h1_ref, a_ref, du_ref, stats_ref):
        @pl.when(pl.program_id(0) == 0)
        def _():
            stats_ref[...] = jnp.zeros_like(stats_ref)

        xhat1_v = xhat_ref[...]
        h1 = xhat1_v * g1_ref[...] + b1_ref[...]
        h1_b = h1.astype(BF16)
        h1_ref[...] = h1_b
        us = []
        ff = jnp.zeros((t, D_MODEL), F32)
        for j in range(npan):
            u = _dot(h1_b, wu_ref[j], NN)
            us.append(u)
            r = jnp.maximum(u, 0.0)
            a_b = (r * r).astype(BF16)
            a_ref[:, j * D_MODEL:(j + 1) * D_MODEL] = a_b
            ff = ff + _dot(a_b, wd_ref[j], NN)
        xhat2, rstd2 = _normalize(ALPHA * h1 + ff)
        err = xhat2 * g2_ref[...] + b2_ref[...] - tgt_ref[...]
        stats_ref[4:5, :] += jnp.sum(err * err, axis=0, keepdims=True)
        dpre2, dg2, db2 = _layer_norm_bwd(err * (1.0 / D_MODEL), xhat2, rstd2, g2_ref[...])
        stats_ref[0:1, :] += dg2
        stats_ref[1:2, :] += db2
        dpre2_b = dpre2.astype(BF16)
        dpre2_ref[...] = dpre2_b
        dh1 = ALPHA * dpre2
        for j in range(npan):
            du_b = (_dot(dpre2_b, wd_ref[j], NT) * (2.0 * jnp.maximum(us[j], 0.0))).astype(BF16)
            du_ref[:, j * D_MODEL:(j + 1) * D_MODEL] = du_b
            dh1 = dh1 + _dot(du_b, wu_ref[j], NT)
        dpre1, dg1, db1 = _layer_norm_bwd(dh1, xhat1_v, rstd_ref[...], g1_ref[...])
        stats_ref[2:3, :] += dg1
        stats_ref[3:4, :] += db1
        dpre1_ref[...] = dpre1

    actb = jax.ShapeDtypeStruct((s, D_MODEL), BF16)
    wide = jax.ShapeDtypeStruct((s, D_FF), BF16)
    return pl.pallas_call(
        body, name=name, grid=(s // t,),
        out_shape=(jax.ShapeDtypeStruct((s, D_MODEL), F32), actb, actb, wide, wide, jax.ShapeDtypeStruct((8, D_MODEL), F32)),
        in_specs=[row(D_MODEL), row(1), row(D_MODEL), vec, vec, vec, vec,
                  _resident((npan, D_MODEL, D_MODEL)), _resident((npan, D_MODEL, D_MODEL))],
        out_specs=(row(D_MODEL), row(D_MODEL), row(D_MODEL), row(D_FF), row(D_FF), pl.BlockSpec((8, D_MODEL), lambda i: (0, 0))),
        compiler_params=_params(("arbitrary",)),
    )(xhat1, rstd1, target, ln1_g, ln1_b, ln2_g, ln2_b, wu, wd)


BRANCH_WEIGHTS = ("w_branch_hg", "w_branch_swa", "w_branch_mem")


def _local_step(x, mem, target, wi_t, wmkv, late, lb_logits, gain, sinks, rel_bias, ln1_g, ln1_b, ln2_g, ln2_b, *, distributed):
    s = x.shape[0]
    tm = min(1024, s)
    tk = min(2048, s)
    xb = x.astype(BF16)
    memb = mem.astype(BF16)
    if distributed:
        cx, cy, cc = lax.axis_index("x"), lax.axis_index("y"), lax.axis_index("c")
        pos = jnp.stack([2 * cx + cy, cc]).astype(jnp.int32)
    gather = (lambda names: [_gather_exchange([late[k] for k in names])]) if distributed else (lambda names: [])
    to_sibling = (lambda grads: [_sibling_halves_exchange(grads)]) if distributed else (lambda grads: [])
    to_chips = (lambda sums: [_chip_partials_exchange([bf for bf, _ in sums])]) if distributed else (lambda sums: [])

    def chip_sums(names, grads, from_sibling):
        return [_add_sibling(g, o, pos, name="add_sibling_" + k) for k, g, o in zip(names, grads, from_sibling)]

    def shard_sums(names, sums, from_chips):
        return {k: _add_chips(mine, o, pos, name="add_chips_" + k) for k, (_, mine), o in zip(names, sums, from_chips)}

    zb = _mm(xb, wi_t, mode="nt", tm=tm, tn=W_B, tk=D_MODEL, name="proj_b", out_dtype=BF16, b_rows=(W_A, W_B))
    mkv = _mm(memb, wmkv, mode="nn", tm=MEM_LEN, tn=512, tk=D_MODEL, name="mem_kv", out_dtype=BF16, b_panels=True)
    onehot, maskrow = _bias_selector()
    bias_tables, sink_lanes = _swa_tables(_bias_table(rel_bias.T, onehot, maskrow, name="bias_table"), sinks)
    (za, o_a, o_raw, states), landed = _hgrn_fwd(xb, wi_t, lb_logits, gain, name="hgrn_fwd", exchanges=gather(("w_up", "w_down")))
    wu, wd = landed[0] if distributed else (late["wu"], late["wd"])
    o_b, landed = _swa_fwd(zb, bias_tables, sink_lanes, name="swa_fwd", exchanges=gather(BRANCH_WEIGHTS + ("w_out",)))
    if distributed:
        wbr = jnp.stack([wb.reshape(D_MODEL, D_MODEL) for wb in landed[0][:3]])
        wo = landed[0][3].reshape(D_MODEL, D_MODEL)
    else:
        wbr, wo = late["wbr"], late["wo"]
    zc, o_c = _mem_fwd(xb, wi_t, mkv, name="mem_fwd")
    zd, xhat1, rstd1, merged, pa, pb, pc = _merge_fwd(o_a, o_b, o_c, x, wi_t, wbr, wo, name="merge_fwd")

    dpre1, dpre2, h1, act, du, ln_stats = _mlp_loss(xhat1, rstd1, target, ln1_g, ln1_b, ln2_g, ln2_b, wu, wd, name="mlp_loss")
    ffn = ("w_down", "w_up")
    g_ffn = [_mm(act, dpre2, mode="tn", tm=1024, tn=D_MODEL, tk=tk, name="grad_w_down").reshape(N_SHARDS, D_FF // N_SHARDS, D_MODEL),
             _mm(h1, du, mode="tn", tm=D_MODEL, tn=1024, tk=tk, name="grad_w_up", out_panels=True)]

    (dzd, do_a, do_b, do_c, *g_merge), landed = _merge_bwd(dpre1, zd, pa, pb, pc, o_a, o_b, o_c, merged, wbr, wo, name="merge_bwd",
                                                           exchanges=to_sibling(g_ffn))
    sums_ffn = chip_sums(ffn, g_ffn, landed[0]) if distributed else []
    merge = BRANCH_WEIGHTS + ("w_out",)
    g_merge = [g.reshape(N_SHARDS, D_MODEL // N_SHARDS, D_MODEL) for g in g_merge]
    (dza, hg_stats), landed = _hgrn_bwd(za, o_raw, do_a, states, lb_logits, gain, name="hgrn_bwd",
                                        exchanges=to_chips(sums_ffn) + to_sibling(g_merge))
    halves = shard_sums(ffn, sums_ffn, landed[0]) if distributed else {}
    sums_merge = chip_sums(merge, g_merge, landed[1]) if distributed else []
    (dzb, dbias_t, dsinks), landed = _swa_bwd(zb, do_b, bias_tables, sink_lanes, name="swa_bwd", exchanges=to_chips(sums_merge))
    if distributed:
        halves.update(shard_sums(merge, sums_merge, landed[0]))
    dbias = dbias_t.reshape(2 * SWA_BLOCK, SWA_HEADS, SWA_BLOCK).transpose(1, 2, 0).reshape(SWA_HEADS, -1)
    d_rel_bias = _bias_grad(dbias, onehot, name="bias_grad").T
    dzc, dmkv, g_wi = _mem_bwd(xb, zc, do_c, mkv, name="mem_bwd")

    proj = ("w_in", "w_mem_kv")
    for dz, offset, nm in ((dza, 0, "grad_w_in_a"), (dzb, W_A, "grad_w_in_b"), (dzd, W_A + W_B + W_C, "grad_w_in_d")):
        g_wi = _mm(dz, xb, mode="tn", tm=dz.shape[1] if dz.shape[1] <= 1280 else 1024, tn=D_MODEL, tk=tk, name=nm,
                   rows_of=IN_COLS, row_offset=offset, into=g_wi)
    g_proj = [g_wi.reshape(N_SHARDS, IN_COLS // N_SHARDS, D_MODEL),
              _mm(memb, dmkv, mode="tn", tm=D_MODEL, tn=512, tk=MEM_LEN, name="grad_w_mem_kv", out_panels=True)]
    sums_proj = chip_sums(proj, g_proj, _run_exchanges(to_sibling(g_proj), name="reduce_sibling_proj")[0]) if distributed else []
    small = dict(lb_logits=hg_stats[1:3], hg_norm_gain=hg_stats[0:1], swa_sinks=dsinks, rel_bias=d_rel_bias,
                 ln1_g=ln_stats[2:3], ln1_b=ln_stats[3:4], ln2_g=ln_stats[0:1], ln2_b=ln_stats[1:2], sq_err=ln_stats[4:5])
    dx_tm = min(512, s)
    small_exchange = [_small_gather_exchange(_pack_small(small, name="pack_small"))] if distributed else []
    grad_x, landed = _dx_matmul([dza, dzb, dzc, dzd], wi_t, dpre1, tm=dx_tm, name="grad_x", tiles=(0, s // dx_tm),
                                exchanges=to_chips(sums_proj) + small_exchange)
    if distributed:
        halves.update(shard_sums(proj, sums_proj, landed[0]))
        small = landed[1][0]
    else:
        halves = dict(zip(ffn + merge + proj, g_ffn + g_merge + g_proj))
    return grad_x, halves, small


def _mesh_position():
    x, y, c = lax.axis_index("x"), lax.axis_index("y"), lax.axis_index("c")
    chips = [(1 - x, y), (x, 1 - y), (1 - x, 1 - y)]
    return x, y, c, chips


class _Exchange(NamedTuple):
    operands: list
    out_shapes: list
    n_sems: int
    start: Callable
    finish: Callable


def _gather_exchange(shards):
    n = len(shards)
    per = 9
    assert all(w.shape[0] % (4 * BF16_SUBLANES) == 0 for w in shards)

    def plan(ins, outs, send_sems, recv_sems):
        x, y, c, (x_nbr, y_nbr, diag) = _mesh_position()
        sibling = (x, y, 1 - c)
        slot = lambda chip: 2 * chip[0] + chip[1]

        def rows(a, chip, hc, quarter=None):
            rh = shards[a].shape[0] // 2
            if quarter is None:
                return outs[a].at[slot(chip), pl.ds(hc * rh, rh), :]
            return outs[a].at[slot(chip), pl.ds(hc * rh + quarter * (rh // 2), rh // 2), :]

        def copy(a, k, src, dst, to):
            return pltpu.make_async_remote_copy(src_ref=src, dst_ref=dst, send_sem=send_sems.at[a * per + k], recv_sem=recv_sems.at[a * per + k],
                                                device_id=to, device_id_type=MESH)

        first, from_sibling = [], []
        landed, then = [[] for _ in range(4)], [[] for _ in range(4)]
        for a in range(n):
            rh = shards[a].shape[0] // 2
            my_half = ins[a].at[pl.ds(c * rh, rh), :]
            first += [copy(a, 4, ins[a], outs[a].at[slot((x, y))], sibling),
                      copy(a, 0, my_half, rows(a, (x, y), c), (*x_nbr, c)), copy(a, 1, my_half, rows(a, (x, y), c), (*y_nbr, c))]
            landed[0].append(copy(a, 0, rows(a, x_nbr, c), rows(a, x_nbr, c), (*x_nbr, c)))
            then[0].append([copy(a, 2, rows(a, x_nbr, c, 0), rows(a, x_nbr, c, 0), (*y_nbr, c)), copy(a, 5, rows(a, x_nbr, c), rows(a, x_nbr, c), sibling)])
            landed[1].append(copy(a, 1, rows(a, y_nbr, c), rows(a, y_nbr, c), (*y_nbr, c)))
            then[1].append([copy(a, 3, rows(a, y_nbr, c, 1), rows(a, y_nbr, c, 1), (*x_nbr, c)), copy(a, 6, rows(a, y_nbr, c), rows(a, y_nbr, c), sibling)])
            landed[2].append(copy(a, 2, rows(a, diag, c, 0), rows(a, diag, c, 0), (*y_nbr, c)))
            then[2].append([copy(a, 7, rows(a, diag, c, 0), rows(a, diag, c, 0), sibling)])
            landed[3].append(copy(a, 3, rows(a, diag, c, 1), rows(a, diag, c, 1), (*x_nbr, c)))
            then[3].append([copy(a, 8, rows(a, diag, c, 1), rows(a, diag, c, 1), sibling)])
            from_sibling += [copy(a, 4, outs[a].at[slot((x, y))], outs[a].at[slot((x, y))], sibling),
                             copy(a, 5, rows(a, x_nbr, 1 - c), rows(a, x_nbr, 1 - c), sibling), copy(a, 6, rows(a, y_nbr, 1 - c), rows(a, y_nbr, 1 - c), sibling),
                             copy(a, 7, rows(a, diag, 1 - c, 0), rows(a, diag, 1 - c, 0), sibling), copy(a, 8, rows(a, diag, 1 - c, 1), rows(a, diag, 1 - c, 1), sibling)]
        return first, sum(landed, []), sum(then, []), from_sibling

    def start(*refs):
        first, _, _, _ = plan(*refs)
        for cp in first:
            cp.start()

    def finish(*refs):
        first, landed, then, from_sibling = plan(*refs)
        for arrival, onward in zip(landed, then):
            arrival.wait_recv()
            for cp in onward:
                cp.start()
        for cp in from_sibling:
            cp.wait_recv()
        for cp in first + [cp for onward in then for cp in onward]:
            cp.wait_send()

    return _Exchange(list(shards), [jax.ShapeDtypeStruct((N_SHARDS,) + w.shape, w.dtype) for w in shards], per * n, start, finish)


def _sibling_halves_exchange(grads):
    n = len(grads)

    def plan(ins, outs, send_sems, recv_sems):
        x, y, c, _ = _mesh_position()
        return [pltpu.make_async_remote_copy(src_ref=ins[a].at[:, pl.ds((1 - c) * (grads[a].shape[1] // 2), grads[a].shape[1] // 2), :],
                                             dst_ref=outs[a], send_sem=send_sems.at[a], recv_sem=recv_sems.at[a],
                                             device_id=(x, y, 1 - c), device_id_type=MESH) for a in range(n)]

    def start(*refs):
        for cp in plan(*refs):
            cp.start()

    def finish(*refs):
        for cp in plan(*refs):
            cp.wait()

    return _Exchange(list(grads), [jax.ShapeDtypeStruct((g.shape[0], g.shape[1] // 2, g.shape[2]), g.dtype) for g in grads], n, start, finish)


def _chip_partials_exchange(sums):
    n = len(sums)

    def plan(ins, outs, send_sems, recv_sems):
        _, _, c, chips = _mesh_position()
        return [pltpu.make_async_remote_copy(src_ref=ins[a].at[2 * cx + cy], dst_ref=outs[a].at[k], send_sem=send_sems.at[a * 3 + k],
                                             recv_sem=recv_sems.at[a * 3 + k], device_id=(cx, cy, c), device_id_type=MESH)
                for k, (cx, cy) in enumerate(chips) for a in range(n)]

    def start(*refs):
        for cp in plan(*refs):
            cp.start()

    def finish(*refs):
        for cp in plan(*refs):
            cp.wait()

    return _Exchange(list(sums), [jax.ShapeDtypeStruct((3,) + g.shape[1:], g.dtype) for g in sums], 3 * n, start, finish)


def _fused_call(body, *, name, grid, in_specs, out_specs, out_shape, scratch_shapes, operands, exchanges=(), aliases=None):
    single = not isinstance(out_shape, (tuple, list))
    out_specs = [out_specs] if single else list(out_specs)
    out_shape = [out_shape] if single else list(out_shape)
    n_in, n_out, n_scr = len(in_specs), len(out_specs), len(scratch_shapes)
    x_in = [len(e.operands) for e in exchanges]
    x_out = [len(e.out_shapes) for e in exchanges]

    def wrapped(*refs):
        refs = list(refs)
        ins = refs[:n_in]
        pos = n_in
        ex_ins = []
        for k in x_in:
            ex_ins.append(refs[pos:pos + k])
            pos += k
        outs = refs[pos:pos + n_out]
        pos += n_out
        ex_outs = []
        for k in x_out:
            ex_outs.append(refs[pos:pos + k])
            pos += k
        scratch = refs[pos:pos + n_scr]
        sems = refs[pos + n_scr:]
        first, last = None, None
        for axis, size in enumerate(grid):
            at_start, at_end = pl.program_id(axis) == 0, pl.program_id(axis) == size - 1
            first = at_start if first is None else first & at_start
            last = at_end if last is None else last & at_end

        @pl.when(first)
        def _():
            for i, e in enumerate(exchanges):
                e.start(ex_ins[i], ex_outs[i], sems[2 * i], sems[2 * i + 1])

        body(*ins, *outs, *scratch)

        @pl.when(last)
        def _():
            for i, e in enumerate(exchanges):
                e.finish(ex_ins[i], ex_outs[i], sems[2 * i], sems[2 * i + 1])

    n_x_in, n_x_out = sum(x_in), sum(x_out)
    results = pl.pallas_call(
        wrapped if exchanges else body, name=name, grid=grid,
        in_specs=list(in_specs) + [HBM] * n_x_in,
        out_specs=out_specs + [HBM] * n_x_out,
        out_shape=out_shape + [s for e in exchanges for s in e.out_shapes],
        scratch_shapes=list(scratch_shapes) + [pltpu.SemaphoreType.DMA((e.n_sems,)) for e in exchanges for _ in range(2)],
        input_output_aliases=aliases or {}, compiler_params=_params(("arbitrary",) * len(grid)),
    )(*operands, *[a for e in exchanges for a in e.operands])
    own = results[0] if single else tuple(results[:n_out])
    landed, pos = [], n_out
    for k in x_out:
        landed.append(list(results[pos:pos + k]))
        pos += k
    return own, landed


def _run_exchanges(exchanges, *, name):
    def body(*refs):
        n_in = sum(len(e.operands) for e in exchanges)
        n_out = sum(len(e.out_shapes) for e in exchanges)
        ins, outs, sems = refs[:n_in], refs[n_in:n_in + n_out], refs[n_in + n_out:]
        spans, i, o = [], 0, 0
        for e in exchanges:
            spans.append((ins[i:i + len(e.operands)], outs[o:o + len(e.out_shapes)]))
            i, o = i + len(e.operands), o + len(e.out_shapes)
        for k, e in enumerate(exchanges):
            e.start(*spans[k], sems[2 * k], sems[2 * k + 1])
        for k, e in enumerate(exchanges):
            e.finish(*spans[k], sems[2 * k], sems[2 * k + 1])

    operands = [a for e in exchanges for a in e.operands]
    shapes = [s for e in exchanges for s in e.out_shapes]
    results = pl.pallas_call(
        body, name=name, out_shape=shapes, in_specs=[HBM] * len(operands), out_specs=[HBM] * len(shapes),
        scratch_shapes=[pltpu.SemaphoreType.DMA((e.n_sems,)) for e in exchanges for _ in range(2)],
    )(*operands)
    landed, pos = [], 0
    for e in exchanges:
        landed.append(list(results[pos:pos + len(e.out_shapes)]))
        pos += len(e.out_shapes)
    return landed


ROW_TILE_MAX = 640
BF16_SUBLANES = 16


def _row_tile(rows):
    for tr in range(min(rows, ROW_TILE_MAX), 0, -1):
        if rows % tr == 0 and tr % BF16_SUBLANES == 0:
            return tr
    raise ValueError(rows)


def _add_sibling(grad, other, pos, *, name):
    p, r, cols = grad.shape
    rh = r // 2
    tr = _row_tile(rh)
    nb = rh // tr

    def body(pos_ref, g_ref, o_ref, sb_ref, mine_ref):
        total = g_ref[...] + o_ref[...]
        sb_ref[...] = total.astype(BF16)

        @pl.when(pl.program_id(1) == pos_ref[0])
        def _():
            mine_ref[...] = total

    return pl.pallas_call(
        body, name=name, out_shape=(jax.ShapeDtypeStruct((p, rh, cols), BF16), jax.ShapeDtypeStruct((rh, cols), F32)),
        grid_spec=pltpu.PrefetchScalarGridSpec(
            num_scalar_prefetch=1, grid=(nb, p),
            in_specs=[pl.BlockSpec((None, tr, cols), lambda i, j, pos_ref: (j, pos_ref[1] * nb + i, 0)),
                      pl.BlockSpec((None, tr, cols), lambda i, j, pos_ref: (j, i, 0))],
            out_specs=(pl.BlockSpec((None, tr, cols), lambda i, j, pos_ref: (j, i, 0)),
                       pl.BlockSpec((tr, cols), lambda i, j, pos_ref: (i, 0)))),
        compiler_params=_params(("parallel", "arbitrary")),
    )(pos, grad, other)


def _add_chips(mine, others, pos, *, name):
    rh, cols = mine.shape
    tr = _row_tile(rh)
    nb = rh // tr

    def body(pos_ref, s_ref, o_ref, r_ref):
        r_ref[...] = ((s_ref[...] + o_ref[0].astype(F32)) + o_ref[1].astype(F32)) + o_ref[2].astype(F32)

    return pl.pallas_call(
        body, name=name, out_shape=jax.ShapeDtypeStruct((2 * rh, cols), F32),
        grid_spec=pltpu.PrefetchScalarGridSpec(
            num_scalar_prefetch=1, grid=(nb,),
            in_specs=[pl.BlockSpec((tr, cols), lambda i, pos_ref: (i, 0)),
                      pl.BlockSpec((3, tr, cols), lambda i, pos_ref: (0, i, 0))],
            out_specs=pl.BlockSpec((tr, cols), lambda i, pos_ref: (pos_ref[1] * nb + i, 0))),
        compiler_params=_params(("parallel",)),
    )(pos, mine, others)


def _join_halves(bufs, *, name):
    n = len(bufs)

    def body(*refs):
        ins, outs = refs[:n], refs[n:2 * n]
        send_sems, recv_sems = refs[2 * n:]
        x, y, c, _ = _mesh_position()

        def copy(a, hc):
            rh = bufs[a].shape[0] // 2
            rows = pl.ds(hc * rh, rh)
            return pltpu.make_async_remote_copy(src_ref=ins[a].at[rows, :], dst_ref=outs[a].at[rows, :], send_sem=send_sems.at[a],
                                                recv_sem=recv_sems.at[a], device_id=(x, y, 1 - c), device_id_type=MESH)

        for a in range(n):
            copy(a, c).start()
        for a in range(n):
            copy(a, c).wait_send()
            copy(a, 1 - c).wait_recv()

    return pl.pallas_call(
        body, name=name, out_shape=[jax.ShapeDtypeStruct(b.shape, b.dtype) for b in bufs],
        in_specs=[HBM] * n, out_specs=[HBM] * n, input_output_aliases={a: a for a in range(n)},
        scratch_shapes=[pltpu.SemaphoreType.DMA((n,)), pltpu.SemaphoreType.DMA((n,))],
    )(*bufs)


SMALL = ["lb_logits", "hg_norm_gain", "swa_sinks", "rel_bias", "ln1_g", "ln1_b", "ln2_g", "ln2_b"]
PACK_ROWS = 48
PACK_AT = dict(lb_logits=(slice(0, 2), slice(0, D_MODEL)), hg_norm_gain=(slice(2, 3), slice(0, D_MODEL)), ln1_g=(slice(3, 4), slice(0, D_MODEL)),
               ln1_b=(slice(4, 5), slice(0, D_MODEL)), ln2_g=(slice(5, 6), slice(0, D_MODEL)), ln2_b=(slice(6, 7), slice(0, D_MODEL)),
               swa_sinks=(slice(7, 8), slice(0, SWA_HEADS)), sq_err=(slice(8, 9), slice(0, D_MODEL)),
               rel_bias=(slice(16, 16 + NUM_BUCKETS), slice(0, SWA_HEADS)))


def _pack_small(grads, *, name):
    names = SMALL + ["sq_err"]

    def body(*refs):
        packed = refs[len(names)]
        packed[...] = jnp.zeros_like(packed)
        for k, g_ref in zip(names, refs):
            packed[PACK_AT[k]] = g_ref[...]

    return pl.pallas_call(body, name=name, out_shape=jax.ShapeDtypeStruct((PACK_ROWS, D_MODEL), F32), compiler_params=_params(),
                          )(*[grads[k] for k in names])


def _small_gather_exchange(packed):
    def plan(ins, outs, send_sems, recv_sems):
        x, y, c, _ = _mesh_position()
        me = 4 * x + 2 * y + c
        own = pltpu.make_async_copy(ins[0], outs[0].at[me], send_sems.at[7])
        remote = []
        for d in range(1, 8):
            dx, dy, dc = (d >> 2) & 1, (d >> 1) & 1, d & 1
            remote.append(pltpu.make_async_remote_copy(src_ref=ins[0], dst_ref=outs[0].at[me], send_sem=send_sems.at[d - 1],
                                                       recv_sem=recv_sems.at[d - 1], device_id=(x ^ dx, y ^ dy, c ^ dc), device_id_type=MESH))
        return own, remote

    def start(*refs):
        own, remote = plan(*refs)
        own.start()
        for cp in remote:
            cp.start()

    def finish(*refs):
        own, remote = plan(*refs)
        for cp in remote:
            cp.wait()
        own.wait()

    return _Exchange([packed], [jax.ShapeDtypeStruct((8,) + packed.shape, packed.dtype)], 8, start, finish)


def _adamw_small(gathered, w, m, v, *, name):
    names = SMALL
    n = len(names)

    def body(*refs):
        gathered_ref = refs[0]
        w_refs, m_refs, v_refs = (dict(zip(names, refs[1 + i * n:1 + (i + 1) * n])) for i in range(3))
        loss_ref = refs[1 + 3 * n]
        go_refs, d_refs, nm_refs, nv_refs = (dict(zip(names, refs[2 + (3 + i) * n:2 + (4 + i) * n])) for i in range(4))
        total_ref = refs[2 + 7 * n]
        total = gathered_ref[0]
        for j in range(1, 8):
            total = total + gathered_ref[j]
        total_ref[...] = total
        loss_ref[...] = (0.5 / D_MODEL) * jnp.sum(total_ref[PACK_AT["sq_err"]], axis=1, keepdims=True)
        for k in names:
            g = total_ref[PACK_AT[k]]
            go_refs[k][...] = g
            d_refs[k][...], nm_refs[k][...], nv_refs[k][...] = _adamw_math(w_refs[k][...], g, m_refs[k][...], v_refs[k][...])

    like = [jax.ShapeDtypeStruct(w[k].shape, F32) for k in names]
    results = pl.pallas_call(body, name=name, out_shape=[jax.ShapeDtypeStruct((1, 1), F32)] + like * 4,
                             scratch_shapes=[pltpu.VMEM((PACK_ROWS, D_MODEL), F32)],
                             compiler_params=_params())(gathered, *[d[k] for d in (w, m, v) for k in names])
    return results[0], {k: tuple(results[1 + i * n + j] for i in range(4)) for j, k in enumerate(names)}


def _adamw_math(w, g, m, v):
    m = ADAM_B1 * m + (1.0 - ADAM_B1) * g
    v = ADAM_B2 * v + (1.0 - ADAM_B2) * (g * g)
    m_hat = m / (1.0 - ADAM_B1 ** ADAM_STEP)
    v_hat = v / (1.0 - ADAM_B2 ** ADAM_STEP)
    delta = -ADAM_LR * (m_hat / (jnp.sqrt(v_hat) + ADAM_EPS) + ADAM_WD * w)
    return delta, m, v


def _adamw(w, g, m, v, *, name):
    _, rows, cols = w.shape
    tr = _row_tile(rows)
    blk = pl.BlockSpec((None, tr, cols), lambda i: (0, i, 0))
    flat = pl.BlockSpec((tr, cols), lambda i: (i, 0))

    def body(w_ref, g_ref, m_ref, v_ref, go_ref, d_ref, nm_ref, nv_ref):
        g_v = g_ref[...]
        go_ref[...] = g_v
        d_ref[...], nm_ref[...], nv_ref[...] = _adamw_math(w_ref[...], g_v, m_ref[...], v_ref[...])

    shape = jax.ShapeDtypeStruct((1, rows, cols), F32)
    return pl.pallas_call(body, name=name, grid=(rows // tr,), out_shape=(shape,) * 4, in_specs=[blk, flat, blk, blk], out_specs=(blk,) * 4,
                          compiler_params=_params(("parallel",)))(w, g, m, v)


WEIGHTS = ["w_in", "lb_logits", "hg_norm_gain", "swa_sinks", "rel_bias", "w_mem_kv", "w_branch_hg", "w_branch_swa", "w_branch_mem",
           "w_out", "ln1_g", "ln1_b", "w_up", "w_down", "ln2_g", "ln2_b"]
BIG = ["w_in", "w_mem_kv", "w_branch_hg", "w_branch_swa", "w_branch_mem", "w_out", "w_up", "w_down"]


def kernel(x, mem, w_in, lb_logits, hg_norm_gain, swa_sinks, rel_bias, w_mem_kv, w_branch_hg, w_branch_swa, w_branch_mem, w_out, ln1_g, ln1_b, w_up, w_down, ln2_g, ln2_b, loss_target, m_w_in, m_lb_logits, m_hg_norm_gain, m_swa_sinks, m_rel_bias, m_w_mem_kv, m_w_branch_hg, m_w_branch_swa, m_w_branch_mem, m_w_out, m_ln1_g, m_ln1_b, m_w_up, m_w_down, m_ln2_g, m_ln2_b, v_w_in, v_lb_logits, v_hg_norm_gain, v_swa_sinks, v_rel_bias, v_w_mem_kv, v_w_branch_hg, v_w_branch_swa, v_w_branch_mem, v_w_out, v_ln1_g, v_ln1_b, v_w_up, v_w_down, v_ln2_g, v_ln2_b):
    w = dict(w_in=w_in, lb_logits=lb_logits, hg_norm_gain=hg_norm_gain, swa_sinks=swa_sinks, rel_bias=rel_bias, w_mem_kv=w_mem_kv,
             w_branch_hg=w_branch_hg, w_branch_swa=w_branch_swa, w_branch_mem=w_branch_mem, w_out=w_out, ln1_g=ln1_g, ln1_b=ln1_b,
             w_up=w_up, w_down=w_down, ln2_g=ln2_g, ln2_b=ln2_b)
    m = dict(w_in=m_w_in, lb_logits=m_lb_logits, hg_norm_gain=m_hg_norm_gain, swa_sinks=m_swa_sinks, rel_bias=m_rel_bias, w_mem_kv=m_w_mem_kv,
             w_branch_hg=m_w_branch_hg, w_branch_swa=m_w_branch_swa, w_branch_mem=m_w_branch_mem, w_out=m_w_out, ln1_g=m_ln1_g, ln1_b=m_ln1_b,
             w_up=m_w_up, w_down=m_w_down, ln2_g=m_ln2_g, ln2_b=m_ln2_b)
    v = dict(w_in=v_w_in, lb_logits=v_lb_logits, hg_norm_gain=v_hg_norm_gain, swa_sinks=v_swa_sinks, rel_bias=v_rel_bias, w_mem_kv=v_w_mem_kv,
             w_branch_hg=v_w_branch_hg, w_branch_swa=v_w_branch_swa, w_branch_mem=v_w_branch_mem, w_out=v_w_out, ln1_g=v_ln1_g, ln1_b=v_ln1_b,
             w_up=v_w_up, w_down=v_w_down, ln2_g=v_ln2_g, ln2_b=v_ln2_b)
    shapes = {k: w[k].shape for k in WEIGHTS}
    for d in (w, m, v):
        d["w_in"] = d["w_in"].reshape(D_MODEL, IN_COLS // N_SHARDS).T[None]
    shards = {k: w[k].reshape(w[k].shape[-2], w[k].shape[-1]).astype(BF16) for k in BIG}
    wi4, wmkv = _run_exchanges([_gather_exchange([shards["w_in"], shards["w_mem_kv"]])], name="gather_weights")[0]
    wi_t = wi4.reshape(IN_COLS, D_MODEL)

    grad_x, halves, small = _local_step(
        x.reshape(x.shape[-2], D_MODEL), mem.reshape(MEM_LEN, D_MODEL), loss_target.reshape(loss_target.shape[-2], D_MODEL),
        wi_t, wmkv, shards, lb_logits, hg_norm_gain, swa_sinks, rel_bias, ln1_g, ln1_b, ln2_g, ln2_b, distributed=True)

    reduced = dict(zip(BIG, _join_halves([halves[k] for k in BIG], name="join_halves")))

    outs = {k: _adamw(w[k], reduced[k], m[k], v[k], name="adamw_" + k) for k in BIG}
    loss, small_outs = _adamw_small(small, w, m, v, name="adamw_small")
    outs.update(small_outs)
    grad_out, delta_out, m_out, v_out = ({k: outs[k][i] for k in WEIGHTS} for i in range(4))
    for out in (grad_out, delta_out, m_out, v_out):
        out["w_in"] = out["w_in"][0].T

    result = [loss.reshape(()), grad_x.reshape(x.shape)]
    for out in (grad_out, delta_out, m_out, v_out):
        result += [out[k].reshape(shapes[k]) for k in WEIGHTS]
    return tuple(result)
```

```python
import math
from typing import Callable, NamedTuple, Optional

import jax
import jax.numpy as jnp
from jax import lax
from jax.experimental import pallas as pl
from jax.experimental.pallas import tpu as pltpu

F32 = jnp.float32
BF16 = jnp.bfloat16
HIGHEST = lax.Precision.HIGHEST
MESH = pl.DeviceIdType.MESH

D_MODEL = 1024
MEM_LEN = 256
HG_HEADS = 8
HG_DK = 128
HG_CHUNK = 64
SWA_HEADS = 16
SWA_KV_HEADS = 2
SWA_GROUP = 8
SWA_HEAD_DIM = 64
SWA_BLOCK = 128
SWA_WINDOW = 128
MEM_HEADS = 4
MEM_HEAD_DIM = 256
NUM_BUCKETS = 32
MAX_DISTANCE = 128
D_FF = 4096
LN_EPS = 1e-5
RMS_EPS = 1e-6
ALPHA = 2.0 ** 0.25
W_A, W_B, W_C, W_D = 4096, 1280, 1024, 3072
IN_COLS = W_A + W_B + W_C + W_D
N_SHARDS = 4
ADAM_LR = 0.001
ADAM_B1 = 0.9
ADAM_B2 = 0.999
ADAM_EPS = 1e-08
ADAM_WD = 0.01
ADAM_STEP = 10
MASK_VALUE = -1e30
VMEM_LIMIT = 56 * 1024 * 1024

NN = ((1,), (0,))
NT = ((1,), (1,))
TN = ((0,), (0,))
HBM = pl.BlockSpec(memory_space=pltpu.HBM)


def _dot(a, b, dims=NN, precision=None):
    return lax.dot_general(a, b, (dims, ((), ())), precision=precision, preferred_element_type=F32)


def _params(sem=None):
    return pltpu.CompilerParams(dimension_semantics=sem, vmem_limit_bytes=VMEM_LIMIT)


def _resident(shape):
    zeros = (0,) * len(shape)
    return pl.BlockSpec(shape, lambda *_: zeros, pipeline_mode=pl.Buffered(1))


def _resident_rows(arr, offset, rows):
    return pl.BlockSpec((pl.Element(rows), pl.Element(arr.shape[1])), lambda *_: (offset, 0), pipeline_mode=pl.Buffered(1))


def _mm(a, b, *, mode, tm, tn, tk, name, out_dtype=F32, b_panels=False, b_rows=None, out_panels=False, rows_of=None, row_offset=0,
        into=None):
    if mode == "tn":
        kdim, m = a.shape
    else:
        m, kdim = a.shape
    if b_panels:
        n = b.shape[0] * b.shape[2]
        assert b.shape[2] == tn and mode == "nn"
    elif b_rows is not None:
        assert mode == "nt"
        b_offset, n = b_rows
    elif mode == "nt":
        n = b.shape[0]
    else:
        n = b.shape[1]
    assert m % tm == 0 and n % tn == 0 and kdim % tk == 0, (name, m, n, kdim)
    nk = kdim // tk
    dims = {"nn": NN, "nt": NT, "tn": TN}[mode]
    a_spec = pl.BlockSpec((tk, tm), lambda i, j, k: (k, i)) if mode == "tn" else pl.BlockSpec((tm, tk), lambda i, j, k: (i, k))
    if b_panels:
        b_spec = pl.BlockSpec((None, tk, tn), lambda i, j, k: (j, k, 0))
    elif b_rows is not None:
        assert b_offset % BF16_SUBLANES == 0 and tn % BF16_SUBLANES == 0 and tk % 128 == 0
        b_spec = pl.BlockSpec((pl.Element(tn), pl.Element(tk)),
                              lambda i, j, k: (pl.multiple_of(b_offset + j * tn, BF16_SUBLANES), pl.multiple_of(k * tk, 128)))
    elif mode == "nt":
        b_spec = pl.BlockSpec((tn, tk), lambda i, j, k: (j, k))
    else:
        b_spec = pl.BlockSpec((tk, tn), lambda i, j, k: (k, j))
    in_specs = [a_spec, b_spec]
    operands = [a, b]
    aliases = {}
    if out_panels:
        out_shape = jax.ShapeDtypeStruct((n // tn, m, tn), out_dtype)
        o_spec = pl.BlockSpec((None, tm, tn), lambda i, j, k: (j, i, 0))
    elif rows_of is not None:
        out_shape = jax.ShapeDtypeStruct((rows_of, n), out_dtype)
        assert row_offset % BF16_SUBLANES == 0 and tm % BF16_SUBLANES == 0 and tn % 128 == 0
        o_spec = pl.BlockSpec((pl.Element(tm), pl.Element(tn)),
                              lambda i, j, k: (pl.multiple_of(row_offset + i * tm, BF16_SUBLANES), pl.multiple_of(j * tn, 128)))
        if into is not None:
            in_specs.append(pl.BlockSpec(memory_space=pl.ANY))
            operands.append(into)
            aliases = {2: 0}
    else:
        out_shape = jax.ShapeDtypeStruct((m, n), out_dtype)
        o_spec = pl.BlockSpec((tm, tn), lambda i, j, k: (i, j))
    n_in = len(operands)

    def body(*refs):
        a_ref, b_ref, o_ref = refs[0], refs[1], refs[n_in]
        part = _dot(a_ref[...].astype(BF16), b_ref[...].astype(BF16), dims)

        def finish(acc):
            o_ref[...] = acc.astype(out_dtype)

        if nk == 1:
            finish(part)
        else:
            acc_ref = refs[-1]
            k = pl.program_id(2)

            @pl.when(k == 0)
            def _():
                acc_ref[...] = part

            @pl.when(k > 0)
            def _():
                acc_ref[...] += part

            @pl.when(k == nk - 1)
            def _():
                finish(acc_ref[...])

    return pl.pallas_call(
        body, name=name, out_shape=out_shape, grid=(m // tm, n // tn, nk), in_specs=in_specs, out_specs=o_spec,
        scratch_shapes=[pltpu.VMEM((tm, tn), F32)] if nk > 1 else [], input_output_aliases=aliases,
        compiler_params=_params(("parallel", "parallel", "arbitrary")),
    )(*operands)


def _dx_matmul(dzs, wi_t, resid, *, tm, name, tiles, into=None, exchanges=()):
    s = resid.shape[0]
    npieces = len(dzs)
    offsets = [sum(dz.shape[1] for dz in dzs[:p]) for p in range(npieces)]
    first, count = tiles
    tile = lambda i: (first + i, 0)
    in_specs = [pl.BlockSpec((tm, dz.shape[1]), tile) for dz in dzs] + [_resident(wi_t.shape), pl.BlockSpec((tm, D_MODEL), tile)]
    operands = [*dzs, wi_t, resid]
    if into is not None:
        in_specs.append(pl.BlockSpec(memory_space=pl.ANY))
        operands.append(into)
    n_in = len(operands)

    def body(*refs):
        dz_refs, w_ref, r_ref, o_ref = refs[:npieces], refs[npieces], refs[npieces + 1], refs[n_in]
        total = ALPHA * r_ref[...]
        for p in range(npieces):
            total = total + _dot(dz_refs[p][...], w_ref[offsets[p]:offsets[p] + dzs[p].shape[1], :], NN)
        o_ref[...] = total

    return _fused_call(
        body, name=name, out_shape=jax.ShapeDtypeStruct((s, D_MODEL), F32), grid=(count,), in_specs=in_specs,
        out_specs=pl.BlockSpec((tm, D_MODEL), tile), scratch_shapes=[], operands=operands, exchanges=exchanges,
        aliases={n_in - 1: 0} if into is not None else None)


def _lower_bound(lbl_ref):
    l0, l1 = lbl_ref[0:1, :], lbl_ref[1:2, :]
    mx = jnp.maximum(l0, l1)
    e0, e1 = jnp.exp(l0 - mx), jnp.exp(l1 - mx)
    return e0 / (e0 + e1)


HEAD_COLS = [slice(h * HG_DK, (h + 1) * HG_DK) for h in range(HG_HEADS)]


def _head_mean(x):
    return jnp.concatenate([jnp.broadcast_to(jnp.mean(x[:, c], axis=-1, keepdims=True), (x.shape[0], HG_DK)) for c in HEAD_COLS], axis=1)


def _triangle_sum(tri_b, x):
    p0 = x.astype(BF16)
    r1 = x - p0.astype(F32)
    p1 = r1.astype(BF16)
    p2 = (r1 - p1.astype(F32)).astype(BF16)
    return _dot(tri_b, p0) + _dot(tri_b, p1) + _dot(tri_b, p2)


def _chunk_forward(q, fl, v, lb, tril_b):
    sg = jax.nn.sigmoid(fl)
    f = lb + (1.0 - lb) * sg
    k = 1.0 - f
    b = _triangle_sum(tril_b, jnp.log(f))
    b_last = b[HG_CHUNK - 1:HG_CHUNK, :]
    eb, enb, eo = jnp.exp(b), jnp.exp(-b), jnp.exp(b_last - b)
    return sg, f, k, b_last, eb, enb, eo, q * eb, k * enb, k * eo


HG_TILE = 256


def _hgrn_fwd(xb, wi_t, lb_logits, gain, *, name, exchanges=()):
    s = xb.shape[0]
    t = min(2 * HG_TILE, s)
    tiles = t // HG_TILE
    ncs = HG_TILE // HG_CHUNK
    nt = s // t
    groups = W_A // D_MODEL

    def body(x_ref, xnext_ref, w_ref, lbl_ref, gain_ref, z_ref, oa_ref, oraw_ref, st_ref, state, z_first):
        @pl.when(pl.program_id(0) == 0)
        def _():
            state[...] = jnp.zeros_like(state)
            z_first[...] = _dot(x_ref[0:HG_TILE, :], w_ref[...], NT)

        z_ref[0:HG_TILE, :] = z_first[...]
        lb_all = _lower_bound(lbl_ref)
        row = lax.broadcasted_iota(jnp.int32, (HG_CHUNK, HG_CHUNK), 0)
        col = lax.broadcasted_iota(jnp.int32, (HG_CHUNK, HG_CHUNK), 1)
        tril = row >= col
        tril_b = tril.astype(BF16)
        gain_all = gain_ref[...]
        for tile in range(tiles):
            if tile + 1 < tiles:
                x_ahead, ahead = x_ref[(tile + 1) * HG_TILE:(tile + 2) * HG_TILE, :], z_ref.at[(tile + 1) * HG_TILE:(tile + 2) * HG_TILE, :]
            else:
                x_ahead, ahead = xnext_ref[...], z_first
            for i in range(ncs):
                if i < groups:
                    ahead[:, i * D_MODEL:(i + 1) * D_MODEL] = _dot(x_ahead, w_ref[i * D_MODEL:(i + 1) * D_MODEL, :], NT)
                r = slice(tile * HG_TILE + i * HG_CHUNK, tile * HG_TILE + (i + 1) * HG_CHUNK)
                q, fl, v, hg = (z_ref[r, j * D_MODEL:(j + 1) * D_MODEL] for j in range(groups))
                _, _, _, b_last, _, _, _, q_in, k_in, k_out = _chunk_forward(q, fl, v, lb_all, tril_b)
                q_in_b, k_in_b, k_out_b, vb = (u.astype(BF16) for u in (q_in, k_in, k_out, v))
                decay = jnp.exp(b_last)
                sts = [state[h] for h in range(HG_HEADS)]
                attn = [_dot(q_in_b[:, c], k_in_b[:, c], NT) for c in HEAD_COLS]
                inter = [_dot(q_in_b[:, c], sts[h].astype(BF16), NT) for h, c in enumerate(HEAD_COLS)]
                upd = [_dot(vb[:, c], k_out_b[:, c], TN) for c in HEAD_COLS]
                attn = [jnp.where(tril, a, 0.0).astype(BF16) for a in attn]
                outs = [_dot(attn[h], vb[:, c], NN) + inter[h] for h, c in enumerate(HEAD_COLS)]
                for h, c in enumerate(HEAD_COLS):
                    st_ref[h, tile * ncs + i] = sts[h]
                    state[h] = sts[h] * decay[:, c] + upd[h]
                o = jnp.concatenate(outs, axis=1)
                oraw_ref[r, :] = o
                n = o * lax.rsqrt(_head_mean(o * o) + RMS_EPS)
                oa_ref[r, :] = (n * gain_all * (hg * jax.nn.sigmoid(hg))).astype(BF16)
            for j in range(ncs, groups):
                ahead[:, j * D_MODEL:(j + 1) * D_MODEL] = _dot(x_ahead, w_ref[j * D_MODEL:(j + 1) * D_MODEL, :], NT)

    step = lambda i: (i, 0)
    last_tile = s // HG_TILE - 1
    return _fused_call(
        body, name=name, grid=(nt,),
        out_shape=(jax.ShapeDtypeStruct((s, W_A), F32), jax.ShapeDtypeStruct((s, D_MODEL), BF16), jax.ShapeDtypeStruct((s, D_MODEL), F32),
                   jax.ShapeDtypeStruct((HG_HEADS, s // HG_CHUNK, HG_DK, HG_DK), F32)),
        in_specs=[pl.BlockSpec((t, D_MODEL), step), pl.BlockSpec((HG_TILE, D_MODEL), lambda i: (jnp.minimum((i + 1) * tiles, last_tile), 0)),
                  _resident_rows(wi_t, 0, W_A), _resident((2, D_MODEL)), _resident((1, D_MODEL))],
        out_specs=(pl.BlockSpec((t, W_A), step), pl.BlockSpec((t, D_MODEL), step), pl.BlockSpec((t, D_MODEL), step),
                   pl.BlockSpec((HG_HEADS, tiles * ncs, HG_DK, HG_DK), lambda i: (0, i, 0, 0))),
        scratch_shapes=[pltpu.VMEM((HG_HEADS, HG_DK, HG_DK), F32), pltpu.VMEM((HG_TILE, W_A), F32)],
        operands=[xb, xb, wi_t, lb_logits, gain], exchanges=exchanges)


def _hgrn_bwd(za, oraw, do_a, states, lb_logits, gain, *, name, exchanges=()):
    s = za.shape[0]
    t = min(256, s)
    ncs = t // HG_CHUNK
    nt = s // t

    def body(z_ref, oraw_ref, do_ref, st_ref, lbl_ref, gain_ref, dz_ref, stats_ref, dstate):
        step = pl.program_id(0)

        @pl.when(step == 0)
        def _():
            dstate[...] = jnp.zeros_like(dstate)
            stats_ref[...] = jnp.zeros_like(stats_ref)

        lb_all = _lower_bound(lbl_ref)
        row = lax.broadcasted_iota(jnp.int32, (HG_CHUNK, HG_CHUNK), 0)
        col = lax.broadcasted_iota(jnp.int32, (HG_CHUNK, HG_CHUNK), 1)
        tril = row >= col
        tril_b = tril.astype(BF16)
        triu_b = (row <= col).astype(BF16)
        gain_all = gain_ref[...]

        def chunk(ii, carry):
            i = ncs - 1 - ii
            r = pl.ds(pl.multiple_of(i * HG_CHUNK, HG_CHUNK), HG_CHUNK)
            q, fl, v, hg = (z_ref[r, j * D_MODEL:(j + 1) * D_MODEL] for j in range(4))
            o = oraw_ref[r, :]
            doa = do_ref[r, :]
            rms = lax.rsqrt(_head_mean(o * o) + RMS_EPS)
            n = o * rms
            sgg = jax.nn.sigmoid(hg)
            silu = hg * sgg
            dhg = doa * n * gain_all * (sgg * (1.0 + hg * (1.0 - sgg)))
            dgain = jnp.sum(doa * n * silu, axis=0, keepdims=True)
            dn = doa * gain_all * silu
            do = rms * (dn - n * _head_mean(dn * n))
            sg, f, k, b_last, eb, enb, eo, q_in, k_in, k_out = _chunk_forward(q, fl, v, lb_all, tril_b)
            q_in_b, k_in_b, k_out_b, vb, dob = (u.astype(BF16) for u in (q_in, k_in, k_out, v, do))
            decay = jnp.exp(b_last)
            sts = [st_ref[h, i] for h in range(HG_HEADS)]
            dsts = [dstate[h] for h in range(HG_HEADS)]
            dsts_b = [d.astype(BF16) for d in dsts]
            heads = list(enumerate(HEAD_COLS))
            attn = [_dot(q_in_b[:, c], k_in_b[:, c], NT) for h, c in heads]
            dattn = [_dot(dob[:, c], vb[:, c], NT) for h, c in heads]
            dq_st = [_dot(dob[:, c], sts[h].astype(BF16), NN) for h, c in heads]
            dk_out = [_dot(vb[:, c], dsts_b[h], NN) for h, c in heads]
            dv_st = [_dot(k_out_b[:, c], dsts_b[h], NT) for h, c in heads]
            dst_o = [_dot(dob[:, c], q_in_b[:, c], TN) for h, c in heads]
            attn = [jnp.where(tril, a, 0.0).astype(BF16) for a in attn]
            dattn = [jnp.where(tril, a, 0.0).astype(BF16) for a in dattn]
            dq_in = jnp.concatenate([_dot(dattn[h], k_in_b[:, c], NN) + dq_st[h] for h, c in heads], axis=1)
            dk_in = jnp.concatenate([_dot(dattn[h], q_in_b[:, c], TN) for h, c in heads], axis=1)
            dv = jnp.concatenate([_dot(attn[h], dob[:, c], TN) + dv_st[h] for h, c in heads], axis=1)
            dk_out = jnp.concatenate(dk_out, axis=1)
            dst_st = jnp.concatenate([jnp.sum(dsts[h] * sts[h], axis=0, keepdims=True) for h in range(HG_HEADS)], axis=1)
            for h, c in heads:
                dstate[h] = dsts[h] * decay[:, c] + dst_o[h]
            db_last = decay * dst_st + jnp.sum(dk_out * k_out, axis=0, keepdims=True)
            db = dq_in * q_in - dk_in * k_in - dk_out * k_out
            dg = _triangle_sum(triu_b, db) + db_last
            dk = dk_in * enb + dk_out * eo
            df = dg / f - dk
            stats_ref[0:1, :] += dgain
            stats_ref[1:2, :] += jnp.sum(df * (1.0 - sg), axis=0, keepdims=True)
            dz_ref[r, 0:1024] = (dq_in * eb).astype(BF16)
            dz_ref[r, 1024:2048] = (df * (1.0 - lb_all) * sg * (1.0 - sg)).astype(BF16)
            dz_ref[r, 2048:3072] = dv.astype(BF16)
            dz_ref[r, 3072:4096] = dhg.astype(BF16)
            return carry

        lax.fori_loop(0, ncs, chunk, 0, unroll=True)

        @pl.when(step == nt - 1)
        def _():
            dl0 = stats_ref[1:2, :] * lb_all * (1.0 - lb_all)
            stats_ref[1:2, :] = dl0
            stats_ref[2:3, :] = -dl0

    rev = lambda i: (nt - 1 - i, 0)
    return _fused_call(
        body, name=name, grid=(nt,),
        out_shape=(jax.ShapeDtypeStruct((s, W_A), BF16), jax.ShapeDtypeStruct((8, D_MODEL), F32)),
        in_specs=[pl.BlockSpec((t, W_A), rev), pl.BlockSpec((t, D_MODEL), rev), pl.BlockSpec((t, D_MODEL), rev),
                  pl.BlockSpec((HG_HEADS, ncs, HG_DK, HG_DK), lambda i: (0, nt - 1 - i, 0, 0)),
                  _resident((2, D_MODEL)), _resident((1, D_MODEL))],
        out_specs=(pl.BlockSpec((t, W_A), rev), pl.BlockSpec((8, D_MODEL), lambda i: (0, 0))),
        scratch_shapes=[pltpu.VMEM((HG_HEADS, HG_DK, HG_DK), F32)],
        operands=[za, oraw, do_a, states, lb_logits, gain], exchanges=exchanges)


def _t5_bucket(n):
    max_exact = NUM_BUCKETS // 2
    nf = jnp.maximum(n, 1).astype(F32)
    large = max_exact + (jnp.log(nf / max_exact) / math.log(MAX_DISTANCE / max_exact) * (NUM_BUCKETS - max_exact)).astype(jnp.int32)
    large = jnp.minimum(large, NUM_BUCKETS - 1)
    return jnp.where(n < max_exact, n, large)


def _bias_selector():
    qi = jnp.arange(SWA_BLOCK)[:, None] + SWA_BLOCK
    kj = jnp.arange(2 * SWA_BLOCK)[None, :]
    dist = qi - kj
    band = ((dist >= 0) & (dist < SWA_WINDOW)).reshape(1, -1)
    bucket = _t5_bucket(jnp.clip(dist, 0, SWA_WINDOW - 1)).reshape(1, -1)
    onehot = ((bucket == jnp.arange(NUM_BUCKETS)[:, None]) & band).astype(F32)
    return onehot, jnp.where(band, 0.0, MASK_VALUE).astype(F32)


def _bias_table(rel_bias_t, onehot, maskrow, *, name):
    def body(rb_ref, oh_ref, mask_ref, o_ref):
        o_ref[...] = _dot(rb_ref[...], oh_ref[...], NN, HIGHEST) + mask_ref[...]

    return pl.pallas_call(body, name=name, out_shape=jax.ShapeDtypeStruct((SWA_HEADS, onehot.shape[1]), F32),
                          compiler_params=_params())(rel_bias_t, onehot, maskrow)


def _bias_grad(dbias2d, onehot, *, name):
    def body(db_ref, oh_ref, o_ref):
        o_ref[...] = _dot(db_ref[...], oh_ref[...], NT, HIGHEST)

    return pl.pallas_call(body, name=name, out_shape=jax.ShapeDtypeStruct((SWA_HEADS, NUM_BUCKETS), F32),
                          compiler_params=_params())(dbias2d, onehot)


GROUP_LANES = SWA_GROUP * SWA_BLOCK


def _swa_operands(zq_ref, kv_cur_ref, kv_prev_ref):
    q = (zq_ref[:, 0:1024] * (SWA_HEAD_DIM ** -0.5)).astype(BF16)
    kv_c = kv_cur_ref[...].astype(BF16)
    kv_p = kv_prev_ref[...].astype(BF16)
    kks = [jnp.concatenate([kv_p[:, g * 64:(g + 1) * 64], kv_c[:, g * 64:(g + 1) * 64]], axis=0) for g in range(SWA_KV_HEADS)]
    vvs = [jnp.concatenate([kv_p[:, 128 + g * 64:128 + (g + 1) * 64], kv_c[:, 128 + g * 64:128 + (g + 1) * 64]], axis=0)
           for g in range(SWA_KV_HEADS)]
    return q, kks, vvs


SWA_PART_HEADS = 8
SWA_PARTS = [(h0 // SWA_GROUP, h0) for h0 in range(0, SWA_HEADS, SWA_PART_HEADS)]


def _part_lanes(h0):
    return slice(h0 * SWA_BLOCK, (h0 + SWA_PART_HEADS) * SWA_BLOCK)


def _stack_heads(x, h0):
    return jnp.concatenate([x[:, h * SWA_HEAD_DIM:(h + 1) * SWA_HEAD_DIM] for h in range(h0, h0 + SWA_PART_HEADS)], axis=0)


def _heads_to_lanes(xt):
    pairs = []
    for j in range(0, xt.shape[1] // SWA_BLOCK, 2):
        two = jnp.concatenate([xt[:, j * SWA_BLOCK:(j + 1) * SWA_BLOCK], xt[:, (j + 1) * SWA_BLOCK:(j + 2) * SWA_BLOCK]], axis=0)
        pairs.append(two.T)
    return jnp.concatenate(pairs, axis=1)


def _swa_softmax(score_t, bias_ref, sink_ref, h0):
    sc = score_t + bias_ref[:, _part_lanes(h0)]
    sink = sink_ref[:, _part_lanes(h0)]
    m = jnp.maximum(jnp.max(sc, axis=0, keepdims=True), sink)
    e = jnp.exp(sc - m)
    e_sink = jnp.exp(sink - m)
    return e, 1.0 / (jnp.sum(e, axis=0, keepdims=True) + e_sink), e_sink


def _swa_tables(bias2d, sinks):
    bias_t = bias2d.reshape(SWA_HEADS, SWA_BLOCK, 2 * SWA_BLOCK).transpose(2, 0, 1).reshape(2 * SWA_BLOCK, SWA_HEADS * SWA_BLOCK)
    first = jnp.where(jnp.arange(2 * SWA_BLOCK)[:, None] < SWA_BLOCK, MASK_VALUE, bias_t)
    return jnp.stack([first, bias_t]), jnp.repeat(sinks, SWA_BLOCK, axis=1)


def _swa_fwd(zb, bias_tables, sink_lanes, *, name, exchanges=()):
    s = zb.shape[0]
    nb = s // SWA_BLOCK

    def body(zq_ref, kvc_ref, kvp_ref, bias_ref, sink_ref, o_ref):
        q, kks, vvs = _swa_operands(zq_ref, kvc_ref, kvp_ref)
        scores = [_dot(kks[g], _stack_heads(q, h0), NT) for g, h0 in SWA_PARTS]
        probs = []
        for score, (_, h0) in zip(scores, SWA_PARTS):
            e, inv, _ = _swa_softmax(score, bias_ref, sink_ref, h0)
            probs.append((e * inv).astype(BF16))
        outs = [_dot(vvs[g], p, TN) for p, (g, _) in zip(probs, SWA_PARTS)]
        o_ref[...] = jnp.concatenate([_heads_to_lanes(o) for o in outs], axis=1).astype(BF16)

    return _fused_call(
        body, name=name, grid=(nb,), out_shape=jax.ShapeDtypeStruct((s, D_MODEL), BF16),
        in_specs=[pl.BlockSpec((SWA_BLOCK, W_B), lambda n: (n, 0)),
                  pl.BlockSpec((SWA_BLOCK, 256), lambda n: (n, 4)),
                  pl.BlockSpec((SWA_BLOCK, 256), lambda n: (jnp.maximum(n - 1, 0), 4)),
                  pl.BlockSpec((None, 2 * SWA_BLOCK, SWA_HEADS * SWA_BLOCK), lambda n: (jnp.minimum(n, 1), 0, 0)),
                  _resident((1, SWA_HEADS * SWA_BLOCK))],
        out_specs=pl.BlockSpec((SWA_BLOCK, D_MODEL), lambda n: (n, 0)), scratch_shapes=[],
        operands=[zb, zb, zb, bias_tables, sink_lanes], exchanges=exchanges)


def _swa_bwd(zb, do_b, bias_tables, sink_lanes, *, name, exchanges=()):
    s = zb.shape[0]
    nb = s // SWA_BLOCK
    scale = SWA_HEAD_DIM ** -0.5

    def body(zq_ref, kvc_ref, kvp_ref, do_ref, bias_ref, sink_ref, dz_ref, dbias_ref, dsink_ref, carry, dsink_acc):
        step = pl.program_id(0)

        @pl.when(step == 0)
        def _():
            carry[...] = jnp.zeros_like(carry)
            dsink_acc[...] = jnp.zeros_like(dsink_acc)
            dbias_ref[...] = jnp.zeros_like(dbias_ref)

        q, kks, vvs = _swa_operands(zq_ref, kvc_ref, kvp_ref)
        do = do_ref[...].astype(BF16)
        parts = range(len(SWA_PARTS))
        q_rows = [_stack_heads(q, h0) for _, h0 in SWA_PARTS]
        do_rows = [_stack_heads(do, h0) for _, h0 in SWA_PARTS]
        scores = [_dot(kks[g], q_rows[i], NT) for i, (g, _) in enumerate(SWA_PARTS)]
        dps = [_dot(vvs[g], do_rows[i], NT) for i, (g, _) in enumerate(SWA_PARTS)]
        ps, dss = [], []
        for i, (_, h0) in enumerate(SWA_PARTS):
            e, inv, e_sink = _swa_softmax(scores[i], bias_ref, sink_ref, h0)
            p = e * inv
            delta = jnp.sum(p * dps[i], axis=0, keepdims=True)
            ds = p * (dps[i] - delta)
            dbias_ref[:, _part_lanes(h0)] += ds
            dsink_acc[:, _part_lanes(h0)] -= e_sink * inv * delta
            ps.append(p.astype(BF16))
            dss.append(ds.astype(BF16))
        dqs = [_dot(kks[g], dss[i], TN) * scale for i, (g, _) in enumerate(SWA_PARTS)]
        in_group = lambda xs, g, axis: jnp.concatenate([xs[i] for i in parts if SWA_PARTS[i][0] == g], axis=axis)
        dkks = [_dot(in_group(dss, g, 1), in_group(q_rows, g, 0), NN) for g in range(SWA_KV_HEADS)]
        dvvs = [_dot(in_group(ps, g, 1), in_group(do_rows, g, 0), NN) for g in range(SWA_KV_HEADS)]
        dkv = jnp.concatenate(dkks + dvvs, axis=1)
        dz_ref[:, 0:1024] = jnp.concatenate([_heads_to_lanes(dq) for dq in dqs], axis=1).astype(BF16)
        dz_ref[:, 1024:1280] = (dkv[SWA_BLOCK:, :] + carry[...]).astype(BF16)
        carry[...] = dkv[:SWA_BLOCK, :]

        @pl.when(step == nb - 1)
        def _():
            acc = dsink_acc[...]
            dsink_ref[...] = jnp.concatenate([jnp.sum(acc[:, h * SWA_BLOCK:(h + 1) * SWA_BLOCK], axis=1, keepdims=True)
                                              for h in range(SWA_HEADS)], axis=1)

    rev = lambda i: (nb - 1 - i, 0)
    table_shape = (2 * SWA_BLOCK, SWA_HEADS * SWA_BLOCK)
    return _fused_call(
        body, name=name, grid=(nb,),
        out_shape=(jax.ShapeDtypeStruct((s, W_B), BF16), jax.ShapeDtypeStruct(table_shape, F32), jax.ShapeDtypeStruct((1, SWA_HEADS), F32)),
        in_specs=[pl.BlockSpec((SWA_BLOCK, W_B), rev),
                  pl.BlockSpec((SWA_BLOCK, 256), lambda i: (nb - 1 - i, 4)),
                  pl.BlockSpec((SWA_BLOCK, 256), lambda i: (jnp.maximum(nb - 2 - i, 0), 4)),
                  pl.BlockSpec((SWA_BLOCK, D_MODEL), rev),
                  pl.BlockSpec((None,) + table_shape, lambda i: (jnp.minimum(nb - 1 - i, 1), 0, 0)),
                  _resident((1, SWA_HEADS * SWA_BLOCK))],
        out_specs=(pl.BlockSpec((SWA_BLOCK, W_B), rev), pl.BlockSpec(table_shape, lambda i: (0, 0)),
                   pl.BlockSpec((1, SWA_HEADS), lambda i: (0, 0))),
        scratch_shapes=[pltpu.VMEM((SWA_BLOCK, 256), F32), pltpu.VMEM((1, SWA_HEADS * SWA_BLOCK), F32)],
        operands=[zb, zb, zb, do_b, bias_tables, sink_lanes], exchanges=exchanges)


MEM_COLS = [slice(h * MEM_HEAD_DIM, (h + 1) * MEM_HEAD_DIM) for h in range(MEM_HEADS)]
MEM_VCOLS = [slice(D_MODEL + h * MEM_HEAD_DIM, D_MODEL + (h + 1) * MEM_HEAD_DIM) for h in range(MEM_HEADS)]


def _mem_probs(zc_ref, mkv_ref):
    qs = [(zc_ref[:, c] * (MEM_HEAD_DIM ** -0.5)).astype(BF16) for c in MEM_COLS]
    scores = [_dot(qs[h], mkv_ref[:, c], NT) for h, c in enumerate(MEM_COLS)]
    ps = []
    for sc in scores:
        e = jnp.exp(sc - jnp.max(sc, axis=-1, keepdims=True))
        ps.append(e / jnp.sum(e, axis=-1, keepdims=True))
    return qs, ps


def _mem_fwd(xb, wi_t, mkv, *, name):
    s = xb.shape[0]
    t = min(512, s)

    def body(x_ref, w_ref, mkv_ref, zc_ref, o_ref):
        zc_ref[...] = _dot(x_ref[...], w_ref[...], NT).astype(BF16)
        _, ps = _mem_probs(zc_ref, mkv_ref)
        ps = [p.astype(BF16) for p in ps]
        o_ref[...] = jnp.concatenate([_dot(ps[h], mkv_ref[:, vc], NN) for h, vc in enumerate(MEM_VCOLS)], axis=1).astype(BF16)

    row = pl.BlockSpec((t, D_MODEL), lambda i: (i, 0))
    return pl.pallas_call(
        body, name=name, grid=(s // t,), out_shape=(jax.ShapeDtypeStruct((s, D_MODEL), BF16),) * 2,
        in_specs=[row, _resident_rows(wi_t, W_A + W_B, W_C), _resident((MEM_LEN, 2 * D_MODEL))],
        out_specs=(row, row), compiler_params=_params(("parallel",)),
    )(xb, wi_t, mkv)


def _mem_bwd(xb, zc, do_c, mkv, *, name):
    s = zc.shape[0]
    t = min(512, s)
    nt = s // t

    def body(x_ref, zc_ref, do_ref, mkv_ref, dz_ref, dmkv_ref, gwi_ref, acc):
        @pl.when(pl.program_id(0) == 0)
        def _():
            dmkv_ref[...] = jnp.zeros_like(dmkv_ref)
            acc[...] = jnp.zeros_like(acc)

        heads = range(MEM_HEADS)
        qs, ps = _mem_probs(zc_ref, mkv_ref)
        dos = [do_ref[:, c].astype(BF16) for c in MEM_COLS]
        dps = [_dot(dos[h], mkv_ref[:, MEM_VCOLS[h]], NT) for h in heads]
        dss = [(ps[h] * (dps[h] - jnp.sum(ps[h] * dps[h], axis=-1, keepdims=True))).astype(BF16) for h in heads]
        ps = [p.astype(BF16) for p in ps]
        dz = jnp.concatenate([_dot(dss[h], mkv_ref[:, MEM_COLS[h]], NN) * (MEM_HEAD_DIM ** -0.5) for h in heads], axis=1).astype(BF16)
        dz_ref[...] = dz
        dmkv_ref[...] += jnp.concatenate([_dot(dss[h], qs[h], TN) for h in heads] + [_dot(ps[h], dos[h], TN) for h in heads], axis=1)
        acc[...] += _dot(dz, x_ref[...], TN)

        @pl.when(pl.program_id(0) == nt - 1)
        def _():
            pltpu.sync_copy(acc, gwi_ref.at[pl.ds(W_A + W_B, W_C), :])

    row = pl.BlockSpec((t, D_MODEL), lambda i: (i, 0))
    return pl.pallas_call(
        body, name=name, grid=(nt,),
        out_shape=(jax.ShapeDtypeStruct((s, D_MODEL), BF16), jax.ShapeDtypeStruct((MEM_LEN, 2 * D_MODEL), F32),
                   jax.ShapeDtypeStruct((IN_COLS, D_MODEL), F32)),
        in_specs=[row, row, row, _resident((MEM_LEN, 2 * D_MODEL))],
        out_specs=(row, pl.BlockSpec((MEM_LEN, 2 * D_MODEL), lambda i: (0, 0)), HBM),
        scratch_shapes=[pltpu.VMEM((W_C, D_MODEL), F32)],
        compiler_params=_params(("arbitrary",)),
    )(xb, zc, do_c, mkv)


def _normalize(pre):
    mu = jnp.mean(pre, axis=-1, keepdims=True)
    xc = pre - mu
    rstd = lax.rsqrt(jnp.mean(xc * xc, axis=-1, keepdims=True) + LN_EPS)
    return xc * rstd, rstd


def _layer_norm_bwd(dh, xhat, rstd, g):
    dxh = dh * g
    dpre = rstd * (dxh - jnp.mean(dxh, axis=-1, keepdims=True) - xhat * jnp.mean(dxh * xhat, axis=-1, keepdims=True))
    return dpre, jnp.sum(dh * xhat, axis=0, keepdims=True), jnp.sum(dh, axis=0, keepdims=True)


def _merge_fwd(o_a, o_b, o_c, x, wi_t, wbr, wo, *, name):
    s = x.shape[0]
    t = min(256, s)
    row = lambda w: pl.BlockSpec((t, w), lambda i: (i, 0))

    def body(oa_ref, ob_ref, oc_ref, x_ref, wg_ref, wbr_ref, wo_ref, zd_ref, xhat_ref, rstd_ref, merged_ref, pa_ref, pb_ref, pc_ref):
        zd_ref[...] = _dot(x_ref[...].astype(BF16), wg_ref[...], NT)
        merged = jnp.zeros((t, D_MODEL), F32)
        for b, (o_ref, p_ref) in enumerate(((oa_ref, pa_ref), (ob_ref, pb_ref), (oc_ref, pc_ref))):
            p = _dot(o_ref[...], wbr_ref[b], NN)
            p_ref[...] = p.astype(BF16)
            merged = merged + jax.nn.sigmoid(zd_ref[:, b * D_MODEL:(b + 1) * D_MODEL]) * p
        merged_b = merged.astype(BF16)
        merged_ref[...] = merged_b
        xhat, rstd = _normalize(ALPHA * x_ref[...] + _dot(merged_b, wo_ref[...], NN))
        xhat_ref[...] = xhat
        rstd_ref[...] = rstd

    act = jax.ShapeDtypeStruct((s, D_MODEL), F32)
    return pl.pallas_call(
        body, name=name, grid=(s // t,),
        out_shape=(jax.ShapeDtypeStruct((s, W_D), F32), act, jax.ShapeDtypeStruct((s, 1), F32)) + (jax.ShapeDtypeStruct((s, D_MODEL), BF16),) * 4,
        in_specs=[row(D_MODEL), row(D_MODEL), row(D_MODEL), row(D_MODEL), _resident_rows(wi_t, W_A + W_B + W_C, W_D),
                  _resident((3, D_MODEL, D_MODEL)), _resident((D_MODEL, D_MODEL))],
        out_specs=(row(W_D), row(D_MODEL), row(1), row(D_MODEL), row(D_MODEL), row(D_MODEL), row(D_MODEL)),
        compiler_params=_params(("parallel",)),
    )(o_a, o_b, o_c, x, wi_t, wbr, wo)


def _merge_bwd(dpre1, zd, pa, pb, pc, o_a, o_b, o_c, merged, wbr, wo, *, name, exchanges=()):
    s = dpre1.shape[0]
    t = min(256, s)
    nt = s // t
    row = lambda w: pl.BlockSpec((t, w), lambda i: (i, 0))

    def body(dpre_ref, zd_ref, pa_ref, pb_ref, pc_ref, oa_ref, ob_ref, oc_ref, mg_ref, wbr_ref, wo_ref,
             dzd_ref, doa_ref, dob_ref, doc_ref, gwa_ref, gwb_ref, gwc_ref, gwo_ref, acc):
        step = pl.program_id(0)

        @pl.when(step == 0)
        def _():
            acc[...] = jnp.zeros_like(acc)

        dpre_b = dpre_ref[...].astype(BF16)
        dmerged = _dot(dpre_b, wo_ref[...], NT)
        acc[3] += _dot(mg_ref[...], dpre_b, TN)
        branches = ((pa_ref, oa_ref, doa_ref), (pb_ref, ob_ref, dob_ref), (pc_ref, oc_ref, doc_ref))
        for b, (p_ref, o_ref, do_ref) in enumerate(branches):
            gate = jax.nn.sigmoid(zd_ref[:, b * D_MODEL:(b + 1) * D_MODEL])
            dzd_ref[:, b * D_MODEL:(b + 1) * D_MODEL] = (dmerged * p_ref[...] * gate * (1.0 - gate)).astype(BF16)
            dp = (dmerged * gate).astype(BF16)
            acc[b] += _dot(o_ref[...], dp, TN)
            do_ref[...] = _dot(dp, wbr_ref[b], NT).astype(do_ref.dtype)

        @pl.when(step == nt - 1)
        def _():
            for b, gw_ref in enumerate((gwa_ref, gwb_ref, gwc_ref, gwo_ref)):
                pltpu.sync_copy(acc.at[b], gw_ref)

    act = jax.ShapeDtypeStruct((s, D_MODEL), F32)
    actb = jax.ShapeDtypeStruct((s, D_MODEL), BF16)
    gw = jax.ShapeDtypeStruct((D_MODEL, D_MODEL), F32)
    return _fused_call(
        body, name=name, grid=(nt,),
        out_shape=(jax.ShapeDtypeStruct((s, W_D), BF16), act, actb, actb, gw, gw, gw, gw),
        in_specs=[row(D_MODEL), row(W_D)] + [row(D_MODEL)] * 7 + [_resident((3, D_MODEL, D_MODEL)), _resident((D_MODEL, D_MODEL))],
        out_specs=(row(W_D),) + (row(D_MODEL),) * 3 + (HBM,) * 4, scratch_shapes=[pltpu.VMEM((4, D_MODEL, D_MODEL), F32)],
        operands=[dpre1, zd, pa, pb, pc, o_a, o_b, o_c, merged, wbr, wo], exchanges=exchanges)


def _mlp_loss(xhat1, rstd1, target, ln1_g, ln1_b, ln2_g, ln2_b, wu, wd, *, name):
    s = xhat1.shape[0]
    t = min(256, s)
    npan = wu.shape[0]
    row = lambda w: pl.BlockSpec((t, w), lambda i: (i, 0))
    vec = _resident((1, D_MODEL))

    def body(xhat_ref, rstd_ref, tgt_ref, g1_ref, b1_ref, g2_ref, b2_ref, wu_ref, wd_ref,
             dpre1_ref, dpre2_ref, h1_ref, a_ref, du_ref, stats_ref):
        @pl.when(pl.program_id(0) == 0)
        def _():
            stats_ref[...] = jnp.zeros_like(stats_ref)

        xhat1_v = xhat_ref[...]
        h1 = xhat1_v * g1_ref[...] + b1_ref[...]
        h1_b = h1.astype(BF16)
        h1_ref[...] = h1_b
        us = []
        ff = jnp.zeros((t, D_MODEL), F32)
        for j in range(npan):
            u = _dot(h1_b, wu_ref[j], NN)
            us.append(u)
            r = jnp.maximum(u, 0.0)
            a_b = (r * r).astype(BF16)
            a_ref[:, j * D_MODEL:(j + 1) * D_MODEL] = a_b
            ff = ff + _dot(a_b, wd_ref[j], NN)
        xhat2, rstd2 = _normalize(ALPHA * h1 + ff)
        err = xhat2 * g2_ref[...] + b2_ref[...] - tgt_ref[...]
        stats_ref[4:5, :] += jnp.sum(err * err, axis=0, keepdims=True)
        dpre2, dg2, db2 = _layer_norm_bwd(err * (1.0 / D_MODEL), xhat2, rstd2, g2_ref[...])
        stats_ref[0:1, :] += dg2
        stats_ref[1:2, :] += db2
        dpre2_b = dpre2.astype(BF16)
        dpre2_ref[...] = dpre2_b
        dh1 = ALPHA * dpre2
        for j in range(npan):
            du_b = (_dot(dpre2_b, wd_ref[j], NT) * (2.0 * jnp.maximum(us[j], 0.0))).astype(BF16)
            du_ref[:, j * D_MODEL:(j + 1) * D_MODEL] = du_b
            dh1 = dh1 + _dot(du_b, wu_ref[j], NT)
        dpre1, dg1, db1 = _layer_norm_bwd(dh1, xhat1_v, rstd_ref[...], g1_ref[...])
        stats_ref[2:3, :] += dg1
        stats_ref[3:4, :] += db1
        dpre1_ref[...] = dpre1

    actb = jax.ShapeDtypeStruct((s, D_MODEL), BF16)
    wide = jax.ShapeDtypeStruct((s, D_FF), BF16)
    return pl.pallas_call(
        body, name=name, grid=(s // t,),
        out_shape=(jax.ShapeDtypeStruct((s, D_MODEL), F32), actb, actb, wide, wide, jax.ShapeDtypeStruct((8, D_MODEL), F32)),
        in_specs=[row(D_MODEL), row(1), row(D_MODEL), vec, vec, vec, vec,
                  _resident((npan, D_MODEL, D_MODEL)), _resident((npan, D_MODEL, D_MODEL))],
        out_specs=(row(D_MODEL), row(D_MODEL), row(D_MODEL), row(D_FF), row(D_FF), pl.BlockSpec((8, D_MODEL), lambda i: (0, 0))),
        compiler_params=_params(("arbitrary",)),
    )(xhat1, rstd1, target, ln1_g, ln1_b, ln2_g, ln2_b, wu, wd)


BRANCH_WEIGHTS = ("w_branch_hg", "w_branch_swa", "w_branch_mem")


def _local_step(x, mem, target, wi_t, wmkv, late, lb_logits, gain, sinks, rel_bias, ln1_g, ln1_b, ln2_g, ln2_b, *, distributed):
    s = x.shape[0]
    tm = min(1024, s)
    tk = min(2048, s)
    xb = x.astype(BF16)
    memb = mem.astype(BF16)
    if distributed:
        cx, cy, cc = lax.axis_index("x"), lax.axis_index("y"), lax.axis_index("c")
        pos = jnp.stack([2 * cx + cy, cc]).astype(jnp.int32)
    gather = (lambda names: [_gather_exchange([late[k] for k in names])]) if distributed else (lambda names: [])
    to_sibling = (lambda grads: [_sibling_halves_exchange(grads)]) if distributed else (lambda grads: [])
    to_chips = (lambda sums: [_chip_partials_exchange([bf for bf, _ in sums])]) if distributed else (lambda sums: [])

    def chip_sums(names, grads, from_sibling):
        return [_add_sibling(g, o, pos, name="add_sibling_" + k) for k, g, o in zip(names, grads, from_sibling)]

    def shard_sums(names, sums, from_chips):
        return {k: _add_chips(mine, o, pos, name="add_chips_" + k) for k, (_, mine), o in zip(names, sums, from_chips)}

    zb = _mm(xb, wi_t, mode="nt", tm=tm, tn=W_B, tk=D_MODEL, name="proj_b", out_dtype=BF16, b_rows=(W_A, W_B))
    mkv = _mm(memb, wmkv, mode="nn", tm=MEM_LEN, tn=512, tk=D_MODEL, name="mem_kv", out_dtype=BF16, b_panels=True)
    onehot, maskrow = _bias_selector()
    bias_tables, sink_lanes = _swa_tables(_bias_table(rel_bias.T, onehot, maskrow, name="bias_table"), sinks)
    (za, o_a, o_raw, states), landed = _hgrn_fwd(xb, wi_t, lb_logits, gain, name="hgrn_fwd", exchanges=gather(("w_up", "w_down")))
    wu, wd = landed[0] if distributed else (late["wu"], late["wd"])
    o_b, landed = _swa_fwd(zb, bias_tables, sink_lanes, name="swa_fwd", exchanges=gather(BRANCH_WEIGHTS + ("w_out",)))
    if distributed:
        wbr = jnp.stack([wb.reshape(D_MODEL, D_MODEL) for wb in landed[0][:3]])
        wo = landed[0][3].reshape(D_MODEL, D_MODEL)
    else:
        wbr, wo = late["wbr"], late["wo"]
    zc, o_c = _mem_fwd(xb, wi_t, mkv, name="mem_fwd")
    zd, xhat1, rstd1, merged, pa, pb, pc = _merge_fwd(o_a, o_b, o_c, x, wi_t, wbr, wo, name="merge_fwd")

    dpre1, dpre2, h1, act, du, ln_stats = _mlp_loss(xhat1, rstd1, target, ln1_g, ln1_b, ln2_g, ln2_b, wu, wd, name="mlp_loss")
    ffn = ("w_down", "w_up")
    g_ffn = [_mm(act, dpre2, mode="tn", tm=1024, tn=D_MODEL, tk=tk, name="grad_w_down").reshape(N_SHARDS, D_FF // N_SHARDS, D_MODEL),
             _mm(h1, du, mode="tn", tm=D_MODEL, tn=1024, tk=tk, name="grad_w_up", out_panels=True)]

    (dzd, do_a, do_b, do_c, *g_merge), landed = _merge_bwd(dpre1, zd, pa, pb, pc, o_a, o_b, o_c, merged, wbr, wo, name="merge_bwd",
                                                           exchanges=to_sibling(g_ffn))
    sums_ffn = chip_sums(ffn, g_ffn, landed[0]) if distributed else []
    merge = BRANCH_WEIGHTS + ("w_out",)
    g_merge = [g.reshape(N_SHARDS, D_MODEL // N_SHARDS, D_MODEL) for g in g_merge]
    (dza, hg_stats), landed = _hgrn_bwd(za, o_raw, do_a, states, lb_logits, gain, name="hgrn_bwd",
                                        exchanges=to_chips(sums_ffn) + to_sibling(g_merge))
    halves = shard_sums(ffn, sums_ffn, landed[0]) if distributed else {}
    sums_merge = chip_sums(merge, g_merge, landed[1]) if distributed else []
    (dzb, dbias_t, dsinks), landed = _swa_bwd(zb, do_b, bias_tables, sink_lanes, name="swa_bwd", exchanges=to_chips(sums_merge))
    if distributed:
        halves.update(shard_sums(merge, sums_merge, landed[0]))
    dbias = dbias_t.reshape(2 * SWA_BLOCK, SWA_HEADS, SWA_BLOCK).transpose(1, 2, 0).reshape(SWA_HEADS, -1)
    d_rel_bias = _bias_grad(dbias, onehot, name="bias_grad").T
    dzc, dmkv, g_wi = _mem_bwd(xb, zc, do_c, mkv, name="mem_bwd")

    proj = ("w_in", "w_mem_kv")
    for dz, offset, nm in ((dza, 0, "grad_w_in_a"), (dzb, W_A, "grad_w_in_b"), (dzd, W_A + W_B + W_C, "grad_w_in_d")):
        g_wi = _mm(dz, xb, mode="tn", tm=dz.shape[1] if dz.shape[1] <= 1280 else 1024, tn=D_MODEL, tk=tk, name=nm,
                   rows_of=IN_COLS, row_offset=offset, into=g_wi)
    g_proj = [g_wi.reshape(N_SHARDS, IN_COLS // N_SHARDS, D_MODEL),
              _mm(memb, dmkv, mode="tn", tm=D_MODEL, tn=512, tk=MEM_LEN, name="grad_w_mem_kv", out_panels=True)]
    sums_proj = chip_sums(proj, g_proj, _run_exchanges(to_sibling(g_proj), name="reduce_sibling_proj")[0]) if distributed else []
    small = dict(lb_logits=hg_stats[1:3], hg_norm_gain=hg_stats[0:1], swa_sinks=dsinks, rel_bias=d_rel_bias,
                 ln1_g=ln_stats[2:3], ln1_b=ln_stats[3:4], ln2_g=ln_stats[0:1], ln2_b=ln_stats[1:2], sq_err=ln_stats[4:5])
    dx_tm = min(512, s)
    small_exchange = [_small_gather_exchange(_pack_small(small, name="pack_small"))] if distributed else []
    grad_x, landed = _dx_matmul([dza, dzb, dzc, dzd], wi_t, dpre1, tm=dx_tm, name="grad_x", tiles=(0, s // dx_tm),
                                exchanges=to_chips(sums_proj) + small_exchange)
    if distributed:
        halves.update(shard_sums(proj, sums_proj, landed[0]))
        small = landed[1][0]
    else:
        halves = dict(zip(ffn + merge + proj, g_ffn + g_merge + g_proj))
    return grad_x, halves, small


def _mesh_position():
    x, y, c = lax.axis_index("x"), lax.axis_index("y"), lax.axis_index("c")
    chips = [(1 - x, y), (x, 1 - y), (1 - x, 1 - y)]
    return x, y, c, chips


class _Exchange(NamedTuple):
    operands: list
    out_shapes: list
    n_sems: int
    start: Callable
    finish: Callable
    halfway: Optional[Callable] = None


def _gather_exchange(shards):
    n = len(shards)
    per = 9
    assert all(w.shape[0] % (4 * BF16_SUBLANES) == 0 for w in shards)

    def plan(ins, outs, send_sems, recv_sems):
        x, y, c, (x_nbr, y_nbr, diag) = _mesh_position()
        sibling = (x, y, 1 - c)
        slot = lambda chip: 2 * chip[0] + chip[1]

        def rows(a, chip, hc, quarter=None):
            rh = shards[a].shape[0] // 2
            if quarter is None:
                return outs[a].at[slot(chip), pl.ds(hc * rh, rh), :]
            return outs[a].at[slot(chip), pl.ds(hc * rh + quarter * (rh // 2), rh // 2), :]

        def copy(a, k, src, dst, to):
            return pltpu.make_async_remote_copy(src_ref=src, dst_ref=dst, send_sem=send_sems.at[a * per + k], recv_sem=recv_sems.at[a * per + k],
                                                device_id=to, device_id_type=MESH)

        first, from_sibling = [], []
        landed, then = [[] for _ in range(4)], [[] for _ in range(4)]
        for a in range(n):
            rh = shards[a].shape[0] // 2
            my_half = ins[a].at[pl.ds(c * rh, rh), :]
            first += [copy(a, 4, ins[a], outs[a].at[slot((x, y))], sibling),
                      copy(a, 0, my_half, rows(a, (x, y), c), (*x_nbr, c)), copy(a, 1, my_half, rows(a, (x, y), c), (*y_nbr, c))]
            landed[0].append(copy(a, 0, rows(a, x_nbr, c), rows(a, x_nbr, c), (*x_nbr, c)))
            then[0].append([copy(a, 2, rows(a, x_nbr, c, 0), rows(a, x_nbr, c, 0), (*y_nbr, c)), copy(a, 5, rows(a, x_nbr, c), rows(a, x_nbr, c), sibling)])
            landed[1].append(copy(a, 1, rows(a, y_nbr, c), rows(a, y_nbr, c), (*y_nbr, c)))
            then[1].append([copy(a, 3, rows(a, y_nbr, c, 1), rows(a, y_nbr, c, 1), (*x_nbr, c)), copy(a, 6, rows(a, y_nbr, c), rows(a, y_nbr, c), sibling)])
            landed[2].append(copy(a, 2, rows(a, diag, c, 0), rows(a, diag, c, 0), (*y_nbr, c)))
            then[2].append([copy(a, 7, rows(a, diag, c, 0), rows(a, diag, c, 0), sibling)])
            landed[3].append(copy(a, 3, rows(a, diag, c, 1), rows(a, diag, c, 1), (*x_nbr, c)))
            then[3].append([copy(a, 8, rows(a, diag, c, 1), rows(a, diag, c, 1), sibling)])
            from_sibling += [copy(a, 4, outs[a].at[slot((x, y))], outs[a].at[slot((x, y))], sibling),
                             copy(a, 5, rows(a, x_nbr, 1 - c), rows(a, x_nbr, 1 - c), sibling), copy(a, 6, rows(a, y_nbr, 1 - c), rows(a, y_nbr, 1 - c), sibling),
                             copy(a, 7, rows(a, diag, 1 - c, 0), rows(a, diag, 1 - c, 0), sibling), copy(a, 8, rows(a, diag, 1 - c, 1), rows(a, diag, 1 - c, 1), sibling)]
        return first, landed, then, from_sibling

    def start(*refs):
        first, _, _, _ = plan(*refs)
        for cp in first:
            cp.start()

    def stages(landed, then, which):
        for stage in which:
            for arrival, onward in zip(landed[stage], then[stage]):
                arrival.wait_recv()
                for cp in onward:
                    cp.start()

    def halfway(*refs):
        _, landed, then, _ = plan(*refs)
        stages(landed, then, (0, 1))

    def finish(*refs):
        first, landed, then, from_sibling = plan(*refs)
        stages(landed, then, (2, 3))
        for cp in from_sibling:
            cp.wait_recv()
        for cp in first + [cp for stage in then for onward in stage for cp in onward]:
            cp.wait_send()

    return _Exchange(list(shards), [jax.ShapeDtypeStruct((N_SHARDS,) + w.shape, w.dtype) for w in shards], per * n, start, finish, halfway)


def _sibling_halves_exchange(grads):
    n = len(grads)

    def plan(ins, outs, send_sems, recv_sems):
        x, y, c, _ = _mesh_position()
        return [pltpu.make_async_remote_copy(src_ref=ins[a].at[:, pl.ds((1 - c) * (grads[a].shape[1] // 2), grads[a].shape[1] // 2), :],
                                             dst_ref=outs[a], send_sem=send_sems.at[a], recv_sem=recv_sems.at[a],
                                             device_id=(x, y, 1 - c), device_id_type=MESH) for a in range(n)]

    def start(*refs):
        for cp in plan(*refs):
            cp.start()

    def finish(*refs):
        for cp in plan(*refs):
            cp.wait()

    return _Exchange(list(grads), [jax.ShapeDtypeStruct((g.shape[0], g.shape[1] // 2, g.shape[2]), g.dtype) for g in grads], n, start, finish)


def _chip_partials_exchange(sums):
    n = len(sums)

    def plan(ins, outs, send_sems, recv_sems):
        _, _, c, chips = _mesh_position()
        return [pltpu.make_async_remote_copy(src_ref=ins[a].at[2 * cx + cy], dst_ref=outs[a].at[k], send_sem=send_sems.at[a * 3 + k],
                                             recv_sem=recv_sems.at[a * 3 + k], device_id=(cx, cy, c), device_id_type=MESH)
                for k, (cx, cy) in enumerate(chips) for a in range(n)]

    def start(*refs):
        for cp in plan(*refs):
            cp.start()

    def finish(*refs):
        for cp in plan(*refs):
            cp.wait()

    return _Exchange(list(sums), [jax.ShapeDtypeStruct((3,) + g.shape[1:], g.dtype) for g in sums], 3 * n, start, finish)


def _fused_call(body, *, name, grid, in_specs, out_specs, out_shape, scratch_shapes, operands, exchanges=(), aliases=None):
    single = not isinstance(out_shape, (tuple, list))
    out_specs = [out_specs] if single else list(out_specs)
    out_shape = [out_shape] if single else list(out_shape)
    n_in, n_out, n_scr = len(in_specs), len(out_specs), len(scratch_shapes)
    x_in = [len(e.operands) for e in exchanges]
    x_out = [len(e.out_shapes) for e in exchanges]

    def wrapped(*refs):
        refs = list(refs)
        ins = refs[:n_in]
        pos = n_in
        ex_ins = []
        for k in x_in:
            ex_ins.append(refs[pos:pos + k])
            pos += k
        outs = refs[pos:pos + n_out]
        pos += n_out
        ex_outs = []
        for k in x_out:
            ex_outs.append(refs[pos:pos + k])
            pos += k
        scratch = refs[pos:pos + n_scr]
        sems = refs[pos + n_scr:]
        first, last, middle = None, None, None
        for axis, size in enumerate(grid):
            at_start, at_end, at_middle = pl.program_id(axis) == 0, pl.program_id(axis) == size - 1, pl.program_id(axis) == size // 2
            first = at_start if first is None else first & at_start
            last = at_end if last is None else last & at_end
            middle = at_middle if middle is None else middle & at_middle

        @pl.when(first)
        def _():
            for i, e in enumerate(exchanges):
                e.start(ex_ins[i], ex_outs[i], sems[2 * i], sems[2 * i + 1])

        if any(e.halfway for e in exchanges):
            @pl.when(middle)
            def _():
                for i, e in enumerate(exchanges):
                    if e.halfway:
                        e.halfway(ex_ins[i], ex_outs[i], sems[2 * i], sems[2 * i + 1])

        body(*ins, *outs, *scratch)

        @pl.when(last)
        def _():
            for i, e in enumerate(exchanges):
                e.finish(ex_ins[i], ex_outs[i], sems[2 * i], sems[2 * i + 1])

    n_x_in, n_x_out = sum(x_in), sum(x_out)
    results = pl.pallas_call(
        wrapped if exchanges else body, name=name, grid=grid,
        in_specs=list(in_specs) + [HBM] * n_x_in,
        out_specs=out_specs + [HBM] * n_x_out,
        out_shape=out_shape + [s for e in exchanges for s in e.out_shapes],
        scratch_shapes=list(scratch_shapes) + [pltpu.SemaphoreType.DMA((e.n_sems,)) for e in exchanges for _ in range(2)],
        input_output_aliases=aliases or {}, compiler_params=_params(("arbitrary",) * len(grid)),
    )(*operands, *[a for e in exchanges for a in e.operands])
    own = results[0] if single else tuple(results[:n_out])
    landed, pos = [], n_out
    for k in x_out:
        landed.append(list(results[pos:pos + k]))
        pos += k
    return own, landed


def _run_exchanges(exchanges, *, name):
    def body(*refs):
        n_in = sum(len(e.operands) for e in exchanges)
        n_out = sum(len(e.out_shapes) for e in exchanges)
        ins, outs, sems = refs[:n_in], refs[n_in:n_in + n_out], refs[n_in + n_out:]
        spans, i, o = [], 0, 0
        for e in exchanges:
            spans.append((ins[i:i + len(e.operands)], outs[o:o + len(e.out_shapes)]))
            i, o = i + len(e.operands), o + len(e.out_shapes)
        for k, e in enumerate(exchanges):
            e.start(*spans[k], sems[2 * k], sems[2 * k + 1])
        for k, e in enumerate(exchanges):
            if e.halfway:
                e.halfway(*spans[k], sems[2 * k], sems[2 * k + 1])
        for k, e in enumerate(exchanges):
            e.finish(*spans[k], sems[2 * k], sems[2 * k + 1])

    operands = [a for e in exchanges for a in e.operands]
    shapes = [s for e in exchanges for s in e.out_shapes]
    results = pl.pallas_call(
        body, name=name, out_shape=shapes, in_specs=[HBM] * len(operands), out_specs=[HBM] * len(shapes),
        scratch_shapes=[pltpu.SemaphoreType.DMA((e.n_sems,)) for e in exchanges for _ in range(2)],
    )(*operands)
    landed, pos = [], 0
    for e in exchanges:
        landed.append(list(results[pos:pos + len(e.out_shapes)]))
        pos += len(e.out_shapes)
    return landed


ROW_TILE_MAX = 640
BF16_SUBLANES = 16


def _row_tile(rows):
    for tr in range(min(rows, ROW_TILE_MAX), 0, -1):
        if rows % tr == 0 and tr % BF16_SUBLANES == 0:
            return tr
    raise ValueError(rows)


def _add_sibling(grad, other, pos, *, name):
    p, r, cols = grad.shape
    rh = r // 2
    tr = _row_tile(rh)
    nb = rh // tr

    def body(pos_ref, g_ref, o_ref, sb_ref, mine_ref):
        total = g_ref[...] + o_ref[...]
        sb_ref[...] = total.astype(BF16)

        @pl.when(pl.program_id(1) == pos_ref[0])
        def _():
            mine_ref[...] = total

    return pl.pallas_call(
        body, name=name, out_shape=(jax.ShapeDtypeStruct((p, rh, cols), BF16), jax.ShapeDtypeStruct((rh, cols), F32)),
        grid_spec=pltpu.PrefetchScalarGridSpec(
            num_scalar_prefetch=1, grid=(nb, p),
            in_specs=[pl.BlockSpec((None, tr, cols), lambda i, j, pos_ref: (j, pos_ref[1] * nb + i, 0)),
                      pl.BlockSpec((None, tr, cols), lambda i, j, pos_ref: (j, i, 0))],
            out_specs=(pl.BlockSpec((None, tr, cols), lambda i, j, pos_ref: (j, i, 0)),
                       pl.BlockSpec((tr, cols), lambda i, j, pos_ref: (i, 0)))),
        compiler_params=_params(("parallel", "arbitrary")),
    )(pos, grad, other)


def _add_chips(mine, others, pos, *, name):
    rh, cols = mine.shape
    tr = _row_tile(rh)
    nb = rh // tr

    def body(pos_ref, s_ref, o_ref, r_ref):
        r_ref[...] = ((s_ref[...] + o_ref[0].astype(F32)) + o_ref[1].astype(F32)) + o_ref[2].astype(F32)

    return pl.pallas_call(
        body, name=name, out_shape=jax.ShapeDtypeStruct((2 * rh, cols), F32),
        grid_spec=pltpu.PrefetchScalarGridSpec(
            num_scalar_prefetch=1, grid=(nb,),
            in_specs=[pl.BlockSpec((tr, cols), lambda i, pos_ref: (i, 0)),
                      pl.BlockSpec((3, tr, cols), lambda i, pos_ref: (0, i, 0))],
            out_specs=pl.BlockSpec((tr, cols), lambda i, pos_ref: (pos_ref[1] * nb + i, 0))),
        compiler_params=_params(("parallel",)),
    )(pos, mine, others)


def _join_halves(bufs, *, name):
    n = len(bufs)

    def body(*refs):
        ins, outs = refs[:n], refs[n:2 * n]
        send_sems, recv_sems = refs[2 * n:]
        x, y, c, _ = _mesh_position()

        def copy(a, hc):
            rh = bufs[a].shape[0] // 2
            rows = pl.ds(hc * rh, rh)
            return pltpu.make_async_remote_copy(src_ref=ins[a].at[rows, :], dst_ref=outs[a].at[rows, :], send_sem=send_sems.at[a],
                                                recv_sem=recv_sems.at[a], device_id=(x, y, 1 - c), device_id_type=MESH)

        for a in range(n):
            copy(a, c).start()
        for a in range(n):
            copy(a, c).wait_send()
            copy(a, 1 - c).wait_recv()

    return pl.pallas_call(
        body, name=name, out_shape=[jax.ShapeDtypeStruct(b.shape, b.dtype) for b in bufs],
        in_specs=[HBM] * n, out_specs=[HBM] * n, input_output_aliases={a: a for a in range(n)},
        scratch_shapes=[pltpu.SemaphoreType.DMA((n,)), pltpu.SemaphoreType.DMA((n,))],
    )(*bufs)


SMALL = ["lb_logits", "hg_norm_gain", "swa_sinks", "rel_bias", "ln1_g", "ln1_b", "ln2_g", "ln2_b"]
PACK_ROWS = 48
PACK_AT = dict(lb_logits=(slice(0, 2), slice(0, D_MODEL)), hg_norm_gain=(slice(2, 3), slice(0, D_MODEL)), ln1_g=(slice(3, 4), slice(0, D_MODEL)),
               ln1_b=(slice(4, 5), slice(0, D_MODEL)), ln2_g=(slice(5, 6), slice(0, D_MODEL)), ln2_b=(slice(6, 7), slice(0, D_MODEL)),
               swa_sinks=(slice(7, 8), slice(0, SWA_HEADS)), sq_err=(slice(8, 9), slice(0, D_MODEL)),
               rel_bias=(slice(16, 16 + NUM_BUCKETS), slice(0, SWA_HEADS)))


def _pack_small(grads, *, name):
    names = SMALL + ["sq_err"]

    def body(*refs):
        packed = refs[len(names)]
        packed[...] = jnp.zeros_like(packed)
        for k, g_ref in zip(names, refs):
            packed[PACK_AT[k]] = g_ref[...]

    return pl.pallas_call(body, name=name, out_shape=jax.ShapeDtypeStruct((PACK_ROWS, D_MODEL), F32), compiler_params=_params(),
                          )(*[grads[k] for k in names])


def _small_gather_exchange(packed):
    def plan(ins, outs, send_sems, recv_sems):
        x, y, c, _ = _mesh_position()
        me = 4 * x + 2 * y + c
        own = pltpu.make_async_copy(ins[0], outs[0].at[me], send_sems.at[7])
        remote = []
        for d in range(1, 8):
            dx, dy, dc = (d >> 2) & 1, (d >> 1) & 1, d & 1
            remote.append(pltpu.make_async_remote_copy(src_ref=ins[0], dst_ref=outs[0].at[me], send_sem=send_sems.at[d - 1],
                                                       recv_sem=recv_sems.at[d - 1], device_id=(x ^ dx, y ^ dy, c ^ dc), device_id_type=MESH))
        return own, remote

    def start(*refs):
        own, remote = plan(*refs)
        own.start()
        for cp in remote:
            cp.start()

    def finish(*refs):
        own, remote = plan(*refs)
        for cp in remote:
            cp.wait()
        own.wait()

    return _Exchange([packed], [jax.ShapeDtypeStruct((8,) + packed.shape, packed.dtype)], 8, start, finish)


def _adamw_small(gathered, w, m, v, *, name):
    names = SMALL
    n = len(names)

    def body(*refs):
        gathered_ref = refs[0]
        w_refs, m_refs, v_refs = (dict(zip(names, refs[1 + i * n:1 + (i + 1) * n])) for i in range(3))
        loss_ref = refs[1 + 3 * n]
        go_refs, d_refs, nm_refs, nv_refs = (dict(zip(names, refs[2 + (3 + i) * n:2 + (4 + i) * n])) for i in range(4))
        total_ref = refs[2 + 7 * n]
        total = gathered_ref[0]
        for j in range(1, 8):
            total = total + gathered_ref[j]
        total_ref[...] = total
        loss_ref[...] = (0.5 / D_MODEL) * jnp.sum(total_ref[PACK_AT["sq_err"]], axis=1, keepdims=True)
        for k in names:
            g = total_ref[PACK_AT[k]]
            go_refs[k][...] = g
            d_refs[k][...], nm_refs[k][...], nv_refs[k][...] = _adamw_math(w_refs[k][...], g, m_refs[k][...], v_refs[k][...])

    like = [jax.ShapeDtypeStruct(w[k].shape, F32) for k in names]
    results = pl.pallas_call(body, name=name, out_shape=[jax.ShapeDtypeStruct((1, 1), F32)] + like * 4,
                             scratch_shapes=[pltpu.VMEM((PACK_ROWS, D_MODEL), F32)],
                             compiler_params=_params())(gathered, *[d[k] for d in (w, m, v) for k in names])
    return results[0], {k: tuple(results[1 + i * n + j] for i in range(4)) for j, k in enumerate(names)}


def _adamw_math(w, g, m, v):
    m = ADAM_B1 * m + (1.0 - ADAM_B1) * g
    v = ADAM_B2 * v + (1.0 - ADAM_B2) * (g * g)
    m_hat = m / (1.0 - ADAM_B1 ** ADAM_STEP)
    v_hat = v / (1.0 - ADAM_B2 ** ADAM_STEP)
    delta = -ADAM_LR * (m_hat / (jnp.sqrt(v_hat) + ADAM_EPS) + ADAM_WD * w)
    return delta, m, v


def _adamw(w, g, m, v, *, name):
    _, rows, cols = w.shape
    tr = _row_tile(rows)
    blk = pl.BlockSpec((None, tr, cols), lambda i: (0, i, 0))
    flat = pl.BlockSpec((tr, cols), lambda i: (i, 0))

    def body(w_ref, g_ref, m_ref, v_ref, go_ref, d_ref, nm_ref, nv_ref):
        g_v = g_ref[...]
        go_ref[...] = g_v
        d_ref[...], nm_ref[...], nv_ref[...] = _adamw_math(w_ref[...], g_v, m_ref[...], v_ref[...])

    shape = jax.ShapeDtypeStruct((1, rows, cols), F32)
    return pl.pallas_call(body, name=name, grid=(rows // tr,), out_shape=(shape,) * 4, in_specs=[blk, flat, blk, blk], out_specs=(blk,) * 4,
                          compiler_params=_params(("parallel",)))(w, g, m, v)


WEIGHTS = ["w_in", "lb_logits", "hg_norm_gain", "swa_sinks", "rel_bias", "w_mem_kv", "w_branch_hg", "w_branch_swa", "w_branch_mem",
           "w_out", "ln1_g", "ln1_b", "w_up", "w_down", "ln2_g", "ln2_b"]
BIG = ["w_in", "w_mem_kv", "w_branch_hg", "w_branch_swa", "w_branch_mem", "w_out", "w_up", "w_down"]


def kernel(x, mem, w_in, lb_logits, hg_norm_gain, swa_sinks, rel_bias, w_mem_kv, w_branch_hg, w_branch_swa, w_branch_mem, w_out, ln1_g, ln1_b, w_up, w_down, ln2_g, ln2_b, loss_target, m_w_in, m_lb_logits, m_hg_norm_gain, m_swa_sinks, m_rel_bias, m_w_mem_kv, m_w_branch_hg, m_w_branch_swa, m_w_branch_mem, m_w_out, m_ln1_g, m_ln1_b, m_w_up, m_w_down, m_ln2_g, m_ln2_b, v_w_in, v_lb_logits, v_hg_norm_gain, v_swa_sinks, v_rel_bias, v_w_mem_kv, v_w_branch_hg, v_w_branch_swa, v_w_branch_mem, v_w_out, v_ln1_g, v_ln1_b, v_w_up, v_w_down, v_ln2_g, v_ln2_b):
    w = dict(w_in=w_in, lb_logits=lb_logits, hg_norm_gain=hg_norm_gain, swa_sinks=swa_sinks, rel_bias=rel_bias, w_mem_kv=w_mem_kv,
             w_branch_hg=w_branch_hg, w_branch_swa=w_branch_swa, w_branch_mem=w_branch_mem, w_out=w_out, ln1_g=ln1_g, ln1_b=ln1_b,
             w_up=w_up, w_down=w_down, ln2_g=ln2_g, ln2_b=ln2_b)
    m = dict(w_in=m_w_in, lb_logits=m_lb_logits, hg_norm_gain=m_hg_norm_gain, swa_sinks=m_swa_sinks, rel_bias=m_rel_bias, w_mem_kv=m_w_mem_kv,
             w_branch_hg=m_w_branch_hg, w_branch_swa=m_w_branch_swa, w_branch_mem=m_w_branch_mem, w_out=m_w_out, ln1_g=m_ln1_g, ln1_b=m_ln1_b,
             w_up=m_w_up, w_down=m_w_down, ln2_g=m_ln2_g, ln2_b=m_ln2_b)
    v = dict(w_in=v_w_in, lb_logits=v_lb_logits, hg_norm_gain=v_hg_norm_gain, swa_sinks=v_swa_sinks, rel_bias=v_rel_bias, w_mem_kv=v_w_mem_kv,
             w_branch_hg=v_w_branch_hg, w_branch_swa=v_w_branch_swa, w_branch_mem=v_w_branch_mem, w_out=v_w_out, ln1_g=v_ln1_g, ln1_b=v_ln1_b,
             w_up=v_w_up, w_down=v_w_down, ln2_g=v_ln2_g, ln2_b=v_ln2_b)
    shapes = {k: w[k].shape for k in WEIGHTS}
    for d in (w, m, v):
        d["w_in"] = d["w_in"].reshape(D_MODEL, IN_COLS // N_SHARDS).T[None]
    shards = {k: w[k].reshape(w[k].shape[-2], w[k].shape[-1]).astype(BF16) for k in BIG}
    wi4, wmkv = _run_exchanges([_gather_exchange([shards["w_in"], shards["w_mem_kv"]])], name="gather_weights")[0]
    wi_t = wi4.reshape(IN_COLS, D_MODEL)

    grad_x, halves, small = _local_step(
        x.reshape(x.shape[-2], D_MODEL), mem.reshape(MEM_LEN, D_MODEL), loss_target.reshape(loss_target.shape[-2], D_MODEL),
        wi_t, wmkv, shards, lb_logits, hg_norm_gain, swa_sinks, rel_bias, ln1_g, ln1_b, ln2_g, ln2_b, distributed=True)

    reduced = dict(zip(BIG, _join_halves([halves[k] for k in BIG], name="join_halves")))

    outs = {k: _adamw(w[k], reduced[k], m[k], v[k], name="adamw_" + k) for k in BIG}
    loss, small_outs = _adamw_small(small, w, m, v, name="adamw_small")
    outs.update(small_outs)
    grad_out, delta_out, m_out, v_out = ({k: outs[k][i] for k in WEIGHTS} for i in range(4))
    for out in (grad_out, delta_out, m_out, v_out):
        out["w_in"] = out["w_in"][0].T

    result = [loss.reshape(()), grad_x.reshape(x.shape)]
    for out in (grad_out, delta_out, m_out, v_out):
        result += [out[k].reshape(shapes[k]) for k in WEIGHTS]
    return tuple(result)
```

```python
import math
from typing import Callable, NamedTuple, Optional

import jax
import jax.numpy as jnp
from jax import lax
from jax.experimental import pallas as pl
from jax.experimental.pallas import tpu as pltpu

F32 = jnp.float32
BF16 = jnp.bfloat16
HIGHEST = lax.Precision.HIGHEST
MESH = pl.DeviceIdType.MESH

D_MODEL = 1024
MEM_LEN = 256
HG_HEADS = 8
HG_DK = 128
HG_CHUNK = 64
SWA_HEADS = 16
SWA_KV_HEADS = 2
SWA_GROUP = 8
SWA_HEAD_DIM = 64
SWA_BLOCK = 128
SWA_WINDOW = 128
MEM_HEADS = 4
MEM_HEAD_DIM = 256
NUM_BUCKETS = 32
MAX_DISTANCE = 128
D_FF = 4096
LN_EPS = 1e-5
RMS_EPS = 1e-6
ALPHA = 2.0 ** 0.25
W_A, W_B, W_C, W_D = 4096, 1280, 1024, 3072
IN_COLS = W_A + W_B + W_C + W_D
N_SHARDS = 4
ADAM_LR = 0.001
ADAM_B1 = 0.9
ADAM_B2 = 0.999
ADAM_EPS = 1e-08
ADAM_WD = 0.01
ADAM_STEP = 10
MASK_VALUE = -1e30
VMEM_LIMIT = 56 * 1024 * 1024

NN = ((1,), (0,))
NT = ((1,), (1,))
TN = ((0,), (0,))
HBM = pl.BlockSpec(memory_space=pltpu.HBM)


def _dot(a, b, dims=NN, precision=None):
    return lax.dot_general(a, b, (dims, ((), ())), precision=precision, preferred_element_type=F32)


def _params(sem=None):
    return pltpu.CompilerParams(dimension_semantics=sem, vmem_limit_bytes=VMEM_LIMIT)


def _resident(shape):
    zeros = (0,) * len(shape)
    return pl.BlockSpec(shape, lambda *_: zeros, pipeline_mode=pl.Buffered(1))


def _resident_rows(arr, offset, rows):
    return pl.BlockSpec((pl.Element(rows), pl.Element(arr.shape[1])), lambda *_: (offset, 0), pipeline_mode=pl.Buffered(1))


def _mm(a, b, *, mode, tm, tn, tk, name, out_dtype=F32, b_panels=False, b_rows=None, out_panels=False, rows_of=None, row_offset=0,
        into=None):
    if mode == "tn":
        kdim, m = a.shape
    else:
        m, kdim = a.shape
    if b_panels:
        n = b.shape[0] * b.shape[2]
        assert b.shape[2] == tn and mode == "nn"
    elif b_rows is not None:
        assert mode == "nt"
        b_offset, n = b_rows
    elif mode == "nt":
        n = b.shape[0]
    else:
        n = b.shape[1]
    assert m % tm == 0 and n % tn == 0 and kdim % tk == 0, (name, m, n, kdim)
    nk = kdim // tk
    dims = {"nn": NN, "nt": NT, "tn": TN}[mode]
    a_spec = pl.BlockSpec((tk, tm), lambda i, j, k: (k, i)) if mode == "tn" else pl.BlockSpec((tm, tk), lambda i, j, k: (i, k))
    if b_panels:
        b_spec = pl.BlockSpec((None, tk, tn), lambda i, j, k: (j, k, 0))
    elif b_rows is not None:
        assert b_offset % BF16_SUBLANES == 0 and tn % BF16_SUBLANES == 0 and tk % 128 == 0
        b_spec = pl.BlockSpec((pl.Element(tn), pl.Element(tk)),
                              lambda i, j, k: (pl.multiple_of(b_offset + j * tn, BF16_SUBLANES), pl.multiple_of(k * tk, 128)))
    elif mode == "nt":
        b_spec = pl.BlockSpec((tn, tk), lambda i, j, k: (j, k))
    else:
        b_spec = pl.BlockSpec((tk, tn), lambda i, j, k: (k, j))
    in_specs = [a_spec, b_spec]
    operands = [a, b]
    aliases = {}
    if out_panels:
        out_shape = jax.ShapeDtypeStruct((n // tn, m, tn), out_dtype)
        o_spec = pl.BlockSpec((None, tm, tn), lambda i, j, k: (j, i, 0))
    elif rows_of is not None:
        out_shape = jax.ShapeDtypeStruct((rows_of, n), out_dtype)
        assert row_offset % BF16_SUBLANES == 0 and tm % BF16_SUBLANES == 0 and tn % 128 == 0
        o_spec = pl.BlockSpec((pl.Element(tm), pl.Element(tn)),
                              lambda i, j, k: (pl.multiple_of(row_offset + i * tm, BF16_SUBLANES), pl.multiple_of(j * tn, 128)))
        if into is not None:
            in_specs.append(pl.BlockSpec(memory_space=pl.ANY))
            operands.append(into)
            aliases = {2: 0}
    else:
        out_shape = jax.ShapeDtypeStruct((m, n), out_dtype)
        o_spec = pl.BlockSpec((tm, tn), lambda i, j, k: (i, j))
    n_in = len(operands)

    def body(*refs):
        a_ref, b_ref, o_ref = refs[0], refs[1], refs[n_in]
        part = _dot(a_ref[...].astype(BF16), b_ref[...].astype(BF16), dims)

        def finish(acc):
            o_ref[...] = acc.astype(out_dtype)

        if nk == 1:
            finish(part)
        else:
            acc_ref = refs[-1]
            k = pl.program_id(2)

            @pl.when(k == 0)
            def _():
                acc_ref[...] = part

            @pl.when(k > 0)
            def _():
                acc_ref[...] += part

            @pl.when(k == nk - 1)
            def _():
                finish(acc_ref[...])

    return pl.pallas_call(
        body, name=name, out_shape=out_shape, grid=(m // tm, n // tn, nk), in_specs=in_specs, out_specs=o_spec,
        scratch_shapes=[pltpu.VMEM((tm, tn), F32)] if nk > 1 else [], input_output_aliases=aliases,
        compiler_params=_params(("parallel", "parallel", "arbitrary")),
    )(*operands)


def _dx_matmul(dzs, wi_t, resid, *, tm, name, tiles, into=None, exchanges=()):
    s = resid.shape[0]
    npieces = len(dzs)
    offsets = [sum(dz.shape[1] for dz in dzs[:p]) for p in range(npieces)]
    first, count = tiles
    tile = lambda i: (first + i, 0)
    in_specs = [pl.BlockSpec((tm, dz.shape[1]), tile) for dz in dzs] + [_resident(wi_t.shape), pl.BlockSpec((tm, D_MODEL), tile)]
    operands = [*dzs, wi_t, resid]
    if into is not None:
        in_specs.append(pl.BlockSpec(memory_space=pl.ANY))
        operands.append(into)
    n_in = len(operands)

    def body(*refs):
        dz_refs, w_ref, r_ref, o_ref = refs[:npieces], refs[npieces], refs[npieces + 1], refs[n_in]
        total = ALPHA * r_ref[...]
        for p in range(npieces):
            total = total + _dot(dz_refs[p][...], w_ref[offsets[p]:offsets[p] + dzs[p].shape[1], :], NN)
        o_ref[...] = total

    return _fused_call(
        body, name=name, out_shape=jax.ShapeDtypeStruct((s, D_MODEL), F32), grid=(count,), in_specs=in_specs,
        out_specs=pl.BlockSpec((tm, D_MODEL), tile), scratch_shapes=[], operands=operands, exchanges=exchanges,
        aliases={n_in - 1: 0} if into is not None else None)


def _lower_bound(lbl_ref):
    l0, l1 = lbl_ref[0:1, :], lbl_ref[1:2, :]
    mx = jnp.maximum(l0, l1)
    e0, e1 = jnp.exp(l0 - mx), jnp.exp(l1 - mx)
    return e0 / (e0 + e1)


HEAD_COLS = [slice(h * HG_DK, (h + 1) * HG_DK) for h in range(HG_HEADS)]


def _head_mean(x):
    return jnp.concatenate([jnp.broadcast_to(jnp.mean(x[:, c], axis=-1, keepdims=True), (x.shape[0], HG_DK)) for c in HEAD_COLS], axis=1)


def _triangle_sum(tri_b, x):
    p0 = x.astype(BF16)
    r1 = x - p0.astype(F32)
    p1 = r1.astype(BF16)
    p2 = (r1 - p1.astype(F32)).astype(BF16)
    return _dot(tri_b, p0) + _dot(tri_b, p1) + _dot(tri_b, p2)


def _chunk_forward(q, fl, v, lb, tril_b):
    sg = jax.nn.sigmoid(fl)
    f = lb + (1.0 - lb) * sg
    k = 1.0 - f
    b = _triangle_sum(tril_b, jnp.log(f))
    b_last = b[HG_CHUNK - 1:HG_CHUNK, :]
    eb, enb, eo = jnp.exp(b), jnp.exp(-b), jnp.exp(b_last - b)
    return sg, f, k, b_last, eb, enb, eo, q * eb, k * enb, k * eo


def _hgrn_fwd(xb, wi_t, lb_logits, gain, *, name, exchanges=()):
    s = xb.shape[0]
    t = min(256, s)
    ncs = t // HG_CHUNK

    def body(x_ref, w_ref, lbl_ref, gain_ref, z_ref, oa_ref, oraw_ref, st_ref, state):
        @pl.when(pl.program_id(0) == 0)
        def _():
            state[...] = jnp.zeros_like(state)

        z_ref[...] = _dot(x_ref[...], w_ref[...], NT)
        lb_all = _lower_bound(lbl_ref)
        row = lax.broadcasted_iota(jnp.int32, (HG_CHUNK, HG_CHUNK), 0)
        col = lax.broadcasted_iota(jnp.int32, (HG_CHUNK, HG_CHUNK), 1)
        tril = row >= col
        tril_b = tril.astype(BF16)
        gain_all = gain_ref[...]

        def chunk(i, carry):
            r = pl.ds(pl.multiple_of(i * HG_CHUNK, HG_CHUNK), HG_CHUNK)
            q, fl, v, hg = (z_ref[r, j * D_MODEL:(j + 1) * D_MODEL] for j in range(4))
            _, _, _, b_last, _, _, _, q_in, k_in, k_out = _chunk_forward(q, fl, v, lb_all, tril_b)
            q_in_b, k_in_b, k_out_b, vb = (u.astype(BF16) for u in (q_in, k_in, k_out, v))
            decay = jnp.exp(b_last)
            sts = [state[h] for h in range(HG_HEADS)]
            attn = [_dot(q_in_b[:, c], k_in_b[:, c], NT) for c in HEAD_COLS]
            inter = [_dot(q_in_b[:, c], sts[h].astype(BF16), NT) for h, c in enumerate(HEAD_COLS)]
            upd = [_dot(vb[:, c], k_out_b[:, c], TN) for c in HEAD_COLS]
            attn = [jnp.where(tril, a, 0.0).astype(BF16) for a in attn]
            outs = [_dot(attn[h], vb[:, c], NN) + inter[h] for h, c in enumerate(HEAD_COLS)]
            for h, c in enumerate(HEAD_COLS):
                st_ref[h, i] = sts[h]
                state[h] = sts[h] * decay[:, c] + upd[h]
            o = jnp.concatenate(outs, axis=1)
            oraw_ref[r, :] = o
            n = o * lax.rsqrt(_head_mean(o * o) + RMS_EPS)
            oa_ref[r, :] = (n * gain_all * (hg * jax.nn.sigmoid(hg))).astype(BF16)
            return carry

        lax.fori_loop(0, ncs, chunk, 0, unroll=True)

    tile = lambda i: (i, 0)
    return _fused_call(
        body, name=name, grid=(s // t,),
        out_shape=(jax.ShapeDtypeStruct((s, W_A), F32), jax.ShapeDtypeStruct((s, D_MODEL), BF16), jax.ShapeDtypeStruct((s, D_MODEL), F32),
                   jax.ShapeDtypeStruct((HG_HEADS, s // HG_CHUNK, HG_DK, HG_DK), F32)),
        in_specs=[pl.BlockSpec((t, D_MODEL), tile), _resident_rows(wi_t, 0, W_A), _resident((2, D_MODEL)), _resident((1, D_MODEL))],
        out_specs=(pl.BlockSpec((t, W_A), tile), pl.BlockSpec((t, D_MODEL), tile), pl.BlockSpec((t, D_MODEL), tile),
                   pl.BlockSpec((HG_HEADS, ncs, HG_DK, HG_DK), lambda i: (0, i, 0, 0))),
        scratch_shapes=[pltpu.VMEM((HG_HEADS, HG_DK, HG_DK), F32)],
        operands=[xb, wi_t, lb_logits, gain], exchanges=exchanges)


def _hgrn_bwd(za, oraw, do_a, states, lb_logits, gain, *, name, exchanges=()):
    s = za.shape[0]
    t = min(256, s)
    ncs = t // HG_CHUNK
    nt = s // t

    def body(z_ref, oraw_ref, do_ref, st_ref, lbl_ref, gain_ref, dz_ref, stats_ref, dstate):
        step = pl.program_id(0)

        @pl.when(step == 0)
        def _():
            dstate[...] = jnp.zeros_like(dstate)
            stats_ref[...] = jnp.zeros_like(stats_ref)

        lb_all = _lower_bound(lbl_ref)
        row = lax.broadcasted_iota(jnp.int32, (HG_CHUNK, HG_CHUNK), 0)
        col = lax.broadcasted_iota(jnp.int32, (HG_CHUNK, HG_CHUNK), 1)
        tril = row >= col
        tril_b = tril.astype(BF16)
        triu_b = (row <= col).astype(BF16)
        gain_all = gain_ref[...]

        def chunk(ii, carry):
            i = ncs - 1 - ii
            r = pl.ds(pl.multiple_of(i * HG_CHUNK, HG_CHUNK), HG_CHUNK)
            q, fl, v, hg = (z_ref[r, j * D_MODEL:(j + 1) * D_MODEL] for j in range(4))
            o = oraw_ref[r, :]
            doa = do_ref[r, :]
            rms = lax.rsqrt(_head_mean(o * o) + RMS_EPS)
            n = o * rms
            sgg = jax.nn.sigmoid(hg)
            silu = hg * sgg
            dhg = doa * n * gain_all * (sgg * (1.0 + hg * (1.0 - sgg)))
            dgain = jnp.sum(doa * n * silu, axis=0, keepdims=True)
            dn = doa * gain_all * silu
            do = rms * (dn - n * _head_mean(dn * n))
            sg, f, k, b_last, eb, enb, eo, q_in, k_in, k_out = _chunk_forward(q, fl, v, lb_all, tril_b)
            q_in_b, k_in_b, k_out_b, vb, dob = (u.astype(BF16) for u in (q_in, k_in, k_out, v, do))
            decay = jnp.exp(b_last)
            sts = [st_ref[h, i] for h in range(HG_HEADS)]
            dsts = [dstate[h] for h in range(HG_HEADS)]
            dsts_b = [d.astype(BF16) for d in dsts]
            heads = list(enumerate(HEAD_COLS))
            attn = [_dot(q_in_b[:, c], k_in_b[:, c], NT) for h, c in heads]
            dattn = [_dot(dob[:, c], vb[:, c], NT) for h, c in heads]
            dq_st = [_dot(dob[:, c], sts[h].astype(BF16), NN) for h, c in heads]
            dk_out = [_dot(vb[:, c], dsts_b[h], NN) for h, c in heads]
            dv_st = [_dot(k_out_b[:, c], dsts_b[h], NT) for h, c in heads]
            dst_o = [_dot(dob[:, c], q_in_b[:, c], TN) for h, c in heads]
            attn = [jnp.where(tril, a, 0.0).astype(BF16) for a in attn]
            dattn = [jnp.where(tril, a, 0.0).astype(BF16) for a in dattn]
            dq_in = jnp.concatenate([_dot(dattn[h], k_in_b[:, c], NN) + dq_st[h] for h, c in heads], axis=1)
            dk_in = jnp.concatenate([_dot(dattn[h], q_in_b[:, c], TN) for h, c in heads], axis=1)
            dv = jnp.concatenate([_dot(attn[h], dob[:, c], TN) + dv_st[h] for h, c in heads], axis=1)
            dk_out = jnp.concatenate(dk_out, axis=1)
            dst_st = jnp.concatenate([jnp.sum(dsts[h] * sts[h], axis=0, keepdims=True) for h in range(HG_HEADS)], axis=1)
            for h, c in heads:
                dstate[h] = dsts[h] * decay[:, c] + dst_o[h]
            db_last = decay * dst_st + jnp.sum(dk_out * k_out, axis=0, keepdims=True)
            db = dq_in * q_in - dk_in * k_in - dk_out * k_out
            dg = _triangle_sum(triu_b, db) + db_last
            dk = dk_in * enb + dk_out * eo
            df = dg / f - dk
            stats_ref[0:1, :] += dgain
            stats_ref[1:2, :] += jnp.sum(df * (1.0 - sg), axis=0, keepdims=True)
            dz_ref[r, 0:1024] = (dq_in * eb).astype(BF16)
            dz_ref[r, 1024:2048] = (df * (1.0 - lb_all) * sg * (1.0 - sg)).astype(BF16)
            dz_ref[r, 2048:3072] = dv.astype(BF16)
            dz_ref[r, 3072:4096] = dhg.astype(BF16)
            return carry

        lax.fori_loop(0, ncs, chunk, 0, unroll=True)

        @pl.when(step == nt - 1)
        def _():
            dl0 = stats_ref[1:2, :] * lb_all * (1.0 - lb_all)
            stats_ref[1:2, :] = dl0
            stats_ref[2:3, :] = -dl0

    rev = lambda i: (nt - 1 - i, 0)
    return _fused_call(
        body, name=name, grid=(nt,),
        out_shape=(jax.ShapeDtypeStruct((s, W_A), BF16), jax.ShapeDtypeStruct((8, D_MODEL), F32)),
        in_specs=[pl.BlockSpec((t, W_A), rev), pl.BlockSpec((t, D_MODEL), rev), pl.BlockSpec((t, D_MODEL), rev),
                  pl.BlockSpec((HG_HEADS, ncs, HG_DK, HG_DK), lambda i: (0, nt - 1 - i, 0, 0)),
                  _resident((2, D_MODEL)), _resident((1, D_MODEL))],
        out_specs=(pl.BlockSpec((t, W_A), rev), pl.BlockSpec((8, D_MODEL), lambda i: (0, 0))),
        scratch_shapes=[pltpu.VMEM((HG_HEADS, HG_DK, HG_DK), F32)],
        operands=[za, oraw, do_a, states, lb_logits, gain], exchanges=exchanges)


def _t5_bucket(n):
    max_exact = NUM_BUCKETS // 2
    nf = jnp.maximum(n, 1).astype(F32)
    large = max_exact + (jnp.log(nf / max_exact) / math.log(MAX_DISTANCE / max_exact) * (NUM_BUCKETS - max_exact)).astype(jnp.int32)
    large = jnp.minimum(large, NUM_BUCKETS - 1)
    return jnp.where(n < max_exact, n, large)


def _bias_selector():
    qi = jnp.arange(SWA_BLOCK)[:, None] + SWA_BLOCK
    kj = jnp.arange(2 * SWA_BLOCK)[None, :]
    dist = qi - kj
    band = ((dist >= 0) & (dist < SWA_WINDOW)).reshape(1, -1)
    bucket = _t5_bucket(jnp.clip(dist, 0, SWA_WINDOW - 1)).reshape(1, -1)
    onehot = ((bucket == jnp.arange(NUM_BUCKETS)[:, None]) & band).astype(F32)
    return onehot, jnp.where(band, 0.0, MASK_VALUE).astype(F32)


def _bias_table(rel_bias_t, onehot, maskrow, *, name):
    def body(rb_ref, oh_ref, mask_ref, o_ref):
        o_ref[...] = _dot(rb_ref[...], oh_ref[...], NN, HIGHEST) + mask_ref[...]

    return pl.pallas_call(body, name=name, out_shape=jax.ShapeDtypeStruct((SWA_HEADS, onehot.shape[1]), F32),
                          compiler_params=_params())(rel_bias_t, onehot, maskrow)


def _bias_grad(dbias2d, onehot, *, name):
    def body(db_ref, oh_ref, o_ref):
        o_ref[...] = _dot(db_ref[...], oh_ref[...], NT, HIGHEST)

    return pl.pallas_call(body, name=name, out_shape=jax.ShapeDtypeStruct((SWA_HEADS, NUM_BUCKETS), F32),
                          compiler_params=_params())(dbias2d, onehot)


def _swa_operands(zq_ref, kv_cur_ref, kv_prev_ref):
    q = (zq_ref[:, 0:1024] * (SWA_HEAD_DIM ** -0.5)).astype(BF16)
    kv_c = kv_cur_ref[...].astype(BF16)
    kv_p = kv_prev_ref[...].astype(BF16)
    kks = [jnp.concatenate([kv_p[:, g * 64:(g + 1) * 64], kv_c[:, g * 64:(g + 1) * 64]], axis=0) for g in range(SWA_KV_HEADS)]
    vvs = [jnp.concatenate([kv_p[:, 128 + g * 64:128 + (g + 1) * 64], kv_c[:, 128 + g * 64:128 + (g + 1) * 64]], axis=0)
           for g in range(SWA_KV_HEADS)]
    return q, kks, vvs


SWA_PART_HEADS = 8
SWA_PARTS = [(h0 // SWA_GROUP, h0) for h0 in range(0, SWA_HEADS, SWA_PART_HEADS)]


def _part_lanes(h0):
    return slice(h0 * SWA_BLOCK, (h0 + SWA_PART_HEADS) * SWA_BLOCK)


def _stack_heads(x, h0):
    return jnp.concatenate([x[:, h * SWA_HEAD_DIM:(h + 1) * SWA_HEAD_DIM] for h in range(h0, h0 + SWA_PART_HEADS)], axis=0)


def _heads_to_lanes(xt):
    pairs = []
    for j in range(0, xt.shape[1] // SWA_BLOCK, 2):
        two = jnp.concatenate([xt[:, j * SWA_BLOCK:(j + 1) * SWA_BLOCK], xt[:, (j + 1) * SWA_BLOCK:(j + 2) * SWA_BLOCK]], axis=0)
        pairs.append(two.T)
    return jnp.concatenate(pairs, axis=1)


def _swa_softmax(score_t, bias_ref, sink_ref, h0):
    sc = score_t + bias_ref[:, _part_lanes(h0)]
    sink = sink_ref[:, _part_lanes(h0)]
    m = jnp.maximum(jnp.max(sc, axis=0, keepdims=True), sink)
    e = jnp.exp(sc - m)
    e_sink = jnp.exp(sink - m)
    return e, 1.0 / (jnp.sum(e, axis=0, keepdims=True) + e_sink), e_sink


def _swa_tables(bias2d, sinks):
    bias_t = bias2d.reshape(SWA_HEADS, SWA_BLOCK, 2 * SWA_BLOCK).transpose(2, 0, 1).reshape(2 * SWA_BLOCK, SWA_HEADS * SWA_BLOCK)
    first = jnp.where(jnp.arange(2 * SWA_BLOCK)[:, None] < SWA_BLOCK, MASK_VALUE, bias_t)
    return jnp.stack([first, bias_t]), jnp.repeat(sinks, SWA_BLOCK, axis=1)


def _swa_fwd(zb, bias_tables, sink_lanes, *, name, exchanges=()):
    s = zb.shape[0]
    nb = s // SWA_BLOCK

    def body(zq_ref, kvc_ref, kvp_ref, bias_ref, sink_ref, o_ref):
        q, kks, vvs = _swa_operands(zq_ref, kvc_ref, kvp_ref)
        scores = [_dot(kks[g], _stack_heads(q, h0), NT) for g, h0 in SWA_PARTS]
        probs = []
        for score, (_, h0) in zip(scores, SWA_PARTS):
            e, inv, _ = _swa_softmax(score, bias_ref, sink_ref, h0)
            probs.append((e * inv).astype(BF16))
        outs = [_dot(vvs[g], p, TN) for p, (g, _) in zip(probs, SWA_PARTS)]
        o_ref[...] = jnp.concatenate([_heads_to_lanes(o) for o in outs], axis=1).astype(BF16)

    return _fused_call(
        body, name=name, grid=(nb,), out_shape=jax.ShapeDtypeStruct((s, D_MODEL), BF16),
        in_specs=[pl.BlockSpec((SWA_BLOCK, W_B), lambda n: (n, 0)),
                  pl.BlockSpec((SWA_BLOCK, 256), lambda n: (n, 4)),
                  pl.BlockSpec((SWA_BLOCK, 256), lambda n: (jnp.maximum(n - 1, 0), 4)),
                  pl.BlockSpec((None, 2 * SWA_BLOCK, SWA_HEADS * SWA_BLOCK), lambda n: (jnp.minimum(n, 1), 0, 0)),
                  _resident((1, SWA_HEADS * SWA_BLOCK))],
        out_specs=pl.BlockSpec((SWA_BLOCK, D_MODEL), lambda n: (n, 0)), scratch_shapes=[],
        operands=[zb, zb, zb, bias_tables, sink_lanes], exchanges=exchanges)


def _swa_bwd(zb, do_b, bias_tables, sink_lanes, *, name, exchanges=()):
    s = zb.shape[0]
    nb = s // SWA_BLOCK
    scale = SWA_HEAD_DIM ** -0.5

    def body(zq_ref, kvc_ref, kvp_ref, do_ref, bias_ref, sink_ref, dz_ref, dbias_ref, dsink_ref, carry, dsink_acc):
        step = pl.program_id(0)

        @pl.when(step == 0)
        def _():
            carry[...] = jnp.zeros_like(carry)
            dsink_acc[...] = jnp.zeros_like(dsink_acc)
            dbias_ref[...] = jnp.zeros_like(dbias_ref)

        q, kks, vvs = _swa_operands(zq_ref, kvc_ref, kvp_ref)
        do = do_ref[...].astype(BF16)
        parts = range(len(SWA_PARTS))
        q_rows = [_stack_heads(q, h0) for _, h0 in SWA_PARTS]
        do_rows = [_stack_heads(do, h0) for _, h0 in SWA_PARTS]
        scores = [_dot(kks[g], q_rows[i], NT) for i, (g, _) in enumerate(SWA_PARTS)]
        soft = [_swa_softmax(scores[i], bias_ref, sink_ref, h0) for i, (_, h0) in enumerate(SWA_PARTS)]
        dps = [_dot(vvs[g], do_rows[i], NT) for i, (g, _) in enumerate(SWA_PARTS)]
        ps, dss = [], []
        for i, (_, h0) in enumerate(SWA_PARTS):
            e, inv, e_sink = soft[i]
            p = e * inv
            delta = jnp.sum(p * dps[i], axis=0, keepdims=True)
            ds = p * (dps[i] - delta)
            dbias_ref[:, _part_lanes(h0)] += ds
            dsink_acc[:, _part_lanes(h0)] -= e_sink * inv * delta
            ps.append(p.astype(BF16))
            dss.append(ds.astype(BF16))
        dqs = [_dot(kks[g], dss[i], TN) * scale for i, (g, _) in enumerate(SWA_PARTS)]
        in_group = lambda xs, g, axis: jnp.concatenate([xs[i] for i in parts if SWA_PARTS[i][0] == g], axis=axis)
        dkks = [_dot(in_group(dss, g, 1), in_group(q_rows, g, 0), NN) for g in range(SWA_KV_HEADS)]
        dvvs = [_dot(in_group(ps, g, 1), in_group(do_rows, g, 0), NN) for g in range(SWA_KV_HEADS)]
        dkv = jnp.concatenate(dkks + dvvs, axis=1)
        dz_ref[:, 0:1024] = jnp.concatenate([_heads_to_lanes(dq) for dq in dqs], axis=1).astype(BF16)
        dz_ref[:, 1024:1280] = (dkv[SWA_BLOCK:, :] + carry[...]).astype(BF16)
        carry[...] = dkv[:SWA_BLOCK, :]

        @pl.when(step == nb - 1)
        def _():
            acc = dsink_acc[...]
            dsink_ref[...] = jnp.concatenate([jnp.sum(acc[:, h * SWA_BLOCK:(h + 1) * SWA_BLOCK], axis=1, keepdims=True)
                                              for h in range(SWA_HEADS)], axis=1)

    rev = lambda i: (nb - 1 - i, 0)
    table_shape = (2 * SWA_BLOCK, SWA_HEADS * SWA_BLOCK)
    return _fused_call(
        body, name=name, grid=(nb,),
        out_shape=(jax.ShapeDtypeStruct((s, W_B), BF16), jax.ShapeDtypeStruct(table_shape, F32), jax.ShapeDtypeStruct((1, SWA_HEADS), F32)),
        in_specs=[pl.BlockSpec((SWA_BLOCK, W_B), rev),
                  pl.BlockSpec((SWA_BLOCK, 256), lambda i: (nb - 1 - i, 4)),
                  pl.BlockSpec((SWA_BLOCK, 256), lambda i: (jnp.maximum(nb - 2 - i, 0), 4)),
                  pl.BlockSpec((SWA_BLOCK, D_MODEL), rev),
                  pl.BlockSpec((None,) + table_shape, lambda i: (jnp.minimum(nb - 1 - i, 1), 0, 0)),
                  _resident((1, SWA_HEADS * SWA_BLOCK))],
        out_specs=(pl.BlockSpec((SWA_BLOCK, W_B), rev), pl.BlockSpec(table_shape, lambda i: (0, 0)),
                   pl.BlockSpec((1, SWA_HEADS), lambda i: (0, 0))),
        scratch_shapes=[pltpu.VMEM((SWA_BLOCK, 256), F32), pltpu.VMEM((1, SWA_HEADS * SWA_BLOCK), F32)],
        operands=[zb, zb, zb, do_b, bias_tables, sink_lanes], exchanges=exchanges)


MEM_COLS = [slice(h * MEM_HEAD_DIM, (h + 1) * MEM_HEAD_DIM) for h in range(MEM_HEADS)]
MEM_VCOLS = [slice(D_MODEL + h * MEM_HEAD_DIM, D_MODEL + (h + 1) * MEM_HEAD_DIM) for h in range(MEM_HEADS)]


def _mem_probs(zc_ref, mkv_ref):
    qs = [(zc_ref[:, c] * (MEM_HEAD_DIM ** -0.5)).astype(BF16) for c in MEM_COLS]
    scores = [_dot(qs[h], mkv_ref[:, c], NT) for h, c in enumerate(MEM_COLS)]
    ps = []
    for sc in scores:
        e = jnp.exp(sc - jnp.max(sc, axis=-1, keepdims=True))
        ps.append(e / jnp.sum(e, axis=-1, keepdims=True))
    return qs, ps


def _mem_fwd(xb, wi_t, mkv, *, name):
    s = xb.shape[0]
    t = min(512, s)

    def body(x_ref, w_ref, mkv_ref, zc_ref, o_ref):
        zc_ref[...] = _dot(x_ref[...], w_ref[...], NT).astype(BF16)
        _, ps = _mem_probs(zc_ref, mkv_ref)
        ps = [p.astype(BF16) for p in ps]
        o_ref[...] = jnp.concatenate([_dot(ps[h], mkv_ref[:, vc], NN) for h, vc in enumerate(MEM_VCOLS)], axis=1).astype(BF16)

    row = pl.BlockSpec((t, D_MODEL), lambda i: (i, 0))
    return pl.pallas_call(
        body, name=name, grid=(s // t,), out_shape=(jax.ShapeDtypeStruct((s, D_MODEL), BF16),) * 2,
        in_specs=[row, _resident_rows(wi_t, W_A + W_B, W_C), _resident((MEM_LEN, 2 * D_MODEL))],
        out_specs=(row, row), compiler_params=_params(("parallel",)),
    )(xb, wi_t, mkv)


def _mem_bwd(xb, zc, do_c, mkv, *, name):
    s = zc.shape[0]
    t = min(512, s)
    nt = s // t

    def body(x_ref, zc_ref, do_ref, mkv_ref, dz_ref, dmkv_ref, gwi_ref, acc):
        @pl.when(pl.program_id(0) == 0)
        def _():
            dmkv_ref[...] = jnp.zeros_like(dmkv_ref)
            acc[...] = jnp.zeros_like(acc)

        heads = range(MEM_HEADS)
        qs, ps = _mem_probs(zc_ref, mkv_ref)
        dos = [do_ref[:, c].astype(BF16) for c in MEM_COLS]
        dps = [_dot(dos[h], mkv_ref[:, MEM_VCOLS[h]], NT) for h in heads]
        dss = [(ps[h] * (dps[h] - jnp.sum(ps[h] * dps[h], axis=-1, keepdims=True))).astype(BF16) for h in heads]
        ps = [p.astype(BF16) for p in ps]
        dz = jnp.concatenate([_dot(dss[h], mkv_ref[:, MEM_COLS[h]], NN) * (MEM_HEAD_DIM ** -0.5) for h in heads], axis=1).astype(BF16)
        dz_ref[...] = dz
        dmkv_ref[...] += jnp.concatenate([_dot(dss[h], qs[h], TN) for h in heads] + [_dot(ps[h], dos[h], TN) for h in heads], axis=1)
        acc[...] += _dot(dz, x_ref[...], TN)

        @pl.when(pl.program_id(0) == nt - 1)
        def _():
            pltpu.sync_copy(acc, gwi_ref.at[pl.ds(W_A + W_B, W_C), :])

    row = pl.BlockSpec((t, D_MODEL), lambda i: (i, 0))
    return pl.pallas_call(
        body, name=name, grid=(nt,),
        out_shape=(jax.ShapeDtypeStruct((s, D_MODEL), BF16), jax.ShapeDtypeStruct((MEM_LEN, 2 * D_MODEL), F32),
                   jax.ShapeDtypeStruct((IN_COLS, D_MODEL), F32)),
        in_specs=[row, row, row, _resident((MEM_LEN, 2 * D_MODEL))],
        out_specs=(row, pl.BlockSpec((MEM_LEN, 2 * D_MODEL), lambda i: (0, 0)), HBM),
        scratch_shapes=[pltpu.VMEM((W_C, D_MODEL), F32)],
        compiler_params=_params(("arbitrary",)),
    )(xb, zc, do_c, mkv)


def _normalize(pre):
    mu = jnp.mean(pre, axis=-1, keepdims=True)
    xc = pre - mu
    rstd = lax.rsqrt(jnp.mean(xc * xc, axis=-1, keepdims=True) + LN_EPS)
    return xc * rstd, rstd


def _layer_norm_bwd(dh, xhat, rstd, g):
    dxh = dh * g
    dpre = rstd * (dxh - jnp.mean(dxh, axis=-1, keepdims=True) - xhat * jnp.mean(dxh * xhat, axis=-1, keepdims=True))
    return dpre, jnp.sum(dh * xhat, axis=0, keepdims=True), jnp.sum(dh, axis=0, keepdims=True)


def _merge_fwd(o_a, o_b, o_c, x, wi_t, wbr, wo, *, name):
    s = x.shape[0]
    t = min(256, s)
    row = lambda w: pl.BlockSpec((t, w), lambda i: (i, 0))

    def body(oa_ref, ob_ref, oc_ref, x_ref, wg_ref, wa_ref, wb_ref, wc_ref, wo_ref, zd_ref, xhat_ref, rstd_ref, merged_ref, pa_ref, pb_ref, pc_ref):
        wbr_refs = (wa_ref, wb_ref, wc_ref)
        zd_ref[...] = _dot(x_ref[...].astype(BF16), wg_ref[...], NT)
        merged = jnp.zeros((t, D_MODEL), F32)
        for b, (o_ref, p_ref) in enumerate(((oa_ref, pa_ref), (ob_ref, pb_ref), (oc_ref, pc_ref))):
            p = _dot(o_ref[...], wbr_refs[b][...], NN)
            p_ref[...] = p.astype(BF16)
            merged = merged + jax.nn.sigmoid(zd_ref[:, b * D_MODEL:(b + 1) * D_MODEL]) * p
        merged_b = merged.astype(BF16)
        merged_ref[...] = merged_b
        xhat, rstd = _normalize(ALPHA * x_ref[...] + _dot(merged_b, wo_ref[...], NN))
        xhat_ref[...] = xhat
        rstd_ref[...] = rstd

    act = jax.ShapeDtypeStruct((s, D_MODEL), F32)
    return pl.pallas_call(
        body, name=name, grid=(s // t,),
        out_shape=(jax.ShapeDtypeStruct((s, W_D), F32), act, jax.ShapeDtypeStruct((s, 1), F32)) + (jax.ShapeDtypeStruct((s, D_MODEL), BF16),) * 4,
        in_specs=[row(D_MODEL)] * 4 + [_resident_rows(wi_t, W_A + W_B + W_C, W_D)] + [_resident((D_MODEL, D_MODEL))] * 4,
        out_specs=(row(W_D), row(D_MODEL), row(1), row(D_MODEL), row(D_MODEL), row(D_MODEL), row(D_MODEL)),
        compiler_params=_params(("parallel",)),
    )(o_a, o_b, o_c, x, wi_t, *wbr, wo)


def _merge_bwd(dpre1, zd, pa, pb, pc, o_a, o_b, o_c, merged, wbr, wo, *, name, exchanges=()):
    s = dpre1.shape[0]
    t = min(256, s)
    nt = s // t
    row = lambda w: pl.BlockSpec((t, w), lambda i: (i, 0))

    def body(dpre_ref, zd_ref, pa_ref, pb_ref, pc_ref, oa_ref, ob_ref, oc_ref, mg_ref, wa_ref, wb_ref, wc_ref, wo_ref,
             dzd_ref, doa_ref, dob_ref, doc_ref, gwa_ref, gwb_ref, gwc_ref, gwo_ref, acc):
        step = pl.program_id(0)

        @pl.when(step == 0)
        def _():
            acc[...] = jnp.zeros_like(acc)

        dpre_b = dpre_ref[...].astype(BF16)
        dmerged = _dot(dpre_b, wo_ref[...], NT)
        acc[3] += _dot(mg_ref[...], dpre_b, TN)
        branches = ((pa_ref, oa_ref, doa_ref), (pb_ref, ob_ref, dob_ref), (pc_ref, oc_ref, doc_ref))
        for b, (p_ref, o_ref, do_ref) in enumerate(branches):
            gate = jax.nn.sigmoid(zd_ref[:, b * D_MODEL:(b + 1) * D_MODEL])
            dzd_ref[:, b * D_MODEL:(b + 1) * D_MODEL] = (dmerged * p_ref[...] * gate * (1.0 - gate)).astype(BF16)
            dp = (dmerged * gate).astype(BF16)
            acc[b] += _dot(o_ref[...], dp, TN)
            do_ref[...] = _dot(dp, (wa_ref, wb_ref, wc_ref)[b][...], NT).astype(do_ref.dtype)

        @pl.when(step == nt - 1)
        def _():
            for b, gw_ref in enumerate((gwa_ref, gwb_ref, gwc_ref, gwo_ref)):
                pltpu.sync_copy(acc.at[b], gw_ref)

    act = jax.ShapeDtypeStruct((s, D_MODEL), F32)
    actb = jax.ShapeDtypeStruct((s, D_MODEL), BF16)
    gw = jax.ShapeDtypeStruct((D_MODEL, D_MODEL), F32)
    return _fused_call(
        body, name=name, grid=(nt,),
        out_shape=(jax.ShapeDtypeStruct((s, W_D), BF16), act, actb, actb, gw, gw, gw, gw),
        in_specs=[row(D_MODEL), row(W_D)] + [row(D_MODEL)] * 7 + [_resident((D_MODEL, D_MODEL))] * 4,
        out_specs=(row(W_D),) + (row(D_MODEL),) * 3 + (HBM,) * 4, scratch_shapes=[pltpu.VMEM((4, D_MODEL, D_MODEL), F32)],
        operands=[dpre1, zd, pa, pb, pc, o_a, o_b, o_c, merged, *wbr, wo], exchanges=exchanges)


def _mlp_loss(xhat1, rstd1, target, ln1_g, ln1_b, ln2_g, ln2_b, wu, wd, *, name):
    s = xhat1.shape[0]
    t = min(256, s)
    npan = wu.shape[0]
    row = lambda w: pl.BlockSpec((t, w), lambda i: (i, 0))
    vec = _resident((1, D_MODEL))

    def body(xhat_ref, rstd_ref, tgt_ref, g1_ref, b1_ref, g2_ref, b2_ref, wu_ref, wd_ref,
             dpre1_ref, dpre2_ref, h1_ref, a_ref, du_ref, stats_ref):
        @pl.when(pl.program_id(0) == 0)
        def _():
            stats_ref[...] = jnp.zeros_like(stats_ref)

        xhat1_v = xhat_ref[...]
        h1 = xhat1_v * g1_ref[...] + b1_ref[...]
        h1_b = h1.astype(BF16)
        h1_ref[...] = h1_b
        us = []
        ff = jnp.zeros((t, D_MODEL), F32)
        for j in range(npan):
            u = _dot(h1_b, wu_ref[j], NN)
            us.append(u)
            r = jnp.maximum(u, 0.0)
            a_b = (r * r).astype(BF16)
            a_ref[:, j * D_MODEL:(j + 1) * D_MODEL] = a_b
            ff = ff + _dot(a_b, wd_ref[j], NN)
        xhat2, rstd2 = _normalize(ALPHA * h1 + ff)
        err = xhat2 * g2_ref[...] + b2_ref[...] - tgt_ref[...]
        stats_ref[4:5, :] += jnp.sum(err * err, axis=0, keepdims=True)
        dpre2, dg2, db2 = _layer_norm_bwd(err * (1.0 / D_MODEL), xhat2, rstd2, g2_ref[...])
        stats_ref[0:1, :] += dg2
        stats_ref[1:2, :] += db2
        dpre2_b = dpre2.astype(BF16)
        dpre2_ref[...] = dpre2_b
        dh1 = ALPHA * dpre2
        for j in range(npan):
            du_b = (_dot(dpre2_b, wd_ref[j], NT) * (2.0 * jnp.maximum(us[j], 0.0))).astype(BF16)
            du_ref[:, j * D_MODEL:(j + 1) * D_MODEL] = du_b
            dh1 = dh1 + _dot(du_b, wu_ref[j], NT)
        dpre1, dg1, db1 = _layer_norm_bwd(dh1, xhat1_v, rstd_ref[...], g1_ref[...])
        stats_ref[2:3, :] += dg1
        stats_ref[3:4, :] += db1
        dpre1_ref[...] = dpre1

    actb = jax.ShapeDtypeStruct((s, D_MODEL), BF16)
    wide = jax.ShapeDtypeStruct((s, D_FF), BF16)
    return pl.pallas_call(
        body, name=name, grid=(s // t,),
        out_shape=(jax.ShapeDtypeStruct((s, D_MODEL), F32), actb, actb, wide, wide, jax.ShapeDtypeStruct((8, D_MODEL), F32)),
        in_specs=[row(D_MODEL), row(1), row(D_MODEL), vec, vec, vec, vec,
                  _resident((npan, D_MODEL, D_MODEL)), _resident((npan, D_MODEL, D_MODEL))],
        out_specs=(row(D_MODEL), row(D_MODEL), row(D_MODEL), row(D_FF), row(D_FF), pl.BlockSpec((8, D_MODEL), lambda i: (0, 0))),
        compiler_params=_params(("arbitrary",)),
    )(xhat1, rstd1, target, ln1_g, ln1_b, ln2_g, ln2_b, wu, wd)


BRANCH_WEIGHTS = ("w_branch_hg", "w_branch_swa", "w_branch_mem")


def _local_step(x, mem, target, wi_t, wmkv, late, lb_logits, gain, sinks, rel_bias, ln1_g, ln1_b, ln2_g, ln2_b, *, distributed):
    s = x.shape[0]
    tm = min(1024, s)
    tk = min(2048, s)
    xb = x.astype(BF16)
    memb = mem.astype(BF16)
    if distributed:
        cx, cy, cc = lax.axis_index("x"), lax.axis_index("y"), lax.axis_index("c")
        pos = jnp.stack([2 * cx + cy, cc]).astype(jnp.int32)
    gather = (lambda names: [_gather_exchange([late[k] for k in names])]) if distributed else (lambda names: [])
    to_sibling = (lambda grads: [_sibling_halves_exchange(grads)]) if distributed else (lambda grads: [])
    to_chips = (lambda sums: [_chip_partials_exchange([bf for bf, _ in sums])]) if distributed else (lambda sums: [])

    def chip_sums(names, grads, from_sibling):
        return [_add_sibling(g, o, pos, name="add_sibling_" + k) for k, g, o in zip(names, grads, from_sibling)]

    def shard_sums(names, sums, from_chips):
        return {k: _add_chips(mine, o, pos, name="add_chips_" + k) for k, (_, mine), o in zip(names, sums, from_chips)}

    zb = _mm(xb, wi_t, mode="nt", tm=tm, tn=W_B, tk=D_MODEL, name="proj_b", out_dtype=BF16, b_rows=(W_A, W_B))
    mkv = _mm(memb, wmkv, mode="nn", tm=MEM_LEN, tn=512, tk=D_MODEL, name="mem_kv", out_dtype=BF16, b_panels=True)
    onehot, maskrow = _bias_selector()
    bias_tables, sink_lanes = _swa_tables(_bias_table(rel_bias.T, onehot, maskrow, name="bias_table"), sinks)
    (za, o_a, o_raw, states), landed = _hgrn_fwd(xb, wi_t, lb_logits, gain, name="hgrn_fwd", exchanges=gather(("w_up", "w_down")))
    wu, wd = landed[0] if distributed else (late["wu"], late["wd"])
    o_b, landed = _swa_fwd(zb, bias_tables, sink_lanes, name="swa_fwd", exchanges=gather(BRANCH_WEIGHTS + ("w_out",)))
    if distributed:
        wbr = [wb.reshape(D_MODEL, D_MODEL) for wb in landed[0][:3]]
        wo = landed[0][3].reshape(D_MODEL, D_MODEL)
    else:
        wbr, wo = [late["wbr"][b] for b in range(3)], late["wo"]
    zc, o_c = _mem_fwd(xb, wi_t, mkv, name="mem_fwd")
    zd, xhat1, rstd1, merged, pa, pb, pc = _merge_fwd(o_a, o_b, o_c, x, wi_t, wbr, wo, name="merge_fwd")

    dpre1, dpre2, h1, act, du, ln_stats = _mlp_loss(xhat1, rstd1, target, ln1_g, ln1_b, ln2_g, ln2_b, wu, wd, name="mlp_loss")
    ffn = ("w_down", "w_up")
    g_ffn = [_mm(act, dpre2, mode="tn", tm=1024, tn=D_MODEL, tk=tk, name="grad_w_down").reshape(N_SHARDS, D_FF // N_SHARDS, D_MODEL),
             _mm(h1, du, mode="tn", tm=D_MODEL, tn=1024, tk=tk, name="grad_w_up", out_panels=True)]

    (dzd, do_a, do_b, do_c, *g_merge), landed = _merge_bwd(dpre1, zd, pa, pb, pc, o_a, o_b, o_c, merged, wbr, wo, name="merge_bwd",
                                                           exchanges=to_sibling(g_ffn))
    sums_ffn = chip_sums(ffn, g_ffn, landed[0]) if distributed else []
    merge = BRANCH_WEIGHTS + ("w_out",)
    g_merge = [g.reshape(N_SHARDS, D_MODEL // N_SHARDS, D_MODEL) for g in g_merge]
    (dza, hg_stats), landed = _hgrn_bwd(za, o_raw, do_a, states, lb_logits, gain, name="hgrn_bwd",
                                        exchanges=to_chips(sums_ffn) + to_sibling(g_merge))
    halves = shard_sums(ffn, sums_ffn, landed[0]) if distributed else {}
    sums_merge = chip_sums(merge, g_merge, landed[1]) if distributed else []
    (dzb, dbias_t, dsinks), landed = _swa_bwd(zb, do_b, bias_tables, sink_lanes, name="swa_bwd", exchanges=to_chips(sums_merge))
    if distributed:
        halves.update(shard_sums(merge, sums_merge, landed[0]))
    dbias = dbias_t.reshape(2 * SWA_BLOCK, SWA_HEADS, SWA_BLOCK).transpose(1, 2, 0).reshape(SWA_HEADS, -1)
    d_rel_bias = _bias_grad(dbias, onehot, name="bias_grad").T
    dzc, dmkv, g_wi = _mem_bwd(xb, zc, do_c, mkv, name="mem_bwd")

    proj = ("w_in", "w_mem_kv")
    for dz, offset, nm in ((dza, 0, "grad_w_in_a"), (dzb, W_A, "grad_w_in_b"), (dzd, W_A + W_B + W_C, "grad_w_in_d")):
        g_wi = _mm(dz, xb, mode="tn", tm=dz.shape[1] if dz.shape[1] <= 1280 else 1024, tn=D_MODEL, tk=tk, name=nm,
                   rows_of=IN_COLS, row_offset=offset, into=g_wi)
    g_proj = [g_wi.reshape(N_SHARDS, IN_COLS // N_SHARDS, D_MODEL),
              _mm(memb, dmkv, mode="tn", tm=D_MODEL, tn=512, tk=MEM_LEN, name="grad_w_mem_kv", out_panels=True)]
    sums_proj = chip_sums(proj, g_proj, _run_exchanges(to_sibling(g_proj), name="reduce_sibling_proj")[0]) if distributed else []
    small = dict(lb_logits=hg_stats[1:3], hg_norm_gain=hg_stats[0:1], swa_sinks=dsinks, rel_bias=d_rel_bias,
                 ln1_g=ln_stats[2:3], ln1_b=ln_stats[3:4], ln2_g=ln_stats[0:1], ln2_b=ln_stats[1:2], sq_err=ln_stats[4:5])
    dx_tm = min(512, s)
    small_exchange = [_small_gather_exchange(_pack_small(small, name="pack_small"))] if distributed else []
    grad_x, landed = _dx_matmul([dza, dzb, dzc, dzd], wi_t, dpre1, tm=dx_tm, name="grad_x", tiles=(0, s // dx_tm),
                                exchanges=to_chips(sums_proj) + small_exchange)
    if distributed:
        halves.update(shard_sums(proj, sums_proj, landed[0]))
        small = landed[1][0]
    else:
        halves = dict(zip(ffn + merge + proj, g_ffn + g_merge + g_proj))
    return grad_x, halves, small


def _mesh_position():
    x, y, c = lax.axis_index("x"), lax.axis_index("y"), lax.axis_index("c")
    chips = [(1 - x, y), (x, 1 - y), (1 - x, 1 - y)]
    return x, y, c, chips


class _Exchange(NamedTuple):
    operands: list
    out_shapes: list
    n_sems: int
    start: Callable
    finish: Callable
    halfway: Optional[Callable] = None


def _gather_exchange(shards):
    n = len(shards)
    per = 9
    assert all(w.shape[0] % (4 * BF16_SUBLANES) == 0 for w in shards)

    def plan(ins, outs, send_sems, recv_sems):
        x, y, c, (x_nbr, y_nbr, diag) = _mesh_position()
        sibling = (x, y, 1 - c)
        slot = lambda chip: 2 * chip[0] + chip[1]

        def rows(a, chip, hc, quarter=None):
            rh = shards[a].shape[0] // 2
            if quarter is None:
                return outs[a].at[slot(chip), pl.ds(hc * rh, rh), :]
            return outs[a].at[slot(chip), pl.ds(hc * rh + quarter * (rh // 2), rh // 2), :]

        def copy(a, k, src, dst, to):
            return pltpu.make_async_remote_copy(src_ref=src, dst_ref=dst, send_sem=send_sems.at[a * per + k], recv_sem=recv_sems.at[a * per + k],
                                                device_id=to, device_id_type=MESH)

        first, from_sibling = [], []
        landed, then = [[] for _ in range(4)], [[] for _ in range(4)]
        for a in range(n):
            rh = shards[a].shape[0] // 2
            my_half = ins[a].at[pl.ds(c * rh, rh), :]
            first += [copy(a, 4, ins[a], outs[a].at[slot((x, y))], sibling),
                      copy(a, 0, my_half, rows(a, (x, y), c), (*x_nbr, c)), copy(a, 1, my_half, rows(a, (x, y), c), (*y_nbr, c))]
            landed[0].append(copy(a, 0, rows(a, x_nbr, c), rows(a, x_nbr, c), (*x_nbr, c)))
            then[0].append([copy(a, 2, rows(a, x_nbr, c, 0), rows(a, x_nbr, c, 0), (*y_nbr, c)), copy(a, 5, rows(a, x_nbr, c), rows(a, x_nbr, c), sibling)])
            landed[1].append(copy(a, 1, rows(a, y_nbr, c), rows(a, y_nbr, c), (*y_nbr, c)))
            then[1].append([copy(a, 3, rows(a, y_nbr, c, 1), rows(a, y_nbr, c, 1), (*x_nbr, c)), copy(a, 6, rows(a, y_nbr, c), rows(a, y_nbr, c), sibling)])
            landed[2].append(copy(a, 2, rows(a, diag, c, 0), rows(a, diag, c, 0), (*y_nbr, c)))
            then[2].append([copy(a, 7, rows(a, diag, c, 0), rows(a, diag, c, 0), sibling)])
            landed[3].append(copy(a, 3, rows(a, diag, c, 1), rows(a, diag, c, 1), (*x_nbr, c)))
            then[3].append([copy(a, 8, rows(a, diag, c, 1), rows(a, diag, c, 1), sibling)])
            from_sibling += [copy(a, 4, outs[a].at[slot((x, y))], outs[a].at[slot((x, y))], sibling),
                             copy(a, 5, rows(a, x_nbr, 1 - c), rows(a, x_nbr, 1 - c), sibling), copy(a, 6, rows(a, y_nbr, 1 - c), rows(a, y_nbr, 1 - c), sibling),
                             copy(a, 7, rows(a, diag, 1 - c, 0), rows(a, diag, 1 - c, 0), sibling), copy(a, 8, rows(a, diag, 1 - c, 1), rows(a, diag, 1 - c, 1), sibling)]
        return first, landed, then, from_sibling

    def start(*refs):
        first, _, _, _ = plan(*refs)
        for cp in first:
            cp.start()

    def stages(landed, then, which):
        for stage in which:
            for arrival, onward in zip(landed[stage], then[stage]):
                arrival.wait_recv()
                for cp in onward:
                    cp.start()

    def halfway(*refs):
        _, landed, then, _ = plan(*refs)
        stages(landed, then, (0, 1))

    def finish(*refs):
        first, landed, then, from_sibling = plan(*refs)
        stages(landed, then, (2, 3))
        for cp in from_sibling:
            cp.wait_recv()
        for cp in first + [cp for stage in then for onward in stage for cp in onward]:
            cp.wait_send()

    return _Exchange(list(shards), [jax.ShapeDtypeStruct((N_SHARDS,) + w.shape, w.dtype) for w in shards], per * n, start, finish, halfway)


def _sibling_halves_exchange(grads):
    n = len(grads)

    def plan(ins, outs, send_sems, recv_sems):
        x, y, c, _ = _mesh_position()
        return [pltpu.make_async_remote_copy(src_ref=ins[a].at[:, pl.ds((1 - c) * (grads[a].shape[1] // 2), grads[a].shape[1] // 2), :],
                                             dst_ref=outs[a], send_sem=send_sems.at[a], recv_sem=recv_sems.at[a],
                                             device_id=(x, y, 1 - c), device_id_type=MESH) for a in range(n)]

    def start(*refs):
        for cp in plan(*refs):
            cp.start()

    def finish(*refs):
        for cp in plan(*refs):
            cp.wait()

    return _Exchange(list(grads), [jax.ShapeDtypeStruct((g.shape[0], g.shape[1] // 2, g.shape[2]), g.dtype) for g in grads], n, start, finish)


def _chip_partials_exchange(sums):
    n = len(sums)

    def plan(ins, outs, send_sems, recv_sems):
        _, _, c, chips = _mesh_position()
        return [pltpu.make_async_remote_copy(src_ref=ins[a].at[2 * cx + cy], dst_ref=outs[a].at[k], send_sem=send_sems.at[a * 3 + k],
                                             recv_sem=recv_sems.at[a * 3 + k], device_id=(cx, cy, c), device_id_type=MESH)
                for k, (cx, cy) in enumerate(chips) for a in range(n)]

    def start(*refs):
        for cp in plan(*refs):
            cp.start()

    def finish(*refs):
        for cp in plan(*refs):
            cp.wait()

    return _Exchange(list(sums), [jax.ShapeDtypeStruct((3,) + g.shape[1:], g.dtype) for g in sums], 3 * n, start, finish)


def _fused_call(body, *, name, grid, in_specs, out_specs, out_shape, scratch_shapes, operands, exchanges=(), aliases=None):
    single = not isinstance(out_shape, (tuple, list))
    out_specs = [out_specs] if single else list(out_specs)
    out_shape = [out_shape] if single else list(out_shape)
    n_in, n_out, n_scr = len(in_specs), len(out_specs), len(scratch_shapes)
    x_in = [len(e.operands) for e in exchanges]
    x_out = [len(e.out_shapes) for e in exchanges]

    def wrapped(*refs):
        refs = list(refs)
        ins = refs[:n_in]
        pos = n_in
        ex_ins = []
        for k in x_in:
            ex_ins.append(refs[pos:pos + k])
            pos += k
        outs = refs[pos:pos + n_out]
        pos += n_out
        ex_outs = []
        for k in x_out:
            ex_outs.append(refs[pos:pos + k])
            pos += k
        scratch = refs[pos:pos + n_scr]
        sems = refs[pos + n_scr:]
        first, last, middle = None, None, None
        for axis, size in enumerate(grid):
            at_start, at_end, at_middle = pl.program_id(axis) == 0, pl.program_id(axis) == size - 1, pl.program_id(axis) == size // 2
            first = at_start if first is None else first & at_start
            last = at_end if last is None else last & at_end
            middle = at_middle if middle is None else middle & at_middle

        @pl.when(first)
        def _():
            for i, e in enumerate(exchanges):
                e.start(ex_ins[i], ex_outs[i], sems[2 * i], sems[2 * i + 1])

        if any(e.halfway for e in exchanges):
            @pl.when(middle)
            def _():
                for i, e in enumerate(exchanges):
                    if e.halfway:
                        e.halfway(ex_ins[i], ex_outs[i], sems[2 * i], sems[2 * i + 1])

        body(*ins, *outs, *scratch)

        @pl.when(last)
        def _():
            for i, e in enumerate(exchanges):
                e.finish(ex_ins[i], ex_outs[i], sems[2 * i], sems[2 * i + 1])

    n_x_in, n_x_out = sum(x_in), sum(x_out)
    results = pl.pallas_call(
        wrapped if exchanges else body, name=name, grid=grid,
        in_specs=list(in_specs) + [HBM] * n_x_in,
        out_specs=out_specs + [HBM] * n_x_out,
        out_shape=out_shape + [s for e in exchanges for s in e.out_shapes],
        scratch_shapes=list(scratch_shapes) + [pltpu.SemaphoreType.DMA((e.n_sems,)) for e in exchanges for _ in range(2)],
        input_output_aliases=aliases or {}, compiler_params=_params(("arbitrary",) * len(grid)),
    )(*operands, *[a for e in exchanges for a in e.operands])
    own = results[0] if single else tuple(results[:n_out])
    landed, pos = [], n_out
    for k in x_out:
        landed.append(list(results[pos:pos + k]))
        pos += k
    return own, landed


def _run_exchanges(exchanges, *, name):
    def body(*refs):
        n_in = sum(len(e.operands) for e in exchanges)
        n_out = sum(len(e.out_shapes) for e in exchanges)
        ins, outs, sems = refs[:n_in], refs[n_in:n_in + n_out], refs[n_in + n_out:]
        spans, i, o = [], 0, 0
        for e in exchanges:
            spans.append((ins[i:i + len(e.operands)], outs[o:o + len(e.out_shapes)]))
            i, o = i + len(e.operands), o + len(e.out_shapes)
        for k, e in enumerate(exchanges):
            e.start(*spans[k], sems[2 * k], sems[2 * k + 1])
        for k, e in enumerate(exchanges):
            if e.halfway:
                e.halfway(*spans[k], sems[2 * k], sems[2 * k + 1])
        for k, e in enumerate(exchanges):
            e.finish(*spans[k], sems[2 * k], sems[2 * k + 1])

    operands = [a for e in exchanges for a in e.operands]
    shapes = [s for e in exchanges for s in e.out_shapes]
    results = pl.pallas_call(
        body, name=name, out_shape=shapes, in_specs=[HBM] * len(operands), out_specs=[HBM] * len(shapes),
        scratch_shapes=[pltpu.SemaphoreType.DMA((e.n_sems,)) for e in exchanges for _ in range(2)],
    )(*operands)
    landed, pos = [], 0
    for e in exchanges:
        landed.append(list(results[pos:pos + len(e.out_shapes)]))
        pos += len(e.out_shapes)
    return landed


ROW_TILE_MAX = 640
BF16_SUBLANES = 16


def _row_tile(rows):
    for tr in range(min(rows, ROW_TILE_MAX), 0, -1):
        if rows % tr == 0 and tr % BF16_SUBLANES == 0:
            return tr
    raise ValueError(rows)


def _add_sibling(grad, other, pos, *, name):
    p, r, cols = grad.shape
    rh = r // 2
    tr = _row_tile(rh)
    nb = rh // tr

    def body(pos_ref, g_ref, o_ref, sb_ref, mine_ref):
        total = g_ref[...] + o_ref[...]
        sb_ref[...] = total.astype(BF16)

        @pl.when(pl.program_id(1) == pos_ref[0])
        def _():
            mine_ref[...] = total

    return pl.pallas_call(
        body, name=name, out_shape=(jax.ShapeDtypeStruct((p, rh, cols), BF16), jax.ShapeDtypeStruct((rh, cols), F32)),
        grid_spec=pltpu.PrefetchScalarGridSpec(
            num_scalar_prefetch=1, grid=(nb, p),
            in_specs=[pl.BlockSpec((None, tr, cols), lambda i, j, pos_ref: (j, pos_ref[1] * nb + i, 0)),
                      pl.BlockSpec((None, tr, cols), lambda i, j, pos_ref: (j, i, 0))],
            out_specs=(pl.BlockSpec((None, tr, cols), lambda i, j, pos_ref: (j, i, 0)),
                       pl.BlockSpec((tr, cols), lambda i, j, pos_ref: (i, 0)))),
        compiler_params=_params(("parallel", "arbitrary")),
    )(pos, grad, other)


def _add_chips(mine, others, pos, *, name):
    rh, cols = mine.shape
    tr = _row_tile(rh)
    nb = rh // tr

    def body(pos_ref, s_ref, o_ref, r_ref):
        r_ref[...] = ((s_ref[...] + o_ref[0].astype(F32)) + o_ref[1].astype(F32)) + o_ref[2].astype(F32)

    return pl.pallas_call(
        body, name=name, out_shape=jax.ShapeDtypeStruct((2 * rh, cols), F32),
        grid_spec=pltpu.PrefetchScalarGridSpec(
            num_scalar_prefetch=1, grid=(nb,),
            in_specs=[pl.BlockSpec((tr, cols), lambda i, pos_ref: (i, 0)),
                      pl.BlockSpec((3, tr, cols), lambda i, pos_ref: (0, i, 0))],
            out_specs=pl.BlockSpec((tr, cols), lambda i, pos_ref: (pos_ref[1] * nb + i, 0))),
        compiler_params=_params(("parallel",)),
    )(pos, mine, others)


def _join_halves(bufs, *, name):
    n = len(bufs)

    def body(*refs):
        ins, outs = refs[:n], refs[n:2 * n]
        send_sems, recv_sems = refs[2 * n:]
        x, y, c, _ = _mesh_position()

        def copy(a, hc):
            rh = bufs[a].shape[0] // 2
            rows = pl.ds(hc * rh, rh)
            return pltpu.make_async_remote_copy(src_ref=ins[a].at[rows, :], dst_ref=outs[a].at[rows, :], send_sem=send_sems.at[a],
                                                recv_sem=recv_sems.at[a], device_id=(x, y, 1 - c), device_id_type=MESH)

        for a in range(n):
            copy(a, c).start()
        for a in range(n):
            copy(a, c).wait_send()
            copy(a, 1 - c).wait_recv()

    return pl.pallas_call(
        body, name=name, out_shape=[jax.ShapeDtypeStruct(b.shape, b.dtype) for b in bufs],
        in_specs=[HBM] * n, out_specs=[HBM] * n, input_output_aliases={a: a for a in range(n)},
        scratch_shapes=[pltpu.SemaphoreType.DMA((n,)), pltpu.SemaphoreType.DMA((n,))],
    )(*bufs)


SMALL = ["lb_logits", "hg_norm_gain", "swa_sinks", "rel_bias", "ln1_g", "ln1_b", "ln2_g", "ln2_b"]
PACK_ROWS = 48
PACK_AT = dict(lb_logits=(slice(0, 2), slice(0, D_MODEL)), hg_norm_gain=(slice(2, 3), slice(0, D_MODEL)), ln1_g=(slice(3, 4), slice(0, D_MODEL)),
               ln1_b=(slice(4, 5), slice(0, D_MODEL)), ln2_g=(slice(5, 6), slice(0, D_MODEL)), ln2_b=(slice(6, 7), slice(0, D_MODEL)),
               swa_sinks=(slice(7, 8), slice(0, SWA_HEADS)), sq_err=(slice(8, 9), slice(0, D_MODEL)),
               rel_bias=(slice(16, 16 + NUM_BUCKETS), slice(0, SWA_HEADS)))


def _pack_small(grads, *, name):
    names = SMALL + ["sq_err"]

    def body(*refs):
        packed = refs[len(names)]
        packed[...] = jnp.zeros_like(packed)
        for k, g_ref in zip(names, refs):
            packed[PACK_AT[k]] = g_ref[...]

    return pl.pallas_call(body, name=name, out_shape=jax.ShapeDtypeStruct((PACK_ROWS, D_MODEL), F32), compiler_params=_params(),
                          )(*[grads[k] for k in names])


def _small_gather_exchange(packed):
    def plan(ins, outs, send_sems, recv_sems):
        x, y, c, _ = _mesh_position()
        me = 4 * x + 2 * y + c
        own = pltpu.make_async_copy(ins[0], outs[0].at[me], send_sems.at[7])
        remote = []
        for d in range(1, 8):
            dx, dy, dc = (d >> 2) & 1, (d >> 1) & 1, d & 1
            remote.append(pltpu.make_async_remote_copy(src_ref=ins[0], dst_ref=outs[0].at[me], send_sem=send_sems.at[d - 1],
                                                       recv_sem=recv_sems.at[d - 1], device_id=(x ^ dx, y ^ dy, c ^ dc), device_id_type=MESH))
        return own, remote

    def start(*refs):
        own, remote = plan(*refs)
        own.start()
        for cp in remote:
            cp.start()

    def finish(*refs):
        own, remote = plan(*refs)
        for cp in remote:
            cp.wait()
        own.wait()

    return _Exchange([packed], [jax.ShapeDtypeStruct((8,) + packed.shape, packed.dtype)], 8, start, finish)


def _adamw_small(gathered, w, m, v, *, name):
    names = SMALL
    n = len(names)

    def body(*refs):
        gathered_ref = refs[0]
        w_refs, m_refs, v_refs = (dict(zip(names, refs[1 + i * n:1 + (i + 1) * n])) for i in range(3))
        loss_ref = refs[1 + 3 * n]
        go_refs, d_refs, nm_refs, nv_refs = (dict(zip(names, refs[2 + (3 + i) * n:2 + (4 + i) * n])) for i in range(4))
        total_ref = refs[2 + 7 * n]
        total = gathered_ref[0]
        for j in range(1, 8):
            total = total + gathered_ref[j]
        total_ref[...] = total
        loss_ref[...] = (0.5 / D_MODEL) * jnp.sum(total_ref[PACK_AT["sq_err"]], axis=1, keepdims=True)
        for k in names:
            g = total_ref[PACK_AT[k]]
            go_refs[k][...] = g
            d_refs[k][...], nm_refs[k][...], nv_refs[k][...] = _adamw_math(w_refs[k][...], g, m_refs[k][...], v_refs[k][...])

    like = [jax.ShapeDtypeStruct(w[k].shape, F32) for k in names]
    results = pl.pallas_call(body, name=name, out_shape=[jax.ShapeDtypeStruct((1, 1), F32)] + like * 4,
                             scratch_shapes=[pltpu.VMEM((PACK_ROWS, D_MODEL), F32)],
                             compiler_params=_params())(gathered, *[d[k] for d in (w, m, v) for k in names])
    return results[0], {k: tuple(results[1 + i * n + j] for i in range(4)) for j, k in enumerate(names)}


def _adamw_math(w, g, m, v):
    m = ADAM_B1 * m + (1.0 - ADAM_B1) * g
    v = ADAM_B2 * v + (1.0 - ADAM_B2) * (g * g)
    m_hat = m / (1.0 - ADAM_B1 ** ADAM_STEP)
    v_hat = v / (1.0 - ADAM_B2 ** ADAM_STEP)
    delta = -ADAM_LR * (m_hat / (jnp.sqrt(v_hat) + ADAM_EPS) + ADAM_WD * w)
    return delta, m, v


def _adamw(w, g, m, v, *, name):
    _, rows, cols = w.shape
    tr = _row_tile(rows)
    blk = pl.BlockSpec((None, tr, cols), lambda i: (0, i, 0))
    flat = pl.BlockSpec((tr, cols), lambda i: (i, 0))

    def body(w_ref, g_ref, m_ref, v_ref, go_ref, d_ref, nm_ref, nv_ref):
        g_v = g_ref[...]
        go_ref[...] = g_v
        d_ref[...], nm_ref[...], nv_ref[...] = _adamw_math(w_ref[...], g_v, m_ref[...], v_ref[...])

    shape = jax.ShapeDtypeStruct((1, rows, cols), F32)
    return pl.pallas_call(body, name=name, grid=(rows // tr,), out_shape=(shape,) * 4, in_specs=[blk, flat, blk, blk], out_specs=(blk,) * 4,
                          compiler_params=_params(("parallel",)))(w, g, m, v)


WEIGHTS = ["w_in", "lb_logits", "hg_norm_gain", "swa_sinks", "rel_bias", "w_mem_kv", "w_branch_hg", "w_branch_swa", "w_branch_mem",
           "w_out", "ln1_g", "ln1_b", "w_up", "w_down", "ln2_g", "ln2_b"]
BIG = ["w_in", "w_mem_kv", "w_branch_hg", "w_branch_swa", "w_branch_mem", "w_out", "w_up", "w_down"]


def kernel(x, mem, w_in, lb_logits, hg_norm_gain, swa_sinks, rel_bias, w_mem_kv, w_branch_hg, w_branch_swa, w_branch_mem, w_out, ln1_g, ln1_b, w_up, w_down, ln2_g, ln2_b, loss_target, m_w_in, m_lb_logits, m_hg_norm_gain, m_swa_sinks, m_rel_bias, m_w_mem_kv, m_w_branch_hg, m_w_branch_swa, m_w_branch_mem, m_w_out, m_ln1_g, m_ln1_b, m_w_up, m_w_down, m_ln2_g, m_ln2_b, v_w_in, v_lb_logits, v_hg_norm_gain, v_swa_sinks, v_rel_bias, v_w_mem_kv, v_w_branch_hg, v_w_branch_swa, v_w_branch_mem, v_w_out, v_ln1_g, v_ln1_b, v_w_up, v_w_down, v_ln2_g, v_ln2_b):
    w = dict(w_in=w_in, lb_logits=lb_logits, hg_norm_gain=hg_norm_gain, swa_sinks=swa_sinks, rel_bias=rel_bias, w_mem_kv=w_mem_kv,
             w_branch_hg=w_branch_hg, w_branch_swa=w_branch_swa, w_branch_mem=w_branch_mem, w_out=w_out, ln1_g=ln1_g, ln1_b=ln1_b,
             w_up=w_up, w_down=w_down, ln2_g=ln2_g, ln2_b=ln2_b)
    m = dict(w_in=m_w_in, lb_logits=m_lb_logits, hg_norm_gain=m_hg_norm_gain, swa_sinks=m_swa_sinks, rel_bias=m_rel_bias, w_mem_kv=m_w_mem_kv,
             w_branch_hg=m_w_branch_hg, w_branch_swa=m_w_branch_swa, w_branch_mem=m_w_branch_mem, w_out=m_w_out, ln1_g=m_ln1_g, ln1_b=m_ln1_b,
             w_up=m_w_up, w_down=m_w_down, ln2_g=m_ln2_g, ln2_b=m_ln2_b)
    v = dict(w_in=v_w_in, lb_logits=v_lb_logits, hg_norm_gain=v_hg_norm_gain, swa_sinks=v_swa_sinks, rel_bias=v_rel_bias, w_mem_kv=v_w_mem_kv,
             w_branch_hg=v_w_branch_hg, w_branch_swa=v_w_branch_swa, w_branch_mem=v_w_branch_mem, w_out=v_w_out, ln1_g=v_ln1_g, ln1_b=v_ln1_b,
             w_up=v_w_up, w_down=v_w_down, ln2_g=v_ln2_g, ln2_b=v_ln2_b)
    shapes = {k: w[k].shape for k in WEIGHTS}
    for d in (w, m, v):
        d["w_in"] = d["w_in"].reshape(D_MODEL, IN_COLS // N_SHARDS).T[None]
    shards = {k: w[k].reshape(w[k].shape[-2], w[k].shape[-1]).astype(BF16) for k in BIG}
    wi4, wmkv = _run_exchanges([_gather_exchange([shards["w_in"], shards["w_mem_kv"]])], name="gather_weights")[0]
    wi_t = wi4.reshape(IN_COLS, D_MODEL)

    grad_x, halves, small = _local_step(
        x.reshape(x.shape[-2], D_MODEL), mem.reshape(MEM_LEN, D_MODEL), loss_target.reshape(loss_target.shape[-2], D_MODEL),
        wi_t, wmkv, shards, lb_logits, hg_norm_gain, swa_sinks, rel_bias, ln1_g, ln1_b, ln2_g, ln2_b, distributed=True)

    reduced = dict(zip(BIG, _join_halves([halves[k] for k in BIG], name="join_halves")))

    outs = {k: _adamw(w[k], reduced[k], m[k], v[k], name="adamw_" + k) for k in BIG}
    loss, small_outs = _adamw_small(small, w, m, v, name="adamw_small")
    outs.update(small_outs)
    grad_out, delta_out, m_out, v_out = ({k: outs[k][i] for k in WEIGHTS} for i in range(4))
    for out in (grad_out, delta_out, m_out, v_out):
        out["w_in"] = out["w_in"][0].T

    result = [loss.reshape(()), grad_x.reshape(x.shape)]
    for out in (grad_out, delta_out, m_out, v_out):
        result += [out[k].reshape(shapes[k]) for k in WEIGHTS]
    return tuple(result)
```

```python
import math
from typing import Callable, NamedTuple, Optional

import jax
import jax.numpy as jnp
from jax import lax
from jax.experimental import pallas as pl
from jax.experimental.pallas import tpu as pltpu

F32 = jnp.float32
BF16 = jnp.bfloat16
HIGHEST = lax.Precision.HIGHEST
MESH = pl.DeviceIdType.MESH

D_MODEL = 1024
MEM_LEN = 256
HG_HEADS = 8
HG_DK = 128
HG_CHUNK = 64
SWA_HEADS = 16
SWA_KV_HEADS = 2
SWA_GROUP = 8
SWA_HEAD_DIM = 64
SWA_BLOCK = 128
SWA_WINDOW = 128
MEM_HEADS = 4
MEM_HEAD_DIM = 256
NUM_BUCKETS = 32
MAX_DISTANCE = 128
D_FF = 4096
LN_EPS = 1e-5
RMS_EPS = 1e-6
ALPHA = 2.0 ** 0.25
W_A, W_B, W_C, W_D = 4096, 1280, 1024, 3072
IN_COLS = W_A + W_B + W_C + W_D
N_SHARDS = 4
ADAM_LR = 0.001
ADAM_B1 = 0.9
ADAM_B2 = 0.999
ADAM_EPS = 1e-08
ADAM_WD = 0.01
ADAM_STEP = 10
MASK_VALUE = -1e30
VMEM_LIMIT = 56 * 1024 * 1024

NN = ((1,), (0,))
NT = ((1,), (1,))
TN = ((0,), (0,))
HBM = pl.BlockSpec(memory_space=pltpu.HBM)


def _dot(a, b, dims=NN, precision=None):
    return lax.dot_general(a, b, (dims, ((), ())), precision=precision, preferred_element_type=F32)


def _params(sem=None):
    return pltpu.CompilerParams(dimension_semantics=sem, vmem_limit_bytes=VMEM_LIMIT)


def _resident(shape):
    zeros = (0,) * len(shape)
    return pl.BlockSpec(shape, lambda *_: zeros, pipeline_mode=pl.Buffered(1))


def _resident_rows(arr, offset, rows):
    return pl.BlockSpec((pl.Element(rows), pl.Element(arr.shape[1])), lambda *_: (offset, 0), pipeline_mode=pl.Buffered(1))


def _mm(a, b, *, mode, tm, tn, tk, name, out_dtype=F32, b_panels=False, b_rows=None, out_panels=False, rows_of=None, row_offset=0,
        into=None):
    if mode == "tn":
        kdim, m = a.shape
    else:
        m, kdim = a.shape
    if b_panels:
        n = b.shape[0] * b.shape[2]
        assert b.shape[2] == tn and mode == "nn"
    elif b_rows is not None:
        assert mode == "nt"
        b_offset, n = b_rows
    elif mode == "nt":
        n = b.shape[0]
    else:
        n = b.shape[1]
    assert m % tm == 0 and n % tn == 0 and kdim % tk == 0, (name, m, n, kdim)
    nk = kdim // tk
    dims = {"nn": NN, "nt": NT, "tn": TN}[mode]
    a_spec = pl.BlockSpec((tk, tm), lambda i, j, k: (k, i)) if mode == "tn" else pl.BlockSpec((tm, tk), lambda i, j, k: (i, k))
    if b_panels:
        b_spec = pl.BlockSpec((None, tk, tn), lambda i, j, k: (j, k, 0))
    elif b_rows is not None:
        assert b_offset % BF16_SUBLANES == 0 and tn % BF16_SUBLANES == 0 and tk % 128 == 0
        b_spec = pl.BlockSpec((pl.Element(tn), pl.Element(tk)),
                              lambda i, j, k: (pl.multiple_of(b_offset + j * tn, BF16_SUBLANES), pl.multiple_of(k * tk, 128)))
    elif mode == "nt":
        b_spec = pl.BlockSpec((tn, tk), lambda i, j, k: (j, k))
    else:
        b_spec = pl.BlockSpec((tk, tn), lambda i, j, k: (k, j))
    in_specs = [a_spec, b_spec]
    operands = [a, b]
    aliases = {}
    if out_panels:
        out_shape = jax.ShapeDtypeStruct((n // tn, m, tn), out_dtype)
        o_spec = pl.BlockSpec((None, tm, tn), lambda i, j, k: (j, i, 0))
    elif rows_of is not None:
        out_shape = jax.ShapeDtypeStruct((rows_of, n), out_dtype)
        assert row_offset % BF16_SUBLANES == 0 and tm % BF16_SUBLANES == 0 and tn % 128 == 0
        o_spec = pl.BlockSpec((pl.Element(tm), pl.Element(tn)),
                              lambda i, j, k: (pl.multiple_of(row_offset + i * tm, BF16_SUBLANES), pl.multiple_of(j * tn, 128)))
        if into is not None:
            in_specs.append(pl.BlockSpec(memory_space=pl.ANY))
            operands.append(into)
            aliases = {2: 0}
    else:
        out_shape = jax.ShapeDtypeStruct((m, n), out_dtype)
        o_spec = pl.BlockSpec((tm, tn), lambda i, j, k: (i, j))
    n_in = len(operands)

    def body(*refs):
        a_ref, b_ref, o_ref = refs[0], refs[1], refs[n_in]
        part = _dot(a_ref[...].astype(BF16), b_ref[...].astype(BF16), dims)

        def finish(acc):
            o_ref[...] = acc.astype(out_dtype)

        if nk == 1:
            finish(part)
        else:
            acc_ref = refs[-1]
            k = pl.program_id(2)

            @pl.when(k == 0)
            def _():
                acc_ref[...] = part

            @pl.when(k > 0)
            def _():
                acc_ref[...] += part

            @pl.when(k == nk - 1)
            def _():
                finish(acc_ref[...])

    return pl.pallas_call(
        body, name=name, out_shape=out_shape, grid=(m // tm, n // tn, nk), in_specs=in_specs, out_specs=o_spec,
        scratch_shapes=[pltpu.VMEM((tm, tn), F32)] if nk > 1 else [], input_output_aliases=aliases,
        compiler_params=_params(("parallel", "parallel", "arbitrary")),
    )(*operands)


def _dx_matmul(dzs, wi_t, resid, *, tm, name, tiles, into=None, exchanges=()):
    s = resid.shape[0]
    npieces = len(dzs)
    offsets = [sum(dz.shape[1] for dz in dzs[:p]) for p in range(npieces)]
    first, count = tiles
    tile = lambda i: (first + i, 0)
    in_specs = [pl.BlockSpec((tm, dz.shape[1]), tile) for dz in dzs] + [_resident(wi_t.shape), pl.BlockSpec((tm, D_MODEL), tile)]
    operands = [*dzs, wi_t, resid]
    if into is not None:
        in_specs.append(pl.BlockSpec(memory_space=pl.ANY))
        operands.append(into)
    n_in = len(operands)

    def body(*refs):
        dz_refs, w_ref, r_ref, o_ref = refs[:npieces], refs[npieces], refs[npieces + 1], refs[n_in]
        total = ALPHA * r_ref[...]
        for p in range(npieces):
            total = total + _dot(dz_refs[p][...], w_ref[offsets[p]:offsets[p] + dzs[p].shape[1], :], NN)
        o_ref[...] = total

    return _fused_call(
        body, name=name, out_shape=jax.ShapeDtypeStruct((s, D_MODEL), F32), grid=(count,), in_specs=in_specs,
        out_specs=pl.BlockSpec((tm, D_MODEL), tile), scratch_shapes=[], operands=operands, exchanges=exchanges,
        aliases={n_in - 1: 0} if into is not None else None)


def _lower_bound(lbl_ref):
    l0, l1 = lbl_ref[0:1, :], lbl_ref[1:2, :]
    mx = jnp.maximum(l0, l1)
    e0, e1 = jnp.exp(l0 - mx), jnp.exp(l1 - mx)
    return e0 / (e0 + e1)


HEAD_COLS = [slice(h * HG_DK, (h + 1) * HG_DK) for h in range(HG_HEADS)]


def _head_mean(x):
    return jnp.concatenate([jnp.broadcast_to(jnp.mean(x[:, c], axis=-1, keepdims=True), (x.shape[0], HG_DK)) for c in HEAD_COLS], axis=1)


def _triangle_sum(tri_b, x):
    p0 = x.astype(BF16)
    r1 = x - p0.astype(F32)
    p1 = r1.astype(BF16)
    p2 = (r1 - p1.astype(F32)).astype(BF16)
    return _dot(tri_b, p0) + _dot(tri_b, p1) + _dot(tri_b, p2)


def _chunk_forward(q, fl, v, lb, tril_b):
    sg = jax.nn.sigmoid(fl)
    f = lb + (1.0 - lb) * sg
    k = 1.0 - f
    b = _triangle_sum(tril_b, jnp.log(f))
    b_last = b[HG_CHUNK - 1:HG_CHUNK, :]
    eb, enb, eo = jnp.exp(b), jnp.exp(-b), jnp.exp(b_last - b)
    return sg, f, k, b_last, eb, enb, eo, q * eb, k * enb, k * eo


def _hgrn_fwd(xb, wi_t, lb_logits, gain, *, name, exchanges=()):
    s = xb.shape[0]
    t = min(256, s)
    ncs = t // HG_CHUNK

    def body(x_ref, w_ref, lbl_ref, gain_ref, z_ref, oa_ref, oraw_ref, st_ref, state):
        @pl.when(pl.program_id(0) == 0)
        def _():
            state[...] = jnp.zeros_like(state)

        z_ref[...] = _dot(x_ref[...], w_ref[...], NT)
        lb_all = _lower_bound(lbl_ref)
        row = lax.broadcasted_iota(jnp.int32, (HG_CHUNK, HG_CHUNK), 0)
        col = lax.broadcasted_iota(jnp.int32, (HG_CHUNK, HG_CHUNK), 1)
        tril = row >= col
        tril_b = tril.astype(BF16)
        gain_all = gain_ref[...]

        def chunk(i, carry):
            r = pl.ds(pl.multiple_of(i * HG_CHUNK, HG_CHUNK), HG_CHUNK)
            q, fl, v, hg = (z_ref[r, j * D_MODEL:(j + 1) * D_MODEL] for j in range(4))
            _, _, _, b_last, _, _, _, q_in, k_in, k_out = _chunk_forward(q, fl, v, lb_all, tril_b)
            q_in_b, k_in_b, k_out_b, vb = (u.astype(BF16) for u in (q_in, k_in, k_out, v))
            decay = jnp.exp(b_last)
            sts = [state[h] for h in range(HG_HEADS)]
            attn = [_dot(q_in_b[:, c], k_in_b[:, c], NT) for c in HEAD_COLS]
            inter = [_dot(q_in_b[:, c], sts[h].astype(BF16), NT) for h, c in enumerate(HEAD_COLS)]
            upd = [_dot(vb[:, c], k_out_b[:, c], TN) for c in HEAD_COLS]
            attn = [jnp.where(tril, a, 0.0).astype(BF16) for a in attn]
            outs = [_dot(attn[h], vb[:, c], NN) + inter[h] for h, c in enumerate(HEAD_COLS)]
            for h, c in enumerate(HEAD_COLS):
                st_ref[h, i] = sts[h]
                state[h] = sts[h] * decay[:, c] + upd[h]
            o = jnp.concatenate(outs, axis=1)
            oraw_ref[r, :] = o
            n = o * lax.rsqrt(_head_mean(o * o) + RMS_EPS)
            oa_ref[r, :] = (n * gain_all * (hg * jax.nn.sigmoid(hg))).astype(BF16)
            return carry

        lax.fori_loop(0, ncs, chunk, 0, unroll=True)

    tile = lambda i: (i, 0)
    return _fused_call(
        body, name=name, grid=(s // t,),
        out_shape=(jax.ShapeDtypeStruct((s, W_A), F32), jax.ShapeDtypeStruct((s, D_MODEL), BF16), jax.ShapeDtypeStruct((s, D_MODEL), F32),
                   jax.ShapeDtypeStruct((HG_HEADS, s // HG_CHUNK, HG_DK, HG_DK), F32)),
        in_specs=[pl.BlockSpec((t, D_MODEL), tile), _resident_rows(wi_t, 0, W_A), _resident((2, D_MODEL)), _resident((1, D_MODEL))],
        out_specs=(pl.BlockSpec((t, W_A), tile), pl.BlockSpec((t, D_MODEL), tile), pl.BlockSpec((t, D_MODEL), tile),
                   pl.BlockSpec((HG_HEADS, ncs, HG_DK, HG_DK), lambda i: (0, i, 0, 0))),
        scratch_shapes=[pltpu.VMEM((HG_HEADS, HG_DK, HG_DK), F32)],
        operands=[xb, wi_t, lb_logits, gain], exchanges=exchanges)


def _hgrn_bwd(za, oraw, do_a, states, lb_logits, gain, *, name, exchanges=()):
    s = za.shape[0]
    t = min(256, s)
    ncs = t // HG_CHUNK
    nt = s // t

    def body(z_ref, oraw_ref, do_ref, st_ref, lbl_ref, gain_ref, dz_ref, stats_ref, dstate):
        step = pl.program_id(0)

        @pl.when(step == 0)
        def _():
            dstate[...] = jnp.zeros_like(dstate)
            stats_ref[...] = jnp.zeros_like(stats_ref)

        lb_all = _lower_bound(lbl_ref)
        row = lax.broadcasted_iota(jnp.int32, (HG_CHUNK, HG_CHUNK), 0)
        col = lax.broadcasted_iota(jnp.int32, (HG_CHUNK, HG_CHUNK), 1)
        tril = row >= col
        tril_b = tril.astype(BF16)
        triu_b = (row <= col).astype(BF16)
        gain_all = gain_ref[...]

        def chunk(ii, carry):
            i = ncs - 1 - ii
            r = pl.ds(pl.multiple_of(i * HG_CHUNK, HG_CHUNK), HG_CHUNK)
            q, fl, v, hg = (z_ref[r, j * D_MODEL:(j + 1) * D_MODEL] for j in range(4))
            o = oraw_ref[r, :]
            doa = do_ref[r, :]
            rms = lax.rsqrt(_head_mean(o * o) + RMS_EPS)
            n = o * rms
            sgg = jax.nn.sigmoid(hg)
            silu = hg * sgg
            dhg = doa * n * gain_all * (sgg * (1.0 + hg * (1.0 - sgg)))
            dgain = jnp.sum(doa * n * silu, axis=0, keepdims=True)
            dn = doa * gain_all * silu
            do = rms * (dn - n * _head_mean(dn * n))
            sg, f, k, b_last, eb, enb, eo, q_in, k_in, k_out = _chunk_forward(q, fl, v, lb_all, tril_b)
            q_in_b, k_in_b, k_out_b, vb, dob = (u.astype(BF16) for u in (q_in, k_in, k_out, v, do))
            decay = jnp.exp(b_last)
            sts = [st_ref[h, i] for h in range(HG_HEADS)]
            dsts = [dstate[h] for h in range(HG_HEADS)]
            dsts_b = [d.astype(BF16) for d in dsts]
            heads = list(enumerate(HEAD_COLS))
            attn = [_dot(q_in_b[:, c], k_in_b[:, c], NT) for h, c in heads]
            dattn = [_dot(dob[:, c], vb[:, c], NT) for h, c in heads]
            dq_st = [_dot(dob[:, c], sts[h].astype(BF16), NN) for h, c in heads]
            dk_out = [_dot(vb[:, c], dsts_b[h], NN) for h, c in heads]
            dv_st = [_dot(k_out_b[:, c], dsts_b[h], NT) for h, c in heads]
            dst_o = [_dot(dob[:, c], q_in_b[:, c], TN) for h, c in heads]
            attn = [jnp.where(tril, a, 0.0).astype(BF16) for a in attn]
            dattn = [jnp.where(tril, a, 0.0).astype(BF16) for a in dattn]
            dq_in = jnp.concatenate([_dot(dattn[h], k_in_b[:, c], NN) + dq_st[h] for h, c in heads], axis=1)
            dk_in = jnp.concatenate([_dot(dattn[h], q_in_b[:, c], TN) for h, c in heads], axis=1)
            dv = jnp.concatenate([_dot(attn[h], dob[:, c], TN) + dv_st[h] for h, c in heads], axis=1)
            dk_out = jnp.concatenate(dk_out, axis=1)
            dst_st = jnp.concatenate([jnp.sum(dsts[h] * sts[h], axis=0, keepdims=True) for h in range(HG_HEADS)], axis=1)
            for h, c in heads:
                dstate[h] = dsts[h] * decay[:, c] + dst_o[h]
            db_last = decay * dst_st + jnp.sum(dk_out * k_out, axis=0, keepdims=True)
            db = dq_in * q_in - dk_in * k_in - dk_out * k_out
            dg = _triangle_sum(triu_b, db) + db_last
            dk = dk_in * enb + dk_out * eo
            df = dg / f - dk
            stats_ref[0:1, :] += dgain
            stats_ref[1:2, :] += jnp.sum(df * (1.0 - sg), axis=0, keepdims=True)
            dz_ref[r, 0:1024] = (dq_in * eb).astype(BF16)
            dz_ref[r, 1024:2048] = (df * (1.0 - lb_all) * sg * (1.0 - sg)).astype(BF16)
            dz_ref[r, 2048:3072] = dv.astype(BF16)
            dz_ref[r, 3072:4096] = dhg.astype(BF16)
            return carry

        lax.fori_loop(0, ncs, chunk, 0, unroll=True)

        @pl.when(step == nt - 1)
        def _():
            dl0 = stats_ref[1:2, :] * lb_all * (1.0 - lb_all)
            stats_ref[1:2, :] = dl0
            stats_ref[2:3, :] = -dl0

    rev = lambda i: (nt - 1 - i, 0)
    return _fused_call(
        body, name=name, grid=(nt,),
        out_shape=(jax.ShapeDtypeStruct((s, W_A), BF16), jax.ShapeDtypeStruct((8, D_MODEL), F32)),
        in_specs=[pl.BlockSpec((t, W_A), rev), pl.BlockSpec((t, D_MODEL), rev), pl.BlockSpec((t, D_MODEL), rev),
                  pl.BlockSpec((HG_HEADS, ncs, HG_DK, HG_DK), lambda i: (0, nt - 1 - i, 0, 0)),
                  _resident((2, D_MODEL)), _resident((1, D_MODEL))],
        out_specs=(pl.BlockSpec((t, W_A), rev), pl.BlockSpec((8, D_MODEL), lambda i: (0, 0))),
        scratch_shapes=[pltpu.VMEM((HG_HEADS, HG_DK, HG_DK), F32)],
        operands=[za, oraw, do_a, states, lb_logits, gain], exchanges=exchanges)


def _t5_bucket(n):
    max_exact = NUM_BUCKETS // 2
    nf = jnp.maximum(n, 1).astype(F32)
    large = max_exact + (jnp.log(nf / max_exact) / math.log(MAX_DISTANCE / max_exact) * (NUM_BUCKETS - max_exact)).astype(jnp.int32)
    large = jnp.minimum(large, NUM_BUCKETS - 1)
    return jnp.where(n < max_exact, n, large)


def _bias_selector():
    qi = jnp.arange(SWA_BLOCK)[:, None] + SWA_BLOCK
    kj = jnp.arange(2 * SWA_BLOCK)[None, :]
    dist = qi - kj
    band = ((dist >= 0) & (dist < SWA_WINDOW)).reshape(1, -1)
    bucket = _t5_bucket(jnp.clip(dist, 0, SWA_WINDOW - 1)).reshape(1, -1)
    onehot = ((bucket == jnp.arange(NUM_BUCKETS)[:, None]) & band).astype(F32)
    return onehot, jnp.where(band, 0.0, MASK_VALUE).astype(F32)


def _bias_table(rel_bias_t, onehot, maskrow, *, name):
    def body(rb_ref, oh_ref, mask_ref, o_ref):
        o_ref[...] = _dot(rb_ref[...], oh_ref[...], NN, HIGHEST) + mask_ref[...]

    return pl.pallas_call(body, name=name, out_shape=jax.ShapeDtypeStruct((SWA_HEADS, onehot.shape[1]), F32),
                          compiler_params=_params())(rel_bias_t, onehot, maskrow)


def _bias_grad(dbias2d, onehot, *, name):
    def body(db_ref, oh_ref, o_ref):
        o_ref[...] = _dot(db_ref[...], oh_ref[...], NT, HIGHEST)

    return pl.pallas_call(body, name=name, out_shape=jax.ShapeDtypeStruct((SWA_HEADS, NUM_BUCKETS), F32),
                          compiler_params=_params())(dbias2d, onehot)


def _swa_operands(zq_ref, kv_cur_ref, kv_prev_ref):
    q = (zq_ref[:, 0:1024] * (SWA_HEAD_DIM ** -0.5)).astype(BF16)
    kv_c = kv_cur_ref[...].astype(BF16)
    kv_p = kv_prev_ref[...].astype(BF16)
    kks = [jnp.concatenate([kv_p[:, g * 64:(g + 1) * 64], kv_c[:, g * 64:(g + 1) * 64]], axis=0) for g in range(SWA_KV_HEADS)]
    vvs = [jnp.concatenate([kv_p[:, 128 + g * 64:128 + (g + 1) * 64], kv_c[:, 128 + g * 64:128 + (g + 1) * 64]], axis=0)
           for g in range(SWA_KV_HEADS)]
    return q, kks, vvs


SWA_PART_HEADS = 8
SWA_PARTS = [(h0 // SWA_GROUP, h0) for h0 in range(0, SWA_HEADS, SWA_PART_HEADS)]


def _part_lanes(h0):
    return slice(h0 * SWA_BLOCK, (h0 + SWA_PART_HEADS) * SWA_BLOCK)


def _stack_heads(x, h0):
    return jnp.concatenate([x[:, h * SWA_HEAD_DIM:(h + 1) * SWA_HEAD_DIM] for h in range(h0, h0 + SWA_PART_HEADS)], axis=0)


def _heads_to_lanes(xt):
    pairs = []
    for j in range(0, xt.shape[1] // SWA_BLOCK, 2):
        two = jnp.concatenate([xt[:, j * SWA_BLOCK:(j + 1) * SWA_BLOCK], xt[:, (j + 1) * SWA_BLOCK:(j + 2) * SWA_BLOCK]], axis=0)
        pairs.append(two.T)
    return jnp.concatenate(pairs, axis=1)


def _swa_softmax(score_t, bias_ref, sink_ref, h0):
    sc = score_t + bias_ref[:, _part_lanes(h0)]
    sink = sink_ref[:, _part_lanes(h0)]
    m = jnp.maximum(jnp.max(sc, axis=0, keepdims=True), sink)
    e = jnp.exp(sc - m)
    e_sink = jnp.exp(sink - m)
    return e, 1.0 / (jnp.sum(e, axis=0, keepdims=True) + e_sink), e_sink


def _swa_tables(bias2d, sinks):
    bias_t = bias2d.reshape(SWA_HEADS, SWA_BLOCK, 2 * SWA_BLOCK).transpose(2, 0, 1).reshape(2 * SWA_BLOCK, SWA_HEADS * SWA_BLOCK)
    first = jnp.where(jnp.arange(2 * SWA_BLOCK)[:, None] < SWA_BLOCK, MASK_VALUE, bias_t)
    return jnp.stack([first, bias_t]), jnp.repeat(sinks, SWA_BLOCK, axis=1)


def _swa_fwd(zb, bias_tables, sink_lanes, *, name, exchanges=()):
    s = zb.shape[0]
    nb = s // SWA_BLOCK

    def body(zq_ref, kvc_ref, kvp_ref, bias_ref, sink_ref, o_ref):
        q, kks, vvs = _swa_operands(zq_ref, kvc_ref, kvp_ref)
        scores = [_dot(kks[g], _stack_heads(q, h0), NT) for g, h0 in SWA_PARTS]
        probs = []
        for score, (_, h0) in zip(scores, SWA_PARTS):
            e, inv, _ = _swa_softmax(score, bias_ref, sink_ref, h0)
            probs.append((e * inv).astype(BF16))
        outs = [_dot(vvs[g], p, TN) for p, (g, _) in zip(probs, SWA_PARTS)]
        o_ref[...] = jnp.concatenate([_heads_to_lanes(o) for o in outs], axis=1).astype(BF16)

    return _fused_call(
        body, name=name, grid=(nb,), out_shape=jax.ShapeDtypeStruct((s, D_MODEL), BF16),
        in_specs=[pl.BlockSpec((SWA_BLOCK, W_B), lambda n: (n, 0)),
                  pl.BlockSpec((SWA_BLOCK, 256), lambda n: (n, 4)),
                  pl.BlockSpec((SWA_BLOCK, 256), lambda n: (jnp.maximum(n - 1, 0), 4)),
                  pl.BlockSpec((None, 2 * SWA_BLOCK, SWA_HEADS * SWA_BLOCK), lambda n: (jnp.minimum(n, 1), 0, 0)),
                  _resident((1, SWA_HEADS * SWA_BLOCK))],
        out_specs=pl.BlockSpec((SWA_BLOCK, D_MODEL), lambda n: (n, 0)), scratch_shapes=[],
        operands=[zb, zb, zb, bias_tables, sink_lanes], exchanges=exchanges)


def _swa_bwd(zb, do_b, bias_tables, sink_lanes, *, name, exchanges=()):
    s = zb.shape[0]
    nb = s // SWA_BLOCK
    scale = SWA_HEAD_DIM ** -0.5

    def body(zq_ref, kvc_ref, kvp_ref, do_ref, bias_ref, sink_ref, dz_ref, dbias_ref, dsink_ref, carry, dsink_acc):
        step = pl.program_id(0)

        @pl.when(step == 0)
        def _():
            carry[...] = jnp.zeros_like(carry)
            dsink_acc[...] = jnp.zeros_like(dsink_acc)
            dbias_ref[...] = jnp.zeros_like(dbias_ref)

        q, kks, vvs = _swa_operands(zq_ref, kvc_ref, kvp_ref)
        do = do_ref[...].astype(BF16)
        parts = range(len(SWA_PARTS))
        q_rows = [_stack_heads(q, h0) for _, h0 in SWA_PARTS]
        do_rows = [_stack_heads(do, h0) for _, h0 in SWA_PARTS]
        scores = [_dot(kks[g], q_rows[i], NT) for i, (g, _) in enumerate(SWA_PARTS)]
        soft = [_swa_softmax(scores[i], bias_ref, sink_ref, h0) for i, (_, h0) in enumerate(SWA_PARTS)]
        dps = [_dot(vvs[g], do_rows[i], NT) for i, (g, _) in enumerate(SWA_PARTS)]
        ps, dss = [], []
        for i, (_, h0) in enumerate(SWA_PARTS):
            e, inv, e_sink = soft[i]
            p = e * inv
            delta = jnp.sum(p * dps[i], axis=0, keepdims=True)
            ds = p * (dps[i] - delta)
            dbias_ref[:, _part_lanes(h0)] += ds
            dsink_acc[:, _part_lanes(h0)] -= e_sink * inv * delta
            ps.append(p.astype(BF16))
            dss.append(ds.astype(BF16))
        dqs = [_dot(kks[g], dss[i], TN) * scale for i, (g, _) in enumerate(SWA_PARTS)]
        in_group = lambda xs, g, axis: jnp.concatenate([xs[i] for i in parts if SWA_PARTS[i][0] == g], axis=axis)
        dkks = [_dot(in_group(dss, g, 1), in_group(q_rows, g, 0), NN) for g in range(SWA_KV_HEADS)]
        dvvs = [_dot(in_group(ps, g, 1), in_group(do_rows, g, 0), NN) for g in range(SWA_KV_HEADS)]
        dkv = jnp.concatenate(dkks + dvvs, axis=1)
        dz_ref[:, 0:1024] = jnp.concatenate([_heads_to_lanes(dq) for dq in dqs], axis=1).astype(BF16)
        dz_ref[:, 1024:1280] = (dkv[SWA_BLOCK:, :] + carry[...]).astype(BF16)
        carry[...] = dkv[:SWA_BLOCK, :]

        @pl.when(step == nb - 1)
        def _():
            acc = dsink_acc[...]
            dsink_ref[...] = jnp.concatenate([jnp.sum(acc[:, h * SWA_BLOCK:(h + 1) * SWA_BLOCK], axis=1, keepdims=True)
                                              for h in range(SWA_HEADS)], axis=1)

    rev = lambda i: (nb - 1 - i, 0)
    table_shape = (2 * SWA_BLOCK, SWA_HEADS * SWA_BLOCK)
    return _fused_call(
        body, name=name, grid=(nb,),
        out_shape=(jax.ShapeDtypeStruct((s, W_B), BF16), jax.ShapeDtypeStruct(table_shape, F32), jax.ShapeDtypeStruct((1, SWA_HEADS), F32)),
        in_specs=[pl.BlockSpec((SWA_BLOCK, W_B), rev),
                  pl.BlockSpec((SWA_BLOCK, 256), lambda i: (nb - 1 - i, 4)),
                  pl.BlockSpec((SWA_BLOCK, 256), lambda i: (jnp.maximum(nb - 2 - i, 0), 4)),
                  pl.BlockSpec((SWA_BLOCK, D_MODEL), rev),
                  pl.BlockSpec((None,) + table_shape, lambda i: (jnp.minimum(nb - 1 - i, 1), 0, 0)),
                  _resident((1, SWA_HEADS * SWA_BLOCK))],
        out_specs=(pl.BlockSpec((SWA_BLOCK, W_B), rev), pl.BlockSpec(table_shape, lambda i: (0, 0)),
                   pl.BlockSpec((1, SWA_HEADS), lambda i: (0, 0))),
        scratch_shapes=[pltpu.VMEM((SWA_BLOCK, 256), F32), pltpu.VMEM((1, SWA_HEADS * SWA_BLOCK), F32)],
        operands=[zb, zb, zb, do_b, bias_tables, sink_lanes], exchanges=exchanges)


MEM_COLS = [slice(h * MEM_HEAD_DIM, (h + 1) * MEM_HEAD_DIM) for h in range(MEM_HEADS)]
MEM_VCOLS = [slice(D_MODEL + h * MEM_HEAD_DIM, D_MODEL + (h + 1) * MEM_HEAD_DIM) for h in range(MEM_HEADS)]


def _mem_probs(zc_ref, mkv_ref):
    qs = [(zc_ref[:, c] * (MEM_HEAD_DIM ** -0.5)).astype(BF16) for c in MEM_COLS]
    scores = [_dot(qs[h], mkv_ref[:, c], NT) for h, c in enumerate(MEM_COLS)]
    ps = []
    for sc in scores:
        e = jnp.exp(sc - jnp.max(sc, axis=-1, keepdims=True))
        ps.append(e / jnp.sum(e, axis=-1, keepdims=True))
    return qs, ps


def _mem_fwd(xb, wi_t, mkv, *, name):
    s = xb.shape[0]
    t = min(512, s)

    def body(x_ref, w_ref, mkv_ref, zc_ref, o_ref):
        zc_ref[...] = _dot(x_ref[...], w_ref[...], NT).astype(BF16)
        _, ps = _mem_probs(zc_ref, mkv_ref)
        ps = [p.astype(BF16) for p in ps]
        o_ref[...] = jnp.concatenate([_dot(ps[h], mkv_ref[:, vc], NN) for h, vc in enumerate(MEM_VCOLS)], axis=1).astype(BF16)

    row = pl.BlockSpec((t, D_MODEL), lambda i: (i, 0))
    return pl.pallas_call(
        body, name=name, grid=(s // t,), out_shape=(jax.ShapeDtypeStruct((s, D_MODEL), BF16),) * 2,
        in_specs=[row, _resident_rows(wi_t, W_A + W_B, W_C), _resident((MEM_LEN, 2 * D_MODEL))],
        out_specs=(row, row), compiler_params=_params(("parallel",)),
    )(xb, wi_t, mkv)


def _mem_bwd(xb, zc, do_c, mkv, *, name):
    s = zc.shape[0]
    t = min(512, s)
    nt = s // t

    def body(x_ref, zc_ref, do_ref, mkv_ref, dz_ref, dmkv_ref, gwi_ref, acc):
        @pl.when(pl.program_id(0) == 0)
        def _():
            dmkv_ref[...] = jnp.zeros_like(dmkv_ref)
            acc[...] = jnp.zeros_like(acc)

        heads = range(MEM_HEADS)
        qs, ps = _mem_probs(zc_ref, mkv_ref)
        dos = [do_ref[:, c].astype(BF16) for c in MEM_COLS]
        dps = [_dot(dos[h], mkv_ref[:, MEM_VCOLS[h]], NT) for h in heads]
        dss = [(ps[h] * (dps[h] - jnp.sum(ps[h] * dps[h], axis=-1, keepdims=True))).astype(BF16) for h in heads]
        ps = [p.astype(BF16) for p in ps]
        dz = jnp.concatenate([_dot(dss[h], mkv_ref[:, MEM_COLS[h]], NN) * (MEM_HEAD_DIM ** -0.5) for h in heads], axis=1).astype(BF16)
        dz_ref[...] = dz
        dmkv_ref[...] += jnp.concatenate([_dot(dss[h], qs[h], TN) for h in heads] + [_dot(ps[h], dos[h], TN) for h in heads], axis=1)
        acc[...] += _dot(dz, x_ref[...], TN)

        @pl.when(pl.program_id(0) == nt - 1)
        def _():
            pltpu.sync_copy(acc, gwi_ref.at[pl.ds(W_A + W_B, W_C), :])

    row = pl.BlockSpec((t, D_MODEL), lambda i: (i, 0))
    return pl.pallas_call(
        body, name=name, grid=(nt,),
        out_shape=(jax.ShapeDtypeStruct((s, D_MODEL), BF16), jax.ShapeDtypeStruct((MEM_LEN, 2 * D_MODEL), F32),
                   jax.ShapeDtypeStruct((IN_COLS, D_MODEL), F32)),
        in_specs=[row, row, row, _resident((MEM_LEN, 2 * D_MODEL))],
        out_specs=(row, pl.BlockSpec((MEM_LEN, 2 * D_MODEL), lambda i: (0, 0)), HBM),
        scratch_shapes=[pltpu.VMEM((W_C, D_MODEL), F32)],
        compiler_params=_params(("arbitrary",)),
    )(xb, zc, do_c, mkv)


def _normalize(pre):
    mu = jnp.mean(pre, axis=-1, keepdims=True)
    xc = pre - mu
    rstd = lax.rsqrt(jnp.mean(xc * xc, axis=-1, keepdims=True) + LN_EPS)
    return xc * rstd, rstd


def _layer_norm_bwd(dh, xhat, rstd, g):
    dxh = dh * g
    dpre = rstd * (dxh - jnp.mean(dxh, axis=-1, keepdims=True) - xhat * jnp.mean(dxh * xhat, axis=-1, keepdims=True))
    return dpre, jnp.sum(dh * xhat, axis=0, keepdims=True), jnp.sum(dh, axis=0, keepdims=True)


def _merge_fwd(o_a, o_b, o_c, x, wi_t, wbr, wo, *, name):
    s = x.shape[0]
    t = min(256, s)
    row = lambda w: pl.BlockSpec((t, w), lambda i: (i, 0))

    def body(oa_ref, ob_ref, oc_ref, x_ref, wg_ref, wa_ref, wb_ref, wc_ref, wo_ref, zd_ref, xhat_ref, rstd_ref, merged_ref, pa_ref, pb_ref, pc_ref):
        wbr_refs = (wa_ref, wb_ref, wc_ref)
        zd_ref[...] = _dot(x_ref[...].astype(BF16), wg_ref[...], NT)
        merged = jnp.zeros((t, D_MODEL), F32)
        for b, (o_ref, p_ref) in enumerate(((oa_ref, pa_ref), (ob_ref, pb_ref), (oc_ref, pc_ref))):
            p = _dot(o_ref[...], wbr_refs[b][...], NN)
            p_ref[...] = p.astype(BF16)
            merged = merged + jax.nn.sigmoid(zd_ref[:, b * D_MODEL:(b + 1) * D_MODEL]) * p
        merged_b = merged.astype(BF16)
        merged_ref[...] = merged_b
        xhat, rstd = _normalize(ALPHA * x_ref[...] + _dot(merged_b, wo_ref[...], NN))
        xhat_ref[...] = xhat
        rstd_ref[...] = rstd

    act = jax.ShapeDtypeStruct((s, D_MODEL), F32)
    return pl.pallas_call(
        body, name=name, grid=(s // t,),
        out_shape=(jax.ShapeDtypeStruct((s, W_D), F32), act, jax.ShapeDtypeStruct((s, 1), F32)) + (jax.ShapeDtypeStruct((s, D_MODEL), BF16),) * 4,
        in_specs=[row(D_MODEL)] * 4 + [_resident_rows(wi_t, W_A + W_B + W_C, W_D)] + [_resident((D_MODEL, D_MODEL))] * 4,
        out_specs=(row(W_D), row(D_MODEL), row(1), row(D_MODEL), row(D_MODEL), row(D_MODEL), row(D_MODEL)),
        compiler_params=_params(("parallel",)),
    )(o_a, o_b, o_c, x, wi_t, *wbr, wo)


def _merge_bwd(dpre1, zd, pa, pb, pc, o_a, o_b, o_c, merged, wbr, wo, *, name, exchanges=()):
    s = dpre1.shape[0]
    t = min(256, s)
    nt = s // t
    row = lambda w: pl.BlockSpec((t, w), lambda i: (i, 0))

    def body(dpre_ref, zd_ref, pa_ref, pb_ref, pc_ref, oa_ref, ob_ref, oc_ref, mg_ref, wa_ref, wb_ref, wc_ref, wo_ref,
             dzd_ref, doa_ref, dob_ref, doc_ref, gwa_ref, gwb_ref, gwc_ref, gwo_ref, acc):
        step = pl.program_id(0)

        @pl.when(step == 0)
        def _():
            acc[...] = jnp.zeros_like(acc)

        dpre_b = dpre_ref[...].astype(BF16)
        dmerged = _dot(dpre_b, wo_ref[...], NT)
        acc[3] += _dot(mg_ref[...], dpre_b, TN)
        branches = ((pa_ref, oa_ref, doa_ref), (pb_ref, ob_ref, dob_ref), (pc_ref, oc_ref, doc_ref))
        for b, (p_ref, o_ref, do_ref) in enumerate(branches):
            gate = jax.nn.sigmoid(zd_ref[:, b * D_MODEL:(b + 1) * D_MODEL])
            dzd_ref[:, b * D_MODEL:(b + 1) * D_MODEL] = (dmerged * p_ref[...] * gate * (1.0 - gate)).astype(BF16)
            dp = (dmerged * gate).astype(BF16)
            acc[b] += _dot(o_ref[...], dp, TN)
            do_ref[...] = _dot(dp, (wa_ref, wb_ref, wc_ref)[b][...], NT).astype(do_ref.dtype)

        @pl.when(step == nt - 1)
        def _():
            for b, gw_ref in enumerate((gwa_ref, gwb_ref, gwc_ref, gwo_ref)):
                pltpu.sync_copy(acc.at[b], gw_ref)

    act = jax.ShapeDtypeStruct((s, D_MODEL), F32)
    actb = jax.ShapeDtypeStruct((s, D_MODEL), BF16)
    gw = jax.ShapeDtypeStruct((D_MODEL, D_MODEL), F32)
    return _fused_call(
        body, name=name, grid=(nt,),
        out_shape=(jax.ShapeDtypeStruct((s, W_D), BF16), act, actb, actb, gw, gw, gw, gw),
        in_specs=[row(D_MODEL), row(W_D)] + [row(D_MODEL)] * 7 + [_resident((D_MODEL, D_MODEL))] * 4,
        out_specs=(row(W_D),) + (row(D_MODEL),) * 3 + (HBM,) * 4, scratch_shapes=[pltpu.VMEM((4, D_MODEL, D_MODEL), F32)],
        operands=[dpre1, zd, pa, pb, pc, o_a, o_b, o_c, merged, *wbr, wo], exchanges=exchanges)


def _mlp_loss(xhat1, rstd1, target, ln1_g, ln1_b, ln2_g, ln2_b, wu, wd, *, name):
    s = xhat1.shape[0]
    t = min(256, s)
    npan = wu.shape[0]
    row = lambda w: pl.BlockSpec((t, w), lambda i: (i, 0))
    vec = _resident((1, D_MODEL))

    def body(xhat_ref, rstd_ref, tgt_ref, g1_ref, b1_ref, g2_ref, b2_ref, wu_ref, wd_ref,
             dpre1_ref, dpre2_ref, h1_ref, a_ref, du_ref, stats_ref):
        @pl.when(pl.program_id(0) == 0)
        def _():
            stats_ref[...] = jnp.zeros_like(stats_ref)

        xhat1_v = xhat_ref[...]
        h1 = xhat1_v * g1_ref[...] + b1_ref[...]
        h1_b = h1.astype(BF16)
        h1_ref[...] = h1_b
        us = []
        ff = jnp.zeros((t, D_MODEL), F32)
        for j in range(npan):
            u = _dot(h1_b, wu_ref[j], NN)
            us.append(u)
            r = jnp.maximum(u, 0.0)
            a_b = (r * r).astype(BF16)
            a_ref[:, j * D_MODEL:(j + 1) * D_MODEL] = a_b
            ff = ff + _dot(a_b, wd_ref[j], NN)
        xhat2, rstd2 = _normalize(ALPHA * h1 + ff)
        err = xhat2 * g2_ref[...] + b2_ref[...] - tgt_ref[...]
        stats_ref[4:5, :] += jnp.sum(err * err, axis=0, keepdims=True)
        dpre2, dg2, db2 = _layer_norm_bwd(err * (1.0 / D_MODEL), xhat2, rstd2, g2_ref[...])
        stats_ref[0:1, :] += dg2
        stats_ref[1:2, :] += db2
        dpre2_b = dpre2.astype(BF16)
        dpre2_ref[...] = dpre2_b
        dh1 = ALPHA * dpre2
        for j in range(npan):
            du_b = (_dot(dpre2_b, wd_ref[j], NT) * (2.0 * jnp.maximum(us[j], 0.0))).astype(BF16)
            du_ref[:, j * D_MODEL:(j + 1) * D_MODEL] = du_b
            dh1 = dh1 + _dot(du_b, wu_ref[j], NT)
        dpre1, dg1, db1 = _layer_norm_bwd(dh1, xhat1_v, rstd_ref[...], g1_ref[...])
        stats_ref[2:3, :] += dg1
        stats_ref[3:4, :] += db1
        dpre1_ref[...] = dpre1

    actb = jax.ShapeDtypeStruct((s, D_MODEL), BF16)
    wide = jax.ShapeDtypeStruct((s, D_FF), BF16)
    return pl.pallas_call(
        body, name=name, grid=(s // t,),
        out_shape=(jax.ShapeDtypeStruct((s, D_MODEL), F32), actb, actb, wide, wide, jax.ShapeDtypeStruct((8, D_MODEL), F32)),
        in_specs=[row(D_MODEL), row(1), row(D_MODEL), vec, vec, vec, vec,
                  _resident((npan, D_MODEL, D_MODEL)), _resident((npan, D_MODEL, D_MODEL))],
        out_specs=(row(D_MODEL), row(D_MODEL), row(D_MODEL), row(D_FF), row(D_FF), pl.BlockSpec((8, D_MODEL), lambda i: (0, 0))),
        compiler_params=_params(("arbitrary",)),
    )(xhat1, rstd1, target, ln1_g, ln1_b, ln2_g, ln2_b, wu, wd)


BRANCH_WEIGHTS = ("w_branch_hg", "w_branch_swa", "w_branch_mem")


def _local_step(x, xb, mem, target, wi_t, wmkv, late, lb_logits, gain, sinks, rel_bias, ln1_g, ln1_b, ln2_g, ln2_b, *, distributed):
    s = x.shape[0]
    tm = min(1024, s)
    tk = min(2048, s)
    memb = mem.astype(BF16)
    if distributed:
        cx, cy, cc = lax.axis_index("x"), lax.axis_index("y"), lax.axis_index("c")
        pos = jnp.stack([2 * cx + cy, cc]).astype(jnp.int32)
    gather = (lambda names: [_gather_exchange([late[k] for k in names])]) if distributed else (lambda names: [])
    to_sibling = (lambda grads: [_sibling_halves_exchange(grads)]) if distributed else (lambda grads: [])
    to_chips = (lambda sums: [_chip_partials_exchange([bf for bf, _ in sums])]) if distributed else (lambda sums: [])

    def chip_sums(names, grads, from_sibling):
        return [_add_sibling(g, o, pos, name="add_sibling_" + k) for k, g, o in zip(names, grads, from_sibling)]

    def shard_sums(names, sums, from_chips):
        return {k: _add_chips(mine, o, pos, name="add_chips_" + k) for k, (_, mine), o in zip(names, sums, from_chips)}

    zb = _mm(xb, wi_t, mode="nt", tm=tm, tn=W_B, tk=D_MODEL, name="proj_b", out_dtype=BF16, b_rows=(W_A, W_B))
    mkv = _mm(memb, wmkv, mode="nn", tm=MEM_LEN, tn=512, tk=D_MODEL, name="mem_kv", out_dtype=BF16, b_panels=True)
    onehot, maskrow = _bias_selector()
    bias_tables, sink_lanes = _swa_tables(_bias_table(rel_bias.T, onehot, maskrow, name="bias_table"), sinks)
    (za, o_a, o_raw, states), landed = _hgrn_fwd(xb, wi_t, lb_logits, gain, name="hgrn_fwd", exchanges=gather(("w_up", "w_down")))
    wu, wd = landed[0] if distributed else (late["wu"], late["wd"])
    o_b, landed = _swa_fwd(zb, bias_tables, sink_lanes, name="swa_fwd", exchanges=gather(BRANCH_WEIGHTS + ("w_out",)))
    if distributed:
        wbr = [wb.reshape(D_MODEL, D_MODEL) for wb in landed[0][:3]]
        wo = landed[0][3].reshape(D_MODEL, D_MODEL)
    else:
        wbr, wo = [late["wbr"][b] for b in range(3)], late["wo"]
    zc, o_c = _mem_fwd(xb, wi_t, mkv, name="mem_fwd")
    zd, xhat1, rstd1, merged, pa, pb, pc = _merge_fwd(o_a, o_b, o_c, x, wi_t, wbr, wo, name="merge_fwd")

    dpre1, dpre2, h1, act, du, ln_stats = _mlp_loss(xhat1, rstd1, target, ln1_g, ln1_b, ln2_g, ln2_b, wu, wd, name="mlp_loss")
    ffn = ("w_down", "w_up")
    g_ffn = [_mm(act, dpre2, mode="tn", tm=1024, tn=D_MODEL, tk=tk, name="grad_w_down").reshape(N_SHARDS, D_FF // N_SHARDS, D_MODEL),
             _mm(h1, du, mode="tn", tm=D_MODEL, tn=1024, tk=tk, name="grad_w_up", out_panels=True)]

    (dzd, do_a, do_b, do_c, *g_merge), landed = _merge_bwd(dpre1, zd, pa, pb, pc, o_a, o_b, o_c, merged, wbr, wo, name="merge_bwd",
                                                           exchanges=to_sibling(g_ffn))
    sums_ffn = chip_sums(ffn, g_ffn, landed[0]) if distributed else []
    merge = BRANCH_WEIGHTS + ("w_out",)
    g_merge = [g.reshape(N_SHARDS, D_MODEL // N_SHARDS, D_MODEL) for g in g_merge]
    (dza, hg_stats), landed = _hgrn_bwd(za, o_raw, do_a, states, lb_logits, gain, name="hgrn_bwd",
                                        exchanges=to_chips(sums_ffn) + to_sibling(g_merge))
    halves = shard_sums(ffn, sums_ffn, landed[0]) if distributed else {}
    sums_merge = chip_sums(merge, g_merge, landed[1]) if distributed else []
    (dzb, dbias_t, dsinks), landed = _swa_bwd(zb, do_b, bias_tables, sink_lanes, name="swa_bwd", exchanges=to_chips(sums_merge))
    if distributed:
        halves.update(shard_sums(merge, sums_merge, landed[0]))
    dbias = dbias_t.reshape(2 * SWA_BLOCK, SWA_HEADS, SWA_BLOCK).transpose(1, 2, 0).reshape(SWA_HEADS, -1)
    d_rel_bias = _bias_grad(dbias, onehot, name="bias_grad").T
    dzc, dmkv, g_wi = _mem_bwd(xb, zc, do_c, mkv, name="mem_bwd")

    proj = ("w_in", "w_mem_kv")
    for dz, offset, nm in ((dza, 0, "grad_w_in_a"), (dzb, W_A, "grad_w_in_b"), (dzd, W_A + W_B + W_C, "grad_w_in_d")):
        g_wi = _mm(dz, xb, mode="tn", tm=dz.shape[1] if dz.shape[1] <= 1280 else 1024, tn=D_MODEL, tk=tk, name=nm,
                   rows_of=IN_COLS, row_offset=offset, into=g_wi)
    g_proj = [g_wi.reshape(N_SHARDS, IN_COLS // N_SHARDS, D_MODEL),
              _mm(memb, dmkv, mode="tn", tm=D_MODEL, tn=512, tk=MEM_LEN, name="grad_w_mem_kv", out_panels=True)]
    sums_proj = chip_sums(proj, g_proj, _run_exchanges(to_sibling(g_proj), name="reduce_sibling_proj")[0]) if distributed else []
    small = dict(lb_logits=hg_stats[1:3], hg_norm_gain=hg_stats[0:1], swa_sinks=dsinks, rel_bias=d_rel_bias,
                 ln1_g=ln_stats[2:3], ln1_b=ln_stats[3:4], ln2_g=ln_stats[0:1], ln2_b=ln_stats[1:2], sq_err=ln_stats[4:5])
    dx_tm = min(512, s)
    small_exchange = [_small_gather_exchange(_pack_small(small, name="pack_small"))] if distributed else []
    grad_x, landed = _dx_matmul([dza, dzb, dzc, dzd], wi_t, dpre1, tm=dx_tm, name="grad_x", tiles=(0, s // dx_tm),
                                exchanges=to_chips(sums_proj) + small_exchange)
    if distributed:
        halves.update(shard_sums(proj, sums_proj, landed[0]))
        small = landed[1][0]
    else:
        halves = dict(zip(ffn + merge + proj, g_ffn + g_merge + g_proj))
    return grad_x, halves, small


def _mesh_position():
    x, y, c = lax.axis_index("x"), lax.axis_index("y"), lax.axis_index("c")
    chips = [(1 - x, y), (x, 1 - y), (1 - x, 1 - y)]
    return x, y, c, chips


class _Exchange(NamedTuple):
    operands: list
    out_shapes: list
    n_sems: int
    start: Callable
    finish: Callable
    halfway: Optional[Callable] = None


def _gather_exchange(shards):
    n = len(shards)
    per = 9
    assert all(w.shape[0] % (4 * BF16_SUBLANES) == 0 for w in shards)

    def plan(ins, outs, send_sems, recv_sems):
        x, y, c, (x_nbr, y_nbr, diag) = _mesh_position()
        sibling = (x, y, 1 - c)
        slot = lambda chip: 2 * chip[0] + chip[1]

        def rows(a, chip, hc, quarter=None):
            rh = shards[a].shape[0] // 2
            if quarter is None:
                return outs[a].at[slot(chip), pl.ds(hc * rh, rh), :]
            return outs[a].at[slot(chip), pl.ds(hc * rh + quarter * (rh // 2), rh // 2), :]

        def copy(a, k, src, dst, to):
            return pltpu.make_async_remote_copy(src_ref=src, dst_ref=dst, send_sem=send_sems.at[a * per + k], recv_sem=recv_sems.at[a * per + k],
                                                device_id=to, device_id_type=MESH)

        first, from_sibling = [], []
        landed, then = [[] for _ in range(4)], [[] for _ in range(4)]
        for a in range(n):
            rh = shards[a].shape[0] // 2
            my_half = ins[a].at[pl.ds(c * rh, rh), :]
            first += [copy(a, 4, ins[a], outs[a].at[slot((x, y))], sibling),
                      copy(a, 0, my_half, rows(a, (x, y), c), (*x_nbr, c)), copy(a, 1, my_half, rows(a, (x, y), c), (*y_nbr, c))]
            landed[0].append(copy(a, 0, rows(a, x_nbr, c), rows(a, x_nbr, c), (*x_nbr, c)))
            then[0].append([copy(a, 2, rows(a, x_nbr, c, 0), rows(a, x_nbr, c, 0), (*y_nbr, c)), copy(a, 5, rows(a, x_nbr, c), rows(a, x_nbr, c), sibling)])
            landed[1].append(copy(a, 1, rows(a, y_nbr, c), rows(a, y_nbr, c), (*y_nbr, c)))
            then[1].append([copy(a, 3, rows(a, y_nbr, c, 1), rows(a, y_nbr, c, 1), (*x_nbr, c)), copy(a, 6, rows(a, y_nbr, c), rows(a, y_nbr, c), sibling)])
            landed[2].append(copy(a, 2, rows(a, diag, c, 0), rows(a, diag, c, 0), (*y_nbr, c)))
            then[2].append([copy(a, 7, rows(a, diag, c, 0), rows(a, diag, c, 0), sibling)])
            landed[3].append(copy(a, 3, rows(a, diag, c, 1), rows(a, diag, c, 1), (*x_nbr, c)))
            then[3].append([copy(a, 8, rows(a, diag, c, 1), rows(a, diag, c, 1), sibling)])
            from_sibling += [copy(a, 4, outs[a].at[slot((x, y))], outs[a].at[slot((x, y))], sibling),
                             copy(a, 5, rows(a, x_nbr, 1 - c), rows(a, x_nbr, 1 - c), sibling), copy(a, 6, rows(a, y_nbr, 1 - c), rows(a, y_nbr, 1 - c), sibling),
                             copy(a, 7, rows(a, diag, 1 - c, 0), rows(a, diag, 1 - c, 0), sibling), copy(a, 8, rows(a, diag, 1 - c, 1), rows(a, diag, 1 - c, 1), sibling)]
        return first, landed, then, from_sibling

    def start(*refs):
        first, _, _, _ = plan(*refs)
        for cp in first:
            cp.start()

    def stages(landed, then, which):
        for stage in which:
            for arrival, onward in zip(landed[stage], then[stage]):
                arrival.wait_recv()
                for cp in onward:
                    cp.start()

    def halfway(*refs):
        _, landed, then, _ = plan(*refs)
        stages(landed, then, (0, 1))

    def finish(*refs):
        first, landed, then, from_sibling = plan(*refs)
        stages(landed, then, (2, 3))
        for cp in from_sibling:
            cp.wait_recv()
        for cp in first + [cp for stage in then for onward in stage for cp in onward]:
            cp.wait_send()

    return _Exchange(list(shards), [jax.ShapeDtypeStruct((N_SHARDS,) + w.shape, w.dtype) for w in shards], per * n, start, finish, halfway)


def _sibling_halves_exchange(grads):
    n = len(grads)

    def plan(ins, outs, send_sems, recv_sems):
        x, y, c, _ = _mesh_position()
        return [pltpu.make_async_remote_copy(src_ref=ins[a].at[:, pl.ds((1 - c) * (grads[a].shape[1] // 2), grads[a].shape[1] // 2), :],
                                             dst_ref=outs[a], send_sem=send_sems.at[a], recv_sem=recv_sems.at[a],
                                             device_id=(x, y, 1 - c), device_id_type=MESH) for a in range(n)]

    def start(*refs):
        for cp in plan(*refs):
            cp.start()

    def finish(*refs):
        for cp in plan(*refs):
            cp.wait()

    return _Exchange(list(grads), [jax.ShapeDtypeStruct((g.shape[0], g.shape[1] // 2, g.shape[2]), g.dtype) for g in grads], n, start, finish)


def _chip_partials_exchange(sums):
    n = len(sums)

    def plan(ins, outs, send_sems, recv_sems):
        _, _, c, chips = _mesh_position()
        return [pltpu.make_async_remote_copy(src_ref=ins[a].at[2 * cx + cy], dst_ref=outs[a].at[k], send_sem=send_sems.at[a * 3 + k],
                                             recv_sem=recv_sems.at[a * 3 + k], device_id=(cx, cy, c), device_id_type=MESH)
                for k, (cx, cy) in enumerate(chips) for a in range(n)]

    def start(*refs):
        for cp in plan(*refs):
            cp.start()

    def finish(*refs):
        for cp in plan(*refs):
            cp.wait()

    return _Exchange(list(sums), [jax.ShapeDtypeStruct((3,) + g.shape[1:], g.dtype) for g in sums], 3 * n, start, finish)


def _fused_call(body, *, name, grid, in_specs, out_specs, out_shape, scratch_shapes, operands, exchanges=(), aliases=None):
    single = not isinstance(out_shape, (tuple, list))
    out_specs = [out_specs] if single else list(out_specs)
    out_shape = [out_shape] if single else list(out_shape)
    n_in, n_out, n_scr = len(in_specs), len(out_specs), len(scratch_shapes)
    x_in = [len(e.operands) for e in exchanges]
    x_out = [len(e.out_shapes) for e in exchanges]

    def wrapped(*refs):
        refs = list(refs)
        ins = refs[:n_in]
        pos = n_in
        ex_ins = []
        for k in x_in:
            ex_ins.append(refs[pos:pos + k])
            pos += k
        outs = refs[pos:pos + n_out]
        pos += n_out
        ex_outs = []
        for k in x_out:
            ex_outs.append(refs[pos:pos + k])
            pos += k
        scratch = refs[pos:pos + n_scr]
        sems = refs[pos + n_scr:]
        first, last, middle = None, None, None
        for axis, size in enumerate(grid):
            at_start, at_end, at_middle = pl.program_id(axis) == 0, pl.program_id(axis) == size - 1, pl.program_id(axis) == size // 2
            first = at_start if first is None else first & at_start
            last = at_end if last is None else last & at_end
            middle = at_middle if middle is None else middle & at_middle

        @pl.when(first)
        def _():
            for i, e in enumerate(exchanges):
                e.start(ex_ins[i], ex_outs[i], sems[2 * i], sems[2 * i + 1])

        if any(e.halfway for e in exchanges):
            @pl.when(middle)
            def _():
                for i, e in enumerate(exchanges):
                    if e.halfway:
                        e.halfway(ex_ins[i], ex_outs[i], sems[2 * i], sems[2 * i + 1])

        body(*ins, *outs, *scratch)

        @pl.when(last)
        def _():
            for i, e in enumerate(exchanges):
                e.finish(ex_ins[i], ex_outs[i], sems[2 * i], sems[2 * i + 1])

    n_x_in, n_x_out = sum(x_in), sum(x_out)
    results = pl.pallas_call(
        wrapped if exchanges else body, name=name, grid=grid,
        in_specs=list(in_specs) + [HBM] * n_x_in,
        out_specs=out_specs + [HBM] * n_x_out,
        out_shape=out_shape + [s for e in exchanges for s in e.out_shapes],
        scratch_shapes=list(scratch_shapes) + [pltpu.SemaphoreType.DMA((e.n_sems,)) for e in exchanges for _ in range(2)],
        input_output_aliases=aliases or {}, compiler_params=_params(("arbitrary",) * len(grid)),
    )(*operands, *[a for e in exchanges for a in e.operands])
    own = results[0] if single else tuple(results[:n_out])
    landed, pos = [], n_out
    for k in x_out:
        landed.append(list(results[pos:pos + k]))
        pos += k
    return own, landed


def _cast_bf16(x, *, name, exchanges=()):
    s, cols = x.shape
    t = min(512, s)

    def body(x_ref, o_ref):
        o_ref[...] = x_ref[...].astype(BF16)

    tile = pl.BlockSpec((t, cols), lambda i: (i, 0))
    return _fused_call(body, name=name, grid=(s // t,), in_specs=[tile], out_specs=tile, out_shape=jax.ShapeDtypeStruct((s, cols), BF16),
                       scratch_shapes=[], operands=[x], exchanges=exchanges)


def _run_exchanges(exchanges, *, name):
    def body(*refs):
        n_in = sum(len(e.operands) for e in exchanges)
        n_out = sum(len(e.out_shapes) for e in exchanges)
        ins, outs, sems = refs[:n_in], refs[n_in:n_in + n_out], refs[n_in + n_out:]
        spans, i, o = [], 0, 0
        for e in exchanges:
            spans.append((ins[i:i + len(e.operands)], outs[o:o + len(e.out_shapes)]))
            i, o = i + len(e.operands), o + len(e.out_shapes)
        for k, e in enumerate(exchanges):
            e.start(*spans[k], sems[2 * k], sems[2 * k + 1])
        for k, e in enumerate(exchanges):
            if e.halfway:
                e.halfway(*spans[k], sems[2 * k], sems[2 * k + 1])
        for k, e in enumerate(exchanges):
            e.finish(*spans[k], sems[2 * k], sems[2 * k + 1])

    operands = [a for e in exchanges for a in e.operands]
    shapes = [s for e in exchanges for s in e.out_shapes]
    results = pl.pallas_call(
        body, name=name, out_shape=shapes, in_specs=[HBM] * len(operands), out_specs=[HBM] * len(shapes),
        scratch_shapes=[pltpu.SemaphoreType.DMA((e.n_sems,)) for e in exchanges for _ in range(2)],
    )(*operands)
    landed, pos = [], 0
    for e in exchanges:
        landed.append(list(results[pos:pos + len(e.out_shapes)]))
        pos += len(e.out_shapes)
    return landed


ROW_TILE_MAX = 640
BF16_SUBLANES = 16


def _row_tile(rows):
    for tr in range(min(rows, ROW_TILE_MAX), 0, -1):
        if rows % tr == 0 and tr % BF16_SUBLANES == 0:
            return tr
    raise ValueError(rows)


def _add_sibling(grad, other, pos, *, name):
    p, r, cols = grad.shape
    rh = r // 2
    tr = _row_tile(rh)
    nb = rh // tr

    def body(pos_ref, g_ref, o_ref, sb_ref, mine_ref):
        total = g_ref[...] + o_ref[...]
        sb_ref[...] = total.astype(BF16)

        @pl.when(pl.program_id(1) == pos_ref[0])
        def _():
            mine_ref[...] = total

    return pl.pallas_call(
        body, name=name, out_shape=(jax.ShapeDtypeStruct((p, rh, cols), BF16), jax.ShapeDtypeStruct((rh, cols), F32)),
        grid_spec=pltpu.PrefetchScalarGridSpec(
            num_scalar_prefetch=1, grid=(nb, p),
            in_specs=[pl.BlockSpec((None, tr, cols), lambda i, j, pos_ref: (j, pos_ref[1] * nb + i, 0)),
                      pl.BlockSpec((None, tr, cols), lambda i, j, pos_ref: (j, i, 0))],
            out_specs=(pl.BlockSpec((None, tr, cols), lambda i, j, pos_ref: (j, i, 0)),
                       pl.BlockSpec((tr, cols), lambda i, j, pos_ref: (i, 0)))),
        compiler_params=_params(("parallel", "arbitrary")),
    )(pos, grad, other)


def _add_chips(mine, others, pos, *, name):
    rh, cols = mine.shape
    tr = _row_tile(rh)
    nb = rh // tr

    def body(pos_ref, s_ref, o_ref, r_ref):
        r_ref[...] = ((s_ref[...] + o_ref[0].astype(F32)) + o_ref[1].astype(F32)) + o_ref[2].astype(F32)

    return pl.pallas_call(
        body, name=name, out_shape=jax.ShapeDtypeStruct((2 * rh, cols), F32),
        grid_spec=pltpu.PrefetchScalarGridSpec(
            num_scalar_prefetch=1, grid=(nb,),
            in_specs=[pl.BlockSpec((tr, cols), lambda i, pos_ref: (i, 0)),
                      pl.BlockSpec((3, tr, cols), lambda i, pos_ref: (0, i, 0))],
            out_specs=pl.BlockSpec((tr, cols), lambda i, pos_ref: (pos_ref[1] * nb + i, 0))),
        compiler_params=_params(("parallel",)),
    )(pos, mine, others)


def _join_halves(bufs, *, name):
    n = len(bufs)

    def body(*refs):
        ins, outs = refs[:n], refs[n:2 * n]
        send_sems, recv_sems = refs[2 * n:]
        x, y, c, _ = _mesh_position()

        def copy(a, hc):
            rh = bufs[a].shape[0] // 2
            rows = pl.ds(hc * rh, rh)
            return pltpu.make_async_remote_copy(src_ref=ins[a].at[rows, :], dst_ref=outs[a].at[rows, :], send_sem=send_sems.at[a],
                                                recv_sem=recv_sems.at[a], device_id=(x, y, 1 - c), device_id_type=MESH)

        for a in range(n):
            copy(a, c).start()
        for a in range(n):
            copy(a, c).wait_send()
            copy(a, 1 - c).wait_recv()

    return pl.pallas_call(
        body, name=name, out_shape=[jax.ShapeDtypeStruct(b.shape, b.dtype) for b in bufs],
        in_specs=[HBM] * n, out_specs=[HBM] * n, input_output_aliases={a: a for a in range(n)},
        scratch_shapes=[pltpu.SemaphoreType.DMA((n,)), pltpu.SemaphoreType.DMA((n,))],
    )(*bufs)


SMALL = ["lb_logits", "hg_norm_gain", "swa_sinks", "rel_bias", "ln1_g", "ln1_b", "ln2_g", "ln2_b"]
PACK_ROWS = 48
PACK_AT = dict(lb_logits=(slice(0, 2), slice(0, D_MODEL)), hg_norm_gain=(slice(2, 3), slice(0, D_MODEL)), ln1_g=(slice(3, 4), slice(0, D_MODEL)),
               ln1_b=(slice(4, 5), slice(0, D_MODEL)), ln2_g=(slice(5, 6), slice(0, D_MODEL)), ln2_b=(slice(6, 7), slice(0, D_MODEL)),
               swa_sinks=(slice(7, 8), slice(0, SWA_HEADS)), sq_err=(slice(8, 9), slice(0, D_MODEL)),
               rel_bias=(slice(16, 16 + NUM_BUCKETS), slice(0, SWA_HEADS)))


def _pack_small(grads, *, name):
    names = SMALL + ["sq_err"]

    def body(*refs):
        packed = refs[len(names)]
        packed[...] = jnp.zeros_like(packed)
        for k, g_ref in zip(names, refs):
            packed[PACK_AT[k]] = g_ref[...]

    return pl.pallas_call(body, name=name, out_shape=jax.ShapeDtypeStruct((PACK_ROWS, D_MODEL), F32), compiler_params=_params(),
                          )(*[grads[k] for k in names])


def _small_gather_exchange(packed):
    def plan(ins, outs, send_sems, recv_sems):
        x, y, c, _ = _mesh_position()
        me = 4 * x + 2 * y + c
        own = pltpu.make_async_copy(ins[0], outs[0].at[me], send_sems.at[7])
        remote = []
        for d in range(1, 8):
            dx, dy, dc = (d >> 2) & 1, (d >> 1) & 1, d & 1
            remote.append(pltpu.make_async_remote_copy(src_ref=ins[0], dst_ref=outs[0].at[me], send_sem=send_sems.at[d - 1],
                                                       recv_sem=recv_sems.at[d - 1], device_id=(x ^ dx, y ^ dy, c ^ dc), device_id_type=MESH))
        return own, remote

    def start(*refs):
        own, remote = plan(*refs)
        own.start()
        for cp in remote:
            cp.start()

    def finish(*refs):
        own, remote = plan(*refs)
        for cp in remote:
            cp.wait()
        own.wait()

    return _Exchange([packed], [jax.ShapeDtypeStruct((8,) + packed.shape, packed.dtype)], 8, start, finish)


def _adamw_small(gathered, w, m, v, *, name):
    names = SMALL
    n = len(names)

    def body(*refs):
        gathered_ref = refs[0]
        w_refs, m_refs, v_refs = (dict(zip(names, refs[1 + i * n:1 + (i + 1) * n])) for i in range(3))
        loss_ref = refs[1 + 3 * n]
        go_refs, d_refs, nm_refs, nv_refs = (dict(zip(names, refs[2 + (3 + i) * n:2 + (4 + i) * n])) for i in range(4))
        total_ref = refs[2 + 7 * n]
        total = gathered_ref[0]
        for j in range(1, 8):
            total = total + gathered_ref[j]
        total_ref[...] = total
        loss_ref[...] = (0.5 / D_MODEL) * jnp.sum(total_ref[PACK_AT["sq_err"]], axis=1, keepdims=True)
        for k in names:
            g = total_ref[PACK_AT[k]]
            go_refs[k][...] = g
            d_refs[k][...], nm_refs[k][...], nv_refs[k][...] = _adamw_math(w_refs[k][...], g, m_refs[k][...], v_refs[k][...])

    like = [jax.ShapeDtypeStruct(w[k].shape, F32) for k in names]
    results = pl.pallas_call(body, name=name, out_shape=[jax.ShapeDtypeStruct((1, 1), F32)] + like * 4,
                             scratch_shapes=[pltpu.VMEM((PACK_ROWS, D_MODEL), F32)],
                             compiler_params=_params())(gathered, *[d[k] for d in (w, m, v) for k in names])
    return results[0], {k: tuple(results[1 + i * n + j] for i in range(4)) for j, k in enumerate(names)}


def _adamw_math(w, g, m, v):
    m = ADAM_B1 * m + (1.0 - ADAM_B1) * g
    v = ADAM_B2 * v + (1.0 - ADAM_B2) * (g * g)
    m_hat = m / (1.0 - ADAM_B1 ** ADAM_STEP)
    v_hat = v / (1.0 - ADAM_B2 ** ADAM_STEP)
    delta = -ADAM_LR * (m_hat / (jnp.sqrt(v_hat) + ADAM_EPS) + ADAM_WD * w)
    return delta, m, v


def _adamw(w, g, m, v, *, name):
    _, rows, cols = w.shape
    tr = _row_tile(rows)
    blk = pl.BlockSpec((None, tr, cols), lambda i: (0, i, 0))
    flat = pl.BlockSpec((tr, cols), lambda i: (i, 0))

    def body(w_ref, g_ref, m_ref, v_ref, go_ref, d_ref, nm_ref, nv_ref):
        g_v = g_ref[...]
        go_ref[...] = g_v
        d_ref[...], nm_ref[...], nv_ref[...] = _adamw_math(w_ref[...], g_v, m_ref[...], v_ref[...])

    shape = jax.ShapeDtypeStruct((1, rows, cols), F32)
    return pl.pallas_call(body, name=name, grid=(rows // tr,), out_shape=(shape,) * 4, in_specs=[blk, flat, blk, blk], out_specs=(blk,) * 4,
                          compiler_params=_params(("parallel",)))(w, g, m, v)


WEIGHTS = ["w_in", "lb_logits", "hg_norm_gain", "swa_sinks", "rel_bias", "w_mem_kv", "w_branch_hg", "w_branch_swa", "w_branch_mem",
           "w_out", "ln1_g", "ln1_b", "w_up", "w_down", "ln2_g", "ln2_b"]
BIG = ["w_in", "w_mem_kv", "w_branch_hg", "w_branch_swa", "w_branch_mem", "w_out", "w_up", "w_down"]


def kernel(x, mem, w_in, lb_logits, hg_norm_gain, swa_sinks, rel_bias, w_mem_kv, w_branch_hg, w_branch_swa, w_branch_mem, w_out, ln1_g, ln1_b, w_up, w_down, ln2_g, ln2_b, loss_target, m_w_in, m_lb_logits, m_hg_norm_gain, m_swa_sinks, m_rel_bias, m_w_mem_kv, m_w_branch_hg, m_w_branch_swa, m_w_branch_mem, m_w_out, m_ln1_g, m_ln1_b, m_w_up, m_w_down, m_ln2_g, m_ln2_b, v_w_in, v_lb_logits, v_hg_norm_gain, v_swa_sinks, v_rel_bias, v_w_mem_kv, v_w_branch_hg, v_w_branch_swa, v_w_branch_mem, v_w_out, v_ln1_g, v_ln1_b, v_w_up, v_w_down, v_ln2_g, v_ln2_b):
    w = dict(w_in=w_in, lb_logits=lb_logits, hg_norm_gain=hg_norm_gain, swa_sinks=swa_sinks, rel_bias=rel_bias, w_mem_kv=w_mem_kv,
             w_branch_hg=w_branch_hg, w_branch_swa=w_branch_swa, w_branch_mem=w_branch_mem, w_out=w_out, ln1_g=ln1_g, ln1_b=ln1_b,
             w_up=w_up, w_down=w_down, ln2_g=ln2_g, ln2_b=ln2_b)
    m = dict(w_in=m_w_in, lb_logits=m_lb_logits, hg_norm_gain=m_hg_norm_gain, swa_sinks=m_swa_sinks, rel_bias=m_rel_bias, w_mem_kv=m_w_mem_kv,
             w_branch_hg=m_w_branch_hg, w_branch_swa=m_w_branch_swa, w_branch_mem=m_w_branch_mem, w_out=m_w_out, ln1_g=m_ln1_g, ln1_b=m_ln1_b,
             w_up=m_w_up, w_down=m_w_down, ln2_g=m_ln2_g, ln2_b=m_ln2_b)
    v = dict(w_in=v_w_in, lb_logits=v_lb_logits, hg_norm_gain=v_hg_norm_gain, swa_sinks=v_swa_sinks, rel_bias=v_rel_bias, w_mem_kv=v_w_mem_kv,
             w_branch_hg=v_w_branch_hg, w_branch_swa=v_w_branch_swa, w_branch_mem=v_w_branch_mem, w_out=v_w_out, ln1_g=v_ln1_g, ln1_b=v_ln1_b,
             w_up=v_w_up, w_down=v_w_down, ln2_g=v_ln2_g, ln2_b=v_ln2_b)
    shapes = {k: w[k].shape for k in WEIGHTS}
    for d in (w, m, v):
        d["w_in"] = d["w_in"].reshape(D_MODEL, IN_COLS // N_SHARDS).T[None]
    shards = {k: w[k].reshape(w[k].shape[-2], w[k].shape[-1]).astype(BF16) for k in BIG}
    x2d = x.reshape(x.shape[-2], D_MODEL)
    xb, ((wi4, wmkv),) = _cast_bf16(x2d, name="gather_weights", exchanges=[_gather_exchange([shards["w_in"], shards["w_mem_kv"]])])
    wi_t = wi4.reshape(IN_COLS, D_MODEL)

    grad_x, halves, small = _local_step(
        x2d, xb, mem.reshape(MEM_LEN, D_MODEL), loss_target.reshape(loss_target.shape[-2], D_MODEL),
        wi_t, wmkv, shards, lb_logits, hg_norm_gain, swa_sinks, rel_bias, ln1_g, ln1_b, ln2_g, ln2_b, distributed=True)

    reduced = dict(zip(BIG, _join_halves([halves[k] for k in BIG], name="join_halves")))

    outs = {k: _adamw(w[k], reduced[k], m[k], v[k], name="adamw_" + k) for k in BIG}
    loss, small_outs = _adamw_small(small, w, m, v, name="adamw_small")
    outs.update(small_outs)
    grad_out, delta_out, m_out, v_out = ({k: outs[k][i] for k in WEIGHTS} for i in range(4))
    for out in (grad_out, delta_out, m_out, v_out):
        out["w_in"] = out["w_in"][0].T

    result = [loss.reshape(()), grad_x.reshape(x.shape)]
    for out in (grad_out, delta_out, m_out, v_out):
        result += [out[k].reshape(shapes[k]) for k in WEIGHTS]
    return tuple(result)
```

```python
import math
from typing import Callable, NamedTuple, Optional

import jax
import jax.numpy as jnp
from jax import lax
from jax.experimental import pallas as pl
from jax.experimental.pallas import tpu as pltpu

F32 = jnp.float32
BF16 = jnp.bfloat16
HIGHEST = lax.Precision.HIGHEST
MESH = pl.DeviceIdType.MESH

D_MODEL = 1024
MEM_LEN = 256
HG_HEADS = 8
HG_DK = 128
HG_CHUNK = 64
SWA_HEADS = 16
SWA_KV_HEADS = 2
SWA_GROUP = 8
SWA_HEAD_DIM = 64
SWA_BLOCK = 128
SWA_WINDOW = 128
MEM_HEADS = 4
MEM_HEAD_DIM = 256
NUM_BUCKETS = 32
MAX_DISTANCE = 128
D_FF = 4096
LN_EPS = 1e-5
RMS_EPS = 1e-6
ALPHA = 2.0 ** 0.25
W_A, W_B, W_C, W_D = 4096, 1280, 1024, 3072
IN_COLS = W_A + W_B + W_C + W_D
N_SHARDS = 4
ADAM_LR = 0.001
ADAM_B1 = 0.9
ADAM_B2 = 0.999
ADAM_EPS = 1e-08
ADAM_WD = 0.01
ADAM_STEP = 10
MASK_VALUE = -1e30
VMEM_LIMIT = 56 * 1024 * 1024

NN = ((1,), (0,))
NT = ((1,), (1,))
TN = ((0,), (0,))
HBM = pl.BlockSpec(memory_space=pltpu.HBM)


def _dot(a, b, dims=NN, precision=None):
    return lax.dot_general(a, b, (dims, ((), ())), precision=precision, preferred_element_type=F32)


def _params(sem=None):
    return pltpu.CompilerParams(dimension_semantics=sem, vmem_limit_bytes=VMEM_LIMIT)


def _resident(shape):
    zeros = (0,) * len(shape)
    return pl.BlockSpec(shape, lambda *_: zeros, pipeline_mode=pl.Buffered(1))


def _resident_rows(arr, offset, rows):
    return pl.BlockSpec((pl.Element(rows), pl.Element(arr.shape[1])), lambda *_: (offset, 0), pipeline_mode=pl.Buffered(1))


def _mm(a, b, *, mode, tm, tn, tk, name, out_dtype=F32, b_panels=False, b_rows=None, out_panels=False, rows_of=None, row_offset=0,
        into=None):
    if mode == "tn":
        kdim, m = a.shape
    else:
        m, kdim = a.shape
    if b_panels:
        n = b.shape[0] * b.shape[2]
        assert b.shape[2] == tn and mode == "nn"
    elif b_rows is not None:
        assert mode == "nt"
        b_offset, n = b_rows
    elif mode == "nt":
        n = b.shape[0]
    else:
        n = b.shape[1]
    assert m % tm == 0 and n % tn == 0 and kdim % tk == 0, (name, m, n, kdim)
    nk = kdim // tk
    dims = {"nn": NN, "nt": NT, "tn": TN}[mode]
    a_spec = pl.BlockSpec((tk, tm), lambda i, j, k: (k, i)) if mode == "tn" else pl.BlockSpec((tm, tk), lambda i, j, k: (i, k))
    if b_panels:
        b_spec = pl.BlockSpec((None, tk, tn), lambda i, j, k: (j, k, 0))
    elif b_rows is not None:
        assert b_offset % BF16_SUBLANES == 0 and tn % BF16_SUBLANES == 0 and tk % 128 == 0
        b_spec = pl.BlockSpec((pl.Element(tn), pl.Element(tk)),
                              lambda i, j, k: (pl.multiple_of(b_offset + j * tn, BF16_SUBLANES), pl.multiple_of(k * tk, 128)))
    elif mode == "nt":
        b_spec = pl.BlockSpec((tn, tk), lambda i, j, k: (j, k))
    else:
        b_spec = pl.BlockSpec((tk, tn), lambda i, j, k: (k, j))
    in_specs = [a_spec, b_spec]
    operands = [a, b]
    aliases = {}
    if out_panels:
        out_shape = jax.ShapeDtypeStruct((n // tn, m, tn), out_dtype)
        o_spec = pl.BlockSpec((None, tm, tn), lambda i, j, k: (j, i, 0))
    elif rows_of is not None:
        out_shape = jax.ShapeDtypeStruct((rows_of, n), out_dtype)
        assert row_offset % BF16_SUBLANES == 0 and tm % BF16_SUBLANES == 0 and tn % 128 == 0
        o_spec = pl.BlockSpec((pl.Element(tm), pl.Element(tn)),
                              lambda i, j, k: (pl.multiple_of(row_offset + i * tm, BF16_SUBLANES), pl.multiple_of(j * tn, 128)))
        if into is not None:
            in_specs.append(pl.BlockSpec(memory_space=pl.ANY))
            operands.append(into)
            aliases = {2: 0}
    else:
        out_shape = jax.ShapeDtypeStruct((m, n), out_dtype)
        o_spec = pl.BlockSpec((tm, tn), lambda i, j, k: (i, j))
    n_in = len(operands)

    def body(*refs):
        a_ref, b_ref, o_ref = refs[0], refs[1], refs[n_in]
        part = _dot(a_ref[...].astype(BF16), b_ref[...].astype(BF16), dims)

        def finish(acc):
            o_ref[...] = acc.astype(out_dtype)

        if nk == 1:
            finish(part)
        else:
            acc_ref = refs[-1]
            k = pl.program_id(2)

            @pl.when(k == 0)
            def _():
                acc_ref[...] = part

            @pl.when(k > 0)
            def _():
                acc_ref[...] += part

            @pl.when(k == nk - 1)
            def _():
                finish(acc_ref[...])

    return pl.pallas_call(
        body, name=name, out_shape=out_shape, grid=(m // tm, n // tn, nk), in_specs=in_specs, out_specs=o_spec,
        scratch_shapes=[pltpu.VMEM((tm, tn), F32)] if nk > 1 else [], input_output_aliases=aliases,
        compiler_params=_params(("parallel", "parallel", "arbitrary")),
    )(*operands)


def _dx_matmul(dzs, wi_t, resid, *, tm, name, exchanges=()):
    s = resid.shape[0]
    npieces = len(dzs)
    offsets = [sum(dz.shape[1] for dz in dzs[:p]) for p in range(npieces)]
    tile = lambda i: (i, 0)
    in_specs = [pl.BlockSpec((tm, dz.shape[1]), tile) for dz in dzs] + [_resident(wi_t.shape), pl.BlockSpec((tm, D_MODEL), tile)]

    def body(*refs):
        dz_refs, w_ref, r_ref, o_ref = refs[:npieces], refs[npieces], refs[npieces + 1], refs[npieces + 2]
        total = ALPHA * r_ref[...]
        for p in range(npieces):
            total = total + _dot(dz_refs[p][...], w_ref[offsets[p]:offsets[p] + dzs[p].shape[1], :], NN)
        o_ref[...] = total

    return _fused_call(
        body, name=name, out_shape=jax.ShapeDtypeStruct((s, D_MODEL), F32), grid=(s // tm,), in_specs=in_specs,
        out_specs=pl.BlockSpec((tm, D_MODEL), tile), scratch_shapes=[], operands=[*dzs, wi_t, resid], exchanges=exchanges)


def _lower_bound(lbl_ref):
    l0, l1 = lbl_ref[0:1, :], lbl_ref[1:2, :]
    mx = jnp.maximum(l0, l1)
    e0, e1 = jnp.exp(l0 - mx), jnp.exp(l1 - mx)
    return e0 / (e0 + e1)


HEAD_COLS = [slice(h * HG_DK, (h + 1) * HG_DK) for h in range(HG_HEADS)]


def _head_mean(x):
    return jnp.concatenate([jnp.broadcast_to(jnp.mean(x[:, c], axis=-1, keepdims=True), (x.shape[0], HG_DK)) for c in HEAD_COLS], axis=1)


def _triangle_sum(tri_b, x):
    p0 = x.astype(BF16)
    r1 = x - p0.astype(F32)
    p1 = r1.astype(BF16)
    p2 = (r1 - p1.astype(F32)).astype(BF16)
    return _dot(tri_b, p0) + _dot(tri_b, p1) + _dot(tri_b, p2)


def _chunk_forward(q, fl, v, lb, tril_b):
    sg = jax.nn.sigmoid(fl)
    f = lb + (1.0 - lb) * sg
    k = 1.0 - f
    b = _triangle_sum(tril_b, jnp.log(f))
    b_last = b[HG_CHUNK - 1:HG_CHUNK, :]
    eb, enb, eo = jnp.exp(b), jnp.exp(-b), jnp.exp(b_last - b)
    return sg, f, k, b_last, eb, enb, eo, q * eb, k * enb, k * eo


def _hgrn_fwd(xb, wi_t, lb_logits, gain, *, name, exchanges=()):
    s = xb.shape[0]
    t = min(256, s)
    ncs = t // HG_CHUNK

    def body(x_ref, w_ref, lbl_ref, gain_ref, z_ref, oa_ref, oraw_ref, st_ref, state):
        @pl.when(pl.program_id(0) == 0)
        def _():
            state[...] = jnp.zeros_like(state)

        z_ref[...] = _dot(x_ref[...], w_ref[...], NT)
        lb_all = _lower_bound(lbl_ref)
        row = lax.broadcasted_iota(jnp.int32, (HG_CHUNK, HG_CHUNK), 0)
        col = lax.broadcasted_iota(jnp.int32, (HG_CHUNK, HG_CHUNK), 1)
        tril = row >= col
        tril_b = tril.astype(BF16)
        gain_all = gain_ref[...]

        def chunk(i, carry):
            r = pl.ds(pl.multiple_of(i * HG_CHUNK, HG_CHUNK), HG_CHUNK)
            q, fl, v, hg = (z_ref[r, j * D_MODEL:(j + 1) * D_MODEL] for j in range(4))
            _, _, _, b_last, _, _, _, q_in, k_in, k_out = _chunk_forward(q, fl, v, lb_all, tril_b)
            q_in_b, k_in_b, k_out_b, vb = (u.astype(BF16) for u in (q_in, k_in, k_out, v))
            decay = jnp.exp(b_last)
            sts = [state[h] for h in range(HG_HEADS)]
            attn = [_dot(q_in_b[:, c], k_in_b[:, c], NT) for c in HEAD_COLS]
            inter = [_dot(q_in_b[:, c], sts[h].astype(BF16), NT) for h, c in enumerate(HEAD_COLS)]
            upd = [_dot(vb[:, c], k_out_b[:, c], TN) for c in HEAD_COLS]
            attn = [jnp.where(tril, a, 0.0).astype(BF16) for a in attn]
            outs = [_dot(attn[h], vb[:, c], NN) + inter[h] for h, c in enumerate(HEAD_COLS)]
            for h, c in enumerate(HEAD_COLS):
                st_ref[h, i] = sts[h]
                state[h] = sts[h] * decay[:, c] + upd[h]
            o = jnp.concatenate(outs, axis=1)
            oraw_ref[r, :] = o
            n = o * lax.rsqrt(_head_mean(o * o) + RMS_EPS)
            oa_ref[r, :] = (n * gain_all * (hg * jax.nn.sigmoid(hg))).astype(BF16)
            return carry

        lax.fori_loop(0, ncs, chunk, 0, unroll=True)

    tile = lambda i: (i, 0)
    return _fused_call(
        body, name=name, grid=(s // t,),
        out_shape=(jax.ShapeDtypeStruct((s, W_A), F32), jax.ShapeDtypeStruct((s, D_MODEL), BF16), jax.ShapeDtypeStruct((s, D_MODEL), F32),
                   jax.ShapeDtypeStruct((HG_HEADS, s // HG_CHUNK, HG_DK, HG_DK), F32)),
        in_specs=[pl.BlockSpec((t, D_MODEL), tile), _resident_rows(wi_t, 0, W_A), _resident((2, D_MODEL)), _resident((1, D_MODEL))],
        out_specs=(pl.BlockSpec((t, W_A), tile), pl.BlockSpec((t, D_MODEL), tile), pl.BlockSpec((t, D_MODEL), tile),
                   pl.BlockSpec((HG_HEADS, ncs, HG_DK, HG_DK), lambda i: (0, i, 0, 0))),
        scratch_shapes=[pltpu.VMEM((HG_HEADS, HG_DK, HG_DK), F32)],
        operands=[xb, wi_t, lb_logits, gain], exchanges=exchanges)


def _hgrn_bwd(za, oraw, do_a, states, lb_logits, gain, *, name, exchanges=()):
    s = za.shape[0]
    t = min(256, s)
    ncs = t // HG_CHUNK
    nt = s // t

    def body(z_ref, oraw_ref, do_ref, st_ref, lbl_ref, gain_ref, dz_ref, stats_ref, dstate):
        step = pl.program_id(0)

        @pl.when(step == 0)
        def _():
            dstate[...] = jnp.zeros_like(dstate)
            stats_ref[...] = jnp.zeros_like(stats_ref)

        lb_all = _lower_bound(lbl_ref)
        row = lax.broadcasted_iota(jnp.int32, (HG_CHUNK, HG_CHUNK), 0)
        col = lax.broadcasted_iota(jnp.int32, (HG_CHUNK, HG_CHUNK), 1)
        tril = row >= col
        tril_b = tril.astype(BF16)
        triu_b = (row <= col).astype(BF16)
        gain_all = gain_ref[...]

        def chunk(ii, carry):
            i = ncs - 1 - ii
            r = pl.ds(pl.multiple_of(i * HG_CHUNK, HG_CHUNK), HG_CHUNK)
            q, fl, v, hg = (z_ref[r, j * D_MODEL:(j + 1) * D_MODEL] for j in range(4))
            o = oraw_ref[r, :]
            doa = do_ref[r, :]
            rms = lax.rsqrt(_head_mean(o * o) + RMS_EPS)
            n = o * rms
            sgg = jax.nn.sigmoid(hg)
            silu = hg * sgg
            dhg = doa * n * gain_all * (sgg * (1.0 + hg * (1.0 - sgg)))
            dgain = jnp.sum(doa * n * silu, axis=0, keepdims=True)
            dn = doa * gain_all * silu
            do = rms * (dn - n * _head_mean(dn * n))
            sg, f, k, b_last, eb, enb, eo, q_in, k_in, k_out = _chunk_forward(q, fl, v, lb_all, tril_b)
            q_in_b, k_in_b, k_out_b, vb, dob = (u.astype(BF16) for u in (q_in, k_in, k_out, v, do))
            decay = jnp.exp(b_last)
            sts = [st_ref[h, i] for h in range(HG_HEADS)]
            dsts = [dstate[h] for h in range(HG_HEADS)]
            dsts_b = [d.astype(BF16) for d in dsts]
            heads = list(enumerate(HEAD_COLS))
            attn = [_dot(q_in_b[:, c], k_in_b[:, c], NT) for h, c in heads]
            dattn = [_dot(dob[:, c], vb[:, c], NT) for h, c in heads]
            dq_st = [_dot(dob[:, c], sts[h].astype(BF16), NN) for h, c in heads]
            dk_out = [_dot(vb[:, c], dsts_b[h], NN) for h, c in heads]
            dv_st = [_dot(k_out_b[:, c], dsts_b[h], NT) for h, c in heads]
            dst_o = [_dot(dob[:, c], q_in_b[:, c], TN) for h, c in heads]
            attn = [jnp.where(tril, a, 0.0).astype(BF16) for a in attn]
            dattn = [jnp.where(tril, a, 0.0).astype(BF16) for a in dattn]
            dq_in = jnp.concatenate([_dot(dattn[h], k_in_b[:, c], NN) + dq_st[h] for h, c in heads], axis=1)
            dk_in = jnp.concatenate([_dot(dattn[h], q_in_b[:, c], TN) for h, c in heads], axis=1)
            dv = jnp.concatenate([_dot(attn[h], dob[:, c], TN) + dv_st[h] for h, c in heads], axis=1)
            dk_out = jnp.concatenate(dk_out, axis=1)
            dst_st = jnp.concatenate([jnp.sum(dsts[h] * sts[h], axis=0, keepdims=True) for h in range(HG_HEADS)], axis=1)
            for h, c in heads:
                dstate[h] = dsts[h] * decay[:, c] + dst_o[h]
            db_last = decay * dst_st + jnp.sum(dk_out * k_out, axis=0, keepdims=True)
            db = dq_in * q_in - dk_in * k_in - dk_out * k_out
            dg = _triangle_sum(triu_b, db) + db_last
            dk = dk_in * enb + dk_out * eo
            df = dg / f - dk
            stats_ref[0:1, :] += dgain
            stats_ref[1:2, :] += jnp.sum(df * (1.0 - sg), axis=0, keepdims=True)
            dz_ref[r, 0:1024] = (dq_in * eb).astype(BF16)
            dz_ref[r, 1024:2048] = (df * (1.0 - lb_all) * sg * (1.0 - sg)).astype(BF16)
            dz_ref[r, 2048:3072] = dv.astype(BF16)
            dz_ref[r, 3072:4096] = dhg.astype(BF16)
            return carry

        lax.fori_loop(0, ncs, chunk, 0, unroll=True)

        @pl.when(step == nt - 1)
        def _():
            dl0 = stats_ref[1:2, :] * lb_all * (1.0 - lb_all)
            stats_ref[1:2, :] = dl0
            stats_ref[2:3, :] = -dl0

    rev = lambda i: (nt - 1 - i, 0)
    return _fused_call(
        body, name=name, grid=(nt,),
        out_shape=(jax.ShapeDtypeStruct((s, W_A), BF16), jax.ShapeDtypeStruct((8, D_MODEL), F32)),
        in_specs=[pl.BlockSpec((t, W_A), rev), pl.BlockSpec((t, D_MODEL), rev), pl.BlockSpec((t, D_MODEL), rev),
                  pl.BlockSpec((HG_HEADS, ncs, HG_DK, HG_DK), lambda i: (0, nt - 1 - i, 0, 0)),
                  _resident((2, D_MODEL)), _resident((1, D_MODEL))],
        out_specs=(pl.BlockSpec((t, W_A), rev), pl.BlockSpec((8, D_MODEL), lambda i: (0, 0))),
        scratch_shapes=[pltpu.VMEM((HG_HEADS, HG_DK, HG_DK), F32)],
        operands=[za, oraw, do_a, states, lb_logits, gain], exchanges=exchanges)


def _t5_bucket(n):
    max_exact = NUM_BUCKETS // 2
    nf = jnp.maximum(n, 1).astype(F32)
    large = max_exact + (jnp.log(nf / max_exact) / math.log(MAX_DISTANCE / max_exact) * (NUM_BUCKETS - max_exact)).astype(jnp.int32)
    large = jnp.minimum(large, NUM_BUCKETS - 1)
    return jnp.where(n < max_exact, n, large)


def _bias_selector():
    qi = jnp.arange(SWA_BLOCK)[:, None] + SWA_BLOCK
    kj = jnp.arange(2 * SWA_BLOCK)[None, :]
    dist = qi - kj
    band = ((dist >= 0) & (dist < SWA_WINDOW)).reshape(1, -1)
    bucket = _t5_bucket(jnp.clip(dist, 0, SWA_WINDOW - 1)).reshape(1, -1)
    onehot = ((bucket == jnp.arange(NUM_BUCKETS)[:, None]) & band).astype(F32)
    return onehot, jnp.where(band, 0.0, MASK_VALUE).astype(F32)


def _bias_table(rel_bias_t, onehot, maskrow, *, name):
    def body(rb_ref, oh_ref, mask_ref, o_ref):
        o_ref[...] = _dot(rb_ref[...], oh_ref[...], NN, HIGHEST) + mask_ref[...]

    return pl.pallas_call(body, name=name, out_shape=jax.ShapeDtypeStruct((SWA_HEADS, onehot.shape[1]), F32),
                          compiler_params=_params())(rel_bias_t, onehot, maskrow)


def _bias_grad(dbias2d, onehot, *, name):
    def body(db_ref, oh_ref, o_ref):
        o_ref[...] = _dot(db_ref[...], oh_ref[...], NT, HIGHEST)

    return pl.pallas_call(body, name=name, out_shape=jax.ShapeDtypeStruct((SWA_HEADS, NUM_BUCKETS), F32),
                          compiler_params=_params())(dbias2d, onehot)


def _swa_operands(zq_ref, kv_cur_ref, kv_prev_ref):
    q = (zq_ref[:, 0:1024] * (SWA_HEAD_DIM ** -0.5)).astype(BF16)
    kv_c = kv_cur_ref[...].astype(BF16)
    kv_p = kv_prev_ref[...].astype(BF16)
    kks = [jnp.concatenate([kv_p[:, g * 64:(g + 1) * 64], kv_c[:, g * 64:(g + 1) * 64]], axis=0) for g in range(SWA_KV_HEADS)]
    vvs = [jnp.concatenate([kv_p[:, 128 + g * 64:128 + (g + 1) * 64], kv_c[:, 128 + g * 64:128 + (g + 1) * 64]], axis=0)
           for g in range(SWA_KV_HEADS)]
    return q, kks, vvs


SWA_PART_HEADS = 8
SWA_PARTS = [(h0 // SWA_GROUP, h0) for h0 in range(0, SWA_HEADS, SWA_PART_HEADS)]


def _part_lanes(h0):
    return slice(h0 * SWA_BLOCK, (h0 + SWA_PART_HEADS) * SWA_BLOCK)


def _stack_heads(x, h0):
    return jnp.concatenate([x[:, h * SWA_HEAD_DIM:(h + 1) * SWA_HEAD_DIM] for h in range(h0, h0 + SWA_PART_HEADS)], axis=0)


def _heads_to_lanes(xt):
    pairs = []
    for j in range(0, xt.shape[1] // SWA_BLOCK, 2):
        two = jnp.concatenate([xt[:, j * SWA_BLOCK:(j + 1) * SWA_BLOCK], xt[:, (j + 1) * SWA_BLOCK:(j + 2) * SWA_BLOCK]], axis=0)
        pairs.append(two.T)
    return jnp.concatenate(pairs, axis=1)


def _swa_softmax(score_t, bias_ref, sink_ref, h0):
    sc = score_t + bias_ref[:, _part_lanes(h0)]
    sink = sink_ref[:, _part_lanes(h0)]
    m = jnp.maximum(jnp.max(sc, axis=0, keepdims=True), sink)
    e = jnp.exp(sc - m)
    e_sink = jnp.exp(sink - m)
    return e, 1.0 / (jnp.sum(e, axis=0, keepdims=True) + e_sink), e_sink


def _swa_tables(bias2d, sinks):
    bias_t = bias2d.reshape(SWA_HEADS, SWA_BLOCK, 2 * SWA_BLOCK).transpose(2, 0, 1).reshape(2 * SWA_BLOCK, SWA_HEADS * SWA_BLOCK)
    first = jnp.where(jnp.arange(2 * SWA_BLOCK)[:, None] < SWA_BLOCK, MASK_VALUE, bias_t)
    return jnp.stack([first, bias_t]), jnp.repeat(sinks, SWA_BLOCK, axis=1)


def _swa_fwd(zb, bias_tables, sink_lanes, *, name, exchanges=()):
    s = zb.shape[0]
    nb = s // SWA_BLOCK

    def body(zq_ref, kvc_ref, kvp_ref, bias_ref, sink_ref, o_ref):
        q, kks, vvs = _swa_operands(zq_ref, kvc_ref, kvp_ref)
        scores = [_dot(kks[g], _stack_heads(q, h0), NT) for g, h0 in SWA_PARTS]
        probs = []
        for score, (_, h0) in zip(scores, SWA_PARTS):
            e, inv, _ = _swa_softmax(score, bias_ref, sink_ref, h0)
            probs.append((e * inv).astype(BF16))
        outs = [_dot(vvs[g], p, TN) for p, (g, _) in zip(probs, SWA_PARTS)]
        o_ref[...] = jnp.concatenate([_heads_to_lanes(o) for o in outs], axis=1).astype(BF16)

    return _fused_call(
        body, name=name, grid=(nb,), out_shape=jax.ShapeDtypeStruct((s, D_MODEL), BF16),
        in_specs=[pl.BlockSpec((SWA_BLOCK, W_B), lambda n: (n, 0)),
                  pl.BlockSpec((SWA_BLOCK, 256), lambda n: (n, 4)),
                  pl.BlockSpec((SWA_BLOCK, 256), lambda n: (jnp.maximum(n - 1, 0), 4)),
                  pl.BlockSpec((None, 2 * SWA_BLOCK, SWA_HEADS * SWA_BLOCK), lambda n: (jnp.minimum(n, 1), 0, 0)),
                  _resident((1, SWA_HEADS * SWA_BLOCK))],
        out_specs=pl.BlockSpec((SWA_BLOCK, D_MODEL), lambda n: (n, 0)), scratch_shapes=[],
        operands=[zb, zb, zb, bias_tables, sink_lanes], exchanges=exchanges)


def _swa_bwd(zb, do_b, bias_tables, sink_lanes, *, name, exchanges=()):
    s = zb.shape[0]
    nb = s // SWA_BLOCK
    scale = SWA_HEAD_DIM ** -0.5

    def body(zq_ref, kvc_ref, kvp_ref, do_ref, bias_ref, sink_ref, dz_ref, dbias_ref, dsink_ref, carry, dsink_acc):
        step = pl.program_id(0)

        @pl.when(step == 0)
        def _():
            carry[...] = jnp.zeros_like(carry)
            dsink_acc[...] = jnp.zeros_like(dsink_acc)
            dbias_ref[...] = jnp.zeros_like(dbias_ref)

        q, kks, vvs = _swa_operands(zq_ref, kvc_ref, kvp_ref)
        do = do_ref[...].astype(BF16)
        parts = range(len(SWA_PARTS))
        q_rows = [_stack_heads(q, h0) for _, h0 in SWA_PARTS]
        do_rows = [_stack_heads(do, h0) for _, h0 in SWA_PARTS]
        scores = [_dot(kks[g], q_rows[i], NT) for i, (g, _) in enumerate(SWA_PARTS)]
        soft = [_swa_softmax(scores[i], bias_ref, sink_ref, h0) for i, (_, h0) in enumerate(SWA_PARTS)]
        dps = [_dot(vvs[g], do_rows[i], NT) for i, (g, _) in enumerate(SWA_PARTS)]
        ps, dss = [], []
        for i, (_, h0) in enumerate(SWA_PARTS):
            e, inv, e_sink = soft[i]
            p = e * inv
            delta = jnp.sum(p * dps[i], axis=0, keepdims=True)
            ds = p * (dps[i] - delta)
            dbias_ref[:, _part_lanes(h0)] += ds
            dsink_acc[:, _part_lanes(h0)] -= e_sink * inv * delta
            ps.append(p.astype(BF16))
            dss.append(ds.astype(BF16))
        dqs = [_dot(kks[g], dss[i], TN) * scale for i, (g, _) in enumerate(SWA_PARTS)]
        in_group = lambda xs, g, axis: jnp.concatenate([xs[i] for i in parts if SWA_PARTS[i][0] == g], axis=axis)
        dkks = [_dot(in_group(dss, g, 1), in_group(q_rows, g, 0), NN) for g in range(SWA_KV_HEADS)]
        dvvs = [_dot(in_group(ps, g, 1), in_group(do_rows, g, 0), NN) for g in range(SWA_KV_HEADS)]
        dkv = jnp.concatenate(dkks + dvvs, axis=1)
        dz_ref[:, 0:1024] = jnp.concatenate([_heads_to_lanes(dq) for dq in dqs], axis=1).astype(BF16)
        dz_ref[:, 1024:1280] = (dkv[SWA_BLOCK:, :] + carry[...]).astype(BF16)
        carry[...] = dkv[:SWA_BLOCK, :]

        @pl.when(step == nb - 1)
        def _():
            acc = dsink_acc[...]
            dsink_ref[...] = jnp.concatenate([jnp.sum(acc[:, h * SWA_BLOCK:(h + 1) * SWA_BLOCK], axis=1, keepdims=True)
                                              for h in range(SWA_HEADS)], axis=1)

    rev = lambda i: (nb - 1 - i, 0)
    table_shape = (2 * SWA_BLOCK, SWA_HEADS * SWA_BLOCK)
    return _fused_call(
        body, name=name, grid=(nb,),
        out_shape=(jax.ShapeDtypeStruct((s, W_B), BF16), jax.ShapeDtypeStruct(table_shape, F32), jax.ShapeDtypeStruct((1, SWA_HEADS), F32)),
        in_specs=[pl.BlockSpec((SWA_BLOCK, W_B), rev),
                  pl.BlockSpec((SWA_BLOCK, 256), lambda i: (nb - 1 - i, 4)),
                  pl.BlockSpec((SWA_BLOCK, 256), lambda i: (jnp.maximum(nb - 2 - i, 0), 4)),
                  pl.BlockSpec((SWA_BLOCK, D_MODEL), rev),
                  pl.BlockSpec((None,) + table_shape, lambda i: (jnp.minimum(nb - 1 - i, 1), 0, 0)),
                  _resident((1, SWA_HEADS * SWA_BLOCK))],
        out_specs=(pl.BlockSpec((SWA_BLOCK, W_B), rev), pl.BlockSpec(table_shape, lambda i: (0, 0)),
                   pl.BlockSpec((1, SWA_HEADS), lambda i: (0, 0))),
        scratch_shapes=[pltpu.VMEM((SWA_BLOCK, 256), F32), pltpu.VMEM((1, SWA_HEADS * SWA_BLOCK), F32)],
        operands=[zb, zb, zb, do_b, bias_tables, sink_lanes], exchanges=exchanges)


MEM_COLS = [slice(h * MEM_HEAD_DIM, (h + 1) * MEM_HEAD_DIM) for h in range(MEM_HEADS)]
MEM_VCOLS = [slice(D_MODEL + h * MEM_HEAD_DIM, D_MODEL + (h + 1) * MEM_HEAD_DIM) for h in range(MEM_HEADS)]


def _mem_probs(zc_ref, mkv_ref):
    qs = [(zc_ref[:, c] * (MEM_HEAD_DIM ** -0.5)).astype(BF16) for c in MEM_COLS]
    scores = [_dot(qs[h], mkv_ref[:, c], NT) for h, c in enumerate(MEM_COLS)]
    ps = []
    for sc in scores:
        e = jnp.exp(sc - jnp.max(sc, axis=-1, keepdims=True))
        ps.append(e / jnp.sum(e, axis=-1, keepdims=True))
    return qs, ps


def _mem_fwd(xb, wi_t, mkv, *, name):
    s = xb.shape[0]
    t = min(512, s)

    def body(x_ref, w_ref, mkv_ref, zc_ref, o_ref):
        zc_ref[...] = _dot(x_ref[...], w_ref[...], NT).astype(BF16)
        _, ps = _mem_probs(zc_ref, mkv_ref)
        ps = [p.astype(BF16) for p in ps]
        o_ref[...] = jnp.concatenate([_dot(ps[h], mkv_ref[:, vc], NN) for h, vc in enumerate(MEM_VCOLS)], axis=1).astype(BF16)

    row = pl.BlockSpec((t, D_MODEL), lambda i: (i, 0))
    return pl.pallas_call(
        body, name=name, grid=(s // t,), out_shape=(jax.ShapeDtypeStruct((s, D_MODEL), BF16),) * 2,
        in_specs=[row, _resident_rows(wi_t, W_A + W_B, W_C), _resident((MEM_LEN, 2 * D_MODEL))],
        out_specs=(row, row), compiler_params=_params(("parallel",)),
    )(xb, wi_t, mkv)


def _mem_bwd(xb, zc, do_c, mkv, *, name):
    s = zc.shape[0]
    t = min(512, s)
    nt = s // t

    def body(x_ref, zc_ref, do_ref, mkv_ref, dz_ref, dmkv_ref, gwi_ref, acc):
        @pl.when(pl.program_id(0) == 0)
        def _():
            dmkv_ref[...] = jnp.zeros_like(dmkv_ref)
            acc[...] = jnp.zeros_like(acc)

        heads = range(MEM_HEADS)
        qs, ps = _mem_probs(zc_ref, mkv_ref)
        dos = [do_ref[:, c].astype(BF16) for c in MEM_COLS]
        dps = [_dot(dos[h], mkv_ref[:, MEM_VCOLS[h]], NT) for h in heads]
        dss = [(ps[h] * (dps[h] - jnp.sum(ps[h] * dps[h], axis=-1, keepdims=True))).astype(BF16) for h in heads]
        ps = [p.astype(BF16) for p in ps]
        dz = jnp.concatenate([_dot(dss[h], mkv_ref[:, MEM_COLS[h]], NN) * (MEM_HEAD_DIM ** -0.5) for h in heads], axis=1).astype(BF16)
        dz_ref[...] = dz
        dmkv_ref[...] += jnp.concatenate([_dot(dss[h], qs[h], TN) for h in heads] + [_dot(ps[h], dos[h], TN) for h in heads], axis=1)
        acc[...] += _dot(dz, x_ref[...], TN)

        @pl.when(pl.program_id(0) == nt - 1)
        def _():
            pltpu.sync_copy(acc, gwi_ref.at[pl.ds(W_A + W_B, W_C), :])

    row = pl.BlockSpec((t, D_MODEL), lambda i: (i, 0))
    return pl.pallas_call(
        body, name=name, grid=(nt,),
        out_shape=(jax.ShapeDtypeStruct((s, D_MODEL), BF16), jax.ShapeDtypeStruct((MEM_LEN, 2 * D_MODEL), F32),
                   jax.ShapeDtypeStruct((IN_COLS, D_MODEL), F32)),
        in_specs=[row, row, row, _resident((MEM_LEN, 2 * D_MODEL))],
        out_specs=(row, pl.BlockSpec((MEM_LEN, 2 * D_MODEL), lambda i: (0, 0)), HBM),
        scratch_shapes=[pltpu.VMEM((W_C, D_MODEL), F32)],
        compiler_params=_params(("arbitrary",)),
    )(xb, zc, do_c, mkv)


def _normalize(pre):
    mu = jnp.mean(pre, axis=-1, keepdims=True)
    xc = pre - mu
    rstd = lax.rsqrt(jnp.mean(xc * xc, axis=-1, keepdims=True) + LN_EPS)
    return xc * rstd, rstd


def _layer_norm_bwd(dh, xhat, rstd, g):
    dxh = dh * g
    dpre = rstd * (dxh - jnp.mean(dxh, axis=-1, keepdims=True) - xhat * jnp.mean(dxh * xhat, axis=-1, keepdims=True))
    return dpre, jnp.sum(dh * xhat, axis=0, keepdims=True), jnp.sum(dh, axis=0, keepdims=True)


def _merge_fwd(o_a, o_b, o_c, x, wi_t, wbr, wo, *, name):
    s = x.shape[0]
    t = min(256, s)
    row = lambda w: pl.BlockSpec((t, w), lambda i: (i, 0))

    def body(oa_ref, ob_ref, oc_ref, x_ref, wg_ref, wa_ref, wb_ref, wc_ref, wo_ref, zd_ref, xhat_ref, rstd_ref, merged_ref, pa_ref, pb_ref, pc_ref):
        wbr_refs = (wa_ref, wb_ref, wc_ref)
        zd_ref[...] = _dot(x_ref[...].astype(BF16), wg_ref[...], NT)
        merged = jnp.zeros((t, D_MODEL), F32)
        for b, (o_ref, p_ref) in enumerate(((oa_ref, pa_ref), (ob_ref, pb_ref), (oc_ref, pc_ref))):
            p = _dot(o_ref[...], wbr_refs[b][...], NN)
            p_ref[...] = p.astype(BF16)
            merged = merged + jax.nn.sigmoid(zd_ref[:, b * D_MODEL:(b + 1) * D_MODEL]) * p
        merged_b = merged.astype(BF16)
        merged_ref[...] = merged_b
        xhat, rstd = _normalize(ALPHA * x_ref[...] + _dot(merged_b, wo_ref[...], NN))
        xhat_ref[...] = xhat
        rstd_ref[...] = rstd

    act = jax.ShapeDtypeStruct((s, D_MODEL), F32)
    return pl.pallas_call(
        body, name=name, grid=(s // t,),
        out_shape=(jax.ShapeDtypeStruct((s, W_D), F32), act, jax.ShapeDtypeStruct((s, 1), F32)) + (jax.ShapeDtypeStruct((s, D_MODEL), BF16),) * 4,
        in_specs=[row(D_MODEL)] * 4 + [_resident_rows(wi_t, W_A + W_B + W_C, W_D)] + [_resident((D_MODEL, D_MODEL))] * 4,
        out_specs=(row(W_D), row(D_MODEL), row(1), row(D_MODEL), row(D_MODEL), row(D_MODEL), row(D_MODEL)),
        compiler_params=_params(("parallel",)),
    )(o_a, o_b, o_c, x, wi_t, *wbr, wo)


def _merge_bwd(dpre1, zd, pa, pb, pc, o_a, o_b, o_c, merged, wbr, wo, *, name, exchanges=()):
    s = dpre1.shape[0]
    t = min(256, s)
    nt = s // t
    row = lambda w: pl.BlockSpec((t, w), lambda i: (i, 0))

    def body(dpre_ref, zd_ref, pa_ref, pb_ref, pc_ref, oa_ref, ob_ref, oc_ref, mg_ref, wa_ref, wb_ref, wc_ref, wo_ref,
             dzd_ref, doa_ref, dob_ref, doc_ref, gwa_ref, gwb_ref, gwc_ref, gwo_ref, acc):
        step = pl.program_id(0)

        @pl.when(step == 0)
        def _():
            acc[...] = jnp.zeros_like(acc)

        dpre_b = dpre_ref[...].astype(BF16)
        dmerged = _dot(dpre_b, wo_ref[...], NT)
        acc[3] += _dot(mg_ref[...], dpre_b, TN)
        branches = ((pa_ref, oa_ref, doa_ref), (pb_ref, ob_ref, dob_ref), (pc_ref, oc_ref, doc_ref))
        for b, (p_ref, o_ref, do_ref) in enumerate(branches):
            gate = jax.nn.sigmoid(zd_ref[:, b * D_MODEL:(b + 1) * D_MODEL])
            dzd_ref[:, b * D_MODEL:(b + 1) * D_MODEL] = (dmerged * p_ref[...] * gate * (1.0 - gate)).astype(BF16)
            dp = (dmerged * gate).astype(BF16)
            acc[b] += _dot(o_ref[...], dp, TN)
            do_ref[...] = _dot(dp, (wa_ref, wb_ref, wc_ref)[b][...], NT).astype(do_ref.dtype)

        @pl.when(step == nt - 1)
        def _():
            for b, gw_ref in enumerate((gwa_ref, gwb_ref, gwc_ref, gwo_ref)):
                pltpu.sync_copy(acc.at[b], gw_ref)

    act = jax.ShapeDtypeStruct((s, D_MODEL), F32)
    actb = jax.ShapeDtypeStruct((s, D_MODEL), BF16)
    gw = jax.ShapeDtypeStruct((D_MODEL, D_MODEL), F32)
    return _fused_call(
        body, name=name, grid=(nt,),
        out_shape=(jax.ShapeDtypeStruct((s, W_D), BF16), act, actb, actb, gw, gw, gw, gw),
        in_specs=[row(D_MODEL), row(W_D)] + [row(D_MODEL)] * 7 + [_resident((D_MODEL, D_MODEL))] * 4,
        out_specs=(row(W_D),) + (row(D_MODEL),) * 3 + (HBM,) * 4, scratch_shapes=[pltpu.VMEM((4, D_MODEL, D_MODEL), F32)],
        operands=[dpre1, zd, pa, pb, pc, o_a, o_b, o_c, merged, *wbr, wo], exchanges=exchanges)


def _mlp_loss(xhat1, rstd1, target, ln1_g, ln1_b, ln2_g, ln2_b, wu, wd, *, name):
    s = xhat1.shape[0]
    t = min(256, s)
    npan = wu.shape[0]
    row = lambda w: pl.BlockSpec((t, w), lambda i: (i, 0))
    vec = _resident((1, D_MODEL))

    def body(xhat_ref, rstd_ref, tgt_ref, g1_ref, b1_ref, g2_ref, b2_ref, wu_ref, wd_ref,
             dpre1_ref, dpre2_ref, h1_ref, a_ref, du_ref, stats_ref):
        @pl.when(pl.program_id(0) == 0)
        def _():
            stats_ref[...] = jnp.zeros_like(stats_ref)

        xhat1_v = xhat_ref[...]
        h1 = xhat1_v * g1_ref[...] + b1_ref[...]
        h1_b = h1.astype(BF16)
        h1_ref[...] = h1_b
        us = []
        ff = jnp.zeros((t, D_MODEL), F32)
        for j in range(npan):
            u = _dot(h1_b, wu_ref[j], NN)
            us.append(u)
            r = jnp.maximum(u, 0.0)
            a_b = (r * r).astype(BF16)
            a_ref[:, j * D_MODEL:(j + 1) * D_MODEL] = a_b
            ff = ff + _dot(a_b, wd_ref[j], NN)
        xhat2, rstd2 = _normalize(ALPHA * h1 + ff)
        err = xhat2 * g2_ref[...] + b2_ref[...] - tgt_ref[...]
        stats_ref[4:5, :] += jnp.sum(err * err, axis=0, keepdims=True)
        dpre2, dg2, db2 = _layer_norm_bwd(err * (1.0 / D_MODEL), xhat2, rstd2, g2_ref[...])
        stats_ref[0:1, :] += dg2
        stats_ref[1:2, :] += db2
        dpre2_b = dpre2.astype(BF16)
        dpre2_ref[...] = dpre2_b
        dh1 = ALPHA * dpre2
        for j in range(npan):
            du_b = (_dot(dpre2_b, wd_ref[j], NT) * (2.0 * jnp.maximum(us[j], 0.0))).astype(BF16)
            du_ref[:, j * D_MODEL:(j + 1) * D_MODEL] = du_b
            dh1 = dh1 + _dot(du_b, wu_ref[j], NT)
        dpre1, dg1, db1 = _layer_norm_bwd(dh1, xhat1_v, rstd_ref[...], g1_ref[...])
        stats_ref[2:3, :] += dg1
        stats_ref[3:4, :] += db1
        dpre1_ref[...] = dpre1

    actb = jax.ShapeDtypeStruct((s, D_MODEL), BF16)
    wide = jax.ShapeDtypeStruct((s, D_FF), BF16)
    return pl.pallas_call(
        body, name=name, grid=(s // t,),
        out_shape=(jax.ShapeDtypeStruct((s, D_MODEL), F32), actb, actb, wide, wide, jax.ShapeDtypeStruct((8, D_MODEL), F32)),
        in_specs=[row(D_MODEL), row(1), row(D_MODEL), vec, vec, vec, vec,
                  _resident((npan, D_MODEL, D_MODEL)), _resident((npan, D_MODEL, D_MODEL))],
        out_specs=(row(D_MODEL), row(D_MODEL), row(D_MODEL), row(D_FF), row(D_FF), pl.BlockSpec((8, D_MODEL), lambda i: (0, 0))),
        compiler_params=_params(("arbitrary",)),
    )(xhat1, rstd1, target, ln1_g, ln1_b, ln2_g, ln2_b, wu, wd)


BRANCH_WEIGHTS = ("w_branch_hg", "w_branch_swa", "w_branch_mem")


def _local_step(x, xb, mem, target, wi_t, wmkv, late, lb_logits, gain, sinks, rel_bias, ln1_g, ln1_b, ln2_g, ln2_b, *, distributed):
    s = x.shape[0]
    tm = min(1024, s)
    tk = min(2048, s)
    memb = mem.astype(BF16)
    if distributed:
        cx, cy, cc = lax.axis_index("x"), lax.axis_index("y"), lax.axis_index("c")
        pos = jnp.stack([2 * cx + cy, cc]).astype(jnp.int32)
    gather = (lambda names: [_gather_exchange([late[k] for k in names])]) if distributed else (lambda names: [])
    to_sibling = (lambda grads: [_sibling_halves_exchange(grads)]) if distributed else (lambda grads: [])
    to_chips = (lambda sums: [_chip_partials_exchange([bf for bf, _ in sums])]) if distributed else (lambda sums: [])

    def chip_sums(names, grads, from_sibling):
        return [_add_sibling(g, o, pos, name="add_sibling_" + k) for k, g, o in zip(names, grads, from_sibling)]

    def shard_sums(names, sums, from_chips):
        return {k: _add_chips(mine, o, pos, name="add_chips_" + k) for k, (_, mine), o in zip(names, sums, from_chips)}

    zb = _mm(xb, wi_t, mode="nt", tm=tm, tn=W_B, tk=D_MODEL, name="proj_b", out_dtype=BF16, b_rows=(W_A, W_B))
    mkv = _mm(memb, wmkv, mode="nn", tm=MEM_LEN, tn=512, tk=D_MODEL, name="mem_kv", out_dtype=BF16, b_panels=True)
    onehot, maskrow = _bias_selector()
    bias_tables, sink_lanes = _swa_tables(_bias_table(rel_bias.T, onehot, maskrow, name="bias_table"), sinks)
    (za, o_a, o_raw, states), landed = _hgrn_fwd(xb, wi_t, lb_logits, gain, name="hgrn_fwd", exchanges=gather(("w_up", "w_down")))
    wu, wd = landed[0] if distributed else (late["wu"], late["wd"])
    o_b, landed = _swa_fwd(zb, bias_tables, sink_lanes, name="swa_fwd", exchanges=gather(BRANCH_WEIGHTS + ("w_out",)))
    if distributed:
        wbr = [wb.reshape(D_MODEL, D_MODEL) for wb in landed[0][:3]]
        wo = landed[0][3].reshape(D_MODEL, D_MODEL)
    else:
        wbr, wo = [late["wbr"][b] for b in range(3)], late["wo"]
    zc, o_c = _mem_fwd(xb, wi_t, mkv, name="mem_fwd")
    zd, xhat1, rstd1, merged, pa, pb, pc = _merge_fwd(o_a, o_b, o_c, x, wi_t, wbr, wo, name="merge_fwd")

    dpre1, dpre2, h1, act, du, ln_stats = _mlp_loss(xhat1, rstd1, target, ln1_g, ln1_b, ln2_g, ln2_b, wu, wd, name="mlp_loss")
    ffn = ("w_down", "w_up")
    g_ffn = [_mm(act, dpre2, mode="tn", tm=1024, tn=D_MODEL, tk=tk, name="grad_w_down").reshape(N_SHARDS, D_FF // N_SHARDS, D_MODEL),
             _mm(h1, du, mode="tn", tm=D_MODEL, tn=1024, tk=tk, name="grad_w_up", out_panels=True)]

    (dzd, do_a, do_b, do_c, *g_merge), landed = _merge_bwd(dpre1, zd, pa, pb, pc, o_a, o_b, o_c, merged, wbr, wo, name="merge_bwd",
                                                           exchanges=to_sibling(g_ffn))
    sums_ffn = chip_sums(ffn, g_ffn, landed[0]) if distributed else []
    merge = BRANCH_WEIGHTS + ("w_out",)
    g_merge = [g.reshape(N_SHARDS, D_MODEL // N_SHARDS, D_MODEL) for g in g_merge]
    (dza, hg_stats), landed = _hgrn_bwd(za, o_raw, do_a, states, lb_logits, gain, name="hgrn_bwd",
                                        exchanges=to_chips(sums_ffn) + to_sibling(g_merge))
    halves = shard_sums(ffn, sums_ffn, landed[0]) if distributed else {}
    sums_merge = chip_sums(merge, g_merge, landed[1]) if distributed else []
    (dzb, dbias_t, dsinks), landed = _swa_bwd(zb, do_b, bias_tables, sink_lanes, name="swa_bwd", exchanges=to_chips(sums_merge))
    if distributed:
        halves.update(shard_sums(merge, sums_merge, landed[0]))
    dbias = dbias_t.reshape(2 * SWA_BLOCK, SWA_HEADS, SWA_BLOCK).transpose(1, 2, 0).reshape(SWA_HEADS, -1)
    d_rel_bias = _bias_grad(dbias, onehot, name="bias_grad").T
    dzc, dmkv, g_wi = _mem_bwd(xb, zc, do_c, mkv, name="mem_bwd")

    proj = ("w_in", "w_mem_kv")
    for dz, offset, nm in ((dza, 0, "grad_w_in_a"), (dzb, W_A, "grad_w_in_b"), (dzd, W_A + W_B + W_C, "grad_w_in_d")):
        g_wi = _mm(dz, xb, mode="tn", tm=dz.shape[1] if dz.shape[1] <= 1280 else 1024, tn=D_MODEL, tk=tk, name=nm,
                   rows_of=IN_COLS, row_offset=offset, into=g_wi)
    g_proj = [g_wi.reshape(N_SHARDS, IN_COLS // N_SHARDS, D_MODEL),
              _mm(memb, dmkv, mode="tn", tm=D_MODEL, tn=512, tk=MEM_LEN, name="grad_w_mem_kv", out_panels=True)]
    sums_proj = chip_sums(proj, g_proj, _run_exchanges(to_sibling(g_proj), name="reduce_sibling_proj")[0]) if distributed else []
    small = dict(lb_logits=hg_stats[1:3], hg_norm_gain=hg_stats[0:1], swa_sinks=dsinks, rel_bias=d_rel_bias,
                 ln1_g=ln_stats[2:3], ln1_b=ln_stats[3:4], ln2_g=ln_stats[0:1], ln2_b=ln_stats[1:2], sq_err=ln_stats[4:5])
    small_exchange = [_small_gather_exchange(_pack_small(small, name="pack_small"))] if distributed else []
    grad_x, landed = _dx_matmul([dza, dzb, dzc, dzd], wi_t, dpre1, tm=min(512, s), name="grad_x", exchanges=to_chips(sums_proj) + small_exchange)
    if distributed:
        halves.update(shard_sums(proj, sums_proj, landed[0]))
        small = landed[1][0]
    else:
        halves = dict(zip(ffn + merge + proj, g_ffn + g_merge + g_proj))
    return grad_x, halves, small


def _mesh_position():
    x, y, c = lax.axis_index("x"), lax.axis_index("y"), lax.axis_index("c")
    chips = [(1 - x, y), (x, 1 - y), (1 - x, 1 - y)]
    return x, y, c, chips


class _Exchange(NamedTuple):
    operands: list
    out_shapes: list
    n_sems: int
    start: Callable
    finish: Callable
    halfway: Optional[Callable] = None


def _gather_exchange(shards):
    n = len(shards)
    per = 9
    assert all(w.shape[0] % (4 * BF16_SUBLANES) == 0 for w in shards)

    def plan(ins, outs, send_sems, recv_sems):
        x, y, c, (x_nbr, y_nbr, diag) = _mesh_position()
        sibling = (x, y, 1 - c)
        slot = lambda chip: 2 * chip[0] + chip[1]

        def rows(a, chip, hc, quarter=None):
            rh = shards[a].shape[0] // 2
            if quarter is None:
                return outs[a].at[slot(chip), pl.ds(hc * rh, rh), :]
            return outs[a].at[slot(chip), pl.ds(hc * rh + quarter * (rh // 2), rh // 2), :]

        def copy(a, k, src, dst, to):
            return pltpu.make_async_remote_copy(src_ref=src, dst_ref=dst, send_sem=send_sems.at[a * per + k], recv_sem=recv_sems.at[a * per + k],
                                                device_id=to, device_id_type=MESH)

        first, from_sibling = [], []
        landed, then = [[] for _ in range(4)], [[] for _ in range(4)]
        for a in range(n):
            rh = shards[a].shape[0] // 2
            my_half = ins[a].at[pl.ds(c * rh, rh), :]
            first += [copy(a, 4, ins[a], outs[a].at[slot((x, y))], sibling),
                      copy(a, 0, my_half, rows(a, (x, y), c), (*x_nbr, c)), copy(a, 1, my_half, rows(a, (x, y), c), (*y_nbr, c))]
            landed[0].append(copy(a, 0, rows(a, x_nbr, c), rows(a, x_nbr, c), (*x_nbr, c)))
            then[0].append([copy(a, 2, rows(a, x_nbr, c, 0), rows(a, x_nbr, c, 0), (*y_nbr, c)), copy(a, 5, rows(a, x_nbr, c), rows(a, x_nbr, c), sibling)])
            landed[1].append(copy(a, 1, rows(a, y_nbr, c), rows(a, y_nbr, c), (*y_nbr, c)))
            then[1].append([copy(a, 3, rows(a, y_nbr, c, 1), rows(a, y_nbr, c, 1), (*x_nbr, c)), copy(a, 6, rows(a, y_nbr, c), rows(a, y_nbr, c), sibling)])
            landed[2].append(copy(a, 2, rows(a, diag, c, 0), rows(a, diag, c, 0), (*y_nbr, c)))
            then[2].append([copy(a, 7, rows(a, diag, c, 0), rows(a, diag, c, 0), sibling)])
            landed[3].append(copy(a, 3, rows(a, diag, c, 1), rows(a, diag, c, 1), (*x_nbr, c)))
            then[3].append([copy(a, 8, rows(a, diag, c, 1), rows(a, diag, c, 1), sibling)])
            from_sibling += [copy(a, 4, outs[a].at[slot((x, y))], outs[a].at[slot((x, y))], sibling),
                             copy(a, 5, rows(a, x_nbr, 1 - c), rows(a, x_nbr, 1 - c), sibling), copy(a, 6, rows(a, y_nbr, 1 - c), rows(a, y_nbr, 1 - c), sibling),
                             copy(a, 7, rows(a, diag, 1 - c, 0), rows(a, diag, 1 - c, 0), sibling), copy(a, 8, rows(a, diag, 1 - c, 1), rows(a, diag, 1 - c, 1), sibling)]
        return first, landed, then, from_sibling

    def start(*refs):
        first, _, _, _ = plan(*refs)
        for cp in first:
            cp.start()

    def stages(landed, then, which):
        for stage in which:
            for arrival, onward in zip(landed[stage], then[stage]):
                arrival.wait_recv()
                for cp in onward:
                    cp.start()

    def halfway(*refs):
        _, landed, then, _ = plan(*refs)
        stages(landed, then, (0, 1))

    def finish(*refs):
        first, landed, then, from_sibling = plan(*refs)
        stages(landed, then, (2, 3))
        for cp in from_sibling:
            cp.wait_recv()
        for cp in first + [cp for stage in then for onward in stage for cp in onward]:
            cp.wait_send()

    return _Exchange(list(shards), [jax.ShapeDtypeStruct((N_SHARDS,) + w.shape, w.dtype) for w in shards], per * n, start, finish, halfway)


def _sibling_halves_exchange(grads):
    n = len(grads)

    def plan(ins, outs, send_sems, recv_sems):
        x, y, c, _ = _mesh_position()
        return [pltpu.make_async_remote_copy(src_ref=ins[a].at[:, pl.ds((1 - c) * (grads[a].shape[1] // 2), grads[a].shape[1] // 2), :],
                                             dst_ref=outs[a], send_sem=send_sems.at[a], recv_sem=recv_sems.at[a],
                                             device_id=(x, y, 1 - c), device_id_type=MESH) for a in range(n)]

    def start(*refs):
        for cp in plan(*refs):
            cp.start()

    def finish(*refs):
        for cp in plan(*refs):
            cp.wait()

    return _Exchange(list(grads), [jax.ShapeDtypeStruct((g.shape[0], g.shape[1] // 2, g.shape[2]), g.dtype) for g in grads], n, start, finish)


def _chip_partials_exchange(sums):
    n = len(sums)

    def plan(ins, outs, send_sems, recv_sems):
        _, _, c, chips = _mesh_position()
        return [pltpu.make_async_remote_copy(src_ref=ins[a].at[2 * cx + cy], dst_ref=outs[a].at[k], send_sem=send_sems.at[a * 3 + k],
                                             recv_sem=recv_sems.at[a * 3 + k], device_id=(cx, cy, c), device_id_type=MESH)
                for k, (cx, cy) in enumerate(chips) for a in range(n)]

    def start(*refs):
        for cp in plan(*refs):
            cp.start()

    def finish(*refs):
        for cp in plan(*refs):
            cp.wait()

    return _Exchange(list(sums), [jax.ShapeDtypeStruct((3,) + g.shape[1:], g.dtype) for g in sums], 3 * n, start, finish)


def _fused_call(body, *, name, grid, in_specs, out_specs, out_shape, scratch_shapes, operands, exchanges=()):
    single = not isinstance(out_shape, (tuple, list))
    out_specs = [out_specs] if single else list(out_specs)
    out_shape = [out_shape] if single else list(out_shape)
    n_in, n_out, n_scr = len(in_specs), len(out_specs), len(scratch_shapes)
    x_in = [len(e.operands) for e in exchanges]
    x_out = [len(e.out_shapes) for e in exchanges]

    def wrapped(*refs):
        refs = list(refs)
        ins = refs[:n_in]
        pos = n_in
        ex_ins = []
        for k in x_in:
            ex_ins.append(refs[pos:pos + k])
            pos += k
        outs = refs[pos:pos + n_out]
        pos += n_out
        ex_outs = []
        for k in x_out:
            ex_outs.append(refs[pos:pos + k])
            pos += k
        scratch = refs[pos:pos + n_scr]
        sems = refs[pos + n_scr:]
        first, last, middle = None, None, None
        for axis, size in enumerate(grid):
            at_start, at_end, at_middle = pl.program_id(axis) == 0, pl.program_id(axis) == size - 1, pl.program_id(axis) == size // 2
            first = at_start if first is None else first & at_start
            last = at_end if last is None else last & at_end
            middle = at_middle if middle is None else middle & at_middle

        @pl.when(first)
        def _():
            for i, e in enumerate(exchanges):
                e.start(ex_ins[i], ex_outs[i], sems[2 * i], sems[2 * i + 1])

        if any(e.halfway for e in exchanges):
            @pl.when(middle)
            def _():
                for i, e in enumerate(exchanges):
                    if e.halfway:
                        e.halfway(ex_ins[i], ex_outs[i], sems[2 * i], sems[2 * i + 1])

        body(*ins, *outs, *scratch)

        @pl.when(last)
        def _():
            for i, e in enumerate(exchanges):
                e.finish(ex_ins[i], ex_outs[i], sems[2 * i], sems[2 * i + 1])

    n_x_in, n_x_out = sum(x_in), sum(x_out)
    results = pl.pallas_call(
        wrapped if exchanges else body, name=name, grid=grid,
        in_specs=list(in_specs) + [HBM] * n_x_in,
        out_specs=out_specs + [HBM] * n_x_out,
        out_shape=out_shape + [s for e in exchanges for s in e.out_shapes],
        scratch_shapes=list(scratch_shapes) + [pltpu.SemaphoreType.DMA((e.n_sems,)) for e in exchanges for _ in range(2)],
        compiler_params=_params(("arbitrary",) * len(grid)),
    )(*operands, *[a for e in exchanges for a in e.operands])
    own = results[0] if single else tuple(results[:n_out])
    landed, pos = [], n_out
    for k in x_out:
        landed.append(list(results[pos:pos + k]))
        pos += k
    return own, landed


def _cast_bf16(x, *, name, exchanges=()):
    s, cols = x.shape
    t = min(512, s)

    def body(x_ref, o_ref):
        o_ref[...] = x_ref[...].astype(BF16)

    tile = pl.BlockSpec((t, cols), lambda i: (i, 0))
    return _fused_call(body, name=name, grid=(s // t,), in_specs=[tile], out_specs=tile, out_shape=jax.ShapeDtypeStruct((s, cols), BF16),
                       scratch_shapes=[], operands=[x], exchanges=exchanges)


def _run_exchanges(exchanges, *, name):
    def body(*refs):
        n_in = sum(len(e.operands) for e in exchanges)
        n_out = sum(len(e.out_shapes) for e in exchanges)
        ins, outs, sems = refs[:n_in], refs[n_in:n_in + n_out], refs[n_in + n_out:]
        spans, i, o = [], 0, 0
        for e in exchanges:
            spans.append((ins[i:i + len(e.operands)], outs[o:o + len(e.out_shapes)]))
            i, o = i + len(e.operands), o + len(e.out_shapes)
        for k, e in enumerate(exchanges):
            e.start(*spans[k], sems[2 * k], sems[2 * k + 1])
        for k, e in enumerate(exchanges):
            if e.halfway:
                e.halfway(*spans[k], sems[2 * k], sems[2 * k + 1])
        for k, e in enumerate(exchanges):
            e.finish(*spans[k], sems[2 * k], sems[2 * k + 1])

    operands = [a for e in exchanges for a in e.operands]
    shapes = [s for e in exchanges for s in e.out_shapes]
    results = pl.pallas_call(
        body, name=name, out_shape=shapes, in_specs=[HBM] * len(operands), out_specs=[HBM] * len(shapes),
        scratch_shapes=[pltpu.SemaphoreType.DMA((e.n_sems,)) for e in exchanges for _ in range(2)],
    )(*operands)
    landed, pos = [], 0
    for e in exchanges:
        landed.append(list(results[pos:pos + len(e.out_shapes)]))
        pos += len(e.out_shapes)
    return landed


ROW_TILE_MAX = 640
BF16_SUBLANES = 16


def _row_tile(rows):
    for tr in range(min(rows, ROW_TILE_MAX), 0, -1):
        if rows % tr == 0 and tr % BF16_SUBLANES == 0:
            return tr
    raise ValueError(rows)


def _add_sibling(grad, other, pos, *, name):
    p, r, cols = grad.shape
    rh = r // 2
    tr = _row_tile(rh)
    nb = rh // tr

    def body(pos_ref, g_ref, o_ref, sb_ref, mine_ref):
        total = g_ref[...] + o_ref[...]
        sb_ref[...] = total.astype(BF16)

        @pl.when(pl.program_id(1) == pos_ref[0])
        def _():
            mine_ref[...] = total

    return pl.pallas_call(
        body, name=name, out_shape=(jax.ShapeDtypeStruct((p, rh, cols), BF16), jax.ShapeDtypeStruct((rh, cols), F32)),
        grid_spec=pltpu.PrefetchScalarGridSpec(
            num_scalar_prefetch=1, grid=(nb, p),
            in_specs=[pl.BlockSpec((None, tr, cols), lambda i, j, pos_ref: (j, pos_ref[1] * nb + i, 0)),
                      pl.BlockSpec((None, tr, cols), lambda i, j, pos_ref: (j, i, 0))],
            out_specs=(pl.BlockSpec((None, tr, cols), lambda i, j, pos_ref: (j, i, 0)),
                       pl.BlockSpec((tr, cols), lambda i, j, pos_ref: (i, 0)))),
        compiler_params=_params(("parallel", "arbitrary")),
    )(pos, grad, other)


def _add_chips(mine, others, pos, *, name):
    rh, cols = mine.shape
    tr = _row_tile(rh)
    nb = rh // tr

    def body(pos_ref, s_ref, o_ref, r_ref):
        r_ref[...] = ((s_ref[...] + o_ref[0].astype(F32)) + o_ref[1].astype(F32)) + o_ref[2].astype(F32)

    return pl.pallas_call(
        body, name=name, out_shape=jax.ShapeDtypeStruct((2 * rh, cols), F32),
        grid_spec=pltpu.PrefetchScalarGridSpec(
            num_scalar_prefetch=1, grid=(nb,),
            in_specs=[pl.BlockSpec((tr, cols), lambda i, pos_ref: (i, 0)),
                      pl.BlockSpec((3, tr, cols), lambda i, pos_ref: (0, i, 0))],
            out_specs=pl.BlockSpec((tr, cols), lambda i, pos_ref: (pos_ref[1] * nb + i, 0))),
        compiler_params=_params(("parallel",)),
    )(pos, mine, others)


def _join_halves(bufs, *, name):
    n = len(bufs)

    def body(*refs):
        ins, outs = refs[:n], refs[n:2 * n]
        send_sems, recv_sems = refs[2 * n:]
        x, y, c, _ = _mesh_position()

        def copy(a, hc):
            rh = bufs[a].shape[0] // 2
            rows = pl.ds(hc * rh, rh)
            return pltpu.make_async_remote_copy(src_ref=ins[a].at[rows, :], dst_ref=outs[a].at[rows, :], send_sem=send_sems.at[a],
                                                recv_sem=recv_sems.at[a], device_id=(x, y, 1 - c), device_id_type=MESH)

        for a in range(n):
            copy(a, c).start()
        for a in range(n):
            copy(a, c).wait_send()
            copy(a, 1 - c).wait_recv()

    return pl.pallas_call(
        body, name=name, out_shape=[jax.ShapeDtypeStruct(b.shape, b.dtype) for b in bufs],
        in_specs=[HBM] * n, out_specs=[HBM] * n, input_output_aliases={a: a for a in range(n)},
        scratch_shapes=[pltpu.SemaphoreType.DMA((n,)), pltpu.SemaphoreType.DMA((n,))],
    )(*bufs)


SMALL = ["lb_logits", "hg_norm_gain", "swa_sinks", "rel_bias", "ln1_g", "ln1_b", "ln2_g", "ln2_b"]
PACK_ROWS = 48
PACK_AT = dict(lb_logits=(slice(0, 2), slice(0, D_MODEL)), hg_norm_gain=(slice(2, 3), slice(0, D_MODEL)), ln1_g=(slice(3, 4), slice(0, D_MODEL)),
               ln1_b=(slice(4, 5), slice(0, D_MODEL)), ln2_g=(slice(5, 6), slice(0, D_MODEL)), ln2_b=(slice(6, 7), slice(0, D_MODEL)),
               swa_sinks=(slice(7, 8), slice(0, SWA_HEADS)), sq_err=(slice(8, 9), slice(0, D_MODEL)),
               rel_bias=(slice(16, 16 + NUM_BUCKETS), slice(0, SWA_HEADS)))


def _pack_small(grads, *, name):
    names = SMALL + ["sq_err"]

    def body(*refs):
        packed = refs[len(names)]
        packed[...] = jnp.zeros_like(packed)
        for k, g_ref in zip(names, refs):
            packed[PACK_AT[k]] = g_ref[...]

    return pl.pallas_call(body, name=name, out_shape=jax.ShapeDtypeStruct((PACK_ROWS, D_MODEL), F32), compiler_params=_params(),
                          )(*[grads[k] for k in names])


def _small_gather_exchange(packed):
    def plan(ins, outs, send_sems, recv_sems):
        x, y, c, _ = _mesh_position()
        me = 4 * x + 2 * y + c
        own = pltpu.make_async_copy(ins[0], outs[0].at[me], send_sems.at[7])
        remote = []
        for d in range(1, 8):
            dx, dy, dc = (d >> 2) & 1, (d >> 1) & 1, d & 1
            remote.append(pltpu.make_async_remote_copy(src_ref=ins[0], dst_ref=outs[0].at[me], send_sem=send_sems.at[d - 1],
                                                       recv_sem=recv_sems.at[d - 1], device_id=(x ^ dx, y ^ dy, c ^ dc), device_id_type=MESH))
        return own, remote

    def start(*refs):
        own, remote = plan(*refs)
        own.start()
        for cp in remote:
            cp.start()

    def finish(*refs):
        own, remote = plan(*refs)
        for cp in remote:
            cp.wait()
        own.wait()

    return _Exchange([packed], [jax.ShapeDtypeStruct((8,) + packed.shape, packed.dtype)], 8, start, finish)


def _adamw_small(gathered, w, m, v, *, name):
    names = SMALL
    n = len(names)

    def body(*refs):
        gathered_ref = refs[0]
        w_refs, m_refs, v_refs = (dict(zip(names, refs[1 + i * n:1 + (i + 1) * n])) for i in range(3))
        loss_ref = refs[1 + 3 * n]
        go_refs, d_refs, nm_refs, nv_refs = (dict(zip(names, refs[2 + (3 + i) * n:2 + (4 + i) * n])) for i in range(4))
        total_ref = refs[2 + 7 * n]
        total = gathered_ref[0]
        for j in range(1, 8):
            total = total + gathered_ref[j]
        total_ref[...] = total
        loss_ref[...] = (0.5 / D_MODEL) * jnp.sum(total_ref[PACK_AT["sq_err"]], axis=1, keepdims=True)
        for k in names:
            g = total_ref[PACK_AT[k]]
            go_refs[k][...] = g
            d_refs[k][...], nm_refs[k][...], nv_refs[k][...] = _adamw_math(w_refs[k][...], g, m_refs[k][...], v_refs[k][...])

    like = [jax.ShapeDtypeStruct(w[k].shape, F32) for k in names]
    results = pl.pallas_call(body, name=name, out_shape=[jax.ShapeDtypeStruct((1, 1), F32)] + like * 4,
                             scratch_shapes=[pltpu.VMEM((PACK_ROWS, D_MODEL), F32)],
                             compiler_params=_params())(gathered, *[d[k] for d in (w, m, v) for k in names])
    return results[0], {k: tuple(results[1 + i * n + j] for i in range(4)) for j, k in enumerate(names)}


def _adamw_math(w, g, m, v):
    m = ADAM_B1 * m + (1.0 - ADAM_B1) * g
    v = ADAM_B2 * v + (1.0 - ADAM_B2) * (g * g)
    m_hat = m / (1.0 - ADAM_B1 ** ADAM_STEP)
    v_hat = v / (1.0 - ADAM_B2 ** ADAM_STEP)
    delta = -ADAM_LR * (m_hat / (jnp.sqrt(v_hat) + ADAM_EPS) + ADAM_WD * w)
    return delta, m, v


def _adamw(w, g, m, v, *, name):
    _, rows, cols = w.shape
    tr = _row_tile(rows)
    blk = pl.BlockSpec((None, tr, cols), lambda i: (0, i, 0))
    flat = pl.BlockSpec((tr, cols), lambda i: (i, 0))

    def body(w_ref, g_ref, m_ref, v_ref, go_ref, d_ref, nm_ref, nv_ref):
        g_v = g_ref[...]
        go_ref[...] = g_v
        d_ref[...], nm_ref[...], nv_ref[...] = _adamw_math(w_ref[...], g_v, m_ref[...], v_ref[...])

    shape = jax.ShapeDtypeStruct((1, rows, cols), F32)
    return pl.pallas_call(body, name=name, grid=(rows // tr,), out_shape=(shape,) * 4, in_specs=[blk, flat, blk, blk], out_specs=(blk,) * 4,
                          compiler_params=_params(("parallel",)))(w, g, m, v)


WEIGHTS = ["w_in", "lb_logits", "hg_norm_gain", "swa_sinks", "rel_bias", "w_mem_kv", "w_branch_hg", "w_branch_swa", "w_branch_mem",
           "w_out", "ln1_g", "ln1_b", "w_up", "w_down", "ln2_g", "ln2_b"]
BIG = ["w_in", "w_mem_kv", "w_branch_hg", "w_branch_swa", "w_branch_mem", "w_out", "w_up", "w_down"]


def kernel(x, mem, w_in, lb_logits, hg_norm_gain, swa_sinks, rel_bias, w_mem_kv, w_branch_hg, w_branch_swa, w_branch_mem, w_out, ln1_g, ln1_b, w_up, w_down, ln2_g, ln2_b, loss_target, m_w_in, m_lb_logits, m_hg_norm_gain, m_swa_sinks, m_rel_bias, m_w_mem_kv, m_w_branch_hg, m_w_branch_swa, m_w_branch_mem, m_w_out, m_ln1_g, m_ln1_b, m_w_up, m_w_down, m_ln2_g, m_ln2_b, v_w_in, v_lb_logits, v_hg_norm_gain, v_swa_sinks, v_rel_bias, v_w_mem_kv, v_w_branch_hg, v_w_branch_swa, v_w_branch_mem, v_w_out, v_ln1_g, v_ln1_b, v_w_up, v_w_down, v_ln2_g, v_ln2_b):
    w = dict(w_in=w_in, lb_logits=lb_logits, hg_norm_gain=hg_norm_gain, swa_sinks=swa_sinks, rel_bias=rel_bias, w_mem_kv=w_mem_kv,
             w_branch_hg=w_branch_hg, w_branch_swa=w_branch_swa, w_branch_mem=w_branch_mem, w_out=w_out, ln1_g=ln1_g, ln1_b=ln1_b,
             w_up=w_up, w_down=w_down, ln2_g=ln2_g, ln2_b=ln2_b)
    m = dict(w_in=m_w_in, lb_logits=m_lb_logits, hg_norm_gain=m_hg_norm_gain, swa_sinks=m_swa_sinks, rel_bias=m_rel_bias, w_mem_kv=m_w_mem_kv,
             w_branch_hg=m_w_branch_hg, w_branch_swa=m_w_branch_swa, w_branch_mem=m_w_branch_mem, w_out=m_w_out, ln1_g=m_ln1_g, ln1_b=m_ln1_b,
             w_up=m_w_up, w_down=m_w_down, ln2_g=m_ln2_g, ln2_b=m_ln2_b)
    v = dict(w_in=v_w_in, lb_logits=v_lb_logits, hg_norm_gain=v_hg_norm_gain, swa_sinks=v_swa_sinks, rel_bias=v_rel_bias, w_mem_kv=v_w_mem_kv,
             w_branch_hg=v_w_branch_hg, w_branch_swa=v_w_branch_swa, w_branch_mem=v_w_branch_mem, w_out=v_w_out, ln1_g=v_ln1_g, ln1_b=v_ln1_b,
             w_up=v_w_up, w_down=v_w_down, ln2_g=v_ln2_g, ln2_b=v_ln2_b)
    shapes = {k: w[k].shape for k in WEIGHTS}
    for d in (w, m, v):
        d["w_in"] = d["w_in"].reshape(D_MODEL, IN_COLS // N_SHARDS).T[None]
    shards = {k: w[k].reshape(w[k].shape[-2], w[k].shape[-1]).astype(BF16) for k in BIG}
    x2d = x.reshape(x.shape[-2], D_MODEL)
    xb, ((wi4, wmkv),) = _cast_bf16(x2d, name="gather_weights", exchanges=[_gather_exchange([shards["w_in"], shards["w_mem_kv"]])])
    wi_t = wi4.reshape(IN_COLS, D_MODEL)

    grad_x, halves, small = _local_step(
        x2d, xb, mem.reshape(MEM_LEN, D_MODEL), loss_target.reshape(loss_target.shape[-2], D_MODEL),
        wi_t, wmkv, shards, lb_logits, hg_norm_gain, swa_sinks, rel_bias, ln1_g, ln1_b, ln2_g, ln2_b, distributed=True)

    reduced = dict(zip(BIG, _join_halves([halves[k] for k in BIG], name="join_halves")))

    outs = {k: _adamw(w[k], reduced[k], m[k], v[k], name="adamw_" + k) for k in BIG}
    loss, small_outs = _adamw_small(small, w, m, v, name="adamw_small")
    outs.update(small_outs)
    grad_out, delta_out, m_out, v_out = ({k: outs[k][i] for k in WEIGHTS} for i in range(4))
    for out in (grad_out, delta_out, m_out, v_out):
        out["w_in"] = out["w_in"][0].T

    result = [loss.reshape(()), grad_x.reshape(x.shape)]
    for out in (grad_out, delta_out, m_out, v_out):
        result += [out[k].reshape(shapes[k]) for k in WEIGHTS]
    return tuple(result)
```

```python
import functools
import math
from typing import Callable, NamedTuple, Optional

import jax
import jax.numpy as jnp
from jax import lax
from jax.experimental import pallas as pl
from jax.experimental.pallas import tpu as pltpu

F32 = jnp.float32
BF16 = jnp.bfloat16
HIGHEST = lax.Precision.HIGHEST
MESH = pl.DeviceIdType.MESH

D_MODEL = 1024
MEM_LEN = 256
HG_HEADS = 8
HG_DK = 128
HG_CHUNK = 64
SWA_HEADS = 16
SWA_KV_HEADS = 2
SWA_GROUP = 8
SWA_HEAD_DIM = 64
SWA_BLOCK = 128
SWA_WINDOW = 128
MEM_HEADS = 4
MEM_HEAD_DIM = 256
NUM_BUCKETS = 32
MAX_DISTANCE = 128
D_FF = 4096
LN_EPS = 1e-5
RMS_EPS = 1e-6
ALPHA = 2.0 ** 0.25
W_A, W_B, W_C, W_D = 4096, 1280, 1024, 3072
IN_COLS = W_A + W_B + W_C + W_D
N_SHARDS = 4
ADAM_LR = 0.001
ADAM_B1 = 0.9
ADAM_B2 = 0.999
ADAM_EPS = 1e-08
ADAM_WD = 0.01
ADAM_STEP = 10
MASK_VALUE = -1e30
VMEM_LIMIT = 56 * 1024 * 1024

NN = ((1,), (0,))
NT = ((1,), (1,))
TN = ((0,), (0,))
HBM = pl.BlockSpec(memory_space=pltpu.HBM)


def _dot(a, b, dims=NN, precision=None):
    return lax.dot_general(a, b, (dims, ((), ())), precision=precision, preferred_element_type=F32)


def _params(sem=None):
    return pltpu.CompilerParams(dimension_semantics=sem, vmem_limit_bytes=VMEM_LIMIT)


def _resident(shape):
    zeros = (0,) * len(shape)
    return pl.BlockSpec(shape, lambda *_: zeros, pipeline_mode=pl.Buffered(1))


def _resident_rows(arr, offset, rows):
    return pl.BlockSpec((pl.Element(rows), pl.Element(arr.shape[1])), lambda *_: (offset, 0), pipeline_mode=pl.Buffered(1))


def _mm(a, b, *, mode, tm, tn, tk, name, out_dtype=F32, b_panels=False, b_rows=None, out_panels=False, rows_of=None, row_offset=0,
        into=None):
    if mode == "tn":
        kdim, m = a.shape
    else:
        m, kdim = a.shape
    if b_panels:
        n = b.shape[0] * b.shape[2]
        assert b.shape[2] == tn and mode == "nn"
    elif b_rows is not None:
        assert mode == "nt"
        b_offset, n = b_rows
    elif mode == "nt":
        n = b.shape[0]
    else:
        n = b.shape[1]
    assert m % tm == 0 and n % tn == 0 and kdim % tk == 0, (name, m, n, kdim)
    nk = kdim // tk
    dims = {"nn": NN, "nt": NT, "tn": TN}[mode]
    a_spec = pl.BlockSpec((tk, tm), lambda i, j, k: (k, i)) if mode == "tn" else pl.BlockSpec((tm, tk), lambda i, j, k: (i, k))
    if b_panels:
        b_spec = pl.BlockSpec((None, tk, tn), lambda i, j, k: (j, k, 0))
    elif b_rows is not None:
        assert b_offset % BF16_SUBLANES == 0 and tn % BF16_SUBLANES == 0 and tk % 128 == 0
        b_spec = pl.BlockSpec((pl.Element(tn), pl.Element(tk)),
                              lambda i, j, k: (pl.multiple_of(b_offset + j * tn, BF16_SUBLANES), pl.multiple_of(k * tk, 128)))
    elif mode == "nt":
        b_spec = pl.BlockSpec((tn, tk), lambda i, j, k: (j, k))
    else:
        b_spec = pl.BlockSpec((tk, tn), lambda i, j, k: (k, j))
    in_specs = [a_spec, b_spec]
    operands = [a, b]
    aliases = {}
    if out_panels:
        out_shape = jax.ShapeDtypeStruct((n // tn, m, tn), out_dtype)
        o_spec = pl.BlockSpec((None, tm, tn), lambda i, j, k: (j, i, 0))
    elif rows_of is not None:
        out_shape = jax.ShapeDtypeStruct((rows_of, n), out_dtype)
        assert row_offset % BF16_SUBLANES == 0 and tm % BF16_SUBLANES == 0 and tn % 128 == 0
        o_spec = pl.BlockSpec((pl.Element(tm), pl.Element(tn)),
                              lambda i, j, k: (pl.multiple_of(row_offset + i * tm, BF16_SUBLANES), pl.multiple_of(j * tn, 128)))
        if into is not None:
            in_specs.append(pl.BlockSpec(memory_space=pl.ANY))
            operands.append(into)
            aliases = {2: 0}
    else:
        out_shape = jax.ShapeDtypeStruct((m, n), out_dtype)
        o_spec = pl.BlockSpec((tm, tn), lambda i, j, k: (i, j))
    n_in = len(operands)

    def body(*refs):
        a_ref, b_ref, o_ref = refs[0], refs[1], refs[n_in]
        part = _dot(a_ref[...].astype(BF16), b_ref[...].astype(BF16), dims)

        def finish(acc):
            o_ref[...] = acc.astype(out_dtype)

        if nk == 1:
            finish(part)
        else:
            acc_ref = refs[-1]
            k = pl.program_id(2)

            @pl.when(k == 0)
            def _():
                acc_ref[...] = part

            @pl.when(k > 0)
            def _():
                acc_ref[...] += part

            @pl.when(k == nk - 1)
            def _():
                finish(acc_ref[...])

    return pl.pallas_call(
        body, name=name, out_shape=out_shape, grid=(m // tm, n // tn, nk), in_specs=in_specs, out_specs=o_spec,
        scratch_shapes=[pltpu.VMEM((tm, tn), F32)] if nk > 1 else [], input_output_aliases=aliases,
        compiler_params=_params(("parallel", "parallel", "arbitrary")),
    )(*operands)


def _dx_matmul(dzs, wi_t, resid, *, tm, tiles, name, exchanges=()):
    first_tile, count = tiles
    npieces = len(dzs)
    offsets = [sum(dz.shape[1] for dz in dzs[:p]) for p in range(npieces)]
    tile = lambda i: (i + first_tile, 0)
    in_specs = [pl.BlockSpec((tm, dz.shape[1]), tile) for dz in dzs] + [_resident(wi_t.shape), pl.BlockSpec((tm, D_MODEL), tile)]

    def body(*refs):
        dz_refs, w_ref, r_ref, o_ref = refs[:npieces], refs[npieces], refs[npieces + 1], refs[npieces + 2]
        total = ALPHA * r_ref[...]
        for p in range(npieces):
            total = total + _dot(dz_refs[p][...], w_ref[offsets[p]:offsets[p] + dzs[p].shape[1], :], NN)
        o_ref[...] = total

    return _fused_call(
        body, name=name, out_shape=jax.ShapeDtypeStruct((count * tm, D_MODEL), F32), grid=(count,), in_specs=in_specs,
        out_specs=pl.BlockSpec((tm, D_MODEL), lambda i: (i, 0)), scratch_shapes=[], operands=[*dzs, wi_t, resid], exchanges=exchanges)


def _lower_bound(lbl_ref):
    l0, l1 = lbl_ref[0:1, :], lbl_ref[1:2, :]
    mx = jnp.maximum(l0, l1)
    e0, e1 = jnp.exp(l0 - mx), jnp.exp(l1 - mx)
    return e0 / (e0 + e1)


HEAD_COLS = [slice(h * HG_DK, (h + 1) * HG_DK) for h in range(HG_HEADS)]


def _head_mean(x):
    return jnp.concatenate([jnp.broadcast_to(jnp.mean(x[:, c], axis=-1, keepdims=True), (x.shape[0], HG_DK)) for c in HEAD_COLS], axis=1)


def _triangle_sum(tri_b, x):
    p0 = x.astype(BF16)
    r1 = x - p0.astype(F32)
    p1 = r1.astype(BF16)
    p2 = (r1 - p1.astype(F32)).astype(BF16)
    return _dot(tri_b, p0) + _dot(tri_b, p1) + _dot(tri_b, p2)


def _chunk_forward(q, fl, v, lb, tril_b):
    sg = jax.nn.sigmoid(fl)
    f = lb + (1.0 - lb) * sg
    k = 1.0 - f
    b = _triangle_sum(tril_b, jnp.log(f))
    b_last = b[HG_CHUNK - 1:HG_CHUNK, :]
    eb, enb, eo = jnp.exp(b), jnp.exp(-b), jnp.exp(b_last - b)
    return sg, f, k, b_last, eb, enb, eo, q * eb, k * enb, k * eo


def _hgrn_fwd(xb, wi_t, lb_logits, gain, *, name, exchanges=()):
    s = xb.shape[0]
    t = min(256, s)
    ncs = t // HG_CHUNK

    def body(x_ref, w_ref, lbl_ref, gain_ref, z_ref, oa_ref, oraw_ref, st_ref, state):
        @pl.when(pl.program_id(0) == 0)
        def _():
            state[...] = jnp.zeros_like(state)

        z_ref[...] = _dot(x_ref[...], w_ref[...], NT)
        lb_all = _lower_bound(lbl_ref)
        row = lax.broadcasted_iota(jnp.int32, (HG_CHUNK, HG_CHUNK), 0)
        col = lax.broadcasted_iota(jnp.int32, (HG_CHUNK, HG_CHUNK), 1)
        tril = row >= col
        tril_b = tril.astype(BF16)
        gain_all = gain_ref[...]

        def chunk(i, carry):
            r = pl.ds(pl.multiple_of(i * HG_CHUNK, HG_CHUNK), HG_CHUNK)
            q, fl, v, hg = (z_ref[r, j * D_MODEL:(j + 1) * D_MODEL] for j in range(4))
            _, _, _, b_last, _, _, _, q_in, k_in, k_out = _chunk_forward(q, fl, v, lb_all, tril_b)
            q_in_b, k_in_b, k_out_b, vb = (u.astype(BF16) for u in (q_in, k_in, k_out, v))
            decay = jnp.exp(b_last)
            sts = [state[h] for h in range(HG_HEADS)]
            attn = [_dot(q_in_b[:, c], k_in_b[:, c], NT) for c in HEAD_COLS]
            inter = [_dot(q_in_b[:, c], sts[h].astype(BF16), NT) for h, c in enumerate(HEAD_COLS)]
            upd = [_dot(vb[:, c], k_out_b[:, c], TN) for c in HEAD_COLS]
            attn = [jnp.where(tril, a, 0.0).astype(BF16) for a in attn]
            outs = [_dot(attn[h], vb[:, c], NN) + inter[h] for h, c in enumerate(HEAD_COLS)]
            for h, c in enumerate(HEAD_COLS):
                st_ref[h, i] = sts[h]
                state[h] = sts[h] * decay[:, c] + upd[h]
            o = jnp.concatenate(outs, axis=1)
            oraw_ref[r, :] = o
            n = o * lax.rsqrt(_head_mean(o * o) + RMS_EPS)
            oa_ref[r, :] = (n * gain_all * (hg * jax.nn.sigmoid(hg))).astype(BF16)
            return carry

        lax.fori_loop(0, ncs, chunk, 0, unroll=True)

    tile = lambda i: (i, 0)
    return _fused_call(
        body, name=name, grid=(s // t,),
        out_shape=(jax.ShapeDtypeStruct((s, W_A), F32), jax.ShapeDtypeStruct((s, D_MODEL), BF16), jax.ShapeDtypeStruct((s, D_MODEL), F32),
                   jax.ShapeDtypeStruct((HG_HEADS, s // HG_CHUNK, HG_DK, HG_DK), F32)),
        in_specs=[pl.BlockSpec((t, D_MODEL), tile), _resident_rows(wi_t, 0, W_A), _resident((2, D_MODEL)), _resident((1, D_MODEL))],
        out_specs=(pl.BlockSpec((t, W_A), tile), pl.BlockSpec((t, D_MODEL), tile), pl.BlockSpec((t, D_MODEL), tile),
                   pl.BlockSpec((HG_HEADS, ncs, HG_DK, HG_DK), lambda i: (0, i, 0, 0))),
        scratch_shapes=[pltpu.VMEM((HG_HEADS, HG_DK, HG_DK), F32)],
        operands=[xb, wi_t, lb_logits, gain], exchanges=exchanges)


def _hgrn_bwd(za, oraw, do_a, states, lb_logits, gain, *, name, exchanges=()):
    s = za.shape[0]
    t = min(256, s)
    ncs = t // HG_CHUNK
    nt = s // t

    def body(z_ref, oraw_ref, do_ref, st_ref, lbl_ref, gain_ref, dz_ref, stats_ref, dstate):
        step = pl.program_id(0)

        @pl.when(step == 0)
        def _():
            dstate[...] = jnp.zeros_like(dstate)
            stats_ref[...] = jnp.zeros_like(stats_ref)

        lb_all = _lower_bound(lbl_ref)
        row = lax.broadcasted_iota(jnp.int32, (HG_CHUNK, HG_CHUNK), 0)
        col = lax.broadcasted_iota(jnp.int32, (HG_CHUNK, HG_CHUNK), 1)
        tril = row >= col
        tril_b = tril.astype(BF16)
        triu_b = (row <= col).astype(BF16)
        gain_all = gain_ref[...]

        def chunk(ii, carry):
            i = ncs - 1 - ii
            r = pl.ds(pl.multiple_of(i * HG_CHUNK, HG_CHUNK), HG_CHUNK)
            q, fl, v, hg = (z_ref[r, j * D_MODEL:(j + 1) * D_MODEL] for j in range(4))
            o = oraw_ref[r, :]
            doa = do_ref[r, :]
            rms = lax.rsqrt(_head_mean(o * o) + RMS_EPS)
            n = o * rms
            sgg = jax.nn.sigmoid(hg)
            silu = hg * sgg
            dhg = doa * n * gain_all * (sgg * (1.0 + hg * (1.0 - sgg)))
            dgain = jnp.sum(doa * n * silu, axis=0, keepdims=True)
            dn = doa * gain_all * silu
            do = rms * (dn - n * _head_mean(dn * n))
            sg, f, k, b_last, eb, enb, eo, q_in, k_in, k_out = _chunk_forward(q, fl, v, lb_all, tril_b)
            q_in_b, k_in_b, k_out_b, vb, dob = (u.astype(BF16) for u in (q_in, k_in, k_out, v, do))
            decay = jnp.exp(b_last)
            sts = [st_ref[h, i] for h in range(HG_HEADS)]
            dsts = [dstate[h] for h in range(HG_HEADS)]
            dsts_b = [d.astype(BF16) for d in dsts]
            heads = list(enumerate(HEAD_COLS))
            attn = [_dot(q_in_b[:, c], k_in_b[:, c], NT) for h, c in heads]
            dattn = [_dot(dob[:, c], vb[:, c], NT) for h, c in heads]
            dq_st = [_dot(dob[:, c], sts[h].astype(BF16), NN) for h, c in heads]
            dk_out = [_dot(vb[:, c], dsts_b[h], NN) for h, c in heads]
            dv_st = [_dot(k_out_b[:, c], dsts_b[h], NT) for h, c in heads]
            dst_o = [_dot(dob[:, c], q_in_b[:, c], TN) for h, c in heads]
            attn = [jnp.where(tril, a, 0.0).astype(BF16) for a in attn]
            dattn = [jnp.where(tril, a, 0.0).astype(BF16) for a in dattn]
            dq_in = jnp.concatenate([_dot(dattn[h], k_in_b[:, c], NN) + dq_st[h] for h, c in heads], axis=1)
            dk_in = jnp.concatenate([_dot(dattn[h], q_in_b[:, c], TN) for h, c in heads], axis=1)
            dv = jnp.concatenate([_dot(attn[h], dob[:, c], TN) + dv_st[h] for h, c in heads], axis=1)
            dk_out = jnp.concatenate(dk_out, axis=1)
            dst_st = jnp.concatenate([jnp.sum(dsts[h] * sts[h], axis=0, keepdims=True) for h in range(HG_HEADS)], axis=1)
            for h, c in heads:
                dstate[h] = dsts[h] * decay[:, c] + dst_o[h]
            db_last = decay * dst_st + jnp.sum(dk_out * k_out, axis=0, keepdims=True)
            db = dq_in * q_in - dk_in * k_in - dk_out * k_out
            dg = _triangle_sum(triu_b, db) + db_last
            dk = dk_in * enb + dk_out * eo
            df = dg / f - dk
            stats_ref[0:1, :] += dgain
            stats_ref[1:2, :] += jnp.sum(df * (1.0 - sg), axis=0, keepdims=True)
            dz_ref[r, 0:1024] = (dq_in * eb).astype(BF16)
            dz_ref[r, 1024:2048] = (df * (1.0 - lb_all) * sg * (1.0 - sg)).astype(BF16)
            dz_ref[r, 2048:3072] = dv.astype(BF16)
            dz_ref[r, 3072:4096] = dhg.astype(BF16)
            return carry

        lax.fori_loop(0, ncs, chunk, 0, unroll=True)

        @pl.when(step == nt - 1)
        def _():
            dl0 = stats_ref[1:2, :] * lb_all * (1.0 - lb_all)
            stats_ref[1:2, :] = dl0
            stats_ref[2:3, :] = -dl0

    rev = lambda i: (nt - 1 - i, 0)
    return _fused_call(
        body, name=name, grid=(nt,),
        out_shape=(jax.ShapeDtypeStruct((s, W_A), BF16), jax.ShapeDtypeStruct((8, D_MODEL), F32)),
        in_specs=[pl.BlockSpec((t, W_A), rev), pl.BlockSpec((t, D_MODEL), rev), pl.BlockSpec((t, D_MODEL), rev),
                  pl.BlockSpec((HG_HEADS, ncs, HG_DK, HG_DK), lambda i: (0, nt - 1 - i, 0, 0)),
                  _resident((2, D_MODEL)), _resident((1, D_MODEL))],
        out_specs=(pl.BlockSpec((t, W_A), rev), pl.BlockSpec((8, D_MODEL), lambda i: (0, 0))),
        scratch_shapes=[pltpu.VMEM((HG_HEADS, HG_DK, HG_DK), F32)],
        operands=[za, oraw, do_a, states, lb_logits, gain], exchanges=exchanges)


def _t5_bucket(n):
    max_exact = NUM_BUCKETS // 2
    nf = jnp.maximum(n, 1).astype(F32)
    large = max_exact + (jnp.log(nf / max_exact) / math.log(MAX_DISTANCE / max_exact) * (NUM_BUCKETS - max_exact)).astype(jnp.int32)
    large = jnp.minimum(large, NUM_BUCKETS - 1)
    return jnp.where(n < max_exact, n, large)


def _bias_selector():
    qi = jnp.arange(SWA_BLOCK)[:, None] + SWA_BLOCK
    kj = jnp.arange(2 * SWA_BLOCK)[None, :]
    dist = qi - kj
    band = ((dist >= 0) & (dist < SWA_WINDOW)).reshape(1, -1)
    bucket = _t5_bucket(jnp.clip(dist, 0, SWA_WINDOW - 1)).reshape(1, -1)
    onehot = ((bucket == jnp.arange(NUM_BUCKETS)[:, None]) & band).astype(F32)
    return onehot, jnp.where(band, 0.0, MASK_VALUE).astype(F32)


def _bias_table(rel_bias_t, onehot, maskrow, *, name):
    def body(rb_ref, oh_ref, mask_ref, o_ref):
        o_ref[...] = _dot(rb_ref[...], oh_ref[...], NN, HIGHEST) + mask_ref[...]

    return pl.pallas_call(body, name=name, out_shape=jax.ShapeDtypeStruct((SWA_HEADS, onehot.shape[1]), F32),
                          compiler_params=_params())(rel_bias_t, onehot, maskrow)


def _bias_grad(dbias2d, onehot, *, name):
    def body(db_ref, oh_ref, o_ref):
        o_ref[...] = _dot(db_ref[...], oh_ref[...], NT, HIGHEST)

    return pl.pallas_call(body, name=name, out_shape=jax.ShapeDtypeStruct((SWA_HEADS, NUM_BUCKETS), F32),
                          compiler_params=_params())(dbias2d, onehot)


def _swa_operands(zq_ref, kv_cur_ref, kv_prev_ref):
    q = (zq_ref[:, 0:1024] * (SWA_HEAD_DIM ** -0.5)).astype(BF16)
    kv_c = kv_cur_ref[...].astype(BF16)
    kv_p = kv_prev_ref[...].astype(BF16)
    kks = [jnp.concatenate([kv_p[:, g * 64:(g + 1) * 64], kv_c[:, g * 64:(g + 1) * 64]], axis=0) for g in range(SWA_KV_HEADS)]
    vvs = [jnp.concatenate([kv_p[:, 128 + g * 64:128 + (g + 1) * 64], kv_c[:, 128 + g * 64:128 + (g + 1) * 64]], axis=0)
           for g in range(SWA_KV_HEADS)]
    return q, kks, vvs


SWA_PART_HEADS = 8
SWA_PARTS = [(h0 // SWA_GROUP, h0) for h0 in range(0, SWA_HEADS, SWA_PART_HEADS)]


def _part_lanes(h0):
    return slice(h0 * SWA_BLOCK, (h0 + SWA_PART_HEADS) * SWA_BLOCK)


def _stack_heads(x, h0):
    return jnp.concatenate([x[:, h * SWA_HEAD_DIM:(h + 1) * SWA_HEAD_DIM] for h in range(h0, h0 + SWA_PART_HEADS)], axis=0)


def _heads_to_lanes(xt):
    pairs = []
    for j in range(0, xt.shape[1] // SWA_BLOCK, 2):
        two = jnp.concatenate([xt[:, j * SWA_BLOCK:(j + 1) * SWA_BLOCK], xt[:, (j + 1) * SWA_BLOCK:(j + 2) * SWA_BLOCK]], axis=0)
        pairs.append(two.T)
    return jnp.concatenate(pairs, axis=1)


def _swa_softmax(score_t, bias_ref, sink_ref, h0):
    sc = score_t + bias_ref[:, _part_lanes(h0)]
    sink = sink_ref[:, _part_lanes(h0)]
    m = jnp.maximum(jnp.max(sc, axis=0, keepdims=True), sink)
    e = jnp.exp(sc - m)
    e_sink = jnp.exp(sink - m)
    return e, 1.0 / (jnp.sum(e, axis=0, keepdims=True) + e_sink), e_sink


def _swa_tables(bias2d, sinks):
    bias_t = bias2d.reshape(SWA_HEADS, SWA_BLOCK, 2 * SWA_BLOCK).transpose(2, 0, 1).reshape(2 * SWA_BLOCK, SWA_HEADS * SWA_BLOCK)
    first = jnp.where(jnp.arange(2 * SWA_BLOCK)[:, None] < SWA_BLOCK, MASK_VALUE, bias_t)
    return jnp.stack([first, bias_t]), jnp.repeat(sinks, SWA_BLOCK, axis=1)


def _swa_fwd(zb, bias_tables, sink_lanes, *, name, exchanges=()):
    s = zb.shape[0]
    nb = s // SWA_BLOCK

    def body(zq_ref, kvc_ref, kvp_ref, bias_ref, sink_ref, o_ref):
        q, kks, vvs = _swa_operands(zq_ref, kvc_ref, kvp_ref)
        scores = [_dot(kks[g], _stack_heads(q, h0), NT) for g, h0 in SWA_PARTS]
        probs = []
        for score, (_, h0) in zip(scores, SWA_PARTS):
            e, inv, _ = _swa_softmax(score, bias_ref, sink_ref, h0)
            probs.append((e * inv).astype(BF16))
        outs = [_dot(vvs[g], p, TN) for p, (g, _) in zip(probs, SWA_PARTS)]
        o_ref[...] = jnp.concatenate([_heads_to_lanes(o) for o in outs], axis=1).astype(BF16)

    return _fused_call(
        body, name=name, grid=(nb,), out_shape=jax.ShapeDtypeStruct((s, D_MODEL), BF16),
        in_specs=[pl.BlockSpec((SWA_BLOCK, W_B), lambda n: (n, 0)),
                  pl.BlockSpec((SWA_BLOCK, 256), lambda n: (n, 4)),
                  pl.BlockSpec((SWA_BLOCK, 256), lambda n: (jnp.maximum(n - 1, 0), 4)),
                  pl.BlockSpec((None, 2 * SWA_BLOCK, SWA_HEADS * SWA_BLOCK), lambda n: (jnp.minimum(n, 1), 0, 0)),
                  _resident((1, SWA_HEADS * SWA_BLOCK))],
        out_specs=pl.BlockSpec((SWA_BLOCK, D_MODEL), lambda n: (n, 0)), scratch_shapes=[],
        operands=[zb, zb, zb, bias_tables, sink_lanes], exchanges=exchanges)


def _swa_bwd(zb, do_b, bias_tables, sink_lanes, *, name, exchanges=()):
    s = zb.shape[0]
    nb = s // SWA_BLOCK
    scale = SWA_HEAD_DIM ** -0.5

    def body(zq_ref, kvc_ref, kvp_ref, do_ref, bias_ref, sink_ref, dz_ref, dbias_ref, dsink_ref, carry, dsink_acc):
        step = pl.program_id(0)

        @pl.when(step == 0)
        def _():
            carry[...] = jnp.zeros_like(carry)
            dsink_acc[...] = jnp.zeros_like(dsink_acc)
            dbias_ref[...] = jnp.zeros_like(dbias_ref)

        q, kks, vvs = _swa_operands(zq_ref, kvc_ref, kvp_ref)
        do = do_ref[...].astype(BF16)
        parts = range(len(SWA_PARTS))
        q_rows = [_stack_heads(q, h0) for _, h0 in SWA_PARTS]
        do_rows = [_stack_heads(do, h0) for _, h0 in SWA_PARTS]
        scores = [_dot(kks[g], q_rows[i], NT) for i, (g, _) in enumerate(SWA_PARTS)]
        soft = [_swa_softmax(scores[i], bias_ref, sink_ref, h0) for i, (_, h0) in enumerate(SWA_PARTS)]
        dps = [_dot(vvs[g], do_rows[i], NT) for i, (g, _) in enumerate(SWA_PARTS)]
        ps, dss = [], []
        for i, (_, h0) in enumerate(SWA_PARTS):
            e, inv, e_sink = soft[i]
            p = e * inv
            delta = jnp.sum(p * dps[i], axis=0, keepdims=True)
            ds = p * (dps[i] - delta)
            dbias_ref[:, _part_lanes(h0)] += ds
            dsink_acc[:, _part_lanes(h0)] -= e_sink * inv * delta
            ps.append(p.astype(BF16))
            dss.append(ds.astype(BF16))
        dqs = [_dot(kks[g], dss[i], TN) * scale for i, (g, _) in enumerate(SWA_PARTS)]
        in_group = lambda xs, g, axis: jnp.concatenate([xs[i] for i in parts if SWA_PARTS[i][0] == g], axis=axis)
        dkks = [_dot(in_group(dss, g, 1), in_group(q_rows, g, 0), NN) for g in range(SWA_KV_HEADS)]
        dvvs = [_dot(in_group(ps, g, 1), in_group(do_rows, g, 0), NN) for g in range(SWA_KV_HEADS)]
        dkv = jnp.concatenate(dkks + dvvs, axis=1)
        dz_ref[:, 0:1024] = jnp.concatenate([_heads_to_lanes(dq) for dq in dqs], axis=1).astype(BF16)
        dz_ref[:, 1024:1280] = (dkv[SWA_BLOCK:, :] + carry[...]).astype(BF16)
        carry[...] = dkv[:SWA_BLOCK, :]

        @pl.when(step == nb - 1)
        def _():
            acc = dsink_acc[...]
            dsink_ref[...] = jnp.concatenate([jnp.sum(acc[:, h * SWA_BLOCK:(h + 1) * SWA_BLOCK], axis=1, keepdims=True)
                                              for h in range(SWA_HEADS)], axis=1)

    rev = lambda i: (nb - 1 - i, 0)
    table_shape = (2 * SWA_BLOCK, SWA_HEADS * SWA_BLOCK)
    return _fused_call(
        body, name=name, grid=(nb,),
        out_shape=(jax.ShapeDtypeStruct((s, W_B), BF16), jax.ShapeDtypeStruct(table_shape, F32), jax.ShapeDtypeStruct((1, SWA_HEADS), F32)),
        in_specs=[pl.BlockSpec((SWA_BLOCK, W_B), rev),
                  pl.BlockSpec((SWA_BLOCK, 256), lambda i: (nb - 1 - i, 4)),
                  pl.BlockSpec((SWA_BLOCK, 256), lambda i: (jnp.maximum(nb - 2 - i, 0), 4)),
                  pl.BlockSpec((SWA_BLOCK, D_MODEL), rev),
                  pl.BlockSpec((None,) + table_shape, lambda i: (jnp.minimum(nb - 1 - i, 1), 0, 0)),
                  _resident((1, SWA_HEADS * SWA_BLOCK))],
        out_specs=(pl.BlockSpec((SWA_BLOCK, W_B), rev), pl.BlockSpec(table_shape, lambda i: (0, 0)),
                   pl.BlockSpec((1, SWA_HEADS), lambda i: (0, 0))),
        scratch_shapes=[pltpu.VMEM((SWA_BLOCK, 256), F32), pltpu.VMEM((1, SWA_HEADS * SWA_BLOCK), F32)],
        operands=[zb, zb, zb, do_b, bias_tables, sink_lanes], exchanges=exchanges)


MEM_COLS = [slice(h * MEM_HEAD_DIM, (h + 1) * MEM_HEAD_DIM) for h in range(MEM_HEADS)]
MEM_VCOLS = [slice(D_MODEL + h * MEM_HEAD_DIM, D_MODEL + (h + 1) * MEM_HEAD_DIM) for h in range(MEM_HEADS)]


def _mem_probs(zc_ref, mkv_ref):
    qs = [(zc_ref[:, c] * (MEM_HEAD_DIM ** -0.5)).astype(BF16) for c in MEM_COLS]
    scores = [_dot(qs[h], mkv_ref[:, c], NT) for h, c in enumerate(MEM_COLS)]
    ps = []
    for sc in scores:
        e = jnp.exp(sc - jnp.max(sc, axis=-1, keepdims=True))
        ps.append(e / jnp.sum(e, axis=-1, keepdims=True))
    return qs, ps


def _mem_fwd(xb, wi_t, mkv, *, name):
    s = xb.shape[0]
    t = min(512, s)

    def body(x_ref, w_ref, mkv_ref, zc_ref, o_ref):
        zc_ref[...] = _dot(x_ref[...], w_ref[...], NT).astype(BF16)
        _, ps = _mem_probs(zc_ref, mkv_ref)
        ps = [p.astype(BF16) for p in ps]
        o_ref[...] = jnp.concatenate([_dot(ps[h], mkv_ref[:, vc], NN) for h, vc in enumerate(MEM_VCOLS)], axis=1).astype(BF16)

    row = pl.BlockSpec((t, D_MODEL), lambda i: (i, 0))
    return pl.pallas_call(
        body, name=name, grid=(s // t,), out_shape=(jax.ShapeDtypeStruct((s, D_MODEL), BF16),) * 2,
        in_specs=[row, _resident_rows(wi_t, W_A + W_B, W_C), _resident((MEM_LEN, 2 * D_MODEL))],
        out_specs=(row, row), compiler_params=_params(("parallel",)),
    )(xb, wi_t, mkv)


def _mem_bwd(xb, zc, do_c, mkv, *, name):
    s = zc.shape[0]
    t = min(512, s)
    nt = s // t

    def body(x_ref, zc_ref, do_ref, mkv_ref, dz_ref, dmkv_ref, gwi_ref, acc):
        @pl.when(pl.program_id(0) == 0)
        def _():
            dmkv_ref[...] = jnp.zeros_like(dmkv_ref)
            acc[...] = jnp.zeros_like(acc)

        heads = range(MEM_HEADS)
        qs, ps = _mem_probs(zc_ref, mkv_ref)
        dos = [do_ref[:, c].astype(BF16) for c in MEM_COLS]
        dps = [_dot(dos[h], mkv_ref[:, MEM_VCOLS[h]], NT) for h in heads]
        dss = [(ps[h] * (dps[h] - jnp.sum(ps[h] * dps[h], axis=-1, keepdims=True))).astype(BF16) for h in heads]
        ps = [p.astype(BF16) for p in ps]
        dz = jnp.concatenate([_dot(dss[h], mkv_ref[:, MEM_COLS[h]], NN) * (MEM_HEAD_DIM ** -0.5) for h in heads], axis=1).astype(BF16)
        dz_ref[...] = dz
        dmkv_ref[...] += jnp.concatenate([_dot(dss[h], qs[h], TN) for h in heads] + [_dot(ps[h], dos[h], TN) for h in heads], axis=1)
        acc[...] += _dot(dz, x_ref[...], TN)

        @pl.when(pl.program_id(0) == nt - 1)
        def _():
            pltpu.sync_copy(acc, gwi_ref.at[pl.ds(W_A + W_B, W_C), :])

    row = pl.BlockSpec((t, D_MODEL), lambda i: (i, 0))
    return pl.pallas_call(
        body, name=name, grid=(nt,),
        out_shape=(jax.ShapeDtypeStruct((s, D_MODEL), BF16), jax.ShapeDtypeStruct((MEM_LEN, 2 * D_MODEL), F32),
                   jax.ShapeDtypeStruct((IN_COLS, D_MODEL), F32)),
        in_specs=[row, row, row, _resident((MEM_LEN, 2 * D_MODEL))],
        out_specs=(row, pl.BlockSpec((MEM_LEN, 2 * D_MODEL), lambda i: (0, 0)), HBM),
        scratch_shapes=[pltpu.VMEM((W_C, D_MODEL), F32)],
        compiler_params=_params(("arbitrary",)),
    )(xb, zc, do_c, mkv)


def _normalize(pre):
    mu = jnp.mean(pre, axis=-1, keepdims=True)
    xc = pre - mu
    rstd = lax.rsqrt(jnp.mean(xc * xc, axis=-1, keepdims=True) + LN_EPS)
    return xc * rstd, rstd


def _layer_norm_bwd(dh, xhat, rstd, g):
    dxh = dh * g
    dpre = rstd * (dxh - jnp.mean(dxh, axis=-1, keepdims=True) - xhat * jnp.mean(dxh * xhat, axis=-1, keepdims=True))
    return dpre, jnp.sum(dh * xhat, axis=0, keepdims=True), jnp.sum(dh, axis=0, keepdims=True)


def _merge_fwd(o_a, o_b, o_c, x, wi_t, wbr, wo, *, name):
    s = x.shape[0]
    t = min(256, s)
    row = lambda w: pl.BlockSpec((t, w), lambda i: (i, 0))

    def body(oa_ref, ob_ref, oc_ref, x_ref, wg_ref, wa_ref, wb_ref, wc_ref, wo_ref, zd_ref, xhat_ref, rstd_ref, merged_ref, pa_ref, pb_ref, pc_ref):
        wbr_refs = (wa_ref, wb_ref, wc_ref)
        zd_ref[...] = _dot(x_ref[...].astype(BF16), wg_ref[...], NT)
        merged = jnp.zeros((t, D_MODEL), F32)
        for b, (o_ref, p_ref) in enumerate(((oa_ref, pa_ref), (ob_ref, pb_ref), (oc_ref, pc_ref))):
            p = _dot(o_ref[...], wbr_refs[b][...], NN)
            p_ref[...] = p.astype(BF16)
            merged = merged + jax.nn.sigmoid(zd_ref[:, b * D_MODEL:(b + 1) * D_MODEL]) * p
        merged_b = merged.astype(BF16)
        merged_ref[...] = merged_b
        xhat, rstd = _normalize(ALPHA * x_ref[...] + _dot(merged_b, wo_ref[...], NN))
        xhat_ref[...] = xhat
        rstd_ref[...] = rstd

    act = jax.ShapeDtypeStruct((s, D_MODEL), F32)
    return pl.pallas_call(
        body, name=name, grid=(s // t,),
        out_shape=(jax.ShapeDtypeStruct((s, W_D), F32), act, jax.ShapeDtypeStruct((s, 1), F32)) + (jax.ShapeDtypeStruct((s, D_MODEL), BF16),) * 4,
        in_specs=[row(D_MODEL)] * 4 + [_resident_rows(wi_t, W_A + W_B + W_C, W_D)] + [_resident((D_MODEL, D_MODEL))] * 4,
        out_specs=(row(W_D), row(D_MODEL), row(1), row(D_MODEL), row(D_MODEL), row(D_MODEL), row(D_MODEL)),
        compiler_params=_params(("parallel",)),
    )(o_a, o_b, o_c, x, wi_t, *wbr, wo)


def _merge_bwd(dpre1, zd, pa, pb, pc, o_a, o_b, o_c, merged, wbr, wo, *, name, exchanges=()):
    s = dpre1.shape[0]
    t = min(256, s)
    nt = s // t
    row = lambda w: pl.BlockSpec((t, w), lambda i: (i, 0))

    def body(dpre_ref, zd_ref, pa_ref, pb_ref, pc_ref, oa_ref, ob_ref, oc_ref, mg_ref, wa_ref, wb_ref, wc_ref, wo_ref,
             dzd_ref, doa_ref, dob_ref, doc_ref, gwa_ref, gwb_ref, gwc_ref, gwo_ref, acc):
        step = pl.program_id(0)

        @pl.when(step == 0)
        def _():
            acc[...] = jnp.zeros_like(acc)

        dpre_b = dpre_ref[...].astype(BF16)
        dmerged = _dot(dpre_b, wo_ref[...], NT)
        acc[3] += _dot(mg_ref[...], dpre_b, TN)
        branches = ((pa_ref, oa_ref, doa_ref), (pb_ref, ob_ref, dob_ref), (pc_ref, oc_ref, doc_ref))
        for b, (p_ref, o_ref, do_ref) in enumerate(branches):
            gate = jax.nn.sigmoid(zd_ref[:, b * D_MODEL:(b + 1) * D_MODEL])
            dzd_ref[:, b * D_MODEL:(b + 1) * D_MODEL] = (dmerged * p_ref[...] * gate * (1.0 - gate)).astype(BF16)
            dp = (dmerged * gate).astype(BF16)
            acc[b] += _dot(o_ref[...], dp, TN)
            do_ref[...] = _dot(dp, (wa_ref, wb_ref, wc_ref)[b][...], NT).astype(do_ref.dtype)

        @pl.when(step == nt - 1)
        def _():
            for b, gw_ref in enumerate((gwa_ref, gwb_ref, gwc_ref, gwo_ref)):
                pltpu.sync_copy(acc.at[b], gw_ref)

    act = jax.ShapeDtypeStruct((s, D_MODEL), F32)
    actb = jax.ShapeDtypeStruct((s, D_MODEL), BF16)
    gw = jax.ShapeDtypeStruct((D_MODEL, D_MODEL), F32)
    return _fused_call(
        body, name=name, grid=(nt,),
        out_shape=(jax.ShapeDtypeStruct((s, W_D), BF16), act, actb, actb, gw, gw, gw, gw),
        in_specs=[row(D_MODEL), row(W_D)] + [row(D_MODEL)] * 7 + [_resident((D_MODEL, D_MODEL))] * 4,
        out_specs=(row(W_D),) + (row(D_MODEL),) * 3 + (HBM,) * 4, scratch_shapes=[pltpu.VMEM((4, D_MODEL, D_MODEL), F32)],
        operands=[dpre1, zd, pa, pb, pc, o_a, o_b, o_c, merged, *wbr, wo], exchanges=exchanges)


def _mlp_loss(xhat1, rstd1, target, ln1_g, ln1_b, ln2_g, ln2_b, wu, wd, *, name):
    s = xhat1.shape[0]
    t = min(256, s)
    npan = wu.shape[0]
    row = lambda w: pl.BlockSpec((t, w), lambda i: (i, 0))
    vec = _resident((1, D_MODEL))

    def body(xhat_ref, rstd_ref, tgt_ref, g1_ref, b1_ref, g2_ref, b2_ref, wu_ref, wd_ref,
             dpre1_ref, dpre2_ref, h1_ref, a_ref, du_ref, stats_ref):
        @pl.when(pl.program_id(0) == 0)
        def _():
            stats_ref[...] = jnp.zeros_like(stats_ref)

        xhat1_v = xhat_ref[...]
        h1 = xhat1_v * g1_ref[...] + b1_ref[...]
        h1_b = h1.astype(BF16)
        h1_ref[...] = h1_b
        us = []
        ff = jnp.zeros((t, D_MODEL), F32)
        for j in range(npan):
            u = _dot(h1_b, wu_ref[j], NN)
            us.append(u)
            r = jnp.maximum(u, 0.0)
            a_b = (r * r).astype(BF16)
            a_ref[:, j * D_MODEL:(j + 1) * D_MODEL] = a_b
            ff = ff + _dot(a_b, wd_ref[j], NN)
        xhat2, rstd2 = _normalize(ALPHA * h1 + ff)
        err = xhat2 * g2_ref[...] + b2_ref[...] - tgt_ref[...]
        stats_ref[4:5, :] += jnp.sum(err * err, axis=0, keepdims=True)
        dpre2, dg2, db2 = _layer_norm_bwd(err * (1.0 / D_MODEL), xhat2, rstd2, g2_ref[...])
        stats_ref[0:1, :] += dg2
        stats_ref[1:2, :] += db2
        dpre2_b = dpre2.astype(BF16)
        dpre2_ref[...] = dpre2_b
        dh1 = ALPHA * dpre2
        for j in range(npan):
            du_b = (_dot(dpre2_b, wd_ref[j], NT) * (2.0 * jnp.maximum(us[j], 0.0))).astype(BF16)
            du_ref[:, j * D_MODEL:(j + 1) * D_MODEL] = du_b
            dh1 = dh1 + _dot(du_b, wu_ref[j], NT)
        dpre1, dg1, db1 = _layer_norm_bwd(dh1, xhat1_v, rstd_ref[...], g1_ref[...])
        stats_ref[2:3, :] += dg1
        stats_ref[3:4, :] += db1
        dpre1_ref[...] = dpre1

    actb = jax.ShapeDtypeStruct((s, D_MODEL), BF16)
    wide = jax.ShapeDtypeStruct((s, D_FF), BF16)
    return pl.pallas_call(
        body, name=name, grid=(s // t,),
        out_shape=(jax.ShapeDtypeStruct((s, D_MODEL), F32), actb, actb, wide, wide, jax.ShapeDtypeStruct((8, D_MODEL), F32)),
        in_specs=[row(D_MODEL), row(1), row(D_MODEL), vec, vec, vec, vec,
                  _resident((npan, D_MODEL, D_MODEL)), _resident((npan, D_MODEL, D_MODEL))],
        out_specs=(row(D_MODEL), row(D_MODEL), row(D_MODEL), row(D_FF), row(D_FF), pl.BlockSpec((8, D_MODEL), lambda i: (0, 0))),
        compiler_params=_params(("arbitrary",)),
    )(xhat1, rstd1, target, ln1_g, ln1_b, ln2_g, ln2_b, wu, wd)


BRANCH_WEIGHTS = ("w_branch_hg", "w_branch_swa", "w_branch_mem")


def _local_step(x, xb, mem, target, wi_t, wmkv, late, lb_logits, gain, sinks, rel_bias, ln1_g, ln1_b, ln2_g, ln2_b, *, distributed):
    s = x.shape[0]
    tm = min(1024, s)
    tk = min(2048, s)
    memb = mem.astype(BF16)
    if distributed:
        cx, cy, cc = lax.axis_index("x"), lax.axis_index("y"), lax.axis_index("c")
        pos = jnp.stack([2 * cx + cy, cc]).astype(jnp.int32)
    gather = (lambda names: [_gather_exchange([late[k] for k in names])]) if distributed else (lambda names: [])
    to_sibling = (lambda grads: [_sibling_halves_exchange(grads)]) if distributed else (lambda grads: [])
    to_chips = (lambda sums: [_chip_partials_exchange([bf for bf, _ in sums])]) if distributed else (lambda sums: [])

    def chip_sums(names, grads, from_sibling):
        return [_add_sibling(g, o, pos, name="add_sibling_" + k) for k, g, o in zip(names, grads, from_sibling)]

    def shard_sums(names, sums, from_chips):
        return {k: _add_chips(mine, o, pos, name="add_chips_" + k) for k, (_, mine), o in zip(names, sums, from_chips)}

    zb = _mm(xb, wi_t, mode="nt", tm=tm, tn=W_B, tk=D_MODEL, name="proj_b", out_dtype=BF16, b_rows=(W_A, W_B))
    mkv = _mm(memb, wmkv, mode="nn", tm=MEM_LEN, tn=512, tk=D_MODEL, name="mem_kv", out_dtype=BF16, b_panels=True)
    onehot, maskrow = _bias_selector()
    bias_tables, sink_lanes = _swa_tables(_bias_table(rel_bias.T, onehot, maskrow, name="bias_table"), sinks)
    (za, o_a, o_raw, states), landed = _hgrn_fwd(xb, wi_t, lb_logits, gain, name="hgrn_fwd", exchanges=gather(("w_up", "w_down")))
    wu, wd = landed[0] if distributed else (late["wu"], late["wd"])
    o_b, landed = _swa_fwd(zb, bias_tables, sink_lanes, name="swa_fwd", exchanges=gather(BRANCH_WEIGHTS + ("w_out",)))
    if distributed:
        wbr = [wb.reshape(D_MODEL, D_MODEL) for wb in landed[0][:3]]
        wo = landed[0][3].reshape(D_MODEL, D_MODEL)
    else:
        wbr, wo = [late["wbr"][b] for b in range(3)], late["wo"]
    zc, o_c = _mem_fwd(xb, wi_t, mkv, name="mem_fwd")
    zd, xhat1, rstd1, merged, pa, pb, pc = _merge_fwd(o_a, o_b, o_c, x, wi_t, wbr, wo, name="merge_fwd")

    dpre1, dpre2, h1, act, du, ln_stats = _mlp_loss(xhat1, rstd1, target, ln1_g, ln1_b, ln2_g, ln2_b, wu, wd, name="mlp_loss")
    ffn = ("w_down", "w_up")
    g_ffn = [_mm(act, dpre2, mode="tn", tm=1024, tn=D_MODEL, tk=tk, name="grad_w_down").reshape(N_SHARDS, D_FF // N_SHARDS, D_MODEL),
             _mm(h1, du, mode="tn", tm=D_MODEL, tn=1024, tk=tk, name="grad_w_up", out_panels=True)]

    (dzd, do_a, do_b, do_c, *g_merge), landed = _merge_bwd(dpre1, zd, pa, pb, pc, o_a, o_b, o_c, merged, wbr, wo, name="merge_bwd",
                                                           exchanges=to_sibling(g_ffn))
    sums_ffn = chip_sums(ffn, g_ffn, landed[0]) if distributed else []
    merge = BRANCH_WEIGHTS + ("w_out",)
    g_merge = [g.reshape(N_SHARDS, D_MODEL // N_SHARDS, D_MODEL) for g in g_merge]
    (dza, hg_stats), landed = _hgrn_bwd(za, o_raw, do_a, states, lb_logits, gain, name="hgrn_bwd",
                                        exchanges=to_chips(sums_ffn) + to_sibling(g_merge))
    halves = shard_sums(ffn, sums_ffn, landed[0]) if distributed else {}
    sums_merge = chip_sums(merge, g_merge, landed[1]) if distributed else []
    (dzb, dbias_t, dsinks), landed = _swa_bwd(zb, do_b, bias_tables, sink_lanes, name="swa_bwd", exchanges=to_chips(sums_merge))
    if distributed:
        halves.update(shard_sums(merge, sums_merge, landed[0]))
    dbias = dbias_t.reshape(2 * SWA_BLOCK, SWA_HEADS, SWA_BLOCK).transpose(1, 2, 0).reshape(SWA_HEADS, -1)
    d_rel_bias = _bias_grad(dbias, onehot, name="bias_grad").T
    dzc, dmkv, g_wi = _mem_bwd(xb, zc, do_c, mkv, name="mem_bwd")

    proj = ("w_in", "w_mem_kv")
    for dz, offset, nm in ((dza, 0, "grad_w_in_a"), (dzb, W_A, "grad_w_in_b"), (dzd, W_A + W_B + W_C, "grad_w_in_d")):
        g_wi = _mm(dz, xb, mode="tn", tm=dz.shape[1] if dz.shape[1] <= 1280 else 1024, tn=D_MODEL, tk=tk, name=nm,
                   rows_of=IN_COLS, row_offset=offset, into=g_wi)
    g_proj = [g_wi.reshape(N_SHARDS, IN_COLS // N_SHARDS, D_MODEL),
              _mm(memb, dmkv, mode="tn", tm=D_MODEL, tn=512, tk=MEM_LEN, name="grad_w_mem_kv", out_panels=True)]
    small = dict(lb_logits=hg_stats[1:3], hg_norm_gain=hg_stats[0:1], swa_sinks=dsinks, rel_bias=d_rel_bias,
                 ln1_g=ln_stats[2:3], ln1_b=ln_stats[3:4], ln2_g=ln_stats[0:1], ln2_b=ln_stats[1:2], sq_err=ln_stats[4:5])
    small_exchange = [_small_gather_exchange(_pack_small(small, name="pack_small"))] if distributed else []
    tx = min(512, s // 2)
    head = max(1, 3 * (s // tx) // 8)
    dx = functools.partial(_dx_matmul, [dza, dzb, dzc, dzd], wi_t, dpre1, tm=tx)
    grad_x_head, landed = dx(tiles=(0, head), name="grad_x_head", exchanges=to_sibling(g_proj))
    sums_proj = chip_sums(proj, g_proj, landed[0]) if distributed else []
    grad_x_tail, landed = dx(tiles=(head, s // tx - head), name="grad_x", exchanges=to_chips(sums_proj) + small_exchange)
    grad_x = jnp.concatenate([grad_x_head, grad_x_tail])
    if distributed:
        halves.update(shard_sums(proj, sums_proj, landed[0]))
        small = landed[1][0]
    else:
        halves = dict(zip(ffn + merge + proj, g_ffn + g_merge + g_proj))
    return grad_x, halves, small


def _mesh_position():
    x, y, c = lax.axis_index("x"), lax.axis_index("y"), lax.axis_index("c")
    chips = [(1 - x, y), (x, 1 - y), (1 - x, 1 - y)]
    return x, y, c, chips


class _Exchange(NamedTuple):
    operands: list
    out_shapes: list
    n_sems: int
    start: Callable
    finish: Callable
    halfway: Optional[Callable] = None


def _gather_exchange(shards):
    n = len(shards)
    per = 9
    assert all(w.shape[0] % (4 * BF16_SUBLANES) == 0 for w in shards)

    def plan(ins, outs, send_sems, recv_sems):
        x, y, c, (x_nbr, y_nbr, diag) = _mesh_position()
        sibling = (x, y, 1 - c)
        slot = lambda chip: 2 * chip[0] + chip[1]

        def rows(a, chip, hc, quarter=None):
            rh = shards[a].shape[0] // 2
            if quarter is None:
                return outs[a].at[slot(chip), pl.ds(hc * rh, rh), :]
            return outs[a].at[slot(chip), pl.ds(hc * rh + quarter * (rh // 2), rh // 2), :]

        def copy(a, k, src, dst, to):
            return pltpu.make_async_remote_copy(src_ref=src, dst_ref=dst, send_sem=send_sems.at[a * per + k], recv_sem=recv_sems.at[a * per + k],
                                                device_id=to, device_id_type=MESH)

        first, from_sibling = [], []
        landed, then = [[] for _ in range(4)], [[] for _ in range(4)]
        for a in range(n):
            rh = shards[a].shape[0] // 2
            my_half = ins[a].at[pl.ds(c * rh, rh), :]
            first += [copy(a, 4, ins[a], outs[a].at[slot((x, y))], sibling),
                      copy(a, 0, my_half, rows(a, (x, y), c), (*x_nbr, c)), copy(a, 1, my_half, rows(a, (x, y), c), (*y_nbr, c))]
            landed[0].append(copy(a, 0, rows(a, x_nbr, c), rows(a, x_nbr, c), (*x_nbr, c)))
            then[0].append([copy(a, 2, rows(a, x_nbr, c, 0), rows(a, x_nbr, c, 0), (*y_nbr, c)), copy(a, 5, rows(a, x_nbr, c), rows(a, x_nbr, c), sibling)])
            landed[1].append(copy(a, 1, rows(a, y_nbr, c), rows(a, y_nbr, c), (*y_nbr, c)))
            then[1].append([copy(a, 3, rows(a, y_nbr, c, 1), rows(a, y_nbr, c, 1), (*x_nbr, c)), copy(a, 6, rows(a, y_nbr, c), rows(a, y_nbr, c), sibling)])
            landed[2].append(copy(a, 2, rows(a, diag, c, 0), rows(a, diag, c, 0), (*y_nbr, c)))
            then[2].append([copy(a, 7, rows(a, diag, c, 0), rows(a, diag, c, 0), sibling)])
            landed[3].append(copy(a, 3, rows(a, diag, c, 1), rows(a, diag, c, 1), (*x_nbr, c)))
            then[3].append([copy(a, 8, rows(a, diag, c, 1), rows(a, diag, c, 1), sibling)])
            from_sibling += [copy(a, 4, outs[a].at[slot((x, y))], outs[a].at[slot((x, y))], sibling),
                             copy(a, 5, rows(a, x_nbr, 1 - c), rows(a, x_nbr, 1 - c), sibling), copy(a, 6, rows(a, y_nbr, 1 - c), rows(a, y_nbr, 1 - c), sibling),
                             copy(a, 7, rows(a, diag, 1 - c, 0), rows(a, diag, 1 - c, 0), sibling), copy(a, 8, rows(a, diag, 1 - c, 1), rows(a, diag, 1 - c, 1), sibling)]
        return first, landed, then, from_sibling

    def start(*refs):
        first, _, _, _ = plan(*refs)
        for cp in first:
            cp.start()

    def stages(landed, then, which):
        for stage in which:
            for arrival, onward in zip(landed[stage], then[stage]):
                arrival.wait_recv()
                for cp in onward:
                    cp.start()

    def halfway(*refs):
        _, landed, then, _ = plan(*refs)
        stages(landed, then, (0, 1))

    def finish(*refs):
        first, landed, then, from_sibling = plan(*refs)
        stages(landed, then, (2, 3))
        for cp in from_sibling:
            cp.wait_recv()
        for cp in first + [cp for stage in then for onward in stage for cp in onward]:
            cp.wait_send()

    return _Exchange(list(shards), [jax.ShapeDtypeStruct((N_SHARDS,) + w.shape, w.dtype) for w in shards], per * n, start, finish, halfway)


def _sibling_halves_exchange(grads):
    n = len(grads)

    def plan(ins, outs, send_sems, recv_sems):
        x, y, c, _ = _mesh_position()
        return [pltpu.make_async_remote_copy(src_ref=ins[a].at[:, pl.ds((1 - c) * (grads[a].shape[1] // 2), grads[a].shape[1] // 2), :],
                                             dst_ref=outs[a], send_sem=send_sems.at[a], recv_sem=recv_sems.at[a],
                                             device_id=(x, y, 1 - c), device_id_type=MESH) for a in range(n)]

    def start(*refs):
        for cp in plan(*refs):
            cp.start()

    def finish(*refs):
        for cp in plan(*refs):
            cp.wait()

    return _Exchange(list(grads), [jax.ShapeDtypeStruct((g.shape[0], g.shape[1] // 2, g.shape[2]), g.dtype) for g in grads], n, start, finish)


def _chip_partials_exchange(sums):
    n = len(sums)

    def plan(ins, outs, send_sems, recv_sems):
        _, _, c, chips = _mesh_position()
        return [pltpu.make_async_remote_copy(src_ref=ins[a].at[2 * cx + cy], dst_ref=outs[a].at[k], send_sem=send_sems.at[a * 3 + k],
                                             recv_sem=recv_sems.at[a * 3 + k], device_id=(cx, cy, c), device_id_type=MESH)
                for k, (cx, cy) in enumerate(chips) for a in range(n)]

    def start(*refs):
        for cp in plan(*refs):
            cp.start()

    def finish(*refs):
        for cp in plan(*refs):
            cp.wait()

    return _Exchange(list(sums), [jax.ShapeDtypeStruct((3,) + g.shape[1:], g.dtype) for g in sums], 3 * n, start, finish)


def _fused_call(body, *, name, grid, in_specs, out_specs, out_shape, scratch_shapes, operands, exchanges=()):
    single = not isinstance(out_shape, (tuple, list))
    out_specs = [out_specs] if single else list(out_specs)
    out_shape = [out_shape] if single else list(out_shape)
    n_in, n_out, n_scr = len(in_specs), len(out_specs), len(scratch_shapes)
    x_in = [len(e.operands) for e in exchanges]
    x_out = [len(e.out_shapes) for e in exchanges]

    def wrapped(*refs):
        refs = list(refs)
        ins = refs[:n_in]
        pos = n_in
        ex_ins = []
        for k in x_in:
            ex_ins.append(refs[pos:pos + k])
            pos += k
        outs = refs[pos:pos + n_out]
        pos += n_out
        ex_outs = []
        for k in x_out:
            ex_outs.append(refs[pos:pos + k])
            pos += k
        scratch = refs[pos:pos + n_scr]
        sems = refs[pos + n_scr:]
        first, last, middle = None, None, None
        for axis, size in enumerate(grid):
            at_start, at_end, at_middle = pl.program_id(axis) == 0, pl.program_id(axis) == size - 1, pl.program_id(axis) == size // 2
            first = at_start if first is None else first & at_start
            last = at_end if last is None else last & at_end
            middle = at_middle if middle is None else middle & at_middle

        @pl.when(first)
        def _():
            for i, e in enumerate(exchanges):
                e.start(ex_ins[i], ex_outs[i], sems[2 * i], sems[2 * i + 1])

        if any(e.halfway for e in exchanges):
            @pl.when(middle)
            def _():
                for i, e in enumerate(exchanges):
                    if e.halfway:
                        e.halfway(ex_ins[i], ex_outs[i], sems[2 * i], sems[2 * i + 1])

        body(*ins, *outs, *scratch)

        @pl.when(last)
        def _():
            for i, e in enumerate(exchanges):
                e.finish(ex_ins[i], ex_outs[i], sems[2 * i], sems[2 * i + 1])

    n_x_in, n_x_out = sum(x_in), sum(x_out)
    results = pl.pallas_call(
        wrapped if exchanges else body, name=name, grid=grid,
        in_specs=list(in_specs) + [HBM] * n_x_in,
        out_specs=out_specs + [HBM] * n_x_out,
        out_shape=out_shape + [s for e in exchanges for s in e.out_shapes],
        scratch_shapes=list(scratch_shapes) + [pltpu.SemaphoreType.DMA((e.n_sems,)) for e in exchanges for _ in range(2)],
        compiler_params=_params(("arbitrary",) * len(grid)),
    )(*operands, *[a for e in exchanges for a in e.operands])
    own = results[0] if single else tuple(results[:n_out])
    landed, pos = [], n_out
    for k in x_out:
        landed.append(list(results[pos:pos + k]))
        pos += k
    return own, landed


def _cast_bf16(x, *, name, exchanges=()):
    s, cols = x.shape
    t = min(512, s)

    def body(x_ref, o_ref):
        o_ref[...] = x_ref[...].astype(BF16)

    tile = pl.BlockSpec((t, cols), lambda i: (i, 0))
    return _fused_call(body, name=name, grid=(s // t,), in_specs=[tile], out_specs=tile, out_shape=jax.ShapeDtypeStruct((s, cols), BF16),
                       scratch_shapes=[], operands=[x], exchanges=exchanges)


ROW_TILE_MAX = 640
BF16_SUBLANES = 16


def _row_tile(rows):
    for tr in range(min(rows, ROW_TILE_MAX), 0, -1):
        if rows % tr == 0 and tr % BF16_SUBLANES == 0:
            return tr
    raise ValueError(rows)


def _add_sibling(grad, other, pos, *, name):
    p, r, cols = grad.shape
    rh = r // 2
    tr = _row_tile(rh)
    nb = rh // tr

    def body(pos_ref, g_ref, o_ref, sb_ref, mine_ref):
        total = g_ref[...] + o_ref[...]
        sb_ref[...] = total.astype(BF16)

        @pl.when(pl.program_id(1) == pos_ref[0])
        def _():
            mine_ref[...] = total

    return pl.pallas_call(
        body, name=name, out_shape=(jax.ShapeDtypeStruct((p, rh, cols), BF16), jax.ShapeDtypeStruct((rh, cols), F32)),
        grid_spec=pltpu.PrefetchScalarGridSpec(
            num_scalar_prefetch=1, grid=(nb, p),
            in_specs=[pl.BlockSpec((None, tr, cols), lambda i, j, pos_ref: (j, pos_ref[1] * nb + i, 0)),
                      pl.BlockSpec((None, tr, cols), lambda i, j, pos_ref: (j, i, 0))],
            out_specs=(pl.BlockSpec((None, tr, cols), lambda i, j, pos_ref: (j, i, 0)),
                       pl.BlockSpec((tr, cols), lambda i, j, pos_ref: (i, 0)))),
        compiler_params=_params(("parallel", "arbitrary")),
    )(pos, grad, other)


def _add_chips(mine, others, pos, *, name):
    rh, cols = mine.shape
    tr = _row_tile(rh)
    nb = rh // tr

    def body(pos_ref, s_ref, o_ref, r_ref):
        r_ref[...] = ((s_ref[...] + o_ref[0].astype(F32)) + o_ref[1].astype(F32)) + o_ref[2].astype(F32)

    return pl.pallas_call(
        body, name=name, out_shape=jax.ShapeDtypeStruct((2 * rh, cols), F32),
        grid_spec=pltpu.PrefetchScalarGridSpec(
            num_scalar_prefetch=1, grid=(nb,),
            in_specs=[pl.BlockSpec((tr, cols), lambda i, pos_ref: (i, 0)),
                      pl.BlockSpec((3, tr, cols), lambda i, pos_ref: (0, i, 0))],
            out_specs=pl.BlockSpec((tr, cols), lambda i, pos_ref: (pos_ref[1] * nb + i, 0))),
        compiler_params=_params(("parallel",)),
    )(pos, mine, others)


def _join_halves(bufs, *, name):
    n = len(bufs)

    def body(*refs):
        ins, outs = refs[:n], refs[n:2 * n]
        send_sems, recv_sems = refs[2 * n:]
        x, y, c, _ = _mesh_position()

        def copy(a, hc):
            rh = bufs[a].shape[0] // 2
            rows = pl.ds(hc * rh, rh)
            return pltpu.make_async_remote_copy(src_ref=ins[a].at[rows, :], dst_ref=outs[a].at[rows, :], send_sem=send_sems.at[a],
                                                recv_sem=recv_sems.at[a], device_id=(x, y, 1 - c), device_id_type=MESH)

        for a in range(n):
            copy(a, c).start()
        for a in range(n):
            copy(a, c).wait_send()
            copy(a, 1 - c).wait_recv()

    return pl.pallas_call(
        body, name=name, out_shape=[jax.ShapeDtypeStruct(b.shape, b.dtype) for b in bufs],
        in_specs=[HBM] * n, out_specs=[HBM] * n, input_output_aliases={a: a for a in range(n)},
        scratch_shapes=[pltpu.SemaphoreType.DMA((n,)), pltpu.SemaphoreType.DMA((n,))],
    )(*bufs)


SMALL = ["lb_logits", "hg_norm_gain", "swa_sinks", "rel_bias", "ln1_g", "ln1_b", "ln2_g", "ln2_b"]
PACK_ROWS = 48
PACK_AT = dict(lb_logits=(slice(0, 2), slice(0, D_MODEL)), hg_norm_gain=(slice(2, 3), slice(0, D_MODEL)), ln1_g=(slice(3, 4), slice(0, D_MODEL)),
               ln1_b=(slice(4, 5), slice(0, D_MODEL)), ln2_g=(slice(5, 6), slice(0, D_MODEL)), ln2_b=(slice(6, 7), slice(0, D_MODEL)),
               swa_sinks=(slice(7, 8), slice(0, SWA_HEADS)), sq_err=(slice(8, 9), slice(0, D_MODEL)),
               rel_bias=(slice(16, 16 + NUM_BUCKETS), slice(0, SWA_HEADS)))


def _pack_small(grads, *, name):
    names = SMALL + ["sq_err"]

    def body(*refs):
        packed = refs[len(names)]
        packed[...] = jnp.zeros_like(packed)
        for k, g_ref in zip(names, refs):
            packed[PACK_AT[k]] = g_ref[...]

    return pl.pallas_call(body, name=name, out_shape=jax.ShapeDtypeStruct((PACK_ROWS, D_MODEL), F32), compiler_params=_params(),
                          )(*[grads[k] for k in names])


def _small_gather_exchange(packed):
    def plan(ins, outs, send_sems, recv_sems):
        x, y, c, _ = _mesh_position()
        me = 4 * x + 2 * y + c
        own = pltpu.make_async_copy(ins[0], outs[0].at[me], send_sems.at[7])
        remote = []
        for d in range(1, 8):
            dx, dy, dc = (d >> 2) & 1, (d >> 1) & 1, d & 1
            remote.append(pltpu.make_async_remote_copy(src_ref=ins[0], dst_ref=outs[0].at[me], send_sem=send_sems.at[d - 1],
                                                       recv_sem=recv_sems.at[d - 1], device_id=(x ^ dx, y ^ dy, c ^ dc), device_id_type=MESH))
        return own, remote

    def start(*refs):
        own, remote = plan(*refs)
        own.start()
        for cp in remote:
            cp.start()

    def finish(*refs):
        own, remote = plan(*refs)
        for cp in remote:
            cp.wait()
        own.wait()

    return _Exchange([packed], [jax.ShapeDtypeStruct((8,) + packed.shape, packed.dtype)], 8, start, finish)


def _adamw_small(gathered, w, m, v, *, name):
    names = SMALL
    n = len(names)

    def body(*refs):
        gathered_ref = refs[0]
        w_refs, m_refs, v_refs = (dict(zip(names, refs[1 + i * n:1 + (i + 1) * n])) for i in range(3))
        loss_ref = refs[1 + 3 * n]
        go_refs, d_refs, nm_refs, nv_refs = (dict(zip(names, refs[2 + (3 + i) * n:2 + (4 + i) * n])) for i in range(4))
        total_ref = refs[2 + 7 * n]
        total = gathered_ref[0]
        for j in range(1, 8):
            total = total + gathered_ref[j]
        total_ref[...] = total
        loss_ref[...] = (0.5 / D_MODEL) * jnp.sum(total_ref[PACK_AT["sq_err"]], axis=1, keepdims=True)
        for k in names:
            g = total_ref[PACK_AT[k]]
            go_refs[k][...] = g
            d_refs[k][...], nm_refs[k][...], nv_refs[k][...] = _adamw_math(w_refs[k][...], g, m_refs[k][...], v_refs[k][...])

    like = [jax.ShapeDtypeStruct(w[k].shape, F32) for k in names]
    results = pl.pallas_call(body, name=name, out_shape=[jax.ShapeDtypeStruct((1, 1), F32)] + like * 4,
                             scratch_shapes=[pltpu.VMEM((PACK_ROWS, D_MODEL), F32)],
                             compiler_params=_params())(gathered, *[d[k] for d in (w, m, v) for k in names])
    return results[0], {k: tuple(results[1 + i * n + j] for i in range(4)) for j, k in enumerate(names)}


def _adamw_math(w, g, m, v):
    m = ADAM_B1 * m + (1.0 - ADAM_B1) * g
    v = ADAM_B2 * v + (1.0 - ADAM_B2) * (g * g)
    m_hat = m / (1.0 - ADAM_B1 ** ADAM_STEP)
    v_hat = v / (1.0 - ADAM_B2 ** ADAM_STEP)
    delta = -ADAM_LR * (m_hat / (jnp.sqrt(v_hat) + ADAM_EPS) + ADAM_WD * w)
    return delta, m, v


def _adamw(w, g, m, v, *, name):
    _, rows, cols = w.shape
    tr = _row_tile(rows)
    blk = pl.BlockSpec((None, tr, cols), lambda i: (0, i, 0))
    flat = pl.BlockSpec((tr, cols), lambda i: (i, 0))

    def body(w_ref, g_ref, m_ref, v_ref, go_ref, d_ref, nm_ref, nv_ref):
        g_v = g_ref[...]
        go_ref[...] = g_v
        d_ref[...], nm_ref[...], nv_ref[...] = _adamw_math(w_ref[...], g_v, m_ref[...], v_ref[...])

    shape = jax.ShapeDtypeStruct((1, rows, cols), F32)
    return pl.pallas_call(body, name=name, grid=(rows // tr,), out_shape=(shape,) * 4, in_specs=[blk, flat, blk, blk], out_specs=(blk,) * 4,
                          compiler_params=_params(("parallel",)))(w, g, m, v)


WEIGHTS = ["w_in", "lb_logits", "hg_norm_gain", "swa_sinks", "rel_bias", "w_mem_kv", "w_branch_hg", "w_branch_swa", "w_branch_mem",
           "w_out", "ln1_g", "ln1_b", "w_up", "w_down", "ln2_g", "ln2_b"]
BIG = ["w_in", "w_mem_kv", "w_branch_hg", "w_branch_swa", "w_branch_mem", "w_out", "w_up", "w_down"]


def kernel(x, mem, w_in, lb_logits, hg_norm_gain, swa_sinks, rel_bias, w_mem_kv, w_branch_hg, w_branch_swa, w_branch_mem, w_out, ln1_g, ln1_b, w_up, w_down, ln2_g, ln2_b, loss_target, m_w_in, m_lb_logits, m_hg_norm_gain, m_swa_sinks, m_rel_bias, m_w_mem_kv, m_w_branch_hg, m_w_branch_swa, m_w_branch_mem, m_w_out, m_ln1_g, m_ln1_b, m_w_up, m_w_down, m_ln2_g, m_ln2_b, v_w_in, v_lb_logits, v_hg_norm_gain, v_swa_sinks, v_rel_bias, v_w_mem_kv, v_w_branch_hg, v_w_branch_swa, v_w_branch_mem, v_w_out, v_ln1_g, v_ln1_b, v_w_up, v_w_down, v_ln2_g, v_ln2_b):
    w = dict(w_in=w_in, lb_logits=lb_logits, hg_norm_gain=hg_norm_gain, swa_sinks=swa_sinks, rel_bias=rel_bias, w_mem_kv=w_mem_kv,
             w_branch_hg=w_branch_hg, w_branch_swa=w_branch_swa, w_branch_mem=w_branch_mem, w_out=w_out, ln1_g=ln1_g, ln1_b=ln1_b,
             w_up=w_up, w_down=w_down, ln2_g=ln2_g, ln2_b=ln2_b)
    m = dict(w_in=m_w_in, lb_logits=m_lb_logits, hg_norm_gain=m_hg_norm_gain, swa_sinks=m_swa_sinks, rel_bias=m_rel_bias, w_mem_kv=m_w_mem_kv,
             w_branch_hg=m_w_branch_hg, w_branch_swa=m_w_branch_swa, w_branch_mem=m_w_branch_mem, w_out=m_w_out, ln1_g=m_ln1_g, ln1_b=m_ln1_b,
             w_up=m_w_up, w_down=m_w_down, ln2_g=m_ln2_g, ln2_b=m_ln2_b)
    v = dict(w_in=v_w_in, lb_logits=v_lb_logits, hg_norm_gain=v_hg_norm_gain, swa_sinks=v_swa_sinks, rel_bias=v_rel_bias, w_mem_kv=v_w_mem_kv,
             w_branch_hg=v_w_branch_hg, w_branch_swa=v_w_branch_swa, w_branch_mem=v_w_branch_mem, w_out=v_w_out, ln1_g=v_ln1_g, ln1_b=v_ln1_b,
             w_up=v_w_up, w_down=v_w_down, ln2_g=v_ln2_g, ln2_b=v_ln2_b)
    shapes = {k: w[k].shape for k in WEIGHTS}
    for d in (w, m, v):
        d["w_in"] = d["w_in"].reshape(D_MODEL, IN_COLS // N_SHARDS).T[None]
    shards = {k: w[k].reshape(w[k].shape[-2], w[k].shape[-1]).astype(BF16) for k in BIG}
    x2d = x.reshape(x.shape[-2], D_MODEL)
    xb, ((wi4, wmkv),) = _cast_bf16(x2d, name="gather_weights", exchanges=[_gather_exchange([shards["w_in"], shards["w_mem_kv"]])])
    wi_t = wi4.reshape(IN_COLS, D_MODEL)

    grad_x, halves, small = _local_step(
        x2d, xb, mem.reshape(MEM_LEN, D_MODEL), loss_target.reshape(loss_target.shape[-2], D_MODEL),
        wi_t, wmkv, shards, lb_logits, hg_norm_gain, swa_sinks, rel_bias, ln1_g, ln1_b, ln2_g, ln2_b, distributed=True)

    reduced = dict(zip(BIG, _join_halves([halves[k] for k in BIG], name="join_halves")))

    outs = {k: _adamw(w[k], reduced[k], m[k], v[k], name="adamw_" + k) for k in BIG}
    loss, small_outs = _adamw_small(small, w, m, v, name="adamw_small")
    outs.update(small_outs)
    grad_out, delta_out, m_out, v_out = ({k: outs[k][i] for k in WEIGHTS} for i in range(4))
    for out in (grad_out, delta_out, m_out, v_out):
        out["w_in"] = out["w_in"][0].T

    result = [loss.reshape(()), grad_x.reshape(x.shape)]
    for out in (grad_out, delta_out, m_out, v_out):
        result += [out[k].reshape(shapes[k]) for k in WEIGHTS]
    return tuple(result)
```

```python
import functools
import math
from typing import Callable, NamedTuple, Optional

import jax
import jax.numpy as jnp
from jax import lax
from jax.experimental import pallas as pl
from jax.experimental.pallas import tpu as pltpu

F32 = jnp.float32
BF16 = jnp.bfloat16
HIGHEST = lax.Precision.HIGHEST
MESH = pl.DeviceIdType.MESH

D_MODEL = 1024
MEM_LEN = 256
HG_HEADS = 8
HG_DK = 128
HG_CHUNK = 64
SWA_HEADS = 16
SWA_KV_HEADS = 2
SWA_GROUP = 8
SWA_HEAD_DIM = 64
SWA_BLOCK = 128
SWA_WINDOW = 128
MEM_HEADS = 4
MEM_HEAD_DIM = 256
NUM_BUCKETS = 32
MAX_DISTANCE = 128
D_FF = 4096
LN_EPS = 1e-5
RMS_EPS = 1e-6
ALPHA = 2.0 ** 0.25
W_A, W_B, W_C, W_D = 4096, 1280, 1024, 3072
IN_COLS = W_A + W_B + W_C + W_D
N_SHARDS = 4
ADAM_LR = 0.001
ADAM_B1 = 0.9
ADAM_B2 = 0.999
ADAM_EPS = 1e-08
ADAM_WD = 0.01
ADAM_STEP = 10
MASK_VALUE = -1e30
VMEM_LIMIT = 56 * 1024 * 1024

NN = ((1,), (0,))
NT = ((1,), (1,))
TN = ((0,), (0,))
HBM = pl.BlockSpec(memory_space=pltpu.HBM)


def _dot(a, b, dims=NN, precision=None):
    return lax.dot_general(a, b, (dims, ((), ())), precision=precision, preferred_element_type=F32)


def _params(sem=None):
    return pltpu.CompilerParams(dimension_semantics=sem, vmem_limit_bytes=VMEM_LIMIT)


def _resident(shape):
    zeros = (0,) * len(shape)
    return pl.BlockSpec(shape, lambda *_: zeros, pipeline_mode=pl.Buffered(1))


def _resident_rows(arr, offset, rows):
    return pl.BlockSpec((pl.Element(rows), pl.Element(arr.shape[1])), lambda *_: (offset, 0), pipeline_mode=pl.Buffered(1))


def _mm(a, b, *, mode, tm, tn, tk, name, out_dtype=F32, b_panels=False, b_rows=None, out_panels=False, rows_of=None, row_offset=0,
        into=None):
    if mode == "tn":
        kdim, m = a.shape
    else:
        m, kdim = a.shape
    if b_panels:
        n = b.shape[0] * b.shape[2]
        assert b.shape[2] == tn and mode == "nn"
    elif b_rows is not None:
        assert mode == "nt"
        b_offset, n = b_rows
    elif mode == "nt":
        n = b.shape[0]
    else:
        n = b.shape[1]
    assert m % tm == 0 and n % tn == 0 and kdim % tk == 0, (name, m, n, kdim)
    nk = kdim // tk
    dims = {"nn": NN, "nt": NT, "tn": TN}[mode]
    a_spec = pl.BlockSpec((tk, tm), lambda i, j, k: (k, i)) if mode == "tn" else pl.BlockSpec((tm, tk), lambda i, j, k: (i, k))
    if b_panels:
        b_spec = pl.BlockSpec((None, tk, tn), lambda i, j, k: (j, k, 0))
    elif b_rows is not None:
        assert b_offset % BF16_SUBLANES == 0 and tn % BF16_SUBLANES == 0 and tk % 128 == 0
        b_spec = pl.BlockSpec((pl.Element(tn), pl.Element(tk)),
                              lambda i, j, k: (pl.multiple_of(b_offset + j * tn, BF16_SUBLANES), pl.multiple_of(k * tk, 128)))
    elif mode == "nt":
        b_spec = pl.BlockSpec((tn, tk), lambda i, j, k: (j, k))
    else:
        b_spec = pl.BlockSpec((tk, tn), lambda i, j, k: (k, j))
    in_specs = [a_spec, b_spec]
    operands = [a, b]
    aliases = {}
    if out_panels:
        out_shape = jax.ShapeDtypeStruct((n // tn, m, tn), out_dtype)
        o_spec = pl.BlockSpec((None, tm, tn), lambda i, j, k: (j, i, 0))
    elif rows_of is not None:
        out_shape = jax.ShapeDtypeStruct((rows_of, n), out_dtype)
        assert row_offset % BF16_SUBLANES == 0 and tm % BF16_SUBLANES == 0 and tn % 128 == 0
        o_spec = pl.BlockSpec((pl.Element(tm), pl.Element(tn)),
                              lambda i, j, k: (pl.multiple_of(row_offset + i * tm, BF16_SUBLANES), pl.multiple_of(j * tn, 128)))
        if into is not None:
            in_specs.append(pl.BlockSpec(memory_space=pl.ANY))
            operands.append(into)
            aliases = {2: 0}
    else:
        out_shape = jax.ShapeDtypeStruct((m, n), out_dtype)
        o_spec = pl.BlockSpec((tm, tn), lambda i, j, k: (i, j))
    n_in = len(operands)

    def body(*refs):
        a_ref, b_ref, o_ref = refs[0], refs[1], refs[n_in]
        part = _dot(a_ref[...].astype(BF16), b_ref[...].astype(BF16), dims)

        def finish(acc):
            o_ref[...] = acc.astype(out_dtype)

        if nk == 1:
            finish(part)
        else:
            acc_ref = refs[-1]
            k = pl.program_id(2)

            @pl.when(k == 0)
            def _():
                acc_ref[...] = part

            @pl.when(k > 0)
            def _():
                acc_ref[...] += part

            @pl.when(k == nk - 1)
            def _():
                finish(acc_ref[...])

    return pl.pallas_call(
        body, name=name, out_shape=out_shape, grid=(m // tm, n // tn, nk), in_specs=in_specs, out_specs=o_spec,
        scratch_shapes=[pltpu.VMEM((tm, tn), F32)] if nk > 1 else [], input_output_aliases=aliases,
        compiler_params=_params(("parallel", "parallel", "arbitrary")),
    )(*operands)


def _dx_matmul(dzs, wi_t, resid, *, tm, tiles, name, exchanges=()):
    first_tile, count = tiles
    npieces = len(dzs)
    offsets = [sum(dz.shape[1] for dz in dzs[:p]) for p in range(npieces)]
    tile = lambda i: (i + first_tile, 0)
    in_specs = [pl.BlockSpec((tm, dz.shape[1]), tile) for dz in dzs] + [_resident(wi_t.shape), pl.BlockSpec((tm, D_MODEL), tile)]

    def body(*refs):
        dz_refs, w_ref, r_ref, o_ref = refs[:npieces], refs[npieces], refs[npieces + 1], refs[npieces + 2]
        total = ALPHA * r_ref[...]
        for p in range(npieces):
            total = total + _dot(dz_refs[p][...], w_ref[offsets[p]:offsets[p] + dzs[p].shape[1], :], NN)
        o_ref[...] = total

    return _fused_call(
        body, name=name, out_shape=jax.ShapeDtypeStruct((count * tm, D_MODEL), F32), grid=(count,), in_specs=in_specs,
        out_specs=pl.BlockSpec((tm, D_MODEL), lambda i: (i, 0)), scratch_shapes=[], operands=[*dzs, wi_t, resid], exchanges=exchanges)


def _lower_bound(lbl_ref):
    l0, l1 = lbl_ref[0:1, :], lbl_ref[1:2, :]
    mx = jnp.maximum(l0, l1)
    e0, e1 = jnp.exp(l0 - mx), jnp.exp(l1 - mx)
    return e0 / (e0 + e1)


HEAD_COLS = [slice(h * HG_DK, (h + 1) * HG_DK) for h in range(HG_HEADS)]


def _head_mean(x):
    return jnp.concatenate([jnp.broadcast_to(jnp.mean(x[:, c], axis=-1, keepdims=True), (x.shape[0], HG_DK)) for c in HEAD_COLS], axis=1)


def _triangle_sum(tri_b, x):
    p0 = x.astype(BF16)
    r1 = x - p0.astype(F32)
    p1 = r1.astype(BF16)
    p2 = (r1 - p1.astype(F32)).astype(BF16)
    return _dot(tri_b, p0) + _dot(tri_b, p1) + _dot(tri_b, p2)


def _chunk_forward(q, fl, v, lb, tril_b):
    sg = jax.nn.sigmoid(fl)
    f = lb + (1.0 - lb) * sg
    k = 1.0 - f
    b = _triangle_sum(tril_b, jnp.log(f))
    b_last = b[HG_CHUNK - 1:HG_CHUNK, :]
    eb, enb, eo = jnp.exp(b), jnp.exp(-b), jnp.exp(b_last - b)
    return sg, f, k, b_last, eb, enb, eo, q * eb, k * enb, k * eo


def _hgrn_fwd(xb, wi_t, lb_logits, gain, *, name, exchanges=()):
    s = xb.shape[0]
    t = min(256, s)
    ncs = t // HG_CHUNK

    def body(x_ref, w_ref, lbl_ref, gain_ref, z_ref, oa_ref, oraw_ref, st_ref, state):
        @pl.when(pl.program_id(0) == 0)
        def _():
            state[...] = jnp.zeros_like(state)

        z_ref[...] = _dot(x_ref[...], w_ref[...], NT)
        lb_all = _lower_bound(lbl_ref)
        row = lax.broadcasted_iota(jnp.int32, (HG_CHUNK, HG_CHUNK), 0)
        col = lax.broadcasted_iota(jnp.int32, (HG_CHUNK, HG_CHUNK), 1)
        tril = row >= col
        tril_b = tril.astype(BF16)
        gain_all = gain_ref[...]

        def chunk(i, carry):
            r = pl.ds(pl.multiple_of(i * HG_CHUNK, HG_CHUNK), HG_CHUNK)
            q, fl, v, hg = (z_ref[r, j * D_MODEL:(j + 1) * D_MODEL] for j in range(4))
            _, _, _, b_last, _, _, _, q_in, k_in, k_out = _chunk_forward(q, fl, v, lb_all, tril_b)
            q_in_b, k_in_b, k_out_b, vb = (u.astype(BF16) for u in (q_in, k_in, k_out, v))
            decay = jnp.exp(b_last)
            sts = [state[h] for h in range(HG_HEADS)]
            attn = [_dot(q_in_b[:, c], k_in_b[:, c], NT) for c in HEAD_COLS]
            inter = [_dot(q_in_b[:, c], sts[h].astype(BF16), NT) for h, c in enumerate(HEAD_COLS)]
            upd = [_dot(vb[:, c], k_out_b[:, c], TN) for c in HEAD_COLS]
            attn = [jnp.where(tril, a, 0.0).astype(BF16) for a in attn]
            outs = [_dot(attn[h], vb[:, c], NN) + inter[h] for h, c in enumerate(HEAD_COLS)]
            for h, c in enumerate(HEAD_COLS):
                st_ref[h, i] = sts[h]
                state[h] = sts[h] * decay[:, c] + upd[h]
            o = jnp.concatenate(outs, axis=1)
            oraw_ref[r, :] = o
            n = o * lax.rsqrt(_head_mean(o * o) + RMS_EPS)
            oa_ref[r, :] = (n * gain_all * (hg * jax.nn.sigmoid(hg))).astype(BF16)
            return carry

        lax.fori_loop(0, ncs, chunk, 0, unroll=True)

    tile = lambda i: (i, 0)
    return _fused_call(
        body, name=name, grid=(s // t,),
        out_shape=(jax.ShapeDtypeStruct((s, W_A), F32), jax.ShapeDtypeStruct((s, D_MODEL), BF16), jax.ShapeDtypeStruct((s, D_MODEL), F32),
                   jax.ShapeDtypeStruct((HG_HEADS, s // HG_CHUNK, HG_DK, HG_DK), F32)),
        in_specs=[pl.BlockSpec((t, D_MODEL), tile), _resident_rows(wi_t, 0, W_A), _resident((2, D_MODEL)), _resident((1, D_MODEL))],
        out_specs=(pl.BlockSpec((t, W_A), tile), pl.BlockSpec((t, D_MODEL), tile), pl.BlockSpec((t, D_MODEL), tile),
                   pl.BlockSpec((HG_HEADS, ncs, HG_DK, HG_DK), lambda i: (0, i, 0, 0))),
        scratch_shapes=[pltpu.VMEM((HG_HEADS, HG_DK, HG_DK), F32)],
        operands=[xb, wi_t, lb_logits, gain], exchanges=exchanges)


def _hgrn_bwd(za, oraw, do_a, states, lb_logits, gain, *, name, exchanges=()):
    s = za.shape[0]
    t = min(256, s)
    ncs = t // HG_CHUNK
    nt = s // t

    def body(z_ref, oraw_ref, do_ref, st_ref, lbl_ref, gain_ref, dz_ref, stats_ref, dstate):
        step = pl.program_id(0)

        @pl.when(step == 0)
        def _():
            dstate[...] = jnp.zeros_like(dstate)
            stats_ref[...] = jnp.zeros_like(stats_ref)

        lb_all = _lower_bound(lbl_ref)
        row = lax.broadcasted_iota(jnp.int32, (HG_CHUNK, HG_CHUNK), 0)
        col = lax.broadcasted_iota(jnp.int32, (HG_CHUNK, HG_CHUNK), 1)
        tril = row >= col
        tril_b = tril.astype(BF16)
        triu_b = (row <= col).astype(BF16)
        gain_all = gain_ref[...]

        def chunk(ii, carry):
            i = ncs - 1 - ii
            r = pl.ds(pl.multiple_of(i * HG_CHUNK, HG_CHUNK), HG_CHUNK)
            q, fl, v, hg = (z_ref[r, j * D_MODEL:(j + 1) * D_MODEL] for j in range(4))
            o = oraw_ref[r, :]
            doa = do_ref[r, :]
            rms = lax.rsqrt(_head_mean(o * o) + RMS_EPS)
            n = o * rms
            sgg = jax.nn.sigmoid(hg)
            silu = hg * sgg
            dhg = doa * n * gain_all * (sgg * (1.0 + hg * (1.0 - sgg)))
            dgain = jnp.sum(doa * n * silu, axis=0, keepdims=True)
            dn = doa * gain_all * silu
            do = rms * (dn - n * _head_mean(dn * n))
            sg, f, k, b_last, eb, enb, eo, q_in, k_in, k_out = _chunk_forward(q, fl, v, lb_all, tril_b)
            q_in_b, k_in_b, k_out_b, vb, dob = (u.astype(BF16) for u in (q_in, k_in, k_out, v, do))
            decay = jnp.exp(b_last)
            sts = [st_ref[h, i] for h in range(HG_HEADS)]
            dsts = [dstate[h] for h in range(HG_HEADS)]
            dsts_b = [d.astype(BF16) for d in dsts]
            heads = list(enumerate(HEAD_COLS))
            attn = [_dot(q_in_b[:, c], k_in_b[:, c], NT) for h, c in heads]
            dattn = [_dot(dob[:, c], vb[:, c], NT) for h, c in heads]
            dq_st = [_dot(dob[:, c], sts[h].astype(BF16), NN) for h, c in heads]
            dk_out = [_dot(vb[:, c], dsts_b[h], NN) for h, c in heads]
            dv_st = [_dot(k_out_b[:, c], dsts_b[h], NT) for h, c in heads]
            dst_o = [_dot(dob[:, c], q_in_b[:, c], TN) for h, c in heads]
            attn = [jnp.where(tril, a, 0.0).astype(BF16) for a in attn]
            dattn = [jnp.where(tril, a, 0.0).astype(BF16) for a in dattn]
            dq_in = jnp.concatenate([_dot(dattn[h], k_in_b[:, c], NN) + dq_st[h] for h, c in heads], axis=1)
            dk_in = jnp.concatenate([_dot(dattn[h], q_in_b[:, c], TN) for h, c in heads], axis=1)
            dv = jnp.concatenate([_dot(attn[h], dob[:, c], TN) + dv_st[h] for h, c in heads], axis=1)
            dk_out = jnp.concatenate(dk_out, axis=1)
            dst_st = jnp.concatenate([jnp.sum(dsts[h] * sts[h], axis=0, keepdims=True) for h in range(HG_HEADS)], axis=1)
            for h, c in heads:
                dstate[h] = dsts[h] * decay[:, c] + dst_o[h]
            db_last = decay * dst_st + jnp.sum(dk_out * k_out, axis=0, keepdims=True)
            db = dq_in * q_in - dk_in * k_in - dk_out * k_out
            dg = _triangle_sum(triu_b, db) + db_last
            dk = dk_in * enb + dk_out * eo
            df = dg / f - dk
            stats_ref[0:1, :] += dgain
            stats_ref[1:2, :] += jnp.sum(df * (1.0 - sg), axis=0, keepdims=True)
            dz_ref[r, 0:1024] = (dq_in * eb).astype(BF16)
            dz_ref[r, 1024:2048] = (df * (1.0 - lb_all) * sg * (1.0 - sg)).astype(BF16)
            dz_ref[r, 2048:3072] = dv.astype(BF16)
            dz_ref[r, 3072:4096] = dhg.astype(BF16)
            return carry

        lax.fori_loop(0, ncs, chunk, 0, unroll=True)

        @pl.when(step == nt - 1)
        def _():
            dl0 = stats_ref[1:2, :] * lb_all * (1.0 - lb_all)
            stats_ref[1:2, :] = dl0
            stats_ref[2:3, :] = -dl0

    rev = lambda i: (nt - 1 - i, 0)
    return _fused_call(
        body, name=name, grid=(nt,),
        out_shape=(jax.ShapeDtypeStruct((s, W_A), BF16), jax.ShapeDtypeStruct((8, D_MODEL), F32)),
        in_specs=[pl.BlockSpec((t, W_A), rev), pl.BlockSpec((t, D_MODEL), rev), pl.BlockSpec((t, D_MODEL), rev),
                  pl.BlockSpec((HG_HEADS, ncs, HG_DK, HG_DK), lambda i: (0, nt - 1 - i, 0, 0)),
                  _resident((2, D_MODEL)), _resident((1, D_MODEL))],
        out_specs=(pl.BlockSpec((t, W_A), rev), pl.BlockSpec((8, D_MODEL), lambda i: (0, 0))),
        scratch_shapes=[pltpu.VMEM((HG_HEADS, HG_DK, HG_DK), F32)],
        operands=[za, oraw, do_a, states, lb_logits, gain], exchanges=exchanges)


def _t5_bucket(n):
    max_exact = NUM_BUCKETS // 2
    nf = jnp.maximum(n, 1).astype(F32)
    large = max_exact + (jnp.log(nf / max_exact) / math.log(MAX_DISTANCE / max_exact) * (NUM_BUCKETS - max_exact)).astype(jnp.int32)
    large = jnp.minimum(large, NUM_BUCKETS - 1)
    return jnp.where(n < max_exact, n, large)


def _bias_selector():
    qi = jnp.arange(SWA_BLOCK)[:, None] + SWA_BLOCK
    kj = jnp.arange(2 * SWA_BLOCK)[None, :]
    dist = qi - kj
    band = ((dist >= 0) & (dist < SWA_WINDOW)).reshape(1, -1)
    bucket = _t5_bucket(jnp.clip(dist, 0, SWA_WINDOW - 1)).reshape(1, -1)
    onehot = ((bucket == jnp.arange(NUM_BUCKETS)[:, None]) & band).astype(F32)
    return onehot, jnp.where(band, 0.0, MASK_VALUE).astype(F32)


def _bias_table(rel_bias_t, onehot, maskrow, *, name):
    def body(rb_ref, oh_ref, mask_ref, o_ref):
        o_ref[...] = _dot(rb_ref[...], oh_ref[...], NN, HIGHEST) + mask_ref[...]

    return pl.pallas_call(body, name=name, out_shape=jax.ShapeDtypeStruct((SWA_HEADS, onehot.shape[1]), F32),
                          compiler_params=_params())(rel_bias_t, onehot, maskrow)


def _bias_grad(dbias2d, onehot, *, name):
    def body(db_ref, oh_ref, o_ref):
        o_ref[...] = _dot(db_ref[...], oh_ref[...], NT, HIGHEST)

    return pl.pallas_call(body, name=name, out_shape=jax.ShapeDtypeStruct((SWA_HEADS, NUM_BUCKETS), F32),
                          compiler_params=_params())(dbias2d, onehot)


def _swa_operands(zq_ref, kv_cur_ref, kv_prev_ref):
    q = (zq_ref[:, 0:1024] * (SWA_HEAD_DIM ** -0.5)).astype(BF16)
    kv_c = kv_cur_ref[...].astype(BF16)
    kv_p = kv_prev_ref[...].astype(BF16)
    kks = [jnp.concatenate([kv_p[:, g * 64:(g + 1) * 64], kv_c[:, g * 64:(g + 1) * 64]], axis=0) for g in range(SWA_KV_HEADS)]
    vvs = [jnp.concatenate([kv_p[:, 128 + g * 64:128 + (g + 1) * 64], kv_c[:, 128 + g * 64:128 + (g + 1) * 64]], axis=0)
           for g in range(SWA_KV_HEADS)]
    return q, kks, vvs


SWA_PART_HEADS = 8
SWA_PARTS = [(h0 // SWA_GROUP, h0) for h0 in range(0, SWA_HEADS, SWA_PART_HEADS)]


def _part_lanes(h0):
    return slice(h0 * SWA_BLOCK, (h0 + SWA_PART_HEADS) * SWA_BLOCK)


def _stack_heads(x, h0):
    return jnp.concatenate([x[:, h * SWA_HEAD_DIM:(h + 1) * SWA_HEAD_DIM] for h in range(h0, h0 + SWA_PART_HEADS)], axis=0)


def _heads_to_lanes(xt):
    pairs = []
    for j in range(0, xt.shape[1] // SWA_BLOCK, 2):
        two = jnp.concatenate([xt[:, j * SWA_BLOCK:(j + 1) * SWA_BLOCK], xt[:, (j + 1) * SWA_BLOCK:(j + 2) * SWA_BLOCK]], axis=0)
        pairs.append(two.T)
    return jnp.concatenate(pairs, axis=1)


def _swa_softmax(score_t, bias_ref, sink_ref, h0):
    sc = score_t + bias_ref[:, _part_lanes(h0)]
    sink = sink_ref[:, _part_lanes(h0)]
    m = jnp.maximum(jnp.max(sc, axis=0, keepdims=True), sink)
    e = jnp.exp(sc - m)
    e_sink = jnp.exp(sink - m)
    return e, 1.0 / (jnp.sum(e, axis=0, keepdims=True) + e_sink), e_sink


def _swa_tables(bias2d, sinks):
    bias_t = bias2d.reshape(SWA_HEADS, SWA_BLOCK, 2 * SWA_BLOCK).transpose(2, 0, 1).reshape(2 * SWA_BLOCK, SWA_HEADS * SWA_BLOCK)
    first = jnp.where(jnp.arange(2 * SWA_BLOCK)[:, None] < SWA_BLOCK, MASK_VALUE, bias_t)
    return jnp.stack([first, bias_t]), jnp.repeat(sinks, SWA_BLOCK, axis=1)


def _swa_fwd(zb, bias_tables, sink_lanes, *, name, exchanges=()):
    s = zb.shape[0]
    nb = s // SWA_BLOCK

    def body(zq_ref, kvc_ref, kvp_ref, bias_ref, sink_ref, o_ref):
        q, kks, vvs = _swa_operands(zq_ref, kvc_ref, kvp_ref)
        scores = [_dot(kks[g], _stack_heads(q, h0), NT) for g, h0 in SWA_PARTS]
        probs = []
        for score, (_, h0) in zip(scores, SWA_PARTS):
            e, inv, _ = _swa_softmax(score, bias_ref, sink_ref, h0)
            probs.append((e * inv).astype(BF16))
        outs = [_dot(vvs[g], p, TN) for p, (g, _) in zip(probs, SWA_PARTS)]
        o_ref[...] = jnp.concatenate([_heads_to_lanes(o) for o in outs], axis=1).astype(BF16)

    return _fused_call(
        body, name=name, grid=(nb,), out_shape=jax.ShapeDtypeStruct((s, D_MODEL), BF16),
        in_specs=[pl.BlockSpec((SWA_BLOCK, W_B), lambda n: (n, 0)),
                  pl.BlockSpec((SWA_BLOCK, 256), lambda n: (n, 4)),
                  pl.BlockSpec((SWA_BLOCK, 256), lambda n: (jnp.maximum(n - 1, 0), 4)),
                  pl.BlockSpec((None, 2 * SWA_BLOCK, SWA_HEADS * SWA_BLOCK), lambda n: (jnp.minimum(n, 1), 0, 0)),
                  _resident((1, SWA_HEADS * SWA_BLOCK))],
        out_specs=pl.BlockSpec((SWA_BLOCK, D_MODEL), lambda n: (n, 0)), scratch_shapes=[],
        operands=[zb, zb, zb, bias_tables, sink_lanes], exchanges=exchanges)


def _swa_bwd(zb, do_b, bias_tables, sink_lanes, *, name, exchanges=()):
    s = zb.shape[0]
    nb = s // SWA_BLOCK
    scale = SWA_HEAD_DIM ** -0.5

    def body(zq_ref, kvc_ref, kvp_ref, do_ref, bias_ref, sink_ref, dz_ref, dbias_ref, dsink_ref, carry, dsink_acc):
        step = pl.program_id(0)

        @pl.when(step == 0)
        def _():
            carry[...] = jnp.zeros_like(carry)
            dsink_acc[...] = jnp.zeros_like(dsink_acc)
            dbias_ref[...] = jnp.zeros_like(dbias_ref)

        q, kks, vvs = _swa_operands(zq_ref, kvc_ref, kvp_ref)
        do = do_ref[...].astype(BF16)
        parts = range(len(SWA_PARTS))
        q_rows = [_stack_heads(q, h0) for _, h0 in SWA_PARTS]
        do_rows = [_stack_heads(do, h0) for _, h0 in SWA_PARTS]
        scores = [_dot(kks[g], q_rows[i], NT) for i, (g, _) in enumerate(SWA_PARTS)]
        soft = [_swa_softmax(scores[i], bias_ref, sink_ref, h0) for i, (_, h0) in enumerate(SWA_PARTS)]
        dps = [_dot(vvs[g], do_rows[i], NT) for i, (g, _) in enumerate(SWA_PARTS)]
        ps, dss = [], []
        for i, (_, h0) in enumerate(SWA_PARTS):
            e, inv, e_sink = soft[i]
            p = e * inv
            delta = jnp.sum(p * dps[i], axis=0, keepdims=True)
            ds = p * (dps[i] - delta)
            dbias_ref[:, _part_lanes(h0)] += ds
            dsink_acc[:, _part_lanes(h0)] -= e_sink * inv * delta
            ps.append(p.astype(BF16))
            dss.append(ds.astype(BF16))
        dqs = [_dot(kks[g], dss[i], TN) * scale for i, (g, _) in enumerate(SWA_PARTS)]
        in_group = lambda xs, g, axis: jnp.concatenate([xs[i] for i in parts if SWA_PARTS[i][0] == g], axis=axis)
        dkks = [_dot(in_group(dss, g, 1), in_group(q_rows, g, 0), NN) for g in range(SWA_KV_HEADS)]
        dvvs = [_dot(in_group(ps, g, 1), in_group(do_rows, g, 0), NN) for g in range(SWA_KV_HEADS)]
        dkv = jnp.concatenate(dkks + dvvs, axis=1)
        dz_ref[:, 0:1024] = jnp.concatenate([_heads_to_lanes(dq) for dq in dqs], axis=1).astype(BF16)
        dz_ref[:, 1024:1280] = (dkv[SWA_BLOCK:, :] + carry[...]).astype(BF16)
        carry[...] = dkv[:SWA_BLOCK, :]

        @pl.when(step == nb - 1)
        def _():
            acc = dsink_acc[...]
            dsink_ref[...] = jnp.concatenate([jnp.sum(acc[:, h * SWA_BLOCK:(h + 1) * SWA_BLOCK], axis=1, keepdims=True)
                                              for h in range(SWA_HEADS)], axis=1)

    rev = lambda i: (nb - 1 - i, 0)
    table_shape = (2 * SWA_BLOCK, SWA_HEADS * SWA_BLOCK)
    return _fused_call(
        body, name=name, grid=(nb,),
        out_shape=(jax.ShapeDtypeStruct((s, W_B), BF16), jax.ShapeDtypeStruct(table_shape, F32), jax.ShapeDtypeStruct((1, SWA_HEADS), F32)),
        in_specs=[pl.BlockSpec((SWA_BLOCK, W_B), rev),
                  pl.BlockSpec((SWA_BLOCK, 256), lambda i: (nb - 1 - i, 4)),
                  pl.BlockSpec((SWA_BLOCK, 256), lambda i: (jnp.maximum(nb - 2 - i, 0), 4)),
                  pl.BlockSpec((SWA_BLOCK, D_MODEL), rev),
                  pl.BlockSpec((None,) + table_shape, lambda i: (jnp.minimum(nb - 1 - i, 1), 0, 0)),
                  _resident((1, SWA_HEADS * SWA_BLOCK))],
        out_specs=(pl.BlockSpec((SWA_BLOCK, W_B), rev), pl.BlockSpec(table_shape, lambda i: (0, 0)),
                   pl.BlockSpec((1, SWA_HEADS), lambda i: (0, 0))),
        scratch_shapes=[pltpu.VMEM((SWA_BLOCK, 256), F32), pltpu.VMEM((1, SWA_HEADS * SWA_BLOCK), F32)],
        operands=[zb, zb, zb, do_b, bias_tables, sink_lanes], exchanges=exchanges)


MEM_COLS = [slice(h * MEM_HEAD_DIM, (h + 1) * MEM_HEAD_DIM) for h in range(MEM_HEADS)]
MEM_VCOLS = [slice(D_MODEL + h * MEM_HEAD_DIM, D_MODEL + (h + 1) * MEM_HEAD_DIM) for h in range(MEM_HEADS)]


def _mem_probs(zc_ref, mkv_ref):
    qs = [(zc_ref[:, c] * (MEM_HEAD_DIM ** -0.5)).astype(BF16) for c in MEM_COLS]
    scores = [_dot(qs[h], mkv_ref[:, c], NT) for h, c in enumerate(MEM_COLS)]
    ps = []
    for sc in scores:
        e = jnp.exp(sc - jnp.max(sc, axis=-1, keepdims=True))
        ps.append(e / jnp.sum(e, axis=-1, keepdims=True))
    return qs, ps


def _mem_fwd(xb, wi_t, mkv, *, name):
    s = xb.shape[0]
    t = min(512, s)

    def body(x_ref, w_ref, mkv_ref, zc_ref, o_ref):
        zc_ref[...] = _dot(x_ref[...], w_ref[...], NT).astype(BF16)
        _, ps = _mem_probs(zc_ref, mkv_ref)
        ps = [p.astype(BF16) for p in ps]
        o_ref[...] = jnp.concatenate([_dot(ps[h], mkv_ref[:, vc], NN) for h, vc in enumerate(MEM_VCOLS)], axis=1).astype(BF16)

    row = pl.BlockSpec((t, D_MODEL), lambda i: (i, 0))
    return pl.pallas_call(
        body, name=name, grid=(s // t,), out_shape=(jax.ShapeDtypeStruct((s, D_MODEL), BF16),) * 2,
        in_specs=[row, _resident_rows(wi_t, W_A + W_B, W_C), _resident((MEM_LEN, 2 * D_MODEL))],
        out_specs=(row, row), compiler_params=_params(("parallel",)),
    )(xb, wi_t, mkv)


def _mem_bwd(xb, zc, do_c, mkv, *, name):
    s = zc.shape[0]
    t = min(512, s)
    nt = s // t

    def body(x_ref, zc_ref, do_ref, mkv_ref, dz_ref, dmkv_ref, gwi_ref, acc):
        @pl.when(pl.program_id(0) == 0)
        def _():
            dmkv_ref[...] = jnp.zeros_like(dmkv_ref)
            acc[...] = jnp.zeros_like(acc)

        heads = range(MEM_HEADS)
        qs, ps = _mem_probs(zc_ref, mkv_ref)
        dos = [do_ref[:, c].astype(BF16) for c in MEM_COLS]
        dps = [_dot(dos[h], mkv_ref[:, MEM_VCOLS[h]], NT) for h in heads]
        dss = [(ps[h] * (dps[h] - jnp.sum(ps[h] * dps[h], axis=-1, keepdims=True))).astype(BF16) for h in heads]
        ps = [p.astype(BF16) for p in ps]
        dz = jnp.concatenate([_dot(dss[h], mkv_ref[:, MEM_COLS[h]], NN) * (MEM_HEAD_DIM ** -0.5) for h in heads], axis=1).astype(BF16)
        dz_ref[...] = dz
        dmkv_ref[...] += jnp.concatenate([_dot(dss[h], qs[h], TN) for h in heads] + [_dot(ps[h], dos[h], TN) for h in heads], axis=1)
        acc[...] += _dot(dz, x_ref[...], TN)

        @pl.when(pl.program_id(0) == nt - 1)
        def _():
            pltpu.sync_copy(acc, gwi_ref.at[pl.ds(W_A + W_B, W_C), :])

    row = pl.BlockSpec((t, D_MODEL), lambda i: (i, 0))
    return pl.pallas_call(
        body, name=name, grid=(nt,),
        out_shape=(jax.ShapeDtypeStruct((s, D_MODEL), BF16), jax.ShapeDtypeStruct((MEM_LEN, 2 * D_MODEL), F32),
                   jax.ShapeDtypeStruct((IN_COLS, D_MODEL), F32)),
        in_specs=[row, row, row, _resident((MEM_LEN, 2 * D_MODEL))],
        out_specs=(row, pl.BlockSpec((MEM_LEN, 2 * D_MODEL), lambda i: (0, 0)), HBM),
        scratch_shapes=[pltpu.VMEM((W_C, D_MODEL), F32)],
        compiler_params=_params(("arbitrary",)),
    )(xb, zc, do_c, mkv)


def _normalize(pre):
    mu = jnp.mean(pre, axis=-1, keepdims=True)
    xc = pre - mu
    rstd = lax.rsqrt(jnp.mean(xc * xc, axis=-1, keepdims=True) + LN_EPS)
    return xc * rstd, rstd


def _layer_norm_bwd(dh, xhat, rstd, g):
    dxh = dh * g
    dpre = rstd * (dxh - jnp.mean(dxh, axis=-1, keepdims=True) - xhat * jnp.mean(dxh * xhat, axis=-1, keepdims=True))
    return dpre, jnp.sum(dh * xhat, axis=0, keepdims=True), jnp.sum(dh, axis=0, keepdims=True)


def _merge_fwd(o_a, o_b, o_c, x, wi_t, wbr, wo, *, name):
    s = x.shape[0]
    t = min(256, s)
    row = lambda w: pl.BlockSpec((t, w), lambda i: (i, 0))

    def body(oa_ref, ob_ref, oc_ref, x_ref, wg_ref, wa_ref, wb_ref, wc_ref, wo_ref, zd_ref, xhat_ref, rstd_ref, merged_ref, pa_ref, pb_ref, pc_ref):
        wbr_refs = (wa_ref, wb_ref, wc_ref)
        zd_ref[...] = _dot(x_ref[...].astype(BF16), wg_ref[...], NT)
        merged = jnp.zeros((t, D_MODEL), F32)
        for b, (o_ref, p_ref) in enumerate(((oa_ref, pa_ref), (ob_ref, pb_ref), (oc_ref, pc_ref))):
            p = _dot(o_ref[...], wbr_refs[b][...], NN)
            p_ref[...] = p.astype(BF16)
            merged = merged + jax.nn.sigmoid(zd_ref[:, b * D_MODEL:(b + 1) * D_MODEL]) * p
        merged_b = merged.astype(BF16)
        merged_ref[...] = merged_b
        xhat, rstd = _normalize(ALPHA * x_ref[...] + _dot(merged_b, wo_ref[...], NN))
        xhat_ref[...] = xhat
        rstd_ref[...] = rstd

    act = jax.ShapeDtypeStruct((s, D_MODEL), F32)
    return pl.pallas_call(
        body, name=name, grid=(s // t,),
        out_shape=(jax.ShapeDtypeStruct((s, W_D), F32), act, jax.ShapeDtypeStruct((s, 1), F32)) + (jax.ShapeDtypeStruct((s, D_MODEL), BF16),) * 4,
        in_specs=[row(D_MODEL)] * 4 + [_resident_rows(wi_t, W_A + W_B + W_C, W_D)] + [_resident((D_MODEL, D_MODEL))] * 4,
        out_specs=(row(W_D), row(D_MODEL), row(1), row(D_MODEL), row(D_MODEL), row(D_MODEL), row(D_MODEL)),
        compiler_params=_params(("parallel",)),
    )(o_a, o_b, o_c, x, wi_t, *wbr, wo)


def _merge_bwd(dpre1, zd, pa, pb, pc, o_a, o_b, o_c, merged, wbr, wo, *, name, exchanges=()):
    s = dpre1.shape[0]
    t = min(256, s)
    nt = s // t
    row = lambda w: pl.BlockSpec((t, w), lambda i: (i, 0))

    def body(dpre_ref, zd_ref, pa_ref, pb_ref, pc_ref, oa_ref, ob_ref, oc_ref, mg_ref, wa_ref, wb_ref, wc_ref, wo_ref,
             dzd_ref, doa_ref, dob_ref, doc_ref, gwa_ref, gwb_ref, gwc_ref, gwo_ref, acc):
        step = pl.program_id(0)

        @pl.when(step == 0)
        def _():
            acc[...] = jnp.zeros_like(acc)

        dpre_b = dpre_ref[...].astype(BF16)
        dmerged = _dot(dpre_b, wo_ref[...], NT)
        acc[3] += _dot(mg_ref[...], dpre_b, TN)
        branches = ((pa_ref, oa_ref, doa_ref), (pb_ref, ob_ref, dob_ref), (pc_ref, oc_ref, doc_ref))
        for b, (p_ref, o_ref, do_ref) in enumerate(branches):
            gate = jax.nn.sigmoid(zd_ref[:, b * D_MODEL:(b + 1) * D_MODEL])
            dzd_ref[:, b * D_MODEL:(b + 1) * D_MODEL] = (dmerged * p_ref[...] * gate * (1.0 - gate)).astype(BF16)
            dp = (dmerged * gate).astype(BF16)
            acc[b] += _dot(o_ref[...], dp, TN)
            do_ref[...] = _dot(dp, (wa_ref, wb_ref, wc_ref)[b][...], NT).astype(do_ref.dtype)

        @pl.when(step == nt - 1)
        def _():
            for b, gw_ref in enumerate((gwa_ref, gwb_ref, gwc_ref, gwo_ref)):
                pltpu.sync_copy(acc.at[b], gw_ref)

    act = jax.ShapeDtypeStruct((s, D_MODEL), F32)
    actb = jax.ShapeDtypeStruct((s, D_MODEL), BF16)
    gw = jax.ShapeDtypeStruct((D_MODEL, D_MODEL), F32)
    return _fused_call(
        body, name=name, grid=(nt,),
        out_shape=(jax.ShapeDtypeStruct((s, W_D), BF16), act, actb, actb, gw, gw, gw, gw),
        in_specs=[row(D_MODEL), row(W_D)] + [row(D_MODEL)] * 7 + [_resident((D_MODEL, D_MODEL))] * 4,
        out_specs=(row(W_D),) + (row(D_MODEL),) * 3 + (HBM,) * 4, scratch_shapes=[pltpu.VMEM((4, D_MODEL, D_MODEL), F32)],
        operands=[dpre1, zd, pa, pb, pc, o_a, o_b, o_c, merged, *wbr, wo], exchanges=exchanges)


def _mlp_loss(xhat1, rstd1, target, ln1_g, ln1_b, ln2_g, ln2_b, wu, wd, *, name):
    s = xhat1.shape[0]
    t = min(256, s)
    npan = wu.shape[0]
    row = lambda w: pl.BlockSpec((t, w), lambda i: (i, 0))
    vec = _resident((1, D_MODEL))

    def body(xhat_ref, rstd_ref, tgt_ref, g1_ref, b1_ref, g2_ref, b2_ref, wu_ref, wd_ref,
             dpre1_ref, dpre2_ref, h1_ref, a_ref, du_ref, stats_ref):
        @pl.when(pl.program_id(0) == 0)
        def _():
            stats_ref[...] = jnp.zeros_like(stats_ref)

        xhat1_v = xhat_ref[...]
        h1 = xhat1_v * g1_ref[...] + b1_ref[...]
        h1_b = h1.astype(BF16)
        h1_ref[...] = h1_b
        us = []
        ff = jnp.zeros((t, D_MODEL), F32)
        for j in range(npan):
            u = _dot(h1_b, wu_ref[j], NN)
            us.append(u)
            r = jnp.maximum(u, 0.0)
            a_b = (r * r).astype(BF16)
            a_ref[:, j * D_MODEL:(j + 1) * D_MODEL] = a_b
            ff = ff + _dot(a_b, wd_ref[j], NN)
        xhat2, rstd2 = _normalize(ALPHA * h1 + ff)
        err = xhat2 * g2_ref[...] + b2_ref[...] - tgt_ref[...]
        stats_ref[4:5, :] += jnp.sum(err * err, axis=0, keepdims=True)
        dpre2, dg2, db2 = _layer_norm_bwd(err * (1.0 / D_MODEL), xhat2, rstd2, g2_ref[...])
        stats_ref[0:1, :] += dg2
        stats_ref[1:2, :] += db2
        dpre2_b = dpre2.astype(BF16)
        dpre2_ref[...] = dpre2_b
        dh1 = ALPHA * dpre2
        for j in range(npan):
            du_b = (_dot(dpre2_b, wd_ref[j], NT) * (2.0 * jnp.maximum(us[j], 0.0))).astype(BF16)
            du_ref[:, j * D_MODEL:(j + 1) * D_MODEL] = du_b
            dh1 = dh1 + _dot(du_b, wu_ref[j], NT)
        dpre1, dg1, db1 = _layer_norm_bwd(dh1, xhat1_v, rstd_ref[...], g1_ref[...])
        stats_ref[2:3, :] += dg1
        stats_ref[3:4, :] += db1
        dpre1_ref[...] = dpre1

    actb = jax.ShapeDtypeStruct((s, D_MODEL), BF16)
    wide = jax.ShapeDtypeStruct((s, D_FF), BF16)
    return pl.pallas_call(
        body, name=name, grid=(s // t,),
        out_shape=(jax.ShapeDtypeStruct((s, D_MODEL), F32), actb, actb, wide, wide, jax.ShapeDtypeStruct((8, D_MODEL), F32)),
        in_specs=[row(D_MODEL), row(1), row(D_MODEL), vec, vec, vec, vec,
                  _resident((npan, D_MODEL, D_MODEL)), _resident((npan, D_MODEL, D_MODEL))],
        out_specs=(row(D_MODEL), row(D_MODEL), row(D_MODEL), row(D_FF), row(D_FF), pl.BlockSpec((8, D_MODEL), lambda i: (0, 0))),
        compiler_params=_params(("arbitrary",)),
    )(xhat1, rstd1, target, ln1_g, ln1_b, ln2_g, ln2_b, wu, wd)


BRANCH_WEIGHTS = ("w_branch_hg", "w_branch_swa", "w_branch_mem")


def _local_step(x, xb, mem, target, wi_t, wmkv, late, lb_logits, gain, sinks, rel_bias, ln1_g, ln1_b, ln2_g, ln2_b, *, distributed):
    s = x.shape[0]
    tm = min(1024, s)
    tk = min(2048, s)
    memb = mem.astype(BF16)
    if distributed:
        cx, cy, cc = lax.axis_index("x"), lax.axis_index("y"), lax.axis_index("c")
        pos = jnp.stack([2 * cx + cy, cc]).astype(jnp.int32)
    gather = (lambda names: [_gather_exchange([late[k] for k in names])]) if distributed else (lambda names: [])
    to_sibling = (lambda grads: [_sibling_halves_exchange(grads)]) if distributed else (lambda grads: [])
    to_chips = (lambda sums: [_chip_partials_exchange([bf for bf, _ in sums])]) if distributed else (lambda sums: [])

    def chip_sums(names, grads, from_sibling):
        return [_add_sibling(g, o, pos, name="add_sibling_" + k) for k, g, o in zip(names, grads, from_sibling)]

    def shard_sums(names, sums, from_chips):
        return {k: _add_chips(mine, o, pos, name="add_chips_" + k) for k, (_, mine), o in zip(names, sums, from_chips)}

    zb = _mm(xb, wi_t, mode="nt", tm=tm, tn=W_B, tk=D_MODEL, name="proj_b", out_dtype=BF16, b_rows=(W_A, W_B))
    mkv = _mm(memb, wmkv, mode="nn", tm=MEM_LEN, tn=512, tk=D_MODEL, name="mem_kv", out_dtype=BF16, b_panels=True)
    onehot, maskrow = _bias_selector()
    bias_tables, sink_lanes = _swa_tables(_bias_table(rel_bias.T, onehot, maskrow, name="bias_table"), sinks)
    (za, o_a, o_raw, states), landed = _hgrn_fwd(xb, wi_t, lb_logits, gain, name="hgrn_fwd", exchanges=gather(("w_up", "w_down")))
    wu, wd = landed[0] if distributed else (late["wu"], late["wd"])
    o_b, landed = _swa_fwd(zb, bias_tables, sink_lanes, name="swa_fwd", exchanges=gather(BRANCH_WEIGHTS + ("w_out",)))
    if distributed:
        wbr = [wb.reshape(D_MODEL, D_MODEL) for wb in landed[0][:3]]
        wo = landed[0][3].reshape(D_MODEL, D_MODEL)
    else:
        wbr, wo = [late["wbr"][b] for b in range(3)], late["wo"]
    zc, o_c = _mem_fwd(xb, wi_t, mkv, name="mem_fwd")
    zd, xhat1, rstd1, merged, pa, pb, pc = _merge_fwd(o_a, o_b, o_c, x, wi_t, wbr, wo, name="merge_fwd")

    dpre1, dpre2, h1, act, du, ln_stats = _mlp_loss(xhat1, rstd1, target, ln1_g, ln1_b, ln2_g, ln2_b, wu, wd, name="mlp_loss")
    ffn = ("w_down", "w_up")
    g_ffn = [_mm(act, dpre2, mode="tn", tm=1024, tn=D_MODEL, tk=tk, name="grad_w_down").reshape(N_SHARDS, D_FF // N_SHARDS, D_MODEL),
             _mm(h1, du, mode="tn", tm=D_MODEL, tn=1024, tk=tk, name="grad_w_up", out_panels=True)]

    (dzd, do_a, do_b, do_c, *g_merge), landed = _merge_bwd(dpre1, zd, pa, pb, pc, o_a, o_b, o_c, merged, wbr, wo, name="merge_bwd",
                                                           exchanges=to_sibling(g_ffn))
    sums_ffn = chip_sums(ffn, g_ffn, landed[0]) if distributed else []
    merge = BRANCH_WEIGHTS + ("w_out",)
    g_merge = [g.reshape(N_SHARDS, D_MODEL // N_SHARDS, D_MODEL) for g in g_merge]
    (dza, hg_stats), landed = _hgrn_bwd(za, o_raw, do_a, states, lb_logits, gain, name="hgrn_bwd",
                                        exchanges=to_chips(sums_ffn) + to_sibling(g_merge))
    halves = shard_sums(ffn, sums_ffn, landed[0]) if distributed else {}
    sums_merge = chip_sums(merge, g_merge, landed[1]) if distributed else []
    (dzb, dbias_t, dsinks), landed = _swa_bwd(zb, do_b, bias_tables, sink_lanes, name="swa_bwd", exchanges=to_chips(sums_merge))
    if distributed:
        halves.update(shard_sums(merge, sums_merge, landed[0]))
    dbias = dbias_t.reshape(2 * SWA_BLOCK, SWA_HEADS, SWA_BLOCK).transpose(1, 2, 0).reshape(SWA_HEADS, -1)
    d_rel_bias = _bias_grad(dbias, onehot, name="bias_grad").T
    dzc, dmkv, g_wi = _mem_bwd(xb, zc, do_c, mkv, name="mem_bwd")

    proj = ("w_in", "w_mem_kv")
    for dz, offset, nm in ((dza, 0, "grad_w_in_a"), (dzb, W_A, "grad_w_in_b"), (dzd, W_A + W_B + W_C, "grad_w_in_d")):
        g_wi = _mm(dz, xb, mode="tn", tm=dz.shape[1] if dz.shape[1] <= 1280 else 1024, tn=D_MODEL, tk=tk, name=nm,
                   rows_of=IN_COLS, row_offset=offset, into=g_wi)
    g_proj = [g_wi.reshape(N_SHARDS, IN_COLS // N_SHARDS, D_MODEL),
              _mm(memb, dmkv, mode="tn", tm=D_MODEL, tn=512, tk=MEM_LEN, name="grad_w_mem_kv", out_panels=True)]
    small = dict(lb_logits=hg_stats[1:3], hg_norm_gain=hg_stats[0:1], swa_sinks=dsinks, rel_bias=d_rel_bias,
                 ln1_g=ln_stats[2:3], ln1_b=ln_stats[3:4], ln2_g=ln_stats[0:1], ln2_b=ln_stats[1:2], sq_err=ln_stats[4:5])
    small_exchange = [_small_gather_exchange(_pack_small(small, name="pack_small"))] if distributed else []
    tx = min(512, s // 2)
    head = max(1, 3 * (s // tx) // 16)
    dx = functools.partial(_dx_matmul, [dza, dzb, dzc, dzd], wi_t, dpre1, tm=tx)
    grad_x_head, landed = dx(tiles=(0, head), name="grad_x_head", exchanges=to_sibling(g_proj))
    sums_proj = chip_sums(proj, g_proj, landed[0]) if distributed else []
    grad_x_tail, landed = dx(tiles=(head, s // tx - head), name="grad_x", exchanges=to_chips(sums_proj) + small_exchange)
    grad_x = jnp.concatenate([grad_x_head, grad_x_tail])
    if distributed:
        halves.update(shard_sums(proj, sums_proj, landed[0]))
        small = landed[1][0]
    else:
        halves = dict(zip(ffn + merge + proj, g_ffn + g_merge + g_proj))
    return grad_x, halves, small


def _mesh_position():
    x, y, c = lax.axis_index("x"), lax.axis_index("y"), lax.axis_index("c")
    chips = [(1 - x, y), (x, 1 - y), (1 - x, 1 - y)]
    return x, y, c, chips


class _Exchange(NamedTuple):
    operands: list
    out_shapes: list
    n_sems: int
    start: Callable
    finish: Callable
    halfway: Optional[Callable] = None


def _gather_exchange(shards):
    n = len(shards)
    per = 9
    assert all(w.shape[0] % (4 * BF16_SUBLANES) == 0 for w in shards)

    def plan(ins, outs, send_sems, recv_sems):
        x, y, c, (x_nbr, y_nbr, diag) = _mesh_position()
        sibling = (x, y, 1 - c)
        slot = lambda chip: 2 * chip[0] + chip[1]

        def rows(a, chip, hc, quarter=None):
            rh = shards[a].shape[0] // 2
            if quarter is None:
                return outs[a].at[slot(chip), pl.ds(hc * rh, rh), :]
            return outs[a].at[slot(chip), pl.ds(hc * rh + quarter * (rh // 2), rh // 2), :]

        def copy(a, k, src, dst, to):
            return pltpu.make_async_remote_copy(src_ref=src, dst_ref=dst, send_sem=send_sems.at[a * per + k], recv_sem=recv_sems.at[a * per + k],
                                                device_id=to, device_id_type=MESH)

        first, from_sibling = [], []
        landed, then = [[] for _ in range(4)], [[] for _ in range(4)]
        for a in range(n):
            rh = shards[a].shape[0] // 2
            my_half = ins[a].at[pl.ds(c * rh, rh), :]
            first += [copy(a, 4, ins[a], outs[a].at[slot((x, y))], sibling),
                      copy(a, 0, my_half, rows(a, (x, y), c), (*x_nbr, c)), copy(a, 1, my_half, rows(a, (x, y), c), (*y_nbr, c))]
            landed[0].append(copy(a, 0, rows(a, x_nbr, c), rows(a, x_nbr, c), (*x_nbr, c)))
            then[0].append([copy(a, 2, rows(a, x_nbr, c, 0), rows(a, x_nbr, c, 0), (*y_nbr, c)), copy(a, 5, rows(a, x_nbr, c), rows(a, x_nbr, c), sibling)])
            landed[1].append(copy(a, 1, rows(a, y_nbr, c), rows(a, y_nbr, c), (*y_nbr, c)))
            then[1].append([copy(a, 3, rows(a, y_nbr, c, 1), rows(a, y_nbr, c, 1), (*x_nbr, c)), copy(a, 6, rows(a, y_nbr, c), rows(a, y_nbr, c), sibling)])
            landed[2].append(copy(a, 2, rows(a, diag, c, 0), rows(a, diag, c, 0), (*y_nbr, c)))
            then[2].append([copy(a, 7, rows(a, diag, c, 0), rows(a, diag, c, 0), sibling)])
            landed[3].append(copy(a, 3, rows(a, diag, c, 1), rows(a, diag, c, 1), (*x_nbr, c)))
            then[3].append([copy(a, 8, rows(a, diag, c, 1), rows(a, diag, c, 1), sibling)])
            from_sibling += [copy(a, 4, outs[a].at[slot((x, y))], outs[a].at[slot((x, y))], sibling),
                             copy(a, 5, rows(a, x_nbr, 1 - c), rows(a, x_nbr, 1 - c), sibling), copy(a, 6, rows(a, y_nbr, 1 - c), rows(a, y_nbr, 1 - c), sibling),
                             copy(a, 7, rows(a, diag, 1 - c, 0), rows(a, diag, 1 - c, 0), sibling), copy(a, 8, rows(a, diag, 1 - c, 1), rows(a, diag, 1 - c, 1), sibling)]
        return first, landed, then, from_sibling

    def start(*refs):
        first, _, _, _ = plan(*refs)
        for cp in first:
            cp.start()

    def stages(landed, then, which):
        for stage in which:
            for arrival, onward in zip(landed[stage], then[stage]):
                arrival.wait_recv()
                for cp in onward:
                    cp.start()

    def halfway(*refs):
        _, landed, then, _ = plan(*refs)
        stages(landed, then, (0, 1))

    def finish(*refs):
        first, landed, then, from_sibling = plan(*refs)
        stages(landed, then, (2, 3))
        for cp in from_sibling:
            cp.wait_recv()
        for cp in first + [cp for stage in then for onward in stage for cp in onward]:
            cp.wait_send()

    return _Exchange(list(shards), [jax.ShapeDtypeStruct((N_SHARDS,) + w.shape, w.dtype) for w in shards], per * n, start, finish, halfway)


def _sibling_halves_exchange(grads):
    n = len(grads)

    def plan(ins, outs, send_sems, recv_sems):
        x, y, c, _ = _mesh_position()
        return [pltpu.make_async_remote_copy(src_ref=ins[a].at[:, pl.ds((1 - c) * (grads[a].shape[1] // 2), grads[a].shape[1] // 2), :],
                                             dst_ref=outs[a], send_sem=send_sems.at[a], recv_sem=recv_sems.at[a],
                                             device_id=(x, y, 1 - c), device_id_type=MESH) for a in range(n)]

    def start(*refs):
        for cp in plan(*refs):
            cp.start()

    def finish(*refs):
        for cp in plan(*refs):
            cp.wait()

    return _Exchange(list(grads), [jax.ShapeDtypeStruct((g.shape[0], g.shape[1] // 2, g.shape[2]), g.dtype) for g in grads], n, start, finish)


def _chip_partials_exchange(sums):
    n = len(sums)

    def plan(ins, outs, send_sems, recv_sems):
        _, _, c, chips = _mesh_position()
        return [pltpu.make_async_remote_copy(src_ref=ins[a].at[2 * cx + cy], dst_ref=outs[a].at[k], send_sem=send_sems.at[a * 3 + k],
                                             recv_sem=recv_sems.at[a * 3 + k], device_id=(cx, cy, c), device_id_type=MESH)
                for k, (cx, cy) in enumerate(chips) for a in range(n)]

    def start(*refs):
        for cp in plan(*refs):
            cp.start()

    def finish(*refs):
        for cp in plan(*refs):
            cp.wait()

    return _Exchange(list(sums), [jax.ShapeDtypeStruct((3,) + g.shape[1:], g.dtype) for g in sums], 3 * n, start, finish)


def _fused_call(body, *, name, grid, in_specs, out_specs, out_shape, scratch_shapes, operands, exchanges=()):
    single = not isinstance(out_shape, (tuple, list))
    out_specs = [out_specs] if single else list(out_specs)
    out_shape = [out_shape] if single else list(out_shape)
    n_in, n_out, n_scr = len(in_specs), len(out_specs), len(scratch_shapes)
    x_in = [len(e.operands) for e in exchanges]
    x_out = [len(e.out_shapes) for e in exchanges]

    def wrapped(*refs):
        refs = list(refs)
        ins = refs[:n_in]
        pos = n_in
        ex_ins = []
        for k in x_in:
            ex_ins.append(refs[pos:pos + k])
            pos += k
        outs = refs[pos:pos + n_out]
        pos += n_out
        ex_outs = []
        for k in x_out:
            ex_outs.append(refs[pos:pos + k])
            pos += k
        scratch = refs[pos:pos + n_scr]
        sems = refs[pos + n_scr:]
        first, last, middle = None, None, None
        for axis, size in enumerate(grid):
            at_start, at_end, at_middle = pl.program_id(axis) == 0, pl.program_id(axis) == size - 1, pl.program_id(axis) == size // 2
            first = at_start if first is None else first & at_start
            last = at_end if last is None else last & at_end
            middle = at_middle if middle is None else middle & at_middle

        @pl.when(first)
        def _():
            for i, e in enumerate(exchanges):
                e.start(ex_ins[i], ex_outs[i], sems[2 * i], sems[2 * i + 1])

        if any(e.halfway for e in exchanges):
            @pl.when(middle)
            def _():
                for i, e in enumerate(exchanges):
                    if e.halfway:
                        e.halfway(ex_ins[i], ex_outs[i], sems[2 * i], sems[2 * i + 1])

        body(*ins, *outs, *scratch)

        @pl.when(last)
        def _():
            for i, e in enumerate(exchanges):
                e.finish(ex_ins[i], ex_outs[i], sems[2 * i], sems[2 * i + 1])

    n_x_in, n_x_out = sum(x_in), sum(x_out)
    results = pl.pallas_call(
        wrapped if exchanges else body, name=name, grid=grid,
        in_specs=list(in_specs) + [HBM] * n_x_in,
        out_specs=out_specs + [HBM] * n_x_out,
        out_shape=out_shape + [s for e in exchanges for s in e.out_shapes],
        scratch_shapes=list(scratch_shapes) + [pltpu.SemaphoreType.DMA((e.n_sems,)) for e in exchanges for _ in range(2)],
        compiler_params=_params(("arbitrary",) * len(grid)),
    )(*operands, *[a for e in exchanges for a in e.operands])
    own = results[0] if single else tuple(results[:n_out])
    landed, pos = [], n_out
    for k in x_out:
        landed.append(list(results[pos:pos + k]))
        pos += k
    return own, landed


def _cast_bf16(x, *, name, exchanges=()):
    s, cols = x.shape
    t = min(512, s)

    def body(x_ref, o_ref):
        o_ref[...] = x_ref[...].astype(BF16)

    tile = pl.BlockSpec((t, cols), lambda i: (i, 0))
    return _fused_call(body, name=name, grid=(s // t,), in_specs=[tile], out_specs=tile, out_shape=jax.ShapeDtypeStruct((s, cols), BF16),
                       scratch_shapes=[], operands=[x], exchanges=exchanges)


ROW_TILE_MAX = 640
BF16_SUBLANES = 16


def _row_tile(rows):
    for tr in range(min(rows, ROW_TILE_MAX), 0, -1):
        if rows % tr == 0 and tr % BF16_SUBLANES == 0:
            return tr
    raise ValueError(rows)


def _add_sibling(grad, other, pos, *, name):
    p, r, cols = grad.shape
    rh = r // 2
    tr = _row_tile(rh)
    nb = rh // tr

    def body(pos_ref, g_ref, o_ref, sb_ref, mine_ref):
        total = g_ref[...] + o_ref[...]
        sb_ref[...] = total.astype(BF16)

        @pl.when(pl.program_id(1) == pos_ref[0])
        def _():
            mine_ref[...] = total

    return pl.pallas_call(
        body, name=name, out_shape=(jax.ShapeDtypeStruct((p, rh, cols), BF16), jax.ShapeDtypeStruct((rh, cols), F32)),
        grid_spec=pltpu.PrefetchScalarGridSpec(
            num_scalar_prefetch=1, grid=(nb, p),
            in_specs=[pl.BlockSpec((None, tr, cols), lambda i, j, pos_ref: (j, pos_ref[1] * nb + i, 0)),
                      pl.BlockSpec((None, tr, cols), lambda i, j, pos_ref: (j, i, 0))],
            out_specs=(pl.BlockSpec((None, tr, cols), lambda i, j, pos_ref: (j, i, 0)),
                       pl.BlockSpec((tr, cols), lambda i, j, pos_ref: (i, 0)))),
        compiler_params=_params(("parallel", "arbitrary")),
    )(pos, grad, other)


def _add_chips(mine, others, pos, *, name):
    rh, cols = mine.shape
    tr = _row_tile(rh)
    nb = rh // tr

    def body(pos_ref, s_ref, o_ref, r_ref):
        r_ref[...] = ((s_ref[...] + o_ref[0].astype(F32)) + o_ref[1].astype(F32)) + o_ref[2].astype(F32)

    return pl.pallas_call(
        body, name=name, out_shape=jax.ShapeDtypeStruct((2 * rh, cols), F32),
        grid_spec=pltpu.PrefetchScalarGridSpec(
            num_scalar_prefetch=1, grid=(nb,),
            in_specs=[pl.BlockSpec((tr, cols), lambda i, pos_ref: (i, 0)),
                      pl.BlockSpec((3, tr, cols), lambda i, pos_ref: (0, i, 0))],
            out_specs=pl.BlockSpec((tr, cols), lambda i, pos_ref: (pos_ref[1] * nb + i, 0))),
        compiler_params=_params(("parallel",)),
    )(pos, mine, others)


def _join_halves(bufs, *, name):
    n = len(bufs)

    def body(*refs):
        ins, outs = refs[:n], refs[n:2 * n]
        send_sems, recv_sems = refs[2 * n:]
        x, y, c, _ = _mesh_position()

        def copy(a, hc):
            rh = bufs[a].shape[0] // 2
            rows = pl.ds(hc * rh, rh)
            return pltpu.make_async_remote_copy(src_ref=ins[a].at[rows, :], dst_ref=outs[a].at[rows, :], send_sem=send_sems.at[a],
                                                recv_sem=recv_sems.at[a], device_id=(x, y, 1 - c), device_id_type=MESH)

        for a in range(n):
            copy(a, c).start()
        for a in range(n):
            copy(a, c).wait_send()
            copy(a, 1 - c).wait_recv()

    return pl.pallas_call(
        body, name=name, out_shape=[jax.ShapeDtypeStruct(b.shape, b.dtype) for b in bufs],
        in_specs=[HBM] * n, out_specs=[HBM] * n, input_output_aliases={a: a for a in range(n)},
        scratch_shapes=[pltpu.SemaphoreType.DMA((n,)), pltpu.SemaphoreType.DMA((n,))],
    )(*bufs)


SMALL = ["lb_logits", "hg_norm_gain", "swa_sinks", "rel_bias", "ln1_g", "ln1_b", "ln2_g", "ln2_b"]
PACK_ROWS = 48
PACK_AT = dict(lb_logits=(slice(0, 2), slice(0, D_MODEL)), hg_norm_gain=(slice(2, 3), slice(0, D_MODEL)), ln1_g=(slice(3, 4), slice(0, D_MODEL)),
               ln1_b=(slice(4, 5), slice(0, D_MODEL)), ln2_g=(slice(5, 6), slice(0, D_MODEL)), ln2_b=(slice(6, 7), slice(0, D_MODEL)),
               swa_sinks=(slice(7, 8), slice(0, SWA_HEADS)), sq_err=(slice(8, 9), slice(0, D_MODEL)),
               rel_bias=(slice(16, 16 + NUM_BUCKETS), slice(0, SWA_HEADS)))


def _pack_small(grads, *, name):
    names = SMALL + ["sq_err"]

    def body(*refs):
        packed = refs[len(names)]
        packed[...] = jnp.zeros_like(packed)
        for k, g_ref in zip(names, refs):
            packed[PACK_AT[k]] = g_ref[...]

    return pl.pallas_call(body, name=name, out_shape=jax.ShapeDtypeStruct((PACK_ROWS, D_MODEL), F32), compiler_params=_params(),
                          )(*[grads[k] for k in names])


def _small_gather_exchange(packed):
    def plan(ins, outs, send_sems, recv_sems):
        x, y, c, _ = _mesh_position()
        me = 4 * x + 2 * y + c
        own = pltpu.make_async_copy(ins[0], outs[0].at[me], send_sems.at[7])
        remote = []
        for d in range(1, 8):
            dx, dy, dc = (d >> 2) & 1, (d >> 1) & 1, d & 1
            remote.append(pltpu.make_async_remote_copy(src_ref=ins[0], dst_ref=outs[0].at[me], send_sem=send_sems.at[d - 1],
                                                       recv_sem=recv_sems.at[d - 1], device_id=(x ^ dx, y ^ dy, c ^ dc), device_id_type=MESH))
        return own, remote

    def start(*refs):
        own, remote = plan(*refs)
        own.start()
        for cp in remote:
            cp.start()

    def finish(*refs):
        own, remote = plan(*refs)
        for cp in remote:
            cp.wait()
        own.wait()

    return _Exchange([packed], [jax.ShapeDtypeStruct((8,) + packed.shape, packed.dtype)], 8, start, finish)


def _adamw_small(gathered, w, m, v, *, name):
    names = SMALL
    n = len(names)

    def body(*refs):
        gathered_ref = refs[0]
        w_refs, m_refs, v_refs = (dict(zip(names, refs[1 + i * n:1 + (i + 1) * n])) for i in range(3))
        loss_ref = refs[1 + 3 * n]
        go_refs, d_refs, nm_refs, nv_refs = (dict(zip(names, refs[2 + (3 + i) * n:2 + (4 + i) * n])) for i in range(4))
        total_ref = refs[2 + 7 * n]
        total = gathered_ref[0]
        for j in range(1, 8):
            total = total + gathered_ref[j]
        total_ref[...] = total
        loss_ref[...] = (0.5 / D_MODEL) * jnp.sum(total_ref[PACK_AT["sq_err"]], axis=1, keepdims=True)
        for k in names:
            g = total_ref[PACK_AT[k]]
            go_refs[k][...] = g
            d_refs[k][...], nm_refs[k][...], nv_refs[k][...] = _adamw_math(w_refs[k][...], g, m_refs[k][...], v_refs[k][...])

    like = [jax.ShapeDtypeStruct(w[k].shape, F32) for k in names]
    results = pl.pallas_call(body, name=name, out_shape=[jax.ShapeDtypeStruct((1, 1), F32)] + like * 4,
                             scratch_shapes=[pltpu.VMEM((PACK_ROWS, D_MODEL), F32)],
                             compiler_params=_params())(gathered, *[d[k] for d in (w, m, v) for k in names])
    return results[0], {k: tuple(results[1 + i * n + j] for i in range(4)) for j, k in enumerate(names)}


def _adamw_math(w, g, m, v):
    m = ADAM_B1 * m + (1.0 - ADAM_B1) * g
    v = ADAM_B2 * v + (1.0 - ADAM_B2) * (g * g)
    m_hat = m / (1.0 - ADAM_B1 ** ADAM_STEP)
    v_hat = v / (1.0 - ADAM_B2 ** ADAM_STEP)
    delta = -ADAM_LR * (m_hat / (jnp.sqrt(v_hat) + ADAM_EPS) + ADAM_WD * w)
    return delta, m, v


def _adamw(w, g, m, v, *, name):
    _, rows, cols = w.shape
    tr = _row_tile(rows)
    blk = pl.BlockSpec((None, tr, cols), lambda i: (0, i, 0))
    flat = pl.BlockSpec((tr, cols), lambda i: (i, 0))

    def body(w_ref, g_ref, m_ref, v_ref, go_ref, d_ref, nm_ref, nv_ref):
        g_v = g_ref[...]
        go_ref[...] = g_v
        d_ref[...], nm_ref[...], nv_ref[...] = _adamw_math(w_ref[...], g_v, m_ref[...], v_ref[...])

    shape = jax.ShapeDtypeStruct((1, rows, cols), F32)
    return pl.pallas_call(body, name=name, grid=(rows // tr,), out_shape=(shape,) * 4, in_specs=[blk, flat, blk, blk], out_specs=(blk,) * 4,
                          compiler_params=_params(("parallel",)))(w, g, m, v)


WEIGHTS = ["w_in", "lb_logits", "hg_norm_gain", "swa_sinks", "rel_bias", "w_mem_kv", "w_branch_hg", "w_branch_swa", "w_branch_mem",
           "w_out", "ln1_g", "ln1_b", "w_up", "w_down", "ln2_g", "ln2_b"]
BIG = ["w_in", "w_mem_kv", "w_branch_hg", "w_branch_swa", "w_branch_mem", "w_out", "w_up", "w_down"]


def kernel(x, mem, w_in, lb_logits, hg_norm_gain, swa_sinks, rel_bias, w_mem_kv, w_branch_hg, w_branch_swa, w_branch_mem, w_out, ln1_g, ln1_b, w_up, w_down, ln2_g, ln2_b, loss_target, m_w_in, m_lb_logits, m_hg_norm_gain, m_swa_sinks, m_rel_bias, m_w_mem_kv, m_w_branch_hg, m_w_branch_swa, m_w_branch_mem, m_w_out, m_ln1_g, m_ln1_b, m_w_up, m_w_down, m_ln2_g, m_ln2_b, v_w_in, v_lb_logits, v_hg_norm_gain, v_swa_sinks, v_rel_bias, v_w_mem_kv, v_w_branch_hg, v_w_branch_swa, v_w_branch_mem, v_w_out, v_ln1_g, v_ln1_b, v_w_up, v_w_down, v_ln2_g, v_ln2_b):
    w = dict(w_in=w_in, lb_logits=lb_logits, hg_norm_gain=hg_norm_gain, swa_sinks=swa_sinks, rel_bias=rel_bias, w_mem_kv=w_mem_kv,
             w_branch_hg=w_branch_hg, w_branch_swa=w_branch_swa, w_branch_mem=w_branch_mem, w_out=w_out, ln1_g=ln1_g, ln1_b=ln1_b,
             w_up=w_up, w_down=w_down, ln2_g=ln2_g, ln2_b=ln2_b)
    m = dict(w_in=m_w_in, lb_logits=m_lb_logits, hg_norm_gain=m_hg_norm_gain, swa_sinks=m_swa_sinks, rel_bias=m_rel_bias, w_mem_kv=m_w_mem_kv,
             w_branch_hg=m_w_branch_hg, w_branch_swa=m_w_branch_swa, w_branch_mem=m_w_branch_mem, w_out=m_w_out, ln1_g=m_ln1_g, ln1_b=m_ln1_b,
             w_up=m_w_up, w_down=m_w_down, ln2_g=m_ln2_g, ln2_b=m_ln2_b)
    v = dict(w_in=v_w_in, lb_logits=v_lb_logits, hg_norm_gain=v_hg_norm_gain, swa_sinks=v_swa_sinks, rel_bias=v_rel_bias, w_mem_kv=v_w_mem_kv,
             w_branch_hg=v_w_branch_hg, w_branch_swa=v_w_branch_swa, w_branch_mem=v_w_branch_mem, w_out=v_w_out, ln1_g=v_ln1_g, ln1_b=v_ln1_b,
             w_up=v_w_up, w_down=v_w_down, ln2_g=v_ln2_g, ln2_b=v_ln2_b)
    shapes = {k: w[k].shape for k in WEIGHTS}
    for d in (w, m, v):
        d["w_in"] = d["w_in"].reshape(D_MODEL, IN_COLS // N_SHARDS).T[None]
    shards = {k: w[k].reshape(w[k].shape[-2], w[k].shape[-1]).astype(BF16) for k in BIG}
    x2d = x.reshape(x.shape[-2], D_MODEL)
    xb, ((wi4, wmkv),) = _cast_bf16(x2d, name="gather_weights", exchanges=[_gather_exchange([shards["w_in"], shards["w_mem_kv"]])])
    wi_t = wi4.reshape(IN_COLS, D_MODEL)

    grad_x, halves, small = _local_step(
        x2d, xb, mem.reshape(MEM_LEN, D_MODEL), loss_target.reshape(loss_target.shape[-2], D_MODEL),
        wi_t, wmkv, shards, lb_logits, hg_norm_gain, swa_sinks, rel_bias, ln1_g, ln1_b, ln2_g, ln2_b, distributed=True)

    reduced = dict(zip(BIG, _join_halves([halves[k] for k in BIG], name="join_halves")))

    outs = {k: _adamw(w[k], reduced[k], m[k], v[k], name="adamw_" + k) for k in BIG}
    loss, small_outs = _adamw_small(small, w, m, v, name="adamw_small")
    outs.update(small_outs)
    grad_out, delta_out, m_out, v_out = ({k: outs[k][i] for k in WEIGHTS} for i in range(4))
    for out in (grad_out, delta_out, m_out, v_out):
        out["w_in"] = out["w_in"][0].T

    result = [loss.reshape(()), grad_x.reshape(x.shape)]
    for out in (grad_out, delta_out, m_out, v_out):
        result += [out[k].reshape(shapes[k]) for k in WEIGHTS]
    return tuple(result)
```

```python
import math
from typing import Callable, NamedTuple, Optional

import jax
import jax.numpy as jnp
from jax import lax
from jax.experimental import pallas as pl
from jax.experimental.pallas import tpu as pltpu

F32 = jnp.float32
BF16 = jnp.bfloat16
HIGHEST = lax.Precision.HIGHEST
MESH = pl.DeviceIdType.MESH

D_MODEL = 1024
MEM_LEN = 256
HG_HEADS = 8
HG_DK = 128
HG_CHUNK = 64
SWA_HEADS = 16
SWA_KV_HEADS = 2
SWA_GROUP = 8
SWA_HEAD_DIM = 64
SWA_BLOCK = 128
SWA_WINDOW = 128
MEM_HEADS = 4
MEM_HEAD_DIM = 256
NUM_BUCKETS = 32
MAX_DISTANCE = 128
D_FF = 4096
LN_EPS = 1e-5
RMS_EPS = 1e-6
ALPHA = 2.0 ** 0.25
W_A, W_B, W_C, W_D = 4096, 1280, 1024, 3072
IN_COLS = W_A + W_B + W_C + W_D
N_SHARDS = 4
ADAM_LR = 0.001
ADAM_B1 = 0.9
ADAM_B2 = 0.999
ADAM_EPS = 1e-08
ADAM_WD = 0.01
ADAM_STEP = 10
MASK_VALUE = -1e30
VMEM_LIMIT = 56 * 1024 * 1024

NN = ((1,), (0,))
NT = ((1,), (1,))
TN = ((0,), (0,))
HBM = pl.BlockSpec(memory_space=pltpu.HBM)


def _dot(a, b, dims=NN, precision=None):
    return lax.dot_general(a, b, (dims, ((), ())), precision=precision, preferred_element_type=F32)


def _params(sem=None):
    return pltpu.CompilerParams(dimension_semantics=sem, vmem_limit_bytes=VMEM_LIMIT)


def _resident(shape):
    zeros = (0,) * len(shape)
    return pl.BlockSpec(shape, lambda *_: zeros, pipeline_mode=pl.Buffered(1))


def _resident_rows(arr, offset, rows):
    return pl.BlockSpec((pl.Element(rows), pl.Element(arr.shape[1])), lambda *_: (offset, 0), pipeline_mode=pl.Buffered(1))


def _mm(a, b, *, mode, tm, tn, tk, name, out_dtype=F32, b_panels=False, b_rows=None, out_panels=False, rows_of=None, row_offset=0,
        into=None):
    if mode == "tn":
        kdim, m = a.shape
    else:
        m, kdim = a.shape
    if b_panels:
        n = b.shape[0] * b.shape[2]
        assert b.shape[2] == tn and mode == "nn"
    elif b_rows is not None:
        assert mode == "nt"
        b_offset, n = b_rows
    elif mode == "nt":
        n = b.shape[0]
    else:
        n = b.shape[1]
    assert m % tm == 0 and n % tn == 0 and kdim % tk == 0, (name, m, n, kdim)
    nk = kdim // tk
    dims = {"nn": NN, "nt": NT, "tn": TN}[mode]
    a_spec = pl.BlockSpec((tk, tm), lambda i, j, k: (k, i)) if mode == "tn" else pl.BlockSpec((tm, tk), lambda i, j, k: (i, k))
    if b_panels:
        b_spec = pl.BlockSpec((None, tk, tn), lambda i, j, k: (j, k, 0))
    elif b_rows is not None:
        assert b_offset % BF16_SUBLANES == 0 and tn % BF16_SUBLANES == 0 and tk % 128 == 0
        b_spec = pl.BlockSpec((pl.Element(tn), pl.Element(tk)),
                              lambda i, j, k: (pl.multiple_of(b_offset + j * tn, BF16_SUBLANES), pl.multiple_of(k * tk, 128)))
    elif mode == "nt":
        b_spec = pl.BlockSpec((tn, tk), lambda i, j, k: (j, k))
    else:
        b_spec = pl.BlockSpec((tk, tn), lambda i, j, k: (k, j))
    in_specs = [a_spec, b_spec]
    operands = [a, b]
    aliases = {}
    if out_panels:
        out_shape = jax.ShapeDtypeStruct((n // tn, m, tn), out_dtype)
        o_spec = pl.BlockSpec((None, tm, tn), lambda i, j, k: (j, i, 0))
    elif rows_of is not None:
        out_shape = jax.ShapeDtypeStruct((rows_of, n), out_dtype)
        assert row_offset % BF16_SUBLANES == 0 and tm % BF16_SUBLANES == 0 and tn % 128 == 0
        o_spec = pl.BlockSpec((pl.Element(tm), pl.Element(tn)),
                              lambda i, j, k: (pl.multiple_of(row_offset + i * tm, BF16_SUBLANES), pl.multiple_of(j * tn, 128)))
        if into is not None:
            in_specs.append(pl.BlockSpec(memory_space=pl.ANY))
            operands.append(into)
            aliases = {2: 0}
    else:
        out_shape = jax.ShapeDtypeStruct((m, n), out_dtype)
        o_spec = pl.BlockSpec((tm, tn), lambda i, j, k: (i, j))
    n_in = len(operands)

    def body(*refs):
        a_ref, b_ref, o_ref = refs[0], refs[1], refs[n_in]
        part = _dot(a_ref[...].astype(BF16), b_ref[...].astype(BF16), dims)

        def finish(acc):
            o_ref[...] = acc.astype(out_dtype)

        if nk == 1:
            finish(part)
        else:
            acc_ref = refs[-1]
            k = pl.program_id(2)

            @pl.when(k == 0)
            def _():
                acc_ref[...] = part

            @pl.when(k > 0)
            def _():
                acc_ref[...] += part

            @pl.when(k == nk - 1)
            def _():
                finish(acc_ref[...])

    return pl.pallas_call(
        body, name=name, out_shape=out_shape, grid=(m // tm, n // tn, nk), in_specs=in_specs, out_specs=o_spec,
        scratch_shapes=[pltpu.VMEM((tm, tn), F32)] if nk > 1 else [], input_output_aliases=aliases,
        compiler_params=_params(("parallel", "parallel", "arbitrary")),
    )(*operands)


def _dx_matmul(dzs, wi_t, resid, *, tm, name, exchanges=()):
    s = resid.shape[0]
    npieces = len(dzs)
    offsets = [sum(dz.shape[1] for dz in dzs[:p]) for p in range(npieces)]
    tile = lambda i: (i, 0)
    in_specs = [pl.BlockSpec((tm, dz.shape[1]), tile) for dz in dzs] + [_resident(wi_t.shape), pl.BlockSpec((tm, D_MODEL), tile)]

    def body(*refs):
        dz_refs, w_ref, r_ref, o_ref = refs[:npieces], refs[npieces], refs[npieces + 1], refs[npieces + 2]
        total = ALPHA * r_ref[...]
        for p in range(npieces):
            total = total + _dot(dz_refs[p][...], w_ref[offsets[p]:offsets[p] + dzs[p].shape[1], :], NN)
        o_ref[...] = total

    return _fused_call(
        body, name=name, out_shape=jax.ShapeDtypeStruct((s, D_MODEL), F32), grid=(s // tm,), in_specs=in_specs,
        out_specs=pl.BlockSpec((tm, D_MODEL), tile), scratch_shapes=[], operands=[*dzs, wi_t, resid], exchanges=exchanges)


def _lower_bound(lbl_ref):
    l0, l1 = lbl_ref[0:1, :], lbl_ref[1:2, :]
    mx = jnp.maximum(l0, l1)
    e0, e1 = jnp.exp(l0 - mx), jnp.exp(l1 - mx)
    return e0 / (e0 + e1)


HEAD_COLS = [slice(h * HG_DK, (h + 1) * HG_DK) for h in range(HG_HEADS)]


def _head_mean(x):
    return jnp.concatenate([jnp.broadcast_to(jnp.mean(x[:, c], axis=-1, keepdims=True), (x.shape[0], HG_DK)) for c in HEAD_COLS], axis=1)


def _triangle_sum(tri_b, x):
    p0 = x.astype(BF16)
    r1 = x - p0.astype(F32)
    p1 = r1.astype(BF16)
    p2 = (r1 - p1.astype(F32)).astype(BF16)
    return _dot(tri_b, p0) + _dot(tri_b, p1) + _dot(tri_b, p2)


def _chunk_forward(q, fl, v, lb, tril_b):
    sg = jax.nn.sigmoid(fl)
    f = lb + (1.0 - lb) * sg
    k = 1.0 - f
    b = _triangle_sum(tril_b, jnp.log(f))
    b_last = b[HG_CHUNK - 1:HG_CHUNK, :]
    eb, enb, eo = jnp.exp(b), jnp.exp(-b), jnp.exp(b_last - b)
    return sg, f, k, b_last, eb, enb, eo, q * eb, k * enb, k * eo


def _hgrn_fwd(xb, wi_t, lb_logits, gain, *, name, exchanges=()):
    s = xb.shape[0]
    t = min(256, s)
    ncs = t // HG_CHUNK

    def body(x_ref, w_ref, lbl_ref, gain_ref, z_ref, oa_ref, oraw_ref, st_ref, state):
        @pl.when(pl.program_id(0) == 0)
        def _():
            state[...] = jnp.zeros_like(state)

        z_ref[...] = _dot(x_ref[...], w_ref[...], NT)
        lb_all = _lower_bound(lbl_ref)
        row = lax.broadcasted_iota(jnp.int32, (HG_CHUNK, HG_CHUNK), 0)
        col = lax.broadcasted_iota(jnp.int32, (HG_CHUNK, HG_CHUNK), 1)
        tril = row >= col
        tril_b = tril.astype(BF16)
        gain_all = gain_ref[...]

        def chunk(i, carry):
            r = pl.ds(pl.multiple_of(i * HG_CHUNK, HG_CHUNK), HG_CHUNK)
            q, fl, v, hg = (z_ref[r, j * D_MODEL:(j + 1) * D_MODEL] for j in range(4))
            _, _, _, b_last, _, _, _, q_in, k_in, k_out = _chunk_forward(q, fl, v, lb_all, tril_b)
            q_in_b, k_in_b, k_out_b, vb = (u.astype(BF16) for u in (q_in, k_in, k_out, v))
            decay = jnp.exp(b_last)
            sts = [state[h] for h in range(HG_HEADS)]
            attn = [_dot(q_in_b[:, c], k_in_b[:, c], NT) for c in HEAD_COLS]
            inter = [_dot(q_in_b[:, c], sts[h].astype(BF16), NT) for h, c in enumerate(HEAD_COLS)]
            upd = [_dot(vb[:, c], k_out_b[:, c], TN) for c in HEAD_COLS]
            attn = [jnp.where(tril, a, 0.0).astype(BF16) for a in attn]
            outs = [_dot(attn[h], vb[:, c], NN) + inter[h] for h, c in enumerate(HEAD_COLS)]
            for h, c in enumerate(HEAD_COLS):
                st_ref[h, i] = sts[h]
                state[h] = sts[h] * decay[:, c] + upd[h]
            o = jnp.concatenate(outs, axis=1)
            oraw_ref[r, :] = o
            n = o * lax.rsqrt(_head_mean(o * o) + RMS_EPS)
            oa_ref[r, :] = (n * gain_all * (hg * jax.nn.sigmoid(hg))).astype(BF16)
            return carry

        lax.fori_loop(0, ncs, chunk, 0, unroll=True)

    tile = lambda i: (i, 0)
    return _fused_call(
        body, name=name, grid=(s // t,),
        out_shape=(jax.ShapeDtypeStruct((s, W_A), F32), jax.ShapeDtypeStruct((s, D_MODEL), BF16), jax.ShapeDtypeStruct((s, D_MODEL), F32),
                   jax.ShapeDtypeStruct((HG_HEADS, s // HG_CHUNK, HG_DK, HG_DK), F32)),
        in_specs=[pl.BlockSpec((t, D_MODEL), tile), _resident_rows(wi_t, 0, W_A), _resident((2, D_MODEL)), _resident((1, D_MODEL))],
        out_specs=(pl.BlockSpec((t, W_A), tile), pl.BlockSpec((t, D_MODEL), tile), pl.BlockSpec((t, D_MODEL), tile),
                   pl.BlockSpec((HG_HEADS, ncs, HG_DK, HG_DK), lambda i: (0, i, 0, 0))),
        scratch_shapes=[pltpu.VMEM((HG_HEADS, HG_DK, HG_DK), F32)],
        operands=[xb, wi_t, lb_logits, gain], exchanges=exchanges)


def _hgrn_bwd(za, oraw, do_a, states, lb_logits, gain, *, name, exchanges=()):
    s = za.shape[0]
    t = min(256, s)
    ncs = t // HG_CHUNK
    nt = s // t

    def body(z_ref, oraw_ref, do_ref, st_ref, lbl_ref, gain_ref, dz_ref, stats_ref, dstate):
        step = pl.program_id(0)

        @pl.when(step == 0)
        def _():
            dstate[...] = jnp.zeros_like(dstate)
            stats_ref[...] = jnp.zeros_like(stats_ref)

        lb_all = _lower_bound(lbl_ref)
        row = lax.broadcasted_iota(jnp.int32, (HG_CHUNK, HG_CHUNK), 0)
        col = lax.broadcasted_iota(jnp.int32, (HG_CHUNK, HG_CHUNK), 1)
        tril = row >= col
        tril_b = tril.astype(BF16)
        triu_b = (row <= col).astype(BF16)
        gain_all = gain_ref[...]

        def chunk(ii, carry):
            i = ncs - 1 - ii
            r = pl.ds(pl.multiple_of(i * HG_CHUNK, HG_CHUNK), HG_CHUNK)
            q, fl, v, hg = (z_ref[r, j * D_MODEL:(j + 1) * D_MODEL] for j in range(4))
            o = oraw_ref[r, :]
            doa = do_ref[r, :]
            rms = lax.rsqrt(_head_mean(o * o) + RMS_EPS)
            n = o * rms
            sgg = jax.nn.sigmoid(hg)
            silu = hg * sgg
            dhg = doa * n * gain_all * (sgg * (1.0 + hg * (1.0 - sgg)))
            dgain = jnp.sum(doa * n * silu, axis=0, keepdims=True)
            dn = doa * gain_all * silu
            do = rms * (dn - n * _head_mean(dn * n))
            sg, f, k, b_last, eb, enb, eo, q_in, k_in, k_out = _chunk_forward(q, fl, v, lb_all, tril_b)
            q_in_b, k_in_b, k_out_b, vb, dob = (u.astype(BF16) for u in (q_in, k_in, k_out, v, do))
            decay = jnp.exp(b_last)
            sts = [st_ref[h, i] for h in range(HG_HEADS)]
            dsts = [dstate[h] for h in range(HG_HEADS)]
            dsts_b = [d.astype(BF16) for d in dsts]
            heads = list(enumerate(HEAD_COLS))
            attn = [_dot(q_in_b[:, c], k_in_b[:, c], NT) for h, c in heads]
            dattn = [_dot(dob[:, c], vb[:, c], NT) for h, c in heads]
            dq_st = [_dot(dob[:, c], sts[h].astype(BF16), NN) for h, c in heads]
            dk_out = [_dot(vb[:, c], dsts_b[h], NN) for h, c in heads]
            dv_st = [_dot(k_out_b[:, c], dsts_b[h], NT) for h, c in heads]
            dst_o = [_dot(dob[:, c], q_in_b[:, c], TN) for h, c in heads]
            attn = [jnp.where(tril, a, 0.0).astype(BF16) for a in attn]
            dattn = [jnp.where(tril, a, 0.0).astype(BF16) for a in dattn]
            dq_in = jnp.concatenate([_dot(dattn[h], k_in_b[:, c], NN) + dq_st[h] for h, c in heads], axis=1)
            dk_in = jnp.concatenate([_dot(dattn[h], q_in_b[:, c], TN) for h, c in heads], axis=1)
            dv = jnp.concatenate([_dot(attn[h], dob[:, c], TN) + dv_st[h] for h, c in heads], axis=1)
            dk_out = jnp.concatenate(dk_out, axis=1)
            dst_st = jnp.concatenate([jnp.sum(dsts[h] * sts[h], axis=0, keepdims=True) for h in range(HG_HEADS)], axis=1)
            for h, c in heads:
                dstate[h] = dsts[h] * decay[:, c] + dst_o[h]
            db_last = decay * dst_st + jnp.sum(dk_out * k_out, axis=0, keepdims=True)
            db = dq_in * q_in - dk_in * k_in - dk_out * k_out
            dg = _triangle_sum(triu_b, db) + db_last
            dk = dk_in * enb + dk_out * eo
            df = dg / f - dk
            stats_ref[0:1, :] += dgain
            stats_ref[1:2, :] += jnp.sum(df * (1.0 - sg), axis=0, keepdims=True)
            dz_ref[r, 0:1024] = (dq_in * eb).astype(BF16)
            dz_ref[r, 1024:2048] = (df * (1.0 - lb_all) * sg * (1.0 - sg)).astype(BF16)
            dz_ref[r, 2048:3072] = dv.astype(BF16)
            dz_ref[r, 3072:4096] = dhg.astype(BF16)
            return carry

        lax.fori_loop(0, ncs, chunk, 0, unroll=True)

        @pl.when(step == nt - 1)
        def _():
            dl0 = stats_ref[1:2, :] * lb_all * (1.0 - lb_all)
            stats_ref[1:2, :] = dl0
            stats_ref[2:3, :] = -dl0

    rev = lambda i: (nt - 1 - i, 0)
    return _fused_call(
        body, name=name, grid=(nt,),
        out_shape=(jax.ShapeDtypeStruct((s, W_A), BF16), jax.ShapeDtypeStruct((8, D_MODEL), F32)),
        in_specs=[pl.BlockSpec((t, W_A), rev), pl.BlockSpec((t, D_MODEL), rev), pl.BlockSpec((t, D_MODEL), rev),
                  pl.BlockSpec((HG_HEADS, ncs, HG_DK, HG_DK), lambda i: (0, nt - 1 - i, 0, 0)),
                  _resident((2, D_MODEL)), _resident((1, D_MODEL))],
        out_specs=(pl.BlockSpec((t, W_A), rev), pl.BlockSpec((8, D_MODEL), lambda i: (0, 0))),
        scratch_shapes=[pltpu.VMEM((HG_HEADS, HG_DK, HG_DK), F32)],
        operands=[za, oraw, do_a, states, lb_logits, gain], exchanges=exchanges)


def _t5_bucket(n):
    max_exact = NUM_BUCKETS // 2
    nf = jnp.maximum(n, 1).astype(F32)
    large = max_exact + (jnp.log(nf / max_exact) / math.log(MAX_DISTANCE / max_exact) * (NUM_BUCKETS - max_exact)).astype(jnp.int32)
    large = jnp.minimum(large, NUM_BUCKETS - 1)
    return jnp.where(n < max_exact, n, large)


def _bias_selector():
    qi = jnp.arange(SWA_BLOCK)[:, None] + SWA_BLOCK
    kj = jnp.arange(2 * SWA_BLOCK)[None, :]
    dist = qi - kj
    band = ((dist >= 0) & (dist < SWA_WINDOW)).reshape(1, -1)
    bucket = _t5_bucket(jnp.clip(dist, 0, SWA_WINDOW - 1)).reshape(1, -1)
    onehot = ((bucket == jnp.arange(NUM_BUCKETS)[:, None]) & band).astype(F32)
    return onehot, jnp.where(band, 0.0, MASK_VALUE).astype(F32)


def _bias_table(rel_bias_t, onehot, maskrow, *, name):
    def body(rb_ref, oh_ref, mask_ref, o_ref):
        o_ref[...] = _dot(rb_ref[...], oh_ref[...], NN, HIGHEST) + mask_ref[...]

    return pl.pallas_call(body, name=name, out_shape=jax.ShapeDtypeStruct((SWA_HEADS, onehot.shape[1]), F32),
                          compiler_params=_params())(rel_bias_t, onehot, maskrow)


def _bias_grad(dbias2d, onehot, *, name):
    def body(db_ref, oh_ref, o_ref):
        o_ref[...] = _dot(db_ref[...], oh_ref[...], NT, HIGHEST)

    return pl.pallas_call(body, name=name, out_shape=jax.ShapeDtypeStruct((SWA_HEADS, NUM_BUCKETS), F32),
                          compiler_params=_params())(dbias2d, onehot)


def _swa_operands(zq_ref, kv_cur_ref, kv_prev_ref):
    q = (zq_ref[:, 0:1024] * (SWA_HEAD_DIM ** -0.5)).astype(BF16)
    kv_c = kv_cur_ref[...].astype(BF16)
    kv_p = kv_prev_ref[...].astype(BF16)
    kks = [jnp.concatenate([kv_p[:, g * 64:(g + 1) * 64], kv_c[:, g * 64:(g + 1) * 64]], axis=0) for g in range(SWA_KV_HEADS)]
    vvs = [jnp.concatenate([kv_p[:, 128 + g * 64:128 + (g + 1) * 64], kv_c[:, 128 + g * 64:128 + (g + 1) * 64]], axis=0)
           for g in range(SWA_KV_HEADS)]
    return q, kks, vvs


SWA_PART_HEADS = 8
SWA_PARTS = [(h0 // SWA_GROUP, h0) for h0 in range(0, SWA_HEADS, SWA_PART_HEADS)]


def _part_lanes(h0):
    return slice(h0 * SWA_BLOCK, (h0 + SWA_PART_HEADS) * SWA_BLOCK)


def _stack_heads(x, h0):
    return jnp.concatenate([x[:, h * SWA_HEAD_DIM:(h + 1) * SWA_HEAD_DIM] for h in range(h0, h0 + SWA_PART_HEADS)], axis=0)


def _heads_to_lanes(xt):
    pairs = []
    for j in range(0, xt.shape[1] // SWA_BLOCK, 2):
        two = jnp.concatenate([xt[:, j * SWA_BLOCK:(j + 1) * SWA_BLOCK], xt[:, (j + 1) * SWA_BLOCK:(j + 2) * SWA_BLOCK]], axis=0)
        pairs.append(two.T)
    return jnp.concatenate(pairs, axis=1)


def _swa_softmax(score_t, bias_ref, sink_ref, h0):
    sc = score_t + bias_ref[:, _part_lanes(h0)]
    sink = sink_ref[:, _part_lanes(h0)]
    m = jnp.maximum(jnp.max(sc, axis=0, keepdims=True), sink)
    e = jnp.exp(sc - m)
    e_sink = jnp.exp(sink - m)
    return e, 1.0 / (jnp.sum(e, axis=0, keepdims=True) + e_sink), e_sink


def _swa_tables(bias2d, sinks):
    bias_t = bias2d.reshape(SWA_HEADS, SWA_BLOCK, 2 * SWA_BLOCK).transpose(2, 0, 1).reshape(2 * SWA_BLOCK, SWA_HEADS * SWA_BLOCK)
    first = jnp.where(jnp.arange(2 * SWA_BLOCK)[:, None] < SWA_BLOCK, MASK_VALUE, bias_t)
    return jnp.stack([first, bias_t]), jnp.repeat(sinks, SWA_BLOCK, axis=1)


def _swa_fwd(zb, bias_tables, sink_lanes, *, name, exchanges=()):
    s = zb.shape[0]
    nb = s // SWA_BLOCK

    def body(zq_ref, kvc_ref, kvp_ref, bias_ref, sink_ref, o_ref):
        q, kks, vvs = _swa_operands(zq_ref, kvc_ref, kvp_ref)
        scores = [_dot(kks[g], _stack_heads(q, h0), NT) for g, h0 in SWA_PARTS]
        probs = []
        for score, (_, h0) in zip(scores, SWA_PARTS):
            e, inv, _ = _swa_softmax(score, bias_ref, sink_ref, h0)
            probs.append((e * inv).astype(BF16))
        outs = [_dot(vvs[g], p, TN) for p, (g, _) in zip(probs, SWA_PARTS)]
        o_ref[...] = jnp.concatenate([_heads_to_lanes(o) for o in outs], axis=1).astype(BF16)

    return _fused_call(
        body, name=name, grid=(nb,), out_shape=jax.ShapeDtypeStruct((s, D_MODEL), BF16),
        in_specs=[pl.BlockSpec((SWA_BLOCK, W_B), lambda n: (n, 0)),
                  pl.BlockSpec((SWA_BLOCK, 256), lambda n: (n, 4)),
                  pl.BlockSpec((SWA_BLOCK, 256), lambda n: (jnp.maximum(n - 1, 0), 4)),
                  pl.BlockSpec((None, 2 * SWA_BLOCK, SWA_HEADS * SWA_BLOCK), lambda n: (jnp.minimum(n, 1), 0, 0)),
                  _resident((1, SWA_HEADS * SWA_BLOCK))],
        out_specs=pl.BlockSpec((SWA_BLOCK, D_MODEL), lambda n: (n, 0)), scratch_shapes=[],
        operands=[zb, zb, zb, bias_tables, sink_lanes], exchanges=exchanges)


def _swa_bwd(zb, do_b, bias_tables, sink_lanes, *, name, exchanges=()):
    s = zb.shape[0]
    nb = s // SWA_BLOCK
    scale = SWA_HEAD_DIM ** -0.5

    def body(zq_ref, kvc_ref, kvp_ref, do_ref, bias_ref, sink_ref, dz_ref, dbias_ref, dsink_ref, carry, dsink_acc):
        step = pl.program_id(0)

        @pl.when(step == 0)
        def _():
            carry[...] = jnp.zeros_like(carry)
            dsink_acc[...] = jnp.zeros_like(dsink_acc)
            dbias_ref[...] = jnp.zeros_like(dbias_ref)

        q, kks, vvs = _swa_operands(zq_ref, kvc_ref, kvp_ref)
        do = do_ref[...].astype(BF16)
        parts = range(len(SWA_PARTS))
        q_rows = [_stack_heads(q, h0) for _, h0 in SWA_PARTS]
        do_rows = [_stack_heads(do, h0) for _, h0 in SWA_PARTS]
        scores = [_dot(kks[g], q_rows[i], NT) for i, (g, _) in enumerate(SWA_PARTS)]
        soft = [_swa_softmax(scores[i], bias_ref, sink_ref, h0) for i, (_, h0) in enumerate(SWA_PARTS)]
        dps = [_dot(vvs[g], do_rows[i], NT) for i, (g, _) in enumerate(SWA_PARTS)]
        ps, dss = [], []
        for i, (_, h0) in enumerate(SWA_PARTS):
            e, inv, e_sink = soft[i]
            p = e * inv
            delta = jnp.sum(p * dps[i], axis=0, keepdims=True)
            ds = p * (dps[i] - delta)
            dbias_ref[:, _part_lanes(h0)] += ds
            dsink_acc[:, _part_lanes(h0)] -= e_sink * inv * delta
            ps.append(p.astype(BF16))
            dss.append(ds.astype(BF16))
        dqs = [_dot(kks[g], dss[i], TN) * scale for i, (g, _) in enumerate(SWA_PARTS)]
        in_group = lambda xs, g, axis: jnp.concatenate([xs[i] for i in parts if SWA_PARTS[i][0] == g], axis=axis)
        dkks = [_dot(in_group(dss, g, 1), in_group(q_rows, g, 0), NN) for g in range(SWA_KV_HEADS)]
        dvvs = [_dot(in_group(ps, g, 1), in_group(do_rows, g, 0), NN) for g in range(SWA_KV_HEADS)]
        dkv = jnp.concatenate(dkks + dvvs, axis=1)
        dz_ref[:, 0:1024] = jnp.concatenate([_heads_to_lanes(dq) for dq in dqs], axis=1).astype(BF16)
        dz_ref[:, 1024:1280] = (dkv[SWA_BLOCK:, :] + carry[...]).astype(BF16)
        carry[...] = dkv[:SWA_BLOCK, :]

        @pl.when(step == nb - 1)
        def _():
            acc = dsink_acc[...]
            dsink_ref[...] = jnp.concatenate([jnp.sum(acc[:, h * SWA_BLOCK:(h + 1) * SWA_BLOCK], axis=1, keepdims=True)
                                              for h in range(SWA_HEADS)], axis=1)

    rev = lambda i: (nb - 1 - i, 0)
    table_shape = (2 * SWA_BLOCK, SWA_HEADS * SWA_BLOCK)
    return _fused_call(
        body, name=name, grid=(nb,),
        out_shape=(jax.ShapeDtypeStruct((s, W_B), BF16), jax.ShapeDtypeStruct(table_shape, F32), jax.ShapeDtypeStruct((1, SWA_HEADS), F32)),
        in_specs=[pl.BlockSpec((SWA_BLOCK, W_B), rev),
                  pl.BlockSpec((SWA_BLOCK, 256), lambda i: (nb - 1 - i, 4)),
                  pl.BlockSpec((SWA_BLOCK, 256), lambda i: (jnp.maximum(nb - 2 - i, 0), 4)),
                  pl.BlockSpec((SWA_BLOCK, D_MODEL), rev),
                  pl.BlockSpec((None,) + table_shape, lambda i: (jnp.minimum(nb - 1 - i, 1), 0, 0)),
                  _resident((1, SWA_HEADS * SWA_BLOCK))],
        out_specs=(pl.BlockSpec((SWA_BLOCK, W_B), rev), pl.BlockSpec(table_shape, lambda i: (0, 0)),
                   pl.BlockSpec((1, SWA_HEADS), lambda i: (0, 0))),
        scratch_shapes=[pltpu.VMEM((SWA_BLOCK, 256), F32), pltpu.VMEM((1, SWA_HEADS * SWA_BLOCK), F32)],
        operands=[zb, zb, zb, do_b, bias_tables, sink_lanes], exchanges=exchanges)


MEM_COLS = [slice(h * MEM_HEAD_DIM, (h + 1) * MEM_HEAD_DIM) for h in range(MEM_HEADS)]
MEM_VCOLS = [slice(D_MODEL + h * MEM_HEAD_DIM, D_MODEL + (h + 1) * MEM_HEAD_DIM) for h in range(MEM_HEADS)]


def _mem_probs(zc_ref, mkv_ref):
    qs = [(zc_ref[:, c] * (MEM_HEAD_DIM ** -0.5)).astype(BF16) for c in MEM_COLS]
    scores = [_dot(qs[h], mkv_ref[:, c], NT) for h, c in enumerate(MEM_COLS)]
    ps = []
    for sc in scores:
        e = jnp.exp(sc - jnp.max(sc, axis=-1, keepdims=True))
        ps.append(e / jnp.sum(e, axis=-1, keepdims=True))
    return qs, ps


def _mem_fwd(xb, wi_t, mkv, *, name):
    s = xb.shape[0]
    t = min(512, s)

    def body(x_ref, w_ref, mkv_ref, zc_ref, o_ref):
        zc_ref[...] = _dot(x_ref[...], w_ref[...], NT).astype(BF16)
        _, ps = _mem_probs(zc_ref, mkv_ref)
        ps = [p.astype(BF16) for p in ps]
        o_ref[...] = jnp.concatenate([_dot(ps[h], mkv_ref[:, vc], NN) for h, vc in enumerate(MEM_VCOLS)], axis=1).astype(BF16)

    row = pl.BlockSpec((t, D_MODEL), lambda i: (i, 0))
    return pl.pallas_call(
        body, name=name, grid=(s // t,), out_shape=(jax.ShapeDtypeStruct((s, D_MODEL), BF16),) * 2,
        in_specs=[row, _resident_rows(wi_t, W_A + W_B, W_C), _resident((MEM_LEN, 2 * D_MODEL))],
        out_specs=(row, row), compiler_params=_params(("parallel",)),
    )(xb, wi_t, mkv)


def _mem_bwd(xb, zc, do_c, mkv, *, name):
    s = zc.shape[0]
    t = min(512, s)
    nt = s // t

    def body(x_ref, zc_ref, do_ref, mkv_ref, dz_ref, dmkv_ref, gwi_ref, acc):
        @pl.when(pl.program_id(0) == 0)
        def _():
            dmkv_ref[...] = jnp.zeros_like(dmkv_ref)
            acc[...] = jnp.zeros_like(acc)

        heads = range(MEM_HEADS)
        qs, ps = _mem_probs(zc_ref, mkv_ref)
        dos = [do_ref[:, c].astype(BF16) for c in MEM_COLS]
        dps = [_dot(dos[h], mkv_ref[:, MEM_VCOLS[h]], NT) for h in heads]
        dss = [(ps[h] * (dps[h] - jnp.sum(ps[h] * dps[h], axis=-1, keepdims=True))).astype(BF16) for h in heads]
        ps = [p.astype(BF16) for p in ps]
        dz = jnp.concatenate([_dot(dss[h], mkv_ref[:, MEM_COLS[h]], NN) * (MEM_HEAD_DIM ** -0.5) for h in heads], axis=1).astype(BF16)
        dz_ref[...] = dz
        dmkv_ref[...] += jnp.concatenate([_dot(dss[h], qs[h], TN) for h in heads] + [_dot(ps[h], dos[h], TN) for h in heads], axis=1)
        acc[...] += _dot(dz, x_ref[...], TN)

        @pl.when(pl.program_id(0) == nt - 1)
        def _():
            pltpu.sync_copy(acc, gwi_ref.at[pl.ds(W_A + W_B, W_C), :])

    row = pl.BlockSpec((t, D_MODEL), lambda i: (i, 0))
    return pl.pallas_call(
        body, name=name, grid=(nt,),
        out_shape=(jax.ShapeDtypeStruct((s, D_MODEL), BF16), jax.ShapeDtypeStruct((MEM_LEN, 2 * D_MODEL), F32),
                   jax.ShapeDtypeStruct((IN_COLS, D_MODEL), F32)),
        in_specs=[row, row, row, _resident((MEM_LEN, 2 * D_MODEL))],
        out_specs=(row, pl.BlockSpec((MEM_LEN, 2 * D_MODEL), lambda i: (0, 0)), HBM),
        scratch_shapes=[pltpu.VMEM((W_C, D_MODEL), F32)],
        compiler_params=_params(("arbitrary",)),
    )(xb, zc, do_c, mkv)


def _normalize(pre):
    mu = jnp.mean(pre, axis=-1, keepdims=True)
    xc = pre - mu
    rstd = lax.rsqrt(jnp.mean(xc * xc, axis=-1, keepdims=True) + LN_EPS)
    return xc * rstd, rstd


def _layer_norm_bwd(dh, xhat, rstd, g):
    dxh = dh * g
    dpre = rstd * (dxh - jnp.mean(dxh, axis=-1, keepdims=True) - xhat * jnp.mean(dxh * xhat, axis=-1, keepdims=True))
    return dpre, jnp.sum(dh * xhat, axis=0, keepdims=True), jnp.sum(dh, axis=0, keepdims=True)


def _merge_fwd(o_a, o_b, o_c, x, wi_t, wbr, wo, *, name):
    s = x.shape[0]
    t = min(256, s)
    row = lambda w: pl.BlockSpec((t, w), lambda i: (i, 0))

    def body(oa_ref, ob_ref, oc_ref, x_ref, wg_ref, wa_ref, wb_ref, wc_ref, wo_ref, zd_ref, xhat_ref, rstd_ref, merged_ref, pa_ref, pb_ref, pc_ref):
        wbr_refs = (wa_ref, wb_ref, wc_ref)
        zd_ref[...] = _dot(x_ref[...].astype(BF16), wg_ref[...], NT)
        merged = jnp.zeros((t, D_MODEL), F32)
        for b, (o_ref, p_ref) in enumerate(((oa_ref, pa_ref), (ob_ref, pb_ref), (oc_ref, pc_ref))):
            p = _dot(o_ref[...], wbr_refs[b][...], NN)
            p_ref[...] = p.astype(BF16)
            merged = merged + jax.nn.sigmoid(zd_ref[:, b * D_MODEL:(b + 1) * D_MODEL]) * p
        merged_b = merged.astype(BF16)
        merged_ref[...] = merged_b
        xhat, rstd = _normalize(ALPHA * x_ref[...] + _dot(merged_b, wo_ref[...], NN))
        xhat_ref[...] = xhat
        rstd_ref[...] = rstd

    act = jax.ShapeDtypeStruct((s, D_MODEL), F32)
    return pl.pallas_call(
        body, name=name, grid=(s // t,),
        out_shape=(jax.ShapeDtypeStruct((s, W_D), F32), act, jax.ShapeDtypeStruct((s, 1), F32)) + (jax.ShapeDtypeStruct((s, D_MODEL), BF16),) * 4,
        in_specs=[row(D_MODEL)] * 4 + [_resident_rows(wi_t, W_A + W_B + W_C, W_D)] + [_resident((D_MODEL, D_MODEL))] * 4,
        out_specs=(row(W_D), row(D_MODEL), row(1), row(D_MODEL), row(D_MODEL), row(D_MODEL), row(D_MODEL)),
        compiler_params=_params(("parallel",)),
    )(o_a, o_b, o_c, x, wi_t, *wbr, wo)


def _merge_bwd(dpre1, zd, pa, pb, pc, o_a, o_b, o_c, merged, wbr, wo, *, name, exchanges=()):
    s = dpre1.shape[0]
    t = min(256, s)
    nt = s // t
    row = lambda w: pl.BlockSpec((t, w), lambda i: (i, 0))

    def body(dpre_ref, zd_ref, pa_ref, pb_ref, pc_ref, oa_ref, ob_ref, oc_ref, mg_ref, wa_ref, wb_ref, wc_ref, wo_ref,
             dzd_ref, doa_ref, dob_ref, doc_ref, gwa_ref, gwb_ref, gwc_ref, gwo_ref, acc):
        step = pl.program_id(0)

        @pl.when(step == 0)
        def _():
            acc[...] = jnp.zeros_like(acc)

        dpre_b = dpre_ref[...].astype(BF16)
        dmerged = _dot(dpre_b, wo_ref[...], NT)
        acc[3] += _dot(mg_ref[...], dpre_b, TN)
        branches = ((pa_ref, oa_ref, doa_ref), (pb_ref, ob_ref, dob_ref), (pc_ref, oc_ref, doc_ref))
        for b, (p_ref, o_ref, do_ref) in enumerate(branches):
            gate = jax.nn.sigmoid(zd_ref[:, b * D_MODEL:(b + 1) * D_MODEL])
            dzd_ref[:, b * D_MODEL:(b + 1) * D_MODEL] = (dmerged * p_ref[...] * gate * (1.0 - gate)).astype(BF16)
            dp = (dmerged * gate).astype(BF16)
            acc[b] += _dot(o_ref[...], dp, TN)
            do_ref[...] = _dot(dp, (wa_ref, wb_ref, wc_ref)[b][...], NT).astype(do_ref.dtype)

        @pl.when(step == nt - 1)
        def _():
            for b, gw_ref in enumerate((gwa_ref, gwb_ref, gwc_ref, gwo_ref)):
                pltpu.sync_copy(acc.at[b], gw_ref)

    act = jax.ShapeDtypeStruct((s, D_MODEL), F32)
    actb = jax.ShapeDtypeStruct((s, D_MODEL), BF16)
    gw = jax.ShapeDtypeStruct((D_MODEL, D_MODEL), F32)
    return _fused_call(
        body, name=name, grid=(nt,),
        out_shape=(jax.ShapeDtypeStruct((s, W_D), BF16), act, actb, actb, gw, gw, gw, gw),
        in_specs=[row(D_MODEL), row(W_D)] + [row(D_MODEL)] * 7 + [_resident((D_MODEL, D_MODEL))] * 4,
        out_specs=(row(W_D),) + (row(D_MODEL),) * 3 + (HBM,) * 4, scratch_shapes=[pltpu.VMEM((4, D_MODEL, D_MODEL), F32)],
        operands=[dpre1, zd, pa, pb, pc, o_a, o_b, o_c, merged, *wbr, wo], exchanges=exchanges)


def _mlp_loss(xhat1, rstd1, target, ln1_g, ln1_b, ln2_g, ln2_b, wu, wd, *, name):
    s = xhat1.shape[0]
    t = min(256, s)
    npan = wu.shape[0]
    row = lambda w: pl.BlockSpec((t, w), lambda i: (i, 0))
    vec = _resident((1, D_MODEL))

    def body(xhat_ref, rstd_ref, tgt_ref, g1_ref, b1_ref, g2_ref, b2_ref, wu_ref, wd_ref,
             dpre1_ref, dpre2_ref, h1_ref, a_ref, du_ref, stats_ref):
        @pl.when(pl.program_id(0) == 0)
        def _():
            stats_ref[...] = jnp.zeros_like(stats_ref)

        xhat1_v = xhat_ref[...]
        h1 = xhat1_v * g1_ref[...] + b1_ref[...]
        h1_b = h1.astype(BF16)
        h1_ref[...] = h1_b
        us = []
        ff = jnp.zeros((t, D_MODEL), F32)
        for j in range(npan):
            u = _dot(h1_b, wu_ref[j], NN)
            us.append(u)
            r = jnp.maximum(u, 0.0)
            a_b = (r * r).astype(BF16)
            a_ref[:, j * D_MODEL:(j + 1) * D_MODEL] = a_b
            ff = ff + _dot(a_b, wd_ref[j], NN)
        xhat2, rstd2 = _normalize(ALPHA * h1 + ff)
        err = xhat2 * g2_ref[...] + b2_ref[...] - tgt_ref[...]
        stats_ref[4:5, :] += jnp.sum(err * err, axis=0, keepdims=True)
        dpre2, dg2, db2 = _layer_norm_bwd(err * (1.0 / D_MODEL), xhat2, rstd2, g2_ref[...])
        stats_ref[0:1, :] += dg2
        stats_ref[1:2, :] += db2
        dpre2_b = dpre2.astype(BF16)
        dpre2_ref[...] = dpre2_b
        dh1 = ALPHA * dpre2
        for j in range(npan):
            du_b = (_dot(dpre2_b, wd_ref[j], NT) * (2.0 * jnp.maximum(us[j], 0.0))).astype(BF16)
            du_ref[:, j * D_MODEL:(j + 1) * D_MODEL] = du_b
            dh1 = dh1 + _dot(du_b, wu_ref[j], NT)
        dpre1, dg1, db1 = _layer_norm_bwd(dh1, xhat1_v, rstd_ref[...], g1_ref[...])
        stats_ref[2:3, :] += dg1
        stats_ref[3:4, :] += db1
        dpre1_ref[...] = dpre1

    actb = jax.ShapeDtypeStruct((s, D_MODEL), BF16)
    wide = jax.ShapeDtypeStruct((s, D_FF), BF16)
    return pl.pallas_call(
        body, name=name, grid=(s // t,),
        out_shape=(jax.ShapeDtypeStruct((s, D_MODEL), F32), actb, actb, wide, wide, jax.ShapeDtypeStruct((8, D_MODEL), F32)),
        in_specs=[row(D_MODEL), row(1), row(D_MODEL), vec, vec, vec, vec,
                  _resident((npan, D_MODEL, D_MODEL)), _resident((npan, D_MODEL, D_MODEL))],
        out_specs=(row(D_MODEL), row(D_MODEL), row(D_MODEL), row(D_FF), row(D_FF), pl.BlockSpec((8, D_MODEL), lambda i: (0, 0))),
        compiler_params=_params(("arbitrary",)),
    )(xhat1, rstd1, target, ln1_g, ln1_b, ln2_g, ln2_b, wu, wd)


BRANCH_WEIGHTS = ("w_branch_hg", "w_branch_swa", "w_branch_mem")


def _local_step(x, xb, mem, target, wi_t, late, lb_logits, gain, sinks, rel_bias, ln1_g, ln1_b, ln2_g, ln2_b, *, distributed):
    s = x.shape[0]
    tm = min(1024, s)
    tk = min(2048, s)
    memb = mem.astype(BF16)
    if distributed:
        cx, cy, cc = lax.axis_index("x"), lax.axis_index("y"), lax.axis_index("c")
        pos = jnp.stack([2 * cx + cy, cc]).astype(jnp.int32)
    gather = (lambda names: [_gather_exchange([late[k] for k in names])]) if distributed else (lambda names: [])
    to_sibling = (lambda grads: [_sibling_halves_exchange(grads)]) if distributed else (lambda grads: [])
    to_chips = (lambda sums: [_chip_partials_exchange([bf for bf, _ in sums])]) if distributed else (lambda sums: [])

    def chip_sums(names, grads, from_sibling):
        return [_add_sibling(g, o, pos, name="add_sibling_" + k) for k, g, o in zip(names, grads, from_sibling)]

    def shard_sums(names, sums, from_chips):
        return {k: _add_chips(mine, o, pos, name="add_chips_" + k) for k, (_, mine), o in zip(names, sums, from_chips)}

    zb = _mm(xb, wi_t, mode="nt", tm=tm, tn=W_B, tk=D_MODEL, name="proj_b", out_dtype=BF16, b_rows=(W_A, W_B))
    onehot, maskrow = _bias_selector()
    bias_tables, sink_lanes = _swa_tables(_bias_table(rel_bias.T, onehot, maskrow, name="bias_table"), sinks)
    (za, o_a, o_raw, states), landed = _hgrn_fwd(xb, wi_t, lb_logits, gain, name="hgrn_fwd", exchanges=gather(("w_up", "w_down", "w_mem_kv")))
    wu, wd, wmkv = landed[0] if distributed else (late["wu"], late["wd"], late["wmkv"])
    mkv = _mm(memb, wmkv, mode="nn", tm=MEM_LEN, tn=512, tk=D_MODEL, name="mem_kv", out_dtype=BF16, b_panels=True)
    o_b, landed = _swa_fwd(zb, bias_tables, sink_lanes, name="swa_fwd", exchanges=gather(BRANCH_WEIGHTS + ("w_out",)))
    if distributed:
        wbr = [wb.reshape(D_MODEL, D_MODEL) for wb in landed[0][:3]]
        wo = landed[0][3].reshape(D_MODEL, D_MODEL)
    else:
        wbr, wo = [late["wbr"][b] for b in range(3)], late["wo"]
    zc, o_c = _mem_fwd(xb, wi_t, mkv, name="mem_fwd")
    zd, xhat1, rstd1, merged, pa, pb, pc = _merge_fwd(o_a, o_b, o_c, x, wi_t, wbr, wo, name="merge_fwd")

    dpre1, dpre2, h1, act, du, ln_stats = _mlp_loss(xhat1, rstd1, target, ln1_g, ln1_b, ln2_g, ln2_b, wu, wd, name="mlp_loss")
    ffn = ("w_down", "w_up")
    g_ffn = [_mm(act, dpre2, mode="tn", tm=1024, tn=D_MODEL, tk=tk, name="grad_w_down").reshape(N_SHARDS, D_FF // N_SHARDS, D_MODEL),
             _mm(h1, du, mode="tn", tm=D_MODEL, tn=1024, tk=tk, name="grad_w_up", out_panels=True)]

    (dzd, do_a, do_b, do_c, *g_merge), landed = _merge_bwd(dpre1, zd, pa, pb, pc, o_a, o_b, o_c, merged, wbr, wo, name="merge_bwd",
                                                           exchanges=to_sibling(g_ffn))
    sums_ffn = chip_sums(ffn, g_ffn, landed[0]) if distributed else []
    merge = BRANCH_WEIGHTS + ("w_out",)
    g_merge = [g.reshape(N_SHARDS, D_MODEL // N_SHARDS, D_MODEL) for g in g_merge]
    (dza, hg_stats), landed = _hgrn_bwd(za, o_raw, do_a, states, lb_logits, gain, name="hgrn_bwd",
                                        exchanges=to_chips(sums_ffn) + to_sibling(g_merge))
    halves = shard_sums(ffn, sums_ffn, landed[0]) if distributed else {}
    sums_merge = chip_sums(merge, g_merge, landed[1]) if distributed else []
    (dzb, dbias_t, dsinks), landed = _swa_bwd(zb, do_b, bias_tables, sink_lanes, name="swa_bwd", exchanges=to_chips(sums_merge))
    if distributed:
        halves.update(shard_sums(merge, sums_merge, landed[0]))
    dbias = dbias_t.reshape(2 * SWA_BLOCK, SWA_HEADS, SWA_BLOCK).transpose(1, 2, 0).reshape(SWA_HEADS, -1)
    d_rel_bias = _bias_grad(dbias, onehot, name="bias_grad").T
    dzc, dmkv, g_wi = _mem_bwd(xb, zc, do_c, mkv, name="mem_bwd")

    proj = ("w_in", "w_mem_kv")
    for dz, offset, nm in ((dza, 0, "grad_w_in_a"), (dzb, W_A, "grad_w_in_b"), (dzd, W_A + W_B + W_C, "grad_w_in_d")):
        g_wi = _mm(dz, xb, mode="tn", tm=dz.shape[1] if dz.shape[1] <= 1280 else 1024, tn=D_MODEL, tk=tk, name=nm,
                   rows_of=IN_COLS, row_offset=offset, into=g_wi)
    g_proj = [g_wi.reshape(N_SHARDS, IN_COLS // N_SHARDS, D_MODEL),
              _mm(memb, dmkv, mode="tn", tm=D_MODEL, tn=512, tk=MEM_LEN, name="grad_w_mem_kv", out_panels=True)]
    sums_proj = chip_sums(proj, g_proj, _run_exchanges(to_sibling(g_proj), name="reduce_sibling_proj")[0]) if distributed else []
    small = dict(lb_logits=hg_stats[1:3], hg_norm_gain=hg_stats[0:1], swa_sinks=dsinks, rel_bias=d_rel_bias,
                 ln1_g=ln_stats[2:3], ln1_b=ln_stats[3:4], ln2_g=ln_stats[0:1], ln2_b=ln_stats[1:2], sq_err=ln_stats[4:5])
    small_exchange = [_small_gather_exchange(_pack_small(small, name="pack_small"))] if distributed else []
    grad_x, landed = _dx_matmul([dza, dzb, dzc, dzd], wi_t, dpre1, tm=min(512, s), name="grad_x", exchanges=to_chips(sums_proj) + small_exchange)
    if distributed:
        halves.update(shard_sums(proj, sums_proj, landed[0]))
        small = landed[1][0]
    else:
        halves = dict(zip(ffn + merge + proj, g_ffn + g_merge + g_proj))
    return grad_x, halves, small


def _mesh_position():
    x, y, c = lax.axis_index("x"), lax.axis_index("y"), lax.axis_index("c")
    chips = [(1 - x, y), (x, 1 - y), (1 - x, 1 - y)]
    return x, y, c, chips


class _Exchange(NamedTuple):
    operands: list
    out_shapes: list
    n_sems: int
    start: Callable
    finish: Callable
    halfway: Optional[Callable] = None


def _gather_exchange(shards):
    n = len(shards)
    per = 9
    assert all(w.shape[0] % (4 * BF16_SUBLANES) == 0 for w in shards)

    def plan(ins, outs, send_sems, recv_sems):
        x, y, c, (x_nbr, y_nbr, diag) = _mesh_position()
        sibling = (x, y, 1 - c)
        slot = lambda chip: 2 * chip[0] + chip[1]

        def rows(a, chip, hc, quarter=None):
            rh = shards[a].shape[0] // 2
            if quarter is None:
                return outs[a].at[slot(chip), pl.ds(hc * rh, rh), :]
            return outs[a].at[slot(chip), pl.ds(hc * rh + quarter * (rh // 2), rh // 2), :]

        def copy(a, k, src, dst, to):
            return pltpu.make_async_remote_copy(src_ref=src, dst_ref=dst, send_sem=send_sems.at[a * per + k], recv_sem=recv_sems.at[a * per + k],
                                                device_id=to, device_id_type=MESH)

        first, from_sibling = [], []
        landed, then = [[] for _ in range(4)], [[] for _ in range(4)]
        for a in range(n):
            rh = shards[a].shape[0] // 2
            my_half = ins[a].at[pl.ds(c * rh, rh), :]
            first += [copy(a, 4, ins[a], outs[a].at[slot((x, y))], sibling),
                      copy(a, 0, my_half, rows(a, (x, y), c), (*x_nbr, c)), copy(a, 1, my_half, rows(a, (x, y), c), (*y_nbr, c))]
            landed[0].append(copy(a, 0, rows(a, x_nbr, c), rows(a, x_nbr, c), (*x_nbr, c)))
            then[0].append([copy(a, 2, rows(a, x_nbr, c, 0), rows(a, x_nbr, c, 0), (*y_nbr, c)), copy(a, 5, rows(a, x_nbr, c), rows(a, x_nbr, c), sibling)])
            landed[1].append(copy(a, 1, rows(a, y_nbr, c), rows(a, y_nbr, c), (*y_nbr, c)))
            then[1].append([copy(a, 3, rows(a, y_nbr, c, 1), rows(a, y_nbr, c, 1), (*x_nbr, c)), copy(a, 6, rows(a, y_nbr, c), rows(a, y_nbr, c), sibling)])
            landed[2].append(copy(a, 2, rows(a, diag, c, 0), rows(a, diag, c, 0), (*y_nbr, c)))
            then[2].append([copy(a, 7, rows(a, diag, c, 0), rows(a, diag, c, 0), sibling)])
            landed[3].append(copy(a, 3, rows(a, diag, c, 1), rows(a, diag, c, 1), (*x_nbr, c)))
            then[3].append([copy(a, 8, rows(a, diag, c, 1), rows(a, diag, c, 1), sibling)])
            from_sibling += [copy(a, 4, outs[a].at[slot((x, y))], outs[a].at[slot((x, y))], sibling),
                             copy(a, 5, rows(a, x_nbr, 1 - c), rows(a, x_nbr, 1 - c), sibling), copy(a, 6, rows(a, y_nbr, 1 - c), rows(a, y_nbr, 1 - c), sibling),
                             copy(a, 7, rows(a, diag, 1 - c, 0), rows(a, diag, 1 - c, 0), sibling), copy(a, 8, rows(a, diag, 1 - c, 1), rows(a, diag, 1 - c, 1), sibling)]
        return first, landed, then, from_sibling

    def start(*refs):
        first, _, _, _ = plan(*refs)
        for cp in first:
            cp.start()

    def stages(landed, then, which):
        for stage in which:
            for arrival, onward in zip(landed[stage], then[stage]):
                arrival.wait_recv()
                for cp in onward:
                    cp.start()

    def halfway(*refs):
        _, landed, then, _ = plan(*refs)
        stages(landed, then, (0, 1))

    def finish(*refs):
        first, landed, then, from_sibling = plan(*refs)
        stages(landed, then, (2, 3))
        for cp in from_sibling:
            cp.wait_recv()
        for cp in first + [cp for stage in then for onward in stage for cp in onward]:
            cp.wait_send()

    return _Exchange(list(shards), [jax.ShapeDtypeStruct((N_SHARDS,) + w.shape, w.dtype) for w in shards], per * n, start, finish, halfway)


def _sibling_halves_exchange(grads):
    n = len(grads)

    def plan(ins, outs, send_sems, recv_sems):
        x, y, c, _ = _mesh_position()
        return [pltpu.make_async_remote_copy(src_ref=ins[a].at[:, pl.ds((1 - c) * (grads[a].shape[1] // 2), grads[a].shape[1] // 2), :],
                                             dst_ref=outs[a], send_sem=send_sems.at[a], recv_sem=recv_sems.at[a],
                                             device_id=(x, y, 1 - c), device_id_type=MESH) for a in range(n)]

    def start(*refs):
        for cp in plan(*refs):
            cp.start()

    def finish(*refs):
        for cp in plan(*refs):
            cp.wait()

    return _Exchange(list(grads), [jax.ShapeDtypeStruct((g.shape[0], g.shape[1] // 2, g.shape[2]), g.dtype) for g in grads], n, start, finish)


def _chip_partials_exchange(sums):
    n = len(sums)

    def plan(ins, outs, send_sems, recv_sems):
        _, _, c, chips = _mesh_position()
        return [pltpu.make_async_remote_copy(src_ref=ins[a].at[2 * cx + cy], dst_ref=outs[a].at[k], send_sem=send_sems.at[a * 3 + k],
                                             recv_sem=recv_sems.at[a * 3 + k], device_id=(cx, cy, c), device_id_type=MESH)
                for k, (cx, cy) in enumerate(chips) for a in range(n)]

    def start(*refs):
        for cp in plan(*refs):
            cp.start()

    def finish(*refs):
        for cp in plan(*refs):
            cp.wait()

    return _Exchange(list(sums), [jax.ShapeDtypeStruct((3,) + g.shape[1:], g.dtype) for g in sums], 3 * n, start, finish)


def _fused_call(body, *, name, grid, in_specs, out_specs, out_shape, scratch_shapes, operands, exchanges=()):
    single = not isinstance(out_shape, (tuple, list))
    out_specs = [out_specs] if single else list(out_specs)
    out_shape = [out_shape] if single else list(out_shape)
    n_in, n_out, n_scr = len(in_specs), len(out_specs), len(scratch_shapes)
    x_in = [len(e.operands) for e in exchanges]
    x_out = [len(e.out_shapes) for e in exchanges]

    def wrapped(*refs):
        refs = list(refs)
        ins = refs[:n_in]
        pos = n_in
        ex_ins = []
        for k in x_in:
            ex_ins.append(refs[pos:pos + k])
            pos += k
        outs = refs[pos:pos + n_out]
        pos += n_out
        ex_outs = []
        for k in x_out:
            ex_outs.append(refs[pos:pos + k])
            pos += k
        scratch = refs[pos:pos + n_scr]
        sems = refs[pos + n_scr:]
        first, last, middle = None, None, None
        for axis, size in enumerate(grid):
            at_start, at_end, at_middle = pl.program_id(axis) == 0, pl.program_id(axis) == size - 1, pl.program_id(axis) == size // 2
            first = at_start if first is None else first & at_start
            last = at_end if last is None else last & at_end
            middle = at_middle if middle is None else middle & at_middle

        @pl.when(first)
        def _():
            for i, e in enumerate(exchanges):
                e.start(ex_ins[i], ex_outs[i], sems[2 * i], sems[2 * i + 1])

        if any(e.halfway for e in exchanges):
            @pl.when(middle)
            def _():
                for i, e in enumerate(exchanges):
                    if e.halfway:
                        e.halfway(ex_ins[i], ex_outs[i], sems[2 * i], sems[2 * i + 1])

        body(*ins, *outs, *scratch)

        @pl.when(last)
        def _():
            for i, e in enumerate(exchanges):
                e.finish(ex_ins[i], ex_outs[i], sems[2 * i], sems[2 * i + 1])

    n_x_in, n_x_out = sum(x_in), sum(x_out)
    results = pl.pallas_call(
        wrapped if exchanges else body, name=name, grid=grid,
        in_specs=list(in_specs) + [HBM] * n_x_in,
        out_specs=out_specs + [HBM] * n_x_out,
        out_shape=out_shape + [s for e in exchanges for s in e.out_shapes],
        scratch_shapes=list(scratch_shapes) + [pltpu.SemaphoreType.DMA((e.n_sems,)) for e in exchanges for _ in range(2)],
        compiler_params=_params(("arbitrary",) * len(grid)),
    )(*operands, *[a for e in exchanges for a in e.operands])
    own = results[0] if single else tuple(results[:n_out])
    landed, pos = [], n_out
    for k in x_out:
        landed.append(list(results[pos:pos + k]))
        pos += k
    return own, landed


def _cast_bf16(x, *, name, exchanges=()):
    s, cols = x.shape
    t = min(512, s)

    def body(x_ref, o_ref):
        o_ref[...] = x_ref[...].astype(BF16)

    tile = pl.BlockSpec((t, cols), lambda i: (i, 0))
    return _fused_call(body, name=name, grid=(s // t,), in_specs=[tile], out_specs=tile, out_shape=jax.ShapeDtypeStruct((s, cols), BF16),
                       scratch_shapes=[], operands=[x], exchanges=exchanges)


def _run_exchanges(exchanges, *, name):
    def body(*refs):
        n_in = sum(len(e.operands) for e in exchanges)
        n_out = sum(len(e.out_shapes) for e in exchanges)
        ins, outs, sems = refs[:n_in], refs[n_in:n_in + n_out], refs[n_in + n_out:]
        spans, i, o = [], 0, 0
        for e in exchanges:
            spans.append((ins[i:i + len(e.operands)], outs[o:o + len(e.out_shapes)]))
            i, o = i + len(e.operands), o + len(e.out_shapes)
        for k, e in enumerate(exchanges):
            e.start(*spans[k], sems[2 * k], sems[2 * k + 1])
        for k, e in enumerate(exchanges):
            if e.halfway:
                e.halfway(*spans[k], sems[2 * k], sems[2 * k + 1])
        for k, e in enumerate(exchanges):
            e.finish(*spans[k], sems[2 * k], sems[2 * k + 1])

    operands = [a for e in exchanges for a in e.operands]
    shapes = [s for e in exchanges for s in e.out_shapes]
    results = pl.pallas_call(
        body, name=name, out_shape=shapes, in_specs=[HBM] * len(operands), out_specs=[HBM] * len(shapes),
        scratch_shapes=[pltpu.SemaphoreType.DMA((e.n_sems,)) for e in exchanges for _ in range(2)],
    )(*operands)
    landed, pos = [], 0
    for e in exchanges:
        landed.append(list(results[pos:pos + len(e.out_shapes)]))
        pos += len(e.out_shapes)
    return landed


ROW_TILE_MAX = 640
BF16_SUBLANES = 16


def _row_tile(rows):
    for tr in range(min(rows, ROW_TILE_MAX), 0, -1):
        if rows % tr == 0 and tr % BF16_SUBLANES == 0:
            return tr
    raise ValueError(rows)


def _add_sibling(grad, other, pos, *, name):
    p, r, cols = grad.shape
    rh = r // 2
    tr = _row_tile(rh)
    nb = rh // tr

    def body(pos_ref, g_ref, o_ref, sb_ref, mine_ref):
        total = g_ref[...] + o_ref[...]
        sb_ref[...] = total.astype(BF16)

        @pl.when(pl.program_id(1) == pos_ref[0])
        def _():
            mine_ref[...] = total

    return pl.pallas_call(
        body, name=name, out_shape=(jax.ShapeDtypeStruct((p, rh, cols), BF16), jax.ShapeDtypeStruct((rh, cols), F32)),
        grid_spec=pltpu.PrefetchScalarGridSpec(
            num_scalar_prefetch=1, grid=(nb, p),
            in_specs=[pl.BlockSpec((None, tr, cols), lambda i, j, pos_ref: (j, pos_ref[1] * nb + i, 0)),
                      pl.BlockSpec((None, tr, cols), lambda i, j, pos_ref: (j, i, 0))],
            out_specs=(pl.BlockSpec((None, tr, cols), lambda i, j, pos_ref: (j, i, 0)),
                       pl.BlockSpec((tr, cols), lambda i, j, pos_ref: (i, 0)))),
        compiler_params=_params(("parallel", "arbitrary")),
    )(pos, grad, other)


def _add_chips(mine, others, pos, *, name):
    rh, cols = mine.shape
    tr = _row_tile(rh)
    nb = rh // tr

    def body(pos_ref, s_ref, o_ref, r_ref):
        r_ref[...] = ((s_ref[...] + o_ref[0].astype(F32)) + o_ref[1].astype(F32)) + o_ref[2].astype(F32)

    return pl.pallas_call(
        body, name=name, out_shape=jax.ShapeDtypeStruct((2 * rh, cols), F32),
        grid_spec=pltpu.PrefetchScalarGridSpec(
            num_scalar_prefetch=1, grid=(nb,),
            in_specs=[pl.BlockSpec((tr, cols), lambda i, pos_ref: (i, 0)),
                      pl.BlockSpec((3, tr, cols), lambda i, pos_ref: (0, i, 0))],
            out_specs=pl.BlockSpec((tr, cols), lambda i, pos_ref: (pos_ref[1] * nb + i, 0))),
        compiler_params=_params(("parallel",)),
    )(pos, mine, others)


def _join_halves(bufs, *, name):
    n = len(bufs)

    def body(*refs):
        ins, outs = refs[:n], refs[n:2 * n]
        send_sems, recv_sems = refs[2 * n:]
        x, y, c, _ = _mesh_position()

        def copy(a, hc):
            rh = bufs[a].shape[0] // 2
            rows = pl.ds(hc * rh, rh)
            return pltpu.make_async_remote_copy(src_ref=ins[a].at[rows, :], dst_ref=outs[a].at[rows, :], send_sem=send_sems.at[a],
                                                recv_sem=recv_sems.at[a], device_id=(x, y, 1 - c), device_id_type=MESH)

        for a in range(n):
            copy(a, c).start()
        for a in range(n):
            copy(a, c).wait_send()
            copy(a, 1 - c).wait_recv()

    return pl.pallas_call(
        body, name=name, out_shape=[jax.ShapeDtypeStruct(b.shape, b.dtype) for b in bufs],
        in_specs=[HBM] * n, out_specs=[HBM] * n, input_output_aliases={a: a for a in range(n)},
        scratch_shapes=[pltpu.SemaphoreType.DMA((n,)), pltpu.SemaphoreType.DMA((n,))],
    )(*bufs)


SMALL = ["lb_logits", "hg_norm_gain", "swa_sinks", "rel_bias", "ln1_g", "ln1_b", "ln2_g", "ln2_b"]
PACK_ROWS = 48
PACK_AT = dict(lb_logits=(slice(0, 2), slice(0, D_MODEL)), hg_norm_gain=(slice(2, 3), slice(0, D_MODEL)), ln1_g=(slice(3, 4), slice(0, D_MODEL)),
               ln1_b=(slice(4, 5), slice(0, D_MODEL)), ln2_g=(slice(5, 6), slice(0, D_MODEL)), ln2_b=(slice(6, 7), slice(0, D_MODEL)),
               swa_sinks=(slice(7, 8), slice(0, SWA_HEADS)), sq_err=(slice(8, 9), slice(0, D_MODEL)),
               rel_bias=(slice(16, 16 + NUM_BUCKETS), slice(0, SWA_HEADS)))


def _pack_small(grads, *, name):
    names = SMALL + ["sq_err"]

    def body(*refs):
        packed = refs[len(names)]
        packed[...] = jnp.zeros_like(packed)
        for k, g_ref in zip(names, refs):
            packed[PACK_AT[k]] = g_ref[...]

    return pl.pallas_call(body, name=name, out_shape=jax.ShapeDtypeStruct((PACK_ROWS, D_MODEL), F32), compiler_params=_params(),
                          )(*[grads[k] for k in names])


def _small_gather_exchange(packed):
    def plan(ins, outs, send_sems, recv_sems):
        x, y, c, _ = _mesh_position()
        me = 4 * x + 2 * y + c
        own = pltpu.make_async_copy(ins[0], outs[0].at[me], send_sems.at[7])
        remote = []
        for d in range(1, 8):
            dx, dy, dc = (d >> 2) & 1, (d >> 1) & 1, d & 1
            remote.append(pltpu.make_async_remote_copy(src_ref=ins[0], dst_ref=outs[0].at[me], send_sem=send_sems.at[d - 1],
                                                       recv_sem=recv_sems.at[d - 1], device_id=(x ^ dx, y ^ dy, c ^ dc), device_id_type=MESH))
        return own, remote

    def start(*refs):
        own, remote = plan(*refs)
        own.start()
        for cp in remote:
            cp.start()

    def finish(*refs):
        own, remote = plan(*refs)
        for cp in remote:
            cp.wait()
        own.wait()

    return _Exchange([packed], [jax.ShapeDtypeStruct((8,) + packed.shape, packed.dtype)], 8, start, finish)


def _adamw_small(gathered, w, m, v, *, name):
    names = SMALL
    n = len(names)

    def body(*refs):
        gathered_ref = refs[0]
        w_refs, m_refs, v_refs = (dict(zip(names, refs[1 + i * n:1 + (i + 1) * n])) for i in range(3))
        loss_ref = refs[1 + 3 * n]
        go_refs, d_refs, nm_refs, nv_refs = (dict(zip(names, refs[2 + (3 + i) * n:2 + (4 + i) * n])) for i in range(4))
        total_ref = refs[2 + 7 * n]
        total = gathered_ref[0]
        for j in range(1, 8):
            total = total + gathered_ref[j]
        total_ref[...] = total
        loss_ref[...] = (0.5 / D_MODEL) * jnp.sum(total_ref[PACK_AT["sq_err"]], axis=1, keepdims=True)
        for k in names:
            g = total_ref[PACK_AT[k]]
            go_refs[k][...] = g
            d_refs[k][...], nm_refs[k][...], nv_refs[k][...] = _adamw_math(w_refs[k][...], g, m_refs[k][...], v_refs[k][...])

    like = [jax.ShapeDtypeStruct(w[k].shape, F32) for k in names]
    results = pl.pallas_call(body, name=name, out_shape=[jax.ShapeDtypeStruct((1, 1), F32)] + like * 4,
                             scratch_shapes=[pltpu.VMEM((PACK_ROWS, D_MODEL), F32)],
                             compiler_params=_params())(gathered, *[d[k] for d in (w, m, v) for k in names])
    return results[0], {k: tuple(results[1 + i * n + j] for i in range(4)) for j, k in enumerate(names)}


def _adamw_math(w, g, m, v):
    m = ADAM_B1 * m + (1.0 - ADAM_B1) * g
    v = ADAM_B2 * v + (1.0 - ADAM_B2) * (g * g)
    m_hat = m / (1.0 - ADAM_B1 ** ADAM_STEP)
    v_hat = v / (1.0 - ADAM_B2 ** ADAM_STEP)
    delta = -ADAM_LR * (m_hat / (jnp.sqrt(v_hat) + ADAM_EPS) + ADAM_WD * w)
    return delta, m, v


def _adamw(w, g, m, v, *, name):
    _, rows, cols = w.shape
    tr = _row_tile(rows)
    blk = pl.BlockSpec((None, tr, cols), lambda i: (0, i, 0))
    flat = pl.BlockSpec((tr, cols), lambda i: (i, 0))

    def body(w_ref, g_ref, m_ref, v_ref, go_ref, d_ref, nm_ref, nv_ref):
        g_v = g_ref[...]
        go_ref[...] = g_v
        d_ref[...], nm_ref[...], nv_ref[...] = _adamw_math(w_ref[...], g_v, m_ref[...], v_ref[...])

    shape = jax.ShapeDtypeStruct((1, rows, cols), F32)
    return pl.pallas_call(body, name=name, grid=(rows // tr,), out_shape=(shape,) * 4, in_specs=[blk, flat, blk, blk], out_specs=(blk,) * 4,
                          compiler_params=_params(("parallel",)))(w, g, m, v)


WEIGHTS = ["w_in", "lb_logits", "hg_norm_gain", "swa_sinks", "rel_bias", "w_mem_kv", "w_branch_hg", "w_branch_swa", "w_branch_mem",
           "w_out", "ln1_g", "ln1_b", "w_up", "w_down", "ln2_g", "ln2_b"]
BIG = ["w_in", "w_mem_kv", "w_branch_hg", "w_branch_swa", "w_branch_mem", "w_out", "w_up", "w_down"]


def kernel(x, mem, w_in, lb_logits, hg_norm_gain, swa_sinks, rel_bias, w_mem_kv, w_branch_hg, w_branch_swa, w_branch_mem, w_out, ln1_g, ln1_b, w_up, w_down, ln2_g, ln2_b, loss_target, m_w_in, m_lb_logits, m_hg_norm_gain, m_swa_sinks, m_rel_bias, m_w_mem_kv, m_w_branch_hg, m_w_branch_swa, m_w_branch_mem, m_w_out, m_ln1_g, m_ln1_b, m_w_up, m_w_down, m_ln2_g, m_ln2_b, v_w_in, v_lb_logits, v_hg_norm_gain, v_swa_sinks, v_rel_bias, v_w_mem_kv, v_w_branch_hg, v_w_branch_swa, v_w_branch_mem, v_w_out, v_ln1_g, v_ln1_b, v_w_up, v_w_down, v_ln2_g, v_ln2_b):
    w = dict(w_in=w_in, lb_logits=lb_logits, hg_norm_gain=hg_norm_gain, swa_sinks=swa_sinks, rel_bias=rel_bias, w_mem_kv=w_mem_kv,
             w_branch_hg=w_branch_hg, w_branch_swa=w_branch_swa, w_branch_mem=w_branch_mem, w_out=w_out, ln1_g=ln1_g, ln1_b=ln1_b,
             w_up=w_up, w_down=w_down, ln2_g=ln2_g, ln2_b=ln2_b)
    m = dict(w_in=m_w_in, lb_logits=m_lb_logits, hg_norm_gain=m_hg_norm_gain, swa_sinks=m_swa_sinks, rel_bias=m_rel_bias, w_mem_kv=m_w_mem_kv,
             w_branch_hg=m_w_branch_hg, w_branch_swa=m_w_branch_swa, w_branch_mem=m_w_branch_mem, w_out=m_w_out, ln1_g=m_ln1_g, ln1_b=m_ln1_b,
             w_up=m_w_up, w_down=m_w_down, ln2_g=m_ln2_g, ln2_b=m_ln2_b)
    v = dict(w_in=v_w_in, lb_logits=v_lb_logits, hg_norm_gain=v_hg_norm_gain, swa_sinks=v_swa_sinks, rel_bias=v_rel_bias, w_mem_kv=v_w_mem_kv,
             w_branch_hg=v_w_branch_hg, w_branch_swa=v_w_branch_swa, w_branch_mem=v_w_branch_mem, w_out=v_w_out, ln1_g=v_ln1_g, ln1_b=v_ln1_b,
             w_up=v_w_up, w_down=v_w_down, ln2_g=v_ln2_g, ln2_b=v_ln2_b)
    shapes = {k: w[k].shape for k in WEIGHTS}
    for d in (w, m, v):
        d["w_in"] = d["w_in"].reshape(D_MODEL, IN_COLS // N_SHARDS).T[None]
    shards = {k: w[k].reshape(w[k].shape[-2], w[k].shape[-1]).astype(BF16) for k in BIG}
    x2d = x.reshape(x.shape[-2], D_MODEL)
    xb, ((wi4,),) = _cast_bf16(x2d, name="gather_weights", exchanges=[_gather_exchange([shards["w_in"]])])
    wi_t = wi4.reshape(IN_COLS, D_MODEL)

    grad_x, halves, small = _local_step(
        x2d, xb, mem.reshape(MEM_LEN, D_MODEL), loss_target.reshape(loss_target.shape[-2], D_MODEL),
        wi_t, shards, lb_logits, hg_norm_gain, swa_sinks, rel_bias, ln1_g, ln1_b, ln2_g, ln2_b, distributed=True)

    reduced = dict(zip(BIG, _join_halves([halves[k] for k in BIG], name="join_halves")))

    outs = {k: _adamw(w[k], reduced[k], m[k], v[k], name="adamw_" + k) for k in BIG}
    loss, small_outs = _adamw_small(small, w, m, v, name="adamw_small")
    outs.update(small_outs)
    grad_out, delta_out, m_out, v_out = ({k: outs[k][i] for k in WEIGHTS} for i in range(4))
    for out in (grad_out, delta_out, m_out, v_out):
        out["w_in"] = out["w_in"][0].T

    result = [loss.reshape(()), grad_x.reshape(x.shape)]
    for out in (grad_out, delta_out, m_out, v_out):
        result += [out[k].reshape(shapes[k]) for k in WEIGHTS]
    return tuple(result)
```

```python
import functools
import math
from typing import Callable, NamedTuple, Optional

import jax
import jax.numpy as jnp
from jax import lax
from jax.experimental import pallas as pl
from jax.experimental.pallas import tpu as pltpu

F32 = jnp.float32
BF16 = jnp.bfloat16
HIGHEST = lax.Precision.HIGHEST
MESH = pl.DeviceIdType.MESH

D_MODEL = 1024
MEM_LEN = 256
HG_HEADS = 8
HG_DK = 128
HG_CHUNK = 64
SWA_HEADS = 16
SWA_KV_HEADS = 2
SWA_GROUP = 8
SWA_HEAD_DIM = 64
SWA_BLOCK = 128
SWA_WINDOW = 128
MEM_HEADS = 4
MEM_HEAD_DIM = 256
NUM_BUCKETS = 32
MAX_DISTANCE = 128
D_FF = 4096
LN_EPS = 1e-5
RMS_EPS = 1e-6
ALPHA = 2.0 ** 0.25
W_A, W_B, W_C, W_D = 4096, 1280, 1024, 3072
IN_COLS = W_A + W_B + W_C + W_D
N_SHARDS = 4
ADAM_LR = 0.001
ADAM_B1 = 0.9
ADAM_B2 = 0.999
ADAM_EPS = 1e-08
ADAM_WD = 0.01
ADAM_STEP = 10
MASK_VALUE = -1e30
VMEM_LIMIT = 56 * 1024 * 1024

NN = ((1,), (0,))
NT = ((1,), (1,))
TN = ((0,), (0,))
HBM = pl.BlockSpec(memory_space=pltpu.HBM)


def _dot(a, b, dims=NN, precision=None):
    return lax.dot_general(a, b, (dims, ((), ())), precision=precision, preferred_element_type=F32)


def _params(sem=None):
    return pltpu.CompilerParams(dimension_semantics=sem, vmem_limit_bytes=VMEM_LIMIT)


def _resident(shape):
    zeros = (0,) * len(shape)
    return pl.BlockSpec(shape, lambda *_: zeros, pipeline_mode=pl.Buffered(1))


def _resident_rows(arr, offset, rows):
    return pl.BlockSpec((pl.Element(rows), pl.Element(arr.shape[1])), lambda *_: (offset, 0), pipeline_mode=pl.Buffered(1))


def _mm(a, b, *, mode, tm, tn, tk, name, out_dtype=F32, b_panels=False, b_rows=None, out_panels=False, rows_of=None, row_offset=0,
        into=None):
    if mode == "tn":
        kdim, m = a.shape
    else:
        m, kdim = a.shape
    if b_panels:
        n = b.shape[0] * b.shape[2]
        assert b.shape[2] == tn and mode == "nn"
    elif b_rows is not None:
        assert mode == "nt"
        b_offset, n = b_rows
    elif mode == "nt":
        n = b.shape[0]
    else:
        n = b.shape[1]
    assert m % tm == 0 and n % tn == 0 and kdim % tk == 0, (name, m, n, kdim)
    nk = kdim // tk
    dims = {"nn": NN, "nt": NT, "tn": TN}[mode]
    a_spec = pl.BlockSpec((tk, tm), lambda i, j, k: (k, i)) if mode == "tn" else pl.BlockSpec((tm, tk), lambda i, j, k: (i, k))
    if b_panels:
        b_spec = pl.BlockSpec((None, tk, tn), lambda i, j, k: (j, k, 0))
    elif b_rows is not None:
        assert b_offset % BF16_SUBLANES == 0 and tn % BF16_SUBLANES == 0 and tk % 128 == 0
        b_spec = pl.BlockSpec((pl.Element(tn), pl.Element(tk)),
                              lambda i, j, k: (pl.multiple_of(b_offset + j * tn, BF16_SUBLANES), pl.multiple_of(k * tk, 128)))
    elif mode == "nt":
        b_spec = pl.BlockSpec((tn, tk), lambda i, j, k: (j, k))
    else:
        b_spec = pl.BlockSpec((tk, tn), lambda i, j, k: (k, j))
    in_specs = [a_spec, b_spec]
    operands = [a, b]
    aliases = {}
    if out_panels:
        out_shape = jax.ShapeDtypeStruct((n // tn, m, tn), out_dtype)
        o_spec = pl.BlockSpec((None, tm, tn), lambda i, j, k: (j, i, 0))
    elif rows_of is not None:
        out_shape = jax.ShapeDtypeStruct((rows_of, n), out_dtype)
        assert row_offset % BF16_SUBLANES == 0 and tm % BF16_SUBLANES == 0 and tn % 128 == 0
        o_spec = pl.BlockSpec((pl.Element(tm), pl.Element(tn)),
                              lambda i, j, k: (pl.multiple_of(row_offset + i * tm, BF16_SUBLANES), pl.multiple_of(j * tn, 128)))
        if into is not None:
            in_specs.append(pl.BlockSpec(memory_space=pl.ANY))
            operands.append(into)
            aliases = {2: 0}
    else:
        out_shape = jax.ShapeDtypeStruct((m, n), out_dtype)
        o_spec = pl.BlockSpec((tm, tn), lambda i, j, k: (i, j))
    n_in = len(operands)

    def body(*refs):
        a_ref, b_ref, o_ref = refs[0], refs[1], refs[n_in]
        part = _dot(a_ref[...].astype(BF16), b_ref[...].astype(BF16), dims)

        def finish(acc):
            o_ref[...] = acc.astype(out_dtype)

        if nk == 1:
            finish(part)
        else:
            acc_ref = refs[-1]
            k = pl.program_id(2)

            @pl.when(k == 0)
            def _():
                acc_ref[...] = part

            @pl.when(k > 0)
            def _():
                acc_ref[...] += part

            @pl.when(k == nk - 1)
            def _():
                finish(acc_ref[...])

    return pl.pallas_call(
        body, name=name, out_shape=out_shape, grid=(m // tm, n // tn, nk), in_specs=in_specs, out_specs=o_spec,
        scratch_shapes=[pltpu.VMEM((tm, tn), F32)] if nk > 1 else [], input_output_aliases=aliases,
        compiler_params=_params(("parallel", "parallel", "arbitrary")),
    )(*operands)


def _dx_matmul(dzs, wi_t, resid, *, tm, tiles, name, exchanges=()):
    first_tile, count = tiles
    npieces = len(dzs)
    offsets = [sum(dz.shape[1] for dz in dzs[:p]) for p in range(npieces)]
    tile = lambda i: (i + first_tile, 0)
    in_specs = [pl.BlockSpec((tm, dz.shape[1]), tile) for dz in dzs] + [_resident(wi_t.shape), pl.BlockSpec((tm, D_MODEL), tile)]

    def body(*refs):
        dz_refs, w_ref, r_ref, o_ref = refs[:npieces], refs[npieces], refs[npieces + 1], refs[npieces + 2]
        total = ALPHA * r_ref[...]
        for p in range(npieces):
            total = total + _dot(dz_refs[p][...], w_ref[offsets[p]:offsets[p] + dzs[p].shape[1], :], NN)
        o_ref[...] = total

    return _fused_call(
        body, name=name, out_shape=jax.ShapeDtypeStruct((count * tm, D_MODEL), F32), grid=(count,), in_specs=in_specs,
        out_specs=pl.BlockSpec((tm, D_MODEL), lambda i: (i, 0)), scratch_shapes=[], operands=[*dzs, wi_t, resid], exchanges=exchanges)


def _lower_bound(lbl_ref):
    l0, l1 = lbl_ref[0:1, :], lbl_ref[1:2, :]
    mx = jnp.maximum(l0, l1)
    e0, e1 = jnp.exp(l0 - mx), jnp.exp(l1 - mx)
    return e0 / (e0 + e1)


HEAD_COLS = [slice(h * HG_DK, (h + 1) * HG_DK) for h in range(HG_HEADS)]


def _head_mean(x):
    return jnp.concatenate([jnp.broadcast_to(jnp.mean(x[:, c], axis=-1, keepdims=True), (x.shape[0], HG_DK)) for c in HEAD_COLS], axis=1)


def _triangle_sum(tri_b, x):
    p0 = x.astype(BF16)
    r1 = x - p0.astype(F32)
    p1 = r1.astype(BF16)
    p2 = (r1 - p1.astype(F32)).astype(BF16)
    return _dot(tri_b, p0) + _dot(tri_b, p1) + _dot(tri_b, p2)


def _chunk_forward(q, fl, v, lb, tril_b):
    sg = jax.nn.sigmoid(fl)
    f = lb + (1.0 - lb) * sg
    k = 1.0 - f
    b = _triangle_sum(tril_b, jnp.log(f))
    b_last = b[HG_CHUNK - 1:HG_CHUNK, :]
    eb, enb, eo = jnp.exp(b), jnp.exp(-b), jnp.exp(b_last - b)
    return sg, f, k, b_last, eb, enb, eo, q * eb, k * enb, k * eo


def _hgrn_fwd(xb, wi_t, lb_logits, gain, *, name, exchanges=()):
    s = xb.shape[0]
    t = min(256, s)
    ncs = t // HG_CHUNK

    def body(x_ref, w_ref, lbl_ref, gain_ref, z_ref, oa_ref, oraw_ref, st_ref, state):
        @pl.when(pl.program_id(0) == 0)
        def _():
            state[...] = jnp.zeros_like(state)

        z_ref[...] = _dot(x_ref[...], w_ref[...], NT)
        lb_all = _lower_bound(lbl_ref)
        row = lax.broadcasted_iota(jnp.int32, (HG_CHUNK, HG_CHUNK), 0)
        col = lax.broadcasted_iota(jnp.int32, (HG_CHUNK, HG_CHUNK), 1)
        tril = row >= col
        tril_b = tril.astype(BF16)
        gain_all = gain_ref[...]

        def chunk(i, carry):
            r = pl.ds(pl.multiple_of(i * HG_CHUNK, HG_CHUNK), HG_CHUNK)
            q, fl, v, hg = (z_ref[r, j * D_MODEL:(j + 1) * D_MODEL] for j in range(4))
            _, _, _, b_last, _, _, _, q_in, k_in, k_out = _chunk_forward(q, fl, v, lb_all, tril_b)
            q_in_b, k_in_b, k_out_b, vb = (u.astype(BF16) for u in (q_in, k_in, k_out, v))
            decay = jnp.exp(b_last)
            sts = [state[h] for h in range(HG_HEADS)]
            attn = [_dot(q_in_b[:, c], k_in_b[:, c], NT) for c in HEAD_COLS]
            inter = [_dot(q_in_b[:, c], sts[h].astype(BF16), NT) for h, c in enumerate(HEAD_COLS)]
            upd = [_dot(vb[:, c], k_out_b[:, c], TN) for c in HEAD_COLS]
            attn = [jnp.where(tril, a, 0.0).astype(BF16) for a in attn]
            outs = [_dot(attn[h], vb[:, c], NN) + inter[h] for h, c in enumerate(HEAD_COLS)]
            for h, c in enumerate(HEAD_COLS):
                st_ref[h, i] = sts[h]
                state[h] = sts[h] * decay[:, c] + upd[h]
            o = jnp.concatenate(outs, axis=1)
            oraw_ref[r, :] = o
            n = o * lax.rsqrt(_head_mean(o * o) + RMS_EPS)
            oa_ref[r, :] = (n * gain_all * (hg * jax.nn.sigmoid(hg))).astype(BF16)
            return carry

        lax.fori_loop(0, ncs, chunk, 0, unroll=True)

    tile = lambda i: (i, 0)
    return _fused_call(
        body, name=name, grid=(s // t,),
        out_shape=(jax.ShapeDtypeStruct((s, W_A), F32), jax.ShapeDtypeStruct((s, D_MODEL), BF16), jax.ShapeDtypeStruct((s, D_MODEL), F32),
                   jax.ShapeDtypeStruct((HG_HEADS, s // HG_CHUNK, HG_DK, HG_DK), F32)),
        in_specs=[pl.BlockSpec((t, D_MODEL), tile), _resident_rows(wi_t, 0, W_A), _resident((2, D_MODEL)), _resident((1, D_MODEL))],
        out_specs=(pl.BlockSpec((t, W_A), tile), pl.BlockSpec((t, D_MODEL), tile), pl.BlockSpec((t, D_MODEL), tile),
                   pl.BlockSpec((HG_HEADS, ncs, HG_DK, HG_DK), lambda i: (0, i, 0, 0))),
        scratch_shapes=[pltpu.VMEM((HG_HEADS, HG_DK, HG_DK), F32)],
        operands=[xb, wi_t, lb_logits, gain], exchanges=exchanges)


def _hgrn_bwd(za, oraw, do_a, states, lb_logits, gain, *, name, exchanges=()):
    s = za.shape[0]
    t = min(256, s)
    ncs = t // HG_CHUNK
    nt = s // t

    def body(z_ref, oraw_ref, do_ref, st_ref, lbl_ref, gain_ref, dz_ref, stats_ref, dstate):
        step = pl.program_id(0)

        @pl.when(step == 0)
        def _():
            dstate[...] = jnp.zeros_like(dstate)
            stats_ref[...] = jnp.zeros_like(stats_ref)

        lb_all = _lower_bound(lbl_ref)
        row = lax.broadcasted_iota(jnp.int32, (HG_CHUNK, HG_CHUNK), 0)
        col = lax.broadcasted_iota(jnp.int32, (HG_CHUNK, HG_CHUNK), 1)
        tril = row >= col
        tril_b = tril.astype(BF16)
        triu_b = (row <= col).astype(BF16)
        gain_all = gain_ref[...]

        def chunk(ii, carry):
            i = ncs - 1 - ii
            r = pl.ds(pl.multiple_of(i * HG_CHUNK, HG_CHUNK), HG_CHUNK)
            q, fl, v, hg = (z_ref[r, j * D_MODEL:(j + 1) * D_MODEL] for j in range(4))
            o = oraw_ref[r, :]
            doa = do_ref[r, :]
            rms = lax.rsqrt(_head_mean(o * o) + RMS_EPS)
            n = o * rms
            sgg = jax.nn.sigmoid(hg)
            silu = hg * sgg
            dhg = doa * n * gain_all * (sgg * (1.0 + hg * (1.0 - sgg)))
            dgain = jnp.sum(doa * n * silu, axis=0, keepdims=True)
            dn = doa * gain_all * silu
            do = rms * (dn - n * _head_mean(dn * n))
            sg, f, k, b_last, eb, enb, eo, q_in, k_in, k_out = _chunk_forward(q, fl, v, lb_all, tril_b)
            q_in_b, k_in_b, k_out_b, vb, dob = (u.astype(BF16) for u in (q_in, k_in, k_out, v, do))
            decay = jnp.exp(b_last)
            sts = [st_ref[h, i] for h in range(HG_HEADS)]
            dsts = [dstate[h] for h in range(HG_HEADS)]
            dsts_b = [d.astype(BF16) for d in dsts]
            heads = list(enumerate(HEAD_COLS))
            attn = [_dot(q_in_b[:, c], k_in_b[:, c], NT) for h, c in heads]
            dattn = [_dot(dob[:, c], vb[:, c], NT) for h, c in heads]
            dq_st = [_dot(dob[:, c], sts[h].astype(BF16), NN) for h, c in heads]
            dk_out = [_dot(vb[:, c], dsts_b[h], NN) for h, c in heads]
            dv_st = [_dot(k_out_b[:, c], dsts_b[h], NT) for h, c in heads]
            dst_o = [_dot(dob[:, c], q_in_b[:, c], TN) for h, c in heads]
            attn = [jnp.where(tril, a, 0.0).astype(BF16) for a in attn]
            dattn = [jnp.where(tril, a, 0.0).astype(BF16) for a in dattn]
            dq_in = jnp.concatenate([_dot(dattn[h], k_in_b[:, c], NN) + dq_st[h] for h, c in heads], axis=1)
            dk_in = jnp.concatenate([_dot(dattn[h], q_in_b[:, c], TN) for h, c in heads], axis=1)
            dv = jnp.concatenate([_dot(attn[h], dob[:, c], TN) + dv_st[h] for h, c in heads], axis=1)
            dk_out = jnp.concatenate(dk_out, axis=1)
            dst_st = jnp.concatenate([jnp.sum(dsts[h] * sts[h], axis=0, keepdims=True) for h in range(HG_HEADS)], axis=1)
            for h, c in heads:
                dstate[h] = dsts[h] * decay[:, c] + dst_o[h]
            db_last = decay * dst_st + jnp.sum(dk_out * k_out, axis=0, keepdims=True)
            db = dq_in * q_in - dk_in * k_in - dk_out * k_out
            dg = _triangle_sum(triu_b, db) + db_last
            dk = dk_in * enb + dk_out * eo
            df = dg / f - dk
            stats_ref[0:1, :] += dgain
            stats_ref[1:2, :] += jnp.sum(df * (1.0 - sg), axis=0, keepdims=True)
            dz_ref[r, 0:1024] = (dq_in * eb).astype(BF16)
            dz_ref[r, 1024:2048] = (df * (1.0 - lb_all) * sg * (1.0 - sg)).astype(BF16)
            dz_ref[r, 2048:3072] = dv.astype(BF16)
            dz_ref[r, 3072:4096] = dhg.astype(BF16)
            return carry

        lax.fori_loop(0, ncs, chunk, 0, unroll=True)

        @pl.when(step == nt - 1)
        def _():
            dl0 = stats_ref[1:2, :] * lb_all * (1.0 - lb_all)
            stats_ref[1:2, :] = dl0
            stats_ref[2:3, :] = -dl0

    rev = lambda i: (nt - 1 - i, 0)
    return _fused_call(
        body, name=name, grid=(nt,),
        out_shape=(jax.ShapeDtypeStruct((s, W_A), BF16), jax.ShapeDtypeStruct((8, D_MODEL), F32)),
        in_specs=[pl.BlockSpec((t, W_A), rev), pl.BlockSpec((t, D_MODEL), rev), pl.BlockSpec((t, D_MODEL), rev),
                  pl.BlockSpec((HG_HEADS, ncs, HG_DK, HG_DK), lambda i: (0, nt - 1 - i, 0, 0)),
                  _resident((2, D_MODEL)), _resident((1, D_MODEL))],
        out_specs=(pl.BlockSpec((t, W_A), rev), pl.BlockSpec((8, D_MODEL), lambda i: (0, 0))),
        scratch_shapes=[pltpu.VMEM((HG_HEADS, HG_DK, HG_DK), F32)],
        operands=[za, oraw, do_a, states, lb_logits, gain], exchanges=exchanges)


def _t5_bucket(n):
    max_exact = NUM_BUCKETS // 2
    nf = jnp.maximum(n, 1).astype(F32)
    large = max_exact + (jnp.log(nf / max_exact) / math.log(MAX_DISTANCE / max_exact) * (NUM_BUCKETS - max_exact)).astype(jnp.int32)
    large = jnp.minimum(large, NUM_BUCKETS - 1)
    return jnp.where(n < max_exact, n, large)


def _bias_selector():
    qi = jnp.arange(SWA_BLOCK)[:, None] + SWA_BLOCK
    kj = jnp.arange(2 * SWA_BLOCK)[None, :]
    dist = qi - kj
    band = ((dist >= 0) & (dist < SWA_WINDOW)).reshape(1, -1)
    bucket = _t5_bucket(jnp.clip(dist, 0, SWA_WINDOW - 1)).reshape(1, -1)
    onehot = ((bucket == jnp.arange(NUM_BUCKETS)[:, None]) & band).astype(F32)
    return onehot, jnp.where(band, 0.0, MASK_VALUE).astype(F32)


def _bias_table(rel_bias_t, onehot, maskrow, *, name):
    def body(rb_ref, oh_ref, mask_ref, o_ref):
        o_ref[...] = _dot(rb_ref[...], oh_ref[...], NN, HIGHEST) + mask_ref[...]

    return pl.pallas_call(body, name=name, out_shape=jax.ShapeDtypeStruct((SWA_HEADS, onehot.shape[1]), F32),
                          compiler_params=_params())(rel_bias_t, onehot, maskrow)


def _bias_grad(dbias2d, onehot, *, name):
    def body(db_ref, oh_ref, o_ref):
        o_ref[...] = _dot(db_ref[...], oh_ref[...], NT, HIGHEST)

    return pl.pallas_call(body, name=name, out_shape=jax.ShapeDtypeStruct((SWA_HEADS, NUM_BUCKETS), F32),
                          compiler_params=_params())(dbias2d, onehot)


def _swa_operands(zq_ref, kv_cur_ref, kv_prev_ref):
    q = (zq_ref[:, 0:1024] * (SWA_HEAD_DIM ** -0.5)).astype(BF16)
    kv_c = kv_cur_ref[...].astype(BF16)
    kv_p = kv_prev_ref[...].astype(BF16)
    kks = [jnp.concatenate([kv_p[:, g * 64:(g + 1) * 64], kv_c[:, g * 64:(g + 1) * 64]], axis=0) for g in range(SWA_KV_HEADS)]
    vvs = [jnp.concatenate([kv_p[:, 128 + g * 64:128 + (g + 1) * 64], kv_c[:, 128 + g * 64:128 + (g + 1) * 64]], axis=0)
           for g in range(SWA_KV_HEADS)]
    return q, kks, vvs


SWA_PART_HEADS = 8
SWA_PARTS = [(h0 // SWA_GROUP, h0) for h0 in range(0, SWA_HEADS, SWA_PART_HEADS)]


def _part_lanes(h0):
    return slice(h0 * SWA_BLOCK, (h0 + SWA_PART_HEADS) * SWA_BLOCK)


def _stack_heads(x, h0):
    return jnp.concatenate([x[:, h * SWA_HEAD_DIM:(h + 1) * SWA_HEAD_DIM] for h in range(h0, h0 + SWA_PART_HEADS)], axis=0)


def _heads_to_lanes(xt):
    pairs = []
    for j in range(0, xt.shape[1] // SWA_BLOCK, 2):
        two = jnp.concatenate([xt[:, j * SWA_BLOCK:(j + 1) * SWA_BLOCK], xt[:, (j + 1) * SWA_BLOCK:(j + 2) * SWA_BLOCK]], axis=0)
        pairs.append(two.T)
    return jnp.concatenate(pairs, axis=1)


def _swa_softmax(score_t, bias_ref, sink_ref, h0):
    sc = score_t + bias_ref[:, _part_lanes(h0)]
    sink = sink_ref[:, _part_lanes(h0)]
    m = jnp.maximum(jnp.max(sc, axis=0, keepdims=True), sink)
    e = jnp.exp(sc - m)
    e_sink = jnp.exp(sink - m)
    return e, 1.0 / (jnp.sum(e, axis=0, keepdims=True) + e_sink), e_sink


def _swa_tables(bias2d, sinks):
    bias_t = bias2d.reshape(SWA_HEADS, SWA_BLOCK, 2 * SWA_BLOCK).transpose(2, 0, 1).reshape(2 * SWA_BLOCK, SWA_HEADS * SWA_BLOCK)
    first = jnp.where(jnp.arange(2 * SWA_BLOCK)[:, None] < SWA_BLOCK, MASK_VALUE, bias_t)
    return jnp.stack([first, bias_t]), jnp.repeat(sinks, SWA_BLOCK, axis=1)


def _swa_fwd(zb, bias_tables, sink_lanes, *, name, exchanges=()):
    s = zb.shape[0]
    nb = s // SWA_BLOCK

    def body(zq_ref, kvc_ref, kvp_ref, bias_ref, sink_ref, o_ref):
        q, kks, vvs = _swa_operands(zq_ref, kvc_ref, kvp_ref)
        scores = [_dot(kks[g], _stack_heads(q, h0), NT) for g, h0 in SWA_PARTS]
        probs = []
        for score, (_, h0) in zip(scores, SWA_PARTS):
            e, inv, _ = _swa_softmax(score, bias_ref, sink_ref, h0)
            probs.append((e * inv).astype(BF16))
        outs = [_dot(vvs[g], p, TN) for p, (g, _) in zip(probs, SWA_PARTS)]
        o_ref[...] = jnp.concatenate([_heads_to_lanes(o) for o in outs], axis=1).astype(BF16)

    return _fused_call(
        body, name=name, grid=(nb,), out_shape=jax.ShapeDtypeStruct((s, D_MODEL), BF16),
        in_specs=[pl.BlockSpec((SWA_BLOCK, W_B), lambda n: (n, 0)),
                  pl.BlockSpec((SWA_BLOCK, 256), lambda n: (n, 4)),
                  pl.BlockSpec((SWA_BLOCK, 256), lambda n: (jnp.maximum(n - 1, 0), 4)),
                  pl.BlockSpec((None, 2 * SWA_BLOCK, SWA_HEADS * SWA_BLOCK), lambda n: (jnp.minimum(n, 1), 0, 0)),
                  _resident((1, SWA_HEADS * SWA_BLOCK))],
        out_specs=pl.BlockSpec((SWA_BLOCK, D_MODEL), lambda n: (n, 0)), scratch_shapes=[],
        operands=[zb, zb, zb, bias_tables, sink_lanes], exchanges=exchanges)


def _swa_bwd(zb, do_b, bias_tables, sink_lanes, *, name, exchanges=()):
    s = zb.shape[0]
    nb = s // SWA_BLOCK
    scale = SWA_HEAD_DIM ** -0.5

    def body(zq_ref, kvc_ref, kvp_ref, do_ref, bias_ref, sink_ref, dz_ref, dbias_ref, dsink_ref, carry, dsink_acc):
        step = pl.program_id(0)

        @pl.when(step == 0)
        def _():
            carry[...] = jnp.zeros_like(carry)
            dsink_acc[...] = jnp.zeros_like(dsink_acc)
            dbias_ref[...] = jnp.zeros_like(dbias_ref)

        q, kks, vvs = _swa_operands(zq_ref, kvc_ref, kvp_ref)
        do = do_ref[...].astype(BF16)
        parts = range(len(SWA_PARTS))
        q_rows = [_stack_heads(q, h0) for _, h0 in SWA_PARTS]
        do_rows = [_stack_heads(do, h0) for _, h0 in SWA_PARTS]
        scores = [_dot(kks[g], q_rows[i], NT) for i, (g, _) in enumerate(SWA_PARTS)]
        soft = [_swa_softmax(scores[i], bias_ref, sink_ref, h0) for i, (_, h0) in enumerate(SWA_PARTS)]
        dps = [_dot(vvs[g], do_rows[i], NT) for i, (g, _) in enumerate(SWA_PARTS)]
        ps, dss = [], []
        for i, (_, h0) in enumerate(SWA_PARTS):
            e, inv, e_sink = soft[i]
            p = e * inv
            delta = jnp.sum(p * dps[i], axis=0, keepdims=True)
            ds = p * (dps[i] - delta)
            dbias_ref[:, _part_lanes(h0)] += ds
            dsink_acc[:, _part_lanes(h0)] -= e_sink * inv * delta
            ps.append(p.astype(BF16))
            dss.append(ds.astype(BF16))
        dqs = [_dot(kks[g], dss[i], TN) * scale for i, (g, _) in enumerate(SWA_PARTS)]
        in_group = lambda xs, g, axis: jnp.concatenate([xs[i] for i in parts if SWA_PARTS[i][0] == g], axis=axis)
        dkks = [_dot(in_group(dss, g, 1), in_group(q_rows, g, 0), NN) for g in range(SWA_KV_HEADS)]
        dvvs = [_dot(in_group(ps, g, 1), in_group(do_rows, g, 0), NN) for g in range(SWA_KV_HEADS)]
        dkv = jnp.concatenate(dkks + dvvs, axis=1)
        dz_ref[:, 0:1024] = jnp.concatenate([_heads_to_lanes(dq) for dq in dqs], axis=1).astype(BF16)
        dz_ref[:, 1024:1280] = (dkv[SWA_BLOCK:, :] + carry[...]).astype(BF16)
        carry[...] = dkv[:SWA_BLOCK, :]

        @pl.when(step == nb - 1)
        def _():
            acc = dsink_acc[...]
            dsink_ref[...] = jnp.concatenate([jnp.sum(acc[:, h * SWA_BLOCK:(h + 1) * SWA_BLOCK], axis=1, keepdims=True)
                                              for h in range(SWA_HEADS)], axis=1)

    rev = lambda i: (nb - 1 - i, 0)
    table_shape = (2 * SWA_BLOCK, SWA_HEADS * SWA_BLOCK)
    return _fused_call(
        body, name=name, grid=(nb,),
        out_shape=(jax.ShapeDtypeStruct((s, W_B), BF16), jax.ShapeDtypeStruct(table_shape, F32), jax.ShapeDtypeStruct((1, SWA_HEADS), F32)),
        in_specs=[pl.BlockSpec((SWA_BLOCK, W_B), rev),
                  pl.BlockSpec((SWA_BLOCK, 256), lambda i: (nb - 1 - i, 4)),
                  pl.BlockSpec((SWA_BLOCK, 256), lambda i: (jnp.maximum(nb - 2 - i, 0), 4)),
                  pl.BlockSpec((SWA_BLOCK, D_MODEL), rev),
                  pl.BlockSpec((None,) + table_shape, lambda i: (jnp.minimum(nb - 1 - i, 1), 0, 0)),
                  _resident((1, SWA_HEADS * SWA_BLOCK))],
        out_specs=(pl.BlockSpec((SWA_BLOCK, W_B), rev), pl.BlockSpec(table_shape, lambda i: (0, 0)),
                   pl.BlockSpec((1, SWA_HEADS), lambda i: (0, 0))),
        scratch_shapes=[pltpu.VMEM((SWA_BLOCK, 256), F32), pltpu.VMEM((1, SWA_HEADS * SWA_BLOCK), F32)],
        operands=[zb, zb, zb, do_b, bias_tables, sink_lanes], exchanges=exchanges)


MEM_COLS = [slice(h * MEM_HEAD_DIM, (h + 1) * MEM_HEAD_DIM) for h in range(MEM_HEADS)]
MEM_VCOLS = [slice(D_MODEL + h * MEM_HEAD_DIM, D_MODEL + (h + 1) * MEM_HEAD_DIM) for h in range(MEM_HEADS)]


def _mem_probs(zc_ref, mkv_ref):
    qs = [(zc_ref[:, c] * (MEM_HEAD_DIM ** -0.5)).astype(BF16) for c in MEM_COLS]
    scores = [_dot(qs[h], mkv_ref[:, c], NT) for h, c in enumerate(MEM_COLS)]
    ps = []
    for sc in scores:
        e = jnp.exp(sc - jnp.max(sc, axis=-1, keepdims=True))
        ps.append(e / jnp.sum(e, axis=-1, keepdims=True))
    return qs, ps


def _mem_fwd(xb, wi_t, mkv, *, name):
    s = xb.shape[0]
    t = min(512, s)

    def body(x_ref, w_ref, mkv_ref, zc_ref, o_ref):
        zc_ref[...] = _dot(x_ref[...], w_ref[...], NT).astype(BF16)
        _, ps = _mem_probs(zc_ref, mkv_ref)
        ps = [p.astype(BF16) for p in ps]
        o_ref[...] = jnp.concatenate([_dot(ps[h], mkv_ref[:, vc], NN) for h, vc in enumerate(MEM_VCOLS)], axis=1).astype(BF16)

    row = pl.BlockSpec((t, D_MODEL), lambda i: (i, 0))
    return pl.pallas_call(
        body, name=name, grid=(s // t,), out_shape=(jax.ShapeDtypeStruct((s, D_MODEL), BF16),) * 2,
        in_specs=[row, _resident_rows(wi_t, W_A + W_B, W_C), _resident((MEM_LEN, 2 * D_MODEL))],
        out_specs=(row, row), compiler_params=_params(("parallel",)),
    )(xb, wi_t, mkv)


def _mem_bwd(xb, zc, do_c, mkv, *, name):
    s = zc.shape[0]
    t = min(512, s)
    nt = s // t

    def body(x_ref, zc_ref, do_ref, mkv_ref, dz_ref, dmkv_ref, gwi_ref, acc):
        @pl.when(pl.program_id(0) == 0)
        def _():
            dmkv_ref[...] = jnp.zeros_like(dmkv_ref)
            acc[...] = jnp.zeros_like(acc)

        heads = range(MEM_HEADS)
        qs, ps = _mem_probs(zc_ref, mkv_ref)
        dos = [do_ref[:, c].astype(BF16) for c in MEM_COLS]
        dps = [_dot(dos[h], mkv_ref[:, MEM_VCOLS[h]], NT) for h in heads]
        dss = [(ps[h] * (dps[h] - jnp.sum(ps[h] * dps[h], axis=-1, keepdims=True))).astype(BF16) for h in heads]
        ps = [p.astype(BF16) for p in ps]
        dz = jnp.concatenate([_dot(dss[h], mkv_ref[:, MEM_COLS[h]], NN) * (MEM_HEAD_DIM ** -0.5) for h in heads], axis=1).astype(BF16)
        dz_ref[...] = dz
        dmkv_ref[...] += jnp.concatenate([_dot(dss[h], qs[h], TN) for h in heads] + [_dot(ps[h], dos[h], TN) for h in heads], axis=1)
        acc[...] += _dot(dz, x_ref[...], TN)

        @pl.when(pl.program_id(0) == nt - 1)
        def _():
            pltpu.sync_copy(acc, gwi_ref.at[pl.ds(W_A + W_B, W_C), :])

    row = pl.BlockSpec((t, D_MODEL), lambda i: (i, 0))
    return pl.pallas_call(
        body, name=name, grid=(nt,),
        out_shape=(jax.ShapeDtypeStruct((s, D_MODEL), BF16), jax.ShapeDtypeStruct((MEM_LEN, 2 * D_MODEL), F32),
                   jax.ShapeDtypeStruct((IN_COLS, D_MODEL), F32)),
        in_specs=[row, row, row, _resident((MEM_LEN, 2 * D_MODEL))],
        out_specs=(row, pl.BlockSpec((MEM_LEN, 2 * D_MODEL), lambda i: (0, 0)), HBM),
        scratch_shapes=[pltpu.VMEM((W_C, D_MODEL), F32)],
        compiler_params=_params(("arbitrary",)),
    )(xb, zc, do_c, mkv)


def _normalize(pre):
    mu = jnp.mean(pre, axis=-1, keepdims=True)
    xc = pre - mu
    rstd = lax.rsqrt(jnp.mean(xc * xc, axis=-1, keepdims=True) + LN_EPS)
    return xc * rstd, rstd


def _layer_norm_bwd(dh, xhat, rstd, g):
    dxh = dh * g
    dpre = rstd * (dxh - jnp.mean(dxh, axis=-1, keepdims=True) - xhat * jnp.mean(dxh * xhat, axis=-1, keepdims=True))
    return dpre, jnp.sum(dh * xhat, axis=0, keepdims=True), jnp.sum(dh, axis=0, keepdims=True)


def _merge_fwd(o_a, o_b, o_c, x, wi_t, wbr, wo, *, name):
    s = x.shape[0]
    t = min(256, s)
    row = lambda w: pl.BlockSpec((t, w), lambda i: (i, 0))

    def body(oa_ref, ob_ref, oc_ref, x_ref, wg_ref, wa_ref, wb_ref, wc_ref, wo_ref, zd_ref, xhat_ref, rstd_ref, merged_ref, pa_ref, pb_ref, pc_ref):
        wbr_refs = (wa_ref, wb_ref, wc_ref)
        zd_ref[...] = _dot(x_ref[...].astype(BF16), wg_ref[...], NT)
        merged = jnp.zeros((t, D_MODEL), F32)
        for b, (o_ref, p_ref) in enumerate(((oa_ref, pa_ref), (ob_ref, pb_ref), (oc_ref, pc_ref))):
            p = _dot(o_ref[...], wbr_refs[b][...], NN)
            p_ref[...] = p.astype(BF16)
            merged = merged + jax.nn.sigmoid(zd_ref[:, b * D_MODEL:(b + 1) * D_MODEL]) * p
        merged_b = merged.astype(BF16)
        merged_ref[...] = merged_b
        xhat, rstd = _normalize(ALPHA * x_ref[...] + _dot(merged_b, wo_ref[...], NN))
        xhat_ref[...] = xhat
        rstd_ref[...] = rstd

    act = jax.ShapeDtypeStruct((s, D_MODEL), F32)
    return pl.pallas_call(
        body, name=name, grid=(s // t,),
        out_shape=(jax.ShapeDtypeStruct((s, W_D), F32), act, jax.ShapeDtypeStruct((s, 1), F32)) + (jax.ShapeDtypeStruct((s, D_MODEL), BF16),) * 4,
        in_specs=[row(D_MODEL)] * 4 + [_resident_rows(wi_t, W_A + W_B + W_C, W_D)] + [_resident((D_MODEL, D_MODEL))] * 4,
        out_specs=(row(W_D), row(D_MODEL), row(1), row(D_MODEL), row(D_MODEL), row(D_MODEL), row(D_MODEL)),
        compiler_params=_params(("parallel",)),
    )(o_a, o_b, o_c, x, wi_t, *wbr, wo)


def _merge_bwd(dpre1, zd, pa, pb, pc, o_a, o_b, o_c, merged, wbr, wo, *, name, exchanges=()):
    s = dpre1.shape[0]
    t = min(256, s)
    nt = s // t
    row = lambda w: pl.BlockSpec((t, w), lambda i: (i, 0))

    def body(dpre_ref, zd_ref, pa_ref, pb_ref, pc_ref, oa_ref, ob_ref, oc_ref, mg_ref, wa_ref, wb_ref, wc_ref, wo_ref,
             dzd_ref, doa_ref, dob_ref, doc_ref, gwa_ref, gwb_ref, gwc_ref, gwo_ref, acc):
        step = pl.program_id(0)

        @pl.when(step == 0)
        def _():
            acc[...] = jnp.zeros_like(acc)

        dpre_b = dpre_ref[...].astype(BF16)
        dmerged = _dot(dpre_b, wo_ref[...], NT)
        acc[3] += _dot(mg_ref[...], dpre_b, TN)
        branches = ((pa_ref, oa_ref, doa_ref), (pb_ref, ob_ref, dob_ref), (pc_ref, oc_ref, doc_ref))
        for b, (p_ref, o_ref, do_ref) in enumerate(branches):
            gate = jax.nn.sigmoid(zd_ref[:, b * D_MODEL:(b + 1) * D_MODEL])
            dzd_ref[:, b * D_MODEL:(b + 1) * D_MODEL] = (dmerged * p_ref[...] * gate * (1.0 - gate)).astype(BF16)
            dp = (dmerged * gate).astype(BF16)
            acc[b] += _dot(o_ref[...], dp, TN)
            do_ref[...] = _dot(dp, (wa_ref, wb_ref, wc_ref)[b][...], NT).astype(do_ref.dtype)

        @pl.when(step == nt - 1)
        def _():
            for b, gw_ref in enumerate((gwa_ref, gwb_ref, gwc_ref, gwo_ref)):
                pltpu.sync_copy(acc.at[b], gw_ref)

    act = jax.ShapeDtypeStruct((s, D_MODEL), F32)
    actb = jax.ShapeDtypeStruct((s, D_MODEL), BF16)
    gw = jax.ShapeDtypeStruct((D_MODEL, D_MODEL), F32)
    return _fused_call(
        body, name=name, grid=(nt,),
        out_shape=(jax.ShapeDtypeStruct((s, W_D), BF16), act, actb, actb, gw, gw, gw, gw),
        in_specs=[row(D_MODEL), row(W_D)] + [row(D_MODEL)] * 7 + [_resident((D_MODEL, D_MODEL))] * 4,
        out_specs=(row(W_D),) + (row(D_MODEL),) * 3 + (HBM,) * 4, scratch_shapes=[pltpu.VMEM((4, D_MODEL, D_MODEL), F32)],
        operands=[dpre1, zd, pa, pb, pc, o_a, o_b, o_c, merged, *wbr, wo], exchanges=exchanges)


def _mlp_loss(xhat1, rstd1, target, ln1_g, ln1_b, ln2_g, ln2_b, wu, wd, *, name):
    s = xhat1.shape[0]
    t = min(256, s)
    npan = wu.shape[0]
    row = lambda w: pl.BlockSpec((t, w), lambda i: (i, 0))
    vec = _resident((1, D_MODEL))

    def body(xhat_ref, rstd_ref, tgt_ref, g1_ref, b1_ref, g2_ref, b2_ref, wu_ref, wd_ref,
             dpre1_ref, dpre2_ref, h1_ref, a_ref, du_ref, stats_ref):
        @pl.when(pl.program_id(0) == 0)
        def _():
            stats_ref[...] = jnp.zeros_like(stats_ref)

        xhat1_v = xhat_ref[...]
        h1 = xhat1_v * g1_ref[...] + b1_ref[...]
        h1_b = h1.astype(BF16)
        h1_ref[...] = h1_b
        us = []
        ff = jnp.zeros((t, D_MODEL), F32)
        for j in range(npan):
            u = _dot(h1_b, wu_ref[j], NN)
            us.append(u)
            r = jnp.maximum(u, 0.0)
            a_b = (r * r).astype(BF16)
            a_ref[:, j * D_MODEL:(j + 1) * D_MODEL] = a_b
            ff = ff + _dot(a_b, wd_ref[j], NN)
        xhat2, rstd2 = _normalize(ALPHA * h1 + ff)
        err = xhat2 * g2_ref[...] + b2_ref[...] - tgt_ref[...]
        stats_ref[4:5, :] += jnp.sum(err * err, axis=0, keepdims=True)
        dpre2, dg2, db2 = _layer_norm_bwd(err * (1.0 / D_MODEL), xhat2, rstd2, g2_ref[...])
        stats_ref[0:1, :] += dg2
        stats_ref[1:2, :] += db2
        dpre2_b = dpre2.astype(BF16)
        dpre2_ref[...] = dpre2_b
        dh1 = ALPHA * dpre2
        for j in range(npan):
            du_b = (_dot(dpre2_b, wd_ref[j], NT) * (2.0 * jnp.maximum(us[j], 0.0))).astype(BF16)
            du_ref[:, j * D_MODEL:(j + 1) * D_MODEL] = du_b
            dh1 = dh1 + _dot(du_b, wu_ref[j], NT)
        dpre1, dg1, db1 = _layer_norm_bwd(dh1, xhat1_v, rstd_ref[...], g1_ref[...])
        stats_ref[2:3, :] += dg1
        stats_ref[3:4, :] += db1
        dpre1_ref[...] = dpre1

    actb = jax.ShapeDtypeStruct((s, D_MODEL), BF16)
    wide = jax.ShapeDtypeStruct((s, D_FF), BF16)
    return pl.pallas_call(
        body, name=name, grid=(s // t,),
        out_shape=(jax.ShapeDtypeStruct((s, D_MODEL), F32), actb, actb, wide, wide, jax.ShapeDtypeStruct((8, D_MODEL), F32)),
        in_specs=[row(D_MODEL), row(1), row(D_MODEL), vec, vec, vec, vec,
                  _resident((npan, D_MODEL, D_MODEL)), _resident((npan, D_MODEL, D_MODEL))],
        out_specs=(row(D_MODEL), row(D_MODEL), row(D_MODEL), row(D_FF), row(D_FF), pl.BlockSpec((8, D_MODEL), lambda i: (0, 0))),
        compiler_params=_params(("arbitrary",)),
    )(xhat1, rstd1, target, ln1_g, ln1_b, ln2_g, ln2_b, wu, wd)


BRANCH_WEIGHTS = ("w_branch_hg", "w_branch_swa", "w_branch_mem")


def _local_step(x, xb, mem, target, wi_t, late, lb_logits, gain, sinks, rel_bias, ln1_g, ln1_b, ln2_g, ln2_b, *, distributed):
    s = x.shape[0]
    tm = min(1024, s)
    tk = min(2048, s)
    memb = mem.astype(BF16)
    if distributed:
        cx, cy, cc = lax.axis_index("x"), lax.axis_index("y"), lax.axis_index("c")
        pos = jnp.stack([2 * cx + cy, cc]).astype(jnp.int32)
    gather = (lambda names: [_gather_exchange([late[k] for k in names])]) if distributed else (lambda names: [])
    to_sibling = (lambda grads: [_sibling_halves_exchange(grads)]) if distributed else (lambda grads: [])
    to_chips = (lambda sums: [_chip_partials_exchange([bf for bf, _ in sums])]) if distributed else (lambda sums: [])

    def chip_sums(names, grads, from_sibling):
        return [_add_sibling(g, o, pos, name="add_sibling_" + k) for k, g, o in zip(names, grads, from_sibling)]

    def shard_sums(names, sums, from_chips):
        return {k: _add_chips(mine, o, pos, name="add_chips_" + k) for k, (_, mine), o in zip(names, sums, from_chips)}

    zb = _mm(xb, wi_t, mode="nt", tm=tm, tn=W_B, tk=D_MODEL, name="proj_b", out_dtype=BF16, b_rows=(W_A, W_B))
    onehot, maskrow = _bias_selector()
    bias_tables, sink_lanes = _swa_tables(_bias_table(rel_bias.T, onehot, maskrow, name="bias_table"), sinks)
    (za, o_a, o_raw, states), landed = _hgrn_fwd(xb, wi_t, lb_logits, gain, name="hgrn_fwd", exchanges=gather(("w_up", "w_down", "w_mem_kv")))
    wu, wd, wmkv = landed[0] if distributed else (late["wu"], late["wd"], late["wmkv"])
    mkv = _mm(memb, wmkv, mode="nn", tm=MEM_LEN, tn=512, tk=D_MODEL, name="mem_kv", out_dtype=BF16, b_panels=True)
    o_b, landed = _swa_fwd(zb, bias_tables, sink_lanes, name="swa_fwd", exchanges=gather(BRANCH_WEIGHTS + ("w_out",)))
    if distributed:
        wbr = [wb.reshape(D_MODEL, D_MODEL) for wb in landed[0][:3]]
        wo = landed[0][3].reshape(D_MODEL, D_MODEL)
    else:
        wbr, wo = [late["wbr"][b] for b in range(3)], late["wo"]
    zc, o_c = _mem_fwd(xb, wi_t, mkv, name="mem_fwd")
    zd, xhat1, rstd1, merged, pa, pb, pc = _merge_fwd(o_a, o_b, o_c, x, wi_t, wbr, wo, name="merge_fwd")

    dpre1, dpre2, h1, act, du, ln_stats = _mlp_loss(xhat1, rstd1, target, ln1_g, ln1_b, ln2_g, ln2_b, wu, wd, name="mlp_loss")
    ffn = ("w_down", "w_up")
    g_ffn = [_mm(act, dpre2, mode="tn", tm=1024, tn=D_MODEL, tk=tk, name="grad_w_down").reshape(N_SHARDS, D_FF // N_SHARDS, D_MODEL),
             _mm(h1, du, mode="tn", tm=D_MODEL, tn=1024, tk=tk, name="grad_w_up", out_panels=True)]

    (dzd, do_a, do_b, do_c, *g_merge), landed = _merge_bwd(dpre1, zd, pa, pb, pc, o_a, o_b, o_c, merged, wbr, wo, name="merge_bwd",
                                                           exchanges=to_sibling(g_ffn))
    sums_ffn = chip_sums(ffn, g_ffn, landed[0]) if distributed else []
    dzc, dmkv, g_wi = _mem_bwd(xb, zc, do_c, mkv, name="mem_bwd")
    merge = BRANCH_WEIGHTS + ("w_out", "w_mem_kv")
    g_merge = [g.reshape(N_SHARDS, D_MODEL // N_SHARDS, D_MODEL) for g in g_merge]
    g_merge.append(_mm(memb, dmkv, mode="tn", tm=D_MODEL, tn=512, tk=MEM_LEN, name="grad_w_mem_kv", out_panels=True))
    (dza, hg_stats), landed = _hgrn_bwd(za, o_raw, do_a, states, lb_logits, gain, name="hgrn_bwd",
                                        exchanges=to_chips(sums_ffn) + to_sibling(g_merge))
    halves = shard_sums(ffn, sums_ffn, landed[0]) if distributed else {}
    sums_merge = chip_sums(merge, g_merge, landed[1]) if distributed else []
    (dzb, dbias_t, dsinks), landed = _swa_bwd(zb, do_b, bias_tables, sink_lanes, name="swa_bwd", exchanges=to_chips(sums_merge))
    if distributed:
        halves.update(shard_sums(merge, sums_merge, landed[0]))
    dbias = dbias_t.reshape(2 * SWA_BLOCK, SWA_HEADS, SWA_BLOCK).transpose(1, 2, 0).reshape(SWA_HEADS, -1)
    d_rel_bias = _bias_grad(dbias, onehot, name="bias_grad").T

    proj = ("w_in",)
    for dz, offset, nm in ((dza, 0, "grad_w_in_a"), (dzb, W_A, "grad_w_in_b"), (dzd, W_A + W_B + W_C, "grad_w_in_d")):
        g_wi = _mm(dz, xb, mode="tn", tm=dz.shape[1] if dz.shape[1] <= 1280 else 1024, tn=D_MODEL, tk=tk, name=nm,
                   rows_of=IN_COLS, row_offset=offset, into=g_wi)
    g_proj = [g_wi.reshape(N_SHARDS, IN_COLS // N_SHARDS, D_MODEL)]
    small = dict(lb_logits=hg_stats[1:3], hg_norm_gain=hg_stats[0:1], swa_sinks=dsinks, rel_bias=d_rel_bias,
                 ln1_g=ln_stats[2:3], ln1_b=ln_stats[3:4], ln2_g=ln_stats[0:1], ln2_b=ln_stats[1:2], sq_err=ln_stats[4:5])
    small_exchange = [_small_gather_exchange(_pack_small(small, name="pack_small"))] if distributed else []
    tx = min(512, s // 2)
    head = max(1, 3 * (s // tx) // 16)
    dx = functools.partial(_dx_matmul, [dza, dzb, dzc, dzd], wi_t, dpre1, tm=tx)
    grad_x_head, landed = dx(tiles=(0, head), name="grad_x_head", exchanges=to_sibling(g_proj))
    sums_proj = chip_sums(proj, g_proj, landed[0]) if distributed else []
    grad_x_tail, landed = dx(tiles=(head, s // tx - head), name="grad_x", exchanges=to_chips(sums_proj) + small_exchange)
    grad_x = jnp.concatenate([grad_x_head, grad_x_tail])
    if distributed:
        halves.update(shard_sums(proj, sums_proj, landed[0]))
        small = landed[1][0]
    else:
        halves = dict(zip(ffn + merge + proj, g_ffn + g_merge + g_proj))
    return grad_x, halves, small


def _mesh_position():
    x, y, c = lax.axis_index("x"), lax.axis_index("y"), lax.axis_index("c")
    chips = [(1 - x, y), (x, 1 - y), (1 - x, 1 - y)]
    return x, y, c, chips


class _Exchange(NamedTuple):
    operands: list
    out_shapes: list
    n_sems: int
    start: Callable
    finish: Callable
    halfway: Optional[Callable] = None


def _gather_exchange(shards):
    n = len(shards)
    per = 9
    assert all(w.shape[0] % (4 * BF16_SUBLANES) == 0 for w in shards)

    def plan(ins, outs, send_sems, recv_sems):
        x, y, c, (x_nbr, y_nbr, diag) = _mesh_position()
        sibling = (x, y, 1 - c)
        slot = lambda chip: 2 * chip[0] + chip[1]

        def rows(a, chip, hc, quarter=None):
            rh = shards[a].shape[0] // 2
            if quarter is None:
                return outs[a].at[slot(chip), pl.ds(hc * rh, rh), :]
            return outs[a].at[slot(chip), pl.ds(hc * rh + quarter * (rh // 2), rh // 2), :]

        def copy(a, k, src, dst, to):
            return pltpu.make_async_remote_copy(src_ref=src, dst_ref=dst, send_sem=send_sems.at[a * per + k], recv_sem=recv_sems.at[a * per + k],
                                                device_id=to, device_id_type=MESH)

        first, from_sibling = [], []
        landed, then = [[] for _ in range(4)], [[] for _ in range(4)]
        for a in range(n):
            rh = shards[a].shape[0] // 2
            my_half = ins[a].at[pl.ds(c * rh, rh), :]
            first += [copy(a, 4, ins[a], outs[a].at[slot((x, y))], sibling),
                      copy(a, 0, my_half, rows(a, (x, y), c), (*x_nbr, c)), copy(a, 1, my_half, rows(a, (x, y), c), (*y_nbr, c))]
            landed[0].append(copy(a, 0, rows(a, x_nbr, c), rows(a, x_nbr, c), (*x_nbr, c)))
            then[0].append([copy(a, 2, rows(a, x_nbr, c, 0), rows(a, x_nbr, c, 0), (*y_nbr, c)), copy(a, 5, rows(a, x_nbr, c), rows(a, x_nbr, c), sibling)])
            landed[1].append(copy(a, 1, rows(a, y_nbr, c), rows(a, y_nbr, c), (*y_nbr, c)))
            then[1].append([copy(a, 3, rows(a, y_nbr, c, 1), rows(a, y_nbr, c, 1), (*x_nbr, c)), copy(a, 6, rows(a, y_nbr, c), rows(a, y_nbr, c), sibling)])
            landed[2].append(copy(a, 2, rows(a, diag, c, 0), rows(a, diag, c, 0), (*y_nbr, c)))
            then[2].append([copy(a, 7, rows(a, diag, c, 0), rows(a, diag, c, 0), sibling)])
            landed[3].append(copy(a, 3, rows(a, diag, c, 1), rows(a, diag, c, 1), (*x_nbr, c)))
            then[3].append([copy(a, 8, rows(a, diag, c, 1), rows(a, diag, c, 1), sibling)])
            from_sibling += [copy(a, 4, outs[a].at[slot((x, y))], outs[a].at[slot((x, y))], sibling),
                             copy(a, 5, rows(a, x_nbr, 1 - c), rows(a, x_nbr, 1 - c), sibling), copy(a, 6, rows(a, y_nbr, 1 - c), rows(a, y_nbr, 1 - c), sibling),
                             copy(a, 7, rows(a, diag, 1 - c, 0), rows(a, diag, 1 - c, 0), sibling), copy(a, 8, rows(a, diag, 1 - c, 1), rows(a, diag, 1 - c, 1), sibling)]
        return first, landed, then, from_sibling

    def start(*refs):
        first, _, _, _ = plan(*refs)
        for cp in first:
            cp.start()

    def stages(landed, then, which):
        for stage in which:
            for arrival, onward in zip(landed[stage], then[stage]):
                arrival.wait_recv()
                for cp in onward:
                    cp.start()

    def halfway(*refs):
        _, landed, then, _ = plan(*refs)
        stages(landed, then, (0, 1))

    def finish(*refs):
        first, landed, then, from_sibling = plan(*refs)
        stages(landed, then, (2, 3))
        for cp in from_sibling:
            cp.wait_recv()
        for cp in first + [cp for stage in then for onward in stage for cp in onward]:
            cp.wait_send()

    return _Exchange(list(shards), [jax.ShapeDtypeStruct((N_SHARDS,) + w.shape, w.dtype) for w in shards], per * n, start, finish, halfway)


def _sibling_halves_exchange(grads):
    n = len(grads)

    def plan(ins, outs, send_sems, recv_sems):
        x, y, c, _ = _mesh_position()
        return [pltpu.make_async_remote_copy(src_ref=ins[a].at[:, pl.ds((1 - c) * (grads[a].shape[1] // 2), grads[a].shape[1] // 2), :],
                                             dst_ref=outs[a], send_sem=send_sems.at[a], recv_sem=recv_sems.at[a],
                                             device_id=(x, y, 1 - c), device_id_type=MESH) for a in range(n)]

    def start(*refs):
        for cp in plan(*refs):
            cp.start()

    def finish(*refs):
        for cp in plan(*refs):
            cp.wait()

    return _Exchange(list(grads), [jax.ShapeDtypeStruct((g.shape[0], g.shape[1] // 2, g.shape[2]), g.dtype) for g in grads], n, start, finish)


def _chip_partials_exchange(sums):
    n = len(sums)

    def plan(ins, outs, send_sems, recv_sems):
        _, _, c, chips = _mesh_position()
        return [pltpu.make_async_remote_copy(src_ref=ins[a].at[2 * cx + cy], dst_ref=outs[a].at[k], send_sem=send_sems.at[a * 3 + k],
                                             recv_sem=recv_sems.at[a * 3 + k], device_id=(cx, cy, c), device_id_type=MESH)
                for k, (cx, cy) in enumerate(chips) for a in range(n)]

    def start(*refs):
        for cp in plan(*refs):
            cp.start()

    def finish(*refs):
        for cp in plan(*refs):
            cp.wait()

    return _Exchange(list(sums), [jax.ShapeDtypeStruct((3,) + g.shape[1:], g.dtype) for g in sums], 3 * n, start, finish)


def _fused_call(body, *, name, grid, in_specs, out_specs, out_shape, scratch_shapes, operands, exchanges=()):
    single = not isinstance(out_shape, (tuple, list))
    out_specs = [out_specs] if single else list(out_specs)
    out_shape = [out_shape] if single else list(out_shape)
    n_in, n_out, n_scr = len(in_specs), len(out_specs), len(scratch_shapes)
    x_in = [len(e.operands) for e in exchanges]
    x_out = [len(e.out_shapes) for e in exchanges]

    def wrapped(*refs):
        refs = list(refs)
        ins = refs[:n_in]
        pos = n_in
        ex_ins = []
        for k in x_in:
            ex_ins.append(refs[pos:pos + k])
            pos += k
        outs = refs[pos:pos + n_out]
        pos += n_out
        ex_outs = []
        for k in x_out:
            ex_outs.append(refs[pos:pos + k])
            pos += k
        scratch = refs[pos:pos + n_scr]
        sems = refs[pos + n_scr:]
        first, last, middle = None, None, None
        for axis, size in enumerate(grid):
            at_start, at_end, at_middle = pl.program_id(axis) == 0, pl.program_id(axis) == size - 1, pl.program_id(axis) == size // 2
            first = at_start if first is None else first & at_start
            last = at_end if last is None else last & at_end
            middle = at_middle if middle is None else middle & at_middle

        @pl.when(first)
        def _():
            for i, e in enumerate(exchanges):
                e.start(ex_ins[i], ex_outs[i], sems[2 * i], sems[2 * i + 1])

        if any(e.halfway for e in exchanges):
            @pl.when(middle)
            def _():
                for i, e in enumerate(exchanges):
                    if e.halfway:
                        e.halfway(ex_ins[i], ex_outs[i], sems[2 * i], sems[2 * i + 1])

        body(*ins, *outs, *scratch)

        @pl.when(last)
        def _():
            for i, e in enumerate(exchanges):
                e.finish(ex_ins[i], ex_outs[i], sems[2 * i], sems[2 * i + 1])

    n_x_in, n_x_out = sum(x_in), sum(x_out)
    results = pl.pallas_call(
        wrapped if exchanges else body, name=name, grid=grid,
        in_specs=list(in_specs) + [HBM] * n_x_in,
        out_specs=out_specs + [HBM] * n_x_out,
        out_shape=out_shape + [s for e in exchanges for s in e.out_shapes],
        scratch_shapes=list(scratch_shapes) + [pltpu.SemaphoreType.DMA((e.n_sems,)) for e in exchanges for _ in range(2)],
        compiler_params=_params(("arbitrary",) * len(grid)),
    )(*operands, *[a for e in exchanges for a in e.operands])
    own = results[0] if single else tuple(results[:n_out])
    landed, pos = [], n_out
    for k in x_out:
        landed.append(list(results[pos:pos + k]))
        pos += k
    return own, landed


def _cast_bf16(x, *, name, exchanges=()):
    s, cols = x.shape
    t = min(512, s)

    def body(x_ref, o_ref):
        o_ref[...] = x_ref[...].astype(BF16)

    tile = pl.BlockSpec((t, cols), lambda i: (i, 0))
    return _fused_call(body, name=name, grid=(s // t,), in_specs=[tile], out_specs=tile, out_shape=jax.ShapeDtypeStruct((s, cols), BF16),
                       scratch_shapes=[], operands=[x], exchanges=exchanges)


ROW_TILE_MAX = 640
BF16_SUBLANES = 16


def _row_tile(rows):
    for tr in range(min(rows, ROW_TILE_MAX), 0, -1):
        if rows % tr == 0 and tr % BF16_SUBLANES == 0:
            return tr
    raise ValueError(rows)


def _add_sibling(grad, other, pos, *, name):
    p, r, cols = grad.shape
    rh = r // 2
    tr = _row_tile(rh)
    nb = rh // tr

    def body(pos_ref, g_ref, o_ref, sb_ref, mine_ref):
        total = g_ref[...] + o_ref[...]
        sb_ref[...] = total.astype(BF16)

        @pl.when(pl.program_id(1) == pos_ref[0])
        def _():
            mine_ref[...] = total

    return pl.pallas_call(
        body, name=name, out_shape=(jax.ShapeDtypeStruct((p, rh, cols), BF16), jax.ShapeDtypeStruct((rh, cols), F32)),
        grid_spec=pltpu.PrefetchScalarGridSpec(
            num_scalar_prefetch=1, grid=(nb, p),
            in_specs=[pl.BlockSpec((None, tr, cols), lambda i, j, pos_ref: (j, pos_ref[1] * nb + i, 0)),
                      pl.BlockSpec((None, tr, cols), lambda i, j, pos_ref: (j, i, 0))],
            out_specs=(pl.BlockSpec((None, tr, cols), lambda i, j, pos_ref: (j, i, 0)),
                       pl.BlockSpec((tr, cols), lambda i, j, pos_ref: (i, 0)))),
        compiler_params=_params(("parallel", "arbitrary")),
    )(pos, grad, other)


def _add_chips(mine, others, pos, *, name):
    rh, cols = mine.shape
    tr = _row_tile(rh)
    nb = rh // tr

    def body(pos_ref, s_ref, o_ref, r_ref):
        r_ref[...] = ((s_ref[...] + o_ref[0].astype(F32)) + o_ref[1].astype(F32)) + o_ref[2].astype(F32)

    return pl.pallas_call(
        body, name=name, out_shape=jax.ShapeDtypeStruct((2 * rh, cols), F32),
        grid_spec=pltpu.PrefetchScalarGridSpec(
            num_scalar_prefetch=1, grid=(nb,),
            in_specs=[pl.BlockSpec((tr, cols), lambda i, pos_ref: (i, 0)),
                      pl.BlockSpec((3, tr, cols), lambda i, pos_ref: (0, i, 0))],
            out_specs=pl.BlockSpec((tr, cols), lambda i, pos_ref: (pos_ref[1] * nb + i, 0))),
        compiler_params=_params(("parallel",)),
    )(pos, mine, others)


def _join_halves(bufs, *, name):
    n = len(bufs)

    def body(*refs):
        ins, outs = refs[:n], refs[n:2 * n]
        send_sems, recv_sems = refs[2 * n:]
        x, y, c, _ = _mesh_position()

        def copy(a, hc):
            rh = bufs[a].shape[0] // 2
            rows = pl.ds(hc * rh, rh)
            return pltpu.make_async_remote_copy(src_ref=ins[a].at[rows, :], dst_ref=outs[a].at[rows, :], send_sem=send_sems.at[a],
                                                recv_sem=recv_sems.at[a], device_id=(x, y, 1 - c), device_id_type=MESH)

        for a in range(n):
            copy(a, c).start()
        for a in range(n):
            copy(a, c).wait_send()
            copy(a, 1 - c).wait_recv()

    return pl.pallas_call(
        body, name=name, out_shape=[jax.ShapeDtypeStruct(b.shape, b.dtype) for b in bufs],
        in_specs=[HBM] * n, out_specs=[HBM] * n, input_output_aliases={a: a for a in range(n)},
        scratch_shapes=[pltpu.SemaphoreType.DMA((n,)), pltpu.SemaphoreType.DMA((n,))],
    )(*bufs)


SMALL = ["lb_logits", "hg_norm_gain", "swa_sinks", "rel_bias", "ln1_g", "ln1_b", "ln2_g", "ln2_b"]
PACK_ROWS = 48
PACK_AT = dict(lb_logits=(slice(0, 2), slice(0, D_MODEL)), hg_norm_gain=(slice(2, 3), slice(0, D_MODEL)), ln1_g=(slice(3, 4), slice(0, D_MODEL)),
               ln1_b=(slice(4, 5), slice(0, D_MODEL)), ln2_g=(slice(5, 6), slice(0, D_MODEL)), ln2_b=(slice(6, 7), slice(0, D_MODEL)),
               swa_sinks=(slice(7, 8), slice(0, SWA_HEADS)), sq_err=(slice(8, 9), slice(0, D_MODEL)),
               rel_bias=(slice(16, 16 + NUM_BUCKETS), slice(0, SWA_HEADS)))


def _pack_small(grads, *, name):
    names = SMALL + ["sq_err"]

    def body(*refs):
        packed = refs[len(names)]
        packed[...] = jnp.zeros_like(packed)
        for k, g_ref in zip(names, refs):
            packed[PACK_AT[k]] = g_ref[...]

    return pl.pallas_call(body, name=name, out_shape=jax.ShapeDtypeStruct((PACK_ROWS, D_MODEL), F32), compiler_params=_params(),
                          )(*[grads[k] for k in names])


def _small_gather_exchange(packed):
    def plan(ins, outs, send_sems, recv_sems):
        x, y, c, _ = _mesh_position()
        me = 4 * x + 2 * y + c
        own = pltpu.make_async_copy(ins[0], outs[0].at[me], send_sems.at[7])
        remote = []
        for d in range(1, 8):
            dx, dy, dc = (d >> 2) & 1, (d >> 1) & 1, d & 1
            remote.append(pltpu.make_async_remote_copy(src_ref=ins[0], dst_ref=outs[0].at[me], send_sem=send_sems.at[d - 1],
                                                       recv_sem=recv_sems.at[d - 1], device_id=(x ^ dx, y ^ dy, c ^ dc), device_id_type=MESH))
        return own, remote

    def start(*refs):
        own, remote = plan(*refs)
        own.start()
        for cp in remote:
            cp.start()

    def finish(*refs):
        own, remote = plan(*refs)
        for cp in remote:
            cp.wait()
        own.wait()

    return _Exchange([packed], [jax.ShapeDtypeStruct((8,) + packed.shape, packed.dtype)], 8, start, finish)


def _adamw_small(gathered, w, m, v, *, name):
    names = SMALL
    n = len(names)

    def body(*refs):
        gathered_ref = refs[0]
        w_refs, m_refs, v_refs = (dict(zip(names, refs[1 + i * n:1 + (i + 1) * n])) for i in range(3))
        loss_ref = refs[1 + 3 * n]
        go_refs, d_refs, nm_refs, nv_refs = (dict(zip(names, refs[2 + (3 + i) * n:2 + (4 + i) * n])) for i in range(4))
        total_ref = refs[2 + 7 * n]
        total = gathered_ref[0]
        for j in range(1, 8):
            total = total + gathered_ref[j]
        total_ref[...] = total
        loss_ref[...] = (0.5 / D_MODEL) * jnp.sum(total_ref[PACK_AT["sq_err"]], axis=1, keepdims=True)
        for k in names:
            g = total_ref[PACK_AT[k]]
            go_refs[k][...] = g
            d_refs[k][...], nm_refs[k][...], nv_refs[k][...] = _adamw_math(w_refs[k][...], g, m_refs[k][...], v_refs[k][...])

    like = [jax.ShapeDtypeStruct(w[k].shape, F32) for k in names]
    results = pl.pallas_call(body, name=name, out_shape=[jax.ShapeDtypeStruct((1, 1), F32)] + like * 4,
                             scratch_shapes=[pltpu.VMEM((PACK_ROWS, D_MODEL), F32)],
                             compiler_params=_params())(gathered, *[d[k] for d in (w, m, v) for k in names])
    return results[0], {k: tuple(results[1 + i * n + j] for i in range(4)) for j, k in enumerate(names)}


def _adamw_math(w, g, m, v):
    m = ADAM_B1 * m + (1.0 - ADAM_B1) * g
    v = ADAM_B2 * v + (1.0 - ADAM_B2) * (g * g)
    m_hat = m / (1.0 - ADAM_B1 ** ADAM_STEP)
    v_hat = v / (1.0 - ADAM_B2 ** ADAM_STEP)
    delta = -ADAM_LR * (m_hat / (jnp.sqrt(v_hat) + ADAM_EPS) + ADAM_WD * w)
    return delta, m, v


def _adamw(w, g, m, v, *, name):
    _, rows, cols = w.shape
    tr = _row_tile(rows)
    blk = pl.BlockSpec((None, tr, cols), lambda i: (0, i, 0))
    flat = pl.BlockSpec((tr, cols), lambda i: (i, 0))

    def body(w_ref, g_ref, m_ref, v_ref, go_ref, d_ref, nm_ref, nv_ref):
        g_v = g_ref[...]
        go_ref[...] = g_v
        d_ref[...], nm_ref[...], nv_ref[...] = _adamw_math(w_ref[...], g_v, m_ref[...], v_ref[...])

    shape = jax.ShapeDtypeStruct((1, rows, cols), F32)
    return pl.pallas_call(body, name=name, grid=(rows // tr,), out_shape=(shape,) * 4, in_specs=[blk, flat, blk, blk], out_specs=(blk,) * 4,
                          compiler_params=_params(("parallel",)))(w, g, m, v)


WEIGHTS = ["w_in", "lb_logits", "hg_norm_gain", "swa_sinks", "rel_bias", "w_mem_kv", "w_branch_hg", "w_branch_swa", "w_branch_mem",
           "w_out", "ln1_g", "ln1_b", "w_up", "w_down", "ln2_g", "ln2_b"]
BIG = ["w_in", "w_mem_kv", "w_branch_hg", "w_branch_swa", "w_branch_mem", "w_out", "w_up", "w_down"]


def kernel(x, mem, w_in, lb_logits, hg_norm_gain, swa_sinks, rel_bias, w_mem_kv, w_branch_hg, w_branch_swa, w_branch_mem, w_out, ln1_g, ln1_b, w_up, w_down, ln2_g, ln2_b, loss_target, m_w_in, m_lb_logits, m_hg_norm_gain, m_swa_sinks, m_rel_bias, m_w_mem_kv, m_w_branch_hg, m_w_branch_swa, m_w_branch_mem, m_w_out, m_ln1_g, m_ln1_b, m_w_up, m_w_down, m_ln2_g, m_ln2_b, v_w_in, v_lb_logits, v_hg_norm_gain, v_swa_sinks, v_rel_bias, v_w_mem_kv, v_w_branch_hg, v_w_branch_swa, v_w_branch_mem, v_w_out, v_ln1_g, v_ln1_b, v_w_up, v_w_down, v_ln2_g, v_ln2_b):
    w = dict(w_in=w_in, lb_logits=lb_logits, hg_norm_gain=hg_norm_gain, swa_sinks=swa_sinks, rel_bias=rel_bias, w_mem_kv=w_mem_kv,
             w_branch_hg=w_branch_hg, w_branch_swa=w_branch_swa, w_branch_mem=w_branch_mem, w_out=w_out, ln1_g=ln1_g, ln1_b=ln1_b,
             w_up=w_up, w_down=w_down, ln2_g=ln2_g, ln2_b=ln2_b)
    m = dict(w_in=m_w_in, lb_logits=m_lb_logits, hg_norm_gain=m_hg_norm_gain, swa_sinks=m_swa_sinks, rel_bias=m_rel_bias, w_mem_kv=m_w_mem_kv,
             w_branch_hg=m_w_branch_hg, w_branch_swa=m_w_branch_swa, w_branch_mem=m_w_branch_mem, w_out=m_w_out, ln1_g=m_ln1_g, ln1_b=m_ln1_b,
             w_up=m_w_up, w_down=m_w_down, ln2_g=m_ln2_g, ln2_b=m_ln2_b)
    v = dict(w_in=v_w_in, lb_logits=v_lb_logits, hg_norm_gain=v_hg_norm_gain, swa_sinks=v_swa_sinks, rel_bias=v_rel_bias, w_mem_kv=v_w_mem_kv,
             w_branch_hg=v_w_branch_hg, w_branch_swa=v_w_branch_swa, w_branch_mem=v_w_branch_mem, w_out=v_w_out, ln1_g=v_ln1_g, ln1_b=v_ln1_b,
             w_up=v_w_up, w_down=v_w_down, ln2_g=v_ln2_g, ln2_b=v_ln2_b)
    shapes = {k: w[k].shape for k in WEIGHTS}
    for d in (w, m, v):
        d["w_in"] = d["w_in"].reshape(D_MODEL, IN_COLS // N_SHARDS).T[None]
    shards = {k: w[k].reshape(w[k].shape[-2], w[k].shape[-1]).astype(BF16) for k in BIG}
    x2d = x.reshape(x.shape[-2], D_MODEL)
    xb, ((wi4,),) = _cast_bf16(x2d, name="gather_weights", exchanges=[_gather_exchange([shards["w_in"]])])
    wi_t = wi4.reshape(IN_COLS, D_MODEL)

    grad_x, halves, small = _local_step(
        x2d, xb, mem.reshape(MEM_LEN, D_MODEL), loss_target.reshape(loss_target.shape[-2], D_MODEL),
        wi_t, shards, lb_logits, hg_norm_gain, swa_sinks, rel_bias, ln1_g, ln1_b, ln2_g, ln2_b, distributed=True)

    reduced = dict(zip(BIG, _join_halves([halves[k] for k in BIG], name="join_halves")))

    outs = {k: _adamw(w[k], reduced[k], m[k], v[k], name="adamw_" + k) for k in BIG}
    loss, small_outs = _adamw_small(small, w, m, v, name="adamw_small")
    outs.update(small_outs)
    grad_out, delta_out, m_out, v_out = ({k: outs[k][i] for k in WEIGHTS} for i in range(4))
    for out in (grad_out, delta_out, m_out, v_out):
        out["w_in"] = out["w_in"][0].T

    result = [loss.reshape(()), grad_x.reshape(x.shape)]
    for out in (grad_out, delta_out, m_out, v_out):
        result += [out[k].reshape(shapes[k]) for k in WEIGHTS]
    return tuple(result)
```

```python
import functools
import math
from typing import Callable, NamedTuple, Optional

import jax
import jax.numpy as jnp
from jax import lax
from jax.experimental import pallas as pl
from jax.experimental.pallas import tpu as pltpu

F32 = jnp.float32
BF16 = jnp.bfloat16
HIGHEST = lax.Precision.HIGHEST
MESH = pl.DeviceIdType.MESH

D_MODEL = 1024
MEM_LEN = 256
HG_HEADS = 8
HG_DK = 128
HG_CHUNK = 64
SWA_HEADS = 16
SWA_KV_HEADS = 2
SWA_GROUP = 8
SWA_HEAD_DIM = 64
SWA_BLOCK = 128
SWA_WINDOW = 128
MEM_HEADS = 4
MEM_HEAD_DIM = 256
NUM_BUCKETS = 32
MAX_DISTANCE = 128
D_FF = 4096
LN_EPS = 1e-5
RMS_EPS = 1e-6
ALPHA = 2.0 ** 0.25
W_A, W_B, W_C, W_D = 4096, 1280, 1024, 3072
IN_COLS = W_A + W_B + W_C + W_D
N_SHARDS = 4
ADAM_LR = 0.001
ADAM_B1 = 0.9
ADAM_B2 = 0.999
ADAM_EPS = 1e-08
ADAM_WD = 0.01
ADAM_STEP = 10
MASK_VALUE = -1e30
VMEM_LIMIT = 56 * 1024 * 1024

NN = ((1,), (0,))
NT = ((1,), (1,))
TN = ((0,), (0,))
HBM = pl.BlockSpec(memory_space=pltpu.HBM)


def _dot(a, b, dims=NN, precision=None):
    return lax.dot_general(a, b, (dims, ((), ())), precision=precision, preferred_element_type=F32)


def _params(sem=None):
    return pltpu.CompilerParams(dimension_semantics=sem, vmem_limit_bytes=VMEM_LIMIT)


def _resident(shape):
    zeros = (0,) * len(shape)
    return pl.BlockSpec(shape, lambda *_: zeros, pipeline_mode=pl.Buffered(1))


def _resident_rows(arr, offset, rows):
    return pl.BlockSpec((pl.Element(rows), pl.Element(arr.shape[1])), lambda *_: (offset, 0), pipeline_mode=pl.Buffered(1))


def _mm(a, b, *, mode, tm, tn, tk, name, out_dtype=F32, b_panels=False, b_rows=None, out_panels=False, rows_of=None, row_offset=0,
        into=None):
    if mode == "tn":
        kdim, m = a.shape
    else:
        m, kdim = a.shape
    if b_panels:
        n = b.shape[0] * b.shape[2]
        assert b.shape[2] == tn and mode == "nn"
    elif b_rows is not None:
        assert mode == "nt"
        b_offset, n = b_rows
    elif mode == "nt":
        n = b.shape[0]
    else:
        n = b.shape[1]
    assert m % tm == 0 and n % tn == 0 and kdim % tk == 0, (name, m, n, kdim)
    nk = kdim // tk
    dims = {"nn": NN, "nt": NT, "tn": TN}[mode]
    a_spec = pl.BlockSpec((tk, tm), lambda i, j, k: (k, i)) if mode == "tn" else pl.BlockSpec((tm, tk), lambda i, j, k: (i, k))
    if b_panels:
        b_spec = pl.BlockSpec((None, tk, tn), lambda i, j, k: (j, k, 0))
    elif b_rows is not None:
        assert b_offset % BF16_SUBLANES == 0 and tn % BF16_SUBLANES == 0 and tk % 128 == 0
        b_spec = pl.BlockSpec((pl.Element(tn), pl.Element(tk)),
                              lambda i, j, k: (pl.multiple_of(b_offset + j * tn, BF16_SUBLANES), pl.multiple_of(k * tk, 128)))
    elif mode == "nt":
        b_spec = pl.BlockSpec((tn, tk), lambda i, j, k: (j, k))
    else:
        b_spec = pl.BlockSpec((tk, tn), lambda i, j, k: (k, j))
    in_specs = [a_spec, b_spec]
    operands = [a, b]
    aliases = {}
    if out_panels:
        out_shape = jax.ShapeDtypeStruct((n // tn, m, tn), out_dtype)
        o_spec = pl.BlockSpec((None, tm, tn), lambda i, j, k: (j, i, 0))
    elif rows_of is not None:
        out_shape = jax.ShapeDtypeStruct((rows_of, n), out_dtype)
        assert row_offset % BF16_SUBLANES == 0 and tm % BF16_SUBLANES == 0 and tn % 128 == 0
        o_spec = pl.BlockSpec((pl.Element(tm), pl.Element(tn)),
                              lambda i, j, k: (pl.multiple_of(row_offset + i * tm, BF16_SUBLANES), pl.multiple_of(j * tn, 128)))
        if into is not None:
            in_specs.append(pl.BlockSpec(memory_space=pl.ANY))
            operands.append(into)
            aliases = {2: 0}
    else:
        out_shape = jax.ShapeDtypeStruct((m, n), out_dtype)
        o_spec = pl.BlockSpec((tm, tn), lambda i, j, k: (i, j))
    n_in = len(operands)

    def body(*refs):
        a_ref, b_ref, o_ref = refs[0], refs[1], refs[n_in]
        part = _dot(a_ref[...].astype(BF16), b_ref[...].astype(BF16), dims)

        def finish(acc):
            o_ref[...] = acc.astype(out_dtype)

        if nk == 1:
            finish(part)
        else:
            acc_ref = refs[-1]
            k = pl.program_id(2)

            @pl.when(k == 0)
            def _():
                acc_ref[...] = part

            @pl.when(k > 0)
            def _():
                acc_ref[...] += part

            @pl.when(k == nk - 1)
            def _():
                finish(acc_ref[...])

    return pl.pallas_call(
        body, name=name, out_shape=out_shape, grid=(m // tm, n // tn, nk), in_specs=in_specs, out_specs=o_spec,
        scratch_shapes=[pltpu.VMEM((tm, tn), F32)] if nk > 1 else [], input_output_aliases=aliases,
        compiler_params=_params(("parallel", "parallel", "arbitrary")),
    )(*operands)


def _dx_matmul(dzs, wi_t, resid, *, tm, tiles, name, exchanges=()):
    first_tile, count = tiles
    npieces = len(dzs)
    offsets = [sum(dz.shape[1] for dz in dzs[:p]) for p in range(npieces)]
    tile = lambda i: (i + first_tile, 0)
    in_specs = [pl.BlockSpec((tm, dz.shape[1]), tile) for dz in dzs] + [_resident(wi_t.shape), pl.BlockSpec((tm, D_MODEL), tile)]

    def body(*refs):
        dz_refs, w_ref, r_ref, o_ref = refs[:npieces], refs[npieces], refs[npieces + 1], refs[npieces + 2]
        total = ALPHA * r_ref[...]
        for p in range(npieces):
            total = total + _dot(dz_refs[p][...], w_ref[offsets[p]:offsets[p] + dzs[p].shape[1], :], NN)
        o_ref[...] = total

    return _fused_call(
        body, name=name, out_shape=jax.ShapeDtypeStruct((count * tm, D_MODEL), F32), grid=(count,), in_specs=in_specs,
        out_specs=pl.BlockSpec((tm, D_MODEL), lambda i: (i, 0)), scratch_shapes=[], operands=[*dzs, wi_t, resid], exchanges=exchanges)


def _lower_bound(lbl_ref):
    l0, l1 = lbl_ref[0:1, :], lbl_ref[1:2, :]
    mx = jnp.maximum(l0, l1)
    e0, e1 = jnp.exp(l0 - mx), jnp.exp(l1 - mx)
    return e0 / (e0 + e1)


HEAD_COLS = [slice(h * HG_DK, (h + 1) * HG_DK) for h in range(HG_HEADS)]


def _head_mean(x):
    return jnp.concatenate([jnp.broadcast_to(jnp.mean(x[:, c], axis=-1, keepdims=True), (x.shape[0], HG_DK)) for c in HEAD_COLS], axis=1)


def _triangle_sum(tri_b, x):
    p0 = x.astype(BF16)
    r1 = x - p0.astype(F32)
    p1 = r1.astype(BF16)
    p2 = (r1 - p1.astype(F32)).astype(BF16)
    return _dot(tri_b, p0) + _dot(tri_b, p1) + _dot(tri_b, p2)


def _chunk_forward(q, fl, v, lb, tril_b):
    sg = jax.nn.sigmoid(fl)
    f = lb + (1.0 - lb) * sg
    k = 1.0 - f
    b = _triangle_sum(tril_b, jnp.log(f))
    b_last = b[HG_CHUNK - 1:HG_CHUNK, :]
    eb, enb, eo = jnp.exp(b), jnp.exp(-b), jnp.exp(b_last - b)
    return sg, f, k, b_last, eb, enb, eo, q * eb, k * enb, k * eo


def _hgrn_fwd(xb, wi_t, lb_logits, gain, *, name, exchanges=()):
    s = xb.shape[0]
    t = min(256, s)
    ncs = t // HG_CHUNK

    def body(x_ref, w_ref, lbl_ref, gain_ref, z_ref, oa_ref, oraw_ref, st_ref, state):
        @pl.when(pl.program_id(0) == 0)
        def _():
            state[...] = jnp.zeros_like(state)

        z_ref[...] = _dot(x_ref[...], w_ref[...], NT)
        lb_all = _lower_bound(lbl_ref)
        row = lax.broadcasted_iota(jnp.int32, (HG_CHUNK, HG_CHUNK), 0)
        col = lax.broadcasted_iota(jnp.int32, (HG_CHUNK, HG_CHUNK), 1)
        tril = row >= col
        tril_b = tril.astype(BF16)
        gain_all = gain_ref[...]

        def chunk(i, carry):
            r = pl.ds(pl.multiple_of(i * HG_CHUNK, HG_CHUNK), HG_CHUNK)
            q, fl, v, hg = (z_ref[r, j * D_MODEL:(j + 1) * D_MODEL] for j in range(4))
            _, _, _, b_last, _, _, _, q_in, k_in, k_out = _chunk_forward(q, fl, v, lb_all, tril_b)
            q_in_b, k_in_b, k_out_b, vb = (u.astype(BF16) for u in (q_in, k_in, k_out, v))
            decay = jnp.exp(b_last)
            sts = [state[h] for h in range(HG_HEADS)]
            attn = [_dot(q_in_b[:, c], k_in_b[:, c], NT) for c in HEAD_COLS]
            inter = [_dot(q_in_b[:, c], sts[h].astype(BF16), NT) for h, c in enumerate(HEAD_COLS)]
            upd = [_dot(vb[:, c], k_out_b[:, c], TN) for c in HEAD_COLS]
            attn = [jnp.where(tril, a, 0.0).astype(BF16) for a in attn]
            outs = [_dot(attn[h], vb[:, c], NN) + inter[h] for h, c in enumerate(HEAD_COLS)]
            for h, c in enumerate(HEAD_COLS):
                st_ref[h, i] = sts[h]
                state[h] = sts[h] * decay[:, c] + upd[h]
            o = jnp.concatenate(outs, axis=1)
            oraw_ref[r, :] = o
            n = o * lax.rsqrt(_head_mean(o * o) + RMS_EPS)
            oa_ref[r, :] = (n * gain_all * (hg * jax.nn.sigmoid(hg))).astype(BF16)
            return carry

        lax.fori_loop(0, ncs, chunk, 0, unroll=True)

    tile = lambda i: (i, 0)
    return _fused_call(
        body, name=name, grid=(s // t,),
        out_shape=(jax.ShapeDtypeStruct((s, W_A), F32), jax.ShapeDtypeStruct((s, D_MODEL), BF16), jax.ShapeDtypeStruct((s, D_MODEL), F32),
                   jax.ShapeDtypeStruct((HG_HEADS, s // HG_CHUNK, HG_DK, HG_DK), F32)),
        in_specs=[pl.BlockSpec((t, D_MODEL), tile), _resident_rows(wi_t, 0, W_A), _resident((2, D_MODEL)), _resident((1, D_MODEL))],
        out_specs=(pl.BlockSpec((t, W_A), tile), pl.BlockSpec((t, D_MODEL), tile), pl.BlockSpec((t, D_MODEL), tile),
                   pl.BlockSpec((HG_HEADS, ncs, HG_DK, HG_DK), lambda i: (0, i, 0, 0))),
        scratch_shapes=[pltpu.VMEM((HG_HEADS, HG_DK, HG_DK), F32)],
        operands=[xb, wi_t, lb_logits, gain], exchanges=exchanges)


def _hgrn_bwd(za, oraw, do_a, states, lb_logits, gain, *, name, exchanges=()):
    s = za.shape[0]
    t = min(256, s)
    ncs = t // HG_CHUNK
    nt = s // t

    def body(z_ref, oraw_ref, do_ref, st_ref, lbl_ref, gain_ref, dz_ref, stats_ref, dstate):
        step = pl.program_id(0)

        @pl.when(step == 0)
        def _():
            dstate[...] = jnp.zeros_like(dstate)
            stats_ref[...] = jnp.zeros_like(stats_ref)

        lb_all = _lower_bound(lbl_ref)
        row = lax.broadcasted_iota(jnp.int32, (HG_CHUNK, HG_CHUNK), 0)
        col = lax.broadcasted_iota(jnp.int32, (HG_CHUNK, HG_CHUNK), 1)
        tril = row >= col
        tril_b = tril.astype(BF16)
        triu_b = (row <= col).astype(BF16)
        gain_all = gain_ref[...]

        def chunk(ii, carry):
            i = ncs - 1 - ii
            r = pl.ds(pl.multiple_of(i * HG_CHUNK, HG_CHUNK), HG_CHUNK)
            q, fl, v, hg = (z_ref[r, j * D_MODEL:(j + 1) * D_MODEL] for j in range(4))
            o = oraw_ref[r, :]
            doa = do_ref[r, :]
            rms = lax.rsqrt(_head_mean(o * o) + RMS_EPS)
            n = o * rms
            sgg = jax.nn.sigmoid(hg)
            silu = hg * sgg
            dhg = doa * n * gain_all * (sgg * (1.0 + hg * (1.0 - sgg)))
            dgain = jnp.sum(doa * n * silu, axis=0, keepdims=True)
            dn = doa * gain_all * silu
            do = rms * (dn - n * _head_mean(dn * n))
            sg, f, k, b_last, eb, enb, eo, q_in, k_in, k_out = _chunk_forward(q, fl, v, lb_all, tril_b)
            q_in_b, k_in_b, k_out_b, vb, dob = (u.astype(BF16) for u in (q_in, k_in, k_out, v, do))
            decay = jnp.exp(b_last)
            sts = [st_ref[h, i] for h in range(HG_HEADS)]
            dsts = [dstate[h] for h in range(HG_HEADS)]
            dsts_b = [d.astype(BF16) for d in dsts]
            heads = list(enumerate(HEAD_COLS))
            attn = [_dot(q_in_b[:, c], k_in_b[:, c], NT) for h, c in heads]
            dattn = [_dot(dob[:, c], vb[:, c], NT) for h, c in heads]
            dq_st = [_dot(dob[:, c], sts[h].astype(BF16), NN) for h, c in heads]
            dk_out = [_dot(vb[:, c], dsts_b[h], NN) for h, c in heads]
            dv_st = [_dot(k_out_b[:, c], dsts_b[h], NT) for h, c in heads]
            dst_o = [_dot(dob[:, c], q_in_b[:, c], TN) for h, c in heads]
            attn = [jnp.where(tril, a, 0.0).astype(BF16) for a in attn]
            dattn = [jnp.where(tril, a, 0.0).astype(BF16) for a in dattn]
            dq_in = jnp.concatenate([_dot(dattn[h], k_in_b[:, c], NN) + dq_st[h] for h, c in heads], axis=1)
            dk_in = jnp.concatenate([_dot(dattn[h], q_in_b[:, c], TN) for h, c in heads], axis=1)
            dv = jnp.concatenate([_dot(attn[h], dob[:, c], TN) + dv_st[h] for h, c in heads], axis=1)
            dk_out = jnp.concatenate(dk_out, axis=1)
            dst_st = jnp.concatenate([jnp.sum(dsts[h] * sts[h], axis=0, keepdims=True) for h in range(HG_HEADS)], axis=1)
            for h, c in heads:
                dstate[h] = dsts[h] * decay[:, c] + dst_o[h]
            db_last = decay * dst_st + jnp.sum(dk_out * k_out, axis=0, keepdims=True)
            db = dq_in * q_in - dk_in * k_in - dk_out * k_out
            dg = _triangle_sum(triu_b, db) + db_last
            dk = dk_in * enb + dk_out * eo
            df = dg / f - dk
            stats_ref[0:1, :] += dgain
            stats_ref[1:2, :] += jnp.sum(df * (1.0 - sg), axis=0, keepdims=True)
            dz_ref[r, 0:1024] = (dq_in * eb).astype(BF16)
            dz_ref[r, 1024:2048] = (df * (1.0 - lb_all) * sg * (1.0 - sg)).astype(BF16)
            dz_ref[r, 2048:3072] = dv.astype(BF16)
            dz_ref[r, 3072:4096] = dhg.astype(BF16)
            return carry

        lax.fori_loop(0, ncs, chunk, 0, unroll=True)

        @pl.when(step == nt - 1)
        def _():
            dl0 = stats_ref[1:2, :] * lb_all * (1.0 - lb_all)
            stats_ref[1:2, :] = dl0
            stats_ref[2:3, :] = -dl0

    rev = lambda i: (nt - 1 - i, 0)
    return _fused_call(
        body, name=name, grid=(nt,),
        out_shape=(jax.ShapeDtypeStruct((s, W_A), BF16), jax.ShapeDtypeStruct((8, D_MODEL), F32)),
        in_specs=[pl.BlockSpec((t, W_A), rev), pl.BlockSpec((t, D_MODEL), rev), pl.BlockSpec((t, D_MODEL), rev),
                  pl.BlockSpec((HG_HEADS, ncs, HG_DK, HG_DK), lambda i: (0, nt - 1 - i, 0, 0)),
                  _resident((2, D_MODEL)), _resident((1, D_MODEL))],
        out_specs=(pl.BlockSpec((t, W_A), rev), pl.BlockSpec((8, D_MODEL), lambda i: (0, 0))),
        scratch_shapes=[pltpu.VMEM((HG_HEADS, HG_DK, HG_DK), F32)],
        operands=[za, oraw, do_a, states, lb_logits, gain], exchanges=exchanges)


def _t5_bucket(n):
    max_exact = NUM_BUCKETS // 2
    nf = jnp.maximum(n, 1).astype(F32)
    large = max_exact + (jnp.log(nf / max_exact) / math.log(MAX_DISTANCE / max_exact) * (NUM_BUCKETS - max_exact)).astype(jnp.int32)
    large = jnp.minimum(large, NUM_BUCKETS - 1)
    return jnp.where(n < max_exact, n, large)


def _bias_selector():
    qi = jnp.arange(SWA_BLOCK)[:, None] + SWA_BLOCK
    kj = jnp.arange(2 * SWA_BLOCK)[None, :]
    dist = qi - kj
    band = ((dist >= 0) & (dist < SWA_WINDOW)).reshape(1, -1)
    bucket = _t5_bucket(jnp.clip(dist, 0, SWA_WINDOW - 1)).reshape(1, -1)
    onehot = ((bucket == jnp.arange(NUM_BUCKETS)[:, None]) & band).astype(F32)
    return onehot, jnp.where(band, 0.0, MASK_VALUE).astype(F32)


def _bias_table(rel_bias_t, onehot, maskrow, *, name):
    def body(rb_ref, oh_ref, mask_ref, o_ref):
        o_ref[...] = _dot(rb_ref[...], oh_ref[...], NN, HIGHEST) + mask_ref[...]

    return pl.pallas_call(body, name=name, out_shape=jax.ShapeDtypeStruct((SWA_HEADS, onehot.shape[1]), F32),
                          compiler_params=_params())(rel_bias_t, onehot, maskrow)


def _bias_grad(dbias2d, onehot, *, name):
    def body(db_ref, oh_ref, o_ref):
        o_ref[...] = _dot(db_ref[...], oh_ref[...], NT, HIGHEST)

    return pl.pallas_call(body, name=name, out_shape=jax.ShapeDtypeStruct((SWA_HEADS, NUM_BUCKETS), F32),
                          compiler_params=_params())(dbias2d, onehot)


def _swa_operands(zq_ref, kv_cur_ref, kv_prev_ref):
    q = (zq_ref[:, 0:1024] * (SWA_HEAD_DIM ** -0.5)).astype(BF16)
    kv_c = kv_cur_ref[...].astype(BF16)
    kv_p = kv_prev_ref[...].astype(BF16)
    kks = [jnp.concatenate([kv_p[:, g * 64:(g + 1) * 64], kv_c[:, g * 64:(g + 1) * 64]], axis=0) for g in range(SWA_KV_HEADS)]
    vvs = [jnp.concatenate([kv_p[:, 128 + g * 64:128 + (g + 1) * 64], kv_c[:, 128 + g * 64:128 + (g + 1) * 64]], axis=0)
           for g in range(SWA_KV_HEADS)]
    return q, kks, vvs


SWA_PART_HEADS = 8
SWA_PARTS = [(h0 // SWA_GROUP, h0) for h0 in range(0, SWA_HEADS, SWA_PART_HEADS)]


def _part_lanes(h0):
    return slice(h0 * SWA_BLOCK, (h0 + SWA_PART_HEADS) * SWA_BLOCK)


def _stack_heads(x, h0):
    return jnp.concatenate([x[:, h * SWA_HEAD_DIM:(h + 1) * SWA_HEAD_DIM] for h in range(h0, h0 + SWA_PART_HEADS)], axis=0)


def _heads_to_lanes(xt):
    pairs = []
    for j in range(0, xt.shape[1] // SWA_BLOCK, 2):
        two = jnp.concatenate([xt[:, j * SWA_BLOCK:(j + 1) * SWA_BLOCK], xt[:, (j + 1) * SWA_BLOCK:(j + 2) * SWA_BLOCK]], axis=0)
        pairs.append(two.T)
    return jnp.concatenate(pairs, axis=1)


def _swa_softmax(score_t, bias_ref, sink_ref, h0):
    sc = score_t + bias_ref[:, _part_lanes(h0)]
    sink = sink_ref[:, _part_lanes(h0)]
    m = jnp.maximum(jnp.max(sc, axis=0, keepdims=True), sink)
    e = jnp.exp(sc - m)
    e_sink = jnp.exp(sink - m)
    return e, 1.0 / (jnp.sum(e, axis=0, keepdims=True) + e_sink), e_sink


def _swa_tables(bias2d, sinks):
    bias_t = bias2d.reshape(SWA_HEADS, SWA_BLOCK, 2 * SWA_BLOCK).transpose(2, 0, 1).reshape(2 * SWA_BLOCK, SWA_HEADS * SWA_BLOCK)
    first = jnp.where(jnp.arange(2 * SWA_BLOCK)[:, None] < SWA_BLOCK, MASK_VALUE, bias_t)
    return jnp.stack([first, bias_t]), jnp.repeat(sinks, SWA_BLOCK, axis=1)


def _swa_fwd(zb, bias_tables, sink_lanes, *, name, exchanges=()):
    s = zb.shape[0]
    nb = s // SWA_BLOCK

    def body(zq_ref, kvc_ref, kvp_ref, bias_ref, sink_ref, o_ref):
        q, kks, vvs = _swa_operands(zq_ref, kvc_ref, kvp_ref)
        scores = [_dot(kks[g], _stack_heads(q, h0), NT) for g, h0 in SWA_PARTS]
        probs = []
        for score, (_, h0) in zip(scores, SWA_PARTS):
            e, inv, _ = _swa_softmax(score, bias_ref, sink_ref, h0)
            probs.append((e * inv).astype(BF16))
        outs = [_dot(vvs[g], p, TN) for p, (g, _) in zip(probs, SWA_PARTS)]
        o_ref[...] = jnp.concatenate([_heads_to_lanes(o) for o in outs], axis=1).astype(BF16)

    return _fused_call(
        body, name=name, grid=(nb,), out_shape=jax.ShapeDtypeStruct((s, D_MODEL), BF16),
        in_specs=[pl.BlockSpec((SWA_BLOCK, W_B), lambda n: (n, 0)),
                  pl.BlockSpec((SWA_BLOCK, 256), lambda n: (n, 4)),
                  pl.BlockSpec((SWA_BLOCK, 256), lambda n: (jnp.maximum(n - 1, 0), 4)),
                  pl.BlockSpec((None, 2 * SWA_BLOCK, SWA_HEADS * SWA_BLOCK), lambda n: (jnp.minimum(n, 1), 0, 0)),
                  _resident((1, SWA_HEADS * SWA_BLOCK))],
        out_specs=pl.BlockSpec((SWA_BLOCK, D_MODEL), lambda n: (n, 0)), scratch_shapes=[],
        operands=[zb, zb, zb, bias_tables, sink_lanes], exchanges=exchanges)


def _swa_bwd(zb, do_b, bias_tables, sink_lanes, *, name, exchanges=()):
    s = zb.shape[0]
    nb = s // SWA_BLOCK
    scale = SWA_HEAD_DIM ** -0.5

    def body(zq_ref, kvc_ref, kvp_ref, do_ref, bias_ref, sink_ref, dz_ref, dbias_ref, dsink_ref, carry, dsink_acc):
        step = pl.program_id(0)

        @pl.when(step == 0)
        def _():
            carry[...] = jnp.zeros_like(carry)
            dsink_acc[...] = jnp.zeros_like(dsink_acc)
            dbias_ref[...] = jnp.zeros_like(dbias_ref)

        q, kks, vvs = _swa_operands(zq_ref, kvc_ref, kvp_ref)
        do = do_ref[...].astype(BF16)
        parts = range(len(SWA_PARTS))
        q_rows = [_stack_heads(q, h0) for _, h0 in SWA_PARTS]
        do_rows = [_stack_heads(do, h0) for _, h0 in SWA_PARTS]
        scores = [_dot(kks[g], q_rows[i], NT) for i, (g, _) in enumerate(SWA_PARTS)]
        soft = [_swa_softmax(scores[i], bias_ref, sink_ref, h0) for i, (_, h0) in enumerate(SWA_PARTS)]
        dps = [_dot(vvs[g], do_rows[i], NT) for i, (g, _) in enumerate(SWA_PARTS)]
        ps, dss = [], []
        for i, (_, h0) in enumerate(SWA_PARTS):
            e, inv, e_sink = soft[i]
            p = e * inv
            delta = jnp.sum(p * dps[i], axis=0, keepdims=True)
            ds = p * (dps[i] - delta)
            dbias_ref[:, _part_lanes(h0)] += ds
            dsink_acc[:, _part_lanes(h0)] -= e_sink * inv * delta
            ps.append(p.astype(BF16))
            dss.append(ds.astype(BF16))
        dqs = [_dot(kks[g], dss[i], TN) * scale for i, (g, _) in enumerate(SWA_PARTS)]
        in_group = lambda xs, g, axis: jnp.concatenate([xs[i] for i in parts if SWA_PARTS[i][0] == g], axis=axis)
        dkks = [_dot(in_group(dss, g, 1), in_group(q_rows, g, 0), NN) for g in range(SWA_KV_HEADS)]
        dvvs = [_dot(in_group(ps, g, 1), in_group(do_rows, g, 0), NN) for g in range(SWA_KV_HEADS)]
        dkv = jnp.concatenate(dkks + dvvs, axis=1)
        dz_ref[:, 0:1024] = jnp.concatenate([_heads_to_lanes(dq) for dq in dqs], axis=1).astype(BF16)
        dz_ref[:, 1024:1280] = (dkv[SWA_BLOCK:, :] + carry[...]).astype(BF16)
        carry[...] = dkv[:SWA_BLOCK, :]

        @pl.when(step == nb - 1)
        def _():
            acc = dsink_acc[...]
            dsink_ref[...] = jnp.concatenate([jnp.sum(acc[:, h * SWA_BLOCK:(h + 1) * SWA_BLOCK], axis=1, keepdims=True)
                                              for h in range(SWA_HEADS)], axis=1)

    rev = lambda i: (nb - 1 - i, 0)
    table_shape = (2 * SWA_BLOCK, SWA_HEADS * SWA_BLOCK)
    return _fused_call(
        body, name=name, grid=(nb,),
        out_shape=(jax.ShapeDtypeStruct((s, W_B), BF16), jax.ShapeDtypeStruct(table_shape, F32), jax.ShapeDtypeStruct((1, SWA_HEADS), F32)),
        in_specs=[pl.BlockSpec((SWA_BLOCK, W_B), rev),
                  pl.BlockSpec((SWA_BLOCK, 256), lambda i: (nb - 1 - i, 4)),
                  pl.BlockSpec((SWA_BLOCK, 256), lambda i: (jnp.maximum(nb - 2 - i, 0), 4)),
                  pl.BlockSpec((SWA_BLOCK, D_MODEL), rev),
                  pl.BlockSpec((None,) + table_shape, lambda i: (jnp.minimum(nb - 1 - i, 1), 0, 0)),
                  _resident((1, SWA_HEADS * SWA_BLOCK))],
        out_specs=(pl.BlockSpec((SWA_BLOCK, W_B), rev), pl.BlockSpec(table_shape, lambda i: (0, 0)),
                   pl.BlockSpec((1, SWA_HEADS), lambda i: (0, 0))),
        scratch_shapes=[pltpu.VMEM((SWA_BLOCK, 256), F32), pltpu.VMEM((1, SWA_HEADS * SWA_BLOCK), F32)],
        operands=[zb, zb, zb, do_b, bias_tables, sink_lanes], exchanges=exchanges)


MEM_COLS = [slice(h * MEM_HEAD_DIM, (h + 1) * MEM_HEAD_DIM) for h in range(MEM_HEADS)]
MEM_VCOLS = [slice(D_MODEL + h * MEM_HEAD_DIM, D_MODEL + (h + 1) * MEM_HEAD_DIM) for h in range(MEM_HEADS)]


def _mem_probs(zc_ref, mkv_ref):
    qs = [(zc_ref[:, c] * (MEM_HEAD_DIM ** -0.5)).astype(BF16) for c in MEM_COLS]
    scores = [_dot(qs[h], mkv_ref[:, c], NT) for h, c in enumerate(MEM_COLS)]
    ps = []
    for sc in scores:
        e = jnp.exp(sc - jnp.max(sc, axis=-1, keepdims=True))
        ps.append(e / jnp.sum(e, axis=-1, keepdims=True))
    return qs, ps


def _mem_fwd(xb, wi_t, mkv, *, name):
    s = xb.shape[0]
    t = min(512, s)

    def body(x_ref, w_ref, mkv_ref, zc_ref, o_ref):
        zc_ref[...] = _dot(x_ref[...], w_ref[...], NT).astype(BF16)
        _, ps = _mem_probs(zc_ref, mkv_ref)
        ps = [p.astype(BF16) for p in ps]
        o_ref[...] = jnp.concatenate([_dot(ps[h], mkv_ref[:, vc], NN) for h, vc in enumerate(MEM_VCOLS)], axis=1).astype(BF16)

    row = pl.BlockSpec((t, D_MODEL), lambda i: (i, 0))
    return pl.pallas_call(
        body, name=name, grid=(s // t,), out_shape=(jax.ShapeDtypeStruct((s, D_MODEL), BF16),) * 2,
        in_specs=[row, _resident_rows(wi_t, W_A + W_B, W_C), _resident((MEM_LEN, 2 * D_MODEL))],
        out_specs=(row, row), compiler_params=_params(("parallel",)),
    )(xb, wi_t, mkv)


def _mem_bwd(xb, zc, do_c, mkv, *, name):
    s = zc.shape[0]
    t = min(512, s)
    nt = s // t

    def body(x_ref, zc_ref, do_ref, mkv_ref, dz_ref, dmkv_ref, gwi_ref, acc):
        @pl.when(pl.program_id(0) == 0)
        def _():
            dmkv_ref[...] = jnp.zeros_like(dmkv_ref)
            acc[...] = jnp.zeros_like(acc)

        heads = range(MEM_HEADS)
        qs, ps = _mem_probs(zc_ref, mkv_ref)
        dos = [do_ref[:, c].astype(BF16) for c in MEM_COLS]
        dps = [_dot(dos[h], mkv_ref[:, MEM_VCOLS[h]], NT) for h in heads]
        dss = [(ps[h] * (dps[h] - jnp.sum(ps[h] * dps[h], axis=-1, keepdims=True))).astype(BF16) for h in heads]
        ps = [p.astype(BF16) for p in ps]
        dz = jnp.concatenate([_dot(dss[h], mkv_ref[:, MEM_COLS[h]], NN) * (MEM_HEAD_DIM ** -0.5) for h in heads], axis=1).astype(BF16)
        dz_ref[...] = dz
        dmkv_ref[...] += jnp.concatenate([_dot(dss[h], qs[h], TN) for h in heads] + [_dot(ps[h], dos[h], TN) for h in heads], axis=1)
        acc[...] += _dot(dz, x_ref[...], TN)

        @pl.when(pl.program_id(0) == nt - 1)
        def _():
            pltpu.sync_copy(acc, gwi_ref.at[pl.ds(W_A + W_B, W_C), :])

    row = pl.BlockSpec((t, D_MODEL), lambda i: (i, 0))
    return pl.pallas_call(
        body, name=name, grid=(nt,),
        out_shape=(jax.ShapeDtypeStruct((s, D_MODEL), BF16), jax.ShapeDtypeStruct((MEM_LEN, 2 * D_MODEL), F32),
                   jax.ShapeDtypeStruct((IN_COLS, D_MODEL), F32)),
        in_specs=[row, row, row, _resident((MEM_LEN, 2 * D_MODEL))],
        out_specs=(row, pl.BlockSpec((MEM_LEN, 2 * D_MODEL), lambda i: (0, 0)), HBM),
        scratch_shapes=[pltpu.VMEM((W_C, D_MODEL), F32)],
        compiler_params=_params(("arbitrary",)),
    )(xb, zc, do_c, mkv)


def _normalize(pre):
    mu = jnp.mean(pre, axis=-1, keepdims=True)
    xc = pre - mu
    rstd = lax.rsqrt(jnp.mean(xc * xc, axis=-1, keepdims=True) + LN_EPS)
    return xc * rstd, rstd


def _layer_norm_bwd(dh, xhat, rstd, g):
    dxh = dh * g
    dpre = rstd * (dxh - jnp.mean(dxh, axis=-1, keepdims=True) - xhat * jnp.mean(dxh * xhat, axis=-1, keepdims=True))
    return dpre, jnp.sum(dh * xhat, axis=0, keepdims=True), jnp.sum(dh, axis=0, keepdims=True)


def _merge_fwd(o_a, o_b, o_c, x, wi_t, wbr, wo, *, name):
    s = x.shape[0]
    t = min(256, s)
    row = lambda w: pl.BlockSpec((t, w), lambda i: (i, 0))

    def body(oa_ref, ob_ref, oc_ref, x_ref, wg_ref, wa_ref, wb_ref, wc_ref, wo_ref, zd_ref, xhat_ref, rstd_ref, merged_ref, pa_ref, pb_ref, pc_ref):
        wbr_refs = (wa_ref, wb_ref, wc_ref)
        zd_ref[...] = _dot(x_ref[...].astype(BF16), wg_ref[...], NT)
        merged = jnp.zeros((t, D_MODEL), F32)
        for b, (o_ref, p_ref) in enumerate(((oa_ref, pa_ref), (ob_ref, pb_ref), (oc_ref, pc_ref))):
            p = _dot(o_ref[...], wbr_refs[b][...], NN)
            p_ref[...] = p.astype(BF16)
            merged = merged + jax.nn.sigmoid(zd_ref[:, b * D_MODEL:(b + 1) * D_MODEL]) * p
        merged_b = merged.astype(BF16)
        merged_ref[...] = merged_b
        xhat, rstd = _normalize(ALPHA * x_ref[...] + _dot(merged_b, wo_ref[...], NN))
        xhat_ref[...] = xhat
        rstd_ref[...] = rstd

    act = jax.ShapeDtypeStruct((s, D_MODEL), F32)
    return pl.pallas_call(
        body, name=name, grid=(s // t,),
        out_shape=(jax.ShapeDtypeStruct((s, W_D), F32), act, jax.ShapeDtypeStruct((s, 1), F32)) + (jax.ShapeDtypeStruct((s, D_MODEL), BF16),) * 4,
        in_specs=[row(D_MODEL)] * 4 + [_resident_rows(wi_t, W_A + W_B + W_C, W_D)] + [_resident((D_MODEL, D_MODEL))] * 4,
        out_specs=(row(W_D), row(D_MODEL), row(1), row(D_MODEL), row(D_MODEL), row(D_MODEL), row(D_MODEL)),
        compiler_params=_params(("parallel",)),
    )(o_a, o_b, o_c, x, wi_t, *wbr, wo)


def _merge_bwd(dpre1, zd, pa, pb, pc, o_a, o_b, o_c, merged, wbr, wo, *, name, exchanges=()):
    s = dpre1.shape[0]
    t = min(256, s)
    nt = s // t
    row = lambda w: pl.BlockSpec((t, w), lambda i: (i, 0))

    def body(dpre_ref, zd_ref, pa_ref, pb_ref, pc_ref, oa_ref, ob_ref, oc_ref, mg_ref, wa_ref, wb_ref, wc_ref, wo_ref,
             dzd_ref, doa_ref, dob_ref, doc_ref, gwa_ref, gwb_ref, gwc_ref, gwo_ref, acc):
        step = pl.program_id(0)

        @pl.when(step == 0)
        def _():
            acc[...] = jnp.zeros_like(acc)

        dpre_b = dpre_ref[...].astype(BF16)
        dmerged = _dot(dpre_b, wo_ref[...], NT)
        acc[3] += _dot(mg_ref[...], dpre_b, TN)
        branches = ((pa_ref, oa_ref, doa_ref), (pb_ref, ob_ref, dob_ref), (pc_ref, oc_ref, doc_ref))
        for b, (p_ref, o_ref, do_ref) in enumerate(branches):
            gate = jax.nn.sigmoid(zd_ref[:, b * D_MODEL:(b + 1) * D_MODEL])
            dzd_ref[:, b * D_MODEL:(b + 1) * D_MODEL] = (dmerged * p_ref[...] * gate * (1.0 - gate)).astype(BF16)
            dp = (dmerged * gate).astype(BF16)
            acc[b] += _dot(o_ref[...], dp, TN)
            do_ref[...] = _dot(dp, (wa_ref, wb_ref, wc_ref)[b][...], NT).astype(do_ref.dtype)

        @pl.when(step == nt - 1)
        def _():
            for b, gw_ref in enumerate((gwa_ref, gwb_ref, gwc_ref, gwo_ref)):
                pltpu.sync_copy(acc.at[b], gw_ref)

    act = jax.ShapeDtypeStruct((s, D_MODEL), F32)
    actb = jax.ShapeDtypeStruct((s, D_MODEL), BF16)
    gw = jax.ShapeDtypeStruct((D_MODEL, D_MODEL), F32)
    return _fused_call(
        body, name=name, grid=(nt,),
        out_shape=(jax.ShapeDtypeStruct((s, W_D), BF16), act, actb, actb, gw, gw, gw, gw),
        in_specs=[row(D_MODEL), row(W_D)] + [row(D_MODEL)] * 7 + [_resident((D_MODEL, D_MODEL))] * 4,
        out_specs=(row(W_D),) + (row(D_MODEL),) * 3 + (HBM,) * 4, scratch_shapes=[pltpu.VMEM((4, D_MODEL, D_MODEL), F32)],
        operands=[dpre1, zd, pa, pb, pc, o_a, o_b, o_c, merged, *wbr, wo], exchanges=exchanges)


def _mlp_loss(xhat1, rstd1, target, ln1_g, ln1_b, ln2_g, ln2_b, wu, wd, *, name):
    s = xhat1.shape[0]
    t = min(256, s)
    npan = wu.shape[0]
    row = lambda w: pl.BlockSpec((t, w), lambda i: (i, 0))
    vec = _resident((1, D_MODEL))

    def body(xhat_ref, rstd_ref, tgt_ref, g1_ref, b1_ref, g2_ref, b2_ref, wu_ref, wd_ref,
             dpre1_ref, dpre2_ref, h1_ref, a_ref, du_ref, stats_ref):
        @pl.when(pl.program_id(0) == 0)
        def _():
            stats_ref[...] = jnp.zeros_like(stats_ref)

        xhat1_v = xhat_ref[...]
        h1 = xhat1_v * g1_ref[...] + b1_ref[...]
        h1_b = h1.astype(BF16)
        h1_ref[...] = h1_b
        us = []
        ff = jnp.zeros((t, D_MODEL), F32)
        for j in range(npan):
            u = _dot(h1_b, wu_ref[j], NN)
            us.append(u)
            r = jnp.maximum(u, 0.0)
            a_b = (r * r).astype(BF16)
            a_ref[:, j * D_MODEL:(j + 1) * D_MODEL] = a_b
            ff = ff + _dot(a_b, wd_ref[j], NN)
        xhat2, rstd2 = _normalize(ALPHA * h1 + ff)
        err = xhat2 * g2_ref[...] + b2_ref[...] - tgt_ref[...]
        stats_ref[4:5, :] += jnp.sum(err * err, axis=0, keepdims=True)
        dpre2, dg2, db2 = _layer_norm_bwd(err * (1.0 / D_MODEL), xhat2, rstd2, g2_ref[...])
        stats_ref[0:1, :] += dg2
        stats_ref[1:2, :] += db2
        dpre2_b = dpre2.astype(BF16)
        dpre2_ref[...] = dpre2_b
        dh1 = ALPHA * dpre2
        for j in range(npan):
            du_b = (_dot(dpre2_b, wd_ref[j], NT) * (2.0 * jnp.maximum(us[j], 0.0))).astype(BF16)
            du_ref[:, j * D_MODEL:(j + 1) * D_MODEL] = du_b
            dh1 = dh1 + _dot(du_b, wu_ref[j], NT)
        dpre1, dg1, db1 = _layer_norm_bwd(dh1, xhat1_v, rstd_ref[...], g1_ref[...])
        stats_ref[2:3, :] += dg1
        stats_ref[3:4, :] += db1
        dpre1_ref[...] = dpre1

    actb = jax.ShapeDtypeStruct((s, D_MODEL), BF16)
    wide = jax.ShapeDtypeStruct((s, D_FF), BF16)
    return pl.pallas_call(
        body, name=name, grid=(s // t,),
        out_shape=(jax.ShapeDtypeStruct((s, D_MODEL), F32), actb, actb, wide, wide, jax.ShapeDtypeStruct((8, D_MODEL), F32)),
        in_specs=[row(D_MODEL), row(1), row(D_MODEL), vec, vec, vec, vec,
                  _resident((npan, D_MODEL, D_MODEL)), _resident((npan, D_MODEL, D_MODEL))],
        out_specs=(row(D_MODEL), row(D_MODEL), row(D_MODEL), row(D_FF), row(D_FF), pl.BlockSpec((8, D_MODEL), lambda i: (0, 0))),
        compiler_params=_params(("arbitrary",)),
    )(xhat1, rstd1, target, ln1_g, ln1_b, ln2_g, ln2_b, wu, wd)


BRANCH_WEIGHTS = ("w_branch_hg", "w_branch_swa", "w_branch_mem")


def _local_step(x, xb, mem, target, wi_t, late, lb_logits, gain, sinks, rel_bias, ln1_g, ln1_b, ln2_g, ln2_b, *, distributed):
    s = x.shape[0]
    tm = min(1024, s)
    tk = min(2048, s)
    memb = mem.astype(BF16)
    if distributed:
        cx, cy, cc = lax.axis_index("x"), lax.axis_index("y"), lax.axis_index("c")
        pos = jnp.stack([2 * cx + cy, cc]).astype(jnp.int32)
    gather = (lambda names: [_gather_exchange([late[k] for k in names])]) if distributed else (lambda names: [])
    to_sibling = (lambda grads: [_sibling_halves_exchange(grads)]) if distributed else (lambda grads: [])
    to_chips = (lambda sums: [_chip_partials_exchange([bf for bf, _ in sums])]) if distributed else (lambda sums: [])

    def chip_sums(names, grads, from_sibling):
        return [_add_sibling(g, o, pos, name="add_sibling_" + k) for k, g, o in zip(names, grads, from_sibling)]

    def shard_sums(names, sums, from_chips):
        return {k: _add_chips(mine, o, pos, name="add_chips_" + k) for k, (_, mine), o in zip(names, sums, from_chips)}

    zb = _mm(xb, wi_t, mode="nt", tm=tm, tn=W_B, tk=D_MODEL, name="proj_b", out_dtype=BF16, b_rows=(W_A, W_B))
    onehot, maskrow = _bias_selector()
    bias_tables, sink_lanes = _swa_tables(_bias_table(rel_bias.T, onehot, maskrow, name="bias_table"), sinks)
    (za, o_a, o_raw, states), landed = _hgrn_fwd(xb, wi_t, lb_logits, gain, name="hgrn_fwd", exchanges=gather(("w_up", "w_down", "w_mem_kv")))
    wu, wd, wmkv = landed[0] if distributed else (late["wu"], late["wd"], late["wmkv"])
    mkv = _mm(memb, wmkv, mode="nn", tm=MEM_LEN, tn=512, tk=D_MODEL, name="mem_kv", out_dtype=BF16, b_panels=True)
    o_b, landed = _swa_fwd(zb, bias_tables, sink_lanes, name="swa_fwd", exchanges=gather(BRANCH_WEIGHTS + ("w_out",)))
    if distributed:
        wbr = [wb.reshape(D_MODEL, D_MODEL) for wb in landed[0][:3]]
        wo = landed[0][3].reshape(D_MODEL, D_MODEL)
    else:
        wbr, wo = [late["wbr"][b] for b in range(3)], late["wo"]
    zc, o_c = _mem_fwd(xb, wi_t, mkv, name="mem_fwd")
    zd, xhat1, rstd1, merged, pa, pb, pc = _merge_fwd(o_a, o_b, o_c, x, wi_t, wbr, wo, name="merge_fwd")

    dpre1, dpre2, h1, act, du, ln_stats = _mlp_loss(xhat1, rstd1, target, ln1_g, ln1_b, ln2_g, ln2_b, wu, wd, name="mlp_loss")
    ffn = ("w_down", "w_up")
    g_ffn = [_mm(act, dpre2, mode="tn", tm=1024, tn=D_MODEL, tk=tk, name="grad_w_down").reshape(N_SHARDS, D_FF // N_SHARDS, D_MODEL),
             _mm(h1, du, mode="tn", tm=D_MODEL, tn=1024, tk=tk, name="grad_w_up", out_panels=True)]

    (dzd, do_a, do_b, do_c, *g_merge), landed = _merge_bwd(dpre1, zd, pa, pb, pc, o_a, o_b, o_c, merged, wbr, wo, name="merge_bwd",
                                                           exchanges=to_sibling(g_ffn))
    sums_ffn = chip_sums(ffn, g_ffn, landed[0]) if distributed else []
    dzc, dmkv, g_wi = _mem_bwd(xb, zc, do_c, mkv, name="mem_bwd")
    merge = BRANCH_WEIGHTS + ("w_out", "w_mem_kv")
    g_merge = [g.reshape(N_SHARDS, D_MODEL // N_SHARDS, D_MODEL) for g in g_merge]
    g_merge.append(_mm(memb, dmkv, mode="tn", tm=D_MODEL, tn=512, tk=MEM_LEN, name="grad_w_mem_kv", out_panels=True))
    (dza, hg_stats), landed = _hgrn_bwd(za, o_raw, do_a, states, lb_logits, gain, name="hgrn_bwd",
                                        exchanges=to_chips(sums_ffn) + to_sibling(g_merge))
    halves = shard_sums(ffn, sums_ffn, landed[0]) if distributed else {}
    sums_merge = chip_sums(merge, g_merge, landed[1]) if distributed else []
    (dzb, dbias_t, dsinks), landed = _swa_bwd(zb, do_b, bias_tables, sink_lanes, name="swa_bwd", exchanges=to_chips(sums_merge))
    if distributed:
        halves.update(shard_sums(merge, sums_merge, landed[0]))
    dbias = dbias_t.reshape(2 * SWA_BLOCK, SWA_HEADS, SWA_BLOCK).transpose(1, 2, 0).reshape(SWA_HEADS, -1)
    d_rel_bias = _bias_grad(dbias, onehot, name="bias_grad").T

    proj = ("w_in",)
    for dz, offset, nm in ((dza, 0, "grad_w_in_a"), (dzb, W_A, "grad_w_in_b"), (dzd, W_A + W_B + W_C, "grad_w_in_d")):
        g_wi = _mm(dz, xb, mode="tn", tm=dz.shape[1] if dz.shape[1] <= 1280 else 1024, tn=D_MODEL, tk=tk, name=nm,
                   rows_of=IN_COLS, row_offset=offset, into=g_wi)
    g_proj = [g_wi.reshape(N_SHARDS, IN_COLS // N_SHARDS, D_MODEL)]
    small = dict(lb_logits=hg_stats[1:3], hg_norm_gain=hg_stats[0:1], swa_sinks=dsinks, rel_bias=d_rel_bias,
                 ln1_g=ln_stats[2:3], ln1_b=ln_stats[3:4], ln2_g=ln_stats[0:1], ln2_b=ln_stats[1:2], sq_err=ln_stats[4:5])
    small_exchange = [_small_gather_exchange(_pack_small(small, name="pack_small"))] if distributed else []
    join_exchange = [_join_exchange([halves[k] for k in ffn + merge])] if distributed else []
    tx = min(512, s // 2)
    head = max(1, 3 * (s // tx) // 16)
    dx = functools.partial(_dx_matmul, [dza, dzb, dzc, dzd], wi_t, dpre1, tm=tx)
    grad_x_head, landed = dx(tiles=(0, head), name="grad_x_head", exchanges=to_sibling(g_proj))
    sums_proj = chip_sums(proj, g_proj, landed[0]) if distributed else []
    grad_x_tail, landed = dx(tiles=(head, s // tx - head), name="grad_x", exchanges=to_chips(sums_proj) + small_exchange + join_exchange)
    grad_x = jnp.concatenate([grad_x_head, grad_x_tail])
    if distributed:
        halves.update(shard_sums(proj, sums_proj, landed[0]))
        small = landed[1][0]
        halves.update(zip(ffn + merge, landed[2]))
    else:
        halves = dict(zip(ffn + merge + proj, g_ffn + g_merge + g_proj))
    return grad_x, halves, small


def _mesh_position():
    x, y, c = lax.axis_index("x"), lax.axis_index("y"), lax.axis_index("c")
    chips = [(1 - x, y), (x, 1 - y), (1 - x, 1 - y)]
    return x, y, c, chips


class _Exchange(NamedTuple):
    operands: list
    out_shapes: list
    n_sems: int
    start: Callable
    finish: Callable
    halfway: Optional[Callable] = None
    in_place: bool = False


def _gather_exchange(shards):
    n = len(shards)
    per = 9
    assert all(w.shape[0] % (4 * BF16_SUBLANES) == 0 for w in shards)

    def plan(ins, outs, send_sems, recv_sems):
        x, y, c, (x_nbr, y_nbr, diag) = _mesh_position()
        sibling = (x, y, 1 - c)
        slot = lambda chip: 2 * chip[0] + chip[1]

        def rows(a, chip, hc, quarter=None):
            rh = shards[a].shape[0] // 2
            if quarter is None:
                return outs[a].at[slot(chip), pl.ds(hc * rh, rh), :]
            return outs[a].at[slot(chip), pl.ds(hc * rh + quarter * (rh // 2), rh // 2), :]

        def copy(a, k, src, dst, to):
            return pltpu.make_async_remote_copy(src_ref=src, dst_ref=dst, send_sem=send_sems.at[a * per + k], recv_sem=recv_sems.at[a * per + k],
                                                device_id=to, device_id_type=MESH)

        first, from_sibling = [], []
        landed, then = [[] for _ in range(4)], [[] for _ in range(4)]
        for a in range(n):
            rh = shards[a].shape[0] // 2
            my_half = ins[a].at[pl.ds(c * rh, rh), :]
            first += [copy(a, 4, ins[a], outs[a].at[slot((x, y))], sibling),
                      copy(a, 0, my_half, rows(a, (x, y), c), (*x_nbr, c)), copy(a, 1, my_half, rows(a, (x, y), c), (*y_nbr, c))]
            landed[0].append(copy(a, 0, rows(a, x_nbr, c), rows(a, x_nbr, c), (*x_nbr, c)))
            then[0].append([copy(a, 2, rows(a, x_nbr, c, 0), rows(a, x_nbr, c, 0), (*y_nbr, c)), copy(a, 5, rows(a, x_nbr, c), rows(a, x_nbr, c), sibling)])
            landed[1].append(copy(a, 1, rows(a, y_nbr, c), rows(a, y_nbr, c), (*y_nbr, c)))
            then[1].append([copy(a, 3, rows(a, y_nbr, c, 1), rows(a, y_nbr, c, 1), (*x_nbr, c)), copy(a, 6, rows(a, y_nbr, c), rows(a, y_nbr, c), sibling)])
            landed[2].append(copy(a, 2, rows(a, diag, c, 0), rows(a, diag, c, 0), (*y_nbr, c)))
            then[2].append([copy(a, 7, rows(a, diag, c, 0), rows(a, diag, c, 0), sibling)])
            landed[3].append(copy(a, 3, rows(a, diag, c, 1), rows(a, diag, c, 1), (*x_nbr, c)))
            then[3].append([copy(a, 8, rows(a, diag, c, 1), rows(a, diag, c, 1), sibling)])
            from_sibling += [copy(a, 4, outs[a].at[slot((x, y))], outs[a].at[slot((x, y))], sibling),
                             copy(a, 5, rows(a, x_nbr, 1 - c), rows(a, x_nbr, 1 - c), sibling), copy(a, 6, rows(a, y_nbr, 1 - c), rows(a, y_nbr, 1 - c), sibling),
                             copy(a, 7, rows(a, diag, 1 - c, 0), rows(a, diag, 1 - c, 0), sibling), copy(a, 8, rows(a, diag, 1 - c, 1), rows(a, diag, 1 - c, 1), sibling)]
        return first, landed, then, from_sibling

    def start(*refs):
        first, _, _, _ = plan(*refs)
        for cp in first:
            cp.start()

    def stages(landed, then, which):
        for stage in which:
            for arrival, onward in zip(landed[stage], then[stage]):
                arrival.wait_recv()
                for cp in onward:
                    cp.start()

    def halfway(*refs):
        _, landed, then, _ = plan(*refs)
        stages(landed, then, (0, 1))

    def finish(*refs):
        first, landed, then, from_sibling = plan(*refs)
        stages(landed, then, (2, 3))
        for cp in from_sibling:
            cp.wait_recv()
        for cp in first + [cp for stage in then for onward in stage for cp in onward]:
            cp.wait_send()

    return _Exchange(list(shards), [jax.ShapeDtypeStruct((N_SHARDS,) + w.shape, w.dtype) for w in shards], per * n, start, finish, halfway)


def _sibling_halves_exchange(grads):
    n = len(grads)

    def plan(ins, outs, send_sems, recv_sems):
        x, y, c, _ = _mesh_position()
        return [pltpu.make_async_remote_copy(src_ref=ins[a].at[:, pl.ds((1 - c) * (grads[a].shape[1] // 2), grads[a].shape[1] // 2), :],
                                             dst_ref=outs[a], send_sem=send_sems.at[a], recv_sem=recv_sems.at[a],
                                             device_id=(x, y, 1 - c), device_id_type=MESH) for a in range(n)]

    def start(*refs):
        for cp in plan(*refs):
            cp.start()

    def finish(*refs):
        for cp in plan(*refs):
            cp.wait()

    return _Exchange(list(grads), [jax.ShapeDtypeStruct((g.shape[0], g.shape[1] // 2, g.shape[2]), g.dtype) for g in grads], n, start, finish)


def _chip_partials_exchange(sums):
    n = len(sums)

    def plan(ins, outs, send_sems, recv_sems):
        _, _, c, chips = _mesh_position()
        return [pltpu.make_async_remote_copy(src_ref=ins[a].at[2 * cx + cy], dst_ref=outs[a].at[k], send_sem=send_sems.at[a * 3 + k],
                                             recv_sem=recv_sems.at[a * 3 + k], device_id=(cx, cy, c), device_id_type=MESH)
                for k, (cx, cy) in enumerate(chips) for a in range(n)]

    def start(*refs):
        for cp in plan(*refs):
            cp.start()

    def finish(*refs):
        for cp in plan(*refs):
            cp.wait()

    return _Exchange(list(sums), [jax.ShapeDtypeStruct((3,) + g.shape[1:], g.dtype) for g in sums], 3 * n, start, finish)


def _fused_call(body, *, name, grid, in_specs, out_specs, out_shape, scratch_shapes, operands, exchanges=()):
    single = not isinstance(out_shape, (tuple, list))
    out_specs = [out_specs] if single else list(out_specs)
    out_shape = [out_shape] if single else list(out_shape)
    n_in, n_out, n_scr = len(in_specs), len(out_specs), len(scratch_shapes)
    x_in = [len(e.operands) for e in exchanges]
    x_out = [len(e.out_shapes) for e in exchanges]

    def wrapped(*refs):
        refs = list(refs)
        ins = refs[:n_in]
        pos = n_in
        ex_ins = []
        for k in x_in:
            ex_ins.append(refs[pos:pos + k])
            pos += k
        outs = refs[pos:pos + n_out]
        pos += n_out
        ex_outs = []
        for k in x_out:
            ex_outs.append(refs[pos:pos + k])
            pos += k
        scratch = refs[pos:pos + n_scr]
        sems = refs[pos + n_scr:]
        first, last, middle = None, None, None
        for axis, size in enumerate(grid):
            at_start, at_end, at_middle = pl.program_id(axis) == 0, pl.program_id(axis) == size - 1, pl.program_id(axis) == size // 2
            first = at_start if first is None else first & at_start
            last = at_end if last is None else last & at_end
            middle = at_middle if middle is None else middle & at_middle

        @pl.when(first)
        def _():
            for i, e in enumerate(exchanges):
                e.start(ex_ins[i], ex_outs[i], sems[2 * i], sems[2 * i + 1])

        if any(e.halfway for e in exchanges):
            @pl.when(middle)
            def _():
                for i, e in enumerate(exchanges):
                    if e.halfway:
                        e.halfway(ex_ins[i], ex_outs[i], sems[2 * i], sems[2 * i + 1])

        body(*ins, *outs, *scratch)

        @pl.when(last)
        def _():
            for i, e in enumerate(exchanges):
                e.finish(ex_ins[i], ex_outs[i], sems[2 * i], sems[2 * i + 1])

    n_x_in, n_x_out = sum(x_in), sum(x_out)
    aliases = {}
    for i, e in enumerate(exchanges):
        if e.in_place:
            aliases.update({n_in + sum(x_in[:i]) + a: n_out + sum(x_out[:i]) + a for a in range(x_in[i])})
    results = pl.pallas_call(
        wrapped if exchanges else body, name=name, grid=grid,
        in_specs=list(in_specs) + [HBM] * n_x_in,
        out_specs=out_specs + [HBM] * n_x_out,
        out_shape=out_shape + [s for e in exchanges for s in e.out_shapes], input_output_aliases=aliases,
        scratch_shapes=list(scratch_shapes) + [pltpu.SemaphoreType.DMA((e.n_sems,)) for e in exchanges for _ in range(2)],
        compiler_params=_params(("arbitrary",) * len(grid)),
    )(*operands, *[a for e in exchanges for a in e.operands])
    own = results[0] if single else tuple(results[:n_out])
    landed, pos = [], n_out
    for k in x_out:
        landed.append(list(results[pos:pos + k]))
        pos += k
    return own, landed


def _cast_bf16(x, *, name, exchanges=()):
    s, cols = x.shape
    t = min(512, s)

    def body(x_ref, o_ref):
        o_ref[...] = x_ref[...].astype(BF16)

    tile = pl.BlockSpec((t, cols), lambda i: (i, 0))
    return _fused_call(body, name=name, grid=(s // t,), in_specs=[tile], out_specs=tile, out_shape=jax.ShapeDtypeStruct((s, cols), BF16),
                       scratch_shapes=[], operands=[x], exchanges=exchanges)


ROW_TILE_MAX = 640
BF16_SUBLANES = 16


def _row_tile(rows):
    for tr in range(min(rows, ROW_TILE_MAX), 0, -1):
        if rows % tr == 0 and tr % BF16_SUBLANES == 0:
            return tr
    raise ValueError(rows)


def _add_sibling(grad, other, pos, *, name):
    p, r, cols = grad.shape
    rh = r // 2
    tr = _row_tile(rh)
    nb = rh // tr

    def body(pos_ref, g_ref, o_ref, sb_ref, mine_ref):
        total = g_ref[...] + o_ref[...]
        sb_ref[...] = total.astype(BF16)

        @pl.when(pl.program_id(1) == pos_ref[0])
        def _():
            mine_ref[...] = total

    return pl.pallas_call(
        body, name=name, out_shape=(jax.ShapeDtypeStruct((p, rh, cols), BF16), jax.ShapeDtypeStruct((rh, cols), F32)),
        grid_spec=pltpu.PrefetchScalarGridSpec(
            num_scalar_prefetch=1, grid=(nb, p),
            in_specs=[pl.BlockSpec((None, tr, cols), lambda i, j, pos_ref: (j, pos_ref[1] * nb + i, 0)),
                      pl.BlockSpec((None, tr, cols), lambda i, j, pos_ref: (j, i, 0))],
            out_specs=(pl.BlockSpec((None, tr, cols), lambda i, j, pos_ref: (j, i, 0)),
                       pl.BlockSpec((tr, cols), lambda i, j, pos_ref: (i, 0)))),
        compiler_params=_params(("parallel", "arbitrary")),
    )(pos, grad, other)


def _add_chips(mine, others, pos, *, name):
    rh, cols = mine.shape
    tr = _row_tile(rh)
    nb = rh // tr

    def body(pos_ref, s_ref, o_ref, r_ref):
        r_ref[...] = ((s_ref[...] + o_ref[0].astype(F32)) + o_ref[1].astype(F32)) + o_ref[2].astype(F32)

    return pl.pallas_call(
        body, name=name, out_shape=jax.ShapeDtypeStruct((2 * rh, cols), F32),
        grid_spec=pltpu.PrefetchScalarGridSpec(
            num_scalar_prefetch=1, grid=(nb,),
            in_specs=[pl.BlockSpec((tr, cols), lambda i, pos_ref: (i, 0)),
                      pl.BlockSpec((3, tr, cols), lambda i, pos_ref: (0, i, 0))],
            out_specs=pl.BlockSpec((tr, cols), lambda i, pos_ref: (pos_ref[1] * nb + i, 0))),
        compiler_params=_params(("parallel",)),
    )(pos, mine, others)


def _join_exchange(bufs):
    n = len(bufs)

    def copy(a, hc, ins, outs, send_sems, recv_sems):
        x, y, c, _ = _mesh_position()
        rh = bufs[a].shape[0] // 2
        rows = pl.ds(hc * rh, rh)
        return pltpu.make_async_remote_copy(src_ref=ins[a].at[rows, :], dst_ref=outs[a].at[rows, :], send_sem=send_sems.at[a],
                                            recv_sem=recv_sems.at[a], device_id=(x, y, 1 - c), device_id_type=MESH)

    def start(*refs):
        c = lax.axis_index("c")
        for a in range(n):
            copy(a, c, *refs).start()

    def finish(*refs):
        c = lax.axis_index("c")
        for a in range(n):
            copy(a, c, *refs).wait_send()
            copy(a, 1 - c, *refs).wait_recv()

    return _Exchange(list(bufs), [jax.ShapeDtypeStruct(b.shape, b.dtype) for b in bufs], n, start, finish, in_place=True)


def _join_halves(bufs, *, name):
    n = len(bufs)
    join = _join_exchange(bufs)

    def body(*refs):
        ins, outs, sems = refs[:n], refs[n:2 * n], refs[2 * n:]
        join.start(ins, outs, *sems)
        join.finish(ins, outs, *sems)

    return pl.pallas_call(
        body, name=name, out_shape=join.out_shapes, in_specs=[HBM] * n, out_specs=[HBM] * n, input_output_aliases={a: a for a in range(n)},
        scratch_shapes=[pltpu.SemaphoreType.DMA((n,)), pltpu.SemaphoreType.DMA((n,))],
    )(*bufs)


SMALL = ["lb_logits", "hg_norm_gain", "swa_sinks", "rel_bias", "ln1_g", "ln1_b", "ln2_g", "ln2_b"]
PACK_ROWS = 48
PACK_AT = dict(lb_logits=(slice(0, 2), slice(0, D_MODEL)), hg_norm_gain=(slice(2, 3), slice(0, D_MODEL)), ln1_g=(slice(3, 4), slice(0, D_MODEL)),
               ln1_b=(slice(4, 5), slice(0, D_MODEL)), ln2_g=(slice(5, 6), slice(0, D_MODEL)), ln2_b=(slice(6, 7), slice(0, D_MODEL)),
               swa_sinks=(slice(7, 8), slice(0, SWA_HEADS)), sq_err=(slice(8, 9), slice(0, D_MODEL)),
               rel_bias=(slice(16, 16 + NUM_BUCKETS), slice(0, SWA_HEADS)))


def _pack_small(grads, *, name):
    names = SMALL + ["sq_err"]

    def body(*refs):
        packed = refs[len(names)]
        packed[...] = jnp.zeros_like(packed)
        for k, g_ref in zip(names, refs):
            packed[PACK_AT[k]] = g_ref[...]

    return pl.pallas_call(body, name=name, out_shape=jax.ShapeDtypeStruct((PACK_ROWS, D_MODEL), F32), compiler_params=_params(),
                          )(*[grads[k] for k in names])


def _small_gather_exchange(packed):
    def plan(ins, outs, send_sems, recv_sems):
        x, y, c, _ = _mesh_position()
        me = 4 * x + 2 * y + c
        own = pltpu.make_async_copy(ins[0], outs[0].at[me], send_sems.at[7])
        remote = []
        for d in range(1, 8):
            dx, dy, dc = (d >> 2) & 1, (d >> 1) & 1, d & 1
            remote.append(pltpu.make_async_remote_copy(src_ref=ins[0], dst_ref=outs[0].at[me], send_sem=send_sems.at[d - 1],
                                                       recv_sem=recv_sems.at[d - 1], device_id=(x ^ dx, y ^ dy, c ^ dc), device_id_type=MESH))
        return own, remote

    def start(*refs):
        own, remote = plan(*refs)
        own.start()
        for cp in remote:
            cp.start()

    def finish(*refs):
        own, remote = plan(*refs)
        for cp in remote:
            cp.wait()
        own.wait()

    return _Exchange([packed], [jax.ShapeDtypeStruct((8,) + packed.shape, packed.dtype)], 8, start, finish)


def _adamw_small(gathered, w, m, v, *, name):
    names = SMALL
    n = len(names)

    def body(*refs):
        gathered_ref = refs[0]
        w_refs, m_refs, v_refs = (dict(zip(names, refs[1 + i * n:1 + (i + 1) * n])) for i in range(3))
        loss_ref = refs[1 + 3 * n]
        go_refs, d_refs, nm_refs, nv_refs = (dict(zip(names, refs[2 + (3 + i) * n:2 + (4 + i) * n])) for i in range(4))
        total_ref = refs[2 + 7 * n]
        total = gathered_ref[0]
        for j in range(1, 8):
            total = total + gathered_ref[j]
        total_ref[...] = total
        loss_ref[...] = (0.5 / D_MODEL) * jnp.sum(total_ref[PACK_AT["sq_err"]], axis=1, keepdims=True)
        for k in names:
            g = total_ref[PACK_AT[k]]
            go_refs[k][...] = g
            d_refs[k][...], nm_refs[k][...], nv_refs[k][...] = _adamw_math(w_refs[k][...], g, m_refs[k][...], v_refs[k][...])

    like = [jax.ShapeDtypeStruct(w[k].shape, F32) for k in names]
    results = pl.pallas_call(body, name=name, out_shape=[jax.ShapeDtypeStruct((1, 1), F32)] + like * 4,
                             scratch_shapes=[pltpu.VMEM((PACK_ROWS, D_MODEL), F32)],
                             compiler_params=_params())(gathered, *[d[k] for d in (w, m, v) for k in names])
    return results[0], {k: tuple(results[1 + i * n + j] for i in range(4)) for j, k in enumerate(names)}


def _adamw_math(w, g, m, v):
    m = ADAM_B1 * m + (1.0 - ADAM_B1) * g
    v = ADAM_B2 * v + (1.0 - ADAM_B2) * (g * g)
    m_hat = m / (1.0 - ADAM_B1 ** ADAM_STEP)
    v_hat = v / (1.0 - ADAM_B2 ** ADAM_STEP)
    delta = -ADAM_LR * (m_hat / (jnp.sqrt(v_hat) + ADAM_EPS) + ADAM_WD * w)
    return delta, m, v


def _adamw(w, g, m, v, *, name):
    _, rows, cols = w.shape
    tr = _row_tile(rows)
    blk = pl.BlockSpec((None, tr, cols), lambda i: (0, i, 0))
    flat = pl.BlockSpec((tr, cols), lambda i: (i, 0))

    def body(w_ref, g_ref, m_ref, v_ref, go_ref, d_ref, nm_ref, nv_ref):
        g_v = g_ref[...]
        go_ref[...] = g_v
        d_ref[...], nm_ref[...], nv_ref[...] = _adamw_math(w_ref[...], g_v, m_ref[...], v_ref[...])

    shape = jax.ShapeDtypeStruct((1, rows, cols), F32)
    return pl.pallas_call(body, name=name, grid=(rows // tr,), out_shape=(shape,) * 4, in_specs=[blk, flat, blk, blk], out_specs=(blk,) * 4,
                          compiler_params=_params(("parallel",)))(w, g, m, v)


WEIGHTS = ["w_in", "lb_logits", "hg_norm_gain", "swa_sinks", "rel_bias", "w_mem_kv", "w_branch_hg", "w_branch_swa", "w_branch_mem",
           "w_out", "ln1_g", "ln1_b", "w_up", "w_down", "ln2_g", "ln2_b"]
BIG = ["w_in", "w_mem_kv", "w_branch_hg", "w_branch_swa", "w_branch_mem", "w_out", "w_up", "w_down"]


def kernel(x, mem, w_in, lb_logits, hg_norm_gain, swa_sinks, rel_bias, w_mem_kv, w_branch_hg, w_branch_swa, w_branch_mem, w_out, ln1_g, ln1_b, w_up, w_down, ln2_g, ln2_b, loss_target, m_w_in, m_lb_logits, m_hg_norm_gain, m_swa_sinks, m_rel_bias, m_w_mem_kv, m_w_branch_hg, m_w_branch_swa, m_w_branch_mem, m_w_out, m_ln1_g, m_ln1_b, m_w_up, m_w_down, m_ln2_g, m_ln2_b, v_w_in, v_lb_logits, v_hg_norm_gain, v_swa_sinks, v_rel_bias, v_w_mem_kv, v_w_branch_hg, v_w_branch_swa, v_w_branch_mem, v_w_out, v_ln1_g, v_ln1_b, v_w_up, v_w_down, v_ln2_g, v_ln2_b):
    w = dict(w_in=w_in, lb_logits=lb_logits, hg_norm_gain=hg_norm_gain, swa_sinks=swa_sinks, rel_bias=rel_bias, w_mem_kv=w_mem_kv,
             w_branch_hg=w_branch_hg, w_branch_swa=w_branch_swa, w_branch_mem=w_branch_mem, w_out=w_out, ln1_g=ln1_g, ln1_b=ln1_b,
             w_up=w_up, w_down=w_down, ln2_g=ln2_g, ln2_b=ln2_b)
    m = dict(w_in=m_w_in, lb_logits=m_lb_logits, hg_norm_gain=m_hg_norm_gain, swa_sinks=m_swa_sinks, rel_bias=m_rel_bias, w_mem_kv=m_w_mem_kv,
             w_branch_hg=m_w_branch_hg, w_branch_swa=m_w_branch_swa, w_branch_mem=m_w_branch_mem, w_out=m_w_out, ln1_g=m_ln1_g, ln1_b=m_ln1_b,
             w_up=m_w_up, w_down=m_w_down, ln2_g=m_ln2_g, ln2_b=m_ln2_b)
    v = dict(w_in=v_w_in, lb_logits=v_lb_logits, hg_norm_gain=v_hg_norm_gain, swa_sinks=v_swa_sinks, rel_bias=v_rel_bias, w_mem_kv=v_w_mem_kv,
             w_branch_hg=v_w_branch_hg, w_branch_swa=v_w_branch_swa, w_branch_mem=v_w_branch_mem, w_out=v_w_out, ln1_g=v_ln1_g, ln1_b=v_ln1_b,
             w_up=v_w_up, w_down=v_w_down, ln2_g=v_ln2_g, ln2_b=v_ln2_b)
    shapes = {k: w[k].shape for k in WEIGHTS}
    for d in (w, m, v):
        d["w_in"] = d["w_in"].reshape(D_MODEL, IN_COLS // N_SHARDS).T[None]
    shards = {k: w[k].reshape(w[k].shape[-2], w[k].shape[-1]).astype(BF16) for k in BIG}
    x2d = x.reshape(x.shape[-2], D_MODEL)
    xb, ((wi4,),) = _cast_bf16(x2d, name="gather_weights", exchanges=[_gather_exchange([shards["w_in"]])])
    wi_t = wi4.reshape(IN_COLS, D_MODEL)

    grad_x, halves, small = _local_step(
        x2d, xb, mem.reshape(MEM_LEN, D_MODEL), loss_target.reshape(loss_target.shape[-2], D_MODEL),
        wi_t, shards, lb_logits, hg_norm_gain, swa_sinks, rel_bias, ln1_g, ln1_b, ln2_g, ln2_b, distributed=True)

    reduced = dict(halves)
    reduced["w_in"], = _join_halves([halves["w_in"]], name="join_halves")

    outs = {k: _adamw(w[k], reduced[k], m[k], v[k], name="adamw_" + k) for k in BIG}
    loss, small_outs = _adamw_small(small, w, m, v, name="adamw_small")
    outs.update(small_outs)
    grad_out, delta_out, m_out, v_out = ({k: outs[k][i] for k in WEIGHTS} for i in range(4))
    for out in (grad_out, delta_out, m_out, v_out):
        out["w_in"] = out["w_in"][0].T

    result = [loss.reshape(()), grad_x.reshape(x.shape)]
    for out in (grad_out, delta_out, m_out, v_out):
        result += [out[k].reshape(shapes[k]) for k in WEIGHTS]
    return tuple(result)
```

```python
import functools
import math
from typing import Callable, NamedTuple, Optional

import jax
import jax.numpy as jnp
from jax import lax
from jax.experimental import pallas as pl
from jax.experimental.pallas import tpu as pltpu

F32 = jnp.float32
BF16 = jnp.bfloat16
HIGHEST = lax.Precision.HIGHEST
MESH = pl.DeviceIdType.MESH

D_MODEL = 1024
MEM_LEN = 256
HG_HEADS = 8
HG_DK = 128
HG_CHUNK = 64
SWA_HEADS = 16
SWA_KV_HEADS = 2
SWA_GROUP = 8
SWA_HEAD_DIM = 64
SWA_BLOCK = 128
SWA_WINDOW = 128
MEM_HEADS = 4
MEM_HEAD_DIM = 256
NUM_BUCKETS = 32
MAX_DISTANCE = 128
D_FF = 4096
LN_EPS = 1e-5
RMS_EPS = 1e-6
ALPHA = 2.0 ** 0.25
W_A, W_B, W_C, W_D = 4096, 1280, 1024, 3072
IN_COLS = W_A + W_B + W_C + W_D
N_SHARDS = 4
ADAM_LR = 0.001
ADAM_B1 = 0.9
ADAM_B2 = 0.999
ADAM_EPS = 1e-08
ADAM_WD = 0.01
ADAM_STEP = 10
MASK_VALUE = -1e30
VMEM_LIMIT = 56 * 1024 * 1024

NN = ((1,), (0,))
NT = ((1,), (1,))
TN = ((0,), (0,))
HBM = pl.BlockSpec(memory_space=pltpu.HBM)


def _dot(a, b, dims=NN, precision=None):
    return lax.dot_general(a, b, (dims, ((), ())), precision=precision, preferred_element_type=F32)


def _params(sem=None):
    return pltpu.CompilerParams(dimension_semantics=sem, vmem_limit_bytes=VMEM_LIMIT)


def _resident(shape):
    zeros = (0,) * len(shape)
    return pl.BlockSpec(shape, lambda *_: zeros, pipeline_mode=pl.Buffered(1))


def _resident_rows(arr, offset, rows):
    return pl.BlockSpec((pl.Element(rows), pl.Element(arr.shape[1])), lambda *_: (offset, 0), pipeline_mode=pl.Buffered(1))


def _mm(a, b, *, mode, tm, tn, tk, name, out_dtype=F32, b_panels=False, b_rows=None, out_panels=False, rows_of=None, row_offset=0,
        into=None):
    if mode == "tn":
        kdim, m = a.shape
    else:
        m, kdim = a.shape
    if b_panels:
        n = b.shape[0] * b.shape[2]
        assert b.shape[2] == tn and mode == "nn"
    elif b_rows is not None:
        assert mode == "nt"
        b_offset, n = b_rows
    elif mode == "nt":
        n = b.shape[0]
    else:
        n = b.shape[1]
    assert m % tm == 0 and n % tn == 0 and kdim % tk == 0, (name, m, n, kdim)
    nk = kdim // tk
    dims = {"nn": NN, "nt": NT, "tn": TN}[mode]
    a_spec = pl.BlockSpec((tk, tm), lambda i, j, k: (k, i)) if mode == "tn" else pl.BlockSpec((tm, tk), lambda i, j, k: (i, k))
    if b_panels:
        b_spec = pl.BlockSpec((None, tk, tn), lambda i, j, k: (j, k, 0))
    elif b_rows is not None:
        assert b_offset % BF16_SUBLANES == 0 and tn % BF16_SUBLANES == 0 and tk % 128 == 0
        b_spec = pl.BlockSpec((pl.Element(tn), pl.Element(tk)),
                              lambda i, j, k: (pl.multiple_of(b_offset + j * tn, BF16_SUBLANES), pl.multiple_of(k * tk, 128)))
    elif mode == "nt":
        b_spec = pl.BlockSpec((tn, tk), lambda i, j, k: (j, k))
    else:
        b_spec = pl.BlockSpec((tk, tn), lambda i, j, k: (k, j))
    in_specs = [a_spec, b_spec]
    operands = [a, b]
    aliases = {}
    if out_panels:
        out_shape = jax.ShapeDtypeStruct((n // tn, m, tn), out_dtype)
        o_spec = pl.BlockSpec((None, tm, tn), lambda i, j, k: (j, i, 0))
    elif rows_of is not None:
        out_shape = jax.ShapeDtypeStruct((rows_of, n), out_dtype)
        assert row_offset % BF16_SUBLANES == 0 and tm % BF16_SUBLANES == 0 and tn % 128 == 0
        o_spec = pl.BlockSpec((pl.Element(tm), pl.Element(tn)),
                              lambda i, j, k: (pl.multiple_of(row_offset + i * tm, BF16_SUBLANES), pl.multiple_of(j * tn, 128)))
        if into is not None:
            in_specs.append(pl.BlockSpec(memory_space=pl.ANY))
            operands.append(into)
            aliases = {2: 0}
    else:
        out_shape = jax.ShapeDtypeStruct((m, n), out_dtype)
        o_spec = pl.BlockSpec((tm, tn), lambda i, j, k: (i, j))
    n_in = len(operands)

    def body(*refs):
        a_ref, b_ref, o_ref = refs[0], refs[1], refs[n_in]
        part = _dot(a_ref[...].astype(BF16), b_ref[...].astype(BF16), dims)

        def finish(acc):
            o_ref[...] = acc.astype(out_dtype)

        if nk == 1:
            finish(part)
        else:
            acc_ref = refs[-1]
            k = pl.program_id(2)

            @pl.when(k == 0)
            def _():
                acc_ref[...] = part

            @pl.when(k > 0)
            def _():
                acc_ref[...] += part

            @pl.when(k == nk - 1)
            def _():
                finish(acc_ref[...])

    return pl.pallas_call(
        body, name=name, out_shape=out_shape, grid=(m // tm, n // tn, nk), in_specs=in_specs, out_specs=o_spec,
        scratch_shapes=[pltpu.VMEM((tm, tn), F32)] if nk > 1 else [], input_output_aliases=aliases,
        compiler_params=_params(("parallel", "parallel", "arbitrary")),
    )(*operands)


def _dx_matmul(dzs, wi_t, resid, *, tm, tiles, name, whole=False, exchanges=()):
    first_tile, count = tiles
    npieces = len(dzs)
    offsets = [sum(dz.shape[1] for dz in dzs[:p]) for p in range(npieces)]
    tile = lambda i: (i + first_tile, 0)
    in_specs = [pl.BlockSpec((tm, dz.shape[1]), tile) for dz in dzs] + [_resident(wi_t.shape), pl.BlockSpec((tm, D_MODEL), tile)]

    def body(*refs):
        dz_refs, w_ref, r_ref, o_ref = refs[:npieces], refs[npieces], refs[npieces + 1], refs[npieces + 2]
        total = ALPHA * r_ref[...]
        for p in range(npieces):
            total = total + _dot(dz_refs[p][...], w_ref[offsets[p]:offsets[p] + dzs[p].shape[1], :], NN)
        o_ref[...] = total

    return _fused_call(
        body, name=name, out_shape=jax.ShapeDtypeStruct((resid.shape[0] if whole else count * tm, D_MODEL), F32), grid=(count,),
        in_specs=in_specs, out_specs=pl.BlockSpec((tm, D_MODEL), tile if whole else (lambda i: (i, 0))), scratch_shapes=[],
        operands=[*dzs, wi_t, resid], exchanges=exchanges)


def _lower_bound(lbl_ref):
    l0, l1 = lbl_ref[0:1, :], lbl_ref[1:2, :]
    mx = jnp.maximum(l0, l1)
    e0, e1 = jnp.exp(l0 - mx), jnp.exp(l1 - mx)
    return e0 / (e0 + e1)


HEAD_COLS = [slice(h * HG_DK, (h + 1) * HG_DK) for h in range(HG_HEADS)]


def _head_mean(x):
    return jnp.concatenate([jnp.broadcast_to(jnp.mean(x[:, c], axis=-1, keepdims=True), (x.shape[0], HG_DK)) for c in HEAD_COLS], axis=1)


def _triangle_sum(tri_b, x):
    p0 = x.astype(BF16)
    r1 = x - p0.astype(F32)
    p1 = r1.astype(BF16)
    p2 = (r1 - p1.astype(F32)).astype(BF16)
    return _dot(tri_b, p0) + _dot(tri_b, p1) + _dot(tri_b, p2)


def _chunk_forward(q, fl, v, lb, tril_b):
    sg = jax.nn.sigmoid(fl)
    f = lb + (1.0 - lb) * sg
    k = 1.0 - f
    b = _triangle_sum(tril_b, jnp.log(f))
    b_last = b[HG_CHUNK - 1:HG_CHUNK, :]
    eb, enb, eo = jnp.exp(b), jnp.exp(-b), jnp.exp(b_last - b)
    return sg, f, k, b_last, eb, enb, eo, q * eb, k * enb, k * eo


def _hgrn_fwd(xb, wi_t, lb_logits, gain, *, name, exchanges=()):
    s = xb.shape[0]
    t = min(256, s)
    ncs = t // HG_CHUNK

    def body(x_ref, w_ref, lbl_ref, gain_ref, z_ref, oa_ref, oraw_ref, st_ref, state):
        @pl.when(pl.program_id(0) == 0)
        def _():
            state[...] = jnp.zeros_like(state)

        z_ref[...] = _dot(x_ref[...], w_ref[...], NT)
        lb_all = _lower_bound(lbl_ref)
        row = lax.broadcasted_iota(jnp.int32, (HG_CHUNK, HG_CHUNK), 0)
        col = lax.broadcasted_iota(jnp.int32, (HG_CHUNK, HG_CHUNK), 1)
        tril = row >= col
        tril_b = tril.astype(BF16)
        gain_all = gain_ref[...]

        def chunk(i, carry):
            r = pl.ds(pl.multiple_of(i * HG_CHUNK, HG_CHUNK), HG_CHUNK)
            q, fl, v, hg = (z_ref[r, j * D_MODEL:(j + 1) * D_MODEL] for j in range(4))
            _, _, _, b_last, _, _, _, q_in, k_in, k_out = _chunk_forward(q, fl, v, lb_all, tril_b)
            q_in_b, k_in_b, k_out_b, vb = (u.astype(BF16) for u in (q_in, k_in, k_out, v))
            decay = jnp.exp(b_last)
            sts = [state[h] for h in range(HG_HEADS)]
            attn = [_dot(q_in_b[:, c], k_in_b[:, c], NT) for c in HEAD_COLS]
            inter = [_dot(q_in_b[:, c], sts[h].astype(BF16), NT) for h, c in enumerate(HEAD_COLS)]
            upd = [_dot(vb[:, c], k_out_b[:, c], TN) for c in HEAD_COLS]
            attn = [jnp.where(tril, a, 0.0).astype(BF16) for a in attn]
            outs = [_dot(attn[h], vb[:, c], NN) + inter[h] for h, c in enumerate(HEAD_COLS)]
            for h, c in enumerate(HEAD_COLS):
                st_ref[h, i] = sts[h]
                state[h] = sts[h] * decay[:, c] + upd[h]
            o = jnp.concatenate(outs, axis=1)
            oraw_ref[r, :] = o
            n = o * lax.rsqrt(_head_mean(o * o) + RMS_EPS)
            oa_ref[r, :] = (n * gain_all * (hg * jax.nn.sigmoid(hg))).astype(BF16)
            return carry

        lax.fori_loop(0, ncs, chunk, 0, unroll=True)

    tile = lambda i: (i, 0)
    return _fused_call(
        body, name=name, grid=(s // t,),
        out_shape=(jax.ShapeDtypeStruct((s, W_A), F32), jax.ShapeDtypeStruct((s, D_MODEL), BF16), jax.ShapeDtypeStruct((s, D_MODEL), F32),
                   jax.ShapeDtypeStruct((HG_HEADS, s // HG_CHUNK, HG_DK, HG_DK), F32)),
        in_specs=[pl.BlockSpec((t, D_MODEL), tile), _resident_rows(wi_t, 0, W_A), _resident((2, D_MODEL)), _resident((1, D_MODEL))],
        out_specs=(pl.BlockSpec((t, W_A), tile), pl.BlockSpec((t, D_MODEL), tile), pl.BlockSpec((t, D_MODEL), tile),
                   pl.BlockSpec((HG_HEADS, ncs, HG_DK, HG_DK), lambda i: (0, i, 0, 0))),
        scratch_shapes=[pltpu.VMEM((HG_HEADS, HG_DK, HG_DK), F32)],
        operands=[xb, wi_t, lb_logits, gain], exchanges=exchanges)


def _hgrn_bwd(za, oraw, do_a, states, lb_logits, gain, *, name, exchanges=()):
    s = za.shape[0]
    t = min(256, s)
    ncs = t // HG_CHUNK
    nt = s // t

    def body(z_ref, oraw_ref, do_ref, st_ref, lbl_ref, gain_ref, dz_ref, stats_ref, dstate):
        step = pl.program_id(0)

        @pl.when(step == 0)
        def _():
            dstate[...] = jnp.zeros_like(dstate)
            stats_ref[...] = jnp.zeros_like(stats_ref)

        lb_all = _lower_bound(lbl_ref)
        row = lax.broadcasted_iota(jnp.int32, (HG_CHUNK, HG_CHUNK), 0)
        col = lax.broadcasted_iota(jnp.int32, (HG_CHUNK, HG_CHUNK), 1)
        tril = row >= col
        tril_b = tril.astype(BF16)
        triu_b = (row <= col).astype(BF16)
        gain_all = gain_ref[...]

        def chunk(ii, carry):
            i = ncs - 1 - ii
            r = pl.ds(pl.multiple_of(i * HG_CHUNK, HG_CHUNK), HG_CHUNK)
            q, fl, v, hg = (z_ref[r, j * D_MODEL:(j + 1) * D_MODEL] for j in range(4))
            o = oraw_ref[r, :]
            doa = do_ref[r, :]
            rms = lax.rsqrt(_head_mean(o * o) + RMS_EPS)
            n = o * rms
            sgg = jax.nn.sigmoid(hg)
            silu = hg * sgg
            dhg = doa * n * gain_all * (sgg * (1.0 + hg * (1.0 - sgg)))
            dgain = jnp.sum(doa * n * silu, axis=0, keepdims=True)
            dn = doa * gain_all * silu
            do = rms * (dn - n * _head_mean(dn * n))
            sg, f, k, b_last, eb, enb, eo, q_in, k_in, k_out = _chunk_forward(q, fl, v, lb_all, tril_b)
            q_in_b, k_in_b, k_out_b, vb, dob = (u.astype(BF16) for u in (q_in, k_in, k_out, v, do))
            decay = jnp.exp(b_last)
            sts = [st_ref[h, i] for h in range(HG_HEADS)]
            dsts = [dstate[h] for h in range(HG_HEADS)]
            dsts_b = [d.astype(BF16) for d in dsts]
            heads = list(enumerate(HEAD_COLS))
            attn = [_dot(q_in_b[:, c], k_in_b[:, c], NT) for h, c in heads]
            dattn = [_dot(dob[:, c], vb[:, c], NT) for h, c in heads]
            dq_st = [_dot(dob[:, c], sts[h].astype(BF16), NN) for h, c in heads]
            dk_out = [_dot(vb[:, c], dsts_b[h], NN) for h, c in heads]
            dv_st = [_dot(k_out_b[:, c], dsts_b[h], NT) for h, c in heads]
            dst_o = [_dot(dob[:, c], q_in_b[:, c], TN) for h, c in heads]
            attn = [jnp.where(tril, a, 0.0).astype(BF16) for a in attn]
            dattn = [jnp.where(tril, a, 0.0).astype(BF16) for a in dattn]
            dq_in = jnp.concatenate([_dot(dattn[h], k_in_b[:, c], NN) + dq_st[h] for h, c in heads], axis=1)
            dk_in = jnp.concatenate([_dot(dattn[h], q_in_b[:, c], TN) for h, c in heads], axis=1)
            dv = jnp.concatenate([_dot(attn[h], dob[:, c], TN) + dv_st[h] for h, c in heads], axis=1)
            dk_out = jnp.concatenate(dk_out, axis=1)
            dst_st = jnp.concatenate([jnp.sum(dsts[h] * sts[h], axis=0, keepdims=True) for h in range(HG_HEADS)], axis=1)
            for h, c in heads:
                dstate[h] = dsts[h] * decay[:, c] + dst_o[h]
            db_last = decay * dst_st + jnp.sum(dk_out * k_out, axis=0, keepdims=True)
            db = dq_in * q_in - dk_in * k_in - dk_out * k_out
            dg = _triangle_sum(triu_b, db) + db_last
            dk = dk_in * enb + dk_out * eo
            df = dg / f - dk
            stats_ref[0:1, :] += dgain
            stats_ref[1:2, :] += jnp.sum(df * (1.0 - sg), axis=0, keepdims=True)
            dz_ref[r, 0:1024] = (dq_in * eb).astype(BF16)
            dz_ref[r, 1024:2048] = (df * (1.0 - lb_all) * sg * (1.0 - sg)).astype(BF16)
            dz_ref[r, 2048:3072] = dv.astype(BF16)
            dz_ref[r, 3072:4096] = dhg.astype(BF16)
            return carry

        lax.fori_loop(0, ncs, chunk, 0, unroll=True)

        @pl.when(step == nt - 1)
        def _():
            dl0 = stats_ref[1:2, :] * lb_all * (1.0 - lb_all)
            stats_ref[1:2, :] = dl0
            stats_ref[2:3, :] = -dl0

    rev = lambda i: (nt - 1 - i, 0)
    return _fused_call(
        body, name=name, grid=(nt,),
        out_shape=(jax.ShapeDtypeStruct((s, W_A), BF16), jax.ShapeDtypeStruct((8, D_MODEL), F32)),
        in_specs=[pl.BlockSpec((t, W_A), rev), pl.BlockSpec((t, D_MODEL), rev), pl.BlockSpec((t, D_MODEL), rev),
                  pl.BlockSpec((HG_HEADS, ncs, HG_DK, HG_DK), lambda i: (0, nt - 1 - i, 0, 0)),
                  _resident((2, D_MODEL)), _resident((1, D_MODEL))],
        out_specs=(pl.BlockSpec((t, W_A), rev), pl.BlockSpec((8, D_MODEL), lambda i: (0, 0))),
        scratch_shapes=[pltpu.VMEM((HG_HEADS, HG_DK, HG_DK), F32)],
        operands=[za, oraw, do_a, states, lb_logits, gain], exchanges=exchanges)


def _t5_bucket(n):
    max_exact = NUM_BUCKETS // 2
    nf = jnp.maximum(n, 1).astype(F32)
    large = max_exact + (jnp.log(nf / max_exact) / math.log(MAX_DISTANCE / max_exact) * (NUM_BUCKETS - max_exact)).astype(jnp.int32)
    large = jnp.minimum(large, NUM_BUCKETS - 1)
    return jnp.where(n < max_exact, n, large)


def _bias_selector():
    qi = jnp.arange(SWA_BLOCK)[:, None] + SWA_BLOCK
    kj = jnp.arange(2 * SWA_BLOCK)[None, :]
    dist = qi - kj
    band = ((dist >= 0) & (dist < SWA_WINDOW)).reshape(1, -1)
    bucket = _t5_bucket(jnp.clip(dist, 0, SWA_WINDOW - 1)).reshape(1, -1)
    onehot = ((bucket == jnp.arange(NUM_BUCKETS)[:, None]) & band).astype(F32)
    return onehot, jnp.where(band, 0.0, MASK_VALUE).astype(F32)


def _bias_table(rel_bias_t, onehot, maskrow, *, name):
    def body(rb_ref, oh_ref, mask_ref, o_ref):
        o_ref[...] = _dot(rb_ref[...], oh_ref[...], NN, HIGHEST) + mask_ref[...]

    return pl.pallas_call(body, name=name, out_shape=jax.ShapeDtypeStruct((SWA_HEADS, onehot.shape[1]), F32),
                          compiler_params=_params())(rel_bias_t, onehot, maskrow)


def _bias_grad(dbias2d, onehot, *, name):
    def body(db_ref, oh_ref, o_ref):
        o_ref[...] = _dot(db_ref[...], oh_ref[...], NT, HIGHEST)

    return pl.pallas_call(body, name=name, out_shape=jax.ShapeDtypeStruct((SWA_HEADS, NUM_BUCKETS), F32),
                          compiler_params=_params())(dbias2d, onehot)


def _swa_operands(zq_ref, kv_cur_ref, kv_prev_ref):
    q = (zq_ref[:, 0:1024] * (SWA_HEAD_DIM ** -0.5)).astype(BF16)
    kv_c = kv_cur_ref[...].astype(BF16)
    kv_p = kv_prev_ref[...].astype(BF16)
    kks = [jnp.concatenate([kv_p[:, g * 64:(g + 1) * 64], kv_c[:, g * 64:(g + 1) * 64]], axis=0) for g in range(SWA_KV_HEADS)]
    vvs = [jnp.concatenate([kv_p[:, 128 + g * 64:128 + (g + 1) * 64], kv_c[:, 128 + g * 64:128 + (g + 1) * 64]], axis=0)
           for g in range(SWA_KV_HEADS)]
    return q, kks, vvs


SWA_PART_HEADS = 8
SWA_PARTS = [(h0 // SWA_GROUP, h0) for h0 in range(0, SWA_HEADS, SWA_PART_HEADS)]


def _part_lanes(h0):
    return slice(h0 * SWA_BLOCK, (h0 + SWA_PART_HEADS) * SWA_BLOCK)


def _stack_heads(x, h0):
    return jnp.concatenate([x[:, h * SWA_HEAD_DIM:(h + 1) * SWA_HEAD_DIM] for h in range(h0, h0 + SWA_PART_HEADS)], axis=0)


def _heads_to_lanes(xt):
    pairs = []
    for j in range(0, xt.shape[1] // SWA_BLOCK, 2):
        two = jnp.concatenate([xt[:, j * SWA_BLOCK:(j + 1) * SWA_BLOCK], xt[:, (j + 1) * SWA_BLOCK:(j + 2) * SWA_BLOCK]], axis=0)
        pairs.append(two.T)
    return jnp.concatenate(pairs, axis=1)


def _swa_softmax(score_t, bias_ref, sink_ref, h0):
    sc = score_t + bias_ref[:, _part_lanes(h0)]
    sink = sink_ref[:, _part_lanes(h0)]
    m = jnp.maximum(jnp.max(sc, axis=0, keepdims=True), sink)
    e = jnp.exp(sc - m)
    e_sink = jnp.exp(sink - m)
    return e, 1.0 / (jnp.sum(e, axis=0, keepdims=True) + e_sink), e_sink


def _swa_tables(bias2d, sinks):
    bias_t = bias2d.reshape(SWA_HEADS, SWA_BLOCK, 2 * SWA_BLOCK).transpose(2, 0, 1).reshape(2 * SWA_BLOCK, SWA_HEADS * SWA_BLOCK)
    first = jnp.where(jnp.arange(2 * SWA_BLOCK)[:, None] < SWA_BLOCK, MASK_VALUE, bias_t)
    return jnp.stack([first, bias_t]), jnp.repeat(sinks, SWA_BLOCK, axis=1)


def _swa_fwd(zb, bias_tables, sink_lanes, *, name, exchanges=()):
    s = zb.shape[0]
    nb = s // SWA_BLOCK

    def body(zq_ref, kvc_ref, kvp_ref, bias_ref, sink_ref, o_ref):
        q, kks, vvs = _swa_operands(zq_ref, kvc_ref, kvp_ref)
        scores = [_dot(kks[g], _stack_heads(q, h0), NT) for g, h0 in SWA_PARTS]
        probs = []
        for score, (_, h0) in zip(scores, SWA_PARTS):
            e, inv, _ = _swa_softmax(score, bias_ref, sink_ref, h0)
            probs.append((e * inv).astype(BF16))
        outs = [_dot(vvs[g], p, TN) for p, (g, _) in zip(probs, SWA_PARTS)]
        o_ref[...] = jnp.concatenate([_heads_to_lanes(o) for o in outs], axis=1).astype(BF16)

    return _fused_call(
        body, name=name, grid=(nb,), out_shape=jax.ShapeDtypeStruct((s, D_MODEL), BF16),
        in_specs=[pl.BlockSpec((SWA_BLOCK, W_B), lambda n: (n, 0)),
                  pl.BlockSpec((SWA_BLOCK, 256), lambda n: (n, 4)),
                  pl.BlockSpec((SWA_BLOCK, 256), lambda n: (jnp.maximum(n - 1, 0), 4)),
                  pl.BlockSpec((None, 2 * SWA_BLOCK, SWA_HEADS * SWA_BLOCK), lambda n: (jnp.minimum(n, 1), 0, 0)),
                  _resident((1, SWA_HEADS * SWA_BLOCK))],
        out_specs=pl.BlockSpec((SWA_BLOCK, D_MODEL), lambda n: (n, 0)), scratch_shapes=[],
        operands=[zb, zb, zb, bias_tables, sink_lanes], exchanges=exchanges)


def _swa_bwd(zb, do_b, bias_tables, sink_lanes, *, name, exchanges=()):
    s = zb.shape[0]
    nb = s // SWA_BLOCK
    scale = SWA_HEAD_DIM ** -0.5

    def body(zq_ref, kvc_ref, kvp_ref, do_ref, bias_ref, sink_ref, dz_ref, dbias_ref, dsink_ref, carry, dsink_acc):
        step = pl.program_id(0)

        @pl.when(step == 0)
        def _():
            carry[...] = jnp.zeros_like(carry)
            dsink_acc[...] = jnp.zeros_like(dsink_acc)
            dbias_ref[...] = jnp.zeros_like(dbias_ref)

        q, kks, vvs = _swa_operands(zq_ref, kvc_ref, kvp_ref)
        do = do_ref[...].astype(BF16)
        parts = range(len(SWA_PARTS))
        q_rows = [_stack_heads(q, h0) for _, h0 in SWA_PARTS]
        do_rows = [_stack_heads(do, h0) for _, h0 in SWA_PARTS]
        scores = [_dot(kks[g], q_rows[i], NT) for i, (g, _) in enumerate(SWA_PARTS)]
        soft = [_swa_softmax(scores[i], bias_ref, sink_ref, h0) for i, (_, h0) in enumerate(SWA_PARTS)]
        dps = [_dot(vvs[g], do_rows[i], NT) for i, (g, _) in enumerate(SWA_PARTS)]
        ps, dss = [], []
        for i, (_, h0) in enumerate(SWA_PARTS):
            e, inv, e_sink = soft[i]
            p = e * inv
            delta = jnp.sum(p * dps[i], axis=0, keepdims=True)
            ds = p * (dps[i] - delta)
            dbias_ref[:, _part_lanes(h0)] += ds
            dsink_acc[:, _part_lanes(h0)] -= e_sink * inv * delta
            ps.append(p.astype(BF16))
            dss.append(ds.astype(BF16))
        dqs = [_dot(kks[g], dss[i], TN) * scale for i, (g, _) in enumerate(SWA_PARTS)]
        in_group = lambda xs, g, axis: jnp.concatenate([xs[i] for i in parts if SWA_PARTS[i][0] == g], axis=axis)
        dkks = [_dot(in_group(dss, g, 1), in_group(q_rows, g, 0), NN) for g in range(SWA_KV_HEADS)]
        dvvs = [_dot(in_group(ps, g, 1), in_group(do_rows, g, 0), NN) for g in range(SWA_KV_HEADS)]
        dkv = jnp.concatenate(dkks + dvvs, axis=1)
        dz_ref[:, 0:1024] = jnp.concatenate([_heads_to_lanes(dq) for dq in dqs], axis=1).astype(BF16)
        dz_ref[:, 1024:1280] = (dkv[SWA_BLOCK:, :] + carry[...]).astype(BF16)
        carry[...] = dkv[:SWA_BLOCK, :]

        @pl.when(step == nb - 1)
        def _():
            acc = dsink_acc[...]
            dsink_ref[...] = jnp.concatenate([jnp.sum(acc[:, h * SWA_BLOCK:(h + 1) * SWA_BLOCK], axis=1, keepdims=True)
                                              for h in range(SWA_HEADS)], axis=1)

    rev = lambda i: (nb - 1 - i, 0)
    table_shape = (2 * SWA_BLOCK, SWA_HEADS * SWA_BLOCK)
    return _fused_call(
        body, name=name, grid=(nb,),
        out_shape=(jax.ShapeDtypeStruct((s, W_B), BF16), jax.ShapeDtypeStruct(table_shape, F32), jax.ShapeDtypeStruct((1, SWA_HEADS), F32)),
        in_specs=[pl.BlockSpec((SWA_BLOCK, W_B), rev),
                  pl.BlockSpec((SWA_BLOCK, 256), lambda i: (nb - 1 - i, 4)),
                  pl.BlockSpec((SWA_BLOCK, 256), lambda i: (jnp.maximum(nb - 2 - i, 0), 4)),
                  pl.BlockSpec((SWA_BLOCK, D_MODEL), rev),
                  pl.BlockSpec((None,) + table_shape, lambda i: (jnp.minimum(nb - 1 - i, 1), 0, 0)),
                  _resident((1, SWA_HEADS * SWA_BLOCK))],
        out_specs=(pl.BlockSpec((SWA_BLOCK, W_B), rev), pl.BlockSpec(table_shape, lambda i: (0, 0)),
                   pl.BlockSpec((1, SWA_HEADS), lambda i: (0, 0))),
        scratch_shapes=[pltpu.VMEM((SWA_BLOCK, 256), F32), pltpu.VMEM((1, SWA_HEADS * SWA_BLOCK), F32)],
        operands=[zb, zb, zb, do_b, bias_tables, sink_lanes], exchanges=exchanges)


MEM_COLS = [slice(h * MEM_HEAD_DIM, (h + 1) * MEM_HEAD_DIM) for h in range(MEM_HEADS)]
MEM_VCOLS = [slice(D_MODEL + h * MEM_HEAD_DIM, D_MODEL + (h + 1) * MEM_HEAD_DIM) for h in range(MEM_HEADS)]


def _mem_probs(zc_ref, mkv_ref):
    qs = [(zc_ref[:, c] * (MEM_HEAD_DIM ** -0.5)).astype(BF16) for c in MEM_COLS]
    scores = [_dot(qs[h], mkv_ref[:, c], NT) for h, c in enumerate(MEM_COLS)]
    ps = []
    for sc in scores:
        e = jnp.exp(sc - jnp.max(sc, axis=-1, keepdims=True))
        ps.append(e / jnp.sum(e, axis=-1, keepdims=True))
    return qs, ps


def _mem_fwd(xb, wi_t, mkv, *, name):
    s = xb.shape[0]
    t = min(512, s)

    def body(x_ref, w_ref, mkv_ref, zc_ref, o_ref):
        zc_ref[...] = _dot(x_ref[...], w_ref[...], NT).astype(BF16)
        _, ps = _mem_probs(zc_ref, mkv_ref)
        ps = [p.astype(BF16) for p in ps]
        o_ref[...] = jnp.concatenate([_dot(ps[h], mkv_ref[:, vc], NN) for h, vc in enumerate(MEM_VCOLS)], axis=1).astype(BF16)

    row = pl.BlockSpec((t, D_MODEL), lambda i: (i, 0))
    return pl.pallas_call(
        body, name=name, grid=(s // t,), out_shape=(jax.ShapeDtypeStruct((s, D_MODEL), BF16),) * 2,
        in_specs=[row, _resident_rows(wi_t, W_A + W_B, W_C), _resident((MEM_LEN, 2 * D_MODEL))],
        out_specs=(row, row), compiler_params=_params(("parallel",)),
    )(xb, wi_t, mkv)


def _mem_bwd(xb, zc, do_c, mkv, *, name):
    s = zc.shape[0]
    t = min(512, s)
    nt = s // t

    def body(x_ref, zc_ref, do_ref, mkv_ref, dz_ref, dmkv_ref, gwi_ref, acc):
        @pl.when(pl.program_id(0) == 0)
        def _():
            dmkv_ref[...] = jnp.zeros_like(dmkv_ref)
            acc[...] = jnp.zeros_like(acc)

        heads = range(MEM_HEADS)
        qs, ps = _mem_probs(zc_ref, mkv_ref)
        dos = [do_ref[:, c].astype(BF16) for c in MEM_COLS]
        dps = [_dot(dos[h], mkv_ref[:, MEM_VCOLS[h]], NT) for h in heads]
        dss = [(ps[h] * (dps[h] - jnp.sum(ps[h] * dps[h], axis=-1, keepdims=True))).astype(BF16) for h in heads]
        ps = [p.astype(BF16) for p in ps]
        dz = jnp.concatenate([_dot(dss[h], mkv_ref[:, MEM_COLS[h]], NN) * (MEM_HEAD_DIM ** -0.5) for h in heads], axis=1).astype(BF16)
        dz_ref[...] = dz
        dmkv_ref[...] += jnp.concatenate([_dot(dss[h], qs[h], TN) for h in heads] + [_dot(ps[h], dos[h], TN) for h in heads], axis=1)
        acc[...] += _dot(dz, x_ref[...], TN)

        @pl.when(pl.program_id(0) == nt - 1)
        def _():
            pltpu.sync_copy(acc, gwi_ref.at[pl.ds(W_A + W_B, W_C), :])

    row = pl.BlockSpec((t, D_MODEL), lambda i: (i, 0))
    return pl.pallas_call(
        body, name=name, grid=(nt,),
        out_shape=(jax.ShapeDtypeStruct((s, D_MODEL), BF16), jax.ShapeDtypeStruct((MEM_LEN, 2 * D_MODEL), F32),
                   jax.ShapeDtypeStruct((IN_COLS, D_MODEL), F32)),
        in_specs=[row, row, row, _resident((MEM_LEN, 2 * D_MODEL))],
        out_specs=(row, pl.BlockSpec((MEM_LEN, 2 * D_MODEL), lambda i: (0, 0)), HBM),
        scratch_shapes=[pltpu.VMEM((W_C, D_MODEL), F32)],
        compiler_params=_params(("arbitrary",)),
    )(xb, zc, do_c, mkv)


def _normalize(pre):
    mu = jnp.mean(pre, axis=-1, keepdims=True)
    xc = pre - mu
    rstd = lax.rsqrt(jnp.mean(xc * xc, axis=-1, keepdims=True) + LN_EPS)
    return xc * rstd, rstd


def _layer_norm_bwd(dh, xhat, rstd, g):
    dxh = dh * g
    dpre = rstd * (dxh - jnp.mean(dxh, axis=-1, keepdims=True) - xhat * jnp.mean(dxh * xhat, axis=-1, keepdims=True))
    return dpre, jnp.sum(dh * xhat, axis=0, keepdims=True), jnp.sum(dh, axis=0, keepdims=True)


def _merge_fwd(o_a, o_b, o_c, x, wi_t, wbr, wo, *, name):
    s = x.shape[0]
    t = min(256, s)
    row = lambda w: pl.BlockSpec((t, w), lambda i: (i, 0))

    def body(oa_ref, ob_ref, oc_ref, x_ref, wg_ref, wa_ref, wb_ref, wc_ref, wo_ref, zd_ref, xhat_ref, rstd_ref, merged_ref, pa_ref, pb_ref, pc_ref):
        wbr_refs = (wa_ref, wb_ref, wc_ref)
        zd_ref[...] = _dot(x_ref[...].astype(BF16), wg_ref[...], NT)
        merged = jnp.zeros((t, D_MODEL), F32)
        for b, (o_ref, p_ref) in enumerate(((oa_ref, pa_ref), (ob_ref, pb_ref), (oc_ref, pc_ref))):
            p = _dot(o_ref[...], wbr_refs[b][...], NN)
            p_ref[...] = p.astype(BF16)
            merged = merged + jax.nn.sigmoid(zd_ref[:, b * D_MODEL:(b + 1) * D_MODEL]) * p
        merged_b = merged.astype(BF16)
        merged_ref[...] = merged_b
        xhat, rstd = _normalize(ALPHA * x_ref[...] + _dot(merged_b, wo_ref[...], NN))
        xhat_ref[...] = xhat
        rstd_ref[...] = rstd

    act = jax.ShapeDtypeStruct((s, D_MODEL), F32)
    return pl.pallas_call(
        body, name=name, grid=(s // t,),
        out_shape=(jax.ShapeDtypeStruct((s, W_D), F32), act, jax.ShapeDtypeStruct((s, 1), F32)) + (jax.ShapeDtypeStruct((s, D_MODEL), BF16),) * 4,
        in_specs=[row(D_MODEL)] * 4 + [_resident_rows(wi_t, W_A + W_B + W_C, W_D)] + [_resident((D_MODEL, D_MODEL))] * 4,
        out_specs=(row(W_D), row(D_MODEL), row(1), row(D_MODEL), row(D_MODEL), row(D_MODEL), row(D_MODEL)),
        compiler_params=_params(("parallel",)),
    )(o_a, o_b, o_c, x, wi_t, *wbr, wo)


def _merge_bwd(dpre1, zd, pa, pb, pc, o_a, o_b, o_c, merged, wbr, wo, *, name, exchanges=()):
    s = dpre1.shape[0]
    t = min(256, s)
    nt = s // t
    row = lambda w: pl.BlockSpec((t, w), lambda i: (i, 0))

    def body(dpre_ref, zd_ref, pa_ref, pb_ref, pc_ref, oa_ref, ob_ref, oc_ref, mg_ref, wa_ref, wb_ref, wc_ref, wo_ref,
             dzd_ref, doa_ref, dob_ref, doc_ref, gwa_ref, gwb_ref, gwc_ref, gwo_ref, acc):
        step = pl.program_id(0)

        @pl.when(step == 0)
        def _():
            acc[...] = jnp.zeros_like(acc)

        dpre_b = dpre_ref[...].astype(BF16)
        dmerged = _dot(dpre_b, wo_ref[...], NT)
        acc[3] += _dot(mg_ref[...], dpre_b, TN)
        branches = ((pa_ref, oa_ref, doa_ref), (pb_ref, ob_ref, dob_ref), (pc_ref, oc_ref, doc_ref))
        for b, (p_ref, o_ref, do_ref) in enumerate(branches):
            gate = jax.nn.sigmoid(zd_ref[:, b * D_MODEL:(b + 1) * D_MODEL])
            dzd_ref[:, b * D_MODEL:(b + 1) * D_MODEL] = (dmerged * p_ref[...] * gate * (1.0 - gate)).astype(BF16)
            dp = (dmerged * gate).astype(BF16)
            acc[b] += _dot(o_ref[...], dp, TN)
            do_ref[...] = _dot(dp, (wa_ref, wb_ref, wc_ref)[b][...], NT).astype(do_ref.dtype)

        @pl.when(step == nt - 1)
        def _():
            for b, gw_ref in enumerate((gwa_ref, gwb_ref, gwc_ref, gwo_ref)):
                pltpu.sync_copy(acc.at[b], gw_ref)

    act = jax.ShapeDtypeStruct((s, D_MODEL), F32)
    actb = jax.ShapeDtypeStruct((s, D_MODEL), BF16)
    gw = jax.ShapeDtypeStruct((D_MODEL, D_MODEL), F32)
    return _fused_call(
        body, name=name, grid=(nt,),
        out_shape=(jax.ShapeDtypeStruct((s, W_D), BF16), act, actb, actb, gw, gw, gw, gw),
        in_specs=[row(D_MODEL), row(W_D)] + [row(D_MODEL)] * 7 + [_resident((D_MODEL, D_MODEL))] * 4,
        out_specs=(row(W_D),) + (row(D_MODEL),) * 3 + (HBM,) * 4, scratch_shapes=[pltpu.VMEM((4, D_MODEL, D_MODEL), F32)],
        operands=[dpre1, zd, pa, pb, pc, o_a, o_b, o_c, merged, *wbr, wo], exchanges=exchanges)


def _mlp_loss(xhat1, rstd1, target, ln1_g, ln1_b, ln2_g, ln2_b, wu, wd, *, name):
    s = xhat1.shape[0]
    t = min(256, s)
    npan = wu.shape[0]
    row = lambda w: pl.BlockSpec((t, w), lambda i: (i, 0))
    vec = _resident((1, D_MODEL))

    def body(xhat_ref, rstd_ref, tgt_ref, g1_ref, b1_ref, g2_ref, b2_ref, wu_ref, wd_ref,
             dpre1_ref, dpre2_ref, h1_ref, a_ref, du_ref, stats_ref):
        @pl.when(pl.program_id(0) == 0)
        def _():
            stats_ref[...] = jnp.zeros_like(stats_ref)

        xhat1_v = xhat_ref[...]
        h1 = xhat1_v * g1_ref[...] + b1_ref[...]
        h1_b = h1.astype(BF16)
        h1_ref[...] = h1_b
        us = []
        ff = jnp.zeros((t, D_MODEL), F32)
        for j in range(npan):
            u = _dot(h1_b, wu_ref[j], NN)
            us.append(u)
            r = jnp.maximum(u, 0.0)
            a_b = (r * r).astype(BF16)
            a_ref[:, j * D_MODEL:(j + 1) * D_MODEL] = a_b
            ff = ff + _dot(a_b, wd_ref[j], NN)
        xhat2, rstd2 = _normalize(ALPHA * h1 + ff)
        err = xhat2 * g2_ref[...] + b2_ref[...] - tgt_ref[...]
        stats_ref[4:5, :] += jnp.sum(err * err, axis=0, keepdims=True)
        dpre2, dg2, db2 = _layer_norm_bwd(err * (1.0 / D_MODEL), xhat2, rstd2, g2_ref[...])
        stats_ref[0:1, :] += dg2
        stats_ref[1:2, :] += db2
        dpre2_b = dpre2.astype(BF16)
        dpre2_ref[...] = dpre2_b
        dh1 = ALPHA * dpre2
        for j in range(npan):
            du_b = (_dot(dpre2_b, wd_ref[j], NT) * (2.0 * jnp.maximum(us[j], 0.0))).astype(BF16)
            du_ref[:, j * D_MODEL:(j + 1) * D_MODEL] = du_b
            dh1 = dh1 + _dot(du_b, wu_ref[j], NT)
        dpre1, dg1, db1 = _layer_norm_bwd(dh1, xhat1_v, rstd_ref[...], g1_ref[...])
        stats_ref[2:3, :] += dg1
        stats_ref[3:4, :] += db1
        dpre1_ref[...] = dpre1

    actb = jax.ShapeDtypeStruct((s, D_MODEL), BF16)
    wide = jax.ShapeDtypeStruct((s, D_FF), BF16)
    return pl.pallas_call(
        body, name=name, grid=(s // t,),
        out_shape=(jax.ShapeDtypeStruct((s, D_MODEL), F32), actb, actb, wide, wide, jax.ShapeDtypeStruct((8, D_MODEL), F32)),
        in_specs=[row(D_MODEL), row(1), row(D_MODEL), vec, vec, vec, vec,
                  _resident((npan, D_MODEL, D_MODEL)), _resident((npan, D_MODEL, D_MODEL))],
        out_specs=(row(D_MODEL), row(D_MODEL), row(D_MODEL), row(D_FF), row(D_FF), pl.BlockSpec((8, D_MODEL), lambda i: (0, 0))),
        compiler_params=_params(("arbitrary",)),
    )(xhat1, rstd1, target, ln1_g, ln1_b, ln2_g, ln2_b, wu, wd)


BRANCH_WEIGHTS = ("w_branch_hg", "w_branch_swa", "w_branch_mem")


def _local_step(x, xb, mem, target, wi_t, late, lb_logits, gain, sinks, rel_bias, ln1_g, ln1_b, ln2_g, ln2_b, *, distributed):
    s = x.shape[0]
    tm = min(1024, s)
    tk = min(2048, s)
    memb = mem.astype(BF16)
    if distributed:
        cx, cy, cc = lax.axis_index("x"), lax.axis_index("y"), lax.axis_index("c")
        pos = jnp.stack([2 * cx + cy, cc]).astype(jnp.int32)
    gather = (lambda names: [_gather_exchange([late[k] for k in names])]) if distributed else (lambda names: [])
    to_sibling = (lambda grads: [_sibling_halves_exchange(grads)]) if distributed else (lambda grads: [])
    to_chips = (lambda sums: [_chip_partials_exchange([bf for bf, _ in sums])]) if distributed else (lambda sums: [])

    def chip_sums(names, grads, from_sibling):
        return [_add_sibling(g, o, pos, name="add_sibling_" + k) for k, g, o in zip(names, grads, from_sibling)]

    def shard_sums(names, sums, from_chips):
        return {k: _add_chips(mine, o, pos, name="add_chips_" + k) for k, (_, mine), o in zip(names, sums, from_chips)}

    zb = _mm(xb, wi_t, mode="nt", tm=tm, tn=W_B, tk=D_MODEL, name="proj_b", out_dtype=BF16, b_rows=(W_A, W_B))
    onehot, maskrow = _bias_selector()
    bias_tables, sink_lanes = _swa_tables(_bias_table(rel_bias.T, onehot, maskrow, name="bias_table"), sinks)
    (za, o_a, o_raw, states), landed = _hgrn_fwd(xb, wi_t, lb_logits, gain, name="hgrn_fwd", exchanges=gather(("w_up", "w_down", "w_mem_kv")))
    wu, wd, wmkv = landed[0] if distributed else (late["wu"], late["wd"], late["wmkv"])
    mkv = _mm(memb, wmkv, mode="nn", tm=MEM_LEN, tn=512, tk=D_MODEL, name="mem_kv", out_dtype=BF16, b_panels=True)
    o_b, landed = _swa_fwd(zb, bias_tables, sink_lanes, name="swa_fwd", exchanges=gather(BRANCH_WEIGHTS + ("w_out",)))
    if distributed:
        wbr = [wb.reshape(D_MODEL, D_MODEL) for wb in landed[0][:3]]
        wo = landed[0][3].reshape(D_MODEL, D_MODEL)
    else:
        wbr, wo = [late["wbr"][b] for b in range(3)], late["wo"]
    zc, o_c = _mem_fwd(xb, wi_t, mkv, name="mem_fwd")
    zd, xhat1, rstd1, merged, pa, pb, pc = _merge_fwd(o_a, o_b, o_c, x, wi_t, wbr, wo, name="merge_fwd")

    dpre1, dpre2, h1, act, du, ln_stats = _mlp_loss(xhat1, rstd1, target, ln1_g, ln1_b, ln2_g, ln2_b, wu, wd, name="mlp_loss")
    ffn = ("w_down", "w_up")
    g_ffn = [_mm(act, dpre2, mode="tn", tm=1024, tn=D_MODEL, tk=tk, name="grad_w_down").reshape(N_SHARDS, D_FF // N_SHARDS, D_MODEL),
             _mm(h1, du, mode="tn", tm=D_MODEL, tn=1024, tk=tk, name="grad_w_up", out_panels=True)]

    (dzd, do_a, do_b, do_c, *g_merge), landed = _merge_bwd(dpre1, zd, pa, pb, pc, o_a, o_b, o_c, merged, wbr, wo, name="merge_bwd",
                                                           exchanges=to_sibling(g_ffn))
    sums_ffn = chip_sums(ffn, g_ffn, landed[0]) if distributed else []
    dzc, dmkv, g_wi = _mem_bwd(xb, zc, do_c, mkv, name="mem_bwd")
    merge = BRANCH_WEIGHTS + ("w_out", "w_mem_kv")
    g_merge = [g.reshape(N_SHARDS, D_MODEL // N_SHARDS, D_MODEL) for g in g_merge]
    g_merge.append(_mm(memb, dmkv, mode="tn", tm=D_MODEL, tn=512, tk=MEM_LEN, name="grad_w_mem_kv", out_panels=True))
    (dza, hg_stats), landed = _hgrn_bwd(za, o_raw, do_a, states, lb_logits, gain, name="hgrn_bwd",
                                        exchanges=to_chips(sums_ffn) + to_sibling(g_merge))
    halves = shard_sums(ffn, sums_ffn, landed[0]) if distributed else {}
    sums_merge = chip_sums(merge, g_merge, landed[1]) if distributed else []
    (dzb, dbias_t, dsinks), landed = _swa_bwd(zb, do_b, bias_tables, sink_lanes, name="swa_bwd", exchanges=to_chips(sums_merge))
    if distributed:
        halves.update(shard_sums(merge, sums_merge, landed[0]))
    dbias = dbias_t.reshape(2 * SWA_BLOCK, SWA_HEADS, SWA_BLOCK).transpose(1, 2, 0).reshape(SWA_HEADS, -1)
    d_rel_bias = _bias_grad(dbias, onehot, name="bias_grad").T

    proj = ("w_in",)
    for dz, offset, nm in ((dza, 0, "grad_w_in_a"), (dzb, W_A, "grad_w_in_b"), (dzd, W_A + W_B + W_C, "grad_w_in_d")):
        g_wi = _mm(dz, xb, mode="tn", tm=dz.shape[1] if dz.shape[1] <= 1280 else 1024, tn=D_MODEL, tk=tk, name=nm,
                   rows_of=IN_COLS, row_offset=offset, into=g_wi)
    g_proj = [g_wi.reshape(N_SHARDS, IN_COLS // N_SHARDS, D_MODEL)]
    small = dict(lb_logits=hg_stats[1:3], hg_norm_gain=hg_stats[0:1], swa_sinks=dsinks, rel_bias=d_rel_bias,
                 ln1_g=ln_stats[2:3], ln1_b=ln_stats[3:4], ln2_g=ln_stats[0:1], ln2_b=ln_stats[1:2], sq_err=ln_stats[4:5])
    small_exchange = [_small_gather_exchange(_pack_small(small, name="pack_small"))] if distributed else []
    join_exchange = [_join_exchange([halves[k] for k in ffn + merge])] if distributed else []
    tx = min(512, s // 2)
    head = max(1, 3 * (s // tx) // 16)
    dx = functools.partial(_dx_matmul, [dza, dzb, dzc, dzd], wi_t, dpre1, tm=tx)
    grad_x_head, landed = dx(tiles=(0, head), name="grad_x_head", exchanges=to_sibling(g_proj))
    sums_proj = chip_sums(proj, g_proj, landed[0]) if distributed else []
    grad_x_tail, landed = dx(tiles=(head, s // tx - head), name="grad_x", whole=True,
                             exchanges=to_chips(sums_proj) + small_exchange + join_exchange)
    grad_x = lax.dynamic_update_slice(grad_x_tail, grad_x_head, (0, 0))
    if distributed:
        halves.update(shard_sums(proj, sums_proj, landed[0]))
        small = landed[1][0]
        halves.update(zip(ffn + merge, landed[2]))
    else:
        halves = dict(zip(ffn + merge + proj, g_ffn + g_merge + g_proj))
    return grad_x, halves, small


def _mesh_position():
    x, y, c = lax.axis_index("x"), lax.axis_index("y"), lax.axis_index("c")
    chips = [(1 - x, y), (x, 1 - y), (1 - x, 1 - y)]
    return x, y, c, chips


class _Exchange(NamedTuple):
    operands: list
    out_shapes: list
    n_sems: int
    start: Callable
    finish: Callable
    halfway: Optional[Callable] = None
    in_place: bool = False


def _gather_exchange(shards):
    n = len(shards)
    per = 9
    assert all(w.shape[0] % (4 * BF16_SUBLANES) == 0 for w in shards)

    def plan(ins, outs, send_sems, recv_sems):
        x, y, c, (x_nbr, y_nbr, diag) = _mesh_position()
        sibling = (x, y, 1 - c)
        slot = lambda chip: 2 * chip[0] + chip[1]

        def rows(a, chip, hc, quarter=None):
            rh = shards[a].shape[0] // 2
            if quarter is None:
                return outs[a].at[slot(chip), pl.ds(hc * rh, rh), :]
            return outs[a].at[slot(chip), pl.ds(hc * rh + quarter * (rh // 2), rh // 2), :]

        def copy(a, k, src, dst, to):
            return pltpu.make_async_remote_copy(src_ref=src, dst_ref=dst, send_sem=send_sems.at[a * per + k], recv_sem=recv_sems.at[a * per + k],
                                                device_id=to, device_id_type=MESH)

        first, from_sibling = [], []
        landed, then = [[] for _ in range(4)], [[] for _ in range(4)]
        for a in range(n):
            rh = shards[a].shape[0] // 2
            my_half = ins[a].at[pl.ds(c * rh, rh), :]
            first += [copy(a, 4, ins[a], outs[a].at[slot((x, y))], sibling),
                      copy(a, 0, my_half, rows(a, (x, y), c), (*x_nbr, c)), copy(a, 1, my_half, rows(a, (x, y), c), (*y_nbr, c))]
            landed[0].append(copy(a, 0, rows(a, x_nbr, c), rows(a, x_nbr, c), (*x_nbr, c)))
            then[0].append([copy(a, 2, rows(a, x_nbr, c, 0), rows(a, x_nbr, c, 0), (*y_nbr, c)), copy(a, 5, rows(a, x_nbr, c), rows(a, x_nbr, c), sibling)])
            landed[1].append(copy(a, 1, rows(a, y_nbr, c), rows(a, y_nbr, c), (*y_nbr, c)))
            then[1].append([copy(a, 3, rows(a, y_nbr, c, 1), rows(a, y_nbr, c, 1), (*x_nbr, c)), copy(a, 6, rows(a, y_nbr, c), rows(a, y_nbr, c), sibling)])
            landed[2].append(copy(a, 2, rows(a, diag, c, 0), rows(a, diag, c, 0), (*y_nbr, c)))
            then[2].append([copy(a, 7, rows(a, diag, c, 0), rows(a, diag, c, 0), sibling)])
            landed[3].append(copy(a, 3, rows(a, diag, c, 1), rows(a, diag, c, 1), (*x_nbr, c)))
            then[3].append([copy(a, 8, rows(a, diag, c, 1), rows(a, diag, c, 1), sibling)])
            from_sibling += [copy(a, 4, outs[a].at[slot((x, y))], outs[a].at[slot((x, y))], sibling),
                             copy(a, 5, rows(a, x_nbr, 1 - c), rows(a, x_nbr, 1 - c), sibling), copy(a, 6, rows(a, y_nbr, 1 - c), rows(a, y_nbr, 1 - c), sibling),
                             copy(a, 7, rows(a, diag, 1 - c, 0), rows(a, diag, 1 - c, 0), sibling), copy(a, 8, rows(a, diag, 1 - c, 1), rows(a, diag, 1 - c, 1), sibling)]
        return first, landed, then, from_sibling

    def start(*refs):
        first, _, _, _ = plan(*refs)
        for cp in first:
            cp.start()

    def stages(landed, then, which):
        for stage in which:
            for arrival, onward in zip(landed[stage], then[stage]):
                arrival.wait_recv()
                for cp in onward:
                    cp.start()

    def halfway(*refs):
        _, landed, then, _ = plan(*refs)
        stages(landed, then, (0, 1))

    def finish(*refs):
        first, landed, then, from_sibling = plan(*refs)
        stages(landed, then, (2, 3))
        for cp in from_sibling:
            cp.wait_recv()
        for cp in first + [cp for stage in then for onward in stage for cp in onward]:
            cp.wait_send()

    return _Exchange(list(shards), [jax.ShapeDtypeStruct((N_SHARDS,) + w.shape, w.dtype) for w in shards], per * n, start, finish, halfway)


def _sibling_halves_exchange(grads):
    n = len(grads)

    def plan(ins, outs, send_sems, recv_sems):
        x, y, c, _ = _mesh_position()
        return [pltpu.make_async_remote_copy(src_ref=ins[a].at[:, pl.ds((1 - c) * (grads[a].shape[1] // 2), grads[a].shape[1] // 2), :],
                                             dst_ref=outs[a], send_sem=send_sems.at[a], recv_sem=recv_sems.at[a],
                                             device_id=(x, y, 1 - c), device_id_type=MESH) for a in range(n)]

    def start(*refs):
        for cp in plan(*refs):
            cp.start()

    def finish(*refs):
        for cp in plan(*refs):
            cp.wait()

    return _Exchange(list(grads), [jax.ShapeDtypeStruct((g.shape[0], g.shape[1] // 2, g.shape[2]), g.dtype) for g in grads], n, start, finish)


def _chip_partials_exchange(sums):
    n = len(sums)

    def plan(ins, outs, send_sems, recv_sems):
        _, _, c, chips = _mesh_position()
        return [pltpu.make_async_remote_copy(src_ref=ins[a].at[2 * cx + cy], dst_ref=outs[a].at[k], send_sem=send_sems.at[a * 3 + k],
                                             recv_sem=recv_sems.at[a * 3 + k], device_id=(cx, cy, c), device_id_type=MESH)
                for k, (cx, cy) in enumerate(chips) for a in range(n)]

    def start(*refs):
        for cp in plan(*refs):
            cp.start()

    def finish(*refs):
        for cp in plan(*refs):
            cp.wait()

    return _Exchange(list(sums), [jax.ShapeDtypeStruct((3,) + g.shape[1:], g.dtype) for g in sums], 3 * n, start, finish)


def _fused_call(body, *, name, grid, in_specs, out_specs, out_shape, scratch_shapes, operands, exchanges=()):
    single = not isinstance(out_shape, (tuple, list))
    out_specs = [out_specs] if single else list(out_specs)
    out_shape = [out_shape] if single else list(out_shape)
    n_in, n_out, n_scr = len(in_specs), len(out_specs), len(scratch_shapes)
    x_in = [len(e.operands) for e in exchanges]
    x_out = [len(e.out_shapes) for e in exchanges]

    def wrapped(*refs):
        refs = list(refs)
        ins = refs[:n_in]
        pos = n_in
        ex_ins = []
        for k in x_in:
            ex_ins.append(refs[pos:pos + k])
            pos += k
        outs = refs[pos:pos + n_out]
        pos += n_out
        ex_outs = []
        for k in x_out:
            ex_outs.append(refs[pos:pos + k])
            pos += k
        scratch = refs[pos:pos + n_scr]
        sems = refs[pos + n_scr:]
        first, last, middle = None, None, None
        for axis, size in enumerate(grid):
            at_start, at_end, at_middle = pl.program_id(axis) == 0, pl.program_id(axis) == size - 1, pl.program_id(axis) == size // 2
            first = at_start if first is None else first & at_start
            last = at_end if last is None else last & at_end
            middle = at_middle if middle is None else middle & at_middle

        @pl.when(first)
        def _():
            for i, e in enumerate(exchanges):
                e.start(ex_ins[i], ex_outs[i], sems[2 * i], sems[2 * i + 1])

        if any(e.halfway for e in exchanges):
            @pl.when(middle)
            def _():
                for i, e in enumerate(exchanges):
                    if e.halfway:
                        e.halfway(ex_ins[i], ex_outs[i], sems[2 * i], sems[2 * i + 1])

        body(*ins, *outs, *scratch)

        @pl.when(last)
        def _():
            for i, e in enumerate(exchanges):
                e.finish(ex_ins[i], ex_outs[i], sems[2 * i], sems[2 * i + 1])

    n_x_in, n_x_out = sum(x_in), sum(x_out)
    aliases = {}
    for i, e in enumerate(exchanges):
        if e.in_place:
            aliases.update({n_in + sum(x_in[:i]) + a: n_out + sum(x_out[:i]) + a for a in range(x_in[i])})
    results = pl.pallas_call(
        wrapped if exchanges else body, name=name, grid=grid,
        in_specs=list(in_specs) + [HBM] * n_x_in,
        out_specs=out_specs + [HBM] * n_x_out,
        out_shape=out_shape + [s for e in exchanges for s in e.out_shapes], input_output_aliases=aliases,
        scratch_shapes=list(scratch_shapes) + [pltpu.SemaphoreType.DMA((e.n_sems,)) for e in exchanges for _ in range(2)],
        compiler_params=_params(("arbitrary",) * len(grid)),
    )(*operands, *[a for e in exchanges for a in e.operands])
    own = results[0] if single else tuple(results[:n_out])
    landed, pos = [], n_out
    for k in x_out:
        landed.append(list(results[pos:pos + k]))
        pos += k
    return own, landed


def _cast_bf16(x, *, name, exchanges=()):
    s, cols = x.shape
    t = min(512, s)

    def body(x_ref, o_ref):
        o_ref[...] = x_ref[...].astype(BF16)

    tile = pl.BlockSpec((t, cols), lambda i: (i, 0))
    return _fused_call(body, name=name, grid=(s // t,), in_specs=[tile], out_specs=tile, out_shape=jax.ShapeDtypeStruct((s, cols), BF16),
                       scratch_shapes=[], operands=[x], exchanges=exchanges)


ROW_TILE_MAX = 640
BF16_SUBLANES = 16


def _row_tile(rows):
    for tr in range(min(rows, ROW_TILE_MAX), 0, -1):
        if rows % tr == 0 and tr % BF16_SUBLANES == 0:
            return tr
    raise ValueError(rows)


def _add_sibling(grad, other, pos, *, name):
    p, r, cols = grad.shape
    rh = r // 2
    tr = _row_tile(rh)
    nb = rh // tr

    def body(pos_ref, g_ref, o_ref, sb_ref, mine_ref):
        total = g_ref[...] + o_ref[...]
        sb_ref[...] = total.astype(BF16)

        @pl.when(pl.program_id(1) == pos_ref[0])
        def _():
            mine_ref[...] = total

    return pl.pallas_call(
        body, name=name, out_shape=(jax.ShapeDtypeStruct((p, rh, cols), BF16), jax.ShapeDtypeStruct((rh, cols), F32)),
        grid_spec=pltpu.PrefetchScalarGridSpec(
            num_scalar_prefetch=1, grid=(nb, p),
            in_specs=[pl.BlockSpec((None, tr, cols), lambda i, j, pos_ref: (j, pos_ref[1] * nb + i, 0)),
                      pl.BlockSpec((None, tr, cols), lambda i, j, pos_ref: (j, i, 0))],
            out_specs=(pl.BlockSpec((None, tr, cols), lambda i, j, pos_ref: (j, i, 0)),
                       pl.BlockSpec((tr, cols), lambda i, j, pos_ref: (i, 0)))),
        compiler_params=_params(("parallel", "arbitrary")),
    )(pos, grad, other)


def _add_chips(mine, others, pos, *, name):
    rh, cols = mine.shape
    tr = _row_tile(rh)
    nb = rh // tr

    def body(pos_ref, s_ref, o_ref, r_ref):
        r_ref[...] = ((s_ref[...] + o_ref[0].astype(F32)) + o_ref[1].astype(F32)) + o_ref[2].astype(F32)

    return pl.pallas_call(
        body, name=name, out_shape=jax.ShapeDtypeStruct((2 * rh, cols), F32),
        grid_spec=pltpu.PrefetchScalarGridSpec(
            num_scalar_prefetch=1, grid=(nb,),
            in_specs=[pl.BlockSpec((tr, cols), lambda i, pos_ref: (i, 0)),
                      pl.BlockSpec((3, tr, cols), lambda i, pos_ref: (0, i, 0))],
            out_specs=pl.BlockSpec((tr, cols), lambda i, pos_ref: (pos_ref[1] * nb + i, 0))),
        compiler_params=_params(("parallel",)),
    )(pos, mine, others)


def _join_exchange(bufs):
    n = len(bufs)

    def copy(a, hc, ins, outs, send_sems, recv_sems):
        x, y, c, _ = _mesh_position()
        rh = bufs[a].shape[0] // 2
        rows = pl.ds(hc * rh, rh)
        return pltpu.make_async_remote_copy(src_ref=ins[a].at[rows, :], dst_ref=outs[a].at[rows, :], send_sem=send_sems.at[a],
                                            recv_sem=recv_sems.at[a], device_id=(x, y, 1 - c), device_id_type=MESH)

    def start(*refs):
        c = lax.axis_index("c")
        for a in range(n):
            copy(a, c, *refs).start()

    def finish(*refs):
        c = lax.axis_index("c")
        for a in range(n):
            copy(a, c, *refs).wait_send()
            copy(a, 1 - c, *refs).wait_recv()

    return _Exchange(list(bufs), [jax.ShapeDtypeStruct(b.shape, b.dtype) for b in bufs], n, start, finish, in_place=True)


def _join_halves(bufs, *, name):
    n = len(bufs)
    join = _join_exchange(bufs)

    def body(*refs):
        ins, outs, sems = refs[:n], refs[n:2 * n], refs[2 * n:]
        join.start(ins, outs, *sems)
        join.finish(ins, outs, *sems)

    return pl.pallas_call(
        body, name=name, out_shape=join.out_shapes, in_specs=[HBM] * n, out_specs=[HBM] * n, input_output_aliases={a: a for a in range(n)},
        scratch_shapes=[pltpu.SemaphoreType.DMA((n,)), pltpu.SemaphoreType.DMA((n,))],
    )(*bufs)


SMALL = ["lb_logits", "hg_norm_gain", "swa_sinks", "rel_bias", "ln1_g", "ln1_b", "ln2_g", "ln2_b"]
PACK_ROWS = 48
PACK_AT = dict(lb_logits=(slice(0, 2), slice(0, D_MODEL)), hg_norm_gain=(slice(2, 3), slice(0, D_MODEL)), ln1_g=(slice(3, 4), slice(0, D_MODEL)),
               ln1_b=(slice(4, 5), slice(0, D_MODEL)), ln2_g=(slice(5, 6), slice(0, D_MODEL)), ln2_b=(slice(6, 7), slice(0, D_MODEL)),
               swa_sinks=(slice(7, 8), slice(0, SWA_HEADS)), sq_err=(slice(8, 9), slice(0, D_MODEL)),
               rel_bias=(slice(16, 16 + NUM_BUCKETS), slice(0, SWA_HEADS)))


def _pack_small(grads, *, name):
    names = SMALL + ["sq_err"]

    def body(*refs):
        packed = refs[len(names)]
        packed[...] = jnp.zeros_like(packed)
        for k, g_ref in zip(names, refs):
            packed[PACK_AT[k]] = g_ref[...]

    return pl.pallas_call(body, name=name, out_shape=jax.ShapeDtypeStruct((PACK_ROWS, D_MODEL), F32), compiler_params=_params(),
                          )(*[grads[k] for k in names])


def _small_gather_exchange(packed):
    def plan(ins, outs, send_sems, recv_sems):
        x, y, c, _ = _mesh_position()
        me = 4 * x + 2 * y + c
        own = pltpu.make_async_copy(ins[0], outs[0].at[me], send_sems.at[7])
        remote = []
        for d in range(1, 8):
            dx, dy, dc = (d >> 2) & 1, (d >> 1) & 1, d & 1
            remote.append(pltpu.make_async_remote_copy(src_ref=ins[0], dst_ref=outs[0].at[me], send_sem=send_sems.at[d - 1],
                                                       recv_sem=recv_sems.at[d - 1], device_id=(x ^ dx, y ^ dy, c ^ dc), device_id_type=MESH))
        return own, remote

    def start(*refs):
        own, remote = plan(*refs)
        own.start()
        for cp in remote:
            cp.start()

    def finish(*refs):
        own, remote = plan(*refs)
        for cp in remote:
            cp.wait()
        own.wait()

    return _Exchange([packed], [jax.ShapeDtypeStruct((8,) + packed.shape, packed.dtype)], 8, start, finish)


def _adamw_small(gathered, w, m, v, *, name):
    names = SMALL
    n = len(names)

    def body(*refs):
        gathered_ref = refs[0]
        w_refs, m_refs, v_refs = (dict(zip(names, refs[1 + i * n:1 + (i + 1) * n])) for i in range(3))
        loss_ref = refs[1 + 3 * n]
        go_refs, d_refs, nm_refs, nv_refs = (dict(zip(names, refs[2 + (3 + i) * n:2 + (4 + i) * n])) for i in range(4))
        total_ref = refs[2 + 7 * n]
        total = gathered_ref[0]
        for j in range(1, 8):
            total = total + gathered_ref[j]
        total_ref[...] = total
        loss_ref[...] = (0.5 / D_MODEL) * jnp.sum(total_ref[PACK_AT["sq_err"]], axis=1, keepdims=True)
        for k in names:
            g = total_ref[PACK_AT[k]]
            go_refs[k][...] = g
            d_refs[k][...], nm_refs[k][...], nv_refs[k][...] = _adamw_math(w_refs[k][...], g, m_refs[k][...], v_refs[k][...])

    like = [jax.ShapeDtypeStruct(w[k].shape, F32) for k in names]
    results = pl.pallas_call(body, name=name, out_shape=[jax.ShapeDtypeStruct((1, 1), F32)] + like * 4,
                             scratch_shapes=[pltpu.VMEM((PACK_ROWS, D_MODEL), F32)],
                             compiler_params=_params())(gathered, *[d[k] for d in (w, m, v) for k in names])
    return results[0], {k: tuple(results[1 + i * n + j] for i in range(4)) for j, k in enumerate(names)}


def _adamw_math(w, g, m, v):
    m = ADAM_B1 * m + (1.0 - ADAM_B1) * g
    v = ADAM_B2 * v + (1.0 - ADAM_B2) * (g * g)
    m_hat = m / (1.0 - ADAM_B1 ** ADAM_STEP)
    v_hat = v / (1.0 - ADAM_B2 ** ADAM_STEP)
    delta = -ADAM_LR * (m_hat / (jnp.sqrt(v_hat) + ADAM_EPS) + ADAM_WD * w)
    return delta, m, v


def _adamw(w, g, m, v, *, name):
    _, rows, cols = w.shape
    tr = _row_tile(rows)
    blk = pl.BlockSpec((None, tr, cols), lambda i: (0, i, 0))
    flat = pl.BlockSpec((tr, cols), lambda i: (i, 0))

    def body(w_ref, g_ref, m_ref, v_ref, go_ref, d_ref, nm_ref, nv_ref):
        g_v = g_ref[...]
        go_ref[...] = g_v
        d_ref[...], nm_ref[...], nv_ref[...] = _adamw_math(w_ref[...], g_v, m_ref[...], v_ref[...])

    shape = jax.ShapeDtypeStruct((1, rows, cols), F32)
    return pl.pallas_call(body, name=name, grid=(rows // tr,), out_shape=(shape,) * 4, in_specs=[blk, flat, blk, blk], out_specs=(blk,) * 4,
                          compiler_params=_params(("parallel",)))(w, g, m, v)


WEIGHTS = ["w_in", "lb_logits", "hg_norm_gain", "swa_sinks", "rel_bias", "w_mem_kv", "w_branch_hg", "w_branch_swa", "w_branch_mem",
           "w_out", "ln1_g", "ln1_b", "w_up", "w_down", "ln2_g", "ln2_b"]
BIG = ["w_in", "w_mem_kv", "w_branch_hg", "w_branch_swa", "w_branch_mem", "w_out", "w_up", "w_down"]


def kernel(x, mem, w_in, lb_logits, hg_norm_gain, swa_sinks, rel_bias, w_mem_kv, w_branch_hg, w_branch_swa, w_branch_mem, w_out, ln1_g, ln1_b, w_up, w_down, ln2_g, ln2_b, loss_target, m_w_in, m_lb_logits, m_hg_norm_gain, m_swa_sinks, m_rel_bias, m_w_mem_kv, m_w_branch_hg, m_w_branch_swa, m_w_branch_mem, m_w_out, m_ln1_g, m_ln1_b, m_w_up, m_w_down, m_ln2_g, m_ln2_b, v_w_in, v_lb_logits, v_hg_norm_gain, v_swa_sinks, v_rel_bias, v_w_mem_kv, v_w_branch_hg, v_w_branch_swa, v_w_branch_mem, v_w_out, v_ln1_g, v_ln1_b, v_w_up, v_w_down, v_ln2_g, v_ln2_b):
    w = dict(w_in=w_in, lb_logits=lb_logits, hg_norm_gain=hg_norm_gain, swa_sinks=swa_sinks, rel_bias=rel_bias, w_mem_kv=w_mem_kv,
             w_branch_hg=w_branch_hg, w_branch_swa=w_branch_swa, w_branch_mem=w_branch_mem, w_out=w_out, ln1_g=ln1_g, ln1_b=ln1_b,
             w_up=w_up, w_down=w_down, ln2_g=ln2_g, ln2_b=ln2_b)
    m = dict(w_in=m_w_in, lb_logits=m_lb_logits, hg_norm_gain=m_hg_norm_gain, swa_sinks=m_swa_sinks, rel_bias=m_rel_bias, w_mem_kv=m_w_mem_kv,
             w_branch_hg=m_w_branch_hg, w_branch_swa=m_w_branch_swa, w_branch_mem=m_w_branch_mem, w_out=m_w_out, ln1_g=m_ln1_g, ln1_b=m_ln1_b,
             w_up=m_w_up, w_down=m_w_down, ln2_g=m_ln2_g, ln2_b=m_ln2_b)
    v = dict(w_in=v_w_in, lb_logits=v_lb_logits, hg_norm_gain=v_hg_norm_gain, swa_sinks=v_swa_sinks, rel_bias=v_rel_bias, w_mem_kv=v_w_mem_kv,
             w_branch_hg=v_w_branch_hg, w_branch_swa=v_w_branch_swa, w_branch_mem=v_w_branch_mem, w_out=v_w_out, ln1_g=v_ln1_g, ln1_b=v_ln1_b,
             w_up=v_w_up, w_down=v_w_down, ln2_g=v_ln2_g, ln2_b=v_ln2_b)
    shapes = {k: w[k].shape for k in WEIGHTS}
    for d in (w, m, v):
        d["w_in"] = d["w_in"].reshape(D_MODEL, IN_COLS // N_SHARDS).T[None]
    shards = {k: w[k].reshape(w[k].shape[-2], w[k].shape[-1]).astype(BF16) for k in BIG}
    x2d = x.reshape(x.shape[-2], D_MODEL)
    xb, ((wi4,),) = _cast_bf16(x2d, name="gather_weights", exchanges=[_gather_exchange([shards["w_in"]])])
    wi_t = wi4.reshape(IN_COLS, D_MODEL)

    grad_x, halves, small = _local_step(
        x2d, xb, mem.reshape(MEM_LEN, D_MODEL), loss_target.reshape(loss_target.shape[-2], D_MODEL),
        wi_t, shards, lb_logits, hg_norm_gain, swa_sinks, rel_bias, ln1_g, ln1_b, ln2_g, ln2_b, distributed=True)

    reduced = dict(halves)
    reduced["w_in"], = _join_halves([halves["w_in"]], name="join_halves")

    outs = {k: _adamw(w[k], reduced[k], m[k], v[k], name="adamw_" + k) for k in BIG}
    loss, small_outs = _adamw_small(small, w, m, v, name="adamw_small")
    outs.update(small_outs)
    grad_out, delta_out, m_out, v_out = ({k: outs[k][i] for k in WEIGHTS} for i in range(4))
    for out in (grad_out, delta_out, m_out, v_out):
        out["w_in"] = out["w_in"][0].T

    result = [loss.reshape(()), grad_x.reshape(x.shape)]
    for out in (grad_out, delta_out, m_out, v_out):
        result += [out[k].reshape(shapes[k]) for k in WEIGHTS]
    return tuple(result)
```

```python
import functools
import math
from typing import Callable, NamedTuple, Optional

import jax
import jax.numpy as jnp
from jax import lax
from jax.experimental import pallas as pl
from jax.experimental.pallas import tpu as pltpu

F32 = jnp.float32
BF16 = jnp.bfloat16
HIGHEST = lax.Precision.HIGHEST
MESH = pl.DeviceIdType.MESH

D_MODEL = 1024
MEM_LEN = 256
HG_HEADS = 8
HG_DK = 128
HG_CHUNK = 64
SWA_HEADS = 16
SWA_KV_HEADS = 2
SWA_GROUP = 8
SWA_HEAD_DIM = 64
SWA_BLOCK = 128
SWA_WINDOW = 128
MEM_HEADS = 4
MEM_HEAD_DIM = 256
NUM_BUCKETS = 32
MAX_DISTANCE = 128
D_FF = 4096
LN_EPS = 1e-5
RMS_EPS = 1e-6
ALPHA = 2.0 ** 0.25
W_A, W_B, W_C, W_D = 4096, 1280, 1024, 3072
IN_COLS = W_A + W_B + W_C + W_D
N_SHARDS = 4
ADAM_LR = 0.001
ADAM_B1 = 0.9
ADAM_B2 = 0.999
ADAM_EPS = 1e-08
ADAM_WD = 0.01
ADAM_STEP = 10
MASK_VALUE = -1e30
VMEM_LIMIT = 56 * 1024 * 1024
ACC_ROWS = 256

NN = ((1,), (0,))
NT = ((1,), (1,))
TN = ((0,), (0,))
HBM = pl.BlockSpec(memory_space=pltpu.HBM)


def _dot(a, b, dims=NN, precision=None):
    return lax.dot_general(a, b, (dims, ((), ())), precision=precision, preferred_element_type=F32)


def _params(sem=None):
    return pltpu.CompilerParams(dimension_semantics=sem, vmem_limit_bytes=VMEM_LIMIT)


def _resident(shape):
    zeros = (0,) * len(shape)
    return pl.BlockSpec(shape, lambda *_: zeros, pipeline_mode=pl.Buffered(1))


def _resident_rows(arr, offset, rows):
    return pl.BlockSpec((pl.Element(rows), pl.Element(arr.shape[1])), lambda *_: (offset, 0), pipeline_mode=pl.Buffered(1))


def _mm(a, b, *, mode, tm, tn, tk, name, out_dtype=F32, b_panels=False, b_rows=None, out_panels=False, rows_of=None, row_offset=0,
        into=None):
    if mode == "tn":
        kdim, m = a.shape
    else:
        m, kdim = a.shape
    if b_panels:
        n = b.shape[0] * b.shape[2]
        assert b.shape[2] == tn and mode == "nn"
    elif b_rows is not None:
        assert mode == "nt"
        b_offset, n = b_rows
    elif mode == "nt":
        n = b.shape[0]
    else:
        n = b.shape[1]
    assert m % tm == 0 and n % tn == 0 and kdim % tk == 0, (name, m, n, kdim)
    nk = kdim // tk
    dims = {"nn": NN, "nt": NT, "tn": TN}[mode]
    a_spec = pl.BlockSpec((tk, tm), lambda i, j, k: (k, i)) if mode == "tn" else pl.BlockSpec((tm, tk), lambda i, j, k: (i, k))
    if b_panels:
        b_spec = pl.BlockSpec((None, tk, tn), lambda i, j, k: (j, k, 0))
    elif b_rows is not None:
        assert b_offset % BF16_SUBLANES == 0 and tn % BF16_SUBLANES == 0 and tk % 128 == 0
        b_spec = pl.BlockSpec((pl.Element(tn), pl.Element(tk)),
                              lambda i, j, k: (pl.multiple_of(b_offset + j * tn, BF16_SUBLANES), pl.multiple_of(k * tk, 128)))
    elif mode == "nt":
        b_spec = pl.BlockSpec((tn, tk), lambda i, j, k: (j, k))
    else:
        b_spec = pl.BlockSpec((tk, tn), lambda i, j, k: (k, j))
    in_specs = [a_spec, b_spec]
    operands = [a, b]
    aliases = {}
    if out_panels:
        out_shape = jax.ShapeDtypeStruct((n // tn, m, tn), out_dtype)
        o_spec = pl.BlockSpec((None, tm, tn), lambda i, j, k: (j, i, 0))
    elif rows_of is not None:
        out_shape = jax.ShapeDtypeStruct((rows_of, n), out_dtype)
        assert row_offset % BF16_SUBLANES == 0 and tm % BF16_SUBLANES == 0 and tn % 128 == 0
        o_spec = pl.BlockSpec((pl.Element(tm), pl.Element(tn)),
                              lambda i, j, k: (pl.multiple_of(row_offset + i * tm, BF16_SUBLANES), pl.multiple_of(j * tn, 128)))
        if into is not None:
            in_specs.append(pl.BlockSpec(memory_space=pl.ANY))
            operands.append(into)
            aliases = {2: 0}
    else:
        out_shape = jax.ShapeDtypeStruct((m, n), out_dtype)
        o_spec = pl.BlockSpec((tm, tn), lambda i, j, k: (i, j))
    n_in = len(operands)

    assert nk == 1 or (out_dtype == F32 and tm % ACC_ROWS == 0), name

    def body(*refs):
        a_ref, b_ref, o_ref = refs[0], refs[1], refs[n_in]
        if nk == 1:
            o_ref[...] = _dot(a_ref[...].astype(BF16), b_ref[...].astype(BF16), dims).astype(out_dtype)
            return

        def bands():
            for r in range(0, tm, ACC_ROWS):
                a_band = a_ref[:, r:r + ACC_ROWS] if mode == "tn" else a_ref[r:r + ACC_ROWS, :]
                yield slice(r, r + ACC_ROWS), _dot(a_band.astype(BF16), b_ref[...].astype(BF16), dims)

        k = pl.program_id(2)

        @pl.when(k == 0)
        def _():
            for rows, part in bands():
                o_ref[rows, :] = part

        @pl.when(k > 0)
        def _():
            for rows, part in bands():
                o_ref[rows, :] += part

    return pl.pallas_call(
        body, name=name, out_shape=out_shape, grid=(m // tm, n // tn, nk), in_specs=in_specs, out_specs=o_spec,
        input_output_aliases=aliases, compiler_params=_params(("parallel", "parallel", "arbitrary")),
    )(*operands)


def _dx_matmul(dzs, wi_t, resid, *, tm, tiles, name, exchanges=()):
    first_tile, count = tiles
    npieces = len(dzs)
    offsets = [sum(dz.shape[1] for dz in dzs[:p]) for p in range(npieces)]
    tile = lambda i: (i + first_tile, 0)
    in_specs = [pl.BlockSpec((tm, dz.shape[1]), tile) for dz in dzs] + [_resident(wi_t.shape), pl.BlockSpec((tm, D_MODEL), tile)]

    def body(*refs):
        dz_refs, w_ref, r_ref, o_ref = refs[:npieces], refs[npieces], refs[npieces + 1], refs[npieces + 2]
        total = ALPHA * r_ref[...]
        for p in range(npieces):
            total = total + _dot(dz_refs[p][...], w_ref[offsets[p]:offsets[p] + dzs[p].shape[1], :], NN)
        o_ref[...] = total

    return _fused_call(
        body, name=name, out_shape=jax.ShapeDtypeStruct((count * tm, D_MODEL), F32), grid=(count,), in_specs=in_specs,
        out_specs=pl.BlockSpec((tm, D_MODEL), lambda i: (i, 0)), scratch_shapes=[], operands=[*dzs, wi_t, resid], exchanges=exchanges)


def _lower_bound(lbl_ref):
    l0, l1 = lbl_ref[0:1, :], lbl_ref[1:2, :]
    mx = jnp.maximum(l0, l1)
    e0, e1 = jnp.exp(l0 - mx), jnp.exp(l1 - mx)
    return e0 / (e0 + e1)


HEAD_COLS = [slice(h * HG_DK, (h + 1) * HG_DK) for h in range(HG_HEADS)]


def _head_mean(x):
    return jnp.concatenate([jnp.broadcast_to(jnp.mean(x[:, c], axis=-1, keepdims=True), (x.shape[0], HG_DK)) for c in HEAD_COLS], axis=1)


def _triangle_sum(tri_b, x):
    p0 = x.astype(BF16)
    r1 = x - p0.astype(F32)
    p1 = r1.astype(BF16)
    p2 = (r1 - p1.astype(F32)).astype(BF16)
    return _dot(tri_b, p0) + _dot(tri_b, p1) + _dot(tri_b, p2)


def _chunk_forward(q, fl, v, lb, tril_b):
    sg = jax.nn.sigmoid(fl)
    f = lb + (1.0 - lb) * sg
    k = 1.0 - f
    b = _triangle_sum(tril_b, jnp.log(f))
    b_last = b[HG_CHUNK - 1:HG_CHUNK, :]
    eb, enb, eo = jnp.exp(b), jnp.exp(-b), jnp.exp(b_last - b)
    return sg, f, k, b_last, eb, enb, eo, q * eb, k * enb, k * eo


def _hgrn_fwd(xb, wi_t, lb_logits, gain, *, name, exchanges=()):
    s = xb.shape[0]
    t = min(256, s)
    ncs = t // HG_CHUNK

    def body(x_ref, w_ref, lbl_ref, gain_ref, z_ref, oa_ref, oraw_ref, st_ref, state):
        @pl.when(pl.program_id(0) == 0)
        def _():
            state[...] = jnp.zeros_like(state)

        z_ref[...] = _dot(x_ref[...], w_ref[...], NT)
        lb_all = _lower_bound(lbl_ref)
        row = lax.broadcasted_iota(jnp.int32, (HG_CHUNK, HG_CHUNK), 0)
        col = lax.broadcasted_iota(jnp.int32, (HG_CHUNK, HG_CHUNK), 1)
        tril = row >= col
        tril_b = tril.astype(BF16)
        gain_all = gain_ref[...]

        def chunk(i, carry):
            r = pl.ds(pl.multiple_of(i * HG_CHUNK, HG_CHUNK), HG_CHUNK)
            q, fl, v, hg = (z_ref[r, j * D_MODEL:(j + 1) * D_MODEL] for j in range(4))
            _, _, _, b_last, _, _, _, q_in, k_in, k_out = _chunk_forward(q, fl, v, lb_all, tril_b)
            q_in_b, k_in_b, k_out_b, vb = (u.astype(BF16) for u in (q_in, k_in, k_out, v))
            decay = jnp.exp(b_last)
            sts = [state[h] for h in range(HG_HEADS)]
            attn = [_dot(q_in_b[:, c], k_in_b[:, c], NT) for c in HEAD_COLS]
            inter = [_dot(q_in_b[:, c], sts[h].astype(BF16), NT) for h, c in enumerate(HEAD_COLS)]
            upd = [_dot(vb[:, c], k_out_b[:, c], TN) for c in HEAD_COLS]
            attn = [jnp.where(tril, a, 0.0).astype(BF16) for a in attn]
            outs = [_dot(attn[h], vb[:, c], NN) + inter[h] for h, c in enumerate(HEAD_COLS)]
            for h, c in enumerate(HEAD_COLS):
                st_ref[h, i] = sts[h]
                state[h] = sts[h] * decay[:, c] + upd[h]
            o = jnp.concatenate(outs, axis=1)
            oraw_ref[r, :] = o
            n = o * lax.rsqrt(_head_mean(o * o) + RMS_EPS)
            oa_ref[r, :] = (n * gain_all * (hg * jax.nn.sigmoid(hg))).astype(BF16)
            return carry

        lax.fori_loop(0, ncs, chunk, 0, unroll=True)

    tile = lambda i: (i, 0)
    return _fused_call(
        body, name=name, grid=(s // t,),
        out_shape=(jax.ShapeDtypeStruct((s, W_A), F32), jax.ShapeDtypeStruct((s, D_MODEL), BF16), jax.ShapeDtypeStruct((s, D_MODEL), F32),
                   jax.ShapeDtypeStruct((HG_HEADS, s // HG_CHUNK, HG_DK, HG_DK), F32)),
        in_specs=[pl.BlockSpec((t, D_MODEL), tile), _resident_rows(wi_t, 0, W_A), _resident((2, D_MODEL)), _resident((1, D_MODEL))],
        out_specs=(pl.BlockSpec((t, W_A), tile), pl.BlockSpec((t, D_MODEL), tile), pl.BlockSpec((t, D_MODEL), tile),
                   pl.BlockSpec((HG_HEADS, ncs, HG_DK, HG_DK), lambda i: (0, i, 0, 0))),
        scratch_shapes=[pltpu.VMEM((HG_HEADS, HG_DK, HG_DK), F32)],
        operands=[xb, wi_t, lb_logits, gain], exchanges=exchanges)


def _hgrn_bwd(za, oraw, do_a, states, lb_logits, gain, *, name, exchanges=()):
    s = za.shape[0]
    t = min(256, s)
    ncs = t // HG_CHUNK
    nt = s // t

    def body(z_ref, oraw_ref, do_ref, st_ref, lbl_ref, gain_ref, dz_ref, stats_ref, dstate):
        step = pl.program_id(0)

        @pl.when(step == 0)
        def _():
            dstate[...] = jnp.zeros_like(dstate)
            stats_ref[...] = jnp.zeros_like(stats_ref)

        lb_all = _lower_bound(lbl_ref)
        row = lax.broadcasted_iota(jnp.int32, (HG_CHUNK, HG_CHUNK), 0)
        col = lax.broadcasted_iota(jnp.int32, (HG_CHUNK, HG_CHUNK), 1)
        tril = row >= col
        tril_b = tril.astype(BF16)
        triu_b = (row <= col).astype(BF16)
        gain_all = gain_ref[...]

        def chunk(ii, carry):
            i = ncs - 1 - ii
            r = pl.ds(pl.multiple_of(i * HG_CHUNK, HG_CHUNK), HG_CHUNK)
            q, fl, v, hg = (z_ref[r, j * D_MODEL:(j + 1) * D_MODEL] for j in range(4))
            o = oraw_ref[r, :]
            doa = do_ref[r, :]
            rms = lax.rsqrt(_head_mean(o * o) + RMS_EPS)
            n = o * rms
            sgg = jax.nn.sigmoid(hg)
            silu = hg * sgg
            dhg = doa * n * gain_all * (sgg * (1.0 + hg * (1.0 - sgg)))
            dgain = jnp.sum(doa * n * silu, axis=0, keepdims=True)
            dn = doa * gain_all * silu
            do = rms * (dn - n * _head_mean(dn * n))
            sg, f, k, b_last, eb, enb, eo, q_in, k_in, k_out = _chunk_forward(q, fl, v, lb_all, tril_b)
            q_in_b, k_in_b, k_out_b, vb, dob = (u.astype(BF16) for u in (q_in, k_in, k_out, v, do))
            decay = jnp.exp(b_last)
            sts = [st_ref[h, i] for h in range(HG_HEADS)]
            dsts = [dstate[h] for h in range(HG_HEADS)]
            dsts_b = [d.astype(BF16) for d in dsts]
            heads = list(enumerate(HEAD_COLS))
            attn = [_dot(q_in_b[:, c], k_in_b[:, c], NT) for h, c in heads]
            dattn = [_dot(dob[:, c], vb[:, c], NT) for h, c in heads]
            dq_st = [_dot(dob[:, c], sts[h].astype(BF16), NN) for h, c in heads]
            dk_out = [_dot(vb[:, c], dsts_b[h], NN) for h, c in heads]
            dv_st = [_dot(k_out_b[:, c], dsts_b[h], NT) for h, c in heads]
            dst_o = [_dot(dob[:, c], q_in_b[:, c], TN) for h, c in heads]
            attn = [jnp.where(tril, a, 0.0).astype(BF16) for a in attn]
            dattn = [jnp.where(tril, a, 0.0).astype(BF16) for a in dattn]
            dq_in = jnp.concatenate([_dot(dattn[h], k_in_b[:, c], NN) + dq_st[h] for h, c in heads], axis=1)
            dk_in = jnp.concatenate([_dot(dattn[h], q_in_b[:, c], TN) for h, c in heads], axis=1)
            dv = jnp.concatenate([_dot(attn[h], dob[:, c], TN) + dv_st[h] for h, c in heads], axis=1)
            dk_out = jnp.concatenate(dk_out, axis=1)
            dst_st = jnp.concatenate([jnp.sum(dsts[h] * sts[h], axis=0, keepdims=True) for h in range(HG_HEADS)], axis=1)
            for h, c in heads:
                dstate[h] = dsts[h] * decay[:, c] + dst_o[h]
            db_last = decay * dst_st + jnp.sum(dk_out * k_out, axis=0, keepdims=True)
            db = dq_in * q_in - dk_in * k_in - dk_out * k_out
            dg = _triangle_sum(triu_b, db) + db_last
            dk = dk_in * enb + dk_out * eo
            df = dg / f - dk
            stats_ref[0:1, :] += dgain
            stats_ref[1:2, :] += jnp.sum(df * (1.0 - sg), axis=0, keepdims=True)
            dz_ref[r, 0:1024] = (dq_in * eb).astype(BF16)
            dz_ref[r, 1024:2048] = (df * (1.0 - lb_all) * sg * (1.0 - sg)).astype(BF16)
            dz_ref[r, 2048:3072] = dv.astype(BF16)
            dz_ref[r, 3072:4096] = dhg.astype(BF16)
            return carry

        lax.fori_loop(0, ncs, chunk, 0, unroll=True)

        @pl.when(step == nt - 1)
        def _():
            dl0 = stats_ref[1:2, :] * lb_all * (1.0 - lb_all)
            stats_ref[1:2, :] = dl0
            stats_ref[2:3, :] = -dl0

    rev = lambda i: (nt - 1 - i, 0)
    return _fused_call(
        body, name=name, grid=(nt,),
        out_shape=(jax.ShapeDtypeStruct((s, W_A), BF16), jax.ShapeDtypeStruct((8, D_MODEL), F32)),
        in_specs=[pl.BlockSpec((t, W_A), rev), pl.BlockSpec((t, D_MODEL), rev), pl.BlockSpec((t, D_MODEL), rev),
                  pl.BlockSpec((HG_HEADS, ncs, HG_DK, HG_DK), lambda i: (0, nt - 1 - i, 0, 0)),
                  _resident((2, D_MODEL)), _resident((1, D_MODEL))],
        out_specs=(pl.BlockSpec((t, W_A), rev), pl.BlockSpec((8, D_MODEL), lambda i: (0, 0))),
        scratch_shapes=[pltpu.VMEM((HG_HEADS, HG_DK, HG_DK), F32)],
        operands=[za, oraw, do_a, states, lb_logits, gain], exchanges=exchanges)


def _t5_bucket(n):
    max_exact = NUM_BUCKETS // 2
    nf = jnp.maximum(n, 1).astype(F32)
    large = max_exact + (jnp.log(nf / max_exact) / math.log(MAX_DISTANCE / max_exact) * (NUM_BUCKETS - max_exact)).astype(jnp.int32)
    large = jnp.minimum(large, NUM_BUCKETS - 1)
    return jnp.where(n < max_exact, n, large)


def _bias_selector():
    qi = jnp.arange(SWA_BLOCK)[:, None] + SWA_BLOCK
    kj = jnp.arange(2 * SWA_BLOCK)[None, :]
    dist = qi - kj
    band = ((dist >= 0) & (dist < SWA_WINDOW)).reshape(1, -1)
    bucket = _t5_bucket(jnp.clip(dist, 0, SWA_WINDOW - 1)).reshape(1, -1)
    onehot = ((bucket == jnp.arange(NUM_BUCKETS)[:, None]) & band).astype(F32)
    return onehot, jnp.where(band, 0.0, MASK_VALUE).astype(F32)


def _bias_table(rel_bias_t, onehot, maskrow, *, name):
    def body(rb_ref, oh_ref, mask_ref, o_ref):
        o_ref[...] = _dot(rb_ref[...], oh_ref[...], NN, HIGHEST) + mask_ref[...]

    return pl.pallas_call(body, name=name, out_shape=jax.ShapeDtypeStruct((SWA_HEADS, onehot.shape[1]), F32),
                          compiler_params=_params())(rel_bias_t, onehot, maskrow)


def _bias_grad(dbias2d, onehot, *, name):
    def body(db_ref, oh_ref, o_ref):
        o_ref[...] = _dot(db_ref[...], oh_ref[...], NT, HIGHEST)

    return pl.pallas_call(body, name=name, out_shape=jax.ShapeDtypeStruct((SWA_HEADS, NUM_BUCKETS), F32),
                          compiler_params=_params())(dbias2d, onehot)


def _swa_operands(zq_ref, kv_cur_ref, kv_prev_ref):
    q = (zq_ref[:, 0:1024] * (SWA_HEAD_DIM ** -0.5)).astype(BF16)
    kv_c = kv_cur_ref[...].astype(BF16)
    kv_p = kv_prev_ref[...].astype(BF16)
    kks = [jnp.concatenate([kv_p[:, g * 64:(g + 1) * 64], kv_c[:, g * 64:(g + 1) * 64]], axis=0) for g in range(SWA_KV_HEADS)]
    vvs = [jnp.concatenate([kv_p[:, 128 + g * 64:128 + (g + 1) * 64], kv_c[:, 128 + g * 64:128 + (g + 1) * 64]], axis=0)
           for g in range(SWA_KV_HEADS)]
    return q, kks, vvs


SWA_PART_HEADS = 8
SWA_PARTS = [(h0 // SWA_GROUP, h0) for h0 in range(0, SWA_HEADS, SWA_PART_HEADS)]


def _part_lanes(h0):
    return slice(h0 * SWA_BLOCK, (h0 + SWA_PART_HEADS) * SWA_BLOCK)


def _stack_heads(x, h0):
    return jnp.concatenate([x[:, h * SWA_HEAD_DIM:(h + 1) * SWA_HEAD_DIM] for h in range(h0, h0 + SWA_PART_HEADS)], axis=0)


def _heads_to_lanes(xt):
    pairs = []
    for j in range(0, xt.shape[1] // SWA_BLOCK, 2):
        two = jnp.concatenate([xt[:, j * SWA_BLOCK:(j + 1) * SWA_BLOCK], xt[:, (j + 1) * SWA_BLOCK:(j + 2) * SWA_BLOCK]], axis=0)
        pairs.append(two.T)
    return jnp.concatenate(pairs, axis=1)


def _swa_softmax(score_t, bias_ref, sink_ref, h0):
    sc = score_t + bias_ref[:, _part_lanes(h0)]
    sink = sink_ref[:, _part_lanes(h0)]
    m = jnp.maximum(jnp.max(sc, axis=0, keepdims=True), sink)
    e = jnp.exp(sc - m)
    e_sink = jnp.exp(sink - m)
    return e, 1.0 / (jnp.sum(e, axis=0, keepdims=True) + e_sink), e_sink


def _swa_tables(bias2d, sinks):
    bias_t = bias2d.reshape(SWA_HEADS, SWA_BLOCK, 2 * SWA_BLOCK).transpose(2, 0, 1).reshape(2 * SWA_BLOCK, SWA_HEADS * SWA_BLOCK)
    first = jnp.where(jnp.arange(2 * SWA_BLOCK)[:, None] < SWA_BLOCK, MASK_VALUE, bias_t)
    return jnp.stack([first, bias_t]), jnp.repeat(sinks, SWA_BLOCK, axis=1)


def _swa_fwd(zb, bias_tables, sink_lanes, *, name, exchanges=()):
    s = zb.shape[0]
    nb = s // SWA_BLOCK

    def body(zq_ref, kvc_ref, kvp_ref, bias_ref, sink_ref, o_ref):
        q, kks, vvs = _swa_operands(zq_ref, kvc_ref, kvp_ref)
        scores = [_dot(kks[g], _stack_heads(q, h0), NT) for g, h0 in SWA_PARTS]
        probs = []
        for score, (_, h0) in zip(scores, SWA_PARTS):
            e, inv, _ = _swa_softmax(score, bias_ref, sink_ref, h0)
            probs.append((e * inv).astype(BF16))
        outs = [_dot(vvs[g], p, TN) for p, (g, _) in zip(probs, SWA_PARTS)]
        o_ref[...] = jnp.concatenate([_heads_to_lanes(o) for o in outs], axis=1).astype(BF16)

    return _fused_call(
        body, name=name, grid=(nb,), out_shape=jax.ShapeDtypeStruct((s, D_MODEL), BF16),
        in_specs=[pl.BlockSpec((SWA_BLOCK, W_B), lambda n: (n, 0)),
                  pl.BlockSpec((SWA_BLOCK, 256), lambda n: (n, 4)),
                  pl.BlockSpec((SWA_BLOCK, 256), lambda n: (jnp.maximum(n - 1, 0), 4)),
                  pl.BlockSpec((None, 2 * SWA_BLOCK, SWA_HEADS * SWA_BLOCK), lambda n: (jnp.minimum(n, 1), 0, 0)),
                  _resident((1, SWA_HEADS * SWA_BLOCK))],
        out_specs=pl.BlockSpec((SWA_BLOCK, D_MODEL), lambda n: (n, 0)), scratch_shapes=[],
        operands=[zb, zb, zb, bias_tables, sink_lanes], exchanges=exchanges)


def _swa_bwd(zb, do_b, bias_tables, sink_lanes, *, name, exchanges=()):
    s = zb.shape[0]
    nb = s // SWA_BLOCK
    scale = SWA_HEAD_DIM ** -0.5

    def body(zq_ref, kvc_ref, kvp_ref, do_ref, bias_ref, sink_ref, dz_ref, dbias_ref, dsink_ref, carry, dsink_acc):
        step = pl.program_id(0)

        @pl.when(step == 0)
        def _():
            carry[...] = jnp.zeros_like(carry)
            dsink_acc[...] = jnp.zeros_like(dsink_acc)
            dbias_ref[...] = jnp.zeros_like(dbias_ref)

        q, kks, vvs = _swa_operands(zq_ref, kvc_ref, kvp_ref)
        do = do_ref[...].astype(BF16)
        parts = range(len(SWA_PARTS))
        q_rows = [_stack_heads(q, h0) for _, h0 in SWA_PARTS]
        do_rows = [_stack_heads(do, h0) for _, h0 in SWA_PARTS]
        scores = [_dot(kks[g], q_rows[i], NT) for i, (g, _) in enumerate(SWA_PARTS)]
        soft = [_swa_softmax(scores[i], bias_ref, sink_ref, h0) for i, (_, h0) in enumerate(SWA_PARTS)]
        dps = [_dot(vvs[g], do_rows[i], NT) for i, (g, _) in enumerate(SWA_PARTS)]
        ps, dss = [], []
        for i, (_, h0) in enumerate(SWA_PARTS):
            e, inv, e_sink = soft[i]
            p = e * inv
            delta = jnp.sum(p * dps[i], axis=0, keepdims=True)
            ds = p * (dps[i] - delta)
            dbias_ref[:, _part_lanes(h0)] += ds
            dsink_acc[:, _part_lanes(h0)] -= e_sink * inv * delta
            ps.append(p.astype(BF16))
            dss.append(ds.astype(BF16))
        dqs = [_dot(kks[g], dss[i], TN) * scale for i, (g, _) in enumerate(SWA_PARTS)]
        in_group = lambda xs, g, axis: jnp.concatenate([xs[i] for i in parts if SWA_PARTS[i][0] == g], axis=axis)
        dkks = [_dot(in_group(dss, g, 1), in_group(q_rows, g, 0), NN) for g in range(SWA_KV_HEADS)]
        dvvs = [_dot(in_group(ps, g, 1), in_group(do_rows, g, 0), NN) for g in range(SWA_KV_HEADS)]
        dkv = jnp.concatenate(dkks + dvvs, axis=1)
        dz_ref[:, 0:1024] = jnp.concatenate([_heads_to_lanes(dq) for dq in dqs], axis=1).astype(BF16)
        dz_ref[:, 1024:1280] = (dkv[SWA_BLOCK:, :] + carry[...]).astype(BF16)
        carry[...] = dkv[:SWA_BLOCK, :]

        @pl.when(step == nb - 1)
        def _():
            acc = dsink_acc[...]
            dsink_ref[...] = jnp.concatenate([jnp.sum(acc[:, h * SWA_BLOCK:(h + 1) * SWA_BLOCK], axis=1, keepdims=True)
                                              for h in range(SWA_HEADS)], axis=1)

    rev = lambda i: (nb - 1 - i, 0)
    table_shape = (2 * SWA_BLOCK, SWA_HEADS * SWA_BLOCK)
    return _fused_call(
        body, name=name, grid=(nb,),
        out_shape=(jax.ShapeDtypeStruct((s, W_B), BF16), jax.ShapeDtypeStruct(table_shape, F32), jax.ShapeDtypeStruct((1, SWA_HEADS), F32)),
        in_specs=[pl.BlockSpec((SWA_BLOCK, W_B), rev),
                  pl.BlockSpec((SWA_BLOCK, 256), lambda i: (nb - 1 - i, 4)),
                  pl.BlockSpec((SWA_BLOCK, 256), lambda i: (jnp.maximum(nb - 2 - i, 0), 4)),
                  pl.BlockSpec((SWA_BLOCK, D_MODEL), rev),
                  pl.BlockSpec((None,) + table_shape, lambda i: (jnp.minimum(nb - 1 - i, 1), 0, 0)),
                  _resident((1, SWA_HEADS * SWA_BLOCK))],
        out_specs=(pl.BlockSpec((SWA_BLOCK, W_B), rev), pl.BlockSpec(table_shape, lambda i: (0, 0)),
                   pl.BlockSpec((1, SWA_HEADS), lambda i: (0, 0))),
        scratch_shapes=[pltpu.VMEM((SWA_BLOCK, 256), F32), pltpu.VMEM((1, SWA_HEADS * SWA_BLOCK), F32)],
        operands=[zb, zb, zb, do_b, bias_tables, sink_lanes], exchanges=exchanges)


MEM_COLS = [slice(h * MEM_HEAD_DIM, (h + 1) * MEM_HEAD_DIM) for h in range(MEM_HEADS)]
MEM_VCOLS = [slice(D_MODEL + h * MEM_HEAD_DIM, D_MODEL + (h + 1) * MEM_HEAD_DIM) for h in range(MEM_HEADS)]


def _mem_probs(zc_ref, mkv_ref):
    qs = [(zc_ref[:, c] * (MEM_HEAD_DIM ** -0.5)).astype(BF16) for c in MEM_COLS]
    scores = [_dot(qs[h], mkv_ref[:, c], NT) for h, c in enumerate(MEM_COLS)]
    ps = []
    for sc in scores:
        e = jnp.exp(sc - jnp.max(sc, axis=-1, keepdims=True))
        ps.append(e / jnp.sum(e, axis=-1, keepdims=True))
    return qs, ps


def _mem_fwd(xb, wi_t, mkv, *, name):
    s = xb.shape[0]
    t = min(512, s)

    def body(x_ref, w_ref, mkv_ref, zc_ref, o_ref):
        zc_ref[...] = _dot(x_ref[...], w_ref[...], NT).astype(BF16)
        _, ps = _mem_probs(zc_ref, mkv_ref)
        ps = [p.astype(BF16) for p in ps]
        o_ref[...] = jnp.concatenate([_dot(ps[h], mkv_ref[:, vc], NN) for h, vc in enumerate(MEM_VCOLS)], axis=1).astype(BF16)

    row = pl.BlockSpec((t, D_MODEL), lambda i: (i, 0))
    return pl.pallas_call(
        body, name=name, grid=(s // t,), out_shape=(jax.ShapeDtypeStruct((s, D_MODEL), BF16),) * 2,
        in_specs=[row, _resident_rows(wi_t, W_A + W_B, W_C), _resident((MEM_LEN, 2 * D_MODEL))],
        out_specs=(row, row), compiler_params=_params(("parallel",)),
    )(xb, wi_t, mkv)


def _mem_bwd(xb, zc, do_c, mkv, *, name):
    s = zc.shape[0]
    t = min(512, s)
    nt = s // t

    def body(x_ref, zc_ref, do_ref, mkv_ref, dz_ref, dmkv_ref, gwi_ref, acc):
        @pl.when(pl.program_id(0) == 0)
        def _():
            dmkv_ref[...] = jnp.zeros_like(dmkv_ref)
            acc[...] = jnp.zeros_like(acc)

        heads = range(MEM_HEADS)
        qs, ps = _mem_probs(zc_ref, mkv_ref)
        dos = [do_ref[:, c].astype(BF16) for c in MEM_COLS]
        dps = [_dot(dos[h], mkv_ref[:, MEM_VCOLS[h]], NT) for h in heads]
        dss = [(ps[h] * (dps[h] - jnp.sum(ps[h] * dps[h], axis=-1, keepdims=True))).astype(BF16) for h in heads]
        ps = [p.astype(BF16) for p in ps]
        dz = jnp.concatenate([_dot(dss[h], mkv_ref[:, MEM_COLS[h]], NN) * (MEM_HEAD_DIM ** -0.5) for h in heads], axis=1).astype(BF16)
        dz_ref[...] = dz
        dmkv_ref[...] += jnp.concatenate([_dot(dss[h], qs[h], TN) for h in heads] + [_dot(ps[h], dos[h], TN) for h in heads], axis=1)
        acc[...] += _dot(dz, x_ref[...], TN)

        @pl.when(pl.program_id(0) == nt - 1)
        def _():
            pltpu.sync_copy(acc, gwi_ref.at[pl.ds(W_A + W_B, W_C), :])

    row = pl.BlockSpec((t, D_MODEL), lambda i: (i, 0))
    return pl.pallas_call(
        body, name=name, grid=(nt,),
        out_shape=(jax.ShapeDtypeStruct((s, D_MODEL), BF16), jax.ShapeDtypeStruct((MEM_LEN, 2 * D_MODEL), F32),
                   jax.ShapeDtypeStruct((IN_COLS, D_MODEL), F32)),
        in_specs=[row, row, row, _resident((MEM_LEN, 2 * D_MODEL))],
        out_specs=(row, pl.BlockSpec((MEM_LEN, 2 * D_MODEL), lambda i: (0, 0)), HBM),
        scratch_shapes=[pltpu.VMEM((W_C, D_MODEL), F32)],
        compiler_params=_params(("arbitrary",)),
    )(xb, zc, do_c, mkv)


def _normalize(pre):
    mu = jnp.mean(pre, axis=-1, keepdims=True)
    xc = pre - mu
    rstd = lax.rsqrt(jnp.mean(xc * xc, axis=-1, keepdims=True) + LN_EPS)
    return xc * rstd, rstd


def _layer_norm_bwd(dh, xhat, rstd, g):
    dxh = dh * g
    dpre = rstd * (dxh - jnp.mean(dxh, axis=-1, keepdims=True) - xhat * jnp.mean(dxh * xhat, axis=-1, keepdims=True))
    return dpre, jnp.sum(dh * xhat, axis=0, keepdims=True), jnp.sum(dh, axis=0, keepdims=True)


def _merge_fwd(o_a, o_b, o_c, x, wi_t, wbr, wo, *, name):
    s = x.shape[0]
    t = min(256, s)
    row = lambda w: pl.BlockSpec((t, w), lambda i: (i, 0))

    def body(oa_ref, ob_ref, oc_ref, x_ref, wg_ref, wa_ref, wb_ref, wc_ref, wo_ref, zd_ref, xhat_ref, rstd_ref, merged_ref, pa_ref, pb_ref, pc_ref):
        wbr_refs = (wa_ref, wb_ref, wc_ref)
        zd_ref[...] = _dot(x_ref[...].astype(BF16), wg_ref[...], NT)
        merged = jnp.zeros((t, D_MODEL), F32)
        for b, (o_ref, p_ref) in enumerate(((oa_ref, pa_ref), (ob_ref, pb_ref), (oc_ref, pc_ref))):
            p = _dot(o_ref[...], wbr_refs[b][...], NN)
            p_ref[...] = p.astype(BF16)
            merged = merged + jax.nn.sigmoid(zd_ref[:, b * D_MODEL:(b + 1) * D_MODEL]) * p
        merged_b = merged.astype(BF16)
        merged_ref[...] = merged_b
        xhat, rstd = _normalize(ALPHA * x_ref[...] + _dot(merged_b, wo_ref[...], NN))
        xhat_ref[...] = xhat
        rstd_ref[...] = rstd

    act = jax.ShapeDtypeStruct((s, D_MODEL), F32)
    return pl.pallas_call(
        body, name=name, grid=(s // t,),
        out_shape=(jax.ShapeDtypeStruct((s, W_D), F32), act, jax.ShapeDtypeStruct((s, 1), F32)) + (jax.ShapeDtypeStruct((s, D_MODEL), BF16),) * 4,
        in_specs=[row(D_MODEL)] * 4 + [_resident_rows(wi_t, W_A + W_B + W_C, W_D)] + [_resident((D_MODEL, D_MODEL))] * 4,
        out_specs=(row(W_D), row(D_MODEL), row(1), row(D_MODEL), row(D_MODEL), row(D_MODEL), row(D_MODEL)),
        compiler_params=_params(("parallel",)),
    )(o_a, o_b, o_c, x, wi_t, *wbr, wo)


def _merge_bwd(dpre1, zd, pa, pb, pc, o_a, o_b, o_c, merged, wbr, wo, *, name, exchanges=()):
    s = dpre1.shape[0]
    t = min(256, s)
    nt = s // t
    row = lambda w: pl.BlockSpec((t, w), lambda i: (i, 0))

    def body(dpre_ref, zd_ref, pa_ref, pb_ref, pc_ref, oa_ref, ob_ref, oc_ref, mg_ref, wa_ref, wb_ref, wc_ref, wo_ref,
             dzd_ref, doa_ref, dob_ref, doc_ref, gwa_ref, gwb_ref, gwc_ref, gwo_ref, acc):
        step = pl.program_id(0)

        @pl.when(step == 0)
        def _():
            acc[...] = jnp.zeros_like(acc)

        dpre_b = dpre_ref[...].astype(BF16)
        dmerged = _dot(dpre_b, wo_ref[...], NT)
        acc[3] += _dot(mg_ref[...], dpre_b, TN)
        branches = ((pa_ref, oa_ref, doa_ref), (pb_ref, ob_ref, dob_ref), (pc_ref, oc_ref, doc_ref))
        for b, (p_ref, o_ref, do_ref) in enumerate(branches):
            gate = jax.nn.sigmoid(zd_ref[:, b * D_MODEL:(b + 1) * D_MODEL])
            dzd_ref[:, b * D_MODEL:(b + 1) * D_MODEL] = (dmerged * p_ref[...] * gate * (1.0 - gate)).astype(BF16)
            dp = (dmerged * gate).astype(BF16)
            acc[b] += _dot(o_ref[...], dp, TN)
            do_ref[...] = _dot(dp, (wa_ref, wb_ref, wc_ref)[b][...], NT).astype(do_ref.dtype)

        @pl.when(step == nt - 1)
        def _():
            for b, gw_ref in enumerate((gwa_ref, gwb_ref, gwc_ref, gwo_ref)):
                pltpu.sync_copy(acc.at[b], gw_ref)

    act = jax.ShapeDtypeStruct((s, D_MODEL), F32)
    actb = jax.ShapeDtypeStruct((s, D_MODEL), BF16)
    gw = jax.ShapeDtypeStruct((D_MODEL, D_MODEL), F32)
    return _fused_call(
        body, name=name, grid=(nt,),
        out_shape=(jax.ShapeDtypeStruct((s, W_D), BF16), act, actb, actb, gw, gw, gw, gw),
        in_specs=[row(D_MODEL), row(W_D)] + [row(D_MODEL)] * 7 + [_resident((D_MODEL, D_MODEL))] * 4,
        out_specs=(row(W_D),) + (row(D_MODEL),) * 3 + (HBM,) * 4, scratch_shapes=[pltpu.VMEM((4, D_MODEL, D_MODEL), F32)],
        operands=[dpre1, zd, pa, pb, pc, o_a, o_b, o_c, merged, *wbr, wo], exchanges=exchanges)


def _mlp_loss(xhat1, rstd1, target, ln1_g, ln1_b, ln2_g, ln2_b, wu, wd, *, name):
    s = xhat1.shape[0]
    t = min(256, s)
    npan = wu.shape[0]
    row = lambda w: pl.BlockSpec((t, w), lambda i: (i, 0))
    vec = _resident((1, D_MODEL))

    def body(xhat_ref, rstd_ref, tgt_ref, g1_ref, b1_ref, g2_ref, b2_ref, wu_ref, wd_ref,
             dpre1_ref, dpre2_ref, h1_ref, a_ref, du_ref, stats_ref):
        @pl.when(pl.program_id(0) == 0)
        def _():
            stats_ref[...] = jnp.zeros_like(stats_ref)

        xhat1_v = xhat_ref[...]
        h1 = xhat1_v * g1_ref[...] + b1_ref[...]
        h1_b = h1.astype(BF16)
        h1_ref[...] = h1_b
        us = []
        ff = jnp.zeros((t, D_MODEL), F32)
        for j in range(npan):
            u = _dot(h1_b, wu_ref[j], NN)
            us.append(u)
            r = jnp.maximum(u, 0.0)
            a_b = (r * r).astype(BF16)
            a_ref[:, j * D_MODEL:(j + 1) * D_MODEL] = a_b
            ff = ff + _dot(a_b, wd_ref[j], NN)
        xhat2, rstd2 = _normalize(ALPHA * h1 + ff)
        err = xhat2 * g2_ref[...] + b2_ref[...] - tgt_ref[...]
        stats_ref[4:5, :] += jnp.sum(err * err, axis=0, keepdims=True)
        dpre2, dg2, db2 = _layer_norm_bwd(err * (1.0 / D_MODEL), xhat2, rstd2, g2_ref[...])
        stats_ref[0:1, :] += dg2
        stats_ref[1:2, :] += db2
        dpre2_b = dpre2.astype(BF16)
        dpre2_ref[...] = dpre2_b
        dh1 = ALPHA * dpre2
        for j in range(npan):
            du_b = (_dot(dpre2_b, wd_ref[j], NT) * (2.0 * jnp.maximum(us[j], 0.0))).astype(BF16)
            du_ref[:, j * D_MODEL:(j + 1) * D_MODEL] = du_b
            dh1 = dh1 + _dot(du_b, wu_ref[j], NT)
        dpre1, dg1, db1 = _layer_norm_bwd(dh1, xhat1_v, rstd_ref[...], g1_ref[...])
        stats_ref[2:3, :] += dg1
        stats_ref[3:4, :] += db1
        dpre1_ref[...] = dpre1

    actb = jax.ShapeDtypeStruct((s, D_MODEL), BF16)
    wide = jax.ShapeDtypeStruct((s, D_FF), BF16)
    return pl.pallas_call(
        body, name=name, grid=(s // t,),
        out_shape=(jax.ShapeDtypeStruct((s, D_MODEL), F32), actb, actb, wide, wide, jax.ShapeDtypeStruct((8, D_MODEL), F32)),
        in_specs=[row(D_MODEL), row(1), row(D_MODEL), vec, vec, vec, vec,
                  _resident((npan, D_MODEL, D_MODEL)), _resident((npan, D_MODEL, D_MODEL))],
        out_specs=(row(D_MODEL), row(D_MODEL), row(D_MODEL), row(D_FF), row(D_FF), pl.BlockSpec((8, D_MODEL), lambda i: (0, 0))),
        compiler_params=_params(("arbitrary",)),
    )(xhat1, rstd1, target, ln1_g, ln1_b, ln2_g, ln2_b, wu, wd)


BRANCH_WEIGHTS = ("w_branch_hg", "w_branch_swa", "w_branch_mem")


def _local_step(x, xb, mem, target, wi_t, late, lb_logits, gain, sinks, rel_bias, ln1_g, ln1_b, ln2_g, ln2_b, *, distributed):
    s = x.shape[0]
    tm = min(1024, s)
    tk = min(2048, s)
    memb = mem.astype(BF16)
    if distributed:
        cx, cy, cc = lax.axis_index("x"), lax.axis_index("y"), lax.axis_index("c")
        pos = jnp.stack([2 * cx + cy, cc]).astype(jnp.int32)
    gather = (lambda names: [_gather_exchange([late[k] for k in names])]) if distributed else (lambda names: [])
    to_sibling = (lambda grads: [_sibling_halves_exchange(grads)]) if distributed else (lambda grads: [])
    to_chips = (lambda sums: [_chip_partials_exchange([bf for bf, _ in sums])]) if distributed else (lambda sums: [])

    def chip_sums(names, grads, from_sibling):
        return [_add_sibling(g, o, pos, name="add_sibling_" + k) for k, g, o in zip(names, grads, from_sibling)]

    def shard_sums(names, sums, from_chips):
        return {k: _add_chips(mine, o, pos, name="add_chips_" + k) for k, (_, mine), o in zip(names, sums, from_chips)}

    zb = _mm(xb, wi_t, mode="nt", tm=tm, tn=W_B, tk=D_MODEL, name="proj_b", out_dtype=BF16, b_rows=(W_A, W_B))
    onehot, maskrow = _bias_selector()
    bias_tables, sink_lanes = _swa_tables(_bias_table(rel_bias.T, onehot, maskrow, name="bias_table"), sinks)
    (za, o_a, o_raw, states), landed = _hgrn_fwd(xb, wi_t, lb_logits, gain, name="hgrn_fwd", exchanges=gather(("w_up", "w_down", "w_mem_kv")))
    wu, wd, wmkv = landed[0] if distributed else (late["wu"], late["wd"], late["wmkv"])
    mkv = _mm(memb, wmkv, mode="nn", tm=MEM_LEN, tn=512, tk=D_MODEL, name="mem_kv", out_dtype=BF16, b_panels=True)
    o_b, landed = _swa_fwd(zb, bias_tables, sink_lanes, name="swa_fwd", exchanges=gather(BRANCH_WEIGHTS + ("w_out",)))
    if distributed:
        wbr = [wb.reshape(D_MODEL, D_MODEL) for wb in landed[0][:3]]
        wo = landed[0][3].reshape(D_MODEL, D_MODEL)
    else:
        wbr, wo = [late["wbr"][b] for b in range(3)], late["wo"]
    zc, o_c = _mem_fwd(xb, wi_t, mkv, name="mem_fwd")
    zd, xhat1, rstd1, merged, pa, pb, pc = _merge_fwd(o_a, o_b, o_c, x, wi_t, wbr, wo, name="merge_fwd")

    dpre1, dpre2, h1, act, du, ln_stats = _mlp_loss(xhat1, rstd1, target, ln1_g, ln1_b, ln2_g, ln2_b, wu, wd, name="mlp_loss")
    ffn = ("w_down", "w_up")
    g_ffn = [_mm(act, dpre2, mode="tn", tm=1024, tn=D_MODEL, tk=tk, name="grad_w_down").reshape(N_SHARDS, D_FF // N_SHARDS, D_MODEL),
             _mm(h1, du, mode="tn", tm=D_MODEL, tn=1024, tk=tk, name="grad_w_up", out_panels=True)]

    (dzd, do_a, do_b, do_c, *g_merge), landed = _merge_bwd(dpre1, zd, pa, pb, pc, o_a, o_b, o_c, merged, wbr, wo, name="merge_bwd",
                                                           exchanges=to_sibling(g_ffn))
    sums_ffn = chip_sums(ffn, g_ffn, landed[0]) if distributed else []
    dzc, dmkv, g_wi = _mem_bwd(xb, zc, do_c, mkv, name="mem_bwd")
    merge = BRANCH_WEIGHTS + ("w_out", "w_mem_kv")
    g_merge = [g.reshape(N_SHARDS, D_MODEL // N_SHARDS, D_MODEL) for g in g_merge]
    g_merge.append(_mm(memb, dmkv, mode="tn", tm=D_MODEL, tn=512, tk=MEM_LEN, name="grad_w_mem_kv", out_panels=True))
    (dza, hg_stats), landed = _hgrn_bwd(za, o_raw, do_a, states, lb_logits, gain, name="hgrn_bwd",
                                        exchanges=to_chips(sums_ffn) + to_sibling(g_merge))
    halves = shard_sums(ffn, sums_ffn, landed[0]) if distributed else {}
    sums_merge = chip_sums(merge, g_merge, landed[1]) if distributed else []
    (dzb, dbias_t, dsinks), landed = _swa_bwd(zb, do_b, bias_tables, sink_lanes, name="swa_bwd", exchanges=to_chips(sums_merge))
    if distributed:
        halves.update(shard_sums(merge, sums_merge, landed[0]))
    dbias = dbias_t.reshape(2 * SWA_BLOCK, SWA_HEADS, SWA_BLOCK).transpose(1, 2, 0).reshape(SWA_HEADS, -1)
    d_rel_bias = _bias_grad(dbias, onehot, name="bias_grad").T

    proj = ("w_in",)
    for dz, offset, nm in ((dza, 0, "grad_w_in_a"), (dzb, W_A, "grad_w_in_b"), (dzd, W_A + W_B + W_C, "grad_w_in_d")):
        g_wi = _mm(dz, xb, mode="tn", tm=dz.shape[1] if dz.shape[1] <= 1280 else 1024, tn=D_MODEL, tk=tk, name=nm,
                   rows_of=IN_COLS, row_offset=offset, into=g_wi)
    g_proj = [g_wi.reshape(N_SHARDS, IN_COLS // N_SHARDS, D_MODEL)]
    small = dict(lb_logits=hg_stats[1:3], hg_norm_gain=hg_stats[0:1], swa_sinks=dsinks, rel_bias=d_rel_bias,
                 ln1_g=ln_stats[2:3], ln1_b=ln_stats[3:4], ln2_g=ln_stats[0:1], ln2_b=ln_stats[1:2], sq_err=ln_stats[4:5])
    small_exchange = [_small_gather_exchange(_pack_small(small, name="pack_small"))] if distributed else []
    join_exchange = [_join_exchange([halves[k] for k in ffn + merge])] if distributed else []
    tx = min(512, s // 2)
    head = max(1, 3 * (s // tx) // 16)
    dx = functools.partial(_dx_matmul, [dza, dzb, dzc, dzd], wi_t, dpre1, tm=tx)
    grad_x_head, landed = dx(tiles=(0, head), name="grad_x_head", exchanges=to_sibling(g_proj))
    sums_proj = chip_sums(proj, g_proj, landed[0]) if distributed else []
    grad_x_tail, landed = dx(tiles=(head, s // tx - head), name="grad_x", exchanges=to_chips(sums_proj) + small_exchange + join_exchange)
    grad_x = jnp.concatenate([grad_x_head, grad_x_tail])
    if distributed:
        halves.update(shard_sums(proj, sums_proj, landed[0]))
        small = landed[1][0]
        halves.update(zip(ffn + merge, landed[2]))
    else:
        halves = dict(zip(ffn + merge + proj, g_ffn + g_merge + g_proj))
    return grad_x, halves, small


def _mesh_position():
    x, y, c = lax.axis_index("x"), lax.axis_index("y"), lax.axis_index("c")
    chips = [(1 - x, y), (x, 1 - y), (1 - x, 1 - y)]
    return x, y, c, chips


class _Exchange(NamedTuple):
    operands: list
    out_shapes: list
    n_sems: int
    start: Callable
    finish: Callable
    halfway: Optional[Callable] = None
    in_place: bool = False


def _gather_exchange(shards):
    n = len(shards)
    per = 9
    assert all(w.shape[0] % (4 * BF16_SUBLANES) == 0 for w in shards)

    def plan(ins, outs, send_sems, recv_sems):
        x, y, c, (x_nbr, y_nbr, diag) = _mesh_position()
        sibling = (x, y, 1 - c)
        slot = lambda chip: 2 * chip[0] + chip[1]

        def rows(a, chip, hc, quarter=None):
            rh = shards[a].shape[0] // 2
            if quarter is None:
                return outs[a].at[slot(chip), pl.ds(hc * rh, rh), :]
            return outs[a].at[slot(chip), pl.ds(hc * rh + quarter * (rh // 2), rh // 2), :]

        def copy(a, k, src, dst, to):
            return pltpu.make_async_remote_copy(src_ref=src, dst_ref=dst, send_sem=send_sems.at[a * per + k], recv_sem=recv_sems.at[a * per + k],
                                                device_id=to, device_id_type=MESH)

        first, from_sibling = [], []
        landed, then = [[] for _ in range(4)], [[] for _ in range(4)]
        for a in range(n):
            rh = shards[a].shape[0] // 2
            my_half = ins[a].at[pl.ds(c * rh, rh), :]
            first += [copy(a, 4, ins[a], outs[a].at[slot((x, y))], sibling),
                      copy(a, 0, my_half, rows(a, (x, y), c), (*x_nbr, c)), copy(a, 1, my_half, rows(a, (x, y), c), (*y_nbr, c))]
            landed[0].append(copy(a, 0, rows(a, x_nbr, c), rows(a, x_nbr, c), (*x_nbr, c)))
            then[0].append([copy(a, 2, rows(a, x_nbr, c, 0), rows(a, x_nbr, c, 0), (*y_nbr, c)), copy(a, 5, rows(a, x_nbr, c), rows(a, x_nbr, c), sibling)])
            landed[1].append(copy(a, 1, rows(a, y_nbr, c), rows(a, y_nbr, c), (*y_nbr, c)))
            then[1].append([copy(a, 3, rows(a, y_nbr, c, 1), rows(a, y_nbr, c, 1), (*x_nbr, c)), copy(a, 6, rows(a, y_nbr, c), rows(a, y_nbr, c), sibling)])
            landed[2].append(copy(a, 2, rows(a, diag, c, 0), rows(a, diag, c, 0), (*y_nbr, c)))
            then[2].append([copy(a, 7, rows(a, diag, c, 0), rows(a, diag, c, 0), sibling)])
            landed[3].append(copy(a, 3, rows(a, diag, c, 1), rows(a, diag, c, 1), (*x_nbr, c)))
            then[3].append([copy(a, 8, rows(a, diag, c, 1), rows(a, diag, c, 1), sibling)])
            from_sibling += [copy(a, 4, outs[a].at[slot((x, y))], outs[a].at[slot((x, y))], sibling),
                             copy(a, 5, rows(a, x_nbr, 1 - c), rows(a, x_nbr, 1 - c), sibling), copy(a, 6, rows(a, y_nbr, 1 - c), rows(a, y_nbr, 1 - c), sibling),
                             copy(a, 7, rows(a, diag, 1 - c, 0), rows(a, diag, 1 - c, 0), sibling), copy(a, 8, rows(a, diag, 1 - c, 1), rows(a, diag, 1 - c, 1), sibling)]
        return first, landed, then, from_sibling

    def start(*refs):
        first, _, _, _ = plan(*refs)
        for cp in first:
            cp.start()

    def stages(landed, then, which):
        for stage in which:
            for arrival, onward in zip(landed[stage], then[stage]):
                arrival.wait_recv()
                for cp in onward:
                    cp.start()

    def halfway(*refs):
        _, landed, then, _ = plan(*refs)
        stages(landed, then, (0, 1))

    def finish(*refs):
        first, landed, then, from_sibling = plan(*refs)
        stages(landed, then, (2, 3))
        for cp in from_sibling:
            cp.wait_recv()
        for cp in first + [cp for stage in then for onward in stage for cp in onward]:
            cp.wait_send()

    return _Exchange(list(shards), [jax.ShapeDtypeStruct((N_SHARDS,) + w.shape, w.dtype) for w in shards], per * n, start, finish, halfway)


def _sibling_halves_exchange(grads):
    n = len(grads)

    def plan(ins, outs, send_sems, recv_sems):
        x, y, c, _ = _mesh_position()
        return [pltpu.make_async_remote_copy(src_ref=ins[a].at[:, pl.ds((1 - c) * (grads[a].shape[1] // 2), grads[a].shape[1] // 2), :],
                                             dst_ref=outs[a], send_sem=send_sems.at[a], recv_sem=recv_sems.at[a],
                                             device_id=(x, y, 1 - c), device_id_type=MESH) for a in range(n)]

    def start(*refs):
        for cp in plan(*refs):
            cp.start()

    def finish(*refs):
        for cp in plan(*refs):
            cp.wait()

    return _Exchange(list(grads), [jax.ShapeDtypeStruct((g.shape[0], g.shape[1] // 2, g.shape[2]), g.dtype) for g in grads], n, start, finish)


def _chip_partials_exchange(sums):
    n = len(sums)

    def plan(ins, outs, send_sems, recv_sems):
        _, _, c, chips = _mesh_position()
        return [pltpu.make_async_remote_copy(src_ref=ins[a].at[2 * cx + cy], dst_ref=outs[a].at[k], send_sem=send_sems.at[a * 3 + k],
                                             recv_sem=recv_sems.at[a * 3 + k], device_id=(cx, cy, c), device_id_type=MESH)
                for k, (cx, cy) in enumerate(chips) for a in range(n)]

    def start(*refs):
        for cp in plan(*refs):
            cp.start()

    def finish(*refs):
        for cp in plan(*refs):
            cp.wait()

    return _Exchange(list(sums), [jax.ShapeDtypeStruct((3,) + g.shape[1:], g.dtype) for g in sums], 3 * n, start, finish)


def _fused_call(body, *, name, grid, in_specs, out_specs, out_shape, scratch_shapes, operands, exchanges=()):
    single = not isinstance(out_shape, (tuple, list))
    out_specs = [out_specs] if single else list(out_specs)
    out_shape = [out_shape] if single else list(out_shape)
    n_in, n_out, n_scr = len(in_specs), len(out_specs), len(scratch_shapes)
    x_in = [len(e.operands) for e in exchanges]
    x_out = [len(e.out_shapes) for e in exchanges]

    def wrapped(*refs):
        refs = list(refs)
        ins = refs[:n_in]
        pos = n_in
        ex_ins = []
        for k in x_in:
            ex_ins.append(refs[pos:pos + k])
            pos += k
        outs = refs[pos:pos + n_out]
        pos += n_out
        ex_outs = []
        for k in x_out:
            ex_outs.append(refs[pos:pos + k])
            pos += k
        scratch = refs[pos:pos + n_scr]
        sems = refs[pos + n_scr:]
        first, last, middle = None, None, None
        for axis, size in enumerate(grid):
            at_start, at_end, at_middle = pl.program_id(axis) == 0, pl.program_id(axis) == size - 1, pl.program_id(axis) == size // 2
            first = at_start if first is None else first & at_start
            last = at_end if last is None else last & at_end
            middle = at_middle if middle is None else middle & at_middle

        @pl.when(first)
        def _():
            for i, e in enumerate(exchanges):
                e.start(ex_ins[i], ex_outs[i], sems[2 * i], sems[2 * i + 1])

        if any(e.halfway for e in exchanges):
            @pl.when(middle)
            def _():
                for i, e in enumerate(exchanges):
                    if e.halfway:
                        e.halfway(ex_ins[i], ex_outs[i], sems[2 * i], sems[2 * i + 1])

        body(*ins, *outs, *scratch)

        @pl.when(last)
        def _():
            for i, e in enumerate(exchanges):
                e.finish(ex_ins[i], ex_outs[i], sems[2 * i], sems[2 * i + 1])

    n_x_in, n_x_out = sum(x_in), sum(x_out)
    aliases = {}
    for i, e in enumerate(exchanges):
        if e.in_place:
            aliases.update({n_in + sum(x_in[:i]) + a: n_out + sum(x_out[:i]) + a for a in range(x_in[i])})
    results = pl.pallas_call(
        wrapped if exchanges else body, name=name, grid=grid,
        in_specs=list(in_specs) + [HBM] * n_x_in,
        out_specs=out_specs + [HBM] * n_x_out,
        out_shape=out_shape + [s for e in exchanges for s in e.out_shapes], input_output_aliases=aliases,
        scratch_shapes=list(scratch_shapes) + [pltpu.SemaphoreType.DMA((e.n_sems,)) for e in exchanges for _ in range(2)],
        compiler_params=_params(("arbitrary",) * len(grid)),
    )(*operands, *[a for e in exchanges for a in e.operands])
    own = results[0] if single else tuple(results[:n_out])
    landed, pos = [], n_out
    for k in x_out:
        landed.append(list(results[pos:pos + k]))
        pos += k
    return own, landed


def _cast_bf16(x, *, name, exchanges=()):
    s, cols = x.shape
    t = min(512, s)

    def body(x_ref, o_ref):
        o_ref[...] = x_ref[...].astype(BF16)

    tile = pl.BlockSpec((t, cols), lambda i: (i, 0))
    return _fused_call(body, name=name, grid=(s // t,), in_specs=[tile], out_specs=tile, out_shape=jax.ShapeDtypeStruct((s, cols), BF16),
                       scratch_shapes=[], operands=[x], exchanges=exchanges)


ROW_TILE_MAX = 640
BF16_SUBLANES = 16


def _row_tile(rows):
    for tr in range(min(rows, ROW_TILE_MAX), 0, -1):
        if rows % tr == 0 and tr % BF16_SUBLANES == 0:
            return tr
    raise ValueError(rows)


def _add_sibling(grad, other, pos, *, name):
    p, r, cols = grad.shape
    rh = r // 2
    tr = _row_tile(rh)
    nb = rh // tr

    def body(pos_ref, g_ref, o_ref, sb_ref, mine_ref):
        total = g_ref[...] + o_ref[...]
        sb_ref[...] = total.astype(BF16)

        @pl.when(pl.program_id(1) == pos_ref[0])
        def _():
            mine_ref[...] = total

    return pl.pallas_call(
        body, name=name, out_shape=(jax.ShapeDtypeStruct((p, rh, cols), BF16), jax.ShapeDtypeStruct((rh, cols), F32)),
        grid_spec=pltpu.PrefetchScalarGridSpec(
            num_scalar_prefetch=1, grid=(nb, p),
            in_specs=[pl.BlockSpec((None, tr, cols), lambda i, j, pos_ref: (j, pos_ref[1] * nb + i, 0)),
                      pl.BlockSpec((None, tr, cols), lambda i, j, pos_ref: (j, i, 0))],
            out_specs=(pl.BlockSpec((None, tr, cols), lambda i, j, pos_ref: (j, i, 0)),
                       pl.BlockSpec((tr, cols), lambda i, j, pos_ref: (i, 0)))),
        compiler_params=_params(("parallel", "arbitrary")),
    )(pos, grad, other)


def _add_chips(mine, others, pos, *, name):
    rh, cols = mine.shape
    tr = _row_tile(rh)
    nb = rh // tr

    def body(pos_ref, s_ref, o_ref, r_ref):
        r_ref[...] = ((s_ref[...] + o_ref[0].astype(F32)) + o_ref[1].astype(F32)) + o_ref[2].astype(F32)

    return pl.pallas_call(
        body, name=name, out_shape=jax.ShapeDtypeStruct((2 * rh, cols), F32),
        grid_spec=pltpu.PrefetchScalarGridSpec(
            num_scalar_prefetch=1, grid=(nb,),
            in_specs=[pl.BlockSpec((tr, cols), lambda i, pos_ref: (i, 0)),
                      pl.BlockSpec((3, tr, cols), lambda i, pos_ref: (0, i, 0))],
            out_specs=pl.BlockSpec((tr, cols), lambda i, pos_ref: (pos_ref[1] * nb + i, 0))),
        compiler_params=_params(("parallel",)),
    )(pos, mine, others)


def _join_exchange(bufs):
    n = len(bufs)

    def copy(a, hc, ins, outs, send_sems, recv_sems):
        x, y, c, _ = _mesh_position()
        rh = bufs[a].shape[0] // 2
        rows = pl.ds(hc * rh, rh)
        return pltpu.make_async_remote_copy(src_ref=ins[a].at[rows, :], dst_ref=outs[a].at[rows, :], send_sem=send_sems.at[a],
                                            recv_sem=recv_sems.at[a], device_id=(x, y, 1 - c), device_id_type=MESH)

    def start(*refs):
        c = lax.axis_index("c")
        for a in range(n):
            copy(a, c, *refs).start()

    def finish(*refs):
        c = lax.axis_index("c")
        for a in range(n):
            copy(a, c, *refs).wait_send()
            copy(a, 1 - c, *refs).wait_recv()

    return _Exchange(list(bufs), [jax.ShapeDtypeStruct(b.shape, b.dtype) for b in bufs], n, start, finish, in_place=True)


def _join_halves(bufs, *, name):
    n = len(bufs)
    join = _join_exchange(bufs)

    def body(*refs):
        ins, outs, sems = refs[:n], refs[n:2 * n], refs[2 * n:]
        join.start(ins, outs, *sems)
        join.finish(ins, outs, *sems)

    return pl.pallas_call(
        body, name=name, out_shape=join.out_shapes, in_specs=[HBM] * n, out_specs=[HBM] * n, input_output_aliases={a: a for a in range(n)},
        scratch_shapes=[pltpu.SemaphoreType.DMA((n,)), pltpu.SemaphoreType.DMA((n,))],
    )(*bufs)


SMALL = ["lb_logits", "hg_norm_gain", "swa_sinks", "rel_bias", "ln1_g", "ln1_b", "ln2_g", "ln2_b"]
PACK_ROWS = 48
PACK_AT = dict(lb_logits=(slice(0, 2), slice(0, D_MODEL)), hg_norm_gain=(slice(2, 3), slice(0, D_MODEL)), ln1_g=(slice(3, 4), slice(0, D_MODEL)),
               ln1_b=(slice(4, 5), slice(0, D_MODEL)), ln2_g=(slice(5, 6), slice(0, D_MODEL)), ln2_b=(slice(6, 7), slice(0, D_MODEL)),
               swa_sinks=(slice(7, 8), slice(0, SWA_HEADS)), sq_err=(slice(8, 9), slice(0, D_MODEL)),
               rel_bias=(slice(16, 16 + NUM_BUCKETS), slice(0, SWA_HEADS)))


def _pack_small(grads, *, name):
    names = SMALL + ["sq_err"]

    def body(*refs):
        packed = refs[len(names)]
        packed[...] = jnp.zeros_like(packed)
        for k, g_ref in zip(names, refs):
            packed[PACK_AT[k]] = g_ref[...]

    return pl.pallas_call(body, name=name, out_shape=jax.ShapeDtypeStruct((PACK_ROWS, D_MODEL), F32), compiler_params=_params(),
                          )(*[grads[k] for k in names])


def _small_gather_exchange(packed):
    def plan(ins, outs, send_sems, recv_sems):
        x, y, c, _ = _mesh_position()
        me = 4 * x + 2 * y + c
        own = pltpu.make_async_copy(ins[0], outs[0].at[me], send_sems.at[7])
        remote = []
        for d in range(1, 8):
            dx, dy, dc = (d >> 2) & 1, (d >> 1) & 1, d & 1
            remote.append(pltpu.make_async_remote_copy(src_ref=ins[0], dst_ref=outs[0].at[me], send_sem=send_sems.at[d - 1],
                                                       recv_sem=recv_sems.at[d - 1], device_id=(x ^ dx, y ^ dy, c ^ dc), device_id_type=MESH))
        return own, remote

    def start(*refs):
        own, remote = plan(*refs)
        own.start()
        for cp in remote:
            cp.start()

    def finish(*refs):
        own, remote = plan(*refs)
        for cp in remote:
            cp.wait()
        own.wait()

    return _Exchange([packed], [jax.ShapeDtypeStruct((8,) + packed.shape, packed.dtype)], 8, start, finish)


def _adamw_small(gathered, w, m, v, *, name):
    names = SMALL
    n = len(names)

    def body(*refs):
        gathered_ref = refs[0]
        w_refs, m_refs, v_refs = (dict(zip(names, refs[1 + i * n:1 + (i + 1) * n])) for i in range(3))
        loss_ref = refs[1 + 3 * n]
        go_refs, d_refs, nm_refs, nv_refs = (dict(zip(names, refs[2 + (3 + i) * n:2 + (4 + i) * n])) for i in range(4))
        total_ref = refs[2 + 7 * n]
        total = gathered_ref[0]
        for j in range(1, 8):
            total = total + gathered_ref[j]
        total_ref[...] = total
        loss_ref[...] = (0.5 / D_MODEL) * jnp.sum(total_ref[PACK_AT["sq_err"]], axis=1, keepdims=True)
        for k in names:
            g = total_ref[PACK_AT[k]]
            go_refs[k][...] = g
            d_refs[k][...], nm_refs[k][...], nv_refs[k][...] = _adamw_math(w_refs[k][...], g, m_refs[k][...], v_refs[k][...])

    like = [jax.ShapeDtypeStruct(w[k].shape, F32) for k in names]
    results = pl.pallas_call(body, name=name, out_shape=[jax.ShapeDtypeStruct((1, 1), F32)] + like * 4,
                             scratch_shapes=[pltpu.VMEM((PACK_ROWS, D_MODEL), F32)],
                             compiler_params=_params())(gathered, *[d[k] for d in (w, m, v) for k in names])
    return results[0], {k: tuple(results[1 + i * n + j] for i in range(4)) for j, k in enumerate(names)}


def _adamw_math(w, g, m, v):
    m = ADAM_B1 * m + (1.0 - ADAM_B1) * g
    v = ADAM_B2 * v + (1.0 - ADAM_B2) * (g * g)
    m_hat = m / (1.0 - ADAM_B1 ** ADAM_STEP)
    v_hat = v / (1.0 - ADAM_B2 ** ADAM_STEP)
    delta = -ADAM_LR * (m_hat / (jnp.sqrt(v_hat) + ADAM_EPS) + ADAM_WD * w)
    return delta, m, v


def _adamw(w, g, m, v, *, name):
    _, rows, cols = w.shape
    tr = _row_tile(rows)
    blk = pl.BlockSpec((None, tr, cols), lambda i: (0, i, 0))
    flat = pl.BlockSpec((tr, cols), lambda i: (i, 0))

    def body(w_ref, g_ref, m_ref, v_ref, go_ref, d_ref, nm_ref, nv_ref):
        g_v = g_ref[...]
        go_ref[...] = g_v
        d_ref[...], nm_ref[...], nv_ref[...] = _adamw_math(w_ref[...], g_v, m_ref[...], v_ref[...])

    shape = jax.ShapeDtypeStruct((1, rows, cols), F32)
    return pl.pallas_call(body, name=name, grid=(rows // tr,), out_shape=(shape,) * 4, in_specs=[blk, flat, blk, blk], out_specs=(blk,) * 4,
                          compiler_params=_params(("parallel",)))(w, g, m, v)


WEIGHTS = ["w_in", "lb_logits", "hg_norm_gain", "swa_sinks", "rel_bias", "w_mem_kv", "w_branch_hg", "w_branch_swa", "w_branch_mem",
           "w_out", "ln1_g", "ln1_b", "w_up", "w_down", "ln2_g", "ln2_b"]
BIG = ["w_in", "w_mem_kv", "w_branch_hg", "w_branch_swa", "w_branch_mem", "w_out", "w_up", "w_down"]


def kernel(x, mem, w_in, lb_logits, hg_norm_gain, swa_sinks, rel_bias, w_mem_kv, w_branch_hg, w_branch_swa, w_branch_mem, w_out, ln1_g, ln1_b, w_up, w_down, ln2_g, ln2_b, loss_target, m_w_in, m_lb_logits, m_hg_norm_gain, m_swa_sinks, m_rel_bias, m_w_mem_kv, m_w_branch_hg, m_w_branch_swa, m_w_branch_mem, m_w_out, m_ln1_g, m_ln1_b, m_w_up, m_w_down, m_ln2_g, m_ln2_b, v_w_in, v_lb_logits, v_hg_norm_gain, v_swa_sinks, v_rel_bias, v_w_mem_kv, v_w_branch_hg, v_w_branch_swa, v_w_branch_mem, v_w_out, v_ln1_g, v_ln1_b, v_w_up, v_w_down, v_ln2_g, v_ln2_b):
    w = dict(w_in=w_in, lb_logits=lb_logits, hg_norm_gain=hg_norm_gain, swa_sinks=swa_sinks, rel_bias=rel_bias, w_mem_kv=w_mem_kv,
             w_branch_hg=w_branch_hg, w_branch_swa=w_branch_swa, w_branch_mem=w_branch_mem, w_out=w_out, ln1_g=ln1_g, ln1_b=ln1_b,
             w_up=w_up, w_down=w_down, ln2_g=ln2_g, ln2_b=ln2_b)
    m = dict(w_in=m_w_in, lb_logits=m_lb_logits, hg_norm_gain=m_hg_norm_gain, swa_sinks=m_swa_sinks, rel_bias=m_rel_bias, w_mem_kv=m_w_mem_kv,
             w_branch_hg=m_w_branch_hg, w_branch_swa=m_w_branch_swa, w_branch_mem=m_w_branch_mem, w_out=m_w_out, ln1_g=m_ln1_g, ln1_b=m_ln1_b,
             w_up=m_w_up, w_down=m_w_down, ln2_g=m_ln2_g, ln2_b=m_ln2_b)
    v = dict(w_in=v_w_in, lb_logits=v_lb_logits, hg_norm_gain=v_hg_norm_gain, swa_sinks=v_swa_sinks, rel_bias=v_rel_bias, w_mem_kv=v_w_mem_kv,
             w_branch_hg=v_w_branch_hg, w_branch_swa=v_w_branch_swa, w_branch_mem=v_w_branch_mem, w_out=v_w_out, ln1_g=v_ln1_g, ln1_b=v_ln1_b,
             w_up=v_w_up, w_down=v_w_down, ln2_g=v_ln2_g, ln2_b=v_ln2_b)
    shapes = {k: w[k].shape for k in WEIGHTS}
    for d in (w, m, v):
        d["w_in"] = d["w_in"].reshape(D_MODEL, IN_COLS // N_SHARDS).T[None]
    shards = {k: w[k].reshape(w[k].shape[-2], w[k].shape[-1]).astype(BF16) for k in BIG}
    x2d = x.reshape(x.shape[-2], D_MODEL)
    xb, ((wi4,),) = _cast_bf16(x2d, name="gather_weights", exchanges=[_gather_exchange([shards["w_in"]])])
    wi_t = wi4.reshape(IN_COLS, D_MODEL)

    grad_x, halves, small = _local_step(
        x2d, xb, mem.reshape(MEM_LEN, D_MODEL), loss_target.reshape(loss_target.shape[-2], D_MODEL),
        wi_t, shards, lb_logits, hg_norm_gain, swa_sinks, rel_bias, ln1_g, ln1_b, ln2_g, ln2_b, distributed=True)

    reduced = dict(halves)
    reduced["w_in"], = _join_halves([halves["w_in"]], name="join_halves")

    outs = {k: _adamw(w[k], reduced[k], m[k], v[k], name="adamw_" + k) for k in BIG}
    loss, small_outs = _adamw_small(small, w, m, v, name="adamw_small")
    outs.update(small_outs)
    grad_out, delta_out, m_out, v_out = ({k: outs[k][i] for k in WEIGHTS} for i in range(4))
    for out in (grad_out, delta_out, m_out, v_out):
        out["w_in"] = out["w_in"][0].T

    result = [loss.reshape(()), grad_x.reshape(x.shape)]
    for out in (grad_out, delta_out, m_out, v_out):
        result += [out[k].reshape(shapes[k]) for k in WEIGHTS]
    return tuple(result)
```
